```python
import jax, jax.numpy as jnp
from jax import lax
import numpy as np

D_MODEL = 1024
BATCH = 8
SEQ = 4096
DEPTH = 1

CHUNK = 64
Q_BLOCK = 128

HEAD_DIM = 64
N_SB_HEADS = 8
D_SB = N_SB_HEADS * HEAD_DIM
N_MLA_HEADS = 8
QK_NOPE_DIM = 64
QK_ROPE_DIM = 32
V_HEAD_DIM = 64
Q_LORA_RANK = 256
KV_LORA_RANK = 128
D_MLA = N_MLA_HEADS * V_HEAD_DIM
D_MIX = D_SB + D_MLA
ROPE_THETA = 10000.0
PLE_DIM = 256
EPS = 1e-6

IN_SPLITS = (D_SB, D_SB, D_SB, D_SB, Q_LORA_RANK, KV_LORA_RANK, QK_ROPE_DIM, D_MLA)
D_IN = sum(IN_SPLITS)
IN_SPLIT_IDX = tuple(int(v) for v in np.cumsum(IN_SPLITS)[:-1])

kernel_name = "hybrid_stickbreak_mla_block"


def rms_norm(x, g):
    xf = x.astype(jnp.float32)
    y = xf * lax.rsqrt(jnp.mean(xf * xf, axis=-1, keepdims=True) + EPS)
    return (y * g.astype(jnp.float32)).astype(x.dtype)


def head_rms_norm(o, g):
    B, S, H, d = o.shape
    return rms_norm(o, g.reshape(H, d)).reshape(B, S, H * d)


def to_blocks(a):
    B, S = a.shape[:2]
    return a.reshape(B, S // Q_BLOCK, Q_BLOCK, *a.shape[2:]).swapaxes(0, 1)


def from_blocks(a):
    a = a.swapaxes(0, 1)
    return a.reshape(a.shape[0], a.shape[1] * a.shape[2], *a.shape[3:])


def apply_rope(x, positions):
    half = x.shape[-1] // 2
    freq = ROPE_THETA ** (-jnp.arange(half, dtype=jnp.float32) / half)
    ang = positions.astype(jnp.float32)[..., None] * freq
    ang = ang.reshape(ang.shape[:2] + (1,) * (x.ndim - 3) + (half,))
    cos, sin = jnp.cos(ang).astype(x.dtype), jnp.sin(ang).astype(x.dtype)
    x1, x2 = x[..., :half], x[..., half:]
    return jnp.concatenate([x1 * cos - x2 * sin, x2 * cos + x1 * sin], axis=-1)


def stick_breaking_attention(q, k, v):
    S = k.shape[1]
    scale = HEAD_DIM ** -0.5
    key_idx = jnp.arange(S)

    def block(args):
        b_idx, q_blk = args
        z = jnp.einsum('bqhd,bkhd->bhqk', q_blk, k).astype(jnp.float32) * scale
        t_idx = b_idx * Q_BLOCK + jnp.arange(Q_BLOCK)
        past = key_idx[None, :] < t_idx[:, None]
        log_fail = jnp.where(past, jax.nn.log_sigmoid(-z), 0.0)
        suffix = lax.cumsum(log_fail, axis=3, reverse=True) - log_fail
        w = jnp.where(past, jnp.exp(jax.nn.log_sigmoid(z) + suffix), 0.0)
        return jnp.einsum('bhqk,bkhd->bqhd', w.astype(v.dtype), v)

    out = lax.map(block, (jnp.arange(S // Q_BLOCK), to_blocks(q)))
    return from_blocks(out)


def latent_attention(q_nope, q_rope, k_nope, k_rope, v):
    S = k_nope.shape[1]
    scale = (QK_NOPE_DIM + QK_ROPE_DIM) ** -0.5
    key_chunk = jnp.arange(S) // CHUNK

    def block(args):
        b_idx, qn, qr = args
        z = (jnp.einsum('bqhd,bkhd->bhqk', qn, k_nope)
             + jnp.einsum('bqhr,bkr->bhqk', qr, k_rope)).astype(jnp.float32) * scale
        q_chunk = (b_idx * Q_BLOCK + jnp.arange(Q_BLOCK)) // CHUNK
        visible = key_chunk[None, :] <= q_chunk[:, None]
        z = jnp.where(visible, z, -jnp.inf)
        w = jax.nn.softmax(z, axis=-1)
        return jnp.einsum('bhqk,bkhd->bqhd', w.astype(v.dtype), v)

    out = lax.map(block, (jnp.arange(S // Q_BLOCK), to_blocks(q_nope), to_blocks(q_rope)))
    return from_blocks(out)


def _fwd_setup_inputs(seed: int = 0) -> dict:
    key = jax.random.key(seed)
    ks = jax.random.split(key, 20)

    def w(k, shape, fan_in):
        return jax.random.normal(k, shape, jnp.float32) * fan_in ** -0.5

    def gain(k, n):
        return 1.0 + 0.05 * jax.random.normal(k, (DEPTH, n), jnp.float32)

    x = jax.random.normal(ks[0], (BATCH, SEQ, D_MODEL), jnp.float32)
    p = jax.random.normal(ks[1], (DEPTH, BATCH, SEQ, PLE_DIM), jnp.float32)
    start = jax.random.randint(ks[2], (BATCH, 1), 0, 4096, dtype=jnp.int32)
    positions = start + jnp.arange(SEQ, dtype=jnp.int32)[None, :]
    return {
        'x': x,
        'p': p,
        'positions': positions,
        'norm_pre_g': gain(ks[3], D_MODEL),
        'w_in': w(ks[4], (DEPTH, D_MODEL, D_IN), D_MODEL),
        'q_norm_g': gain(ks[5], Q_LORA_RANK),
        'w_uq': w(ks[6], (DEPTH, Q_LORA_RANK, N_MLA_HEADS * (QK_NOPE_DIM + QK_ROPE_DIM)), Q_LORA_RANK),
        'kv_norm_g': gain(ks[7], KV_LORA_RANK),
        'w_ukv': w(ks[8], (DEPTH, KV_LORA_RANK, N_MLA_HEADS * (QK_NOPE_DIM + V_HEAD_DIM)), KV_LORA_RANK),
        'sb_out_norm_g': gain(ks[9], D_SB),
        'mla_out_norm_g': gain(ks[10], D_MLA),
        'w_out': w(ks[11], (DEPTH, D_MIX, D_MODEL), D_MIX),
        'norm_post_g': gain(ks[12], D_MODEL),
        'w_ple': w(ks[13], (DEPTH, PLE_DIM, D_MODEL), PLE_DIM),
        'ple_norm_g': gain(ks[14], D_MODEL),
        'w_ple_gate': w(ks[15], (DEPTH, D_MODEL, D_MODEL), D_MODEL),
        'b_ple_gate': 0.02 * jax.random.normal(ks[16], (DEPTH, D_MODEL), jnp.float32),
    }


def _fwd_reference(x, p, positions, norm_pre_g, w_in, q_norm_g, w_uq, kv_norm_g, w_ukv,
              sb_out_norm_g, mla_out_norm_g, w_out, norm_post_g, w_ple, ple_norm_g,
              w_ple_gate, b_ple_gate):
    B, S, _ = x.shape
    for i in range(DEPTH):
        h = rms_norm(x, norm_pre_g[i])
        proj = h @ w_in[i]
        sb_q, sb_k, sb_v, sb_g, c_q, c_kv, k_rope, mla_g = jnp.split(proj, IN_SPLIT_IDX, axis=-1)

        sb_o = stick_breaking_attention(sb_q.reshape(B, S, N_SB_HEADS, HEAD_DIM),
                                        sb_k.reshape(B, S, N_SB_HEADS, HEAD_DIM),
                                        sb_v.reshape(B, S, N_SB_HEADS, HEAD_DIM))
        sb_y = head_rms_norm(sb_o, sb_out_norm_g[i]) * jax.nn.silu(sb_g)

        q = (rms_norm(c_q, q_norm_g[i]) @ w_uq[i]).reshape(B, S, N_MLA_HEADS, QK_NOPE_DIM + QK_ROPE_DIM)
        q_nope, q_rope = q[..., :QK_NOPE_DIM], apply_rope(q[..., QK_NOPE_DIM:], positions)
        kv = (rms_norm(c_kv, kv_norm_g[i]) @ w_ukv[i]).reshape(B, S, N_MLA_HEADS, QK_NOPE_DIM + V_HEAD_DIM)
        k_nope, v = kv[..., :QK_NOPE_DIM], kv[..., QK_NOPE_DIM:]
        k_rope = apply_rope(k_rope, positions)
        mla_o = latent_attention(q_nope, q_rope, k_nope, k_rope, v)
        mla_y = head_rms_norm(mla_o, mla_out_norm_g[i]) * jax.nn.silu(mla_g)

        y = jnp.concatenate([sb_y, mla_y], axis=-1) @ w_out[i]
        x = x + rms_norm(y, norm_post_g[i])

        ple = rms_norm(p[i] @ w_ple[i], ple_norm_g[i])
        x = x + ple * jax.nn.sigmoid(x @ w_ple_gate[i] + b_ple_gate[i])
    return x


import jax as _jax
import jax.numpy as _jnp

TWIN_FORMAT = 'train_step'
FWD_PARAMS = ['x', 'p', 'positions', 'norm_pre_g', 'w_in', 'q_norm_g', 'w_uq', 'kv_norm_g', 'w_ukv', 'sb_out_norm_g', 'mla_out_norm_g', 'w_out', 'norm_post_g', 'w_ple', 'ple_norm_g', 'w_ple_gate', 'b_ple_gate']
TWIN_WEIGHTS = ['norm_pre_g', 'w_in', 'q_norm_g', 'w_uq', 'kv_norm_g', 'w_ukv', 'sb_out_norm_g', 'mla_out_norm_g', 'w_out', 'norm_post_g', 'w_ple', 'ple_norm_g', 'w_ple_gate', 'b_ple_gate']
TWIN_DIFF_INPUT = 'x'
TWIN_INPUTS = ['x', 'p', 'positions', 'norm_pre_g', 'w_in', 'q_norm_g', 'w_uq', 'kv_norm_g', 'w_ukv', 'sb_out_norm_g', 'mla_out_norm_g', 'w_out', 'norm_post_g', 'w_ple', 'ple_norm_g', 'w_ple_gate', 'b_ple_gate', 'loss_target', 'm_norm_pre_g', 'm_w_in', 'm_q_norm_g', 'm_w_uq', 'm_kv_norm_g', 'm_w_ukv', 'm_sb_out_norm_g', 'm_mla_out_norm_g', 'm_w_out', 'm_norm_post_g', 'm_w_ple', 'm_ple_norm_g', 'm_w_ple_gate', 'm_b_ple_gate', 'v_norm_pre_g', 'v_w_in', 'v_q_norm_g', 'v_w_uq', 'v_kv_norm_g', 'v_w_ukv', 'v_sb_out_norm_g', 'v_mla_out_norm_g', 'v_w_out', 'v_norm_post_g', 'v_w_ple', 'v_ple_norm_g', 'v_w_ple_gate', 'v_b_ple_gate']
TWIN_OUTPUTS = ['loss', 'grad_x', 'grad_norm_pre_g', 'grad_w_in', 'grad_q_norm_g', 'grad_w_uq', 'grad_kv_norm_g', 'grad_w_ukv', 'grad_sb_out_norm_g', 'grad_mla_out_norm_g', 'grad_w_out', 'grad_norm_post_g', 'grad_w_ple', 'grad_ple_norm_g', 'grad_w_ple_gate', 'grad_b_ple_gate', 'delta_norm_pre_g', 'delta_w_in', 'delta_q_norm_g', 'delta_w_uq', 'delta_kv_norm_g', 'delta_w_ukv', 'delta_sb_out_norm_g', 'delta_mla_out_norm_g', 'delta_w_out', 'delta_norm_post_g', 'delta_w_ple', 'delta_ple_norm_g', 'delta_w_ple_gate', 'delta_b_ple_gate', 'new_m_norm_pre_g', 'new_m_w_in', 'new_m_q_norm_g', 'new_m_w_uq', 'new_m_kv_norm_g', 'new_m_w_ukv', 'new_m_sb_out_norm_g', 'new_m_mla_out_norm_g', 'new_m_w_out', 'new_m_norm_post_g', 'new_m_w_ple', 'new_m_ple_norm_g', 'new_m_w_ple_gate', 'new_m_b_ple_gate', 'new_v_norm_pre_g', 'new_v_w_in', 'new_v_q_norm_g', 'new_v_w_uq', 'new_v_kv_norm_g', 'new_v_w_ukv', 'new_v_sb_out_norm_g', 'new_v_mla_out_norm_g', 'new_v_w_out', 'new_v_norm_post_g', 'new_v_w_ple', 'new_v_ple_norm_g', 'new_v_w_ple_gate', 'new_v_b_ple_gate']
TWIN_LEAF_KINDS = {'loss': 'loss', 'grad_x': 'grad_x', 'grad_norm_pre_g': 'grad_w', 'grad_w_in': 'grad_w', 'grad_q_norm_g': 'grad_w', 'grad_w_uq': 'grad_w', 'grad_kv_norm_g': 'grad_w', 'grad_w_ukv': 'grad_w', 'grad_sb_out_norm_g': 'grad_w', 'grad_mla_out_norm_g': 'grad_w', 'grad_w_out': 'grad_w', 'grad_norm_post_g': 'grad_w', 'grad_w_ple': 'grad_w', 'grad_ple_norm_g': 'grad_w', 'grad_w_ple_gate': 'grad_w', 'grad_b_ple_gate': 'grad_w', 'delta_norm_pre_g': 'delta_w', 'delta_w_in': 'delta_w', 'delta_q_norm_g': 'delta_w', 'delta_w_uq': 'delta_w', 'delta_kv_norm_g': 'delta_w', 'delta_w_ukv': 'delta_w', 'delta_sb_out_norm_g': 'delta_w', 'delta_mla_out_norm_g': 'delta_w', 'delta_w_out': 'delta_w', 'delta_norm_post_g': 'delta_w', 'delta_w_ple': 'delta_w', 'delta_ple_norm_g': 'delta_w', 'delta_w_ple_gate': 'delta_w', 'delta_b_ple_gate': 'delta_w', 'new_m_norm_pre_g': 'new_m', 'new_m_w_in': 'new_m', 'new_m_q_norm_g': 'new_m', 'new_m_w_uq': 'new_m', 'new_m_kv_norm_g': 'new_m', 'new_m_w_ukv': 'new_m', 'new_m_sb_out_norm_g': 'new_m', 'new_m_mla_out_norm_g': 'new_m', 'new_m_w_out': 'new_m', 'new_m_norm_post_g': 'new_m', 'new_m_w_ple': 'new_m', 'new_m_ple_norm_g': 'new_m', 'new_m_w_ple_gate': 'new_m', 'new_m_b_ple_gate': 'new_m', 'new_v_norm_pre_g': 'new_v', 'new_v_w_in': 'new_v', 'new_v_q_norm_g': 'new_v', 'new_v_w_uq': 'new_v', 'new_v_kv_norm_g': 'new_v', 'new_v_w_ukv': 'new_v', 'new_v_sb_out_norm_g': 'new_v', 'new_v_mla_out_norm_g': 'new_v', 'new_v_w_out': 'new_v', 'new_v_norm_post_g': 'new_v', 'new_v_w_ple': 'new_v', 'new_v_ple_norm_g': 'new_v', 'new_v_w_ple_gate': 'new_v', 'new_v_b_ple_gate': 'new_v'}


def _forward(args):
    return _fwd_reference(*[args[k] for k in FWD_PARAMS])


def _output_shape():
    out = _jax.eval_shape(lambda: _forward(_fwd_setup_inputs(0)))
    return out.shape, out.dtype

N_MICROBATCH = 1
ADAM_LR = 0.001
ADAM_B1 = 0.9
ADAM_B2 = 0.999
ADAM_EPS = 1e-08
ADAM_WD = 0.01
ADAM_STEP = 10
PER_EXAMPLE_BATCH_AXIS = {'x': 0, 'p': 1, 'positions': 0, 'loss_target': 0}
SHARED_INPUTS = []
_WEIGHT_DTYPES = {'norm_pre_g': _jnp.float32, 'w_in': _jnp.float32, 'q_norm_g': _jnp.float32, 'w_uq': _jnp.float32, 'kv_norm_g': _jnp.float32, 'w_ukv': _jnp.float32, 'sb_out_norm_g': _jnp.float32, 'mla_out_norm_g': _jnp.float32, 'w_out': _jnp.float32, 'norm_post_g': _jnp.float32, 'w_ple': _jnp.float32, 'ple_norm_g': _jnp.float32, 'w_ple_gate': _jnp.float32, 'b_ple_gate': _jnp.float32}
MOMENT_SCALE = {'norm_pre_g': 4.960096e-01, 'w_in': 2.882087e-01, 'q_norm_g': 4.510577e-01, 'w_uq': 2.686253e-01, 'kv_norm_g': 1.228133e+00, 'w_ukv': 3.291302e-01, 'sb_out_norm_g': 2.786464e-01, 'mla_out_norm_g': 3.636730e-01, 'w_out': 3.125601e-01, 'norm_post_g': 3.270551e+01, 'w_ple': 1.818982e-01, 'ple_norm_g': 1.015494e+01, 'w_ple_gate': 1.142887e-01, 'b_ple_gate': 2.910933e+00}


def _to_microbatches(a, axis):
    t = _jnp.moveaxis(a, axis, 0)
    t = t.reshape((N_MICROBATCH, t.shape[0] // N_MICROBATCH) + t.shape[1:])
    return _jnp.moveaxis(t, 1, axis + 1)


def setup_inputs(seed: int = 0) -> dict:
    inp = _fwd_setup_inputs(seed)
    key = _jax.random.fold_in(_jax.random.key(seed), 7919)
    shape, _ = _output_shape()
    out = dict(inp)
    out["loss_target"] = _jax.random.normal(_jax.random.fold_in(key, 0), shape, _jnp.float32)
    for i, name in enumerate(TWIN_WEIGHTS):
        w = inp[name].astype(_jnp.float32)
        if MOMENT_SCALE is None:
            s = _jnp.sqrt(_jnp.mean(_jnp.square(w)) + 1e-30)
        else:
            s = MOMENT_SCALE[name]
        km, kv = _jax.random.split(_jax.random.fold_in(key, i + 1))
        out[name] = w
        out["m_" + name] = s * _jax.random.normal(km, w.shape, _jnp.float32)
        out["v_" + name] = (s * s) * _jax.random.uniform(kv, w.shape, _jnp.float32, 0.5, 1.5)
    if N_MICROBATCH > 1:
        for name, axis in PER_EXAMPLE_BATCH_AXIS.items():
            out[name] = _to_microbatches(out[name], axis)
    return {'x': out['x'], 'p': out['p'], 'positions': out['positions'], 'norm_pre_g': out['norm_pre_g'], 'w_in': out['w_in'], 'q_norm_g': out['q_norm_g'], 'w_uq': out['w_uq'], 'kv_norm_g': out['kv_norm_g'], 'w_ukv': out['w_ukv'], 'sb_out_norm_g': out['sb_out_norm_g'], 'mla_out_norm_g': out['mla_out_norm_g'], 'w_out': out['w_out'], 'norm_post_g': out['norm_post_g'], 'w_ple': out['w_ple'], 'ple_norm_g': out['ple_norm_g'], 'w_ple_gate': out['w_ple_gate'], 'b_ple_gate': out['b_ple_gate'], 'loss_target': out['loss_target'], 'm_norm_pre_g': out['m_norm_pre_g'], 'm_w_in': out['m_w_in'], 'm_q_norm_g': out['m_q_norm_g'], 'm_w_uq': out['m_w_uq'], 'm_kv_norm_g': out['m_kv_norm_g'], 'm_w_ukv': out['m_w_ukv'], 'm_sb_out_norm_g': out['m_sb_out_norm_g'], 'm_mla_out_norm_g': out['m_mla_out_norm_g'], 'm_w_out': out['m_w_out'], 'm_norm_post_g': out['m_norm_post_g'], 'm_w_ple': out['m_w_ple'], 'm_ple_norm_g': out['m_ple_norm_g'], 'm_w_ple_gate': out['m_w_ple_gate'], 'm_b_ple_gate': out['m_b_ple_gate'], 'v_norm_pre_g': out['v_norm_pre_g'], 'v_w_in': out['v_w_in'], 'v_q_norm_g': out['v_q_norm_g'], 'v_w_uq': out['v_w_uq'], 'v_kv_norm_g': out['v_kv_norm_g'], 'v_w_ukv': out['v_w_ukv'], 'v_sb_out_norm_g': out['v_sb_out_norm_g'], 'v_mla_out_norm_g': out['v_mla_out_norm_g'], 'v_w_out': out['v_w_out'], 'v_norm_post_g': out['v_norm_post_g'], 'v_w_ple': out['v_w_ple'], 'v_ple_norm_g': out['v_ple_norm_g'], 'v_w_ple_gate': out['v_w_ple_gate'], 'v_b_ple_gate': out['v_b_ple_gate']}


def _loss(weights, diff, rest, loss_target):
    with _jax.named_scope("forward"):
        args = {**rest, TWIN_DIFF_INPUT: diff, **{k: w.astype(_WEIGHT_DTYPES[k]) for k, w in weights.items()}}
        y = _forward(args)
    with _jax.named_scope("loss_head"):
        err = _jnp.square(y.astype(_jnp.float32) - loss_target)
        return 0.5 * _jnp.sum(_jnp.mean(err, axis=-1)) if err.ndim else 0.5 * err


def _adamw(w, g, m, v):
    m = ADAM_B1 * m + (1.0 - ADAM_B1) * g
    v = ADAM_B2 * v + (1.0 - ADAM_B2) * _jnp.square(g)
    m_hat = m / (1.0 - ADAM_B1 ** ADAM_STEP)
    v_hat = v / (1.0 - ADAM_B2 ** ADAM_STEP)
    delta = -ADAM_LR * (m_hat / (_jnp.sqrt(v_hat) + ADAM_EPS) + ADAM_WD * w)
    return delta, m, v


def reference(x, p, positions, norm_pre_g, w_in, q_norm_g, w_uq, kv_norm_g, w_ukv, sb_out_norm_g, mla_out_norm_g, w_out, norm_post_g, w_ple, ple_norm_g, w_ple_gate, b_ple_gate, loss_target, m_norm_pre_g, m_w_in, m_q_norm_g, m_w_uq, m_kv_norm_g, m_w_ukv, m_sb_out_norm_g, m_mla_out_norm_g, m_w_out, m_norm_post_g, m_w_ple, m_ple_norm_g, m_w_ple_gate, m_b_ple_gate, v_norm_pre_g, v_w_in, v_q_norm_g, v_w_uq, v_kv_norm_g, v_w_ukv, v_sb_out_norm_g, v_mla_out_norm_g, v_w_out, v_norm_post_g, v_w_ple, v_ple_norm_g, v_w_ple_gate, v_b_ple_gate):
    given = dict(x=x, p=p, positions=positions, norm_pre_g=norm_pre_g, w_in=w_in, q_norm_g=q_norm_g, w_uq=w_uq, kv_norm_g=kv_norm_g, w_ukv=w_ukv, sb_out_norm_g=sb_out_norm_g, mla_out_norm_g=mla_out_norm_g, w_out=w_out, norm_post_g=norm_post_g, w_ple=w_ple, ple_norm_g=ple_norm_g, w_ple_gate=w_ple_gate, b_ple_gate=b_ple_gate, loss_target=loss_target, m_norm_pre_g=m_norm_pre_g, m_w_in=m_w_in, m_q_norm_g=m_q_norm_g, m_w_uq=m_w_uq, m_kv_norm_g=m_kv_norm_g, m_w_ukv=m_w_ukv, m_sb_out_norm_g=m_sb_out_norm_g, m_mla_out_norm_g=m_mla_out_norm_g, m_w_out=m_w_out, m_norm_post_g=m_norm_post_g, m_w_ple=m_w_ple, m_ple_norm_g=m_ple_norm_g, m_w_ple_gate=m_w_ple_gate, m_b_ple_gate=m_b_ple_gate, v_norm_pre_g=v_norm_pre_g, v_w_in=v_w_in, v_q_norm_g=v_q_norm_g, v_w_uq=v_w_uq, v_kv_norm_g=v_kv_norm_g, v_w_ukv=v_w_ukv, v_sb_out_norm_g=v_sb_out_norm_g, v_mla_out_norm_g=v_mla_out_norm_g, v_w_out=v_w_out, v_norm_post_g=v_norm_post_g, v_w_ple=v_w_ple, v_ple_norm_g=v_ple_norm_g, v_w_ple_gate=v_w_ple_gate, v_b_ple_gate=v_b_ple_gate)
    weights = {n: given[n] for n in TWIN_WEIGHTS}
    shared = {n: given[n] for n in SHARED_INPUTS}
    per_example = {n: given[n] for n in ['x', 'p', 'positions']}
    grad_fn = _jax.value_and_grad(_loss, argnums=(0, 1))

    def one_microbatch(ex, loss_target):
        ex = dict(ex)
        diff = ex.pop(TWIN_DIFF_INPUT)
        return grad_fn(weights, diff, {**shared, **ex}, loss_target)

    if N_MICROBATCH == 1:
        loss, (grad_w, grad_x) = one_microbatch(per_example, given["loss_target"])
    else:
        def body(carry, xs):
            loss_sum, grad_sum = carry
            l_k, (gw_k, gx_k) = one_microbatch(xs[0], xs[1])
            with _jax.named_scope("update"):
                return (loss_sum + l_k, _jax.tree.map(_jnp.add, grad_sum, gw_k)), gx_k

        init = (_jnp.zeros((), _jnp.float32), _jax.tree.map(_jnp.zeros_like, weights))
        (loss, grad_w), grad_x = _jax.lax.scan(body, init, (per_example, given["loss_target"]))
    with _jax.named_scope("update"):
        delta_w, new_m, new_v = {}, {}, {}
        for n in TWIN_WEIGHTS:
            delta_w[n], new_m[n], new_v[n] = _adamw(weights[n], grad_w[n], given["m_" + n], given["v_" + n])
    return (loss, grad_x, *[grad_w[n] for n in TWIN_WEIGHTS], *[delta_w[n] for n in TWIN_WEIGHTS],
            *[new_m[n] for n in TWIN_WEIGHTS], *[new_v[n] for n in TWIN_WEIGHTS])
```

```python
import jax
import jax.numpy as jnp
from jax import lax
from jax.experimental import pallas as pl
from jax.experimental.pallas import tpu as pltpu

F32 = jnp.float32
BF16 = jnp.bfloat16
MESH = pl.DeviceIdType.MESH

N_DEV = 8
D_MODEL = 1024
N_HEADS = 8
HEAD_DIM = 64
D_GRP = N_HEADS * HEAD_DIM
Q_LORA = 256
KV_LORA = 128
ROPE_DIM = 32
PLE_DIM = 256
CHUNK_SHIFT = 6
ROPE_THETA = 10000.0
EPS = 1e-6
SB_SCALE = HEAD_DIM ** -0.5
MLA_SCALE = (HEAD_DIM + ROPE_DIM) ** -0.5
NEG = -1e30

ADAM_LR = 0.001
ADAM_B1 = 0.9
ADAM_B2 = 0.999
ADAM_EPS = 1e-08
ADAM_WD = 0.01
ADAM_STEP = 10

LANES = 128
TQ = 256
TK = 256
TM = 256

R_IN, R_UQ, R_UKV, R_OUT, R_PLE, R_PG = 2976, 192, 128, 1024, 256, 1024
R_BIG = R_IN + R_UQ + R_UKV + R_OUT + R_PLE + R_PG
R_SMALL = 56
D_IN_P = 3072

_NT = (((1,), (1,)), ((), ()))
_TN = (((0,), (0,)), ((), ()))


def _params(sem, vmem_mb):
    return pltpu.CompilerParams(dimension_semantics=sem, vmem_limit_bytes=vmem_mb << 20)


def _dot(a, b):
    return jnp.dot(a, b, preferred_element_type=F32)


def _dot_nt(a, b):
    return lax.dot_general(a, b, _NT, preferred_element_type=F32)


def _dot_tn(a, b):
    return lax.dot_general(a, b, _TN, preferred_element_type=F32)


def _hl_dot(a, b):
    hi = a.astype(BF16)
    lo = (a - hi.astype(F32)).astype(BF16)
    return _dot(hi, b) + _dot(lo, b)


def _sigmoid(x):
    return 1.0 / (1.0 + jnp.exp(-x))


def _rope_swap(x, lane):
    left = pltpu.roll(x, LANES - 16, axis=1)
    right = pltpu.roll(x, 16, axis=1)
    lo = (lane >= 64) & (lane < 80)
    hi = (lane >= 80) & (lane < 96)
    return jnp.where(lo, left, jnp.where(hi, right, 0.0))


def _all_gather(shard):
    m_per, n = shard.shape

    def body(x_ref, out_ref, send_sems, recv_sems, local_sem):
        x, y, c = lax.axis_index("x"), lax.axis_index("y"), lax.axis_index("c")
        me, sibling = (x, y, c), (x, y, 1 - c)
        chips = [(1 - x, y), (x, 1 - y), (1 - x, 1 - y)]

        def rows(px, py, pc):
            return out_ref.at[pl.ds((4 * px + 2 * py + pc) * m_per, m_per), :]

        def copy(k, block, to, src=None):
            return pltpu.make_async_remote_copy(
                src_ref=rows(*block) if src is None else src, dst_ref=rows(*block),
                send_sem=send_sems.at[k], recv_sem=recv_sems.at[k],
                device_id=to, device_id_type=MESH)

        mine = pltpu.make_async_copy(x_ref, rows(*me), local_sem)
        mine.start()
        first = [copy(0, me, sibling, src=x_ref)]
        first += [copy(1 + j, me, (*chip, c), src=x_ref) for j, chip in enumerate(chips)]
        for cp in first:
            cp.start()
        passed = [copy(4 + j, (*chip, c), sibling) for j, chip in enumerate(chips)]
        for j, chip in enumerate(chips):
            copy(1 + j, (*chip, c), me).wait_recv()
            passed[j].start()
        copy(0, sibling, me).wait_recv()
        for j, chip in enumerate(chips):
            copy(4 + j, (*chip, 1 - c), me).wait_recv()
        for cp in first + passed:
            cp.wait_send()
        mine.wait()

    return pl.pallas_call(
        body, name="weight_all_gather",
        out_shape=jax.ShapeDtypeStruct((N_DEV * m_per, n), shard.dtype),
        in_specs=[pl.BlockSpec(memory_space=pltpu.VMEM)],
        out_specs=pl.BlockSpec(memory_space=pltpu.VMEM),
        scratch_shapes=[pltpu.SemaphoreType.DMA((7,)), pltpu.SemaphoreType.DMA((7,)),
                        pltpu.SemaphoreType.DMA],
        compiler_params=pltpu.CompilerParams(vmem_limit_bytes=40 << 20),
    )(shard)


def _grad_exchange(g_all, small):
    _, r, n = g_all.shape
    sr = small.shape[0]

    def body(g_ref, s_ref, land_ref, sland_ref, ssem, rsem, ssem2, rsem2, lsem):
        x, y, c = lax.axis_index("x"), lax.axis_index("y"), lax.axis_index("c")
        me = 4 * x + 2 * y + c
        own = pltpu.make_async_copy(g_ref.at[me], land_ref.at[me], lsem.at[0])
        own2 = pltpu.make_async_copy(s_ref, sland_ref.at[me], lsem.at[1])
        own.start()
        own2.start()
        copies = []
        for k in range(1, N_DEV):
            px = 1 - x if (k >> 2) & 1 else x
            py = 1 - y if (k >> 1) & 1 else y
            pc = 1 - c if k & 1 else c
            pid = 4 * px + 2 * py + pc
            copies.append(pltpu.make_async_remote_copy(
                src_ref=g_ref.at[pid], dst_ref=land_ref.at[me],
                send_sem=ssem.at[k], recv_sem=rsem.at[k], device_id=(px, py, pc), device_id_type=MESH))
            copies.append(pltpu.make_async_remote_copy(
                src_ref=s_ref, dst_ref=sland_ref.at[me],
                send_sem=ssem2.at[k], recv_sem=rsem2.at[k], device_id=(px, py, pc), device_id_type=MESH))
        for cp in copies:
            cp.start()
        for cp in copies:
            cp.wait()
        own.wait()
        own2.wait()

    any_spec = pl.BlockSpec(memory_space=pl.ANY)
    return pl.pallas_call(
        body, name="grad_exchange",
        out_shape=(jax.ShapeDtypeStruct((N_DEV, r, n), F32), jax.ShapeDtypeStruct((N_DEV, sr, n), F32)),
        in_specs=[any_spec, any_spec], out_specs=(any_spec, any_spec),
        scratch_shapes=[pltpu.SemaphoreType.DMA((N_DEV,)), pltpu.SemaphoreType.DMA((N_DEV,)),
                        pltpu.SemaphoreType.DMA((N_DEV,)), pltpu.SemaphoreType.DMA((N_DEV,)),
                        pltpu.SemaphoreType.DMA((2,))],
    )(g_all, small)


def _adamw(land, w, m, v, block_rows, name):
    _, r, n = land.shape
    c1 = 1.0 - ADAM_B1
    c2 = 1.0 - ADAM_B2
    bc1 = 1.0 - ADAM_B1 ** ADAM_STEP
    bc2 = 1.0 - ADAM_B2 ** ADAM_STEP

    def body(l_ref, w_ref, m_ref, v_ref, g_out, d_out, m_out, v_out):
        g = l_ref[0]
        for j in range(1, N_DEV):
            g = g + l_ref[j]
        mn = ADAM_B1 * m_ref[...] + c1 * g
        vn = ADAM_B2 * v_ref[...] + c2 * (g * g)
        m_hat = mn / bc1
        v_hat = vn / bc2
        g_out[...] = g
        d_out[...] = -ADAM_LR * (m_hat / (jnp.sqrt(v_hat) + ADAM_EPS) + ADAM_WD * w_ref[...])
        m_out[...] = mn
        v_out[...] = vn

    row = pl.BlockSpec((block_rows, n), lambda i: (i, 0))
    shp = jax.ShapeDtypeStruct((r, n), F32)
    return pl.pallas_call(
        body, name=name, grid=(r // block_rows,),
        in_specs=[pl.BlockSpec((N_DEV, block_rows, n), lambda i: (0, i, 0)), row, row, row],
        out_specs=(row, row, row, row), out_shape=(shp, shp, shp, shp),
        compiler_params=_params(("parallel",), 40),
    )(land, w, m, v)


def _in_proj(x, g, w):
    s = x.shape[0]

    def body(x_ref, g_ref, w_ref, qkv_ref, rest_ref, h_ref):
        xv = x_ref[...]
        r = lax.rsqrt(jnp.mean(xv * xv, axis=-1, keepdims=True) + EPS)
        h = ((xv * r) * g_ref[...]).astype(BF16)
        h_ref[...] = h
        qkv_ref[...] = _dot(h, w_ref[:, :1536]).astype(BF16)
        rest_ref[...] = _dot(h, w_ref[:, 1536:])

    return pl.pallas_call(
        body, name="in_proj", grid=(s // TM,),
        in_specs=[pl.BlockSpec((TM, D_MODEL), lambda i: (i, 0)),
                  pl.BlockSpec((1, D_MODEL), lambda i: (0, 0)),
                  pl.BlockSpec((D_MODEL, D_IN_P), lambda i: (0, 0))],
        out_specs=(pl.BlockSpec((TM, 1536), lambda i: (i, 0)),
                   pl.BlockSpec((TM, 1536), lambda i: (i, 0)),
                   pl.BlockSpec((TM, D_MODEL), lambda i: (i, 0))),
        out_shape=(jax.ShapeDtypeStruct((s, 1536), BF16), jax.ShapeDtypeStruct((s, 1536), F32),
                   jax.ShapeDtypeStruct((s, D_MODEL), BF16)),
        compiler_params=_params(("parallel",), 48),
    )(x, g, w)


def _mla_prep(rest, gq, gkv, wuq, wuk, wuv, cos_t, sin_t):
    s = rest.shape[0]

    def body(cq_ref, ckv_ref, kr_ref, gq_ref, gkv_ref, wuq_ref, wuk_ref, wuv_ref, c_ref, s_ref,
             qp_ref, kp_ref, vv_ref, cqn_ref, ckvn_ref):
        lane = lax.broadcasted_iota(jnp.int32, (1, LANES), 1)
        cos_v, sin_v = c_ref[...], s_ref[...]
        cq = cq_ref[...]
        rq = lax.rsqrt(jnp.mean(cq * cq, axis=-1, keepdims=True) + EPS)
        cqn = ((cq * rq) * gq_ref[...]).astype(BF16)
        cqn_ref[...] = cqn
        q = _dot(cqn, wuq_ref[...])
        ckv = ckv_ref[...]
        rkv = lax.rsqrt(jnp.mean(ckv * ckv, axis=-1, keepdims=True) + EPS)
        ckvn = ((ckv * rkv) * gkv_ref[...]).astype(BF16)
        ckvn_ref[...] = ckvn
        kn = _dot(ckvn, wuk_ref[...])
        vv_ref[...] = _dot(ckvn, wuv_ref[...]).astype(BF16)
        kr = kr_ref[...]
        kr_roped = kr * cos_v + _rope_swap(kr, lane) * sin_v
        for h in range(N_HEADS):
            sl = slice(h * LANES, (h + 1) * LANES)
            qh = q[:, sl]
            qp_ref[:, sl] = (qh * cos_v + _rope_swap(qh, lane) * sin_v).astype(BF16)
            kp_ref[:, sl] = (kn[:, sl] + kr_roped).astype(BF16)

    def row(width, idx):
        return pl.BlockSpec((TM, width), lambda i: (i, idx))

    def full(a):
        return pl.BlockSpec(a.shape, lambda i: (0, 0))

    return pl.pallas_call(
        body, name="mla_prep", grid=(s // TM,),
        in_specs=[row(Q_LORA, 4), row(KV_LORA, 10), row(LANES, 11), full(gq), full(gkv),
                  full(wuq), full(wuk), full(wuv), row(LANES, 0), row(LANES, 0)],
        out_specs=(row(1024, 0), row(1024, 0), row(D_GRP, 0), row(Q_LORA, 0), row(KV_LORA, 0)),
        out_shape=(jax.ShapeDtypeStruct((s, 1024), BF16), jax.ShapeDtypeStruct((s, 1024), BF16),
                   jax.ShapeDtypeStruct((s, D_GRP), BF16), jax.ShapeDtypeStruct((s, Q_LORA), BF16),
                   jax.ShapeDtypeStruct((s, KV_LORA), BF16)),
        compiler_params=_params(("parallel",), 32),
    )(rest, rest, rest, gq, gkv, wuq, wuk, wuv, cos_t, sin_t)


def _pair_masks(k_ref, v_ref, ka, kb, va, vb, is_a):
    kv = k_ref[...]
    zk = jnp.zeros_like(kv)
    ka[...] = jnp.where(is_a, kv, zk)
    kb[...] = jnp.where(is_a, zk, kv)
    vv = v_ref[...]
    va[...] = jnp.where(is_a, vv, zk)
    vb[...] = jnp.where(is_a, zk, vv)


def _sb_fwd(qkv):
    s = qkv.shape[0]

    def body(q_ref, k_ref, v_ref, o_ref, ka, kb, va, vb, acc):
        qi = pl.program_id(1)
        lane = lax.broadcasted_iota(jnp.int32, (1, LANES), 1)
        is_a = lane < HEAD_DIM

        @pl.when(qi == 0)
        def _():
            _pair_masks(k_ref, v_ref, ka, kb, va, vb, is_a)

        qs = q_ref[...] * SB_SCALE
        r_i = lax.broadcasted_iota(jnp.int32, (TQ, TK), 0)
        c_i = lax.broadcasted_iota(jnp.int32, (TQ, TK), 1)
        past = c_i < r_i
        upper = (r_i > c_i).astype(BF16)
        acc[...] = jnp.zeros_like(acc)

        def tile(j, carries, diag):
            ks = pl.ds(pl.multiple_of(j * TK, TK), TK)
            out = []
            for kx, vx, c in ((ka, va, carries[0]), (kb, vb, carries[1])):
                z = _dot_nt(qs, kx[ks, :])
                e = jnp.exp(-jnp.abs(z))
                lf = -(jnp.maximum(z, 0.0) + jnp.log(1.0 + e))
                if diag:
                    lf = jnp.where(past, lf, 0.0)
                suf = _hl_dot(lf, upper) + c
                w = jnp.exp(z + lf + suf)
                if diag:
                    w = jnp.where(past, w, 0.0)
                acc[...] += _dot(w.astype(BF16), vx[ks, :])
                out.append(c + jnp.sum(lf, axis=1, keepdims=True))
            return tuple(out)

        zero = jnp.zeros((TQ, 1), F32)
        carries = tile(qi, (zero, zero), True)
        lax.fori_loop(0, qi, lambda n, cs: tile(qi - 1 - n, cs, False), carries)
        o_ref[...] = acc[...]

    slab = lambda off: pl.BlockSpec((s, LANES), lambda p, qi: (0, off + p))
    return pl.pallas_call(
        body, name="sb_fwd", grid=(4, s // TQ),
        in_specs=[pl.BlockSpec((TQ, LANES), lambda p, qi: (qi, p)), slab(4), slab(8)],
        out_specs=pl.BlockSpec((TQ, LANES), lambda p, qi: (qi, p)),
        out_shape=jax.ShapeDtypeStruct((s, D_GRP), F32),
        scratch_shapes=[pltpu.VMEM((s, LANES), BF16)] * 4 + [pltpu.VMEM((TQ, LANES), F32)],
        compiler_params=_params(("arbitrary", "arbitrary"), 40),
    )(qkv, qkv, qkv)


def _sb_bwd(qkv, d_o):
    s = qkv.shape[0]
    nq = s // TQ
    nk = s // TK

    def body(q_ref, k_ref, v_ref, do_ref, dq_ref, dk_ref, dv_ref,
             ka, kb, va, vb, x1s, bts, dqacc, dkacc, dvacc):
        qi = pl.program_id(1)
        lane = lax.broadcasted_iota(jnp.int32, (1, LANES), 1)
        is_a = lane < HEAD_DIM

        @pl.when(qi == 0)
        def _():
            _pair_masks(k_ref, v_ref, ka, kb, va, vb, is_a)
            dkacc[...] = jnp.zeros_like(dkacc)
            dvacc[...] = jnp.zeros_like(dvacc)

        qs = q_ref[...] * SB_SCALE
        zq = jnp.zeros_like(qs)
        qs_x = (jnp.where(is_a, qs, zq), jnp.where(is_a, zq, qs))
        dob = do_ref[...].astype(BF16)
        do_x = (jnp.where(is_a, dob, zq), jnp.where(is_a, zq, dob))
        r_i = lax.broadcasted_iota(jnp.int32, (TQ, TK), 0)
        c_i = lax.broadcasted_iota(jnp.int32, (TQ, TK), 1)
        past = c_i < r_i
        upper = (r_i > c_i).astype(BF16)
        upper_incl = (r_i >= c_i).astype(BF16)
        dqacc[...] = jnp.zeros_like(dqacc)
        k_x = (ka, kb)
        v_x = (va, vb)

        def tile1(j, carries, diag):
            ks = pl.ds(pl.multiple_of(j * TK, TK), TK)
            out = []
            for hx in range(2):
                c, gsum = carries[2 * hx], carries[2 * hx + 1]
                z = _dot_nt(qs, k_x[hx][ks, :])
                e = jnp.exp(-jnp.abs(z))
                den = 1.0 + e
                lf = -(jnp.maximum(z, 0.0) + jnp.log(den))
                rden = 1.0 / den
                pos = z >= 0.0
                beta = jnp.where(pos, rden, e * rden)
                omb = jnp.where(pos, e * rden, rden)
                if diag:
                    lf = jnp.where(past, lf, 0.0)
                    beta = jnp.where(past, beta, 0.0)
                suf = _hl_dot(lf, upper) + c
                a = jnp.exp(z + lf + suf)
                if diag:
                    a = jnp.where(past, a, 0.0)
                d_a = _dot_nt(dob, v_x[hx][ks, :])
                g = a * d_a
                sg = _hl_dot(g, upper_incl) + gsum
                x1s[j, hx] = g * omb + beta * sg
                bts[j, hx] = beta
                dvacc[ks, :] += _dot_tn(a.astype(BF16), do_x[hx])
                out.append(c + jnp.sum(lf, axis=1, keepdims=True))
                out.append(gsum + jnp.sum(g, axis=1, keepdims=True))
            return tuple(out)

        zero = jnp.zeros((TQ, 1), F32)
        carries = tile1(qi, (zero, zero, zero, zero), True)
        carries = lax.fori_loop(0, qi, lambda n, cs: tile1(qi - 1 - n, cs, False), carries)
        g_tot = (carries[1], carries[3])

        def tile2(j, _):
            ks = pl.ds(pl.multiple_of(j * TK, TK), TK)
            for hx in range(2):
                dz = (x1s[j, hx] - bts[j, hx] * g_tot[hx]).astype(BF16)
                dqacc[...] += _dot(dz, k_x[hx][ks, :])
                dkacc[ks, :] += _dot_tn(dz, qs_x[hx])
            return 0

        lax.fori_loop(0, qi + 1, tile2, 0)
        dq_ref[...] = (dqacc[...] * SB_SCALE).astype(BF16)

        @pl.when(qi == nq - 1)
        def _():
            dk_ref[...] = dkacc[...].astype(BF16)
            dv_ref[...] = dvacc[...].astype(BF16)

    slab = lambda off: pl.BlockSpec((s, LANES), lambda p, qi: (0, off + p))
    blk = pl.BlockSpec((TQ, LANES), lambda p, qi: (qi, p))
    out_slab = pl.BlockSpec((s, LANES), lambda p, qi: (0, p))
    shp = jax.ShapeDtypeStruct((s, D_GRP), BF16)
    return pl.pallas_call(
        body, name="sb_bwd", grid=(4, nq),
        in_specs=[blk, slab(4), slab(8), blk],
        out_specs=(blk, out_slab, out_slab), out_shape=(shp, shp, shp),
        scratch_shapes=[pltpu.VMEM((s, LANES), BF16)] * 4
        + [pltpu.VMEM((nk, 2, TQ, TK), F32)] * 2
        + [pltpu.VMEM((TQ, LANES), F32), pltpu.VMEM((s, LANES), F32), pltpu.VMEM((s, LANES), F32)],
        compiler_params=_params(("arbitrary", "arbitrary"), 56),
    )(qkv, qkv, qkv, d_o)


def _mla_fwd(qp, kp, vv):
    s = qp.shape[0]

    def body(q_ref, k_ref, v_ref, o_ref, lse_ref, va, vb, acc):
        qi = pl.program_id(1)
        lane = lax.broadcasted_iota(jnp.int32, (1, LANES), 1)
        is_a = lane < HEAD_DIM

        @pl.when(qi == 0)
        def _():
            vv_ = v_ref[...]
            zv = jnp.zeros_like(vv_)
            va[...] = jnp.where(is_a, vv_, zv)
            vb[...] = jnp.where(is_a, zv, vv_)

        r_i = lax.broadcasted_iota(jnp.int32, (TQ, TK), 0)
        c_i = lax.broadcasted_iota(jnp.int32, (TQ, TK), 1)
        visible = (c_i >> CHUNK_SHIFT) <= (r_i >> CHUNK_SHIFT)
        res = []
        for hx, vx in enumerate((va, vb)):
            sl = slice(hx * LANES, (hx + 1) * LANES)
            q = q_ref[:, sl]
            acc[...] = jnp.zeros_like(acc)

            def tile(j, ml, diag, q=q, vx=vx, sl=sl):
                m, l = ml
                ks = pl.ds(pl.multiple_of(j * TK, TK), TK)
                z = _dot_nt(q, k_ref[ks, sl]) * MLA_SCALE
                if diag:
                    z = jnp.where(visible, z, NEG)
                mn = jnp.maximum(m, jnp.max(z, axis=1, keepdims=True))
                alpha = jnp.exp(m - mn)
                p = jnp.exp(z - mn)
                acc[...] = acc[...] * alpha + _dot(p.astype(BF16), vx[ks, :])
                return mn, l * alpha + jnp.sum(p, axis=1, keepdims=True)

            ml = (jnp.full((TQ, 1), NEG, F32), jnp.zeros((TQ, 1), F32))
            ml = lax.fori_loop(0, qi, lambda j, c: tile(j, c, False), ml)
            m, l = tile(qi, ml, True)
            res.append((acc[...] * (1.0 / l), m + jnp.log(l)))
        o_ref[...] = res[0][0] + res[1][0]
        lse_ref[...] = jnp.where(is_a, res[0][1], res[1][1])

    blk = pl.BlockSpec((TQ, LANES), lambda p, qi: (qi, p))
    shp = jax.ShapeDtypeStruct((s, D_GRP), F32)
    return pl.pallas_call(
        body, name="mla_fwd", grid=(4, s // TQ),
        in_specs=[pl.BlockSpec((TQ, 2 * LANES), lambda p, qi: (qi, p)),
                  pl.BlockSpec((s, 2 * LANES), lambda p, qi: (0, p)),
                  pl.BlockSpec((s, LANES), lambda p, qi: (0, p))],
        out_specs=(blk, blk), out_shape=(shp, shp),
        scratch_shapes=[pltpu.VMEM((s, LANES), BF16)] * 2 + [pltpu.VMEM((TQ, LANES), F32)],
        compiler_params=_params(("arbitrary", "arbitrary"), 40),
    )(qp, kp, vv)


def _mla_bwd(qp, kp, vv, d_o, o, lse):
    s = qp.shape[0]
    nq = s // TQ

    def body(q_ref, k_ref, v_ref, do_ref, o_ref, lse_ref, dq_ref, dk_ref, dv_ref,
             va, vb, dqacc, dkacc, dvacc):
        qi = pl.program_id(1)
        lane = lax.broadcasted_iota(jnp.int32, (1, LANES), 1)
        is_a = lane < HEAD_DIM

        @pl.when(qi == 0)
        def _():
            vv_ = v_ref[...]
            zv = jnp.zeros_like(vv_)
            va[...] = jnp.where(is_a, vv_, zv)
            vb[...] = jnp.where(is_a, zv, vv_)
            dkacc[...] = jnp.zeros_like(dkacc)
            dvacc[...] = jnp.zeros_like(dvacc)

        r_i = lax.broadcasted_iota(jnp.int32, (TQ, TK), 0)
        c_i = lax.broadcasted_iota(jnp.int32, (TQ, TK), 1)
        visible = (c_i >> CHUNK_SHIFT) <= (r_i >> CHUNK_SHIFT)
        d_o = do_ref[...]
        prod = d_o * o_ref[...]
        dob = d_o.astype(BF16)
        zb = jnp.zeros_like(dob)
        lse_blk = lse_ref[...]
        for hx, vx in enumerate((va, vb)):
            sl = slice(hx * LANES, (hx + 1) * LANES)
            mine = is_a if hx == 0 else jnp.logical_not(is_a)
            delta = jnp.sum(jnp.where(mine, prod, 0.0), axis=1, keepdims=True)
            lse_h = jnp.sum(jnp.where(lane == hx * HEAD_DIM, lse_blk, 0.0), axis=1, keepdims=True)
            do_h = jnp.where(mine, dob, zb)
            q = q_ref[:, sl]
            dqacc[...] = jnp.zeros_like(dqacc)

            def tile(j, diag, q=q, vx=vx, sl=sl, delta=delta, lse_h=lse_h, do_h=do_h):
                ks = pl.ds(pl.multiple_of(j * TK, TK), TK)
                kt = k_ref[ks, sl]
                z = _dot_nt(q, kt) * MLA_SCALE
                if diag:
                    z = jnp.where(visible, z, NEG)
                p = jnp.exp(z - lse_h)
                dp = _dot_nt(dob, vx[ks, :])
                ds = (p * (dp - delta) * MLA_SCALE).astype(BF16)
                dqacc[...] += _dot(ds, kt)
                dkacc[ks, sl] += _dot_tn(ds, q)
                dvacc[ks, :] += _dot_tn(p.astype(BF16), do_h)

            def loop(j, c):
                tile(j, False)
                return c

            lax.fori_loop(0, qi, loop, 0)
            tile(qi, True)
            dq_ref[:, sl] = dqacc[...]

        @pl.when(qi == nq - 1)
        def _():
            dk_ref[...] = dkacc[...]
            dv_ref[...] = dvacc[...]

    blk = pl.BlockSpec((TQ, LANES), lambda p, qi: (qi, p))
    return pl.pallas_call(
        body, name="mla_bwd", grid=(4, nq),
        in_specs=[pl.BlockSpec((TQ, 2 * LANES), lambda p, qi: (qi, p)),
                  pl.BlockSpec((s, 2 * LANES), lambda p, qi: (0, p)),
                  pl.BlockSpec((s, LANES), lambda p, qi: (0, p)), blk, blk, blk],
        out_specs=(pl.BlockSpec((TQ, 2 * LANES), lambda p, qi: (qi, p)),
                   pl.BlockSpec((s, 2 * LANES), lambda p, qi: (0, p)),
                   pl.BlockSpec((s, LANES), lambda p, qi: (0, p))),
        out_shape=(jax.ShapeDtypeStruct((s, 1024), F32), jax.ShapeDtypeStruct((s, 1024), F32),
                   jax.ShapeDtypeStruct((s, D_GRP), F32)),
        scratch_shapes=[pltpu.VMEM((s, LANES), BF16)] * 2
        + [pltpu.VMEM((TQ, LANES), F32), pltpu.VMEM((s, 2 * LANES), F32), pltpu.VMEM((s, LANES), F32)],
        compiler_params=_params(("arbitrary", "arbitrary"), 48),
    )(qp, kp, vv, d_o, o, lse)


def _mid(x, p, target, sb_o, mla_o, rest, g_sb, g_mla, w_out, g_post, w_ple, g_ple, w_pg, b_pg, bd):
    s = x.shape[0]

    def body(x_ref, p_ref, t_ref, sbo_ref, mlo_ref, sbg_ref, mlg_ref, gsb_ref, gml_ref, wout_ref,
             gpost_ref, wple_ref, gple_ref, wpg_ref, bpg_ref, bd_ref,
             dx1_ref, dsbo_ref, dmlo_ref, dsbg_ref, dmlg_ref, x1b_ref, dglb_ref, ycb_ref, dyb_ref,
             pb_ref, dub_ref, small_ref):
        i = pl.program_id(0)
        bd_m = bd_ref[...]

        def seg_mean(v):
            return _hl_dot(v, bd_m) * (1.0 / HEAD_DIM)

        groups = []
        for o_ref, gate_ref, gain_ref in ((sbo_ref, sbg_ref, gsb_ref), (mlo_ref, mlg_ref, gml_ref)):
            o = o_ref[...]
            r = lax.rsqrt(seg_mean(o * o) + EPS)
            n = o * r
            hn = n * gain_ref[...]
            gate = gate_ref[...]
            sg = _sigmoid(gate)
            si = gate * sg
            groups.append((r, n, hn, gate, sg, si, gain_ref[...]))
        ya = (groups[0][2] * groups[0][5]).astype(BF16)
        yb = (groups[1][2] * groups[1][5]).astype(BF16)
        ycb_ref[:, :D_GRP] = ya
        ycb_ref[:, D_GRP:] = yb
        y = _dot(ya, wout_ref[:D_GRP, :]) + _dot(yb, wout_ref[D_GRP:, :])
        ry = lax.rsqrt(jnp.mean(y * y, axis=-1, keepdims=True) + EPS)
        ny = y * ry
        x1 = x_ref[...] + ny * gpost_ref[...]
        x1b = x1.astype(BF16)
        x1b_ref[...] = x1b
        pb = p_ref[...].astype(BF16)
        pb_ref[...] = pb
        u = _dot(pb, wple_ref[...])
        ru = lax.rsqrt(jnp.mean(u * u, axis=-1, keepdims=True) + EPS)
        nu = u * ru
        ple = nu * gple_ref[...]
        gate = _sigmoid(_dot(x1b, wpg_ref[...]) + bpg_ref[...])
        x2 = x1 + ple * gate
        diff = x2 - t_ref[...]
        dx2 = diff * (1.0 / D_MODEL)

        d_ple = dx2 * gate
        d_glin = (dx2 * ple) * (gate * (1.0 - gate))
        dglb = d_glin.astype(BF16)
        dglb_ref[...] = dglb
        dx1 = dx2 + _dot_nt(dglb, wpg_ref[...])
        dx1_ref[...] = dx1
        d_nu = d_ple * gple_ref[...]
        d_u = ru * (d_nu - nu * jnp.mean(d_nu * nu, axis=-1, keepdims=True))
        dub_ref[...] = d_u.astype(BF16)
        d_ny = dx1 * gpost_ref[...]
        d_y = ry * (d_ny - ny * jnp.mean(d_ny * ny, axis=-1, keepdims=True))
        dyb = d_y.astype(BF16)
        dyb_ref[...] = dyb
        d_yc = (_dot_nt(dyb, wout_ref[:D_GRP, :]), _dot_nt(dyb, wout_ref[D_GRP:, :]))

        d_gain = []
        for gx, (do_ref, dg_ref) in enumerate(((dsbo_ref, dsbg_ref), (dmlo_ref, dmlg_ref))):
            r, n, hn, gate_g, sg, si, gain = groups[gx]
            dyg = d_yc[gx]
            d_hn = dyg * si
            dg_ref[...] = (dyg * hn * (sg * (1.0 + gate_g * (1.0 - sg)))).astype(BF16)
            d_gain.append(jnp.sum(d_hn * n, axis=0, keepdims=True))
            d_n = d_hn * gain
            do_ref[...] = r * (d_n - n * seg_mean(d_n * n))

        @pl.when(i == 0)
        def _():
            small_ref[...] = jnp.zeros_like(small_ref)

        small_ref[0:1, :] += jnp.sum(d_glin, axis=0, keepdims=True)
        small_ref[1:2, :] += jnp.sum(d_ple * nu, axis=0, keepdims=True)
        small_ref[2:3, :] += jnp.sum(dx1 * ny, axis=0, keepdims=True)
        small_ref[3:4, :D_GRP] += d_gain[0]
        small_ref[3:4, D_GRP:] += d_gain[1]
        small_ref[4:5, :] += jnp.sum(diff * diff, axis=0, keepdims=True) * (0.5 / D_MODEL)

    def row(width, idx=0):
        return pl.BlockSpec((TM, width), lambda i: (i, idx))

    def full(a):
        return pl.BlockSpec(a.shape, lambda i: (0, 0))

    f32 = lambda w: jax.ShapeDtypeStruct((s, w), F32)
    b16 = lambda w: jax.ShapeDtypeStruct((s, w), BF16)
    return pl.pallas_call(
        body, name="mid", grid=(s // TM,),
        in_specs=[row(D_MODEL), row(PLE_DIM), row(D_MODEL), row(D_GRP), row(D_GRP),
                  row(D_GRP, 0), row(D_GRP, 1), full(g_sb), full(g_mla), full(w_out), full(g_post),
                  full(w_ple), full(g_ple), full(w_pg), full(b_pg), full(bd)],
        out_specs=(row(D_MODEL), row(D_GRP), row(D_GRP), row(D_GRP), row(D_GRP), row(D_MODEL),
                   row(D_MODEL), row(D_MODEL), row(D_MODEL), row(PLE_DIM), row(D_MODEL),
                   pl.BlockSpec((8, D_MODEL), lambda i: (0, 0))),
        out_shape=(f32(D_MODEL), f32(D_GRP), f32(D_GRP), b16(D_GRP), b16(D_GRP), b16(D_MODEL),
                   b16(D_MODEL), b16(D_MODEL), b16(D_MODEL), b16(PLE_DIM), b16(D_MODEL),
                   jax.ShapeDtypeStruct((8, D_MODEL), F32)),
        compiler_params=_params(("arbitrary",), 56),
    )(x, p, target, sb_o, mla_o, rest, rest, g_sb, g_mla, w_out, g_post, w_ple, g_ple, w_pg, b_pg, bd)


def _mla_prep_bwd(dqp, dkp, dvv, rest, gq, gkv, wuq, wuk, wuv, cos_t, sin_t):
    s = rest.shape[0]

    def body(dqp_ref, dkp_ref, dvv_ref, cq_ref, ckv_ref, gq_ref, gkv_ref, wuq_ref, wuk_ref, wuv_ref,
             c_ref, s_ref, dcq_ref, dckv_ref, dkr_ref, dqb_ref, dkb_ref, dvb_ref, small_ref):
        i = pl.program_id(0)
        lane = lax.broadcasted_iota(jnp.int32, (1, LANES), 1)
        in_rope = (lane >= HEAD_DIM) & (lane < HEAD_DIM + ROPE_DIM)
        cos_v, sin_v = c_ref[...], s_ref[...]
        dkr_roped = jnp.zeros((TM, LANES), F32)
        for h in range(N_HEADS):
            sl = slice(h * LANES, (h + 1) * LANES)
            dy = dqp_ref[:, sl]
            dqb_ref[:, sl] = (dy * cos_v + _rope_swap(dy * sin_v, lane)).astype(BF16)
            dkh = dkp_ref[:, sl]
            dkb_ref[:, sl] = dkh.astype(BF16)
            dkr_roped = dkr_roped + jnp.where(in_rope, dkh, 0.0)
        dkr_ref[...] = (dkr_roped * cos_v + _rope_swap(dkr_roped * sin_v, lane)).astype(BF16)
        dvb = dvv_ref[...].astype(BF16)
        dvb_ref[...] = dvb

        cq = cq_ref[...]
        rq = lax.rsqrt(jnp.mean(cq * cq, axis=-1, keepdims=True) + EPS)
        nq_ = cq * rq
        d_cqn = _dot_nt(dqb_ref[...], wuq_ref[...])
        d_n = d_cqn * gq_ref[...]
        dcq_ref[...] = (rq * (d_n - nq_ * jnp.mean(d_n * nq_, axis=-1, keepdims=True))).astype(BF16)

        ckv = ckv_ref[...]
        rkv = lax.rsqrt(jnp.mean(ckv * ckv, axis=-1, keepdims=True) + EPS)
        nkv = ckv * rkv
        d_ckvn = _dot_nt(dkb_ref[...], wuk_ref[...]) + _dot_nt(dvb, wuv_ref[...])
        d_n2 = d_ckvn * gkv_ref[...]
        dckv_ref[...] = (rkv * (d_n2 - nkv * jnp.mean(d_n2 * nkv, axis=-1, keepdims=True))).astype(BF16)

        @pl.when(i == 0)
        def _():
            small_ref[...] = jnp.zeros_like(small_ref)

        small_ref[0:1, :] += jnp.sum(d_cqn * nq_, axis=0, keepdims=True)
        small_ref[1:2, :KV_LORA] += jnp.sum(d_ckvn * nkv, axis=0, keepdims=True)

    def row(width, idx=0):
        return pl.BlockSpec((TM, width), lambda i: (i, idx))

    def full(a):
        return pl.BlockSpec(a.shape, lambda i: (0, 0))

    b16 = lambda w: jax.ShapeDtypeStruct((s, w), BF16)
    return pl.pallas_call(
        body, name="mla_prep_bwd", grid=(s // TM,),
        in_specs=[row(1024), row(1024), row(D_GRP), row(Q_LORA, 4), row(KV_LORA, 10), full(gq), full(gkv),
                  full(wuq), full(wuk), full(wuv), row(LANES), row(LANES)],
        out_specs=(row(Q_LORA), row(KV_LORA), row(LANES), row(1024), row(1024), row(D_GRP),
                   pl.BlockSpec((8, Q_LORA), lambda i: (0, 0))),
        out_shape=(b16(Q_LORA), b16(KV_LORA), b16(LANES), b16(1024), b16(1024), b16(D_GRP),
                   jax.ShapeDtypeStruct((8, Q_LORA), F32)),
        compiler_params=_params(("arbitrary",), 40),
    )(dqp, dkp, dvv, rest, rest, gq, gkv, wuq, wuk, wuv, cos_t, sin_t)


def _in_bwd(x, g, dx1, pieces, w):
    s = x.shape[0]
    widths = [a.shape[1] for a in pieces]
    offs = [sum(widths[:k]) for k in range(len(widths))]

    def body(x_ref, g_ref, dx1_ref, *refs):
        piece_refs = refs[:len(pieces)]
        w_ref, dx_ref, small_ref = refs[len(pieces):]
        i = pl.program_id(0)
        dh = jnp.zeros((TM, D_MODEL), F32)
        for pr, off, wd in zip(piece_refs, offs, widths):
            dh = dh + _dot_nt(pr[...], w_ref[:, off:off + wd])
        xv = x_ref[...]
        r = lax.rsqrt(jnp.mean(xv * xv, axis=-1, keepdims=True) + EPS)
        n = xv * r
        d_n = dh * g_ref[...]
        dx_ref[...] = dx1_ref[...] + r * (d_n - n * jnp.mean(d_n * n, axis=-1, keepdims=True))

        @pl.when(i == 0)
        def _():
            small_ref[...] = jnp.zeros_like(small_ref)

        small_ref[0:1, :] += jnp.sum(dh * n, axis=0, keepdims=True)

    def row(width):
        return pl.BlockSpec((TM, width), lambda i: (i, 0))

    return pl.pallas_call(
        body, name="in_bwd", grid=(s // TM,),
        in_specs=[row(D_MODEL), pl.BlockSpec((1, D_MODEL), lambda i: (0, 0)), row(D_MODEL)]
        + [row(wd) for wd in widths] + [pl.BlockSpec(w.shape, lambda i: (0, 0))],
        out_specs=(row(D_MODEL), pl.BlockSpec((8, D_MODEL), lambda i: (0, 0))),
        out_shape=(jax.ShapeDtypeStruct((s, D_MODEL), F32), jax.ShapeDtypeStruct((8, D_MODEL), F32)),
        compiler_params=_params(("arbitrary",), 48),
    )(x, g, dx1, *pieces, w)


def _tn_matmul(a, b, name):
    s, k = a.shape
    n = b.shape[1]
    ts = 512
    tn = min(n, 512)
    steps = s // ts

    def body(a_ref, b_ref, o_ref):
        t = pl.program_id(1)

        @pl.when(t == 0)
        def _():
            o_ref[...] = jnp.zeros_like(o_ref)

        o_ref[...] += _dot_tn(a_ref[...], b_ref[...])

    return pl.pallas_call(
        body, name=name, grid=(n // tn, steps),
        in_specs=[pl.BlockSpec((ts, k), lambda j, t: (t, 0)), pl.BlockSpec((ts, tn), lambda j, t: (t, j))],
        out_specs=pl.BlockSpec((k, tn), lambda j, t: (0, j)),
        out_shape=jax.ShapeDtypeStruct((k, n), F32),
        compiler_params=_params(("parallel", "arbitrary"), 40),
    )(a, b)


def _pack_shards(w_in, w_uq, w_ukv, w_out, w_ple, w_pg):
    return jnp.concatenate([a.reshape(-1, LANES) for a in (w_in, w_uq, w_ukv, w_out, w_ple, w_pg)], axis=0)


def _unpack_shards(packed):
    out, o = [], 0
    for rows, shape in ((R_IN, (D_MODEL, 372)), (R_UQ, (Q_LORA, 96)), (R_UKV, (KV_LORA, 128)),
                        (R_OUT, (128, D_MODEL)), (R_PLE, (PLE_DIM, 128)), (R_PG, (128, D_MODEL))):
        out.append(packed[o:o + rows].reshape((1,) + shape))
        o += rows
    return out


def _unpack_gathered(gath):
    o = 0
    seg = {}
    for name, rows in (("in", R_IN), ("uq", R_UQ), ("ukv", R_UKV), ("out", R_OUT), ("ple", R_PLE), ("pg", R_PG)):
        seg[name] = gath[:, o:o + rows]
        o += rows
    cols = lambda a, k, per: a.reshape(N_DEV, k, per).transpose(1, 0, 2).reshape(k, N_DEV * per)
    return (cols(seg["in"], D_MODEL, 372), cols(seg["uq"], Q_LORA, 96), cols(seg["ukv"], KV_LORA, 128),
            seg["out"].reshape(D_MODEL, D_MODEL), cols(seg["ple"], PLE_DIM, 128),
            seg["pg"].reshape(D_MODEL, D_MODEL))


def _pack_full_grads(d_in, d_uq, d_ukv, d_out, d_ple, d_pg):
    cols = lambda a, per: a.reshape(a.shape[0], N_DEV, per).transpose(1, 0, 2).reshape(N_DEV, -1, LANES)
    return jnp.concatenate([cols(d_in, 372), cols(d_uq, 96), cols(d_ukv, 128),
                            d_out.reshape(N_DEV, -1, LANES), cols(d_ple, 128),
                            d_pg.reshape(N_DEV, -1, LANES)], axis=1)


_VEC_SIZES = (D_MODEL, Q_LORA, KV_LORA, D_GRP, D_GRP, D_MODEL, D_MODEL, D_MODEL)


def _pack_vectors(vs, tail=None):
    flat = jnp.concatenate([v.reshape(-1) for v in vs])
    if tail is None:
        tail = jnp.zeros((D_MODEL,), F32)
    flat = jnp.concatenate([flat, tail.reshape(-1)])
    flat = jnp.pad(flat, (0, R_SMALL * LANES - flat.shape[0]))
    return flat.reshape(R_SMALL, LANES)


def _unpack_vectors(packed):
    flat = packed.reshape(-1)
    out, o = [], 0
    for n in _VEC_SIZES:
        out.append(flat[o:o + n].reshape(1, n))
        o += n
    return out, flat[o:o + D_MODEL]


def kernel(x, p, positions, norm_pre_g, w_in, q_norm_g, w_uq, kv_norm_g, w_ukv, sb_out_norm_g, mla_out_norm_g, w_out, norm_post_g, w_ple, ple_norm_g, w_ple_gate, b_ple_gate, loss_target, m_norm_pre_g, m_w_in, m_q_norm_g, m_w_uq, m_kv_norm_g, m_w_ukv, m_sb_out_norm_g, m_mla_out_norm_g, m_w_out, m_norm_post_g, m_w_ple, m_ple_norm_g, m_w_ple_gate, m_b_ple_gate, v_norm_pre_g, v_w_in, v_q_norm_g, v_w_uq, v_kv_norm_g, v_w_ukv, v_sb_out_norm_g, v_mla_out_norm_g, v_w_out, v_norm_post_g, v_w_ple, v_ple_norm_g, v_w_ple_gate, v_b_ple_gate):
    xs, ps, tgt = x[0], p[0, 0], loss_target[0]
    s = xs.shape[0]

    w_packed = _pack_shards(w_in, w_uq, w_ukv, w_out, w_ple, w_ple_gate)
    gath = _all_gather(w_packed.astype(BF16)).reshape(N_DEV, R_BIG, LANES)
    f_in, f_uq, f_ukv, f_out, f_ple, f_pg = _unpack_gathered(gath)
    grad_x, d_mats, vec_partials, loss_vec = _local_grads(
        xs, ps, positions[0], tgt, norm_pre_g, q_norm_g, kv_norm_g, sb_out_norm_g, mla_out_norm_g,
        norm_post_g, ple_norm_g, b_ple_gate, f_in, f_uq, f_ukv, f_out, f_ple, f_pg)

    g_all = _pack_full_grads(*d_mats)
    small = _pack_vectors(vec_partials, tail=loss_vec)
    land, sland = _grad_exchange(g_all, small)
    return _update(land, sland, grad_x, (w_in, w_uq, w_ukv, w_out, w_ple, w_ple_gate), (m_w_in, m_w_uq, m_w_ukv, m_w_out, m_w_ple, m_w_ple_gate), (v_w_in, v_w_uq, v_w_ukv, v_w_out, v_w_ple, v_w_ple_gate), (norm_pre_g, q_norm_g, kv_norm_g, sb_out_norm_g, mla_out_norm_g, norm_post_g, ple_norm_g, b_ple_gate), (m_norm_pre_g, m_q_norm_g, m_kv_norm_g, m_sb_out_norm_g, m_mla_out_norm_g, m_norm_post_g, m_ple_norm_g, m_b_ple_gate), (v_norm_pre_g, v_q_norm_g, v_kv_norm_g, v_sb_out_norm_g, v_mla_out_norm_g, v_norm_post_g, v_ple_norm_g, v_b_ple_gate))


def _local_grads(xs, ps, pos, tgt, norm_pre_g, q_norm_g, kv_norm_g, sb_out_norm_g, mla_out_norm_g,
                 norm_post_g, ple_norm_g, b_ple_gate, f_in, f_uq, f_ukv, f_out, f_ple, f_pg):
    s = xs.shape[0]
    zc = lambda n: jnp.zeros((D_MODEL, n), BF16)
    w_in_p = jnp.concatenate([f_in[:, :2048], f_in[:, 2464:2976], f_in[:, 2048:2432],
                              zc(64), f_in[:, 2432:2464], zc(32)], axis=1)
    w_uq_p = jnp.pad(f_uq.reshape(Q_LORA, N_HEADS, 96), ((0, 0), (0, 0), (0, 32))).reshape(Q_LORA, 1024)
    ukv4 = f_ukv.reshape(KV_LORA, N_HEADS, 2, HEAD_DIM)
    w_uk_p = jnp.pad(ukv4[:, :, 0], ((0, 0), (0, 0), (0, 64))).reshape(KV_LORA, 1024)
    w_uv = ukv4[:, :, 1].reshape(KV_LORA, D_GRP)

    half = ROPE_DIM // 2
    freq = ROPE_THETA ** (-jnp.arange(half, dtype=F32) / half)
    ang = pos.astype(F32)[:, None] * freq
    cos, sin = jnp.cos(ang), jnp.sin(ang)
    cos_t = jnp.concatenate([jnp.ones((s, 64), F32), cos, cos, jnp.zeros((s, 32), F32)], axis=1)
    sin_t = jnp.concatenate([jnp.zeros((s, 64), F32), -sin, sin, jnp.zeros((s, 32), F32)], axis=1)
    seg = jnp.arange(D_GRP) // HEAD_DIM
    bd = (seg[:, None] == seg[None, :]).astype(BF16)

    qkv, rest, h_b = _in_proj(xs, norm_pre_g, w_in_p)
    sb_o = _sb_fwd(qkv)
    qp, kp, vv, cqn_b, ckvn_b = _mla_prep(rest, q_norm_g, kv_norm_g, w_uq_p, w_uk_p, w_uv, cos_t, sin_t)
    mla_o, lse = _mla_fwd(qp, kp, vv)

    (dx1, d_sbo, d_mlo, d_sbg, d_mlg, x1_b, dgl_b, yc_b, dy_b, p_b, du_b, small_mid) = _mid(
        xs, ps, tgt, sb_o, mla_o, rest, sb_out_norm_g, mla_out_norm_g, f_out, norm_post_g,
        f_ple, ple_norm_g, f_pg, b_ple_gate, bd)
    dqp, dkp, dvv = _mla_bwd(qp, kp, vv, d_mlo, mla_o, lse)
    dq_sb, dk_sb, dv_sb = _sb_bwd(qkv, d_sbo)
    dcq, dckv, dkr, dq_b, dk_b, dv_b, small_prep = _mla_prep_bwd(
        dqp, dkp, dvv, rest, q_norm_g, kv_norm_g, w_uq_p, w_uk_p, w_uv, cos_t, sin_t)
    pieces = [dq_sb, dk_sb, dv_sb, d_sbg, d_mlg, dcq, dckv, dkr]
    grad_x, small_in = _in_bwd(xs, norm_pre_g, dx1, pieces, w_in_p)

    d_cols = [_tn_matmul(h_b, pc, "dw_in_%d" % k) for k, pc in enumerate(pieces)]
    d_in = jnp.concatenate([d_cols[0], d_cols[1], d_cols[2], d_cols[3], d_cols[5], d_cols[6],
                            d_cols[7][:, 64:96], d_cols[4]], axis=1)
    d_uq = _tn_matmul(cqn_b, dq_b, "dw_uq").reshape(Q_LORA, N_HEADS, LANES)[:, :, :96].reshape(Q_LORA, 768)
    d_uk = _tn_matmul(ckvn_b, dk_b, "dw_uk").reshape(KV_LORA, N_HEADS, LANES)[:, :, :HEAD_DIM]
    d_uv = _tn_matmul(ckvn_b, dv_b, "dw_uv").reshape(KV_LORA, N_HEADS, HEAD_DIM)
    d_ukv = jnp.stack([d_uk, d_uv], axis=2).reshape(KV_LORA, 1024)
    d_out = _tn_matmul(yc_b, dy_b, "dw_out")
    d_ple = _tn_matmul(p_b, du_b, "dw_ple")
    d_pg = _tn_matmul(x1_b, dgl_b, "dw_pg")

    vec_partials = [small_in[0], small_prep[0], small_prep[1, :KV_LORA], small_mid[3, :D_GRP],
                    small_mid[3, D_GRP:], small_mid[2], small_mid[1], small_mid[0]]
    return grad_x, (d_in, d_uq, d_ukv, d_out, d_ple, d_pg), vec_partials, small_mid[4]


def _update(land, sland, grad_x, mats, m_mats, v_mats, vecs, m_vecs, v_vecs):
    big = _adamw(land, _pack_shards(*mats), _pack_shards(*m_mats), _pack_shards(*v_mats), 800, "adamw_matrices")
    sm = _adamw(sland, _pack_vectors(vecs), _pack_vectors(m_vecs), _pack_vectors(v_vecs), R_SMALL, "adamw_vectors")

    loss = jnp.sum(_unpack_vectors(sm[0])[1])
    outs = []
    for kind in range(4):
        mat = _unpack_shards(big[kind])
        vec = _unpack_vectors(sm[kind])[0]
        outs += [vec[0], mat[0], vec[1], mat[1], vec[2], mat[2], vec[3], vec[4], mat[3], vec[5],
                 mat[4], vec[6], mat[5], vec[7]]
    return (loss, grad_x[None], *outs)
```

```python
import jax
import jax.numpy as jnp
from jax import lax
from jax.experimental import pallas as pl
from jax.experimental.pallas import tpu as pltpu

F32 = jnp.float32
BF16 = jnp.bfloat16
MESH = pl.DeviceIdType.MESH

N_DEV = 8
D_MODEL = 1024
N_HEADS = 8
HEAD_DIM = 64
D_GRP = N_HEADS * HEAD_DIM
Q_LORA = 256
KV_LORA = 128
ROPE_DIM = 32
PLE_DIM = 256
CHUNK_SHIFT = 6
ROPE_THETA = 10000.0
EPS = 1e-6
SB_SCALE = HEAD_DIM ** -0.5
MLA_SCALE = (HEAD_DIM + ROPE_DIM) ** -0.5
NEG = -1e30
SB_CUTOFF = 110.0

ADAM_LR = 0.001
ADAM_B1 = 0.9
ADAM_B2 = 0.999
ADAM_EPS = 1e-08
ADAM_WD = 0.01
ADAM_STEP = 10

LANES = 128
TQ = 256
TK = 256
TM = 256

R_IN, R_UQ, R_UKV, R_OUT, R_PLE, R_PG = 2976, 192, 128, 1024, 256, 1024
R_BIG = R_IN + R_UQ + R_UKV + R_OUT + R_PLE + R_PG
R_SMALL = 56
D_IN_P = 3072

_NT = (((1,), (1,)), ((), ()))
_TN = (((0,), (0,)), ((), ()))


def _params(sem, vmem_mb):
    return pltpu.CompilerParams(dimension_semantics=sem, vmem_limit_bytes=vmem_mb << 20)


def _dot(a, b):
    return jnp.dot(a, b, preferred_element_type=F32)


def _dot_nt(a, b):
    return lax.dot_general(a, b, _NT, preferred_element_type=F32)


def _dot_tn(a, b):
    return lax.dot_general(a, b, _TN, preferred_element_type=F32)


def _hl_dot(a, b):
    hi = a.astype(BF16)
    lo = (a - hi.astype(F32)).astype(BF16)
    return _dot(hi, b) + _dot(lo, b)


def _sigmoid(x):
    return 1.0 / (1.0 + jnp.exp(-x))


def _rope_swap(x, lane):
    left = pltpu.roll(x, LANES - 16, axis=1)
    right = pltpu.roll(x, 16, axis=1)
    lo = (lane >= 64) & (lane < 80)
    hi = (lane >= 80) & (lane < 96)
    return jnp.where(lo, left, jnp.where(hi, right, 0.0))


def _all_gather(shard):
    m_per, n = shard.shape

    def body(x_ref, out_ref, send_sems, recv_sems, local_sem):
        x, y, c = lax.axis_index("x"), lax.axis_index("y"), lax.axis_index("c")
        me, sibling = (x, y, c), (x, y, 1 - c)
        chips = [(1 - x, y), (x, 1 - y), (1 - x, 1 - y)]

        def rows(px, py, pc):
            return out_ref.at[pl.ds((4 * px + 2 * py + pc) * m_per, m_per), :]

        def copy(k, block, to, src=None):
            return pltpu.make_async_remote_copy(
                src_ref=rows(*block) if src is None else src, dst_ref=rows(*block),
                send_sem=send_sems.at[k], recv_sem=recv_sems.at[k],
                device_id=to, device_id_type=MESH)

        mine = pltpu.make_async_copy(x_ref, rows(*me), local_sem)
        mine.start()
        first = [copy(0, me, sibling, src=x_ref)]
        first += [copy(1 + j, me, (*chip, c), src=x_ref) for j, chip in enumerate(chips)]
        for cp in first:
            cp.start()
        passed = [copy(4 + j, (*chip, c), sibling) for j, chip in enumerate(chips)]
        for j, chip in enumerate(chips):
            copy(1 + j, (*chip, c), me).wait_recv()
            passed[j].start()
        copy(0, sibling, me).wait_recv()
        for j, chip in enumerate(chips):
            copy(4 + j, (*chip, 1 - c), me).wait_recv()
        for cp in first + passed:
            cp.wait_send()
        mine.wait()

    return pl.pallas_call(
        body, name="weight_all_gather",
        out_shape=jax.ShapeDtypeStruct((N_DEV * m_per, n), shard.dtype),
        in_specs=[pl.BlockSpec(memory_space=pltpu.VMEM)],
        out_specs=pl.BlockSpec(memory_space=pltpu.VMEM),
        scratch_shapes=[pltpu.SemaphoreType.DMA((7,)), pltpu.SemaphoreType.DMA((7,)),
                        pltpu.SemaphoreType.DMA],
        compiler_params=pltpu.CompilerParams(vmem_limit_bytes=40 << 20),
    )(shard)


def _grad_exchange(g_all, small):
    _, r, n = g_all.shape
    sr = small.shape[0]

    def body(g_ref, s_ref, land_ref, sland_ref, ssem, rsem, ssem2, rsem2, lsem):
        x, y, c = lax.axis_index("x"), lax.axis_index("y"), lax.axis_index("c")
        me = 4 * x + 2 * y + c
        own = pltpu.make_async_copy(g_ref.at[me], land_ref.at[me], lsem.at[0])
        own2 = pltpu.make_async_copy(s_ref, sland_ref.at[me], lsem.at[1])
        own.start()
        own2.start()
        copies = []
        for k in range(1, N_DEV):
            px = 1 - x if (k >> 2) & 1 else x
            py = 1 - y if (k >> 1) & 1 else y
            pc = 1 - c if k & 1 else c
            pid = 4 * px + 2 * py + pc
            copies.append(pltpu.make_async_remote_copy(
                src_ref=g_ref.at[pid], dst_ref=land_ref.at[me],
                send_sem=ssem.at[k], recv_sem=rsem.at[k], device_id=(px, py, pc), device_id_type=MESH))
            copies.append(pltpu.make_async_remote_copy(
                src_ref=s_ref, dst_ref=sland_ref.at[me],
                send_sem=ssem2.at[k], recv_sem=rsem2.at[k], device_id=(px, py, pc), device_id_type=MESH))
        for cp in copies:
            cp.start()
        for cp in copies:
            cp.wait()
        own.wait()
        own2.wait()

    any_spec = pl.BlockSpec(memory_space=pl.ANY)
    return pl.pallas_call(
        body, name="grad_exchange",
        out_shape=(jax.ShapeDtypeStruct((N_DEV, r, n), F32), jax.ShapeDtypeStruct((N_DEV, sr, n), F32)),
        in_specs=[any_spec, any_spec], out_specs=(any_spec, any_spec),
        scratch_shapes=[pltpu.SemaphoreType.DMA((N_DEV,)), pltpu.SemaphoreType.DMA((N_DEV,)),
                        pltpu.SemaphoreType.DMA((N_DEV,)), pltpu.SemaphoreType.DMA((N_DEV,)),
                        pltpu.SemaphoreType.DMA((2,))],
    )(g_all, small)


def _adamw(land, w, m, v, block_rows, name):
    _, r, n = land.shape
    c1 = 1.0 - ADAM_B1
    c2 = 1.0 - ADAM_B2
    bc1 = 1.0 - ADAM_B1 ** ADAM_STEP
    bc2 = 1.0 - ADAM_B2 ** ADAM_STEP

    def body(l_ref, w_ref, m_ref, v_ref, g_out, d_out, m_out, v_out):
        g = l_ref[0]
        for j in range(1, N_DEV):
            g = g + l_ref[j]
        mn = ADAM_B1 * m_ref[...] + c1 * g
        vn = ADAM_B2 * v_ref[...] + c2 * (g * g)
        m_hat = mn / bc1
        v_hat = vn / bc2
        g_out[...] = g
        d_out[...] = -ADAM_LR * (m_hat / (jnp.sqrt(v_hat) + ADAM_EPS) + ADAM_WD * w_ref[...])
        m_out[...] = mn
        v_out[...] = vn

    row = pl.BlockSpec((block_rows, n), lambda i: (i, 0))
    shp = jax.ShapeDtypeStruct((r, n), F32)
    return pl.pallas_call(
        body, name=name, grid=(r // block_rows,),
        in_specs=[pl.BlockSpec((N_DEV, block_rows, n), lambda i: (0, i, 0)), row, row, row],
        out_specs=(row, row, row, row), out_shape=(shp, shp, shp, shp),
        compiler_params=_params(("parallel",), 40),
    )(land, w, m, v)


def _in_proj(x, g, w):
    s = x.shape[0]

    def body(x_ref, g_ref, w_ref, qkv_ref, rest_ref, h_ref):
        xv = x_ref[...]
        r = lax.rsqrt(jnp.mean(xv * xv, axis=-1, keepdims=True) + EPS)
        h = ((xv * r) * g_ref[...]).astype(BF16)
        h_ref[...] = h
        qkv_ref[...] = _dot(h, w_ref[:, :1536]).astype(BF16)
        rest_ref[...] = _dot(h, w_ref[:, 1536:])

    return pl.pallas_call(
        body, name="in_proj", grid=(s // TM,),
        in_specs=[pl.BlockSpec((TM, D_MODEL), lambda i: (i, 0)),
                  pl.BlockSpec((1, D_MODEL), lambda i: (0, 0)),
                  pl.BlockSpec((D_MODEL, D_IN_P), lambda i: (0, 0))],
        out_specs=(pl.BlockSpec((TM, 1536), lambda i: (i, 0)),
                   pl.BlockSpec((TM, 1536), lambda i: (i, 0)),
                   pl.BlockSpec((TM, D_MODEL), lambda i: (i, 0))),
        out_shape=(jax.ShapeDtypeStruct((s, 1536), BF16), jax.ShapeDtypeStruct((s, 1536), F32),
                   jax.ShapeDtypeStruct((s, D_MODEL), BF16)),
        compiler_params=_params(("parallel",), 48),
    )(x, g, w)


def _mla_prep(rest, gq, gkv, wuq, wuk, wuv, cos_t, sin_t):
    s = rest.shape[0]

    def body(cq_ref, ckv_ref, kr_ref, gq_ref, gkv_ref, wuq_ref, wuk_ref, wuv_ref, c_ref, s_ref,
             qp_ref, kp_ref, vv_ref, cqn_ref, ckvn_ref):
        lane = lax.broadcasted_iota(jnp.int32, (1, LANES), 1)
        cos_v, sin_v = c_ref[...], s_ref[...]
        cq = cq_ref[...]
        rq = lax.rsqrt(jnp.mean(cq * cq, axis=-1, keepdims=True) + EPS)
        cqn = ((cq * rq) * gq_ref[...]).astype(BF16)
        cqn_ref[...] = cqn
        q = _dot(cqn, wuq_ref[...])
        ckv = ckv_ref[...]
        rkv = lax.rsqrt(jnp.mean(ckv * ckv, axis=-1, keepdims=True) + EPS)
        ckvn = ((ckv * rkv) * gkv_ref[...]).astype(BF16)
        ckvn_ref[...] = ckvn
        kn = _dot(ckvn, wuk_ref[...])
        vv_ref[...] = _dot(ckvn, wuv_ref[...]).astype(BF16)
        kr = kr_ref[...]
        kr_roped = kr * cos_v + _rope_swap(kr, lane) * sin_v
        for h in range(N_HEADS):
            sl = slice(h * LANES, (h + 1) * LANES)
            qh = q[:, sl]
            qp_ref[:, sl] = (qh * cos_v + _rope_swap(qh, lane) * sin_v).astype(BF16)
            kp_ref[:, sl] = (kn[:, sl] + kr_roped).astype(BF16)

    def row(width, idx):
        return pl.BlockSpec((TM, width), lambda i: (i, idx))

    def full(a):
        return pl.BlockSpec(a.shape, lambda i: (0, 0))

    return pl.pallas_call(
        body, name="mla_prep", grid=(s // TM,),
        in_specs=[row(Q_LORA, 4), row(KV_LORA, 10), row(LANES, 11), full(gq), full(gkv),
                  full(wuq), full(wuk), full(wuv), row(LANES, 0), row(LANES, 0)],
        out_specs=(row(1024, 0), row(1024, 0), row(D_GRP, 0), row(Q_LORA, 0), row(KV_LORA, 0)),
        out_shape=(jax.ShapeDtypeStruct((s, 1024), BF16), jax.ShapeDtypeStruct((s, 1024), BF16),
                   jax.ShapeDtypeStruct((s, D_GRP), BF16), jax.ShapeDtypeStruct((s, Q_LORA), BF16),
                   jax.ShapeDtypeStruct((s, KV_LORA), BF16)),
        compiler_params=_params(("parallel",), 32),
    )(rest, rest, rest, gq, gkv, wuq, wuk, wuv, cos_t, sin_t)


def _pair_masks(k_ref, v_ref, ka, kb, va, vb, is_a):
    kv = k_ref[...]
    zk = jnp.zeros_like(kv)
    ka[...] = jnp.where(is_a, kv, zk)
    kb[...] = jnp.where(is_a, zk, kv)
    vv = v_ref[...]
    va[...] = jnp.where(is_a, vv, zk)
    vb[...] = jnp.where(is_a, zk, vv)


def _sb_live(n, qi, c_a, c_b):
    live = jnp.maximum(jnp.max(c_a), jnp.max(c_b)) > -SB_CUTOFF
    return jnp.logical_and(n < qi, live)


def _sb_fwd(qkv):
    s = qkv.shape[0]

    def body(q_ref, k_ref, v_ref, o_ref, ka, kb, va, vb, acc):
        qi = pl.program_id(1)
        lane = lax.broadcasted_iota(jnp.int32, (1, LANES), 1)
        is_a = lane < HEAD_DIM

        @pl.when(qi == 0)
        def _():
            _pair_masks(k_ref, v_ref, ka, kb, va, vb, is_a)

        qs = q_ref[...] * SB_SCALE
        r_i = lax.broadcasted_iota(jnp.int32, (TQ, TK), 0)
        c_i = lax.broadcasted_iota(jnp.int32, (TQ, TK), 1)
        past = c_i < r_i
        upper = (r_i > c_i).astype(BF16)
        acc[...] = jnp.zeros_like(acc)

        def tile(j, carries, diag):
            ks = pl.ds(pl.multiple_of(j * TK, TK), TK)
            out = []
            for kx, vx, c in ((ka, va, carries[0]), (kb, vb, carries[1])):
                z = _dot_nt(qs, kx[ks, :])
                e = jnp.exp(-jnp.abs(z))
                lf = -(jnp.maximum(z, 0.0) + jnp.log(1.0 + e))
                if diag:
                    lf = jnp.where(past, lf, 0.0)
                suf = _hl_dot(lf, upper) + c
                w = jnp.exp(z + lf + suf)
                if diag:
                    w = jnp.where(past, w, 0.0)
                acc[...] += _dot(w.astype(BF16), vx[ks, :])
                out.append(c + jnp.sum(lf, axis=1, keepdims=True))
            return tuple(out)

        zero = jnp.zeros((TQ, 1), F32)
        carries = tile(qi, (zero, zero), True)

        def step(st):
            return (st[0] + 1,) + tile(qi - 1 - st[0], st[1:], False)

        lax.while_loop(lambda st: _sb_live(st[0], qi, st[1], st[2]), step, (0,) + carries)
        o_ref[...] = acc[...]

    slab = lambda off: pl.BlockSpec((s, LANES), lambda p, qi: (0, off + p))
    return pl.pallas_call(
        body, name="sb_fwd", grid=(4, s // TQ),
        in_specs=[pl.BlockSpec((TQ, LANES), lambda p, qi: (qi, p)), slab(4), slab(8)],
        out_specs=pl.BlockSpec((TQ, LANES), lambda p, qi: (qi, p)),
        out_shape=jax.ShapeDtypeStruct((s, D_GRP), F32),
        scratch_shapes=[pltpu.VMEM((s, LANES), BF16)] * 4 + [pltpu.VMEM((TQ, LANES), F32)],
        compiler_params=_params(("arbitrary", "arbitrary"), 40),
    )(qkv, qkv, qkv)


def _sb_bwd(qkv, d_o):
    s = qkv.shape[0]
    nq = s // TQ
    nk = s // TK

    def body(q_ref, k_ref, v_ref, do_ref, dq_ref, dk_ref, dv_ref,
             ka, kb, va, vb, x1s, bts, dqacc, dkacc, dvacc):
        qi = pl.program_id(1)
        lane = lax.broadcasted_iota(jnp.int32, (1, LANES), 1)
        is_a = lane < HEAD_DIM

        @pl.when(qi == 0)
        def _():
            _pair_masks(k_ref, v_ref, ka, kb, va, vb, is_a)
            dkacc[...] = jnp.zeros_like(dkacc)
            dvacc[...] = jnp.zeros_like(dvacc)

        qs = q_ref[...] * SB_SCALE
        zq = jnp.zeros_like(qs)
        qs_x = (jnp.where(is_a, qs, zq), jnp.where(is_a, zq, qs))
        dob = do_ref[...].astype(BF16)
        do_x = (jnp.where(is_a, dob, zq), jnp.where(is_a, zq, dob))
        r_i = lax.broadcasted_iota(jnp.int32, (TQ, TK), 0)
        c_i = lax.broadcasted_iota(jnp.int32, (TQ, TK), 1)
        past = c_i < r_i
        upper = (r_i > c_i).astype(BF16)
        upper_incl = (r_i >= c_i).astype(BF16)
        dqacc[...] = jnp.zeros_like(dqacc)
        k_x = (ka, kb)
        v_x = (va, vb)

        def tile1(j, carries, diag):
            ks = pl.ds(pl.multiple_of(j * TK, TK), TK)
            out = []
            for hx in range(2):
                c, gsum = carries[2 * hx], carries[2 * hx + 1]
                z = _dot_nt(qs, k_x[hx][ks, :])
                e = jnp.exp(-jnp.abs(z))
                den = 1.0 + e
                lf = -(jnp.maximum(z, 0.0) + jnp.log(den))
                rden = 1.0 / den
                pos = z >= 0.0
                beta = jnp.where(pos, rden, e * rden)
                omb = jnp.where(pos, e * rden, rden)
                if diag:
                    lf = jnp.where(past, lf, 0.0)
                    beta = jnp.where(past, beta, 0.0)
                suf = _hl_dot(lf, upper) + c
                a = jnp.exp(z + lf + suf)
                if diag:
                    a = jnp.where(past, a, 0.0)
                d_a = _dot_nt(dob, v_x[hx][ks, :])
                g = a * d_a
                sg = _hl_dot(g, upper_incl) + gsum
                x1s[j, hx] = g * omb + beta * sg
                bts[j, hx] = beta
                dvacc[ks, :] += _dot_tn(a.astype(BF16), do_x[hx])
                out.append(c + jnp.sum(lf, axis=1, keepdims=True))
                out.append(gsum + jnp.sum(g, axis=1, keepdims=True))
            return tuple(out)

        zero = jnp.zeros((TQ, 1), F32)
        carries = tile1(qi, (zero, zero, zero, zero), True)

        def step(st):
            return (st[0] + 1,) + tile1(qi - 1 - st[0], st[1:], False)

        swept = lax.while_loop(lambda st: _sb_live(st[0], qi, st[1], st[3]), step, (0,) + carries)
        g_tot = (swept[2], swept[4])

        def tile2(j, _):
            ks = pl.ds(pl.multiple_of(j * TK, TK), TK)
            for hx in range(2):
                dz = (x1s[j, hx] - bts[j, hx] * g_tot[hx]).astype(BF16)
                dqacc[...] += _dot(dz, k_x[hx][ks, :])
                dkacc[ks, :] += _dot_tn(dz, qs_x[hx])
            return 0

        lax.fori_loop(qi - swept[0], qi + 1, tile2, 0)
        dq_ref[...] = (dqacc[...] * SB_SCALE).astype(BF16)

        @pl.when(qi == nq - 1)
        def _():
            dk_ref[...] = dkacc[...].astype(BF16)
            dv_ref[...] = dvacc[...].astype(BF16)

    slab = lambda off: pl.BlockSpec((s, LANES), lambda p, qi: (0, off + p))
    blk = pl.BlockSpec((TQ, LANES), lambda p, qi: (qi, p))
    out_slab = pl.BlockSpec((s, LANES), lambda p, qi: (0, p))
    shp = jax.ShapeDtypeStruct((s, D_GRP), BF16)
    return pl.pallas_call(
        body, name="sb_bwd", grid=(4, nq),
        in_specs=[blk, slab(4), slab(8), blk],
        out_specs=(blk, out_slab, out_slab), out_shape=(shp, shp, shp),
        scratch_shapes=[pltpu.VMEM((s, LANES), BF16)] * 4
        + [pltpu.VMEM((nk, 2, TQ, TK), F32)] * 2
        + [pltpu.VMEM((TQ, LANES), F32), pltpu.VMEM((s, LANES), F32), pltpu.VMEM((s, LANES), F32)],
        compiler_params=_params(("arbitrary", "arbitrary"), 56),
    )(qkv, qkv, qkv, d_o)


def _mla_fwd(qp, kp, vv):
    s = qp.shape[0]

    def body(q_ref, k_ref, v_ref, o_ref, lse_ref, va, vb, acc):
        qi = pl.program_id(1)
        lane = lax.broadcasted_iota(jnp.int32, (1, LANES), 1)
        is_a = lane < HEAD_DIM

        @pl.when(qi == 0)
        def _():
            vv_ = v_ref[...]
            zv = jnp.zeros_like(vv_)
            va[...] = jnp.where(is_a, vv_, zv)
            vb[...] = jnp.where(is_a, zv, vv_)

        r_i = lax.broadcasted_iota(jnp.int32, (TQ, TK), 0)
        c_i = lax.broadcasted_iota(jnp.int32, (TQ, TK), 1)
        visible = (c_i >> CHUNK_SHIFT) <= (r_i >> CHUNK_SHIFT)
        acc[...] = jnp.zeros_like(acc)
        v_x = (va, vb)

        def tile(j, state, diag):
            ks = pl.ds(pl.multiple_of(j * TK, TK), TK)
            out = []
            for hx in range(2):
                sl = slice(hx * LANES, (hx + 1) * LANES)
                m, l = state[2 * hx], state[2 * hx + 1]
                z = _dot_nt(q_ref[:, sl], k_ref[ks, sl]) * MLA_SCALE
                if diag:
                    z = jnp.where(visible, z, NEG)
                mn = jnp.maximum(m, jnp.max(z, axis=1, keepdims=True))
                alpha = jnp.exp(m - mn)
                p = jnp.exp(z - mn)
                acc[hx] = acc[hx] * alpha + _dot(p.astype(BF16), v_x[hx][ks, :])
                out += [mn, l * alpha + jnp.sum(p, axis=1, keepdims=True)]
            return tuple(out)

        m0, l0 = jnp.full((TQ, 1), NEG, F32), jnp.zeros((TQ, 1), F32)
        state = lax.fori_loop(0, qi, lambda j, c: tile(j, c, False), (m0, l0, m0, l0))
        m_a, l_a, m_b, l_b = tile(qi, state, True)
        o_ref[...] = acc[0] * (1.0 / l_a) + acc[1] * (1.0 / l_b)
        lse_ref[...] = jnp.where(is_a, m_a + jnp.log(l_a), m_b + jnp.log(l_b))

    blk = pl.BlockSpec((TQ, LANES), lambda p, qi: (qi, p))
    shp = jax.ShapeDtypeStruct((s, D_GRP), F32)
    return pl.pallas_call(
        body, name="mla_fwd", grid=(4, s // TQ),
        in_specs=[pl.BlockSpec((TQ, 2 * LANES), lambda p, qi: (qi, p)),
                  pl.BlockSpec((s, 2 * LANES), lambda p, qi: (0, p)),
                  pl.BlockSpec((s, LANES), lambda p, qi: (0, p))],
        out_specs=(blk, blk), out_shape=(shp, shp),
        scratch_shapes=[pltpu.VMEM((s, LANES), BF16)] * 2 + [pltpu.VMEM((2, TQ, LANES), F32)],
        compiler_params=_params(("arbitrary", "arbitrary"), 40),
    )(qp, kp, vv)


def _mla_bwd(qp, kp, vv, d_o, o, lse):
    s = qp.shape[0]
    nq = s // TQ

    def body(q_ref, k_ref, v_ref, do_ref, o_ref, lse_ref, dq_ref, dk_ref, dv_ref,
             va, vb, dqacc, dkacc, dvacc):
        qi = pl.program_id(1)
        lane = lax.broadcasted_iota(jnp.int32, (1, LANES), 1)
        is_a = lane < HEAD_DIM

        @pl.when(qi == 0)
        def _():
            vv_ = v_ref[...]
            zv = jnp.zeros_like(vv_)
            va[...] = jnp.where(is_a, vv_, zv)
            vb[...] = jnp.where(is_a, zv, vv_)
            dkacc[...] = jnp.zeros_like(dkacc)
            dvacc[...] = jnp.zeros_like(dvacc)

        r_i = lax.broadcasted_iota(jnp.int32, (TQ, TK), 0)
        c_i = lax.broadcasted_iota(jnp.int32, (TQ, TK), 1)
        visible = (c_i >> CHUNK_SHIFT) <= (r_i >> CHUNK_SHIFT)
        d_o = do_ref[...]
        prod = d_o * o_ref[...]
        dob = d_o.astype(BF16)
        zb = jnp.zeros_like(dob)
        lse_blk = lse_ref[...]
        v_x = (va, vb)
        per_head = []
        for hx in range(2):
            mine = is_a if hx == 0 else jnp.logical_not(is_a)
            delta = jnp.sum(jnp.where(mine, prod, 0.0), axis=1, keepdims=True)
            lse_h = jnp.sum(jnp.where(lane == hx * HEAD_DIM, lse_blk, 0.0), axis=1, keepdims=True)
            per_head.append((delta, lse_h, jnp.where(mine, dob, zb)))
        dqacc[...] = jnp.zeros_like(dqacc)

        def tile(j, diag):
            ks = pl.ds(pl.multiple_of(j * TK, TK), TK)
            for hx in range(2):
                sl = slice(hx * LANES, (hx + 1) * LANES)
                delta, lse_h, do_h = per_head[hx]
                q = q_ref[:, sl]
                kt = k_ref[ks, sl]
                z = _dot_nt(q, kt) * MLA_SCALE
                if diag:
                    z = jnp.where(visible, z, NEG)
                p = jnp.exp(z - lse_h)
                dp = _dot_nt(dob, v_x[hx][ks, :])
                ds = (p * (dp - delta) * MLA_SCALE).astype(BF16)
                dqacc[hx] += _dot(ds, kt)
                dkacc[ks, sl] += _dot_tn(ds, q)
                dvacc[ks, :] += _dot_tn(p.astype(BF16), do_h)

        def loop(j, c):
            tile(j, False)
            return c

        lax.fori_loop(0, qi, loop, 0)
        tile(qi, True)
        dq_ref[:, :LANES] = dqacc[0]
        dq_ref[:, LANES:] = dqacc[1]

        @pl.when(qi == nq - 1)
        def _():
            dk_ref[...] = dkacc[...]
            dv_ref[...] = dvacc[...]

    blk = pl.BlockSpec((TQ, LANES), lambda p, qi: (qi, p))
    return pl.pallas_call(
        body, name="mla_bwd", grid=(4, nq),
        in_specs=[pl.BlockSpec((TQ, 2 * LANES), lambda p, qi: (qi, p)),
                  pl.BlockSpec((s, 2 * LANES), lambda p, qi: (0, p)),
                  pl.BlockSpec((s, LANES), lambda p, qi: (0, p)), blk, blk, blk],
        out_specs=(pl.BlockSpec((TQ, 2 * LANES), lambda p, qi: (qi, p)),
                   pl.BlockSpec((s, 2 * LANES), lambda p, qi: (0, p)),
                   pl.BlockSpec((s, LANES), lambda p, qi: (0, p))),
        out_shape=(jax.ShapeDtypeStruct((s, 1024), F32), jax.ShapeDtypeStruct((s, 1024), F32),
                   jax.ShapeDtypeStruct((s, D_GRP), F32)),
        scratch_shapes=[pltpu.VMEM((s, LANES), BF16)] * 2
        + [pltpu.VMEM((2, TQ, LANES), F32), pltpu.VMEM((s, 2 * LANES), F32), pltpu.VMEM((s, LANES), F32)],
        compiler_params=_params(("arbitrary", "arbitrary"), 48),
    )(qp, kp, vv, d_o, o, lse)


def _mid(x, p, target, sb_o, mla_o, rest, g_sb, g_mla, w_out, g_post, w_ple, g_ple, w_pg, b_pg, bd):
    s = x.shape[0]

    def body(x_ref, p_ref, t_ref, sbo_ref, mlo_ref, sbg_ref, mlg_ref, gsb_ref, gml_ref, wout_ref,
             gpost_ref, wple_ref, gple_ref, wpg_ref, bpg_ref, bd_ref,
             dx1_ref, dsbo_ref, dmlo_ref, dsbg_ref, dmlg_ref, x1b_ref, dglb_ref, ycb_ref, dyb_ref,
             pb_ref, dub_ref, small_ref):
        i = pl.program_id(0)
        bd_m = bd_ref[...]

        def seg_mean(v):
            return _hl_dot(v, bd_m) * (1.0 / HEAD_DIM)

        groups = []
        for o_ref, gate_ref, gain_ref in ((sbo_ref, sbg_ref, gsb_ref), (mlo_ref, mlg_ref, gml_ref)):
            o = o_ref[...]
            r = lax.rsqrt(seg_mean(o * o) + EPS)
            n = o * r
            hn = n * gain_ref[...]
            gate = gate_ref[...]
            sg = _sigmoid(gate)
            si = gate * sg
            groups.append((r, n, hn, gate, sg, si, gain_ref[...]))
        ya = (groups[0][2] * groups[0][5]).astype(BF16)
        yb = (groups[1][2] * groups[1][5]).astype(BF16)
        ycb_ref[:, :D_GRP] = ya
        ycb_ref[:, D_GRP:] = yb
        y = _dot(ya, wout_ref[:D_GRP, :]) + _dot(yb, wout_ref[D_GRP:, :])
        ry = lax.rsqrt(jnp.mean(y * y, axis=-1, keepdims=True) + EPS)
        ny = y * ry
        x1 = x_ref[...] + ny * gpost_ref[...]
        x1b = x1.astype(BF16)
        x1b_ref[...] = x1b
        pb = p_ref[...].astype(BF16)
        pb_ref[...] = pb
        u = _dot(pb, wple_ref[...])
        ru = lax.rsqrt(jnp.mean(u * u, axis=-1, keepdims=True) + EPS)
        nu = u * ru
        ple = nu * gple_ref[...]
        gate = _sigmoid(_dot(x1b, wpg_ref[...]) + bpg_ref[...])
        x2 = x1 + ple * gate
        diff = x2 - t_ref[...]
        dx2 = diff * (1.0 / D_MODEL)

        d_ple = dx2 * gate
        d_glin = (dx2 * ple) * (gate * (1.0 - gate))
        dglb = d_glin.astype(BF16)
        dglb_ref[...] = dglb
        dx1 = dx2 + _dot_nt(dglb, wpg_ref[...])
        dx1_ref[...] = dx1
        d_nu = d_ple * gple_ref[...]
        d_u = ru * (d_nu - nu * jnp.mean(d_nu * nu, axis=-1, keepdims=True))
        dub_ref[...] = d_u.astype(BF16)
        d_ny = dx1 * gpost_ref[...]
        d_y = ry * (d_ny - ny * jnp.mean(d_ny * ny, axis=-1, keepdims=True))
        dyb = d_y.astype(BF16)
        dyb_ref[...] = dyb
        d_yc = (_dot_nt(dyb, wout_ref[:D_GRP, :]), _dot_nt(dyb, wout_ref[D_GRP:, :]))

        d_gain = []
        for gx, (do_ref, dg_ref) in enumerate(((dsbo_ref, dsbg_ref), (dmlo_ref, dmlg_ref))):
            r, n, hn, gate_g, sg, si, gain = groups[gx]
            dyg = d_yc[gx]
            d_hn = dyg * si
            dg_ref[...] = (dyg * hn * (sg * (1.0 + gate_g * (1.0 - sg)))).astype(BF16)
            d_gain.append(jnp.sum(d_hn * n, axis=0, keepdims=True))
            d_n = d_hn * gain
            do_ref[...] = r * (d_n - n * seg_mean(d_n * n))

        @pl.when(i == 0)
        def _():
            small_ref[...] = jnp.zeros_like(small_ref)

        small_ref[0:1, :] += jnp.sum(d_glin, axis=0, keepdims=True)
        small_ref[1:2, :] += jnp.sum(d_ple * nu, axis=0, keepdims=True)
        small_ref[2:3, :] += jnp.sum(dx1 * ny, axis=0, keepdims=True)
        small_ref[3:4, :D_GRP] += d_gain[0]
        small_ref[3:4, D_GRP:] += d_gain[1]
        small_ref[4:5, :] += jnp.sum(diff * diff, axis=0, keepdims=True) * (0.5 / D_MODEL)

    def row(width, idx=0):
        return pl.BlockSpec((TM, width), lambda i: (i, idx))

    def full(a):
        return pl.BlockSpec(a.shape, lambda i: (0, 0))

    f32 = lambda w: jax.ShapeDtypeStruct((s, w), F32)
    b16 = lambda w: jax.ShapeDtypeStruct((s, w), BF16)
    return pl.pallas_call(
        body, name="mid", grid=(s // TM,),
        in_specs=[row(D_MODEL), row(PLE_DIM), row(D_MODEL), row(D_GRP), row(D_GRP),
                  row(D_GRP, 0), row(D_GRP, 1), full(g_sb), full(g_mla), full(w_out), full(g_post),
                  full(w_ple), full(g_ple), full(w_pg), full(b_pg), full(bd)],
        out_specs=(row(D_MODEL), row(D_GRP), row(D_GRP), row(D_GRP), row(D_GRP), row(D_MODEL),
                   row(D_MODEL), row(D_MODEL), row(D_MODEL), row(PLE_DIM), row(D_MODEL),
                   pl.BlockSpec((8, D_MODEL), lambda i: (0, 0))),
        out_shape=(f32(D_MODEL), f32(D_GRP), f32(D_GRP), b16(D_GRP), b16(D_GRP), b16(D_MODEL),
                   b16(D_MODEL), b16(D_MODEL), b16(D_MODEL), b16(PLE_DIM), b16(D_MODEL),
                   jax.ShapeDtypeStruct((8, D_MODEL), F32)),
        compiler_params=_params(("arbitrary",), 56),
    )(x, p, target, sb_o, mla_o, rest, rest, g_sb, g_mla, w_out, g_post, w_ple, g_ple, w_pg, b_pg, bd)


def _mla_prep_bwd(dqp, dkp, dvv, rest, gq, gkv, wuq, wuk, wuv, cos_t, sin_t):
    s = rest.shape[0]

    def body(dqp_ref, dkp_ref, dvv_ref, cq_ref, ckv_ref, gq_ref, gkv_ref, wuq_ref, wuk_ref, wuv_ref,
             c_ref, s_ref, dcq_ref, dckv_ref, dkr_ref, dqb_ref, dkb_ref, dvb_ref, small_ref):
        i = pl.program_id(0)
        lane = lax.broadcasted_iota(jnp.int32, (1, LANES), 1)
        in_rope = (lane >= HEAD_DIM) & (lane < HEAD_DIM + ROPE_DIM)
        cos_v, sin_v = c_ref[...], s_ref[...]
        dkr_roped = jnp.zeros((TM, LANES), F32)
        for h in range(N_HEADS):
            sl = slice(h * LANES, (h + 1) * LANES)
            dy = dqp_ref[:, sl]
            dqb_ref[:, sl] = (dy * cos_v + _rope_swap(dy * sin_v, lane)).astype(BF16)
            dkh = dkp_ref[:, sl]
            dkb_ref[:, sl] = dkh.astype(BF16)
            dkr_roped = dkr_roped + jnp.where(in_rope, dkh, 0.0)
        dkr_ref[...] = (dkr_roped * cos_v + _rope_swap(dkr_roped * sin_v, lane)).astype(BF16)
        dvb = dvv_ref[...].astype(BF16)
        dvb_ref[...] = dvb

        cq = cq_ref[...]
        rq = lax.rsqrt(jnp.mean(cq * cq, axis=-1, keepdims=True) + EPS)
        nq_ = cq * rq
        d_cqn = _dot_nt(dqb_ref[...], wuq_ref[...])
        d_n = d_cqn * gq_ref[...]
        dcq_ref[...] = (rq * (d_n - nq_ * jnp.mean(d_n * nq_, axis=-1, keepdims=True))).astype(BF16)

        ckv = ckv_ref[...]
        rkv = lax.rsqrt(jnp.mean(ckv * ckv, axis=-1, keepdims=True) + EPS)
        nkv = ckv * rkv
        d_ckvn = _dot_nt(dkb_ref[...], wuk_ref[...]) + _dot_nt(dvb, wuv_ref[...])
        d_n2 = d_ckvn * gkv_ref[...]
        dckv_ref[...] = (rkv * (d_n2 - nkv * jnp.mean(d_n2 * nkv, axis=-1, keepdims=True))).astype(BF16)

        @pl.when(i == 0)
        def _():
            small_ref[...] = jnp.zeros_like(small_ref)

        small_ref[0:1, :] += jnp.sum(d_cqn * nq_, axis=0, keepdims=True)
        small_ref[1:2, :KV_LORA] += jnp.sum(d_ckvn * nkv, axis=0, keepdims=True)

    def row(width, idx=0):
        return pl.BlockSpec((TM, width), lambda i: (i, idx))

    def full(a):
        return pl.BlockSpec(a.shape, lambda i: (0, 0))

    b16 = lambda w: jax.ShapeDtypeStruct((s, w), BF16)
    return pl.pallas_call(
        body, name="mla_prep_bwd", grid=(s // TM,),
        in_specs=[row(1024), row(1024), row(D_GRP), row(Q_LORA, 4), row(KV_LORA, 10), full(gq), full(gkv),
                  full(wuq), full(wuk), full(wuv), row(LANES), row(LANES)],
        out_specs=(row(Q_LORA), row(KV_LORA), row(LANES), row(1024), row(1024), row(D_GRP),
                   pl.BlockSpec((8, Q_LORA), lambda i: (0, 0))),
        out_shape=(b16(Q_LORA), b16(KV_LORA), b16(LANES), b16(1024), b16(1024), b16(D_GRP),
                   jax.ShapeDtypeStruct((8, Q_LORA), F32)),
        compiler_params=_params(("arbitrary",), 40),
    )(dqp, dkp, dvv, rest, rest, gq, gkv, wuq, wuk, wuv, cos_t, sin_t)


def _in_bwd(x, g, dx1, pieces, w):
    s = x.shape[0]
    widths = [a.shape[1] for a in pieces]
    offs = [sum(widths[:k]) for k in range(len(widths))]

    def body(x_ref, g_ref, dx1_ref, *refs):
        piece_refs = refs[:len(pieces)]
        w_ref, dx_ref, small_ref = refs[len(pieces):]
        i = pl.program_id(0)
        dh = jnp.zeros((TM, D_MODEL), F32)
        for pr, off, wd in zip(piece_refs, offs, widths):
            dh = dh + _dot_nt(pr[...], w_ref[:, off:off + wd])
        xv = x_ref[...]
        r = lax.rsqrt(jnp.mean(xv * xv, axis=-1, keepdims=True) + EPS)
        n = xv * r
        d_n = dh * g_ref[...]
        dx_ref[...] = dx1_ref[...] + r * (d_n - n * jnp.mean(d_n * n, axis=-1, keepdims=True))

        @pl.when(i == 0)
        def _():
            small_ref[...] = jnp.zeros_like(small_ref)

        small_ref[0:1, :] += jnp.sum(dh * n, axis=0, keepdims=True)

    def row(width):
        return pl.BlockSpec((TM, width), lambda i: (i, 0))

    return pl.pallas_call(
        body, name="in_bwd", grid=(s // TM,),
        in_specs=[row(D_MODEL), pl.BlockSpec((1, D_MODEL), lambda i: (0, 0)), row(D_MODEL)]
        + [row(wd) for wd in widths] + [pl.BlockSpec(w.shape, lambda i: (0, 0))],
        out_specs=(row(D_MODEL), pl.BlockSpec((8, D_MODEL), lambda i: (0, 0))),
        out_shape=(jax.ShapeDtypeStruct((s, D_MODEL), F32), jax.ShapeDtypeStruct((8, D_MODEL), F32)),
        compiler_params=_params(("arbitrary",), 48),
    )(x, g, dx1, *pieces, w)


def _tn_matmul(a, b, name):
    s, k = a.shape
    n = b.shape[1]
    ts = 512
    tn = min(n, 512)
    steps = s // ts

    def body(a_ref, b_ref, o_ref):
        t = pl.program_id(1)

        @pl.when(t == 0)
        def _():
            o_ref[...] = jnp.zeros_like(o_ref)

        o_ref[...] += _dot_tn(a_ref[...], b_ref[...])

    return pl.pallas_call(
        body, name=name, grid=(n // tn, steps),
        in_specs=[pl.BlockSpec((ts, k), lambda j, t: (t, 0)), pl.BlockSpec((ts, tn), lambda j, t: (t, j))],
        out_specs=pl.BlockSpec((k, tn), lambda j, t: (0, j)),
        out_shape=jax.ShapeDtypeStruct((k, n), F32),
        compiler_params=_params(("parallel", "arbitrary"), 40),
    )(a, b)


def _pack_shards(w_in, w_uq, w_ukv, w_out, w_ple, w_pg):
    return jnp.concatenate([a.reshape(-1, LANES) for a in (w_in, w_uq, w_ukv, w_out, w_ple, w_pg)], axis=0)


def _unpack_shards(packed):
    out, o = [], 0
    for rows, shape in ((R_IN, (D_MODEL, 372)), (R_UQ, (Q_LORA, 96)), (R_UKV, (KV_LORA, 128)),
                        (R_OUT, (128, D_MODEL)), (R_PLE, (PLE_DIM, 128)), (R_PG, (128, D_MODEL))):
        out.append(packed[o:o + rows].reshape((1,) + shape))
        o += rows
    return out


def _unpack_gathered(gath):
    o = 0
    seg = {}
    for name, rows in (("in", R_IN), ("uq", R_UQ), ("ukv", R_UKV), ("out", R_OUT), ("ple", R_PLE), ("pg", R_PG)):
        seg[name] = gath[:, o:o + rows]
        o += rows
    cols = lambda a, k, per: a.reshape(N_DEV, k, per).transpose(1, 0, 2).reshape(k, N_DEV * per)
    return (cols(seg["in"], D_MODEL, 372), cols(seg["uq"], Q_LORA, 96), cols(seg["ukv"], KV_LORA, 128),
            seg["out"].reshape(D_MODEL, D_MODEL), cols(seg["ple"], PLE_DIM, 128),
            seg["pg"].reshape(D_MODEL, D_MODEL))


def _pack_full_grads(d_in, d_uq, d_ukv, d_out, d_ple, d_pg):
    cols = lambda a, per: a.reshape(a.shape[0], N_DEV, per).transpose(1, 0, 2).reshape(N_DEV, -1, LANES)
    return jnp.concatenate([cols(d_in, 372), cols(d_uq, 96), cols(d_ukv, 128),
                            d_out.reshape(N_DEV, -1, LANES), cols(d_ple, 128),
                            d_pg.reshape(N_DEV, -1, LANES)], axis=1)


_VEC_SIZES = (D_MODEL, Q_LORA, KV_LORA, D_GRP, D_GRP, D_MODEL, D_MODEL, D_MODEL)


def _pack_vectors(vs, tail=None):
    flat = jnp.concatenate([v.reshape(-1) for v in vs])
    if tail is None:
        tail = jnp.zeros((D_MODEL,), F32)
    flat = jnp.concatenate([flat, tail.reshape(-1)])
    flat = jnp.pad(flat, (0, R_SMALL * LANES - flat.shape[0]))
    return flat.reshape(R_SMALL, LANES)


def _unpack_vectors(packed):
    flat = packed.reshape(-1)
    out, o = [], 0
    for n in _VEC_SIZES:
        out.append(flat[o:o + n].reshape(1, n))
        o += n
    return out, flat[o:o + D_MODEL]


def kernel(x, p, positions, norm_pre_g, w_in, q_norm_g, w_uq, kv_norm_g, w_ukv, sb_out_norm_g, mla_out_norm_g, w_out, norm_post_g, w_ple, ple_norm_g, w_ple_gate, b_ple_gate, loss_target, m_norm_pre_g, m_w_in, m_q_norm_g, m_w_uq, m_kv_norm_g, m_w_ukv, m_sb_out_norm_g, m_mla_out_norm_g, m_w_out, m_norm_post_g, m_w_ple, m_ple_norm_g, m_w_ple_gate, m_b_ple_gate, v_norm_pre_g, v_w_in, v_q_norm_g, v_w_uq, v_kv_norm_g, v_w_ukv, v_sb_out_norm_g, v_mla_out_norm_g, v_w_out, v_norm_post_g, v_w_ple, v_ple_norm_g, v_w_ple_gate, v_b_ple_gate):
    xs, ps, tgt = x[0], p[0, 0], loss_target[0]
    s = xs.shape[0]

    w_packed = _pack_shards(w_in, w_uq, w_ukv, w_out, w_ple, w_ple_gate)
    gath = _all_gather(w_packed.astype(BF16)).reshape(N_DEV, R_BIG, LANES)
    f_in, f_uq, f_ukv, f_out, f_ple, f_pg = _unpack_gathered(gath)
    grad_x, d_mats, vec_partials, loss_vec = _local_grads(
        xs, ps, positions[0], tgt, norm_pre_g, q_norm_g, kv_norm_g, sb_out_norm_g, mla_out_norm_g,
        norm_post_g, ple_norm_g, b_ple_gate, f_in, f_uq, f_ukv, f_out, f_ple, f_pg)

    g_all = _pack_full_grads(*d_mats)
    small = _pack_vectors(vec_partials, tail=loss_vec)
    land, sland = _grad_exchange(g_all, small)
    return _update(land, sland, grad_x, (w_in, w_uq, w_ukv, w_out, w_ple, w_ple_gate), (m_w_in, m_w_uq, m_w_ukv, m_w_out, m_w_ple, m_w_ple_gate), (v_w_in, v_w_uq, v_w_ukv, v_w_out, v_w_ple, v_w_ple_gate), (norm_pre_g, q_norm_g, kv_norm_g, sb_out_norm_g, mla_out_norm_g, norm_post_g, ple_norm_g, b_ple_gate), (m_norm_pre_g, m_q_norm_g, m_kv_norm_g, m_sb_out_norm_g, m_mla_out_norm_g, m_norm_post_g, m_ple_norm_g, m_b_ple_gate), (v_norm_pre_g, v_q_norm_g, v_kv_norm_g, v_sb_out_norm_g, v_mla_out_norm_g, v_norm_post_g, v_ple_norm_g, v_b_ple_gate))


def _local_grads(xs, ps, pos, tgt, norm_pre_g, q_norm_g, kv_norm_g, sb_out_norm_g, mla_out_norm_g,
                 norm_post_g, ple_norm_g, b_ple_gate, f_in, f_uq, f_ukv, f_out, f_ple, f_pg):
    s = xs.shape[0]
    zc = lambda n: jnp.zeros((D_MODEL, n), BF16)
    w_in_p = jnp.concatenate([f_in[:, :2048], f_in[:, 2464:2976], f_in[:, 2048:2432],
                              zc(64), f_in[:, 2432:2464], zc(32)], axis=1)
    w_uq_p = jnp.pad(f_uq.reshape(Q_LORA, N_HEADS, 96), ((0, 0), (0, 0), (0, 32))).reshape(Q_LORA, 1024)
    ukv4 = f_ukv.reshape(KV_LORA, N_HEADS, 2, HEAD_DIM)
    w_uk_p = jnp.pad(ukv4[:, :, 0], ((0, 0), (0, 0), (0, 64))).reshape(KV_LORA, 1024)
    w_uv = ukv4[:, :, 1].reshape(KV_LORA, D_GRP)

    half = ROPE_DIM // 2
    freq = ROPE_THETA ** (-jnp.arange(half, dtype=F32) / half)
    ang = pos.astype(F32)[:, None] * freq
    cos, sin = jnp.cos(ang), jnp.sin(ang)
    cos_t = jnp.concatenate([jnp.ones((s, 64), F32), cos, cos, jnp.zeros((s, 32), F32)], axis=1)
    sin_t = jnp.concatenate([jnp.zeros((s, 64), F32), -sin, sin, jnp.zeros((s, 32), F32)], axis=1)
    seg = jnp.arange(D_GRP) // HEAD_DIM
    bd = (seg[:, None] == seg[None, :]).astype(BF16)

    qkv, rest, h_b = _in_proj(xs, norm_pre_g, w_in_p)
    sb_o = _sb_fwd(qkv)
    qp, kp, vv, cqn_b, ckvn_b = _mla_prep(rest, q_norm_g, kv_norm_g, w_uq_p, w_uk_p, w_uv, cos_t, sin_t)
    mla_o, lse = _mla_fwd(qp, kp, vv)

    (dx1, d_sbo, d_mlo, d_sbg, d_mlg, x1_b, dgl_b, yc_b, dy_b, p_b, du_b, small_mid) = _mid(
        xs, ps, tgt, sb_o, mla_o, rest, sb_out_norm_g, mla_out_norm_g, f_out, norm_post_g,
        f_ple, ple_norm_g, f_pg, b_ple_gate, bd)
    dqp, dkp, dvv = _mla_bwd(qp, kp, vv, d_mlo, mla_o, lse)
    dq_sb, dk_sb, dv_sb = _sb_bwd(qkv, d_sbo)
    dcq, dckv, dkr, dq_b, dk_b, dv_b, small_prep = _mla_prep_bwd(
        dqp, dkp, dvv, rest, q_norm_g, kv_norm_g, w_uq_p, w_uk_p, w_uv, cos_t, sin_t)
    pieces = [dq_sb, dk_sb, dv_sb, d_sbg, d_mlg, dcq, dckv, dkr]
    grad_x, small_in = _in_bwd(xs, norm_pre_g, dx1, pieces, w_in_p)

    d_cols = [_tn_matmul(h_b, pc, "dw_in_%d" % k) for k, pc in enumerate(pieces)]
    d_in = jnp.concatenate([d_cols[0], d_cols[1], d_cols[2], d_cols[3], d_cols[5], d_cols[6],
                            d_cols[7][:, 64:96], d_cols[4]], axis=1)
    d_uq = _tn_matmul(cqn_b, dq_b, "dw_uq").reshape(Q_LORA, N_HEADS, LANES)[:, :, :96].reshape(Q_LORA, 768)
    d_uk = _tn_matmul(ckvn_b, dk_b, "dw_uk").reshape(KV_LORA, N_HEADS, LANES)[:, :, :HEAD_DIM]
    d_uv = _tn_matmul(ckvn_b, dv_b, "dw_uv").reshape(KV_LORA, N_HEADS, HEAD_DIM)
    d_ukv = jnp.stack([d_uk, d_uv], axis=2).reshape(KV_LORA, 1024)
    d_out = _tn_matmul(yc_b, dy_b, "dw_out")
    d_ple = _tn_matmul(p_b, du_b, "dw_ple")
    d_pg = _tn_matmul(x1_b, dgl_b, "dw_pg")

    vec_partials = [small_in[0], small_prep[0], small_prep[1, :KV_LORA], small_mid[3, :D_GRP],
                    small_mid[3, D_GRP:], small_mid[2], small_mid[1], small_mid[0]]
    return grad_x, (d_in, d_uq, d_ukv, d_out, d_ple, d_pg), vec_partials, small_mid[4]


def _update(land, sland, grad_x, mats, m_mats, v_mats, vecs, m_vecs, v_vecs):
    big = _adamw(land, _pack_shards(*mats), _pack_shards(*m_mats), _pack_shards(*v_mats), 800, "adamw_matrices")
    sm = _adamw(sland, _pack_vectors(vecs), _pack_vectors(m_vecs), _pack_vectors(v_vecs), R_SMALL, "adamw_vectors")

    loss = jnp.sum(_unpack_vectors(sm[0])[1])
    outs = []
    for kind in range(4):
        mat = _unpack_shards(big[kind])
        vec = _unpack_vectors(sm[kind])[0]
        outs += [vec[0], mat[0], vec[1], mat[1], vec[2], mat[2], vec[3], vec[4], mat[3], vec[5],
                 mat[4], vec[6], mat[5], vec[7]]
    return (loss, grad_x[None], *outs)
```

```python
import jax
import jax.numpy as jnp
from jax import lax
from jax.experimental import pallas as pl
from jax.experimental.pallas import tpu as pltpu

F32 = jnp.float32
BF16 = jnp.bfloat16
MESH = pl.DeviceIdType.MESH

N_DEV = 8
D_MODEL = 1024
N_HEADS = 8
HEAD_DIM = 64
D_GRP = N_HEADS * HEAD_DIM
Q_LORA = 256
KV_LORA = 128
ROPE_DIM = 32
PLE_DIM = 256
CHUNK_SHIFT = 6
ROPE_THETA = 10000.0
EPS = 1e-6
SB_SCALE = HEAD_DIM ** -0.5
MLA_SCALE = (HEAD_DIM + ROPE_DIM) ** -0.5
NEG = -1e30
LOG2_E = 1.4426950408889634
LN_2 = 0.6931471805599453
SB_CUTOFF = 110.0

ADAM_LR = 0.001
ADAM_B1 = 0.9
ADAM_B2 = 0.999
ADAM_EPS = 1e-08
ADAM_WD = 0.01
ADAM_STEP = 10

LANES = 128
TQ = 256
TK = 256
TM = 256

R_IN, R_UQ, R_UKV, R_OUT, R_PLE, R_PG = 2976, 192, 128, 1024, 256, 1024
R_BIG = R_IN + R_UQ + R_UKV + R_OUT + R_PLE + R_PG
R_SMALL = 56
D_IN_P = 3072

_NT = (((1,), (1,)), ((), ()))
_TN = (((0,), (0,)), ((), ()))


def _params(sem, vmem_mb):
    return pltpu.CompilerParams(dimension_semantics=sem, vmem_limit_bytes=vmem_mb << 20)


def _dot(a, b):
    return jnp.dot(a, b, preferred_element_type=F32)


def _dot_nt(a, b):
    return lax.dot_general(a, b, _NT, preferred_element_type=F32)


def _dot_tn(a, b):
    return lax.dot_general(a, b, _TN, preferred_element_type=F32)


def _hl_dot(a, b):
    hi = a.astype(BF16)
    lo = (a - hi.astype(F32)).astype(BF16)
    return _dot(hi, b) + _dot(lo, b)


def _sigmoid(x):
    return 1.0 / (1.0 + jnp.exp(-x))


def _rope_swap(x, lane):
    left = pltpu.roll(x, LANES - 16, axis=1)
    right = pltpu.roll(x, 16, axis=1)
    lo = (lane >= 64) & (lane < 80)
    hi = (lane >= 80) & (lane < 96)
    return jnp.where(lo, left, jnp.where(hi, right, 0.0))


def _all_gather(shard):
    m_per, n = shard.shape

    def body(x_ref, out_ref, send_sems, recv_sems, local_sem):
        x, y, c = lax.axis_index("x"), lax.axis_index("y"), lax.axis_index("c")
        me, sibling = (x, y, c), (x, y, 1 - c)
        chips = [(1 - x, y), (x, 1 - y), (1 - x, 1 - y)]

        def rows(px, py, pc):
            return out_ref.at[pl.ds((4 * px + 2 * py + pc) * m_per, m_per), :]

        def copy(k, block, to, src=None):
            return pltpu.make_async_remote_copy(
                src_ref=rows(*block) if src is None else src, dst_ref=rows(*block),
                send_sem=send_sems.at[k], recv_sem=recv_sems.at[k],
                device_id=to, device_id_type=MESH)

        mine = pltpu.make_async_copy(x_ref, rows(*me), local_sem)
        mine.start()
        first = [copy(0, me, sibling, src=x_ref)]
        first += [copy(1 + j, me, (*chip, c), src=x_ref) for j, chip in enumerate(chips)]
        for cp in first:
            cp.start()
        passed = [copy(4 + j, (*chip, c), sibling) for j, chip in enumerate(chips)]
        for j, chip in enumerate(chips):
            copy(1 + j, (*chip, c), me).wait_recv()
            passed[j].start()
        copy(0, sibling, me).wait_recv()
        for j, chip in enumerate(chips):
            copy(4 + j, (*chip, 1 - c), me).wait_recv()
        for cp in first + passed:
            cp.wait_send()
        mine.wait()

    return pl.pallas_call(
        body, name="weight_all_gather",
        out_shape=jax.ShapeDtypeStruct((N_DEV * m_per, n), shard.dtype),
        in_specs=[pl.BlockSpec(memory_space=pltpu.VMEM)],
        out_specs=pl.BlockSpec(memory_space=pltpu.VMEM),
        scratch_shapes=[pltpu.SemaphoreType.DMA((7,)), pltpu.SemaphoreType.DMA((7,)),
                        pltpu.SemaphoreType.DMA],
        compiler_params=pltpu.CompilerParams(vmem_limit_bytes=40 << 20),
    )(shard)


def _grad_exchange(g_all, small):
    _, r, n = g_all.shape
    sr = small.shape[0]

    def body(g_ref, s_ref, land_ref, sland_ref, ssem, rsem, ssem2, rsem2, lsem):
        x, y, c = lax.axis_index("x"), lax.axis_index("y"), lax.axis_index("c")
        me = 4 * x + 2 * y + c
        own = pltpu.make_async_copy(g_ref.at[me], land_ref.at[me], lsem.at[0])
        own2 = pltpu.make_async_copy(s_ref, sland_ref.at[me], lsem.at[1])
        own.start()
        own2.start()
        copies = []
        for k in range(1, N_DEV):
            px = 1 - x if (k >> 2) & 1 else x
            py = 1 - y if (k >> 1) & 1 else y
            pc = 1 - c if k & 1 else c
            pid = 4 * px + 2 * py + pc
            copies.append(pltpu.make_async_remote_copy(
                src_ref=g_ref.at[pid], dst_ref=land_ref.at[me],
                send_sem=ssem.at[k], recv_sem=rsem.at[k], device_id=(px, py, pc), device_id_type=MESH))
            copies.append(pltpu.make_async_remote_copy(
                src_ref=s_ref, dst_ref=sland_ref.at[me],
                send_sem=ssem2.at[k], recv_sem=rsem2.at[k], device_id=(px, py, pc), device_id_type=MESH))
        for cp in copies:
            cp.start()
        for cp in copies:
            cp.wait()
        own.wait()
        own2.wait()

    any_spec = pl.BlockSpec(memory_space=pl.ANY)
    return pl.pallas_call(
        body, name="grad_exchange",
        out_shape=(jax.ShapeDtypeStruct((N_DEV, r, n), F32), jax.ShapeDtypeStruct((N_DEV, sr, n), F32)),
        in_specs=[any_spec, any_spec], out_specs=(any_spec, any_spec),
        scratch_shapes=[pltpu.SemaphoreType.DMA((N_DEV,)), pltpu.SemaphoreType.DMA((N_DEV,)),
                        pltpu.SemaphoreType.DMA((N_DEV,)), pltpu.SemaphoreType.DMA((N_DEV,)),
                        pltpu.SemaphoreType.DMA((2,))],
    )(g_all, small)


def _adamw(land, w, m, v, block_rows, name):
    _, r, n = land.shape
    c1 = 1.0 - ADAM_B1
    c2 = 1.0 - ADAM_B2
    bc1 = 1.0 - ADAM_B1 ** ADAM_STEP
    bc2 = 1.0 - ADAM_B2 ** ADAM_STEP

    def body(l_ref, w_ref, m_ref, v_ref, g_out, d_out, m_out, v_out):
        g = l_ref[0]
        for j in range(1, N_DEV):
            g = g + l_ref[j]
        mn = ADAM_B1 * m_ref[...] + c1 * g
        vn = ADAM_B2 * v_ref[...] + c2 * (g * g)
        m_hat = mn / bc1
        v_hat = vn / bc2
        g_out[...] = g
        d_out[...] = -ADAM_LR * (m_hat / (jnp.sqrt(v_hat) + ADAM_EPS) + ADAM_WD * w_ref[...])
        m_out[...] = mn
        v_out[...] = vn

    row = pl.BlockSpec((block_rows, n), lambda i: (i, 0))
    shp = jax.ShapeDtypeStruct((r, n), F32)
    return pl.pallas_call(
        body, name=name, grid=(r // block_rows,),
        in_specs=[pl.BlockSpec((N_DEV, block_rows, n), lambda i: (0, i, 0)), row, row, row],
        out_specs=(row, row, row, row), out_shape=(shp, shp, shp, shp),
        compiler_params=_params(("parallel",), 40),
    )(land, w, m, v)


def _in_proj(x, g, w):
    s = x.shape[0]

    def body(x_ref, g_ref, w_ref, qkv_ref, rest_ref, h_ref):
        xv = x_ref[...]
        r = lax.rsqrt(jnp.mean(xv * xv, axis=-1, keepdims=True) + EPS)
        h = ((xv * r) * g_ref[...]).astype(BF16)
        h_ref[...] = h
        qkv_ref[...] = _dot(h, w_ref[:, :1536]).astype(BF16)
        rest_ref[...] = _dot(h, w_ref[:, 1536:])

    return pl.pallas_call(
        body, name="in_proj", grid=(s // TM,),
        in_specs=[pl.BlockSpec((TM, D_MODEL), lambda i: (i, 0)),
                  pl.BlockSpec((1, D_MODEL), lambda i: (0, 0)),
                  pl.BlockSpec((D_MODEL, D_IN_P), lambda i: (0, 0))],
        out_specs=(pl.BlockSpec((TM, 1536), lambda i: (i, 0)),
                   pl.BlockSpec((TM, 1536), lambda i: (i, 0)),
                   pl.BlockSpec((TM, D_MODEL), lambda i: (i, 0))),
        out_shape=(jax.ShapeDtypeStruct((s, 1536), BF16), jax.ShapeDtypeStruct((s, 1536), F32),
                   jax.ShapeDtypeStruct((s, D_MODEL), BF16)),
        compiler_params=_params(("parallel",), 48),
    )(x, g, w)


def _mla_prep(rest, gq, gkv, wuq, wuk, wuv, cos_t, sin_t):
    s = rest.shape[0]

    def body(cq_ref, ckv_ref, kr_ref, gq_ref, gkv_ref, wuq_ref, wuk_ref, wuv_ref, c_ref, s_ref,
             qp_ref, kp_ref, vv_ref, cqn_ref, ckvn_ref):
        lane = lax.broadcasted_iota(jnp.int32, (1, LANES), 1)
        cos_v, sin_v = c_ref[...], s_ref[...]
        cq = cq_ref[...]
        rq = lax.rsqrt(jnp.mean(cq * cq, axis=-1, keepdims=True) + EPS)
        cqn = ((cq * rq) * gq_ref[...]).astype(BF16)
        cqn_ref[...] = cqn
        q = _dot(cqn, wuq_ref[...])
        ckv = ckv_ref[...]
        rkv = lax.rsqrt(jnp.mean(ckv * ckv, axis=-1, keepdims=True) + EPS)
        ckvn = ((ckv * rkv) * gkv_ref[...]).astype(BF16)
        ckvn_ref[...] = ckvn
        kn = _dot(ckvn, wuk_ref[...])
        vv_ref[...] = _dot(ckvn, wuv_ref[...]).astype(BF16)
        kr = kr_ref[...]
        kr_roped = kr * cos_v + _rope_swap(kr, lane) * sin_v
        for h in range(N_HEADS):
            sl = slice(h * LANES, (h + 1) * LANES)
            qh = q[:, sl]
            qp_ref[:, sl] = (qh * cos_v + _rope_swap(qh, lane) * sin_v).astype(BF16)
            kp_ref[:, sl] = (kn[:, sl] + kr_roped).astype(BF16)

    def row(width, idx):
        return pl.BlockSpec((TM, width), lambda i: (i, idx))

    def full(a):
        return pl.BlockSpec(a.shape, lambda i: (0, 0))

    return pl.pallas_call(
        body, name="mla_prep", grid=(s // TM,),
        in_specs=[row(Q_LORA, 4), row(KV_LORA, 10), row(LANES, 11), full(gq), full(gkv),
                  full(wuq), full(wuk), full(wuv), row(LANES, 0), row(LANES, 0)],
        out_specs=(row(1024, 0), row(1024, 0), row(D_GRP, 0), row(Q_LORA, 0), row(KV_LORA, 0)),
        out_shape=(jax.ShapeDtypeStruct((s, 1024), BF16), jax.ShapeDtypeStruct((s, 1024), BF16),
                   jax.ShapeDtypeStruct((s, D_GRP), BF16), jax.ShapeDtypeStruct((s, Q_LORA), BF16),
                   jax.ShapeDtypeStruct((s, KV_LORA), BF16)),
        compiler_params=_params(("parallel",), 32),
    )(rest, rest, rest, gq, gkv, wuq, wuk, wuv, cos_t, sin_t)


def _pair_masks(k_ref, v_ref, ka, kb, va, vb, is_a):
    kv = k_ref[...]
    zk = jnp.zeros_like(kv)
    ka[...] = jnp.where(is_a, kv, zk)
    kb[...] = jnp.where(is_a, zk, kv)
    vv = v_ref[...]
    va[...] = jnp.where(is_a, vv, zk)
    vb[...] = jnp.where(is_a, zk, vv)


def _sb_live(n, qi, c_a, c_b):
    live = jnp.maximum(jnp.max(c_a), jnp.max(c_b)) > -SB_CUTOFF
    return jnp.logical_and(n < qi, live)


def _sb_fwd(qkv):
    s = qkv.shape[0]

    def body(q_ref, k_ref, v_ref, o_ref, ka, kb, va, vb, acc):
        qi = pl.program_id(1)
        lane = lax.broadcasted_iota(jnp.int32, (1, LANES), 1)
        is_a = lane < HEAD_DIM

        @pl.when(qi == 0)
        def _():
            _pair_masks(k_ref, v_ref, ka, kb, va, vb, is_a)

        qs = q_ref[...] * SB_SCALE
        r_i = lax.broadcasted_iota(jnp.int32, (TQ, TK), 0)
        c_i = lax.broadcasted_iota(jnp.int32, (TQ, TK), 1)
        past = c_i < r_i
        upper = (r_i > c_i).astype(BF16)
        acc[...] = jnp.zeros_like(acc)

        def tile(j, carries, diag):
            ks = pl.ds(pl.multiple_of(j * TK, TK), TK)
            out = []
            for kx, vx, c in ((ka, va, carries[0]), (kb, vb, carries[1])):
                z = _dot_nt(qs, kx[ks, :])
                e = jnp.exp(-jnp.abs(z))
                lf = -(jnp.maximum(z, 0.0) + jnp.log(1.0 + e))
                if diag:
                    lf = jnp.where(past, lf, 0.0)
                suf = _hl_dot(lf, upper) + c
                w = jnp.exp(z + lf + suf)
                if diag:
                    w = jnp.where(past, w, 0.0)
                acc[...] += _dot(w.astype(BF16), vx[ks, :])
                out.append(c + jnp.sum(lf, axis=1, keepdims=True))
            return tuple(out)

        zero = jnp.zeros((TQ, 1), F32)
        carries = tile(qi, (zero, zero), True)

        def step(st):
            return (st[0] + 1,) + tile(qi - 1 - st[0], st[1:], False)

        lax.while_loop(lambda st: _sb_live(st[0], qi, st[1], st[2]), step, (0,) + carries)
        o_ref[...] = acc[...]

    slab = lambda off: pl.BlockSpec((s, LANES), lambda p, qi: (0, off + p))
    return pl.pallas_call(
        body, name="sb_fwd", grid=(4, s // TQ),
        in_specs=[pl.BlockSpec((TQ, LANES), lambda p, qi: (qi, p)), slab(4), slab(8)],
        out_specs=pl.BlockSpec((TQ, LANES), lambda p, qi: (qi, p)),
        out_shape=jax.ShapeDtypeStruct((s, D_GRP), F32),
        scratch_shapes=[pltpu.VMEM((s, LANES), BF16)] * 4 + [pltpu.VMEM((TQ, LANES), F32)],
        compiler_params=_params(("arbitrary", "arbitrary"), 40),
    )(qkv, qkv, qkv)


def _sb_bwd(qkv, d_o):
    s = qkv.shape[0]
    nq = s // TQ
    nk = s // TK

    def body(q_ref, k_ref, v_ref, do_ref, dq_ref, dk_ref, dv_ref,
             ka, kb, va, vb, x1s, bts, dqacc, dkacc, dvacc):
        qi = pl.program_id(1)
        lane = lax.broadcasted_iota(jnp.int32, (1, LANES), 1)
        is_a = lane < HEAD_DIM

        @pl.when(qi == 0)
        def _():
            _pair_masks(k_ref, v_ref, ka, kb, va, vb, is_a)
            dkacc[...] = jnp.zeros_like(dkacc)
            dvacc[...] = jnp.zeros_like(dvacc)

        qs = q_ref[...] * SB_SCALE
        zq = jnp.zeros_like(qs)
        qs_x = (jnp.where(is_a, qs, zq), jnp.where(is_a, zq, qs))
        dob = do_ref[...].astype(BF16)
        do_x = (jnp.where(is_a, dob, zq), jnp.where(is_a, zq, dob))
        r_i = lax.broadcasted_iota(jnp.int32, (TQ, TK), 0)
        c_i = lax.broadcasted_iota(jnp.int32, (TQ, TK), 1)
        past = c_i < r_i
        upper = (r_i > c_i).astype(BF16)
        upper_incl = (r_i >= c_i).astype(BF16)
        dqacc[...] = jnp.zeros_like(dqacc)
        k_x = (ka, kb)
        v_x = (va, vb)

        def tile1(j, carries, diag):
            ks = pl.ds(pl.multiple_of(j * TK, TK), TK)
            out = []
            for hx in range(2):
                c, gsum = carries[2 * hx], carries[2 * hx + 1]
                z = _dot_nt(qs, k_x[hx][ks, :])
                e = jnp.exp(-jnp.abs(z))
                den = 1.0 + e
                lf = -(jnp.maximum(z, 0.0) + jnp.log(den))
                rden = 1.0 / den
                pos = z >= 0.0
                beta = jnp.where(pos, rden, e * rden)
                omb = jnp.where(pos, e * rden, rden)
                if diag:
                    lf = jnp.where(past, lf, 0.0)
                    beta = jnp.where(past, beta, 0.0)
                suf = _hl_dot(lf, upper) + c
                a = jnp.exp(z + lf + suf)
                if diag:
                    a = jnp.where(past, a, 0.0)
                d_a = _dot_nt(dob, v_x[hx][ks, :])
                g = a * d_a
                sg = _hl_dot(g, upper_incl) + gsum
                x1s[j, hx] = g * omb + beta * sg
                bts[j, hx] = beta
                dvacc[ks, :] += _dot_tn(a.astype(BF16), do_x[hx])
                out.append(c + jnp.sum(lf, axis=1, keepdims=True))
                out.append(gsum + jnp.sum(g, axis=1, keepdims=True))
            return tuple(out)

        zero = jnp.zeros((TQ, 1), F32)
        carries = tile1(qi, (zero, zero, zero, zero), True)

        def step(st):
            return (st[0] + 1,) + tile1(qi - 1 - st[0], st[1:], False)

        swept = lax.while_loop(lambda st: _sb_live(st[0], qi, st[1], st[3]), step, (0,) + carries)
        g_tot = (swept[2], swept[4])

        def tile2(j, _):
            ks = pl.ds(pl.multiple_of(j * TK, TK), TK)
            for hx in range(2):
                dz = (x1s[j, hx] - bts[j, hx] * g_tot[hx]).astype(BF16)
                dqacc[...] += _dot(dz, k_x[hx][ks, :])
                dkacc[ks, :] += _dot_tn(dz, qs_x[hx])
            return 0

        lax.fori_loop(qi - swept[0], qi + 1, tile2, 0)
        dq_ref[...] = (dqacc[...] * SB_SCALE).astype(BF16)

        @pl.when(qi == nq - 1)
        def _():
            dk_ref[...] = dkacc[...].astype(BF16)
            dv_ref[...] = dvacc[...].astype(BF16)

    slab = lambda off: pl.BlockSpec((s, LANES), lambda p, qi: (0, off + p))
    blk = pl.BlockSpec((TQ, LANES), lambda p, qi: (qi, p))
    out_slab = pl.BlockSpec((s, LANES), lambda p, qi: (0, p))
    shp = jax.ShapeDtypeStruct((s, D_GRP), BF16)
    return pl.pallas_call(
        body, name="sb_bwd", grid=(4, nq),
        in_specs=[blk, slab(4), slab(8), blk],
        out_specs=(blk, out_slab, out_slab), out_shape=(shp, shp, shp),
        scratch_shapes=[pltpu.VMEM((s, LANES), BF16)] * 4
        + [pltpu.VMEM((nk, 2, TQ, TK), F32)] * 2
        + [pltpu.VMEM((TQ, LANES), F32), pltpu.VMEM((s, LANES), F32), pltpu.VMEM((s, LANES), F32)],
        compiler_params=_params(("arbitrary", "arbitrary"), 56),
    )(qkv, qkv, qkv, d_o)


def _mla_fwd(qp, kp, vv, hb):
    s = qp.shape[0]
    c2 = MLA_SCALE * LOG2_E

    def body(q_ref, k_ref, v_ref, o_ref, lse_ref, vaug, mrun, mb, acc):
        qi = pl.program_id(1)
        lane = lax.broadcasted_iota(jnp.int32, (1, LANES), 1)
        is_a = lane < HEAD_DIM

        @pl.when(qi == 0)
        def _():
            for h in range(hb):
                vp = v_ref[:, (h // 2) * LANES:(h // 2 + 1) * LANES]
                mine = is_a if h % 2 == 0 else jnp.logical_not(is_a)
                vaug[h] = jnp.where(mine, vp, jnp.ones_like(vp))

        r_i = lax.broadcasted_iota(jnp.int32, (TQ, TK), 0)
        c_i = lax.broadcasted_iota(jnp.int32, (TQ, TK), 1)
        visible = (c_i >> CHUNK_SHIFT) <= (r_i >> CHUNK_SHIFT)

        def scores(j):
            ks = pl.ds(pl.multiple_of(j * TK, TK), TK)
            return ks, [_dot_nt(q_ref[:, h * LANES:(h + 1) * LANES], k_ref[ks, h * LANES:(h + 1) * LANES])
                        for h in range(hb)]

        def sweep(tile):
            def loop(j, carry):
                tile(j, False)
                return carry

            lax.fori_loop(0, qi, loop, 0)
            tile(qi, True)

        mrun[...] = jnp.full_like(mrun, NEG)

        def tile_max(j, diag):
            _, zs = scores(j)
            for h in range(hb):
                z = jnp.where(visible, zs[h], NEG) if diag else zs[h]
                mrun[h] = jnp.maximum(mrun[h], z)

        sweep(tile_max)
        for h in range(hb):
            m = jnp.max(mrun[h], axis=1, keepdims=True) * c2
            mb[h] = jnp.broadcast_to(m, (TQ, TK))
        acc[...] = jnp.zeros_like(acc)

        def tile_pv(j, diag):
            ks, zs = scores(j)
            for h in range(hb):
                e = zs[h] * c2 - mb[h]
                if diag:
                    e = jnp.where(visible, e, NEG)
                acc[h] += _dot(jnp.exp2(e).astype(BF16), vaug[h, ks, :])

        sweep(tile_pv)
        for pr in range(hb // 2):
            a, b = 2 * pr, 2 * pr + 1
            psl = slice(pr * LANES, (pr + 1) * LANES)
            acc_a, acc_b = acc[a], acc[b]
            l_a = pltpu.roll(acc_a, HEAD_DIM, axis=1)
            l_b = pltpu.roll(acc_b, HEAD_DIM, axis=1)
            o_ref[:, psl] = jnp.where(is_a, acc_a * (1.0 / l_a), acc_b * (1.0 / l_b))
            lse_ref[:, psl] = jnp.where(is_a, mb[a, :, :LANES] * LN_2 + jnp.log(l_a),
                                        mb[b, :, :LANES] * LN_2 + jnp.log(l_b))

    blk = pl.BlockSpec((TQ, hb * HEAD_DIM), lambda g, qi: (qi, g))
    shp = jax.ShapeDtypeStruct((s, D_GRP), F32)
    return pl.pallas_call(
        body, name="mla_fwd", grid=(N_HEADS // hb, s // TQ),
        in_specs=[pl.BlockSpec((TQ, hb * LANES), lambda g, qi: (qi, g)),
                  pl.BlockSpec((s, hb * LANES), lambda g, qi: (0, g)),
                  pl.BlockSpec((s, hb * HEAD_DIM), lambda g, qi: (0, g))],
        out_specs=(blk, blk), out_shape=(shp, shp),
        scratch_shapes=[pltpu.VMEM((hb, s, LANES), BF16), pltpu.VMEM((hb, TQ, TK), F32),
                        pltpu.VMEM((hb, TQ, TK), F32), pltpu.VMEM((hb, TQ, LANES), F32)],
        compiler_params=_params(("arbitrary", "arbitrary"), 56),
    )(qp, kp, vv)


def _mla_bwd(qp, kp, vv, d_o, o, lse, hb):
    s = qp.shape[0]
    c2 = MLA_SCALE * LOG2_E

    def body(q_ref, k_ref, v_ref, do_ref, o_ref, lse_ref, dq_ref, dk_ref, dv_ref, dqacc, lse_b, delta_b):
        qi = pl.program_id(1)
        lane = lax.broadcasted_iota(jnp.int32, (1, LANES), 1)
        is_a = lane < HEAD_DIM

        @pl.when(qi == 0)
        def _():
            dk_ref[...] = jnp.zeros_like(dk_ref)
            dv_ref[...] = jnp.zeros_like(dv_ref)

        r_i = lax.broadcasted_iota(jnp.int32, (TQ, TK), 0)
        c_i = lax.broadcasted_iota(jnp.int32, (TQ, TK), 1)
        visible = (c_i >> CHUNK_SHIFT) <= (r_i >> CHUNK_SHIFT)
        do_x = []
        for h in range(hb):
            psl = slice((h // 2) * LANES, (h // 2 + 1) * LANES)
            mine = is_a if h % 2 == 0 else jnp.logical_not(is_a)
            d_o = do_ref[:, psl]
            delta = jnp.sum(jnp.where(mine, d_o * o_ref[:, psl], 0.0), axis=1, keepdims=True)
            lse_h = jnp.sum(jnp.where(lane == (h % 2) * HEAD_DIM, lse_ref[:, psl], 0.0), axis=1, keepdims=True)
            lse_b[h] = jnp.broadcast_to(lse_h * LOG2_E, (TQ, TK))
            delta_b[h] = jnp.broadcast_to(delta, (TQ, TK))
            do_x.append(jnp.where(mine, d_o, 0.0).astype(BF16))
        dqacc[...] = jnp.zeros_like(dqacc)

        def tile(j, diag):
            ks = pl.ds(pl.multiple_of(j * TK, TK), TK)
            head = lambda h: slice(h * LANES, (h + 1) * LANES)
            pair = lambda h: slice((h // 2) * LANES, (h // 2 + 1) * LANES)
            zs = [_dot_nt(q_ref[:, head(h)], k_ref[ks, head(h)]) for h in range(hb)]
            dps = [_dot_nt(do_x[h], v_ref[ks, pair(h)]) for h in range(hb)]
            for h in range(hb):
                e = zs[h] * c2 - lse_b[h]
                if diag:
                    e = jnp.where(visible, e, NEG)
                p = jnp.exp2(e)
                ds = (p * (dps[h] - delta_b[h]) * MLA_SCALE).astype(BF16)
                dqacc[h] += _dot(ds, k_ref[ks, head(h)])
                dk_ref[ks, head(h)] += _dot_tn(ds, q_ref[:, head(h)])
                dv_ref[ks, pair(h)] += _dot_tn(p.astype(BF16), do_x[h])

        def loop(j, c):
            tile(j, False)
            return c

        lax.fori_loop(0, qi, loop, 0)
        tile(qi, True)
        for h in range(hb):
            dq_ref[:, h * LANES:(h + 1) * LANES] = dqacc[h]

    blk = pl.BlockSpec((TQ, hb * HEAD_DIM), lambda g, qi: (qi, g))
    return pl.pallas_call(
        body, name="mla_bwd", grid=(N_HEADS // hb, s // TQ),
        in_specs=[pl.BlockSpec((TQ, hb * LANES), lambda g, qi: (qi, g)),
                  pl.BlockSpec((s, hb * LANES), lambda g, qi: (0, g)),
                  pl.BlockSpec((s, hb * HEAD_DIM), lambda g, qi: (0, g)), blk, blk, blk],
        out_specs=(pl.BlockSpec((TQ, hb * LANES), lambda g, qi: (qi, g)),
                   pl.BlockSpec((s, hb * LANES), lambda g, qi: (0, g)),
                   pl.BlockSpec((s, hb * HEAD_DIM), lambda g, qi: (0, g))),
        out_shape=(jax.ShapeDtypeStruct((s, 1024), F32), jax.ShapeDtypeStruct((s, 1024), F32),
                   jax.ShapeDtypeStruct((s, D_GRP), F32)),
        scratch_shapes=[pltpu.VMEM((hb, TQ, LANES), F32), pltpu.VMEM((hb, TQ, TK), F32),
                        pltpu.VMEM((hb, TQ, TK), F32)],
        compiler_params=_params(("arbitrary", "arbitrary"), 56),
    )(qp, kp, vv, d_o, o, lse)


def _mid(x, p, target, sb_o, mla_o, rest, g_sb, g_mla, w_out, g_post, w_ple, g_ple, w_pg, b_pg, bd):
    s = x.shape[0]

    def body(x_ref, p_ref, t_ref, sbo_ref, mlo_ref, sbg_ref, mlg_ref, gsb_ref, gml_ref, wout_ref,
             gpost_ref, wple_ref, gple_ref, wpg_ref, bpg_ref, bd_ref,
             dx1_ref, dsbo_ref, dmlo_ref, dsbg_ref, dmlg_ref, x1b_ref, dglb_ref, ycb_ref, dyb_ref,
             pb_ref, dub_ref, small_ref):
        i = pl.program_id(0)
        bd_m = bd_ref[...]

        def seg_mean(v):
            return _hl_dot(v, bd_m) * (1.0 / HEAD_DIM)

        groups = []
        for o_ref, gate_ref, gain_ref in ((sbo_ref, sbg_ref, gsb_ref), (mlo_ref, mlg_ref, gml_ref)):
            o = o_ref[...]
            r = lax.rsqrt(seg_mean(o * o) + EPS)
            n = o * r
            hn = n * gain_ref[...]
            gate = gate_ref[...]
            sg = _sigmoid(gate)
            si = gate * sg
            groups.append((r, n, hn, gate, sg, si, gain_ref[...]))
        ya = (groups[0][2] * groups[0][5]).astype(BF16)
        yb = (groups[1][2] * groups[1][5]).astype(BF16)
        ycb_ref[:, :D_GRP] = ya
        ycb_ref[:, D_GRP:] = yb
        y = _dot(ya, wout_ref[:D_GRP, :]) + _dot(yb, wout_ref[D_GRP:, :])
        ry = lax.rsqrt(jnp.mean(y * y, axis=-1, keepdims=True) + EPS)
        ny = y * ry
        x1 = x_ref[...] + ny * gpost_ref[...]
        x1b = x1.astype(BF16)
        x1b_ref[...] = x1b
        pb = p_ref[...].astype(BF16)
        pb_ref[...] = pb
        u = _dot(pb, wple_ref[...])
        ru = lax.rsqrt(jnp.mean(u * u, axis=-1, keepdims=True) + EPS)
        nu = u * ru
        ple = nu * gple_ref[...]
        gate = _sigmoid(_dot(x1b, wpg_ref[...]) + bpg_ref[...])
        x2 = x1 + ple * gate
        diff = x2 - t_ref[...]
        dx2 = diff * (1.0 / D_MODEL)

        d_ple = dx2 * gate
        d_glin = (dx2 * ple) * (gate * (1.0 - gate))
        dglb = d_glin.astype(BF16)
        dglb_ref[...] = dglb
        dx1 = dx2 + _dot_nt(dglb, wpg_ref[...])
        dx1_ref[...] = dx1
        d_nu = d_ple * gple_ref[...]
        d_u = ru * (d_nu - nu * jnp.mean(d_nu * nu, axis=-1, keepdims=True))
        dub_ref[...] = d_u.astype(BF16)
        d_ny = dx1 * gpost_ref[...]
        d_y = ry * (d_ny - ny * jnp.mean(d_ny * ny, axis=-1, keepdims=True))
        dyb = d_y.astype(BF16)
        dyb_ref[...] = dyb
        d_yc = (_dot_nt(dyb, wout_ref[:D_GRP, :]), _dot_nt(dyb, wout_ref[D_GRP:, :]))

        d_gain = []
        for gx, (do_ref, dg_ref) in enumerate(((dsbo_ref, dsbg_ref), (dmlo_ref, dmlg_ref))):
            r, n, hn, gate_g, sg, si, gain = groups[gx]
            dyg = d_yc[gx]
            d_hn = dyg * si
            dg_ref[...] = (dyg * hn * (sg * (1.0 + gate_g * (1.0 - sg)))).astype(BF16)
            d_gain.append(jnp.sum(d_hn * n, axis=0, keepdims=True))
            d_n = d_hn * gain
            do_ref[...] = r * (d_n - n * seg_mean(d_n * n))

        @pl.when(i == 0)
        def _():
            small_ref[...] = jnp.zeros_like(small_ref)

        small_ref[0:1, :] += jnp.sum(d_glin, axis=0, keepdims=True)
        small_ref[1:2, :] += jnp.sum(d_ple * nu, axis=0, keepdims=True)
        small_ref[2:3, :] += jnp.sum(dx1 * ny, axis=0, keepdims=True)
        small_ref[3:4, :D_GRP] += d_gain[0]
        small_ref[3:4, D_GRP:] += d_gain[1]
        small_ref[4:5, :] += jnp.sum(diff * diff, axis=0, keepdims=True) * (0.5 / D_MODEL)

    def row(width, idx=0):
        return pl.BlockSpec((TM, width), lambda i: (i, idx))

    def full(a):
        return pl.BlockSpec(a.shape, lambda i: (0, 0))

    f32 = lambda w: jax.ShapeDtypeStruct((s, w), F32)
    b16 = lambda w: jax.ShapeDtypeStruct((s, w), BF16)
    return pl.pallas_call(
        body, name="mid", grid=(s // TM,),
        in_specs=[row(D_MODEL), row(PLE_DIM), row(D_MODEL), row(D_GRP), row(D_GRP),
                  row(D_GRP, 0), row(D_GRP, 1), full(g_sb), full(g_mla), full(w_out), full(g_post),
                  full(w_ple), full(g_ple), full(w_pg), full(b_pg), full(bd)],
        out_specs=(row(D_MODEL), row(D_GRP), row(D_GRP), row(D_GRP), row(D_GRP), row(D_MODEL),
                   row(D_MODEL), row(D_MODEL), row(D_MODEL), row(PLE_DIM), row(D_MODEL),
                   pl.BlockSpec((8, D_MODEL), lambda i: (0, 0))),
        out_shape=(f32(D_MODEL), f32(D_GRP), f32(D_GRP), b16(D_GRP), b16(D_GRP), b16(D_MODEL),
                   b16(D_MODEL), b16(D_MODEL), b16(D_MODEL), b16(PLE_DIM), b16(D_MODEL),
                   jax.ShapeDtypeStruct((8, D_MODEL), F32)),
        compiler_params=_params(("arbitrary",), 56),
    )(x, p, target, sb_o, mla_o, rest, rest, g_sb, g_mla, w_out, g_post, w_ple, g_ple, w_pg, b_pg, bd)


def _mla_prep_bwd(dqp, dkp, dvv, rest, gq, gkv, wuq, wuk, wuv, cos_t, sin_t):
    s = rest.shape[0]

    def body(dqp_ref, dkp_ref, dvv_ref, cq_ref, ckv_ref, gq_ref, gkv_ref, wuq_ref, wuk_ref, wuv_ref,
             c_ref, s_ref, dcq_ref, dckv_ref, dkr_ref, dqb_ref, dkb_ref, dvb_ref, small_ref):
        i = pl.program_id(0)
        lane = lax.broadcasted_iota(jnp.int32, (1, LANES), 1)
        in_rope = (lane >= HEAD_DIM) & (lane < HEAD_DIM + ROPE_DIM)
        cos_v, sin_v = c_ref[...], s_ref[...]
        dkr_roped = jnp.zeros((TM, LANES), F32)
        for h in range(N_HEADS):
            sl = slice(h * LANES, (h + 1) * LANES)
            dy = dqp_ref[:, sl]
            dqb_ref[:, sl] = (dy * cos_v + _rope_swap(dy * sin_v, lane)).astype(BF16)
            dkh = dkp_ref[:, sl]
            dkb_ref[:, sl] = dkh.astype(BF16)
            dkr_roped = dkr_roped + jnp.where(in_rope, dkh, 0.0)
        dkr_ref[...] = (dkr_roped * cos_v + _rope_swap(dkr_roped * sin_v, lane)).astype(BF16)
        dvb = dvv_ref[...].astype(BF16)
        dvb_ref[...] = dvb

        cq = cq_ref[...]
        rq = lax.rsqrt(jnp.mean(cq * cq, axis=-1, keepdims=True) + EPS)
        nq_ = cq * rq
        d_cqn = _dot_nt(dqb_ref[...], wuq_ref[...])
        d_n = d_cqn * gq_ref[...]
        dcq_ref[...] = (rq * (d_n - nq_ * jnp.mean(d_n * nq_, axis=-1, keepdims=True))).astype(BF16)

        ckv = ckv_ref[...]
        rkv = lax.rsqrt(jnp.mean(ckv * ckv, axis=-1, keepdims=True) + EPS)
        nkv = ckv * rkv
        d_ckvn = _dot_nt(dkb_ref[...], wuk_ref[...]) + _dot_nt(dvb, wuv_ref[...])
        d_n2 = d_ckvn * gkv_ref[...]
        dckv_ref[...] = (rkv * (d_n2 - nkv * jnp.mean(d_n2 * nkv, axis=-1, keepdims=True))).astype(BF16)

        @pl.when(i == 0)
        def _():
            small_ref[...] = jnp.zeros_like(small_ref)

        small_ref[0:1, :] += jnp.sum(d_cqn * nq_, axis=0, keepdims=True)
        small_ref[1:2, :KV_LORA] += jnp.sum(d_ckvn * nkv, axis=0, keepdims=True)

    def row(width, idx=0):
        return pl.BlockSpec((TM, width), lambda i: (i, idx))

    def full(a):
        return pl.BlockSpec(a.shape, lambda i: (0, 0))

    b16 = lambda w: jax.ShapeDtypeStruct((s, w), BF16)
    return pl.pallas_call(
        body, name="mla_prep_bwd", grid=(s // TM,),
        in_specs=[row(1024), row(1024), row(D_GRP), row(Q_LORA, 4), row(KV_LORA, 10), full(gq), full(gkv),
                  full(wuq), full(wuk), full(wuv), row(LANES), row(LANES)],
        out_specs=(row(Q_LORA), row(KV_LORA), row(LANES), row(1024), row(1024), row(D_GRP),
                   pl.BlockSpec((8, Q_LORA), lambda i: (0, 0))),
        out_shape=(b16(Q_LORA), b16(KV_LORA), b16(LANES), b16(1024), b16(1024), b16(D_GRP),
                   jax.ShapeDtypeStruct((8, Q_LORA), F32)),
        compiler_params=_params(("arbitrary",), 40),
    )(dqp, dkp, dvv, rest, rest, gq, gkv, wuq, wuk, wuv, cos_t, sin_t)


def _in_bwd(x, g, dx1, pieces, w):
    s = x.shape[0]
    widths = [a.shape[1] for a in pieces]
    offs = [sum(widths[:k]) for k in range(len(widths))]

    def body(x_ref, g_ref, dx1_ref, *refs):
        piece_refs = refs[:len(pieces)]
        w_ref, dx_ref, small_ref = refs[len(pieces):]
        i = pl.program_id(0)
        dh = jnp.zeros((TM, D_MODEL), F32)
        for pr, off, wd in zip(piece_refs, offs, widths):
            dh = dh + _dot_nt(pr[...], w_ref[:, off:off + wd])
        xv = x_ref[...]
        r = lax.rsqrt(jnp.mean(xv * xv, axis=-1, keepdims=True) + EPS)
        n = xv * r
        d_n = dh * g_ref[...]
        dx_ref[...] = dx1_ref[...] + r * (d_n - n * jnp.mean(d_n * n, axis=-1, keepdims=True))

        @pl.when(i == 0)
        def _():
            small_ref[...] = jnp.zeros_like(small_ref)

        small_ref[0:1, :] += jnp.sum(dh * n, axis=0, keepdims=True)

    def row(width):
        return pl.BlockSpec((TM, width), lambda i: (i, 0))

    return pl.pallas_call(
        body, name="in_bwd", grid=(s // TM,),
        in_specs=[row(D_MODEL), pl.BlockSpec((1, D_MODEL), lambda i: (0, 0)), row(D_MODEL)]
        + [row(wd) for wd in widths] + [pl.BlockSpec(w.shape, lambda i: (0, 0))],
        out_specs=(row(D_MODEL), pl.BlockSpec((8, D_MODEL), lambda i: (0, 0))),
        out_shape=(jax.ShapeDtypeStruct((s, D_MODEL), F32), jax.ShapeDtypeStruct((8, D_MODEL), F32)),
        compiler_params=_params(("arbitrary",), 48),
    )(x, g, dx1, *pieces, w)


def _tn_matmul(a, b, name):
    s, k = a.shape
    n = b.shape[1]
    ts = 512
    tn = min(n, 512)
    steps = s // ts

    def body(a_ref, b_ref, o_ref):
        t = pl.program_id(1)

        @pl.when(t == 0)
        def _():
            o_ref[...] = jnp.zeros_like(o_ref)

        o_ref[...] += _dot_tn(a_ref[...], b_ref[...])

    return pl.pallas_call(
        body, name=name, grid=(n // tn, steps),
        in_specs=[pl.BlockSpec((ts, k), lambda j, t: (t, 0)), pl.BlockSpec((ts, tn), lambda j, t: (t, j))],
        out_specs=pl.BlockSpec((k, tn), lambda j, t: (0, j)),
        out_shape=jax.ShapeDtypeStruct((k, n), F32),
        compiler_params=_params(("parallel", "arbitrary"), 40),
    )(a, b)


def _pack_shards(w_in, w_uq, w_ukv, w_out, w_ple, w_pg):
    return jnp.concatenate([a.reshape(-1, LANES) for a in (w_in, w_uq, w_ukv, w_out, w_ple, w_pg)], axis=0)


def _unpack_shards(packed):
    out, o = [], 0
    for rows, shape in ((R_IN, (D_MODEL, 372)), (R_UQ, (Q_LORA, 96)), (R_UKV, (KV_LORA, 128)),
                        (R_OUT, (128, D_MODEL)), (R_PLE, (PLE_DIM, 128)), (R_PG, (128, D_MODEL))):
        out.append(packed[o:o + rows].reshape((1,) + shape))
        o += rows
    return out


def _unpack_gathered(gath):
    o = 0
    seg = {}
    for name, rows in (("in", R_IN), ("uq", R_UQ), ("ukv", R_UKV), ("out", R_OUT), ("ple", R_PLE), ("pg", R_PG)):
        seg[name] = gath[:, o:o + rows]
        o += rows
    cols = lambda a, k, per: a.reshape(N_DEV, k, per).transpose(1, 0, 2).reshape(k, N_DEV * per)
    return (cols(seg["in"], D_MODEL, 372), cols(seg["uq"], Q_LORA, 96), cols(seg["ukv"], KV_LORA, 128),
            seg["out"].reshape(D_MODEL, D_MODEL), cols(seg["ple"], PLE_DIM, 128),
            seg["pg"].reshape(D_MODEL, D_MODEL))


def _pack_full_grads(d_in, d_uq, d_ukv, d_out, d_ple, d_pg):
    cols = lambda a, per: a.reshape(a.shape[0], N_DEV, per).transpose(1, 0, 2).reshape(N_DEV, -1, LANES)
    return jnp.concatenate([cols(d_in, 372), cols(d_uq, 96), cols(d_ukv, 128),
                            d_out.reshape(N_DEV, -1, LANES), cols(d_ple, 128),
                            d_pg.reshape(N_DEV, -1, LANES)], axis=1)


_VEC_SIZES = (D_MODEL, Q_LORA, KV_LORA, D_GRP, D_GRP, D_MODEL, D_MODEL, D_MODEL)


def _pack_vectors(vs, tail=None):
    flat = jnp.concatenate([v.reshape(-1) for v in vs])
    if tail is None:
        tail = jnp.zeros((D_MODEL,), F32)
    flat = jnp.concatenate([flat, tail.reshape(-1)])
    flat = jnp.pad(flat, (0, R_SMALL * LANES - flat.shape[0]))
    return flat.reshape(R_SMALL, LANES)


def _unpack_vectors(packed):
    flat = packed.reshape(-1)
    out, o = [], 0
    for n in _VEC_SIZES:
        out.append(flat[o:o + n].reshape(1, n))
        o += n
    return out, flat[o:o + D_MODEL]


def kernel(x, p, positions, norm_pre_g, w_in, q_norm_g, w_uq, kv_norm_g, w_ukv, sb_out_norm_g, mla_out_norm_g, w_out, norm_post_g, w_ple, ple_norm_g, w_ple_gate, b_ple_gate, loss_target, m_norm_pre_g, m_w_in, m_q_norm_g, m_w_uq, m_kv_norm_g, m_w_ukv, m_sb_out_norm_g, m_mla_out_norm_g, m_w_out, m_norm_post_g, m_w_ple, m_ple_norm_g, m_w_ple_gate, m_b_ple_gate, v_norm_pre_g, v_w_in, v_q_norm_g, v_w_uq, v_kv_norm_g, v_w_ukv, v_sb_out_norm_g, v_mla_out_norm_g, v_w_out, v_norm_post_g, v_w_ple, v_ple_norm_g, v_w_ple_gate, v_b_ple_gate):
    xs, ps, tgt = x[0], p[0, 0], loss_target[0]
    s = xs.shape[0]

    w_packed = _pack_shards(w_in, w_uq, w_ukv, w_out, w_ple, w_ple_gate)
    gath = _all_gather(w_packed.astype(BF16)).reshape(N_DEV, R_BIG, LANES)
    f_in, f_uq, f_ukv, f_out, f_ple, f_pg = _unpack_gathered(gath)
    grad_x, d_mats, vec_partials, loss_vec = _local_grads(
        xs, ps, positions[0], tgt, norm_pre_g, q_norm_g, kv_norm_g, sb_out_norm_g, mla_out_norm_g,
        norm_post_g, ple_norm_g, b_ple_gate, f_in, f_uq, f_ukv, f_out, f_ple, f_pg)

    g_all = _pack_full_grads(*d_mats)
    small = _pack_vectors(vec_partials, tail=loss_vec)
    land, sland = _grad_exchange(g_all, small)
    return _update(land, sland, grad_x, (w_in, w_uq, w_ukv, w_out, w_ple, w_ple_gate), (m_w_in, m_w_uq, m_w_ukv, m_w_out, m_w_ple, m_w_ple_gate), (v_w_in, v_w_uq, v_w_ukv, v_w_out, v_w_ple, v_w_ple_gate), (norm_pre_g, q_norm_g, kv_norm_g, sb_out_norm_g, mla_out_norm_g, norm_post_g, ple_norm_g, b_ple_gate), (m_norm_pre_g, m_q_norm_g, m_kv_norm_g, m_sb_out_norm_g, m_mla_out_norm_g, m_norm_post_g, m_ple_norm_g, m_b_ple_gate), (v_norm_pre_g, v_q_norm_g, v_kv_norm_g, v_sb_out_norm_g, v_mla_out_norm_g, v_norm_post_g, v_ple_norm_g, v_b_ple_gate))


def _local_grads(xs, ps, pos, tgt, norm_pre_g, q_norm_g, kv_norm_g, sb_out_norm_g, mla_out_norm_g,
                 norm_post_g, ple_norm_g, b_ple_gate, f_in, f_uq, f_ukv, f_out, f_ple, f_pg):
    s = xs.shape[0]
    zc = lambda n: jnp.zeros((D_MODEL, n), BF16)
    w_in_p = jnp.concatenate([f_in[:, :2048], f_in[:, 2464:2976], f_in[:, 2048:2432],
                              zc(64), f_in[:, 2432:2464], zc(32)], axis=1)
    w_uq_p = jnp.pad(f_uq.reshape(Q_LORA, N_HEADS, 96), ((0, 0), (0, 0), (0, 32))).reshape(Q_LORA, 1024)
    ukv4 = f_ukv.reshape(KV_LORA, N_HEADS, 2, HEAD_DIM)
    w_uk_p = jnp.pad(ukv4[:, :, 0], ((0, 0), (0, 0), (0, 64))).reshape(KV_LORA, 1024)
    w_uv = ukv4[:, :, 1].reshape(KV_LORA, D_GRP)

    half = ROPE_DIM // 2
    freq = ROPE_THETA ** (-jnp.arange(half, dtype=F32) / half)
    ang = pos.astype(F32)[:, None] * freq
    cos, sin = jnp.cos(ang), jnp.sin(ang)
    cos_t = jnp.concatenate([jnp.ones((s, 64), F32), cos, cos, jnp.zeros((s, 32), F32)], axis=1)
    sin_t = jnp.concatenate([jnp.zeros((s, 64), F32), -sin, sin, jnp.zeros((s, 32), F32)], axis=1)
    seg = jnp.arange(D_GRP) // HEAD_DIM
    bd = (seg[:, None] == seg[None, :]).astype(BF16)

    qkv, rest, h_b = _in_proj(xs, norm_pre_g, w_in_p)
    sb_o = _sb_fwd(qkv)
    qp, kp, vv, cqn_b, ckvn_b = _mla_prep(rest, q_norm_g, kv_norm_g, w_uq_p, w_uk_p, w_uv, cos_t, sin_t)
    mla_o, lse = _mla_fwd(qp, kp, vv, 8)

    (dx1, d_sbo, d_mlo, d_sbg, d_mlg, x1_b, dgl_b, yc_b, dy_b, p_b, du_b, small_mid) = _mid(
        xs, ps, tgt, sb_o, mla_o, rest, sb_out_norm_g, mla_out_norm_g, f_out, norm_post_g,
        f_ple, ple_norm_g, f_pg, b_ple_gate, bd)
    dqp, dkp, dvv = _mla_bwd(qp, kp, vv, d_mlo, mla_o, lse, 4)
    dq_sb, dk_sb, dv_sb = _sb_bwd(qkv, d_sbo)
    dcq, dckv, dkr, dq_b, dk_b, dv_b, small_prep = _mla_prep_bwd(
        dqp, dkp, dvv, rest, q_norm_g, kv_norm_g, w_uq_p, w_uk_p, w_uv, cos_t, sin_t)
    pieces = [dq_sb, dk_sb, dv_sb, d_sbg, d_mlg, dcq, dckv, dkr]
    grad_x, small_in = _in_bwd(xs, norm_pre_g, dx1, pieces, w_in_p)

    d_cols = [_tn_matmul(h_b, pc, "dw_in_%d" % k) for k, pc in enumerate(pieces)]
    d_in = jnp.concatenate([d_cols[0], d_cols[1], d_cols[2], d_cols[3], d_cols[5], d_cols[6],
                            d_cols[7][:, 64:96], d_cols[4]], axis=1)
    d_uq = _tn_matmul(cqn_b, dq_b, "dw_uq").reshape(Q_LORA, N_HEADS, LANES)[:, :, :96].reshape(Q_LORA, 768)
    d_uk = _tn_matmul(ckvn_b, dk_b, "dw_uk").reshape(KV_LORA, N_HEADS, LANES)[:, :, :HEAD_DIM]
    d_uv = _tn_matmul(ckvn_b, dv_b, "dw_uv").reshape(KV_LORA, N_HEADS, HEAD_DIM)
    d_ukv = jnp.stack([d_uk, d_uv], axis=2).reshape(KV_LORA, 1024)
    d_out = _tn_matmul(yc_b, dy_b, "dw_out")
    d_ple = _tn_matmul(p_b, du_b, "dw_ple")
    d_pg = _tn_matmul(x1_b, dgl_b, "dw_pg")

    vec_partials = [small_in[0], small_prep[0], small_prep[1, :KV_LORA], small_mid[3, :D_GRP],
                    small_mid[3, D_GRP:], small_mid[2], small_mid[1], small_mid[0]]
    return grad_x, (d_in, d_uq, d_ukv, d_out, d_ple, d_pg), vec_partials, small_mid[4]


def _update(land, sland, grad_x, mats, m_mats, v_mats, vecs, m_vecs, v_vecs):
    big = _adamw(land, _pack_shards(*mats), _pack_shards(*m_mats), _pack_shards(*v_mats), 800, "adamw_matrices")
    sm = _adamw(sland, _pack_vectors(vecs), _pack_vectors(m_vecs), _pack_vectors(v_vecs), R_SMALL, "adamw_vectors")

    loss = jnp.sum(_unpack_vectors(sm[0])[1])
    outs = []
    for kind in range(4):
        mat = _unpack_shards(big[kind])
        vec = _unpack_vectors(sm[kind])[0]
        outs += [vec[0], mat[0], vec[1], mat[1], vec[2], mat[2], vec[3], vec[4], mat[3], vec[5],
                 mat[4], vec[6], mat[5], vec[7]]
    return (loss, grad_x[None], *outs)
```

```python
import jax
import jax.numpy as jnp
from jax import lax
from jax.experimental import pallas as pl
from jax.experimental.pallas import tpu as pltpu

F32 = jnp.float32
BF16 = jnp.bfloat16
MESH = pl.DeviceIdType.MESH

N_DEV = 8
D_MODEL = 1024
N_HEADS = 8
HEAD_DIM = 64
D_GRP = N_HEADS * HEAD_DIM
Q_LORA = 256
KV_LORA = 128
ROPE_DIM = 32
PLE_DIM = 256
CHUNK_SHIFT = 6
ROPE_THETA = 10000.0
EPS = 1e-6
SB_SCALE = HEAD_DIM ** -0.5
MLA_SCALE = (HEAD_DIM + ROPE_DIM) ** -0.5
NEG = -1e30
LOG2_E = 1.4426950408889634
LN_2 = 0.6931471805599453
SB_CUTOFF = 110.0

ADAM_LR = 0.001
ADAM_B1 = 0.9
ADAM_B2 = 0.999
ADAM_EPS = 1e-08
ADAM_WD = 0.01
ADAM_STEP = 10

LANES = 128
TQ = 256
TK = 256
TM = 256

R_SMALL = 56
D_IN_P = 3072

_NT = (((1,), (1,)), ((), ()))
_TN = (((0,), (0,)), ((), ()))


def _params(sem, vmem_mb):
    return pltpu.CompilerParams(dimension_semantics=sem, vmem_limit_bytes=vmem_mb << 20)


def _dot(a, b):
    return jnp.dot(a, b, preferred_element_type=F32)


def _dot_nt(a, b):
    return lax.dot_general(a, b, _NT, preferred_element_type=F32)


def _dot_tn(a, b):
    return lax.dot_general(a, b, _TN, preferred_element_type=F32)


def _hl_dot(a, b):
    hi = a.astype(BF16)
    lo = (a - hi.astype(F32)).astype(BF16)
    return _dot(hi, b) + _dot(lo, b)


def _sigmoid(x):
    return 1.0 / (1.0 + jnp.exp(-x))


def _rope_swap(x, lane):
    left = pltpu.roll(x, LANES - 16, axis=1)
    right = pltpu.roll(x, 16, axis=1)
    lo = (lane >= 64) & (lane < 80)
    hi = (lane >= 80) & (lane < 96)
    return jnp.where(lo, left, jnp.where(hi, right, 0.0))


def _all_gather(shards):
    n_op = len(shards)

    def body(*refs):
        x_refs, out_refs = refs[:n_op], refs[n_op:2 * n_op]
        send_sems, recv_sems, local_sems = refs[2 * n_op:]
        x, y, c = lax.axis_index("x"), lax.axis_index("y"), lax.axis_index("c")
        me, sibling = (x, y, c), (x, y, 1 - c)
        chips = [(1 - x, y), (x, 1 - y), (1 - x, 1 - y)]

        def slot(o, px, py, pc):
            return out_refs[o].at[4 * px + 2 * py + pc]

        def copy(o, k, block, to, src=None):
            return pltpu.make_async_remote_copy(
                src_ref=slot(o, *block) if src is None else src, dst_ref=slot(o, *block),
                send_sem=send_sems.at[o, k], recv_sem=recv_sems.at[o, k],
                device_id=to, device_id_type=MESH)

        ops = range(n_op)
        mine = [pltpu.make_async_copy(x_refs[o], slot(o, *me), local_sems.at[o]) for o in ops]
        first = [copy(o, 0, me, sibling, src=x_refs[o]) for o in ops]
        first += [copy(o, 1 + j, me, (*chip, c), src=x_refs[o]) for j, chip in enumerate(chips) for o in ops]
        for cp in mine + first:
            cp.start()
        passed = []
        for j, chip in enumerate(chips):
            for o in ops:
                copy(o, 1 + j, (*chip, c), me).wait_recv()
                passed.append(copy(o, 4 + j, (*chip, c), sibling))
                passed[-1].start()
        for o in ops:
            copy(o, 0, sibling, me).wait_recv()
        for j, chip in enumerate(chips):
            for o in ops:
                copy(o, 4 + j, (*chip, 1 - c), me).wait_recv()
        for cp in first + passed:
            cp.wait_send()
        for cp in mine:
            cp.wait()

    vmem = pl.BlockSpec(memory_space=pltpu.VMEM)
    return pl.pallas_call(
        body, name="weight_all_gather",
        out_shape=[jax.ShapeDtypeStruct((N_DEV,) + a.shape, a.dtype) for a in shards],
        in_specs=[vmem] * n_op, out_specs=[vmem] * n_op,
        scratch_shapes=[pltpu.SemaphoreType.DMA((n_op, 7)), pltpu.SemaphoreType.DMA((n_op, 7)),
                        pltpu.SemaphoreType.DMA((n_op,))],
        compiler_params=pltpu.CompilerParams(vmem_limit_bytes=48 << 20),
    )(*shards)


def _pair_exchange(pays, small):
    n_op = len(pays)
    sr, n = small.shape

    def body(*refs):
        g_refs, s_ref = refs[:n_op], refs[n_op]
        l_refs, sland_ref = refs[n_op + 1:2 * n_op + 1], refs[2 * n_op + 1]
        ssem, rsem, ssem2, rsem2, lsem = refs[2 * n_op + 2:]
        x, y, c = lax.axis_index("x"), lax.axis_index("y"), lax.axis_index("c")
        me = 4 * x + 2 * y + c
        copies = []
        for o in range(n_op):
            for chip in range(4):
                copies.append(pltpu.make_async_remote_copy(
                    src_ref=g_refs[o].at[2 * chip + (1 - c)], dst_ref=l_refs[o].at[chip],
                    send_sem=ssem.at[o, chip], recv_sem=rsem.at[o, chip],
                    device_id=(x, y, 1 - c), device_id_type=MESH))
        for k in range(1, N_DEV):
            peer = (1 - x if (k >> 2) & 1 else x, 1 - y if (k >> 1) & 1 else y, 1 - c if k & 1 else c)
            copies.append(pltpu.make_async_remote_copy(
                src_ref=s_ref, dst_ref=sland_ref.at[me], send_sem=ssem2.at[k], recv_sem=rsem2.at[k],
                device_id=peer, device_id_type=MESH))
        own = pltpu.make_async_copy(s_ref, sland_ref.at[me], lsem)
        own.start()
        for cp in copies:
            cp.start()
        for cp in copies:
            cp.wait()
        own.wait()

    any_spec = pl.BlockSpec(memory_space=pl.ANY)
    return pl.pallas_call(
        body, name="grad_pair_exchange",
        out_shape=[jax.ShapeDtypeStruct((4,) + a.shape[1:], F32) for a in pays]
        + [jax.ShapeDtypeStruct((N_DEV, sr, n), F32)],
        in_specs=[any_spec] * (n_op + 1), out_specs=[any_spec] * (n_op + 1),
        scratch_shapes=[pltpu.SemaphoreType.DMA((n_op, 4)), pltpu.SemaphoreType.DMA((n_op, 4)),
                        pltpu.SemaphoreType.DMA((N_DEV,)), pltpu.SemaphoreType.DMA((N_DEV,)),
                        pltpu.SemaphoreType.DMA],
    )(*pays, small)


def _pair_sum(pay, landed, place, name):
    _, r, c = pay.shape

    def body(place_ref, g_ref, l_ref, s_ref, own_ref):
        i = pl.program_id(0)
        tot = g_ref[...] + l_ref[...]
        s_ref[...] = tot.astype(BF16)

        @pl.when(i == place_ref[1])
        def _():
            own_ref[...] = tot

    grid_spec = pltpu.PrefetchScalarGridSpec(
        num_scalar_prefetch=1, grid=(4,),
        in_specs=[pl.BlockSpec((None, r, c), lambda i, pr: (2 * i + pr[0], 0, 0)),
                  pl.BlockSpec((None, r, c), lambda i, pr: (i, 0, 0))],
        out_specs=[pl.BlockSpec((None, r, c), lambda i, pr: (i, 0, 0)),
                   pl.BlockSpec((r, c), lambda i, pr: (0, 0))])
    return pl.pallas_call(
        body, name=name, grid_spec=grid_spec,
        out_shape=[jax.ShapeDtypeStruct((4, r, c), BF16), jax.ShapeDtypeStruct((r, c), F32)],
        compiler_params=_params(("arbitrary",), 40),
    )(place, pay, landed)


def _chip_exchange(sums):
    n_op = len(sums)

    def body(*refs):
        s_refs, l_refs = refs[:n_op], refs[n_op:2 * n_op]
        ssem, rsem = refs[2 * n_op:]
        x, y, c = lax.axis_index("x"), lax.axis_index("y"), lax.axis_index("c")
        copies = []
        for rel in range(1, 4):
            px = 1 - x if rel & 2 else x
            py = 1 - y if rel & 1 else y
            for o in range(n_op):
                copies.append(pltpu.make_async_remote_copy(
                    src_ref=s_refs[o].at[2 * px + py], dst_ref=l_refs[o].at[rel - 1],
                    send_sem=ssem.at[o, rel - 1], recv_sem=rsem.at[o, rel - 1],
                    device_id=(px, py, c), device_id_type=MESH))
        for cp in copies:
            cp.start()
        for cp in copies:
            cp.wait()

    any_spec = pl.BlockSpec(memory_space=pl.ANY)
    return pl.pallas_call(
        body, name="grad_chip_exchange",
        out_shape=[jax.ShapeDtypeStruct((3,) + a.shape[1:], BF16) for a in sums],
        in_specs=[any_spec] * n_op, out_specs=[any_spec] * n_op,
        scratch_shapes=[pltpu.SemaphoreType.DMA((n_op, 3)), pltpu.SemaphoreType.DMA((n_op, 3))],
    )(*sums)


def _adamw_math(g, w, m, v):
    mn = ADAM_B1 * m + (1.0 - ADAM_B1) * g
    vn = ADAM_B2 * v + (1.0 - ADAM_B2) * (g * g)
    m_hat = mn / (1.0 - ADAM_B1 ** ADAM_STEP)
    v_hat = vn / (1.0 - ADAM_B2 ** ADAM_STEP)
    return -ADAM_LR * (m_hat / (jnp.sqrt(v_hat) + ADAM_EPS) + ADAM_WD * w), mn, vn


def _adamw_matrix(own, landed, w, m, v, name):
    r, c = own.shape
    br = min(r, 256)

    def body(own_ref, l_ref, w_ref, m_ref, v_ref, g_out, d_out, m_out, v_out):
        g = own_ref[...]
        for k in range(3):
            g = g + l_ref[k].astype(F32)
        g_out[...] = g
        d_out[...], m_out[...], v_out[...] = _adamw_math(g, w_ref[...], m_ref[...], v_ref[...])

    row = pl.BlockSpec((br, c), lambda i: (i, 0))
    shp = jax.ShapeDtypeStruct((r, c), F32)
    return pl.pallas_call(
        body, name=name, grid=(r // br,),
        in_specs=[row, pl.BlockSpec((3, br, c), lambda i: (0, i, 0)), row, row, row],
        out_specs=(row, row, row, row), out_shape=(shp, shp, shp, shp),
        compiler_params=_params(("parallel",), 40),
    )(own, landed, w, m, v)


def _adamw_vectors(sland, w, m, v):
    _, r, n = sland.shape

    def body(l_ref, w_ref, m_ref, v_ref, g_out, d_out, m_out, v_out):
        g = l_ref[0]
        for j in range(1, N_DEV):
            g = g + l_ref[j]
        g_out[...] = g
        d_out[...], m_out[...], v_out[...] = _adamw_math(g, w_ref[...], m_ref[...], v_ref[...])

    shp = jax.ShapeDtypeStruct((r, n), F32)
    vmem = pl.BlockSpec(memory_space=pltpu.VMEM)
    return pl.pallas_call(
        body, name="adamw_vectors", in_specs=[vmem] * 4, out_specs=(vmem,) * 4,
        out_shape=(shp, shp, shp, shp),
    )(sland, w, m, v)


def _in_proj(x, g, w):
    s = x.shape[0]

    def body(x_ref, g_ref, w_ref, qkv_ref, rest_ref, h_ref):
        xv = x_ref[...]
        r = lax.rsqrt(jnp.mean(xv * xv, axis=-1, keepdims=True) + EPS)
        h = ((xv * r) * g_ref[...]).astype(BF16)
        h_ref[...] = h
        qkv_ref[...] = _dot(h, w_ref[:, :1536]).astype(BF16)
        rest_ref[...] = _dot(h, w_ref[:, 1536:])

    return pl.pallas_call(
        body, name="in_proj", grid=(s // TM,),
        in_specs=[pl.BlockSpec((TM, D_MODEL), lambda i: (i, 0)),
                  pl.BlockSpec((1, D_MODEL), lambda i: (0, 0)),
                  pl.BlockSpec((D_MODEL, D_IN_P), lambda i: (0, 0))],
        out_specs=(pl.BlockSpec((TM, 1536), lambda i: (i, 0)),
                   pl.BlockSpec((TM, 1536), lambda i: (i, 0)),
                   pl.BlockSpec((TM, D_MODEL), lambda i: (i, 0))),
        out_shape=(jax.ShapeDtypeStruct((s, 1536), BF16), jax.ShapeDtypeStruct((s, 1536), F32),
                   jax.ShapeDtypeStruct((s, D_MODEL), BF16)),
        compiler_params=_params(("parallel",), 48),
    )(x, g, w)


def _mla_prep(rest, gq, gkv, wuq, wuk, wuv, cos_t, sin_t):
    s = rest.shape[0]

    def body(cq_ref, ckv_ref, kr_ref, gq_ref, gkv_ref, wuq_ref, wuk_ref, wuv_ref, c_ref, s_ref,
             qp_ref, kp_ref, vv_ref, cqn_ref, ckvn_ref):
        lane = lax.broadcasted_iota(jnp.int32, (1, LANES), 1)
        cos_v, sin_v = c_ref[...], s_ref[...]
        cq = cq_ref[...]
        rq = lax.rsqrt(jnp.mean(cq * cq, axis=-1, keepdims=True) + EPS)
        cqn = ((cq * rq) * gq_ref[...]).astype(BF16)
        cqn_ref[...] = cqn
        q = _dot(cqn, wuq_ref[...])
        ckv = ckv_ref[...]
        rkv = lax.rsqrt(jnp.mean(ckv * ckv, axis=-1, keepdims=True) + EPS)
        ckvn = ((ckv * rkv) * gkv_ref[...]).astype(BF16)
        ckvn_ref[...] = ckvn
        kn = _dot(ckvn, wuk_ref[...])
        vv_ref[...] = _dot(ckvn, wuv_ref[...]).astype(BF16)
        kr = kr_ref[...]
        kr_roped = kr * cos_v + _rope_swap(kr, lane) * sin_v
        for h in range(N_HEADS):
            sl = slice(h * LANES, (h + 1) * LANES)
            qh = q[:, sl]
            qp_ref[:, sl] = (qh * cos_v + _rope_swap(qh, lane) * sin_v).astype(BF16)
            kp_ref[:, sl] = (kn[:, sl] + kr_roped).astype(BF16)

    def row(width, idx):
        return pl.BlockSpec((TM, width), lambda i: (i, idx))

    def full(a):
        return pl.BlockSpec(a.shape, lambda i: (0, 0))

    return pl.pallas_call(
        body, name="mla_prep", grid=(s // TM,),
        in_specs=[row(Q_LORA, 4), row(KV_LORA, 10), row(LANES, 11), full(gq), full(gkv),
                  full(wuq), full(wuk), full(wuv), row(LANES, 0), row(LANES, 0)],
        out_specs=(row(1024, 0), row(1024, 0), row(D_GRP, 0), row(Q_LORA, 0), row(KV_LORA, 0)),
        out_shape=(jax.ShapeDtypeStruct((s, 1024), BF16), jax.ShapeDtypeStruct((s, 1024), BF16),
                   jax.ShapeDtypeStruct((s, D_GRP), BF16), jax.ShapeDtypeStruct((s, Q_LORA), BF16),
                   jax.ShapeDtypeStruct((s, KV_LORA), BF16)),
        compiler_params=_params(("parallel",), 32),
    )(rest, rest, rest, gq, gkv, wuq, wuk, wuv, cos_t, sin_t)


def _pair_masks(k_ref, v_ref, ka, kb, va, vb, is_a):
    kv = k_ref[...]
    zk = jnp.zeros_like(kv)
    ka[...] = jnp.where(is_a, kv, zk)
    kb[...] = jnp.where(is_a, zk, kv)
    vv = v_ref[...]
    va[...] = jnp.where(is_a, vv, zk)
    vb[...] = jnp.where(is_a, zk, vv)


def _sb_live(n, qi, c_a, c_b):
    live = jnp.maximum(jnp.max(c_a), jnp.max(c_b)) > -SB_CUTOFF
    return jnp.logical_and(n < qi, live)


def _sb_fwd(qkv):
    s = qkv.shape[0]

    def body(q_ref, k_ref, v_ref, o_ref, ka, kb, va, vb, acc):
        qi = pl.program_id(1)
        lane = lax.broadcasted_iota(jnp.int32, (1, LANES), 1)
        is_a = lane < HEAD_DIM

        @pl.when(qi == 0)
        def _():
            _pair_masks(k_ref, v_ref, ka, kb, va, vb, is_a)

        qs = q_ref[...] * SB_SCALE
        r_i = lax.broadcasted_iota(jnp.int32, (TQ, TK), 0)
        c_i = lax.broadcasted_iota(jnp.int32, (TQ, TK), 1)
        past = c_i < r_i
        upper = (r_i > c_i).astype(BF16)
        acc[...] = jnp.zeros_like(acc)

        def tile(j, carries, diag):
            ks = pl.ds(pl.multiple_of(j * TK, TK), TK)
            out = []
            for kx, vx, c in ((ka, va, carries[0]), (kb, vb, carries[1])):
                z = _dot_nt(qs, kx[ks, :])
                e = jnp.exp(-jnp.abs(z))
                lf = -(jnp.maximum(z, 0.0) + jnp.log(1.0 + e))
                if diag:
                    lf = jnp.where(past, lf, 0.0)
                suf = _hl_dot(lf, upper) + c
                w = jnp.exp(z + lf + suf)
                if diag:
                    w = jnp.where(past, w, 0.0)
                acc[...] += _dot(w.astype(BF16), vx[ks, :])
                out.append(c + jnp.sum(lf, axis=1, keepdims=True))
            return tuple(out)

        zero = jnp.zeros((TQ, 1), F32)
        carries = tile(qi, (zero, zero), True)

        def step(st):
            return (st[0] + 1,) + tile(qi - 1 - st[0], st[1:], False)

        lax.while_loop(lambda st: _sb_live(st[0], qi, st[1], st[2]), step, (0,) + carries)
        o_ref[...] = acc[...]

    slab = lambda off: pl.BlockSpec((s, LANES), lambda p, qi: (0, off + p))
    return pl.pallas_call(
        body, name="sb_fwd", grid=(4, s // TQ),
        in_specs=[pl.BlockSpec((TQ, LANES), lambda p, qi: (qi, p)), slab(4), slab(8)],
        out_specs=pl.BlockSpec((TQ, LANES), lambda p, qi: (qi, p)),
        out_shape=jax.ShapeDtypeStruct((s, D_GRP), F32),
        scratch_shapes=[pltpu.VMEM((s, LANES), BF16)] * 4 + [pltpu.VMEM((TQ, LANES), F32)],
        compiler_params=_params(("arbitrary", "arbitrary"), 40),
    )(qkv, qkv, qkv)


def _sb_bwd(qkv, d_o):
    s = qkv.shape[0]
    nq = s // TQ
    nk = s // TK

    def body(q_ref, k_ref, v_ref, do_ref, dq_ref, dk_ref, dv_ref,
             ka, kb, va, vb, x1s, bts, dqacc, dkacc, dvacc):
        qi = pl.program_id(1)
        lane = lax.broadcasted_iota(jnp.int32, (1, LANES), 1)
        is_a = lane < HEAD_DIM

        @pl.when(qi == 0)
        def _():
            _pair_masks(k_ref, v_ref, ka, kb, va, vb, is_a)
            dkacc[...] = jnp.zeros_like(dkacc)
            dvacc[...] = jnp.zeros_like(dvacc)

        qs = q_ref[...] * SB_SCALE
        zq = jnp.zeros_like(qs)
        qs_x = (jnp.where(is_a, qs, zq), jnp.where(is_a, zq, qs))
        dob = do_ref[...].astype(BF16)
        do_x = (jnp.where(is_a, dob, zq), jnp.where(is_a, zq, dob))
        r_i = lax.broadcasted_iota(jnp.int32, (TQ, TK), 0)
        c_i = lax.broadcasted_iota(jnp.int32, (TQ, TK), 1)
        past = c_i < r_i
        upper = (r_i > c_i).astype(BF16)
        upper_incl = (r_i >= c_i).astype(BF16)
        dqacc[...] = jnp.zeros_like(dqacc)
        k_x = (ka, kb)
        v_x = (va, vb)

        def tile1(j, carries, diag):
            ks = pl.ds(pl.multiple_of(j * TK, TK), TK)
            out = []
            for hx in range(2):
                c, gsum = carries[2 * hx], carries[2 * hx + 1]
                z = _dot_nt(qs, k_x[hx][ks, :])
                e = jnp.exp(-jnp.abs(z))
                den = 1.0 + e
                lf = -(jnp.maximum(z, 0.0) + jnp.log(den))
                rden = 1.0 / den
                pos = z >= 0.0
                beta = jnp.where(pos, rden, e * rden)
                omb = jnp.where(pos, e * rden, rden)
                if diag:
                    lf = jnp.where(past, lf, 0.0)
                    beta = jnp.where(past, beta, 0.0)
                suf = _hl_dot(lf, upper) + c
                a = jnp.exp(z + lf + suf)
                if diag:
                    a = jnp.where(past, a, 0.0)
                d_a = _dot_nt(dob, v_x[hx][ks, :])
                g = a * d_a
                sg = _hl_dot(g, upper_incl) + gsum
                x1s[j, hx] = g * omb + beta * sg
                bts[j, hx] = beta
                dvacc[ks, :] += _dot_tn(a.astype(BF16), do_x[hx])
                out.append(c + jnp.sum(lf, axis=1, keepdims=True))
                out.append(gsum + jnp.sum(g, axis=1, keepdims=True))
            return tuple(out)

        zero = jnp.zeros((TQ, 1), F32)
        carries = tile1(qi, (zero, zero, zero, zero), True)

        def step(st):
            return (st[0] + 1,) + tile1(qi - 1 - st[0], st[1:], False)

        swept = lax.while_loop(lambda st: _sb_live(st[0], qi, st[1], st[3]), step, (0,) + carries)
        g_tot = (swept[2], swept[4])

        def tile2(j, _):
            ks = pl.ds(pl.multiple_of(j * TK, TK), TK)
            for hx in range(2):
                dz = (x1s[j, hx] - bts[j, hx] * g_tot[hx]).astype(BF16)
                dqacc[...] += _dot(dz, k_x[hx][ks, :])
                dkacc[ks, :] += _dot_tn(dz, qs_x[hx])
            return 0

        lax.fori_loop(qi - swept[0], qi + 1, tile2, 0)
        dq_ref[...] = (dqacc[...] * SB_SCALE).astype(BF16)

        @pl.when(qi == nq - 1)
        def _():
            dk_ref[...] = dkacc[...].astype(BF16)
            dv_ref[...] = dvacc[...].astype(BF16)

    slab = lambda off: pl.BlockSpec((s, LANES), lambda p, qi: (0, off + p))
    blk = pl.BlockSpec((TQ, LANES), lambda p, qi: (qi, p))
    out_slab = pl.BlockSpec((s, LANES), lambda p, qi: (0, p))
    shp = jax.ShapeDtypeStruct((s, D_GRP), BF16)
    return pl.pallas_call(
        body, name="sb_bwd", grid=(4, nq),
        in_specs=[blk, slab(4), slab(8), blk],
        out_specs=(blk, out_slab, out_slab), out_shape=(shp, shp, shp),
        scratch_shapes=[pltpu.VMEM((s, LANES), BF16)] * 4
        + [pltpu.VMEM((nk, 2, TQ, TK), F32)] * 2
        + [pltpu.VMEM((TQ, LANES), F32), pltpu.VMEM((s, LANES), F32), pltpu.VMEM((s, LANES), F32)],
        compiler_params=_params(("arbitrary", "arbitrary"), 56),
    )(qkv, qkv, qkv, d_o)


def _mla_fwd(qp, kp, vv, hb):
    s = qp.shape[0]
    c2 = MLA_SCALE * LOG2_E

    def body(q_ref, k_ref, v_ref, o_ref, lse_ref, vaug, mrun, mb, acc):
        qi = pl.program_id(1)
        lane = lax.broadcasted_iota(jnp.int32, (1, LANES), 1)
        is_a = lane < HEAD_DIM

        @pl.when(qi == 0)
        def _():
            for h in range(hb):
                vp = v_ref[:, (h // 2) * LANES:(h // 2 + 1) * LANES]
                mine = is_a if h % 2 == 0 else jnp.logical_not(is_a)
                vaug[h] = jnp.where(mine, vp, jnp.ones_like(vp))

        r_i = lax.broadcasted_iota(jnp.int32, (TQ, TK), 0)
        c_i = lax.broadcasted_iota(jnp.int32, (TQ, TK), 1)
        visible = (c_i >> CHUNK_SHIFT) <= (r_i >> CHUNK_SHIFT)

        def scores(j):
            ks = pl.ds(pl.multiple_of(j * TK, TK), TK)
            return ks, [_dot_nt(q_ref[:, h * LANES:(h + 1) * LANES], k_ref[ks, h * LANES:(h + 1) * LANES])
                        for h in range(hb)]

        def sweep(tile):
            def loop(j, carry):
                tile(j, False)
                return carry

            lax.fori_loop(0, qi, loop, 0)
            tile(qi, True)

        mrun[...] = jnp.full_like(mrun, NEG)

        def tile_max(j, diag):
            _, zs = scores(j)
            for h in range(hb):
                z = jnp.where(visible, zs[h], NEG) if diag else zs[h]
                mrun[h] = jnp.maximum(mrun[h], z)

        sweep(tile_max)
        for h in range(hb):
            m = jnp.max(mrun[h], axis=1, keepdims=True) * c2
            mb[h] = jnp.broadcast_to(m, (TQ, TK))
        acc[...] = jnp.zeros_like(acc)

        def tile_pv(j, diag):
            ks, zs = scores(j)
            for h in range(hb):
                e = zs[h] * c2 - mb[h]
                if diag:
                    e = jnp.where(visible, e, NEG)
                acc[h] += _dot(jnp.exp2(e).astype(BF16), vaug[h, ks, :])

        sweep(tile_pv)
        for pr in range(hb // 2):
            a, b = 2 * pr, 2 * pr + 1
            psl = slice(pr * LANES, (pr + 1) * LANES)
            acc_a, acc_b = acc[a], acc[b]
            l_a = pltpu.roll(acc_a, HEAD_DIM, axis=1)
            l_b = pltpu.roll(acc_b, HEAD_DIM, axis=1)
            o_ref[:, psl] = jnp.where(is_a, acc_a * (1.0 / l_a), acc_b * (1.0 / l_b))
            lse_ref[:, psl] = jnp.where(is_a, mb[a, :, :LANES] * LN_2 + jnp.log(l_a),
                                        mb[b, :, :LANES] * LN_2 + jnp.log(l_b))

    blk = pl.BlockSpec((TQ, hb * HEAD_DIM), lambda g, qi: (qi, g))
    shp = jax.ShapeDtypeStruct((s, D_GRP), F32)
    return pl.pallas_call(
        body, name="mla_fwd", grid=(N_HEADS // hb, s // TQ),
        in_specs=[pl.BlockSpec((TQ, hb * LANES), lambda g, qi: (qi, g)),
                  pl.BlockSpec((s, hb * LANES), lambda g, qi: (0, g)),
                  pl.BlockSpec((s, hb * HEAD_DIM), lambda g, qi: (0, g))],
        out_specs=(blk, blk), out_shape=(shp, shp),
        scratch_shapes=[pltpu.VMEM((hb, s, LANES), BF16), pltpu.VMEM((hb, TQ, TK), F32),
                        pltpu.VMEM((hb, TQ, TK), F32), pltpu.VMEM((hb, TQ, LANES), F32)],
        compiler_params=_params(("arbitrary", "arbitrary"), 56),
    )(qp, kp, vv)


def _mla_bwd(qp, kp, vv, d_o, o, lse, hb):
    s = qp.shape[0]
    c2 = MLA_SCALE * LOG2_E

    def body(q_ref, k_ref, v_ref, do_ref, o_ref, lse_ref, dq_ref, dk_ref, dv_ref, dqacc, lse_b, delta_b):
        qi = pl.program_id(1)
        lane = lax.broadcasted_iota(jnp.int32, (1, LANES), 1)
        is_a = lane < HEAD_DIM

        @pl.when(qi == 0)
        def _():
            dk_ref[...] = jnp.zeros_like(dk_ref)
            dv_ref[...] = jnp.zeros_like(dv_ref)

        r_i = lax.broadcasted_iota(jnp.int32, (TQ, TK), 0)
        c_i = lax.broadcasted_iota(jnp.int32, (TQ, TK), 1)
        visible = (c_i >> CHUNK_SHIFT) <= (r_i >> CHUNK_SHIFT)
        do_x = []
        for h in range(hb):
            psl = slice((h // 2) * LANES, (h // 2 + 1) * LANES)
            mine = is_a if h % 2 == 0 else jnp.logical_not(is_a)
            d_o = do_ref[:, psl]
            delta = jnp.sum(jnp.where(mine, d_o * o_ref[:, psl], 0.0), axis=1, keepdims=True)
            lse_h = jnp.sum(jnp.where(lane == (h % 2) * HEAD_DIM, lse_ref[:, psl], 0.0), axis=1, keepdims=True)
            lse_b[h] = jnp.broadcast_to(lse_h * LOG2_E, (TQ, TK))
            delta_b[h] = jnp.broadcast_to(delta, (TQ, TK))
            do_x.append(jnp.where(mine, d_o, 0.0).astype(BF16))
        dqacc[...] = jnp.zeros_like(dqacc)

        def tile(j, diag):
            ks = pl.ds(pl.multiple_of(j * TK, TK), TK)
            head = lambda h: slice(h * LANES, (h + 1) * LANES)
            pair = lambda h: slice((h // 2) * LANES, (h // 2 + 1) * LANES)
            zs = [_dot_nt(q_ref[:, head(h)], k_ref[ks, head(h)]) for h in range(hb)]
            dps = [_dot_nt(do_x[h], v_ref[ks, pair(h)]) for h in range(hb)]
            for h in range(hb):
                e = zs[h] * c2 - lse_b[h]
                if diag:
                    e = jnp.where(visible, e, NEG)
                p = jnp.exp2(e)
                ds = (p * (dps[h] - delta_b[h]) * MLA_SCALE).astype(BF16)
                dqacc[h] += _dot(ds, k_ref[ks, head(h)])
                dk_ref[ks, head(h)] += _dot_tn(ds, q_ref[:, head(h)])
                dv_ref[ks, pair(h)] += _dot_tn(p.astype(BF16), do_x[h])

        def loop(j, c):
            tile(j, False)
            return c

        lax.fori_loop(0, qi, loop, 0)
        tile(qi, True)
        for h in range(hb):
            dq_ref[:, h * LANES:(h + 1) * LANES] = dqacc[h]

    blk = pl.BlockSpec((TQ, hb * HEAD_DIM), lambda g, qi: (qi, g))
    return pl.pallas_call(
        body, name="mla_bwd", grid=(N_HEADS // hb, s // TQ),
        in_specs=[pl.BlockSpec((TQ, hb * LANES), lambda g, qi: (qi, g)),
                  pl.BlockSpec((s, hb * LANES), lambda g, qi: (0, g)),
                  pl.BlockSpec((s, hb * HEAD_DIM), lambda g, qi: (0, g)), blk, blk, blk],
        out_specs=(pl.BlockSpec((TQ, hb * LANES), lambda g, qi: (qi, g)),
                   pl.BlockSpec((s, hb * LANES), lambda g, qi: (0, g)),
                   pl.BlockSpec((s, hb * HEAD_DIM), lambda g, qi: (0, g))),
        out_shape=(jax.ShapeDtypeStruct((s, 1024), F32), jax.ShapeDtypeStruct((s, 1024), F32),
                   jax.ShapeDtypeStruct((s, D_GRP), F32)),
        scratch_shapes=[pltpu.VMEM((hb, TQ, LANES), F32), pltpu.VMEM((hb, TQ, TK), F32),
                        pltpu.VMEM((hb, TQ, TK), F32)],
        compiler_params=_params(("arbitrary", "arbitrary"), 56),
    )(qp, kp, vv, d_o, o, lse)


def _mid(x, p, target, sb_o, mla_o, rest, g_sb, g_mla, w_out, g_post, w_ple, g_ple, w_pg, b_pg, bd):
    s = x.shape[0]

    def body(x_ref, p_ref, t_ref, sbo_ref, mlo_ref, sbg_ref, mlg_ref, gsb_ref, gml_ref, wout_ref,
             gpost_ref, wple_ref, gple_ref, wpg_ref, bpg_ref, bd_ref,
             dx1_ref, dsbo_ref, dmlo_ref, dsbg_ref, dmlg_ref, x1b_ref, dglb_ref, ycb_ref, dyb_ref,
             pb_ref, dub_ref, small_ref):
        i = pl.program_id(0)
        bd_m = bd_ref[...]

        def seg_mean(v):
            return _hl_dot(v, bd_m) * (1.0 / HEAD_DIM)

        groups = []
        for o_ref, gate_ref, gain_ref in ((sbo_ref, sbg_ref, gsb_ref), (mlo_ref, mlg_ref, gml_ref)):
            o = o_ref[...]
            r = lax.rsqrt(seg_mean(o * o) + EPS)
            n = o * r
            hn = n * gain_ref[...]
            gate = gate_ref[...]
            sg = _sigmoid(gate)
            si = gate * sg
            groups.append((r, n, hn, gate, sg, si, gain_ref[...]))
        ya = (groups[0][2] * groups[0][5]).astype(BF16)
        yb = (groups[1][2] * groups[1][5]).astype(BF16)
        ycb_ref[:, :D_GRP] = ya
        ycb_ref[:, D_GRP:] = yb
        y = _dot(ya, wout_ref[:D_GRP, :]) + _dot(yb, wout_ref[D_GRP:, :])
        ry = lax.rsqrt(jnp.mean(y * y, axis=-1, keepdims=True) + EPS)
        ny = y * ry
        x1 = x_ref[...] + ny * gpost_ref[...]
        x1b = x1.astype(BF16)
        x1b_ref[...] = x1b
        pb = p_ref[...].astype(BF16)
        pb_ref[...] = pb
        u = _dot(pb, wple_ref[...])
        ru = lax.rsqrt(jnp.mean(u * u, axis=-1, keepdims=True) + EPS)
        nu = u * ru
        ple = nu * gple_ref[...]
        gate = _sigmoid(_dot(x1b, wpg_ref[...]) + bpg_ref[...])
        x2 = x1 + ple * gate
        diff = x2 - t_ref[...]
        dx2 = diff * (1.0 / D_MODEL)

        d_ple = dx2 * gate
        d_glin = (dx2 * ple) * (gate * (1.0 - gate))
        dglb = d_glin.astype(BF16)
        dglb_ref[...] = dglb
        dx1 = dx2 + _dot_nt(dglb, wpg_ref[...])
        dx1_ref[...] = dx1
        d_nu = d_ple * gple_ref[...]
        d_u = ru * (d_nu - nu * jnp.mean(d_nu * nu, axis=-1, keepdims=True))
        dub_ref[...] = d_u.astype(BF16)
        d_ny = dx1 * gpost_ref[...]
        d_y = ry * (d_ny - ny * jnp.mean(d_ny * ny, axis=-1, keepdims=True))
        dyb = d_y.astype(BF16)
        dyb_ref[...] = dyb
        d_yc = (_dot_nt(dyb, wout_ref[:D_GRP, :]), _dot_nt(dyb, wout_ref[D_GRP:, :]))

        d_gain = []
        for gx, (do_ref, dg_ref) in enumerate(((dsbo_ref, dsbg_ref), (dmlo_ref, dmlg_ref))):
            r, n, hn, gate_g, sg, si, gain = groups[gx]
            dyg = d_yc[gx]
            d_hn = dyg * si
            dg_ref[...] = (dyg * hn * (sg * (1.0 + gate_g * (1.0 - sg)))).astype(BF16)
            d_gain.append(jnp.sum(d_hn * n, axis=0, keepdims=True))
            d_n = d_hn * gain
            do_ref[...] = r * (d_n - n * seg_mean(d_n * n))

        @pl.when(i == 0)
        def _():
            small_ref[...] = jnp.zeros_like(small_ref)

        small_ref[0:1, :] += jnp.sum(d_glin, axis=0, keepdims=True)
        small_ref[1:2, :] += jnp.sum(d_ple * nu, axis=0, keepdims=True)
        small_ref[2:3, :] += jnp.sum(dx1 * ny, axis=0, keepdims=True)
        small_ref[3:4, :D_GRP] += d_gain[0]
        small_ref[3:4, D_GRP:] += d_gain[1]
        small_ref[4:5, :] += jnp.sum(diff * diff, axis=0, keepdims=True) * (0.5 / D_MODEL)

    def row(width, idx=0):
        return pl.BlockSpec((TM, width), lambda i: (i, idx))

    def full(a):
        return pl.BlockSpec(a.shape, lambda i: (0, 0))

    f32 = lambda w: jax.ShapeDtypeStruct((s, w), F32)
    b16 = lambda w: jax.ShapeDtypeStruct((s, w), BF16)
    return pl.pallas_call(
        body, name="mid", grid=(s // TM,),
        in_specs=[row(D_MODEL), row(PLE_DIM), row(D_MODEL), row(D_GRP), row(D_GRP),
                  row(D_GRP, 0), row(D_GRP, 1), full(g_sb), full(g_mla), full(w_out), full(g_post),
                  full(w_ple), full(g_ple), full(w_pg), full(b_pg), full(bd)],
        out_specs=(row(D_MODEL), row(D_GRP), row(D_GRP), row(D_GRP), row(D_GRP), row(D_MODEL),
                   row(D_MODEL), row(D_MODEL), row(D_MODEL), row(PLE_DIM), row(D_MODEL),
                   pl.BlockSpec((8, D_MODEL), lambda i: (0, 0))),
        out_shape=(f32(D_MODEL), f32(D_GRP), f32(D_GRP), b16(D_GRP), b16(D_GRP), b16(D_MODEL),
                   b16(D_MODEL), b16(D_MODEL), b16(D_MODEL), b16(PLE_DIM), b16(D_MODEL),
                   jax.ShapeDtypeStruct((8, D_MODEL), F32)),
        compiler_params=_params(("arbitrary",), 56),
    )(x, p, target, sb_o, mla_o, rest, rest, g_sb, g_mla, w_out, g_post, w_ple, g_ple, w_pg, b_pg, bd)


def _mla_prep_bwd(dqp, dkp, dvv, rest, gq, gkv, wuq, wuk, wuv, cos_t, sin_t):
    s = rest.shape[0]

    def body(dqp_ref, dkp_ref, dvv_ref, cq_ref, ckv_ref, gq_ref, gkv_ref, wuq_ref, wuk_ref, wuv_ref,
             c_ref, s_ref, dcq_ref, dckv_ref, dkr_ref, dqb_ref, dkb_ref, dvb_ref, small_ref):
        i = pl.program_id(0)
        lane = lax.broadcasted_iota(jnp.int32, (1, LANES), 1)
        in_rope = (lane >= HEAD_DIM) & (lane < HEAD_DIM + ROPE_DIM)
        cos_v, sin_v = c_ref[...], s_ref[...]
        dkr_roped = jnp.zeros((TM, LANES), F32)
        for h in range(N_HEADS):
            sl = slice(h * LANES, (h + 1) * LANES)
            dy = dqp_ref[:, sl]
            dqb_ref[:, sl] = (dy * cos_v + _rope_swap(dy * sin_v, lane)).astype(BF16)
            dkh = dkp_ref[:, sl]
            dkb_ref[:, sl] = dkh.astype(BF16)
            dkr_roped = dkr_roped + jnp.where(in_rope, dkh, 0.0)
        dkr_ref[...] = (dkr_roped * cos_v + _rope_swap(dkr_roped * sin_v, lane)).astype(BF16)
        dvb = dvv_ref[...].astype(BF16)
        dvb_ref[...] = dvb

        cq = cq_ref[...]
        rq = lax.rsqrt(jnp.mean(cq * cq, axis=-1, keepdims=True) + EPS)
        nq_ = cq * rq
        d_cqn = _dot_nt(dqb_ref[...], wuq_ref[...])
        d_n = d_cqn * gq_ref[...]
        dcq_ref[...] = (rq * (d_n - nq_ * jnp.mean(d_n * nq_, axis=-1, keepdims=True))).astype(BF16)

        ckv = ckv_ref[...]
        rkv = lax.rsqrt(jnp.mean(ckv * ckv, axis=-1, keepdims=True) + EPS)
        nkv = ckv * rkv
        d_ckvn = _dot_nt(dkb_ref[...], wuk_ref[...]) + _dot_nt(dvb, wuv_ref[...])
        d_n2 = d_ckvn * gkv_ref[...]
        dckv_ref[...] = (rkv * (d_n2 - nkv * jnp.mean(d_n2 * nkv, axis=-1, keepdims=True))).astype(BF16)

        @pl.when(i == 0)
        def _():
            small_ref[...] = jnp.zeros_like(small_ref)

        small_ref[0:1, :] += jnp.sum(d_cqn * nq_, axis=0, keepdims=True)
        small_ref[1:2, :KV_LORA] += jnp.sum(d_ckvn * nkv, axis=0, keepdims=True)

    def row(width, idx=0):
        return pl.BlockSpec((TM, width), lambda i: (i, idx))

    def full(a):
        return pl.BlockSpec(a.shape, lambda i: (0, 0))

    b16 = lambda w: jax.ShapeDtypeStruct((s, w), BF16)
    return pl.pallas_call(
        body, name="mla_prep_bwd", grid=(s // TM,),
        in_specs=[row(1024), row(1024), row(D_GRP), row(Q_LORA, 4), row(KV_LORA, 10), full(gq), full(gkv),
                  full(wuq), full(wuk), full(wuv), row(LANES), row(LANES)],
        out_specs=(row(Q_LORA), row(KV_LORA), row(LANES), row(1024), row(1024), row(D_GRP),
                   pl.BlockSpec((8, Q_LORA), lambda i: (0, 0))),
        out_shape=(b16(Q_LORA), b16(KV_LORA), b16(LANES), b16(1024), b16(1024), b16(D_GRP),
                   jax.ShapeDtypeStruct((8, Q_LORA), F32)),
        compiler_params=_params(("arbitrary",), 40),
    )(dqp, dkp, dvv, rest, rest, gq, gkv, wuq, wuk, wuv, cos_t, sin_t)


def _in_bwd(x, g, dx1, pieces, w):
    s = x.shape[0]
    widths = [a.shape[1] for a in pieces]
    offs = [sum(widths[:k]) for k in range(len(widths))]

    def body(x_ref, g_ref, dx1_ref, *refs):
        piece_refs = refs[:len(pieces)]
        w_ref, dx_ref, small_ref = refs[len(pieces):]
        i = pl.program_id(0)
        dh = jnp.zeros((TM, D_MODEL), F32)
        for pr, off, wd in zip(piece_refs, offs, widths):
            dh = dh + _dot_nt(pr[...], w_ref[:, off:off + wd])
        xv = x_ref[...]
        r = lax.rsqrt(jnp.mean(xv * xv, axis=-1, keepdims=True) + EPS)
        n = xv * r
        d_n = dh * g_ref[...]
        dx_ref[...] = dx1_ref[...] + r * (d_n - n * jnp.mean(d_n * n, axis=-1, keepdims=True))

        @pl.when(i == 0)
        def _():
            small_ref[...] = jnp.zeros_like(small_ref)

        small_ref[0:1, :] += jnp.sum(dh * n, axis=0, keepdims=True)

    def row(width):
        return pl.BlockSpec((TM, width), lambda i: (i, 0))

    return pl.pallas_call(
        body, name="in_bwd", grid=(s // TM,),
        in_specs=[row(D_MODEL), pl.BlockSpec((1, D_MODEL), lambda i: (0, 0)), row(D_MODEL)]
        + [row(wd) for wd in widths] + [pl.BlockSpec(w.shape, lambda i: (0, 0))],
        out_specs=(row(D_MODEL), pl.BlockSpec((8, D_MODEL), lambda i: (0, 0))),
        out_shape=(jax.ShapeDtypeStruct((s, D_MODEL), F32), jax.ShapeDtypeStruct((8, D_MODEL), F32)),
        compiler_params=_params(("arbitrary",), 48),
    )(x, g, dx1, *pieces, w)


def _tn_matmul(a, b, name, blocked=False):
    s, k = a.shape
    n = b.shape[1]
    ts = 512
    tn = LANES if blocked else min(n, 512)
    steps = s // ts

    def body(a_ref, b_ref, o_ref):
        t = pl.program_id(1)

        @pl.when(t == 0)
        def _():
            o_ref[...] = jnp.zeros_like(o_ref)

        o_ref[...] += _dot_tn(a_ref[...], b_ref[...])

    if blocked:
        out_spec = pl.BlockSpec((None, k, tn), lambda j, t: (j, 0, 0))
        out_shape = jax.ShapeDtypeStruct((n // tn, k, tn), F32)
    else:
        out_spec = pl.BlockSpec((k, tn), lambda j, t: (0, j))
        out_shape = jax.ShapeDtypeStruct((k, n), F32)
    return pl.pallas_call(
        body, name=name, grid=(n // tn, steps),
        in_specs=[pl.BlockSpec((ts, k), lambda j, t: (t, 0)), pl.BlockSpec((ts, tn), lambda j, t: (t, j))],
        out_specs=out_spec, out_shape=out_shape,
        compiler_params=_params(("parallel", "arbitrary"), 40),
    )(a, b)


IN_SHARD = 372
_IN_KERNEL_ORDER = ((0, 2048), (2464, 2976), (2048, 2432))
_IN_ROPE = (2432, 2464)
_IN_GRAD_SRC = ((0, 512, 0, 0), (512, 1024, 1, 0), (1024, 1536, 2, 0), (1536, 2048, 3, 0),
                (2048, 2304, 5, 0), (2304, 2432, 6, 0), (2432, 2464, 7, 64), (2464, 2976, 4, 0))


def _shard_cols(gath_in, lo, hi):
    out = []
    while lo < hi:
        j, a = divmod(lo, IN_SHARD)
        b = min(IN_SHARD, a + hi - lo)
        out.append(gath_in[j][:, a:b])
        lo += b - a
    return out


def _kernel_weights(gath):
    g_in, g_uq, g_ukv, g_out, g_ple, g_pg = gath
    zc = lambda n: jnp.zeros((D_MODEL, n), BF16)
    parts = [pc for lo, hi in _IN_KERNEL_ORDER for pc in _shard_cols(g_in, lo, hi)]
    parts += [zc(64)] + _shard_cols(g_in, *_IN_ROPE) + [zc(32)]
    w_in_p = jnp.concatenate(parts, axis=1)
    w_uq_p = jnp.pad(g_uq, ((0, 0), (0, 0), (0, 32))).transpose(1, 0, 2).reshape(Q_LORA, 1024)
    k_only = jnp.where(jnp.arange(LANES) < HEAD_DIM, g_ukv, jnp.zeros_like(g_ukv))
    w_uk_p = k_only.transpose(1, 0, 2).reshape(KV_LORA, 1024)
    w_uv = g_ukv[:, :, HEAD_DIM:].transpose(1, 0, 2).reshape(KV_LORA, D_GRP)
    w_ple = g_ple.transpose(1, 0, 2).reshape(PLE_DIM, D_MODEL)
    return (w_in_p, w_uq_p, w_uk_p, w_uv, g_out.reshape(D_MODEL, D_MODEL), w_ple,
            g_pg.reshape(D_MODEL, D_MODEL))


def _grad_payloads(d_cols, duq_blk, duk_blk, d_uv, d_out, dple_blk, d_pg):
    blocks = []
    for j in range(N_DEV):
        lo, hi = j * IN_SHARD, (j + 1) * IN_SHARD
        parts = []
        for o_lo, o_hi, idx, off in _IN_GRAD_SRC:
            a, b = max(lo, o_lo), min(hi, o_hi)
            if a < b:
                parts.append(d_cols[idx][:, off + a - o_lo:off + b - o_lo])
        blocks.append(jnp.concatenate(parts, axis=1))
    pay_in = jnp.stack(blocks)
    dv_blk = d_uv.reshape(KV_LORA, N_HEADS, HEAD_DIM).transpose(1, 0, 2)
    pay_ukv = jnp.concatenate([duk_blk[:, :, :HEAD_DIM], dv_blk], axis=2)
    return [pay_in, duq_blk, pay_ukv, d_out.reshape(N_DEV, 128, D_MODEL), dple_blk,
            d_pg.reshape(N_DEV, 128, D_MODEL)]


_VEC_SIZES = (D_MODEL, Q_LORA, KV_LORA, D_GRP, D_GRP, D_MODEL, D_MODEL, D_MODEL)


def _pack_vectors(vs, tail=None):
    flat = jnp.concatenate([v.reshape(-1) for v in vs])
    if tail is None:
        tail = jnp.zeros((D_MODEL,), F32)
    flat = jnp.concatenate([flat, tail.reshape(-1)])
    flat = jnp.pad(flat, (0, R_SMALL * LANES - flat.shape[0]))
    return flat.reshape(R_SMALL, LANES)


def _unpack_vectors(packed):
    flat = packed.reshape(-1)
    out, o = [], 0
    for n in _VEC_SIZES:
        out.append(flat[o:o + n].reshape(1, n))
        o += n
    return out, flat[o:o + D_MODEL]


def kernel(x, p, positions, norm_pre_g, w_in, q_norm_g, w_uq, kv_norm_g, w_ukv, sb_out_norm_g, mla_out_norm_g, w_out, norm_post_g, w_ple, ple_norm_g, w_ple_gate, b_ple_gate, loss_target, m_norm_pre_g, m_w_in, m_q_norm_g, m_w_uq, m_kv_norm_g, m_w_ukv, m_sb_out_norm_g, m_mla_out_norm_g, m_w_out, m_norm_post_g, m_w_ple, m_ple_norm_g, m_w_ple_gate, m_b_ple_gate, v_norm_pre_g, v_w_in, v_q_norm_g, v_w_uq, v_kv_norm_g, v_w_ukv, v_sb_out_norm_g, v_mla_out_norm_g, v_w_out, v_norm_post_g, v_w_ple, v_ple_norm_g, v_w_ple_gate, v_b_ple_gate):
    mats = (w_in, w_uq, w_ukv, w_out, w_ple, w_ple_gate)
    m_mats = (m_w_in, m_w_uq, m_w_ukv, m_w_out, m_w_ple, m_w_ple_gate)
    v_mats = (v_w_in, v_w_uq, v_w_ukv, v_w_out, v_w_ple, v_w_ple_gate)
    vecs = (norm_pre_g, q_norm_g, kv_norm_g, sb_out_norm_g, mla_out_norm_g, norm_post_g, ple_norm_g, b_ple_gate)
    m_vecs = (m_norm_pre_g, m_q_norm_g, m_kv_norm_g, m_sb_out_norm_g, m_mla_out_norm_g, m_norm_post_g,
              m_ple_norm_g, m_b_ple_gate)
    v_vecs = (v_norm_pre_g, v_q_norm_g, v_kv_norm_g, v_sb_out_norm_g, v_mla_out_norm_g, v_norm_post_g,
              v_ple_norm_g, v_b_ple_gate)

    gath = _all_gather([a[0].astype(BF16) for a in mats])
    grad_x, d_parts, vec_partials, loss_vec = _local_grads(
        x[0], p[0, 0], positions[0], loss_target[0], *vecs, *_kernel_weights(gath))
    pays = _grad_payloads(*d_parts)
    landed1 = _pair_exchange(pays, _pack_vectors(vec_partials, tail=loss_vec))
    place = jnp.stack([lax.axis_index("c"), 2 * lax.axis_index("x") + lax.axis_index("y")]).astype(jnp.int32)
    pair = [_pair_sum(g, l, place, "grad_pair_sum_%d" % o) for o, (g, l) in enumerate(zip(pays, landed1[:-1]))]
    landed2 = _chip_exchange([s for s, _ in pair])
    pad_uq = lambda a: jnp.pad(a[0], ((0, 0), (0, LANES - 96))) if a.shape[-1] == 96 else a[0]
    upd = [_adamw_matrix(own, l2, pad_uq(w), pad_uq(m), pad_uq(v), "adamw_%d" % o)
           for o, ((_, own), l2, w, m, v) in enumerate(zip(pair, landed2, mats, m_mats, v_mats))]
    sm = _adamw_vectors(landed1[-1], _pack_vectors(vecs), _pack_vectors(m_vecs), _pack_vectors(v_vecs))

    loss = jnp.sum(_unpack_vectors(sm[0])[1])
    outs = []
    for kind in range(4):
        mat = [upd[o][kind][:, :w.shape[-1]][None] for o, w in enumerate(mats)]
        vec = _unpack_vectors(sm[kind])[0]
        outs += [vec[0], mat[0], vec[1], mat[1], vec[2], mat[2], vec[3], vec[4], mat[3], vec[5],
                 mat[4], vec[6], mat[5], vec[7]]
    return (loss, grad_x[None], *outs)


def _local_grads(xs, ps, pos, tgt, norm_pre_g, q_norm_g, kv_norm_g, sb_out_norm_g, mla_out_norm_g,
                 norm_post_g, ple_norm_g, b_ple_gate, w_in_p, w_uq_p, w_uk_p, w_uv, f_out, f_ple, f_pg):
    s = xs.shape[0]

    half = ROPE_DIM // 2
    freq = ROPE_THETA ** (-jnp.arange(half, dtype=F32) / half)
    ang = pos.astype(F32)[:, None] * freq
    cos, sin = jnp.cos(ang), jnp.sin(ang)
    cos_t = jnp.concatenate([jnp.ones((s, 64), F32), cos, cos, jnp.zeros((s, 32), F32)], axis=1)
    sin_t = jnp.concatenate([jnp.zeros((s, 64), F32), -sin, sin, jnp.zeros((s, 32), F32)], axis=1)
    seg = jnp.arange(D_GRP) // HEAD_DIM
    bd = (seg[:, None] == seg[None, :]).astype(BF16)

    qkv, rest, h_b = _in_proj(xs, norm_pre_g, w_in_p)
    sb_o = _sb_fwd(qkv)
    qp, kp, vv, cqn_b, ckvn_b = _mla_prep(rest, q_norm_g, kv_norm_g, w_uq_p, w_uk_p, w_uv, cos_t, sin_t)
    mla_o, lse = _mla_fwd(qp, kp, vv, 8)

    (dx1, d_sbo, d_mlo, d_sbg, d_mlg, x1_b, dgl_b, yc_b, dy_b, p_b, du_b, small_mid) = _mid(
        xs, ps, tgt, sb_o, mla_o, rest, sb_out_norm_g, mla_out_norm_g, f_out, norm_post_g,
        f_ple, ple_norm_g, f_pg, b_ple_gate, bd)
    dqp, dkp, dvv = _mla_bwd(qp, kp, vv, d_mlo, mla_o, lse, 4)
    dq_sb, dk_sb, dv_sb = _sb_bwd(qkv, d_sbo)
    dcq, dckv, dkr, dq_b, dk_b, dv_b, small_prep = _mla_prep_bwd(
        dqp, dkp, dvv, rest, q_norm_g, kv_norm_g, w_uq_p, w_uk_p, w_uv, cos_t, sin_t)
    pieces = [dq_sb, dk_sb, dv_sb, d_sbg, d_mlg, dcq, dckv, dkr]
    grad_x, small_in = _in_bwd(xs, norm_pre_g, dx1, pieces, w_in_p)

    d_cols = [_tn_matmul(h_b, pc, "dw_in_%d" % k) for k, pc in enumerate(pieces)]
    d_parts = (d_cols, _tn_matmul(cqn_b, dq_b, "dw_uq", blocked=True),
               _tn_matmul(ckvn_b, dk_b, "dw_uk", blocked=True), _tn_matmul(ckvn_b, dv_b, "dw_uv"),
               _tn_matmul(yc_b, dy_b, "dw_out"), _tn_matmul(p_b, du_b, "dw_ple", blocked=True),
               _tn_matmul(x1_b, dgl_b, "dw_pg"))
    vec_partials = [small_in[0], small_prep[0], small_prep[1, :KV_LORA], small_mid[3, :D_GRP],
                    small_mid[3, D_GRP:], small_mid[2], small_mid[1], small_mid[0]]
    return grad_x, d_parts, vec_partials, small_mid[4]
```

```python
import jax
import jax.numpy as jnp
from jax import lax
from jax.experimental import pallas as pl
from jax.experimental.pallas import tpu as pltpu

F32 = jnp.float32
BF16 = jnp.bfloat16
MESH = pl.DeviceIdType.MESH

N_DEV = 8
D_MODEL = 1024
N_HEADS = 8
HEAD_DIM = 64
D_GRP = N_HEADS * HEAD_DIM
Q_LORA = 256
KV_LORA = 128
ROPE_DIM = 32
PLE_DIM = 256
CHUNK_SHIFT = 6
ROPE_THETA = 10000.0
EPS = 1e-6
SB_SCALE = HEAD_DIM ** -0.5
MLA_SCALE = (HEAD_DIM + ROPE_DIM) ** -0.5
NEG = -1e30
LOG2_E = 1.4426950408889634
LN_2 = 0.6931471805599453
SB_CUTOFF = 110.0

ADAM_LR = 0.001
ADAM_B1 = 0.9
ADAM_B2 = 0.999
ADAM_EPS = 1e-08
ADAM_WD = 0.01
ADAM_STEP = 10

LANES = 128
TQ = 256
TK = 256
TM = 256

R_SMALL = 56
D_IN_P = 3072

_NT = (((1,), (1,)), ((), ()))
_TN = (((0,), (0,)), ((), ()))


def _params(sem, vmem_mb):
    return pltpu.CompilerParams(dimension_semantics=sem, vmem_limit_bytes=vmem_mb << 20)


def _dot(a, b):
    return jnp.dot(a, b, preferred_element_type=F32)


def _dot_nt(a, b):
    return lax.dot_general(a, b, _NT, preferred_element_type=F32)


def _dot_tn(a, b):
    return lax.dot_general(a, b, _TN, preferred_element_type=F32)


def _hl_dot(a, b):
    hi = a.astype(BF16)
    lo = (a - hi.astype(F32)).astype(BF16)
    return _dot(hi, b) + _dot(lo, b)


def _sigmoid(x):
    return 1.0 / (1.0 + jnp.exp(-x))


def _rope_swap(x, lane):
    left = pltpu.roll(x, LANES - 16, axis=1)
    right = pltpu.roll(x, 16, axis=1)
    lo = (lane >= 64) & (lane < 80)
    hi = (lane >= 80) & (lane < 96)
    return jnp.where(lo, left, jnp.where(hi, right, 0.0))


def _all_gather(shards):
    n_op = len(shards)

    def body(*refs):
        x_refs, out_refs = refs[:n_op], refs[n_op:2 * n_op]
        send_sems, recv_sems, local_sems = refs[2 * n_op:]
        x, y, c = lax.axis_index("x"), lax.axis_index("y"), lax.axis_index("c")
        me, sibling = (x, y, c), (x, y, 1 - c)
        chips = [(1 - x, y), (x, 1 - y), (1 - x, 1 - y)]

        def slot(o, px, py, pc):
            return out_refs[o].at[4 * px + 2 * py + pc]

        def copy(o, k, block, to, src=None):
            return pltpu.make_async_remote_copy(
                src_ref=slot(o, *block) if src is None else src, dst_ref=slot(o, *block),
                send_sem=send_sems.at[o, k], recv_sem=recv_sems.at[o, k],
                device_id=to, device_id_type=MESH)

        ops = range(n_op)
        mine = [pltpu.make_async_copy(x_refs[o], slot(o, *me), local_sems.at[o]) for o in ops]
        first = [copy(o, 0, me, sibling, src=x_refs[o]) for o in ops]
        first += [copy(o, 1 + j, me, (*chip, c), src=x_refs[o]) for j, chip in enumerate(chips) for o in ops]
        for cp in mine + first:
            cp.start()
        passed = []
        for j, chip in enumerate(chips):
            for o in ops:
                copy(o, 1 + j, (*chip, c), me).wait_recv()
                passed.append(copy(o, 4 + j, (*chip, c), sibling))
                passed[-1].start()
        for o in ops:
            copy(o, 0, sibling, me).wait_recv()
        for j, chip in enumerate(chips):
            for o in ops:
                copy(o, 4 + j, (*chip, 1 - c), me).wait_recv()
        for cp in first + passed:
            cp.wait_send()
        for cp in mine:
            cp.wait()

    vmem = pl.BlockSpec(memory_space=pltpu.VMEM)
    return pl.pallas_call(
        body, name="weight_all_gather",
        out_shape=[jax.ShapeDtypeStruct((N_DEV,) + a.shape, a.dtype) for a in shards],
        in_specs=[vmem] * n_op, out_specs=[vmem] * n_op,
        scratch_shapes=[pltpu.SemaphoreType.DMA((n_op, 7)), pltpu.SemaphoreType.DMA((n_op, 7)),
                        pltpu.SemaphoreType.DMA((n_op,))],
        compiler_params=pltpu.CompilerParams(vmem_limit_bytes=48 << 20),
    )(*shards)


def _pair_exchange(pays, small):
    n_op = len(pays)
    sr, n = small.shape

    def body(*refs):
        g_refs, s_ref = refs[:n_op], refs[n_op]
        l_refs, sland_ref = refs[n_op + 1:2 * n_op + 1], refs[2 * n_op + 1]
        ssem, rsem, ssem2, rsem2, lsem = refs[2 * n_op + 2:]
        x, y, c = lax.axis_index("x"), lax.axis_index("y"), lax.axis_index("c")
        me = 4 * x + 2 * y + c
        copies = []
        for o in range(n_op):
            for chip in range(4):
                copies.append(pltpu.make_async_remote_copy(
                    src_ref=g_refs[o].at[2 * chip + (1 - c)], dst_ref=l_refs[o].at[chip],
                    send_sem=ssem.at[o, chip], recv_sem=rsem.at[o, chip],
                    device_id=(x, y, 1 - c), device_id_type=MESH))
        for k in range(1, N_DEV):
            peer = (1 - x if (k >> 2) & 1 else x, 1 - y if (k >> 1) & 1 else y, 1 - c if k & 1 else c)
            copies.append(pltpu.make_async_remote_copy(
                src_ref=s_ref, dst_ref=sland_ref.at[me], send_sem=ssem2.at[k], recv_sem=rsem2.at[k],
                device_id=peer, device_id_type=MESH))
        own = pltpu.make_async_copy(s_ref, sland_ref.at[me], lsem)
        own.start()
        for cp in copies:
            cp.start()
        for cp in copies:
            cp.wait()
        own.wait()

    any_spec = pl.BlockSpec(memory_space=pl.ANY)
    return pl.pallas_call(
        body, name="grad_pair_exchange",
        out_shape=[jax.ShapeDtypeStruct((4,) + a.shape[1:], F32) for a in pays]
        + [jax.ShapeDtypeStruct((N_DEV, sr, n), F32)],
        in_specs=[any_spec] * (n_op + 1), out_specs=[any_spec] * (n_op + 1),
        scratch_shapes=[pltpu.SemaphoreType.DMA((n_op, 4)), pltpu.SemaphoreType.DMA((n_op, 4)),
                        pltpu.SemaphoreType.DMA((N_DEV,)), pltpu.SemaphoreType.DMA((N_DEV,)),
                        pltpu.SemaphoreType.DMA],
    )(*pays, small)


def _pair_sum(pay, landed, place, name):
    _, r, c = pay.shape

    def body(place_ref, g_ref, l_ref, s_ref, own_ref):
        i = pl.program_id(0)
        tot = g_ref[...] + l_ref[...]
        s_ref[...] = tot.astype(BF16)

        @pl.when(i == place_ref[1])
        def _():
            own_ref[...] = tot

    grid_spec = pltpu.PrefetchScalarGridSpec(
        num_scalar_prefetch=1, grid=(4,),
        in_specs=[pl.BlockSpec((None, r, c), lambda i, pr: (2 * i + pr[0], 0, 0)),
                  pl.BlockSpec((None, r, c), lambda i, pr: (i, 0, 0))],
        out_specs=[pl.BlockSpec((None, r, c), lambda i, pr: (i, 0, 0)),
                   pl.BlockSpec((r, c), lambda i, pr: (0, 0))])
    return pl.pallas_call(
        body, name=name, grid_spec=grid_spec,
        out_shape=[jax.ShapeDtypeStruct((4, r, c), BF16), jax.ShapeDtypeStruct((r, c), F32)],
        compiler_params=_params(("arbitrary",), 40),
    )(place, pay, landed)


def _chip_exchange(sums):
    n_op = len(sums)

    def body(*refs):
        s_refs, l_refs = refs[:n_op], refs[n_op:2 * n_op]
        ssem, rsem = refs[2 * n_op:]
        x, y, c = lax.axis_index("x"), lax.axis_index("y"), lax.axis_index("c")
        copies = []
        for rel in range(1, 4):
            px = 1 - x if rel & 2 else x
            py = 1 - y if rel & 1 else y
            for o in range(n_op):
                copies.append(pltpu.make_async_remote_copy(
                    src_ref=s_refs[o].at[2 * px + py], dst_ref=l_refs[o].at[rel - 1],
                    send_sem=ssem.at[o, rel - 1], recv_sem=rsem.at[o, rel - 1],
                    device_id=(px, py, c), device_id_type=MESH))
        for cp in copies:
            cp.start()
        for cp in copies:
            cp.wait()

    any_spec = pl.BlockSpec(memory_space=pl.ANY)
    return pl.pallas_call(
        body, name="grad_chip_exchange",
        out_shape=[jax.ShapeDtypeStruct((3,) + a.shape[1:], BF16) for a in sums],
        in_specs=[any_spec] * n_op, out_specs=[any_spec] * n_op,
        scratch_shapes=[pltpu.SemaphoreType.DMA((n_op, 3)), pltpu.SemaphoreType.DMA((n_op, 3))],
    )(*sums)


def _adamw_math(g, w, m, v):
    mn = ADAM_B1 * m + (1.0 - ADAM_B1) * g
    vn = ADAM_B2 * v + (1.0 - ADAM_B2) * (g * g)
    m_hat = mn / (1.0 - ADAM_B1 ** ADAM_STEP)
    v_hat = vn / (1.0 - ADAM_B2 ** ADAM_STEP)
    return -ADAM_LR * (m_hat / (jnp.sqrt(v_hat) + ADAM_EPS) + ADAM_WD * w), mn, vn


def _adamw_matrix(own, landed, w, m, v, name):
    r, c = own.shape
    br = min(r, 256)

    def body(own_ref, l_ref, w_ref, m_ref, v_ref, g_out, d_out, m_out, v_out):
        g = own_ref[...]
        for k in range(3):
            g = g + l_ref[k].astype(F32)
        g_out[...] = g
        d_out[...], m_out[...], v_out[...] = _adamw_math(g, w_ref[...], m_ref[...], v_ref[...])

    row = pl.BlockSpec((br, c), lambda i: (i, 0))
    shp = jax.ShapeDtypeStruct((r, c), F32)
    return pl.pallas_call(
        body, name=name, grid=(r // br,),
        in_specs=[row, pl.BlockSpec((3, br, c), lambda i: (0, i, 0)), row, row, row],
        out_specs=(row, row, row, row), out_shape=(shp, shp, shp, shp),
        compiler_params=_params(("parallel",), 40),
    )(own, landed, w, m, v)


def _adamw_vectors(sland, w, m, v):
    _, r, n = sland.shape

    def body(l_ref, w_ref, m_ref, v_ref, g_out, d_out, m_out, v_out):
        g = l_ref[0]
        for j in range(1, N_DEV):
            g = g + l_ref[j]
        g_out[...] = g
        d_out[...], m_out[...], v_out[...] = _adamw_math(g, w_ref[...], m_ref[...], v_ref[...])

    shp = jax.ShapeDtypeStruct((r, n), F32)
    vmem = pl.BlockSpec(memory_space=pltpu.VMEM)
    return pl.pallas_call(
        body, name="adamw_vectors", in_specs=[vmem] * 4, out_specs=(vmem,) * 4,
        out_shape=(shp, shp, shp, shp),
    )(sland, w, m, v)


def _in_proj(x, g, w):
    s = x.shape[0]

    def body(x_ref, g_ref, w_ref, qkv_ref, rest_ref, h_ref):
        xv = x_ref[...]
        r = lax.rsqrt(jnp.mean(xv * xv, axis=-1, keepdims=True) + EPS)
        h = ((xv * r) * g_ref[...]).astype(BF16)
        h_ref[...] = h
        qkv_ref[...] = _dot(h, w_ref[:, :1536]).astype(BF16)
        rest_ref[...] = _dot(h, w_ref[:, 1536:])

    return pl.pallas_call(
        body, name="in_proj", grid=(s // TM,),
        in_specs=[pl.BlockSpec((TM, D_MODEL), lambda i: (i, 0)),
                  pl.BlockSpec((1, D_MODEL), lambda i: (0, 0)),
                  pl.BlockSpec((D_MODEL, D_IN_P), lambda i: (0, 0))],
        out_specs=(pl.BlockSpec((TM, 1536), lambda i: (i, 0)),
                   pl.BlockSpec((TM, 1536), lambda i: (i, 0)),
                   pl.BlockSpec((TM, D_MODEL), lambda i: (i, 0))),
        out_shape=(jax.ShapeDtypeStruct((s, 1536), BF16), jax.ShapeDtypeStruct((s, 1536), F32),
                   jax.ShapeDtypeStruct((s, D_MODEL), BF16)),
        compiler_params=_params(("parallel",), 48),
    )(x, g, w)


def _mla_prep(rest, gq, gkv, wuq, wuk, wuv, cos_t, sin_t):
    s = rest.shape[0]

    def body(cq_ref, ckv_ref, kr_ref, gq_ref, gkv_ref, wuq_ref, wuk_ref, wuv_ref, c_ref, s_ref,
             qp_ref, kp_ref, vv_ref, cqn_ref, ckvn_ref):
        lane = lax.broadcasted_iota(jnp.int32, (1, LANES), 1)
        cos_v, sin_v = c_ref[...], s_ref[...]
        cq = cq_ref[...]
        rq = lax.rsqrt(jnp.mean(cq * cq, axis=-1, keepdims=True) + EPS)
        cqn = ((cq * rq) * gq_ref[...]).astype(BF16)
        cqn_ref[...] = cqn
        q = _dot(cqn, wuq_ref[...])
        ckv = ckv_ref[...]
        rkv = lax.rsqrt(jnp.mean(ckv * ckv, axis=-1, keepdims=True) + EPS)
        ckvn = ((ckv * rkv) * gkv_ref[...]).astype(BF16)
        ckvn_ref[...] = ckvn
        kn = _dot(ckvn, wuk_ref[...])
        vv_ref[...] = _dot(ckvn, wuv_ref[...]).astype(BF16)
        kr = kr_ref[...]
        kr_roped = kr * cos_v + _rope_swap(kr, lane) * sin_v
        for h in range(N_HEADS):
            sl = slice(h * LANES, (h + 1) * LANES)
            qh = q[:, sl]
            qp_ref[:, sl] = (qh * cos_v + _rope_swap(qh, lane) * sin_v).astype(BF16)
            kp_ref[:, sl] = (kn[:, sl] + kr_roped).astype(BF16)

    def row(width, idx):
        return pl.BlockSpec((TM, width), lambda i: (i, idx))

    def full(a):
        return pl.BlockSpec(a.shape, lambda i: (0, 0))

    return pl.pallas_call(
        body, name="mla_prep", grid=(s // TM,),
        in_specs=[row(Q_LORA, 4), row(KV_LORA, 10), row(LANES, 11), full(gq), full(gkv),
                  full(wuq), full(wuk), full(wuv), row(LANES, 0), row(LANES, 0)],
        out_specs=(row(1024, 0), row(1024, 0), row(D_GRP, 0), row(Q_LORA, 0), row(KV_LORA, 0)),
        out_shape=(jax.ShapeDtypeStruct((s, 1024), BF16), jax.ShapeDtypeStruct((s, 1024), BF16),
                   jax.ShapeDtypeStruct((s, D_GRP), BF16), jax.ShapeDtypeStruct((s, Q_LORA), BF16),
                   jax.ShapeDtypeStruct((s, KV_LORA), BF16)),
        compiler_params=_params(("parallel",), 32),
    )(rest, rest, rest, gq, gkv, wuq, wuk, wuv, cos_t, sin_t)


def _sb_live(n, qi, carries):
    top = carries[0]
    for c in carries[1:]:
        top = jnp.maximum(top, c)
    return jnp.logical_and(n < qi, jnp.max(top) > -SB_CUTOFF)


def _sb_fwd(qkv, hb):
    s = qkv.shape[0]

    def body(q_ref, k_ref, v_ref, o_ref, acc):
        qi = pl.program_id(1)
        lane = lax.broadcasted_iota(jnp.int32, (1, LANES), 1)
        is_a = lane < HEAD_DIM
        pair = lambda h: slice((h // 2) * LANES, (h // 2 + 1) * LANES)
        q_h = []
        for h in range(hb):
            qs = q_ref[:, pair(h)] * SB_SCALE
            mine = is_a if h % 2 == 0 else jnp.logical_not(is_a)
            q_h.append(jnp.where(mine, qs, jnp.zeros_like(qs)))
        r_i = lax.broadcasted_iota(jnp.int32, (TQ, TK), 0)
        c_i = lax.broadcasted_iota(jnp.int32, (TQ, TK), 1)
        past = c_i < r_i
        upper = (r_i > c_i).astype(BF16)
        acc[...] = jnp.zeros_like(acc)

        def tile(j, carries, diag):
            ks = pl.ds(pl.multiple_of(j * TK, TK), TK)
            zs = [_dot_nt(q_h[h], k_ref[ks, pair(h)]) for h in range(hb)]
            if diag:
                zs = [jnp.where(past, z, NEG) for z in zs]
            lfs = [-(jnp.maximum(z, 0.0) + jnp.log(1.0 + jnp.exp(-jnp.abs(z)))) for z in zs]
            sufs = [_hl_dot(lfs[h], upper) for h in range(hb)]
            out = []
            for h in range(hb):
                w = jnp.exp(zs[h] + lfs[h] + (sufs[h] + carries[h]))
                acc[h] += _dot(w.astype(BF16), v_ref[ks, pair(h)])
                out.append(carries[h] + jnp.sum(lfs[h], axis=1, keepdims=True))
            return tuple(out)

        zero = jnp.zeros((TQ, 1), F32)
        carries = tile(qi, (zero,) * hb, True)

        def step(st):
            return (st[0] + 1,) + tile(qi - 1 - st[0], st[1:], False)

        lax.while_loop(lambda st: _sb_live(st[0], qi, st[1:]), step, (0,) + carries)
        for pr in range(hb // 2):
            o_ref[:, pr * LANES:(pr + 1) * LANES] = jnp.where(is_a, acc[2 * pr], acc[2 * pr + 1])

    width = hb * HEAD_DIM
    nb = D_GRP // width
    slab = lambda part: pl.BlockSpec((s, width), lambda g, qi: (0, part * nb + g))
    blk = pl.BlockSpec((TQ, width), lambda g, qi: (qi, g))
    return pl.pallas_call(
        body, name="sb_fwd", grid=(nb, s // TQ),
        in_specs=[blk, slab(1), slab(2)], out_specs=blk,
        out_shape=jax.ShapeDtypeStruct((s, D_GRP), F32),
        scratch_shapes=[pltpu.VMEM((hb, TQ, LANES), F32)],
        compiler_params=_params(("arbitrary", "arbitrary"), 48),
    )(qkv, qkv, qkv)


def _sb_bwd(qkv, d_o):
    s = qkv.shape[0]
    nq = s // TQ
    nk = s // TK

    def body(q_ref, k_ref, v_ref, do_ref, dq_ref, dk_ref, dv_ref, x1s, bts, dqacc, dkacc, dvacc):
        qi = pl.program_id(1)
        lane = lax.broadcasted_iota(jnp.int32, (1, LANES), 1)
        is_a = lane < HEAD_DIM

        @pl.when(qi == 0)
        def _():
            dkacc[...] = jnp.zeros_like(dkacc)
            dvacc[...] = jnp.zeros_like(dvacc)

        qs = q_ref[...] * SB_SCALE
        zq = jnp.zeros_like(qs)
        qs_x = (jnp.where(is_a, qs, zq), jnp.where(is_a, zq, qs))
        dob = do_ref[...].astype(BF16)
        do_x = (jnp.where(is_a, dob, zq), jnp.where(is_a, zq, dob))
        r_i = lax.broadcasted_iota(jnp.int32, (TQ, TK), 0)
        c_i = lax.broadcasted_iota(jnp.int32, (TQ, TK), 1)
        past = c_i < r_i
        upper = (r_i > c_i).astype(BF16)
        upper_incl = (r_i >= c_i).astype(BF16)
        dqacc[...] = jnp.zeros_like(dqacc)
        both = ((0, 0), (0, 1), (1, 0), (1, 1))

        def tiles(n):
            j_hi = qi - 2 * n
            lo_ok = j_hi >= 1
            j_lo = jnp.maximum(j_hi - 1, 0)
            ks = (pl.ds(pl.multiple_of(j_hi * TK, TK), TK), pl.ds(pl.multiple_of(j_lo * TK, TK), TK))
            return j_hi, lo_ok, j_lo, ks

        def sweep(n, carries):
            j_hi, lo_ok, j_lo, ks = tiles(n)
            slot = (j_hi, jnp.where(lo_ok, j_lo, nk))
            valid = (jnp.logical_or(past, j_hi < qi), lo_ok)
            z = {th: jnp.where(valid[th[0]], _dot_nt(qs_x[th[1]], k_ref[ks[th[0]], :]), NEG) for th in both}
            d_a = {th: _dot_nt(do_x[th[1]], v_ref[ks[th[0]], :]) for th in both}
            lf, beta, omb = {}, {}, {}
            for th in both:
                e = jnp.exp(-jnp.abs(z[th]))
                den = 1.0 + e
                rden = 1.0 / den
                pos = z[th] >= 0.0
                lf[th] = -(jnp.maximum(z[th], 0.0) + jnp.log(den))
                beta[th] = jnp.where(pos, rden, e * rden)
                omb[th] = jnp.where(pos, e * rden, rden)
            suf = {th: _hl_dot(lf[th], upper) for th in both}
            c, g_in = {}, {}
            for h in range(2):
                c[0, h], g_in[0, h] = carries[2 * h], carries[2 * h + 1]
                c[1, h] = c[0, h] + jnp.sum(lf[0, h], axis=1, keepdims=True)
            a, g = {}, {}
            for th in both:
                a[th] = jnp.exp(z[th] + lf[th] + (suf[th] + c[th]))
                g[th] = a[th] * d_a[th]
            sg = {th: _hl_dot(g[th], upper_incl) for th in both}
            for h in range(2):
                g_in[1, h] = g_in[0, h] + jnp.sum(g[0, h], axis=1, keepdims=True)
            for th in both:
                t, h = th
                x1s[slot[t], h] = g[th] * omb[th] + beta[th] * (sg[th] + g_in[th])
                bts[slot[t], h] = beta[th]
                dvacc[ks[t], :] += _dot_tn(a[th].astype(BF16), do_x[h])
            out = []
            for h in range(2):
                out.append(c[1, h] + jnp.sum(lf[1, h], axis=1, keepdims=True))
                out.append(g_in[1, h] + jnp.sum(g[1, h], axis=1, keepdims=True))
            return tuple(out)

        zero = jnp.zeros((TQ, 1), F32)
        first = sweep(0, (zero, zero, zero, zero))

        def more(st):
            return jnp.logical_and(2 * st[0] <= qi, jnp.max(jnp.maximum(st[1], st[3])) > -SB_CUTOFF)

        swept = lax.while_loop(more, lambda st: (st[0] + 1,) + sweep(st[0], st[1:]), (1,) + first)
        g_tot = (swept[2], swept[4])

        def apply(n, carry):
            j_hi, lo_ok, j_lo, ks = tiles(n)

            def one(j, kslice):
                for h in range(2):
                    dz = (x1s[j, h] - bts[j, h] * g_tot[h]).astype(BF16)
                    dqacc[h] += _dot(dz, k_ref[kslice, :])
                    dkacc[kslice, :] += _dot_tn(dz, qs_x[h])

            one(j_hi, ks[0])

            @pl.when(lo_ok)
            def _():
                one(j_lo, ks[1])

            return carry

        lax.fori_loop(0, swept[0], apply, 0)
        dq_ref[...] = (jnp.where(is_a, dqacc[0], dqacc[1]) * SB_SCALE).astype(BF16)

        @pl.when(qi == nq - 1)
        def _():
            dk_ref[...] = dkacc[...].astype(BF16)
            dv_ref[...] = dvacc[...].astype(BF16)

    slab = lambda off: pl.BlockSpec((s, LANES), lambda p, qi: (0, off + p))
    blk = pl.BlockSpec((TQ, LANES), lambda p, qi: (qi, p))
    out_slab = pl.BlockSpec((s, LANES), lambda p, qi: (0, p))
    shp = jax.ShapeDtypeStruct((s, D_GRP), BF16)
    return pl.pallas_call(
        body, name="sb_bwd", grid=(4, nq),
        in_specs=[blk, slab(4), slab(8), blk],
        out_specs=(blk, out_slab, out_slab), out_shape=(shp, shp, shp),
        scratch_shapes=[pltpu.VMEM((nk + 1, 2, TQ, TK), F32)] * 2
        + [pltpu.VMEM((2, TQ, LANES), F32), pltpu.VMEM((s, LANES), F32), pltpu.VMEM((s, LANES), F32)],
        compiler_params=_params(("arbitrary", "arbitrary"), 56),
    )(qkv, qkv, qkv, d_o)


def _mla_fwd(qp, kp, vv, hb):
    s = qp.shape[0]
    c2 = MLA_SCALE * LOG2_E

    def body(q_ref, k_ref, v_ref, o_ref, lse_ref, vaug, mrun, mb, acc):
        qi = pl.program_id(1)
        lane = lax.broadcasted_iota(jnp.int32, (1, LANES), 1)
        is_a = lane < HEAD_DIM

        @pl.when(qi == 0)
        def _():
            for h in range(hb):
                vp = v_ref[:, (h // 2) * LANES:(h // 2 + 1) * LANES]
                mine = is_a if h % 2 == 0 else jnp.logical_not(is_a)
                vaug[h] = jnp.where(mine, vp, jnp.ones_like(vp))

        r_i = lax.broadcasted_iota(jnp.int32, (TQ, TK), 0)
        c_i = lax.broadcasted_iota(jnp.int32, (TQ, TK), 1)
        visible = (c_i >> CHUNK_SHIFT) <= (r_i >> CHUNK_SHIFT)

        def scores(j):
            ks = pl.ds(pl.multiple_of(j * TK, TK), TK)
            return ks, [_dot_nt(q_ref[:, h * LANES:(h + 1) * LANES], k_ref[ks, h * LANES:(h + 1) * LANES])
                        for h in range(hb)]

        def sweep(tile):
            def loop(j, carry):
                tile(j, False)
                return carry

            lax.fori_loop(0, qi, loop, 0)
            tile(qi, True)

        mrun[...] = jnp.full_like(mrun, NEG)

        def tile_max(j, diag):
            _, zs = scores(j)
            for h in range(hb):
                z = jnp.where(visible, zs[h], NEG) if diag else zs[h]
                mrun[h] = jnp.maximum(mrun[h], z)

        sweep(tile_max)
        for h in range(hb):
            m = jnp.max(mrun[h], axis=1, keepdims=True) * c2
            mb[h] = jnp.broadcast_to(m, (TQ, TK))
        acc[...] = jnp.zeros_like(acc)

        def tile_pv(j, diag):
            ks, zs = scores(j)
            for h in range(hb):
                e = zs[h] * c2 - mb[h]
                if diag:
                    e = jnp.where(visible, e, NEG)
                acc[h] += _dot(jnp.exp2(e).astype(BF16), vaug[h, ks, :])

        sweep(tile_pv)
        for pr in range(hb // 2):
            a, b = 2 * pr, 2 * pr + 1
            psl = slice(pr * LANES, (pr + 1) * LANES)
            acc_a, acc_b = acc[a], acc[b]
            l_a = pltpu.roll(acc_a, HEAD_DIM, axis=1)
            l_b = pltpu.roll(acc_b, HEAD_DIM, axis=1)
            o_ref[:, psl] = jnp.where(is_a, acc_a * (1.0 / l_a), acc_b * (1.0 / l_b))
            lse_ref[:, psl] = jnp.where(is_a, mb[a, :, :LANES] * LN_2 + jnp.log(l_a),
                                        mb[b, :, :LANES] * LN_2 + jnp.log(l_b))

    blk = pl.BlockSpec((TQ, hb * HEAD_DIM), lambda g, qi: (qi, g))
    shp = jax.ShapeDtypeStruct((s, D_GRP), F32)
    return pl.pallas_call(
        body, name="mla_fwd", grid=(N_HEADS // hb, s // TQ),
        in_specs=[pl.BlockSpec((TQ, hb * LANES), lambda g, qi: (qi, g)),
                  pl.BlockSpec((s, hb * LANES), lambda g, qi: (0, g)),
                  pl.BlockSpec((s, hb * HEAD_DIM), lambda g, qi: (0, g))],
        out_specs=(blk, blk), out_shape=(shp, shp),
        scratch_shapes=[pltpu.VMEM((hb, s, LANES), BF16), pltpu.VMEM((hb, TQ, TK), F32),
                        pltpu.VMEM((hb, TQ, TK), F32), pltpu.VMEM((hb, TQ, LANES), F32)],
        compiler_params=_params(("arbitrary", "arbitrary"), 56),
    )(qp, kp, vv)


def _mla_bwd(qp, kp, vv, d_o, o, lse, hb):
    s = qp.shape[0]
    c2 = MLA_SCALE * LOG2_E

    def body(q_ref, k_ref, v_ref, do_ref, o_ref, lse_ref, dq_ref, dk_ref, dv_ref, dqacc, lse_b, delta_b):
        qi = pl.program_id(1)
        lane = lax.broadcasted_iota(jnp.int32, (1, LANES), 1)
        is_a = lane < HEAD_DIM

        @pl.when(qi == 0)
        def _():
            dk_ref[...] = jnp.zeros_like(dk_ref)
            dv_ref[...] = jnp.zeros_like(dv_ref)

        r_i = lax.broadcasted_iota(jnp.int32, (TQ, TK), 0)
        c_i = lax.broadcasted_iota(jnp.int32, (TQ, TK), 1)
        visible = (c_i >> CHUNK_SHIFT) <= (r_i >> CHUNK_SHIFT)
        do_x = []
        for h in range(hb):
            psl = slice((h // 2) * LANES, (h // 2 + 1) * LANES)
            mine = is_a if h % 2 == 0 else jnp.logical_not(is_a)
            d_o = do_ref[:, psl]
            delta = jnp.sum(jnp.where(mine, d_o * o_ref[:, psl], 0.0), axis=1, keepdims=True)
            lse_h = jnp.sum(jnp.where(lane == (h % 2) * HEAD_DIM, lse_ref[:, psl], 0.0), axis=1, keepdims=True)
            lse_b[h] = jnp.broadcast_to(lse_h * LOG2_E, (TQ, TK))
            delta_b[h] = jnp.broadcast_to(delta, (TQ, TK))
            do_x.append(jnp.where(mine, d_o, 0.0).astype(BF16))
        dqacc[...] = jnp.zeros_like(dqacc)

        def tile(j, diag):
            ks = pl.ds(pl.multiple_of(j * TK, TK), TK)
            head = lambda h: slice(h * LANES, (h + 1) * LANES)
            pair = lambda h: slice((h // 2) * LANES, (h // 2 + 1) * LANES)
            zs = [_dot_nt(q_ref[:, head(h)], k_ref[ks, head(h)]) for h in range(hb)]
            dps = [_dot_nt(do_x[h], v_ref[ks, pair(h)]) for h in range(hb)]
            for h in range(hb):
                e = zs[h] * c2 - lse_b[h]
                if diag:
                    e = jnp.where(visible, e, NEG)
                p = jnp.exp2(e)
                ds = (p * (dps[h] - delta_b[h]) * MLA_SCALE).astype(BF16)
                dqacc[h] += _dot(ds, k_ref[ks, head(h)])
                dk_ref[ks, head(h)] += _dot_tn(ds, q_ref[:, head(h)])
                dv_ref[ks, pair(h)] += _dot_tn(p.astype(BF16), do_x[h])

        def loop(j, c):
            tile(j, False)
            return c

        lax.fori_loop(0, qi, loop, 0)
        tile(qi, True)
        for h in range(hb):
            dq_ref[:, h * LANES:(h + 1) * LANES] = dqacc[h]

    blk = pl.BlockSpec((TQ, hb * HEAD_DIM), lambda g, qi: (qi, g))
    return pl.pallas_call(
        body, name="mla_bwd", grid=(N_HEADS // hb, s // TQ),
        in_specs=[pl.BlockSpec((TQ, hb * LANES), lambda g, qi: (qi, g)),
                  pl.BlockSpec((s, hb * LANES), lambda g, qi: (0, g)),
                  pl.BlockSpec((s, hb * HEAD_DIM), lambda g, qi: (0, g)), blk, blk, blk],
        out_specs=(pl.BlockSpec((TQ, hb * LANES), lambda g, qi: (qi, g)),
                   pl.BlockSpec((s, hb * LANES), lambda g, qi: (0, g)),
                   pl.BlockSpec((s, hb * HEAD_DIM), lambda g, qi: (0, g))),
        out_shape=(jax.ShapeDtypeStruct((s, 1024), F32), jax.ShapeDtypeStruct((s, 1024), F32),
                   jax.ShapeDtypeStruct((s, D_GRP), F32)),
        scratch_shapes=[pltpu.VMEM((hb, TQ, LANES), F32), pltpu.VMEM((hb, TQ, TK), F32),
                        pltpu.VMEM((hb, TQ, TK), F32)],
        compiler_params=_params(("arbitrary", "arbitrary"), 56),
    )(qp, kp, vv, d_o, o, lse)


def _mid(x, p, target, sb_o, mla_o, rest, g_sb, g_mla, w_out, g_post, w_ple, g_ple, w_pg, b_pg, bd):
    s = x.shape[0]

    def body(x_ref, p_ref, t_ref, sbo_ref, mlo_ref, sbg_ref, mlg_ref, gsb_ref, gml_ref, wout_ref,
             gpost_ref, wple_ref, gple_ref, wpg_ref, bpg_ref, bd_ref,
             dx1_ref, dsbo_ref, dmlo_ref, dsbg_ref, dmlg_ref, x1b_ref, dglb_ref, ycb_ref, dyb_ref,
             pb_ref, dub_ref, small_ref):
        i = pl.program_id(0)
        bd_m = bd_ref[...]

        def seg_mean(v):
            return _hl_dot(v, bd_m) * (1.0 / HEAD_DIM)

        groups = []
        for o_ref, gate_ref, gain_ref in ((sbo_ref, sbg_ref, gsb_ref), (mlo_ref, mlg_ref, gml_ref)):
            o = o_ref[...]
            r = lax.rsqrt(seg_mean(o * o) + EPS)
            n = o * r
            hn = n * gain_ref[...]
            gate = gate_ref[...]
            sg = _sigmoid(gate)
            si = gate * sg
            groups.append((r, n, hn, gate, sg, si, gain_ref[...]))
        ya = (groups[0][2] * groups[0][5]).astype(BF16)
        yb = (groups[1][2] * groups[1][5]).astype(BF16)
        ycb_ref[:, :D_GRP] = ya
        ycb_ref[:, D_GRP:] = yb
        y = _dot(ya, wout_ref[:D_GRP, :]) + _dot(yb, wout_ref[D_GRP:, :])
        ry = lax.rsqrt(jnp.mean(y * y, axis=-1, keepdims=True) + EPS)
        ny = y * ry
        x1 = x_ref[...] + ny * gpost_ref[...]
        x1b = x1.astype(BF16)
        x1b_ref[...] = x1b
        pb = p_ref[...].astype(BF16)
        pb_ref[...] = pb
        u = _dot(pb, wple_ref[...])
        ru = lax.rsqrt(jnp.mean(u * u, axis=-1, keepdims=True) + EPS)
        nu = u * ru
        ple = nu * gple_ref[...]
        gate = _sigmoid(_dot(x1b, wpg_ref[...]) + bpg_ref[...])
        x2 = x1 + ple * gate
        diff = x2 - t_ref[...]
        dx2 = diff * (1.0 / D_MODEL)

        d_ple = dx2 * gate
        d_glin = (dx2 * ple) * (gate * (1.0 - gate))
        dglb = d_glin.astype(BF16)
        dglb_ref[...] = dglb
        dx1 = dx2 + _dot_nt(dglb, wpg_ref[...])
        dx1_ref[...] = dx1
        d_nu = d_ple * gple_ref[...]
        d_u = ru * (d_nu - nu * jnp.mean(d_nu * nu, axis=-1, keepdims=True))
        dub_ref[...] = d_u.astype(BF16)
        d_ny = dx1 * gpost_ref[...]
        d_y = ry * (d_ny - ny * jnp.mean(d_ny * ny, axis=-1, keepdims=True))
        dyb = d_y.astype(BF16)
        dyb_ref[...] = dyb
        d_yc = (_dot_nt(dyb, wout_ref[:D_GRP, :]), _dot_nt(dyb, wout_ref[D_GRP:, :]))

        d_gain = []
        for gx, (do_ref, dg_ref) in enumerate(((dsbo_ref, dsbg_ref), (dmlo_ref, dmlg_ref))):
            r, n, hn, gate_g, sg, si, gain = groups[gx]
            dyg = d_yc[gx]
            d_hn = dyg * si
            dg_ref[...] = (dyg * hn * (sg * (1.0 + gate_g * (1.0 - sg)))).astype(BF16)
            d_gain.append(jnp.sum(d_hn * n, axis=0, keepdims=True))
            d_n = d_hn * gain
            do_ref[...] = r * (d_n - n * seg_mean(d_n * n))

        @pl.when(i == 0)
        def _():
            small_ref[...] = jnp.zeros_like(small_ref)

        small_ref[0:1, :] += jnp.sum(d_glin, axis=0, keepdims=True)
        small_ref[1:2, :] += jnp.sum(d_ple * nu, axis=0, keepdims=True)
        small_ref[2:3, :] += jnp.sum(dx1 * ny, axis=0, keepdims=True)
        small_ref[3:4, :D_GRP] += d_gain[0]
        small_ref[3:4, D_GRP:] += d_gain[1]
        small_ref[4:5, :] += jnp.sum(diff * diff, axis=0, keepdims=True) * (0.5 / D_MODEL)

    def row(width, idx=0):
        return pl.BlockSpec((TM, width), lambda i: (i, idx))

    def full(a):
        return pl.BlockSpec(a.shape, lambda i: (0, 0))

    f32 = lambda w: jax.ShapeDtypeStruct((s, w), F32)
    b16 = lambda w: jax.ShapeDtypeStruct((s, w), BF16)
    return pl.pallas_call(
        body, name="mid", grid=(s // TM,),
        in_specs=[row(D_MODEL), row(PLE_DIM), row(D_MODEL), row(D_GRP), row(D_GRP),
                  row(D_GRP, 0), row(D_GRP, 1), full(g_sb), full(g_mla), full(w_out), full(g_post),
                  full(w_ple), full(g_ple), full(w_pg), full(b_pg), full(bd)],
        out_specs=(row(D_MODEL), row(D_GRP), row(D_GRP), row(D_GRP), row(D_GRP), row(D_MODEL),
                   row(D_MODEL), row(D_MODEL), row(D_MODEL), row(PLE_DIM), row(D_MODEL),
                   pl.BlockSpec((8, D_MODEL), lambda i: (0, 0))),
        out_shape=(f32(D_MODEL), f32(D_GRP), f32(D_GRP), b16(D_GRP), b16(D_GRP), b16(D_MODEL),
                   b16(D_MODEL), b16(D_MODEL), b16(D_MODEL), b16(PLE_DIM), b16(D_MODEL),
                   jax.ShapeDtypeStruct((8, D_MODEL), F32)),
        compiler_params=_params(("arbitrary",), 56),
    )(x, p, target, sb_o, mla_o, rest, rest, g_sb, g_mla, w_out, g_post, w_ple, g_ple, w_pg, b_pg, bd)


def _mla_prep_bwd(dqp, dkp, dvv, rest, gq, gkv, wuq, wuk, wuv, cos_t, sin_t):
    s = rest.shape[0]

    def body(dqp_ref, dkp_ref, dvv_ref, cq_ref, ckv_ref, gq_ref, gkv_ref, wuq_ref, wuk_ref, wuv_ref,
             c_ref, s_ref, dcq_ref, dckv_ref, dkr_ref, dqb_ref, dkb_ref, dvb_ref, small_ref):
        i = pl.program_id(0)
        lane = lax.broadcasted_iota(jnp.int32, (1, LANES), 1)
        in_rope = (lane >= HEAD_DIM) & (lane < HEAD_DIM + ROPE_DIM)
        cos_v, sin_v = c_ref[...], s_ref[...]
        dkr_roped = jnp.zeros((TM, LANES), F32)
        for h in range(N_HEADS):
            sl = slice(h * LANES, (h + 1) * LANES)
            dy = dqp_ref[:, sl]
            dqb_ref[:, sl] = (dy * cos_v + _rope_swap(dy * sin_v, lane)).astype(BF16)
            dkh = dkp_ref[:, sl]
            dkb_ref[:, sl] = dkh.astype(BF16)
            dkr_roped = dkr_roped + jnp.where(in_rope, dkh, 0.0)
        dkr_ref[...] = (dkr_roped * cos_v + _rope_swap(dkr_roped * sin_v, lane)).astype(BF16)
        dvb = dvv_ref[...].astype(BF16)
        dvb_ref[...] = dvb

        cq = cq_ref[...]
        rq = lax.rsqrt(jnp.mean(cq * cq, axis=-1, keepdims=True) + EPS)
        nq_ = cq * rq
        d_cqn = _dot_nt(dqb_ref[...], wuq_ref[...])
        d_n = d_cqn * gq_ref[...]
        dcq_ref[...] = (rq * (d_n - nq_ * jnp.mean(d_n * nq_, axis=-1, keepdims=True))).astype(BF16)

        ckv = ckv_ref[...]
        rkv = lax.rsqrt(jnp.mean(ckv * ckv, axis=-1, keepdims=True) + EPS)
        nkv = ckv * rkv
        d_ckvn = _dot_nt(dkb_ref[...], wuk_ref[...]) + _dot_nt(dvb, wuv_ref[...])
        d_n2 = d_ckvn * gkv_ref[...]
        dckv_ref[...] = (rkv * (d_n2 - nkv * jnp.mean(d_n2 * nkv, axis=-1, keepdims=True))).astype(BF16)

        @pl.when(i == 0)
        def _():
            small_ref[...] = jnp.zeros_like(small_ref)

        small_ref[0:1, :] += jnp.sum(d_cqn * nq_, axis=0, keepdims=True)
        small_ref[1:2, :KV_LORA] += jnp.sum(d_ckvn * nkv, axis=0, keepdims=True)

    def row(width, idx=0):
        return pl.BlockSpec((TM, width), lambda i: (i, idx))

    def full(a):
        return pl.BlockSpec(a.shape, lambda i: (0, 0))

    b16 = lambda w: jax.ShapeDtypeStruct((s, w), BF16)
    return pl.pallas_call(
        body, name="mla_prep_bwd", grid=(s // TM,),
        in_specs=[row(1024), row(1024), row(D_GRP), row(Q_LORA, 4), row(KV_LORA, 10), full(gq), full(gkv),
                  full(wuq), full(wuk), full(wuv), row(LANES), row(LANES)],
        out_specs=(row(Q_LORA), row(KV_LORA), row(LANES), row(1024), row(1024), row(D_GRP),
                   pl.BlockSpec((8, Q_LORA), lambda i: (0, 0))),
        out_shape=(b16(Q_LORA), b16(KV_LORA), b16(LANES), b16(1024), b16(1024), b16(D_GRP),
                   jax.ShapeDtypeStruct((8, Q_LORA), F32)),
        compiler_params=_params(("arbitrary",), 40),
    )(dqp, dkp, dvv, rest, rest, gq, gkv, wuq, wuk, wuv, cos_t, sin_t)


def _in_bwd(x, g, dx1, pieces, w):
    s = x.shape[0]
    widths = [a.shape[1] for a in pieces]
    offs = [sum(widths[:k]) for k in range(len(widths))]

    def body(x_ref, g_ref, dx1_ref, *refs):
        piece_refs = refs[:len(pieces)]
        w_ref, dx_ref, small_ref = refs[len(pieces):]
        i = pl.program_id(0)
        dh = jnp.zeros((TM, D_MODEL), F32)
        for pr, off, wd in zip(piece_refs, offs, widths):
            dh = dh + _dot_nt(pr[...], w_ref[:, off:off + wd])
        xv = x_ref[...]
        r = lax.rsqrt(jnp.mean(xv * xv, axis=-1, keepdims=True) + EPS)
        n = xv * r
        d_n = dh * g_ref[...]
        dx_ref[...] = dx1_ref[...] + r * (d_n - n * jnp.mean(d_n * n, axis=-1, keepdims=True))

        @pl.when(i == 0)
        def _():
            small_ref[...] = jnp.zeros_like(small_ref)

        small_ref[0:1, :] += jnp.sum(dh * n, axis=0, keepdims=True)

    def row(width):
        return pl.BlockSpec((TM, width), lambda i: (i, 0))

    return pl.pallas_call(
        body, name="in_bwd", grid=(s // TM,),
        in_specs=[row(D_MODEL), pl.BlockSpec((1, D_MODEL), lambda i: (0, 0)), row(D_MODEL)]
        + [row(wd) for wd in widths] + [pl.BlockSpec(w.shape, lambda i: (0, 0))],
        out_specs=(row(D_MODEL), pl.BlockSpec((8, D_MODEL), lambda i: (0, 0))),
        out_shape=(jax.ShapeDtypeStruct((s, D_MODEL), F32), jax.ShapeDtypeStruct((8, D_MODEL), F32)),
        compiler_params=_params(("arbitrary",), 48),
    )(x, g, dx1, *pieces, w)


def _tn_matmul(a, b, name, blocked=False):
    s, k = a.shape
    n = b.shape[1]
    ts = 512
    tn = n if blocked else min(n, 512)
    steps = s // ts

    def body(a_ref, b_ref, o_ref):
        t = pl.program_id(1)

        @pl.when(t == 0)
        def _():
            o_ref[...] = jnp.zeros_like(o_ref)

        prod = _dot_tn(a_ref[...], b_ref[...])
        if blocked:
            for j in range(n // LANES):
                o_ref[j] += prod[:, j * LANES:(j + 1) * LANES]
        else:
            o_ref[...] += prod

    if blocked:
        out_spec = pl.BlockSpec((n // LANES, k, LANES), lambda j, t: (0, 0, 0))
        out_shape = jax.ShapeDtypeStruct((n // LANES, k, LANES), F32)
    else:
        out_spec = pl.BlockSpec((k, tn), lambda j, t: (0, j))
        out_shape = jax.ShapeDtypeStruct((k, n), F32)
    return pl.pallas_call(
        body, name=name, grid=(n // tn, steps),
        in_specs=[pl.BlockSpec((ts, k), lambda j, t: (t, 0)), pl.BlockSpec((ts, tn), lambda j, t: (t, j))],
        out_specs=out_spec, out_shape=out_shape,
        compiler_params=_params(("parallel", "arbitrary"), 40),
    )(a, b)


IN_SHARD = 372
_IN_KERNEL_ORDER = ((0, 2048), (2464, 2976), (2048, 2432))
_IN_ROPE = (2432, 2464)
_IN_GRAD_SRC = ((0, 512, 0, 0), (512, 1024, 1, 0), (1024, 1536, 2, 0), (1536, 2048, 3, 0),
                (2048, 2304, 5, 0), (2304, 2432, 6, 0), (2432, 2464, 7, 64), (2464, 2976, 4, 0))


def _shard_cols(gath_in, lo, hi):
    out = []
    while lo < hi:
        j, a = divmod(lo, IN_SHARD)
        b = min(IN_SHARD, a + hi - lo)
        out.append(gath_in[j][:, a:b])
        lo += b - a
    return out


def _kernel_weights(gath):
    g_in, g_uq, g_ukv, g_out, g_ple, g_pg = gath
    zc = lambda n: jnp.zeros((D_MODEL, n), BF16)
    parts = [pc for lo, hi in _IN_KERNEL_ORDER for pc in _shard_cols(g_in, lo, hi)]
    parts += [zc(64)] + _shard_cols(g_in, *_IN_ROPE) + [zc(32)]
    w_in_p = jnp.concatenate(parts, axis=1)
    w_uq_p = jnp.pad(g_uq, ((0, 0), (0, 0), (0, 32))).transpose(1, 0, 2).reshape(Q_LORA, 1024)
    k_only = jnp.where(jnp.arange(LANES) < HEAD_DIM, g_ukv, jnp.zeros_like(g_ukv))
    w_uk_p = k_only.transpose(1, 0, 2).reshape(KV_LORA, 1024)
    w_uv = g_ukv[:, :, HEAD_DIM:].transpose(1, 0, 2).reshape(KV_LORA, D_GRP)
    w_ple = g_ple.transpose(1, 0, 2).reshape(PLE_DIM, D_MODEL)
    return (w_in_p, w_uq_p, w_uk_p, w_uv, g_out.reshape(D_MODEL, D_MODEL), w_ple,
            g_pg.reshape(D_MODEL, D_MODEL))


def _grad_payloads(d_cols, duq_blk, duk_blk, d_uv, d_out, dple_blk, d_pg):
    blocks = []
    for j in range(N_DEV):
        lo, hi = j * IN_SHARD, (j + 1) * IN_SHARD
        parts = []
        for o_lo, o_hi, idx, off in _IN_GRAD_SRC:
            a, b = max(lo, o_lo), min(hi, o_hi)
            if a < b:
                parts.append(d_cols[idx][:, off + a - o_lo:off + b - o_lo])
        blocks.append(jnp.concatenate(parts, axis=1))
    pay_in = jnp.stack(blocks)
    dv_blk = d_uv.reshape(KV_LORA, N_HEADS, HEAD_DIM).transpose(1, 0, 2)
    pay_ukv = jnp.concatenate([duk_blk[:, :, :HEAD_DIM], dv_blk], axis=2)
    return [pay_in, duq_blk, pay_ukv, d_out.reshape(N_DEV, 128, D_MODEL), dple_blk,
            d_pg.reshape(N_DEV, 128, D_MODEL)]


_VEC_SIZES = (D_MODEL, Q_LORA, KV_LORA, D_GRP, D_GRP, D_MODEL, D_MODEL, D_MODEL)


def _pack_vectors(vs, tail=None):
    flat = jnp.concatenate([v.reshape(-1) for v in vs])
    if tail is None:
        tail = jnp.zeros((D_MODEL,), F32)
    flat = jnp.concatenate([flat, tail.reshape(-1)])
    flat = jnp.pad(flat, (0, R_SMALL * LANES - flat.shape[0]))
    return flat.reshape(R_SMALL, LANES)


def _unpack_vectors(packed):
    flat = packed.reshape(-1)
    out, o = [], 0
    for n in _VEC_SIZES:
        out.append(flat[o:o + n].reshape(1, n))
        o += n
    return out, flat[o:o + D_MODEL]


def kernel(x, p, positions, norm_pre_g, w_in, q_norm_g, w_uq, kv_norm_g, w_ukv, sb_out_norm_g, mla_out_norm_g, w_out, norm_post_g, w_ple, ple_norm_g, w_ple_gate, b_ple_gate, loss_target, m_norm_pre_g, m_w_in, m_q_norm_g, m_w_uq, m_kv_norm_g, m_w_ukv, m_sb_out_norm_g, m_mla_out_norm_g, m_w_out, m_norm_post_g, m_w_ple, m_ple_norm_g, m_w_ple_gate, m_b_ple_gate, v_norm_pre_g, v_w_in, v_q_norm_g, v_w_uq, v_kv_norm_g, v_w_ukv, v_sb_out_norm_g, v_mla_out_norm_g, v_w_out, v_norm_post_g, v_w_ple, v_ple_norm_g, v_w_ple_gate, v_b_ple_gate):
    mats = (w_in, w_uq, w_ukv, w_out, w_ple, w_ple_gate)
    m_mats = (m_w_in, m_w_uq, m_w_ukv, m_w_out, m_w_ple, m_w_ple_gate)
    v_mats = (v_w_in, v_w_uq, v_w_ukv, v_w_out, v_w_ple, v_w_ple_gate)
    vecs = (norm_pre_g, q_norm_g, kv_norm_g, sb_out_norm_g, mla_out_norm_g, norm_post_g, ple_norm_g, b_ple_gate)
    m_vecs = (m_norm_pre_g, m_q_norm_g, m_kv_norm_g, m_sb_out_norm_g, m_mla_out_norm_g, m_norm_post_g,
              m_ple_norm_g, m_b_ple_gate)
    v_vecs = (v_norm_pre_g, v_q_norm_g, v_kv_norm_g, v_sb_out_norm_g, v_mla_out_norm_g, v_norm_post_g,
              v_ple_norm_g, v_b_ple_gate)

    gath = _all_gather([a[0].astype(BF16) for a in mats])
    grad_x, d_parts, vec_partials, loss_vec = _local_grads(
        x[0], p[0, 0], positions[0], loss_target[0], *vecs, *_kernel_weights(gath))
    pays = _grad_payloads(*d_parts)
    landed1 = _pair_exchange(pays, _pack_vectors(vec_partials, tail=loss_vec))
    place = jnp.stack([lax.axis_index("c"), 2 * lax.axis_index("x") + lax.axis_index("y")]).astype(jnp.int32)
    pair = [_pair_sum(g, l, place, "grad_pair_sum_%d" % o) for o, (g, l) in enumerate(zip(pays, landed1[:-1]))]
    landed2 = _chip_exchange([s for s, _ in pair])
    pad_uq = lambda a: jnp.pad(a[0], ((0, 0), (0, LANES - 96))) if a.shape[-1] == 96 else a[0]
    upd = [_adamw_matrix(own, l2, pad_uq(w), pad_uq(m), pad_uq(v), "adamw_%d" % o)
           for o, ((_, own), l2, w, m, v) in enumerate(zip(pair, landed2, mats, m_mats, v_mats))]
    sm = _adamw_vectors(landed1[-1], _pack_vectors(vecs), _pack_vectors(m_vecs), _pack_vectors(v_vecs))

    loss = jnp.sum(_unpack_vectors(sm[0])[1])
    outs = []
    for kind in range(4):
        mat = [upd[o][kind][:, :w.shape[-1]][None] for o, w in enumerate(mats)]
        vec = _unpack_vectors(sm[kind])[0]
        outs += [vec[0], mat[0], vec[1], mat[1], vec[2], mat[2], vec[3], vec[4], mat[3], vec[5],
                 mat[4], vec[6], mat[5], vec[7]]
    return (loss, grad_x[None], *outs)


def _local_grads(xs, ps, pos, tgt, norm_pre_g, q_norm_g, kv_norm_g, sb_out_norm_g, mla_out_norm_g,
                 norm_post_g, ple_norm_g, b_ple_gate, w_in_p, w_uq_p, w_uk_p, w_uv, f_out, f_ple, f_pg):
    s = xs.shape[0]

    half = ROPE_DIM // 2
    freq = ROPE_THETA ** (-jnp.arange(half, dtype=F32) / half)
    ang = pos.astype(F32)[:, None] * freq
    cos, sin = jnp.cos(ang), jnp.sin(ang)
    cos_t = jnp.concatenate([jnp.ones((s, 64), F32), cos, cos, jnp.zeros((s, 32), F32)], axis=1)
    sin_t = jnp.concatenate([jnp.zeros((s, 64), F32), -sin, sin, jnp.zeros((s, 32), F32)], axis=1)
    seg = jnp.arange(D_GRP) // HEAD_DIM
    bd = (seg[:, None] == seg[None, :]).astype(BF16)

    qkv, rest, h_b = _in_proj(xs, norm_pre_g, w_in_p)
    sb_o = _sb_fwd(qkv, 8)
    qp, kp, vv, cqn_b, ckvn_b = _mla_prep(rest, q_norm_g, kv_norm_g, w_uq_p, w_uk_p, w_uv, cos_t, sin_t)
    mla_o, lse = _mla_fwd(qp, kp, vv, 8)

    (dx1, d_sbo, d_mlo, d_sbg, d_mlg, x1_b, dgl_b, yc_b, dy_b, p_b, du_b, small_mid) = _mid(
        xs, ps, tgt, sb_o, mla_o, rest, sb_out_norm_g, mla_out_norm_g, f_out, norm_post_g,
        f_ple, ple_norm_g, f_pg, b_ple_gate, bd)
    dqp, dkp, dvv = _mla_bwd(qp, kp, vv, d_mlo, mla_o, lse, 4)
    dq_sb, dk_sb, dv_sb = _sb_bwd(qkv, d_sbo)
    dcq, dckv, dkr, dq_b, dk_b, dv_b, small_prep = _mla_prep_bwd(
        dqp, dkp, dvv, rest, q_norm_g, kv_norm_g, w_uq_p, w_uk_p, w_uv, cos_t, sin_t)
    pieces = [dq_sb, dk_sb, dv_sb, d_sbg, d_mlg, dcq, dckv, dkr]
    grad_x, small_in = _in_bwd(xs, norm_pre_g, dx1, pieces, w_in_p)

    d_cols = [_tn_matmul(h_b, pc, "dw_in_%d" % k) for k, pc in enumerate(pieces)]
    d_parts = (d_cols, _tn_matmul(cqn_b, dq_b, "dw_uq", blocked=True),
               _tn_matmul(ckvn_b, dk_b, "dw_uk", blocked=True), _tn_matmul(ckvn_b, dv_b, "dw_uv"),
               _tn_matmul(yc_b, dy_b, "dw_out"), _tn_matmul(p_b, du_b, "dw_ple", blocked=True),
               _tn_matmul(x1_b, dgl_b, "dw_pg"))
    vec_partials = [small_in[0], small_prep[0], small_prep[1, :KV_LORA], small_mid[3, :D_GRP],
                    small_mid[3, D_GRP:], small_mid[2], small_mid[1], small_mid[0]]
    return grad_x, d_parts, vec_partials, small_mid[4]
```

```python
import jax
import jax.numpy as jnp
from jax import lax
from jax.experimental import pallas as pl
from jax.experimental.pallas import tpu as pltpu

F32 = jnp.float32
BF16 = jnp.bfloat16
MESH = pl.DeviceIdType.MESH

N_DEV = 8
D_MODEL = 1024
N_HEADS = 8
HEAD_DIM = 64
D_GRP = N_HEADS * HEAD_DIM
Q_LORA = 256
KV_LORA = 128
ROPE_DIM = 32
PLE_DIM = 256
CHUNK_SHIFT = 6
ROPE_THETA = 10000.0
EPS = 1e-6
SB_SCALE = HEAD_DIM ** -0.5
MLA_SCALE = (HEAD_DIM + ROPE_DIM) ** -0.5
NEG = -1e30
LOG2_E = 1.4426950408889634
LN_2 = 0.6931471805599453
SB_CUTOFF = 110.0

ADAM_LR = 0.001
ADAM_B1 = 0.9
ADAM_B2 = 0.999
ADAM_EPS = 1e-08
ADAM_WD = 0.01
ADAM_STEP = 10

LANES = 128
TQ = 256
TK = 256
TM = 256

D_IN_P = 3072

_NT = (((1,), (1,)), ((), ()))
_TN = (((0,), (0,)), ((), ()))


def _params(sem, vmem_mb):
    return pltpu.CompilerParams(dimension_semantics=sem, vmem_limit_bytes=vmem_mb << 20)


def _dot(a, b):
    return jnp.dot(a, b, preferred_element_type=F32)


def _dot_nt(a, b):
    return lax.dot_general(a, b, _NT, preferred_element_type=F32)


def _dot_tn(a, b):
    return lax.dot_general(a, b, _TN, preferred_element_type=F32)


def _hl_dot(a, b):
    hi = a.astype(BF16)
    lo = (a - hi.astype(F32)).astype(BF16)
    return _dot(hi, b) + _dot(lo, b)


def _sigmoid(x):
    return 1.0 / (1.0 + jnp.exp(-x))


def _rope_swap(x, lane):
    left = pltpu.roll(x, LANES - 16, axis=1)
    right = pltpu.roll(x, 16, axis=1)
    lo = (lane >= 64) & (lane < 80)
    hi = (lane >= 80) & (lane < 96)
    return jnp.where(lo, left, jnp.where(hi, right, 0.0))


def _all_gather(shards):
    n_op = len(shards)

    def body(*refs):
        x_refs, out_refs = refs[:n_op], refs[n_op:2 * n_op]
        send_sems, recv_sems, local_sems = refs[2 * n_op:]
        x, y, c = lax.axis_index("x"), lax.axis_index("y"), lax.axis_index("c")
        me, sibling = (x, y, c), (x, y, 1 - c)
        chips = [(1 - x, y), (x, 1 - y), (1 - x, 1 - y)]

        def slot(o, px, py, pc):
            return out_refs[o].at[4 * px + 2 * py + pc]

        def copy(o, k, block, to, src=None):
            return pltpu.make_async_remote_copy(
                src_ref=slot(o, *block) if src is None else src, dst_ref=slot(o, *block),
                send_sem=send_sems.at[o, k], recv_sem=recv_sems.at[o, k],
                device_id=to, device_id_type=MESH)

        ops = range(n_op)
        mine = [pltpu.make_async_copy(x_refs[o], slot(o, *me), local_sems.at[o]) for o in ops]
        first = [copy(o, 0, me, sibling, src=x_refs[o]) for o in ops]
        first += [copy(o, 1 + j, me, (*chip, c), src=x_refs[o]) for j, chip in enumerate(chips) for o in ops]
        for cp in mine + first:
            cp.start()
        passed = []
        for j, chip in enumerate(chips):
            for o in ops:
                copy(o, 1 + j, (*chip, c), me).wait_recv()
                passed.append(copy(o, 4 + j, (*chip, c), sibling))
                passed[-1].start()
        for o in ops:
            copy(o, 0, sibling, me).wait_recv()
        for j, chip in enumerate(chips):
            for o in ops:
                copy(o, 4 + j, (*chip, 1 - c), me).wait_recv()
        for cp in first + passed:
            cp.wait_send()
        for cp in mine:
            cp.wait()

    vmem = pl.BlockSpec(memory_space=pltpu.VMEM)
    return pl.pallas_call(
        body, name="weight_all_gather",
        out_shape=[jax.ShapeDtypeStruct((N_DEV,) + a.shape, a.dtype) for a in shards],
        in_specs=[vmem] * n_op, out_specs=[vmem] * n_op,
        scratch_shapes=[pltpu.SemaphoreType.DMA((n_op, 7)), pltpu.SemaphoreType.DMA((n_op, 7)),
                        pltpu.SemaphoreType.DMA((n_op,))],
        compiler_params=pltpu.CompilerParams(vmem_limit_bytes=48 << 20),
    )(*shards)


def _pair_exchange(pays, small):
    n_op = len(pays)
    sr, n = small.shape

    def body(*refs):
        g_refs, s_ref = refs[:n_op], refs[n_op]
        l_refs, sland_ref = refs[n_op + 1:2 * n_op + 1], refs[2 * n_op + 1]
        ssem, rsem, ssem2, rsem2, lsem = refs[2 * n_op + 2:]
        x, y, c = lax.axis_index("x"), lax.axis_index("y"), lax.axis_index("c")
        me = 4 * x + 2 * y + c
        copies = []
        for o in range(n_op):
            for chip in range(4):
                copies.append(pltpu.make_async_remote_copy(
                    src_ref=g_refs[o].at[2 * chip + (1 - c)], dst_ref=l_refs[o].at[chip],
                    send_sem=ssem.at[o, chip], recv_sem=rsem.at[o, chip],
                    device_id=(x, y, 1 - c), device_id_type=MESH))
        for k in range(1, N_DEV):
            peer = (1 - x if (k >> 2) & 1 else x, 1 - y if (k >> 1) & 1 else y, 1 - c if k & 1 else c)
            copies.append(pltpu.make_async_remote_copy(
                src_ref=s_ref, dst_ref=sland_ref.at[me], send_sem=ssem2.at[k], recv_sem=rsem2.at[k],
                device_id=peer, device_id_type=MESH))
        own = pltpu.make_async_copy(s_ref, sland_ref.at[me], lsem)
        own.start()
        for cp in copies:
            cp.start()
        for cp in copies:
            cp.wait()
        own.wait()

    any_spec = pl.BlockSpec(memory_space=pl.ANY)
    return pl.pallas_call(
        body, name="grad_pair_exchange",
        out_shape=[jax.ShapeDtypeStruct((4,) + a.shape[1:], F32) for a in pays]
        + [jax.ShapeDtypeStruct((N_DEV, sr, n), F32)],
        in_specs=[any_spec] * (n_op + 1), out_specs=[any_spec] * (n_op + 1),
        scratch_shapes=[pltpu.SemaphoreType.DMA((n_op, 4)), pltpu.SemaphoreType.DMA((n_op, 4)),
                        pltpu.SemaphoreType.DMA((N_DEV,)), pltpu.SemaphoreType.DMA((N_DEV,)),
                        pltpu.SemaphoreType.DMA],
    )(*pays, small)


def _pair_sum(pay, landed, place, name):
    _, r, c = pay.shape

    def body(place_ref, g_ref, l_ref, s_ref, own_ref):
        i = pl.program_id(0)
        tot = g_ref[...] + l_ref[...]
        s_ref[...] = tot.astype(BF16)

        @pl.when(i == place_ref[1])
        def _():
            own_ref[...] = tot

    grid_spec = pltpu.PrefetchScalarGridSpec(
        num_scalar_prefetch=1, grid=(4,),
        in_specs=[pl.BlockSpec((None, r, c), lambda i, pr: (2 * i + pr[0], 0, 0)),
                  pl.BlockSpec((None, r, c), lambda i, pr: (i, 0, 0))],
        out_specs=[pl.BlockSpec((None, r, c), lambda i, pr: (i, 0, 0)),
                   pl.BlockSpec((r, c), lambda i, pr: (0, 0))])
    return pl.pallas_call(
        body, name=name, grid_spec=grid_spec,
        out_shape=[jax.ShapeDtypeStruct((4, r, c), BF16), jax.ShapeDtypeStruct((r, c), F32)],
        compiler_params=_params(("arbitrary",), 40),
    )(place, pay, landed)


def _chip_exchange(sums):
    n_op = len(sums)

    def body(*refs):
        s_refs, l_refs = refs[:n_op], refs[n_op:2 * n_op]
        ssem, rsem = refs[2 * n_op:]
        x, y, c = lax.axis_index("x"), lax.axis_index("y"), lax.axis_index("c")
        copies = []
        for rel in range(1, 4):
            px = 1 - x if rel & 2 else x
            py = 1 - y if rel & 1 else y
            for o in range(n_op):
                copies.append(pltpu.make_async_remote_copy(
                    src_ref=s_refs[o].at[2 * px + py], dst_ref=l_refs[o].at[rel - 1],
                    send_sem=ssem.at[o, rel - 1], recv_sem=rsem.at[o, rel - 1],
                    device_id=(px, py, c), device_id_type=MESH))
        for cp in copies:
            cp.start()
        for cp in copies:
            cp.wait()

    any_spec = pl.BlockSpec(memory_space=pl.ANY)
    return pl.pallas_call(
        body, name="grad_chip_exchange",
        out_shape=[jax.ShapeDtypeStruct((3,) + a.shape[1:], BF16) for a in sums],
        in_specs=[any_spec] * n_op, out_specs=[any_spec] * n_op,
        scratch_shapes=[pltpu.SemaphoreType.DMA((n_op, 3)), pltpu.SemaphoreType.DMA((n_op, 3))],
    )(*sums)


def _adamw_math(g, w, m, v):
    mn = ADAM_B1 * m + (1.0 - ADAM_B1) * g
    vn = ADAM_B2 * v + (1.0 - ADAM_B2) * (g * g)
    m_hat = mn / (1.0 - ADAM_B1 ** ADAM_STEP)
    v_hat = vn / (1.0 - ADAM_B2 ** ADAM_STEP)
    return -ADAM_LR * (m_hat / (jnp.sqrt(v_hat) + ADAM_EPS) + ADAM_WD * w), mn, vn


def _adamw_matrix(own, landed, w, m, v, name):
    r, c = w.shape
    cp = own.shape[1]
    br = min(r, 256)

    def body(own_ref, l_ref, w_ref, m_ref, v_ref, g_out, d_out, m_out, v_out):
        g = own_ref[...]
        for k in range(3):
            g = g + l_ref[k].astype(F32)
        g = g[:, :c]
        g_out[...] = g
        d_out[...], m_out[...], v_out[...] = _adamw_math(g, w_ref[...], m_ref[...], v_ref[...])

    row = pl.BlockSpec((br, c), lambda i: (i, 0))
    shp = jax.ShapeDtypeStruct((r, c), F32)
    return pl.pallas_call(
        body, name=name, grid=(r // br,),
        in_specs=[pl.BlockSpec((br, cp), lambda i: (i, 0)), pl.BlockSpec((3, br, cp), lambda i: (0, i, 0)),
                  row, row, row],
        out_specs=(row, row, row, row), out_shape=(shp, shp, shp, shp),
        compiler_params=_params(("parallel",), 40),
    )(own, landed, w, m, v)


_VEC_PLACE = ((0, 0), (1, 0), (2, 0), (3, 0), (3, D_GRP), (4, 0), (5, 0), (6, 0))


def _adamw_vectors(sland, ws, ms, vs):
    nv = len(ws)

    def body(l_ref, *refs):
        w_refs, m_refs, v_refs = refs[:nv], refs[nv:2 * nv], refs[2 * nv:3 * nv]
        loss_ref = refs[3 * nv]
        outs = refs[3 * nv + 1:]
        g_all = l_ref[0]
        for j in range(1, N_DEV):
            g_all = g_all + l_ref[j]
        loss_ref[...] = jnp.sum(g_all[7:8, :], axis=1, keepdims=True)
        for k, (row, lane0) in enumerate(_VEC_PLACE):
            n = w_refs[k].shape[1]
            g = g_all[row:row + 1, lane0:lane0 + n]
            d, mn, vn = _adamw_math(g, w_refs[k][...], m_refs[k][...], v_refs[k][...])
            outs[k][...] = g
            outs[nv + k][...] = d
            outs[2 * nv + k][...] = mn
            outs[3 * nv + k][...] = vn

    vmem = pl.BlockSpec(memory_space=pltpu.VMEM)
    shapes = [jax.ShapeDtypeStruct(w.shape, F32) for w in ws]
    return pl.pallas_call(
        body, name="adamw_vectors", in_specs=[vmem] * (1 + 3 * nv), out_specs=[vmem] * (1 + 4 * nv),
        out_shape=[jax.ShapeDtypeStruct((1, 1), F32)] + shapes * 4,
    )(sland, *ws, *ms, *vs)


def _in_proj(x, g, w):
    s = x.shape[0]

    def body(x_ref, g_ref, w_ref, qkv_ref, rest_ref, h_ref):
        xv = x_ref[...]
        r = lax.rsqrt(jnp.mean(xv * xv, axis=-1, keepdims=True) + EPS)
        h = ((xv * r) * g_ref[...]).astype(BF16)
        h_ref[...] = h
        qkv_ref[...] = _dot(h, w_ref[:, :1536]).astype(BF16)
        rest_ref[...] = _dot(h, w_ref[:, 1536:])

    return pl.pallas_call(
        body, name="in_proj", grid=(s // TM,),
        in_specs=[pl.BlockSpec((TM, D_MODEL), lambda i: (i, 0)),
                  pl.BlockSpec((1, D_MODEL), lambda i: (0, 0)),
                  pl.BlockSpec((D_MODEL, D_IN_P), lambda i: (0, 0))],
        out_specs=(pl.BlockSpec((TM, 1536), lambda i: (i, 0)),
                   pl.BlockSpec((TM, 1536), lambda i: (i, 0)),
                   pl.BlockSpec((TM, D_MODEL), lambda i: (i, 0))),
        out_shape=(jax.ShapeDtypeStruct((s, 1536), BF16), jax.ShapeDtypeStruct((s, 1536), F32),
                   jax.ShapeDtypeStruct((s, D_MODEL), BF16)),
        compiler_params=_params(("parallel",), 48),
    )(x, g, w)


def _mla_prep(rest, gq, gkv, wuq, wuk, wuv, cos_t, sin_t):
    s = rest.shape[0]

    def body(cq_ref, ckv_ref, kr_ref, gq_ref, gkv_ref, wuq_ref, wuk_ref, wuv_ref, c_ref, s_ref,
             qp_ref, kp_ref, vv_ref, cqn_ref, ckvn_ref):
        lane = lax.broadcasted_iota(jnp.int32, (1, LANES), 1)
        cos_v, sin_v = c_ref[...], s_ref[...]
        cq = cq_ref[...]
        rq = lax.rsqrt(jnp.mean(cq * cq, axis=-1, keepdims=True) + EPS)
        cqn = ((cq * rq) * gq_ref[...]).astype(BF16)
        cqn_ref[...] = cqn
        q = _dot(cqn, wuq_ref[...])
        ckv = ckv_ref[...]
        rkv = lax.rsqrt(jnp.mean(ckv * ckv, axis=-1, keepdims=True) + EPS)
        ckvn = ((ckv * rkv) * gkv_ref[...]).astype(BF16)
        ckvn_ref[...] = ckvn
        kn = _dot(ckvn, wuk_ref[...])
        vv_ref[...] = _dot(ckvn, wuv_ref[...]).astype(BF16)
        kr = kr_ref[...]
        kr_roped = kr * cos_v + _rope_swap(kr, lane) * sin_v
        for h in range(N_HEADS):
            sl = slice(h * LANES, (h + 1) * LANES)
            qh = q[:, sl]
            qp_ref[:, sl] = (qh * cos_v + _rope_swap(qh, lane) * sin_v).astype(BF16)
            kp_ref[:, sl] = (kn[:, sl] + kr_roped).astype(BF16)

    def row(width, idx):
        return pl.BlockSpec((TM, width), lambda i: (i, idx))

    def full(a):
        return pl.BlockSpec(a.shape, lambda i: (0, 0))

    return pl.pallas_call(
        body, name="mla_prep", grid=(s // TM,),
        in_specs=[row(Q_LORA, 4), row(KV_LORA, 10), row(LANES, 11), full(gq), full(gkv),
                  full(wuq), full(wuk), full(wuv), row(LANES, 0), row(LANES, 0)],
        out_specs=(row(1024, 0), row(1024, 0), row(D_GRP, 0), row(Q_LORA, 0), row(KV_LORA, 0)),
        out_shape=(jax.ShapeDtypeStruct((s, 1024), BF16), jax.ShapeDtypeStruct((s, 1024), BF16),
                   jax.ShapeDtypeStruct((s, D_GRP), BF16), jax.ShapeDtypeStruct((s, Q_LORA), BF16),
                   jax.ShapeDtypeStruct((s, KV_LORA), BF16)),
        compiler_params=_params(("parallel",), 32),
    )(rest, rest, rest, gq, gkv, wuq, wuk, wuv, cos_t, sin_t)


def _sb_live(n, qi, carries):
    top = carries[0]
    for c in carries[1:]:
        top = jnp.maximum(top, c)
    return jnp.logical_and(n < qi, jnp.max(top) > -SB_CUTOFF)


def _sb_fwd(qkv, hb):
    s = qkv.shape[0]

    def body(q_ref, k_ref, v_ref, o_ref, acc):
        qi = pl.program_id(1)
        lane = lax.broadcasted_iota(jnp.int32, (1, LANES), 1)
        is_a = lane < HEAD_DIM
        pair = lambda h: slice((h // 2) * LANES, (h // 2 + 1) * LANES)
        q_h = []
        for h in range(hb):
            qs = q_ref[:, pair(h)] * SB_SCALE
            mine = is_a if h % 2 == 0 else jnp.logical_not(is_a)
            q_h.append(jnp.where(mine, qs, jnp.zeros_like(qs)))
        r_i = lax.broadcasted_iota(jnp.int32, (TQ, TK), 0)
        c_i = lax.broadcasted_iota(jnp.int32, (TQ, TK), 1)
        past = c_i < r_i
        upper = (r_i > c_i).astype(BF16)
        acc[...] = jnp.zeros_like(acc)

        def tile(j, carries, diag):
            ks = pl.ds(pl.multiple_of(j * TK, TK), TK)
            zs = [_dot_nt(q_h[h], k_ref[ks, pair(h)]) for h in range(hb)]
            if diag:
                zs = [jnp.where(past, z, NEG) for z in zs]
            lfs = [-(jnp.maximum(z, 0.0) + jnp.log(1.0 + jnp.exp(-jnp.abs(z)))) for z in zs]
            sufs = [_hl_dot(lfs[h], upper) for h in range(hb)]
            out = []
            for h in range(hb):
                w = jnp.exp(zs[h] + lfs[h] + (sufs[h] + carries[h]))
                acc[h] += _dot(w.astype(BF16), v_ref[ks, pair(h)])
                out.append(carries[h] + jnp.sum(lfs[h], axis=1, keepdims=True))
            return tuple(out)

        zero = jnp.zeros((TQ, 1), F32)
        carries = tile(qi, (zero,) * hb, True)

        def step(st):
            return (st[0] + 1,) + tile(qi - 1 - st[0], st[1:], False)

        lax.while_loop(lambda st: _sb_live(st[0], qi, st[1:]), step, (0,) + carries)
        for pr in range(hb // 2):
            o_ref[:, pr * LANES:(pr + 1) * LANES] = jnp.where(is_a, acc[2 * pr], acc[2 * pr + 1])

    width = hb * HEAD_DIM
    nb = D_GRP // width
    slab = lambda part: pl.BlockSpec((s, width), lambda g, qi: (0, part * nb + g))
    blk = pl.BlockSpec((TQ, width), lambda g, qi: (qi, g))
    return pl.pallas_call(
        body, name="sb_fwd", grid=(nb, s // TQ),
        in_specs=[blk, slab(1), slab(2)], out_specs=blk,
        out_shape=jax.ShapeDtypeStruct((s, D_GRP), F32),
        scratch_shapes=[pltpu.VMEM((hb, TQ, LANES), F32)],
        compiler_params=_params(("arbitrary", "arbitrary"), 48),
    )(qkv, qkv, qkv)


def _sb_bwd(qkv, d_o):
    s = qkv.shape[0]
    nq = s // TQ
    nk = s // TK

    def body(q_ref, k_ref, v_ref, do_ref, dq_ref, dk_ref, dv_ref, x1s, bts, dqacc, dkacc, dvacc):
        qi = pl.program_id(1)
        lane = lax.broadcasted_iota(jnp.int32, (1, LANES), 1)
        is_a = lane < HEAD_DIM

        @pl.when(qi == 0)
        def _():
            dkacc[...] = jnp.zeros_like(dkacc)
            dvacc[...] = jnp.zeros_like(dvacc)

        qs = q_ref[...] * SB_SCALE
        zq = jnp.zeros_like(qs)
        qs_x = (jnp.where(is_a, qs, zq), jnp.where(is_a, zq, qs))
        dob = do_ref[...].astype(BF16)
        do_x = (jnp.where(is_a, dob, zq), jnp.where(is_a, zq, dob))
        r_i = lax.broadcasted_iota(jnp.int32, (TQ, TK), 0)
        c_i = lax.broadcasted_iota(jnp.int32, (TQ, TK), 1)
        past = c_i < r_i
        upper = (r_i > c_i).astype(BF16)
        upper_incl = (r_i >= c_i).astype(BF16)
        dqacc[...] = jnp.zeros_like(dqacc)
        both = ((0, 0), (0, 1), (1, 0), (1, 1))

        def tiles(n):
            j_hi = qi - 2 * n
            lo_ok = j_hi >= 1
            j_lo = jnp.maximum(j_hi - 1, 0)
            ks = (pl.ds(pl.multiple_of(j_hi * TK, TK), TK), pl.ds(pl.multiple_of(j_lo * TK, TK), TK))
            return j_hi, lo_ok, j_lo, ks

        def sweep(n, carries):
            j_hi, lo_ok, j_lo, ks = tiles(n)
            slot = (j_hi, jnp.where(lo_ok, j_lo, nk))
            valid = (jnp.logical_or(past, j_hi < qi), lo_ok)
            z = {th: jnp.where(valid[th[0]], _dot_nt(qs_x[th[1]], k_ref[ks[th[0]], :]), NEG) for th in both}
            d_a = {th: _dot_nt(do_x[th[1]], v_ref[ks[th[0]], :]) for th in both}
            lf, beta, omb = {}, {}, {}
            for th in both:
                e = jnp.exp(-jnp.abs(z[th]))
                den = 1.0 + e
                rden = 1.0 / den
                pos = z[th] >= 0.0
                lf[th] = -(jnp.maximum(z[th], 0.0) + jnp.log(den))
                beta[th] = jnp.where(pos, rden, e * rden)
                omb[th] = jnp.where(pos, e * rden, rden)
            suf = {th: _hl_dot(lf[th], upper) for th in both}
            c, g_in = {}, {}
            for h in range(2):
                c[0, h], g_in[0, h] = carries[2 * h], carries[2 * h + 1]
                c[1, h] = c[0, h] + jnp.sum(lf[0, h], axis=1, keepdims=True)
            a, g = {}, {}
            for th in both:
                a[th] = jnp.exp(z[th] + lf[th] + (suf[th] + c[th]))
                g[th] = a[th] * d_a[th]
            sg = {th: _hl_dot(g[th], upper_incl) for th in both}
            for h in range(2):
                g_in[1, h] = g_in[0, h] + jnp.sum(g[0, h], axis=1, keepdims=True)
            for th in both:
                t, h = th
                x1s[slot[t], h] = g[th] * omb[th] + beta[th] * (sg[th] + g_in[th])
                bts[slot[t], h] = beta[th]
                dvacc[ks[t], :] += _dot_tn(a[th].astype(BF16), do_x[h])
            out = []
            for h in range(2):
                out.append(c[1, h] + jnp.sum(lf[1, h], axis=1, keepdims=True))
                out.append(g_in[1, h] + jnp.sum(g[1, h], axis=1, keepdims=True))
            return tuple(out)

        zero = jnp.zeros((TQ, 1), F32)
        first = sweep(0, (zero, zero, zero, zero))

        def more(st):
            return jnp.logical_and(2 * st[0] <= qi, jnp.max(jnp.maximum(st[1], st[3])) > -SB_CUTOFF)

        swept = lax.while_loop(more, lambda st: (st[0] + 1,) + sweep(st[0], st[1:]), (1,) + first)
        g_tot = (swept[2], swept[4])

        def apply(n, carry):
            j_hi, lo_ok, j_lo, ks = tiles(n)

            def one(j, kslice):
                for h in range(2):
                    dz = (x1s[j, h] - bts[j, h] * g_tot[h]).astype(BF16)
                    dqacc[h] += _dot(dz, k_ref[kslice, :])
                    dkacc[kslice, :] += _dot_tn(dz, qs_x[h])

            one(j_hi, ks[0])

            @pl.when(lo_ok)
            def _():
                one(j_lo, ks[1])

            return carry

        lax.fori_loop(0, swept[0], apply, 0)
        dq_ref[...] = (jnp.where(is_a, dqacc[0], dqacc[1]) * SB_SCALE).astype(BF16)

        @pl.when(qi == nq - 1)
        def _():
            dk_ref[...] = dkacc[...].astype(BF16)
            dv_ref[...] = dvacc[...].astype(BF16)

    slab = lambda off: pl.BlockSpec((s, LANES), lambda p, qi: (0, off + p))
    blk = pl.BlockSpec((TQ, LANES), lambda p, qi: (qi, p))
    out_slab = pl.BlockSpec((s, LANES), lambda p, qi: (0, p))
    shp = jax.ShapeDtypeStruct((s, D_GRP), BF16)
    return pl.pallas_call(
        body, name="sb_bwd", grid=(4, nq),
        in_specs=[blk, slab(4), slab(8), blk],
        out_specs=(blk, out_slab, out_slab), out_shape=(shp, shp, shp),
        scratch_shapes=[pltpu.VMEM((nk + 1, 2, TQ, TK), F32)] * 2
        + [pltpu.VMEM((2, TQ, LANES), F32), pltpu.VMEM((s, LANES), F32), pltpu.VMEM((s, LANES), F32)],
        compiler_params=_params(("arbitrary", "arbitrary"), 56),
    )(qkv, qkv, qkv, d_o)


def _mla_fwd(qp, kp, vv, hb):
    s = qp.shape[0]
    c2 = MLA_SCALE * LOG2_E

    def body(q_ref, k_ref, v_ref, o_ref, lse_ref, vaug, mrun, mb, acc):
        qi = pl.program_id(1)
        lane = lax.broadcasted_iota(jnp.int32, (1, LANES), 1)
        is_a = lane < HEAD_DIM

        @pl.when(qi == 0)
        def _():
            for h in range(hb):
                vp = v_ref[:, (h // 2) * LANES:(h // 2 + 1) * LANES]
                mine = is_a if h % 2 == 0 else jnp.logical_not(is_a)
                vaug[h] = jnp.where(mine, vp, jnp.ones_like(vp))

        r_i = lax.broadcasted_iota(jnp.int32, (TQ, TK), 0)
        c_i = lax.broadcasted_iota(jnp.int32, (TQ, TK), 1)
        visible = (c_i >> CHUNK_SHIFT) <= (r_i >> CHUNK_SHIFT)

        def scores(j):
            ks = pl.ds(pl.multiple_of(j * TK, TK), TK)
            return ks, [_dot_nt(q_ref[:, h * LANES:(h + 1) * LANES], k_ref[ks, h * LANES:(h + 1) * LANES])
                        for h in range(hb)]

        def sweep(tile):
            def loop(j, carry):
                tile(j, False)
                return carry

            lax.fori_loop(0, qi, loop, 0)
            tile(qi, True)

        mrun[...] = jnp.full_like(mrun, NEG)

        def tile_max(j, diag):
            _, zs = scores(j)
            for h in range(hb):
                z = jnp.where(visible, zs[h], NEG) if diag else zs[h]
                mrun[h] = jnp.maximum(mrun[h], z)

        sweep(tile_max)
        for h in range(hb):
            m = jnp.max(mrun[h], axis=1, keepdims=True) * c2
            mb[h] = jnp.broadcast_to(m, (TQ, TK))
        acc[...] = jnp.zeros_like(acc)

        def tile_pv(j, diag):
            ks, zs = scores(j)
            for h in range(hb):
                e = zs[h] * c2 - mb[h]
                if diag:
                    e = jnp.where(visible, e, NEG)
                acc[h] += _dot(jnp.exp2(e).astype(BF16), vaug[h, ks, :])

        sweep(tile_pv)
        for pr in range(hb // 2):
            a, b = 2 * pr, 2 * pr + 1
            psl = slice(pr * LANES, (pr + 1) * LANES)
            acc_a, acc_b = acc[a], acc[b]
            l_a = pltpu.roll(acc_a, HEAD_DIM, axis=1)
            l_b = pltpu.roll(acc_b, HEAD_DIM, axis=1)
            o_ref[:, psl] = jnp.where(is_a, acc_a * (1.0 / l_a), acc_b * (1.0 / l_b))
            lse_ref[:, psl] = jnp.where(is_a, mb[a, :, :LANES] * LN_2 + jnp.log(l_a),
                                        mb[b, :, :LANES] * LN_2 + jnp.log(l_b))

    blk = pl.BlockSpec((TQ, hb * HEAD_DIM), lambda g, qi: (qi, g))
    shp = jax.ShapeDtypeStruct((s, D_GRP), F32)
    return pl.pallas_call(
        body, name="mla_fwd", grid=(N_HEADS // hb, s // TQ),
        in_specs=[pl.BlockSpec((TQ, hb * LANES), lambda g, qi: (qi, g)),
                  pl.BlockSpec((s, hb * LANES), lambda g, qi: (0, g)),
                  pl.BlockSpec((s, hb * HEAD_DIM), lambda g, qi: (0, g))],
        out_specs=(blk, blk), out_shape=(shp, shp),
        scratch_shapes=[pltpu.VMEM((hb, s, LANES), BF16), pltpu.VMEM((hb, TQ, TK), F32),
                        pltpu.VMEM((hb, TQ, TK), F32), pltpu.VMEM((hb, TQ, LANES), F32)],
        compiler_params=_params(("arbitrary", "arbitrary"), 56),
    )(qp, kp, vv)


def _mla_bwd(qp, kp, vv, d_o, o, lse, hb):
    s = qp.shape[0]
    c2 = MLA_SCALE * LOG2_E

    def body(q_ref, k_ref, v_ref, do_ref, o_ref, lse_ref, dq_ref, dk_ref, dv_ref, dqacc, lse_b, delta_b):
        qi = pl.program_id(1)
        lane = lax.broadcasted_iota(jnp.int32, (1, LANES), 1)
        is_a = lane < HEAD_DIM

        @pl.when(qi == 0)
        def _():
            dk_ref[...] = jnp.zeros_like(dk_ref)
            dv_ref[...] = jnp.zeros_like(dv_ref)

        r_i = lax.broadcasted_iota(jnp.int32, (TQ, TK), 0)
        c_i = lax.broadcasted_iota(jnp.int32, (TQ, TK), 1)
        visible = (c_i >> CHUNK_SHIFT) <= (r_i >> CHUNK_SHIFT)
        do_x = []
        for h in range(hb):
            psl = slice((h // 2) * LANES, (h // 2 + 1) * LANES)
            mine = is_a if h % 2 == 0 else jnp.logical_not(is_a)
            d_o = do_ref[:, psl]
            delta = jnp.sum(jnp.where(mine, d_o * o_ref[:, psl], 0.0), axis=1, keepdims=True)
            lse_h = jnp.sum(jnp.where(lane == (h % 2) * HEAD_DIM, lse_ref[:, psl], 0.0), axis=1, keepdims=True)
            lse_b[h] = jnp.broadcast_to(lse_h * LOG2_E, (TQ, TK))
            delta_b[h] = jnp.broadcast_to(delta, (TQ, TK))
            do_x.append(jnp.where(mine, d_o, 0.0).astype(BF16))
        dqacc[...] = jnp.zeros_like(dqacc)

        def tile(j, diag):
            ks = pl.ds(pl.multiple_of(j * TK, TK), TK)
            head = lambda h: slice(h * LANES, (h + 1) * LANES)
            pair = lambda h: slice((h // 2) * LANES, (h // 2 + 1) * LANES)
            zs = [_dot_nt(q_ref[:, head(h)], k_ref[ks, head(h)]) for h in range(hb)]
            dps = [_dot_nt(do_x[h], v_ref[ks, pair(h)]) for h in range(hb)]
            for h in range(hb):
                e = zs[h] * c2 - lse_b[h]
                if diag:
                    e = jnp.where(visible, e, NEG)
                p = jnp.exp2(e)
                ds = (p * (dps[h] - delta_b[h]) * MLA_SCALE).astype(BF16)
                dqacc[h] += _dot(ds, k_ref[ks, head(h)])
                dk_ref[ks, head(h)] += _dot_tn(ds, q_ref[:, head(h)])
                dv_ref[ks, pair(h)] += _dot_tn(p.astype(BF16), do_x[h])

        def loop(j, c):
            tile(j, False)
            return c

        lax.fori_loop(0, qi, loop, 0)
        tile(qi, True)
        for h in range(hb):
            dq_ref[:, h * LANES:(h + 1) * LANES] = dqacc[h]

    blk = pl.BlockSpec((TQ, hb * HEAD_DIM), lambda g, qi: (qi, g))
    return pl.pallas_call(
        body, name="mla_bwd", grid=(N_HEADS // hb, s // TQ),
        in_specs=[pl.BlockSpec((TQ, hb * LANES), lambda g, qi: (qi, g)),
                  pl.BlockSpec((s, hb * LANES), lambda g, qi: (0, g)),
                  pl.BlockSpec((s, hb * HEAD_DIM), lambda g, qi: (0, g)), blk, blk, blk],
        out_specs=(pl.BlockSpec((TQ, hb * LANES), lambda g, qi: (qi, g)),
                   pl.BlockSpec((s, hb * LANES), lambda g, qi: (0, g)),
                   pl.BlockSpec((s, hb * HEAD_DIM), lambda g, qi: (0, g))),
        out_shape=(jax.ShapeDtypeStruct((s, 1024), F32), jax.ShapeDtypeStruct((s, 1024), F32),
                   jax.ShapeDtypeStruct((s, D_GRP), F32)),
        scratch_shapes=[pltpu.VMEM((hb, TQ, LANES), F32), pltpu.VMEM((hb, TQ, TK), F32),
                        pltpu.VMEM((hb, TQ, TK), F32)],
        compiler_params=_params(("arbitrary", "arbitrary"), 56),
    )(qp, kp, vv, d_o, o, lse)


def _mid(x, p, target, sb_o, mla_o, rest, g_sb, g_mla, w_out, g_post, w_ple, g_ple, w_pg, b_pg, bd):
    s = x.shape[0]

    def body(x_ref, p_ref, t_ref, sbo_ref, mlo_ref, sbg_ref, mlg_ref, gsb_ref, gml_ref, wout_ref,
             gpost_ref, wple_ref, gple_ref, wpg_ref, bpg_ref, bd_ref,
             dx1_ref, dsbo_ref, dmlo_ref, dsbg_ref, dmlg_ref, x1b_ref, dglb_ref, ycb_ref, dyb_ref,
             pb_ref, dub_ref, small_ref):
        i = pl.program_id(0)
        bd_m = bd_ref[...]

        def seg_mean(v):
            return _dot(v.astype(BF16), bd_m) * (1.0 / HEAD_DIM)

        groups = []
        for o_ref, gate_ref, gain_ref in ((sbo_ref, sbg_ref, gsb_ref), (mlo_ref, mlg_ref, gml_ref)):
            o = o_ref[...]
            r = lax.rsqrt(seg_mean(o * o) + EPS)
            n = o * r
            hn = n * gain_ref[...]
            gate = gate_ref[...]
            sg = _sigmoid(gate)
            si = gate * sg
            groups.append((r, n, hn, gate, sg, si, gain_ref[...]))
        ya = (groups[0][2] * groups[0][5]).astype(BF16)
        yb = (groups[1][2] * groups[1][5]).astype(BF16)
        ycb_ref[:, :D_GRP] = ya
        ycb_ref[:, D_GRP:] = yb
        y = _dot(ya, wout_ref[:D_GRP, :]) + _dot(yb, wout_ref[D_GRP:, :])
        ry = lax.rsqrt(jnp.mean(y * y, axis=-1, keepdims=True) + EPS)
        ny = y * ry
        x1 = x_ref[...] + ny * gpost_ref[...]
        x1b = x1.astype(BF16)
        x1b_ref[...] = x1b
        pb = p_ref[...].astype(BF16)
        pb_ref[...] = pb
        u = _dot(pb, wple_ref[...])
        ru = lax.rsqrt(jnp.mean(u * u, axis=-1, keepdims=True) + EPS)
        nu = u * ru
        ple = nu * gple_ref[...]
        gate = _sigmoid(_dot(x1b, wpg_ref[...]) + bpg_ref[...])
        x2 = x1 + ple * gate
        diff = x2 - t_ref[...]
        dx2 = diff * (1.0 / D_MODEL)

        d_ple = dx2 * gate
        d_glin = (dx2 * ple) * (gate * (1.0 - gate))
        dglb = d_glin.astype(BF16)
        dglb_ref[...] = dglb
        dx1 = dx2 + _dot_nt(dglb, wpg_ref[...])
        dx1_ref[...] = dx1
        d_nu = d_ple * gple_ref[...]
        d_u = ru * (d_nu - nu * jnp.mean(d_nu * nu, axis=-1, keepdims=True))
        dub_ref[...] = d_u.astype(BF16)
        d_ny = dx1 * gpost_ref[...]
        d_y = ry * (d_ny - ny * jnp.mean(d_ny * ny, axis=-1, keepdims=True))
        dyb = d_y.astype(BF16)
        dyb_ref[...] = dyb
        d_yc = (_dot_nt(dyb, wout_ref[:D_GRP, :]), _dot_nt(dyb, wout_ref[D_GRP:, :]))

        d_gain = []
        for gx, (do_ref, dg_ref) in enumerate(((dsbo_ref, dsbg_ref), (dmlo_ref, dmlg_ref))):
            r, n, hn, gate_g, sg, si, gain = groups[gx]
            dyg = d_yc[gx]
            d_hn = dyg * si
            dg_ref[...] = (dyg * hn * (sg * (1.0 + gate_g * (1.0 - sg)))).astype(BF16)
            d_gain.append(jnp.sum(d_hn * n, axis=0, keepdims=True))
            d_n = d_hn * gain
            do_ref[...] = r * (d_n - n * seg_mean(d_n * n))

        @pl.when(i == 0)
        def _():
            small_ref[...] = jnp.zeros_like(small_ref)

        small_ref[3:4, :D_GRP] += d_gain[0]
        small_ref[3:4, D_GRP:] += d_gain[1]
        small_ref[4:5, :] += jnp.sum(dx1 * ny, axis=0, keepdims=True)
        small_ref[5:6, :] += jnp.sum(d_ple * nu, axis=0, keepdims=True)
        small_ref[6:7, :] += jnp.sum(d_glin, axis=0, keepdims=True)
        small_ref[7:8, :] += jnp.sum(diff * diff, axis=0, keepdims=True) * (0.5 / D_MODEL)

    def row(width, idx=0):
        return pl.BlockSpec((TM, width), lambda i: (i, idx))

    def full(a):
        return pl.BlockSpec(a.shape, lambda i: (0, 0))

    f32 = lambda w: jax.ShapeDtypeStruct((s, w), F32)
    b16 = lambda w: jax.ShapeDtypeStruct((s, w), BF16)
    return pl.pallas_call(
        body, name="mid", grid=(s // TM,),
        in_specs=[row(D_MODEL), row(PLE_DIM), row(D_MODEL), row(D_GRP), row(D_GRP),
                  row(D_GRP, 0), row(D_GRP, 1), full(g_sb), full(g_mla), full(w_out), full(g_post),
                  full(w_ple), full(g_ple), full(w_pg), full(b_pg), full(bd)],
        out_specs=(row(D_MODEL), row(D_GRP), row(D_GRP), row(D_GRP), row(D_GRP), row(D_MODEL),
                   row(D_MODEL), row(D_MODEL), row(D_MODEL), row(PLE_DIM), row(D_MODEL),
                   pl.BlockSpec((8, D_MODEL), lambda i: (0, 0))),
        out_shape=(f32(D_MODEL), f32(D_GRP), f32(D_GRP), b16(D_GRP), b16(D_GRP), b16(D_MODEL),
                   b16(D_MODEL), b16(D_MODEL), b16(D_MODEL), b16(PLE_DIM), b16(D_MODEL),
                   jax.ShapeDtypeStruct((8, D_MODEL), F32)),
        compiler_params=_params(("arbitrary",), 56),
    )(x, p, target, sb_o, mla_o, rest, rest, g_sb, g_mla, w_out, g_post, w_ple, g_ple, w_pg, b_pg, bd)


def _mla_prep_bwd(dqp, dkp, dvv, rest, gq, gkv, wuq, wuk, wuv, cos_t, sin_t):
    s = rest.shape[0]

    def body(dqp_ref, dkp_ref, dvv_ref, cq_ref, ckv_ref, gq_ref, gkv_ref, wuq_ref, wuk_ref, wuv_ref,
             c_ref, s_ref, dcq_ref, dckv_ref, dkr_ref, dqb_ref, dkb_ref, dvb_ref, small_ref):
        i = pl.program_id(0)
        lane = lax.broadcasted_iota(jnp.int32, (1, LANES), 1)
        in_rope = (lane >= HEAD_DIM) & (lane < HEAD_DIM + ROPE_DIM)
        cos_v, sin_v = c_ref[...], s_ref[...]
        dkr_roped = jnp.zeros((TM, LANES), F32)
        for h in range(N_HEADS):
            sl = slice(h * LANES, (h + 1) * LANES)
            dy = dqp_ref[:, sl]
            dqb_ref[:, sl] = (dy * cos_v + _rope_swap(dy * sin_v, lane)).astype(BF16)
            dkh = dkp_ref[:, sl]
            dkb_ref[:, sl] = dkh.astype(BF16)
            dkr_roped = dkr_roped + jnp.where(in_rope, dkh, 0.0)
        dkr_ref[...] = (dkr_roped * cos_v + _rope_swap(dkr_roped * sin_v, lane)).astype(BF16)
        dvb = dvv_ref[...].astype(BF16)
        dvb_ref[...] = dvb

        cq = cq_ref[...]
        rq = lax.rsqrt(jnp.mean(cq * cq, axis=-1, keepdims=True) + EPS)
        nq_ = cq * rq
        d_cqn = _dot_nt(dqb_ref[...], wuq_ref[...])
        d_n = d_cqn * gq_ref[...]
        dcq_ref[...] = (rq * (d_n - nq_ * jnp.mean(d_n * nq_, axis=-1, keepdims=True))).astype(BF16)

        ckv = ckv_ref[...]
        rkv = lax.rsqrt(jnp.mean(ckv * ckv, axis=-1, keepdims=True) + EPS)
        nkv = ckv * rkv
        d_ckvn = _dot_nt(dkb_ref[...], wuk_ref[...]) + _dot_nt(dvb, wuv_ref[...])
        d_n2 = d_ckvn * gkv_ref[...]
        dckv_ref[...] = (rkv * (d_n2 - nkv * jnp.mean(d_n2 * nkv, axis=-1, keepdims=True))).astype(BF16)

        @pl.when(i == 0)
        def _():
            small_ref[...] = jnp.zeros_like(small_ref)

        small_ref[0:1, :] += jnp.sum(d_cqn * nq_, axis=0, keepdims=True)
        small_ref[1:2, :KV_LORA] += jnp.sum(d_ckvn * nkv, axis=0, keepdims=True)

    def row(width, idx=0):
        return pl.BlockSpec((TM, width), lambda i: (i, idx))

    def full(a):
        return pl.BlockSpec(a.shape, lambda i: (0, 0))

    b16 = lambda w: jax.ShapeDtypeStruct((s, w), BF16)
    return pl.pallas_call(
        body, name="mla_prep_bwd", grid=(s // TM,),
        in_specs=[row(1024), row(1024), row(D_GRP), row(Q_LORA, 4), row(KV_LORA, 10), full(gq), full(gkv),
                  full(wuq), full(wuk), full(wuv), row(LANES), row(LANES)],
        out_specs=(row(Q_LORA), row(KV_LORA), row(LANES), row(1024), row(1024), row(D_GRP),
                   pl.BlockSpec((8, Q_LORA), lambda i: (0, 0))),
        out_shape=(b16(Q_LORA), b16(KV_LORA), b16(LANES), b16(1024), b16(1024), b16(D_GRP),
                   jax.ShapeDtypeStruct((8, Q_LORA), F32)),
        compiler_params=_params(("arbitrary",), 40),
    )(dqp, dkp, dvv, rest, rest, gq, gkv, wuq, wuk, wuv, cos_t, sin_t)


def _in_bwd(x, g, dx1, pieces, w):
    s = x.shape[0]
    widths = [a.shape[1] for a in pieces]
    offs = [sum(widths[:k]) for k in range(len(widths))]

    def body(x_ref, g_ref, dx1_ref, *refs):
        piece_refs = refs[:len(pieces)]
        w_ref, dx_ref, small_ref = refs[len(pieces):]
        i = pl.program_id(0)
        dh = jnp.zeros((TM, D_MODEL), F32)
        for pr, off, wd in zip(piece_refs, offs, widths):
            dh = dh + _dot_nt(pr[...], w_ref[:, off:off + wd])
        xv = x_ref[...]
        r = lax.rsqrt(jnp.mean(xv * xv, axis=-1, keepdims=True) + EPS)
        n = xv * r
        d_n = dh * g_ref[...]
        dx_ref[...] = dx1_ref[...] + r * (d_n - n * jnp.mean(d_n * n, axis=-1, keepdims=True))

        @pl.when(i == 0)
        def _():
            small_ref[...] = jnp.zeros_like(small_ref)

        small_ref[0:1, :] += jnp.sum(dh * n, axis=0, keepdims=True)

    def row(width):
        return pl.BlockSpec((TM, width), lambda i: (i, 0))

    return pl.pallas_call(
        body, name="in_bwd", grid=(s // TM,),
        in_specs=[row(D_MODEL), pl.BlockSpec((1, D_MODEL), lambda i: (0, 0)), row(D_MODEL)]
        + [row(wd) for wd in widths] + [pl.BlockSpec(w.shape, lambda i: (0, 0))],
        out_specs=(row(D_MODEL), pl.BlockSpec((8, D_MODEL), lambda i: (0, 0))),
        out_shape=(jax.ShapeDtypeStruct((s, D_MODEL), F32), jax.ShapeDtypeStruct((8, D_MODEL), F32)),
        compiler_params=_params(("arbitrary",), 48),
    )(x, g, dx1, *pieces, w)


def _tn_matmul(a, b, name, blocked=False):
    s, k = a.shape
    n = b.shape[1]
    ts = 512
    tn = n if blocked else min(n, 512)
    steps = s // ts

    def body(a_ref, b_ref, o_ref):
        t = pl.program_id(1)

        @pl.when(t == 0)
        def _():
            o_ref[...] = jnp.zeros_like(o_ref)

        prod = _dot_tn(a_ref[...], b_ref[...])
        if blocked:
            for j in range(n // LANES):
                o_ref[j] += prod[:, j * LANES:(j + 1) * LANES]
        else:
            o_ref[...] += prod

    if blocked:
        out_spec = pl.BlockSpec((n // LANES, k, LANES), lambda j, t: (0, 0, 0))
        out_shape = jax.ShapeDtypeStruct((n // LANES, k, LANES), F32)
    else:
        out_spec = pl.BlockSpec((k, tn), lambda j, t: (0, j))
        out_shape = jax.ShapeDtypeStruct((k, n), F32)
    return pl.pallas_call(
        body, name=name, grid=(n // tn, steps),
        in_specs=[pl.BlockSpec((ts, k), lambda j, t: (t, 0)), pl.BlockSpec((ts, tn), lambda j, t: (t, j))],
        out_specs=out_spec, out_shape=out_shape,
        compiler_params=_params(("parallel", "arbitrary"), 40),
    )(a, b)


IN_SHARD = 372
_IN_KERNEL_ORDER = ((0, 2048), (2464, 2976), (2048, 2432))
_IN_ROPE = (2432, 2464)
_IN_GRAD_SRC = ((0, 512, 0, 0), (512, 1024, 1, 0), (1024, 1536, 2, 0), (1536, 2048, 3, 0),
                (2048, 2304, 5, 0), (2304, 2432, 6, 0), (2432, 2464, 7, 64), (2464, 2976, 4, 0))


def _shard_cols(gath_in, lo, hi):
    out = []
    while lo < hi:
        j, a = divmod(lo, IN_SHARD)
        b = min(IN_SHARD, a + hi - lo)
        out.append(gath_in[j][:, a:b])
        lo += b - a
    return out


def _kernel_weights(gath):
    g_in, g_uq, g_ukv, g_out, g_ple, g_pg = gath
    zc = lambda n: jnp.zeros((D_MODEL, n), BF16)
    parts = [pc for lo, hi in _IN_KERNEL_ORDER for pc in _shard_cols(g_in, lo, hi)]
    parts += [zc(64)] + _shard_cols(g_in, *_IN_ROPE) + [zc(32)]
    w_in_p = jnp.concatenate(parts, axis=1)
    w_uq_p = jnp.pad(g_uq, ((0, 0), (0, 0), (0, 32))).transpose(1, 0, 2).reshape(Q_LORA, 1024)
    k_only = jnp.where(jnp.arange(LANES) < HEAD_DIM, g_ukv, jnp.zeros_like(g_ukv))
    w_uk_p = k_only.transpose(1, 0, 2).reshape(KV_LORA, 1024)
    w_uv = g_ukv[:, :, HEAD_DIM:].transpose(1, 0, 2).reshape(KV_LORA, D_GRP)
    w_ple = g_ple.transpose(1, 0, 2).reshape(PLE_DIM, D_MODEL)
    return (w_in_p, w_uq_p, w_uk_p, w_uv, g_out.reshape(D_MODEL, D_MODEL), w_ple,
            g_pg.reshape(D_MODEL, D_MODEL))


def _grad_payloads(d_cols, duq_blk, duk_blk, d_uv, d_out, dple_blk, d_pg):
    blocks = []
    for j in range(N_DEV):
        lo, hi = j * IN_SHARD, (j + 1) * IN_SHARD
        parts = []
        for o_lo, o_hi, idx, off in _IN_GRAD_SRC:
            a, b = max(lo, o_lo), min(hi, o_hi)
            if a < b:
                parts.append(d_cols[idx][:, off + a - o_lo:off + b - o_lo])
        blocks.append(jnp.concatenate(parts, axis=1))
    pay_in = jnp.stack(blocks)
    dv_blk = d_uv.reshape(KV_LORA, N_HEADS, HEAD_DIM).transpose(1, 0, 2)
    pay_ukv = jnp.concatenate([duk_blk[:, :, :HEAD_DIM], dv_blk], axis=2)
    return [pay_in, duq_blk, pay_ukv, d_out.reshape(N_DEV, 128, D_MODEL), dple_blk,
            d_pg.reshape(N_DEV, 128, D_MODEL)]


def kernel(x, p, positions, norm_pre_g, w_in, q_norm_g, w_uq, kv_norm_g, w_ukv, sb_out_norm_g, mla_out_norm_g, w_out, norm_post_g, w_ple, ple_norm_g, w_ple_gate, b_ple_gate, loss_target, m_norm_pre_g, m_w_in, m_q_norm_g, m_w_uq, m_kv_norm_g, m_w_ukv, m_sb_out_norm_g, m_mla_out_norm_g, m_w_out, m_norm_post_g, m_w_ple, m_ple_norm_g, m_w_ple_gate, m_b_ple_gate, v_norm_pre_g, v_w_in, v_q_norm_g, v_w_uq, v_kv_norm_g, v_w_ukv, v_sb_out_norm_g, v_mla_out_norm_g, v_w_out, v_norm_post_g, v_w_ple, v_ple_norm_g, v_w_ple_gate, v_b_ple_gate):
    mats = (w_in, w_uq, w_ukv, w_out, w_ple, w_ple_gate)
    m_mats = (m_w_in, m_w_uq, m_w_ukv, m_w_out, m_w_ple, m_w_ple_gate)
    v_mats = (v_w_in, v_w_uq, v_w_ukv, v_w_out, v_w_ple, v_w_ple_gate)
    vecs = (norm_pre_g, q_norm_g, kv_norm_g, sb_out_norm_g, mla_out_norm_g, norm_post_g, ple_norm_g, b_ple_gate)
    m_vecs = (m_norm_pre_g, m_q_norm_g, m_kv_norm_g, m_sb_out_norm_g, m_mla_out_norm_g, m_norm_post_g,
              m_ple_norm_g, m_b_ple_gate)
    v_vecs = (v_norm_pre_g, v_q_norm_g, v_kv_norm_g, v_sb_out_norm_g, v_mla_out_norm_g, v_norm_post_g,
              v_ple_norm_g, v_b_ple_gate)

    gath = _all_gather([a[0].astype(BF16) for a in mats])
    grad_x, d_parts, vec_slab = _local_grads(
        x[0], p[0, 0], positions[0], loss_target[0], *vecs, *_kernel_weights(gath))
    pays = _grad_payloads(*d_parts)
    landed1 = _pair_exchange(pays, vec_slab)
    place = jnp.stack([lax.axis_index("c"), 2 * lax.axis_index("x") + lax.axis_index("y")]).astype(jnp.int32)
    pair = [_pair_sum(g, l, place, "grad_pair_sum_%d" % o) for o, (g, l) in enumerate(zip(pays, landed1[:-1]))]
    landed2 = _chip_exchange([s for s, _ in pair])
    upd = [_adamw_matrix(own, l2, w[0], m[0], v[0], "adamw_%d" % o)
           for o, ((_, own), l2, w, m, v) in enumerate(zip(pair, landed2, mats, m_mats, v_mats))]
    sm = _adamw_vectors(landed1[-1], vecs, m_vecs, v_vecs)

    outs = []
    for kind in range(4):
        mat = [upd[o][kind][None] for o in range(len(mats))]
        vec = sm[1 + 8 * kind:9 + 8 * kind]
        outs += [vec[0], mat[0], vec[1], mat[1], vec[2], mat[2], vec[3], vec[4], mat[3], vec[5],
                 mat[4], vec[6], mat[5], vec[7]]
    return (sm[0][0, 0], grad_x[None], *outs)


def _local_grads(xs, ps, pos, tgt, norm_pre_g, q_norm_g, kv_norm_g, sb_out_norm_g, mla_out_norm_g,
                 norm_post_g, ple_norm_g, b_ple_gate, w_in_p, w_uq_p, w_uk_p, w_uv, f_out, f_ple, f_pg):
    s = xs.shape[0]

    half = ROPE_DIM // 2
    freq = ROPE_THETA ** (-jnp.arange(half, dtype=F32) / half)
    ang = pos.astype(F32)[:, None] * freq
    cos, sin = jnp.cos(ang), jnp.sin(ang)
    cos_t = jnp.concatenate([jnp.ones((s, 64), F32), cos, cos, jnp.zeros((s, 32), F32)], axis=1)
    sin_t = jnp.concatenate([jnp.zeros((s, 64), F32), -sin, sin, jnp.zeros((s, 32), F32)], axis=1)
    seg = jnp.arange(D_GRP) // HEAD_DIM
    bd = (seg[:, None] == seg[None, :]).astype(BF16)

    qkv, rest, h_b = _in_proj(xs, norm_pre_g, w_in_p)
    sb_o = _sb_fwd(qkv, 8)
    qp, kp, vv, cqn_b, ckvn_b = _mla_prep(rest, q_norm_g, kv_norm_g, w_uq_p, w_uk_p, w_uv, cos_t, sin_t)
    mla_o, lse = _mla_fwd(qp, kp, vv, 8)

    (dx1, d_sbo, d_mlo, d_sbg, d_mlg, x1_b, dgl_b, yc_b, dy_b, p_b, du_b, small_mid) = _mid(
        xs, ps, tgt, sb_o, mla_o, rest, sb_out_norm_g, mla_out_norm_g, f_out, norm_post_g,
        f_ple, ple_norm_g, f_pg, b_ple_gate, bd)
    dqp, dkp, dvv = _mla_bwd(qp, kp, vv, d_mlo, mla_o, lse, 4)
    dq_sb, dk_sb, dv_sb = _sb_bwd(qkv, d_sbo)
    dcq, dckv, dkr, dq_b, dk_b, dv_b, small_prep = _mla_prep_bwd(
        dqp, dkp, dvv, rest, q_norm_g, kv_norm_g, w_uq_p, w_uk_p, w_uv, cos_t, sin_t)
    pieces = [dq_sb, dk_sb, dv_sb, d_sbg, d_mlg, dcq, dckv, dkr]
    grad_x, small_in = _in_bwd(xs, norm_pre_g, dx1, pieces, w_in_p)

    d_cols = [_tn_matmul(h_b, pc, "dw_in_%d" % k) for k, pc in enumerate(pieces)]
    d_parts = (d_cols, _tn_matmul(cqn_b, dq_b, "dw_uq", blocked=True),
               _tn_matmul(ckvn_b, dk_b, "dw_uk", blocked=True), _tn_matmul(ckvn_b, dv_b, "dw_uv"),
               _tn_matmul(yc_b, dy_b, "dw_out"), _tn_matmul(p_b, du_b, "dw_ple", blocked=True),
               _tn_matmul(x1_b, dgl_b, "dw_pg"))
    slab = jnp.concatenate([small_in[0:1], jnp.pad(small_prep[0:2], ((0, 0), (0, D_MODEL - Q_LORA))),
                            small_mid[3:8]], axis=0)
    return grad_x, d_parts, slab
```

```python
import jax
import jax.numpy as jnp
from jax import lax
from jax.experimental import pallas as pl
from jax.experimental.pallas import tpu as pltpu

F32 = jnp.float32
BF16 = jnp.bfloat16
MESH = pl.DeviceIdType.MESH

N_DEV = 8
D_MODEL = 1024
N_HEADS = 8
HEAD_DIM = 64
D_GRP = N_HEADS * HEAD_DIM
Q_LORA = 256
KV_LORA = 128
ROPE_DIM = 32
PLE_DIM = 256
CHUNK_SHIFT = 6
ROPE_THETA = 10000.0
EPS = 1e-6
SB_SCALE = HEAD_DIM ** -0.5
MLA_SCALE = (HEAD_DIM + ROPE_DIM) ** -0.5
NEG = -1e30
LOG2_E = 1.4426950408889634
LN_2 = 0.6931471805599453
SB_CUTOFF = 110.0

ADAM_LR = 0.001
ADAM_B1 = 0.9
ADAM_B2 = 0.999
ADAM_EPS = 1e-08
ADAM_WD = 0.01
ADAM_STEP = 10

LANES = 128
TQ = 256
TK = 256
TM = 256

D_IN_P = 3072

_NT = (((1,), (1,)), ((), ()))
_TN = (((0,), (0,)), ((), ()))


def _params(sem, vmem_mb):
    return pltpu.CompilerParams(dimension_semantics=sem, vmem_limit_bytes=vmem_mb << 20)


def _dot(a, b):
    return jnp.dot(a, b, preferred_element_type=F32)


def _dot_nt(a, b):
    return lax.dot_general(a, b, _NT, preferred_element_type=F32)


def _dot_tn(a, b):
    return lax.dot_general(a, b, _TN, preferred_element_type=F32)


def _hl_dot(a, b):
    hi = a.astype(BF16)
    lo = (a - hi.astype(F32)).astype(BF16)
    return _dot(hi, b) + _dot(lo, b)


def _sigmoid(x):
    return 1.0 / (1.0 + jnp.exp(-x))


def _rope_swap(x, lane):
    left = pltpu.roll(x, LANES - 16, axis=1)
    right = pltpu.roll(x, 16, axis=1)
    lo = (lane >= 64) & (lane < 80)
    hi = (lane >= 80) & (lane < 96)
    return jnp.where(lo, left, jnp.where(hi, right, 0.0))


def _all_gather(shards):
    n_op = len(shards)

    def body(*refs):
        x_refs, out_refs = refs[:n_op], refs[n_op:2 * n_op]
        send_sems, recv_sems, local_sems = refs[2 * n_op:]
        x, y, c = lax.axis_index("x"), lax.axis_index("y"), lax.axis_index("c")
        me, sibling = (x, y, c), (x, y, 1 - c)
        chips = [(1 - x, y), (x, 1 - y), (1 - x, 1 - y)]

        def slot(o, px, py, pc):
            return out_refs[o].at[4 * px + 2 * py + pc]

        def copy(o, k, block, to, src=None):
            return pltpu.make_async_remote_copy(
                src_ref=slot(o, *block) if src is None else src, dst_ref=slot(o, *block),
                send_sem=send_sems.at[o, k], recv_sem=recv_sems.at[o, k],
                device_id=to, device_id_type=MESH)

        ops = range(n_op)
        mine = [pltpu.make_async_copy(x_refs[o], slot(o, *me), local_sems.at[o]) for o in ops]
        first = [copy(o, 0, me, sibling, src=x_refs[o]) for o in ops]
        first += [copy(o, 1 + j, me, (*chip, c), src=x_refs[o]) for j, chip in enumerate(chips) for o in ops]
        for cp in mine + first:
            cp.start()
        passed = []
        for j, chip in enumerate(chips):
            for o in ops:
                copy(o, 1 + j, (*chip, c), me).wait_recv()
                passed.append(copy(o, 4 + j, (*chip, c), sibling))
                passed[-1].start()
        for o in ops:
            copy(o, 0, sibling, me).wait_recv()
        for j, chip in enumerate(chips):
            for o in ops:
                copy(o, 4 + j, (*chip, 1 - c), me).wait_recv()
        for cp in first + passed:
            cp.wait_send()
        for cp in mine:
            cp.wait()

    vmem = pl.BlockSpec(memory_space=pltpu.VMEM)
    return pl.pallas_call(
        body, name="weight_all_gather",
        out_shape=[jax.ShapeDtypeStruct((N_DEV,) + a.shape, a.dtype) for a in shards],
        in_specs=[vmem] * n_op, out_specs=[vmem] * n_op,
        scratch_shapes=[pltpu.SemaphoreType.DMA((n_op, 7)), pltpu.SemaphoreType.DMA((n_op, 7)),
                        pltpu.SemaphoreType.DMA((n_op,))],
        compiler_params=pltpu.CompilerParams(vmem_limit_bytes=48 << 20),
    )(*shards)


def _pair_exchange(pays, small):
    n_op = len(pays)
    sr, n = small.shape

    def body(*refs):
        g_refs, s_ref = refs[:n_op], refs[n_op]
        l_refs, sland_ref = refs[n_op + 1:2 * n_op + 1], refs[2 * n_op + 1]
        ssem, rsem, ssem2, rsem2, lsem = refs[2 * n_op + 2:]
        x, y, c = lax.axis_index("x"), lax.axis_index("y"), lax.axis_index("c")
        me = 4 * x + 2 * y + c
        copies = []
        for o in range(n_op):
            for chip in range(4):
                copies.append(pltpu.make_async_remote_copy(
                    src_ref=g_refs[o].at[2 * chip + (1 - c)], dst_ref=l_refs[o].at[chip],
                    send_sem=ssem.at[o, chip], recv_sem=rsem.at[o, chip],
                    device_id=(x, y, 1 - c), device_id_type=MESH))
        for k in range(1, N_DEV):
            peer = (1 - x if (k >> 2) & 1 else x, 1 - y if (k >> 1) & 1 else y, 1 - c if k & 1 else c)
            copies.append(pltpu.make_async_remote_copy(
                src_ref=s_ref, dst_ref=sland_ref.at[me], send_sem=ssem2.at[k], recv_sem=rsem2.at[k],
                device_id=peer, device_id_type=MESH))
        own = pltpu.make_async_copy(s_ref, sland_ref.at[me], lsem)
        own.start()
        for cp in copies:
            cp.start()
        for cp in copies:
            cp.wait()
        own.wait()

    any_spec = pl.BlockSpec(memory_space=pl.ANY)
    return pl.pallas_call(
        body, name="grad_pair_exchange",
        out_shape=[jax.ShapeDtypeStruct((4,) + a.shape[1:], F32) for a in pays]
        + [jax.ShapeDtypeStruct((N_DEV, sr, n), F32)],
        in_specs=[any_spec] * (n_op + 1), out_specs=[any_spec] * (n_op + 1),
        scratch_shapes=[pltpu.SemaphoreType.DMA((n_op, 4)), pltpu.SemaphoreType.DMA((n_op, 4)),
                        pltpu.SemaphoreType.DMA((N_DEV,)), pltpu.SemaphoreType.DMA((N_DEV,)),
                        pltpu.SemaphoreType.DMA],
    )(*pays, small)


def _pair_sum(pay, landed, place, name):
    _, r, c = pay.shape

    def body(place_ref, g_ref, l_ref, s_ref, own_ref):
        i = pl.program_id(0)
        tot = g_ref[...] + l_ref[...]
        s_ref[...] = tot.astype(BF16)

        @pl.when(i == place_ref[1])
        def _():
            own_ref[...] = tot

    grid_spec = pltpu.PrefetchScalarGridSpec(
        num_scalar_prefetch=1, grid=(4,),
        in_specs=[pl.BlockSpec((None, r, c), lambda i, pr: (2 * i + pr[0], 0, 0)),
                  pl.BlockSpec((None, r, c), lambda i, pr: (i, 0, 0))],
        out_specs=[pl.BlockSpec((None, r, c), lambda i, pr: (i, 0, 0)),
                   pl.BlockSpec((r, c), lambda i, pr: (0, 0))])
    return pl.pallas_call(
        body, name=name, grid_spec=grid_spec,
        out_shape=[jax.ShapeDtypeStruct((4, r, c), BF16), jax.ShapeDtypeStruct((r, c), F32)],
        compiler_params=_params(("arbitrary",), 40),
    )(place, pay, landed)


def _chip_exchange(sums):
    n_op = len(sums)

    def body(*refs):
        s_refs, l_refs = refs[:n_op], refs[n_op:2 * n_op]
        ssem, rsem = refs[2 * n_op:]
        x, y, c = lax.axis_index("x"), lax.axis_index("y"), lax.axis_index("c")
        copies = []
        for rel in range(1, 4):
            px = 1 - x if rel & 2 else x
            py = 1 - y if rel & 1 else y
            for o in range(n_op):
                copies.append(pltpu.make_async_remote_copy(
                    src_ref=s_refs[o].at[2 * px + py], dst_ref=l_refs[o].at[rel - 1],
                    send_sem=ssem.at[o, rel - 1], recv_sem=rsem.at[o, rel - 1],
                    device_id=(px, py, c), device_id_type=MESH))
        for cp in copies:
            cp.start()
        for cp in copies:
            cp.wait()

    any_spec = pl.BlockSpec(memory_space=pl.ANY)
    return pl.pallas_call(
        body, name="grad_chip_exchange",
        out_shape=[jax.ShapeDtypeStruct((3,) + a.shape[1:], BF16) for a in sums],
        in_specs=[any_spec] * n_op, out_specs=[any_spec] * n_op,
        scratch_shapes=[pltpu.SemaphoreType.DMA((n_op, 3)), pltpu.SemaphoreType.DMA((n_op, 3))],
    )(*sums)


def _adamw_math(g, w, m, v):
    mn = ADAM_B1 * m + (1.0 - ADAM_B1) * g
    vn = ADAM_B2 * v + (1.0 - ADAM_B2) * (g * g)
    m_hat = mn / (1.0 - ADAM_B1 ** ADAM_STEP)
    v_hat = vn / (1.0 - ADAM_B2 ** ADAM_STEP)
    return -ADAM_LR * (m_hat / (jnp.sqrt(v_hat) + ADAM_EPS) + ADAM_WD * w), mn, vn


def _adamw_matrix(own, landed, w, m, v, name):
    r, c = w.shape
    cp = own.shape[1]
    br = min(r, 256)

    def body(own_ref, l_ref, w_ref, m_ref, v_ref, g_out, d_out, m_out, v_out):
        g = own_ref[...]
        for k in range(3):
            g = g + l_ref[k].astype(F32)
        g = g[:, :c]
        g_out[...] = g
        d_out[...], m_out[...], v_out[...] = _adamw_math(g, w_ref[...], m_ref[...], v_ref[...])

    row = pl.BlockSpec((br, c), lambda i: (i, 0))
    shp = jax.ShapeDtypeStruct((r, c), F32)
    return pl.pallas_call(
        body, name=name, grid=(r // br,),
        in_specs=[pl.BlockSpec((br, cp), lambda i: (i, 0)), pl.BlockSpec((3, br, cp), lambda i: (0, i, 0)),
                  row, row, row],
        out_specs=(row, row, row, row), out_shape=(shp, shp, shp, shp),
        compiler_params=_params(("parallel",), 40),
    )(own, landed, w, m, v)


_VEC_PLACE = ((0, 0), (1, 0), (2, 0), (3, 0), (3, D_GRP), (4, 0), (5, 0), (6, 0))


def _adamw_vectors(sland, ws, ms, vs):
    nv = len(ws)

    def body(l_ref, *refs):
        w_refs, m_refs, v_refs = refs[:nv], refs[nv:2 * nv], refs[2 * nv:3 * nv]
        loss_ref = refs[3 * nv]
        outs = refs[3 * nv + 1:]
        g_all = l_ref[0]
        for j in range(1, N_DEV):
            g_all = g_all + l_ref[j]
        loss_ref[...] = jnp.sum(g_all[7:8, :], axis=1, keepdims=True)
        for k, (row, lane0) in enumerate(_VEC_PLACE):
            n = w_refs[k].shape[1]
            g = g_all[row:row + 1, lane0:lane0 + n]
            d, mn, vn = _adamw_math(g, w_refs[k][...], m_refs[k][...], v_refs[k][...])
            outs[k][...] = g
            outs[nv + k][...] = d
            outs[2 * nv + k][...] = mn
            outs[3 * nv + k][...] = vn

    vmem = pl.BlockSpec(memory_space=pltpu.VMEM)
    shapes = [jax.ShapeDtypeStruct(w.shape, F32) for w in ws]
    return pl.pallas_call(
        body, name="adamw_vectors", in_specs=[vmem] * (1 + 3 * nv), out_specs=[vmem] * (1 + 4 * nv),
        out_shape=[jax.ShapeDtypeStruct((1, 1), F32)] + shapes * 4,
    )(sland, *ws, *ms, *vs)


def _in_proj(x, g, w):
    s = x.shape[0]

    def body(x_ref, g_ref, w_ref, qkv_ref, rest_ref, h_ref):
        xv = x_ref[...]
        r = lax.rsqrt(jnp.mean(xv * xv, axis=-1, keepdims=True) + EPS)
        h = ((xv * r) * g_ref[...]).astype(BF16)
        h_ref[...] = h
        qkv_ref[...] = _dot(h, w_ref[:, :1536]).astype(BF16)
        rest_ref[...] = _dot(h, w_ref[:, 1536:])

    return pl.pallas_call(
        body, name="in_proj", grid=(s // TM,),
        in_specs=[pl.BlockSpec((TM, D_MODEL), lambda i: (i, 0)),
                  pl.BlockSpec((1, D_MODEL), lambda i: (0, 0)),
                  pl.BlockSpec((D_MODEL, D_IN_P), lambda i: (0, 0))],
        out_specs=(pl.BlockSpec((TM, 1536), lambda i: (i, 0)),
                   pl.BlockSpec((TM, 1536), lambda i: (i, 0)),
                   pl.BlockSpec((TM, D_MODEL), lambda i: (i, 0))),
        out_shape=(jax.ShapeDtypeStruct((s, 1536), BF16), jax.ShapeDtypeStruct((s, 1536), F32),
                   jax.ShapeDtypeStruct((s, D_MODEL), BF16)),
        compiler_params=_params(("parallel",), 48),
    )(x, g, w)


def _mla_prep(rest, gq, gkv, wuq, wuk, wuv, cos_t, sin_t):
    s = rest.shape[0]

    def body(cq_ref, ckv_ref, kr_ref, gq_ref, gkv_ref, wuq_ref, wuk_ref, wuv_ref, c_ref, s_ref,
             qp_ref, kp_ref, vv_ref, cqn_ref, ckvn_ref):
        lane = lax.broadcasted_iota(jnp.int32, (1, LANES), 1)
        cos_v, sin_v = c_ref[...], s_ref[...]
        cq = cq_ref[...]
        rq = lax.rsqrt(jnp.mean(cq * cq, axis=-1, keepdims=True) + EPS)
        cqn = ((cq * rq) * gq_ref[...]).astype(BF16)
        cqn_ref[...] = cqn
        q = _dot(cqn, wuq_ref[...])
        ckv = ckv_ref[...]
        rkv = lax.rsqrt(jnp.mean(ckv * ckv, axis=-1, keepdims=True) + EPS)
        ckvn = ((ckv * rkv) * gkv_ref[...]).astype(BF16)
        ckvn_ref[...] = ckvn
        kn = _dot(ckvn, wuk_ref[...])
        vv_ref[...] = _dot(ckvn, wuv_ref[...]).astype(BF16)
        kr = kr_ref[...]
        kr_roped = kr * cos_v + _rope_swap(kr, lane) * sin_v
        for h in range(N_HEADS):
            sl = slice(h * LANES, (h + 1) * LANES)
            qh = q[:, sl]
            qp_ref[:, sl] = (qh * cos_v + _rope_swap(qh, lane) * sin_v).astype(BF16)
            kp_ref[:, sl] = (kn[:, sl] + kr_roped).astype(BF16)

    def row(width, idx):
        return pl.BlockSpec((TM, width), lambda i: (i, idx))

    def full(a):
        return pl.BlockSpec(a.shape, lambda i: (0, 0))

    return pl.pallas_call(
        body, name="mla_prep", grid=(s // TM,),
        in_specs=[row(Q_LORA, 4), row(KV_LORA, 10), row(LANES, 11), full(gq), full(gkv),
                  full(wuq), full(wuk), full(wuv), row(LANES, 0), row(LANES, 0)],
        out_specs=(row(1024, 0), row(1024, 0), row(D_GRP, 0), row(Q_LORA, 0), row(KV_LORA, 0)),
        out_shape=(jax.ShapeDtypeStruct((s, 1024), BF16), jax.ShapeDtypeStruct((s, 1024), BF16),
                   jax.ShapeDtypeStruct((s, D_GRP), BF16), jax.ShapeDtypeStruct((s, Q_LORA), BF16),
                   jax.ShapeDtypeStruct((s, KV_LORA), BF16)),
        compiler_params=_params(("parallel",), 32),
    )(rest, rest, rest, gq, gkv, wuq, wuk, wuv, cos_t, sin_t)


def _sb_live(n, qi, carries):
    top = carries[0]
    for c in carries[1:]:
        top = jnp.maximum(top, c)
    return jnp.logical_and(n < qi, jnp.max(top) > -SB_CUTOFF)


def _sb_fwd(qkv, hb):
    s = qkv.shape[0]

    def body(q_ref, k_ref, v_ref, o_ref, acc):
        qi = pl.program_id(1)
        lane = lax.broadcasted_iota(jnp.int32, (1, LANES), 1)
        is_a = lane < HEAD_DIM
        pair = lambda h: slice((h // 2) * LANES, (h // 2 + 1) * LANES)
        q_h = []
        for h in range(hb):
            qs = q_ref[:, pair(h)] * SB_SCALE
            mine = is_a if h % 2 == 0 else jnp.logical_not(is_a)
            q_h.append(jnp.where(mine, qs, jnp.zeros_like(qs)))
        r_i = lax.broadcasted_iota(jnp.int32, (TQ, TK), 0)
        c_i = lax.broadcasted_iota(jnp.int32, (TQ, TK), 1)
        past = c_i < r_i
        upper = (r_i > c_i).astype(BF16)
        acc[...] = jnp.zeros_like(acc)

        def tile(j, carries, diag):
            ks = pl.ds(pl.multiple_of(j * TK, TK), TK)
            zs = [_dot_nt(q_h[h], k_ref[ks, pair(h)]) for h in range(hb)]
            if diag:
                zs = [jnp.where(past, z, NEG) for z in zs]
            lfs = [-(jnp.maximum(z, 0.0) + jnp.log(1.0 + jnp.exp(-jnp.abs(z)))) for z in zs]
            sufs = [_hl_dot(lfs[h], upper) for h in range(hb)]
            out = []
            for h in range(hb):
                w = jnp.exp(zs[h] + lfs[h] + (sufs[h] + carries[h]))
                acc[h] += _dot(w.astype(BF16), v_ref[ks, pair(h)])
                out.append(carries[h] + jnp.sum(lfs[h], axis=1, keepdims=True))
            return tuple(out)

        zero = jnp.zeros((TQ, 1), F32)
        carries = tile(qi, (zero,) * hb, True)

        def step(st):
            return (st[0] + 1,) + tile(qi - 1 - st[0], st[1:], False)

        lax.while_loop(lambda st: _sb_live(st[0], qi, st[1:]), step, (0,) + carries)
        for pr in range(hb // 2):
            o_ref[:, pr * LANES:(pr + 1) * LANES] = jnp.where(is_a, acc[2 * pr], acc[2 * pr + 1])

    width = hb * HEAD_DIM
    nb = D_GRP // width
    slab = lambda part: pl.BlockSpec((s, width), lambda g, qi: (0, part * nb + g))
    blk = pl.BlockSpec((TQ, width), lambda g, qi: (qi, g))
    return pl.pallas_call(
        body, name="sb_fwd", grid=(nb, s // TQ),
        in_specs=[blk, slab(1), slab(2)], out_specs=blk,
        out_shape=jax.ShapeDtypeStruct((s, D_GRP), F32),
        scratch_shapes=[pltpu.VMEM((hb, TQ, LANES), F32)],
        compiler_params=_params(("arbitrary", "arbitrary"), 48),
    )(qkv, qkv, qkv)


def _sb_bwd(qkv, d_o):
    s = qkv.shape[0]
    nq = s // TQ
    nk = s // TK

    def body(q_ref, k_ref, v_ref, do_ref, dq_ref, dk_ref, dv_ref, x1s, bts, dqacc, dkacc, dvacc):
        qi = pl.program_id(1)
        lane = lax.broadcasted_iota(jnp.int32, (1, LANES), 1)
        is_a = lane < HEAD_DIM

        @pl.when(qi == 0)
        def _():
            dkacc[...] = jnp.zeros_like(dkacc)
            dvacc[...] = jnp.zeros_like(dvacc)

        qs = q_ref[...] * SB_SCALE
        zq = jnp.zeros_like(qs)
        qs_x = (jnp.where(is_a, qs, zq), jnp.where(is_a, zq, qs))
        dob = do_ref[...].astype(BF16)
        do_x = (jnp.where(is_a, dob, zq), jnp.where(is_a, zq, dob))
        r_i = lax.broadcasted_iota(jnp.int32, (TQ, TK), 0)
        c_i = lax.broadcasted_iota(jnp.int32, (TQ, TK), 1)
        past = c_i < r_i
        upper = (r_i > c_i).astype(BF16)
        upper_incl = (r_i >= c_i).astype(BF16)
        dqacc[...] = jnp.zeros_like(dqacc)
        both = ((0, 0), (0, 1), (1, 0), (1, 1))

        def tiles(n):
            j_hi = qi - 2 * n
            lo_ok = j_hi >= 1
            j_lo = jnp.maximum(j_hi - 1, 0)
            ks = (pl.ds(pl.multiple_of(j_hi * TK, TK), TK), pl.ds(pl.multiple_of(j_lo * TK, TK), TK))
            return j_hi, lo_ok, j_lo, ks

        def sweep(n, carries):
            j_hi, lo_ok, j_lo, ks = tiles(n)
            slot = (j_hi, jnp.where(lo_ok, j_lo, nk))
            valid = (jnp.logical_or(past, j_hi < qi), lo_ok)
            z = {th: jnp.where(valid[th[0]], _dot_nt(qs_x[th[1]], k_ref[ks[th[0]], :]), NEG) for th in both}
            d_a = {th: _dot_nt(do_x[th[1]], v_ref[ks[th[0]], :]) for th in both}
            lf, beta, omb = {}, {}, {}
            for th in both:
                e = jnp.exp(-jnp.abs(z[th]))
                den = 1.0 + e
                rden = 1.0 / den
                pos = z[th] >= 0.0
                lf[th] = -(jnp.maximum(z[th], 0.0) + jnp.log(den))
                beta[th] = jnp.where(pos, rden, e * rden)
                omb[th] = jnp.where(pos, e * rden, rden)
            suf = {th: _hl_dot(lf[th], upper) for th in both}
            c, g_in = {}, {}
            for h in range(2):
                c[0, h], g_in[0, h] = carries[2 * h], carries[2 * h + 1]
                c[1, h] = c[0, h] + jnp.sum(lf[0, h], axis=1, keepdims=True)
            a, g = {}, {}
            for th in both:
                a[th] = jnp.exp(z[th] + lf[th] + (suf[th] + c[th]))
                g[th] = a[th] * d_a[th]
            sg = {th: _hl_dot(g[th], upper_incl) for th in both}
            for h in range(2):
                g_in[1, h] = g_in[0, h] + jnp.sum(g[0, h], axis=1, keepdims=True)
            for th in both:
                t, h = th
                x1s[slot[t], h] = g[th] * omb[th] + beta[th] * (sg[th] + g_in[th])
                bts[slot[t], h] = beta[th]
                dvacc[ks[t], :] += _dot_tn(a[th].astype(BF16), do_x[h])
            out = []
            for h in range(2):
                out.append(c[1, h] + jnp.sum(lf[1, h], axis=1, keepdims=True))
                out.append(g_in[1, h] + jnp.sum(g[1, h], axis=1, keepdims=True))
            return tuple(out)

        zero = jnp.zeros((TQ, 1), F32)
        first = sweep(0, (zero, zero, zero, zero))

        def more(st):
            return jnp.logical_and(2 * st[0] <= qi, jnp.max(jnp.maximum(st[1], st[3])) > -SB_CUTOFF)

        swept = lax.while_loop(more, lambda st: (st[0] + 1,) + sweep(st[0], st[1:]), (1,) + first)
        g_tot = (swept[2], swept[4])

        def apply(n, carry):
            j_hi, lo_ok, j_lo, ks = tiles(n)

            def one(j, kslice):
                for h in range(2):
                    dz = (x1s[j, h] - bts[j, h] * g_tot[h]).astype(BF16)
                    dqacc[h] += _dot(dz, k_ref[kslice, :])
                    dkacc[kslice, :] += _dot_tn(dz, qs_x[h])

            one(j_hi, ks[0])

            @pl.when(lo_ok)
            def _():
                one(j_lo, ks[1])

            return carry

        lax.fori_loop(0, swept[0], apply, 0)
        dq_ref[...] = (jnp.where(is_a, dqacc[0], dqacc[1]) * SB_SCALE).astype(BF16)

        @pl.when(qi == nq - 1)
        def _():
            dk_ref[...] = dkacc[...].astype(BF16)
            dv_ref[...] = dvacc[...].astype(BF16)

    slab = lambda off: pl.BlockSpec((s, LANES), lambda p, qi: (0, off + p))
    blk = pl.BlockSpec((TQ, LANES), lambda p, qi: (qi, p))
    out_slab = pl.BlockSpec((s, LANES), lambda p, qi: (0, p))
    shp = jax.ShapeDtypeStruct((s, D_GRP), BF16)
    return pl.pallas_call(
        body, name="sb_bwd", grid=(4, nq),
        in_specs=[blk, slab(4), slab(8), blk],
        out_specs=(blk, out_slab, out_slab), out_shape=(shp, shp, shp),
        scratch_shapes=[pltpu.VMEM((nk + 1, 2, TQ, TK), F32)] * 2
        + [pltpu.VMEM((2, TQ, LANES), F32), pltpu.VMEM((s, LANES), F32), pltpu.VMEM((s, LANES), F32)],
        compiler_params=_params(("arbitrary", "arbitrary"), 56),
    )(qkv, qkv, qkv, d_o)


def _mla_fwd(qp, kp, vv, hb):
    s = qp.shape[0]
    c2 = MLA_SCALE * LOG2_E

    def body(q_ref, k_ref, v_ref, o_ref, lse_ref, vaug, mrun, mb, acc, zbuf):
        qi = pl.program_id(1)
        lane = lax.broadcasted_iota(jnp.int32, (1, LANES), 1)
        is_a = lane < HEAD_DIM

        @pl.when(qi == 0)
        def _():
            for h in range(hb):
                vp = v_ref[:, (h // 2) * LANES:(h // 2 + 1) * LANES]
                mine = is_a if h % 2 == 0 else jnp.logical_not(is_a)
                vaug[h] = jnp.where(mine, vp, jnp.ones_like(vp))

        r_i = lax.broadcasted_iota(jnp.int32, (TQ, TK), 0)
        c_i = lax.broadcasted_iota(jnp.int32, (TQ, TK), 1)
        visible = (c_i >> CHUNK_SHIFT) <= (r_i >> CHUNK_SHIFT)

        def key_rows(j):
            return pl.ds(pl.multiple_of(j * TK, TK), TK)

        def sweep(tiles):
            def loop(n, carry):
                tiles(((2 * n, False), (2 * n + 1, False)))
                return carry

            lax.fori_loop(0, qi // 2, loop, 0)

            @pl.when(qi % 2 == 1)
            def _():
                tiles(((qi - 1, False), (qi, True)))

            @pl.when(qi % 2 == 0)
            def _():
                tiles(((qi, True),))

        mrun[...] = jnp.full_like(mrun, NEG)

        def tiles_max(js):
            zs = [[_dot_nt(q_ref[:, h * LANES:(h + 1) * LANES], k_ref[key_rows(j), h * LANES:(h + 1) * LANES])
                   for h in range(hb)] for j, _ in js]
            for t, (j, diag) in enumerate(js):
                for h in range(hb):
                    z = jnp.where(visible, zs[t][h], NEG) if diag else zs[t][h]
                    zbuf[j, h] = z
                    mrun[h] = jnp.maximum(mrun[h], z)

        sweep(tiles_max)
        for h in range(hb):
            m = jnp.max(mrun[h], axis=1, keepdims=True) * c2
            mb[h] = jnp.broadcast_to(m, (TQ, TK))
        acc[...] = jnp.zeros_like(acc)

        def tiles_pv(js):
            ps = [[jnp.exp2(zbuf[j, h] * c2 - mb[h]).astype(BF16) for h in range(hb)] for j, _ in js]
            for t, (j, _) in enumerate(js):
                for h in range(hb):
                    acc[h] += _dot(ps[t][h], vaug[h, key_rows(j), :])

        sweep(tiles_pv)
        for pr in range(hb // 2):
            a, b = 2 * pr, 2 * pr + 1
            psl = slice(pr * LANES, (pr + 1) * LANES)
            acc_a, acc_b = acc[a], acc[b]
            l_a = pltpu.roll(acc_a, HEAD_DIM, axis=1)
            l_b = pltpu.roll(acc_b, HEAD_DIM, axis=1)
            o_ref[:, psl] = jnp.where(is_a, acc_a * (1.0 / l_a), acc_b * (1.0 / l_b))
            lse_ref[:, psl] = jnp.where(is_a, mb[a, :, :LANES] * LN_2 + jnp.log(l_a),
                                        mb[b, :, :LANES] * LN_2 + jnp.log(l_b))

    blk = pl.BlockSpec((TQ, hb * HEAD_DIM), lambda g, qi: (qi, g))
    shp = jax.ShapeDtypeStruct((s, D_GRP), F32)
    return pl.pallas_call(
        body, name="mla_fwd", grid=(N_HEADS // hb, s // TQ),
        in_specs=[pl.BlockSpec((TQ, hb * LANES), lambda g, qi: (qi, g)),
                  pl.BlockSpec((s, hb * LANES), lambda g, qi: (0, g)),
                  pl.BlockSpec((s, hb * HEAD_DIM), lambda g, qi: (0, g))],
        out_specs=(blk, blk), out_shape=(shp, shp),
        scratch_shapes=[pltpu.VMEM((hb, s, LANES), BF16), pltpu.VMEM((hb, TQ, TK), F32),
                        pltpu.VMEM((hb, TQ, TK), F32), pltpu.VMEM((hb, TQ, LANES), F32),
                        pltpu.VMEM((s // TK, hb, TQ, TK), F32)],
        compiler_params=_params(("arbitrary", "arbitrary"), 56),
    )(qp, kp, vv)


def _mla_bwd(qp, kp, vv, d_o, o, lse, hb):
    s = qp.shape[0]
    c2 = MLA_SCALE * LOG2_E

    def body(q_ref, k_ref, v_ref, do_ref, o_ref, lse_ref, dq_ref, dk_ref, dv_ref, dqacc, lse_b, delta_b):
        qi = pl.program_id(1)
        lane = lax.broadcasted_iota(jnp.int32, (1, LANES), 1)
        is_a = lane < HEAD_DIM

        @pl.when(qi == 0)
        def _():
            dk_ref[...] = jnp.zeros_like(dk_ref)
            dv_ref[...] = jnp.zeros_like(dv_ref)

        r_i = lax.broadcasted_iota(jnp.int32, (TQ, TK), 0)
        c_i = lax.broadcasted_iota(jnp.int32, (TQ, TK), 1)
        visible = (c_i >> CHUNK_SHIFT) <= (r_i >> CHUNK_SHIFT)
        do_x = []
        for h in range(hb):
            psl = slice((h // 2) * LANES, (h // 2 + 1) * LANES)
            mine = is_a if h % 2 == 0 else jnp.logical_not(is_a)
            d_o = do_ref[:, psl]
            delta = jnp.sum(jnp.where(mine, d_o * o_ref[:, psl], 0.0), axis=1, keepdims=True)
            lse_h = jnp.sum(jnp.where(lane == (h % 2) * HEAD_DIM, lse_ref[:, psl], 0.0), axis=1, keepdims=True)
            lse_b[h] = jnp.broadcast_to(lse_h * LOG2_E, (TQ, TK))
            delta_b[h] = jnp.broadcast_to(delta, (TQ, TK))
            do_x.append(jnp.where(mine, d_o, 0.0).astype(BF16))
        dqacc[...] = jnp.zeros_like(dqacc)

        head = lambda h: slice(h * LANES, (h + 1) * LANES)
        pair = lambda h: slice((h // 2) * LANES, (h // 2 + 1) * LANES)

        def tiles(js):
            th = [(j, diag, pl.ds(pl.multiple_of(j * TK, TK), TK), h) for j, diag in js for h in range(hb)]
            zs = [_dot_nt(q_ref[:, head(h)], k_ref[ks, head(h)]) for _, _, ks, h in th]
            dps = [_dot_nt(do_x[h], v_ref[ks, pair(h)]) for _, _, ks, h in th]
            for i, (j, diag, ks, h) in enumerate(th):
                e = zs[i] * c2 - lse_b[h]
                if diag:
                    e = jnp.where(visible, e, NEG)
                p = jnp.exp2(e)
                ds = (p * (dps[i] - delta_b[h]) * MLA_SCALE).astype(BF16)
                dqacc[h] += _dot(ds, k_ref[ks, head(h)])
                dk_ref[ks, head(h)] += _dot_tn(ds, q_ref[:, head(h)])
                dv_ref[ks, pair(h)] += _dot_tn(p.astype(BF16), do_x[h])

        def loop(n, c):
            tiles(((2 * n, False), (2 * n + 1, False)))
            return c

        lax.fori_loop(0, qi // 2, loop, 0)

        @pl.when(qi % 2 == 1)
        def _():
            tiles(((qi - 1, False), (qi, True)))

        @pl.when(qi % 2 == 0)
        def _():
            tiles(((qi, True),))

        for h in range(hb):
            dq_ref[:, h * LANES:(h + 1) * LANES] = dqacc[h]

    blk = pl.BlockSpec((TQ, hb * HEAD_DIM), lambda g, qi: (qi, g))
    return pl.pallas_call(
        body, name="mla_bwd", grid=(N_HEADS // hb, s // TQ),
        in_specs=[pl.BlockSpec((TQ, hb * LANES), lambda g, qi: (qi, g)),
                  pl.BlockSpec((s, hb * LANES), lambda g, qi: (0, g)),
                  pl.BlockSpec((s, hb * HEAD_DIM), lambda g, qi: (0, g)), blk, blk, blk],
        out_specs=(pl.BlockSpec((TQ, hb * LANES), lambda g, qi: (qi, g)),
                   pl.BlockSpec((s, hb * LANES), lambda g, qi: (0, g)),
                   pl.BlockSpec((s, hb * HEAD_DIM), lambda g, qi: (0, g))),
        out_shape=(jax.ShapeDtypeStruct((s, 1024), F32), jax.ShapeDtypeStruct((s, 1024), F32),
                   jax.ShapeDtypeStruct((s, D_GRP), F32)),
        scratch_shapes=[pltpu.VMEM((hb, TQ, LANES), F32), pltpu.VMEM((hb, TQ, TK), F32),
                        pltpu.VMEM((hb, TQ, TK), F32)],
        compiler_params=_params(("arbitrary", "arbitrary"), 56),
    )(qp, kp, vv, d_o, o, lse)


def _mid(x, p, target, sb_o, mla_o, rest, g_sb, g_mla, w_out, g_post, w_ple, g_ple, w_pg, b_pg, bd):
    s = x.shape[0]

    def body(x_ref, p_ref, t_ref, sbo_ref, mlo_ref, sbg_ref, mlg_ref, gsb_ref, gml_ref, wout_ref,
             gpost_ref, wple_ref, gple_ref, wpg_ref, bpg_ref, bd_ref,
             dx1_ref, dsbo_ref, dmlo_ref, dsbg_ref, dmlg_ref, x1b_ref, dglb_ref, ycb_ref, dyb_ref,
             pb_ref, dub_ref, small_ref):
        i = pl.program_id(0)
        bd_m = bd_ref[...]

        def seg_mean(v):
            return _dot(v.astype(BF16), bd_m) * (1.0 / HEAD_DIM)

        groups = []
        for o_ref, gate_ref, gain_ref in ((sbo_ref, sbg_ref, gsb_ref), (mlo_ref, mlg_ref, gml_ref)):
            o = o_ref[...]
            r = lax.rsqrt(seg_mean(o * o) + EPS)
            n = o * r
            hn = n * gain_ref[...]
            gate = gate_ref[...]
            sg = _sigmoid(gate)
            si = gate * sg
            groups.append((r, n, hn, gate, sg, si, gain_ref[...]))
        ya = (groups[0][2] * groups[0][5]).astype(BF16)
        yb = (groups[1][2] * groups[1][5]).astype(BF16)
        ycb_ref[:, :D_GRP] = ya
        ycb_ref[:, D_GRP:] = yb
        y = _dot(ya, wout_ref[:D_GRP, :]) + _dot(yb, wout_ref[D_GRP:, :])
        ry = lax.rsqrt(jnp.mean(y * y, axis=-1, keepdims=True) + EPS)
        ny = y * ry
        x1 = x_ref[...] + ny * gpost_ref[...]
        x1b = x1.astype(BF16)
        x1b_ref[...] = x1b
        pb = p_ref[...].astype(BF16)
        pb_ref[...] = pb
        u = _dot(pb, wple_ref[...])
        ru = lax.rsqrt(jnp.mean(u * u, axis=-1, keepdims=True) + EPS)
        nu = u * ru
        ple = nu * gple_ref[...]
        gate = _sigmoid(_dot(x1b, wpg_ref[...]) + bpg_ref[...])
        x2 = x1 + ple * gate
        diff = x2 - t_ref[...]
        dx2 = diff * (1.0 / D_MODEL)

        d_ple = dx2 * gate
        d_glin = (dx2 * ple) * (gate * (1.0 - gate))
        dglb = d_glin.astype(BF16)
        dglb_ref[...] = dglb
        dx1 = dx2 + _dot_nt(dglb, wpg_ref[...])
        dx1_ref[...] = dx1
        d_nu = d_ple * gple_ref[...]
        d_u = ru * (d_nu - nu * jnp.mean(d_nu * nu, axis=-1, keepdims=True))
        dub_ref[...] = d_u.astype(BF16)
        d_ny = dx1 * gpost_ref[...]
        d_y = ry * (d_ny - ny * jnp.mean(d_ny * ny, axis=-1, keepdims=True))
        dyb = d_y.astype(BF16)
        dyb_ref[...] = dyb
        d_yc = (_dot_nt(dyb, wout_ref[:D_GRP, :]), _dot_nt(dyb, wout_ref[D_GRP:, :]))

        d_gain = []
        for gx, (do_ref, dg_ref) in enumerate(((dsbo_ref, dsbg_ref), (dmlo_ref, dmlg_ref))):
            r, n, hn, gate_g, sg, si, gain = groups[gx]
            dyg = d_yc[gx]
            d_hn = dyg * si
            dg_ref[...] = (dyg * hn * (sg * (1.0 + gate_g * (1.0 - sg)))).astype(BF16)
            d_gain.append(jnp.sum(d_hn * n, axis=0, keepdims=True))
            d_n = d_hn * gain
            do_ref[...] = r * (d_n - n * seg_mean(d_n * n))

        @pl.when(i == 0)
        def _():
            small_ref[...] = jnp.zeros_like(small_ref)

        small_ref[3:4, :D_GRP] += d_gain[0]
        small_ref[3:4, D_GRP:] += d_gain[1]
        small_ref[4:5, :] += jnp.sum(dx1 * ny, axis=0, keepdims=True)
        small_ref[5:6, :] += jnp.sum(d_ple * nu, axis=0, keepdims=True)
        small_ref[6:7, :] += jnp.sum(d_glin, axis=0, keepdims=True)
        small_ref[7:8, :] += jnp.sum(diff * diff, axis=0, keepdims=True) * (0.5 / D_MODEL)

    def row(width, idx=0):
        return pl.BlockSpec((TM, width), lambda i: (i, idx))

    def full(a):
        return pl.BlockSpec(a.shape, lambda i: (0, 0))

    f32 = lambda w: jax.ShapeDtypeStruct((s, w), F32)
    b16 = lambda w: jax.ShapeDtypeStruct((s, w), BF16)
    return pl.pallas_call(
        body, name="mid", grid=(s // TM,),
        in_specs=[row(D_MODEL), row(PLE_DIM), row(D_MODEL), row(D_GRP), row(D_GRP),
                  row(D_GRP, 0), row(D_GRP, 1), full(g_sb), full(g_mla), full(w_out), full(g_post),
                  full(w_ple), full(g_ple), full(w_pg), full(b_pg), full(bd)],
        out_specs=(row(D_MODEL), row(D_GRP), row(D_GRP), row(D_GRP), row(D_GRP), row(D_MODEL),
                   row(D_MODEL), row(D_MODEL), row(D_MODEL), row(PLE_DIM), row(D_MODEL),
                   pl.BlockSpec((8, D_MODEL), lambda i: (0, 0))),
        out_shape=(f32(D_MODEL), f32(D_GRP), f32(D_GRP), b16(D_GRP), b16(D_GRP), b16(D_MODEL),
                   b16(D_MODEL), b16(D_MODEL), b16(D_MODEL), b16(PLE_DIM), b16(D_MODEL),
                   jax.ShapeDtypeStruct((8, D_MODEL), F32)),
        compiler_params=_params(("arbitrary",), 56),
    )(x, p, target, sb_o, mla_o, rest, rest, g_sb, g_mla, w_out, g_post, w_ple, g_ple, w_pg, b_pg, bd)


def _mla_prep_bwd(dqp, dkp, dvv, rest, gq, gkv, wuq, wuk, wuv, cos_t, sin_t):
    s = rest.shape[0]

    def body(dqp_ref, dkp_ref, dvv_ref, cq_ref, ckv_ref, gq_ref, gkv_ref, wuq_ref, wuk_ref, wuv_ref,
             c_ref, s_ref, dcq_ref, dckv_ref, dkr_ref, dqb_ref, dkb_ref, dvb_ref, small_ref):
        i = pl.program_id(0)
        lane = lax.broadcasted_iota(jnp.int32, (1, LANES), 1)
        in_rope = (lane >= HEAD_DIM) & (lane < HEAD_DIM + ROPE_DIM)
        cos_v, sin_v = c_ref[...], s_ref[...]
        dkr_roped = jnp.zeros((TM, LANES), F32)
        for h in range(N_HEADS):
            sl = slice(h * LANES, (h + 1) * LANES)
            dy = dqp_ref[:, sl]
            dqb_ref[:, sl] = (dy * cos_v + _rope_swap(dy * sin_v, lane)).astype(BF16)
            dkh = dkp_ref[:, sl]
            dkb_ref[:, sl] = dkh.astype(BF16)
            dkr_roped = dkr_roped + jnp.where(in_rope, dkh, 0.0)
        dkr_ref[...] = (dkr_roped * cos_v + _rope_swap(dkr_roped * sin_v, lane)).astype(BF16)
        dvb = dvv_ref[...].astype(BF16)
        dvb_ref[...] = dvb

        cq = cq_ref[...]
        rq = lax.rsqrt(jnp.mean(cq * cq, axis=-1, keepdims=True) + EPS)
        nq_ = cq * rq
        d_cqn = _dot_nt(dqb_ref[...], wuq_ref[...])
        d_n = d_cqn * gq_ref[...]
        dcq_ref[...] = (rq * (d_n - nq_ * jnp.mean(d_n * nq_, axis=-1, keepdims=True))).astype(BF16)

        ckv = ckv_ref[...]
        rkv = lax.rsqrt(jnp.mean(ckv * ckv, axis=-1, keepdims=True) + EPS)
        nkv = ckv * rkv
        d_ckvn = _dot_nt(dkb_ref[...], wuk_ref[...]) + _dot_nt(dvb, wuv_ref[...])
        d_n2 = d_ckvn * gkv_ref[...]
        dckv_ref[...] = (rkv * (d_n2 - nkv * jnp.mean(d_n2 * nkv, axis=-1, keepdims=True))).astype(BF16)

        @pl.when(i == 0)
        def _():
            small_ref[...] = jnp.zeros_like(small_ref)

        small_ref[0:1, :] += jnp.sum(d_cqn * nq_, axis=0, keepdims=True)
        small_ref[1:2, :KV_LORA] += jnp.sum(d_ckvn * nkv, axis=0, keepdims=True)

    def row(width, idx=0):
        return pl.BlockSpec((TM, width), lambda i: (i, idx))

    def full(a):
        return pl.BlockSpec(a.shape, lambda i: (0, 0))

    b16 = lambda w: jax.ShapeDtypeStruct((s, w), BF16)
    return pl.pallas_call(
        body, name="mla_prep_bwd", grid=(s // TM,),
        in_specs=[row(1024), row(1024), row(D_GRP), row(Q_LORA, 4), row(KV_LORA, 10), full(gq), full(gkv),
                  full(wuq), full(wuk), full(wuv), row(LANES), row(LANES)],
        out_specs=(row(Q_LORA), row(KV_LORA), row(LANES), row(1024), row(1024), row(D_GRP),
                   pl.BlockSpec((8, Q_LORA), lambda i: (0, 0))),
        out_shape=(b16(Q_LORA), b16(KV_LORA), b16(LANES), b16(1024), b16(1024), b16(D_GRP),
                   jax.ShapeDtypeStruct((8, Q_LORA), F32)),
        compiler_params=_params(("arbitrary",), 40),
    )(dqp, dkp, dvv, rest, rest, gq, gkv, wuq, wuk, wuv, cos_t, sin_t)


def _in_bwd(x, g, dx1, pieces, w):
    s = x.shape[0]
    widths = [a.shape[1] for a in pieces]
    offs = [sum(widths[:k]) for k in range(len(widths))]

    def body(x_ref, g_ref, dx1_ref, *refs):
        piece_refs = refs[:len(pieces)]
        w_ref, dx_ref, small_ref = refs[len(pieces):]
        i = pl.program_id(0)
        dh = jnp.zeros((TM, D_MODEL), F32)
        for pr, off, wd in zip(piece_refs, offs, widths):
            dh = dh + _dot_nt(pr[...], w_ref[:, off:off + wd])
        xv = x_ref[...]
        r = lax.rsqrt(jnp.mean(xv * xv, axis=-1, keepdims=True) + EPS)
        n = xv * r
        d_n = dh * g_ref[...]
        dx_ref[...] = dx1_ref[...] + r * (d_n - n * jnp.mean(d_n * n, axis=-1, keepdims=True))

        @pl.when(i == 0)
        def _():
            small_ref[...] = jnp.zeros_like(small_ref)

        small_ref[0:1, :] += jnp.sum(dh * n, axis=0, keepdims=True)

    def row(width):
        return pl.BlockSpec((TM, width), lambda i: (i, 0))

    return pl.pallas_call(
        body, name="in_bwd", grid=(s // TM,),
        in_specs=[row(D_MODEL), pl.BlockSpec((1, D_MODEL), lambda i: (0, 0)), row(D_MODEL)]
        + [row(wd) for wd in widths] + [pl.BlockSpec(w.shape, lambda i: (0, 0))],
        out_specs=(row(D_MODEL), pl.BlockSpec((8, D_MODEL), lambda i: (0, 0))),
        out_shape=(jax.ShapeDtypeStruct((s, D_MODEL), F32), jax.ShapeDtypeStruct((8, D_MODEL), F32)),
        compiler_params=_params(("arbitrary",), 48),
    )(x, g, dx1, *pieces, w)


def _tn_matmul(a, b, name, blocked=False):
    s, k = a.shape
    n = b.shape[1]
    ts = 512
    tn = n if blocked else min(n, 512)
    steps = s // ts

    def body(a_ref, b_ref, o_ref):
        t = pl.program_id(1)

        @pl.when(t == 0)
        def _():
            o_ref[...] = jnp.zeros_like(o_ref)

        prod = _dot_tn(a_ref[...], b_ref[...])
        if blocked:
            for j in range(n // LANES):
                o_ref[j] += prod[:, j * LANES:(j + 1) * LANES]
        else:
            o_ref[...] += prod

    if blocked:
        out_spec = pl.BlockSpec((n // LANES, k, LANES), lambda j, t: (0, 0, 0))
        out_shape = jax.ShapeDtypeStruct((n // LANES, k, LANES), F32)
    else:
        out_spec = pl.BlockSpec((k, tn), lambda j, t: (0, j))
        out_shape = jax.ShapeDtypeStruct((k, n), F32)
    return pl.pallas_call(
        body, name=name, grid=(n // tn, steps),
        in_specs=[pl.BlockSpec((ts, k), lambda j, t: (t, 0)), pl.BlockSpec((ts, tn), lambda j, t: (t, j))],
        out_specs=out_spec, out_shape=out_shape,
        compiler_params=_params(("parallel", "arbitrary"), 40),
    )(a, b)


IN_SHARD = 372
_IN_KERNEL_ORDER = ((0, 2048), (2464, 2976), (2048, 2432))
_IN_ROPE = (2432, 2464)
_IN_GRAD_SRC = ((0, 512, 0, 0), (512, 1024, 1, 0), (1024, 1536, 2, 0), (1536, 2048, 3, 0),
                (2048, 2304, 5, 0), (2304, 2432, 6, 0), (2432, 2464, 7, 64), (2464, 2976, 4, 0))


def _shard_cols(gath_in, lo, hi):
    out = []
    while lo < hi:
        j, a = divmod(lo, IN_SHARD)
        b = min(IN_SHARD, a + hi - lo)
        out.append(gath_in[j][:, a:b])
        lo += b - a
    return out


def _kernel_weights(gath):
    g_in, g_uq, g_ukv, g_out, g_ple, g_pg = gath
    zc = lambda n: jnp.zeros((D_MODEL, n), BF16)
    parts = [pc for lo, hi in _IN_KERNEL_ORDER for pc in _shard_cols(g_in, lo, hi)]
    parts += [zc(64)] + _shard_cols(g_in, *_IN_ROPE) + [zc(32)]
    w_in_p = jnp.concatenate(parts, axis=1)
    w_uq_p = jnp.pad(g_uq, ((0, 0), (0, 0), (0, 32))).transpose(1, 0, 2).reshape(Q_LORA, 1024)
    k_only = jnp.where(jnp.arange(LANES) < HEAD_DIM, g_ukv, jnp.zeros_like(g_ukv))
    w_uk_p = k_only.transpose(1, 0, 2).reshape(KV_LORA, 1024)
    w_uv = g_ukv[:, :, HEAD_DIM:].transpose(1, 0, 2).reshape(KV_LORA, D_GRP)
    w_ple = g_ple.transpose(1, 0, 2).reshape(PLE_DIM, D_MODEL)
    return (w_in_p, w_uq_p, w_uk_p, w_uv, g_out.reshape(D_MODEL, D_MODEL), w_ple,
            g_pg.reshape(D_MODEL, D_MODEL))


def _grad_payloads(d_cols, duq_blk, duk_blk, d_uv, d_out, dple_blk, d_pg):
    blocks = []
    for j in range(N_DEV):
        lo, hi = j * IN_SHARD, (j + 1) * IN_SHARD
        parts = []
        for o_lo, o_hi, idx, off in _IN_GRAD_SRC:
            a, b = max(lo, o_lo), min(hi, o_hi)
            if a < b:
                parts.append(d_cols[idx][:, off + a - o_lo:off + b - o_lo])
        blocks.append(jnp.concatenate(parts, axis=1))
    pay_in = jnp.stack(blocks)
    dv_blk = d_uv.reshape(KV_LORA, N_HEADS, HEAD_DIM).transpose(1, 0, 2)
    pay_ukv = jnp.concatenate([duk_blk[:, :, :HEAD_DIM], dv_blk], axis=2)
    return [pay_in, duq_blk, pay_ukv, d_out.reshape(N_DEV, 128, D_MODEL), dple_blk,
            d_pg.reshape(N_DEV, 128, D_MODEL)]


def kernel(x, p, positions, norm_pre_g, w_in, q_norm_g, w_uq, kv_norm_g, w_ukv, sb_out_norm_g, mla_out_norm_g, w_out, norm_post_g, w_ple, ple_norm_g, w_ple_gate, b_ple_gate, loss_target, m_norm_pre_g, m_w_in, m_q_norm_g, m_w_uq, m_kv_norm_g, m_w_ukv, m_sb_out_norm_g, m_mla_out_norm_g, m_w_out, m_norm_post_g, m_w_ple, m_ple_norm_g, m_w_ple_gate, m_b_ple_gate, v_norm_pre_g, v_w_in, v_q_norm_g, v_w_uq, v_kv_norm_g, v_w_ukv, v_sb_out_norm_g, v_mla_out_norm_g, v_w_out, v_norm_post_g, v_w_ple, v_ple_norm_g, v_w_ple_gate, v_b_ple_gate):
    mats = (w_in, w_uq, w_ukv, w_out, w_ple, w_ple_gate)
    m_mats = (m_w_in, m_w_uq, m_w_ukv, m_w_out, m_w_ple, m_w_ple_gate)
    v_mats = (v_w_in, v_w_uq, v_w_ukv, v_w_out, v_w_ple, v_w_ple_gate)
    vecs = (norm_pre_g, q_norm_g, kv_norm_g, sb_out_norm_g, mla_out_norm_g, norm_post_g, ple_norm_g, b_ple_gate)
    m_vecs = (m_norm_pre_g, m_q_norm_g, m_kv_norm_g, m_sb_out_norm_g, m_mla_out_norm_g, m_norm_post_g,
              m_ple_norm_g, m_b_ple_gate)
    v_vecs = (v_norm_pre_g, v_q_norm_g, v_kv_norm_g, v_sb_out_norm_g, v_mla_out_norm_g, v_norm_post_g,
              v_ple_norm_g, v_b_ple_gate)

    gath = _all_gather([a[0].astype(BF16) for a in mats])
    grad_x, d_parts, vec_slab = _local_grads(
        x[0], p[0, 0], positions[0], loss_target[0], *vecs, *_kernel_weights(gath))
    pays = _grad_payloads(*d_parts)
    landed1 = _pair_exchange(pays, vec_slab)
    place = jnp.stack([lax.axis_index("c"), 2 * lax.axis_index("x") + lax.axis_index("y")]).astype(jnp.int32)
    pair = [_pair_sum(g, l, place, "grad_pair_sum_%d" % o) for o, (g, l) in enumerate(zip(pays, landed1[:-1]))]
    landed2 = _chip_exchange([s for s, _ in pair])
    upd = [_adamw_matrix(own, l2, w[0], m[0], v[0], "adamw_%d" % o)
           for o, ((_, own), l2, w, m, v) in enumerate(zip(pair, landed2, mats, m_mats, v_mats))]
    sm = _adamw_vectors(landed1[-1], vecs, m_vecs, v_vecs)

    outs = []
    for kind in range(4):
        mat = [upd[o][kind][None] for o in range(len(mats))]
        vec = sm[1 + 8 * kind:9 + 8 * kind]
        outs += [vec[0], mat[0], vec[1], mat[1], vec[2], mat[2], vec[3], vec[4], mat[3], vec[5],
                 mat[4], vec[6], mat[5], vec[7]]
    return (sm[0][0, 0], grad_x[None], *outs)


def _local_grads(xs, ps, pos, tgt, norm_pre_g, q_norm_g, kv_norm_g, sb_out_norm_g, mla_out_norm_g,
                 norm_post_g, ple_norm_g, b_ple_gate, w_in_p, w_uq_p, w_uk_p, w_uv, f_out, f_ple, f_pg):
    s = xs.shape[0]

    half = ROPE_DIM // 2
    freq = ROPE_THETA ** (-jnp.arange(half, dtype=F32) / half)
    ang = pos.astype(F32)[:, None] * freq
    cos, sin = jnp.cos(ang), jnp.sin(ang)
    cos_t = jnp.concatenate([jnp.ones((s, 64), F32), cos, cos, jnp.zeros((s, 32), F32)], axis=1)
    sin_t = jnp.concatenate([jnp.zeros((s, 64), F32), -sin, sin, jnp.zeros((s, 32), F32)], axis=1)
    seg = jnp.arange(D_GRP) // HEAD_DIM
    bd = (seg[:, None] == seg[None, :]).astype(BF16)

    qkv, rest, h_b = _in_proj(xs, norm_pre_g, w_in_p)
    sb_o = _sb_fwd(qkv, 8)
    qp, kp, vv, cqn_b, ckvn_b = _mla_prep(rest, q_norm_g, kv_norm_g, w_uq_p, w_uk_p, w_uv, cos_t, sin_t)
    mla_o, lse = _mla_fwd(qp, kp, vv, 4)

    (dx1, d_sbo, d_mlo, d_sbg, d_mlg, x1_b, dgl_b, yc_b, dy_b, p_b, du_b, small_mid) = _mid(
        xs, ps, tgt, sb_o, mla_o, rest, sb_out_norm_g, mla_out_norm_g, f_out, norm_post_g,
        f_ple, ple_norm_g, f_pg, b_ple_gate, bd)
    dqp, dkp, dvv = _mla_bwd(qp, kp, vv, d_mlo, mla_o, lse, 4)
    dq_sb, dk_sb, dv_sb = _sb_bwd(qkv, d_sbo)
    dcq, dckv, dkr, dq_b, dk_b, dv_b, small_prep = _mla_prep_bwd(
        dqp, dkp, dvv, rest, q_norm_g, kv_norm_g, w_uq_p, w_uk_p, w_uv, cos_t, sin_t)
    pieces = [dq_sb, dk_sb, dv_sb, d_sbg, d_mlg, dcq, dckv, dkr]
    grad_x, small_in = _in_bwd(xs, norm_pre_g, dx1, pieces, w_in_p)

    d_cols = [_tn_matmul(h_b, pc, "dw_in_%d" % k) for k, pc in enumerate(pieces)]
    d_parts = (d_cols, _tn_matmul(cqn_b, dq_b, "dw_uq", blocked=True),
               _tn_matmul(ckvn_b, dk_b, "dw_uk", blocked=True), _tn_matmul(ckvn_b, dv_b, "dw_uv"),
               _tn_matmul(yc_b, dy_b, "dw_out"), _tn_matmul(p_b, du_b, "dw_ple", blocked=True),
               _tn_matmul(x1_b, dgl_b, "dw_pg"))
    slab = jnp.concatenate([small_in[0:1], jnp.pad(small_prep[0:2], ((0, 0), (0, D_MODEL - Q_LORA))),
                            small_mid[3:8]], axis=0)
    return grad_x, d_parts, slab
```

```python
import jax
import jax.numpy as jnp
from jax import lax
from jax.experimental import pallas as pl
from jax.experimental.pallas import tpu as pltpu

F32 = jnp.float32
BF16 = jnp.bfloat16
MESH = pl.DeviceIdType.MESH

N_DEV = 8
D_MODEL = 1024
N_HEADS = 8
HEAD_DIM = 64
D_GRP = N_HEADS * HEAD_DIM
Q_LORA = 256
KV_LORA = 128
ROPE_DIM = 32
PLE_DIM = 256
CHUNK_SHIFT = 6
ROPE_THETA = 10000.0
EPS = 1e-6
SB_SCALE = HEAD_DIM ** -0.5
MLA_SCALE = (HEAD_DIM + ROPE_DIM) ** -0.5
NEG = -1e30
LOG2_E = 1.4426950408889634
LN_2 = 0.6931471805599453
SB_CUTOFF = 110.0

ADAM_LR = 0.001
ADAM_B1 = 0.9
ADAM_B2 = 0.999
ADAM_EPS = 1e-08
ADAM_WD = 0.01
ADAM_STEP = 10

LANES = 128
TQ = 256
TK = 256
TM = 256

D_IN_P = 3072

_NT = (((1,), (1,)), ((), ()))
_TN = (((0,), (0,)), ((), ()))


def _params(sem, vmem_mb):
    return pltpu.CompilerParams(dimension_semantics=sem, vmem_limit_bytes=vmem_mb << 20)


def _dot(a, b):
    return jnp.dot(a, b, preferred_element_type=F32)


def _dot_nt(a, b):
    return lax.dot_general(a, b, _NT, preferred_element_type=F32)


def _dot_tn(a, b):
    return lax.dot_general(a, b, _TN, preferred_element_type=F32)


def _hl_dot(a, b):
    hi = a.astype(BF16)
    lo = (a - hi.astype(F32)).astype(BF16)
    return _dot(hi, b) + _dot(lo, b)


def _sigmoid(x):
    return 1.0 / (1.0 + jnp.exp(-x))


def _rope_swap(x, lane):
    left = pltpu.roll(x, LANES - 16, axis=1)
    right = pltpu.roll(x, 16, axis=1)
    lo = (lane >= 64) & (lane < 80)
    hi = (lane >= 80) & (lane < 96)
    return jnp.where(lo, left, jnp.where(hi, right, 0.0))


def _two_level_gather(x_refs, out_refs, send_sems, recv_sems, local_sems):
    x, y, c = lax.axis_index("x"), lax.axis_index("y"), lax.axis_index("c")
    me, sibling = (x, y, c), (x, y, 1 - c)
    chips = [(1 - x, y), (x, 1 - y), (1 - x, 1 - y)]
    ops = range(len(x_refs))

    def slot(o, px, py, pc):
        return out_refs[o].at[4 * px + 2 * py + pc]

    def copy(o, k, block, to, src=None):
        return pltpu.make_async_remote_copy(
            src_ref=slot(o, *block) if src is None else src, dst_ref=slot(o, *block),
            send_sem=send_sems.at[o, k], recv_sem=recv_sems.at[o, k],
            device_id=to, device_id_type=MESH)

    def mine():
        return [pltpu.make_async_copy(x_refs[o], slot(o, *me), local_sems.at[o]) for o in ops]

    def first():
        return ([copy(o, 0, me, sibling, src=x_refs[o]) for o in ops]
                + [copy(o, 1 + j, me, (*chip, c), src=x_refs[o]) for j, chip in enumerate(chips) for o in ops])

    def start():
        for cp in mine() + first():
            cp.start()

    def finish():
        passed = []
        for j, chip in enumerate(chips):
            for o in ops:
                copy(o, 1 + j, (*chip, c), me).wait_recv()
                passed.append(copy(o, 4 + j, (*chip, c), sibling))
                passed[-1].start()
        for o in ops:
            copy(o, 0, sibling, me).wait_recv()
        for j, chip in enumerate(chips):
            for o in ops:
                copy(o, 4 + j, (*chip, 1 - c), me).wait_recv()
        for cp in first() + passed:
            cp.wait_send()
        for cp in mine():
            cp.wait()

    return start, finish


def _gather_sems(n_op):
    return [pltpu.SemaphoreType.DMA((n_op, 7)), pltpu.SemaphoreType.DMA((n_op, 7)),
            pltpu.SemaphoreType.DMA((n_op,))]


def _all_gather(shards):
    n_op = len(shards)

    def body(*refs):
        start, finish = _two_level_gather(refs[:n_op], refs[n_op:2 * n_op], *refs[2 * n_op:])
        start()
        finish()

    vmem = pl.BlockSpec(memory_space=pltpu.VMEM)
    return pl.pallas_call(
        body, name="weight_all_gather",
        out_shape=[jax.ShapeDtypeStruct((N_DEV,) + a.shape, a.dtype) for a in shards],
        in_specs=[vmem] * n_op, out_specs=[vmem] * n_op, scratch_shapes=_gather_sems(n_op),
        compiler_params=pltpu.CompilerParams(vmem_limit_bytes=48 << 20),
    )(*shards)


def _pair_exchange(pays, small):
    n_op = len(pays)
    sr, n = small.shape

    def body(*refs):
        g_refs, s_ref = refs[:n_op], refs[n_op]
        l_refs, sland_ref = refs[n_op + 1:2 * n_op + 1], refs[2 * n_op + 1]
        ssem, rsem, ssem2, rsem2, lsem = refs[2 * n_op + 2:]
        x, y, c = lax.axis_index("x"), lax.axis_index("y"), lax.axis_index("c")
        me = 4 * x + 2 * y + c
        copies = []
        for o in range(n_op):
            for chip in range(4):
                copies.append(pltpu.make_async_remote_copy(
                    src_ref=g_refs[o].at[2 * chip + (1 - c)], dst_ref=l_refs[o].at[chip],
                    send_sem=ssem.at[o, chip], recv_sem=rsem.at[o, chip],
                    device_id=(x, y, 1 - c), device_id_type=MESH))
        for k in range(1, N_DEV):
            peer = (1 - x if (k >> 2) & 1 else x, 1 - y if (k >> 1) & 1 else y, 1 - c if k & 1 else c)
            copies.append(pltpu.make_async_remote_copy(
                src_ref=s_ref, dst_ref=sland_ref.at[me], send_sem=ssem2.at[k], recv_sem=rsem2.at[k],
                device_id=peer, device_id_type=MESH))
        own = pltpu.make_async_copy(s_ref, sland_ref.at[me], lsem)
        own.start()
        for cp in copies:
            cp.start()
        for cp in copies:
            cp.wait()
        own.wait()

    any_spec = pl.BlockSpec(memory_space=pl.ANY)
    return pl.pallas_call(
        body, name="grad_pair_exchange",
        out_shape=[jax.ShapeDtypeStruct((4,) + a.shape[1:], F32) for a in pays]
        + [jax.ShapeDtypeStruct((N_DEV, sr, n), F32)],
        in_specs=[any_spec] * (n_op + 1), out_specs=[any_spec] * (n_op + 1),
        scratch_shapes=[pltpu.SemaphoreType.DMA((n_op, 4)), pltpu.SemaphoreType.DMA((n_op, 4)),
                        pltpu.SemaphoreType.DMA((N_DEV,)), pltpu.SemaphoreType.DMA((N_DEV,)),
                        pltpu.SemaphoreType.DMA],
    )(*pays, small)


def _pair_sum(pay, landed, place, name):
    _, r, c = pay.shape

    def body(place_ref, g_ref, l_ref, s_ref, own_ref):
        i = pl.program_id(0)
        tot = g_ref[...] + l_ref[...]
        s_ref[...] = tot.astype(BF16)

        @pl.when(i == place_ref[1])
        def _():
            own_ref[...] = tot

    grid_spec = pltpu.PrefetchScalarGridSpec(
        num_scalar_prefetch=1, grid=(4,),
        in_specs=[pl.BlockSpec((None, r, c), lambda i, pr: (2 * i + pr[0], 0, 0)),
                  pl.BlockSpec((None, r, c), lambda i, pr: (i, 0, 0))],
        out_specs=[pl.BlockSpec((None, r, c), lambda i, pr: (i, 0, 0)),
                   pl.BlockSpec((r, c), lambda i, pr: (0, 0))])
    return pl.pallas_call(
        body, name=name, grid_spec=grid_spec,
        out_shape=[jax.ShapeDtypeStruct((4, r, c), BF16), jax.ShapeDtypeStruct((r, c), F32)],
        compiler_params=_params(("arbitrary",), 40),
    )(place, pay, landed)


def _chip_exchange(sums):
    n_op = len(sums)

    def body(*refs):
        s_refs, l_refs = refs[:n_op], refs[n_op:2 * n_op]
        ssem, rsem = refs[2 * n_op:]
        x, y, c = lax.axis_index("x"), lax.axis_index("y"), lax.axis_index("c")
        copies = []
        for rel in range(1, 4):
            px = 1 - x if rel & 2 else x
            py = 1 - y if rel & 1 else y
            for o in range(n_op):
                copies.append(pltpu.make_async_remote_copy(
                    src_ref=s_refs[o].at[2 * px + py], dst_ref=l_refs[o].at[rel - 1],
                    send_sem=ssem.at[o, rel - 1], recv_sem=rsem.at[o, rel - 1],
                    device_id=(px, py, c), device_id_type=MESH))
        for cp in copies:
            cp.start()
        for cp in copies:
            cp.wait()

    any_spec = pl.BlockSpec(memory_space=pl.ANY)
    return pl.pallas_call(
        body, name="grad_chip_exchange",
        out_shape=[jax.ShapeDtypeStruct((3,) + a.shape[1:], BF16) for a in sums],
        in_specs=[any_spec] * n_op, out_specs=[any_spec] * n_op,
        scratch_shapes=[pltpu.SemaphoreType.DMA((n_op, 3)), pltpu.SemaphoreType.DMA((n_op, 3))],
    )(*sums)


def _adamw_math(g, w, m, v):
    mn = ADAM_B1 * m + (1.0 - ADAM_B1) * g
    vn = ADAM_B2 * v + (1.0 - ADAM_B2) * (g * g)
    m_hat = mn / (1.0 - ADAM_B1 ** ADAM_STEP)
    v_hat = vn / (1.0 - ADAM_B2 ** ADAM_STEP)
    return -ADAM_LR * (m_hat / (jnp.sqrt(v_hat) + ADAM_EPS) + ADAM_WD * w), mn, vn


def _adamw_matrix(own, landed, w, m, v, name):
    r, c = w.shape
    cp = own.shape[1]
    br = min(r, 256)

    def body(own_ref, l_ref, w_ref, m_ref, v_ref, g_out, d_out, m_out, v_out):
        g = own_ref[...]
        for k in range(3):
            g = g + l_ref[k].astype(F32)
        g = g[:, :c]
        g_out[...] = g
        d_out[...], m_out[...], v_out[...] = _adamw_math(g, w_ref[...], m_ref[...], v_ref[...])

    row = pl.BlockSpec((br, c), lambda i: (i, 0))
    shp = jax.ShapeDtypeStruct((r, c), F32)
    return pl.pallas_call(
        body, name=name, grid=(r // br,),
        in_specs=[pl.BlockSpec((br, cp), lambda i: (i, 0)), pl.BlockSpec((3, br, cp), lambda i: (0, i, 0)),
                  row, row, row],
        out_specs=(row, row, row, row), out_shape=(shp, shp, shp, shp),
        compiler_params=_params(("parallel",), 40),
    )(own, landed, w, m, v)


_VEC_PLACE = ((0, 0), (1, 0), (2, 0), (3, 0), (3, D_GRP), (4, 0), (5, 0), (6, 0))


def _adamw_vectors(sland, ws, ms, vs):
    nv = len(ws)

    def body(l_ref, *refs):
        w_refs, m_refs, v_refs = refs[:nv], refs[nv:2 * nv], refs[2 * nv:3 * nv]
        loss_ref = refs[3 * nv]
        outs = refs[3 * nv + 1:]
        g_all = l_ref[0]
        for j in range(1, N_DEV):
            g_all = g_all + l_ref[j]
        loss_ref[...] = jnp.sum(g_all[7:8, :], axis=1, keepdims=True)
        for k, (row, lane0) in enumerate(_VEC_PLACE):
            n = w_refs[k].shape[1]
            g = g_all[row:row + 1, lane0:lane0 + n]
            d, mn, vn = _adamw_math(g, w_refs[k][...], m_refs[k][...], v_refs[k][...])
            outs[k][...] = g
            outs[nv + k][...] = d
            outs[2 * nv + k][...] = mn
            outs[3 * nv + k][...] = vn

    vmem = pl.BlockSpec(memory_space=pltpu.VMEM)
    shapes = [jax.ShapeDtypeStruct(w.shape, F32) for w in ws]
    return pl.pallas_call(
        body, name="adamw_vectors", in_specs=[vmem] * (1 + 3 * nv), out_specs=[vmem] * (1 + 4 * nv),
        out_shape=[jax.ShapeDtypeStruct((1, 1), F32)] + shapes * 4,
    )(sland, *ws, *ms, *vs)


def _in_proj(x, g, w, shards):
    s = x.shape[0]
    n_op = len(shards)
    steps = s // TM

    def body(x_ref, g_ref, w_ref, *refs):
        shard_refs = refs[:n_op]
        qkv_ref, rest_ref, h_ref = refs[n_op:n_op + 3]
        gath_refs = refs[n_op + 3:2 * n_op + 3]
        start, finish = _two_level_gather(shard_refs, gath_refs, *refs[2 * n_op + 3:])
        i = pl.program_id(0)

        @pl.when(i == 0)
        def _():
            start()

        xv = x_ref[...]
        r = lax.rsqrt(jnp.mean(xv * xv, axis=-1, keepdims=True) + EPS)
        h = ((xv * r) * g_ref[...]).astype(BF16)
        h_ref[...] = h
        qkv_ref[...] = _dot(h, w_ref[:, :1536]).astype(BF16)
        rest_ref[...] = _dot(h, w_ref[:, 1536:])

        @pl.when(i == steps - 1)
        def _():
            finish()

    any_spec = pl.BlockSpec(memory_space=pl.ANY)
    return pl.pallas_call(
        body, name="in_proj", grid=(steps,),
        in_specs=[pl.BlockSpec((TM, D_MODEL), lambda i: (i, 0)),
                  pl.BlockSpec((1, D_MODEL), lambda i: (0, 0)),
                  pl.BlockSpec((D_MODEL, D_IN_P), lambda i: (0, 0))] + [any_spec] * n_op,
        out_specs=[pl.BlockSpec((TM, 1536), lambda i: (i, 0)),
                   pl.BlockSpec((TM, 1536), lambda i: (i, 0)),
                   pl.BlockSpec((TM, D_MODEL), lambda i: (i, 0))] + [any_spec] * n_op,
        out_shape=[jax.ShapeDtypeStruct((s, 1536), BF16), jax.ShapeDtypeStruct((s, 1536), F32),
                   jax.ShapeDtypeStruct((s, D_MODEL), BF16)]
        + [jax.ShapeDtypeStruct((N_DEV,) + a.shape, a.dtype) for a in shards],
        scratch_shapes=_gather_sems(n_op),
        compiler_params=_params(("arbitrary",), 48),
    )(x, g, w, *shards)


def _mla_prep(rest, gq, gkv, wuq, wuk, wuv, cos_t, sin_t):
    s = rest.shape[0]

    def body(cq_ref, ckv_ref, kr_ref, gq_ref, gkv_ref, wuq_ref, wuk_ref, wuv_ref, c_ref, s_ref,
             qp_ref, kp_ref, vv_ref, cqn_ref, ckvn_ref):
        lane = lax.broadcasted_iota(jnp.int32, (1, LANES), 1)
        cos_v, sin_v = c_ref[...], s_ref[...]
        cq = cq_ref[...]
        rq = lax.rsqrt(jnp.mean(cq * cq, axis=-1, keepdims=True) + EPS)
        cqn = ((cq * rq) * gq_ref[...]).astype(BF16)
        cqn_ref[...] = cqn
        q = _dot(cqn, wuq_ref[...])
        ckv = ckv_ref[...]
        rkv = lax.rsqrt(jnp.mean(ckv * ckv, axis=-1, keepdims=True) + EPS)
        ckvn = ((ckv * rkv) * gkv_ref[...]).astype(BF16)
        ckvn_ref[...] = ckvn
        kn = _dot(ckvn, wuk_ref[...])
        vv_ref[...] = _dot(ckvn, wuv_ref[...]).astype(BF16)
        kr = kr_ref[...]
        kr_roped = kr * cos_v + _rope_swap(kr, lane) * sin_v
        for h in range(N_HEADS):
            sl = slice(h * LANES, (h + 1) * LANES)
            qh = q[:, sl]
            qp_ref[:, sl] = (qh * cos_v + _rope_swap(qh, lane) * sin_v).astype(BF16)
            kp_ref[:, sl] = (kn[:, sl] + kr_roped).astype(BF16)

    def row(width, idx):
        return pl.BlockSpec((TM, width), lambda i: (i, idx))

    def full(a):
        return pl.BlockSpec(a.shape, lambda i: (0, 0))

    return pl.pallas_call(
        body, name="mla_prep", grid=(s // TM,),
        in_specs=[row(Q_LORA, 4), row(KV_LORA, 10), row(LANES, 11), full(gq), full(gkv),
                  full(wuq), full(wuk), full(wuv), row(LANES, 0), row(LANES, 0)],
        out_specs=(row(1024, 0), row(1024, 0), row(D_GRP, 0), row(Q_LORA, 0), row(KV_LORA, 0)),
        out_shape=(jax.ShapeDtypeStruct((s, 1024), BF16), jax.ShapeDtypeStruct((s, 1024), BF16),
                   jax.ShapeDtypeStruct((s, D_GRP), BF16), jax.ShapeDtypeStruct((s, Q_LORA), BF16),
                   jax.ShapeDtypeStruct((s, KV_LORA), BF16)),
        compiler_params=_params(("parallel",), 32),
    )(rest, rest, rest, gq, gkv, wuq, wuk, wuv, cos_t, sin_t)


def _sb_live(n, qi, carries):
    top = carries[0]
    for c in carries[1:]:
        top = jnp.maximum(top, c)
    return jnp.logical_and(n < qi, jnp.max(top) > -SB_CUTOFF)


def _sb_fwd(qkv, hb):
    s = qkv.shape[0]

    def body(q_ref, k_ref, v_ref, o_ref, acc):
        qi = pl.program_id(1)
        lane = lax.broadcasted_iota(jnp.int32, (1, LANES), 1)
        is_a = lane < HEAD_DIM
        pair = lambda h: slice((h // 2) * LANES, (h // 2 + 1) * LANES)
        q_h = []
        for h in range(hb):
            qs = q_ref[:, pair(h)] * SB_SCALE
            mine = is_a if h % 2 == 0 else jnp.logical_not(is_a)
            q_h.append(jnp.where(mine, qs, jnp.zeros_like(qs)))
        r_i = lax.broadcasted_iota(jnp.int32, (TQ, TK), 0)
        c_i = lax.broadcasted_iota(jnp.int32, (TQ, TK), 1)
        past = c_i < r_i
        upper = (r_i > c_i).astype(BF16)
        acc[...] = jnp.zeros_like(acc)

        def tile(j, carries, diag):
            ks = pl.ds(pl.multiple_of(j * TK, TK), TK)
            zs = [_dot_nt(q_h[h], k_ref[ks, pair(h)]) for h in range(hb)]
            if diag:
                zs = [jnp.where(past, z, NEG) for z in zs]
            lfs = [-(jnp.maximum(z, 0.0) + jnp.log(1.0 + jnp.exp(-jnp.abs(z)))) for z in zs]
            sufs = [_hl_dot(lfs[h], upper) for h in range(hb)]
            out = []
            for h in range(hb):
                w = jnp.exp(zs[h] + lfs[h] + (sufs[h] + carries[h]))
                acc[h] += _dot(w.astype(BF16), v_ref[ks, pair(h)])
                out.append(carries[h] + jnp.sum(lfs[h], axis=1, keepdims=True))
            return tuple(out)

        zero = jnp.zeros((TQ, 1), F32)
        carries = tile(qi, (zero,) * hb, True)

        def step(st):
            return (st[0] + 1,) + tile(qi - 1 - st[0], st[1:], False)

        lax.while_loop(lambda st: _sb_live(st[0], qi, st[1:]), step, (0,) + carries)
        for pr in range(hb // 2):
            o_ref[:, pr * LANES:(pr + 1) * LANES] = jnp.where(is_a, acc[2 * pr], acc[2 * pr + 1])

    width = hb * HEAD_DIM
    nb = D_GRP // width
    slab = lambda part: pl.BlockSpec((s, width), lambda g, qi: (0, part * nb + g))
    blk = pl.BlockSpec((TQ, width), lambda g, qi: (qi, g))
    return pl.pallas_call(
        body, name="sb_fwd", grid=(nb, s // TQ),
        in_specs=[blk, slab(1), slab(2)], out_specs=blk,
        out_shape=jax.ShapeDtypeStruct((s, D_GRP), F32),
        scratch_shapes=[pltpu.VMEM((hb, TQ, LANES), F32)],
        compiler_params=_params(("arbitrary", "arbitrary"), 48),
    )(qkv, qkv, qkv)


def _sb_bwd(qkv, d_o):
    s = qkv.shape[0]
    nq = s // TQ
    nk = s // TK

    def body(q_ref, k_ref, v_ref, do_ref, dq_ref, dk_ref, dv_ref, x1s, bts, dqacc, dkacc, dvacc):
        qi = pl.program_id(1)
        lane = lax.broadcasted_iota(jnp.int32, (1, LANES), 1)
        is_a = lane < HEAD_DIM

        @pl.when(qi == 0)
        def _():
            dkacc[...] = jnp.zeros_like(dkacc)
            dvacc[...] = jnp.zeros_like(dvacc)

        qs = q_ref[...] * SB_SCALE
        zq = jnp.zeros_like(qs)
        qs_x = (jnp.where(is_a, qs, zq), jnp.where(is_a, zq, qs))
        dob = do_ref[...].astype(BF16)
        do_x = (jnp.where(is_a, dob, zq), jnp.where(is_a, zq, dob))
        r_i = lax.broadcasted_iota(jnp.int32, (TQ, TK), 0)
        c_i = lax.broadcasted_iota(jnp.int32, (TQ, TK), 1)
        past = c_i < r_i
        upper = (r_i > c_i).astype(BF16)
        upper_incl = (r_i >= c_i).astype(BF16)
        dqacc[...] = jnp.zeros_like(dqacc)
        both = ((0, 0), (0, 1), (1, 0), (1, 1))

        def tiles(n):
            j_hi = qi - 2 * n
            lo_ok = j_hi >= 1
            j_lo = jnp.maximum(j_hi - 1, 0)
            ks = (pl.ds(pl.multiple_of(j_hi * TK, TK), TK), pl.ds(pl.multiple_of(j_lo * TK, TK), TK))
            return j_hi, lo_ok, j_lo, ks

        def sweep(n, carries):
            j_hi, lo_ok, j_lo, ks = tiles(n)
            slot = (j_hi, jnp.where(lo_ok, j_lo, nk))
            valid = (jnp.logical_or(past, j_hi < qi), lo_ok)
            z = {th: jnp.where(valid[th[0]], _dot_nt(qs_x[th[1]], k_ref[ks[th[0]], :]), NEG) for th in both}
            d_a = {th: _dot_nt(do_x[th[1]], v_ref[ks[th[0]], :]) for th in both}
            lf, beta, omb = {}, {}, {}
            for th in both:
                e = jnp.exp(-jnp.abs(z[th]))
                den = 1.0 + e
                rden = 1.0 / den
                pos = z[th] >= 0.0
                lf[th] = -(jnp.maximum(z[th], 0.0) + jnp.log(den))
                beta[th] = jnp.where(pos, rden, e * rden)
                omb[th] = jnp.where(pos, e * rden, rden)
            suf = {th: _hl_dot(lf[th], upper) for th in both}
            c, g_in = {}, {}
            for h in range(2):
                c[0, h], g_in[0, h] = carries[2 * h], carries[2 * h + 1]
                c[1, h] = c[0, h] + jnp.sum(lf[0, h], axis=1, keepdims=True)
            a, g = {}, {}
            for th in both:
                a[th] = jnp.exp(z[th] + lf[th] + (suf[th] + c[th]))
                g[th] = a[th] * d_a[th]
            sg = {th: _hl_dot(g[th], upper_incl) for th in both}
            for h in range(2):
                g_in[1, h] = g_in[0, h] + jnp.sum(g[0, h], axis=1, keepdims=True)
            for th in both:
                t, h = th
                x1s[slot[t], h] = g[th] * omb[th] + beta[th] * (sg[th] + g_in[th])
                bts[slot[t], h] = beta[th]
                dvacc[ks[t], :] += _dot_tn(a[th].astype(BF16), do_x[h])
            out = []
            for h in range(2):
                out.append(c[1, h] + jnp.sum(lf[1, h], axis=1, keepdims=True))
                out.append(g_in[1, h] + jnp.sum(g[1, h], axis=1, keepdims=True))
            return tuple(out)

        zero = jnp.zeros((TQ, 1), F32)
        first = sweep(0, (zero, zero, zero, zero))

        def more(st):
            return jnp.logical_and(2 * st[0] <= qi, jnp.max(jnp.maximum(st[1], st[3])) > -SB_CUTOFF)

        swept = lax.while_loop(more, lambda st: (st[0] + 1,) + sweep(st[0], st[1:]), (1,) + first)
        g_tot = (swept[2], swept[4])

        def apply(n, carry):
            j_hi, lo_ok, j_lo, ks = tiles(n)

            def one(j, kslice):
                for h in range(2):
                    dz = (x1s[j, h] - bts[j, h] * g_tot[h]).astype(BF16)
                    dqacc[h] += _dot(dz, k_ref[kslice, :])
                    dkacc[kslice, :] += _dot_tn(dz, qs_x[h])

            one(j_hi, ks[0])

            @pl.when(lo_ok)
            def _():
                one(j_lo, ks[1])

            return carry

        lax.fori_loop(0, swept[0], apply, 0)
        dq_ref[...] = (jnp.where(is_a, dqacc[0], dqacc[1]) * SB_SCALE).astype(BF16)

        @pl.when(qi == nq - 1)
        def _():
            dk_ref[...] = dkacc[...].astype(BF16)
            dv_ref[...] = dvacc[...].astype(BF16)

    slab = lambda off: pl.BlockSpec((s, LANES), lambda p, qi: (0, off + p))
    blk = pl.BlockSpec((TQ, LANES), lambda p, qi: (qi, p))
    out_slab = pl.BlockSpec((s, LANES), lambda p, qi: (0, p))
    shp = jax.ShapeDtypeStruct((s, D_GRP), BF16)
    return pl.pallas_call(
        body, name="sb_bwd", grid=(4, nq),
        in_specs=[blk, slab(4), slab(8), blk],
        out_specs=(blk, out_slab, out_slab), out_shape=(shp, shp, shp),
        scratch_shapes=[pltpu.VMEM((nk + 1, 2, TQ, TK), F32)] * 2
        + [pltpu.VMEM((2, TQ, LANES), F32), pltpu.VMEM((s, LANES), F32), pltpu.VMEM((s, LANES), F32)],
        compiler_params=_params(("arbitrary", "arbitrary"), 56),
    )(qkv, qkv, qkv, d_o)


def _mla_fwd(qp, kp, vv, hb):
    s = qp.shape[0]
    c2 = MLA_SCALE * LOG2_E

    def body(q_ref, k_ref, v_ref, o_ref, lse_ref, vaug, mrun, mb, acc, zbuf):
        qi = pl.program_id(1)
        lane = lax.broadcasted_iota(jnp.int32, (1, LANES), 1)
        is_a = lane < HEAD_DIM

        @pl.when(qi == 0)
        def _():
            for h in range(hb):
                vp = v_ref[:, (h // 2) * LANES:(h // 2 + 1) * LANES]
                mine = is_a if h % 2 == 0 else jnp.logical_not(is_a)
                vaug[h] = jnp.where(mine, vp, jnp.ones_like(vp))

        r_i = lax.broadcasted_iota(jnp.int32, (TQ, TK), 0)
        c_i = lax.broadcasted_iota(jnp.int32, (TQ, TK), 1)
        visible = (c_i >> CHUNK_SHIFT) <= (r_i >> CHUNK_SHIFT)

        def key_rows(j):
            return pl.ds(pl.multiple_of(j * TK, TK), TK)

        def sweep(tiles):
            def loop(n, carry):
                tiles(((2 * n, False), (2 * n + 1, False)))
                return carry

            lax.fori_loop(0, qi // 2, loop, 0)

            @pl.when(qi % 2 == 1)
            def _():
                tiles(((qi - 1, False), (qi, True)))

            @pl.when(qi % 2 == 0)
            def _():
                tiles(((qi, True),))

        mrun[...] = jnp.full_like(mrun, NEG)

        def tiles_max(js):
            zs = [[_dot_nt(q_ref[:, h * LANES:(h + 1) * LANES], k_ref[key_rows(j), h * LANES:(h + 1) * LANES])
                   for h in range(hb)] for j, _ in js]
            for t, (j, diag) in enumerate(js):
                for h in range(hb):
                    z = jnp.where(visible, zs[t][h], NEG) if diag else zs[t][h]
                    zbuf[j, h] = z
                    mrun[h] = jnp.maximum(mrun[h], z)

        sweep(tiles_max)
        for h in range(hb):
            m = jnp.max(mrun[h], axis=1, keepdims=True) * c2
            mb[h] = jnp.broadcast_to(m, (TQ, TK))
        acc[...] = jnp.zeros_like(acc)

        def tiles_pv(js):
            ps = [[jnp.exp2(zbuf[j, h] * c2 - mb[h]).astype(BF16) for h in range(hb)] for j, _ in js]
            for t, (j, _) in enumerate(js):
                for h in range(hb):
                    acc[h] += _dot(ps[t][h], vaug[h, key_rows(j), :])

        sweep(tiles_pv)
        for pr in range(hb // 2):
            a, b = 2 * pr, 2 * pr + 1
            psl = slice(pr * LANES, (pr + 1) * LANES)
            acc_a, acc_b = acc[a], acc[b]
            l_a = pltpu.roll(acc_a, HEAD_DIM, axis=1)
            l_b = pltpu.roll(acc_b, HEAD_DIM, axis=1)
            o_ref[:, psl] = jnp.where(is_a, acc_a * (1.0 / l_a), acc_b * (1.0 / l_b))
            lse_ref[:, psl] = jnp.where(is_a, mb[a, :, :LANES] * LN_2 + jnp.log(l_a),
                                        mb[b, :, :LANES] * LN_2 + jnp.log(l_b))

    blk = pl.BlockSpec((TQ, hb * HEAD_DIM), lambda g, qi: (qi, g))
    shp = jax.ShapeDtypeStruct((s, D_GRP), F32)
    return pl.pallas_call(
        body, name="mla_fwd", grid=(N_HEADS // hb, s // TQ),
        in_specs=[pl.BlockSpec((TQ, hb * LANES), lambda g, qi: (qi, g)),
                  pl.BlockSpec((s, hb * LANES), lambda g, qi: (0, g)),
                  pl.BlockSpec((s, hb * HEAD_DIM), lambda g, qi: (0, g))],
        out_specs=(blk, blk), out_shape=(shp, shp),
        scratch_shapes=[pltpu.VMEM((hb, s, LANES), BF16), pltpu.VMEM((hb, TQ, TK), F32),
                        pltpu.VMEM((hb, TQ, TK), F32), pltpu.VMEM((hb, TQ, LANES), F32),
                        pltpu.VMEM((s // TK, hb, TQ, TK), F32)],
        compiler_params=_params(("arbitrary", "arbitrary"), 56),
    )(qp, kp, vv)


def _mla_bwd(qp, kp, vv, d_o, o, lse, hb):
    s = qp.shape[0]
    c2 = MLA_SCALE * LOG2_E

    def body(q_ref, k_ref, v_ref, do_ref, o_ref, lse_ref, dq_ref, dk_ref, dv_ref, dqacc, lse_b, delta_b):
        qi = pl.program_id(1)
        lane = lax.broadcasted_iota(jnp.int32, (1, LANES), 1)
        is_a = lane < HEAD_DIM

        @pl.when(qi == 0)
        def _():
            dk_ref[...] = jnp.zeros_like(dk_ref)
            dv_ref[...] = jnp.zeros_like(dv_ref)

        r_i = lax.broadcasted_iota(jnp.int32, (TQ, TK), 0)
        c_i = lax.broadcasted_iota(jnp.int32, (TQ, TK), 1)
        visible = (c_i >> CHUNK_SHIFT) <= (r_i >> CHUNK_SHIFT)
        do_x = []
        for h in range(hb):
            psl = slice((h // 2) * LANES, (h // 2 + 1) * LANES)
            mine = is_a if h % 2 == 0 else jnp.logical_not(is_a)
            d_o = do_ref[:, psl]
            delta = jnp.sum(jnp.where(mine, d_o * o_ref[:, psl], 0.0), axis=1, keepdims=True)
            lse_h = jnp.sum(jnp.where(lane == (h % 2) * HEAD_DIM, lse_ref[:, psl], 0.0), axis=1, keepdims=True)
            lse_b[h] = jnp.broadcast_to(lse_h * LOG2_E, (TQ, TK))
            delta_b[h] = jnp.broadcast_to(delta, (TQ, TK))
            do_x.append(jnp.where(mine, d_o, 0.0).astype(BF16))
        dqacc[...] = jnp.zeros_like(dqacc)

        head = lambda h: slice(h * LANES, (h + 1) * LANES)
        pair = lambda h: slice((h // 2) * LANES, (h // 2 + 1) * LANES)

        def tiles(js):
            th = [(j, diag, pl.ds(pl.multiple_of(j * TK, TK), TK), h) for j, diag in js for h in range(hb)]
            zs = [_dot_nt(q_ref[:, head(h)], k_ref[ks, head(h)]) for _, _, ks, h in th]
            dps = [_dot_nt(do_x[h], v_ref[ks, pair(h)]) for _, _, ks, h in th]
            for i, (j, diag, ks, h) in enumerate(th):
                e = zs[i] * c2 - lse_b[h]
                if diag:
                    e = jnp.where(visible, e, NEG)
                p = jnp.exp2(e)
                ds = (p * (dps[i] - delta_b[h]) * MLA_SCALE).astype(BF16)
                dqacc[h] += _dot(ds, k_ref[ks, head(h)])
                dk_ref[ks, head(h)] += _dot_tn(ds, q_ref[:, head(h)])
                dv_ref[ks, pair(h)] += _dot_tn(p.astype(BF16), do_x[h])

        def loop(n, c):
            tiles(((2 * n, False), (2 * n + 1, False)))
            return c

        lax.fori_loop(0, qi // 2, loop, 0)

        @pl.when(qi % 2 == 1)
        def _():
            tiles(((qi - 1, False), (qi, True)))

        @pl.when(qi % 2 == 0)
        def _():
            tiles(((qi, True),))

        for h in range(hb):
            dq_ref[:, h * LANES:(h + 1) * LANES] = dqacc[h]

    blk = pl.BlockSpec((TQ, hb * HEAD_DIM), lambda g, qi: (qi, g))
    return pl.pallas_call(
        body, name="mla_bwd", grid=(N_HEADS // hb, s // TQ),
        in_specs=[pl.BlockSpec((TQ, hb * LANES), lambda g, qi: (qi, g)),
                  pl.BlockSpec((s, hb * LANES), lambda g, qi: (0, g)),
                  pl.BlockSpec((s, hb * HEAD_DIM), lambda g, qi: (0, g)), blk, blk, blk],
        out_specs=(pl.BlockSpec((TQ, hb * LANES), lambda g, qi: (qi, g)),
                   pl.BlockSpec((s, hb * LANES), lambda g, qi: (0, g)),
                   pl.BlockSpec((s, hb * HEAD_DIM), lambda g, qi: (0, g))),
        out_shape=(jax.ShapeDtypeStruct((s, 1024), F32), jax.ShapeDtypeStruct((s, 1024), F32),
                   jax.ShapeDtypeStruct((s, D_GRP), F32)),
        scratch_shapes=[pltpu.VMEM((hb, TQ, LANES), F32), pltpu.VMEM((hb, TQ, TK), F32),
                        pltpu.VMEM((hb, TQ, TK), F32)],
        compiler_params=_params(("arbitrary", "arbitrary"), 56),
    )(qp, kp, vv, d_o, o, lse)


def _mid(x, p, target, sb_o, mla_o, rest, g_sb, g_mla, w_out, g_post, w_ple, g_ple, w_pg, b_pg, bd):
    s = x.shape[0]

    def body(x_ref, p_ref, t_ref, sbo_ref, mlo_ref, sbg_ref, mlg_ref, gsb_ref, gml_ref, wout_ref,
             gpost_ref, wple_ref, gple_ref, wpg_ref, bpg_ref, bd_ref,
             dx1_ref, dsbo_ref, dmlo_ref, dsbg_ref, dmlg_ref, x1b_ref, dglb_ref, ycb_ref, dyb_ref,
             pb_ref, dub_ref, small_ref):
        i = pl.program_id(0)
        bd_m = bd_ref[...]

        def seg_mean(v):
            return _dot(v.astype(BF16), bd_m) * (1.0 / HEAD_DIM)

        groups = []
        for o_ref, gate_ref, gain_ref in ((sbo_ref, sbg_ref, gsb_ref), (mlo_ref, mlg_ref, gml_ref)):
            o = o_ref[...]
            r = lax.rsqrt(seg_mean(o * o) + EPS)
            n = o * r
            hn = n * gain_ref[...]
            gate = gate_ref[...]
            sg = _sigmoid(gate)
            si = gate * sg
            groups.append((r, n, hn, gate, sg, si, gain_ref[...]))
        ya = (groups[0][2] * groups[0][5]).astype(BF16)
        yb = (groups[1][2] * groups[1][5]).astype(BF16)
        ycb_ref[:, :D_GRP] = ya
        ycb_ref[:, D_GRP:] = yb
        y = _dot(ya, wout_ref[:D_GRP, :]) + _dot(yb, wout_ref[D_GRP:, :])
        ry = lax.rsqrt(jnp.mean(y * y, axis=-1, keepdims=True) + EPS)
        ny = y * ry
        x1 = x_ref[...] + ny * gpost_ref[...]
        x1b = x1.astype(BF16)
        x1b_ref[...] = x1b
        pb = p_ref[...].astype(BF16)
        pb_ref[...] = pb
        u = _dot(pb, wple_ref[...])
        ru = lax.rsqrt(jnp.mean(u * u, axis=-1, keepdims=True) + EPS)
        nu = u * ru
        ple = nu * gple_ref[...]
        gate = _sigmoid(_dot(x1b, wpg_ref[...]) + bpg_ref[...])
        x2 = x1 + ple * gate
        diff = x2 - t_ref[...]
        dx2 = diff * (1.0 / D_MODEL)

        d_ple = dx2 * gate
        d_glin = (dx2 * ple) * (gate * (1.0 - gate))
        dglb = d_glin.astype(BF16)
        dglb_ref[...] = dglb
        dx1 = dx2 + _dot_nt(dglb, wpg_ref[...])
        dx1_ref[...] = dx1
        d_nu = d_ple * gple_ref[...]
        d_u = ru * (d_nu - nu * jnp.mean(d_nu * nu, axis=-1, keepdims=True))
        dub_ref[...] = d_u.astype(BF16)
        d_ny = dx1 * gpost_ref[...]
        d_y = ry * (d_ny - ny * jnp.mean(d_ny * ny, axis=-1, keepdims=True))
        dyb = d_y.astype(BF16)
        dyb_ref[...] = dyb
        d_yc = (_dot_nt(dyb, wout_ref[:D_GRP, :]), _dot_nt(dyb, wout_ref[D_GRP:, :]))

        d_gain = []
        for gx, (do_ref, dg_ref) in enumerate(((dsbo_ref, dsbg_ref), (dmlo_ref, dmlg_ref))):
            r, n, hn, gate_g, sg, si, gain = groups[gx]
            dyg = d_yc[gx]
            d_hn = dyg * si
            dg_ref[...] = (dyg * hn * (sg * (1.0 + gate_g * (1.0 - sg)))).astype(BF16)
            d_gain.append(jnp.sum(d_hn * n, axis=0, keepdims=True))
            d_n = d_hn * gain
            do_ref[...] = r * (d_n - n * seg_mean(d_n * n))

        @pl.when(i == 0)
        def _():
            small_ref[...] = jnp.zeros_like(small_ref)

        small_ref[3:4, :D_GRP] += d_gain[0]
        small_ref[3:4, D_GRP:] += d_gain[1]
        small_ref[4:5, :] += jnp.sum(dx1 * ny, axis=0, keepdims=True)
        small_ref[5:6, :] += jnp.sum(d_ple * nu, axis=0, keepdims=True)
        small_ref[6:7, :] += jnp.sum(d_glin, axis=0, keepdims=True)
        small_ref[7:8, :] += jnp.sum(diff * diff, axis=0, keepdims=True) * (0.5 / D_MODEL)

    def row(width, idx=0):
        return pl.BlockSpec((TM, width), lambda i: (i, idx))

    def full(a):
        return pl.BlockSpec(a.shape, lambda i: (0, 0))

    f32 = lambda w: jax.ShapeDtypeStruct((s, w), F32)
    b16 = lambda w: jax.ShapeDtypeStruct((s, w), BF16)
    return pl.pallas_call(
        body, name="mid", grid=(s // TM,),
        in_specs=[row(D_MODEL), row(PLE_DIM), row(D_MODEL), row(D_GRP), row(D_GRP),
                  row(D_GRP, 0), row(D_GRP, 1), full(g_sb), full(g_mla), full(w_out), full(g_post),
                  full(w_ple), full(g_ple), full(w_pg), full(b_pg), full(bd)],
        out_specs=(row(D_MODEL), row(D_GRP), row(D_GRP), row(D_GRP), row(D_GRP), row(D_MODEL),
                   row(D_MODEL), row(D_MODEL), row(D_MODEL), row(PLE_DIM), row(D_MODEL),
                   pl.BlockSpec((8, D_MODEL), lambda i: (0, 0))),
        out_shape=(f32(D_MODEL), f32(D_GRP), f32(D_GRP), b16(D_GRP), b16(D_GRP), b16(D_MODEL),
                   b16(D_MODEL), b16(D_MODEL), b16(D_MODEL), b16(PLE_DIM), b16(D_MODEL),
                   jax.ShapeDtypeStruct((8, D_MODEL), F32)),
        compiler_params=_params(("arbitrary",), 56),
    )(x, p, target, sb_o, mla_o, rest, rest, g_sb, g_mla, w_out, g_post, w_ple, g_ple, w_pg, b_pg, bd)


def _mla_prep_bwd(dqp, dkp, dvv, rest, gq, gkv, wuq, wuk, wuv, cos_t, sin_t):
    s = rest.shape[0]

    def body(dqp_ref, dkp_ref, dvv_ref, cq_ref, ckv_ref, gq_ref, gkv_ref, wuq_ref, wuk_ref, wuv_ref,
             c_ref, s_ref, dcq_ref, dckv_ref, dkr_ref, dqb_ref, dkb_ref, dvb_ref, small_ref):
        i = pl.program_id(0)
        lane = lax.broadcasted_iota(jnp.int32, (1, LANES), 1)
        in_rope = (lane >= HEAD_DIM) & (lane < HEAD_DIM + ROPE_DIM)
        cos_v, sin_v = c_ref[...], s_ref[...]
        dkr_roped = jnp.zeros((TM, LANES), F32)
        for h in range(N_HEADS):
            sl = slice(h * LANES, (h + 1) * LANES)
            dy = dqp_ref[:, sl]
            dqb_ref[:, sl] = (dy * cos_v + _rope_swap(dy * sin_v, lane)).astype(BF16)
            dkh = dkp_ref[:, sl]
            dkb_ref[:, sl] = dkh.astype(BF16)
            dkr_roped = dkr_roped + jnp.where(in_rope, dkh, 0.0)
        dkr_ref[...] = (dkr_roped * cos_v + _rope_swap(dkr_roped * sin_v, lane)).astype(BF16)
        dvb = dvv_ref[...].astype(BF16)
        dvb_ref[...] = dvb

        cq = cq_ref[...]
        rq = lax.rsqrt(jnp.mean(cq * cq, axis=-1, keepdims=True) + EPS)
        nq_ = cq * rq
        d_cqn = _dot_nt(dqb_ref[...], wuq_ref[...])
        d_n = d_cqn * gq_ref[...]
        dcq_ref[...] = (rq * (d_n - nq_ * jnp.mean(d_n * nq_, axis=-1, keepdims=True))).astype(BF16)

        ckv = ckv_ref[...]
        rkv = lax.rsqrt(jnp.mean(ckv * ckv, axis=-1, keepdims=True) + EPS)
        nkv = ckv * rkv
        d_ckvn = _dot_nt(dkb_ref[...], wuk_ref[...]) + _dot_nt(dvb, wuv_ref[...])
        d_n2 = d_ckvn * gkv_ref[...]
        dckv_ref[...] = (rkv * (d_n2 - nkv * jnp.mean(d_n2 * nkv, axis=-1, keepdims=True))).astype(BF16)

        @pl.when(i == 0)
        def _():
            small_ref[...] = jnp.zeros_like(small_ref)

        small_ref[0:1, :] += jnp.sum(d_cqn * nq_, axis=0, keepdims=True)
        small_ref[1:2, :KV_LORA] += jnp.sum(d_ckvn * nkv, axis=0, keepdims=True)

    def row(width, idx=0):
        return pl.BlockSpec((TM, width), lambda i: (i, idx))

    def full(a):
        return pl.BlockSpec(a.shape, lambda i: (0, 0))

    b16 = lambda w: jax.ShapeDtypeStruct((s, w), BF16)
    return pl.pallas_call(
        body, name="mla_prep_bwd", grid=(s // TM,),
        in_specs=[row(1024), row(1024), row(D_GRP), row(Q_LORA, 4), row(KV_LORA, 10), full(gq), full(gkv),
                  full(wuq), full(wuk), full(wuv), row(LANES), row(LANES)],
        out_specs=(row(Q_LORA), row(KV_LORA), row(LANES), row(1024), row(1024), row(D_GRP),
                   pl.BlockSpec((8, Q_LORA), lambda i: (0, 0))),
        out_shape=(b16(Q_LORA), b16(KV_LORA), b16(LANES), b16(1024), b16(1024), b16(D_GRP),
                   jax.ShapeDtypeStruct((8, Q_LORA), F32)),
        compiler_params=_params(("arbitrary",), 40),
    )(dqp, dkp, dvv, rest, rest, gq, gkv, wuq, wuk, wuv, cos_t, sin_t)


def _in_bwd(x, g, dx1, pieces, w):
    s = x.shape[0]
    widths = [a.shape[1] for a in pieces]
    offs = [sum(widths[:k]) for k in range(len(widths))]

    def body(x_ref, g_ref, dx1_ref, *refs):
        piece_refs = refs[:len(pieces)]
        w_ref, dx_ref, small_ref = refs[len(pieces):]
        i = pl.program_id(0)
        dh = jnp.zeros((TM, D_MODEL), F32)
        for pr, off, wd in zip(piece_refs, offs, widths):
            dh = dh + _dot_nt(pr[...], w_ref[:, off:off + wd])
        xv = x_ref[...]
        r = lax.rsqrt(jnp.mean(xv * xv, axis=-1, keepdims=True) + EPS)
        n = xv * r
        d_n = dh * g_ref[...]
        dx_ref[...] = dx1_ref[...] + r * (d_n - n * jnp.mean(d_n * n, axis=-1, keepdims=True))

        @pl.when(i == 0)
        def _():
            small_ref[...] = jnp.zeros_like(small_ref)

        small_ref[0:1, :] += jnp.sum(dh * n, axis=0, keepdims=True)

    def row(width):
        return pl.BlockSpec((TM, width), lambda i: (i, 0))

    return pl.pallas_call(
        body, name="in_bwd", grid=(s // TM,),
        in_specs=[row(D_MODEL), pl.BlockSpec((1, D_MODEL), lambda i: (0, 0)), row(D_MODEL)]
        + [row(wd) for wd in widths] + [pl.BlockSpec(w.shape, lambda i: (0, 0))],
        out_specs=(row(D_MODEL), pl.BlockSpec((8, D_MODEL), lambda i: (0, 0))),
        out_shape=(jax.ShapeDtypeStruct((s, D_MODEL), F32), jax.ShapeDtypeStruct((8, D_MODEL), F32)),
        compiler_params=_params(("arbitrary",), 48),
    )(x, g, dx1, *pieces, w)


def _tn_matmul(a, b, name, blocked=False):
    s, k = a.shape
    n = b.shape[1]
    ts = 512
    tn = n if blocked else min(n, 512)
    steps = s // ts

    def body(a_ref, b_ref, o_ref):
        t = pl.program_id(1)

        @pl.when(t == 0)
        def _():
            o_ref[...] = jnp.zeros_like(o_ref)

        prod = _dot_tn(a_ref[...], b_ref[...])
        if blocked:
            for j in range(n // LANES):
                o_ref[j] += prod[:, j * LANES:(j + 1) * LANES]
        else:
            o_ref[...] += prod

    if blocked:
        out_spec = pl.BlockSpec((n // LANES, k, LANES), lambda j, t: (0, 0, 0))
        out_shape = jax.ShapeDtypeStruct((n // LANES, k, LANES), F32)
    else:
        out_spec = pl.BlockSpec((k, tn), lambda j, t: (0, j))
        out_shape = jax.ShapeDtypeStruct((k, n), F32)
    return pl.pallas_call(
        body, name=name, grid=(n // tn, steps),
        in_specs=[pl.BlockSpec((ts, k), lambda j, t: (t, 0)), pl.BlockSpec((ts, tn), lambda j, t: (t, j))],
        out_specs=out_spec, out_shape=out_shape,
        compiler_params=_params(("parallel", "arbitrary"), 40),
    )(a, b)


def _tn_matmul_multi(a, bs, name):
    s, k = a.shape
    widths = [b.shape[1] for b in bs]
    ts = 512

    def body(a_ref, *refs):
        b_refs, o_ref = refs[:-1], refs[-1]
        t = pl.program_id(0)

        @pl.when(t == 0)
        def _():
            o_ref[...] = jnp.zeros_like(o_ref)

        av = a_ref[...]
        off = 0
        for b_ref, wd in zip(b_refs, widths):
            o_ref[:, off:off + wd] += _dot_tn(av, b_ref[...])
            off += wd

    return pl.pallas_call(
        body, name=name, grid=(s // ts,),
        in_specs=[pl.BlockSpec((ts, k), lambda t: (t, 0))] + [pl.BlockSpec((ts, wd), lambda t: (t, 0)) for wd in widths],
        out_specs=pl.BlockSpec((k, sum(widths)), lambda t: (0, 0)),
        out_shape=jax.ShapeDtypeStruct((k, sum(widths)), F32),
        compiler_params=_params(("arbitrary",), 40),
    )(a, *bs)


IN_SHARD = 372
_IN_KERNEL_ORDER = ((0, 2048), (2464, 2976), (2048, 2432))
_IN_ROPE = (2432, 2464)
_IN_GRAD_SRC = ((0, 512, 0, 0), (512, 1024, 0, 512), (1024, 1536, 1, 0), (1536, 2048, 1, 512),
                (2048, 2304, 2, 512), (2304, 2432, 2, 768), (2432, 2464, 2, 960), (2464, 2976, 2, 0))


def _shard_cols(gath_in, lo, hi):
    out = []
    while lo < hi:
        j, a = divmod(lo, IN_SHARD)
        b = min(IN_SHARD, a + hi - lo)
        out.append(gath_in[j][:, a:b])
        lo += b - a
    return out


def _kernel_w_in(g_in):
    zc = lambda n: jnp.zeros((D_MODEL, n), BF16)
    parts = [pc for lo, hi in _IN_KERNEL_ORDER for pc in _shard_cols(g_in, lo, hi)]
    parts += [zc(64)] + _shard_cols(g_in, *_IN_ROPE) + [zc(32)]
    return jnp.concatenate(parts, axis=1)


def _kernel_weights(gath):
    g_uq, g_ukv, g_out, g_ple, g_pg = gath
    w_uq_p = jnp.pad(g_uq, ((0, 0), (0, 0), (0, 32))).transpose(1, 0, 2).reshape(Q_LORA, 1024)
    k_only = jnp.where(jnp.arange(LANES) < HEAD_DIM, g_ukv, jnp.zeros_like(g_ukv))
    w_uk_p = k_only.transpose(1, 0, 2).reshape(KV_LORA, 1024)
    w_uv = g_ukv[:, :, HEAD_DIM:].transpose(1, 0, 2).reshape(KV_LORA, D_GRP)
    w_ple = g_ple.transpose(1, 0, 2).reshape(PLE_DIM, D_MODEL)
    return (w_uq_p, w_uk_p, w_uv, g_out.reshape(D_MODEL, D_MODEL), w_ple, g_pg.reshape(D_MODEL, D_MODEL))


def _grad_payloads(d_cols, duq_blk, duk_blk, d_uv, d_out, dple_blk, d_pg):
    blocks = []
    for j in range(N_DEV):
        lo, hi = j * IN_SHARD, (j + 1) * IN_SHARD
        parts = []
        for o_lo, o_hi, idx, off in _IN_GRAD_SRC:
            a, b = max(lo, o_lo), min(hi, o_hi)
            if a < b:
                parts.append(d_cols[idx][:, off + a - o_lo:off + b - o_lo])
        blocks.append(jnp.concatenate(parts, axis=1))
    pay_in = jnp.stack(blocks)
    dv_blk = d_uv.reshape(KV_LORA, N_HEADS, HEAD_DIM).transpose(1, 0, 2)
    pay_ukv = jnp.concatenate([duk_blk[:, :, :HEAD_DIM], dv_blk], axis=2)
    return [pay_in, duq_blk, pay_ukv, d_out.reshape(N_DEV, 128, D_MODEL), dple_blk,
            d_pg.reshape(N_DEV, 128, D_MODEL)]


def kernel(x, p, positions, norm_pre_g, w_in, q_norm_g, w_uq, kv_norm_g, w_ukv, sb_out_norm_g, mla_out_norm_g, w_out, norm_post_g, w_ple, ple_norm_g, w_ple_gate, b_ple_gate, loss_target, m_norm_pre_g, m_w_in, m_q_norm_g, m_w_uq, m_kv_norm_g, m_w_ukv, m_sb_out_norm_g, m_mla_out_norm_g, m_w_out, m_norm_post_g, m_w_ple, m_ple_norm_g, m_w_ple_gate, m_b_ple_gate, v_norm_pre_g, v_w_in, v_q_norm_g, v_w_uq, v_kv_norm_g, v_w_ukv, v_sb_out_norm_g, v_mla_out_norm_g, v_w_out, v_norm_post_g, v_w_ple, v_ple_norm_g, v_w_ple_gate, v_b_ple_gate):
    mats = (w_in, w_uq, w_ukv, w_out, w_ple, w_ple_gate)
    m_mats = (m_w_in, m_w_uq, m_w_ukv, m_w_out, m_w_ple, m_w_ple_gate)
    v_mats = (v_w_in, v_w_uq, v_w_ukv, v_w_out, v_w_ple, v_w_ple_gate)
    vecs = (norm_pre_g, q_norm_g, kv_norm_g, sb_out_norm_g, mla_out_norm_g, norm_post_g, ple_norm_g, b_ple_gate)
    m_vecs = (m_norm_pre_g, m_q_norm_g, m_kv_norm_g, m_sb_out_norm_g, m_mla_out_norm_g, m_norm_post_g,
              m_ple_norm_g, m_b_ple_gate)
    v_vecs = (v_norm_pre_g, v_q_norm_g, v_kv_norm_g, v_sb_out_norm_g, v_mla_out_norm_g, v_norm_post_g,
              v_ple_norm_g, v_b_ple_gate)

    shards = [a[0].astype(BF16) for a in mats]
    w_in_p = _kernel_w_in(_all_gather(shards[:1])[0])
    grad_x, d_parts, vec_slab = _local_grads(
        x[0], p[0, 0], positions[0], loss_target[0], *vecs, w_in_p, shards[1:])
    pays = _grad_payloads(*d_parts)
    landed1 = _pair_exchange(pays, vec_slab)
    place = jnp.stack([lax.axis_index("c"), 2 * lax.axis_index("x") + lax.axis_index("y")]).astype(jnp.int32)
    pair = [_pair_sum(g, l, place, "grad_pair_sum_%d" % o) for o, (g, l) in enumerate(zip(pays, landed1[:-1]))]
    landed2 = _chip_exchange([s for s, _ in pair])
    upd = [_adamw_matrix(own, l2, w[0], m[0], v[0], "adamw_%d" % o)
           for o, ((_, own), l2, w, m, v) in enumerate(zip(pair, landed2, mats, m_mats, v_mats))]
    sm = _adamw_vectors(landed1[-1], vecs, m_vecs, v_vecs)

    outs = []
    for kind in range(4):
        mat = [upd[o][kind][None] for o in range(len(mats))]
        vec = sm[1 + 8 * kind:9 + 8 * kind]
        outs += [vec[0], mat[0], vec[1], mat[1], vec[2], mat[2], vec[3], vec[4], mat[3], vec[5],
                 mat[4], vec[6], mat[5], vec[7]]
    return (sm[0][0, 0], grad_x[None], *outs)


def _local_grads(xs, ps, pos, tgt, norm_pre_g, q_norm_g, kv_norm_g, sb_out_norm_g, mla_out_norm_g,
                 norm_post_g, ple_norm_g, b_ple_gate, w_in_p, shards):
    s = xs.shape[0]

    half = ROPE_DIM // 2
    freq = ROPE_THETA ** (-jnp.arange(half, dtype=F32) / half)
    ang = pos.astype(F32)[:, None] * freq
    cos, sin = jnp.cos(ang), jnp.sin(ang)
    cos_t = jnp.concatenate([jnp.ones((s, 64), F32), cos, cos, jnp.zeros((s, 32), F32)], axis=1)
    sin_t = jnp.concatenate([jnp.zeros((s, 64), F32), -sin, sin, jnp.zeros((s, 32), F32)], axis=1)
    seg = jnp.arange(D_GRP) // HEAD_DIM
    bd = (seg[:, None] == seg[None, :]).astype(BF16)

    qkv, rest, h_b, *gath = _in_proj(xs, norm_pre_g, w_in_p, shards)
    w_uq_p, w_uk_p, w_uv, f_out, f_ple, f_pg = _kernel_weights(gath)
    sb_o = _sb_fwd(qkv, 8)
    qp, kp, vv, cqn_b, ckvn_b = _mla_prep(rest, q_norm_g, kv_norm_g, w_uq_p, w_uk_p, w_uv, cos_t, sin_t)
    mla_o, lse = _mla_fwd(qp, kp, vv, 4)

    (dx1, d_sbo, d_mlo, d_sbg, d_mlg, x1_b, dgl_b, yc_b, dy_b, p_b, du_b, small_mid) = _mid(
        xs, ps, tgt, sb_o, mla_o, rest, sb_out_norm_g, mla_out_norm_g, f_out, norm_post_g,
        f_ple, ple_norm_g, f_pg, b_ple_gate, bd)
    dqp, dkp, dvv = _mla_bwd(qp, kp, vv, d_mlo, mla_o, lse, 4)
    dq_sb, dk_sb, dv_sb = _sb_bwd(qkv, d_sbo)
    dcq, dckv, dkr, dq_b, dk_b, dv_b, small_prep = _mla_prep_bwd(
        dqp, dkp, dvv, rest, q_norm_g, kv_norm_g, w_uq_p, w_uk_p, w_uv, cos_t, sin_t)
    pieces = [dq_sb, dk_sb, dv_sb, d_sbg, d_mlg, dcq, dckv, dkr]
    grad_x, small_in = _in_bwd(xs, norm_pre_g, dx1, pieces, w_in_p)

    d_cols = [_tn_matmul_multi(h_b, pieces[0:2], "dw_in_0"), _tn_matmul_multi(h_b, pieces[2:4], "dw_in_1"),
              _tn_matmul_multi(h_b, pieces[4:8], "dw_in_2")]
    d_parts = (d_cols, _tn_matmul(cqn_b, dq_b, "dw_uq", blocked=True),
               _tn_matmul(ckvn_b, dk_b, "dw_uk", blocked=True), _tn_matmul(ckvn_b, dv_b, "dw_uv"),
               _tn_matmul(yc_b, dy_b, "dw_out"), _tn_matmul(p_b, du_b, "dw_ple", blocked=True),
               _tn_matmul(x1_b, dgl_b, "dw_pg"))
    slab = jnp.concatenate([small_in[0:1], jnp.pad(small_prep[0:2], ((0, 0), (0, D_MODEL - Q_LORA))),
                            small_mid[3:8]], axis=0)
    return grad_x, d_parts, slab
```

```python
import jax
import jax.numpy as jnp
from jax import lax
from jax.experimental import pallas as pl
from jax.experimental.pallas import tpu as pltpu

F32 = jnp.float32
BF16 = jnp.bfloat16
MESH = pl.DeviceIdType.MESH

N_DEV = 8
D_MODEL = 1024
N_HEADS = 8
HEAD_DIM = 64
D_GRP = N_HEADS * HEAD_DIM
Q_LORA = 256
KV_LORA = 128
ROPE_DIM = 32
PLE_DIM = 256
CHUNK_SHIFT = 6
ROPE_THETA = 10000.0
EPS = 1e-6
SB_SCALE = HEAD_DIM ** -0.5
MLA_SCALE = (HEAD_DIM + ROPE_DIM) ** -0.5
NEG = -1e30
LOG2_E = 1.4426950408889634
LN_2 = 0.6931471805599453
SB_CUTOFF = 110.0

ADAM_LR = 0.001
ADAM_B1 = 0.9
ADAM_B2 = 0.999
ADAM_EPS = 1e-08
ADAM_WD = 0.01
ADAM_STEP = 10

LANES = 128
TQ = 256
TK = 256
TM = 256

D_IN_P = 3072

_NT = (((1,), (1,)), ((), ()))
_TN = (((0,), (0,)), ((), ()))


def _params(sem, vmem_mb):
    return pltpu.CompilerParams(dimension_semantics=sem, vmem_limit_bytes=vmem_mb << 20)


def _dot(a, b):
    return jnp.dot(a, b, preferred_element_type=F32)


def _dot_nt(a, b):
    return lax.dot_general(a, b, _NT, preferred_element_type=F32)


def _dot_tn(a, b):
    return lax.dot_general(a, b, _TN, preferred_element_type=F32)


def _hl_dot(a, b):
    hi = a.astype(BF16)
    lo = (a - hi.astype(F32)).astype(BF16)
    return _dot(hi, b) + _dot(lo, b)


def _sigmoid(x):
    return 1.0 / (1.0 + jnp.exp(-x))


def _rope_swap(x, lane):
    left = pltpu.roll(x, LANES - 16, axis=1)
    right = pltpu.roll(x, 16, axis=1)
    lo = (lane >= 64) & (lane < 80)
    hi = (lane >= 80) & (lane < 96)
    return jnp.where(lo, left, jnp.where(hi, right, 0.0))


def _two_level_gather(x_refs, out_refs, send_sems, recv_sems, local_sems):
    x, y, c = lax.axis_index("x"), lax.axis_index("y"), lax.axis_index("c")
    me, sibling = (x, y, c), (x, y, 1 - c)
    chips = [(1 - x, y), (x, 1 - y), (1 - x, 1 - y)]
    ops = range(len(x_refs))

    def slot(o, px, py, pc):
        return out_refs[o].at[4 * px + 2 * py + pc]

    def copy(o, k, block, to, src=None):
        return pltpu.make_async_remote_copy(
            src_ref=slot(o, *block) if src is None else src, dst_ref=slot(o, *block),
            send_sem=send_sems.at[o, k], recv_sem=recv_sems.at[o, k],
            device_id=to, device_id_type=MESH)

    def mine():
        return [pltpu.make_async_copy(x_refs[o], slot(o, *me), local_sems.at[o]) for o in ops]

    def first():
        return ([copy(o, 0, me, sibling, src=x_refs[o]) for o in ops]
                + [copy(o, 1 + j, me, (*chip, c), src=x_refs[o]) for j, chip in enumerate(chips) for o in ops])

    def start():
        for cp in mine() + first():
            cp.start()

    def finish():
        passed = []
        for j, chip in enumerate(chips):
            for o in ops:
                copy(o, 1 + j, (*chip, c), me).wait_recv()
                passed.append(copy(o, 4 + j, (*chip, c), sibling))
                passed[-1].start()
        for o in ops:
            copy(o, 0, sibling, me).wait_recv()
        for j, chip in enumerate(chips):
            for o in ops:
                copy(o, 4 + j, (*chip, 1 - c), me).wait_recv()
        for cp in first() + passed:
            cp.wait_send()
        for cp in mine():
            cp.wait()

    return start, finish


def _gather_sems(n_op):
    return [pltpu.SemaphoreType.DMA((n_op, 7)), pltpu.SemaphoreType.DMA((n_op, 7)),
            pltpu.SemaphoreType.DMA((n_op,))]


def _all_gather(shards):
    n_op = len(shards)

    def body(*refs):
        start, finish = _two_level_gather(refs[:n_op], refs[n_op:2 * n_op], *refs[2 * n_op:])
        start()
        finish()

    vmem = pl.BlockSpec(memory_space=pltpu.VMEM)
    return pl.pallas_call(
        body, name="weight_all_gather",
        out_shape=[jax.ShapeDtypeStruct((N_DEV,) + a.shape, a.dtype) for a in shards],
        in_specs=[vmem] * n_op, out_specs=[vmem] * n_op, scratch_shapes=_gather_sems(n_op),
        compiler_params=pltpu.CompilerParams(vmem_limit_bytes=48 << 20),
    )(*shards)


def _pair_exchange(pays, name):
    n_op = len(pays)

    def body(*refs):
        g_refs, l_refs = refs[:n_op], refs[n_op:2 * n_op]
        ssem, rsem = refs[2 * n_op:]
        x, y, c = lax.axis_index("x"), lax.axis_index("y"), lax.axis_index("c")
        copies = []
        for o in range(n_op):
            for chip in range(4):
                copies.append(pltpu.make_async_remote_copy(
                    src_ref=g_refs[o].at[2 * chip + (1 - c)], dst_ref=l_refs[o].at[chip],
                    send_sem=ssem.at[o, chip], recv_sem=rsem.at[o, chip],
                    device_id=(x, y, 1 - c), device_id_type=MESH))
        for cp in copies:
            cp.start()
        for cp in copies:
            cp.wait()

    any_spec = pl.BlockSpec(memory_space=pl.ANY)
    return pl.pallas_call(
        body, name=name,
        out_shape=[jax.ShapeDtypeStruct((4,) + a.shape[1:], F32) for a in pays],
        in_specs=[any_spec] * n_op, out_specs=[any_spec] * n_op,
        scratch_shapes=[pltpu.SemaphoreType.DMA((n_op, 4)), pltpu.SemaphoreType.DMA((n_op, 4))],
    )(*pays)


def _slab_exchange(small):
    sr, n = small.shape

    def body(s_ref, sland_ref, ssem, rsem, lsem):
        x, y, c = lax.axis_index("x"), lax.axis_index("y"), lax.axis_index("c")
        me = 4 * x + 2 * y + c
        copies = []
        for k in range(1, N_DEV):
            peer = (1 - x if (k >> 2) & 1 else x, 1 - y if (k >> 1) & 1 else y, 1 - c if k & 1 else c)
            copies.append(pltpu.make_async_remote_copy(
                src_ref=s_ref, dst_ref=sland_ref.at[me], send_sem=ssem.at[k], recv_sem=rsem.at[k],
                device_id=peer, device_id_type=MESH))
        own = pltpu.make_async_copy(s_ref, sland_ref.at[me], lsem)
        own.start()
        for cp in copies:
            cp.start()
        for cp in copies:
            cp.wait()
        own.wait()

    vmem = pl.BlockSpec(memory_space=pltpu.VMEM)
    return pl.pallas_call(
        body, name="grad_slab_exchange", out_shape=jax.ShapeDtypeStruct((N_DEV, sr, n), F32),
        in_specs=[vmem], out_specs=vmem,
        scratch_shapes=[pltpu.SemaphoreType.DMA((N_DEV,)), pltpu.SemaphoreType.DMA((N_DEV,)),
                        pltpu.SemaphoreType.DMA],
    )(small)


def _pair_sum(pay, landed, place, name):
    _, r, c = pay.shape

    def body(place_ref, g_ref, l_ref, s_ref, own_ref):
        i = pl.program_id(0)
        tot = g_ref[...] + l_ref[...]
        s_ref[...] = tot.astype(BF16)

        @pl.when(i == place_ref[1])
        def _():
            own_ref[...] = tot

    grid_spec = pltpu.PrefetchScalarGridSpec(
        num_scalar_prefetch=1, grid=(4,),
        in_specs=[pl.BlockSpec((None, r, c), lambda i, pr: (2 * i + pr[0], 0, 0)),
                  pl.BlockSpec((None, r, c), lambda i, pr: (i, 0, 0))],
        out_specs=[pl.BlockSpec((None, r, c), lambda i, pr: (i, 0, 0)),
                   pl.BlockSpec((r, c), lambda i, pr: (0, 0))])
    return pl.pallas_call(
        body, name=name, grid_spec=grid_spec,
        out_shape=[jax.ShapeDtypeStruct((4, r, c), BF16), jax.ShapeDtypeStruct((r, c), F32)],
        compiler_params=_params(("arbitrary",), 40),
    )(place, pay, landed)


def _chip_copies(s_refs, l_refs, ssem, rsem):
    x, y, c = lax.axis_index("x"), lax.axis_index("y"), lax.axis_index("c")
    copies = []
    for rel in range(1, 4):
        px = 1 - x if rel & 2 else x
        py = 1 - y if rel & 1 else y
        for o in range(len(s_refs)):
            copies.append(pltpu.make_async_remote_copy(
                src_ref=s_refs[o].at[2 * px + py], dst_ref=l_refs[o].at[rel - 1],
                send_sem=ssem.at[o, rel - 1], recv_sem=rsem.at[o, rel - 1],
                device_id=(px, py, c), device_id_type=MESH))
    return copies


def _chip_specs(sums):
    n_op = len(sums)
    any_spec = pl.BlockSpec(memory_space=pl.ANY)
    return ([any_spec] * n_op, [any_spec] * n_op,
            [jax.ShapeDtypeStruct((3,) + a.shape[1:], BF16) for a in sums],
            [pltpu.SemaphoreType.DMA((n_op, 3)), pltpu.SemaphoreType.DMA((n_op, 3))])


def _adamw_math(g, w, m, v):
    mn = ADAM_B1 * m + (1.0 - ADAM_B1) * g
    vn = ADAM_B2 * v + (1.0 - ADAM_B2) * (g * g)
    m_hat = mn / (1.0 - ADAM_B1 ** ADAM_STEP)
    v_hat = vn / (1.0 - ADAM_B2 ** ADAM_STEP)
    return -ADAM_LR * (m_hat / (jnp.sqrt(v_hat) + ADAM_EPS) + ADAM_WD * w), mn, vn


def _adamw_matrix(own, landed, w, m, v, name):
    r, c = w.shape
    cp = own.shape[1]
    br = min(r, 256)

    def body(own_ref, l_ref, w_ref, m_ref, v_ref, g_out, d_out, m_out, v_out):
        g = own_ref[...]
        for k in range(3):
            g = g + l_ref[k].astype(F32)
        g = g[:, :c]
        g_out[...] = g
        d_out[...], m_out[...], v_out[...] = _adamw_math(g, w_ref[...], m_ref[...], v_ref[...])

    row = pl.BlockSpec((br, c), lambda i: (i, 0))
    shp = jax.ShapeDtypeStruct((r, c), F32)
    return pl.pallas_call(
        body, name=name, grid=(r // br,),
        in_specs=[pl.BlockSpec((br, cp), lambda i: (i, 0)), pl.BlockSpec((3, br, cp), lambda i: (0, i, 0)),
                  row, row, row],
        out_specs=(row, row, row, row), out_shape=(shp, shp, shp, shp),
        compiler_params=_params(("parallel",), 40),
    )(own, landed, w, m, v)


_VEC_PLACE = ((0, 0), (1, 0), (2, 0), (3, 0), (3, D_GRP), (4, 0), (5, 0), (6, 0))


def _adamw_vectors(sland, ws, ms, vs):
    nv = len(ws)

    def body(l_ref, *refs):
        w_refs, m_refs, v_refs = refs[:nv], refs[nv:2 * nv], refs[2 * nv:3 * nv]
        loss_ref = refs[3 * nv]
        outs = refs[3 * nv + 1:]
        g_all = l_ref[0]
        for j in range(1, N_DEV):
            g_all = g_all + l_ref[j]
        loss_ref[...] = jnp.sum(g_all[7:8, :], axis=1, keepdims=True)
        for k, (row, lane0) in enumerate(_VEC_PLACE):
            n = w_refs[k].shape[1]
            g = g_all[row:row + 1, lane0:lane0 + n]
            d, mn, vn = _adamw_math(g, w_refs[k][...], m_refs[k][...], v_refs[k][...])
            outs[k][...] = g
            outs[nv + k][...] = d
            outs[2 * nv + k][...] = mn
            outs[3 * nv + k][...] = vn

    vmem = pl.BlockSpec(memory_space=pltpu.VMEM)
    shapes = [jax.ShapeDtypeStruct(w.shape, F32) for w in ws]
    return pl.pallas_call(
        body, name="adamw_vectors", in_specs=[vmem] * (1 + 3 * nv), out_specs=[vmem] * (1 + 4 * nv),
        out_shape=[jax.ShapeDtypeStruct((1, 1), F32)] + shapes * 4,
    )(sland, *ws, *ms, *vs)


def _in_proj(x, g, w, shards):
    s = x.shape[0]
    n_op = len(shards)
    steps = s // TM

    def body(x_ref, g_ref, w_ref, *refs):
        shard_refs = refs[:n_op]
        qkv_ref, rest_ref, h_ref = refs[n_op:n_op + 3]
        gath_refs = refs[n_op + 3:2 * n_op + 3]
        start, finish = _two_level_gather(shard_refs, gath_refs, *refs[2 * n_op + 3:])
        i = pl.program_id(0)

        @pl.when(i == 0)
        def _():
            start()

        xv = x_ref[...]
        r = lax.rsqrt(jnp.mean(xv * xv, axis=-1, keepdims=True) + EPS)
        h = ((xv * r) * g_ref[...]).astype(BF16)
        h_ref[...] = h
        qkv_ref[...] = _dot(h, w_ref[:, :1536]).astype(BF16)
        rest_ref[...] = _dot(h, w_ref[:, 1536:])

        @pl.when(i == steps - 1)
        def _():
            finish()

    any_spec = pl.BlockSpec(memory_space=pl.ANY)
    return pl.pallas_call(
        body, name="in_proj", grid=(steps,),
        in_specs=[pl.BlockSpec((TM, D_MODEL), lambda i: (i, 0)),
                  pl.BlockSpec((1, D_MODEL), lambda i: (0, 0)),
                  pl.BlockSpec((D_MODEL, D_IN_P), lambda i: (0, 0))] + [any_spec] * n_op,
        out_specs=[pl.BlockSpec((TM, 1536), lambda i: (i, 0)),
                   pl.BlockSpec((TM, 1536), lambda i: (i, 0)),
                   pl.BlockSpec((TM, D_MODEL), lambda i: (i, 0))] + [any_spec] * n_op,
        out_shape=[jax.ShapeDtypeStruct((s, 1536), BF16), jax.ShapeDtypeStruct((s, 1536), F32),
                   jax.ShapeDtypeStruct((s, D_MODEL), BF16)]
        + [jax.ShapeDtypeStruct((N_DEV,) + a.shape, a.dtype) for a in shards],
        scratch_shapes=_gather_sems(n_op),
        compiler_params=_params(("arbitrary",), 48),
    )(x, g, w, *shards)


def _mla_prep(rest, gq, gkv, wuq, wuk, wuv, cos_t, sin_t):
    s = rest.shape[0]

    def body(cq_ref, ckv_ref, kr_ref, gq_ref, gkv_ref, wuq_ref, wuk_ref, wuv_ref, c_ref, s_ref,
             qp_ref, kp_ref, vv_ref, cqn_ref, ckvn_ref):
        lane = lax.broadcasted_iota(jnp.int32, (1, LANES), 1)
        cos_v, sin_v = c_ref[...], s_ref[...]
        cq = cq_ref[...]
        rq = lax.rsqrt(jnp.mean(cq * cq, axis=-1, keepdims=True) + EPS)
        cqn = ((cq * rq) * gq_ref[...]).astype(BF16)
        cqn_ref[...] = cqn
        q = _dot(cqn, wuq_ref[...])
        ckv = ckv_ref[...]
        rkv = lax.rsqrt(jnp.mean(ckv * ckv, axis=-1, keepdims=True) + EPS)
        ckvn = ((ckv * rkv) * gkv_ref[...]).astype(BF16)
        ckvn_ref[...] = ckvn
        kn = _dot(ckvn, wuk_ref[...])
        vv_ref[...] = _dot(ckvn, wuv_ref[...]).astype(BF16)
        kr = kr_ref[...]
        kr_roped = kr * cos_v + _rope_swap(kr, lane) * sin_v
        for h in range(N_HEADS):
            sl = slice(h * LANES, (h + 1) * LANES)
            qh = q[:, sl]
            qp_ref[:, sl] = (qh * cos_v + _rope_swap(qh, lane) * sin_v).astype(BF16)
            kp_ref[:, sl] = (kn[:, sl] + kr_roped).astype(BF16)

    def row(width, idx):
        return pl.BlockSpec((TM, width), lambda i: (i, idx))

    def full(a):
        return pl.BlockSpec(a.shape, lambda i: (0, 0))

    return pl.pallas_call(
        body, name="mla_prep", grid=(s // TM,),
        in_specs=[row(Q_LORA, 4), row(KV_LORA, 10), row(LANES, 11), full(gq), full(gkv),
                  full(wuq), full(wuk), full(wuv), row(LANES, 0), row(LANES, 0)],
        out_specs=(row(1024, 0), row(1024, 0), row(D_GRP, 0), row(Q_LORA, 0), row(KV_LORA, 0)),
        out_shape=(jax.ShapeDtypeStruct((s, 1024), BF16), jax.ShapeDtypeStruct((s, 1024), BF16),
                   jax.ShapeDtypeStruct((s, D_GRP), BF16), jax.ShapeDtypeStruct((s, Q_LORA), BF16),
                   jax.ShapeDtypeStruct((s, KV_LORA), BF16)),
        compiler_params=_params(("parallel",), 32),
    )(rest, rest, rest, gq, gkv, wuq, wuk, wuv, cos_t, sin_t)


def _sb_live(n, qi, carries):
    top = carries[0]
    for c in carries[1:]:
        top = jnp.maximum(top, c)
    return jnp.logical_and(n < qi, jnp.max(top) > -SB_CUTOFF)


def _sb_fwd(qkv, hb):
    s = qkv.shape[0]

    def body(q_ref, k_ref, v_ref, o_ref, acc):
        qi = pl.program_id(1)
        lane = lax.broadcasted_iota(jnp.int32, (1, LANES), 1)
        is_a = lane < HEAD_DIM
        pair = lambda h: slice((h // 2) * LANES, (h // 2 + 1) * LANES)
        q_h = []
        for h in range(hb):
            qs = q_ref[:, pair(h)] * SB_SCALE
            mine = is_a if h % 2 == 0 else jnp.logical_not(is_a)
            q_h.append(jnp.where(mine, qs, jnp.zeros_like(qs)))
        r_i = lax.broadcasted_iota(jnp.int32, (TQ, TK), 0)
        c_i = lax.broadcasted_iota(jnp.int32, (TQ, TK), 1)
        past = c_i < r_i
        upper = (r_i > c_i).astype(BF16)
        acc[...] = jnp.zeros_like(acc)

        def tile(j, carries, diag):
            ks = pl.ds(pl.multiple_of(j * TK, TK), TK)
            zs = [_dot_nt(q_h[h], k_ref[ks, pair(h)]) for h in range(hb)]
            if diag:
                zs = [jnp.where(past, z, NEG) for z in zs]
            lfs = [-(jnp.maximum(z, 0.0) + jnp.log(1.0 + jnp.exp(-jnp.abs(z)))) for z in zs]
            sufs = [_hl_dot(lfs[h], upper) for h in range(hb)]
            out = []
            for h in range(hb):
                w = jnp.exp(zs[h] + lfs[h] + (sufs[h] + carries[h]))
                acc[h] += _dot(w.astype(BF16), v_ref[ks, pair(h)])
                out.append(carries[h] + jnp.sum(lfs[h], axis=1, keepdims=True))
            return tuple(out)

        zero = jnp.zeros((TQ, 1), F32)
        carries = tile(qi, (zero,) * hb, True)

        def step(st):
            return (st[0] + 1,) + tile(qi - 1 - st[0], st[1:], False)

        lax.while_loop(lambda st: _sb_live(st[0], qi, st[1:]), step, (0,) + carries)
        for pr in range(hb // 2):
            o_ref[:, pr * LANES:(pr + 1) * LANES] = jnp.where(is_a, acc[2 * pr], acc[2 * pr + 1])

    width = hb * HEAD_DIM
    nb = D_GRP // width
    slab = lambda part: pl.BlockSpec((s, width), lambda g, qi: (0, part * nb + g))
    blk = pl.BlockSpec((TQ, width), lambda g, qi: (qi, g))
    return pl.pallas_call(
        body, name="sb_fwd", grid=(nb, s // TQ),
        in_specs=[blk, slab(1), slab(2)], out_specs=blk,
        out_shape=jax.ShapeDtypeStruct((s, D_GRP), F32),
        scratch_shapes=[pltpu.VMEM((hb, TQ, LANES), F32)],
        compiler_params=_params(("arbitrary", "arbitrary"), 48),
    )(qkv, qkv, qkv)


def _sb_bwd(qkv, d_o, sums):
    s = qkv.shape[0]
    nq = s // TQ
    nk = s // TK
    n_op = len(sums)
    ride_in, ride_out, ride_shape, ride_sems = _chip_specs(sums)

    def body(q_ref, k_ref, v_ref, do_ref, *refs):
        s_refs = refs[:n_op]
        dq_ref, dk_ref, dv_ref = refs[n_op:n_op + 3]
        l_refs = refs[n_op + 3:2 * n_op + 3]
        x1s, bts, dqacc, dkacc, dvacc, ssem, rsem = refs[2 * n_op + 3:]
        qi = pl.program_id(1)
        first_step = jnp.logical_and(pl.program_id(0) == 0, qi == 0)
        last_step = jnp.logical_and(pl.program_id(0) == pl.num_programs(0) - 1, qi == nq - 1)

        @pl.when(first_step)
        def _():
            for cp in _chip_copies(s_refs, l_refs, ssem, rsem):
                cp.start()

        lane = lax.broadcasted_iota(jnp.int32, (1, LANES), 1)
        is_a = lane < HEAD_DIM

        @pl.when(qi == 0)
        def _():
            dkacc[...] = jnp.zeros_like(dkacc)
            dvacc[...] = jnp.zeros_like(dvacc)

        qs = q_ref[...] * SB_SCALE
        zq = jnp.zeros_like(qs)
        qs_x = (jnp.where(is_a, qs, zq), jnp.where(is_a, zq, qs))
        dob = do_ref[...].astype(BF16)
        do_x = (jnp.where(is_a, dob, zq), jnp.where(is_a, zq, dob))
        r_i = lax.broadcasted_iota(jnp.int32, (TQ, TK), 0)
        c_i = lax.broadcasted_iota(jnp.int32, (TQ, TK), 1)
        past = c_i < r_i
        upper = (r_i > c_i).astype(BF16)
        upper_incl = (r_i >= c_i).astype(BF16)
        dqacc[...] = jnp.zeros_like(dqacc)
        both = ((0, 0), (0, 1), (1, 0), (1, 1))

        def tiles(n):
            j_hi = qi - 2 * n
            lo_ok = j_hi >= 1
            j_lo = jnp.maximum(j_hi - 1, 0)
            ks = (pl.ds(pl.multiple_of(j_hi * TK, TK), TK), pl.ds(pl.multiple_of(j_lo * TK, TK), TK))
            return j_hi, lo_ok, j_lo, ks

        def sweep(n, carries):
            j_hi, lo_ok, j_lo, ks = tiles(n)
            slot = (j_hi, jnp.where(lo_ok, j_lo, nk))
            valid = (jnp.logical_or(past, j_hi < qi), lo_ok)
            z = {th: jnp.where(valid[th[0]], _dot_nt(qs_x[th[1]], k_ref[ks[th[0]], :]), NEG) for th in both}
            d_a = {th: _dot_nt(do_x[th[1]], v_ref[ks[th[0]], :]) for th in both}
            lf, beta, omb = {}, {}, {}
            for th in both:
                e = jnp.exp(-jnp.abs(z[th]))
                den = 1.0 + e
                rden = 1.0 / den
                pos = z[th] >= 0.0
                lf[th] = -(jnp.maximum(z[th], 0.0) + jnp.log(den))
                beta[th] = jnp.where(pos, rden, e * rden)
                omb[th] = jnp.where(pos, e * rden, rden)
            suf = {th: _hl_dot(lf[th], upper) for th in both}
            c, g_in = {}, {}
            for h in range(2):
                c[0, h], g_in[0, h] = carries[2 * h], carries[2 * h + 1]
                c[1, h] = c[0, h] + jnp.sum(lf[0, h], axis=1, keepdims=True)
            a, g = {}, {}
            for th in both:
                a[th] = jnp.exp(z[th] + lf[th] + (suf[th] + c[th]))
                g[th] = a[th] * d_a[th]
            sg = {th: _hl_dot(g[th], upper_incl) for th in both}
            for h in range(2):
                g_in[1, h] = g_in[0, h] + jnp.sum(g[0, h], axis=1, keepdims=True)
            for th in both:
                t, h = th
                x1s[slot[t], h] = g[th] * omb[th] + beta[th] * (sg[th] + g_in[th])
                bts[slot[t], h] = beta[th]
                dvacc[ks[t], :] += _dot_tn(a[th].astype(BF16), do_x[h])
            out = []
            for h in range(2):
                out.append(c[1, h] + jnp.sum(lf[1, h], axis=1, keepdims=True))
                out.append(g_in[1, h] + jnp.sum(g[1, h], axis=1, keepdims=True))
            return tuple(out)

        zero = jnp.zeros((TQ, 1), F32)
        first = sweep(0, (zero, zero, zero, zero))

        def more(st):
            return jnp.logical_and(2 * st[0] <= qi, jnp.max(jnp.maximum(st[1], st[3])) > -SB_CUTOFF)

        swept = lax.while_loop(more, lambda st: (st[0] + 1,) + sweep(st[0], st[1:]), (1,) + first)
        g_tot = (swept[2], swept[4])

        def apply(n, carry):
            j_hi, lo_ok, j_lo, ks = tiles(n)

            def one(j, kslice):
                for h in range(2):
                    dz = (x1s[j, h] - bts[j, h] * g_tot[h]).astype(BF16)
                    dqacc[h] += _dot(dz, k_ref[kslice, :])
                    dkacc[kslice, :] += _dot_tn(dz, qs_x[h])

            one(j_hi, ks[0])

            @pl.when(lo_ok)
            def _():
                one(j_lo, ks[1])

            return carry

        lax.fori_loop(0, swept[0], apply, 0)
        dq_ref[...] = (jnp.where(is_a, dqacc[0], dqacc[1]) * SB_SCALE).astype(BF16)

        @pl.when(qi == nq - 1)
        def _():
            dk_ref[...] = dkacc[...].astype(BF16)
            dv_ref[...] = dvacc[...].astype(BF16)

        @pl.when(last_step)
        def _():
            for cp in _chip_copies(s_refs, l_refs, ssem, rsem):
                cp.wait()

    slab = lambda off: pl.BlockSpec((s, LANES), lambda p, qi: (0, off + p))
    blk = pl.BlockSpec((TQ, LANES), lambda p, qi: (qi, p))
    out_slab = pl.BlockSpec((s, LANES), lambda p, qi: (0, p))
    shp = jax.ShapeDtypeStruct((s, D_GRP), BF16)
    return pl.pallas_call(
        body, name="sb_bwd", grid=(4, nq),
        in_specs=[blk, slab(4), slab(8), blk] + ride_in,
        out_specs=[blk, out_slab, out_slab] + ride_out, out_shape=[shp, shp, shp] + ride_shape,
        scratch_shapes=[pltpu.VMEM((nk + 1, 2, TQ, TK), F32)] * 2
        + [pltpu.VMEM((2, TQ, LANES), F32), pltpu.VMEM((s, LANES), F32), pltpu.VMEM((s, LANES), F32)]
        + ride_sems,
        compiler_params=_params(("arbitrary", "arbitrary"), 56),
    )(qkv, qkv, qkv, d_o, *sums)


def _mla_fwd(qp, kp, vv, hb):
    s = qp.shape[0]
    c2 = MLA_SCALE * LOG2_E

    def body(q_ref, k_ref, v_ref, o_ref, lse_ref, vaug, mrun, mb, acc, zbuf):
        qi = pl.program_id(1)
        lane = lax.broadcasted_iota(jnp.int32, (1, LANES), 1)
        is_a = lane < HEAD_DIM

        @pl.when(qi == 0)
        def _():
            for h in range(hb):
                vp = v_ref[:, (h // 2) * LANES:(h // 2 + 1) * LANES]
                mine = is_a if h % 2 == 0 else jnp.logical_not(is_a)
                vaug[h] = jnp.where(mine, vp, jnp.ones_like(vp))

        r_i = lax.broadcasted_iota(jnp.int32, (TQ, TK), 0)
        c_i = lax.broadcasted_iota(jnp.int32, (TQ, TK), 1)
        visible = (c_i >> CHUNK_SHIFT) <= (r_i >> CHUNK_SHIFT)

        def key_rows(j):
            return pl.ds(pl.multiple_of(j * TK, TK), TK)

        def sweep(tiles):
            def loop(n, carry):
                tiles(((2 * n, False), (2 * n + 1, False)))
                return carry

            lax.fori_loop(0, qi // 2, loop, 0)

            @pl.when(qi % 2 == 1)
            def _():
                tiles(((qi - 1, False), (qi, True)))

            @pl.when(qi % 2 == 0)
            def _():
                tiles(((qi, True),))

        mrun[...] = jnp.full_like(mrun, NEG)

        def tiles_max(js):
            zs = [[_dot_nt(q_ref[:, h * LANES:(h + 1) * LANES], k_ref[key_rows(j), h * LANES:(h + 1) * LANES])
                   for h in range(hb)] for j, _ in js]
            for t, (j, diag) in enumerate(js):
                for h in range(hb):
                    z = jnp.where(visible, zs[t][h], NEG) if diag else zs[t][h]
                    zbuf[j, h] = z
                    mrun[h] = jnp.maximum(mrun[h], z)

        sweep(tiles_max)
        for h in range(hb):
            m = jnp.max(mrun[h], axis=1, keepdims=True) * c2
            mb[h] = jnp.broadcast_to(m, (TQ, TK))
        acc[...] = jnp.zeros_like(acc)

        def tiles_pv(js):
            ps = [[jnp.exp2(zbuf[j, h] * c2 - mb[h]).astype(BF16) for h in range(hb)] for j, _ in js]
            for t, (j, _) in enumerate(js):
                for h in range(hb):
                    acc[h] += _dot(ps[t][h], vaug[h, key_rows(j), :])

        sweep(tiles_pv)
        for pr in range(hb // 2):
            a, b = 2 * pr, 2 * pr + 1
            psl = slice(pr * LANES, (pr + 1) * LANES)
            acc_a, acc_b = acc[a], acc[b]
            l_a = pltpu.roll(acc_a, HEAD_DIM, axis=1)
            l_b = pltpu.roll(acc_b, HEAD_DIM, axis=1)
            o_ref[:, psl] = jnp.where(is_a, acc_a * (1.0 / l_a), acc_b * (1.0 / l_b))
            lse_ref[:, psl] = jnp.where(is_a, mb[a, :, :LANES] * LN_2 + jnp.log(l_a),
                                        mb[b, :, :LANES] * LN_2 + jnp.log(l_b))

    blk = pl.BlockSpec((TQ, hb * HEAD_DIM), lambda g, qi: (qi, g))
    shp = jax.ShapeDtypeStruct((s, D_GRP), F32)
    return pl.pallas_call(
        body, name="mla_fwd", grid=(N_HEADS // hb, s // TQ),
        in_specs=[pl.BlockSpec((TQ, hb * LANES), lambda g, qi: (qi, g)),
                  pl.BlockSpec((s, hb * LANES), lambda g, qi: (0, g)),
                  pl.BlockSpec((s, hb * HEAD_DIM), lambda g, qi: (0, g))],
        out_specs=(blk, blk), out_shape=(shp, shp),
        scratch_shapes=[pltpu.VMEM((hb, s, LANES), BF16), pltpu.VMEM((hb, TQ, TK), F32),
                        pltpu.VMEM((hb, TQ, TK), F32), pltpu.VMEM((hb, TQ, LANES), F32),
                        pltpu.VMEM((s // TK, hb, TQ, TK), F32)],
        compiler_params=_params(("arbitrary", "arbitrary"), 56),
    )(qp, kp, vv)


def _mla_bwd(qp, kp, vv, d_o, o, lse, hb):
    s = qp.shape[0]
    c2 = MLA_SCALE * LOG2_E

    def body(q_ref, k_ref, v_ref, do_ref, o_ref, lse_ref, dq_ref, dk_ref, dv_ref, dqacc, lse_b, delta_b):
        qi = pl.program_id(1)
        lane = lax.broadcasted_iota(jnp.int32, (1, LANES), 1)
        is_a = lane < HEAD_DIM

        @pl.when(qi == 0)
        def _():
            dk_ref[...] = jnp.zeros_like(dk_ref)
            dv_ref[...] = jnp.zeros_like(dv_ref)

        r_i = lax.broadcasted_iota(jnp.int32, (TQ, TK), 0)
        c_i = lax.broadcasted_iota(jnp.int32, (TQ, TK), 1)
        visible = (c_i >> CHUNK_SHIFT) <= (r_i >> CHUNK_SHIFT)
        do_x = []
        for h in range(hb):
            psl = slice((h // 2) * LANES, (h // 2 + 1) * LANES)
            mine = is_a if h % 2 == 0 else jnp.logical_not(is_a)
            d_o = do_ref[:, psl]
            delta = jnp.sum(jnp.where(mine, d_o * o_ref[:, psl], 0.0), axis=1, keepdims=True)
            lse_h = jnp.sum(jnp.where(lane == (h % 2) * HEAD_DIM, lse_ref[:, psl], 0.0), axis=1, keepdims=True)
            lse_b[h] = jnp.broadcast_to(lse_h * LOG2_E, (TQ, TK))
            delta_b[h] = jnp.broadcast_to(delta, (TQ, TK))
            do_x.append(jnp.where(mine, d_o, 0.0).astype(BF16))
        dqacc[...] = jnp.zeros_like(dqacc)

        head = lambda h: slice(h * LANES, (h + 1) * LANES)
        pair = lambda h: slice((h // 2) * LANES, (h // 2 + 1) * LANES)

        def tiles(js):
            th = [(j, diag, pl.ds(pl.multiple_of(j * TK, TK), TK), h) for j, diag in js for h in range(hb)]
            zs = [_dot_nt(q_ref[:, head(h)], k_ref[ks, head(h)]) for _, _, ks, h in th]
            dps = [_dot_nt(do_x[h], v_ref[ks, pair(h)]) for _, _, ks, h in th]
            for i, (j, diag, ks, h) in enumerate(th):
                e = zs[i] * c2 - lse_b[h]
                if diag:
                    e = jnp.where(visible, e, NEG)
                p = jnp.exp2(e)
                ds = (p * (dps[i] - delta_b[h]) * MLA_SCALE).astype(BF16)
                dqacc[h] += _dot(ds, k_ref[ks, head(h)])
                dk_ref[ks, head(h)] += _dot_tn(ds, q_ref[:, head(h)])
                dv_ref[ks, pair(h)] += _dot_tn(p.astype(BF16), do_x[h])

        def loop(n, c):
            tiles(((2 * n, False), (2 * n + 1, False)))
            return c

        lax.fori_loop(0, qi // 2, loop, 0)

        @pl.when(qi % 2 == 1)
        def _():
            tiles(((qi - 1, False), (qi, True)))

        @pl.when(qi % 2 == 0)
        def _():
            tiles(((qi, True),))

        for h in range(hb):
            dq_ref[:, h * LANES:(h + 1) * LANES] = dqacc[h]

    blk = pl.BlockSpec((TQ, hb * HEAD_DIM), lambda g, qi: (qi, g))
    return pl.pallas_call(
        body, name="mla_bwd", grid=(N_HEADS // hb, s // TQ),
        in_specs=[pl.BlockSpec((TQ, hb * LANES), lambda g, qi: (qi, g)),
                  pl.BlockSpec((s, hb * LANES), lambda g, qi: (0, g)),
                  pl.BlockSpec((s, hb * HEAD_DIM), lambda g, qi: (0, g)), blk, blk, blk],
        out_specs=(pl.BlockSpec((TQ, hb * LANES), lambda g, qi: (qi, g)),
                   pl.BlockSpec((s, hb * LANES), lambda g, qi: (0, g)),
                   pl.BlockSpec((s, hb * HEAD_DIM), lambda g, qi: (0, g))),
        out_shape=(jax.ShapeDtypeStruct((s, 1024), F32), jax.ShapeDtypeStruct((s, 1024), F32),
                   jax.ShapeDtypeStruct((s, D_GRP), F32)),
        scratch_shapes=[pltpu.VMEM((hb, TQ, LANES), F32), pltpu.VMEM((hb, TQ, TK), F32),
                        pltpu.VMEM((hb, TQ, TK), F32)],
        compiler_params=_params(("arbitrary", "arbitrary"), 56),
    )(qp, kp, vv, d_o, o, lse)


def _mid(x, p, target, sb_o, mla_o, rest, g_sb, g_mla, w_out, g_post, w_ple, g_ple, w_pg, b_pg, bd):
    s = x.shape[0]

    def body(x_ref, p_ref, t_ref, sbo_ref, mlo_ref, sbg_ref, mlg_ref, gsb_ref, gml_ref, wout_ref,
             gpost_ref, wple_ref, gple_ref, wpg_ref, bpg_ref, bd_ref,
             dx1_ref, dsbo_ref, dmlo_ref, dsbg_ref, dmlg_ref, x1b_ref, dglb_ref, ycb_ref, dyb_ref,
             pb_ref, dub_ref, small_ref):
        i = pl.program_id(0)
        bd_m = bd_ref[...]

        def seg_mean(v):
            return _dot(v.astype(BF16), bd_m) * (1.0 / HEAD_DIM)

        groups = []
        for o_ref, gate_ref, gain_ref in ((sbo_ref, sbg_ref, gsb_ref), (mlo_ref, mlg_ref, gml_ref)):
            o = o_ref[...]
            r = lax.rsqrt(seg_mean(o * o) + EPS)
            n = o * r
            hn = n * gain_ref[...]
            gate = gate_ref[...]
            sg = _sigmoid(gate)
            si = gate * sg
            groups.append((r, n, hn, gate, sg, si, gain_ref[...]))
        ya = (groups[0][2] * groups[0][5]).astype(BF16)
        yb = (groups[1][2] * groups[1][5]).astype(BF16)
        ycb_ref[:, :D_GRP] = ya
        ycb_ref[:, D_GRP:] = yb
        y = _dot(ya, wout_ref[:D_GRP, :]) + _dot(yb, wout_ref[D_GRP:, :])
        ry = lax.rsqrt(jnp.mean(y * y, axis=-1, keepdims=True) + EPS)
        ny = y * ry
        x1 = x_ref[...] + ny * gpost_ref[...]
        x1b = x1.astype(BF16)
        x1b_ref[...] = x1b
        pb = p_ref[...].astype(BF16)
        pb_ref[...] = pb
        u = _dot(pb, wple_ref[...])
        ru = lax.rsqrt(jnp.mean(u * u, axis=-1, keepdims=True) + EPS)
        nu = u * ru
        ple = nu * gple_ref[...]
        gate = _sigmoid(_dot(x1b, wpg_ref[...]) + bpg_ref[...])
        x2 = x1 + ple * gate
        diff = x2 - t_ref[...]
        dx2 = diff * (1.0 / D_MODEL)

        d_ple = dx2 * gate
        d_glin = (dx2 * ple) * (gate * (1.0 - gate))
        dglb = d_glin.astype(BF16)
        dglb_ref[...] = dglb
        dx1 = dx2 + _dot_nt(dglb, wpg_ref[...])
        dx1_ref[...] = dx1
        d_nu = d_ple * gple_ref[...]
        d_u = ru * (d_nu - nu * jnp.mean(d_nu * nu, axis=-1, keepdims=True))
        dub_ref[...] = d_u.astype(BF16)
        d_ny = dx1 * gpost_ref[...]
        d_y = ry * (d_ny - ny * jnp.mean(d_ny * ny, axis=-1, keepdims=True))
        dyb = d_y.astype(BF16)
        dyb_ref[...] = dyb
        d_yc = (_dot_nt(dyb, wout_ref[:D_GRP, :]), _dot_nt(dyb, wout_ref[D_GRP:, :]))

        d_gain = []
        for gx, (do_ref, dg_ref) in enumerate(((dsbo_ref, dsbg_ref), (dmlo_ref, dmlg_ref))):
            r, n, hn, gate_g, sg, si, gain = groups[gx]
            dyg = d_yc[gx]
            d_hn = dyg * si
            dg_ref[...] = (dyg * hn * (sg * (1.0 + gate_g * (1.0 - sg)))).astype(BF16)
            d_gain.append(jnp.sum(d_hn * n, axis=0, keepdims=True))
            d_n = d_hn * gain
            do_ref[...] = r * (d_n - n * seg_mean(d_n * n))

        @pl.when(i == 0)
        def _():
            small_ref[...] = jnp.zeros_like(small_ref)

        small_ref[3:4, :D_GRP] += d_gain[0]
        small_ref[3:4, D_GRP:] += d_gain[1]
        small_ref[4:5, :] += jnp.sum(dx1 * ny, axis=0, keepdims=True)
        small_ref[5:6, :] += jnp.sum(d_ple * nu, axis=0, keepdims=True)
        small_ref[6:7, :] += jnp.sum(d_glin, axis=0, keepdims=True)
        small_ref[7:8, :] += jnp.sum(diff * diff, axis=0, keepdims=True) * (0.5 / D_MODEL)

    def row(width, idx=0):
        return pl.BlockSpec((TM, width), lambda i: (i, idx))

    def full(a):
        return pl.BlockSpec(a.shape, lambda i: (0, 0))

    f32 = lambda w: jax.ShapeDtypeStruct((s, w), F32)
    b16 = lambda w: jax.ShapeDtypeStruct((s, w), BF16)
    return pl.pallas_call(
        body, name="mid", grid=(s // TM,),
        in_specs=[row(D_MODEL), row(PLE_DIM), row(D_MODEL), row(D_GRP), row(D_GRP),
                  row(D_GRP, 0), row(D_GRP, 1), full(g_sb), full(g_mla), full(w_out), full(g_post),
                  full(w_ple), full(g_ple), full(w_pg), full(b_pg), full(bd)],
        out_specs=(row(D_MODEL), row(D_GRP), row(D_GRP), row(D_GRP), row(D_GRP), row(D_MODEL),
                   row(D_MODEL), row(D_MODEL), row(D_MODEL), row(PLE_DIM), row(D_MODEL),
                   pl.BlockSpec((8, D_MODEL), lambda i: (0, 0))),
        out_shape=(f32(D_MODEL), f32(D_GRP), f32(D_GRP), b16(D_GRP), b16(D_GRP), b16(D_MODEL),
                   b16(D_MODEL), b16(D_MODEL), b16(D_MODEL), b16(PLE_DIM), b16(D_MODEL),
                   jax.ShapeDtypeStruct((8, D_MODEL), F32)),
        compiler_params=_params(("arbitrary",), 56),
    )(x, p, target, sb_o, mla_o, rest, rest, g_sb, g_mla, w_out, g_post, w_ple, g_ple, w_pg, b_pg, bd)


def _mla_prep_bwd(dqp, dkp, dvv, rest, gq, gkv, wuq, wuk, wuv, cos_t, sin_t):
    s = rest.shape[0]

    def body(dqp_ref, dkp_ref, dvv_ref, cq_ref, ckv_ref, gq_ref, gkv_ref, wuq_ref, wuk_ref, wuv_ref,
             c_ref, s_ref, dcq_ref, dckv_ref, dkr_ref, dqb_ref, dkb_ref, dvb_ref, small_ref):
        i = pl.program_id(0)
        lane = lax.broadcasted_iota(jnp.int32, (1, LANES), 1)
        in_rope = (lane >= HEAD_DIM) & (lane < HEAD_DIM + ROPE_DIM)
        cos_v, sin_v = c_ref[...], s_ref[...]
        dkr_roped = jnp.zeros((TM, LANES), F32)
        for h in range(N_HEADS):
            sl = slice(h * LANES, (h + 1) * LANES)
            dy = dqp_ref[:, sl]
            dqb_ref[:, sl] = (dy * cos_v + _rope_swap(dy * sin_v, lane)).astype(BF16)
            dkh = dkp_ref[:, sl]
            dkb_ref[:, sl] = dkh.astype(BF16)
            dkr_roped = dkr_roped + jnp.where(in_rope, dkh, 0.0)
        dkr_ref[...] = (dkr_roped * cos_v + _rope_swap(dkr_roped * sin_v, lane)).astype(BF16)
        dvb = dvv_ref[...].astype(BF16)
        dvb_ref[...] = dvb

        cq = cq_ref[...]
        rq = lax.rsqrt(jnp.mean(cq * cq, axis=-1, keepdims=True) + EPS)
        nq_ = cq * rq
        d_cqn = _dot_nt(dqb_ref[...], wuq_ref[...])
        d_n = d_cqn * gq_ref[...]
        dcq_ref[...] = (rq * (d_n - nq_ * jnp.mean(d_n * nq_, axis=-1, keepdims=True))).astype(BF16)

        ckv = ckv_ref[...]
        rkv = lax.rsqrt(jnp.mean(ckv * ckv, axis=-1, keepdims=True) + EPS)
        nkv = ckv * rkv
        d_ckvn = _dot_nt(dkb_ref[...], wuk_ref[...]) + _dot_nt(dvb, wuv_ref[...])
        d_n2 = d_ckvn * gkv_ref[...]
        dckv_ref[...] = (rkv * (d_n2 - nkv * jnp.mean(d_n2 * nkv, axis=-1, keepdims=True))).astype(BF16)

        @pl.when(i == 0)
        def _():
            small_ref[...] = jnp.zeros_like(small_ref)

        small_ref[0:1, :] += jnp.sum(d_cqn * nq_, axis=0, keepdims=True)
        small_ref[1:2, :KV_LORA] += jnp.sum(d_ckvn * nkv, axis=0, keepdims=True)

    def row(width, idx=0):
        return pl.BlockSpec((TM, width), lambda i: (i, idx))

    def full(a):
        return pl.BlockSpec(a.shape, lambda i: (0, 0))

    b16 = lambda w: jax.ShapeDtypeStruct((s, w), BF16)
    return pl.pallas_call(
        body, name="mla_prep_bwd", grid=(s // TM,),
        in_specs=[row(1024), row(1024), row(D_GRP), row(Q_LORA, 4), row(KV_LORA, 10), full(gq), full(gkv),
                  full(wuq), full(wuk), full(wuv), row(LANES), row(LANES)],
        out_specs=(row(Q_LORA), row(KV_LORA), row(LANES), row(1024), row(1024), row(D_GRP),
                   pl.BlockSpec((8, Q_LORA), lambda i: (0, 0))),
        out_shape=(b16(Q_LORA), b16(KV_LORA), b16(LANES), b16(1024), b16(1024), b16(D_GRP),
                   jax.ShapeDtypeStruct((8, Q_LORA), F32)),
        compiler_params=_params(("arbitrary",), 40),
    )(dqp, dkp, dvv, rest, rest, gq, gkv, wuq, wuk, wuv, cos_t, sin_t)


def _in_bwd(x, g, dx1, pieces, w, sums):
    s = x.shape[0]
    steps = s // TM
    widths = [a.shape[1] for a in pieces]
    offs = [sum(widths[:k]) for k in range(len(widths))]
    n_pc, n_op = len(pieces), len(sums)
    ride_in, ride_out, ride_shape, ride_sems = _chip_specs(sums)

    def body(x_ref, g_ref, dx1_ref, *refs):
        piece_refs = refs[:n_pc]
        w_ref = refs[n_pc]
        s_refs = refs[n_pc + 1:n_pc + 1 + n_op]
        dx_ref, small_ref = refs[n_pc + 1 + n_op:n_pc + 3 + n_op]
        l_refs = refs[n_pc + 3 + n_op:n_pc + 3 + 2 * n_op]
        ssem, rsem = refs[n_pc + 3 + 2 * n_op:]
        i = pl.program_id(0)

        @pl.when(i == 0)
        def _():
            for cp in _chip_copies(s_refs, l_refs, ssem, rsem):
                cp.start()

        dh = jnp.zeros((TM, D_MODEL), F32)
        for pr, off, wd in zip(piece_refs, offs, widths):
            dh = dh + _dot_nt(pr[...], w_ref[:, off:off + wd])
        xv = x_ref[...]
        r = lax.rsqrt(jnp.mean(xv * xv, axis=-1, keepdims=True) + EPS)
        n = xv * r
        d_n = dh * g_ref[...]
        dx_ref[...] = dx1_ref[...] + r * (d_n - n * jnp.mean(d_n * n, axis=-1, keepdims=True))

        @pl.when(i == 0)
        def _():
            small_ref[...] = jnp.zeros_like(small_ref)

        small_ref[0:1, :] += jnp.sum(dh * n, axis=0, keepdims=True)

        @pl.when(i == steps - 1)
        def _():
            for cp in _chip_copies(s_refs, l_refs, ssem, rsem):
                cp.wait()

    def row(width):
        return pl.BlockSpec((TM, width), lambda i: (i, 0))

    return pl.pallas_call(
        body, name="in_bwd", grid=(steps,),
        in_specs=[row(D_MODEL), pl.BlockSpec((1, D_MODEL), lambda i: (0, 0)), row(D_MODEL)]
        + [row(wd) for wd in widths] + [pl.BlockSpec(w.shape, lambda i: (0, 0))] + ride_in,
        out_specs=[row(D_MODEL), pl.BlockSpec((8, D_MODEL), lambda i: (0, 0))] + ride_out,
        out_shape=[jax.ShapeDtypeStruct((s, D_MODEL), F32), jax.ShapeDtypeStruct((8, D_MODEL), F32)]
        + ride_shape,
        scratch_shapes=ride_sems,
        compiler_params=_params(("arbitrary",), 48),
    )(x, g, dx1, *pieces, w, *sums)


def _tn_matmul(a, b, name, blocked=False):
    s, k = a.shape
    n = b.shape[1]
    ts = 512
    tn = n if blocked else min(n, 512)
    steps = s // ts

    def body(a_ref, b_ref, o_ref):
        t = pl.program_id(1)

        @pl.when(t == 0)
        def _():
            o_ref[...] = jnp.zeros_like(o_ref)

        prod = _dot_tn(a_ref[...], b_ref[...])
        if blocked:
            for j in range(n // LANES):
                o_ref[j] += prod[:, j * LANES:(j + 1) * LANES]
        else:
            o_ref[...] += prod

    if blocked:
        out_spec = pl.BlockSpec((n // LANES, k, LANES), lambda j, t: (0, 0, 0))
        out_shape = jax.ShapeDtypeStruct((n // LANES, k, LANES), F32)
    else:
        out_spec = pl.BlockSpec((k, tn), lambda j, t: (0, j))
        out_shape = jax.ShapeDtypeStruct((k, n), F32)
    return pl.pallas_call(
        body, name=name, grid=(n // tn, steps),
        in_specs=[pl.BlockSpec((ts, k), lambda j, t: (t, 0)), pl.BlockSpec((ts, tn), lambda j, t: (t, j))],
        out_specs=out_spec, out_shape=out_shape,
        compiler_params=_params(("parallel", "arbitrary"), 40),
    )(a, b)


def _tn_matmul_multi(a, bs, name):
    s, k = a.shape
    widths = [b.shape[1] for b in bs]
    ts = 512

    def body(a_ref, *refs):
        b_refs, o_ref = refs[:-1], refs[-1]
        t = pl.program_id(0)

        @pl.when(t == 0)
        def _():
            o_ref[...] = jnp.zeros_like(o_ref)

        av = a_ref[...]
        off = 0
        for b_ref, wd in zip(b_refs, widths):
            o_ref[:, off:off + wd] += _dot_tn(av, b_ref[...])
            off += wd

    return pl.pallas_call(
        body, name=name, grid=(s // ts,),
        in_specs=[pl.BlockSpec((ts, k), lambda t: (t, 0))] + [pl.BlockSpec((ts, wd), lambda t: (t, 0)) for wd in widths],
        out_specs=pl.BlockSpec((k, sum(widths)), lambda t: (0, 0)),
        out_shape=jax.ShapeDtypeStruct((k, sum(widths)), F32),
        compiler_params=_params(("arbitrary",), 40),
    )(a, *bs)


IN_SHARD = 372
_IN_KERNEL_ORDER = ((0, 2048), (2464, 2976), (2048, 2432))
_IN_ROPE = (2432, 2464)
_IN_GRAD_SRC = ((0, 512, 0, 0), (512, 1024, 0, 512), (1024, 1536, 1, 0), (1536, 2048, 1, 512),
                (2048, 2304, 2, 512), (2304, 2432, 2, 768), (2432, 2464, 2, 960), (2464, 2976, 2, 0))


def _shard_cols(gath_in, lo, hi):
    out = []
    while lo < hi:
        j, a = divmod(lo, IN_SHARD)
        b = min(IN_SHARD, a + hi - lo)
        out.append(gath_in[j][:, a:b])
        lo += b - a
    return out


def _kernel_w_in(g_in):
    zc = lambda n: jnp.zeros((D_MODEL, n), BF16)
    parts = [pc for lo, hi in _IN_KERNEL_ORDER for pc in _shard_cols(g_in, lo, hi)]
    parts += [zc(64)] + _shard_cols(g_in, *_IN_ROPE) + [zc(32)]
    return jnp.concatenate(parts, axis=1)


def _kernel_weights(gath):
    g_uq, g_ukv, g_out, g_ple, g_pg = gath
    w_uq_p = jnp.pad(g_uq, ((0, 0), (0, 0), (0, 32))).transpose(1, 0, 2).reshape(Q_LORA, 1024)
    k_only = jnp.where(jnp.arange(LANES) < HEAD_DIM, g_ukv, jnp.zeros_like(g_ukv))
    w_uk_p = k_only.transpose(1, 0, 2).reshape(KV_LORA, 1024)
    w_uv = g_ukv[:, :, HEAD_DIM:].transpose(1, 0, 2).reshape(KV_LORA, D_GRP)
    w_ple = g_ple.transpose(1, 0, 2).reshape(PLE_DIM, D_MODEL)
    return (w_uq_p, w_uk_p, w_uv, g_out.reshape(D_MODEL, D_MODEL), w_ple, g_pg.reshape(D_MODEL, D_MODEL))


def _payload_in(d_cols):
    blocks = []
    for j in range(N_DEV):
        lo, hi = j * IN_SHARD, (j + 1) * IN_SHARD
        parts = []
        for o_lo, o_hi, idx, off in _IN_GRAD_SRC:
            a, b = max(lo, o_lo), min(hi, o_hi)
            if a < b:
                parts.append(d_cols[idx][:, off + a - o_lo:off + b - o_lo])
        blocks.append(jnp.concatenate(parts, axis=1))
    return jnp.stack(blocks)


def _payload_ukv(duk_blk, d_uv):
    dv_blk = d_uv.reshape(KV_LORA, N_HEADS, HEAD_DIM).transpose(1, 0, 2)
    return jnp.concatenate([duk_blk[:, :, :HEAD_DIM], dv_blk], axis=2)


def _reduce_pairs(pays, place, tag):
    landed = _pair_exchange(pays, "grad_pair_exchange_" + tag)
    return [_pair_sum(g, l, place, "grad_pair_sum_%s%d" % (tag, o)) for o, (g, l) in enumerate(zip(pays, landed))]


def kernel(x, p, positions, norm_pre_g, w_in, q_norm_g, w_uq, kv_norm_g, w_ukv, sb_out_norm_g, mla_out_norm_g, w_out, norm_post_g, w_ple, ple_norm_g, w_ple_gate, b_ple_gate, loss_target, m_norm_pre_g, m_w_in, m_q_norm_g, m_w_uq, m_kv_norm_g, m_w_ukv, m_sb_out_norm_g, m_mla_out_norm_g, m_w_out, m_norm_post_g, m_w_ple, m_ple_norm_g, m_w_ple_gate, m_b_ple_gate, v_norm_pre_g, v_w_in, v_q_norm_g, v_w_uq, v_kv_norm_g, v_w_ukv, v_sb_out_norm_g, v_mla_out_norm_g, v_w_out, v_norm_post_g, v_w_ple, v_ple_norm_g, v_w_ple_gate, v_b_ple_gate):
    mats = (w_in, w_uq, w_ukv, w_out, w_ple, w_ple_gate)
    m_mats = (m_w_in, m_w_uq, m_w_ukv, m_w_out, m_w_ple, m_w_ple_gate)
    v_mats = (v_w_in, v_w_uq, v_w_ukv, v_w_out, v_w_ple, v_w_ple_gate)
    vecs = (norm_pre_g, q_norm_g, kv_norm_g, sb_out_norm_g, mla_out_norm_g, norm_post_g, ple_norm_g, b_ple_gate)
    m_vecs = (m_norm_pre_g, m_q_norm_g, m_kv_norm_g, m_sb_out_norm_g, m_mla_out_norm_g, m_norm_post_g,
              m_ple_norm_g, m_b_ple_gate)
    v_vecs = (v_norm_pre_g, v_q_norm_g, v_kv_norm_g, v_sb_out_norm_g, v_mla_out_norm_g, v_norm_post_g,
              v_ple_norm_g, v_b_ple_gate)

    shards = [a[0].astype(BF16) for a in mats]
    w_in_p = _kernel_w_in(_all_gather(shards[:1])[0])
    grad_x, reduced, vec_slab = _step(x[0], p[0, 0], positions[0], loss_target[0], *vecs, w_in_p, shards[1:])
    upd = [_adamw_matrix(own, l2, w[0], m[0], v[0], "adamw_%d" % o)
           for o, ((own, l2), w, m, v) in enumerate(zip(reduced, mats, m_mats, v_mats))]
    sm = _adamw_vectors(_slab_exchange(vec_slab), vecs, m_vecs, v_vecs)

    outs = []
    for kind in range(4):
        mat = [upd[o][kind][None] for o in range(len(mats))]
        vec = sm[1 + 8 * kind:9 + 8 * kind]
        outs += [vec[0], mat[0], vec[1], mat[1], vec[2], mat[2], vec[3], vec[4], mat[3], vec[5],
                 mat[4], vec[6], mat[5], vec[7]]
    return (sm[0][0, 0], grad_x[None], *outs)


def _step(xs, ps, pos, tgt, norm_pre_g, q_norm_g, kv_norm_g, sb_out_norm_g, mla_out_norm_g,
          norm_post_g, ple_norm_g, b_ple_gate, w_in_p, shards):
    s = xs.shape[0]
    place = jnp.stack([lax.axis_index("c"), 2 * lax.axis_index("x") + lax.axis_index("y")]).astype(jnp.int32)

    half = ROPE_DIM // 2
    freq = ROPE_THETA ** (-jnp.arange(half, dtype=F32) / half)
    ang = pos.astype(F32)[:, None] * freq
    cos, sin = jnp.cos(ang), jnp.sin(ang)
    cos_t = jnp.concatenate([jnp.ones((s, 64), F32), cos, cos, jnp.zeros((s, 32), F32)], axis=1)
    sin_t = jnp.concatenate([jnp.zeros((s, 64), F32), -sin, sin, jnp.zeros((s, 32), F32)], axis=1)
    seg = jnp.arange(D_GRP) // HEAD_DIM
    bd = (seg[:, None] == seg[None, :]).astype(BF16)

    qkv, rest, h_b, *gath = _in_proj(xs, norm_pre_g, w_in_p, shards)
    w_uq_p, w_uk_p, w_uv, f_out, f_ple, f_pg = _kernel_weights(gath)
    sb_o = _sb_fwd(qkv, 8)
    qp, kp, vv, cqn_b, ckvn_b = _mla_prep(rest, q_norm_g, kv_norm_g, w_uq_p, w_uk_p, w_uv, cos_t, sin_t)
    mla_o, lse = _mla_fwd(qp, kp, vv, 4)

    (dx1, d_sbo, d_mlo, d_sbg, d_mlg, x1_b, dgl_b, yc_b, dy_b, p_b, du_b, small_mid) = _mid(
        xs, ps, tgt, sb_o, mla_o, rest, sb_out_norm_g, mla_out_norm_g, f_out, norm_post_g,
        f_ple, ple_norm_g, f_pg, b_ple_gate, bd)
    pay_a = [_tn_matmul(yc_b, dy_b, "dw_out").reshape(N_DEV, 128, D_MODEL),
             _tn_matmul(p_b, du_b, "dw_ple", blocked=True),
             _tn_matmul(x1_b, dgl_b, "dw_pg").reshape(N_DEV, 128, D_MODEL)]
    pair_a = _reduce_pairs(pay_a, place, "a")
    dqp, dkp, dvv = _mla_bwd(qp, kp, vv, d_mlo, mla_o, lse, 4)
    dq_sb, dk_sb, dv_sb, *landed_a = _sb_bwd(qkv, d_sbo, [sm for sm, _ in pair_a])
    dcq, dckv, dkr, dq_b, dk_b, dv_b, small_prep = _mla_prep_bwd(
        dqp, dkp, dvv, rest, q_norm_g, kv_norm_g, w_uq_p, w_uk_p, w_uv, cos_t, sin_t)
    pieces = [dq_sb, dk_sb, dv_sb, d_sbg, d_mlg, dcq, dckv, dkr]
    d_cols = [_tn_matmul_multi(h_b, pieces[0:2], "dw_in_0"), _tn_matmul_multi(h_b, pieces[2:4], "dw_in_1"),
              _tn_matmul_multi(h_b, pieces[4:8], "dw_in_2")]
    pay_b = [_payload_in(d_cols), _tn_matmul(cqn_b, dq_b, "dw_uq", blocked=True),
             _payload_ukv(_tn_matmul(ckvn_b, dk_b, "dw_uk", blocked=True), _tn_matmul(ckvn_b, dv_b, "dw_uv"))]
    pair_b = _reduce_pairs(pay_b, place, "b")
    grad_x, small_in, *landed_b = _in_bwd(xs, norm_pre_g, dx1, pieces, w_in_p, [sm for sm, _ in pair_b])
    reduced = [(own, l2) for (_, own), l2 in zip(pair_b + pair_a, landed_b + landed_a)]
    slab = jnp.concatenate([small_in[0:1], jnp.pad(small_prep[0:2], ((0, 0), (0, D_MODEL - Q_LORA))),
                            small_mid[3:8]], axis=0)
    return grad_x, reduced, slab
```

```python
import jax
import jax.numpy as jnp
from jax import lax
from jax.experimental import pallas as pl
from jax.experimental.pallas import tpu as pltpu

F32 = jnp.float32
BF16 = jnp.bfloat16
MESH = pl.DeviceIdType.MESH

N_DEV = 8
D_MODEL = 1024
N_HEADS = 8
HEAD_DIM = 64
D_GRP = N_HEADS * HEAD_DIM
Q_LORA = 256
KV_LORA = 128
ROPE_DIM = 32
PLE_DIM = 256
CHUNK_SHIFT = 6
ROPE_THETA = 10000.0
EPS = 1e-6
SB_SCALE = HEAD_DIM ** -0.5
MLA_SCALE = (HEAD_DIM + ROPE_DIM) ** -0.5
NEG = -1e30
LOG2_E = 1.4426950408889634
LN_2 = 0.6931471805599453
SB_CUTOFF = 110.0

ADAM_LR = 0.001
ADAM_B1 = 0.9
ADAM_B2 = 0.999
ADAM_EPS = 1e-08
ADAM_WD = 0.01
ADAM_STEP = 10

LANES = 128
TQ = 256
TK = 256
TM = 256

D_IN_P = 3072

_NT = (((1,), (1,)), ((), ()))
_TN = (((0,), (0,)), ((), ()))


def _params(sem, vmem_mb):
    return pltpu.CompilerParams(dimension_semantics=sem, vmem_limit_bytes=vmem_mb << 20)


def _dot(a, b):
    return jnp.dot(a, b, preferred_element_type=F32)


def _dot_nt(a, b):
    return lax.dot_general(a, b, _NT, preferred_element_type=F32)


def _dot_tn(a, b):
    return lax.dot_general(a, b, _TN, preferred_element_type=F32)


def _hl_dot(a, b):
    hi = a.astype(BF16)
    lo = (a - hi.astype(F32)).astype(BF16)
    return _dot(hi, b) + _dot(lo, b)


def _sigmoid(x):
    return 1.0 / (1.0 + jnp.exp(-x))


def _rope_swap(x, lane):
    left = pltpu.roll(x, LANES - 16, axis=1)
    right = pltpu.roll(x, 16, axis=1)
    lo = (lane >= 64) & (lane < 80)
    hi = (lane >= 80) & (lane < 96)
    return jnp.where(lo, left, jnp.where(hi, right, 0.0))


def _two_level_gather(x_refs, out_refs, send_sems, recv_sems, local_sems):
    x, y, c = lax.axis_index("x"), lax.axis_index("y"), lax.axis_index("c")
    me, sibling = (x, y, c), (x, y, 1 - c)
    chips = [(1 - x, y), (x, 1 - y), (1 - x, 1 - y)]
    ops = range(len(x_refs))

    def slot(o, px, py, pc):
        return out_refs[o].at[4 * px + 2 * py + pc]

    def copy(o, k, block, to, src=None):
        return pltpu.make_async_remote_copy(
            src_ref=slot(o, *block) if src is None else src, dst_ref=slot(o, *block),
            send_sem=send_sems.at[o, k], recv_sem=recv_sems.at[o, k],
            device_id=to, device_id_type=MESH)

    def mine():
        return [pltpu.make_async_copy(x_refs[o], slot(o, *me), local_sems.at[o]) for o in ops]

    def first():
        return ([copy(o, 0, me, sibling, src=x_refs[o]) for o in ops]
                + [copy(o, 1 + j, me, (*chip, c), src=x_refs[o]) for j, chip in enumerate(chips) for o in ops])

    def start():
        for cp in mine() + first():
            cp.start()

    def finish():
        passed = []
        for j, chip in enumerate(chips):
            for o in ops:
                copy(o, 1 + j, (*chip, c), me).wait_recv()
                passed.append(copy(o, 4 + j, (*chip, c), sibling))
                passed[-1].start()
        for o in ops:
            copy(o, 0, sibling, me).wait_recv()
        for j, chip in enumerate(chips):
            for o in ops:
                copy(o, 4 + j, (*chip, 1 - c), me).wait_recv()
        for cp in first() + passed:
            cp.wait_send()
        for cp in mine():
            cp.wait()

    return start, finish


def _gather_sems(n_op):
    return [pltpu.SemaphoreType.DMA((n_op, 7)), pltpu.SemaphoreType.DMA((n_op, 7)),
            pltpu.SemaphoreType.DMA((n_op,))]


def _all_gather(shards):
    n_op = len(shards)

    def body(*refs):
        start, finish = _two_level_gather(refs[:n_op], refs[n_op:2 * n_op], *refs[2 * n_op:])
        start()
        finish()

    vmem = pl.BlockSpec(memory_space=pltpu.VMEM)
    return pl.pallas_call(
        body, name="weight_all_gather",
        out_shape=[jax.ShapeDtypeStruct((N_DEV,) + a.shape, a.dtype) for a in shards],
        in_specs=[vmem] * n_op, out_specs=[vmem] * n_op, scratch_shapes=_gather_sems(n_op),
        compiler_params=pltpu.CompilerParams(vmem_limit_bytes=48 << 20),
    )(*shards)


def _pair_copies(g_refs, l_refs, ssem, rsem):
    x, y, c = lax.axis_index("x"), lax.axis_index("y"), lax.axis_index("c")
    copies = []
    for o in range(len(g_refs)):
        for chip in range(4):
            copies.append(pltpu.make_async_remote_copy(
                src_ref=g_refs[o].at[2 * chip + (1 - c)], dst_ref=l_refs[o].at[chip],
                send_sem=ssem.at[o, chip], recv_sem=rsem.at[o, chip],
                device_id=(x, y, 1 - c), device_id_type=MESH))
    return copies


def _pair_specs(pays):
    n_op = len(pays)
    any_spec = pl.BlockSpec(memory_space=pl.ANY)
    return ([any_spec] * n_op, [any_spec] * n_op,
            [jax.ShapeDtypeStruct((4,) + a.shape[1:], F32) for a in pays],
            [pltpu.SemaphoreType.DMA((n_op, 4)), pltpu.SemaphoreType.DMA((n_op, 4))])


def _pair_exchange(pays, name):
    n_op = len(pays)
    in_specs, out_specs, out_shape, sems = _pair_specs(pays)

    def body(*refs):
        copies = _pair_copies(refs[:n_op], refs[n_op:2 * n_op], *refs[2 * n_op:])
        for cp in copies:
            cp.start()
        for cp in copies:
            cp.wait()

    return pl.pallas_call(body, name=name, out_shape=out_shape, in_specs=in_specs, out_specs=out_specs,
                          scratch_shapes=sems)(*pays)


def _slab_exchange(small):
    sr, n = small.shape

    def body(s_ref, sland_ref, ssem, rsem, lsem):
        x, y, c = lax.axis_index("x"), lax.axis_index("y"), lax.axis_index("c")
        me = 4 * x + 2 * y + c
        copies = []
        for k in range(1, N_DEV):
            peer = (1 - x if (k >> 2) & 1 else x, 1 - y if (k >> 1) & 1 else y, 1 - c if k & 1 else c)
            copies.append(pltpu.make_async_remote_copy(
                src_ref=s_ref, dst_ref=sland_ref.at[me], send_sem=ssem.at[k], recv_sem=rsem.at[k],
                device_id=peer, device_id_type=MESH))
        own = pltpu.make_async_copy(s_ref, sland_ref.at[me], lsem)
        own.start()
        for cp in copies:
            cp.start()
        for cp in copies:
            cp.wait()
        own.wait()

    vmem = pl.BlockSpec(memory_space=pltpu.VMEM)
    return pl.pallas_call(
        body, name="grad_slab_exchange", out_shape=jax.ShapeDtypeStruct((N_DEV, sr, n), F32),
        in_specs=[vmem], out_specs=vmem,
        scratch_shapes=[pltpu.SemaphoreType.DMA((N_DEV,)), pltpu.SemaphoreType.DMA((N_DEV,)),
                        pltpu.SemaphoreType.DMA],
    )(small)


def _pair_sum(pay, landed, place, name):
    _, r, c = pay.shape

    def body(place_ref, g_ref, l_ref, s_ref, own_ref):
        i = pl.program_id(0)
        tot = g_ref[...] + l_ref[...]
        s_ref[...] = tot.astype(BF16)

        @pl.when(i == place_ref[1])
        def _():
            own_ref[...] = tot

    grid_spec = pltpu.PrefetchScalarGridSpec(
        num_scalar_prefetch=1, grid=(4,),
        in_specs=[pl.BlockSpec((None, r, c), lambda i, pr: (2 * i + pr[0], 0, 0)),
                  pl.BlockSpec((None, r, c), lambda i, pr: (i, 0, 0))],
        out_specs=[pl.BlockSpec((None, r, c), lambda i, pr: (i, 0, 0)),
                   pl.BlockSpec((r, c), lambda i, pr: (0, 0))])
    return pl.pallas_call(
        body, name=name, grid_spec=grid_spec,
        out_shape=[jax.ShapeDtypeStruct((4, r, c), BF16), jax.ShapeDtypeStruct((r, c), F32)],
        compiler_params=_params(("arbitrary",), 40),
    )(place, pay, landed)


def _chip_copies(s_refs, l_refs, ssem, rsem):
    x, y, c = lax.axis_index("x"), lax.axis_index("y"), lax.axis_index("c")
    copies = []
    for rel in range(1, 4):
        px = 1 - x if rel & 2 else x
        py = 1 - y if rel & 1 else y
        for o in range(len(s_refs)):
            copies.append(pltpu.make_async_remote_copy(
                src_ref=s_refs[o].at[2 * px + py], dst_ref=l_refs[o].at[rel - 1],
                send_sem=ssem.at[o, rel - 1], recv_sem=rsem.at[o, rel - 1],
                device_id=(px, py, c), device_id_type=MESH))
    return copies


def _chip_specs(sums):
    n_op = len(sums)
    any_spec = pl.BlockSpec(memory_space=pl.ANY)
    return ([any_spec] * n_op, [any_spec] * n_op,
            [jax.ShapeDtypeStruct((3,) + a.shape[1:], BF16) for a in sums],
            [pltpu.SemaphoreType.DMA((n_op, 3)), pltpu.SemaphoreType.DMA((n_op, 3))])


def _adamw_math(g, w, m, v):
    mn = ADAM_B1 * m + (1.0 - ADAM_B1) * g
    vn = ADAM_B2 * v + (1.0 - ADAM_B2) * (g * g)
    m_hat = mn / (1.0 - ADAM_B1 ** ADAM_STEP)
    v_hat = vn / (1.0 - ADAM_B2 ** ADAM_STEP)
    return -ADAM_LR * (m_hat / (jnp.sqrt(v_hat) + ADAM_EPS) + ADAM_WD * w), mn, vn


def _adamw_matrix(own, landed, w, m, v, name):
    r, c = w.shape
    cp = own.shape[1]
    br = min(r, 256)

    def body(own_ref, l_ref, w_ref, m_ref, v_ref, g_out, d_out, m_out, v_out):
        g = own_ref[...]
        for k in range(3):
            g = g + l_ref[k].astype(F32)
        g = g[:, :c]
        g_out[...] = g
        d_out[...], m_out[...], v_out[...] = _adamw_math(g, w_ref[...], m_ref[...], v_ref[...])

    row = pl.BlockSpec((br, c), lambda i: (i, 0))
    shp = jax.ShapeDtypeStruct((r, c), F32)
    return pl.pallas_call(
        body, name=name, grid=(r // br,),
        in_specs=[pl.BlockSpec((br, cp), lambda i: (i, 0)), pl.BlockSpec((3, br, cp), lambda i: (0, i, 0)),
                  row, row, row],
        out_specs=(row, row, row, row), out_shape=(shp, shp, shp, shp),
        compiler_params=_params(("parallel",), 40),
    )(own, landed, w, m, v)


_VEC_PLACE = ((0, 0), (1, 0), (2, 0), (3, 0), (3, D_GRP), (4, 0), (5, 0), (6, 0))


def _adamw_vectors(sland, ws, ms, vs):
    nv = len(ws)

    def body(l_ref, *refs):
        w_refs, m_refs, v_refs = refs[:nv], refs[nv:2 * nv], refs[2 * nv:3 * nv]
        loss_ref = refs[3 * nv]
        outs = refs[3 * nv + 1:]
        g_all = l_ref[0]
        for j in range(1, N_DEV):
            g_all = g_all + l_ref[j]
        loss_ref[...] = jnp.sum(g_all[7:8, :], axis=1, keepdims=True)
        for k, (row, lane0) in enumerate(_VEC_PLACE):
            n = w_refs[k].shape[1]
            g = g_all[row:row + 1, lane0:lane0 + n]
            d, mn, vn = _adamw_math(g, w_refs[k][...], m_refs[k][...], v_refs[k][...])
            outs[k][...] = g
            outs[nv + k][...] = d
            outs[2 * nv + k][...] = mn
            outs[3 * nv + k][...] = vn

    vmem = pl.BlockSpec(memory_space=pltpu.VMEM)
    shapes = [jax.ShapeDtypeStruct(w.shape, F32) for w in ws]
    return pl.pallas_call(
        body, name="adamw_vectors", in_specs=[vmem] * (1 + 3 * nv), out_specs=[vmem] * (1 + 4 * nv),
        out_shape=[jax.ShapeDtypeStruct((1, 1), F32)] + shapes * 4,
    )(sland, *ws, *ms, *vs)


def _in_proj(x, g, w, shards):
    s = x.shape[0]
    n_op = len(shards)
    steps = s // TM

    def body(x_ref, g_ref, w_ref, *refs):
        shard_refs = refs[:n_op]
        qkv_ref, rest_ref, h_ref = refs[n_op:n_op + 3]
        gath_refs = refs[n_op + 3:2 * n_op + 3]
        start, finish = _two_level_gather(shard_refs, gath_refs, *refs[2 * n_op + 3:])
        i = pl.program_id(0)

        @pl.when(i == 0)
        def _():
            start()

        xv = x_ref[...]
        r = lax.rsqrt(jnp.mean(xv * xv, axis=-1, keepdims=True) + EPS)
        h = ((xv * r) * g_ref[...]).astype(BF16)
        h_ref[...] = h
        qkv_ref[...] = _dot(h, w_ref[:, :1536]).astype(BF16)
        rest_ref[...] = _dot(h, w_ref[:, 1536:])

        @pl.when(i == steps - 1)
        def _():
            finish()

    any_spec = pl.BlockSpec(memory_space=pl.ANY)
    return pl.pallas_call(
        body, name="in_proj", grid=(steps,),
        in_specs=[pl.BlockSpec((TM, D_MODEL), lambda i: (i, 0)),
                  pl.BlockSpec((1, D_MODEL), lambda i: (0, 0)),
                  pl.BlockSpec((D_MODEL, D_IN_P), lambda i: (0, 0))] + [any_spec] * n_op,
        out_specs=[pl.BlockSpec((TM, 1536), lambda i: (i, 0)),
                   pl.BlockSpec((TM, 1536), lambda i: (i, 0)),
                   pl.BlockSpec((TM, D_MODEL), lambda i: (i, 0))] + [any_spec] * n_op,
        out_shape=[jax.ShapeDtypeStruct((s, 1536), BF16), jax.ShapeDtypeStruct((s, 1536), F32),
                   jax.ShapeDtypeStruct((s, D_MODEL), BF16)]
        + [jax.ShapeDtypeStruct((N_DEV,) + a.shape, a.dtype) for a in shards],
        scratch_shapes=_gather_sems(n_op),
        compiler_params=_params(("arbitrary",), 48),
    )(x, g, w, *shards)


def _mla_prep(rest, gq, gkv, wuq, wuk, wuv, cos_t, sin_t):
    s = rest.shape[0]

    def body(cq_ref, ckv_ref, kr_ref, gq_ref, gkv_ref, wuq_ref, wuk_ref, wuv_ref, c_ref, s_ref,
             qp_ref, kp_ref, vv_ref, cqn_ref, ckvn_ref):
        lane = lax.broadcasted_iota(jnp.int32, (1, LANES), 1)
        cos_v, sin_v = c_ref[...], s_ref[...]
        cq = cq_ref[...]
        rq = lax.rsqrt(jnp.mean(cq * cq, axis=-1, keepdims=True) + EPS)
        cqn = ((cq * rq) * gq_ref[...]).astype(BF16)
        cqn_ref[...] = cqn
        q = _dot(cqn, wuq_ref[...])
        ckv = ckv_ref[...]
        rkv = lax.rsqrt(jnp.mean(ckv * ckv, axis=-1, keepdims=True) + EPS)
        ckvn = ((ckv * rkv) * gkv_ref[...]).astype(BF16)
        ckvn_ref[...] = ckvn
        kn = _dot(ckvn, wuk_ref[...])
        vv_ref[...] = _dot(ckvn, wuv_ref[...]).astype(BF16)
        kr = kr_ref[...]
        kr_roped = kr * cos_v + _rope_swap(kr, lane) * sin_v
        for h in range(N_HEADS):
            sl = slice(h * LANES, (h + 1) * LANES)
            qh = q[:, sl]
            qp_ref[:, sl] = (qh * cos_v + _rope_swap(qh, lane) * sin_v).astype(BF16)
            kp_ref[:, sl] = (kn[:, sl] + kr_roped).astype(BF16)

    def row(width, idx):
        return pl.BlockSpec((TM, width), lambda i: (i, idx))

    def full(a):
        return pl.BlockSpec(a.shape, lambda i: (0, 0))

    return pl.pallas_call(
        body, name="mla_prep", grid=(s // TM,),
        in_specs=[row(Q_LORA, 4), row(KV_LORA, 10), row(LANES, 11), full(gq), full(gkv),
                  full(wuq), full(wuk), full(wuv), row(LANES, 0), row(LANES, 0)],
        out_specs=(row(1024, 0), row(1024, 0), row(D_GRP, 0), row(Q_LORA, 0), row(KV_LORA, 0)),
        out_shape=(jax.ShapeDtypeStruct((s, 1024), BF16), jax.ShapeDtypeStruct((s, 1024), BF16),
                   jax.ShapeDtypeStruct((s, D_GRP), BF16), jax.ShapeDtypeStruct((s, Q_LORA), BF16),
                   jax.ShapeDtypeStruct((s, KV_LORA), BF16)),
        compiler_params=_params(("parallel",), 32),
    )(rest, rest, rest, gq, gkv, wuq, wuk, wuv, cos_t, sin_t)


def _sb_live(n, qi, carries):
    top = carries[0]
    for c in carries[1:]:
        top = jnp.maximum(top, c)
    return jnp.logical_and(n < qi, jnp.max(top) > -SB_CUTOFF)


def _sb_fwd(qkv, hb):
    s = qkv.shape[0]

    def body(q_ref, k_ref, v_ref, o_ref, acc):
        qi = pl.program_id(1)
        lane = lax.broadcasted_iota(jnp.int32, (1, LANES), 1)
        is_a = lane < HEAD_DIM
        pair = lambda h: slice((h // 2) * LANES, (h // 2 + 1) * LANES)
        q_h = []
        for h in range(hb):
            qs = q_ref[:, pair(h)] * SB_SCALE
            mine = is_a if h % 2 == 0 else jnp.logical_not(is_a)
            q_h.append(jnp.where(mine, qs, jnp.zeros_like(qs)))
        r_i = lax.broadcasted_iota(jnp.int32, (TQ, TK), 0)
        c_i = lax.broadcasted_iota(jnp.int32, (TQ, TK), 1)
        past = c_i < r_i
        upper = (r_i > c_i).astype(BF16)
        acc[...] = jnp.zeros_like(acc)

        def tile(j, carries, diag):
            ks = pl.ds(pl.multiple_of(j * TK, TK), TK)
            zs = [_dot_nt(q_h[h], k_ref[ks, pair(h)]) for h in range(hb)]
            if diag:
                zs = [jnp.where(past, z, NEG) for z in zs]
            lfs = [-(jnp.maximum(z, 0.0) + jnp.log(1.0 + jnp.exp(-jnp.abs(z)))) for z in zs]
            sufs = [_hl_dot(lfs[h], upper) for h in range(hb)]
            out = []
            for h in range(hb):
                w = jnp.exp(zs[h] + lfs[h] + (sufs[h] + carries[h]))
                acc[h] += _dot(w.astype(BF16), v_ref[ks, pair(h)])
                out.append(carries[h] + jnp.sum(lfs[h], axis=1, keepdims=True))
            return tuple(out)

        zero = jnp.zeros((TQ, 1), F32)
        carries = tile(qi, (zero,) * hb, True)

        def step(st):
            return (st[0] + 1,) + tile(qi - 1 - st[0], st[1:], False)

        lax.while_loop(lambda st: _sb_live(st[0], qi, st[1:]), step, (0,) + carries)
        for pr in range(hb // 2):
            o_ref[:, pr * LANES:(pr + 1) * LANES] = jnp.where(is_a, acc[2 * pr], acc[2 * pr + 1])

    width = hb * HEAD_DIM
    nb = D_GRP // width
    slab = lambda part: pl.BlockSpec((s, width), lambda g, qi: (0, part * nb + g))
    blk = pl.BlockSpec((TQ, width), lambda g, qi: (qi, g))
    return pl.pallas_call(
        body, name="sb_fwd", grid=(nb, s // TQ),
        in_specs=[blk, slab(1), slab(2)], out_specs=blk,
        out_shape=jax.ShapeDtypeStruct((s, D_GRP), F32),
        scratch_shapes=[pltpu.VMEM((hb, TQ, LANES), F32)],
        compiler_params=_params(("arbitrary", "arbitrary"), 48),
    )(qkv, qkv, qkv)


def _sb_bwd(qkv, d_o, sums):
    s = qkv.shape[0]
    nq = s // TQ
    nk = s // TK
    n_op = len(sums)
    ride_in, ride_out, ride_shape, ride_sems = _chip_specs(sums)

    def body(q_ref, k_ref, v_ref, do_ref, *refs):
        s_refs = refs[:n_op]
        dq_ref, dk_ref, dv_ref = refs[n_op:n_op + 3]
        l_refs = refs[n_op + 3:2 * n_op + 3]
        x1s, bts, dqacc, dkacc, dvacc, ssem, rsem = refs[2 * n_op + 3:]
        qi = pl.program_id(1)
        first_step = jnp.logical_and(pl.program_id(0) == 0, qi == 0)
        last_step = jnp.logical_and(pl.program_id(0) == pl.num_programs(0) - 1, qi == nq - 1)

        @pl.when(first_step)
        def _():
            for cp in _chip_copies(s_refs, l_refs, ssem, rsem):
                cp.start()

        lane = lax.broadcasted_iota(jnp.int32, (1, LANES), 1)
        is_a = lane < HEAD_DIM

        @pl.when(qi == 0)
        def _():
            dkacc[...] = jnp.zeros_like(dkacc)
            dvacc[...] = jnp.zeros_like(dvacc)

        qs = q_ref[...] * SB_SCALE
        zq = jnp.zeros_like(qs)
        qs_x = (jnp.where(is_a, qs, zq), jnp.where(is_a, zq, qs))
        dob = do_ref[...].astype(BF16)
        do_x = (jnp.where(is_a, dob, zq), jnp.where(is_a, zq, dob))
        r_i = lax.broadcasted_iota(jnp.int32, (TQ, TK), 0)
        c_i = lax.broadcasted_iota(jnp.int32, (TQ, TK), 1)
        past = c_i < r_i
        upper = (r_i > c_i).astype(BF16)
        upper_incl = (r_i >= c_i).astype(BF16)
        dqacc[...] = jnp.zeros_like(dqacc)
        both = ((0, 0), (0, 1), (1, 0), (1, 1))

        def tiles(n):
            j_hi = qi - 2 * n
            lo_ok = j_hi >= 1
            j_lo = jnp.maximum(j_hi - 1, 0)
            ks = (pl.ds(pl.multiple_of(j_hi * TK, TK), TK), pl.ds(pl.multiple_of(j_lo * TK, TK), TK))
            return j_hi, lo_ok, j_lo, ks

        def sweep(n, carries):
            j_hi, lo_ok, j_lo, ks = tiles(n)
            slot = (j_hi, jnp.where(lo_ok, j_lo, nk))
            valid = (jnp.logical_or(past, j_hi < qi), lo_ok)
            z = {th: jnp.where(valid[th[0]], _dot_nt(qs_x[th[1]], k_ref[ks[th[0]], :]), NEG) for th in both}
            d_a = {th: _dot_nt(do_x[th[1]], v_ref[ks[th[0]], :]) for th in both}
            lf, beta, omb = {}, {}, {}
            for th in both:
                e = jnp.exp(-jnp.abs(z[th]))
                den = 1.0 + e
                rden = 1.0 / den
                pos = z[th] >= 0.0
                lf[th] = -(jnp.maximum(z[th], 0.0) + jnp.log(den))
                beta[th] = jnp.where(pos, rden, e * rden)
                omb[th] = jnp.where(pos, e * rden, rden)
            suf = {th: _hl_dot(lf[th], upper) for th in both}
            c, g_in = {}, {}
            for h in range(2):
                c[0, h], g_in[0, h] = carries[2 * h], carries[2 * h + 1]
                c[1, h] = c[0, h] + jnp.sum(lf[0, h], axis=1, keepdims=True)
            a, g = {}, {}
            for th in both:
                a[th] = jnp.exp(z[th] + lf[th] + (suf[th] + c[th]))
                g[th] = a[th] * d_a[th]
            sg = {th: _hl_dot(g[th], upper_incl) for th in both}
            for h in range(2):
                g_in[1, h] = g_in[0, h] + jnp.sum(g[0, h], axis=1, keepdims=True)
            for th in both:
                t, h = th
                x1s[slot[t], h] = g[th] * omb[th] + beta[th] * (sg[th] + g_in[th])
                bts[slot[t], h] = beta[th]
                dvacc[ks[t], :] += _dot_tn(a[th].astype(BF16), do_x[h])
            out = []
            for h in range(2):
                out.append(c[1, h] + jnp.sum(lf[1, h], axis=1, keepdims=True))
                out.append(g_in[1, h] + jnp.sum(g[1, h], axis=1, keepdims=True))
            return tuple(out)

        zero = jnp.zeros((TQ, 1), F32)
        first = sweep(0, (zero, zero, zero, zero))

        def more(st):
            return jnp.logical_and(2 * st[0] <= qi, jnp.max(jnp.maximum(st[1], st[3])) > -SB_CUTOFF)

        swept = lax.while_loop(more, lambda st: (st[0] + 1,) + sweep(st[0], st[1:]), (1,) + first)
        g_tot = (swept[2], swept[4])

        def apply(n, carry):
            j_hi, lo_ok, j_lo, ks = tiles(n)

            def one(j, kslice):
                for h in range(2):
                    dz = (x1s[j, h] - bts[j, h] * g_tot[h]).astype(BF16)
                    dqacc[h] += _dot(dz, k_ref[kslice, :])
                    dkacc[kslice, :] += _dot_tn(dz, qs_x[h])

            one(j_hi, ks[0])

            @pl.when(lo_ok)
            def _():
                one(j_lo, ks[1])

            return carry

        lax.fori_loop(0, swept[0], apply, 0)
        dq_ref[...] = (jnp.where(is_a, dqacc[0], dqacc[1]) * SB_SCALE).astype(BF16)

        @pl.when(qi == nq - 1)
        def _():
            dk_ref[...] = dkacc[...].astype(BF16)
            dv_ref[...] = dvacc[...].astype(BF16)

        @pl.when(last_step)
        def _():
            for cp in _chip_copies(s_refs, l_refs, ssem, rsem):
                cp.wait()

    slab = lambda off: pl.BlockSpec((s, LANES), lambda p, qi: (0, off + p))
    blk = pl.BlockSpec((TQ, LANES), lambda p, qi: (qi, p))
    out_slab = pl.BlockSpec((s, LANES), lambda p, qi: (0, p))
    shp = jax.ShapeDtypeStruct((s, D_GRP), BF16)
    return pl.pallas_call(
        body, name="sb_bwd", grid=(4, nq),
        in_specs=[blk, slab(4), slab(8), blk] + ride_in,
        out_specs=[blk, out_slab, out_slab] + ride_out, out_shape=[shp, shp, shp] + ride_shape,
        scratch_shapes=[pltpu.VMEM((nk + 1, 2, TQ, TK), F32)] * 2
        + [pltpu.VMEM((2, TQ, LANES), F32), pltpu.VMEM((s, LANES), F32), pltpu.VMEM((s, LANES), F32)]
        + ride_sems,
        compiler_params=_params(("arbitrary", "arbitrary"), 56),
    )(qkv, qkv, qkv, d_o, *sums)


def _mla_fwd(qp, kp, vv, hb):
    s = qp.shape[0]
    c2 = MLA_SCALE * LOG2_E

    def body(q_ref, k_ref, v_ref, o_ref, lse_ref, vaug, mrun, mb, acc, zbuf):
        qi = pl.program_id(1)
        lane = lax.broadcasted_iota(jnp.int32, (1, LANES), 1)
        is_a = lane < HEAD_DIM

        @pl.when(qi == 0)
        def _():
            for h in range(hb):
                vp = v_ref[:, (h // 2) * LANES:(h // 2 + 1) * LANES]
                mine = is_a if h % 2 == 0 else jnp.logical_not(is_a)
                vaug[h] = jnp.where(mine, vp, jnp.ones_like(vp))

        r_i = lax.broadcasted_iota(jnp.int32, (TQ, TK), 0)
        c_i = lax.broadcasted_iota(jnp.int32, (TQ, TK), 1)
        visible = (c_i >> CHUNK_SHIFT) <= (r_i >> CHUNK_SHIFT)

        def key_rows(j):
            return pl.ds(pl.multiple_of(j * TK, TK), TK)

        def sweep(tiles):
            def loop(n, carry):
                tiles(((2 * n, False), (2 * n + 1, False)))
                return carry

            lax.fori_loop(0, qi // 2, loop, 0)

            @pl.when(qi % 2 == 1)
            def _():
                tiles(((qi - 1, False), (qi, True)))

            @pl.when(qi % 2 == 0)
            def _():
                tiles(((qi, True),))

        mrun[...] = jnp.full_like(mrun, NEG)

        def tiles_max(js):
            zs = [[_dot_nt(q_ref[:, h * LANES:(h + 1) * LANES], k_ref[key_rows(j), h * LANES:(h + 1) * LANES])
                   for h in range(hb)] for j, _ in js]
            for t, (j, diag) in enumerate(js):
                for h in range(hb):
                    z = jnp.where(visible, zs[t][h], NEG) if diag else zs[t][h]
                    zbuf[j, h] = z
                    mrun[h] = jnp.maximum(mrun[h], z)

        sweep(tiles_max)
        for h in range(hb):
            m = jnp.max(mrun[h], axis=1, keepdims=True) * c2
            mb[h] = jnp.broadcast_to(m, (TQ, TK))
        acc[...] = jnp.zeros_like(acc)

        def tiles_pv(js):
            ps = [[jnp.exp2((zbuf[j, h] * c2 - mb[h]).astype(BF16)) for h in range(hb)] for j, _ in js]
            for t, (j, _) in enumerate(js):
                for h in range(hb):
                    acc[h] += _dot(ps[t][h], vaug[h, key_rows(j), :])

        sweep(tiles_pv)
        for pr in range(hb // 2):
            a, b = 2 * pr, 2 * pr + 1
            psl = slice(pr * LANES, (pr + 1) * LANES)
            acc_a, acc_b = acc[a], acc[b]
            l_a = pltpu.roll(acc_a, HEAD_DIM, axis=1)
            l_b = pltpu.roll(acc_b, HEAD_DIM, axis=1)
            o_ref[:, psl] = jnp.where(is_a, acc_a * (1.0 / l_a), acc_b * (1.0 / l_b))
            lse_ref[:, psl] = jnp.where(is_a, mb[a, :, :LANES] * LN_2 + jnp.log(l_a),
                                        mb[b, :, :LANES] * LN_2 + jnp.log(l_b))

    blk = pl.BlockSpec((TQ, hb * HEAD_DIM), lambda g, qi: (qi, g))
    shp = jax.ShapeDtypeStruct((s, D_GRP), F32)
    return pl.pallas_call(
        body, name="mla_fwd", grid=(N_HEADS // hb, s // TQ),
        in_specs=[pl.BlockSpec((TQ, hb * LANES), lambda g, qi: (qi, g)),
                  pl.BlockSpec((s, hb * LANES), lambda g, qi: (0, g)),
                  pl.BlockSpec((s, hb * HEAD_DIM), lambda g, qi: (0, g))],
        out_specs=(blk, blk), out_shape=(shp, shp),
        scratch_shapes=[pltpu.VMEM((hb, s, LANES), BF16), pltpu.VMEM((hb, TQ, TK), F32),
                        pltpu.VMEM((hb, TQ, TK), F32), pltpu.VMEM((hb, TQ, LANES), F32),
                        pltpu.VMEM((s // TK, hb, TQ, TK), F32)],
        compiler_params=_params(("arbitrary", "arbitrary"), 56),
    )(qp, kp, vv)


def _mla_bwd(qp, kp, vv, d_o, o, lse, hb, pays):
    s = qp.shape[0]
    nq = s // TQ
    c2 = MLA_SCALE * LOG2_E
    n_op = len(pays)
    ride_in, ride_out, ride_shape, ride_sems = _pair_specs(pays)

    def body(q_ref, k_ref, v_ref, do_ref, o_ref, lse_ref, *refs):
        g_refs = refs[:n_op]
        dq_ref, dk_ref, dv_ref = refs[n_op:n_op + 3]
        l_refs = refs[n_op + 3:2 * n_op + 3]
        dqacc, lse_b, delta_b, ssem, rsem = refs[2 * n_op + 3:]
        qi = pl.program_id(1)

        @pl.when(jnp.logical_and(pl.program_id(0) == 0, qi == 0))
        def _():
            for cp in _pair_copies(g_refs, l_refs, ssem, rsem):
                cp.start()

        lane = lax.broadcasted_iota(jnp.int32, (1, LANES), 1)
        is_a = lane < HEAD_DIM

        @pl.when(qi == 0)
        def _():
            dk_ref[...] = jnp.zeros_like(dk_ref)
            dv_ref[...] = jnp.zeros_like(dv_ref)

        r_i = lax.broadcasted_iota(jnp.int32, (TQ, TK), 0)
        c_i = lax.broadcasted_iota(jnp.int32, (TQ, TK), 1)
        visible = (c_i >> CHUNK_SHIFT) <= (r_i >> CHUNK_SHIFT)
        do_x = []
        for h in range(hb):
            psl = slice((h // 2) * LANES, (h // 2 + 1) * LANES)
            mine = is_a if h % 2 == 0 else jnp.logical_not(is_a)
            d_o = do_ref[:, psl]
            delta = jnp.sum(jnp.where(mine, d_o * o_ref[:, psl], 0.0), axis=1, keepdims=True)
            lse_h = jnp.sum(jnp.where(lane == (h % 2) * HEAD_DIM, lse_ref[:, psl], 0.0), axis=1, keepdims=True)
            lse_b[h] = jnp.broadcast_to(lse_h * LOG2_E, (TQ, TK))
            delta_b[h] = jnp.broadcast_to(delta, (TQ, TK))
            do_x.append(jnp.where(mine, d_o, 0.0).astype(BF16))
        dqacc[...] = jnp.zeros_like(dqacc)

        head = lambda h: slice(h * LANES, (h + 1) * LANES)
        pair = lambda h: slice((h // 2) * LANES, (h // 2 + 1) * LANES)

        def tiles(js):
            th = [(j, diag, pl.ds(pl.multiple_of(j * TK, TK), TK), h) for j, diag in js for h in range(hb)]
            zs = [_dot_nt(q_ref[:, head(h)], k_ref[ks, head(h)]) for _, _, ks, h in th]
            dps = [_dot_nt(do_x[h], v_ref[ks, pair(h)]) for _, _, ks, h in th]
            for i, (j, diag, ks, h) in enumerate(th):
                e = zs[i] * c2 - lse_b[h]
                if diag:
                    e = jnp.where(visible, e, NEG)
                p = jnp.exp2(e)
                ds = (p * (dps[i] - delta_b[h]) * MLA_SCALE).astype(BF16)
                dqacc[h] += _dot(ds, k_ref[ks, head(h)])
                dk_ref[ks, head(h)] += _dot_tn(ds, q_ref[:, head(h)])
                dv_ref[ks, pair(h)] += _dot_tn(p.astype(BF16), do_x[h])

        def loop(n, c):
            tiles(((2 * n, False), (2 * n + 1, False)))
            return c

        lax.fori_loop(0, qi // 2, loop, 0)

        @pl.when(qi % 2 == 1)
        def _():
            tiles(((qi - 1, False), (qi, True)))

        @pl.when(qi % 2 == 0)
        def _():
            tiles(((qi, True),))

        for h in range(hb):
            dq_ref[:, h * LANES:(h + 1) * LANES] = dqacc[h]

        @pl.when(jnp.logical_and(pl.program_id(0) == pl.num_programs(0) - 1, qi == nq - 1))
        def _():
            for cp in _pair_copies(g_refs, l_refs, ssem, rsem):
                cp.wait()

    blk = pl.BlockSpec((TQ, hb * HEAD_DIM), lambda g, qi: (qi, g))
    return pl.pallas_call(
        body, name="mla_bwd", grid=(N_HEADS // hb, nq),
        in_specs=[pl.BlockSpec((TQ, hb * LANES), lambda g, qi: (qi, g)),
                  pl.BlockSpec((s, hb * LANES), lambda g, qi: (0, g)),
                  pl.BlockSpec((s, hb * HEAD_DIM), lambda g, qi: (0, g)), blk, blk, blk] + ride_in,
        out_specs=[pl.BlockSpec((TQ, hb * LANES), lambda g, qi: (qi, g)),
                   pl.BlockSpec((s, hb * LANES), lambda g, qi: (0, g)),
                   pl.BlockSpec((s, hb * HEAD_DIM), lambda g, qi: (0, g))] + ride_out,
        out_shape=[jax.ShapeDtypeStruct((s, 1024), F32), jax.ShapeDtypeStruct((s, 1024), F32),
                   jax.ShapeDtypeStruct((s, D_GRP), F32)] + ride_shape,
        scratch_shapes=[pltpu.VMEM((hb, TQ, LANES), F32), pltpu.VMEM((hb, TQ, TK), F32),
                        pltpu.VMEM((hb, TQ, TK), F32)] + ride_sems,
        compiler_params=_params(("arbitrary", "arbitrary"), 56),
    )(qp, kp, vv, d_o, o, lse, *pays)


def _mid(x, p, target, sb_o, mla_o, rest, g_sb, g_mla, w_out, g_post, w_ple, g_ple, w_pg, b_pg, bd):
    s = x.shape[0]

    def body(x_ref, p_ref, t_ref, sbo_ref, mlo_ref, sbg_ref, mlg_ref, gsb_ref, gml_ref, wout_ref,
             gpost_ref, wple_ref, gple_ref, wpg_ref, bpg_ref, bd_ref,
             dx1_ref, dsbo_ref, dmlo_ref, dsbg_ref, dmlg_ref, x1b_ref, dglb_ref, ycb_ref, dyb_ref,
             pb_ref, dub_ref, small_ref):
        i = pl.program_id(0)
        bd_m = bd_ref[...]

        def seg_mean(v):
            return _dot(v.astype(BF16), bd_m) * (1.0 / HEAD_DIM)

        groups = []
        for o_ref, gate_ref, gain_ref in ((sbo_ref, sbg_ref, gsb_ref), (mlo_ref, mlg_ref, gml_ref)):
            o = o_ref[...]
            r = lax.rsqrt(seg_mean(o * o) + EPS)
            n = o * r
            hn = n * gain_ref[...]
            gate = gate_ref[...]
            sg = _sigmoid(gate)
            si = gate * sg
            groups.append((r, n, hn, gate, sg, si, gain_ref[...]))
        ya = (groups[0][2] * groups[0][5]).astype(BF16)
        yb = (groups[1][2] * groups[1][5]).astype(BF16)
        ycb_ref[:, :D_GRP] = ya
        ycb_ref[:, D_GRP:] = yb
        y = _dot(ya, wout_ref[:D_GRP, :]) + _dot(yb, wout_ref[D_GRP:, :])
        ry = lax.rsqrt(jnp.mean(y * y, axis=-1, keepdims=True) + EPS)
        ny = y * ry
        x1 = x_ref[...] + ny * gpost_ref[...]
        x1b = x1.astype(BF16)
        x1b_ref[...] = x1b
        pb = p_ref[...].astype(BF16)
        pb_ref[...] = pb
        u = _dot(pb, wple_ref[...])
        ru = lax.rsqrt(jnp.mean(u * u, axis=-1, keepdims=True) + EPS)
        nu = u * ru
        ple = nu * gple_ref[...]
        gate = _sigmoid(_dot(x1b, wpg_ref[...]) + bpg_ref[...])
        x2 = x1 + ple * gate
        diff = x2 - t_ref[...]
        dx2 = diff * (1.0 / D_MODEL)

        d_ple = dx2 * gate
        d_glin = (dx2 * ple) * (gate * (1.0 - gate))
        dglb = d_glin.astype(BF16)
        dglb_ref[...] = dglb
        dx1 = dx2 + _dot_nt(dglb, wpg_ref[...])
        dx1_ref[...] = dx1
        d_nu = d_ple * gple_ref[...]
        d_u = ru * (d_nu - nu * jnp.mean(d_nu * nu, axis=-1, keepdims=True))
        dub_ref[...] = d_u.astype(BF16)
        d_ny = dx1 * gpost_ref[...]
        d_y = ry * (d_ny - ny * jnp.mean(d_ny * ny, axis=-1, keepdims=True))
        dyb = d_y.astype(BF16)
        dyb_ref[...] = dyb
        d_yc = (_dot_nt(dyb, wout_ref[:D_GRP, :]), _dot_nt(dyb, wout_ref[D_GRP:, :]))

        d_gain = []
        for gx, (do_ref, dg_ref) in enumerate(((dsbo_ref, dsbg_ref), (dmlo_ref, dmlg_ref))):
            r, n, hn, gate_g, sg, si, gain = groups[gx]
            dyg = d_yc[gx]
            d_hn = dyg * si
            dg_ref[...] = (dyg * hn * (sg * (1.0 + gate_g * (1.0 - sg)))).astype(BF16)
            d_gain.append(jnp.sum(d_hn * n, axis=0, keepdims=True))
            d_n = d_hn * gain
            do_ref[...] = r * (d_n - n * seg_mean(d_n * n))

        @pl.when(i == 0)
        def _():
            small_ref[...] = jnp.zeros_like(small_ref)

        small_ref[3:4, :D_GRP] += d_gain[0]
        small_ref[3:4, D_GRP:] += d_gain[1]
        small_ref[4:5, :] += jnp.sum(dx1 * ny, axis=0, keepdims=True)
        small_ref[5:6, :] += jnp.sum(d_ple * nu, axis=0, keepdims=True)
        small_ref[6:7, :] += jnp.sum(d_glin, axis=0, keepdims=True)
        small_ref[7:8, :] += jnp.sum(diff * diff, axis=0, keepdims=True) * (0.5 / D_MODEL)

    def row(width, idx=0):
        return pl.BlockSpec((TM, width), lambda i: (i, idx))

    def full(a):
        return pl.BlockSpec(a.shape, lambda i: (0, 0))

    f32 = lambda w: jax.ShapeDtypeStruct((s, w), F32)
    b16 = lambda w: jax.ShapeDtypeStruct((s, w), BF16)
    return pl.pallas_call(
        body, name="mid", grid=(s // TM,),
        in_specs=[row(D_MODEL), row(PLE_DIM), row(D_MODEL), row(D_GRP), row(D_GRP),
                  row(D_GRP, 0), row(D_GRP, 1), full(g_sb), full(g_mla), full(w_out), full(g_post),
                  full(w_ple), full(g_ple), full(w_pg), full(b_pg), full(bd)],
        out_specs=(row(D_MODEL), row(D_GRP), row(D_GRP), row(D_GRP), row(D_GRP), row(D_MODEL),
                   row(D_MODEL), row(D_MODEL), row(D_MODEL), row(PLE_DIM), row(D_MODEL),
                   pl.BlockSpec((8, D_MODEL), lambda i: (0, 0))),
        out_shape=(f32(D_MODEL), f32(D_GRP), f32(D_GRP), b16(D_GRP), b16(D_GRP), b16(D_MODEL),
                   b16(D_MODEL), b16(D_MODEL), b16(D_MODEL), b16(PLE_DIM), b16(D_MODEL),
                   jax.ShapeDtypeStruct((8, D_MODEL), F32)),
        compiler_params=_params(("arbitrary",), 56),
    )(x, p, target, sb_o, mla_o, rest, rest, g_sb, g_mla, w_out, g_post, w_ple, g_ple, w_pg, b_pg, bd)


def _mla_prep_bwd(dqp, dkp, dvv, rest, gq, gkv, wuq, wuk, wuv, cos_t, sin_t):
    s = rest.shape[0]

    def body(dqp_ref, dkp_ref, dvv_ref, cq_ref, ckv_ref, gq_ref, gkv_ref, wuq_ref, wuk_ref, wuv_ref,
             c_ref, s_ref, dcq_ref, dckv_ref, dkr_ref, dqb_ref, dkb_ref, dvb_ref, small_ref):
        i = pl.program_id(0)
        lane = lax.broadcasted_iota(jnp.int32, (1, LANES), 1)
        in_rope = (lane >= HEAD_DIM) & (lane < HEAD_DIM + ROPE_DIM)
        cos_v, sin_v = c_ref[...], s_ref[...]
        dkr_roped = jnp.zeros((TM, LANES), F32)
        for h in range(N_HEADS):
            sl = slice(h * LANES, (h + 1) * LANES)
            dy = dqp_ref[:, sl]
            dqb_ref[:, sl] = (dy * cos_v + _rope_swap(dy * sin_v, lane)).astype(BF16)
            dkh = dkp_ref[:, sl]
            dkb_ref[:, sl] = dkh.astype(BF16)
            dkr_roped = dkr_roped + jnp.where(in_rope, dkh, 0.0)
        dkr_ref[...] = (dkr_roped * cos_v + _rope_swap(dkr_roped * sin_v, lane)).astype(BF16)
        dvb = dvv_ref[...].astype(BF16)
        dvb_ref[...] = dvb

        cq = cq_ref[...]
        rq = lax.rsqrt(jnp.mean(cq * cq, axis=-1, keepdims=True) + EPS)
        nq_ = cq * rq
        d_cqn = _dot_nt(dqb_ref[...], wuq_ref[...])
        d_n = d_cqn * gq_ref[...]
        dcq_ref[...] = (rq * (d_n - nq_ * jnp.mean(d_n * nq_, axis=-1, keepdims=True))).astype(BF16)

        ckv = ckv_ref[...]
        rkv = lax.rsqrt(jnp.mean(ckv * ckv, axis=-1, keepdims=True) + EPS)
        nkv = ckv * rkv
        d_ckvn = _dot_nt(dkb_ref[...], wuk_ref[...]) + _dot_nt(dvb, wuv_ref[...])
        d_n2 = d_ckvn * gkv_ref[...]
        dckv_ref[...] = (rkv * (d_n2 - nkv * jnp.mean(d_n2 * nkv, axis=-1, keepdims=True))).astype(BF16)

        @pl.when(i == 0)
        def _():
            small_ref[...] = jnp.zeros_like(small_ref)

        small_ref[0:1, :] += jnp.sum(d_cqn * nq_, axis=0, keepdims=True)
        small_ref[1:2, :KV_LORA] += jnp.sum(d_ckvn * nkv, axis=0, keepdims=True)

    def row(width, idx=0):
        return pl.BlockSpec((TM, width), lambda i: (i, idx))

    def full(a):
        return pl.BlockSpec(a.shape, lambda i: (0, 0))

    b16 = lambda w: jax.ShapeDtypeStruct((s, w), BF16)
    return pl.pallas_call(
        body, name="mla_prep_bwd", grid=(s // TM,),
        in_specs=[row(1024), row(1024), row(D_GRP), row(Q_LORA, 4), row(KV_LORA, 10), full(gq), full(gkv),
                  full(wuq), full(wuk), full(wuv), row(LANES), row(LANES)],
        out_specs=(row(Q_LORA), row(KV_LORA), row(LANES), row(1024), row(1024), row(D_GRP),
                   pl.BlockSpec((8, Q_LORA), lambda i: (0, 0))),
        out_shape=(b16(Q_LORA), b16(KV_LORA), b16(LANES), b16(1024), b16(1024), b16(D_GRP),
                   jax.ShapeDtypeStruct((8, Q_LORA), F32)),
        compiler_params=_params(("arbitrary",), 40),
    )(dqp, dkp, dvv, rest, rest, gq, gkv, wuq, wuk, wuv, cos_t, sin_t)


def _in_bwd(x, g, dx1, pieces, w, sums):
    s = x.shape[0]
    steps = s // TM
    widths = [a.shape[1] for a in pieces]
    offs = [sum(widths[:k]) for k in range(len(widths))]
    n_pc, n_op = len(pieces), len(sums)
    ride_in, ride_out, ride_shape, ride_sems = _chip_specs(sums)

    def body(x_ref, g_ref, dx1_ref, *refs):
        piece_refs = refs[:n_pc]
        w_ref = refs[n_pc]
        s_refs = refs[n_pc + 1:n_pc + 1 + n_op]
        dx_ref, small_ref = refs[n_pc + 1 + n_op:n_pc + 3 + n_op]
        l_refs = refs[n_pc + 3 + n_op:n_pc + 3 + 2 * n_op]
        ssem, rsem = refs[n_pc + 3 + 2 * n_op:]
        i = pl.program_id(0)

        @pl.when(i == 0)
        def _():
            for cp in _chip_copies(s_refs, l_refs, ssem, rsem):
                cp.start()

        dh = jnp.zeros((TM, D_MODEL), F32)
        for pr, off, wd in zip(piece_refs, offs, widths):
            dh = dh + _dot_nt(pr[...], w_ref[:, off:off + wd])
        xv = x_ref[...]
        r = lax.rsqrt(jnp.mean(xv * xv, axis=-1, keepdims=True) + EPS)
        n = xv * r
        d_n = dh * g_ref[...]
        dx_ref[...] = dx1_ref[...] + r * (d_n - n * jnp.mean(d_n * n, axis=-1, keepdims=True))

        @pl.when(i == 0)
        def _():
            small_ref[...] = jnp.zeros_like(small_ref)

        small_ref[0:1, :] += jnp.sum(dh * n, axis=0, keepdims=True)

        @pl.when(i == steps - 1)
        def _():
            for cp in _chip_copies(s_refs, l_refs, ssem, rsem):
                cp.wait()

    def row(width):
        return pl.BlockSpec((TM, width), lambda i: (i, 0))

    return pl.pallas_call(
        body, name="in_bwd", grid=(steps,),
        in_specs=[row(D_MODEL), pl.BlockSpec((1, D_MODEL), lambda i: (0, 0)), row(D_MODEL)]
        + [row(wd) for wd in widths] + [pl.BlockSpec(w.shape, lambda i: (0, 0))] + ride_in,
        out_specs=[row(D_MODEL), pl.BlockSpec((8, D_MODEL), lambda i: (0, 0))] + ride_out,
        out_shape=[jax.ShapeDtypeStruct((s, D_MODEL), F32), jax.ShapeDtypeStruct((8, D_MODEL), F32)]
        + ride_shape,
        scratch_shapes=ride_sems,
        compiler_params=_params(("arbitrary",), 48),
    )(x, g, dx1, *pieces, w, *sums)


def _tn_matmul(a, b, name, blocked=False):
    s, k = a.shape
    n = b.shape[1]
    ts = 512
    tn = n if blocked else min(n, 512)
    steps = s // ts

    def body(a_ref, b_ref, o_ref):
        t = pl.program_id(1)

        @pl.when(t == 0)
        def _():
            o_ref[...] = jnp.zeros_like(o_ref)

        prod = _dot_tn(a_ref[...], b_ref[...])
        if blocked:
            for j in range(n // LANES):
                o_ref[j] += prod[:, j * LANES:(j + 1) * LANES]
        else:
            o_ref[...] += prod

    if blocked:
        out_spec = pl.BlockSpec((n // LANES, k, LANES), lambda j, t: (0, 0, 0))
        out_shape = jax.ShapeDtypeStruct((n // LANES, k, LANES), F32)
    else:
        out_spec = pl.BlockSpec((k, tn), lambda j, t: (0, j))
        out_shape = jax.ShapeDtypeStruct((k, n), F32)
    return pl.pallas_call(
        body, name=name, grid=(n // tn, steps),
        in_specs=[pl.BlockSpec((ts, k), lambda j, t: (t, 0)), pl.BlockSpec((ts, tn), lambda j, t: (t, j))],
        out_specs=out_spec, out_shape=out_shape,
        compiler_params=_params(("parallel", "arbitrary"), 40),
    )(a, b)


def _tn_matmul_multi(a, bs, name):
    s, k = a.shape
    widths = [b.shape[1] for b in bs]
    ts = 512

    def body(a_ref, *refs):
        b_refs, o_ref = refs[:-1], refs[-1]
        t = pl.program_id(0)

        @pl.when(t == 0)
        def _():
            o_ref[...] = jnp.zeros_like(o_ref)

        av = a_ref[...]
        off = 0
        for b_ref, wd in zip(b_refs, widths):
            o_ref[:, off:off + wd] += _dot_tn(av, b_ref[...])
            off += wd

    return pl.pallas_call(
        body, name=name, grid=(s // ts,),
        in_specs=[pl.BlockSpec((ts, k), lambda t: (t, 0))] + [pl.BlockSpec((ts, wd), lambda t: (t, 0)) for wd in widths],
        out_specs=pl.BlockSpec((k, sum(widths)), lambda t: (0, 0)),
        out_shape=jax.ShapeDtypeStruct((k, sum(widths)), F32),
        compiler_params=_params(("arbitrary",), 40),
    )(a, *bs)


IN_SHARD = 372
_IN_KERNEL_ORDER = ((0, 2048), (2464, 2976), (2048, 2432))
_IN_ROPE = (2432, 2464)
_IN_GRAD_SRC = ((0, 512, 0, 0), (512, 1024, 0, 512), (1024, 1536, 1, 0), (1536, 2048, 1, 512),
                (2048, 2304, 2, 512), (2304, 2432, 2, 768), (2432, 2464, 2, 960), (2464, 2976, 2, 0))


def _shard_cols(gath_in, lo, hi):
    out = []
    while lo < hi:
        j, a = divmod(lo, IN_SHARD)
        b = min(IN_SHARD, a + hi - lo)
        out.append(gath_in[j][:, a:b])
        lo += b - a
    return out


def _kernel_w_in(g_in):
    zc = lambda n: jnp.zeros((D_MODEL, n), BF16)
    parts = [pc for lo, hi in _IN_KERNEL_ORDER for pc in _shard_cols(g_in, lo, hi)]
    parts += [zc(64)] + _shard_cols(g_in, *_IN_ROPE) + [zc(32)]
    return jnp.concatenate(parts, axis=1)


def _kernel_weights(gath):
    g_uq, g_ukv, g_out, g_ple, g_pg = gath
    w_uq_p = jnp.pad(g_uq, ((0, 0), (0, 0), (0, 32))).transpose(1, 0, 2).reshape(Q_LORA, 1024)
    k_only = jnp.where(jnp.arange(LANES) < HEAD_DIM, g_ukv, jnp.zeros_like(g_ukv))
    w_uk_p = k_only.transpose(1, 0, 2).reshape(KV_LORA, 1024)
    w_uv = g_ukv[:, :, HEAD_DIM:].transpose(1, 0, 2).reshape(KV_LORA, D_GRP)
    w_ple = g_ple.transpose(1, 0, 2).reshape(PLE_DIM, D_MODEL)
    return (w_uq_p, w_uk_p, w_uv, g_out.reshape(D_MODEL, D_MODEL), w_ple, g_pg.reshape(D_MODEL, D_MODEL))


def _payload_in(d_cols):
    blocks = []
    for j in range(N_DEV):
        lo, hi = j * IN_SHARD, (j + 1) * IN_SHARD
        parts = []
        for o_lo, o_hi, idx, off in _IN_GRAD_SRC:
            a, b = max(lo, o_lo), min(hi, o_hi)
            if a < b:
                parts.append(d_cols[idx][:, off + a - o_lo:off + b - o_lo])
        blocks.append(jnp.concatenate(parts, axis=1))
    return jnp.stack(blocks)


def _payload_ukv(duk_blk, d_uv):
    dv_blk = d_uv.reshape(KV_LORA, N_HEADS, HEAD_DIM).transpose(1, 0, 2)
    return jnp.concatenate([duk_blk[:, :, :HEAD_DIM], dv_blk], axis=2)


def _pair_sums(pays, landed, place, tag):
    return [_pair_sum(g, l, place, "grad_pair_sum_%s%d" % (tag, o)) for o, (g, l) in enumerate(zip(pays, landed))]


def kernel(x, p, positions, norm_pre_g, w_in, q_norm_g, w_uq, kv_norm_g, w_ukv, sb_out_norm_g, mla_out_norm_g, w_out, norm_post_g, w_ple, ple_norm_g, w_ple_gate, b_ple_gate, loss_target, m_norm_pre_g, m_w_in, m_q_norm_g, m_w_uq, m_kv_norm_g, m_w_ukv, m_sb_out_norm_g, m_mla_out_norm_g, m_w_out, m_norm_post_g, m_w_ple, m_ple_norm_g, m_w_ple_gate, m_b_ple_gate, v_norm_pre_g, v_w_in, v_q_norm_g, v_w_uq, v_kv_norm_g, v_w_ukv, v_sb_out_norm_g, v_mla_out_norm_g, v_w_out, v_norm_post_g, v_w_ple, v_ple_norm_g, v_w_ple_gate, v_b_ple_gate):
    mats = (w_in, w_uq, w_ukv, w_out, w_ple, w_ple_gate)
    m_mats = (m_w_in, m_w_uq, m_w_ukv, m_w_out, m_w_ple, m_w_ple_gate)
    v_mats = (v_w_in, v_w_uq, v_w_ukv, v_w_out, v_w_ple, v_w_ple_gate)
    vecs = (norm_pre_g, q_norm_g, kv_norm_g, sb_out_norm_g, mla_out_norm_g, norm_post_g, ple_norm_g, b_ple_gate)
    m_vecs = (m_norm_pre_g, m_q_norm_g, m_kv_norm_g, m_sb_out_norm_g, m_mla_out_norm_g, m_norm_post_g,
              m_ple_norm_g, m_b_ple_gate)
    v_vecs = (v_norm_pre_g, v_q_norm_g, v_kv_norm_g, v_sb_out_norm_g, v_mla_out_norm_g, v_norm_post_g,
              v_ple_norm_g, v_b_ple_gate)

    shards = [a[0].astype(BF16) for a in mats]
    w_in_p = _kernel_w_in(_all_gather(shards[:1])[0])
    grad_x, reduced, vec_slab = _step(x[0], p[0, 0], positions[0], loss_target[0], *vecs, w_in_p, shards[1:])
    upd = [_adamw_matrix(own, l2, w[0], m[0], v[0], "adamw_%d" % o)
           for o, ((own, l2), w, m, v) in enumerate(zip(reduced, mats, m_mats, v_mats))]
    sm = _adamw_vectors(_slab_exchange(vec_slab), vecs, m_vecs, v_vecs)

    outs = []
    for kind in range(4):
        mat = [upd[o][kind][None] for o in range(len(mats))]
        vec = sm[1 + 8 * kind:9 + 8 * kind]
        outs += [vec[0], mat[0], vec[1], mat[1], vec[2], mat[2], vec[3], vec[4], mat[3], vec[5],
                 mat[4], vec[6], mat[5], vec[7]]
    return (sm[0][0, 0], grad_x[None], *outs)


def _step(xs, ps, pos, tgt, norm_pre_g, q_norm_g, kv_norm_g, sb_out_norm_g, mla_out_norm_g,
          norm_post_g, ple_norm_g, b_ple_gate, w_in_p, shards):
    s = xs.shape[0]
    place = jnp.stack([lax.axis_index("c"), 2 * lax.axis_index("x") + lax.axis_index("y")]).astype(jnp.int32)

    half = ROPE_DIM // 2
    freq = ROPE_THETA ** (-jnp.arange(half, dtype=F32) / half)
    ang = pos.astype(F32)[:, None] * freq
    cos, sin = jnp.cos(ang), jnp.sin(ang)
    cos_t = jnp.concatenate([jnp.ones((s, 64), F32), cos, cos, jnp.zeros((s, 32), F32)], axis=1)
    sin_t = jnp.concatenate([jnp.zeros((s, 64), F32), -sin, sin, jnp.zeros((s, 32), F32)], axis=1)
    seg = jnp.arange(D_GRP) // HEAD_DIM
    bd = (seg[:, None] == seg[None, :]).astype(BF16)

    qkv, rest, h_b, *gath = _in_proj(xs, norm_pre_g, w_in_p, shards)
    w_uq_p, w_uk_p, w_uv, f_out, f_ple, f_pg = _kernel_weights(gath)
    sb_o = _sb_fwd(qkv, 8)
    qp, kp, vv, cqn_b, ckvn_b = _mla_prep(rest, q_norm_g, kv_norm_g, w_uq_p, w_uk_p, w_uv, cos_t, sin_t)
    mla_o, lse = _mla_fwd(qp, kp, vv, 4)

    (dx1, d_sbo, d_mlo, d_sbg, d_mlg, x1_b, dgl_b, yc_b, dy_b, p_b, du_b, small_mid) = _mid(
        xs, ps, tgt, sb_o, mla_o, rest, sb_out_norm_g, mla_out_norm_g, f_out, norm_post_g,
        f_ple, ple_norm_g, f_pg, b_ple_gate, bd)
    pay_a = [_tn_matmul(yc_b, dy_b, "dw_out").reshape(N_DEV, 128, D_MODEL),
             _tn_matmul(p_b, du_b, "dw_ple", blocked=True),
             _tn_matmul(x1_b, dgl_b, "dw_pg").reshape(N_DEV, 128, D_MODEL)]
    dqp, dkp, dvv, *sib_a = _mla_bwd(qp, kp, vv, d_mlo, mla_o, lse, 4, pay_a)
    pair_a = _pair_sums(pay_a, sib_a, place, "a")
    dq_sb, dk_sb, dv_sb, *landed_a = _sb_bwd(qkv, d_sbo, [sm for sm, _ in pair_a])
    dcq, dckv, dkr, dq_b, dk_b, dv_b, small_prep = _mla_prep_bwd(
        dqp, dkp, dvv, rest, q_norm_g, kv_norm_g, w_uq_p, w_uk_p, w_uv, cos_t, sin_t)
    pieces = [dq_sb, dk_sb, dv_sb, d_sbg, d_mlg, dcq, dckv, dkr]
    d_cols = [_tn_matmul_multi(h_b, pieces[0:2], "dw_in_0"), _tn_matmul_multi(h_b, pieces[2:4], "dw_in_1"),
              _tn_matmul_multi(h_b, pieces[4:8], "dw_in_2")]
    pay_b = [_payload_in(d_cols), _tn_matmul(cqn_b, dq_b, "dw_uq", blocked=True),
             _payload_ukv(_tn_matmul(ckvn_b, dk_b, "dw_uk", blocked=True), _tn_matmul(ckvn_b, dv_b, "dw_uv"))]
    pair_b = _pair_sums(pay_b, _pair_exchange(pay_b, "grad_pair_exchange"), place, "b")
    grad_x, small_in, *landed_b = _in_bwd(xs, norm_pre_g, dx1, pieces, w_in_p, [sm for sm, _ in pair_b])
    reduced = [(own, l2) for (_, own), l2 in zip(pair_b + pair_a, landed_b + landed_a)]
    slab = jnp.concatenate([small_in[0:1], jnp.pad(small_prep[0:2], ((0, 0), (0, D_MODEL - Q_LORA))),
                            small_mid[3:8]], axis=0)
    return grad_x, reduced, slab
```

```python
import jax
import jax.numpy as jnp
from jax import lax
from jax.experimental import pallas as pl
from jax.experimental.pallas import tpu as pltpu

F32 = jnp.float32
BF16 = jnp.bfloat16
MESH = pl.DeviceIdType.MESH

N_DEV = 8
D_MODEL = 1024
N_HEADS = 8
HEAD_DIM = 64
D_GRP = N_HEADS * HEAD_DIM
Q_LORA = 256
KV_LORA = 128
ROPE_DIM = 32
PLE_DIM = 256
CHUNK_SHIFT = 6
ROPE_THETA = 10000.0
EPS = 1e-6
SB_SCALE = HEAD_DIM ** -0.5
MLA_SCALE = (HEAD_DIM + ROPE_DIM) ** -0.5
NEG = -1e30
LOG2_E = 1.4426950408889634
LN_2 = 0.6931471805599453
SB_CUTOFF = 110.0

ADAM_LR = 0.001
ADAM_B1 = 0.9
ADAM_B2 = 0.999
ADAM_EPS = 1e-08
ADAM_WD = 0.01
ADAM_STEP = 10

LANES = 128
TQ = 256
TK = 256
TM = 256
TS_DW = 2048

D_IN_P = 3072

_NT = (((1,), (1,)), ((), ()))
_TN = (((0,), (0,)), ((), ()))


def _params(sem, vmem_mb):
    return pltpu.CompilerParams(dimension_semantics=sem, vmem_limit_bytes=vmem_mb << 20)


def _dot(a, b):
    return jnp.dot(a, b, preferred_element_type=F32)


def _dot_nt(a, b):
    return lax.dot_general(a, b, _NT, preferred_element_type=F32)


def _dot_tn(a, b):
    return lax.dot_general(a, b, _TN, preferred_element_type=F32)


def _hl_dot(a, b):
    hi = a.astype(BF16)
    lo = (a - hi.astype(F32)).astype(BF16)
    return _dot(hi, b) + _dot(lo, b)


def _sigmoid(x):
    return 1.0 / (1.0 + jnp.exp(-x))


def _rope_swap(x, lane):
    left = pltpu.roll(x, LANES - 16, axis=1)
    right = pltpu.roll(x, 16, axis=1)
    lo = (lane >= 64) & (lane < 80)
    hi = (lane >= 80) & (lane < 96)
    return jnp.where(lo, left, jnp.where(hi, right, 0.0))


def _two_level_gather(x_refs, out_refs, send_sems, recv_sems, local_sems):
    x, y, c = lax.axis_index("x"), lax.axis_index("y"), lax.axis_index("c")
    me, sibling = (x, y, c), (x, y, 1 - c)
    chips = [(1 - x, y), (x, 1 - y), (1 - x, 1 - y)]
    ops = range(len(x_refs))

    def slot(o, px, py, pc):
        return out_refs[o].at[4 * px + 2 * py + pc]

    def copy(o, k, block, to, src=None):
        return pltpu.make_async_remote_copy(
            src_ref=slot(o, *block) if src is None else src, dst_ref=slot(o, *block),
            send_sem=send_sems.at[o, k], recv_sem=recv_sems.at[o, k],
            device_id=to, device_id_type=MESH)

    def mine():
        return [pltpu.make_async_copy(x_refs[o], slot(o, *me), local_sems.at[o]) for o in ops]

    def first():
        return ([copy(o, 0, me, sibling, src=x_refs[o]) for o in ops]
                + [copy(o, 1 + j, me, (*chip, c), src=x_refs[o]) for j, chip in enumerate(chips) for o in ops])

    def start():
        for cp in mine() + first():
            cp.start()

    def finish():
        passed = []
        for j, chip in enumerate(chips):
            for o in ops:
                copy(o, 1 + j, (*chip, c), me).wait_recv()
                passed.append(copy(o, 4 + j, (*chip, c), sibling))
                passed[-1].start()
        for o in ops:
            copy(o, 0, sibling, me).wait_recv()
        for j, chip in enumerate(chips):
            for o in ops:
                copy(o, 4 + j, (*chip, 1 - c), me).wait_recv()
        for cp in first() + passed:
            cp.wait_send()
        for cp in mine():
            cp.wait()

    return start, finish


def _gather_sems(n_op):
    return [pltpu.SemaphoreType.DMA((n_op, 7)), pltpu.SemaphoreType.DMA((n_op, 7)),
            pltpu.SemaphoreType.DMA((n_op,))]


def _all_gather(shards):
    n_op = len(shards)

    def body(*refs):
        start, finish = _two_level_gather(refs[:n_op], refs[n_op:2 * n_op], *refs[2 * n_op:])
        start()
        finish()

    vmem = pl.BlockSpec(memory_space=pltpu.VMEM)
    return pl.pallas_call(
        body, name="weight_all_gather",
        out_shape=[jax.ShapeDtypeStruct((N_DEV,) + a.shape, a.dtype) for a in shards],
        in_specs=[vmem] * n_op, out_specs=[vmem] * n_op, scratch_shapes=_gather_sems(n_op),
        compiler_params=pltpu.CompilerParams(vmem_limit_bytes=48 << 20),
    )(*shards)


def _pair_copies(g_refs, l_refs, ssem, rsem):
    x, y, c = lax.axis_index("x"), lax.axis_index("y"), lax.axis_index("c")
    copies = []
    for o in range(len(g_refs)):
        for chip in range(4):
            copies.append(pltpu.make_async_remote_copy(
                src_ref=g_refs[o].at[2 * chip + (1 - c)], dst_ref=l_refs[o].at[chip],
                send_sem=ssem.at[o, chip], recv_sem=rsem.at[o, chip],
                device_id=(x, y, 1 - c), device_id_type=MESH))
    return copies


def _pair_specs(pays):
    n_op = len(pays)
    any_spec = pl.BlockSpec(memory_space=pl.ANY)
    return ([any_spec] * n_op, [any_spec] * n_op,
            [jax.ShapeDtypeStruct((4,) + a.shape[1:], F32) for a in pays],
            [pltpu.SemaphoreType.DMA((n_op, 4)), pltpu.SemaphoreType.DMA((n_op, 4))])


def _pair_exchange(pays, name):
    n_op = len(pays)
    in_specs, out_specs, out_shape, sems = _pair_specs(pays)

    def body(*refs):
        copies = _pair_copies(refs[:n_op], refs[n_op:2 * n_op], *refs[2 * n_op:])
        for cp in copies:
            cp.start()
        for cp in copies:
            cp.wait()

    return pl.pallas_call(body, name=name, out_shape=out_shape, in_specs=in_specs, out_specs=out_specs,
                          scratch_shapes=sems)(*pays)


def _slab_exchange(small):
    sr, n = small.shape

    def body(s_ref, sland_ref, ssem, rsem, lsem):
        x, y, c = lax.axis_index("x"), lax.axis_index("y"), lax.axis_index("c")
        me = 4 * x + 2 * y + c
        copies = []
        for k in range(1, N_DEV):
            peer = (1 - x if (k >> 2) & 1 else x, 1 - y if (k >> 1) & 1 else y, 1 - c if k & 1 else c)
            copies.append(pltpu.make_async_remote_copy(
                src_ref=s_ref, dst_ref=sland_ref.at[me], send_sem=ssem.at[k], recv_sem=rsem.at[k],
                device_id=peer, device_id_type=MESH))
        own = pltpu.make_async_copy(s_ref, sland_ref.at[me], lsem)
        own.start()
        for cp in copies:
            cp.start()
        for cp in copies:
            cp.wait()
        own.wait()

    vmem = pl.BlockSpec(memory_space=pltpu.VMEM)
    return pl.pallas_call(
        body, name="grad_slab_exchange", out_shape=jax.ShapeDtypeStruct((N_DEV, sr, n), F32),
        in_specs=[vmem], out_specs=vmem,
        scratch_shapes=[pltpu.SemaphoreType.DMA((N_DEV,)), pltpu.SemaphoreType.DMA((N_DEV,)),
                        pltpu.SemaphoreType.DMA],
    )(small)


def _pair_sum(pay, landed, place, name):
    _, r, c = pay.shape

    def body(place_ref, g_ref, l_ref, s_ref, own_ref):
        i = pl.program_id(0)
        tot = g_ref[...] + l_ref[...]
        s_ref[...] = tot.astype(BF16)

        @pl.when(i == place_ref[1])
        def _():
            own_ref[...] = tot

    grid_spec = pltpu.PrefetchScalarGridSpec(
        num_scalar_prefetch=1, grid=(4,),
        in_specs=[pl.BlockSpec((None, r, c), lambda i, pr: (2 * i + pr[0], 0, 0)),
                  pl.BlockSpec((None, r, c), lambda i, pr: (i, 0, 0))],
        out_specs=[pl.BlockSpec((None, r, c), lambda i, pr: (i, 0, 0)),
                   pl.BlockSpec((r, c), lambda i, pr: (0, 0))])
    return pl.pallas_call(
        body, name=name, grid_spec=grid_spec,
        out_shape=[jax.ShapeDtypeStruct((4, r, c), BF16), jax.ShapeDtypeStruct((r, c), F32)],
        compiler_params=_params(("arbitrary",), 40),
    )(place, pay, landed)


def _chip_copies(s_refs, l_refs, ssem, rsem):
    x, y, c = lax.axis_index("x"), lax.axis_index("y"), lax.axis_index("c")
    copies = []
    for rel in range(1, 4):
        px = 1 - x if rel & 2 else x
        py = 1 - y if rel & 1 else y
        for o in range(len(s_refs)):
            copies.append(pltpu.make_async_remote_copy(
                src_ref=s_refs[o].at[2 * px + py], dst_ref=l_refs[o].at[rel - 1],
                send_sem=ssem.at[o, rel - 1], recv_sem=rsem.at[o, rel - 1],
                device_id=(px, py, c), device_id_type=MESH))
    return copies


def _chip_specs(sums):
    n_op = len(sums)
    any_spec = pl.BlockSpec(memory_space=pl.ANY)
    return ([any_spec] * n_op, [any_spec] * n_op,
            [jax.ShapeDtypeStruct((3,) + a.shape[1:], BF16) for a in sums],
            [pltpu.SemaphoreType.DMA((n_op, 3)), pltpu.SemaphoreType.DMA((n_op, 3))])


def _adamw_math(g, w, m, v):
    mn = ADAM_B1 * m + (1.0 - ADAM_B1) * g
    vn = ADAM_B2 * v + (1.0 - ADAM_B2) * (g * g)
    m_hat = mn / (1.0 - ADAM_B1 ** ADAM_STEP)
    v_hat = vn / (1.0 - ADAM_B2 ** ADAM_STEP)
    return -ADAM_LR * (m_hat / (jnp.sqrt(v_hat) + ADAM_EPS) + ADAM_WD * w), mn, vn


def _adamw_matrix(own, landed, w, m, v, name):
    r, c = w.shape
    cp = own.shape[1]
    br = min(r, 256)

    def body(own_ref, l_ref, w_ref, m_ref, v_ref, g_out, d_out, m_out, v_out):
        g = own_ref[...]
        for k in range(3):
            g = g + l_ref[k].astype(F32)
        g = g[:, :c]
        g_out[...] = g
        d_out[...], m_out[...], v_out[...] = _adamw_math(g, w_ref[...], m_ref[...], v_ref[...])

    row = pl.BlockSpec((br, c), lambda i: (i, 0))
    shp = jax.ShapeDtypeStruct((r, c), F32)
    return pl.pallas_call(
        body, name=name, grid=(r // br,),
        in_specs=[pl.BlockSpec((br, cp), lambda i: (i, 0)), pl.BlockSpec((3, br, cp), lambda i: (0, i, 0)),
                  row, row, row],
        out_specs=(row, row, row, row), out_shape=(shp, shp, shp, shp),
        compiler_params=_params(("parallel",), 40),
    )(own, landed, w, m, v)


_VEC_PLACE = ((0, 0), (1, 0), (2, 0), (3, 0), (3, D_GRP), (4, 0), (5, 0), (6, 0))


def _adamw_vectors(sland, ws, ms, vs):
    nv = len(ws)

    def body(l_ref, *refs):
        w_refs, m_refs, v_refs = refs[:nv], refs[nv:2 * nv], refs[2 * nv:3 * nv]
        loss_ref = refs[3 * nv]
        outs = refs[3 * nv + 1:]
        g_all = l_ref[0]
        for j in range(1, N_DEV):
            g_all = g_all + l_ref[j]
        loss_ref[...] = jnp.sum(g_all[7:8, :], axis=1, keepdims=True)
        for k, (row, lane0) in enumerate(_VEC_PLACE):
            n = w_refs[k].shape[1]
            g = g_all[row:row + 1, lane0:lane0 + n]
            d, mn, vn = _adamw_math(g, w_refs[k][...], m_refs[k][...], v_refs[k][...])
            outs[k][...] = g
            outs[nv + k][...] = d
            outs[2 * nv + k][...] = mn
            outs[3 * nv + k][...] = vn

    vmem = pl.BlockSpec(memory_space=pltpu.VMEM)
    shapes = [jax.ShapeDtypeStruct(w.shape, F32) for w in ws]
    return pl.pallas_call(
        body, name="adamw_vectors", in_specs=[vmem] * (1 + 3 * nv), out_specs=[vmem] * (1 + 4 * nv),
        out_shape=[jax.ShapeDtypeStruct((1, 1), F32)] + shapes * 4,
    )(sland, *ws, *ms, *vs)


def _in_proj(x, g, w, shards):
    s = x.shape[0]
    n_op = len(shards)
    steps = s // TM

    def body(x_ref, g_ref, w_ref, *refs):
        shard_refs = refs[:n_op]
        qkv_ref, rest_ref, h_ref = refs[n_op:n_op + 3]
        gath_refs = refs[n_op + 3:2 * n_op + 3]
        start, finish = _two_level_gather(shard_refs, gath_refs, *refs[2 * n_op + 3:])
        i = pl.program_id(0)

        @pl.when(i == 0)
        def _():
            start()

        xv = x_ref[...]
        r = lax.rsqrt(jnp.mean(xv * xv, axis=-1, keepdims=True) + EPS)
        h = ((xv * r) * g_ref[...]).astype(BF16)
        h_ref[...] = h
        qkv_ref[...] = _dot(h, w_ref[:, :1536]).astype(BF16)
        rest_ref[...] = _dot(h, w_ref[:, 1536:])

        @pl.when(i == steps - 1)
        def _():
            finish()

    any_spec = pl.BlockSpec(memory_space=pl.ANY)
    return pl.pallas_call(
        body, name="in_proj", grid=(steps,),
        in_specs=[pl.BlockSpec((TM, D_MODEL), lambda i: (i, 0)),
                  pl.BlockSpec((1, D_MODEL), lambda i: (0, 0)),
                  pl.BlockSpec((D_MODEL, D_IN_P), lambda i: (0, 0))] + [any_spec] * n_op,
        out_specs=[pl.BlockSpec((TM, 1536), lambda i: (i, 0)),
                   pl.BlockSpec((TM, 1536), lambda i: (i, 0)),
                   pl.BlockSpec((TM, D_MODEL), lambda i: (i, 0))] + [any_spec] * n_op,
        out_shape=[jax.ShapeDtypeStruct((s, 1536), BF16), jax.ShapeDtypeStruct((s, 1536), F32),
                   jax.ShapeDtypeStruct((s, D_MODEL), BF16)]
        + [jax.ShapeDtypeStruct((N_DEV,) + a.shape, a.dtype) for a in shards],
        scratch_shapes=_gather_sems(n_op),
        compiler_params=_params(("arbitrary",), 48),
    )(x, g, w, *shards)


def _mla_prep(rest, gq, gkv, wuq, wuk, wuv, cos_t, sin_t):
    s = rest.shape[0]

    def body(cq_ref, ckv_ref, kr_ref, gq_ref, gkv_ref, wuq_ref, wuk_ref, wuv_ref, c_ref, s_ref,
             qp_ref, kp_ref, vv_ref, cqn_ref, ckvn_ref):
        lane = lax.broadcasted_iota(jnp.int32, (1, LANES), 1)
        cos_v, sin_v = c_ref[...], s_ref[...]
        cq = cq_ref[...]
        rq = lax.rsqrt(jnp.mean(cq * cq, axis=-1, keepdims=True) + EPS)
        cqn = ((cq * rq) * gq_ref[...]).astype(BF16)
        cqn_ref[...] = cqn
        q = _dot(cqn, wuq_ref[...])
        ckv = ckv_ref[...]
        rkv = lax.rsqrt(jnp.mean(ckv * ckv, axis=-1, keepdims=True) + EPS)
        ckvn = ((ckv * rkv) * gkv_ref[...]).astype(BF16)
        ckvn_ref[...] = ckvn
        kn = _dot(ckvn, wuk_ref[...])
        vv_ref[...] = _dot(ckvn, wuv_ref[...]).astype(BF16)
        kr = kr_ref[...]
        kr_roped = kr * cos_v + _rope_swap(kr, lane) * sin_v
        for h in range(N_HEADS):
            sl = slice(h * LANES, (h + 1) * LANES)
            qh = q[:, sl]
            qp_ref[:, sl] = (qh * cos_v + _rope_swap(qh, lane) * sin_v).astype(BF16)
            kp_ref[:, sl] = (kn[:, sl] + kr_roped).astype(BF16)

    def row(width, idx):
        return pl.BlockSpec((TM, width), lambda i: (i, idx))

    def full(a):
        return pl.BlockSpec(a.shape, lambda i: (0, 0))

    return pl.pallas_call(
        body, name="mla_prep", grid=(s // TM,),
        in_specs=[row(Q_LORA, 4), row(KV_LORA, 10), row(LANES, 11), full(gq), full(gkv),
                  full(wuq), full(wuk), full(wuv), row(LANES, 0), row(LANES, 0)],
        out_specs=(row(1024, 0), row(1024, 0), row(D_GRP, 0), row(Q_LORA, 0), row(KV_LORA, 0)),
        out_shape=(jax.ShapeDtypeStruct((s, 1024), BF16), jax.ShapeDtypeStruct((s, 1024), BF16),
                   jax.ShapeDtypeStruct((s, D_GRP), BF16), jax.ShapeDtypeStruct((s, Q_LORA), BF16),
                   jax.ShapeDtypeStruct((s, KV_LORA), BF16)),
        compiler_params=_params(("parallel",), 32),
    )(rest, rest, rest, gq, gkv, wuq, wuk, wuv, cos_t, sin_t)


def _sb_live(n, qi, carries):
    top = carries[0]
    for c in carries[1:]:
        top = jnp.maximum(top, c)
    return jnp.logical_and(n < qi, jnp.max(top) > -SB_CUTOFF)


def _sb_fwd(qkv, hb):
    s = qkv.shape[0]

    def body(q_ref, k_ref, v_ref, o_ref, acc):
        qi = pl.program_id(1)
        lane = lax.broadcasted_iota(jnp.int32, (1, LANES), 1)
        is_a = lane < HEAD_DIM
        pair = lambda h: slice((h // 2) * LANES, (h // 2 + 1) * LANES)
        q_h = []
        for h in range(hb):
            qs = q_ref[:, pair(h)] * SB_SCALE
            mine = is_a if h % 2 == 0 else jnp.logical_not(is_a)
            q_h.append(jnp.where(mine, qs, jnp.zeros_like(qs)))
        r_i = lax.broadcasted_iota(jnp.int32, (TQ, TK), 0)
        c_i = lax.broadcasted_iota(jnp.int32, (TQ, TK), 1)
        past = c_i < r_i
        upper = (r_i > c_i).astype(BF16)
        acc[...] = jnp.zeros_like(acc)

        def tile(j, carries, diag):
            ks = pl.ds(pl.multiple_of(j * TK, TK), TK)
            zs = [_dot_nt(q_h[h], k_ref[ks, pair(h)]) for h in range(hb)]
            if diag:
                zs = [jnp.where(past, z, NEG) for z in zs]
            lfs = [-(jnp.maximum(z, 0.0) + jnp.log(1.0 + jnp.exp(-jnp.abs(z)))) for z in zs]
            sufs = [_hl_dot(lfs[h], upper) for h in range(hb)]
            out = []
            for h in range(hb):
                w = jnp.exp(zs[h] + lfs[h] + (sufs[h] + carries[h]))
                acc[h] += _dot(w.astype(BF16), v_ref[ks, pair(h)])
                out.append(carries[h] + jnp.sum(lfs[h], axis=1, keepdims=True))
            return tuple(out)

        zero = jnp.zeros((TQ, 1), F32)
        carries = tile(qi, (zero,) * hb, True)

        def step(st):
            return (st[0] + 1,) + tile(qi - 1 - st[0], st[1:], False)

        lax.while_loop(lambda st: _sb_live(st[0], qi, st[1:]), step, (0,) + carries)
        for pr in range(hb // 2):
            o_ref[:, pr * LANES:(pr + 1) * LANES] = jnp.where(is_a, acc[2 * pr], acc[2 * pr + 1])

    width = hb * HEAD_DIM
    nb = D_GRP // width
    slab = lambda part: pl.BlockSpec((s, width), lambda g, qi: (0, part * nb + g))
    blk = pl.BlockSpec((TQ, width), lambda g, qi: (qi, g))
    return pl.pallas_call(
        body, name="sb_fwd", grid=(nb, s // TQ),
        in_specs=[blk, slab(1), slab(2)], out_specs=blk,
        out_shape=jax.ShapeDtypeStruct((s, D_GRP), F32),
        scratch_shapes=[pltpu.VMEM((hb, TQ, LANES), F32)],
        compiler_params=_params(("arbitrary", "arbitrary"), 48),
    )(qkv, qkv, qkv)


def _sb_bwd(qkv, d_o, sums):
    s = qkv.shape[0]
    nq = s // TQ
    nk = s // TK
    n_op = len(sums)
    ride_in, ride_out, ride_shape, ride_sems = _chip_specs(sums)

    def body(q_ref, k_ref, v_ref, do_ref, *refs):
        s_refs = refs[:n_op]
        dq_ref, dk_ref, dv_ref = refs[n_op:n_op + 3]
        l_refs = refs[n_op + 3:2 * n_op + 3]
        x1s, bts, dqacc, dkacc, dvacc, ssem, rsem = refs[2 * n_op + 3:]
        qi = pl.program_id(1)
        first_step = jnp.logical_and(pl.program_id(0) == 0, qi == 0)
        last_step = jnp.logical_and(pl.program_id(0) == pl.num_programs(0) - 1, qi == nq - 1)

        @pl.when(first_step)
        def _():
            for cp in _chip_copies(s_refs, l_refs, ssem, rsem):
                cp.start()

        lane = lax.broadcasted_iota(jnp.int32, (1, LANES), 1)
        is_a = lane < HEAD_DIM

        @pl.when(qi == 0)
        def _():
            dkacc[...] = jnp.zeros_like(dkacc)
            dvacc[...] = jnp.zeros_like(dvacc)

        qs = q_ref[...] * SB_SCALE
        zq = jnp.zeros_like(qs)
        qs_x = (jnp.where(is_a, qs, zq), jnp.where(is_a, zq, qs))
        dob = do_ref[...].astype(BF16)
        do_x = (jnp.where(is_a, dob, zq), jnp.where(is_a, zq, dob))
        r_i = lax.broadcasted_iota(jnp.int32, (TQ, TK), 0)
        c_i = lax.broadcasted_iota(jnp.int32, (TQ, TK), 1)
        past = c_i < r_i
        upper = (r_i > c_i).astype(BF16)
        upper_incl = (r_i >= c_i).astype(BF16)
        dqacc[...] = jnp.zeros_like(dqacc)
        both = ((0, 0), (0, 1), (1, 0), (1, 1))

        def tiles(n):
            j_hi = qi - 2 * n
            lo_ok = j_hi >= 1
            j_lo = jnp.maximum(j_hi - 1, 0)
            ks = (pl.ds(pl.multiple_of(j_hi * TK, TK), TK), pl.ds(pl.multiple_of(j_lo * TK, TK), TK))
            return j_hi, lo_ok, j_lo, ks

        def sweep(n, carries):
            j_hi, lo_ok, j_lo, ks = tiles(n)
            slot = (j_hi, jnp.where(lo_ok, j_lo, nk))
            valid = (jnp.logical_or(past, j_hi < qi), lo_ok)
            z = {th: jnp.where(valid[th[0]], _dot_nt(qs_x[th[1]], k_ref[ks[th[0]], :]), NEG) for th in both}
            log_b, lf_sum, suf = {}, {}, {}
            for th in both:
                lf = -(jnp.maximum(z[th], 0.0) + jnp.log(1.0 + jnp.exp(-jnp.abs(z[th]))))
                log_b[th] = z[th] + lf
                lf_sum[th] = jnp.sum(lf, axis=1, keepdims=True)
                suf[th] = _hl_dot(lf, upper)
            c, g_in = {}, {}
            for h in range(2):
                c[0, h], g_in[0, h] = carries[2 * h], carries[2 * h + 1]
                c[1, h] = c[0, h] + lf_sum[0, h]
            d_a = {th: _dot_nt(do_x[th[1]], v_ref[ks[th[0]], :]) for th in both}
            a_b, g, g_sum, sg = {}, {}, {}, {}
            for th in both:
                a = jnp.exp(log_b[th] + (suf[th] + c[th]))
                a_b[th] = a.astype(BF16)
                g[th] = a * d_a[th]
                g_sum[th] = jnp.sum(g[th], axis=1, keepdims=True)
                sg[th] = _hl_dot(g[th], upper_incl)
            for h in range(2):
                g_in[1, h] = g_in[0, h] + g_sum[0, h]
            for th in both:
                t, h = th
                beta = jnp.exp(log_b[th])
                x1s[slot[t], h] = g[th] * (1.0 - beta) + beta * (sg[th] + g_in[th])
                bts[slot[t], h] = beta
                dvacc[ks[t], :] += _dot_tn(a_b[th], do_x[h])
            out = []
            for h in range(2):
                out.append(c[1, h] + lf_sum[1, h])
                out.append(g_in[1, h] + g_sum[1, h])
            return tuple(out)

        zero = jnp.zeros((TQ, 1), F32)
        first = sweep(0, (zero, zero, zero, zero))

        def more(st):
            return jnp.logical_and(2 * st[0] <= qi, jnp.max(jnp.maximum(st[1], st[3])) > -SB_CUTOFF)

        swept = lax.while_loop(more, lambda st: (st[0] + 1,) + sweep(st[0], st[1:]), (1,) + first)
        g_tot = (swept[2], swept[4])

        def apply(n, carry):
            j_hi, lo_ok, j_lo, ks = tiles(n)

            def one(j, kslice):
                for h in range(2):
                    dz = (x1s[j, h] - bts[j, h] * g_tot[h]).astype(BF16)
                    dqacc[h] += _dot(dz, k_ref[kslice, :])
                    dkacc[kslice, :] += _dot_tn(dz, qs_x[h])

            one(j_hi, ks[0])

            @pl.when(lo_ok)
            def _():
                one(j_lo, ks[1])

            return carry

        lax.fori_loop(0, swept[0], apply, 0)
        dq_ref[...] = (jnp.where(is_a, dqacc[0], dqacc[1]) * SB_SCALE).astype(BF16)

        @pl.when(qi == nq - 1)
        def _():
            dk_ref[...] = dkacc[...].astype(BF16)
            dv_ref[...] = dvacc[...].astype(BF16)

        @pl.when(last_step)
        def _():
            for cp in _chip_copies(s_refs, l_refs, ssem, rsem):
                cp.wait()

    slab = lambda off: pl.BlockSpec((s, LANES), lambda p, qi: (0, off + p))
    blk = pl.BlockSpec((TQ, LANES), lambda p, qi: (qi, p))
    out_slab = pl.BlockSpec((s, LANES), lambda p, qi: (0, p))
    shp = jax.ShapeDtypeStruct((s, D_GRP), BF16)
    return pl.pallas_call(
        body, name="sb_bwd", grid=(4, nq),
        in_specs=[blk, slab(4), slab(8), blk] + ride_in,
        out_specs=[blk, out_slab, out_slab] + ride_out, out_shape=[shp, shp, shp] + ride_shape,
        scratch_shapes=[pltpu.VMEM((nk + 1, 2, TQ, TK), F32)] * 2
        + [pltpu.VMEM((2, TQ, LANES), F32), pltpu.VMEM((s, LANES), F32), pltpu.VMEM((s, LANES), F32)]
        + ride_sems,
        compiler_params=_params(("arbitrary", "arbitrary"), 56),
    )(qkv, qkv, qkv, d_o, *sums)


def _mla_fwd(qp, kp, vv, hb):
    s = qp.shape[0]
    c2 = MLA_SCALE * LOG2_E

    def body(q_ref, k_ref, v_ref, o_ref, lse_ref, vaug, mrun, mb, acc, zbuf):
        qi = pl.program_id(1)
        lane = lax.broadcasted_iota(jnp.int32, (1, LANES), 1)
        is_a = lane < HEAD_DIM

        @pl.when(qi == 0)
        def _():
            for h in range(hb):
                vp = v_ref[:, (h // 2) * LANES:(h // 2 + 1) * LANES]
                mine = is_a if h % 2 == 0 else jnp.logical_not(is_a)
                vaug[h] = jnp.where(mine, vp, jnp.ones_like(vp))

        r_i = lax.broadcasted_iota(jnp.int32, (TQ, TK), 0)
        c_i = lax.broadcasted_iota(jnp.int32, (TQ, TK), 1)
        visible = (c_i >> CHUNK_SHIFT) <= (r_i >> CHUNK_SHIFT)

        def key_rows(j):
            return pl.ds(pl.multiple_of(j * TK, TK), TK)

        def sweep(tiles):
            def loop(n, carry):
                tiles(((2 * n, False), (2 * n + 1, False)))
                return carry

            lax.fori_loop(0, qi // 2, loop, 0)

            @pl.when(qi % 2 == 1)
            def _():
                tiles(((qi - 1, False), (qi, True)))

            @pl.when(qi % 2 == 0)
            def _():
                tiles(((qi, True),))

        mrun[...] = jnp.full_like(mrun, NEG)

        def tiles_max(js):
            zs = [[_dot_nt(q_ref[:, h * LANES:(h + 1) * LANES], k_ref[key_rows(j), h * LANES:(h + 1) * LANES])
                   for h in range(hb)] for j, _ in js]
            for t, (j, diag) in enumerate(js):
                for h in range(hb):
                    z = jnp.where(visible, zs[t][h], NEG) if diag else zs[t][h]
                    zbuf[j, h] = z
                    mrun[h] = jnp.maximum(mrun[h], z)

        sweep(tiles_max)
        for h in range(hb):
            m = jnp.max(mrun[h], axis=1, keepdims=True) * c2
            mb[h] = jnp.broadcast_to(m, (TQ, TK))
        acc[...] = jnp.zeros_like(acc)

        def tiles_pv(js):
            ps = [[jnp.exp2((zbuf[j, h] * c2 - mb[h]).astype(BF16)) for h in range(hb)] for j, _ in js]
            for t, (j, _) in enumerate(js):
                for h in range(hb):
                    acc[h] += _dot(ps[t][h], vaug[h, key_rows(j), :])

        sweep(tiles_pv)
        for pr in range(hb // 2):
            a, b = 2 * pr, 2 * pr + 1
            psl = slice(pr * LANES, (pr + 1) * LANES)
            acc_a, acc_b = acc[a], acc[b]
            l_a = pltpu.roll(acc_a, HEAD_DIM, axis=1)
            l_b = pltpu.roll(acc_b, HEAD_DIM, axis=1)
            o_ref[:, psl] = jnp.where(is_a, acc_a * (1.0 / l_a), acc_b * (1.0 / l_b))
            lse_ref[:, psl] = jnp.where(is_a, mb[a, :, :LANES] * LN_2 + jnp.log(l_a),
                                        mb[b, :, :LANES] * LN_2 + jnp.log(l_b))

    blk = pl.BlockSpec((TQ, hb * HEAD_DIM), lambda g, qi: (qi, g))
    shp = jax.ShapeDtypeStruct((s, D_GRP), F32)
    return pl.pallas_call(
        body, name="mla_fwd", grid=(N_HEADS // hb, s // TQ),
        in_specs=[pl.BlockSpec((TQ, hb * LANES), lambda g, qi: (qi, g)),
                  pl.BlockSpec((s, hb * LANES), lambda g, qi: (0, g)),
                  pl.BlockSpec((s, hb * HEAD_DIM), lambda g, qi: (0, g))],
        out_specs=(blk, blk), out_shape=(shp, shp),
        scratch_shapes=[pltpu.VMEM((hb, s, LANES), BF16), pltpu.VMEM((hb, TQ, TK), F32),
                        pltpu.VMEM((hb, TQ, TK), F32), pltpu.VMEM((hb, TQ, LANES), F32),
                        pltpu.VMEM((s // TK, hb, TQ, TK), F32)],
        compiler_params=_params(("arbitrary", "arbitrary"), 56),
    )(qp, kp, vv)


def _mla_bwd(qp, kp, vv, d_o, o, lse, hb, pays):
    s = qp.shape[0]
    nq = s // TQ
    c2 = MLA_SCALE * LOG2_E
    n_op = len(pays)
    ride_in, ride_out, ride_shape, ride_sems = _pair_specs(pays)

    def body(q_ref, k_ref, v_ref, do_ref, o_ref, lse_ref, *refs):
        g_refs = refs[:n_op]
        dq_ref, dk_ref, dv_ref = refs[n_op:n_op + 3]
        l_refs = refs[n_op + 3:2 * n_op + 3]
        dqacc, lse_b, delta_b, ssem, rsem = refs[2 * n_op + 3:]
        qi = pl.program_id(1)

        @pl.when(jnp.logical_and(pl.program_id(0) == 0, qi == 0))
        def _():
            for cp in _pair_copies(g_refs, l_refs, ssem, rsem):
                cp.start()

        lane = lax.broadcasted_iota(jnp.int32, (1, LANES), 1)
        is_a = lane < HEAD_DIM

        @pl.when(qi == 0)
        def _():
            dk_ref[...] = jnp.zeros_like(dk_ref)
            dv_ref[...] = jnp.zeros_like(dv_ref)

        r_i = lax.broadcasted_iota(jnp.int32, (TQ, TK), 0)
        c_i = lax.broadcasted_iota(jnp.int32, (TQ, TK), 1)
        visible = (c_i >> CHUNK_SHIFT) <= (r_i >> CHUNK_SHIFT)
        do_x = []
        for h in range(hb):
            psl = slice((h // 2) * LANES, (h // 2 + 1) * LANES)
            mine = is_a if h % 2 == 0 else jnp.logical_not(is_a)
            d_o = do_ref[:, psl]
            delta = jnp.sum(jnp.where(mine, d_o * o_ref[:, psl], 0.0), axis=1, keepdims=True)
            lse_h = jnp.sum(jnp.where(lane == (h % 2) * HEAD_DIM, lse_ref[:, psl], 0.0), axis=1, keepdims=True)
            lse_b[h] = jnp.broadcast_to(lse_h * LOG2_E, (TQ, TK))
            delta_b[h] = jnp.broadcast_to(delta, (TQ, TK))
            do_x.append(jnp.where(mine, d_o, 0.0).astype(BF16))
        dqacc[...] = jnp.zeros_like(dqacc)

        head = lambda h: slice(h * LANES, (h + 1) * LANES)
        pair = lambda h: slice((h // 2) * LANES, (h // 2 + 1) * LANES)

        def tiles(js):
            th = [(j, diag, pl.ds(pl.multiple_of(j * TK, TK), TK), h) for j, diag in js for h in range(hb)]
            zs = [_dot_nt(q_ref[:, head(h)], k_ref[ks, head(h)]) for _, _, ks, h in th]
            dps = [_dot_nt(do_x[h], v_ref[ks, pair(h)]) for _, _, ks, h in th]
            for i, (j, diag, ks, h) in enumerate(th):
                e = zs[i] * c2 - lse_b[h]
                if diag:
                    e = jnp.where(visible, e, NEG)
                p = jnp.exp2(e)
                ds = (p * (dps[i] - delta_b[h]) * MLA_SCALE).astype(BF16)
                dqacc[h] += _dot(ds, k_ref[ks, head(h)])
                dk_ref[ks, head(h)] += _dot_tn(ds, q_ref[:, head(h)])
                dv_ref[ks, pair(h)] += _dot_tn(p.astype(BF16), do_x[h])

        def loop(n, c):
            tiles(((2 * n, False), (2 * n + 1, False)))
            return c

        lax.fori_loop(0, qi // 2, loop, 0)

        @pl.when(qi % 2 == 1)
        def _():
            tiles(((qi - 1, False), (qi, True)))

        @pl.when(qi % 2 == 0)
        def _():
            tiles(((qi, True),))

        for h in range(hb):
            dq_ref[:, h * LANES:(h + 1) * LANES] = dqacc[h]

        @pl.when(jnp.logical_and(pl.program_id(0) == pl.num_programs(0) - 1, qi == nq - 1))
        def _():
            for cp in _pair_copies(g_refs, l_refs, ssem, rsem):
                cp.wait()

    blk = pl.BlockSpec((TQ, hb * HEAD_DIM), lambda g, qi: (qi, g))
    return pl.pallas_call(
        body, name="mla_bwd", grid=(N_HEADS // hb, nq),
        in_specs=[pl.BlockSpec((TQ, hb * LANES), lambda g, qi: (qi, g)),
                  pl.BlockSpec((s, hb * LANES), lambda g, qi: (0, g)),
                  pl.BlockSpec((s, hb * HEAD_DIM), lambda g, qi: (0, g)), blk, blk, blk] + ride_in,
        out_specs=[pl.BlockSpec((TQ, hb * LANES), lambda g, qi: (qi, g)),
                   pl.BlockSpec((s, hb * LANES), lambda g, qi: (0, g)),
                   pl.BlockSpec((s, hb * HEAD_DIM), lambda g, qi: (0, g))] + ride_out,
        out_shape=[jax.ShapeDtypeStruct((s, 1024), F32), jax.ShapeDtypeStruct((s, 1024), F32),
                   jax.ShapeDtypeStruct((s, D_GRP), F32)] + ride_shape,
        scratch_shapes=[pltpu.VMEM((hb, TQ, LANES), F32), pltpu.VMEM((hb, TQ, TK), F32),
                        pltpu.VMEM((hb, TQ, TK), F32)] + ride_sems,
        compiler_params=_params(("arbitrary", "arbitrary"), 56),
    )(qp, kp, vv, d_o, o, lse, *pays)


def _mid(x, p, target, sb_o, mla_o, rest, g_sb, g_mla, w_out, g_post, w_ple, g_ple, w_pg, b_pg, bd):
    s = x.shape[0]

    def body(x_ref, p_ref, t_ref, sbo_ref, mlo_ref, sbg_ref, mlg_ref, gsb_ref, gml_ref, wout_ref,
             gpost_ref, wple_ref, gple_ref, wpg_ref, bpg_ref, bd_ref,
             dx1_ref, dsbo_ref, dmlo_ref, dsbg_ref, dmlg_ref, x1b_ref, dglb_ref, ycb_ref, dyb_ref,
             pb_ref, dub_ref, small_ref):
        i = pl.program_id(0)
        bd_m = bd_ref[...]

        def seg_mean(v):
            return _dot(v.astype(BF16), bd_m) * (1.0 / HEAD_DIM)

        groups = []
        for o_ref, gate_ref, gain_ref in ((sbo_ref, sbg_ref, gsb_ref), (mlo_ref, mlg_ref, gml_ref)):
            o = o_ref[...]
            r = lax.rsqrt(seg_mean(o * o) + EPS)
            n = o * r
            hn = n * gain_ref[...]
            gate = gate_ref[...]
            sg = _sigmoid(gate)
            si = gate * sg
            groups.append((r, n, hn, gate, sg, si, gain_ref[...]))
        ya = (groups[0][2] * groups[0][5]).astype(BF16)
        yb = (groups[1][2] * groups[1][5]).astype(BF16)
        ycb_ref[:, :D_GRP] = ya
        ycb_ref[:, D_GRP:] = yb
        y = _dot(ya, wout_ref[:D_GRP, :]) + _dot(yb, wout_ref[D_GRP:, :])
        ry = lax.rsqrt(jnp.mean(y * y, axis=-1, keepdims=True) + EPS)
        ny = y * ry
        x1 = x_ref[...] + ny * gpost_ref[...]
        x1b = x1.astype(BF16)
        x1b_ref[...] = x1b
        pb = p_ref[...].astype(BF16)
        pb_ref[...] = pb
        u = _dot(pb, wple_ref[...])
        ru = lax.rsqrt(jnp.mean(u * u, axis=-1, keepdims=True) + EPS)
        nu = u * ru
        ple = nu * gple_ref[...]
        gate = _sigmoid(_dot(x1b, wpg_ref[...]) + bpg_ref[...])
        x2 = x1 + ple * gate
        diff = x2 - t_ref[...]
        dx2 = diff * (1.0 / D_MODEL)

        d_ple = dx2 * gate
        d_glin = (dx2 * ple) * (gate * (1.0 - gate))
        dglb = d_glin.astype(BF16)
        dglb_ref[...] = dglb
        dx1 = dx2 + _dot_nt(dglb, wpg_ref[...])
        dx1_ref[...] = dx1
        d_nu = d_ple * gple_ref[...]
        d_u = ru * (d_nu - nu * jnp.mean(d_nu * nu, axis=-1, keepdims=True))
        dub_ref[...] = d_u.astype(BF16)
        d_ny = dx1 * gpost_ref[...]
        d_y = ry * (d_ny - ny * jnp.mean(d_ny * ny, axis=-1, keepdims=True))
        dyb = d_y.astype(BF16)
        dyb_ref[...] = dyb
        d_yc = (_dot_nt(dyb, wout_ref[:D_GRP, :]), _dot_nt(dyb, wout_ref[D_GRP:, :]))

        d_gain = []
        for gx, (do_ref, dg_ref) in enumerate(((dsbo_ref, dsbg_ref), (dmlo_ref, dmlg_ref))):
            r, n, hn, gate_g, sg, si, gain = groups[gx]
            dyg = d_yc[gx]
            d_hn = dyg * si
            dg_ref[...] = (dyg * hn * (sg * (1.0 + gate_g * (1.0 - sg)))).astype(BF16)
            d_gain.append(jnp.sum(d_hn * n, axis=0, keepdims=True))
            d_n = d_hn * gain
            do_ref[...] = r * (d_n - n * seg_mean(d_n * n))

        @pl.when(i == 0)
        def _():
            small_ref[...] = jnp.zeros_like(small_ref)

        small_ref[3:4, :D_GRP] += d_gain[0]
        small_ref[3:4, D_GRP:] += d_gain[1]
        small_ref[4:5, :] += jnp.sum(dx1 * ny, axis=0, keepdims=True)
        small_ref[5:6, :] += jnp.sum(d_ple * nu, axis=0, keepdims=True)
        small_ref[6:7, :] += jnp.sum(d_glin, axis=0, keepdims=True)
        small_ref[7:8, :] += jnp.sum(diff * diff, axis=0, keepdims=True) * (0.5 / D_MODEL)

    def row(width, idx=0):
        return pl.BlockSpec((TM, width), lambda i: (i, idx))

    def full(a):
        return pl.BlockSpec(a.shape, lambda i: (0, 0))

    f32 = lambda w: jax.ShapeDtypeStruct((s, w), F32)
    b16 = lambda w: jax.ShapeDtypeStruct((s, w), BF16)
    return pl.pallas_call(
        body, name="mid", grid=(s // TM,),
        in_specs=[row(D_MODEL), row(PLE_DIM), row(D_MODEL), row(D_GRP), row(D_GRP),
                  row(D_GRP, 0), row(D_GRP, 1), full(g_sb), full(g_mla), full(w_out), full(g_post),
                  full(w_ple), full(g_ple), full(w_pg), full(b_pg), full(bd)],
        out_specs=(row(D_MODEL), row(D_GRP), row(D_GRP), row(D_GRP), row(D_GRP), row(D_MODEL),
                   row(D_MODEL), row(D_MODEL), row(D_MODEL), row(PLE_DIM), row(D_MODEL),
                   pl.BlockSpec((8, D_MODEL), lambda i: (0, 0))),
        out_shape=(f32(D_MODEL), f32(D_GRP), f32(D_GRP), b16(D_GRP), b16(D_GRP), b16(D_MODEL),
                   b16(D_MODEL), b16(D_MODEL), b16(D_MODEL), b16(PLE_DIM), b16(D_MODEL),
                   jax.ShapeDtypeStruct((8, D_MODEL), F32)),
        compiler_params=_params(("arbitrary",), 56),
    )(x, p, target, sb_o, mla_o, rest, rest, g_sb, g_mla, w_out, g_post, w_ple, g_ple, w_pg, b_pg, bd)


def _mla_prep_bwd(dqp, dkp, dvv, rest, gq, gkv, wuq, wuk, wuv, cos_t, sin_t):
    s = rest.shape[0]

    def body(dqp_ref, dkp_ref, dvv_ref, cq_ref, ckv_ref, gq_ref, gkv_ref, wuq_ref, wuk_ref, wuv_ref,
             c_ref, s_ref, dcq_ref, dckv_ref, dkr_ref, dqb_ref, dkb_ref, dvb_ref, small_ref):
        i = pl.program_id(0)
        lane = lax.broadcasted_iota(jnp.int32, (1, LANES), 1)
        in_rope = (lane >= HEAD_DIM) & (lane < HEAD_DIM + ROPE_DIM)
        cos_v, sin_v = c_ref[...], s_ref[...]
        dkr_roped = jnp.zeros((TM, LANES), F32)
        for h in range(N_HEADS):
            sl = slice(h * LANES, (h + 1) * LANES)
            dy = dqp_ref[:, sl]
            dqb_ref[:, sl] = (dy * cos_v + _rope_swap(dy * sin_v, lane)).astype(BF16)
            dkh = dkp_ref[:, sl]
            dkb_ref[:, sl] = dkh.astype(BF16)
            dkr_roped = dkr_roped + jnp.where(in_rope, dkh, 0.0)
        dkr_ref[...] = (dkr_roped * cos_v + _rope_swap(dkr_roped * sin_v, lane)).astype(BF16)
        dvb = dvv_ref[...].astype(BF16)
        dvb_ref[...] = dvb

        cq = cq_ref[...]
        rq = lax.rsqrt(jnp.mean(cq * cq, axis=-1, keepdims=True) + EPS)
        nq_ = cq * rq
        d_cqn = _dot_nt(dqb_ref[...], wuq_ref[...])
        d_n = d_cqn * gq_ref[...]
        dcq_ref[...] = (rq * (d_n - nq_ * jnp.mean(d_n * nq_, axis=-1, keepdims=True))).astype(BF16)

        ckv = ckv_ref[...]
        rkv = lax.rsqrt(jnp.mean(ckv * ckv, axis=-1, keepdims=True) + EPS)
        nkv = ckv * rkv
        d_ckvn = _dot_nt(dkb_ref[...], wuk_ref[...]) + _dot_nt(dvb, wuv_ref[...])
        d_n2 = d_ckvn * gkv_ref[...]
        dckv_ref[...] = (rkv * (d_n2 - nkv * jnp.mean(d_n2 * nkv, axis=-1, keepdims=True))).astype(BF16)

        @pl.when(i == 0)
        def _():
            small_ref[...] = jnp.zeros_like(small_ref)

        small_ref[0:1, :] += jnp.sum(d_cqn * nq_, axis=0, keepdims=True)
        small_ref[1:2, :KV_LORA] += jnp.sum(d_ckvn * nkv, axis=0, keepdims=True)

    def row(width, idx=0):
        return pl.BlockSpec((TM, width), lambda i: (i, idx))

    def full(a):
        return pl.BlockSpec(a.shape, lambda i: (0, 0))

    b16 = lambda w: jax.ShapeDtypeStruct((s, w), BF16)
    return pl.pallas_call(
        body, name="mla_prep_bwd", grid=(s // TM,),
        in_specs=[row(1024), row(1024), row(D_GRP), row(Q_LORA, 4), row(KV_LORA, 10), full(gq), full(gkv),
                  full(wuq), full(wuk), full(wuv), row(LANES), row(LANES)],
        out_specs=(row(Q_LORA), row(KV_LORA), row(LANES), row(1024), row(1024), row(D_GRP),
                   pl.BlockSpec((8, Q_LORA), lambda i: (0, 0))),
        out_shape=(b16(Q_LORA), b16(KV_LORA), b16(LANES), b16(1024), b16(1024), b16(D_GRP),
                   jax.ShapeDtypeStruct((8, Q_LORA), F32)),
        compiler_params=_params(("arbitrary",), 40),
    )(dqp, dkp, dvv, rest, rest, gq, gkv, wuq, wuk, wuv, cos_t, sin_t)


def _in_bwd(x, g, dx1, pieces, w, sums):
    s = x.shape[0]
    steps = s // TM
    widths = [a.shape[1] for a in pieces]
    offs = [sum(widths[:k]) for k in range(len(widths))]
    n_pc, n_op = len(pieces), len(sums)
    ride_in, ride_out, ride_shape, ride_sems = _chip_specs(sums)

    def body(x_ref, g_ref, dx1_ref, *refs):
        piece_refs = refs[:n_pc]
        w_ref = refs[n_pc]
        s_refs = refs[n_pc + 1:n_pc + 1 + n_op]
        dx_ref, small_ref = refs[n_pc + 1 + n_op:n_pc + 3 + n_op]
        l_refs = refs[n_pc + 3 + n_op:n_pc + 3 + 2 * n_op]
        ssem, rsem = refs[n_pc + 3 + 2 * n_op:]
        i = pl.program_id(0)

        @pl.when(i == 0)
        def _():
            for cp in _chip_copies(s_refs, l_refs, ssem, rsem):
                cp.start()

        dh = jnp.zeros((TM, D_MODEL), F32)
        for pr, off, wd in zip(piece_refs, offs, widths):
            dh = dh + _dot_nt(pr[...], w_ref[:, off:off + wd])
        xv = x_ref[...]
        r = lax.rsqrt(jnp.mean(xv * xv, axis=-1, keepdims=True) + EPS)
        n = xv * r
        d_n = dh * g_ref[...]
        dx_ref[...] = dx1_ref[...] + r * (d_n - n * jnp.mean(d_n * n, axis=-1, keepdims=True))

        @pl.when(i == 0)
        def _():
            small_ref[...] = jnp.zeros_like(small_ref)

        small_ref[0:1, :] += jnp.sum(dh * n, axis=0, keepdims=True)

        @pl.when(i == steps - 1)
        def _():
            for cp in _chip_copies(s_refs, l_refs, ssem, rsem):
                cp.wait()

    def row(width):
        return pl.BlockSpec((TM, width), lambda i: (i, 0))

    return pl.pallas_call(
        body, name="in_bwd", grid=(steps,),
        in_specs=[row(D_MODEL), pl.BlockSpec((1, D_MODEL), lambda i: (0, 0)), row(D_MODEL)]
        + [row(wd) for wd in widths] + [pl.BlockSpec(w.shape, lambda i: (0, 0))] + ride_in,
        out_specs=[row(D_MODEL), pl.BlockSpec((8, D_MODEL), lambda i: (0, 0))] + ride_out,
        out_shape=[jax.ShapeDtypeStruct((s, D_MODEL), F32), jax.ShapeDtypeStruct((8, D_MODEL), F32)]
        + ride_shape,
        scratch_shapes=ride_sems,
        compiler_params=_params(("arbitrary",), 48),
    )(x, g, dx1, *pieces, w, *sums)


def _tn_matmul(a, b, name, blocked=False):
    s, k = a.shape
    n = b.shape[1]
    ts = min(s, TS_DW)
    tn = n if blocked else min(n, 512)
    steps = s // ts

    def body(a_ref, b_ref, o_ref):
        t = pl.program_id(1)

        @pl.when(t == 0)
        def _():
            o_ref[...] = jnp.zeros_like(o_ref)

        prod = _dot_tn(a_ref[...], b_ref[...])
        if blocked:
            for j in range(n // LANES):
                o_ref[j] += prod[:, j * LANES:(j + 1) * LANES]
        else:
            o_ref[...] += prod

    if blocked:
        out_spec = pl.BlockSpec((n // LANES, k, LANES), lambda j, t: (0, 0, 0))
        out_shape = jax.ShapeDtypeStruct((n // LANES, k, LANES), F32)
    else:
        out_spec = pl.BlockSpec((k, tn), lambda j, t: (0, j))
        out_shape = jax.ShapeDtypeStruct((k, n), F32)
    return pl.pallas_call(
        body, name=name, grid=(n // tn, steps),
        in_specs=[pl.BlockSpec((ts, k), lambda j, t: (t, 0)), pl.BlockSpec((ts, tn), lambda j, t: (t, j))],
        out_specs=out_spec, out_shape=out_shape,
        compiler_params=_params(("parallel", "arbitrary"), 40),
    )(a, b)


def _tn_matmul_multi(a, bs, name):
    s, k = a.shape
    widths = [b.shape[1] for b in bs]
    ts = min(s, TS_DW)

    def body(a_ref, *refs):
        b_refs, o_ref = refs[:-1], refs[-1]
        t = pl.program_id(0)

        @pl.when(t == 0)
        def _():
            o_ref[...] = jnp.zeros_like(o_ref)

        av = a_ref[...]
        off = 0
        for b_ref, wd in zip(b_refs, widths):
            o_ref[:, off:off + wd] += _dot_tn(av, b_ref[...])
            off += wd

    return pl.pallas_call(
        body, name=name, grid=(s // ts,),
        in_specs=[pl.BlockSpec((ts, k), lambda t: (t, 0))] + [pl.BlockSpec((ts, wd), lambda t: (t, 0)) for wd in widths],
        out_specs=pl.BlockSpec((k, sum(widths)), lambda t: (0, 0)),
        out_shape=jax.ShapeDtypeStruct((k, sum(widths)), F32),
        compiler_params=_params(("arbitrary",), 40),
    )(a, *bs)


IN_SHARD = 372
_IN_KERNEL_ORDER = ((0, 2048), (2464, 2976), (2048, 2432))
_IN_ROPE = (2432, 2464)
_IN_GRAD_SRC = ((0, 512, 0, 0), (512, 1024, 0, 512), (1024, 1536, 1, 0), (1536, 2048, 1, 512),
                (2048, 2304, 2, 512), (2304, 2432, 2, 768), (2432, 2464, 2, 960), (2464, 2976, 2, 0))


def _shard_cols(gath_in, lo, hi):
    out = []
    while lo < hi:
        j, a = divmod(lo, IN_SHARD)
        b = min(IN_SHARD, a + hi - lo)
        out.append(gath_in[j][:, a:b])
        lo += b - a
    return out


def _kernel_w_in(g_in):
    zc = lambda n: jnp.zeros((D_MODEL, n), BF16)
    parts = [pc for lo, hi in _IN_KERNEL_ORDER for pc in _shard_cols(g_in, lo, hi)]
    parts += [zc(64)] + _shard_cols(g_in, *_IN_ROPE) + [zc(32)]
    return jnp.concatenate(parts, axis=1)


def _kernel_weights(gath):
    g_uq, g_ukv, g_out, g_ple, g_pg = gath
    w_uq_p = jnp.pad(g_uq, ((0, 0), (0, 0), (0, 32))).transpose(1, 0, 2).reshape(Q_LORA, 1024)
    k_only = jnp.where(jnp.arange(LANES) < HEAD_DIM, g_ukv, jnp.zeros_like(g_ukv))
    w_uk_p = k_only.transpose(1, 0, 2).reshape(KV_LORA, 1024)
    w_uv = g_ukv[:, :, HEAD_DIM:].transpose(1, 0, 2).reshape(KV_LORA, D_GRP)
    w_ple = g_ple.transpose(1, 0, 2).reshape(PLE_DIM, D_MODEL)
    return (w_uq_p, w_uk_p, w_uv, g_out.reshape(D_MODEL, D_MODEL), w_ple, g_pg.reshape(D_MODEL, D_MODEL))


def _payload_in(d_cols):
    blocks = []
    for j in range(N_DEV):
        lo, hi = j * IN_SHARD, (j + 1) * IN_SHARD
        parts = []
        for o_lo, o_hi, idx, off in _IN_GRAD_SRC:
            a, b = max(lo, o_lo), min(hi, o_hi)
            if a < b:
                parts.append(d_cols[idx][:, off + a - o_lo:off + b - o_lo])
        blocks.append(jnp.concatenate(parts, axis=1))
    return jnp.stack(blocks)


def _payload_ukv(duk_blk, d_uv):
    dv_blk = d_uv.reshape(KV_LORA, N_HEADS, HEAD_DIM).transpose(1, 0, 2)
    return jnp.concatenate([duk_blk[:, :, :HEAD_DIM], dv_blk], axis=2)


def _pair_sums(pays, landed, place, tag):
    return [_pair_sum(g, l, place, "grad_pair_sum_%s%d" % (tag, o)) for o, (g, l) in enumerate(zip(pays, landed))]


def kernel(x, p, positions, norm_pre_g, w_in, q_norm_g, w_uq, kv_norm_g, w_ukv, sb_out_norm_g, mla_out_norm_g, w_out, norm_post_g, w_ple, ple_norm_g, w_ple_gate, b_ple_gate, loss_target, m_norm_pre_g, m_w_in, m_q_norm_g, m_w_uq, m_kv_norm_g, m_w_ukv, m_sb_out_norm_g, m_mla_out_norm_g, m_w_out, m_norm_post_g, m_w_ple, m_ple_norm_g, m_w_ple_gate, m_b_ple_gate, v_norm_pre_g, v_w_in, v_q_norm_g, v_w_uq, v_kv_norm_g, v_w_ukv, v_sb_out_norm_g, v_mla_out_norm_g, v_w_out, v_norm_post_g, v_w_ple, v_ple_norm_g, v_w_ple_gate, v_b_ple_gate):
    mats = (w_in, w_uq, w_ukv, w_out, w_ple, w_ple_gate)
    m_mats = (m_w_in, m_w_uq, m_w_ukv, m_w_out, m_w_ple, m_w_ple_gate)
    v_mats = (v_w_in, v_w_uq, v_w_ukv, v_w_out, v_w_ple, v_w_ple_gate)
    vecs = (norm_pre_g, q_norm_g, kv_norm_g, sb_out_norm_g, mla_out_norm_g, norm_post_g, ple_norm_g, b_ple_gate)
    m_vecs = (m_norm_pre_g, m_q_norm_g, m_kv_norm_g, m_sb_out_norm_g, m_mla_out_norm_g, m_norm_post_g,
              m_ple_norm_g, m_b_ple_gate)
    v_vecs = (v_norm_pre_g, v_q_norm_g, v_kv_norm_g, v_sb_out_norm_g, v_mla_out_norm_g, v_norm_post_g,
              v_ple_norm_g, v_b_ple_gate)

    shards = [a[0].astype(BF16) for a in mats]
    w_in_p = _kernel_w_in(_all_gather(shards[:1])[0])
    grad_x, reduced, vec_slab = _step(x[0], p[0, 0], positions[0], loss_target[0], *vecs, w_in_p, shards[1:])
    upd = [_adamw_matrix(own, l2, w[0], m[0], v[0], "adamw_%d" % o)
           for o, ((own, l2), w, m, v) in enumerate(zip(reduced, mats, m_mats, v_mats))]
    sm = _adamw_vectors(_slab_exchange(vec_slab), vecs, m_vecs, v_vecs)

    outs = []
    for kind in range(4):
        mat = [upd[o][kind][None] for o in range(len(mats))]
        vec = sm[1 + 8 * kind:9 + 8 * kind]
        outs += [vec[0], mat[0], vec[1], mat[1], vec[2], mat[2], vec[3], vec[4], mat[3], vec[5],
                 mat[4], vec[6], mat[5], vec[7]]
    return (sm[0][0, 0], grad_x[None], *outs)


def _step(xs, ps, pos, tgt, norm_pre_g, q_norm_g, kv_norm_g, sb_out_norm_g, mla_out_norm_g,
          norm_post_g, ple_norm_g, b_ple_gate, w_in_p, shards):
    s = xs.shape[0]
    place = jnp.stack([lax.axis_index("c"), 2 * lax.axis_index("x") + lax.axis_index("y")]).astype(jnp.int32)

    half = ROPE_DIM // 2
    freq = ROPE_THETA ** (-jnp.arange(half, dtype=F32) / half)
    ang = pos.astype(F32)[:, None] * freq
    cos, sin = jnp.cos(ang), jnp.sin(ang)
    cos_t = jnp.concatenate([jnp.ones((s, 64), F32), cos, cos, jnp.zeros((s, 32), F32)], axis=1)
    sin_t = jnp.concatenate([jnp.zeros((s, 64), F32), -sin, sin, jnp.zeros((s, 32), F32)], axis=1)
    seg = jnp.arange(D_GRP) // HEAD_DIM
    bd = (seg[:, None] == seg[None, :]).astype(BF16)

    qkv, rest, h_b, *gath = _in_proj(xs, norm_pre_g, w_in_p, shards)
    w_uq_p, w_uk_p, w_uv, f_out, f_ple, f_pg = _kernel_weights(gath)
    sb_o = _sb_fwd(qkv, 8)
    qp, kp, vv, cqn_b, ckvn_b = _mla_prep(rest, q_norm_g, kv_norm_g, w_uq_p, w_uk_p, w_uv, cos_t, sin_t)
    mla_o, lse = _mla_fwd(qp, kp, vv, 4)

    (dx1, d_sbo, d_mlo, d_sbg, d_mlg, x1_b, dgl_b, yc_b, dy_b, p_b, du_b, small_mid) = _mid(
        xs, ps, tgt, sb_o, mla_o, rest, sb_out_norm_g, mla_out_norm_g, f_out, norm_post_g,
        f_ple, ple_norm_g, f_pg, b_ple_gate, bd)
    pay_a = [_tn_matmul(yc_b, dy_b, "dw_out").reshape(N_DEV, 128, D_MODEL),
             _tn_matmul(p_b, du_b, "dw_ple", blocked=True),
             _tn_matmul(x1_b, dgl_b, "dw_pg").reshape(N_DEV, 128, D_MODEL)]
    dqp, dkp, dvv, *sib_a = _mla_bwd(qp, kp, vv, d_mlo, mla_o, lse, 4, pay_a)
    pair_a = _pair_sums(pay_a, sib_a, place, "a")
    dq_sb, dk_sb, dv_sb, *landed_a = _sb_bwd(qkv, d_sbo, [sm for sm, _ in pair_a])
    dcq, dckv, dkr, dq_b, dk_b, dv_b, small_prep = _mla_prep_bwd(
        dqp, dkp, dvv, rest, q_norm_g, kv_norm_g, w_uq_p, w_uk_p, w_uv, cos_t, sin_t)
    pieces = [dq_sb, dk_sb, dv_sb, d_sbg, d_mlg, dcq, dckv, dkr]
    d_cols = [_tn_matmul_multi(h_b, pieces[0:2], "dw_in_0"), _tn_matmul_multi(h_b, pieces[2:4], "dw_in_1"),
              _tn_matmul_multi(h_b, pieces[4:8], "dw_in_2")]
    pay_b = [_payload_in(d_cols), _tn_matmul(cqn_b, dq_b, "dw_uq", blocked=True),
             _payload_ukv(_tn_matmul(ckvn_b, dk_b, "dw_uk", blocked=True), _tn_matmul(ckvn_b, dv_b, "dw_uv"))]
    pair_b = _pair_sums(pay_b, _pair_exchange(pay_b, "grad_pair_exchange"), place, "b")
    grad_x, small_in, *landed_b = _in_bwd(xs, norm_pre_g, dx1, pieces, w_in_p, [sm for sm, _ in pair_b])
    reduced = [(own, l2) for (_, own), l2 in zip(pair_b + pair_a, landed_b + landed_a)]
    slab = jnp.concatenate([small_in[0:1], jnp.pad(small_prep[0:2], ((0, 0), (0, D_MODEL - Q_LORA))),
                            small_mid[3:8]], axis=0)
    return grad_x, reduced, slab
```

```python
import jax
import jax.numpy as jnp
from jax import lax
from jax.experimental import pallas as pl
from jax.experimental.pallas import tpu as pltpu

F32 = jnp.float32
BF16 = jnp.bfloat16
MESH = pl.DeviceIdType.MESH

N_DEV = 8
D_MODEL = 1024
N_HEADS = 8
HEAD_DIM = 64
D_GRP = N_HEADS * HEAD_DIM
Q_LORA = 256
KV_LORA = 128
ROPE_DIM = 32
PLE_DIM = 256
CHUNK_SHIFT = 6
ROPE_THETA = 10000.0
EPS = 1e-6
SB_SCALE = HEAD_DIM ** -0.5
MLA_SCALE = (HEAD_DIM + ROPE_DIM) ** -0.5
NEG = -1e30
LOG2_E = 1.4426950408889634
LN_2 = 0.6931471805599453
SB_CUTOFF = 110.0

ADAM_LR = 0.001
ADAM_B1 = 0.9
ADAM_B2 = 0.999
ADAM_EPS = 1e-08
ADAM_WD = 0.01
ADAM_STEP = 10

LANES = 128
TQ = 256
TK = 256
TM = 256
TS_DW = 2048

D_IN_P = 3072

_NT = (((1,), (1,)), ((), ()))
_TN = (((0,), (0,)), ((), ()))


VMEM_RESERVE = 60 << 20


def _params(sem):
    return pltpu.CompilerParams(dimension_semantics=sem, vmem_limit_bytes=VMEM_RESERVE)


def _dot(a, b):
    return jnp.dot(a, b, preferred_element_type=F32)


def _dot_nt(a, b):
    return lax.dot_general(a, b, _NT, preferred_element_type=F32)


def _dot_tn(a, b):
    return lax.dot_general(a, b, _TN, preferred_element_type=F32)


def _hl_dot(a, b):
    hi = a.astype(BF16)
    lo = (a - hi.astype(F32)).astype(BF16)
    return _dot(hi, b) + _dot(lo, b)


def _sigmoid(x):
    return 1.0 / (1.0 + jnp.exp(-x))


def _rope_swap(x, lane):
    left = pltpu.roll(x, LANES - 16, axis=1)
    right = pltpu.roll(x, 16, axis=1)
    lo = (lane >= 64) & (lane < 80)
    hi = (lane >= 80) & (lane < 96)
    return jnp.where(lo, left, jnp.where(hi, right, 0.0))


def _two_level_gather(x_refs, out_refs, send_sems, recv_sems, local_sems):
    x, y, c = lax.axis_index("x"), lax.axis_index("y"), lax.axis_index("c")
    me, sibling = (x, y, c), (x, y, 1 - c)
    chips = [(1 - x, y), (x, 1 - y), (1 - x, 1 - y)]
    ops = range(len(x_refs))

    def slot(o, px, py, pc):
        return out_refs[o].at[4 * px + 2 * py + pc]

    def copy(o, k, block, to, src=None):
        return pltpu.make_async_remote_copy(
            src_ref=slot(o, *block) if src is None else src, dst_ref=slot(o, *block),
            send_sem=send_sems.at[o, k], recv_sem=recv_sems.at[o, k],
            device_id=to, device_id_type=MESH)

    def mine():
        return [pltpu.make_async_copy(x_refs[o], slot(o, *me), local_sems.at[o]) for o in ops]

    def first():
        return ([copy(o, 0, me, sibling, src=x_refs[o]) for o in ops]
                + [copy(o, 1 + j, me, (*chip, c), src=x_refs[o]) for j, chip in enumerate(chips) for o in ops])

    def start():
        for cp in mine() + first():
            cp.start()

    def finish():
        passed = []
        for j, chip in enumerate(chips):
            for o in ops:
                copy(o, 1 + j, (*chip, c), me).wait_recv()
                passed.append(copy(o, 4 + j, (*chip, c), sibling))
                passed[-1].start()
        for o in ops:
            copy(o, 0, sibling, me).wait_recv()
        for j, chip in enumerate(chips):
            for o in ops:
                copy(o, 4 + j, (*chip, 1 - c), me).wait_recv()
        for cp in first() + passed:
            cp.wait_send()
        for cp in mine():
            cp.wait()

    return start, finish


def _gather_sems(n_op):
    return [pltpu.SemaphoreType.DMA((n_op, 7)), pltpu.SemaphoreType.DMA((n_op, 7)),
            pltpu.SemaphoreType.DMA((n_op,))]


def _all_gather(shards):
    n_op = len(shards)

    def body(*refs):
        start, finish = _two_level_gather(refs[:n_op], refs[n_op:2 * n_op], *refs[2 * n_op:])
        start()
        finish()

    any_spec = pl.BlockSpec(memory_space=pl.ANY)
    return pl.pallas_call(
        body, name="weight_all_gather",
        out_shape=[jax.ShapeDtypeStruct((N_DEV,) + a.shape, a.dtype) for a in shards],
        in_specs=[any_spec] * n_op, out_specs=[any_spec] * n_op, scratch_shapes=_gather_sems(n_op),
        compiler_params=pltpu.CompilerParams(vmem_limit_bytes=VMEM_RESERVE),
    )(*shards)


def _pair_copies(g_refs, l_refs, ssem, rsem):
    x, y, c = lax.axis_index("x"), lax.axis_index("y"), lax.axis_index("c")
    copies = []
    for o in range(len(g_refs)):
        for chip in range(4):
            copies.append(pltpu.make_async_remote_copy(
                src_ref=g_refs[o].at[2 * chip + (1 - c)], dst_ref=l_refs[o].at[chip],
                send_sem=ssem.at[o, chip], recv_sem=rsem.at[o, chip],
                device_id=(x, y, 1 - c), device_id_type=MESH))
    return copies


def _pair_specs(pays):
    n_op = len(pays)
    any_spec = pl.BlockSpec(memory_space=pl.ANY)
    return ([any_spec] * n_op, [any_spec] * n_op,
            [jax.ShapeDtypeStruct((4,) + a.shape[1:], F32) for a in pays],
            [pltpu.SemaphoreType.DMA((n_op, 4)), pltpu.SemaphoreType.DMA((n_op, 4))])


def _pair_exchange(pays, name):
    n_op = len(pays)
    in_specs, out_specs, out_shape, sems = _pair_specs(pays)

    def body(*refs):
        copies = _pair_copies(refs[:n_op], refs[n_op:2 * n_op], *refs[2 * n_op:])
        for cp in copies:
            cp.start()
        for cp in copies:
            cp.wait()

    return pl.pallas_call(body, name=name, out_shape=out_shape, in_specs=in_specs, out_specs=out_specs,
                          scratch_shapes=sems,
                          compiler_params=pltpu.CompilerParams(vmem_limit_bytes=VMEM_RESERVE))(*pays)


def _slab_exchange(small):
    sr, n = small.shape

    def body(s_ref, sland_ref, ssem, rsem, lsem):
        x, y, c = lax.axis_index("x"), lax.axis_index("y"), lax.axis_index("c")
        me = 4 * x + 2 * y + c
        copies = []
        for k in range(1, N_DEV):
            peer = (1 - x if (k >> 2) & 1 else x, 1 - y if (k >> 1) & 1 else y, 1 - c if k & 1 else c)
            copies.append(pltpu.make_async_remote_copy(
                src_ref=s_ref, dst_ref=sland_ref.at[me], send_sem=ssem.at[k], recv_sem=rsem.at[k],
                device_id=peer, device_id_type=MESH))
        own = pltpu.make_async_copy(s_ref, sland_ref.at[me], lsem)
        own.start()
        for cp in copies:
            cp.start()
        for cp in copies:
            cp.wait()
        own.wait()

    any_spec = pl.BlockSpec(memory_space=pl.ANY)
    return pl.pallas_call(
        body, name="grad_slab_exchange", out_shape=jax.ShapeDtypeStruct((N_DEV, sr, n), F32),
        in_specs=[any_spec], out_specs=any_spec,
        scratch_shapes=[pltpu.SemaphoreType.DMA((N_DEV,)), pltpu.SemaphoreType.DMA((N_DEV,)),
                        pltpu.SemaphoreType.DMA],
        compiler_params=pltpu.CompilerParams(vmem_limit_bytes=VMEM_RESERVE),
    )(small)


def _pair_sum(pay, landed, place, name):
    _, r, c = pay.shape

    def body(place_ref, g_ref, l_ref, s_ref, own_ref):
        i = pl.program_id(0)
        tot = g_ref[...] + l_ref[...]
        s_ref[...] = tot.astype(BF16)

        @pl.when(i == place_ref[1])
        def _():
            own_ref[...] = tot

    grid_spec = pltpu.PrefetchScalarGridSpec(
        num_scalar_prefetch=1, grid=(4,),
        in_specs=[pl.BlockSpec((None, r, c), lambda i, pr: (2 * i + pr[0], 0, 0)),
                  pl.BlockSpec((None, r, c), lambda i, pr: (i, 0, 0))],
        out_specs=[pl.BlockSpec((None, r, c), lambda i, pr: (i, 0, 0)),
                   pl.BlockSpec((r, c), lambda i, pr: (0, 0))])
    return pl.pallas_call(
        body, name=name, grid_spec=grid_spec,
        out_shape=[jax.ShapeDtypeStruct((4, r, c), BF16), jax.ShapeDtypeStruct((r, c), F32)],
        compiler_params=_params(("arbitrary",)),
    )(place, pay, landed)


def _chip_copies(s_refs, l_refs, ssem, rsem):
    x, y, c = lax.axis_index("x"), lax.axis_index("y"), lax.axis_index("c")
    copies = []
    for rel in range(1, 4):
        px = 1 - x if rel & 2 else x
        py = 1 - y if rel & 1 else y
        for o in range(len(s_refs)):
            copies.append(pltpu.make_async_remote_copy(
                src_ref=s_refs[o].at[2 * px + py], dst_ref=l_refs[o].at[rel - 1],
                send_sem=ssem.at[o, rel - 1], recv_sem=rsem.at[o, rel - 1],
                device_id=(px, py, c), device_id_type=MESH))
    return copies


def _chip_specs(sums):
    n_op = len(sums)
    any_spec = pl.BlockSpec(memory_space=pl.ANY)
    return ([any_spec] * n_op, [any_spec] * n_op,
            [jax.ShapeDtypeStruct((3,) + a.shape[1:], BF16) for a in sums],
            [pltpu.SemaphoreType.DMA((n_op, 3)), pltpu.SemaphoreType.DMA((n_op, 3))])


def _adamw_math(g, w, m, v):
    mn = ADAM_B1 * m + (1.0 - ADAM_B1) * g
    vn = ADAM_B2 * v + (1.0 - ADAM_B2) * (g * g)
    m_hat = mn / (1.0 - ADAM_B1 ** ADAM_STEP)
    v_hat = vn / (1.0 - ADAM_B2 ** ADAM_STEP)
    return -ADAM_LR * (m_hat / (jnp.sqrt(v_hat) + ADAM_EPS) + ADAM_WD * w), mn, vn


def _adamw_matrix(own, landed, w, m, v, name):
    _, r, c = w.shape
    cp = own.shape[1]
    br = min(r, 256)

    def body(own_ref, l_ref, w_ref, m_ref, v_ref, g_out, d_out, m_out, v_out):
        g = own_ref[...]
        for k in range(3):
            g = g + l_ref[k].astype(F32)
        g = g[:, :c]
        g_out[...] = g
        d_out[...], m_out[...], v_out[...] = _adamw_math(g, w_ref[...], m_ref[...], v_ref[...])

    row = pl.BlockSpec((None, br, c), lambda i: (0, i, 0))
    shp = jax.ShapeDtypeStruct((1, r, c), F32)
    return pl.pallas_call(
        body, name=name, grid=(r // br,),
        in_specs=[pl.BlockSpec((br, cp), lambda i: (i, 0)), pl.BlockSpec((3, br, cp), lambda i: (0, i, 0)),
                  row, row, row],
        out_specs=(row, row, row, row), out_shape=(shp, shp, shp, shp),
        compiler_params=_params(("parallel",)),
    )(own, landed, w, m, v)


_VEC_PLACE = ((0, 0), (1, 0), (2, 0), (3, 0), (3, D_GRP), (4, 0), (5, 0), (6, 0))


def _adamw_vectors(sland, ws, ms, vs):
    nv = len(ws)

    def body(l_ref, *refs):
        w_refs, m_refs, v_refs = refs[:nv], refs[nv:2 * nv], refs[2 * nv:3 * nv]
        loss_ref = refs[3 * nv]
        outs = refs[3 * nv + 1:]
        g_all = l_ref[0]
        for j in range(1, N_DEV):
            g_all = g_all + l_ref[j]
        loss_ref[...] = jnp.sum(g_all[7:8, :], axis=1, keepdims=True)
        for k, (row, lane0) in enumerate(_VEC_PLACE):
            n = w_refs[k].shape[1]
            g = g_all[row:row + 1, lane0:lane0 + n]
            d, mn, vn = _adamw_math(g, w_refs[k][...], m_refs[k][...], v_refs[k][...])
            outs[k][...] = g
            outs[nv + k][...] = d
            outs[2 * nv + k][...] = mn
            outs[3 * nv + k][...] = vn

    def whole(shape):
        return pl.BlockSpec(shape, lambda i: (0,) * len(shape))

    shapes = [jax.ShapeDtypeStruct(w.shape, F32) for w in ws]
    return pl.pallas_call(
        body, name="adamw_vectors", grid=(1,),
        in_specs=[whole(sland.shape)] + [whole(w.shape) for w in ws] * 3,
        out_specs=[whole((1, 1))] + [whole(w.shape) for w in ws] * 4,
        out_shape=[jax.ShapeDtypeStruct((1, 1), F32)] + shapes * 4,
        compiler_params=_params(("arbitrary",)),
    )(sland, *ws, *ms, *vs)


def _in_proj(x, g, w, shards):
    s = x.shape[0]
    n_op = len(shards)
    steps = s // TM

    def body(x_ref, g_ref, w_ref, *refs):
        shard_refs = refs[:n_op]
        qkv_ref, rest_ref, h_ref = refs[n_op:n_op + 3]
        gath_refs = refs[n_op + 3:2 * n_op + 3]
        start, finish = _two_level_gather(shard_refs, gath_refs, *refs[2 * n_op + 3:])
        i = pl.program_id(0)

        @pl.when(i == 0)
        def _():
            start()

        xv = x_ref[...]
        r = lax.rsqrt(jnp.mean(xv * xv, axis=-1, keepdims=True) + EPS)
        h = ((xv * r) * g_ref[...]).astype(BF16)
        h_ref[...] = h
        qkv_ref[...] = _dot(h, w_ref[:, :1536]).astype(BF16)
        rest_ref[...] = _dot(h, w_ref[:, 1536:])

        @pl.when(i == steps - 1)
        def _():
            finish()

    any_spec = pl.BlockSpec(memory_space=pl.ANY)
    return pl.pallas_call(
        body, name="in_proj", grid=(steps,),
        in_specs=[pl.BlockSpec((TM, D_MODEL), lambda i: (i, 0)),
                  pl.BlockSpec((1, D_MODEL), lambda i: (0, 0)),
                  pl.BlockSpec((D_MODEL, D_IN_P), lambda i: (0, 0))] + [any_spec] * n_op,
        out_specs=[pl.BlockSpec((TM, 1536), lambda i: (i, 0)),
                   pl.BlockSpec((TM, 1536), lambda i: (i, 0)),
                   pl.BlockSpec((TM, D_MODEL), lambda i: (i, 0))] + [any_spec] * n_op,
        out_shape=[jax.ShapeDtypeStruct((s, 1536), BF16), jax.ShapeDtypeStruct((s, 1536), F32),
                   jax.ShapeDtypeStruct((s, D_MODEL), BF16)]
        + [jax.ShapeDtypeStruct((N_DEV,) + a.shape, a.dtype) for a in shards],
        scratch_shapes=_gather_sems(n_op),
        compiler_params=_params(("arbitrary",)),
    )(x, g, w, *shards)


def _mla_prep(rest, gq, gkv, wuq, wuk, wuv, cos_t, sin_t):
    s = rest.shape[0]

    def body(cq_ref, ckv_ref, kr_ref, gq_ref, gkv_ref, wuq_ref, wuk_ref, wuv_ref, c_ref, s_ref,
             qp_ref, kp_ref, vv_ref, cqn_ref, ckvn_ref):
        lane = lax.broadcasted_iota(jnp.int32, (1, LANES), 1)
        cos_v, sin_v = c_ref[...], s_ref[...]
        cq = cq_ref[...]
        rq = lax.rsqrt(jnp.mean(cq * cq, axis=-1, keepdims=True) + EPS)
        cqn = ((cq * rq) * gq_ref[...]).astype(BF16)
        cqn_ref[...] = cqn
        q = _dot(cqn, wuq_ref[...])
        ckv = ckv_ref[...]
        rkv = lax.rsqrt(jnp.mean(ckv * ckv, axis=-1, keepdims=True) + EPS)
        ckvn = ((ckv * rkv) * gkv_ref[...]).astype(BF16)
        ckvn_ref[...] = ckvn
        kn = _dot(ckvn, wuk_ref[...])
        vv_ref[...] = _dot(ckvn, wuv_ref[...]).astype(BF16)
        kr = kr_ref[...]
        kr_roped = kr * cos_v + _rope_swap(kr, lane) * sin_v
        for h in range(N_HEADS):
            sl = slice(h * LANES, (h + 1) * LANES)
            qh = q[:, sl]
            qp_ref[:, sl] = (qh * cos_v + _rope_swap(qh, lane) * sin_v).astype(BF16)
            kp_ref[:, sl] = (kn[:, sl] + kr_roped).astype(BF16)

    def row(width, idx):
        return pl.BlockSpec((TM, width), lambda i: (i, idx))

    def full(a):
        return pl.BlockSpec(a.shape, lambda i: (0, 0))

    return pl.pallas_call(
        body, name="mla_prep", grid=(s // TM,),
        in_specs=[row(Q_LORA, 4), row(KV_LORA, 10), row(LANES, 11), full(gq), full(gkv),
                  full(wuq), full(wuk), full(wuv), row(LANES, 0), row(LANES, 0)],
        out_specs=(row(1024, 0), row(1024, 0), row(D_GRP, 0), row(Q_LORA, 0), row(KV_LORA, 0)),
        out_shape=(jax.ShapeDtypeStruct((s, 1024), BF16), jax.ShapeDtypeStruct((s, 1024), BF16),
                   jax.ShapeDtypeStruct((s, D_GRP), BF16), jax.ShapeDtypeStruct((s, Q_LORA), BF16),
                   jax.ShapeDtypeStruct((s, KV_LORA), BF16)),
        compiler_params=_params(("parallel",)),
    )(rest, rest, rest, gq, gkv, wuq, wuk, wuv, cos_t, sin_t)


def _sb_live(n, qi, carries):
    top = carries[0]
    for c in carries[1:]:
        top = jnp.maximum(top, c)
    return jnp.logical_and(n < qi, jnp.max(top) > -SB_CUTOFF)


def _sb_fwd(qkv, hb):
    s = qkv.shape[0]

    def body(q_ref, k_ref, v_ref, o_ref, acc):
        qi = pl.program_id(1)
        lane = lax.broadcasted_iota(jnp.int32, (1, LANES), 1)
        is_a = lane < HEAD_DIM
        pair = lambda h: slice((h // 2) * LANES, (h // 2 + 1) * LANES)
        q_h = []
        for h in range(hb):
            qs = q_ref[:, pair(h)] * SB_SCALE
            mine = is_a if h % 2 == 0 else jnp.logical_not(is_a)
            q_h.append(jnp.where(mine, qs, jnp.zeros_like(qs)))
        r_i = lax.broadcasted_iota(jnp.int32, (TQ, TK), 0)
        c_i = lax.broadcasted_iota(jnp.int32, (TQ, TK), 1)
        past = c_i < r_i
        upper = (r_i > c_i).astype(BF16)
        acc[...] = jnp.zeros_like(acc)

        def tile(j, carries, diag):
            ks = pl.ds(pl.multiple_of(j * TK, TK), TK)
            zs = [_dot_nt(q_h[h], k_ref[ks, pair(h)]) for h in range(hb)]
            if diag:
                zs = [jnp.where(past, z, NEG) for z in zs]
            lfs = [-(jnp.maximum(z, 0.0) + jnp.log(1.0 + jnp.exp(-jnp.abs(z)))) for z in zs]
            sufs = [_hl_dot(lfs[h], upper) for h in range(hb)]
            out = []
            for h in range(hb):
                w = jnp.exp(zs[h] + lfs[h] + (sufs[h] + carries[h]))
                acc[h] += _dot(w.astype(BF16), v_ref[ks, pair(h)])
                out.append(carries[h] + jnp.sum(lfs[h], axis=1, keepdims=True))
            return tuple(out)

        zero = jnp.zeros((TQ, 1), F32)
        carries = tile(qi, (zero,) * hb, True)

        def step(st):
            return (st[0] + 1,) + tile(qi - 1 - st[0], st[1:], False)

        lax.while_loop(lambda st: _sb_live(st[0], qi, st[1:]), step, (0,) + carries)
        for pr in range(hb // 2):
            o_ref[:, pr * LANES:(pr + 1) * LANES] = jnp.where(is_a, acc[2 * pr], acc[2 * pr + 1])

    width = hb * HEAD_DIM
    nb = D_GRP // width
    slab = lambda part: pl.BlockSpec((s, width), lambda g, qi: (0, part * nb + g))
    blk = pl.BlockSpec((TQ, width), lambda g, qi: (qi, g))
    return pl.pallas_call(
        body, name="sb_fwd", grid=(nb, s // TQ),
        in_specs=[blk, slab(1), slab(2)], out_specs=blk,
        out_shape=jax.ShapeDtypeStruct((s, D_GRP), F32),
        scratch_shapes=[pltpu.VMEM((hb, TQ, LANES), F32)],
        compiler_params=_params(("arbitrary", "arbitrary")),
    )(qkv, qkv, qkv)


def _sb_bwd(qkv, d_o, sums):
    s = qkv.shape[0]
    nq = s // TQ
    nk = s // TK
    n_op = len(sums)
    ride_in, ride_out, ride_shape, ride_sems = _chip_specs(sums)

    def body(q_ref, k_ref, v_ref, do_ref, *refs):
        s_refs = refs[:n_op]
        dq_ref, dk_ref, dv_ref = refs[n_op:n_op + 3]
        l_refs = refs[n_op + 3:2 * n_op + 3]
        x1s, bts, dqacc, dkacc, dvacc, ssem, rsem = refs[2 * n_op + 3:]
        qi = pl.program_id(1)
        first_step = jnp.logical_and(pl.program_id(0) == 0, qi == 0)
        last_step = jnp.logical_and(pl.program_id(0) == pl.num_programs(0) - 1, qi == nq - 1)

        @pl.when(first_step)
        def _():
            for cp in _chip_copies(s_refs, l_refs, ssem, rsem):
                cp.start()

        lane = lax.broadcasted_iota(jnp.int32, (1, LANES), 1)
        is_a = lane < HEAD_DIM

        @pl.when(qi == 0)
        def _():
            dkacc[...] = jnp.zeros_like(dkacc)
            dvacc[...] = jnp.zeros_like(dvacc)

        qs = q_ref[...] * SB_SCALE
        zq = jnp.zeros_like(qs)
        qs_x = (jnp.where(is_a, qs, zq), jnp.where(is_a, zq, qs))
        dob = do_ref[...].astype(BF16)
        do_x = (jnp.where(is_a, dob, zq), jnp.where(is_a, zq, dob))
        r_i = lax.broadcasted_iota(jnp.int32, (TQ, TK), 0)
        c_i = lax.broadcasted_iota(jnp.int32, (TQ, TK), 1)
        past = c_i < r_i
        upper = (r_i > c_i).astype(BF16)
        upper_incl = (r_i >= c_i).astype(BF16)
        dqacc[...] = jnp.zeros_like(dqacc)
        both = ((0, 0), (0, 1), (1, 0), (1, 1))

        def tiles(n):
            j_hi = qi - 2 * n
            lo_ok = j_hi >= 1
            j_lo = jnp.maximum(j_hi - 1, 0)
            ks = (pl.ds(pl.multiple_of(j_hi * TK, TK), TK), pl.ds(pl.multiple_of(j_lo * TK, TK), TK))
            return j_hi, lo_ok, j_lo, ks

        def sweep(n, carries):
            j_hi, lo_ok, j_lo, ks = tiles(n)
            slot = (j_hi, jnp.where(lo_ok, j_lo, nk))
            valid = (jnp.logical_or(past, j_hi < qi), lo_ok)
            z = {th: jnp.where(valid[th[0]], _dot_nt(qs_x[th[1]], k_ref[ks[th[0]], :]), NEG) for th in both}
            log_b, lf_sum, suf = {}, {}, {}
            for th in both:
                lf = -(jnp.maximum(z[th], 0.0) + jnp.log(1.0 + jnp.exp(-jnp.abs(z[th]))))
                log_b[th] = z[th] + lf
                lf_sum[th] = jnp.sum(lf, axis=1, keepdims=True)
                suf[th] = _hl_dot(lf, upper)
            c, g_in = {}, {}
            for h in range(2):
                c[0, h], g_in[0, h] = carries[2 * h], carries[2 * h + 1]
                c[1, h] = c[0, h] + lf_sum[0, h]
            d_a = {th: _dot_nt(do_x[th[1]], v_ref[ks[th[0]], :]) for th in both}
            a_b, g, g_sum, sg = {}, {}, {}, {}
            for th in both:
                a = jnp.exp(log_b[th] + (suf[th] + c[th]))
                a_b[th] = a.astype(BF16)
                g[th] = a * d_a[th]
                g_sum[th] = jnp.sum(g[th], axis=1, keepdims=True)
                sg[th] = _hl_dot(g[th], upper_incl)
            for h in range(2):
                g_in[1, h] = g_in[0, h] + g_sum[0, h]
            for th in both:
                t, h = th
                beta = jnp.exp(log_b[th])
                x1s[slot[t], h] = g[th] * (1.0 - beta) + beta * (sg[th] + g_in[th])
                bts[slot[t], h] = beta
                dvacc[ks[t], :] += _dot_tn(a_b[th], do_x[h])
            out = []
            for h in range(2):
                out.append(c[1, h] + lf_sum[1, h])
                out.append(g_in[1, h] + g_sum[1, h])
            return tuple(out)

        zero = jnp.zeros((TQ, 1), F32)
        first = sweep(0, (zero, zero, zero, zero))

        def more(st):
            return jnp.logical_and(2 * st[0] <= qi, jnp.max(jnp.maximum(st[1], st[3])) > -SB_CUTOFF)

        swept = lax.while_loop(more, lambda st: (st[0] + 1,) + sweep(st[0], st[1:]), (1,) + first)
        g_tot = (swept[2], swept[4])

        def apply(n, carry):
            j_hi, lo_ok, j_lo, ks = tiles(n)

            def one(j, kslice):
                for h in range(2):
                    dz = (x1s[j, h] - bts[j, h] * g_tot[h]).astype(BF16)
                    dqacc[h] += _dot(dz, k_ref[kslice, :])
                    dkacc[kslice, :] += _dot_tn(dz, qs_x[h])

            one(j_hi, ks[0])

            @pl.when(lo_ok)
            def _():
                one(j_lo, ks[1])

            return carry

        lax.fori_loop(0, swept[0], apply, 0)
        dq_ref[...] = (jnp.where(is_a, dqacc[0], dqacc[1]) * SB_SCALE).astype(BF16)

        @pl.when(qi == nq - 1)
        def _():
            dk_ref[...] = dkacc[...].astype(BF16)
            dv_ref[...] = dvacc[...].astype(BF16)

        @pl.when(last_step)
        def _():
            for cp in _chip_copies(s_refs, l_refs, ssem, rsem):
                cp.wait()

    slab = lambda off: pl.BlockSpec((s, LANES), lambda p, qi: (0, off + p))
    blk = pl.BlockSpec((TQ, LANES), lambda p, qi: (qi, p))
    out_slab = pl.BlockSpec((s, LANES), lambda p, qi: (0, p))
    shp = jax.ShapeDtypeStruct((s, D_GRP), BF16)
    return pl.pallas_call(
        body, name="sb_bwd", grid=(4, nq),
        in_specs=[blk, slab(4), slab(8), blk] + ride_in,
        out_specs=[blk, out_slab, out_slab] + ride_out, out_shape=[shp, shp, shp] + ride_shape,
        scratch_shapes=[pltpu.VMEM((nk + 1, 2, TQ, TK), F32)] * 2
        + [pltpu.VMEM((2, TQ, LANES), F32), pltpu.VMEM((s, LANES), F32), pltpu.VMEM((s, LANES), F32)]
        + ride_sems,
        compiler_params=_params(("arbitrary", "arbitrary")),
    )(qkv, qkv, qkv, d_o, *sums)


def _mla_fwd(qp, kp, vv, hb):
    s = qp.shape[0]
    c2 = MLA_SCALE * LOG2_E

    def body(q_ref, k_ref, v_ref, o_ref, lse_ref, vaug, mrun, mb, acc, zbuf):
        qi = pl.program_id(1)
        lane = lax.broadcasted_iota(jnp.int32, (1, LANES), 1)
        is_a = lane < HEAD_DIM

        @pl.when(qi == 0)
        def _():
            for h in range(hb):
                vp = v_ref[:, (h // 2) * LANES:(h // 2 + 1) * LANES]
                mine = is_a if h % 2 == 0 else jnp.logical_not(is_a)
                vaug[h] = jnp.where(mine, vp, jnp.ones_like(vp))

        r_i = lax.broadcasted_iota(jnp.int32, (TQ, TK), 0)
        c_i = lax.broadcasted_iota(jnp.int32, (TQ, TK), 1)
        visible = (c_i >> CHUNK_SHIFT) <= (r_i >> CHUNK_SHIFT)

        def key_rows(j):
            return pl.ds(pl.multiple_of(j * TK, TK), TK)

        def sweep(tiles):
            def loop(n, carry):
                tiles(((2 * n, False), (2 * n + 1, False)))
                return carry

            lax.fori_loop(0, qi // 2, loop, 0)

            @pl.when(qi % 2 == 1)
            def _():
                tiles(((qi - 1, False), (qi, True)))

            @pl.when(qi % 2 == 0)
            def _():
                tiles(((qi, True),))

        mrun[...] = jnp.full_like(mrun, NEG)

        def tiles_max(js):
            zs = [[_dot_nt(q_ref[:, h * LANES:(h + 1) * LANES], k_ref[key_rows(j), h * LANES:(h + 1) * LANES])
                   for h in range(hb)] for j, _ in js]
            for t, (j, diag) in enumerate(js):
                for h in range(hb):
                    z = jnp.where(visible, zs[t][h], NEG) if diag else zs[t][h]
                    zbuf[j, h] = z
                    mrun[h] = jnp.maximum(mrun[h], z)

        sweep(tiles_max)
        for h in range(hb):
            m = jnp.max(mrun[h], axis=1, keepdims=True) * c2
            mb[h] = jnp.broadcast_to(m, (TQ, TK))
        acc[...] = jnp.zeros_like(acc)

        def tiles_pv(js):
            ps = [[jnp.exp2((zbuf[j, h] * c2 - mb[h]).astype(BF16)) for h in range(hb)] for j, _ in js]
            for t, (j, _) in enumerate(js):
                for h in range(hb):
                    acc[h] += _dot(ps[t][h], vaug[h, key_rows(j), :])

        sweep(tiles_pv)
        for pr in range(hb // 2):
            a, b = 2 * pr, 2 * pr + 1
            psl = slice(pr * LANES, (pr + 1) * LANES)
            acc_a, acc_b = acc[a], acc[b]
            l_a = pltpu.roll(acc_a, HEAD_DIM, axis=1)
            l_b = pltpu.roll(acc_b, HEAD_DIM, axis=1)
            o_ref[:, psl] = jnp.where(is_a, acc_a * (1.0 / l_a), acc_b * (1.0 / l_b))
            lse_ref[:, psl] = jnp.where(is_a, mb[a, :, :LANES] * LN_2 + jnp.log(l_a),
                                        mb[b, :, :LANES] * LN_2 + jnp.log(l_b))

    blk = pl.BlockSpec((TQ, hb * HEAD_DIM), lambda g, qi: (qi, g))
    shp = jax.ShapeDtypeStruct((s, D_GRP), F32)
    return pl.pallas_call(
        body, name="mla_fwd", grid=(N_HEADS // hb, s // TQ),
        in_specs=[pl.BlockSpec((TQ, hb * LANES), lambda g, qi: (qi, g)),
                  pl.BlockSpec((s, hb * LANES), lambda g, qi: (0, g)),
                  pl.BlockSpec((s, hb * HEAD_DIM), lambda g, qi: (0, g))],
        out_specs=(blk, blk), out_shape=(shp, shp),
        scratch_shapes=[pltpu.VMEM((hb, s, LANES), BF16), pltpu.VMEM((hb, TQ, TK), F32),
                        pltpu.VMEM((hb, TQ, TK), F32), pltpu.VMEM((hb, TQ, LANES), F32),
                        pltpu.VMEM((s // TK, hb, TQ, TK), F32)],
        compiler_params=_params(("arbitrary", "arbitrary")),
    )(qp, kp, vv)


def _mla_bwd(qp, kp, vv, d_o, o, lse, hb, pays):
    s = qp.shape[0]
    nq = s // TQ
    c2 = MLA_SCALE * LOG2_E
    n_op = len(pays)
    ride_in, ride_out, ride_shape, ride_sems = _pair_specs(pays)

    def body(q_ref, k_ref, v_ref, do_ref, o_ref, lse_ref, *refs):
        g_refs = refs[:n_op]
        dq_ref, dk_ref, dv_ref = refs[n_op:n_op + 3]
        l_refs = refs[n_op + 3:2 * n_op + 3]
        dqacc, lse_b, delta_b, ssem, rsem = refs[2 * n_op + 3:]
        qi = pl.program_id(1)

        @pl.when(jnp.logical_and(pl.program_id(0) == 0, qi == 0))
        def _():
            for cp in _pair_copies(g_refs, l_refs, ssem, rsem):
                cp.start()

        lane = lax.broadcasted_iota(jnp.int32, (1, LANES), 1)
        is_a = lane < HEAD_DIM

        @pl.when(qi == 0)
        def _():
            dk_ref[...] = jnp.zeros_like(dk_ref)
            dv_ref[...] = jnp.zeros_like(dv_ref)

        r_i = lax.broadcasted_iota(jnp.int32, (TQ, TK), 0)
        c_i = lax.broadcasted_iota(jnp.int32, (TQ, TK), 1)
        visible = (c_i >> CHUNK_SHIFT) <= (r_i >> CHUNK_SHIFT)
        do_x = []
        for h in range(hb):
            psl = slice((h // 2) * LANES, (h // 2 + 1) * LANES)
            mine = is_a if h % 2 == 0 else jnp.logical_not(is_a)
            d_o = do_ref[:, psl]
            delta = jnp.sum(jnp.where(mine, d_o * o_ref[:, psl], 0.0), axis=1, keepdims=True)
            lse_h = jnp.sum(jnp.where(lane == (h % 2) * HEAD_DIM, lse_ref[:, psl], 0.0), axis=1, keepdims=True)
            lse_b[h] = jnp.broadcast_to(lse_h * LOG2_E, (TQ, TK))
            delta_b[h] = jnp.broadcast_to(delta, (TQ, TK))
            do_x.append(jnp.where(mine, d_o, 0.0).astype(BF16))
        dqacc[...] = jnp.zeros_like(dqacc)

        head = lambda h: slice(h * LANES, (h + 1) * LANES)
        pair = lambda h: slice((h // 2) * LANES, (h // 2 + 1) * LANES)

        def tiles(js):
            th = [(j, diag, pl.ds(pl.multiple_of(j * TK, TK), TK), h) for j, diag in js for h in range(hb)]
            zs = [_dot_nt(q_ref[:, head(h)], k_ref[ks, head(h)]) for _, _, ks, h in th]
            dps = [_dot_nt(do_x[h], v_ref[ks, pair(h)]) for _, _, ks, h in th]
            for i, (j, diag, ks, h) in enumerate(th):
                e = zs[i] * c2 - lse_b[h]
                if diag:
                    e = jnp.where(visible, e, NEG)
                p = jnp.exp2(e)
                ds = (p * (dps[i] - delta_b[h]) * MLA_SCALE).astype(BF16)
                dqacc[h] += _dot(ds, k_ref[ks, head(h)])
                dk_ref[ks, head(h)] += _dot_tn(ds, q_ref[:, head(h)])
                dv_ref[ks, pair(h)] += _dot_tn(p.astype(BF16), do_x[h])

        def loop(n, c):
            tiles(((2 * n, False), (2 * n + 1, False)))
            return c

        lax.fori_loop(0, qi // 2, loop, 0)

        @pl.when(qi % 2 == 1)
        def _():
            tiles(((qi - 1, False), (qi, True)))

        @pl.when(qi % 2 == 0)
        def _():
            tiles(((qi, True),))

        for h in range(hb):
            dq_ref[:, h * LANES:(h + 1) * LANES] = dqacc[h]

        @pl.when(jnp.logical_and(pl.program_id(0) == pl.num_programs(0) - 1, qi == nq - 1))
        def _():
            for cp in _pair_copies(g_refs, l_refs, ssem, rsem):
                cp.wait()

    blk = pl.BlockSpec((TQ, hb * HEAD_DIM), lambda g, qi: (qi, g))
    return pl.pallas_call(
        body, name="mla_bwd", grid=(N_HEADS // hb, nq),
        in_specs=[pl.BlockSpec((TQ, hb * LANES), lambda g, qi: (qi, g)),
                  pl.BlockSpec((s, hb * LANES), lambda g, qi: (0, g)),
                  pl.BlockSpec((s, hb * HEAD_DIM), lambda g, qi: (0, g)), blk, blk, blk] + ride_in,
        out_specs=[pl.BlockSpec((TQ, hb * LANES), lambda g, qi: (qi, g)),
                   pl.BlockSpec((s, hb * LANES), lambda g, qi: (0, g)),
                   pl.BlockSpec((s, hb * HEAD_DIM), lambda g, qi: (0, g))] + ride_out,
        out_shape=[jax.ShapeDtypeStruct((s, 1024), F32), jax.ShapeDtypeStruct((s, 1024), F32),
                   jax.ShapeDtypeStruct((s, D_GRP), F32)] + ride_shape,
        scratch_shapes=[pltpu.VMEM((hb, TQ, LANES), F32), pltpu.VMEM((hb, TQ, TK), F32),
                        pltpu.VMEM((hb, TQ, TK), F32)] + ride_sems,
        compiler_params=_params(("arbitrary", "arbitrary")),
    )(qp, kp, vv, d_o, o, lse, *pays)


def _mid(x, p, target, sb_o, mla_o, rest, g_sb, g_mla, w_out, g_post, w_ple, g_ple, w_pg, b_pg, bd):
    s = x.shape[0]

    def body(x_ref, p_ref, t_ref, sbo_ref, mlo_ref, sbg_ref, mlg_ref, gsb_ref, gml_ref, wout_ref,
             gpost_ref, wple_ref, gple_ref, wpg_ref, bpg_ref, bd_ref,
             dx1_ref, dsbo_ref, dmlo_ref, dsbg_ref, dmlg_ref, x1b_ref, dglb_ref, ycb_ref, dyb_ref,
             pb_ref, dub_ref, small_ref):
        i = pl.program_id(0)
        bd_m = bd_ref[...]

        def seg_mean(v):
            return _dot(v.astype(BF16), bd_m) * (1.0 / HEAD_DIM)

        groups = []
        for o_ref, gate_ref, gain_ref in ((sbo_ref, sbg_ref, gsb_ref), (mlo_ref, mlg_ref, gml_ref)):
            o = o_ref[...]
            r = lax.rsqrt(seg_mean(o * o) + EPS)
            n = o * r
            hn = n * gain_ref[...]
            gate = gate_ref[...]
            sg = _sigmoid(gate)
            si = gate * sg
            groups.append((r, n, hn, gate, sg, si, gain_ref[...]))
        ya = (groups[0][2] * groups[0][5]).astype(BF16)
        yb = (groups[1][2] * groups[1][5]).astype(BF16)
        ycb_ref[:, :D_GRP] = ya
        ycb_ref[:, D_GRP:] = yb
        y = _dot(ya, wout_ref[:D_GRP, :]) + _dot(yb, wout_ref[D_GRP:, :])
        ry = lax.rsqrt(jnp.mean(y * y, axis=-1, keepdims=True) + EPS)
        ny = y * ry
        x1 = x_ref[...] + ny * gpost_ref[...]
        x1b = x1.astype(BF16)
        x1b_ref[...] = x1b
        pb = p_ref[...].astype(BF16)
        pb_ref[...] = pb
        u = _dot(pb, wple_ref[...])
        ru = lax.rsqrt(jnp.mean(u * u, axis=-1, keepdims=True) + EPS)
        nu = u * ru
        ple = nu * gple_ref[...]
        gate = _sigmoid(_dot(x1b, wpg_ref[...]) + bpg_ref[...])
        x2 = x1 + ple * gate
        diff = x2 - t_ref[...]
        dx2 = diff * (1.0 / D_MODEL)

        d_ple = dx2 * gate
        d_glin = (dx2 * ple) * (gate * (1.0 - gate))
        dglb = d_glin.astype(BF16)
        dglb_ref[...] = dglb
        dx1 = dx2 + _dot_nt(dglb, wpg_ref[...])
        dx1_ref[...] = dx1
        d_nu = d_ple * gple_ref[...]
        d_u = ru * (d_nu - nu * jnp.mean(d_nu * nu, axis=-1, keepdims=True))
        dub_ref[...] = d_u.astype(BF16)
        d_ny = dx1 * gpost_ref[...]
        d_y = ry * (d_ny - ny * jnp.mean(d_ny * ny, axis=-1, keepdims=True))
        dyb = d_y.astype(BF16)
        dyb_ref[...] = dyb
        d_yc = (_dot_nt(dyb, wout_ref[:D_GRP, :]), _dot_nt(dyb, wout_ref[D_GRP:, :]))

        d_gain = []
        for gx, (do_ref, dg_ref) in enumerate(((dsbo_ref, dsbg_ref), (dmlo_ref, dmlg_ref))):
            r, n, hn, gate_g, sg, si, gain = groups[gx]
            dyg = d_yc[gx]
            d_hn = dyg * si
            dg_ref[...] = (dyg * hn * (sg * (1.0 + gate_g * (1.0 - sg)))).astype(BF16)
            d_gain.append(jnp.sum(d_hn * n, axis=0, keepdims=True))
            d_n = d_hn * gain
            do_ref[...] = r * (d_n - n * seg_mean(d_n * n))

        @pl.when(i == 0)
        def _():
            small_ref[...] = jnp.zeros_like(small_ref)

        small_ref[3:4, :D_GRP] += d_gain[0]
        small_ref[3:4, D_GRP:] += d_gain[1]
        small_ref[4:5, :] += jnp.sum(dx1 * ny, axis=0, keepdims=True)
        small_ref[5:6, :] += jnp.sum(d_ple * nu, axis=0, keepdims=True)
        small_ref[6:7, :] += jnp.sum(d_glin, axis=0, keepdims=True)
        small_ref[7:8, :] += jnp.sum(diff * diff, axis=0, keepdims=True) * (0.5 / D_MODEL)

    def row(width, idx=0):
        return pl.BlockSpec((TM, width), lambda i: (i, idx))

    def full(a):
        return pl.BlockSpec(a.shape, lambda i: (0, 0))

    f32 = lambda w: jax.ShapeDtypeStruct((s, w), F32)
    b16 = lambda w: jax.ShapeDtypeStruct((s, w), BF16)
    return pl.pallas_call(
        body, name="mid", grid=(s // TM,),
        in_specs=[row(D_MODEL), row(PLE_DIM), row(D_MODEL), row(D_GRP), row(D_GRP),
                  row(D_GRP, 0), row(D_GRP, 1), full(g_sb), full(g_mla), full(w_out), full(g_post),
                  full(w_ple), full(g_ple), full(w_pg), full(b_pg), full(bd)],
        out_specs=(row(D_MODEL), row(D_GRP), row(D_GRP), row(D_GRP), row(D_GRP), row(D_MODEL),
                   row(D_MODEL), row(D_MODEL), row(D_MODEL), row(PLE_DIM), row(D_MODEL),
                   pl.BlockSpec((8, D_MODEL), lambda i: (0, 0))),
        out_shape=(f32(D_MODEL), f32(D_GRP), f32(D_GRP), b16(D_GRP), b16(D_GRP), b16(D_MODEL),
                   b16(D_MODEL), b16(D_MODEL), b16(D_MODEL), b16(PLE_DIM), b16(D_MODEL),
                   jax.ShapeDtypeStruct((8, D_MODEL), F32)),
        compiler_params=_params(("arbitrary",)),
    )(x, p, target, sb_o, mla_o, rest, rest, g_sb, g_mla, w_out, g_post, w_ple, g_ple, w_pg, b_pg, bd)


def _mla_prep_bwd(dqp, dkp, dvv, rest, gq, gkv, wuq, wuk, wuv, cos_t, sin_t):
    s = rest.shape[0]

    def body(dqp_ref, dkp_ref, dvv_ref, cq_ref, ckv_ref, gq_ref, gkv_ref, wuq_ref, wuk_ref, wuv_ref,
             c_ref, s_ref, dcq_ref, dckv_ref, dkr_ref, dqb_ref, dkb_ref, dvb_ref, small_ref):
        i = pl.program_id(0)
        lane = lax.broadcasted_iota(jnp.int32, (1, LANES), 1)
        in_rope = (lane >= HEAD_DIM) & (lane < HEAD_DIM + ROPE_DIM)
        cos_v, sin_v = c_ref[...], s_ref[...]
        dkr_roped = jnp.zeros((TM, LANES), F32)
        for h in range(N_HEADS):
            sl = slice(h * LANES, (h + 1) * LANES)
            dy = dqp_ref[:, sl]
            dqb_ref[:, sl] = (dy * cos_v + _rope_swap(dy * sin_v, lane)).astype(BF16)
            dkh = dkp_ref[:, sl]
            dkb_ref[:, sl] = dkh.astype(BF16)
            dkr_roped = dkr_roped + jnp.where(in_rope, dkh, 0.0)
        dkr_ref[...] = (dkr_roped * cos_v + _rope_swap(dkr_roped * sin_v, lane)).astype(BF16)
        dvb = dvv_ref[...].astype(BF16)
        dvb_ref[...] = dvb

        cq = cq_ref[...]
        rq = lax.rsqrt(jnp.mean(cq * cq, axis=-1, keepdims=True) + EPS)
        nq_ = cq * rq
        d_cqn = _dot_nt(dqb_ref[...], wuq_ref[...])
        d_n = d_cqn * gq_ref[...]
        dcq_ref[...] = (rq * (d_n - nq_ * jnp.mean(d_n * nq_, axis=-1, keepdims=True))).astype(BF16)

        ckv = ckv_ref[...]
        rkv = lax.rsqrt(jnp.mean(ckv * ckv, axis=-1, keepdims=True) + EPS)
        nkv = ckv * rkv
        d_ckvn = _dot_nt(dkb_ref[...], wuk_ref[...]) + _dot_nt(dvb, wuv_ref[...])
        d_n2 = d_ckvn * gkv_ref[...]
        dckv_ref[...] = (rkv * (d_n2 - nkv * jnp.mean(d_n2 * nkv, axis=-1, keepdims=True))).astype(BF16)

        @pl.when(i == 0)
        def _():
            small_ref[...] = jnp.zeros_like(small_ref)

        small_ref[0:1, :] += jnp.sum(d_cqn * nq_, axis=0, keepdims=True)
        small_ref[1:2, :KV_LORA] += jnp.sum(d_ckvn * nkv, axis=0, keepdims=True)

    def row(width, idx=0):
        return pl.BlockSpec((TM, width), lambda i: (i, idx))

    def full(a):
        return pl.BlockSpec(a.shape, lambda i: (0, 0))

    b16 = lambda w: jax.ShapeDtypeStruct((s, w), BF16)
    return pl.pallas_call(
        body, name="mla_prep_bwd", grid=(s // TM,),
        in_specs=[row(1024), row(1024), row(D_GRP), row(Q_LORA, 4), row(KV_LORA, 10), full(gq), full(gkv),
                  full(wuq), full(wuk), full(wuv), row(LANES), row(LANES)],
        out_specs=(row(Q_LORA), row(KV_LORA), row(LANES), row(1024), row(1024), row(D_GRP),
                   pl.BlockSpec((8, Q_LORA), lambda i: (0, 0))),
        out_shape=(b16(Q_LORA), b16(KV_LORA), b16(LANES), b16(1024), b16(1024), b16(D_GRP),
                   jax.ShapeDtypeStruct((8, Q_LORA), F32)),
        compiler_params=_params(("arbitrary",)),
    )(dqp, dkp, dvv, rest, rest, gq, gkv, wuq, wuk, wuv, cos_t, sin_t)


def _in_bwd(x, g, dx1, pieces, w, sums):
    s = x.shape[0]
    steps = s // TM
    widths = [a.shape[1] for a in pieces]
    offs = [sum(widths[:k]) for k in range(len(widths))]
    n_pc, n_op = len(pieces), len(sums)
    ride_in, ride_out, ride_shape, ride_sems = _chip_specs(sums)

    def body(x_ref, g_ref, dx1_ref, *refs):
        piece_refs = refs[:n_pc]
        w_ref = refs[n_pc]
        s_refs = refs[n_pc + 1:n_pc + 1 + n_op]
        dx_ref, small_ref = refs[n_pc + 1 + n_op:n_pc + 3 + n_op]
        l_refs = refs[n_pc + 3 + n_op:n_pc + 3 + 2 * n_op]
        ssem, rsem = refs[n_pc + 3 + 2 * n_op:]
        i = pl.program_id(0)

        @pl.when(i == 0)
        def _():
            for cp in _chip_copies(s_refs, l_refs, ssem, rsem):
                cp.start()

        dh = jnp.zeros((TM, D_MODEL), F32)
        for pr, off, wd in zip(piece_refs, offs, widths):
            dh = dh + _dot_nt(pr[...], w_ref[:, off:off + wd])
        xv = x_ref[...]
        r = lax.rsqrt(jnp.mean(xv * xv, axis=-1, keepdims=True) + EPS)
        n = xv * r
        d_n = dh * g_ref[...]
        dx_ref[...] = dx1_ref[...] + r * (d_n - n * jnp.mean(d_n * n, axis=-1, keepdims=True))

        @pl.when(i == 0)
        def _():
            small_ref[...] = jnp.zeros_like(small_ref)

        small_ref[0:1, :] += jnp.sum(dh * n, axis=0, keepdims=True)

        @pl.when(i == steps - 1)
        def _():
            for cp in _chip_copies(s_refs, l_refs, ssem, rsem):
                cp.wait()

    def row(width):
        return pl.BlockSpec((TM, width), lambda i: (i, 0))

    return pl.pallas_call(
        body, name="in_bwd", grid=(steps,),
        in_specs=[row(D_MODEL), pl.BlockSpec((1, D_MODEL), lambda i: (0, 0)), row(D_MODEL)]
        + [row(wd) for wd in widths] + [pl.BlockSpec(w.shape, lambda i: (0, 0))] + ride_in,
        out_specs=[row(D_MODEL), pl.BlockSpec((8, D_MODEL), lambda i: (0, 0))] + ride_out,
        out_shape=[jax.ShapeDtypeStruct((s, D_MODEL), F32), jax.ShapeDtypeStruct((8, D_MODEL), F32)]
        + ride_shape,
        scratch_shapes=ride_sems,
        compiler_params=_params(("arbitrary",)),
    )(x, g, dx1, *pieces, w, *sums)


def _tn_matmul(a, b, name, blocked=False):
    s, k = a.shape
    n = b.shape[1]
    ts = min(s, TS_DW)
    tn = n if blocked else min(n, 512)
    steps = s // ts

    def body(a_ref, b_ref, o_ref):
        t = pl.program_id(1)

        @pl.when(t == 0)
        def _():
            o_ref[...] = jnp.zeros_like(o_ref)

        prod = _dot_tn(a_ref[...], b_ref[...])
        if blocked:
            for j in range(n // LANES):
                o_ref[j] += prod[:, j * LANES:(j + 1) * LANES]
        else:
            o_ref[...] += prod

    if blocked:
        out_spec = pl.BlockSpec((n // LANES, k, LANES), lambda j, t: (0, 0, 0))
        out_shape = jax.ShapeDtypeStruct((n // LANES, k, LANES), F32)
    else:
        out_spec = pl.BlockSpec((k, tn), lambda j, t: (0, j))
        out_shape = jax.ShapeDtypeStruct((k, n), F32)
    return pl.pallas_call(
        body, name=name, grid=(n // tn, steps),
        in_specs=[pl.BlockSpec((ts, k), lambda j, t: (t, 0)), pl.BlockSpec((ts, tn), lambda j, t: (t, j))],
        out_specs=out_spec, out_shape=out_shape,
        compiler_params=_params(("parallel", "arbitrary")),
    )(a, b)


def _tn_matmul_multi(a, bs, name):
    s, k = a.shape
    widths = [b.shape[1] for b in bs]
    ts = min(s, TS_DW)

    def body(a_ref, *refs):
        b_refs, o_ref = refs[:-1], refs[-1]
        t = pl.program_id(0)

        @pl.when(t == 0)
        def _():
            o_ref[...] = jnp.zeros_like(o_ref)

        av = a_ref[...]
        off = 0
        for b_ref, wd in zip(b_refs, widths):
            o_ref[:, off:off + wd] += _dot_tn(av, b_ref[...])
            off += wd

    return pl.pallas_call(
        body, name=name, grid=(s // ts,),
        in_specs=[pl.BlockSpec((ts, k), lambda t: (t, 0))] + [pl.BlockSpec((ts, wd), lambda t: (t, 0)) for wd in widths],
        out_specs=pl.BlockSpec((k, sum(widths)), lambda t: (0, 0)),
        out_shape=jax.ShapeDtypeStruct((k, sum(widths)), F32),
        compiler_params=_params(("arbitrary",)),
    )(a, *bs)


IN_SHARD = 372
_IN_KERNEL_ORDER = ((0, 2048), (2464, 2976), (2048, 2432))
_IN_ROPE = (2432, 2464)
_IN_GRAD_SRC = ((0, 512, 0, 0), (512, 1024, 0, 512), (1024, 1536, 1, 0), (1536, 2048, 1, 512),
                (2048, 2304, 2, 512), (2304, 2432, 2, 768), (2432, 2464, 2, 960), (2464, 2976, 2, 0))


def _shard_cols(gath_in, lo, hi):
    out = []
    while lo < hi:
        j, a = divmod(lo, IN_SHARD)
        b = min(IN_SHARD, a + hi - lo)
        out.append(gath_in[j][:, a:b])
        lo += b - a
    return out


def _kernel_w_in(g_in):
    zc = lambda n: jnp.zeros((D_MODEL, n), BF16)
    parts = [pc for lo, hi in _IN_KERNEL_ORDER for pc in _shard_cols(g_in, lo, hi)]
    parts += [zc(64)] + _shard_cols(g_in, *_IN_ROPE) + [zc(32)]
    return jnp.concatenate(parts, axis=1)


def _kernel_weights(gath):
    g_uq, g_ukv, g_out, g_ple, g_pg = gath
    w_uq_p = jnp.pad(g_uq, ((0, 0), (0, 0), (0, 32))).transpose(1, 0, 2).reshape(Q_LORA, 1024)
    k_only = jnp.where(jnp.arange(LANES) < HEAD_DIM, g_ukv, jnp.zeros_like(g_ukv))
    w_uk_p = k_only.transpose(1, 0, 2).reshape(KV_LORA, 1024)
    w_uv = g_ukv[:, :, HEAD_DIM:].transpose(1, 0, 2).reshape(KV_LORA, D_GRP)
    w_ple = g_ple.transpose(1, 0, 2).reshape(PLE_DIM, D_MODEL)
    return (w_uq_p, w_uk_p, w_uv, g_out.reshape(D_MODEL, D_MODEL), w_ple, g_pg.reshape(D_MODEL, D_MODEL))


def _payload_in(d_cols):
    blocks = []
    for j in range(N_DEV):
        lo, hi = j * IN_SHARD, (j + 1) * IN_SHARD
        parts = []
        for o_lo, o_hi, idx, off in _IN_GRAD_SRC:
            a, b = max(lo, o_lo), min(hi, o_hi)
            if a < b:
                parts.append(d_cols[idx][:, off + a - o_lo:off + b - o_lo])
        blocks.append(jnp.concatenate(parts, axis=1))
    return jnp.stack(blocks)


def _payload_ukv(duk_blk, d_uv):
    dv_blk = d_uv.reshape(KV_LORA, N_HEADS, HEAD_DIM).transpose(1, 0, 2)
    return jnp.concatenate([duk_blk[:, :, :HEAD_DIM], dv_blk], axis=2)


def _pair_sums(pays, landed, place, tag):
    return [_pair_sum(g, l, place, "grad_pair_sum_%s%d" % (tag, o)) for o, (g, l) in enumerate(zip(pays, landed))]


def kernel(x, p, positions, norm_pre_g, w_in, q_norm_g, w_uq, kv_norm_g, w_ukv, sb_out_norm_g, mla_out_norm_g, w_out, norm_post_g, w_ple, ple_norm_g, w_ple_gate, b_ple_gate, loss_target, m_norm_pre_g, m_w_in, m_q_norm_g, m_w_uq, m_kv_norm_g, m_w_ukv, m_sb_out_norm_g, m_mla_out_norm_g, m_w_out, m_norm_post_g, m_w_ple, m_ple_norm_g, m_w_ple_gate, m_b_ple_gate, v_norm_pre_g, v_w_in, v_q_norm_g, v_w_uq, v_kv_norm_g, v_w_ukv, v_sb_out_norm_g, v_mla_out_norm_g, v_w_out, v_norm_post_g, v_w_ple, v_ple_norm_g, v_w_ple_gate, v_b_ple_gate):
    mats = (w_in, w_uq, w_ukv, w_out, w_ple, w_ple_gate)
    m_mats = (m_w_in, m_w_uq, m_w_ukv, m_w_out, m_w_ple, m_w_ple_gate)
    v_mats = (v_w_in, v_w_uq, v_w_ukv, v_w_out, v_w_ple, v_w_ple_gate)
    vecs = (norm_pre_g, q_norm_g, kv_norm_g, sb_out_norm_g, mla_out_norm_g, norm_post_g, ple_norm_g, b_ple_gate)
    m_vecs = (m_norm_pre_g, m_q_norm_g, m_kv_norm_g, m_sb_out_norm_g, m_mla_out_norm_g, m_norm_post_g,
              m_ple_norm_g, m_b_ple_gate)
    v_vecs = (v_norm_pre_g, v_q_norm_g, v_kv_norm_g, v_sb_out_norm_g, v_mla_out_norm_g, v_norm_post_g,
              v_ple_norm_g, v_b_ple_gate)

    shards = [a[0].astype(BF16) for a in mats]
    w_in_p = _kernel_w_in(_all_gather(shards[:1])[0])
    grad_x, reduced, vec_slab = _step(x[0], p[0, 0], positions[0], loss_target[0], *vecs, w_in_p, shards[1:])
    upd = [_adamw_matrix(own, l2, w, m, v, "adamw_%d" % o)
           for o, ((own, l2), w, m, v) in enumerate(zip(reduced, mats, m_mats, v_mats))]
    sm = _adamw_vectors(_slab_exchange(vec_slab), vecs, m_vecs, v_vecs)

    outs = []
    for kind in range(4):
        mat = [upd[o][kind] for o in range(len(mats))]
        vec = sm[1 + 8 * kind:9 + 8 * kind]
        outs += [vec[0], mat[0], vec[1], mat[1], vec[2], mat[2], vec[3], vec[4], mat[3], vec[5],
                 mat[4], vec[6], mat[5], vec[7]]
    return (sm[0][0, 0], grad_x[None], *outs)


def _step(xs, ps, pos, tgt, norm_pre_g, q_norm_g, kv_norm_g, sb_out_norm_g, mla_out_norm_g,
          norm_post_g, ple_norm_g, b_ple_gate, w_in_p, shards):
    s = xs.shape[0]
    place = jnp.stack([lax.axis_index("c"), 2 * lax.axis_index("x") + lax.axis_index("y")]).astype(jnp.int32)

    half = ROPE_DIM // 2
    freq = ROPE_THETA ** (-jnp.arange(half, dtype=F32) / half)
    ang = pos.astype(F32)[:, None] * freq
    cos, sin = jnp.cos(ang), jnp.sin(ang)
    cos_t = jnp.concatenate([jnp.ones((s, 64), F32), cos, cos, jnp.zeros((s, 32), F32)], axis=1)
    sin_t = jnp.concatenate([jnp.zeros((s, 64), F32), -sin, sin, jnp.zeros((s, 32), F32)], axis=1)
    seg = jnp.arange(D_GRP) // HEAD_DIM
    bd = (seg[:, None] == seg[None, :]).astype(BF16)

    qkv, rest, h_b, *gath = _in_proj(xs, norm_pre_g, w_in_p, shards)
    w_uq_p, w_uk_p, w_uv, f_out, f_ple, f_pg = _kernel_weights(gath)
    sb_o = _sb_fwd(qkv, 8)
    qp, kp, vv, cqn_b, ckvn_b = _mla_prep(rest, q_norm_g, kv_norm_g, w_uq_p, w_uk_p, w_uv, cos_t, sin_t)
    mla_o, lse = _mla_fwd(qp, kp, vv, 4)

    (dx1, d_sbo, d_mlo, d_sbg, d_mlg, x1_b, dgl_b, yc_b, dy_b, p_b, du_b, small_mid) = _mid(
        xs, ps, tgt, sb_o, mla_o, rest, sb_out_norm_g, mla_out_norm_g, f_out, norm_post_g,
        f_ple, ple_norm_g, f_pg, b_ple_gate, bd)
    pay_a = [_tn_matmul(yc_b, dy_b, "dw_out").reshape(N_DEV, 128, D_MODEL),
             _tn_matmul(p_b, du_b, "dw_ple", blocked=True),
             _tn_matmul(x1_b, dgl_b, "dw_pg").reshape(N_DEV, 128, D_MODEL)]
    dqp, dkp, dvv, *sib_a = _mla_bwd(qp, kp, vv, d_mlo, mla_o, lse, 4, pay_a)
    pair_a = _pair_sums(pay_a, sib_a, place, "a")
    dq_sb, dk_sb, dv_sb, *landed_a = _sb_bwd(qkv, d_sbo, [sm for sm, _ in pair_a])
    dcq, dckv, dkr, dq_b, dk_b, dv_b, small_prep = _mla_prep_bwd(
        dqp, dkp, dvv, rest, q_norm_g, kv_norm_g, w_uq_p, w_uk_p, w_uv, cos_t, sin_t)
    pieces = [dq_sb, dk_sb, dv_sb, d_sbg, d_mlg, dcq, dckv, dkr]
    d_cols = [_tn_matmul_multi(h_b, pieces[0:2], "dw_in_0"), _tn_matmul_multi(h_b, pieces[2:4], "dw_in_1"),
              _tn_matmul_multi(h_b, pieces[4:8], "dw_in_2")]
    pay_b = [_payload_in(d_cols), _tn_matmul(cqn_b, dq_b, "dw_uq", blocked=True),
             _payload_ukv(_tn_matmul(ckvn_b, dk_b, "dw_uk", blocked=True), _tn_matmul(ckvn_b, dv_b, "dw_uv"))]
    pair_b = _pair_sums(pay_b, _pair_exchange(pay_b, "grad_pair_exchange"), place, "b")
    grad_x, small_in, *landed_b = _in_bwd(xs, norm_pre_g, dx1, pieces, w_in_p, [sm for sm, _ in pair_b])
    reduced = [(own, l2) for (_, own), l2 in zip(pair_b + pair_a, landed_b + landed_a)]
    slab = jnp.concatenate([small_in[0:1], jnp.pad(small_prep[0:2], ((0, 0), (0, D_MODEL - Q_LORA))),
                            small_mid[3:8]], axis=0)
    return grad_x, reduced, slab
```

```python
import jax
import jax.numpy as jnp
from jax import lax
from jax.experimental import pallas as pl
from jax.experimental.pallas import tpu as pltpu

F32 = jnp.float32
BF16 = jnp.bfloat16
MESH = pl.DeviceIdType.MESH

N_DEV = 8
D_MODEL = 1024
N_HEADS = 8
HEAD_DIM = 64
D_GRP = N_HEADS * HEAD_DIM
Q_LORA = 256
KV_LORA = 128
ROPE_DIM = 32
PLE_DIM = 256
CHUNK_SHIFT = 6
ROPE_THETA = 10000.0
EPS = 1e-6
SB_SCALE = HEAD_DIM ** -0.5
MLA_SCALE = (HEAD_DIM + ROPE_DIM) ** -0.5
NEG = -1e30
LOG2_E = 1.4426950408889634
LN_2 = 0.6931471805599453
SB_CUTOFF = 110.0

ADAM_LR = 0.001
ADAM_B1 = 0.9
ADAM_B2 = 0.999
ADAM_EPS = 1e-08
ADAM_WD = 0.01
ADAM_STEP = 10

LANES = 128
TQ = 256
TK = 256
TM = 256
TS_DW = 2048

D_IN_P = 3072

_NT = (((1,), (1,)), ((), ()))
_TN = (((0,), (0,)), ((), ()))


def _params(sem, vmem_mb):
    return pltpu.CompilerParams(dimension_semantics=sem, vmem_limit_bytes=vmem_mb << 20)


def _dot(a, b):
    return jnp.dot(a, b, preferred_element_type=F32)


def _dot_nt(a, b):
    return lax.dot_general(a, b, _NT, preferred_element_type=F32)


def _dot_tn(a, b):
    return lax.dot_general(a, b, _TN, preferred_element_type=F32)


def _hl_dot(a, b):
    hi = a.astype(BF16)
    lo = (a - hi.astype(F32)).astype(BF16)
    return _dot(hi, b) + _dot(lo, b)


def _sigmoid(x):
    return 1.0 / (1.0 + jnp.exp(-x))


def _rope_swap(x, lane):
    left = pltpu.roll(x, LANES - 16, axis=1)
    right = pltpu.roll(x, 16, axis=1)
    lo = (lane >= 64) & (lane < 80)
    hi = (lane >= 80) & (lane < 96)
    return jnp.where(lo, left, jnp.where(hi, right, 0.0))


def _two_level_gather(x_refs, out_refs, send_sems, recv_sems, local_sems):
    x, y, c = lax.axis_index("x"), lax.axis_index("y"), lax.axis_index("c")
    me, sibling = (x, y, c), (x, y, 1 - c)
    chips = [(1 - x, y), (x, 1 - y), (1 - x, 1 - y)]
    ops = range(len(x_refs))

    def slot(o, px, py, pc):
        return out_refs[o].at[4 * px + 2 * py + pc]

    def copy(o, k, block, to, src=None):
        return pltpu.make_async_remote_copy(
            src_ref=slot(o, *block) if src is None else src, dst_ref=slot(o, *block),
            send_sem=send_sems.at[o, k], recv_sem=recv_sems.at[o, k],
            device_id=to, device_id_type=MESH)

    def mine():
        return [pltpu.make_async_copy(x_refs[o], slot(o, *me), local_sems.at[o]) for o in ops]

    def first():
        return ([copy(o, 0, me, sibling, src=x_refs[o]) for o in ops]
                + [copy(o, 1 + j, me, (*chip, c), src=x_refs[o]) for j, chip in enumerate(chips) for o in ops])

    def start():
        for cp in mine() + first():
            cp.start()

    def finish():
        passed = []
        for j, chip in enumerate(chips):
            for o in ops:
                copy(o, 1 + j, (*chip, c), me).wait_recv()
                passed.append(copy(o, 4 + j, (*chip, c), sibling))
                passed[-1].start()
        for o in ops:
            copy(o, 0, sibling, me).wait_recv()
        for j, chip in enumerate(chips):
            for o in ops:
                copy(o, 4 + j, (*chip, 1 - c), me).wait_recv()
        for cp in first() + passed:
            cp.wait_send()
        for cp in mine():
            cp.wait()

    return start, finish


def _gather_sems(n_op):
    return [pltpu.SemaphoreType.DMA((n_op, 7)), pltpu.SemaphoreType.DMA((n_op, 7)),
            pltpu.SemaphoreType.DMA((n_op,))]


def _all_gather(shards):
    n_op = len(shards)

    def body(*refs):
        start, finish = _two_level_gather(refs[:n_op], refs[n_op:2 * n_op], *refs[2 * n_op:])
        start()
        finish()

    any_spec = pl.BlockSpec(memory_space=pl.ANY)
    return pl.pallas_call(
        body, name="weight_all_gather",
        out_shape=[jax.ShapeDtypeStruct((N_DEV,) + a.shape, a.dtype) for a in shards],
        in_specs=[any_spec] * n_op, out_specs=[any_spec] * n_op, scratch_shapes=_gather_sems(n_op),
        compiler_params=pltpu.CompilerParams(vmem_limit_bytes=4 << 20),
    )(*shards)


def _pair_copies(g_refs, l_refs, ssem, rsem):
    x, y, c = lax.axis_index("x"), lax.axis_index("y"), lax.axis_index("c")
    copies = []
    for o in range(len(g_refs)):
        for chip in range(4):
            copies.append(pltpu.make_async_remote_copy(
                src_ref=g_refs[o].at[2 * chip + (1 - c)], dst_ref=l_refs[o].at[chip],
                send_sem=ssem.at[o, chip], recv_sem=rsem.at[o, chip],
                device_id=(x, y, 1 - c), device_id_type=MESH))
    return copies


def _pair_specs(pays):
    n_op = len(pays)
    any_spec = pl.BlockSpec(memory_space=pl.ANY)
    return ([any_spec] * n_op, [any_spec] * n_op,
            [jax.ShapeDtypeStruct((4,) + a.shape[1:], F32) for a in pays],
            [pltpu.SemaphoreType.DMA((n_op, 4)), pltpu.SemaphoreType.DMA((n_op, 4))])


def _pair_exchange(pays, name):
    n_op = len(pays)
    in_specs, out_specs, out_shape, sems = _pair_specs(pays)

    def body(*refs):
        copies = _pair_copies(refs[:n_op], refs[n_op:2 * n_op], *refs[2 * n_op:])
        for cp in copies:
            cp.start()
        for cp in copies:
            cp.wait()

    return pl.pallas_call(body, name=name, out_shape=out_shape, in_specs=in_specs, out_specs=out_specs,
                          scratch_shapes=sems,
                          compiler_params=pltpu.CompilerParams(vmem_limit_bytes=4 << 20))(*pays)


def _slab_exchange(small):
    sr, n = small.shape

    def body(s_ref, sland_ref, ssem, rsem, lsem):
        x, y, c = lax.axis_index("x"), lax.axis_index("y"), lax.axis_index("c")
        me = 4 * x + 2 * y + c
        copies = []
        for k in range(1, N_DEV):
            peer = (1 - x if (k >> 2) & 1 else x, 1 - y if (k >> 1) & 1 else y, 1 - c if k & 1 else c)
            copies.append(pltpu.make_async_remote_copy(
                src_ref=s_ref, dst_ref=sland_ref.at[me], send_sem=ssem.at[k], recv_sem=rsem.at[k],
                device_id=peer, device_id_type=MESH))
        own = pltpu.make_async_copy(s_ref, sland_ref.at[me], lsem)
        own.start()
        for cp in copies:
            cp.start()
        for cp in copies:
            cp.wait()
        own.wait()

    any_spec = pl.BlockSpec(memory_space=pl.ANY)
    return pl.pallas_call(
        body, name="grad_slab_exchange", out_shape=jax.ShapeDtypeStruct((N_DEV, sr, n), F32),
        in_specs=[any_spec], out_specs=any_spec,
        scratch_shapes=[pltpu.SemaphoreType.DMA((N_DEV,)), pltpu.SemaphoreType.DMA((N_DEV,)),
                        pltpu.SemaphoreType.DMA],
        compiler_params=pltpu.CompilerParams(vmem_limit_bytes=4 << 20),
    )(small)


def _pair_sum(pay, landed, place, name):
    _, r, c = pay.shape

    def body(place_ref, g_ref, l_ref, s_ref, own_ref):
        i = pl.program_id(0)
        tot = g_ref[...] + l_ref[...]
        s_ref[...] = tot.astype(BF16)

        @pl.when(i == place_ref[1])
        def _():
            own_ref[...] = tot

    grid_spec = pltpu.PrefetchScalarGridSpec(
        num_scalar_prefetch=1, grid=(4,),
        in_specs=[pl.BlockSpec((None, r, c), lambda i, pr: (2 * i + pr[0], 0, 0)),
                  pl.BlockSpec((None, r, c), lambda i, pr: (i, 0, 0))],
        out_specs=[pl.BlockSpec((None, r, c), lambda i, pr: (i, 0, 0)),
                   pl.BlockSpec((r, c), lambda i, pr: (0, 0))])
    return pl.pallas_call(
        body, name=name, grid_spec=grid_spec,
        out_shape=[jax.ShapeDtypeStruct((4, r, c), BF16), jax.ShapeDtypeStruct((r, c), F32)],
        compiler_params=_params(("arbitrary",), 14),
    )(place, pay, landed)


def _chip_copies(s_refs, l_refs, ssem, rsem):
    x, y, c = lax.axis_index("x"), lax.axis_index("y"), lax.axis_index("c")
    copies = []
    for rel in range(1, 4):
        px = 1 - x if rel & 2 else x
        py = 1 - y if rel & 1 else y
        for o in range(len(s_refs)):
            copies.append(pltpu.make_async_remote_copy(
                src_ref=s_refs[o].at[2 * px + py], dst_ref=l_refs[o].at[rel - 1],
                send_sem=ssem.at[o, rel - 1], recv_sem=rsem.at[o, rel - 1],
                device_id=(px, py, c), device_id_type=MESH))
    return copies


def _chip_specs(sums):
    n_op = len(sums)
    any_spec = pl.BlockSpec(memory_space=pl.ANY)
    return ([any_spec] * n_op, [any_spec] * n_op,
            [jax.ShapeDtypeStruct((3,) + a.shape[1:], BF16) for a in sums],
            [pltpu.SemaphoreType.DMA((n_op, 3)), pltpu.SemaphoreType.DMA((n_op, 3))])


def _adamw_math(g, w, m, v):
    mn = ADAM_B1 * m + (1.0 - ADAM_B1) * g
    vn = ADAM_B2 * v + (1.0 - ADAM_B2) * (g * g)
    m_hat = mn / (1.0 - ADAM_B1 ** ADAM_STEP)
    v_hat = vn / (1.0 - ADAM_B2 ** ADAM_STEP)
    return -ADAM_LR * (m_hat / (jnp.sqrt(v_hat) + ADAM_EPS) + ADAM_WD * w), mn, vn


def _adamw_matrix(own, landed, w, m, v, name):
    _, r, c = w.shape
    cp = own.shape[1]
    br = min(r, 256)

    def body(own_ref, l_ref, w_ref, m_ref, v_ref, g_out, d_out, m_out, v_out):
        g = own_ref[...]
        for k in range(3):
            g = g + l_ref[k].astype(F32)
        g = g[:, :c]
        g_out[...] = g
        d_out[...], m_out[...], v_out[...] = _adamw_math(g, w_ref[...], m_ref[...], v_ref[...])

    row = pl.BlockSpec((None, br, c), lambda i: (0, i, 0))
    shp = jax.ShapeDtypeStruct((1, r, c), F32)
    return pl.pallas_call(
        body, name=name, grid=(r // br,),
        in_specs=[pl.BlockSpec((br, cp), lambda i: (i, 0)), pl.BlockSpec((3, br, cp), lambda i: (0, i, 0)),
                  row, row, row],
        out_specs=(row, row, row, row), out_shape=(shp, shp, shp, shp),
        compiler_params=_params(("parallel",), 12),
    )(own, landed, w, m, v)


_VEC_PLACE = ((0, 0), (1, 0), (2, 0), (3, 0), (3, D_GRP), (4, 0), (5, 0), (6, 0))


def _adamw_vectors(sland, ws, ms, vs):
    nv = len(ws)

    def body(l_ref, *refs):
        w_refs, m_refs, v_refs = refs[:nv], refs[nv:2 * nv], refs[2 * nv:3 * nv]
        loss_ref = refs[3 * nv]
        outs = refs[3 * nv + 1:]
        g_all = l_ref[0]
        for j in range(1, N_DEV):
            g_all = g_all + l_ref[j]
        loss_ref[...] = jnp.sum(g_all[7:8, :], axis=1, keepdims=True)
        for k, (row, lane0) in enumerate(_VEC_PLACE):
            n = w_refs[k].shape[1]
            g = g_all[row:row + 1, lane0:lane0 + n]
            d, mn, vn = _adamw_math(g, w_refs[k][...], m_refs[k][...], v_refs[k][...])
            outs[k][...] = g
            outs[nv + k][...] = d
            outs[2 * nv + k][...] = mn
            outs[3 * nv + k][...] = vn

    def whole(shape):
        return pl.BlockSpec(shape, lambda i: (0,) * len(shape))

    shapes = [jax.ShapeDtypeStruct(w.shape, F32) for w in ws]
    return pl.pallas_call(
        body, name="adamw_vectors", grid=(1,),
        in_specs=[whole(sland.shape)] + [whole(w.shape) for w in ws] * 3,
        out_specs=[whole((1, 1))] + [whole(w.shape) for w in ws] * 4,
        out_shape=[jax.ShapeDtypeStruct((1, 1), F32)] + shapes * 4,
        compiler_params=_params(("arbitrary",), 4),
    )(sland, *ws, *ms, *vs)


def _in_proj(x, g, w, shards):
    s = x.shape[0]
    n_op = len(shards)
    steps = s // TM

    def body(x_ref, g_ref, w_ref, *refs):
        shard_refs = refs[:n_op]
        qkv_ref, rest_ref, h_ref = refs[n_op:n_op + 3]
        gath_refs = refs[n_op + 3:2 * n_op + 3]
        start, finish = _two_level_gather(shard_refs, gath_refs, *refs[2 * n_op + 3:])
        i = pl.program_id(0)

        @pl.when(i == 0)
        def _():
            start()

        xv = x_ref[...]
        r = lax.rsqrt(jnp.mean(xv * xv, axis=-1, keepdims=True) + EPS)
        h = ((xv * r) * g_ref[...]).astype(BF16)
        h_ref[...] = h
        qkv_ref[...] = _dot(h, w_ref[:, :1536]).astype(BF16)
        rest_ref[...] = _dot(h, w_ref[:, 1536:])

        @pl.when(i == steps - 1)
        def _():
            finish()

    any_spec = pl.BlockSpec(memory_space=pl.ANY)
    return pl.pallas_call(
        body, name="in_proj", grid=(steps,),
        in_specs=[pl.BlockSpec((TM, D_MODEL), lambda i: (i, 0)),
                  pl.BlockSpec((1, D_MODEL), lambda i: (0, 0)),
                  pl.BlockSpec((D_MODEL, D_IN_P), lambda i: (0, 0))] + [any_spec] * n_op,
        out_specs=[pl.BlockSpec((TM, 1536), lambda i: (i, 0)),
                   pl.BlockSpec((TM, 1536), lambda i: (i, 0)),
                   pl.BlockSpec((TM, D_MODEL), lambda i: (i, 0))] + [any_spec] * n_op,
        out_shape=[jax.ShapeDtypeStruct((s, 1536), BF16), jax.ShapeDtypeStruct((s, 1536), F32),
                   jax.ShapeDtypeStruct((s, D_MODEL), BF16)]
        + [jax.ShapeDtypeStruct((N_DEV,) + a.shape, a.dtype) for a in shards],
        scratch_shapes=_gather_sems(n_op),
        compiler_params=_params(("arbitrary",), 20),
    )(x, g, w, *shards)


def _mla_prep(rest, gq, gkv, wuq, wuk, wuv, cos_t, sin_t):
    s = rest.shape[0]

    def body(cq_ref, ckv_ref, kr_ref, gq_ref, gkv_ref, wuq_ref, wuk_ref, wuv_ref, c_ref, s_ref,
             qp_ref, kp_ref, vv_ref, cqn_ref, ckvn_ref):
        lane = lax.broadcasted_iota(jnp.int32, (1, LANES), 1)
        cos_v, sin_v = c_ref[...], s_ref[...]
        cq = cq_ref[...]
        rq = lax.rsqrt(jnp.mean(cq * cq, axis=-1, keepdims=True) + EPS)
        cqn = ((cq * rq) * gq_ref[...]).astype(BF16)
        cqn_ref[...] = cqn
        q = _dot(cqn, wuq_ref[...])
        ckv = ckv_ref[...]
        rkv = lax.rsqrt(jnp.mean(ckv * ckv, axis=-1, keepdims=True) + EPS)
        ckvn = ((ckv * rkv) * gkv_ref[...]).astype(BF16)
        ckvn_ref[...] = ckvn
        kn = _dot(ckvn, wuk_ref[...])
        vv_ref[...] = _dot(ckvn, wuv_ref[...]).astype(BF16)
        kr = kr_ref[...]
        kr_roped = kr * cos_v + _rope_swap(kr, lane) * sin_v
        for h in range(N_HEADS):
            sl = slice(h * LANES, (h + 1) * LANES)
            qh = q[:, sl]
            qp_ref[:, sl] = (qh * cos_v + _rope_swap(qh, lane) * sin_v).astype(BF16)
            kp_ref[:, sl] = (kn[:, sl] + kr_roped).astype(BF16)

    def row(width, idx):
        return pl.BlockSpec((TM, width), lambda i: (i, idx))

    def full(a):
        return pl.BlockSpec(a.shape, lambda i: (0, 0))

    return pl.pallas_call(
        body, name="mla_prep", grid=(s // TM,),
        in_specs=[row(Q_LORA, 4), row(KV_LORA, 10), row(LANES, 11), full(gq), full(gkv),
                  full(wuq), full(wuk), full(wuv), row(LANES, 0), row(LANES, 0)],
        out_specs=(row(1024, 0), row(1024, 0), row(D_GRP, 0), row(Q_LORA, 0), row(KV_LORA, 0)),
        out_shape=(jax.ShapeDtypeStruct((s, 1024), BF16), jax.ShapeDtypeStruct((s, 1024), BF16),
                   jax.ShapeDtypeStruct((s, D_GRP), BF16), jax.ShapeDtypeStruct((s, Q_LORA), BF16),
                   jax.ShapeDtypeStruct((s, KV_LORA), BF16)),
        compiler_params=_params(("parallel",), 12),
    )(rest, rest, rest, gq, gkv, wuq, wuk, wuv, cos_t, sin_t)


def _sb_live(n, qi, carries):
    top = carries[0]
    for c in carries[1:]:
        top = jnp.maximum(top, c)
    return jnp.logical_and(n < qi, jnp.max(top) > -SB_CUTOFF)


def _sb_fwd(qkv, hb):
    s = qkv.shape[0]

    def body(q_ref, k_ref, v_ref, o_ref, acc):
        qi = pl.program_id(1)
        lane = lax.broadcasted_iota(jnp.int32, (1, LANES), 1)
        is_a = lane < HEAD_DIM
        pair = lambda h: slice((h // 2) * LANES, (h // 2 + 1) * LANES)
        q_h = []
        for h in range(hb):
            qs = q_ref[:, pair(h)] * SB_SCALE
            mine = is_a if h % 2 == 0 else jnp.logical_not(is_a)
            q_h.append(jnp.where(mine, qs, jnp.zeros_like(qs)))
        r_i = lax.broadcasted_iota(jnp.int32, (TQ, TK), 0)
        c_i = lax.broadcasted_iota(jnp.int32, (TQ, TK), 1)
        past = c_i < r_i
        upper = (r_i > c_i).astype(BF16)
        acc[...] = jnp.zeros_like(acc)

        def tile(j, carries, diag):
            ks = pl.ds(pl.multiple_of(j * TK, TK), TK)
            zs = [_dot_nt(q_h[h], k_ref[ks, pair(h)]) for h in range(hb)]
            if diag:
                zs = [jnp.where(past, z, NEG) for z in zs]
            lfs = [-(jnp.maximum(z, 0.0) + jnp.log(1.0 + jnp.exp(-jnp.abs(z)))) for z in zs]
            sufs = [_hl_dot(lfs[h], upper) for h in range(hb)]
            out = []
            for h in range(hb):
                w = jnp.exp(zs[h] + lfs[h] + (sufs[h] + carries[h]))
                acc[h] += _dot(w.astype(BF16), v_ref[ks, pair(h)])
                out.append(carries[h] + jnp.sum(lfs[h], axis=1, keepdims=True))
            return tuple(out)

        zero = jnp.zeros((TQ, 1), F32)
        carries = tile(qi, (zero,) * hb, True)

        def step(st):
            return (st[0] + 1,) + tile(qi - 1 - st[0], st[1:], False)

        lax.while_loop(lambda st: _sb_live(st[0], qi, st[1:]), step, (0,) + carries)
        for pr in range(hb // 2):
            o_ref[:, pr * LANES:(pr + 1) * LANES] = jnp.where(is_a, acc[2 * pr], acc[2 * pr + 1])

    width = hb * HEAD_DIM
    nb = D_GRP // width
    slab = lambda part: pl.BlockSpec((s, width), lambda g, qi: (0, part * nb + g))
    blk = pl.BlockSpec((TQ, width), lambda g, qi: (qi, g))
    return pl.pallas_call(
        body, name="sb_fwd", grid=(nb, s // TQ),
        in_specs=[blk, slab(1), slab(2)], out_specs=blk,
        out_shape=jax.ShapeDtypeStruct((s, D_GRP), F32),
        scratch_shapes=[pltpu.VMEM((hb, TQ, LANES), F32)],
        compiler_params=_params(("arbitrary", "arbitrary"), 28),
    )(qkv, qkv, qkv)


def _sb_bwd(qkv, d_o, sums):
    s = qkv.shape[0]
    nq = s // TQ
    nk = s // TK
    n_op = len(sums)
    ride_in, ride_out, ride_shape, ride_sems = _chip_specs(sums)

    def body(q_ref, k_ref, v_ref, do_ref, *refs):
        s_refs = refs[:n_op]
        dq_ref, dk_ref, dv_ref = refs[n_op:n_op + 3]
        l_refs = refs[n_op + 3:2 * n_op + 3]
        x1s, bts, dqacc, dkacc, dvacc, ssem, rsem = refs[2 * n_op + 3:]
        qi = pl.program_id(1)
        first_step = jnp.logical_and(pl.program_id(0) == 0, qi == 0)
        last_step = jnp.logical_and(pl.program_id(0) == pl.num_programs(0) - 1, qi == nq - 1)

        @pl.when(first_step)
        def _():
            for cp in _chip_copies(s_refs, l_refs, ssem, rsem):
                cp.start()

        lane = lax.broadcasted_iota(jnp.int32, (1, LANES), 1)
        is_a = lane < HEAD_DIM

        @pl.when(qi == 0)
        def _():
            dkacc[...] = jnp.zeros_like(dkacc)
            dvacc[...] = jnp.zeros_like(dvacc)

        qs = q_ref[...] * SB_SCALE
        zq = jnp.zeros_like(qs)
        qs_x = (jnp.where(is_a, qs, zq), jnp.where(is_a, zq, qs))
        dob = do_ref[...].astype(BF16)
        do_x = (jnp.where(is_a, dob, zq), jnp.where(is_a, zq, dob))
        r_i = lax.broadcasted_iota(jnp.int32, (TQ, TK), 0)
        c_i = lax.broadcasted_iota(jnp.int32, (TQ, TK), 1)
        past = c_i < r_i
        upper = (r_i > c_i).astype(BF16)
        upper_incl = (r_i >= c_i).astype(BF16)
        dqacc[...] = jnp.zeros_like(dqacc)
        both = ((0, 0), (0, 1), (1, 0), (1, 1))

        def tiles(n):
            j_hi = qi - 2 * n
            lo_ok = j_hi >= 1
            j_lo = jnp.maximum(j_hi - 1, 0)
            ks = (pl.ds(pl.multiple_of(j_hi * TK, TK), TK), pl.ds(pl.multiple_of(j_lo * TK, TK), TK))
            return j_hi, lo_ok, j_lo, ks

        def sweep(n, carries):
            j_hi, lo_ok, j_lo, ks = tiles(n)
            slot = (j_hi, jnp.where(lo_ok, j_lo, nk))
            valid = (jnp.logical_or(past, j_hi < qi), lo_ok)
            z = {th: jnp.where(valid[th[0]], _dot_nt(qs_x[th[1]], k_ref[ks[th[0]], :]), NEG) for th in both}
            log_b, lf_sum, suf = {}, {}, {}
            for th in both:
                lf = -(jnp.maximum(z[th], 0.0) + jnp.log(1.0 + jnp.exp(-jnp.abs(z[th]))))
                log_b[th] = z[th] + lf
                lf_sum[th] = jnp.sum(lf, axis=1, keepdims=True)
                suf[th] = _hl_dot(lf, upper)
            c, g_in = {}, {}
            for h in range(2):
                c[0, h], g_in[0, h] = carries[2 * h], carries[2 * h + 1]
                c[1, h] = c[0, h] + lf_sum[0, h]
            d_a = {th: _dot_nt(do_x[th[1]], v_ref[ks[th[0]], :]) for th in both}
            a_b, g, g_sum, sg = {}, {}, {}, {}
            for th in both:
                a = jnp.exp(log_b[th] + (suf[th] + c[th]))
                a_b[th] = a.astype(BF16)
                g[th] = a * d_a[th]
                g_sum[th] = jnp.sum(g[th], axis=1, keepdims=True)
                sg[th] = _hl_dot(g[th], upper_incl)
            for h in range(2):
                g_in[1, h] = g_in[0, h] + g_sum[0, h]
            for th in both:
                t, h = th
                beta = jnp.exp(log_b[th])
                x1s[slot[t], h] = g[th] * (1.0 - beta) + beta * (sg[th] + g_in[th])
                bts[slot[t], h] = beta
                dvacc[ks[t], :] += _dot_tn(a_b[th], do_x[h])
            out = []
            for h in range(2):
                out.append(c[1, h] + lf_sum[1, h])
                out.append(g_in[1, h] + g_sum[1, h])
            return tuple(out)

        zero = jnp.zeros((TQ, 1), F32)
        first = sweep(0, (zero, zero, zero, zero))

        def more(st):
            return jnp.logical_and(2 * st[0] <= qi, jnp.max(jnp.maximum(st[1], st[3])) > -SB_CUTOFF)

        swept = lax.while_loop(more, lambda st: (st[0] + 1,) + sweep(st[0], st[1:]), (1,) + first)
        g_tot = (swept[2], swept[4])

        def apply(n, carry):
            j_hi, lo_ok, j_lo, ks = tiles(n)

            def one(j, kslice):
                for h in range(2):
                    dz = (x1s[j, h] - bts[j, h] * g_tot[h]).astype(BF16)
                    dqacc[h] += _dot(dz, k_ref[kslice, :])
                    dkacc[kslice, :] += _dot_tn(dz, qs_x[h])

            one(j_hi, ks[0])

            @pl.when(lo_ok)
            def _():
                one(j_lo, ks[1])

            return carry

        lax.fori_loop(0, swept[0], apply, 0)
        dq_ref[...] = (jnp.where(is_a, dqacc[0], dqacc[1]) * SB_SCALE).astype(BF16)

        @pl.when(qi == nq - 1)
        def _():
            dk_ref[...] = dkacc[...].astype(BF16)
            dv_ref[...] = dvacc[...].astype(BF16)

        @pl.when(last_step)
        def _():
            for cp in _chip_copies(s_refs, l_refs, ssem, rsem):
                cp.wait()

    slab = lambda off: pl.BlockSpec((s, LANES), lambda p, qi: (0, off + p))
    blk = pl.BlockSpec((TQ, LANES), lambda p, qi: (qi, p))
    out_slab = pl.BlockSpec((s, LANES), lambda p, qi: (0, p))
    shp = jax.ShapeDtypeStruct((s, D_GRP), BF16)
    return pl.pallas_call(
        body, name="sb_bwd", grid=(4, nq),
        in_specs=[blk, slab(4), slab(8), blk] + ride_in,
        out_specs=[blk, out_slab, out_slab] + ride_out, out_shape=[shp, shp, shp] + ride_shape,
        scratch_shapes=[pltpu.VMEM((nk + 1, 2, TQ, TK), F32)] * 2
        + [pltpu.VMEM((2, TQ, LANES), F32), pltpu.VMEM((s, LANES), F32), pltpu.VMEM((s, LANES), F32)]
        + ride_sems,
        compiler_params=_params(("arbitrary", "arbitrary"), 44),
    )(qkv, qkv, qkv, d_o, *sums)


def _mla_fwd(qp, kp, vv, hb):
    s = qp.shape[0]
    c2 = MLA_SCALE * LOG2_E

    def body(q_ref, k_ref, v_ref, o_ref, lse_ref, vaug, mrun, mb, acc, zbuf):
        qi = pl.program_id(1)
        lane = lax.broadcasted_iota(jnp.int32, (1, LANES), 1)
        is_a = lane < HEAD_DIM

        @pl.when(qi == 0)
        def _():
            for h in range(hb):
                vp = v_ref[:, (h // 2) * LANES:(h // 2 + 1) * LANES]
                mine = is_a if h % 2 == 0 else jnp.logical_not(is_a)
                vaug[h] = jnp.where(mine, vp, jnp.ones_like(vp))

        r_i = lax.broadcasted_iota(jnp.int32, (TQ, TK), 0)
        c_i = lax.broadcasted_iota(jnp.int32, (TQ, TK), 1)
        visible = (c_i >> CHUNK_SHIFT) <= (r_i >> CHUNK_SHIFT)

        def key_rows(j):
            return pl.ds(pl.multiple_of(j * TK, TK), TK)

        def sweep(tiles):
            def loop(n, carry):
                tiles(((2 * n, False), (2 * n + 1, False)))
                return carry

            lax.fori_loop(0, qi // 2, loop, 0)

            @pl.when(qi % 2 == 1)
            def _():
                tiles(((qi - 1, False), (qi, True)))

            @pl.when(qi % 2 == 0)
            def _():
                tiles(((qi, True),))

        mrun[...] = jnp.full_like(mrun, NEG)

        def tiles_max(js):
            zs = [[_dot_nt(q_ref[:, h * LANES:(h + 1) * LANES], k_ref[key_rows(j), h * LANES:(h + 1) * LANES])
                   for h in range(hb)] for j, _ in js]
            for t, (j, diag) in enumerate(js):
                for h in range(hb):
                    z = jnp.where(visible, zs[t][h], NEG) if diag else zs[t][h]
                    zbuf[j, h] = z
                    mrun[h] = jnp.maximum(mrun[h], z)

        sweep(tiles_max)
        for h in range(hb):
            m = jnp.max(mrun[h], axis=1, keepdims=True) * c2
            mb[h] = jnp.broadcast_to(m, (TQ, TK))
        acc[...] = jnp.zeros_like(acc)

        def tiles_pv(js):
            ps = [[jnp.exp2((zbuf[j, h] * c2 - mb[h]).astype(BF16)) for h in range(hb)] for j, _ in js]
            for t, (j, _) in enumerate(js):
                for h in range(hb):
                    acc[h] += _dot(ps[t][h], vaug[h, key_rows(j), :])

        sweep(tiles_pv)
        for pr in range(hb // 2):
            a, b = 2 * pr, 2 * pr + 1
            psl = slice(pr * LANES, (pr + 1) * LANES)
            acc_a, acc_b = acc[a], acc[b]
            l_a = pltpu.roll(acc_a, HEAD_DIM, axis=1)
            l_b = pltpu.roll(acc_b, HEAD_DIM, axis=1)
            o_ref[:, psl] = jnp.where(is_a, acc_a * (1.0 / l_a), acc_b * (1.0 / l_b))
            lse_ref[:, psl] = jnp.where(is_a, mb[a, :, :LANES] * LN_2 + jnp.log(l_a),
                                        mb[b, :, :LANES] * LN_2 + jnp.log(l_b))

    blk = pl.BlockSpec((TQ, hb * HEAD_DIM), lambda g, qi: (qi, g))
    shp = jax.ShapeDtypeStruct((s, D_GRP), F32)
    return pl.pallas_call(
        body, name="mla_fwd", grid=(N_HEADS // hb, s // TQ),
        in_specs=[pl.BlockSpec((TQ, hb * LANES), lambda g, qi: (qi, g)),
                  pl.BlockSpec((s, hb * LANES), lambda g, qi: (0, g)),
                  pl.BlockSpec((s, hb * HEAD_DIM), lambda g, qi: (0, g))],
        out_specs=(blk, blk), out_shape=(shp, shp),
        scratch_shapes=[pltpu.VMEM((hb, s, LANES), BF16), pltpu.VMEM((hb, TQ, TK), F32),
                        pltpu.VMEM((hb, TQ, TK), F32), pltpu.VMEM((hb, TQ, LANES), F32),
                        pltpu.VMEM((s // TK, hb, TQ, TK), F32)],
        compiler_params=_params(("arbitrary", "arbitrary"), 44),
    )(qp, kp, vv)


def _mla_bwd(qp, kp, vv, d_o, o, lse, hb, pays):
    s = qp.shape[0]
    nq = s // TQ
    c2 = MLA_SCALE * LOG2_E
    n_op = len(pays)
    ride_in, ride_out, ride_shape, ride_sems = _pair_specs(pays)

    def body(q_ref, k_ref, v_ref, do_ref, o_ref, lse_ref, *refs):
        g_refs = refs[:n_op]
        dq_ref, dk_ref, dv_ref = refs[n_op:n_op + 3]
        l_refs = refs[n_op + 3:2 * n_op + 3]
        dqacc, lse_b, delta_b, ssem, rsem = refs[2 * n_op + 3:]
        qi = pl.program_id(1)

        @pl.when(jnp.logical_and(pl.program_id(0) == 0, qi == 0))
        def _():
            for cp in _pair_copies(g_refs, l_refs, ssem, rsem):
                cp.start()

        lane = lax.broadcasted_iota(jnp.int32, (1, LANES), 1)
        is_a = lane < HEAD_DIM

        @pl.when(qi == 0)
        def _():
            dk_ref[...] = jnp.zeros_like(dk_ref)
            dv_ref[...] = jnp.zeros_like(dv_ref)

        r_i = lax.broadcasted_iota(jnp.int32, (TQ, TK), 0)
        c_i = lax.broadcasted_iota(jnp.int32, (TQ, TK), 1)
        visible = (c_i >> CHUNK_SHIFT) <= (r_i >> CHUNK_SHIFT)
        do_x = []
        for h in range(hb):
            psl = slice((h // 2) * LANES, (h // 2 + 1) * LANES)
            mine = is_a if h % 2 == 0 else jnp.logical_not(is_a)
            d_o = do_ref[:, psl]
            delta = jnp.sum(jnp.where(mine, d_o * o_ref[:, psl], 0.0), axis=1, keepdims=True)
            lse_h = jnp.sum(jnp.where(lane == (h % 2) * HEAD_DIM, lse_ref[:, psl], 0.0), axis=1, keepdims=True)
            lse_b[h] = jnp.broadcast_to(lse_h * LOG2_E, (TQ, TK))
            delta_b[h] = jnp.broadcast_to(delta, (TQ, TK))
            do_x.append(jnp.where(mine, d_o, 0.0).astype(BF16))
        dqacc[...] = jnp.zeros_like(dqacc)

        head = lambda h: slice(h * LANES, (h + 1) * LANES)
        pair = lambda h: slice((h // 2) * LANES, (h // 2 + 1) * LANES)

        def tiles(js):
            th = [(j, diag, pl.ds(pl.multiple_of(j * TK, TK), TK), h) for j, diag in js for h in range(hb)]
            zs = [_dot_nt(q_ref[:, head(h)], k_ref[ks, head(h)]) for _, _, ks, h in th]
            dps = [_dot_nt(do_x[h], v_ref[ks, pair(h)]) for _, _, ks, h in th]
            for i, (j, diag, ks, h) in enumerate(th):
                e = zs[i] * c2 - lse_b[h]
                if diag:
                    e = jnp.where(visible, e, NEG)
                p = jnp.exp2(e)
                ds = (p * (dps[i] - delta_b[h]) * MLA_SCALE).astype(BF16)
                dqacc[h] += _dot(ds, k_ref[ks, head(h)])
                dk_ref[ks, head(h)] += _dot_tn(ds, q_ref[:, head(h)])
                dv_ref[ks, pair(h)] += _dot_tn(p.astype(BF16), do_x[h])

        def loop(n, c):
            tiles(((2 * n, False), (2 * n + 1, False)))
            return c

        lax.fori_loop(0, qi // 2, loop, 0)

        @pl.when(qi % 2 == 1)
        def _():
            tiles(((qi - 1, False), (qi, True)))

        @pl.when(qi % 2 == 0)
        def _():
            tiles(((qi, True),))

        for h in range(hb):
            dq_ref[:, h * LANES:(h + 1) * LANES] = dqacc[h]

        @pl.when(jnp.logical_and(pl.program_id(0) == pl.num_programs(0) - 1, qi == nq - 1))
        def _():
            for cp in _pair_copies(g_refs, l_refs, ssem, rsem):
                cp.wait()

    blk = pl.BlockSpec((TQ, hb * HEAD_DIM), lambda g, qi: (qi, g))
    return pl.pallas_call(
        body, name="mla_bwd", grid=(N_HEADS // hb, nq),
        in_specs=[pl.BlockSpec((TQ, hb * LANES), lambda g, qi: (qi, g)),
                  pl.BlockSpec((s, hb * LANES), lambda g, qi: (0, g)),
                  pl.BlockSpec((s, hb * HEAD_DIM), lambda g, qi: (0, g)), blk, blk, blk] + ride_in,
        out_specs=[pl.BlockSpec((TQ, hb * LANES), lambda g, qi: (qi, g)),
                   pl.BlockSpec((s, hb * LANES), lambda g, qi: (0, g)),
                   pl.BlockSpec((s, hb * HEAD_DIM), lambda g, qi: (0, g))] + ride_out,
        out_shape=[jax.ShapeDtypeStruct((s, 1024), F32), jax.ShapeDtypeStruct((s, 1024), F32),
                   jax.ShapeDtypeStruct((s, D_GRP), F32)] + ride_shape,
        scratch_shapes=[pltpu.VMEM((hb, TQ, LANES), F32), pltpu.VMEM((hb, TQ, TK), F32),
                        pltpu.VMEM((hb, TQ, TK), F32)] + ride_sems,
        compiler_params=_params(("arbitrary", "arbitrary"), 52),
    )(qp, kp, vv, d_o, o, lse, *pays)


def _mid(x, p, target, sb_o, mla_o, rest, g_sb, g_mla, w_out, g_post, w_ple, g_ple, w_pg, b_pg, bd):
    s = x.shape[0]

    def body(x_ref, p_ref, t_ref, sbo_ref, mlo_ref, sbg_ref, mlg_ref, gsb_ref, gml_ref, wout_ref,
             gpost_ref, wple_ref, gple_ref, wpg_ref, bpg_ref, bd_ref,
             dx1_ref, dsbo_ref, dmlo_ref, dsbg_ref, dmlg_ref, x1b_ref, dglb_ref, ycb_ref, dyb_ref,
             pb_ref, dub_ref, small_ref):
        i = pl.program_id(0)
        bd_m = bd_ref[...]

        def seg_mean(v):
            return _dot(v.astype(BF16), bd_m) * (1.0 / HEAD_DIM)

        groups = []
        for o_ref, gate_ref, gain_ref in ((sbo_ref, sbg_ref, gsb_ref), (mlo_ref, mlg_ref, gml_ref)):
            o = o_ref[...]
            r = lax.rsqrt(seg_mean(o * o) + EPS)
            n = o * r
            hn = n * gain_ref[...]
            gate = gate_ref[...]
            sg = _sigmoid(gate)
            si = gate * sg
            groups.append((r, n, hn, gate, sg, si, gain_ref[...]))
        ya = (groups[0][2] * groups[0][5]).astype(BF16)
        yb = (groups[1][2] * groups[1][5]).astype(BF16)
        ycb_ref[:, :D_GRP] = ya
        ycb_ref[:, D_GRP:] = yb
        y = _dot(ya, wout_ref[:D_GRP, :]) + _dot(yb, wout_ref[D_GRP:, :])
        ry = lax.rsqrt(jnp.mean(y * y, axis=-1, keepdims=True) + EPS)
        ny = y * ry
        x1 = x_ref[...] + ny * gpost_ref[...]
        x1b = x1.astype(BF16)
        x1b_ref[...] = x1b
        pb = p_ref[...].astype(BF16)
        pb_ref[...] = pb
        u = _dot(pb, wple_ref[...])
        ru = lax.rsqrt(jnp.mean(u * u, axis=-1, keepdims=True) + EPS)
        nu = u * ru
        ple = nu * gple_ref[...]
        gate = _sigmoid(_dot(x1b, wpg_ref[...]) + bpg_ref[...])
        x2 = x1 + ple * gate
        diff = x2 - t_ref[...]
        dx2 = diff * (1.0 / D_MODEL)

        d_ple = dx2 * gate
        d_glin = (dx2 * ple) * (gate * (1.0 - gate))
        dglb = d_glin.astype(BF16)
        dglb_ref[...] = dglb
        dx1 = dx2 + _dot_nt(dglb, wpg_ref[...])
        dx1_ref[...] = dx1
        d_nu = d_ple * gple_ref[...]
        d_u = ru * (d_nu - nu * jnp.mean(d_nu * nu, axis=-1, keepdims=True))
        dub_ref[...] = d_u.astype(BF16)
        d_ny = dx1 * gpost_ref[...]
        d_y = ry * (d_ny - ny * jnp.mean(d_ny * ny, axis=-1, keepdims=True))
        dyb = d_y.astype(BF16)
        dyb_ref[...] = dyb
        d_yc = (_dot_nt(dyb, wout_ref[:D_GRP, :]), _dot_nt(dyb, wout_ref[D_GRP:, :]))

        d_gain = []
        for gx, (do_ref, dg_ref) in enumerate(((dsbo_ref, dsbg_ref), (dmlo_ref, dmlg_ref))):
            r, n, hn, gate_g, sg, si, gain = groups[gx]
            dyg = d_yc[gx]
            d_hn = dyg * si
            dg_ref[...] = (dyg * hn * (sg * (1.0 + gate_g * (1.0 - sg)))).astype(BF16)
            d_gain.append(jnp.sum(d_hn * n, axis=0, keepdims=True))
            d_n = d_hn * gain
            do_ref[...] = r * (d_n - n * seg_mean(d_n * n))

        @pl.when(i == 0)
        def _():
            small_ref[...] = jnp.zeros_like(small_ref)

        small_ref[3:4, :D_GRP] += d_gain[0]
        small_ref[3:4, D_GRP:] += d_gain[1]
        small_ref[4:5, :] += jnp.sum(dx1 * ny, axis=0, keepdims=True)
        small_ref[5:6, :] += jnp.sum(d_ple * nu, axis=0, keepdims=True)
        small_ref[6:7, :] += jnp.sum(d_glin, axis=0, keepdims=True)
        small_ref[7:8, :] += jnp.sum(diff * diff, axis=0, keepdims=True) * (0.5 / D_MODEL)

    def row(width, idx=0):
        return pl.BlockSpec((TM, width), lambda i: (i, idx))

    def full(a):
        return pl.BlockSpec(a.shape, lambda i: (0, 0))

    f32 = lambda w: jax.ShapeDtypeStruct((s, w), F32)
    b16 = lambda w: jax.ShapeDtypeStruct((s, w), BF16)
    return pl.pallas_call(
        body, name="mid", grid=(s // TM,),
        in_specs=[row(D_MODEL), row(PLE_DIM), row(D_MODEL), row(D_GRP), row(D_GRP),
                  row(D_GRP, 0), row(D_GRP, 1), full(g_sb), full(g_mla), full(w_out), full(g_post),
                  full(w_ple), full(g_ple), full(w_pg), full(b_pg), full(bd)],
        out_specs=(row(D_MODEL), row(D_GRP), row(D_GRP), row(D_GRP), row(D_GRP), row(D_MODEL),
                   row(D_MODEL), row(D_MODEL), row(D_MODEL), row(PLE_DIM), row(D_MODEL),
                   pl.BlockSpec((8, D_MODEL), lambda i: (0, 0))),
        out_shape=(f32(D_MODEL), f32(D_GRP), f32(D_GRP), b16(D_GRP), b16(D_GRP), b16(D_MODEL),
                   b16(D_MODEL), b16(D_MODEL), b16(D_MODEL), b16(PLE_DIM), b16(D_MODEL),
                   jax.ShapeDtypeStruct((8, D_MODEL), F32)),
        compiler_params=_params(("arbitrary",), 46),
    )(x, p, target, sb_o, mla_o, rest, rest, g_sb, g_mla, w_out, g_post, w_ple, g_ple, w_pg, b_pg, bd)


def _mla_prep_bwd(dqp, dkp, dvv, rest, gq, gkv, wuq, wuk, wuv, cos_t, sin_t):
    s = rest.shape[0]

    def body(dqp_ref, dkp_ref, dvv_ref, cq_ref, ckv_ref, gq_ref, gkv_ref, wuq_ref, wuk_ref, wuv_ref,
             c_ref, s_ref, dcq_ref, dckv_ref, dkr_ref, dqb_ref, dkb_ref, dvb_ref, small_ref):
        i = pl.program_id(0)
        lane = lax.broadcasted_iota(jnp.int32, (1, LANES), 1)
        in_rope = (lane >= HEAD_DIM) & (lane < HEAD_DIM + ROPE_DIM)
        cos_v, sin_v = c_ref[...], s_ref[...]
        dkr_roped = jnp.zeros((TM, LANES), F32)
        for h in range(N_HEADS):
            sl = slice(h * LANES, (h + 1) * LANES)
            dy = dqp_ref[:, sl]
            dqb_ref[:, sl] = (dy * cos_v + _rope_swap(dy * sin_v, lane)).astype(BF16)
            dkh = dkp_ref[:, sl]
            dkb_ref[:, sl] = dkh.astype(BF16)
            dkr_roped = dkr_roped + jnp.where(in_rope, dkh, 0.0)
        dkr_ref[...] = (dkr_roped * cos_v + _rope_swap(dkr_roped * sin_v, lane)).astype(BF16)
        dvb = dvv_ref[...].astype(BF16)
        dvb_ref[...] = dvb

        cq = cq_ref[...]
        rq = lax.rsqrt(jnp.mean(cq * cq, axis=-1, keepdims=True) + EPS)
        nq_ = cq * rq
        d_cqn = _dot_nt(dqb_ref[...], wuq_ref[...])
        d_n = d_cqn * gq_ref[...]
        dcq_ref[...] = (rq * (d_n - nq_ * jnp.mean(d_n * nq_, axis=-1, keepdims=True))).astype(BF16)

        ckv = ckv_ref[...]
        rkv = lax.rsqrt(jnp.mean(ckv * ckv, axis=-1, keepdims=True) + EPS)
        nkv = ckv * rkv
        d_ckvn = _dot_nt(dkb_ref[...], wuk_ref[...]) + _dot_nt(dvb, wuv_ref[...])
        d_n2 = d_ckvn * gkv_ref[...]
        dckv_ref[...] = (rkv * (d_n2 - nkv * jnp.mean(d_n2 * nkv, axis=-1, keepdims=True))).astype(BF16)

        @pl.when(i == 0)
        def _():
            small_ref[...] = jnp.zeros_like(small_ref)

        small_ref[0:1, :] += jnp.sum(d_cqn * nq_, axis=0, keepdims=True)
        small_ref[1:2, :KV_LORA] += jnp.sum(d_ckvn * nkv, axis=0, keepdims=True)

    def row(width, idx=0):
        return pl.BlockSpec((TM, width), lambda i: (i, idx))

    def full(a):
        return pl.BlockSpec(a.shape, lambda i: (0, 0))

    b16 = lambda w: jax.ShapeDtypeStruct((s, w), BF16)
    return pl.pallas_call(
        body, name="mla_prep_bwd", grid=(s // TM,),
        in_specs=[row(1024), row(1024), row(D_GRP), row(Q_LORA, 4), row(KV_LORA, 10), full(gq), full(gkv),
                  full(wuq), full(wuk), full(wuv), row(LANES), row(LANES)],
        out_specs=(row(Q_LORA), row(KV_LORA), row(LANES), row(1024), row(1024), row(D_GRP),
                   pl.BlockSpec((8, Q_LORA), lambda i: (0, 0))),
        out_shape=(b16(Q_LORA), b16(KV_LORA), b16(LANES), b16(1024), b16(1024), b16(D_GRP),
                   jax.ShapeDtypeStruct((8, Q_LORA), F32)),
        compiler_params=_params(("arbitrary",), 16),
    )(dqp, dkp, dvv, rest, rest, gq, gkv, wuq, wuk, wuv, cos_t, sin_t)


def _in_bwd(x, g, dx1, pieces, w, sums):
    s = x.shape[0]
    steps = s // TM
    widths = [a.shape[1] for a in pieces]
    offs = [sum(widths[:k]) for k in range(len(widths))]
    n_pc, n_op = len(pieces), len(sums)
    ride_in, ride_out, ride_shape, ride_sems = _chip_specs(sums)

    def body(x_ref, g_ref, dx1_ref, *refs):
        piece_refs = refs[:n_pc]
        w_ref = refs[n_pc]
        s_refs = refs[n_pc + 1:n_pc + 1 + n_op]
        dx_ref, small_ref = refs[n_pc + 1 + n_op:n_pc + 3 + n_op]
        l_refs = refs[n_pc + 3 + n_op:n_pc + 3 + 2 * n_op]
        ssem, rsem = refs[n_pc + 3 + 2 * n_op:]
        i = pl.program_id(0)

        @pl.when(i == 0)
        def _():
            for cp in _chip_copies(s_refs, l_refs, ssem, rsem):
                cp.start()

        dh = jnp.zeros((TM, D_MODEL), F32)
        for pr, off, wd in zip(piece_refs, offs, widths):
            dh = dh + _dot_nt(pr[...], w_ref[:, off:off + wd])
        xv = x_ref[...]
        r = lax.rsqrt(jnp.mean(xv * xv, axis=-1, keepdims=True) + EPS)
        n = xv * r
        d_n = dh * g_ref[...]
        dx_ref[...] = dx1_ref[...] + r * (d_n - n * jnp.mean(d_n * n, axis=-1, keepdims=True))

        @pl.when(i == 0)
        def _():
            small_ref[...] = jnp.zeros_like(small_ref)

        small_ref[0:1, :] += jnp.sum(dh * n, axis=0, keepdims=True)

        @pl.when(i == steps - 1)
        def _():
            for cp in _chip_copies(s_refs, l_refs, ssem, rsem):
                cp.wait()

    def row(width):
        return pl.BlockSpec((TM, width), lambda i: (i, 0))

    return pl.pallas_call(
        body, name="in_bwd", grid=(steps,),
        in_specs=[row(D_MODEL), pl.BlockSpec((1, D_MODEL), lambda i: (0, 0)), row(D_MODEL)]
        + [row(wd) for wd in widths] + [pl.BlockSpec(w.shape, lambda i: (0, 0))] + ride_in,
        out_specs=[row(D_MODEL), pl.BlockSpec((8, D_MODEL), lambda i: (0, 0))] + ride_out,
        out_shape=[jax.ShapeDtypeStruct((s, D_MODEL), F32), jax.ShapeDtypeStruct((8, D_MODEL), F32)]
        + ride_shape,
        scratch_shapes=ride_sems,
        compiler_params=_params(("arbitrary",), 24),
    )(x, g, dx1, *pieces, w, *sums)


def _tn_matmul(a, b, name, blocked=False):
    s, k = a.shape
    n = b.shape[1]
    ts = min(s, TS_DW)
    tn = n if blocked else min(n, 512)
    steps = s // ts

    def body(a_ref, b_ref, o_ref):
        t = pl.program_id(1)

        @pl.when(t == 0)
        def _():
            o_ref[...] = jnp.zeros_like(o_ref)

        prod = _dot_tn(a_ref[...], b_ref[...])
        if blocked:
            for j in range(n // LANES):
                o_ref[j] += prod[:, j * LANES:(j + 1) * LANES]
        else:
            o_ref[...] += prod

    if blocked:
        out_spec = pl.BlockSpec((n // LANES, k, LANES), lambda j, t: (0, 0, 0))
        out_shape = jax.ShapeDtypeStruct((n // LANES, k, LANES), F32)
    else:
        out_spec = pl.BlockSpec((k, tn), lambda j, t: (0, j))
        out_shape = jax.ShapeDtypeStruct((k, n), F32)
    return pl.pallas_call(
        body, name=name, grid=(n // tn, steps),
        in_specs=[pl.BlockSpec((ts, k), lambda j, t: (t, 0)), pl.BlockSpec((ts, tn), lambda j, t: (t, j))],
        out_specs=out_spec, out_shape=out_shape,
        compiler_params=_params(("parallel", "arbitrary"), 20),
    )(a, b)


def _tn_matmul_multi(a, bs, name):
    s, k = a.shape
    widths = [b.shape[1] for b in bs]
    ts = min(s, TS_DW)

    def body(a_ref, *refs):
        b_refs, o_ref = refs[:-1], refs[-1]
        t = pl.program_id(0)

        @pl.when(t == 0)
        def _():
            o_ref[...] = jnp.zeros_like(o_ref)

        av = a_ref[...]
        off = 0
        for b_ref, wd in zip(b_refs, widths):
            o_ref[:, off:off + wd] += _dot_tn(av, b_ref[...])
            off += wd

    return pl.pallas_call(
        body, name=name, grid=(s // ts,),
        in_specs=[pl.BlockSpec((ts, k), lambda t: (t, 0))] + [pl.BlockSpec((ts, wd), lambda t: (t, 0)) for wd in widths],
        out_specs=pl.BlockSpec((k, sum(widths)), lambda t: (0, 0)),
        out_shape=jax.ShapeDtypeStruct((k, sum(widths)), F32),
        compiler_params=_params(("arbitrary",), 30),
    )(a, *bs)


IN_SHARD = 372
_IN_KERNEL_ORDER = ((0, 2048), (2464, 2976), (2048, 2432))
_IN_ROPE = (2432, 2464)
_IN_GRAD_SRC = ((0, 512, 0, 0), (512, 1024, 0, 512), (1024, 1536, 1, 0), (1536, 2048, 1, 512),
                (2048, 2304, 2, 512), (2304, 2432, 2, 768), (2432, 2464, 2, 960), (2464, 2976, 2, 0))


def _shard_cols(gath_in, lo, hi):
    out = []
    while lo < hi:
        j, a = divmod(lo, IN_SHARD)
        b = min(IN_SHARD, a + hi - lo)
        out.append(gath_in[j][:, a:b])
        lo += b - a
    return out


def _kernel_w_in(g_in):
    zc = lambda n: jnp.zeros((D_MODEL, n), BF16)
    parts = [pc for lo, hi in _IN_KERNEL_ORDER for pc in _shard_cols(g_in, lo, hi)]
    parts += [zc(64)] + _shard_cols(g_in, *_IN_ROPE) + [zc(32)]
    return jnp.concatenate(parts, axis=1)


def _kernel_weights(gath):
    g_uq, g_ukv, g_out, g_ple, g_pg = gath
    w_uq_p = jnp.pad(g_uq, ((0, 0), (0, 0), (0, 32))).transpose(1, 0, 2).reshape(Q_LORA, 1024)
    k_only = jnp.where(jnp.arange(LANES) < HEAD_DIM, g_ukv, jnp.zeros_like(g_ukv))
    w_uk_p = k_only.transpose(1, 0, 2).reshape(KV_LORA, 1024)
    w_uv = g_ukv[:, :, HEAD_DIM:].transpose(1, 0, 2).reshape(KV_LORA, D_GRP)
    w_ple = g_ple.transpose(1, 0, 2).reshape(PLE_DIM, D_MODEL)
    return (w_uq_p, w_uk_p, w_uv, g_out.reshape(D_MODEL, D_MODEL), w_ple, g_pg.reshape(D_MODEL, D_MODEL))


def _payload_in(d_cols):
    blocks = []
    for j in range(N_DEV):
        lo, hi = j * IN_SHARD, (j + 1) * IN_SHARD
        parts = []
        for o_lo, o_hi, idx, off in _IN_GRAD_SRC:
            a, b = max(lo, o_lo), min(hi, o_hi)
            if a < b:
                parts.append(d_cols[idx][:, off + a - o_lo:off + b - o_lo])
        blocks.append(jnp.concatenate(parts, axis=1))
    return jnp.stack(blocks)


def _payload_ukv(duk_blk, d_uv):
    dv_blk = d_uv.reshape(KV_LORA, N_HEADS, HEAD_DIM).transpose(1, 0, 2)
    return jnp.concatenate([duk_blk[:, :, :HEAD_DIM], dv_blk], axis=2)


def _pair_sums(pays, landed, place, tag):
    return [_pair_sum(g, l, place, "grad_pair_sum_%s%d" % (tag, o)) for o, (g, l) in enumerate(zip(pays, landed))]


def kernel(x, p, positions, norm_pre_g, w_in, q_norm_g, w_uq, kv_norm_g, w_ukv, sb_out_norm_g, mla_out_norm_g, w_out, norm_post_g, w_ple, ple_norm_g, w_ple_gate, b_ple_gate, loss_target, m_norm_pre_g, m_w_in, m_q_norm_g, m_w_uq, m_kv_norm_g, m_w_ukv, m_sb_out_norm_g, m_mla_out_norm_g, m_w_out, m_norm_post_g, m_w_ple, m_ple_norm_g, m_w_ple_gate, m_b_ple_gate, v_norm_pre_g, v_w_in, v_q_norm_g, v_w_uq, v_kv_norm_g, v_w_ukv, v_sb_out_norm_g, v_mla_out_norm_g, v_w_out, v_norm_post_g, v_w_ple, v_ple_norm_g, v_w_ple_gate, v_b_ple_gate):
    mats = (w_in, w_uq, w_ukv, w_out, w_ple, w_ple_gate)
    m_mats = (m_w_in, m_w_uq, m_w_ukv, m_w_out, m_w_ple, m_w_ple_gate)
    v_mats = (v_w_in, v_w_uq, v_w_ukv, v_w_out, v_w_ple, v_w_ple_gate)
    vecs = (norm_pre_g, q_norm_g, kv_norm_g, sb_out_norm_g, mla_out_norm_g, norm_post_g, ple_norm_g, b_ple_gate)
    m_vecs = (m_norm_pre_g, m_q_norm_g, m_kv_norm_g, m_sb_out_norm_g, m_mla_out_norm_g, m_norm_post_g,
              m_ple_norm_g, m_b_ple_gate)
    v_vecs = (v_norm_pre_g, v_q_norm_g, v_kv_norm_g, v_sb_out_norm_g, v_mla_out_norm_g, v_norm_post_g,
              v_ple_norm_g, v_b_ple_gate)

    shards = [a[0].astype(BF16) for a in mats]
    w_in_p = _kernel_w_in(_all_gather(shards[:1])[0])
    grad_x, reduced, vec_slab = _step(x[0], p[0, 0], positions[0], loss_target[0], *vecs, w_in_p, shards[1:])
    upd = [_adamw_matrix(own, l2, w, m, v, "adamw_%d" % o)
           for o, ((own, l2), w, m, v) in enumerate(zip(reduced, mats, m_mats, v_mats))]
    sm = _adamw_vectors(_slab_exchange(vec_slab), vecs, m_vecs, v_vecs)

    outs = []
    for kind in range(4):
        mat = [upd[o][kind] for o in range(len(mats))]
        vec = sm[1 + 8 * kind:9 + 8 * kind]
        outs += [vec[0], mat[0], vec[1], mat[1], vec[2], mat[2], vec[3], vec[4], mat[3], vec[5],
                 mat[4], vec[6], mat[5], vec[7]]
    return (sm[0][0, 0], grad_x[None], *outs)


def _step(xs, ps, pos, tgt, norm_pre_g, q_norm_g, kv_norm_g, sb_out_norm_g, mla_out_norm_g,
          norm_post_g, ple_norm_g, b_ple_gate, w_in_p, shards):
    s = xs.shape[0]
    place = jnp.stack([lax.axis_index("c"), 2 * lax.axis_index("x") + lax.axis_index("y")]).astype(jnp.int32)

    half = ROPE_DIM // 2
    freq = ROPE_THETA ** (-jnp.arange(half, dtype=F32) / half)
    ang = pos.astype(F32)[:, None] * freq
    cos, sin = jnp.cos(ang), jnp.sin(ang)
    cos_t = jnp.concatenate([jnp.ones((s, 64), F32), cos, cos, jnp.zeros((s, 32), F32)], axis=1)
    sin_t = jnp.concatenate([jnp.zeros((s, 64), F32), -sin, sin, jnp.zeros((s, 32), F32)], axis=1)
    seg = jnp.arange(D_GRP) // HEAD_DIM
    bd = (seg[:, None] == seg[None, :]).astype(BF16)

    qkv, rest, h_b, *gath = _in_proj(xs, norm_pre_g, w_in_p, shards)
    w_uq_p, w_uk_p, w_uv, f_out, f_ple, f_pg = _kernel_weights(gath)
    sb_o = _sb_fwd(qkv, 8)
    qp, kp, vv, cqn_b, ckvn_b = _mla_prep(rest, q_norm_g, kv_norm_g, w_uq_p, w_uk_p, w_uv, cos_t, sin_t)
    mla_o, lse = _mla_fwd(qp, kp, vv, 4)

    (dx1, d_sbo, d_mlo, d_sbg, d_mlg, x1_b, dgl_b, yc_b, dy_b, p_b, du_b, small_mid) = _mid(
        xs, ps, tgt, sb_o, mla_o, rest, sb_out_norm_g, mla_out_norm_g, f_out, norm_post_g,
        f_ple, ple_norm_g, f_pg, b_ple_gate, bd)
    pay_a = [_tn_matmul(yc_b, dy_b, "dw_out").reshape(N_DEV, 128, D_MODEL),
             _tn_matmul(p_b, du_b, "dw_ple", blocked=True),
             _tn_matmul(x1_b, dgl_b, "dw_pg").reshape(N_DEV, 128, D_MODEL)]
    dqp, dkp, dvv, *sib_a = _mla_bwd(qp, kp, vv, d_mlo, mla_o, lse, 4, pay_a)
    pair_a = _pair_sums(pay_a, sib_a, place, "a")
    dq_sb, dk_sb, dv_sb, *landed_a = _sb_bwd(qkv, d_sbo, [sm for sm, _ in pair_a])
    dcq, dckv, dkr, dq_b, dk_b, dv_b, small_prep = _mla_prep_bwd(
        dqp, dkp, dvv, rest, q_norm_g, kv_norm_g, w_uq_p, w_uk_p, w_uv, cos_t, sin_t)
    pieces = [dq_sb, dk_sb, dv_sb, d_sbg, d_mlg, dcq, dckv, dkr]
    d_cols = [_tn_matmul_multi(h_b, pieces[0:2], "dw_in_0"), _tn_matmul_multi(h_b, pieces[2:4], "dw_in_1"),
              _tn_matmul_multi(h_b, pieces[4:8], "dw_in_2")]
    pay_b = [_payload_in(d_cols), _tn_matmul(cqn_b, dq_b, "dw_uq", blocked=True),
             _payload_ukv(_tn_matmul(ckvn_b, dk_b, "dw_uk", blocked=True), _tn_matmul(ckvn_b, dv_b, "dw_uv"))]
    pair_b = _pair_sums(pay_b, _pair_exchange(pay_b, "grad_pair_exchange"), place, "b")
    grad_x, small_in, *landed_b = _in_bwd(xs, norm_pre_g, dx1, pieces, w_in_p, [sm for sm, _ in pair_b])
    reduced = [(own, l2) for (_, own), l2 in zip(pair_b + pair_a, landed_b + landed_a)]
    slab = jnp.concatenate([small_in[0:1], jnp.pad(small_prep[0:2], ((0, 0), (0, D_MODEL - Q_LORA))),
                            small_mid[3:8]], axis=0)
    return grad_x, reduced, slab
```

```python
import jax
import jax.numpy as jnp
from jax import lax
from jax.experimental import pallas as pl
from jax.experimental.pallas import tpu as pltpu

F32 = jnp.float32
BF16 = jnp.bfloat16
MESH = pl.DeviceIdType.MESH

N_DEV = 8
D_MODEL = 1024
N_HEADS = 8
HEAD_DIM = 64
D_GRP = N_HEADS * HEAD_DIM
Q_LORA = 256
KV_LORA = 128
ROPE_DIM = 32
PLE_DIM = 256
CHUNK_SHIFT = 6
ROPE_THETA = 10000.0
EPS = 1e-6
SB_SCALE = HEAD_DIM ** -0.5
MLA_SCALE = (HEAD_DIM + ROPE_DIM) ** -0.5
NEG = -1e30
LOG2_E = 1.4426950408889634
LN_2 = 0.6931471805599453
SB_CUTOFF = 110.0

ADAM_LR = 0.001
ADAM_B1 = 0.9
ADAM_B2 = 0.999
ADAM_EPS = 1e-08
ADAM_WD = 0.01
ADAM_STEP = 10

LANES = 128
TQ = 256
TK = 256
TM = 256
TM_IO = 512
TS_DW = 2048

D_IN_P = 3072

_NT = (((1,), (1,)), ((), ()))
_TN = (((0,), (0,)), ((), ()))


def _params(sem, vmem_mb):
    return pltpu.CompilerParams(dimension_semantics=sem, vmem_limit_bytes=vmem_mb << 20)


def _dot(a, b):
    return jnp.dot(a, b, preferred_element_type=F32)


def _dot_nt(a, b):
    return lax.dot_general(a, b, _NT, preferred_element_type=F32)


def _dot_tn(a, b):
    return lax.dot_general(a, b, _TN, preferred_element_type=F32)


def _hl_dot(a, b):
    hi = a.astype(BF16)
    lo = (a - hi.astype(F32)).astype(BF16)
    return _dot(hi, b) + _dot(lo, b)


def _sigmoid(x):
    return 1.0 / (1.0 + jnp.exp(-x))


def _rope_swap(x, lane):
    left = pltpu.roll(x, LANES - 16, axis=1)
    right = pltpu.roll(x, 16, axis=1)
    lo = (lane >= 64) & (lane < 80)
    hi = (lane >= 80) & (lane < 96)
    return jnp.where(lo, left, jnp.where(hi, right, 0.0))


def _two_level_gather(x_refs, out_refs, send_sems, recv_sems, local_sems):
    x, y, c = lax.axis_index("x"), lax.axis_index("y"), lax.axis_index("c")
    me, sibling = (x, y, c), (x, y, 1 - c)
    chips = [(1 - x, y), (x, 1 - y), (1 - x, 1 - y)]
    ops = range(len(x_refs))

    def slot(o, px, py, pc):
        return out_refs[o].at[4 * px + 2 * py + pc]

    def copy(o, k, block, to, src=None):
        return pltpu.make_async_remote_copy(
            src_ref=slot(o, *block) if src is None else src, dst_ref=slot(o, *block),
            send_sem=send_sems.at[o, k], recv_sem=recv_sems.at[o, k],
            device_id=to, device_id_type=MESH)

    def mine():
        return [pltpu.make_async_copy(x_refs[o], slot(o, *me), local_sems.at[o]) for o in ops]

    def first():
        return ([copy(o, 0, me, sibling, src=x_refs[o]) for o in ops]
                + [copy(o, 1 + j, me, (*chip, c), src=x_refs[o]) for j, chip in enumerate(chips) for o in ops])

    def start():
        for cp in mine() + first():
            cp.start()

    def finish():
        passed = []
        for j, chip in enumerate(chips):
            for o in ops:
                copy(o, 1 + j, (*chip, c), me).wait_recv()
                passed.append(copy(o, 4 + j, (*chip, c), sibling))
                passed[-1].start()
        for o in ops:
            copy(o, 0, sibling, me).wait_recv()
        for j, chip in enumerate(chips):
            for o in ops:
                copy(o, 4 + j, (*chip, 1 - c), me).wait_recv()
        for cp in first() + passed:
            cp.wait_send()
        for cp in mine():
            cp.wait()

    return start, finish


def _gather_sems(n_op):
    return [pltpu.SemaphoreType.DMA((n_op, 7)), pltpu.SemaphoreType.DMA((n_op, 7)),
            pltpu.SemaphoreType.DMA((n_op,))]


def _all_gather(shards):
    n_op = len(shards)

    def body(*refs):
        start, finish = _two_level_gather(refs[:n_op], refs[n_op:2 * n_op], *refs[2 * n_op:])
        start()
        finish()

    any_spec = pl.BlockSpec(memory_space=pl.ANY)
    return pl.pallas_call(
        body, name="weight_all_gather",
        out_shape=[jax.ShapeDtypeStruct((N_DEV,) + a.shape, a.dtype) for a in shards],
        in_specs=[any_spec] * n_op, out_specs=[any_spec] * n_op, scratch_shapes=_gather_sems(n_op),
        compiler_params=pltpu.CompilerParams(vmem_limit_bytes=4 << 20),
    )(*shards)


def _pair_copies(g_refs, l_refs, ssem, rsem):
    x, y, c = lax.axis_index("x"), lax.axis_index("y"), lax.axis_index("c")
    copies = []
    for o in range(len(g_refs)):
        for chip in range(4):
            copies.append(pltpu.make_async_remote_copy(
                src_ref=g_refs[o].at[2 * chip + (1 - c)], dst_ref=l_refs[o].at[chip],
                send_sem=ssem.at[o, chip], recv_sem=rsem.at[o, chip],
                device_id=(x, y, 1 - c), device_id_type=MESH))
    return copies


def _pair_specs(pays):
    n_op = len(pays)
    any_spec = pl.BlockSpec(memory_space=pl.ANY)
    return ([any_spec] * n_op, [any_spec] * n_op,
            [jax.ShapeDtypeStruct((4,) + a.shape[1:], F32) for a in pays],
            [pltpu.SemaphoreType.DMA((n_op, 4)), pltpu.SemaphoreType.DMA((n_op, 4))])


def _pair_exchange(pays, name):
    n_op = len(pays)
    in_specs, out_specs, out_shape, sems = _pair_specs(pays)

    def body(*refs):
        copies = _pair_copies(refs[:n_op], refs[n_op:2 * n_op], *refs[2 * n_op:])
        for cp in copies:
            cp.start()
        for cp in copies:
            cp.wait()

    return pl.pallas_call(body, name=name, out_shape=out_shape, in_specs=in_specs, out_specs=out_specs,
                          scratch_shapes=sems,
                          compiler_params=pltpu.CompilerParams(vmem_limit_bytes=4 << 20))(*pays)


def _slab_exchange(small):
    sr, n = small.shape

    def body(s_ref, sland_ref, ssem, rsem, lsem):
        x, y, c = lax.axis_index("x"), lax.axis_index("y"), lax.axis_index("c")
        me = 4 * x + 2 * y + c
        copies = []
        for k in range(1, N_DEV):
            peer = (1 - x if (k >> 2) & 1 else x, 1 - y if (k >> 1) & 1 else y, 1 - c if k & 1 else c)
            copies.append(pltpu.make_async_remote_copy(
                src_ref=s_ref, dst_ref=sland_ref.at[me], send_sem=ssem.at[k], recv_sem=rsem.at[k],
                device_id=peer, device_id_type=MESH))
        own = pltpu.make_async_copy(s_ref, sland_ref.at[me], lsem)
        own.start()
        for cp in copies:
            cp.start()
        for cp in copies:
            cp.wait()
        own.wait()

    any_spec = pl.BlockSpec(memory_space=pl.ANY)
    return pl.pallas_call(
        body, name="grad_slab_exchange", out_shape=jax.ShapeDtypeStruct((N_DEV, sr, n), F32),
        in_specs=[any_spec], out_specs=any_spec,
        scratch_shapes=[pltpu.SemaphoreType.DMA((N_DEV,)), pltpu.SemaphoreType.DMA((N_DEV,)),
                        pltpu.SemaphoreType.DMA],
        compiler_params=pltpu.CompilerParams(vmem_limit_bytes=4 << 20),
    )(small)


def _pair_sums(pays, landed, place, name):
    n = len(pays)
    dims = [p.shape[1:] for p in pays]

    def body(place_ref, *refs):
        g_refs, l_refs, s_refs, own_refs = refs[:n], refs[n:2 * n], refs[2 * n:3 * n], refs[3 * n:]
        i = pl.program_id(0)
        for o in range(n):
            tot = g_refs[o][...] + l_refs[o][...]
            s_refs[o][...] = tot.astype(BF16)

            @pl.when(i == place_ref[1])
            def _(o=o, tot=tot):
                own_refs[o][...] = tot

    grid_spec = pltpu.PrefetchScalarGridSpec(
        num_scalar_prefetch=1, grid=(4,),
        in_specs=[pl.BlockSpec((None, r, c), lambda i, pr: (2 * i + pr[0], 0, 0)) for r, c in dims]
        + [pl.BlockSpec((None, r, c), lambda i, pr: (i, 0, 0)) for r, c in dims],
        out_specs=[pl.BlockSpec((None, r, c), lambda i, pr: (i, 0, 0)) for r, c in dims]
        + [pl.BlockSpec((r, c), lambda i, pr: (0, 0)) for r, c in dims])
    out = pl.pallas_call(
        body, name=name, grid_spec=grid_spec,
        out_shape=[jax.ShapeDtypeStruct((4, r, c), BF16) for r, c in dims]
        + [jax.ShapeDtypeStruct((r, c), F32) for r, c in dims],
        compiler_params=_params(("arbitrary",), 16),
    )(place, *pays, *landed)
    return list(zip(out[:n], out[n:]))


def _chip_copies(s_refs, l_refs, ssem, rsem):
    x, y, c = lax.axis_index("x"), lax.axis_index("y"), lax.axis_index("c")
    copies = []
    for rel in range(1, 4):
        px = 1 - x if rel & 2 else x
        py = 1 - y if rel & 1 else y
        for o in range(len(s_refs)):
            copies.append(pltpu.make_async_remote_copy(
                src_ref=s_refs[o].at[2 * px + py], dst_ref=l_refs[o].at[rel - 1],
                send_sem=ssem.at[o, rel - 1], recv_sem=rsem.at[o, rel - 1],
                device_id=(px, py, c), device_id_type=MESH))
    return copies


def _chip_specs(sums):
    n_op = len(sums)
    any_spec = pl.BlockSpec(memory_space=pl.ANY)
    return ([any_spec] * n_op, [any_spec] * n_op,
            [jax.ShapeDtypeStruct((3,) + a.shape[1:], BF16) for a in sums],
            [pltpu.SemaphoreType.DMA((n_op, 3)), pltpu.SemaphoreType.DMA((n_op, 3))])


def _adamw_math(g, w, m, v):
    mn = ADAM_B1 * m + (1.0 - ADAM_B1) * g
    vn = ADAM_B2 * v + (1.0 - ADAM_B2) * (g * g)
    m_hat = mn / (1.0 - ADAM_B1 ** ADAM_STEP)
    v_hat = vn / (1.0 - ADAM_B2 ** ADAM_STEP)
    return -ADAM_LR * (m_hat / (jnp.sqrt(v_hat) + ADAM_EPS) + ADAM_WD * w), mn, vn


def _adamw_matrix(own, landed, w, m, v, name):
    _, r, c = w.shape
    cp = own.shape[1]
    br = min(r, 256)

    def body(own_ref, l_ref, w_ref, m_ref, v_ref, g_out, d_out, m_out, v_out):
        g = own_ref[...]
        for k in range(3):
            g = g + l_ref[k].astype(F32)
        g = g[:, :c]
        g_out[...] = g
        d_out[...], m_out[...], v_out[...] = _adamw_math(g, w_ref[...], m_ref[...], v_ref[...])

    row = pl.BlockSpec((None, br, c), lambda i: (0, i, 0))
    shp = jax.ShapeDtypeStruct((1, r, c), F32)
    return pl.pallas_call(
        body, name=name, grid=(r // br,),
        in_specs=[pl.BlockSpec((br, cp), lambda i: (i, 0)), pl.BlockSpec((3, br, cp), lambda i: (0, i, 0)),
                  row, row, row],
        out_specs=(row, row, row, row), out_shape=(shp, shp, shp, shp),
        compiler_params=_params(("parallel",), 12),
    )(own, landed, w, m, v)


_VEC_PLACE = ((0, 0), (1, 0), (2, 0), (3, 0), (3, D_GRP), (4, 0), (5, 0), (6, 0))


def _adamw_vectors(sland, ws, ms, vs):
    nv = len(ws)

    def body(l_ref, *refs):
        w_refs, m_refs, v_refs = refs[:nv], refs[nv:2 * nv], refs[2 * nv:3 * nv]
        loss_ref = refs[3 * nv]
        outs = refs[3 * nv + 1:]
        g_all = l_ref[0]
        for j in range(1, N_DEV):
            g_all = g_all + l_ref[j]
        loss_ref[...] = jnp.sum(g_all[7:8, :], axis=1, keepdims=True)
        for k, (row, lane0) in enumerate(_VEC_PLACE):
            n = w_refs[k].shape[1]
            g = g_all[row:row + 1, lane0:lane0 + n]
            d, mn, vn = _adamw_math(g, w_refs[k][...], m_refs[k][...], v_refs[k][...])
            outs[k][...] = g
            outs[nv + k][...] = d
            outs[2 * nv + k][...] = mn
            outs[3 * nv + k][...] = vn

    def whole(shape):
        return pl.BlockSpec(shape, lambda i: (0,) * len(shape))

    shapes = [jax.ShapeDtypeStruct(w.shape, F32) for w in ws]
    return pl.pallas_call(
        body, name="adamw_vectors", grid=(1,),
        in_specs=[whole(sland.shape)] + [whole(w.shape) for w in ws] * 3,
        out_specs=[whole((1, 1))] + [whole(w.shape) for w in ws] * 4,
        out_shape=[jax.ShapeDtypeStruct((1, 1), F32)] + shapes * 4,
        compiler_params=_params(("arbitrary",), 4),
    )(sland, *ws, *ms, *vs)


def _in_proj(x, g, w, shards):
    s = x.shape[0]
    n_op = len(shards)
    steps = s // TM_IO

    def body(x_ref, g_ref, w_ref, *refs):
        shard_refs = refs[:n_op]
        qkv_ref, rest_ref, h_ref = refs[n_op:n_op + 3]
        gath_refs = refs[n_op + 3:2 * n_op + 3]
        start, finish = _two_level_gather(shard_refs, gath_refs, *refs[2 * n_op + 3:])
        i = pl.program_id(0)

        @pl.when(i == 0)
        def _():
            start()

        xv = x_ref[...]
        r = lax.rsqrt(jnp.mean(xv * xv, axis=-1, keepdims=True) + EPS)
        h = ((xv * r) * g_ref[...]).astype(BF16)
        h_ref[...] = h
        qkv_ref[...] = _dot(h, w_ref[:, :1536]).astype(BF16)
        rest_ref[...] = _dot(h, w_ref[:, 1536:])

        @pl.when(i == steps - 1)
        def _():
            finish()

    any_spec = pl.BlockSpec(memory_space=pl.ANY)
    return pl.pallas_call(
        body, name="in_proj", grid=(steps,),
        in_specs=[pl.BlockSpec((TM_IO, D_MODEL), lambda i: (i, 0)),
                  pl.BlockSpec((1, D_MODEL), lambda i: (0, 0)),
                  pl.BlockSpec((D_MODEL, D_IN_P), lambda i: (0, 0))] + [any_spec] * n_op,
        out_specs=[pl.BlockSpec((TM_IO, 1536), lambda i: (i, 0)),
                   pl.BlockSpec((TM_IO, 1536), lambda i: (i, 0)),
                   pl.BlockSpec((TM_IO, D_MODEL), lambda i: (i, 0))] + [any_spec] * n_op,
        out_shape=[jax.ShapeDtypeStruct((s, 1536), BF16), jax.ShapeDtypeStruct((s, 1536), F32),
                   jax.ShapeDtypeStruct((s, D_MODEL), BF16)]
        + [jax.ShapeDtypeStruct((N_DEV,) + a.shape, a.dtype) for a in shards],
        scratch_shapes=_gather_sems(n_op),
        compiler_params=_params(("arbitrary",), 32),
    )(x, g, w, *shards)


def _mla_prep(rest, gq, gkv, wuq, wuk, wuv, cos_t, sin_t):
    s = rest.shape[0]

    def body(cq_ref, ckv_ref, kr_ref, gq_ref, gkv_ref, wuq_ref, wuk_ref, wuv_ref, c_ref, s_ref,
             qp_ref, kp_ref, vv_ref, cqn_ref, ckvn_ref):
        lane = lax.broadcasted_iota(jnp.int32, (1, LANES), 1)
        cos_v, sin_v = c_ref[...], s_ref[...]
        cq = cq_ref[...]
        rq = lax.rsqrt(jnp.mean(cq * cq, axis=-1, keepdims=True) + EPS)
        cqn = ((cq * rq) * gq_ref[...]).astype(BF16)
        cqn_ref[...] = cqn
        q = _dot(cqn, wuq_ref[...])
        ckv = ckv_ref[...]
        rkv = lax.rsqrt(jnp.mean(ckv * ckv, axis=-1, keepdims=True) + EPS)
        ckvn = ((ckv * rkv) * gkv_ref[...]).astype(BF16)
        ckvn_ref[...] = ckvn
        kn = _dot(ckvn, wuk_ref[...])
        vv_ref[...] = _dot(ckvn, wuv_ref[...]).astype(BF16)
        kr = kr_ref[...]
        kr_roped = kr * cos_v + _rope_swap(kr, lane) * sin_v
        for h in range(N_HEADS):
            sl = slice(h * LANES, (h + 1) * LANES)
            qh = q[:, sl]
            qp_ref[:, sl] = (qh * cos_v + _rope_swap(qh, lane) * sin_v).astype(BF16)
            kp_ref[:, sl] = (kn[:, sl] + kr_roped).astype(BF16)

    def row(width, idx):
        return pl.BlockSpec((TM, width), lambda i: (i, idx))

    def full(a):
        return pl.BlockSpec(a.shape, lambda i: (0, 0))

    return pl.pallas_call(
        body, name="mla_prep", grid=(s // TM,),
        in_specs=[row(Q_LORA, 4), row(KV_LORA, 10), row(LANES, 11), full(gq), full(gkv),
                  full(wuq), full(wuk), full(wuv), row(LANES, 0), row(LANES, 0)],
        out_specs=(row(1024, 0), row(1024, 0), row(D_GRP, 0), row(Q_LORA, 0), row(KV_LORA, 0)),
        out_shape=(jax.ShapeDtypeStruct((s, 1024), BF16), jax.ShapeDtypeStruct((s, 1024), BF16),
                   jax.ShapeDtypeStruct((s, D_GRP), BF16), jax.ShapeDtypeStruct((s, Q_LORA), BF16),
                   jax.ShapeDtypeStruct((s, KV_LORA), BF16)),
        compiler_params=_params(("parallel",), 12),
    )(rest, rest, rest, gq, gkv, wuq, wuk, wuv, cos_t, sin_t)


def _sb_live(n, qi, carries):
    top = carries[0]
    for c in carries[1:]:
        top = jnp.maximum(top, c)
    return jnp.logical_and(n < qi, jnp.max(top) > -SB_CUTOFF)


def _sb_fwd(qkv, hb):
    s = qkv.shape[0]

    def body(q_ref, k_ref, v_ref, o_ref, acc):
        qi = pl.program_id(1)
        lane = lax.broadcasted_iota(jnp.int32, (1, LANES), 1)
        is_a = lane < HEAD_DIM
        pair = lambda h: slice((h // 2) * LANES, (h // 2 + 1) * LANES)
        q_h = []
        for h in range(hb):
            qs = q_ref[:, pair(h)] * SB_SCALE
            mine = is_a if h % 2 == 0 else jnp.logical_not(is_a)
            q_h.append(jnp.where(mine, qs, jnp.zeros_like(qs)))
        r_i = lax.broadcasted_iota(jnp.int32, (TQ, TK), 0)
        c_i = lax.broadcasted_iota(jnp.int32, (TQ, TK), 1)
        past = c_i < r_i
        upper = (r_i > c_i).astype(BF16)
        acc[...] = jnp.zeros_like(acc)

        def tile(j, carries, diag):
            ks = pl.ds(pl.multiple_of(j * TK, TK), TK)
            zs = [_dot_nt(q_h[h], k_ref[ks, pair(h)]) for h in range(hb)]
            if diag:
                zs = [jnp.where(past, z, NEG) for z in zs]
            lfs = [-(jnp.maximum(z, 0.0) + jnp.log(1.0 + jnp.exp(-jnp.abs(z)))) for z in zs]
            sufs = [_hl_dot(lfs[h], upper) for h in range(hb)]
            out = []
            for h in range(hb):
                w = jnp.exp(zs[h] + lfs[h] + (sufs[h] + carries[h]))
                acc[h] += _dot(w.astype(BF16), v_ref[ks, pair(h)])
                out.append(carries[h] + jnp.sum(lfs[h], axis=1, keepdims=True))
            return tuple(out)

        zero = jnp.zeros((TQ, 1), F32)
        carries = tile(qi, (zero,) * hb, True)

        def step(st):
            return (st[0] + 1,) + tile(qi - 1 - st[0], st[1:], False)

        lax.while_loop(lambda st: _sb_live(st[0], qi, st[1:]), step, (0,) + carries)
        for pr in range(hb // 2):
            o_ref[:, pr * LANES:(pr + 1) * LANES] = jnp.where(is_a, acc[2 * pr], acc[2 * pr + 1])

    width = hb * HEAD_DIM
    nb = D_GRP // width
    slab = lambda part: pl.BlockSpec((s, width), lambda g, qi: (0, part * nb + g))
    blk = pl.BlockSpec((TQ, width), lambda g, qi: (qi, g))
    return pl.pallas_call(
        body, name="sb_fwd", grid=(nb, s // TQ),
        in_specs=[blk, slab(1), slab(2)], out_specs=blk,
        out_shape=jax.ShapeDtypeStruct((s, D_GRP), F32),
        scratch_shapes=[pltpu.VMEM((hb, TQ, LANES), F32)],
        compiler_params=_params(("arbitrary", "arbitrary"), 28),
    )(qkv, qkv, qkv)


def _sb_bwd(qkv, d_o, sums):
    s = qkv.shape[0]
    nq = s // TQ
    nk = s // TK
    n_op = len(sums)
    ride_in, ride_out, ride_shape, ride_sems = _chip_specs(sums)

    def body(q_ref, k_ref, v_ref, do_ref, *refs):
        s_refs = refs[:n_op]
        dq_ref, dk_ref, dv_ref = refs[n_op:n_op + 3]
        l_refs = refs[n_op + 3:2 * n_op + 3]
        x1s, bts, dqacc, dkacc, dvacc, ssem, rsem = refs[2 * n_op + 3:]
        qi = pl.program_id(1)
        first_step = jnp.logical_and(pl.program_id(0) == 0, qi == 0)
        last_step = jnp.logical_and(pl.program_id(0) == pl.num_programs(0) - 1, qi == nq - 1)

        @pl.when(first_step)
        def _():
            for cp in _chip_copies(s_refs, l_refs, ssem, rsem):
                cp.start()

        lane = lax.broadcasted_iota(jnp.int32, (1, LANES), 1)
        is_a = lane < HEAD_DIM

        @pl.when(qi == 0)
        def _():
            dkacc[...] = jnp.zeros_like(dkacc)
            dvacc[...] = jnp.zeros_like(dvacc)

        qs = q_ref[...] * SB_SCALE
        zq = jnp.zeros_like(qs)
        qs_x = (jnp.where(is_a, qs, zq), jnp.where(is_a, zq, qs))
        dob = do_ref[...].astype(BF16)
        do_x = (jnp.where(is_a, dob, zq), jnp.where(is_a, zq, dob))
        r_i = lax.broadcasted_iota(jnp.int32, (TQ, TK), 0)
        c_i = lax.broadcasted_iota(jnp.int32, (TQ, TK), 1)
        past = c_i < r_i
        upper = (r_i > c_i).astype(BF16)
        upper_incl = (r_i >= c_i).astype(BF16)
        dqacc[...] = jnp.zeros_like(dqacc)
        both = ((0, 0), (0, 1), (1, 0), (1, 1))

        def tiles(n):
            j_hi = qi - 2 * n
            lo_ok = j_hi >= 1
            j_lo = jnp.maximum(j_hi - 1, 0)
            ks = (pl.ds(pl.multiple_of(j_hi * TK, TK), TK), pl.ds(pl.multiple_of(j_lo * TK, TK), TK))
            return j_hi, lo_ok, j_lo, ks

        def sweep(n, carries):
            j_hi, lo_ok, j_lo, ks = tiles(n)
            slot = (j_hi, jnp.where(lo_ok, j_lo, nk))
            valid = (jnp.logical_or(past, j_hi < qi), lo_ok)
            z = {th: jnp.where(valid[th[0]], _dot_nt(qs_x[th[1]], k_ref[ks[th[0]], :]), NEG) for th in both}
            log_b, lf_sum, suf = {}, {}, {}
            for th in both:
                lf = -(jnp.maximum(z[th], 0.0) + jnp.log(1.0 + jnp.exp(-jnp.abs(z[th]))))
                log_b[th] = z[th] + lf
                lf_sum[th] = jnp.sum(lf, axis=1, keepdims=True)
                suf[th] = _hl_dot(lf, upper)
            c, g_in = {}, {}
            for h in range(2):
                c[0, h], g_in[0, h] = carries[2 * h], carries[2 * h + 1]
                c[1, h] = c[0, h] + lf_sum[0, h]
            d_a = {th: _dot_nt(do_x[th[1]], v_ref[ks[th[0]], :]) for th in both}
            a_b, g, g_sum, sg = {}, {}, {}, {}
            for th in both:
                a = jnp.exp(log_b[th] + (suf[th] + c[th]))
                a_b[th] = a.astype(BF16)
                g[th] = a * d_a[th]
                g_sum[th] = jnp.sum(g[th], axis=1, keepdims=True)
                sg[th] = _hl_dot(g[th], upper_incl)
            for h in range(2):
                g_in[1, h] = g_in[0, h] + g_sum[0, h]
            for th in both:
                t, h = th
                beta = jnp.exp(log_b[th])
                x1s[slot[t], h] = g[th] * (1.0 - beta) + beta * (sg[th] + g_in[th])
                bts[slot[t], h] = beta
                dvacc[ks[t], :] += _dot_tn(a_b[th], do_x[h])
            out = []
            for h in range(2):
                out.append(c[1, h] + lf_sum[1, h])
                out.append(g_in[1, h] + g_sum[1, h])
            return tuple(out)

        zero = jnp.zeros((TQ, 1), F32)
        first = sweep(0, (zero, zero, zero, zero))

        def more(st):
            return jnp.logical_and(2 * st[0] <= qi, jnp.max(jnp.maximum(st[1], st[3])) > -SB_CUTOFF)

        swept = lax.while_loop(more, lambda st: (st[0] + 1,) + sweep(st[0], st[1:]), (1,) + first)
        g_tot = (swept[2], swept[4])

        def apply(n, carry):
            j_hi, lo_ok, j_lo, ks = tiles(n)

            def one(j, kslice):
                for h in range(2):
                    dz = (x1s[j, h] - bts[j, h] * g_tot[h]).astype(BF16)
                    dqacc[h] += _dot(dz, k_ref[kslice, :])
                    dkacc[kslice, :] += _dot_tn(dz, qs_x[h])

            one(j_hi, ks[0])

            @pl.when(lo_ok)
            def _():
                one(j_lo, ks[1])

            return carry

        lax.fori_loop(0, swept[0], apply, 0)
        dq_ref[...] = (jnp.where(is_a, dqacc[0], dqacc[1]) * SB_SCALE).astype(BF16)

        @pl.when(qi == nq - 1)
        def _():
            dk_ref[...] = dkacc[...].astype(BF16)
            dv_ref[...] = dvacc[...].astype(BF16)

        @pl.when(last_step)
        def _():
            for cp in _chip_copies(s_refs, l_refs, ssem, rsem):
                cp.wait()

    slab = lambda off: pl.BlockSpec((s, LANES), lambda p, qi: (0, off + p))
    blk = pl.BlockSpec((TQ, LANES), lambda p, qi: (qi, p))
    out_slab = pl.BlockSpec((s, LANES), lambda p, qi: (0, p))
    shp = jax.ShapeDtypeStruct((s, D_GRP), BF16)
    return pl.pallas_call(
        body, name="sb_bwd", grid=(4, nq),
        in_specs=[blk, slab(4), slab(8), blk] + ride_in,
        out_specs=[blk, out_slab, out_slab] + ride_out, out_shape=[shp, shp, shp] + ride_shape,
        scratch_shapes=[pltpu.VMEM((nk + 1, 2, TQ, TK), F32)] * 2
        + [pltpu.VMEM((2, TQ, LANES), F32), pltpu.VMEM((s, LANES), F32), pltpu.VMEM((s, LANES), F32)]
        + ride_sems,
        compiler_params=_params(("arbitrary", "arbitrary"), 44),
    )(qkv, qkv, qkv, d_o, *sums)


def _mla_fwd(qp, kp, vv, hb):
    s = qp.shape[0]
    c2 = MLA_SCALE * LOG2_E

    def body(q_ref, k_ref, v_ref, o_ref, lse_ref, vaug, mrun, mb, acc, zbuf):
        qi = pl.program_id(1)
        lane = lax.broadcasted_iota(jnp.int32, (1, LANES), 1)
        is_a = lane < HEAD_DIM

        @pl.when(qi == 0)
        def _():
            for h in range(hb):
                vp = v_ref[:, (h // 2) * LANES:(h // 2 + 1) * LANES]
                mine = is_a if h % 2 == 0 else jnp.logical_not(is_a)
                vaug[h] = jnp.where(mine, vp, jnp.ones_like(vp))

        r_i = lax.broadcasted_iota(jnp.int32, (TQ, TK), 0)
        c_i = lax.broadcasted_iota(jnp.int32, (TQ, TK), 1)
        visible = (c_i >> CHUNK_SHIFT) <= (r_i >> CHUNK_SHIFT)

        def key_rows(j):
            return pl.ds(pl.multiple_of(j * TK, TK), TK)

        def sweep(tiles):
            def loop(n, carry):
                tiles(((2 * n, False), (2 * n + 1, False)))
                return carry

            lax.fori_loop(0, qi // 2, loop, 0)

            @pl.when(qi % 2 == 1)
            def _():
                tiles(((qi - 1, False), (qi, True)))

            @pl.when(qi % 2 == 0)
            def _():
                tiles(((qi, True),))

        mrun[...] = jnp.full_like(mrun, NEG)

        def tiles_max(js):
            zs = [[_dot_nt(q_ref[:, h * LANES:(h + 1) * LANES], k_ref[key_rows(j), h * LANES:(h + 1) * LANES])
                   for h in range(hb)] for j, _ in js]
            for t, (j, diag) in enumerate(js):
                for h in range(hb):
                    z = jnp.where(visible, zs[t][h], NEG) if diag else zs[t][h]
                    zbuf[j, h] = z
                    mrun[h] = jnp.maximum(mrun[h], z)

        sweep(tiles_max)
        for h in range(hb):
            m = jnp.max(mrun[h], axis=1, keepdims=True) * c2
            mb[h] = jnp.broadcast_to(m, (TQ, TK))
        acc[...] = jnp.zeros_like(acc)

        def tiles_pv(js):
            ps = [[jnp.exp2((zbuf[j, h] * c2 - mb[h]).astype(BF16)) for h in range(hb)] for j, _ in js]
            for t, (j, _) in enumerate(js):
                for h in range(hb):
                    acc[h] += _dot(ps[t][h], vaug[h, key_rows(j), :])

        sweep(tiles_pv)
        for pr in range(hb // 2):
            a, b = 2 * pr, 2 * pr + 1
            psl = slice(pr * LANES, (pr + 1) * LANES)
            acc_a, acc_b = acc[a], acc[b]
            l_a = pltpu.roll(acc_a, HEAD_DIM, axis=1)
            l_b = pltpu.roll(acc_b, HEAD_DIM, axis=1)
            o_ref[:, psl] = jnp.where(is_a, acc_a * (1.0 / l_a), acc_b * (1.0 / l_b))
            lse_ref[:, psl] = jnp.where(is_a, mb[a, :, :LANES] * LN_2 + jnp.log(l_a),
                                        mb[b, :, :LANES] * LN_2 + jnp.log(l_b))

    blk = pl.BlockSpec((TQ, hb * HEAD_DIM), lambda g, qi: (qi, g))
    shp = jax.ShapeDtypeStruct((s, D_GRP), F32)
    return pl.pallas_call(
        body, name="mla_fwd", grid=(N_HEADS // hb, s // TQ),
        in_specs=[pl.BlockSpec((TQ, hb * LANES), lambda g, qi: (qi, g)),
                  pl.BlockSpec((s, hb * LANES), lambda g, qi: (0, g)),
                  pl.BlockSpec((s, hb * HEAD_DIM), lambda g, qi: (0, g))],
        out_specs=(blk, blk), out_shape=(shp, shp),
        scratch_shapes=[pltpu.VMEM((hb, s, LANES), BF16), pltpu.VMEM((hb, TQ, TK), F32),
                        pltpu.VMEM((hb, TQ, TK), F32), pltpu.VMEM((hb, TQ, LANES), F32),
                        pltpu.VMEM((s // TK, hb, TQ, TK), F32)],
        compiler_params=_params(("arbitrary", "arbitrary"), 44),
    )(qp, kp, vv)


def _mla_bwd(qp, kp, vv, d_o, o, lse, hb, pays):
    s = qp.shape[0]
    nq = s // TQ
    c2 = MLA_SCALE * LOG2_E
    n_op = len(pays)
    ride_in, ride_out, ride_shape, ride_sems = _pair_specs(pays)

    def body(q_ref, k_ref, v_ref, do_ref, o_ref, lse_ref, *refs):
        g_refs = refs[:n_op]
        dq_ref, dk_ref, dv_ref = refs[n_op:n_op + 3]
        l_refs = refs[n_op + 3:2 * n_op + 3]
        dqacc, lse_b, delta_b, ssem, rsem = refs[2 * n_op + 3:]
        qi = pl.program_id(1)

        @pl.when(jnp.logical_and(pl.program_id(0) == 0, qi == 0))
        def _():
            for cp in _pair_copies(g_refs, l_refs, ssem, rsem):
                cp.start()

        lane = lax.broadcasted_iota(jnp.int32, (1, LANES), 1)
        is_a = lane < HEAD_DIM

        @pl.when(qi == 0)
        def _():
            dk_ref[...] = jnp.zeros_like(dk_ref)
            dv_ref[...] = jnp.zeros_like(dv_ref)

        r_i = lax.broadcasted_iota(jnp.int32, (TQ, TK), 0)
        c_i = lax.broadcasted_iota(jnp.int32, (TQ, TK), 1)
        visible = (c_i >> CHUNK_SHIFT) <= (r_i >> CHUNK_SHIFT)
        do_x = []
        for h in range(hb):
            psl = slice((h // 2) * LANES, (h // 2 + 1) * LANES)
            mine = is_a if h % 2 == 0 else jnp.logical_not(is_a)
            d_o = do_ref[:, psl]
            delta = jnp.sum(jnp.where(mine, d_o * o_ref[:, psl], 0.0), axis=1, keepdims=True)
            lse_h = jnp.sum(jnp.where(lane == (h % 2) * HEAD_DIM, lse_ref[:, psl], 0.0), axis=1, keepdims=True)
            lse_b[h] = jnp.broadcast_to(lse_h * LOG2_E, (TQ, TK))
            delta_b[h] = jnp.broadcast_to(delta, (TQ, TK))
            do_x.append(jnp.where(mine, d_o, 0.0).astype(BF16))
        dqacc[...] = jnp.zeros_like(dqacc)

        head = lambda h: slice(h * LANES, (h + 1) * LANES)
        pair = lambda h: slice((h // 2) * LANES, (h // 2 + 1) * LANES)

        def tiles(js):
            th = [(j, diag, pl.ds(pl.multiple_of(j * TK, TK), TK), h) for j, diag in js for h in range(hb)]
            zs = [_dot_nt(q_ref[:, head(h)], k_ref[ks, head(h)]) for _, _, ks, h in th]
            dps = [_dot_nt(do_x[h], v_ref[ks, pair(h)]) for _, _, ks, h in th]
            for i, (j, diag, ks, h) in enumerate(th):
                e = zs[i] * c2 - lse_b[h]
                if diag:
                    e = jnp.where(visible, e, NEG)
                p = jnp.exp2(e)
                ds = (p * (dps[i] - delta_b[h]) * MLA_SCALE).astype(BF16)
                dqacc[h] += _dot(ds, k_ref[ks, head(h)])
                dk_ref[ks, head(h)] += _dot_tn(ds, q_ref[:, head(h)])
                dv_ref[ks, pair(h)] += _dot_tn(p.astype(BF16), do_x[h])

        def loop(n, c):
            tiles(((2 * n, False), (2 * n + 1, False)))
            return c

        lax.fori_loop(0, qi // 2, loop, 0)

        @pl.when(qi % 2 == 1)
        def _():
            tiles(((qi - 1, False), (qi, True)))

        @pl.when(qi % 2 == 0)
        def _():
            tiles(((qi, True),))

        for h in range(hb):
            dq_ref[:, h * LANES:(h + 1) * LANES] = dqacc[h]

        @pl.when(jnp.logical_and(pl.program_id(0) == pl.num_programs(0) - 1, qi == nq - 1))
        def _():
            for cp in _pair_copies(g_refs, l_refs, ssem, rsem):
                cp.wait()

    blk = pl.BlockSpec((TQ, hb * HEAD_DIM), lambda g, qi: (qi, g))
    return pl.pallas_call(
        body, name="mla_bwd", grid=(N_HEADS // hb, nq),
        in_specs=[pl.BlockSpec((TQ, hb * LANES), lambda g, qi: (qi, g)),
                  pl.BlockSpec((s, hb * LANES), lambda g, qi: (0, g)),
                  pl.BlockSpec((s, hb * HEAD_DIM), lambda g, qi: (0, g)), blk, blk, blk] + ride_in,
        out_specs=[pl.BlockSpec((TQ, hb * LANES), lambda g, qi: (qi, g)),
                   pl.BlockSpec((s, hb * LANES), lambda g, qi: (0, g)),
                   pl.BlockSpec((s, hb * HEAD_DIM), lambda g, qi: (0, g))] + ride_out,
        out_shape=[jax.ShapeDtypeStruct((s, 1024), F32), jax.ShapeDtypeStruct((s, 1024), F32),
                   jax.ShapeDtypeStruct((s, D_GRP), F32)] + ride_shape,
        scratch_shapes=[pltpu.VMEM((hb, TQ, LANES), F32), pltpu.VMEM((hb, TQ, TK), F32),
                        pltpu.VMEM((hb, TQ, TK), F32)] + ride_sems,
        compiler_params=_params(("arbitrary", "arbitrary"), 52),
    )(qp, kp, vv, d_o, o, lse, *pays)


def _mid(x, p, target, sb_o, mla_o, rest, g_sb, g_mla, w_out, g_post, w_ple, g_ple, w_pg, b_pg, bd):
    s = x.shape[0]

    def body(x_ref, p_ref, t_ref, sbo_ref, mlo_ref, sbg_ref, mlg_ref, gsb_ref, gml_ref, wout_ref,
             gpost_ref, wple_ref, gple_ref, wpg_ref, bpg_ref, bd_ref,
             dx1_ref, dsbo_ref, dmlo_ref, dsbg_ref, dmlg_ref, x1b_ref, dglb_ref, ycb_ref, dyb_ref,
             pb_ref, dub_ref, small_ref):
        i = pl.program_id(0)
        bd_m = bd_ref[...]

        def seg_mean(v):
            return _dot(v.astype(BF16), bd_m) * (1.0 / HEAD_DIM)

        groups = []
        for o_ref, gate_ref, gain_ref in ((sbo_ref, sbg_ref, gsb_ref), (mlo_ref, mlg_ref, gml_ref)):
            o = o_ref[...]
            r = lax.rsqrt(seg_mean(o * o) + EPS)
            n = o * r
            hn = n * gain_ref[...]
            gate = gate_ref[...]
            sg = _sigmoid(gate)
            si = gate * sg
            groups.append((r, n, hn, gate, sg, si, gain_ref[...]))
        ya = (groups[0][2] * groups[0][5]).astype(BF16)
        yb = (groups[1][2] * groups[1][5]).astype(BF16)
        ycb_ref[:, :D_GRP] = ya
        ycb_ref[:, D_GRP:] = yb
        y = _dot(ya, wout_ref[:D_GRP, :]) + _dot(yb, wout_ref[D_GRP:, :])
        ry = lax.rsqrt(jnp.mean(y * y, axis=-1, keepdims=True) + EPS)
        ny = y * ry
        x1 = x_ref[...] + ny * gpost_ref[...]
        x1b = x1.astype(BF16)
        x1b_ref[...] = x1b
        pb = p_ref[...].astype(BF16)
        pb_ref[...] = pb
        u = _dot(pb, wple_ref[...])
        ru = lax.rsqrt(jnp.mean(u * u, axis=-1, keepdims=True) + EPS)
        nu = u * ru
        ple = nu * gple_ref[...]
        gate = _sigmoid(_dot(x1b, wpg_ref[...]) + bpg_ref[...])
        x2 = x1 + ple * gate
        diff = x2 - t_ref[...]
        dx2 = diff * (1.0 / D_MODEL)

        d_ple = dx2 * gate
        d_glin = (dx2 * ple) * (gate * (1.0 - gate))
        dglb = d_glin.astype(BF16)
        dglb_ref[...] = dglb
        dx1 = dx2 + _dot_nt(dglb, wpg_ref[...])
        dx1_ref[...] = dx1
        d_nu = d_ple * gple_ref[...]
        d_u = ru * (d_nu - nu * jnp.mean(d_nu * nu, axis=-1, keepdims=True))
        dub_ref[...] = d_u.astype(BF16)
        d_ny = dx1 * gpost_ref[...]
        d_y = ry * (d_ny - ny * jnp.mean(d_ny * ny, axis=-1, keepdims=True))
        dyb = d_y.astype(BF16)
        dyb_ref[...] = dyb
        d_yc = (_dot_nt(dyb, wout_ref[:D_GRP, :]), _dot_nt(dyb, wout_ref[D_GRP:, :]))

        d_gain = []
        for gx, (do_ref, dg_ref) in enumerate(((dsbo_ref, dsbg_ref), (dmlo_ref, dmlg_ref))):
            r, n, hn, gate_g, sg, si, gain = groups[gx]
            dyg = d_yc[gx]
            d_hn = dyg * si
            dg_ref[...] = (dyg * hn * (sg * (1.0 + gate_g * (1.0 - sg)))).astype(BF16)
            d_gain.append(jnp.sum(d_hn * n, axis=0, keepdims=True))
            d_n = d_hn * gain
            do_ref[...] = r * (d_n - n * seg_mean(d_n * n))

        @pl.when(i == 0)
        def _():
            small_ref[...] = jnp.zeros_like(small_ref)

        small_ref[3:4, :D_GRP] += d_gain[0]
        small_ref[3:4, D_GRP:] += d_gain[1]
        small_ref[4:5, :] += jnp.sum(dx1 * ny, axis=0, keepdims=True)
        small_ref[5:6, :] += jnp.sum(d_ple * nu, axis=0, keepdims=True)
        small_ref[6:7, :] += jnp.sum(d_glin, axis=0, keepdims=True)
        small_ref[7:8, :] += jnp.sum(diff * diff, axis=0, keepdims=True) * (0.5 / D_MODEL)

    def row(width, idx=0):
        return pl.BlockSpec((TM, width), lambda i: (i, idx))

    def full(a):
        return pl.BlockSpec(a.shape, lambda i: (0, 0))

    f32 = lambda w: jax.ShapeDtypeStruct((s, w), F32)
    b16 = lambda w: jax.ShapeDtypeStruct((s, w), BF16)
    return pl.pallas_call(
        body, name="mid", grid=(s // TM,),
        in_specs=[row(D_MODEL), row(PLE_DIM), row(D_MODEL), row(D_GRP), row(D_GRP),
                  row(D_GRP, 0), row(D_GRP, 1), full(g_sb), full(g_mla), full(w_out), full(g_post),
                  full(w_ple), full(g_ple), full(w_pg), full(b_pg), full(bd)],
        out_specs=(row(D_MODEL), row(D_GRP), row(D_GRP), row(D_GRP), row(D_GRP), row(D_MODEL),
                   row(D_MODEL), row(D_MODEL), row(D_MODEL), row(PLE_DIM), row(D_MODEL),
                   pl.BlockSpec((8, D_MODEL), lambda i: (0, 0))),
        out_shape=(f32(D_MODEL), f32(D_GRP), f32(D_GRP), b16(D_GRP), b16(D_GRP), b16(D_MODEL),
                   b16(D_MODEL), b16(D_MODEL), b16(D_MODEL), b16(PLE_DIM), b16(D_MODEL),
                   jax.ShapeDtypeStruct((8, D_MODEL), F32)),
        compiler_params=_params(("arbitrary",), 46),
    )(x, p, target, sb_o, mla_o, rest, rest, g_sb, g_mla, w_out, g_post, w_ple, g_ple, w_pg, b_pg, bd)


def _mla_prep_bwd(dqp, dkp, dvv, rest, gq, gkv, wuq, wuk, wuv, cos_t, sin_t):
    s = rest.shape[0]

    def body(dqp_ref, dkp_ref, dvv_ref, cq_ref, ckv_ref, gq_ref, gkv_ref, wuq_ref, wuk_ref, wuv_ref,
             c_ref, s_ref, dcq_ref, dckv_ref, dkr_ref, dqb_ref, dkb_ref, dvb_ref, small_ref):
        i = pl.program_id(0)
        lane = lax.broadcasted_iota(jnp.int32, (1, LANES), 1)
        in_rope = (lane >= HEAD_DIM) & (lane < HEAD_DIM + ROPE_DIM)
        cos_v, sin_v = c_ref[...], s_ref[...]
        dkr_roped = jnp.zeros((TM, LANES), F32)
        for h in range(N_HEADS):
            sl = slice(h * LANES, (h + 1) * LANES)
            dy = dqp_ref[:, sl]
            dqb_ref[:, sl] = (dy * cos_v + _rope_swap(dy * sin_v, lane)).astype(BF16)
            dkh = dkp_ref[:, sl]
            dkb_ref[:, sl] = dkh.astype(BF16)
            dkr_roped = dkr_roped + jnp.where(in_rope, dkh, 0.0)
        dkr_ref[...] = (dkr_roped * cos_v + _rope_swap(dkr_roped * sin_v, lane)).astype(BF16)
        dvb = dvv_ref[...].astype(BF16)
        dvb_ref[...] = dvb

        cq = cq_ref[...]
        rq = lax.rsqrt(jnp.mean(cq * cq, axis=-1, keepdims=True) + EPS)
        nq_ = cq * rq
        d_cqn = _dot_nt(dqb_ref[...], wuq_ref[...])
        d_n = d_cqn * gq_ref[...]
        dcq_ref[...] = (rq * (d_n - nq_ * jnp.mean(d_n * nq_, axis=-1, keepdims=True))).astype(BF16)

        ckv = ckv_ref[...]
        rkv = lax.rsqrt(jnp.mean(ckv * ckv, axis=-1, keepdims=True) + EPS)
        nkv = ckv * rkv
        d_ckvn = _dot_nt(dkb_ref[...], wuk_ref[...]) + _dot_nt(dvb, wuv_ref[...])
        d_n2 = d_ckvn * gkv_ref[...]
        dckv_ref[...] = (rkv * (d_n2 - nkv * jnp.mean(d_n2 * nkv, axis=-1, keepdims=True))).astype(BF16)

        @pl.when(i == 0)
        def _():
            small_ref[...] = jnp.zeros_like(small_ref)

        small_ref[0:1, :] += jnp.sum(d_cqn * nq_, axis=0, keepdims=True)
        small_ref[1:2, :KV_LORA] += jnp.sum(d_ckvn * nkv, axis=0, keepdims=True)

    def row(width, idx=0):
        return pl.BlockSpec((TM, width), lambda i: (i, idx))

    def full(a):
        return pl.BlockSpec(a.shape, lambda i: (0, 0))

    b16 = lambda w: jax.ShapeDtypeStruct((s, w), BF16)
    return pl.pallas_call(
        body, name="mla_prep_bwd", grid=(s // TM,),
        in_specs=[row(1024), row(1024), row(D_GRP), row(Q_LORA, 4), row(KV_LORA, 10), full(gq), full(gkv),
                  full(wuq), full(wuk), full(wuv), row(LANES), row(LANES)],
        out_specs=(row(Q_LORA), row(KV_LORA), row(LANES), row(1024), row(1024), row(D_GRP),
                   pl.BlockSpec((8, Q_LORA), lambda i: (0, 0))),
        out_shape=(b16(Q_LORA), b16(KV_LORA), b16(LANES), b16(1024), b16(1024), b16(D_GRP),
                   jax.ShapeDtypeStruct((8, Q_LORA), F32)),
        compiler_params=_params(("arbitrary",), 16),
    )(dqp, dkp, dvv, rest, rest, gq, gkv, wuq, wuk, wuv, cos_t, sin_t)


def _in_bwd(x, g, dx1, pieces, w, sums):
    s = x.shape[0]
    steps = s // TM_IO
    widths = [a.shape[1] for a in pieces]
    offs = [sum(widths[:k]) for k in range(len(widths))]
    n_pc, n_op = len(pieces), len(sums)
    ride_in, ride_out, ride_shape, ride_sems = _chip_specs(sums)

    def body(x_ref, g_ref, dx1_ref, *refs):
        piece_refs = refs[:n_pc]
        w_ref = refs[n_pc]
        s_refs = refs[n_pc + 1:n_pc + 1 + n_op]
        dx_ref, small_ref = refs[n_pc + 1 + n_op:n_pc + 3 + n_op]
        l_refs = refs[n_pc + 3 + n_op:n_pc + 3 + 2 * n_op]
        ssem, rsem = refs[n_pc + 3 + 2 * n_op:]
        i = pl.program_id(0)

        @pl.when(i == 0)
        def _():
            for cp in _chip_copies(s_refs, l_refs, ssem, rsem):
                cp.start()

        dh = jnp.zeros((TM_IO, D_MODEL), F32)
        for pr, off, wd in zip(piece_refs, offs, widths):
            dh = dh + _dot_nt(pr[...], w_ref[:, off:off + wd])
        xv = x_ref[...]
        r = lax.rsqrt(jnp.mean(xv * xv, axis=-1, keepdims=True) + EPS)
        n = xv * r
        d_n = dh * g_ref[...]
        dx_ref[...] = dx1_ref[...] + r * (d_n - n * jnp.mean(d_n * n, axis=-1, keepdims=True))

        @pl.when(i == 0)
        def _():
            small_ref[...] = jnp.zeros_like(small_ref)

        small_ref[0:1, :] += jnp.sum(dh * n, axis=0, keepdims=True)

        @pl.when(i == steps - 1)
        def _():
            for cp in _chip_copies(s_refs, l_refs, ssem, rsem):
                cp.wait()

    def row(width):
        return pl.BlockSpec((TM_IO, width), lambda i: (i, 0))

    return pl.pallas_call(
        body, name="in_bwd", grid=(steps,),
        in_specs=[row(D_MODEL), pl.BlockSpec((1, D_MODEL), lambda i: (0, 0)), row(D_MODEL)]
        + [row(wd) for wd in widths] + [pl.BlockSpec(w.shape, lambda i: (0, 0))] + ride_in,
        out_specs=[row(D_MODEL), pl.BlockSpec((8, D_MODEL), lambda i: (0, 0))] + ride_out,
        out_shape=[jax.ShapeDtypeStruct((s, D_MODEL), F32), jax.ShapeDtypeStruct((8, D_MODEL), F32)]
        + ride_shape,
        scratch_shapes=ride_sems,
        compiler_params=_params(("arbitrary",), 40),
    )(x, g, dx1, *pieces, w, *sums)


def _tn_matmul(a, b, name, blocked=False):
    s, k = a.shape
    n = b.shape[1]
    ts = min(s, TS_DW)
    tn = n if blocked else min(n, 512)
    steps = s // ts

    def body(a_ref, b_ref, o_ref):
        t = pl.program_id(1)

        @pl.when(t == 0)
        def _():
            o_ref[...] = jnp.zeros_like(o_ref)

        prod = _dot_tn(a_ref[...], b_ref[...])
        if blocked:
            for j in range(n // LANES):
                o_ref[j] += prod[:, j * LANES:(j + 1) * LANES]
        else:
            o_ref[...] += prod

    if blocked:
        out_spec = pl.BlockSpec((n // LANES, k, LANES), lambda j, t: (0, 0, 0))
        out_shape = jax.ShapeDtypeStruct((n // LANES, k, LANES), F32)
    else:
        out_spec = pl.BlockSpec((k, tn), lambda j, t: (0, j))
        out_shape = jax.ShapeDtypeStruct((k, n), F32)
    return pl.pallas_call(
        body, name=name, grid=(n // tn, steps),
        in_specs=[pl.BlockSpec((ts, k), lambda j, t: (t, 0)), pl.BlockSpec((ts, tn), lambda j, t: (t, j))],
        out_specs=out_spec, out_shape=out_shape,
        compiler_params=_params(("parallel", "arbitrary"), 20),
    )(a, b)


def _tn_matmul_multi(a, bs, name):
    s, k = a.shape
    widths = [b.shape[1] for b in bs]
    ts = min(s, TS_DW)

    def body(a_ref, *refs):
        b_refs, o_ref = refs[:-1], refs[-1]
        t = pl.program_id(0)

        @pl.when(t == 0)
        def _():
            o_ref[...] = jnp.zeros_like(o_ref)

        av = a_ref[...]
        off = 0
        for b_ref, wd in zip(b_refs, widths):
            o_ref[:, off:off + wd] += _dot_tn(av, b_ref[...])
            off += wd

    return pl.pallas_call(
        body, name=name, grid=(s // ts,),
        in_specs=[pl.BlockSpec((ts, k), lambda t: (t, 0))] + [pl.BlockSpec((ts, wd), lambda t: (t, 0)) for wd in widths],
        out_specs=pl.BlockSpec((k, sum(widths)), lambda t: (0, 0)),
        out_shape=jax.ShapeDtypeStruct((k, sum(widths)), F32),
        compiler_params=_params(("arbitrary",), 30),
    )(a, *bs)


IN_SHARD = 372
_IN_KERNEL_ORDER = ((0, 2048), (2464, 2976), (2048, 2432))
_IN_ROPE = (2432, 2464)
_IN_GRAD_SRC = ((0, 512, 0, 0), (512, 1024, 0, 512), (1024, 1536, 1, 0), (1536, 2048, 1, 512),
                (2048, 2304, 2, 512), (2304, 2432, 2, 768), (2432, 2464, 2, 960), (2464, 2976, 2, 0))


def _shard_cols(gath_in, lo, hi):
    out = []
    while lo < hi:
        j, a = divmod(lo, IN_SHARD)
        b = min(IN_SHARD, a + hi - lo)
        out.append(gath_in[j][:, a:b])
        lo += b - a
    return out


def _kernel_w_in(g_in):
    zc = lambda n: jnp.zeros((D_MODEL, n), BF16)
    parts = [pc for lo, hi in _IN_KERNEL_ORDER for pc in _shard_cols(g_in, lo, hi)]
    parts += [zc(64)] + _shard_cols(g_in, *_IN_ROPE) + [zc(32)]
    return jnp.concatenate(parts, axis=1)


def _kernel_weights(gath):
    g_uq, g_ukv, g_out, g_ple, g_pg = gath
    w_uq_p = jnp.pad(g_uq, ((0, 0), (0, 0), (0, 32))).transpose(1, 0, 2).reshape(Q_LORA, 1024)
    k_only = jnp.where(jnp.arange(LANES) < HEAD_DIM, g_ukv, jnp.zeros_like(g_ukv))
    w_uk_p = k_only.transpose(1, 0, 2).reshape(KV_LORA, 1024)
    w_uv = g_ukv[:, :, HEAD_DIM:].transpose(1, 0, 2).reshape(KV_LORA, D_GRP)
    w_ple = g_ple.transpose(1, 0, 2).reshape(PLE_DIM, D_MODEL)
    return (w_uq_p, w_uk_p, w_uv, g_out.reshape(D_MODEL, D_MODEL), w_ple, g_pg.reshape(D_MODEL, D_MODEL))


def _payload_in(d_cols):
    blocks = []
    for j in range(N_DEV):
        lo, hi = j * IN_SHARD, (j + 1) * IN_SHARD
        parts = []
        for o_lo, o_hi, idx, off in _IN_GRAD_SRC:
            a, b = max(lo, o_lo), min(hi, o_hi)
            if a < b:
                parts.append(d_cols[idx][:, off + a - o_lo:off + b - o_lo])
        blocks.append(jnp.concatenate(parts, axis=1))
    return jnp.stack(blocks)


def _payload_ukv(duk_blk, d_uv):
    dv_blk = d_uv.reshape(KV_LORA, N_HEADS, HEAD_DIM).transpose(1, 0, 2)
    return jnp.concatenate([duk_blk[:, :, :HEAD_DIM], dv_blk], axis=2)


def kernel(x, p, positions, norm_pre_g, w_in, q_norm_g, w_uq, kv_norm_g, w_ukv, sb_out_norm_g, mla_out_norm_g, w_out, norm_post_g, w_ple, ple_norm_g, w_ple_gate, b_ple_gate, loss_target, m_norm_pre_g, m_w_in, m_q_norm_g, m_w_uq, m_kv_norm_g, m_w_ukv, m_sb_out_norm_g, m_mla_out_norm_g, m_w_out, m_norm_post_g, m_w_ple, m_ple_norm_g, m_w_ple_gate, m_b_ple_gate, v_norm_pre_g, v_w_in, v_q_norm_g, v_w_uq, v_kv_norm_g, v_w_ukv, v_sb_out_norm_g, v_mla_out_norm_g, v_w_out, v_norm_post_g, v_w_ple, v_ple_norm_g, v_w_ple_gate, v_b_ple_gate):
    mats = (w_in, w_uq, w_ukv, w_out, w_ple, w_ple_gate)
    m_mats = (m_w_in, m_w_uq, m_w_ukv, m_w_out, m_w_ple, m_w_ple_gate)
    v_mats = (v_w_in, v_w_uq, v_w_ukv, v_w_out, v_w_ple, v_w_ple_gate)
    vecs = (norm_pre_g, q_norm_g, kv_norm_g, sb_out_norm_g, mla_out_norm_g, norm_post_g, ple_norm_g, b_ple_gate)
    m_vecs = (m_norm_pre_g, m_q_norm_g, m_kv_norm_g, m_sb_out_norm_g, m_mla_out_norm_g, m_norm_post_g,
              m_ple_norm_g, m_b_ple_gate)
    v_vecs = (v_norm_pre_g, v_q_norm_g, v_kv_norm_g, v_sb_out_norm_g, v_mla_out_norm_g, v_norm_post_g,
              v_ple_norm_g, v_b_ple_gate)

    shards = [a[0].astype(BF16) for a in mats]
    w_in_p = _kernel_w_in(_all_gather(shards[:1])[0])
    grad_x, reduced, vec_slab = _step(x[0], p[0, 0], positions[0], loss_target[0], *vecs, w_in_p, shards[1:])
    upd = [_adamw_matrix(own, l2, w, m, v, "adamw_%d" % o)
           for o, ((own, l2), w, m, v) in enumerate(zip(reduced, mats, m_mats, v_mats))]
    sm = _adamw_vectors(_slab_exchange(vec_slab), vecs, m_vecs, v_vecs)

    outs = []
    for kind in range(4):
        mat = [upd[o][kind] for o in range(len(mats))]
        vec = sm[1 + 8 * kind:9 + 8 * kind]
        outs += [vec[0], mat[0], vec[1], mat[1], vec[2], mat[2], vec[3], vec[4], mat[3], vec[5],
                 mat[4], vec[6], mat[5], vec[7]]
    return (sm[0][0, 0], grad_x[None], *outs)


def _step(xs, ps, pos, tgt, norm_pre_g, q_norm_g, kv_norm_g, sb_out_norm_g, mla_out_norm_g,
          norm_post_g, ple_norm_g, b_ple_gate, w_in_p, shards):
    s = xs.shape[0]
    place = jnp.stack([lax.axis_index("c"), 2 * lax.axis_index("x") + lax.axis_index("y")]).astype(jnp.int32)

    half = ROPE_DIM // 2
    freq = ROPE_THETA ** (-jnp.arange(half, dtype=F32) / half)
    ang = pos.astype(F32)[:, None] * freq
    cos, sin = jnp.cos(ang), jnp.sin(ang)
    cos_t = jnp.concatenate([jnp.ones((s, 64), F32), cos, cos, jnp.zeros((s, 32), F32)], axis=1)
    sin_t = jnp.concatenate([jnp.zeros((s, 64), F32), -sin, sin, jnp.zeros((s, 32), F32)], axis=1)
    seg = jnp.arange(D_GRP) // HEAD_DIM
    bd = (seg[:, None] == seg[None, :]).astype(BF16)

    qkv, rest, h_b, *gath = _in_proj(xs, norm_pre_g, w_in_p, shards)
    w_uq_p, w_uk_p, w_uv, f_out, f_ple, f_pg = _kernel_weights(gath)
    sb_o = _sb_fwd(qkv, 8)
    qp, kp, vv, cqn_b, ckvn_b = _mla_prep(rest, q_norm_g, kv_norm_g, w_uq_p, w_uk_p, w_uv, cos_t, sin_t)
    mla_o, lse = _mla_fwd(qp, kp, vv, 4)

    (dx1, d_sbo, d_mlo, d_sbg, d_mlg, x1_b, dgl_b, yc_b, dy_b, p_b, du_b, small_mid) = _mid(
        xs, ps, tgt, sb_o, mla_o, rest, sb_out_norm_g, mla_out_norm_g, f_out, norm_post_g,
        f_ple, ple_norm_g, f_pg, b_ple_gate, bd)
    pay_a = [_tn_matmul(yc_b, dy_b, "dw_out").reshape(N_DEV, 128, D_MODEL),
             _tn_matmul(p_b, du_b, "dw_ple", blocked=True),
             _tn_matmul(x1_b, dgl_b, "dw_pg").reshape(N_DEV, 128, D_MODEL)]
    dqp, dkp, dvv, *sib_a = _mla_bwd(qp, kp, vv, d_mlo, mla_o, lse, 4, pay_a)
    pair_a = _pair_sums(pay_a, sib_a, place, "grad_pair_sums_a")
    dq_sb, dk_sb, dv_sb, *landed_a = _sb_bwd(qkv, d_sbo, [sm for sm, _ in pair_a])
    dcq, dckv, dkr, dq_b, dk_b, dv_b, small_prep = _mla_prep_bwd(
        dqp, dkp, dvv, rest, q_norm_g, kv_norm_g, w_uq_p, w_uk_p, w_uv, cos_t, sin_t)
    pieces = [dq_sb, dk_sb, dv_sb, d_sbg, d_mlg, dcq, dckv, dkr]
    d_cols = [_tn_matmul_multi(h_b, pieces[0:2], "dw_in_0"), _tn_matmul_multi(h_b, pieces[2:4], "dw_in_1"),
              _tn_matmul_multi(h_b, pieces[4:8], "dw_in_2")]
    pay_b = [_payload_in(d_cols), _tn_matmul(cqn_b, dq_b, "dw_uq", blocked=True),
             _payload_ukv(_tn_matmul(ckvn_b, dk_b, "dw_uk", blocked=True), _tn_matmul(ckvn_b, dv_b, "dw_uv"))]
    pair_b = _pair_sums(pay_b, _pair_exchange(pay_b, "grad_pair_exchange"), place, "grad_pair_sums_b")
    grad_x, small_in, *landed_b = _in_bwd(xs, norm_pre_g, dx1, pieces, w_in_p, [sm for sm, _ in pair_b])
    reduced = [(own, l2) for (_, own), l2 in zip(pair_b + pair_a, landed_b + landed_a)]
    slab = jnp.concatenate([small_in[0:1], jnp.pad(small_prep[0:2], ((0, 0), (0, D_MODEL - Q_LORA))),
                            small_mid[3:8]], axis=0)
    return grad_x, reduced, slab
```

```python
import jax
import jax.numpy as jnp
from jax import lax
from jax.experimental import pallas as pl
from jax.experimental.pallas import tpu as pltpu

F32 = jnp.float32
BF16 = jnp.bfloat16
MESH = pl.DeviceIdType.MESH

N_DEV = 8
D_MODEL = 1024
N_HEADS = 8
HEAD_DIM = 64
D_GRP = N_HEADS * HEAD_DIM
Q_LORA = 256
KV_LORA = 128
ROPE_DIM = 32
PLE_DIM = 256
CHUNK_SHIFT = 6
ROPE_THETA = 10000.0
EPS = 1e-6
SB_SCALE = HEAD_DIM ** -0.5
MLA_SCALE = (HEAD_DIM + ROPE_DIM) ** -0.5
NEG = -1e30
LOG2_E = 1.4426950408889634
LN_2 = 0.6931471805599453
SB_CUTOFF = 110.0

ADAM_LR = 0.001
ADAM_B1 = 0.9
ADAM_B2 = 0.999
ADAM_EPS = 1e-08
ADAM_WD = 0.01
ADAM_STEP = 10

LANES = 128
TQ = 256
TK = 256
TM = 256
TM_IO = 512
TS_DW = 2048

D_IN_P = 3072

_NT = (((1,), (1,)), ((), ()))
_TN = (((0,), (0,)), ((), ()))


def _params(sem, vmem_mb):
    return pltpu.CompilerParams(dimension_semantics=sem, vmem_limit_bytes=vmem_mb << 20)


def _hbm(*arrays):
    return [pltpu.with_memory_space_constraint(a, pltpu.HBM) for a in arrays]


def _dot(a, b):
    return jnp.dot(a, b, preferred_element_type=F32)


def _dot_nt(a, b):
    return lax.dot_general(a, b, _NT, preferred_element_type=F32)


def _dot_tn(a, b):
    return lax.dot_general(a, b, _TN, preferred_element_type=F32)


def _hl_dot(a, b):
    hi = a.astype(BF16)
    lo = (a - hi.astype(F32)).astype(BF16)
    return _dot(hi, b) + _dot(lo, b)


def _sigmoid(x):
    return 1.0 / (1.0 + jnp.exp(-x))


def _rope_swap(x, lane):
    left = pltpu.roll(x, LANES - 16, axis=1)
    right = pltpu.roll(x, 16, axis=1)
    lo = (lane >= 64) & (lane < 80)
    hi = (lane >= 80) & (lane < 96)
    return jnp.where(lo, left, jnp.where(hi, right, 0.0))


def _two_level_gather(x_refs, out_refs, send_sems, recv_sems, local_sems):
    x, y, c = lax.axis_index("x"), lax.axis_index("y"), lax.axis_index("c")
    me, sibling = (x, y, c), (x, y, 1 - c)
    chips = [(1 - x, y), (x, 1 - y), (1 - x, 1 - y)]
    ops = range(len(x_refs))

    def slot(o, px, py, pc):
        return out_refs[o].at[4 * px + 2 * py + pc]

    def copy(o, k, block, to, src=None):
        return pltpu.make_async_remote_copy(
            src_ref=slot(o, *block) if src is None else src, dst_ref=slot(o, *block),
            send_sem=send_sems.at[o, k], recv_sem=recv_sems.at[o, k],
            device_id=to, device_id_type=MESH)

    def mine():
        return [pltpu.make_async_copy(x_refs[o], slot(o, *me), local_sems.at[o]) for o in ops]

    def first():
        return ([copy(o, 0, me, sibling, src=x_refs[o]) for o in ops]
                + [copy(o, 1 + j, me, (*chip, c), src=x_refs[o]) for j, chip in enumerate(chips) for o in ops])

    def start():
        for cp in mine() + first():
            cp.start()

    def finish():
        passed = []
        for j, chip in enumerate(chips):
            for o in ops:
                copy(o, 1 + j, (*chip, c), me).wait_recv()
                passed.append(copy(o, 4 + j, (*chip, c), sibling))
                passed[-1].start()
        for o in ops:
            copy(o, 0, sibling, me).wait_recv()
        for j, chip in enumerate(chips):
            for o in ops:
                copy(o, 4 + j, (*chip, 1 - c), me).wait_recv()
        for cp in first() + passed:
            cp.wait_send()
        for cp in mine():
            cp.wait()

    return start, finish


def _gather_sems(n_op):
    return [pltpu.SemaphoreType.DMA((n_op, 7)), pltpu.SemaphoreType.DMA((n_op, 7)),
            pltpu.SemaphoreType.DMA((n_op,))]


def _all_gather(shards):
    n_op = len(shards)

    def body(*refs):
        start, finish = _two_level_gather(refs[:n_op], refs[n_op:2 * n_op], *refs[2 * n_op:])
        start()
        finish()

    any_spec = pl.BlockSpec(memory_space=pl.ANY)
    return pl.pallas_call(
        body, name="weight_all_gather",
        out_shape=[jax.ShapeDtypeStruct((N_DEV,) + a.shape, a.dtype) for a in shards],
        in_specs=[any_spec] * n_op, out_specs=[any_spec] * n_op, scratch_shapes=_gather_sems(n_op),
        compiler_params=pltpu.CompilerParams(vmem_limit_bytes=4 << 20),
    )(*shards)


def _pair_copies(g_refs, l_refs, ssem, rsem):
    x, y, c = lax.axis_index("x"), lax.axis_index("y"), lax.axis_index("c")
    copies = []
    for o in range(len(g_refs)):
        for chip in range(4):
            copies.append(pltpu.make_async_remote_copy(
                src_ref=g_refs[o].at[2 * chip + (1 - c)], dst_ref=l_refs[o].at[chip],
                send_sem=ssem.at[o, chip], recv_sem=rsem.at[o, chip],
                device_id=(x, y, 1 - c), device_id_type=MESH))
    return copies


def _pair_specs(pays):
    n_op = len(pays)
    any_spec = pl.BlockSpec(memory_space=pl.ANY)
    return ([any_spec] * n_op, [any_spec] * n_op,
            [jax.ShapeDtypeStruct((4,) + a.shape[1:], F32) for a in pays],
            [pltpu.SemaphoreType.DMA((n_op, 4)), pltpu.SemaphoreType.DMA((n_op, 4))])


def _pair_exchange(pays, name):
    n_op = len(pays)
    in_specs, out_specs, out_shape, sems = _pair_specs(pays)

    def body(*refs):
        copies = _pair_copies(refs[:n_op], refs[n_op:2 * n_op], *refs[2 * n_op:])
        for cp in copies:
            cp.start()
        for cp in copies:
            cp.wait()

    return pl.pallas_call(body, name=name, out_shape=out_shape, in_specs=in_specs, out_specs=out_specs,
                          scratch_shapes=sems,
                          compiler_params=pltpu.CompilerParams(vmem_limit_bytes=4 << 20))(*pays)


def _slab_exchange(small):
    sr, n = small.shape

    def body(s_ref, sland_ref, ssem, rsem, lsem):
        x, y, c = lax.axis_index("x"), lax.axis_index("y"), lax.axis_index("c")
        me = 4 * x + 2 * y + c
        copies = []
        for k in range(1, N_DEV):
            peer = (1 - x if (k >> 2) & 1 else x, 1 - y if (k >> 1) & 1 else y, 1 - c if k & 1 else c)
            copies.append(pltpu.make_async_remote_copy(
                src_ref=s_ref, dst_ref=sland_ref.at[me], send_sem=ssem.at[k], recv_sem=rsem.at[k],
                device_id=peer, device_id_type=MESH))
        own = pltpu.make_async_copy(s_ref, sland_ref.at[me], lsem)
        own.start()
        for cp in copies:
            cp.start()
        for cp in copies:
            cp.wait()
        own.wait()

    any_spec = pl.BlockSpec(memory_space=pl.ANY)
    return pl.pallas_call(
        body, name="grad_slab_exchange", out_shape=jax.ShapeDtypeStruct((N_DEV, sr, n), F32),
        in_specs=[any_spec], out_specs=any_spec,
        scratch_shapes=[pltpu.SemaphoreType.DMA((N_DEV,)), pltpu.SemaphoreType.DMA((N_DEV,)),
                        pltpu.SemaphoreType.DMA],
        compiler_params=pltpu.CompilerParams(vmem_limit_bytes=4 << 20),
    )(small)


def _pair_sums(pays, landed, place, name):
    n = len(pays)
    dims = [p.shape[1:] for p in pays]

    def body(place_ref, *refs):
        g_refs, l_refs, s_refs, own_refs = refs[:n], refs[n:2 * n], refs[2 * n:3 * n], refs[3 * n:]
        i = pl.program_id(0)
        for o in range(n):
            tot = g_refs[o][...] + l_refs[o][...]
            s_refs[o][...] = tot.astype(BF16)

            @pl.when(i == place_ref[1])
            def _(o=o, tot=tot):
                own_refs[o][...] = tot

    grid_spec = pltpu.PrefetchScalarGridSpec(
        num_scalar_prefetch=1, grid=(4,),
        in_specs=[pl.BlockSpec((None, r, c), lambda i, pr: (2 * i + pr[0], 0, 0)) for r, c in dims]
        + [pl.BlockSpec((None, r, c), lambda i, pr: (i, 0, 0)) for r, c in dims],
        out_specs=[pl.BlockSpec((None, r, c), lambda i, pr: (i, 0, 0)) for r, c in dims]
        + [pl.BlockSpec((r, c), lambda i, pr: (0, 0)) for r, c in dims])
    out = pl.pallas_call(
        body, name=name, grid_spec=grid_spec,
        out_shape=[jax.ShapeDtypeStruct((4, r, c), BF16) for r, c in dims]
        + [jax.ShapeDtypeStruct((r, c), F32) for r, c in dims],
        compiler_params=_params(("arbitrary",), 16),
    )(place, *pays, *landed)
    return list(zip(out[:n], out[n:]))


def _chip_copies(s_refs, l_refs, ssem, rsem):
    x, y, c = lax.axis_index("x"), lax.axis_index("y"), lax.axis_index("c")
    copies = []
    for rel in range(1, 4):
        px = 1 - x if rel & 2 else x
        py = 1 - y if rel & 1 else y
        for o in range(len(s_refs)):
            copies.append(pltpu.make_async_remote_copy(
                src_ref=s_refs[o].at[2 * px + py], dst_ref=l_refs[o].at[rel - 1],
                send_sem=ssem.at[o, rel - 1], recv_sem=rsem.at[o, rel - 1],
                device_id=(px, py, c), device_id_type=MESH))
    return copies


def _chip_specs(sums):
    n_op = len(sums)
    any_spec = pl.BlockSpec(memory_space=pl.ANY)
    return ([any_spec] * n_op, [any_spec] * n_op,
            [jax.ShapeDtypeStruct((3,) + a.shape[1:], BF16) for a in sums],
            [pltpu.SemaphoreType.DMA((n_op, 3)), pltpu.SemaphoreType.DMA((n_op, 3))])


def _adamw_math(g, w, m, v):
    mn = ADAM_B1 * m + (1.0 - ADAM_B1) * g
    vn = ADAM_B2 * v + (1.0 - ADAM_B2) * (g * g)
    m_hat = mn / (1.0 - ADAM_B1 ** ADAM_STEP)
    v_hat = vn / (1.0 - ADAM_B2 ** ADAM_STEP)
    return -ADAM_LR * (m_hat / (jnp.sqrt(v_hat) + ADAM_EPS) + ADAM_WD * w), mn, vn


def _adamw_matrix(own, landed, w, m, v, name):
    _, r, c = w.shape
    cp = own.shape[1]
    br = min(r, 256)

    def body(own_ref, l_ref, w_ref, m_ref, v_ref, g_out, d_out, m_out, v_out):
        g = own_ref[...]
        for k in range(3):
            g = g + l_ref[k].astype(F32)
        g = g[:, :c]
        g_out[...] = g
        d_out[...], m_out[...], v_out[...] = _adamw_math(g, w_ref[...], m_ref[...], v_ref[...])

    row = pl.BlockSpec((None, br, c), lambda i: (0, i, 0))
    shp = jax.ShapeDtypeStruct((1, r, c), F32)
    return pl.pallas_call(
        body, name=name, grid=(r // br,),
        in_specs=[pl.BlockSpec((br, cp), lambda i: (i, 0)), pl.BlockSpec((3, br, cp), lambda i: (0, i, 0)),
                  row, row, row],
        out_specs=(row, row, row, row), out_shape=(shp, shp, shp, shp),
        compiler_params=_params(("parallel",), 12),
    )(own, landed, w, m, v)


_VEC_PLACE = ((0, 0), (1, 0), (2, 0), (3, 0), (3, D_GRP), (4, 0), (5, 0), (6, 0))


def _adamw_vectors(sland, ws, ms, vs):
    nv = len(ws)

    def body(l_ref, *refs):
        w_refs, m_refs, v_refs = refs[:nv], refs[nv:2 * nv], refs[2 * nv:3 * nv]
        loss_ref = refs[3 * nv]
        outs = refs[3 * nv + 1:]
        g_all = l_ref[0]
        for j in range(1, N_DEV):
            g_all = g_all + l_ref[j]
        loss_ref[...] = jnp.sum(g_all[7:8, :], axis=1, keepdims=True)
        for k, (row, lane0) in enumerate(_VEC_PLACE):
            n = w_refs[k].shape[1]
            g = g_all[row:row + 1, lane0:lane0 + n]
            d, mn, vn = _adamw_math(g, w_refs[k][...], m_refs[k][...], v_refs[k][...])
            outs[k][...] = g
            outs[nv + k][...] = d
            outs[2 * nv + k][...] = mn
            outs[3 * nv + k][...] = vn

    def whole(shape):
        return pl.BlockSpec(shape, lambda i: (0,) * len(shape))

    shapes = [jax.ShapeDtypeStruct(w.shape, F32) for w in ws]
    return pl.pallas_call(
        body, name="adamw_vectors", grid=(1,),
        in_specs=[whole(sland.shape)] + [whole(w.shape) for w in ws] * 3,
        out_specs=[whole((1, 1))] + [whole(w.shape) for w in ws] * 4,
        out_shape=[jax.ShapeDtypeStruct((1, 1), F32)] + shapes * 4,
        compiler_params=_params(("arbitrary",), 4),
    )(sland, *ws, *ms, *vs)


def _in_proj(x, g, w, shards):
    s = x.shape[0]
    n_op = len(shards)
    steps = s // TM_IO

    def body(x_ref, g_ref, w_ref, *refs):
        shard_refs = refs[:n_op]
        qkv_ref, rest_ref, h_ref = refs[n_op:n_op + 3]
        gath_refs = refs[n_op + 3:2 * n_op + 3]
        start, finish = _two_level_gather(shard_refs, gath_refs, *refs[2 * n_op + 3:])
        i = pl.program_id(0)

        @pl.when(i == 0)
        def _():
            start()

        xv = x_ref[...]
        r = lax.rsqrt(jnp.mean(xv * xv, axis=-1, keepdims=True) + EPS)
        h = ((xv * r) * g_ref[...]).astype(BF16)
        h_ref[...] = h
        qkv_ref[...] = _dot(h, w_ref[:, :1536]).astype(BF16)
        rest_ref[...] = _dot(h, w_ref[:, 1536:])

        @pl.when(i == steps - 1)
        def _():
            finish()

    any_spec = pl.BlockSpec(memory_space=pl.ANY)
    return pl.pallas_call(
        body, name="in_proj", grid=(steps,),
        in_specs=[pl.BlockSpec((TM_IO, D_MODEL), lambda i: (i, 0)),
                  pl.BlockSpec((1, D_MODEL), lambda i: (0, 0)),
                  pl.BlockSpec((D_MODEL, D_IN_P), lambda i: (0, 0))] + [any_spec] * n_op,
        out_specs=[pl.BlockSpec((TM_IO, 1536), lambda i: (i, 0)),
                   pl.BlockSpec((TM_IO, 1536), lambda i: (i, 0)),
                   pl.BlockSpec((TM_IO, D_MODEL), lambda i: (i, 0))] + [any_spec] * n_op,
        out_shape=[jax.ShapeDtypeStruct((s, 1536), BF16), jax.ShapeDtypeStruct((s, 1536), F32),
                   jax.ShapeDtypeStruct((s, D_MODEL), BF16)]
        + [jax.ShapeDtypeStruct((N_DEV,) + a.shape, a.dtype) for a in shards],
        scratch_shapes=_gather_sems(n_op),
        compiler_params=_params(("arbitrary",), 32),
    )(x, g, w, *shards)


def _mla_prep(rest, gq, gkv, wuq, wuk, wuv, cos_t, sin_t):
    s = rest.shape[0]

    def body(cq_ref, ckv_ref, kr_ref, gq_ref, gkv_ref, wuq_ref, wuk_ref, wuv_ref, c_ref, s_ref,
             qp_ref, kp_ref, vv_ref, cqn_ref, ckvn_ref):
        lane = lax.broadcasted_iota(jnp.int32, (1, LANES), 1)
        cos_v, sin_v = c_ref[...], s_ref[...]
        cq = cq_ref[...]
        rq = lax.rsqrt(jnp.mean(cq * cq, axis=-1, keepdims=True) + EPS)
        cqn = ((cq * rq) * gq_ref[...]).astype(BF16)
        cqn_ref[...] = cqn
        q = _dot(cqn, wuq_ref[...])
        ckv = ckv_ref[...]
        rkv = lax.rsqrt(jnp.mean(ckv * ckv, axis=-1, keepdims=True) + EPS)
        ckvn = ((ckv * rkv) * gkv_ref[...]).astype(BF16)
        ckvn_ref[...] = ckvn
        kn = _dot(ckvn, wuk_ref[...])
        vv_ref[...] = _dot(ckvn, wuv_ref[...]).astype(BF16)
        kr = kr_ref[...]
        kr_roped = kr * cos_v + _rope_swap(kr, lane) * sin_v
        for h in range(N_HEADS):
            sl = slice(h * LANES, (h + 1) * LANES)
            qh = q[:, sl]
            qp_ref[:, sl] = (qh * cos_v + _rope_swap(qh, lane) * sin_v).astype(BF16)
            kp_ref[:, sl] = (kn[:, sl] + kr_roped).astype(BF16)

    def row(width, idx):
        return pl.BlockSpec((TM, width), lambda i: (i, idx))

    def full(a):
        return pl.BlockSpec(a.shape, lambda i: (0, 0))

    return pl.pallas_call(
        body, name="mla_prep", grid=(s // TM,),
        in_specs=[row(Q_LORA, 4), row(KV_LORA, 10), row(LANES, 11), full(gq), full(gkv),
                  full(wuq), full(wuk), full(wuv), row(LANES, 0), row(LANES, 0)],
        out_specs=(row(1024, 0), row(1024, 0), row(D_GRP, 0), row(Q_LORA, 0), row(KV_LORA, 0)),
        out_shape=(jax.ShapeDtypeStruct((s, 1024), BF16), jax.ShapeDtypeStruct((s, 1024), BF16),
                   jax.ShapeDtypeStruct((s, D_GRP), BF16), jax.ShapeDtypeStruct((s, Q_LORA), BF16),
                   jax.ShapeDtypeStruct((s, KV_LORA), BF16)),
        compiler_params=_params(("parallel",), 12),
    )(*_hbm(rest, rest, rest), gq, gkv, wuq, wuk, wuv, cos_t, sin_t)


def _sb_live(n, qi, carries):
    top = carries[0]
    for c in carries[1:]:
        top = jnp.maximum(top, c)
    return jnp.logical_and(n < qi, jnp.max(top) > -SB_CUTOFF)


def _sb_fwd(qkv, hb):
    s = qkv.shape[0]

    def body(q_ref, k_ref, v_ref, o_ref, acc):
        qi = pl.program_id(1)
        lane = lax.broadcasted_iota(jnp.int32, (1, LANES), 1)
        is_a = lane < HEAD_DIM
        pair = lambda h: slice((h // 2) * LANES, (h // 2 + 1) * LANES)
        q_h = []
        for h in range(hb):
            qs = q_ref[:, pair(h)] * SB_SCALE
            mine = is_a if h % 2 == 0 else jnp.logical_not(is_a)
            q_h.append(jnp.where(mine, qs, jnp.zeros_like(qs)))
        r_i = lax.broadcasted_iota(jnp.int32, (TQ, TK), 0)
        c_i = lax.broadcasted_iota(jnp.int32, (TQ, TK), 1)
        past = c_i < r_i
        upper = (r_i > c_i).astype(BF16)
        acc[...] = jnp.zeros_like(acc)

        def tile(j, carries, diag):
            ks = pl.ds(pl.multiple_of(j * TK, TK), TK)
            zs = [_dot_nt(q_h[h], k_ref[ks, pair(h)]) for h in range(hb)]
            if diag:
                zs = [jnp.where(past, z, NEG) for z in zs]
            lfs = [-(jnp.maximum(z, 0.0) + jnp.log(1.0 + jnp.exp(-jnp.abs(z)))) for z in zs]
            sufs = [_hl_dot(lfs[h], upper) for h in range(hb)]
            out = []
            for h in range(hb):
                w = jnp.exp(zs[h] + lfs[h] + (sufs[h] + carries[h]))
                acc[h] += _dot(w.astype(BF16), v_ref[ks, pair(h)])
                out.append(carries[h] + jnp.sum(lfs[h], axis=1, keepdims=True))
            return tuple(out)

        zero = jnp.zeros((TQ, 1), F32)
        carries = tile(qi, (zero,) * hb, True)

        def step(st):
            return (st[0] + 1,) + tile(qi - 1 - st[0], st[1:], False)

        lax.while_loop(lambda st: _sb_live(st[0], qi, st[1:]), step, (0,) + carries)
        for pr in range(hb // 2):
            o_ref[:, pr * LANES:(pr + 1) * LANES] = jnp.where(is_a, acc[2 * pr], acc[2 * pr + 1])

    width = hb * HEAD_DIM
    nb = D_GRP // width
    slab = lambda part: pl.BlockSpec((s, width), lambda g, qi: (0, part * nb + g))
    blk = pl.BlockSpec((TQ, width), lambda g, qi: (qi, g))
    return pl.pallas_call(
        body, name="sb_fwd", grid=(nb, s // TQ),
        in_specs=[blk, slab(1), slab(2)], out_specs=blk,
        out_shape=jax.ShapeDtypeStruct((s, D_GRP), F32),
        scratch_shapes=[pltpu.VMEM((hb, TQ, LANES), F32)],
        compiler_params=_params(("arbitrary", "arbitrary"), 28),
    )(*_hbm(qkv, qkv, qkv))


def _sb_bwd(qkv, d_o, sums):
    s = qkv.shape[0]
    nq = s // TQ
    nk = s // TK
    n_op = len(sums)
    ride_in, ride_out, ride_shape, ride_sems = _chip_specs(sums)

    def body(q_ref, k_ref, v_ref, do_ref, *refs):
        s_refs = refs[:n_op]
        dq_ref, dk_ref, dv_ref = refs[n_op:n_op + 3]
        l_refs = refs[n_op + 3:2 * n_op + 3]
        x1s, bts, dqacc, dkacc, dvacc, ssem, rsem = refs[2 * n_op + 3:]
        qi = pl.program_id(1)
        first_step = jnp.logical_and(pl.program_id(0) == 0, qi == 0)
        last_step = jnp.logical_and(pl.program_id(0) == pl.num_programs(0) - 1, qi == nq - 1)

        @pl.when(first_step)
        def _():
            for cp in _chip_copies(s_refs, l_refs, ssem, rsem):
                cp.start()

        lane = lax.broadcasted_iota(jnp.int32, (1, LANES), 1)
        is_a = lane < HEAD_DIM

        @pl.when(qi == 0)
        def _():
            dkacc[...] = jnp.zeros_like(dkacc)
            dvacc[...] = jnp.zeros_like(dvacc)

        qs = q_ref[...] * SB_SCALE
        zq = jnp.zeros_like(qs)
        qs_x = (jnp.where(is_a, qs, zq), jnp.where(is_a, zq, qs))
        dob = do_ref[...].astype(BF16)
        do_x = (jnp.where(is_a, dob, zq), jnp.where(is_a, zq, dob))
        r_i = lax.broadcasted_iota(jnp.int32, (TQ, TK), 0)
        c_i = lax.broadcasted_iota(jnp.int32, (TQ, TK), 1)
        past = c_i < r_i
        upper = (r_i > c_i).astype(BF16)
        upper_incl = (r_i >= c_i).astype(BF16)
        dqacc[...] = jnp.zeros_like(dqacc)
        both = ((0, 0), (0, 1), (1, 0), (1, 1))

        def tiles(n):
            j_hi = qi - 2 * n
            lo_ok = j_hi >= 1
            j_lo = jnp.maximum(j_hi - 1, 0)
            ks = (pl.ds(pl.multiple_of(j_hi * TK, TK), TK), pl.ds(pl.multiple_of(j_lo * TK, TK), TK))
            return j_hi, lo_ok, j_lo, ks

        def sweep(n, carries):
            j_hi, lo_ok, j_lo, ks = tiles(n)
            slot = (j_hi, jnp.where(lo_ok, j_lo, nk))
            valid = (jnp.logical_or(past, j_hi < qi), lo_ok)
            z = {th: jnp.where(valid[th[0]], _dot_nt(qs_x[th[1]], k_ref[ks[th[0]], :]), NEG) for th in both}
            log_b, lf_sum, suf = {}, {}, {}
            for th in both:
                lf = -(jnp.maximum(z[th], 0.0) + jnp.log(1.0 + jnp.exp(-jnp.abs(z[th]))))
                log_b[th] = z[th] + lf
                lf_sum[th] = jnp.sum(lf, axis=1, keepdims=True)
                suf[th] = _hl_dot(lf, upper)
            c, g_in = {}, {}
            for h in range(2):
                c[0, h], g_in[0, h] = carries[2 * h], carries[2 * h + 1]
                c[1, h] = c[0, h] + lf_sum[0, h]
            d_a = {th: _dot_nt(do_x[th[1]], v_ref[ks[th[0]], :]) for th in both}
            a_b, g, g_sum, sg = {}, {}, {}, {}
            for th in both:
                a = jnp.exp(log_b[th] + (suf[th] + c[th]))
                a_b[th] = a.astype(BF16)
                g[th] = a * d_a[th]
                g_sum[th] = jnp.sum(g[th], axis=1, keepdims=True)
                sg[th] = _hl_dot(g[th], upper_incl)
            for h in range(2):
                g_in[1, h] = g_in[0, h] + g_sum[0, h]
            for th in both:
                t, h = th
                beta = jnp.exp(log_b[th])
                x1s[slot[t], h] = g[th] * (1.0 - beta) + beta * (sg[th] + g_in[th])
                bts[slot[t], h] = beta
                dvacc[ks[t], :] += _dot_tn(a_b[th], do_x[h])
            out = []
            for h in range(2):
                out.append(c[1, h] + lf_sum[1, h])
                out.append(g_in[1, h] + g_sum[1, h])
            return tuple(out)

        zero = jnp.zeros((TQ, 1), F32)
        first = sweep(0, (zero, zero, zero, zero))

        def more(st):
            return jnp.logical_and(2 * st[0] <= qi, jnp.max(jnp.maximum(st[1], st[3])) > -SB_CUTOFF)

        swept = lax.while_loop(more, lambda st: (st[0] + 1,) + sweep(st[0], st[1:]), (1,) + first)
        g_tot = (swept[2], swept[4])

        def apply(n, carry):
            j_hi, lo_ok, j_lo, ks = tiles(n)

            def one(j, kslice):
                for h in range(2):
                    dz = (x1s[j, h] - bts[j, h] * g_tot[h]).astype(BF16)
                    dqacc[h] += _dot(dz, k_ref[kslice, :])
                    dkacc[kslice, :] += _dot_tn(dz, qs_x[h])

            one(j_hi, ks[0])

            @pl.when(lo_ok)
            def _():
                one(j_lo, ks[1])

            return carry

        lax.fori_loop(0, swept[0], apply, 0)
        dq_ref[...] = (jnp.where(is_a, dqacc[0], dqacc[1]) * SB_SCALE).astype(BF16)

        @pl.when(qi == nq - 1)
        def _():
            dk_ref[...] = dkacc[...].astype(BF16)
            dv_ref[...] = dvacc[...].astype(BF16)

        @pl.when(last_step)
        def _():
            for cp in _chip_copies(s_refs, l_refs, ssem, rsem):
                cp.wait()

    slab = lambda off: pl.BlockSpec((s, LANES), lambda p, qi: (0, off + p))
    blk = pl.BlockSpec((TQ, LANES), lambda p, qi: (qi, p))
    out_slab = pl.BlockSpec((s, LANES), lambda p, qi: (0, p))
    shp = jax.ShapeDtypeStruct((s, D_GRP), BF16)
    return pl.pallas_call(
        body, name="sb_bwd", grid=(4, nq),
        in_specs=[blk, slab(4), slab(8), blk] + ride_in,
        out_specs=[blk, out_slab, out_slab] + ride_out, out_shape=[shp, shp, shp] + ride_shape,
        scratch_shapes=[pltpu.VMEM((nk + 1, 2, TQ, TK), F32)] * 2
        + [pltpu.VMEM((2, TQ, LANES), F32), pltpu.VMEM((s, LANES), F32), pltpu.VMEM((s, LANES), F32)]
        + ride_sems,
        compiler_params=_params(("arbitrary", "arbitrary"), 44),
    )(*_hbm(qkv, qkv, qkv, d_o), *sums)


def _mla_fwd(qp, kp, vv, hb):
    s = qp.shape[0]
    c2 = MLA_SCALE * LOG2_E

    def body(q_ref, k_ref, v_ref, o_ref, lse_ref, vaug, mrun, mb, acc, zbuf):
        qi = pl.program_id(1)
        lane = lax.broadcasted_iota(jnp.int32, (1, LANES), 1)
        is_a = lane < HEAD_DIM

        @pl.when(qi == 0)
        def _():
            for h in range(hb):
                vp = v_ref[:, (h // 2) * LANES:(h // 2 + 1) * LANES]
                mine = is_a if h % 2 == 0 else jnp.logical_not(is_a)
                vaug[h] = jnp.where(mine, vp, jnp.ones_like(vp))

        r_i = lax.broadcasted_iota(jnp.int32, (TQ, TK), 0)
        c_i = lax.broadcasted_iota(jnp.int32, (TQ, TK), 1)
        visible = (c_i >> CHUNK_SHIFT) <= (r_i >> CHUNK_SHIFT)

        def key_rows(j):
            return pl.ds(pl.multiple_of(j * TK, TK), TK)

        def sweep(tiles):
            def loop(n, carry):
                tiles(((2 * n, False), (2 * n + 1, False)))
                return carry

            lax.fori_loop(0, qi // 2, loop, 0)

            @pl.when(qi % 2 == 1)
            def _():
                tiles(((qi - 1, False), (qi, True)))

            @pl.when(qi % 2 == 0)
            def _():
                tiles(((qi, True),))

        mrun[...] = jnp.full_like(mrun, NEG)

        def tiles_max(js):
            zs = [[_dot_nt(q_ref[:, h * LANES:(h + 1) * LANES], k_ref[key_rows(j), h * LANES:(h + 1) * LANES])
                   for h in range(hb)] for j, _ in js]
            for t, (j, diag) in enumerate(js):
                for h in range(hb):
                    z = jnp.where(visible, zs[t][h], NEG) if diag else zs[t][h]
                    zbuf[j, h] = z
                    mrun[h] = jnp.maximum(mrun[h], z)

        sweep(tiles_max)
        for h in range(hb):
            m = jnp.max(mrun[h], axis=1, keepdims=True) * c2
            mb[h] = jnp.broadcast_to(m, (TQ, TK))
        acc[...] = jnp.zeros_like(acc)

        def tiles_pv(js):
            ps = [[jnp.exp2((zbuf[j, h] * c2 - mb[h]).astype(BF16)) for h in range(hb)] for j, _ in js]
            for t, (j, _) in enumerate(js):
                for h in range(hb):
                    acc[h] += _dot(ps[t][h], vaug[h, key_rows(j), :])

        sweep(tiles_pv)
        for pr in range(hb // 2):
            a, b = 2 * pr, 2 * pr + 1
            psl = slice(pr * LANES, (pr + 1) * LANES)
            acc_a, acc_b = acc[a], acc[b]
            l_a = pltpu.roll(acc_a, HEAD_DIM, axis=1)
            l_b = pltpu.roll(acc_b, HEAD_DIM, axis=1)
            o_ref[:, psl] = jnp.where(is_a, acc_a * (1.0 / l_a), acc_b * (1.0 / l_b))
            lse_ref[:, psl] = jnp.where(is_a, mb[a, :, :LANES] * LN_2 + jnp.log(l_a),
                                        mb[b, :, :LANES] * LN_2 + jnp.log(l_b))

    blk = pl.BlockSpec((TQ, hb * HEAD_DIM), lambda g, qi: (qi, g))
    shp = jax.ShapeDtypeStruct((s, D_GRP), F32)
    return pl.pallas_call(
        body, name="mla_fwd", grid=(N_HEADS // hb, s // TQ),
        in_specs=[pl.BlockSpec((TQ, hb * LANES), lambda g, qi: (qi, g)),
                  pl.BlockSpec((s, hb * LANES), lambda g, qi: (0, g)),
                  pl.BlockSpec((s, hb * HEAD_DIM), lambda g, qi: (0, g))],
        out_specs=(blk, blk), out_shape=(shp, shp),
        scratch_shapes=[pltpu.VMEM((hb, s, LANES), BF16), pltpu.VMEM((hb, TQ, TK), F32),
                        pltpu.VMEM((hb, TQ, TK), F32), pltpu.VMEM((hb, TQ, LANES), F32),
                        pltpu.VMEM((s // TK, hb, TQ, TK), F32)],
        compiler_params=_params(("arbitrary", "arbitrary"), 44),
    )(*_hbm(qp, kp, vv))


def _mla_bwd(qp, kp, vv, d_o, o, lse, hb, pays):
    s = qp.shape[0]
    nq = s // TQ
    c2 = MLA_SCALE * LOG2_E
    n_op = len(pays)
    ride_in, ride_out, ride_shape, ride_sems = _pair_specs(pays)

    def body(q_ref, k_ref, v_ref, do_ref, o_ref, lse_ref, *refs):
        g_refs = refs[:n_op]
        dq_ref, dk_ref, dv_ref = refs[n_op:n_op + 3]
        l_refs = refs[n_op + 3:2 * n_op + 3]
        dqacc, lse_b, delta_b, ssem, rsem = refs[2 * n_op + 3:]
        qi = pl.program_id(1)

        @pl.when(jnp.logical_and(pl.program_id(0) == 0, qi == 0))
        def _():
            for cp in _pair_copies(g_refs, l_refs, ssem, rsem):
                cp.start()

        lane = lax.broadcasted_iota(jnp.int32, (1, LANES), 1)
        is_a = lane < HEAD_DIM

        @pl.when(qi == 0)
        def _():
            dk_ref[...] = jnp.zeros_like(dk_ref)
            dv_ref[...] = jnp.zeros_like(dv_ref)

        r_i = lax.broadcasted_iota(jnp.int32, (TQ, TK), 0)
        c_i = lax.broadcasted_iota(jnp.int32, (TQ, TK), 1)
        visible = (c_i >> CHUNK_SHIFT) <= (r_i >> CHUNK_SHIFT)
        do_x = []
        for h in range(hb):
            psl = slice((h // 2) * LANES, (h // 2 + 1) * LANES)
            mine = is_a if h % 2 == 0 else jnp.logical_not(is_a)
            d_o = do_ref[:, psl]
            delta = jnp.sum(jnp.where(mine, d_o * o_ref[:, psl], 0.0), axis=1, keepdims=True)
            lse_h = jnp.sum(jnp.where(lane == (h % 2) * HEAD_DIM, lse_ref[:, psl], 0.0), axis=1, keepdims=True)
            lse_b[h] = jnp.broadcast_to(lse_h * LOG2_E, (TQ, TK))
            delta_b[h] = jnp.broadcast_to(delta, (TQ, TK))
            do_x.append(jnp.where(mine, d_o, 0.0).astype(BF16))
        dqacc[...] = jnp.zeros_like(dqacc)

        head = lambda h: slice(h * LANES, (h + 1) * LANES)
        pair = lambda h: slice((h // 2) * LANES, (h // 2 + 1) * LANES)

        def tiles(js):
            th = [(j, diag, pl.ds(pl.multiple_of(j * TK, TK), TK), h) for j, diag in js for h in range(hb)]
            zs = [_dot_nt(q_ref[:, head(h)], k_ref[ks, head(h)]) for _, _, ks, h in th]
            dps = [_dot_nt(do_x[h], v_ref[ks, pair(h)]) for _, _, ks, h in th]
            for i, (j, diag, ks, h) in enumerate(th):
                e = zs[i] * c2 - lse_b[h]
                if diag:
                    e = jnp.where(visible, e, NEG)
                p = jnp.exp2(e)
                ds = (p * (dps[i] - delta_b[h]) * MLA_SCALE).astype(BF16)
                dqacc[h] += _dot(ds, k_ref[ks, head(h)])
                dk_ref[ks, head(h)] += _dot_tn(ds, q_ref[:, head(h)])
                dv_ref[ks, pair(h)] += _dot_tn(p.astype(BF16), do_x[h])

        def loop(n, c):
            tiles(((2 * n, False), (2 * n + 1, False)))
            return c

        lax.fori_loop(0, qi // 2, loop, 0)

        @pl.when(qi % 2 == 1)
        def _():
            tiles(((qi - 1, False), (qi, True)))

        @pl.when(qi % 2 == 0)
        def _():
            tiles(((qi, True),))

        for h in range(hb):
            dq_ref[:, h * LANES:(h + 1) * LANES] = dqacc[h]

        @pl.when(jnp.logical_and(pl.program_id(0) == pl.num_programs(0) - 1, qi == nq - 1))
        def _():
            for cp in _pair_copies(g_refs, l_refs, ssem, rsem):
                cp.wait()

    blk = pl.BlockSpec((TQ, hb * HEAD_DIM), lambda g, qi: (qi, g))
    return pl.pallas_call(
        body, name="mla_bwd", grid=(N_HEADS // hb, nq),
        in_specs=[pl.BlockSpec((TQ, hb * LANES), lambda g, qi: (qi, g)),
                  pl.BlockSpec((s, hb * LANES), lambda g, qi: (0, g)),
                  pl.BlockSpec((s, hb * HEAD_DIM), lambda g, qi: (0, g)), blk, blk, blk] + ride_in,
        out_specs=[pl.BlockSpec((TQ, hb * LANES), lambda g, qi: (qi, g)),
                   pl.BlockSpec((s, hb * LANES), lambda g, qi: (0, g)),
                   pl.BlockSpec((s, hb * HEAD_DIM), lambda g, qi: (0, g))] + ride_out,
        out_shape=[jax.ShapeDtypeStruct((s, 1024), F32), jax.ShapeDtypeStruct((s, 1024), F32),
                   jax.ShapeDtypeStruct((s, D_GRP), F32)] + ride_shape,
        scratch_shapes=[pltpu.VMEM((hb, TQ, LANES), F32), pltpu.VMEM((hb, TQ, TK), F32),
                        pltpu.VMEM((hb, TQ, TK), F32)] + ride_sems,
        compiler_params=_params(("arbitrary", "arbitrary"), 52),
    )(*_hbm(qp, kp, vv, d_o, o, lse), *pays)


def _mid(x, p, target, sb_o, mla_o, rest, g_sb, g_mla, w_out, g_post, w_ple, g_ple, w_pg, b_pg, bd):
    s = x.shape[0]

    def body(x_ref, p_ref, t_ref, sbo_ref, mlo_ref, sbg_ref, mlg_ref, gsb_ref, gml_ref, wout_ref,
             gpost_ref, wple_ref, gple_ref, wpg_ref, bpg_ref, bd_ref,
             dx1_ref, dsbo_ref, dmlo_ref, dsbg_ref, dmlg_ref, x1b_ref, dglb_ref, ycb_ref, dyb_ref,
             pb_ref, dub_ref, small_ref):
        i = pl.program_id(0)
        bd_m = bd_ref[...]

        def seg_mean(v):
            return _dot(v.astype(BF16), bd_m) * (1.0 / HEAD_DIM)

        groups = []
        for o_ref, gate_ref, gain_ref in ((sbo_ref, sbg_ref, gsb_ref), (mlo_ref, mlg_ref, gml_ref)):
            o = o_ref[...]
            r = lax.rsqrt(seg_mean(o * o) + EPS)
            n = o * r
            hn = n * gain_ref[...]
            gate = gate_ref[...]
            sg = _sigmoid(gate)
            si = gate * sg
            groups.append((r, n, hn, gate, sg, si, gain_ref[...]))
        ya = (groups[0][2] * groups[0][5]).astype(BF16)
        yb = (groups[1][2] * groups[1][5]).astype(BF16)
        ycb_ref[:, :D_GRP] = ya
        ycb_ref[:, D_GRP:] = yb
        y = _dot(ya, wout_ref[:D_GRP, :]) + _dot(yb, wout_ref[D_GRP:, :])
        ry = lax.rsqrt(jnp.mean(y * y, axis=-1, keepdims=True) + EPS)
        ny = y * ry
        x1 = x_ref[...] + ny * gpost_ref[...]
        x1b = x1.astype(BF16)
        x1b_ref[...] = x1b
        pb = p_ref[...].astype(BF16)
        pb_ref[...] = pb
        u = _dot(pb, wple_ref[...])
        ru = lax.rsqrt(jnp.mean(u * u, axis=-1, keepdims=True) + EPS)
        nu = u * ru
        ple = nu * gple_ref[...]
        gate = _sigmoid(_dot(x1b, wpg_ref[...]) + bpg_ref[...])
        x2 = x1 + ple * gate
        diff = x2 - t_ref[...]
        dx2 = diff * (1.0 / D_MODEL)

        d_ple = dx2 * gate
        d_glin = (dx2 * ple) * (gate * (1.0 - gate))
        dglb = d_glin.astype(BF16)
        dglb_ref[...] = dglb
        dx1 = dx2 + _dot_nt(dglb, wpg_ref[...])
        dx1_ref[...] = dx1
        d_nu = d_ple * gple_ref[...]
        d_u = ru * (d_nu - nu * jnp.mean(d_nu * nu, axis=-1, keepdims=True))
        dub_ref[...] = d_u.astype(BF16)
        d_ny = dx1 * gpost_ref[...]
        d_y = ry * (d_ny - ny * jnp.mean(d_ny * ny, axis=-1, keepdims=True))
        dyb = d_y.astype(BF16)
        dyb_ref[...] = dyb
        d_yc = (_dot_nt(dyb, wout_ref[:D_GRP, :]), _dot_nt(dyb, wout_ref[D_GRP:, :]))

        d_gain = []
        for gx, (do_ref, dg_ref) in enumerate(((dsbo_ref, dsbg_ref), (dmlo_ref, dmlg_ref))):
            r, n, hn, gate_g, sg, si, gain = groups[gx]
            dyg = d_yc[gx]
            d_hn = dyg * si
            dg_ref[...] = (dyg * hn * (sg * (1.0 + gate_g * (1.0 - sg)))).astype(BF16)
            d_gain.append(jnp.sum(d_hn * n, axis=0, keepdims=True))
            d_n = d_hn * gain
            do_ref[...] = r * (d_n - n * seg_mean(d_n * n))

        @pl.when(i == 0)
        def _():
            small_ref[...] = jnp.zeros_like(small_ref)

        small_ref[3:4, :D_GRP] += d_gain[0]
        small_ref[3:4, D_GRP:] += d_gain[1]
        small_ref[4:5, :] += jnp.sum(dx1 * ny, axis=0, keepdims=True)
        small_ref[5:6, :] += jnp.sum(d_ple * nu, axis=0, keepdims=True)
        small_ref[6:7, :] += jnp.sum(d_glin, axis=0, keepdims=True)
        small_ref[7:8, :] += jnp.sum(diff * diff, axis=0, keepdims=True) * (0.5 / D_MODEL)

    def row(width, idx=0):
        return pl.BlockSpec((TM, width), lambda i: (i, idx))

    def full(a):
        return pl.BlockSpec(a.shape, lambda i: (0, 0))

    f32 = lambda w: jax.ShapeDtypeStruct((s, w), F32)
    b16 = lambda w: jax.ShapeDtypeStruct((s, w), BF16)
    return pl.pallas_call(
        body, name="mid", grid=(s // TM,),
        in_specs=[row(D_MODEL), row(PLE_DIM), row(D_MODEL), row(D_GRP), row(D_GRP),
                  row(D_GRP, 0), row(D_GRP, 1), full(g_sb), full(g_mla), full(w_out), full(g_post),
                  full(w_ple), full(g_ple), full(w_pg), full(b_pg), full(bd)],
        out_specs=(row(D_MODEL), row(D_GRP), row(D_GRP), row(D_GRP), row(D_GRP), row(D_MODEL),
                   row(D_MODEL), row(D_MODEL), row(D_MODEL), row(PLE_DIM), row(D_MODEL),
                   pl.BlockSpec((8, D_MODEL), lambda i: (0, 0))),
        out_shape=(f32(D_MODEL), f32(D_GRP), f32(D_GRP), b16(D_GRP), b16(D_GRP), b16(D_MODEL),
                   b16(D_MODEL), b16(D_MODEL), b16(D_MODEL), b16(PLE_DIM), b16(D_MODEL),
                   jax.ShapeDtypeStruct((8, D_MODEL), F32)),
        compiler_params=_params(("arbitrary",), 46),
    )(*_hbm(x, p, target, sb_o, mla_o, rest, rest), g_sb, g_mla, w_out, g_post, w_ple, g_ple, w_pg, b_pg, bd)


def _mla_prep_bwd(dqp, dkp, dvv, rest, gq, gkv, wuq, wuk, wuv, cos_t, sin_t):
    s = rest.shape[0]

    def body(dqp_ref, dkp_ref, dvv_ref, cq_ref, ckv_ref, gq_ref, gkv_ref, wuq_ref, wuk_ref, wuv_ref,
             c_ref, s_ref, dcq_ref, dckv_ref, dkr_ref, dqb_ref, dkb_ref, dvb_ref, small_ref):
        i = pl.program_id(0)
        lane = lax.broadcasted_iota(jnp.int32, (1, LANES), 1)
        in_rope = (lane >= HEAD_DIM) & (lane < HEAD_DIM + ROPE_DIM)
        cos_v, sin_v = c_ref[...], s_ref[...]
        dkr_roped = jnp.zeros((TM, LANES), F32)
        for h in range(N_HEADS):
            sl = slice(h * LANES, (h + 1) * LANES)
            dy = dqp_ref[:, sl]
            dqb_ref[:, sl] = (dy * cos_v + _rope_swap(dy * sin_v, lane)).astype(BF16)
            dkh = dkp_ref[:, sl]
            dkb_ref[:, sl] = dkh.astype(BF16)
            dkr_roped = dkr_roped + jnp.where(in_rope, dkh, 0.0)
        dkr_ref[...] = (dkr_roped * cos_v + _rope_swap(dkr_roped * sin_v, lane)).astype(BF16)
        dvb = dvv_ref[...].astype(BF16)
        dvb_ref[...] = dvb

        cq = cq_ref[...]
        rq = lax.rsqrt(jnp.mean(cq * cq, axis=-1, keepdims=True) + EPS)
        nq_ = cq * rq
        d_cqn = _dot_nt(dqb_ref[...], wuq_ref[...])
        d_n = d_cqn * gq_ref[...]
        dcq_ref[...] = (rq * (d_n - nq_ * jnp.mean(d_n * nq_, axis=-1, keepdims=True))).astype(BF16)

        ckv = ckv_ref[...]
        rkv = lax.rsqrt(jnp.mean(ckv * ckv, axis=-1, keepdims=True) + EPS)
        nkv = ckv * rkv
        d_ckvn = _dot_nt(dkb_ref[...], wuk_ref[...]) + _dot_nt(dvb, wuv_ref[...])
        d_n2 = d_ckvn * gkv_ref[...]
        dckv_ref[...] = (rkv * (d_n2 - nkv * jnp.mean(d_n2 * nkv, axis=-1, keepdims=True))).astype(BF16)

        @pl.when(i == 0)
        def _():
            small_ref[...] = jnp.zeros_like(small_ref)

        small_ref[0:1, :] += jnp.sum(d_cqn * nq_, axis=0, keepdims=True)
        small_ref[1:2, :KV_LORA] += jnp.sum(d_ckvn * nkv, axis=0, keepdims=True)

    def row(width, idx=0):
        return pl.BlockSpec((TM, width), lambda i: (i, idx))

    def full(a):
        return pl.BlockSpec(a.shape, lambda i: (0, 0))

    b16 = lambda w: jax.ShapeDtypeStruct((s, w), BF16)
    return pl.pallas_call(
        body, name="mla_prep_bwd", grid=(s // TM,),
        in_specs=[row(1024), row(1024), row(D_GRP), row(Q_LORA, 4), row(KV_LORA, 10), full(gq), full(gkv),
                  full(wuq), full(wuk), full(wuv), row(LANES), row(LANES)],
        out_specs=(row(Q_LORA), row(KV_LORA), row(LANES), row(1024), row(1024), row(D_GRP),
                   pl.BlockSpec((8, Q_LORA), lambda i: (0, 0))),
        out_shape=(b16(Q_LORA), b16(KV_LORA), b16(LANES), b16(1024), b16(1024), b16(D_GRP),
                   jax.ShapeDtypeStruct((8, Q_LORA), F32)),
        compiler_params=_params(("arbitrary",), 16),
    )(*_hbm(dqp, dkp, dvv, rest, rest), gq, gkv, wuq, wuk, wuv, cos_t, sin_t)


def _in_bwd(x, g, dx1, pieces, w, sums):
    s = x.shape[0]
    steps = s // TM_IO
    widths = [a.shape[1] for a in pieces]
    offs = [sum(widths[:k]) for k in range(len(widths))]
    n_pc, n_op = len(pieces), len(sums)
    ride_in, ride_out, ride_shape, ride_sems = _chip_specs(sums)

    def body(x_ref, g_ref, dx1_ref, *refs):
        piece_refs = refs[:n_pc]
        w_ref = refs[n_pc]
        s_refs = refs[n_pc + 1:n_pc + 1 + n_op]
        dx_ref, small_ref = refs[n_pc + 1 + n_op:n_pc + 3 + n_op]
        l_refs = refs[n_pc + 3 + n_op:n_pc + 3 + 2 * n_op]
        ssem, rsem = refs[n_pc + 3 + 2 * n_op:]
        i = pl.program_id(0)

        @pl.when(i == 0)
        def _():
            for cp in _chip_copies(s_refs, l_refs, ssem, rsem):
                cp.start()

        dh = jnp.zeros((TM_IO, D_MODEL), F32)
        for pr, off, wd in zip(piece_refs, offs, widths):
            dh = dh + _dot_nt(pr[...], w_ref[:, off:off + wd])
        xv = x_ref[...]
        r = lax.rsqrt(jnp.mean(xv * xv, axis=-1, keepdims=True) + EPS)
        n = xv * r
        d_n = dh * g_ref[...]
        dx_ref[...] = dx1_ref[...] + r * (d_n - n * jnp.mean(d_n * n, axis=-1, keepdims=True))

        @pl.when(i == 0)
        def _():
            small_ref[...] = jnp.zeros_like(small_ref)

        small_ref[0:1, :] += jnp.sum(dh * n, axis=0, keepdims=True)

        @pl.when(i == steps - 1)
        def _():
            for cp in _chip_copies(s_refs, l_refs, ssem, rsem):
                cp.wait()

    def row(width):
        return pl.BlockSpec((TM_IO, width), lambda i: (i, 0))

    return pl.pallas_call(
        body, name="in_bwd", grid=(steps,),
        in_specs=[row(D_MODEL), pl.BlockSpec((1, D_MODEL), lambda i: (0, 0)), row(D_MODEL)]
        + [row(wd) for wd in widths] + [pl.BlockSpec(w.shape, lambda i: (0, 0))] + ride_in,
        out_specs=[row(D_MODEL), pl.BlockSpec((8, D_MODEL), lambda i: (0, 0))] + ride_out,
        out_shape=[jax.ShapeDtypeStruct((s, D_MODEL), F32), jax.ShapeDtypeStruct((8, D_MODEL), F32)]
        + ride_shape,
        scratch_shapes=ride_sems,
        compiler_params=_params(("arbitrary",), 40),
    )(*_hbm(x), g, *_hbm(dx1, *pieces), w, *sums)


def _tn_matmul(a, b, name, blocked=False):
    s, k = a.shape
    n = b.shape[1]
    ts = min(s, TS_DW)
    tn = n if blocked else min(n, 512)
    steps = s // ts

    def body(a_ref, b_ref, o_ref):
        t = pl.program_id(1)

        @pl.when(t == 0)
        def _():
            o_ref[...] = jnp.zeros_like(o_ref)

        prod = _dot_tn(a_ref[...], b_ref[...])
        if blocked:
            for j in range(n // LANES):
                o_ref[j] += prod[:, j * LANES:(j + 1) * LANES]
        else:
            o_ref[...] += prod

    if blocked:
        out_spec = pl.BlockSpec((n // LANES, k, LANES), lambda j, t: (0, 0, 0))
        out_shape = jax.ShapeDtypeStruct((n // LANES, k, LANES), F32)
    else:
        out_spec = pl.BlockSpec((k, tn), lambda j, t: (0, j))
        out_shape = jax.ShapeDtypeStruct((k, n), F32)
    return pl.pallas_call(
        body, name=name, grid=(n // tn, steps),
        in_specs=[pl.BlockSpec((ts, k), lambda j, t: (t, 0)), pl.BlockSpec((ts, tn), lambda j, t: (t, j))],
        out_specs=out_spec, out_shape=out_shape,
        compiler_params=_params(("parallel", "arbitrary"), 20),
    )(a, b)


def _tn_matmul_multi(a, bs, name):
    s, k = a.shape
    widths = [b.shape[1] for b in bs]
    ts = min(s, TS_DW)

    def body(a_ref, *refs):
        b_refs, o_ref = refs[:-1], refs[-1]
        t = pl.program_id(0)

        @pl.when(t == 0)
        def _():
            o_ref[...] = jnp.zeros_like(o_ref)

        av = a_ref[...]
        off = 0
        for b_ref, wd in zip(b_refs, widths):
            o_ref[:, off:off + wd] += _dot_tn(av, b_ref[...])
            off += wd

    return pl.pallas_call(
        body, name=name, grid=(s // ts,),
        in_specs=[pl.BlockSpec((ts, k), lambda t: (t, 0))] + [pl.BlockSpec((ts, wd), lambda t: (t, 0)) for wd in widths],
        out_specs=pl.BlockSpec((k, sum(widths)), lambda t: (0, 0)),
        out_shape=jax.ShapeDtypeStruct((k, sum(widths)), F32),
        compiler_params=_params(("arbitrary",), 30),
    )(a, *bs)


IN_SHARD = 372
_IN_KERNEL_ORDER = ((0, 2048), (2464, 2976), (2048, 2432))
_IN_ROPE = (2432, 2464)
_IN_GRAD_SRC = ((0, 512, 0, 0), (512, 1024, 0, 512), (1024, 1536, 1, 0), (1536, 2048, 1, 512),
                (2048, 2304, 2, 512), (2304, 2432, 2, 768), (2432, 2464, 2, 960), (2464, 2976, 2, 0))


def _shard_cols(gath_in, lo, hi):
    out = []
    while lo < hi:
        j, a = divmod(lo, IN_SHARD)
        b = min(IN_SHARD, a + hi - lo)
        out.append(gath_in[j][:, a:b])
        lo += b - a
    return out


def _kernel_w_in(g_in):
    zc = lambda n: jnp.zeros((D_MODEL, n), BF16)
    parts = [pc for lo, hi in _IN_KERNEL_ORDER for pc in _shard_cols(g_in, lo, hi)]
    parts += [zc(64)] + _shard_cols(g_in, *_IN_ROPE) + [zc(32)]
    return jnp.concatenate(parts, axis=1)


def _kernel_weights(gath):
    g_uq, g_ukv, g_out, g_ple, g_pg = gath
    w_uq_p = jnp.pad(g_uq, ((0, 0), (0, 0), (0, 32))).transpose(1, 0, 2).reshape(Q_LORA, 1024)
    k_only = jnp.where(jnp.arange(LANES) < HEAD_DIM, g_ukv, jnp.zeros_like(g_ukv))
    w_uk_p = k_only.transpose(1, 0, 2).reshape(KV_LORA, 1024)
    w_uv = g_ukv[:, :, HEAD_DIM:].transpose(1, 0, 2).reshape(KV_LORA, D_GRP)
    w_ple = g_ple.transpose(1, 0, 2).reshape(PLE_DIM, D_MODEL)
    return (w_uq_p, w_uk_p, w_uv, g_out.reshape(D_MODEL, D_MODEL), w_ple, g_pg.reshape(D_MODEL, D_MODEL))


def _payload_in(d_cols):
    blocks = []
    for j in range(N_DEV):
        lo, hi = j * IN_SHARD, (j + 1) * IN_SHARD
        parts = []
        for o_lo, o_hi, idx, off in _IN_GRAD_SRC:
            a, b = max(lo, o_lo), min(hi, o_hi)
            if a < b:
                parts.append(d_cols[idx][:, off + a - o_lo:off + b - o_lo])
        blocks.append(jnp.concatenate(parts, axis=1))
    return jnp.stack(blocks)


def _payload_ukv(duk_blk, d_uv):
    dv_blk = d_uv.reshape(KV_LORA, N_HEADS, HEAD_DIM).transpose(1, 0, 2)
    return jnp.concatenate([duk_blk[:, :, :HEAD_DIM], dv_blk], axis=2)


def kernel(x, p, positions, norm_pre_g, w_in, q_norm_g, w_uq, kv_norm_g, w_ukv, sb_out_norm_g, mla_out_norm_g, w_out, norm_post_g, w_ple, ple_norm_g, w_ple_gate, b_ple_gate, loss_target, m_norm_pre_g, m_w_in, m_q_norm_g, m_w_uq, m_kv_norm_g, m_w_ukv, m_sb_out_norm_g, m_mla_out_norm_g, m_w_out, m_norm_post_g, m_w_ple, m_ple_norm_g, m_w_ple_gate, m_b_ple_gate, v_norm_pre_g, v_w_in, v_q_norm_g, v_w_uq, v_kv_norm_g, v_w_ukv, v_sb_out_norm_g, v_mla_out_norm_g, v_w_out, v_norm_post_g, v_w_ple, v_ple_norm_g, v_w_ple_gate, v_b_ple_gate):
    mats = (w_in, w_uq, w_ukv, w_out, w_ple, w_ple_gate)
    m_mats = (m_w_in, m_w_uq, m_w_ukv, m_w_out, m_w_ple, m_w_ple_gate)
    v_mats = (v_w_in, v_w_uq, v_w_ukv, v_w_out, v_w_ple, v_w_ple_gate)
    vecs = (norm_pre_g, q_norm_g, kv_norm_g, sb_out_norm_g, mla_out_norm_g, norm_post_g, ple_norm_g, b_ple_gate)
    m_vecs = (m_norm_pre_g, m_q_norm_g, m_kv_norm_g, m_sb_out_norm_g, m_mla_out_norm_g, m_norm_post_g,
              m_ple_norm_g, m_b_ple_gate)
    v_vecs = (v_norm_pre_g, v_q_norm_g, v_kv_norm_g, v_sb_out_norm_g, v_mla_out_norm_g, v_norm_post_g,
              v_ple_norm_g, v_b_ple_gate)

    shards = [a[0].astype(BF16) for a in mats]
    w_in_p = _kernel_w_in(_all_gather(shards[:1])[0])
    grad_x, reduced, vec_slab = _step(x[0], p[0, 0], positions[0], loss_target[0], *vecs, w_in_p, shards[1:])
    upd = [_adamw_matrix(own, l2, w, m, v, "adamw_%d" % o)
           for o, ((own, l2), w, m, v) in enumerate(zip(reduced, mats, m_mats, v_mats))]
    sm = _adamw_vectors(_slab_exchange(vec_slab), vecs, m_vecs, v_vecs)

    outs = []
    for kind in range(4):
        mat = [upd[o][kind] for o in range(len(mats))]
        vec = sm[1 + 8 * kind:9 + 8 * kind]
        outs += [vec[0], mat[0], vec[1], mat[1], vec[2], mat[2], vec[3], vec[4], mat[3], vec[5],
                 mat[4], vec[6], mat[5], vec[7]]
    return (sm[0][0, 0], grad_x[None], *outs)


def _step(xs, ps, pos, tgt, norm_pre_g, q_norm_g, kv_norm_g, sb_out_norm_g, mla_out_norm_g,
          norm_post_g, ple_norm_g, b_ple_gate, w_in_p, shards):
    s = xs.shape[0]
    place = jnp.stack([lax.axis_index("c"), 2 * lax.axis_index("x") + lax.axis_index("y")]).astype(jnp.int32)

    half = ROPE_DIM // 2
    freq = ROPE_THETA ** (-jnp.arange(half, dtype=F32) / half)
    ang = pos.astype(F32)[:, None] * freq
    cos, sin = jnp.cos(ang), jnp.sin(ang)
    cos_t = jnp.concatenate([jnp.ones((s, 64), F32), cos, cos, jnp.zeros((s, 32), F32)], axis=1)
    sin_t = jnp.concatenate([jnp.zeros((s, 64), F32), -sin, sin, jnp.zeros((s, 32), F32)], axis=1)
    seg = jnp.arange(D_GRP) // HEAD_DIM
    bd = (seg[:, None] == seg[None, :]).astype(BF16)

    qkv, rest, h_b, *gath = _in_proj(xs, norm_pre_g, w_in_p, shards)
    w_uq_p, w_uk_p, w_uv, f_out, f_ple, f_pg = _kernel_weights(gath)
    sb_o = _sb_fwd(qkv, 8)
    qp, kp, vv, cqn_b, ckvn_b = _mla_prep(rest, q_norm_g, kv_norm_g, w_uq_p, w_uk_p, w_uv, cos_t, sin_t)
    mla_o, lse = _mla_fwd(qp, kp, vv, 4)

    (dx1, d_sbo, d_mlo, d_sbg, d_mlg, x1_b, dgl_b, yc_b, dy_b, p_b, du_b, small_mid) = _mid(
        xs, ps, tgt, sb_o, mla_o, rest, sb_out_norm_g, mla_out_norm_g, f_out, norm_post_g,
        f_ple, ple_norm_g, f_pg, b_ple_gate, bd)
    pay_a = [_tn_matmul(yc_b, dy_b, "dw_out").reshape(N_DEV, 128, D_MODEL),
             _tn_matmul(p_b, du_b, "dw_ple", blocked=True),
             _tn_matmul(x1_b, dgl_b, "dw_pg").reshape(N_DEV, 128, D_MODEL)]
    dqp, dkp, dvv, *sib_a = _mla_bwd(qp, kp, vv, d_mlo, mla_o, lse, 4, pay_a)
    pair_a = _pair_sums(pay_a, sib_a, place, "grad_pair_sums_a")
    dq_sb, dk_sb, dv_sb, *landed_a = _sb_bwd(qkv, d_sbo, [sm for sm, _ in pair_a])
    dcq, dckv, dkr, dq_b, dk_b, dv_b, small_prep = _mla_prep_bwd(
        dqp, dkp, dvv, rest, q_norm_g, kv_norm_g, w_uq_p, w_uk_p, w_uv, cos_t, sin_t)
    pieces = [dq_sb, dk_sb, dv_sb, d_sbg, d_mlg, dcq, dckv, dkr]
    d_cols = [_tn_matmul_multi(h_b, pieces[0:2], "dw_in_0"), _tn_matmul_multi(h_b, pieces[2:4], "dw_in_1"),
              _tn_matmul_multi(h_b, pieces[4:8], "dw_in_2")]
    pay_b = [_payload_in(d_cols), _tn_matmul(cqn_b, dq_b, "dw_uq", blocked=True),
             _payload_ukv(_tn_matmul(ckvn_b, dk_b, "dw_uk", blocked=True), _tn_matmul(ckvn_b, dv_b, "dw_uv"))]
    pair_b = _pair_sums(pay_b, _pair_exchange(pay_b, "grad_pair_exchange"), place, "grad_pair_sums_b")
    grad_x, small_in, *landed_b = _in_bwd(xs, norm_pre_g, dx1, pieces, w_in_p, [sm for sm, _ in pair_b])
    reduced = [(own, l2) for (_, own), l2 in zip(pair_b + pair_a, landed_b + landed_a)]
    slab = jnp.concatenate([small_in[0:1], jnp.pad(small_prep[0:2], ((0, 0), (0, D_MODEL - Q_LORA))),
                            small_mid[3:8]], axis=0)
    return grad_x, reduced, slab
```

```python
import jax
import jax.numpy as jnp
from jax import lax
from jax.experimental import pallas as pl
from jax.experimental.pallas import tpu as pltpu

F32 = jnp.float32
BF16 = jnp.bfloat16
MESH = pl.DeviceIdType.MESH

N_DEV = 8
D_MODEL = 1024
N_HEADS = 8
HEAD_DIM = 64
D_GRP = N_HEADS * HEAD_DIM
Q_LORA = 256
KV_LORA = 128
ROPE_DIM = 32
PLE_DIM = 256
CHUNK_SHIFT = 6
ROPE_THETA = 10000.0
EPS = 1e-6
SB_SCALE = HEAD_DIM ** -0.5
MLA_SCALE = (HEAD_DIM + ROPE_DIM) ** -0.5
NEG = -1e30
LOG2_E = 1.4426950408889634
LN_2 = 0.6931471805599453
SB_CUTOFF = 110.0

ADAM_LR = 0.001
ADAM_B1 = 0.9
ADAM_B2 = 0.999
ADAM_EPS = 1e-08
ADAM_WD = 0.01
ADAM_STEP = 10

LANES = 128
TQ = 256
TK = 256
TM = 256
TM_IO = 512
TS_DW = 2048

D_IN_P = 3072

_NT = (((1,), (1,)), ((), ()))
_TN = (((0,), (0,)), ((), ()))


def _params(sem, vmem_mb):
    return pltpu.CompilerParams(dimension_semantics=sem, vmem_limit_bytes=vmem_mb << 20)


def _dot(a, b):
    return jnp.dot(a, b, preferred_element_type=F32)


def _dot_nt(a, b):
    return lax.dot_general(a, b, _NT, preferred_element_type=F32)


def _dot_tn(a, b):
    return lax.dot_general(a, b, _TN, preferred_element_type=F32)


def _hl_dot(a, b):
    hi = a.astype(BF16)
    lo = (a - hi.astype(F32)).astype(BF16)
    return _dot(hi, b) + _dot(lo, b)


def _sigmoid(x):
    return 1.0 / (1.0 + jnp.exp(-x))


def _rope_swap(x, lane):
    left = pltpu.roll(x, LANES - 16, axis=1)
    right = pltpu.roll(x, 16, axis=1)
    lo = (lane >= 64) & (lane < 80)
    hi = (lane >= 80) & (lane < 96)
    return jnp.where(lo, left, jnp.where(hi, right, 0.0))


def _two_level_gather(x_refs, out_refs, send_sems, recv_sems, local_sems):
    x, y, c = lax.axis_index("x"), lax.axis_index("y"), lax.axis_index("c")
    me, sibling = (x, y, c), (x, y, 1 - c)
    chips = [(1 - x, y), (x, 1 - y), (1 - x, 1 - y)]
    ops = range(len(x_refs))

    def slot(o, px, py, pc):
        return out_refs[o].at[4 * px + 2 * py + pc]

    def copy(o, k, block, to, src=None):
        return pltpu.make_async_remote_copy(
            src_ref=slot(o, *block) if src is None else src, dst_ref=slot(o, *block),
            send_sem=send_sems.at[o, k], recv_sem=recv_sems.at[o, k],
            device_id=to, device_id_type=MESH)

    def mine():
        return [pltpu.make_async_copy(x_refs[o], slot(o, *me), local_sems.at[o]) for o in ops]

    def first():
        return ([copy(o, 0, me, sibling, src=x_refs[o]) for o in ops]
                + [copy(o, 1 + j, me, (*chip, c), src=x_refs[o]) for j, chip in enumerate(chips) for o in ops])

    def start():
        for cp in mine() + first():
            cp.start()

    def finish():
        passed = []
        for j, chip in enumerate(chips):
            for o in ops:
                copy(o, 1 + j, (*chip, c), me).wait_recv()
                passed.append(copy(o, 4 + j, (*chip, c), sibling))
                passed[-1].start()
        for o in ops:
            copy(o, 0, sibling, me).wait_recv()
        for j, chip in enumerate(chips):
            for o in ops:
                copy(o, 4 + j, (*chip, 1 - c), me).wait_recv()
        for cp in first() + passed:
            cp.wait_send()
        for cp in mine():
            cp.wait()

    return start, finish


def _gather_sems(n_op):
    return [pltpu.SemaphoreType.DMA((n_op, 7)), pltpu.SemaphoreType.DMA((n_op, 7)),
            pltpu.SemaphoreType.DMA((n_op,))]


def _all_gather(shards):
    n_op = len(shards)

    def body(*refs):
        start, finish = _two_level_gather(refs[:n_op], refs[n_op:2 * n_op], *refs[2 * n_op:])
        start()
        finish()

    any_spec = pl.BlockSpec(memory_space=pl.ANY)
    return pl.pallas_call(
        body, name="weight_all_gather",
        out_shape=[jax.ShapeDtypeStruct((N_DEV,) + a.shape, a.dtype) for a in shards],
        in_specs=[any_spec] * n_op, out_specs=[any_spec] * n_op, scratch_shapes=_gather_sems(n_op),
        compiler_params=pltpu.CompilerParams(vmem_limit_bytes=4 << 20),
    )(*shards)


def _pair_copies(g_refs, l_refs, ssem, rsem):
    x, y, c = lax.axis_index("x"), lax.axis_index("y"), lax.axis_index("c")
    copies = []
    for o in range(len(g_refs)):
        for chip in range(4):
            copies.append(pltpu.make_async_remote_copy(
                src_ref=g_refs[o].at[2 * chip + (1 - c)], dst_ref=l_refs[o].at[chip],
                send_sem=ssem.at[o, chip], recv_sem=rsem.at[o, chip],
                device_id=(x, y, 1 - c), device_id_type=MESH))
    return copies


def _pair_specs(pays):
    n_op = len(pays)
    any_spec = pl.BlockSpec(memory_space=pl.ANY)
    return ([any_spec] * n_op, [any_spec] * n_op,
            [jax.ShapeDtypeStruct((4,) + a.shape[1:], F32) for a in pays],
            [pltpu.SemaphoreType.DMA((n_op, 4)), pltpu.SemaphoreType.DMA((n_op, 4))])


def _pair_exchange(pays, name):
    n_op = len(pays)
    in_specs, out_specs, out_shape, sems = _pair_specs(pays)

    def body(*refs):
        copies = _pair_copies(refs[:n_op], refs[n_op:2 * n_op], *refs[2 * n_op:])
        for cp in copies:
            cp.start()
        for cp in copies:
            cp.wait()

    return pl.pallas_call(body, name=name, out_shape=out_shape, in_specs=in_specs, out_specs=out_specs,
                          scratch_shapes=sems,
                          compiler_params=pltpu.CompilerParams(vmem_limit_bytes=4 << 20))(*pays)


def _slab_exchange(small):
    sr, n = small.shape

    def body(s_ref, sland_ref, ssem, rsem, lsem):
        x, y, c = lax.axis_index("x"), lax.axis_index("y"), lax.axis_index("c")
        me = 4 * x + 2 * y + c
        copies = []
        for k in range(1, N_DEV):
            peer = (1 - x if (k >> 2) & 1 else x, 1 - y if (k >> 1) & 1 else y, 1 - c if k & 1 else c)
            copies.append(pltpu.make_async_remote_copy(
                src_ref=s_ref, dst_ref=sland_ref.at[me], send_sem=ssem.at[k], recv_sem=rsem.at[k],
                device_id=peer, device_id_type=MESH))
        own = pltpu.make_async_copy(s_ref, sland_ref.at[me], lsem)
        own.start()
        for cp in copies:
            cp.start()
        for cp in copies:
            cp.wait()
        own.wait()

    any_spec = pl.BlockSpec(memory_space=pl.ANY)
    return pl.pallas_call(
        body, name="grad_slab_exchange", out_shape=jax.ShapeDtypeStruct((N_DEV, sr, n), F32),
        in_specs=[any_spec], out_specs=any_spec,
        scratch_shapes=[pltpu.SemaphoreType.DMA((N_DEV,)), pltpu.SemaphoreType.DMA((N_DEV,)),
                        pltpu.SemaphoreType.DMA],
        compiler_params=pltpu.CompilerParams(vmem_limit_bytes=4 << 20),
    )(small)


def _pair_sums(pays, landed, place, name):
    n = len(pays)
    dims = [p.shape[1:] for p in pays]

    def body(place_ref, *refs):
        g_refs, l_refs, s_refs, own_refs = refs[:n], refs[n:2 * n], refs[2 * n:3 * n], refs[3 * n:]
        i = pl.program_id(0)
        for o in range(n):
            tot = g_refs[o][...] + l_refs[o][...]
            s_refs[o][...] = tot.astype(BF16)

            @pl.when(i == place_ref[1])
            def _(o=o, tot=tot):
                own_refs[o][...] = tot

    grid_spec = pltpu.PrefetchScalarGridSpec(
        num_scalar_prefetch=1, grid=(4,),
        in_specs=[pl.BlockSpec((None, r, c), lambda i, pr: (2 * i + pr[0], 0, 0)) for r, c in dims]
        + [pl.BlockSpec((None, r, c), lambda i, pr: (i, 0, 0)) for r, c in dims],
        out_specs=[pl.BlockSpec((None, r, c), lambda i, pr: (i, 0, 0)) for r, c in dims]
        + [pl.BlockSpec((r, c), lambda i, pr: (0, 0)) for r, c in dims])
    out = pl.pallas_call(
        body, name=name, grid_spec=grid_spec,
        out_shape=[jax.ShapeDtypeStruct((4, r, c), BF16) for r, c in dims]
        + [jax.ShapeDtypeStruct((r, c), F32) for r, c in dims],
        compiler_params=_params(("arbitrary",), 6),
    )(place, *pays, *landed)
    return list(zip(out[:n], out[n:]))


def _chip_copies(s_refs, l_refs, ssem, rsem):
    x, y, c = lax.axis_index("x"), lax.axis_index("y"), lax.axis_index("c")
    copies = []
    for rel in range(1, 4):
        px = 1 - x if rel & 2 else x
        py = 1 - y if rel & 1 else y
        for o in range(len(s_refs)):
            copies.append(pltpu.make_async_remote_copy(
                src_ref=s_refs[o].at[2 * px + py], dst_ref=l_refs[o].at[rel - 1],
                send_sem=ssem.at[o, rel - 1], recv_sem=rsem.at[o, rel - 1],
                device_id=(px, py, c), device_id_type=MESH))
    return copies


def _chip_specs(sums):
    n_op = len(sums)
    any_spec = pl.BlockSpec(memory_space=pl.ANY)
    return ([any_spec] * n_op, [any_spec] * n_op,
            [jax.ShapeDtypeStruct((3,) + a.shape[1:], BF16) for a in sums],
            [pltpu.SemaphoreType.DMA((n_op, 3)), pltpu.SemaphoreType.DMA((n_op, 3))])


def _adamw_math(g, w, m, v):
    mn = ADAM_B1 * m + (1.0 - ADAM_B1) * g
    vn = ADAM_B2 * v + (1.0 - ADAM_B2) * (g * g)
    m_hat = mn / (1.0 - ADAM_B1 ** ADAM_STEP)
    v_hat = vn / (1.0 - ADAM_B2 ** ADAM_STEP)
    return -ADAM_LR * (m_hat / (jnp.sqrt(v_hat) + ADAM_EPS) + ADAM_WD * w), mn, vn


def _adamw_matrix(own, landed, w, m, v, name):
    _, r, c = w.shape
    cp = own.shape[1]
    br = min(r, 256)

    def body(own_ref, l_ref, w_ref, m_ref, v_ref, g_out, d_out, m_out, v_out):
        g = own_ref[...]
        for k in range(3):
            g = g + l_ref[k].astype(F32)
        g = g[:, :c]
        g_out[...] = g
        d_out[...], m_out[...], v_out[...] = _adamw_math(g, w_ref[...], m_ref[...], v_ref[...])

    row = pl.BlockSpec((None, br, c), lambda i: (0, i, 0))
    shp = jax.ShapeDtypeStruct((1, r, c), F32)
    return pl.pallas_call(
        body, name=name, grid=(r // br,),
        in_specs=[pl.BlockSpec((br, cp), lambda i: (i, 0)), pl.BlockSpec((3, br, cp), lambda i: (0, i, 0)),
                  row, row, row],
        out_specs=(row, row, row, row), out_shape=(shp, shp, shp, shp),
        compiler_params=_params(("parallel",), 6),
    )(own, landed, w, m, v)


_VEC_PLACE = ((0, 0), (1, 0), (2, 0), (3, 0), (3, D_GRP), (4, 0), (5, 0), (6, 0))


def _adamw_vectors(sland, ws, ms, vs):
    nv = len(ws)

    def body(l_ref, *refs):
        w_refs, m_refs, v_refs = refs[:nv], refs[nv:2 * nv], refs[2 * nv:3 * nv]
        loss_ref = refs[3 * nv]
        outs = refs[3 * nv + 1:]
        g_all = l_ref[0]
        for j in range(1, N_DEV):
            g_all = g_all + l_ref[j]
        loss_ref[...] = jnp.sum(g_all[7:8, :], axis=1, keepdims=True)
        for k, (row, lane0) in enumerate(_VEC_PLACE):
            n = w_refs[k].shape[1]
            g = g_all[row:row + 1, lane0:lane0 + n]
            d, mn, vn = _adamw_math(g, w_refs[k][...], m_refs[k][...], v_refs[k][...])
            outs[k][...] = g
            outs[nv + k][...] = d
            outs[2 * nv + k][...] = mn
            outs[3 * nv + k][...] = vn

    def whole(shape):
        return pl.BlockSpec(shape, lambda i: (0,) * len(shape))

    shapes = [jax.ShapeDtypeStruct(w.shape, F32) for w in ws]
    return pl.pallas_call(
        body, name="adamw_vectors", grid=(1,),
        in_specs=[whole(sland.shape)] + [whole(w.shape) for w in ws] * 3,
        out_specs=[whole((1, 1))] + [whole(w.shape) for w in ws] * 4,
        out_shape=[jax.ShapeDtypeStruct((1, 1), F32)] + shapes * 4,
        compiler_params=_params(("arbitrary",), 2),
    )(sland, *ws, *ms, *vs)


def _in_proj(x, g, w, shards):
    s = x.shape[0]
    n_op = len(shards)
    steps = s // TM_IO

    def body(x_ref, g_ref, w_ref, *refs):
        shard_refs = refs[:n_op]
        qkv_ref, rest_ref, h_ref = refs[n_op:n_op + 3]
        gath_refs = refs[n_op + 3:2 * n_op + 3]
        start, finish = _two_level_gather(shard_refs, gath_refs, *refs[2 * n_op + 3:])
        i = pl.program_id(0)

        @pl.when(i == 0)
        def _():
            start()

        xv = x_ref[...]
        r = lax.rsqrt(jnp.mean(xv * xv, axis=-1, keepdims=True) + EPS)
        h = ((xv * r) * g_ref[...]).astype(BF16)
        h_ref[...] = h
        qkv_ref[...] = _dot(h, w_ref[:, :1536]).astype(BF16)
        rest_ref[...] = _dot(h, w_ref[:, 1536:])

        @pl.when(i == steps - 1)
        def _():
            finish()

    any_spec = pl.BlockSpec(memory_space=pl.ANY)
    return pl.pallas_call(
        body, name="in_proj", grid=(steps,),
        in_specs=[pl.BlockSpec((TM_IO, D_MODEL), lambda i: (i, 0)),
                  pl.BlockSpec((1, D_MODEL), lambda i: (0, 0)),
                  pl.BlockSpec((D_MODEL, D_IN_P), lambda i: (0, 0))] + [any_spec] * n_op,
        out_specs=[pl.BlockSpec((TM_IO, 1536), lambda i: (i, 0)),
                   pl.BlockSpec((TM_IO, 1536), lambda i: (i, 0)),
                   pl.BlockSpec((TM_IO, D_MODEL), lambda i: (i, 0))] + [any_spec] * n_op,
        out_shape=[jax.ShapeDtypeStruct((s, 1536), BF16), jax.ShapeDtypeStruct((s, 1536), F32),
                   jax.ShapeDtypeStruct((s, D_MODEL), BF16)]
        + [jax.ShapeDtypeStruct((N_DEV,) + a.shape, a.dtype) for a in shards],
        scratch_shapes=_gather_sems(n_op),
        compiler_params=_params(("arbitrary",), 18),
    )(x, g, w, *shards)


def _mla_prep(rest, gq, gkv, wuq, wuk, wuv, cos_t, sin_t):
    s = rest.shape[0]

    def body(cq_ref, ckv_ref, kr_ref, gq_ref, gkv_ref, wuq_ref, wuk_ref, wuv_ref, c_ref, s_ref,
             qp_ref, kp_ref, vv_ref, cqn_ref, ckvn_ref):
        lane = lax.broadcasted_iota(jnp.int32, (1, LANES), 1)
        cos_v, sin_v = c_ref[...], s_ref[...]
        cq = cq_ref[...]
        rq = lax.rsqrt(jnp.mean(cq * cq, axis=-1, keepdims=True) + EPS)
        cqn = ((cq * rq) * gq_ref[...]).astype(BF16)
        cqn_ref[...] = cqn
        q = _dot(cqn, wuq_ref[...])
        ckv = ckv_ref[...]
        rkv = lax.rsqrt(jnp.mean(ckv * ckv, axis=-1, keepdims=True) + EPS)
        ckvn = ((ckv * rkv) * gkv_ref[...]).astype(BF16)
        ckvn_ref[...] = ckvn
        kn = _dot(ckvn, wuk_ref[...])
        vv_ref[...] = _dot(ckvn, wuv_ref[...]).astype(BF16)
        kr = kr_ref[...]
        kr_roped = kr * cos_v + _rope_swap(kr, lane) * sin_v
        for h in range(N_HEADS):
            sl = slice(h * LANES, (h + 1) * LANES)
            qh = q[:, sl]
            qp_ref[:, sl] = (qh * cos_v + _rope_swap(qh, lane) * sin_v).astype(BF16)
            kp_ref[:, sl] = (kn[:, sl] + kr_roped).astype(BF16)

    def row(width, idx):
        return pl.BlockSpec((TM, width), lambda i: (i, idx))

    def full(a):
        return pl.BlockSpec(a.shape, lambda i: (0, 0))

    return pl.pallas_call(
        body, name="mla_prep", grid=(s // TM,),
        in_specs=[row(Q_LORA, 4), row(KV_LORA, 10), row(LANES, 11), full(gq), full(gkv),
                  full(wuq), full(wuk), full(wuv), row(LANES, 0), row(LANES, 0)],
        out_specs=(row(1024, 0), row(1024, 0), row(D_GRP, 0), row(Q_LORA, 0), row(KV_LORA, 0)),
        out_shape=(jax.ShapeDtypeStruct((s, 1024), BF16), jax.ShapeDtypeStruct((s, 1024), BF16),
                   jax.ShapeDtypeStruct((s, D_GRP), BF16), jax.ShapeDtypeStruct((s, Q_LORA), BF16),
                   jax.ShapeDtypeStruct((s, KV_LORA), BF16)),
        compiler_params=_params(("parallel",), 6),
    )(rest, rest, rest, gq, gkv, wuq, wuk, wuv, cos_t, sin_t)


def _sb_live(n, qi, carries):
    top = carries[0]
    for c in carries[1:]:
        top = jnp.maximum(top, c)
    return jnp.logical_and(n < qi, jnp.max(top) > -SB_CUTOFF)


def _sb_fwd(qkv, hb):
    s = qkv.shape[0]

    def body(q_ref, k_ref, v_ref, o_ref, acc):
        qi = pl.program_id(1)
        lane = lax.broadcasted_iota(jnp.int32, (1, LANES), 1)
        is_a = lane < HEAD_DIM
        pair = lambda h: slice((h // 2) * LANES, (h // 2 + 1) * LANES)
        q_h = []
        for h in range(hb):
            qs = q_ref[:, pair(h)] * SB_SCALE
            mine = is_a if h % 2 == 0 else jnp.logical_not(is_a)
            q_h.append(jnp.where(mine, qs, jnp.zeros_like(qs)))
        r_i = lax.broadcasted_iota(jnp.int32, (TQ, TK), 0)
        c_i = lax.broadcasted_iota(jnp.int32, (TQ, TK), 1)
        past = c_i < r_i
        upper = (r_i > c_i).astype(BF16)
        acc[...] = jnp.zeros_like(acc)

        def tile(j, carries, diag):
            ks = pl.ds(pl.multiple_of(j * TK, TK), TK)
            zs = [_dot_nt(q_h[h], k_ref[ks, pair(h)]) for h in range(hb)]
            if diag:
                zs = [jnp.where(past, z, NEG) for z in zs]
            lfs = [-(jnp.maximum(z, 0.0) + jnp.log(1.0 + jnp.exp(-jnp.abs(z)))) for z in zs]
            sufs = [_hl_dot(lfs[h], upper) for h in range(hb)]
            out = []
            for h in range(hb):
                w = jnp.exp(zs[h] + lfs[h] + (sufs[h] + carries[h]))
                acc[h] += _dot(w.astype(BF16), v_ref[ks, pair(h)])
                out.append(carries[h] + jnp.sum(lfs[h], axis=1, keepdims=True))
            return tuple(out)

        zero = jnp.zeros((TQ, 1), F32)
        carries = tile(qi, (zero,) * hb, True)

        def step(st):
            return (st[0] + 1,) + tile(qi - 1 - st[0], st[1:], False)

        lax.while_loop(lambda st: _sb_live(st[0], qi, st[1:]), step, (0,) + carries)
        for pr in range(hb // 2):
            o_ref[:, pr * LANES:(pr + 1) * LANES] = jnp.where(is_a, acc[2 * pr], acc[2 * pr + 1])

    width = hb * HEAD_DIM
    nb = D_GRP // width
    slab = lambda part: pl.BlockSpec((s, width), lambda g, qi: (0, part * nb + g))
    blk = pl.BlockSpec((TQ, width), lambda g, qi: (qi, g))
    return pl.pallas_call(
        body, name="sb_fwd", grid=(nb, s // TQ),
        in_specs=[blk, slab(1), slab(2)], out_specs=blk,
        out_shape=jax.ShapeDtypeStruct((s, D_GRP), F32),
        scratch_shapes=[pltpu.VMEM((hb, TQ, LANES), F32)],
        compiler_params=_params(("arbitrary", "arbitrary"), 24),
    )(qkv, qkv, qkv)


def _sb_bwd(qkv, d_o, sums):
    s = qkv.shape[0]
    nq = s // TQ
    nk = s // TK
    n_op = len(sums)
    ride_in, ride_out, ride_shape, ride_sems = _chip_specs(sums)

    def body(q_ref, k_ref, v_ref, do_ref, *refs):
        s_refs = refs[:n_op]
        dq_ref, dk_ref, dv_ref = refs[n_op:n_op + 3]
        l_refs = refs[n_op + 3:2 * n_op + 3]
        x1s, bts, dqacc, dkacc, dvacc, ssem, rsem = refs[2 * n_op + 3:]
        qi = pl.program_id(1)
        first_step = jnp.logical_and(pl.program_id(0) == 0, qi == 0)
        last_step = jnp.logical_and(pl.program_id(0) == pl.num_programs(0) - 1, qi == nq - 1)

        @pl.when(first_step)
        def _():
            for cp in _chip_copies(s_refs, l_refs, ssem, rsem):
                cp.start()

        lane = lax.broadcasted_iota(jnp.int32, (1, LANES), 1)
        is_a = lane < HEAD_DIM

        @pl.when(qi == 0)
        def _():
            dkacc[...] = jnp.zeros_like(dkacc)
            dvacc[...] = jnp.zeros_like(dvacc)

        qs = q_ref[...] * SB_SCALE
        zq = jnp.zeros_like(qs)
        qs_x = (jnp.where(is_a, qs, zq), jnp.where(is_a, zq, qs))
        dob = do_ref[...].astype(BF16)
        do_x = (jnp.where(is_a, dob, zq), jnp.where(is_a, zq, dob))
        r_i = lax.broadcasted_iota(jnp.int32, (TQ, TK), 0)
        c_i = lax.broadcasted_iota(jnp.int32, (TQ, TK), 1)
        past = c_i < r_i
        upper = (r_i > c_i).astype(BF16)
        upper_incl = (r_i >= c_i).astype(BF16)
        dqacc[...] = jnp.zeros_like(dqacc)
        both = ((0, 0), (0, 1), (1, 0), (1, 1))

        def tiles(n):
            j_hi = qi - 2 * n
            lo_ok = j_hi >= 1
            j_lo = jnp.maximum(j_hi - 1, 0)
            ks = (pl.ds(pl.multiple_of(j_hi * TK, TK), TK), pl.ds(pl.multiple_of(j_lo * TK, TK), TK))
            return j_hi, lo_ok, j_lo, ks

        def sweep(n, carries):
            j_hi, lo_ok, j_lo, ks = tiles(n)
            slot = (j_hi, jnp.where(lo_ok, j_lo, nk))
            valid = (jnp.logical_or(past, j_hi < qi), lo_ok)
            z = {th: jnp.where(valid[th[0]], _dot_nt(qs_x[th[1]], k_ref[ks[th[0]], :]), NEG) for th in both}
            log_b, lf_sum, suf = {}, {}, {}
            for th in both:
                lf = -(jnp.maximum(z[th], 0.0) + jnp.log(1.0 + jnp.exp(-jnp.abs(z[th]))))
                log_b[th] = z[th] + lf
                lf_sum[th] = jnp.sum(lf, axis=1, keepdims=True)
                suf[th] = _hl_dot(lf, upper)
            c, g_in = {}, {}
            for h in range(2):
                c[0, h], g_in[0, h] = carries[2 * h], carries[2 * h + 1]
                c[1, h] = c[0, h] + lf_sum[0, h]
            d_a = {th: _dot_nt(do_x[th[1]], v_ref[ks[th[0]], :]) for th in both}
            a_b, g, g_sum, sg = {}, {}, {}, {}
            for th in both:
                a = jnp.exp(log_b[th] + (suf[th] + c[th]))
                a_b[th] = a.astype(BF16)
                g[th] = a * d_a[th]
                g_sum[th] = jnp.sum(g[th], axis=1, keepdims=True)
                sg[th] = _hl_dot(g[th], upper_incl)
            for h in range(2):
                g_in[1, h] = g_in[0, h] + g_sum[0, h]
            for th in both:
                t, h = th
                beta = jnp.exp(log_b[th])
                x1s[slot[t], h] = g[th] * (1.0 - beta) + beta * (sg[th] + g_in[th])
                bts[slot[t], h] = beta
                dvacc[ks[t], :] += _dot_tn(a_b[th], do_x[h])
            out = []
            for h in range(2):
                out.append(c[1, h] + lf_sum[1, h])
                out.append(g_in[1, h] + g_sum[1, h])
            return tuple(out)

        zero = jnp.zeros((TQ, 1), F32)
        first = sweep(0, (zero, zero, zero, zero))

        def more(st):
            return jnp.logical_and(2 * st[0] <= qi, jnp.max(jnp.maximum(st[1], st[3])) > -SB_CUTOFF)

        swept = lax.while_loop(more, lambda st: (st[0] + 1,) + sweep(st[0], st[1:]), (1,) + first)
        g_tot = (swept[2], swept[4])

        def apply(n, carry):
            j_hi, lo_ok, j_lo, ks = tiles(n)

            def one(j, kslice):
                for h in range(2):
                    dz = (x1s[j, h] - bts[j, h] * g_tot[h]).astype(BF16)
                    dqacc[h] += _dot(dz, k_ref[kslice, :])
                    dkacc[kslice, :] += _dot_tn(dz, qs_x[h])

            one(j_hi, ks[0])

            @pl.when(lo_ok)
            def _():
                one(j_lo, ks[1])

            return carry

        lax.fori_loop(0, swept[0], apply, 0)
        dq_ref[...] = (jnp.where(is_a, dqacc[0], dqacc[1]) * SB_SCALE).astype(BF16)

        @pl.when(qi == nq - 1)
        def _():
            dk_ref[...] = dkacc[...].astype(BF16)
            dv_ref[...] = dvacc[...].astype(BF16)

        @pl.when(last_step)
        def _():
            for cp in _chip_copies(s_refs, l_refs, ssem, rsem):
                cp.wait()

    slab = lambda off: pl.BlockSpec((s, LANES), lambda p, qi: (0, off + p))
    blk = pl.BlockSpec((TQ, LANES), lambda p, qi: (qi, p))
    out_slab = pl.BlockSpec((s, LANES), lambda p, qi: (0, p))
    shp = jax.ShapeDtypeStruct((s, D_GRP), BF16)
    return pl.pallas_call(
        body, name="sb_bwd", grid=(4, nq),
        in_specs=[blk, slab(4), slab(8), blk] + ride_in,
        out_specs=[blk, out_slab, out_slab] + ride_out, out_shape=[shp, shp, shp] + ride_shape,
        scratch_shapes=[pltpu.VMEM((nk + 1, 2, TQ, TK), F32)] * 2
        + [pltpu.VMEM((2, TQ, LANES), F32), pltpu.VMEM((s, LANES), F32), pltpu.VMEM((s, LANES), F32)]
        + ride_sems,
        compiler_params=_params(("arbitrary", "arbitrary"), 40),
    )(qkv, qkv, qkv, d_o, *sums)


def _mla_fwd(qp, kp, vv, hb):
    s = qp.shape[0]
    c2 = MLA_SCALE * LOG2_E

    def body(q_ref, k_ref, v_ref, o_ref, lse_ref, vaug, mrun, mb, acc, zbuf):
        qi = pl.program_id(1)
        lane = lax.broadcasted_iota(jnp.int32, (1, LANES), 1)
        is_a = lane < HEAD_DIM

        @pl.when(qi == 0)
        def _():
            for h in range(hb):
                vp = v_ref[:, (h // 2) * LANES:(h // 2 + 1) * LANES]
                mine = is_a if h % 2 == 0 else jnp.logical_not(is_a)
                vaug[h] = jnp.where(mine, vp, jnp.ones_like(vp))

        r_i = lax.broadcasted_iota(jnp.int32, (TQ, TK), 0)
        c_i = lax.broadcasted_iota(jnp.int32, (TQ, TK), 1)
        visible = (c_i >> CHUNK_SHIFT) <= (r_i >> CHUNK_SHIFT)

        def key_rows(j):
            return pl.ds(pl.multiple_of(j * TK, TK), TK)

        def sweep(tiles):
            def loop(n, carry):
                tiles(((2 * n, False), (2 * n + 1, False)))
                return carry

            lax.fori_loop(0, qi // 2, loop, 0)

            @pl.when(qi % 2 == 1)
            def _():
                tiles(((qi - 1, False), (qi, True)))

            @pl.when(qi % 2 == 0)
            def _():
                tiles(((qi, True),))

        mrun[...] = jnp.full_like(mrun, NEG)

        def tiles_max(js):
            zs = [[_dot_nt(q_ref[:, h * LANES:(h + 1) * LANES], k_ref[key_rows(j), h * LANES:(h + 1) * LANES])
                   for h in range(hb)] for j, _ in js]
            for t, (j, diag) in enumerate(js):
                for h in range(hb):
                    z = jnp.where(visible, zs[t][h], NEG) if diag else zs[t][h]
                    zbuf[j, h] = z
                    mrun[h] = jnp.maximum(mrun[h], z)

        sweep(tiles_max)
        for h in range(hb):
            m = jnp.max(mrun[h], axis=1, keepdims=True) * c2
            mb[h] = jnp.broadcast_to(m, (TQ, TK))
        acc[...] = jnp.zeros_like(acc)

        def tiles_pv(js):
            ps = [[jnp.exp2((zbuf[j, h] * c2 - mb[h]).astype(BF16)) for h in range(hb)] for j, _ in js]
            for t, (j, _) in enumerate(js):
                for h in range(hb):
                    acc[h] += _dot(ps[t][h], vaug[h, key_rows(j), :])

        sweep(tiles_pv)
        for pr in range(hb // 2):
            a, b = 2 * pr, 2 * pr + 1
            psl = slice(pr * LANES, (pr + 1) * LANES)
            acc_a, acc_b = acc[a], acc[b]
            l_a = pltpu.roll(acc_a, HEAD_DIM, axis=1)
            l_b = pltpu.roll(acc_b, HEAD_DIM, axis=1)
            o_ref[:, psl] = jnp.where(is_a, acc_a * (1.0 / l_a), acc_b * (1.0 / l_b))
            lse_ref[:, psl] = jnp.where(is_a, mb[a, :, :LANES] * LN_2 + jnp.log(l_a),
                                        mb[b, :, :LANES] * LN_2 + jnp.log(l_b))

    blk = pl.BlockSpec((TQ, hb * HEAD_DIM), lambda g, qi: (qi, g))
    shp = jax.ShapeDtypeStruct((s, D_GRP), F32)
    return pl.pallas_call(
        body, name="mla_fwd", grid=(N_HEADS // hb, s // TQ),
        in_specs=[pl.BlockSpec((TQ, hb * LANES), lambda g, qi: (qi, g)),
                  pl.BlockSpec((s, hb * LANES), lambda g, qi: (0, g)),
                  pl.BlockSpec((s, hb * HEAD_DIM), lambda g, qi: (0, g))],
        out_specs=(blk, blk), out_shape=(shp, shp),
        scratch_shapes=[pltpu.VMEM((hb, s, LANES), BF16), pltpu.VMEM((hb, TQ, TK), F32),
                        pltpu.VMEM((hb, TQ, TK), F32), pltpu.VMEM((hb, TQ, LANES), F32),
                        pltpu.VMEM((s // TK, hb, TQ, TK), F32)],
        compiler_params=_params(("arbitrary", "arbitrary"), 40),
    )(qp, kp, vv)


def _mla_bwd(qp, kp, vv, d_o, o, lse, hb, pays):
    s = qp.shape[0]
    nq = s // TQ
    c2 = MLA_SCALE * LOG2_E
    n_op = len(pays)
    ride_in, ride_out, ride_shape, ride_sems = _pair_specs(pays)

    def body(q_ref, k_ref, v_ref, do_ref, o_ref, lse_ref, *refs):
        g_refs = refs[:n_op]
        dq_ref, dk_ref, dv_ref = refs[n_op:n_op + 3]
        l_refs = refs[n_op + 3:2 * n_op + 3]
        dqacc, lse_b, delta_b, ssem, rsem = refs[2 * n_op + 3:]
        qi = pl.program_id(1)

        @pl.when(jnp.logical_and(pl.program_id(0) == 0, qi == 0))
        def _():
            for cp in _pair_copies(g_refs, l_refs, ssem, rsem):
                cp.start()

        lane = lax.broadcasted_iota(jnp.int32, (1, LANES), 1)
        is_a = lane < HEAD_DIM

        @pl.when(qi == 0)
        def _():
            dk_ref[...] = jnp.zeros_like(dk_ref)
            dv_ref[...] = jnp.zeros_like(dv_ref)

        r_i = lax.broadcasted_iota(jnp.int32, (TQ, TK), 0)
        c_i = lax.broadcasted_iota(jnp.int32, (TQ, TK), 1)
        visible = (c_i >> CHUNK_SHIFT) <= (r_i >> CHUNK_SHIFT)
        do_x = []
        for h in range(hb):
            psl = slice((h // 2) * LANES, (h // 2 + 1) * LANES)
            mine = is_a if h % 2 == 0 else jnp.logical_not(is_a)
            d_o = do_ref[:, psl]
            delta = jnp.sum(jnp.where(mine, d_o * o_ref[:, psl], 0.0), axis=1, keepdims=True)
            lse_h = jnp.sum(jnp.where(lane == (h % 2) * HEAD_DIM, lse_ref[:, psl], 0.0), axis=1, keepdims=True)
            lse_b[h] = jnp.broadcast_to(lse_h * LOG2_E, (TQ, TK))
            delta_b[h] = jnp.broadcast_to(delta, (TQ, TK))
            do_x.append(jnp.where(mine, d_o, 0.0).astype(BF16))
        dqacc[...] = jnp.zeros_like(dqacc)

        head = lambda h: slice(h * LANES, (h + 1) * LANES)
        pair = lambda h: slice((h // 2) * LANES, (h // 2 + 1) * LANES)

        def tiles(js):
            th = [(j, diag, pl.ds(pl.multiple_of(j * TK, TK), TK), h) for j, diag in js for h in range(hb)]
            zs = [_dot_nt(q_ref[:, head(h)], k_ref[ks, head(h)]) for _, _, ks, h in th]
            dps = [_dot_nt(do_x[h], v_ref[ks, pair(h)]) for _, _, ks, h in th]
            for i, (j, diag, ks, h) in enumerate(th):
                e = zs[i] * c2 - lse_b[h]
                if diag:
                    e = jnp.where(visible, e, NEG)
                p = jnp.exp2(e)
                ds = (p * (dps[i] - delta_b[h]) * MLA_SCALE).astype(BF16)
                dqacc[h] += _dot(ds, k_ref[ks, head(h)])
                dk_ref[ks, head(h)] += _dot_tn(ds, q_ref[:, head(h)])
                dv_ref[ks, pair(h)] += _dot_tn(p.astype(BF16), do_x[h])

        def loop(n, c):
            tiles(((2 * n, False), (2 * n + 1, False)))
            return c

        lax.fori_loop(0, qi // 2, loop, 0)

        @pl.when(qi % 2 == 1)
        def _():
            tiles(((qi - 1, False), (qi, True)))

        @pl.when(qi % 2 == 0)
        def _():
            tiles(((qi, True),))

        for h in range(hb):
            dq_ref[:, h * LANES:(h + 1) * LANES] = dqacc[h]

        @pl.when(jnp.logical_and(pl.program_id(0) == pl.num_programs(0) - 1, qi == nq - 1))
        def _():
            for cp in _pair_copies(g_refs, l_refs, ssem, rsem):
                cp.wait()

    blk = pl.BlockSpec((TQ, hb * HEAD_DIM), lambda g, qi: (qi, g))
    return pl.pallas_call(
        body, name="mla_bwd", grid=(N_HEADS // hb, nq),
        in_specs=[pl.BlockSpec((TQ, hb * LANES), lambda g, qi: (qi, g)),
                  pl.BlockSpec((s, hb * LANES), lambda g, qi: (0, g)),
                  pl.BlockSpec((s, hb * HEAD_DIM), lambda g, qi: (0, g)), blk, blk, blk] + ride_in,
        out_specs=[pl.BlockSpec((TQ, hb * LANES), lambda g, qi: (qi, g)),
                   pl.BlockSpec((s, hb * LANES), lambda g, qi: (0, g)),
                   pl.BlockSpec((s, hb * HEAD_DIM), lambda g, qi: (0, g))] + ride_out,
        out_shape=[jax.ShapeDtypeStruct((s, 1024), F32), jax.ShapeDtypeStruct((s, 1024), F32),
                   jax.ShapeDtypeStruct((s, D_GRP), F32)] + ride_shape,
        scratch_shapes=[pltpu.VMEM((hb, TQ, LANES), F32), pltpu.VMEM((hb, TQ, TK), F32),
                        pltpu.VMEM((hb, TQ, TK), F32)] + ride_sems,
        compiler_params=_params(("arbitrary", "arbitrary"), 48),
    )(qp, kp, vv, d_o, o, lse, *pays)


def _mid(x, p, target, sb_o, mla_o, rest, g_sb, g_mla, w_out, g_post, w_ple, g_ple, w_pg, b_pg, bd):
    s = x.shape[0]

    def body(x_ref, p_ref, t_ref, sbo_ref, mlo_ref, sbg_ref, mlg_ref, gsb_ref, gml_ref, wout_ref,
             gpost_ref, wple_ref, gple_ref, wpg_ref, bpg_ref, bd_ref,
             dx1_ref, dsbo_ref, dmlo_ref, dsbg_ref, dmlg_ref, x1b_ref, dglb_ref, ycb_ref, dyb_ref,
             pb_ref, dub_ref, small_ref):
        i = pl.program_id(0)
        bd_m = bd_ref[...]

        def seg_mean(v):
            return _dot(v.astype(BF16), bd_m) * (1.0 / HEAD_DIM)

        groups = []
        for o_ref, gate_ref, gain_ref in ((sbo_ref, sbg_ref, gsb_ref), (mlo_ref, mlg_ref, gml_ref)):
            o = o_ref[...]
            r = lax.rsqrt(seg_mean(o * o) + EPS)
            n = o * r
            hn = n * gain_ref[...]
            gate = gate_ref[...]
            sg = _sigmoid(gate)
            si = gate * sg
            groups.append((r, n, hn, gate, sg, si, gain_ref[...]))
        ya = (groups[0][2] * groups[0][5]).astype(BF16)
        yb = (groups[1][2] * groups[1][5]).astype(BF16)
        ycb_ref[:, :D_GRP] = ya
        ycb_ref[:, D_GRP:] = yb
        y = _dot(ya, wout_ref[:D_GRP, :]) + _dot(yb, wout_ref[D_GRP:, :])
        ry = lax.rsqrt(jnp.mean(y * y, axis=-1, keepdims=True) + EPS)
        ny = y * ry
        x1 = x_ref[...] + ny * gpost_ref[...]
        x1b = x1.astype(BF16)
        x1b_ref[...] = x1b
        pb = p_ref[...].astype(BF16)
        pb_ref[...] = pb
        u = _dot(pb, wple_ref[...])
        ru = lax.rsqrt(jnp.mean(u * u, axis=-1, keepdims=True) + EPS)
        nu = u * ru
        ple = nu * gple_ref[...]
        gate = _sigmoid(_dot(x1b, wpg_ref[...]) + bpg_ref[...])
        x2 = x1 + ple * gate
        diff = x2 - t_ref[...]
        dx2 = diff * (1.0 / D_MODEL)

        d_ple = dx2 * gate
        d_glin = (dx2 * ple) * (gate * (1.0 - gate))
        dglb = d_glin.astype(BF16)
        dglb_ref[...] = dglb
        dx1 = dx2 + _dot_nt(dglb, wpg_ref[...])
        dx1_ref[...] = dx1
        d_nu = d_ple * gple_ref[...]
        d_u = ru * (d_nu - nu * jnp.mean(d_nu * nu, axis=-1, keepdims=True))
        dub_ref[...] = d_u.astype(BF16)
        d_ny = dx1 * gpost_ref[...]
        d_y = ry * (d_ny - ny * jnp.mean(d_ny * ny, axis=-1, keepdims=True))
        dyb = d_y.astype(BF16)
        dyb_ref[...] = dyb
        d_yc = (_dot_nt(dyb, wout_ref[:D_GRP, :]), _dot_nt(dyb, wout_ref[D_GRP:, :]))

        d_gain = []
        for gx, (do_ref, dg_ref) in enumerate(((dsbo_ref, dsbg_ref), (dmlo_ref, dmlg_ref))):
            r, n, hn, gate_g, sg, si, gain = groups[gx]
            dyg = d_yc[gx]
            d_hn = dyg * si
            dg_ref[...] = (dyg * hn * (sg * (1.0 + gate_g * (1.0 - sg)))).astype(BF16)
            d_gain.append(jnp.sum(d_hn * n, axis=0, keepdims=True))
            d_n = d_hn * gain
            do_ref[...] = r * (d_n - n * seg_mean(d_n * n))

        @pl.when(i == 0)
        def _():
            small_ref[...] = jnp.zeros_like(small_ref)

        small_ref[3:4, :D_GRP] += d_gain[0]
        small_ref[3:4, D_GRP:] += d_gain[1]
        small_ref[4:5, :] += jnp.sum(dx1 * ny, axis=0, keepdims=True)
        small_ref[5:6, :] += jnp.sum(d_ple * nu, axis=0, keepdims=True)
        small_ref[6:7, :] += jnp.sum(d_glin, axis=0, keepdims=True)
        small_ref[7:8, :] += jnp.sum(diff * diff, axis=0, keepdims=True) * (0.5 / D_MODEL)

    def row(width, idx=0):
        return pl.BlockSpec((TM, width), lambda i: (i, idx))

    def full(a):
        return pl.BlockSpec(a.shape, lambda i: (0, 0))

    f32 = lambda w: jax.ShapeDtypeStruct((s, w), F32)
    b16 = lambda w: jax.ShapeDtypeStruct((s, w), BF16)
    return pl.pallas_call(
        body, name="mid", grid=(s // TM,),
        in_specs=[row(D_MODEL), row(PLE_DIM), row(D_MODEL), row(D_GRP), row(D_GRP),
                  row(D_GRP, 0), row(D_GRP, 1), full(g_sb), full(g_mla), full(w_out), full(g_post),
                  full(w_ple), full(g_ple), full(w_pg), full(b_pg), full(bd)],
        out_specs=(row(D_MODEL), row(D_GRP), row(D_GRP), row(D_GRP), row(D_GRP), row(D_MODEL),
                   row(D_MODEL), row(D_MODEL), row(D_MODEL), row(PLE_DIM), row(D_MODEL),
                   pl.BlockSpec((8, D_MODEL), lambda i: (0, 0))),
        out_shape=(f32(D_MODEL), f32(D_GRP), f32(D_GRP), b16(D_GRP), b16(D_GRP), b16(D_MODEL),
                   b16(D_MODEL), b16(D_MODEL), b16(D_MODEL), b16(PLE_DIM), b16(D_MODEL),
                   jax.ShapeDtypeStruct((8, D_MODEL), F32)),
        compiler_params=_params(("arbitrary",), 36),
    )(x, p, target, sb_o, mla_o, rest, rest, g_sb, g_mla, w_out, g_post, w_ple, g_ple, w_pg, b_pg, bd)


def _mla_prep_bwd(dqp, dkp, dvv, rest, gq, gkv, wuq, wuk, wuv, cos_t, sin_t):
    s = rest.shape[0]

    def body(dqp_ref, dkp_ref, dvv_ref, cq_ref, ckv_ref, gq_ref, gkv_ref, wuq_ref, wuk_ref, wuv_ref,
             c_ref, s_ref, dcq_ref, dckv_ref, dkr_ref, dqb_ref, dkb_ref, dvb_ref, small_ref):
        i = pl.program_id(0)
        lane = lax.broadcasted_iota(jnp.int32, (1, LANES), 1)
        in_rope = (lane >= HEAD_DIM) & (lane < HEAD_DIM + ROPE_DIM)
        cos_v, sin_v = c_ref[...], s_ref[...]
        dkr_roped = jnp.zeros((TM, LANES), F32)
        for h in range(N_HEADS):
            sl = slice(h * LANES, (h + 1) * LANES)
            dy = dqp_ref[:, sl]
            dqb_ref[:, sl] = (dy * cos_v + _rope_swap(dy * sin_v, lane)).astype(BF16)
            dkh = dkp_ref[:, sl]
            dkb_ref[:, sl] = dkh.astype(BF16)
            dkr_roped = dkr_roped + jnp.where(in_rope, dkh, 0.0)
        dkr_ref[...] = (dkr_roped * cos_v + _rope_swap(dkr_roped * sin_v, lane)).astype(BF16)
        dvb = dvv_ref[...].astype(BF16)
        dvb_ref[...] = dvb

        cq = cq_ref[...]
        rq = lax.rsqrt(jnp.mean(cq * cq, axis=-1, keepdims=True) + EPS)
        nq_ = cq * rq
        d_cqn = _dot_nt(dqb_ref[...], wuq_ref[...])
        d_n = d_cqn * gq_ref[...]
        dcq_ref[...] = (rq * (d_n - nq_ * jnp.mean(d_n * nq_, axis=-1, keepdims=True))).astype(BF16)

        ckv = ckv_ref[...]
        rkv = lax.rsqrt(jnp.mean(ckv * ckv, axis=-1, keepdims=True) + EPS)
        nkv = ckv * rkv
        d_ckvn = _dot_nt(dkb_ref[...], wuk_ref[...]) + _dot_nt(dvb, wuv_ref[...])
        d_n2 = d_ckvn * gkv_ref[...]
        dckv_ref[...] = (rkv * (d_n2 - nkv * jnp.mean(d_n2 * nkv, axis=-1, keepdims=True))).astype(BF16)

        @pl.when(i == 0)
        def _():
            small_ref[...] = jnp.zeros_like(small_ref)

        small_ref[0:1, :] += jnp.sum(d_cqn * nq_, axis=0, keepdims=True)
        small_ref[1:2, :KV_LORA] += jnp.sum(d_ckvn * nkv, axis=0, keepdims=True)

    def row(width, idx=0):
        return pl.BlockSpec((TM, width), lambda i: (i, idx))

    def full(a):
        return pl.BlockSpec(a.shape, lambda i: (0, 0))

    b16 = lambda w: jax.ShapeDtypeStruct((s, w), BF16)
    return pl.pallas_call(
        body, name="mla_prep_bwd", grid=(s // TM,),
        in_specs=[row(1024), row(1024), row(D_GRP), row(Q_LORA, 4), row(KV_LORA, 10), full(gq), full(gkv),
                  full(wuq), full(wuk), full(wuv), row(LANES), row(LANES)],
        out_specs=(row(Q_LORA), row(KV_LORA), row(LANES), row(1024), row(1024), row(D_GRP),
                   pl.BlockSpec((8, Q_LORA), lambda i: (0, 0))),
        out_shape=(b16(Q_LORA), b16(KV_LORA), b16(LANES), b16(1024), b16(1024), b16(D_GRP),
                   jax.ShapeDtypeStruct((8, Q_LORA), F32)),
        compiler_params=_params(("arbitrary",), 10),
    )(dqp, dkp, dvv, rest, rest, gq, gkv, wuq, wuk, wuv, cos_t, sin_t)


def _in_bwd(x, g, dx1, pieces, w, sums):
    s = x.shape[0]
    steps = s // TM_IO
    widths = [a.shape[1] for a in pieces]
    offs = [sum(widths[:k]) for k in range(len(widths))]
    n_pc, n_op = len(pieces), len(sums)
    ride_in, ride_out, ride_shape, ride_sems = _chip_specs(sums)

    def body(x_ref, g_ref, dx1_ref, *refs):
        piece_refs = refs[:n_pc]
        w_ref = refs[n_pc]
        s_refs = refs[n_pc + 1:n_pc + 1 + n_op]
        dx_ref, small_ref = refs[n_pc + 1 + n_op:n_pc + 3 + n_op]
        l_refs = refs[n_pc + 3 + n_op:n_pc + 3 + 2 * n_op]
        ssem, rsem = refs[n_pc + 3 + 2 * n_op:]
        i = pl.program_id(0)

        @pl.when(i == 0)
        def _():
            for cp in _chip_copies(s_refs, l_refs, ssem, rsem):
                cp.start()

        dh = jnp.zeros((TM_IO, D_MODEL), F32)
        for pr, off, wd in zip(piece_refs, offs, widths):
            dh = dh + _dot_nt(pr[...], w_ref[:, off:off + wd])
        xv = x_ref[...]
        r = lax.rsqrt(jnp.mean(xv * xv, axis=-1, keepdims=True) + EPS)
        n = xv * r
        d_n = dh * g_ref[...]
        dx_ref[...] = dx1_ref[...] + r * (d_n - n * jnp.mean(d_n * n, axis=-1, keepdims=True))

        @pl.when(i == 0)
        def _():
            small_ref[...] = jnp.zeros_like(small_ref)

        small_ref[0:1, :] += jnp.sum(dh * n, axis=0, keepdims=True)

        @pl.when(i == steps - 1)
        def _():
            for cp in _chip_copies(s_refs, l_refs, ssem, rsem):
                cp.wait()

    def row(width):
        return pl.BlockSpec((TM_IO, width), lambda i: (i, 0))

    return pl.pallas_call(
        body, name="in_bwd", grid=(steps,),
        in_specs=[row(D_MODEL), pl.BlockSpec((1, D_MODEL), lambda i: (0, 0)), row(D_MODEL)]
        + [row(wd) for wd in widths] + [pl.BlockSpec(w.shape, lambda i: (0, 0))] + ride_in,
        out_specs=[row(D_MODEL), pl.BlockSpec((8, D_MODEL), lambda i: (0, 0))] + ride_out,
        out_shape=[jax.ShapeDtypeStruct((s, D_MODEL), F32), jax.ShapeDtypeStruct((8, D_MODEL), F32)]
        + ride_shape,
        scratch_shapes=ride_sems,
        compiler_params=_params(("arbitrary",), 32),
    )(x, g, dx1, *pieces, w, *sums)


def _tn_matmul(a, b, name, blocked=False):
    s, k = a.shape
    n = b.shape[1]
    ts = min(s, TS_DW)
    tn = n if blocked else min(n, 512)
    steps = s // ts

    def body(a_ref, b_ref, o_ref):
        t = pl.program_id(1)

        @pl.when(t == 0)
        def _():
            o_ref[...] = jnp.zeros_like(o_ref)

        prod = _dot_tn(a_ref[...], b_ref[...])
        if blocked:
            for j in range(n // LANES):
                o_ref[j] += prod[:, j * LANES:(j + 1) * LANES]
        else:
            o_ref[...] += prod

    if blocked:
        out_spec = pl.BlockSpec((n // LANES, k, LANES), lambda j, t: (0, 0, 0))
        out_shape = jax.ShapeDtypeStruct((n // LANES, k, LANES), F32)
    else:
        out_spec = pl.BlockSpec((k, tn), lambda j, t: (0, j))
        out_shape = jax.ShapeDtypeStruct((k, n), F32)
    return pl.pallas_call(
        body, name=name, grid=(n // tn, steps),
        in_specs=[pl.BlockSpec((ts, k), lambda j, t: (t, 0)), pl.BlockSpec((ts, tn), lambda j, t: (t, j))],
        out_specs=out_spec, out_shape=out_shape,
        compiler_params=_params(("parallel", "arbitrary"), 12),
    )(a, b)


def _tn_matmul_multi(a, bs, name):
    s, k = a.shape
    widths = [b.shape[1] for b in bs]
    ts = min(s, TS_DW)

    def body(a_ref, *refs):
        b_refs, o_ref = refs[:-1], refs[-1]
        t = pl.program_id(0)

        @pl.when(t == 0)
        def _():
            o_ref[...] = jnp.zeros_like(o_ref)

        av = a_ref[...]
        off = 0
        for b_ref, wd in zip(b_refs, widths):
            o_ref[:, off:off + wd] += _dot_tn(av, b_ref[...])
            off += wd

    return pl.pallas_call(
        body, name=name, grid=(s // ts,),
        in_specs=[pl.BlockSpec((ts, k), lambda t: (t, 0))] + [pl.BlockSpec((ts, wd), lambda t: (t, 0)) for wd in widths],
        out_specs=pl.BlockSpec((k, sum(widths)), lambda t: (0, 0)),
        out_shape=jax.ShapeDtypeStruct((k, sum(widths)), F32),
        compiler_params=_params(("arbitrary",), 12),
    )(a, *bs)


IN_SHARD = 372
_IN_KERNEL_ORDER = ((0, 2048), (2464, 2976), (2048, 2432))
_IN_ROPE = (2432, 2464)
_IN_GRAD_SRC = ((0, 512, 0, 0), (512, 1024, 0, 512), (1024, 1536, 1, 0), (1536, 2048, 1, 512),
                (2048, 2304, 2, 512), (2304, 2432, 2, 768), (2432, 2464, 2, 960), (2464, 2976, 2, 0))


def _shard_cols(gath_in, lo, hi):
    out = []
    while lo < hi:
        j, a = divmod(lo, IN_SHARD)
        b = min(IN_SHARD, a + hi - lo)
        out.append(gath_in[j][:, a:b])
        lo += b - a
    return out


def _kernel_w_in(g_in):
    zc = lambda n: jnp.zeros((D_MODEL, n), BF16)
    parts = [pc for lo, hi in _IN_KERNEL_ORDER for pc in _shard_cols(g_in, lo, hi)]
    parts += [zc(64)] + _shard_cols(g_in, *_IN_ROPE) + [zc(32)]
    return jnp.concatenate(parts, axis=1)


def _kernel_weights(gath):
    g_uq, g_ukv, g_out, g_ple, g_pg = gath
    w_uq_p = jnp.pad(g_uq, ((0, 0), (0, 0), (0, 32))).transpose(1, 0, 2).reshape(Q_LORA, 1024)
    k_only = jnp.where(jnp.arange(LANES) < HEAD_DIM, g_ukv, jnp.zeros_like(g_ukv))
    w_uk_p = k_only.transpose(1, 0, 2).reshape(KV_LORA, 1024)
    w_uv = g_ukv[:, :, HEAD_DIM:].transpose(1, 0, 2).reshape(KV_LORA, D_GRP)
    w_ple = g_ple.transpose(1, 0, 2).reshape(PLE_DIM, D_MODEL)
    return (w_uq_p, w_uk_p, w_uv, g_out.reshape(D_MODEL, D_MODEL), w_ple, g_pg.reshape(D_MODEL, D_MODEL))


def _payload_in(d_cols):
    blocks = []
    for j in range(N_DEV):
        lo, hi = j * IN_SHARD, (j + 1) * IN_SHARD
        parts = []
        for o_lo, o_hi, idx, off in _IN_GRAD_SRC:
            a, b = max(lo, o_lo), min(hi, o_hi)
            if a < b:
                parts.append(d_cols[idx][:, off + a - o_lo:off + b - o_lo])
        blocks.append(jnp.concatenate(parts, axis=1))
    return jnp.stack(blocks)


def _payload_ukv(duk_blk, d_uv):
    dv_blk = d_uv.reshape(KV_LORA, N_HEADS, HEAD_DIM).transpose(1, 0, 2)
    return jnp.concatenate([duk_blk[:, :, :HEAD_DIM], dv_blk], axis=2)


def kernel(x, p, positions, norm_pre_g, w_in, q_norm_g, w_uq, kv_norm_g, w_ukv, sb_out_norm_g, mla_out_norm_g, w_out, norm_post_g, w_ple, ple_norm_g, w_ple_gate, b_ple_gate, loss_target, m_norm_pre_g, m_w_in, m_q_norm_g, m_w_uq, m_kv_norm_g, m_w_ukv, m_sb_out_norm_g, m_mla_out_norm_g, m_w_out, m_norm_post_g, m_w_ple, m_ple_norm_g, m_w_ple_gate, m_b_ple_gate, v_norm_pre_g, v_w_in, v_q_norm_g, v_w_uq, v_kv_norm_g, v_w_ukv, v_sb_out_norm_g, v_mla_out_norm_g, v_w_out, v_norm_post_g, v_w_ple, v_ple_norm_g, v_w_ple_gate, v_b_ple_gate):
    mats = (w_in, w_uq, w_ukv, w_out, w_ple, w_ple_gate)
    m_mats = (m_w_in, m_w_uq, m_w_ukv, m_w_out, m_w_ple, m_w_ple_gate)
    v_mats = (v_w_in, v_w_uq, v_w_ukv, v_w_out, v_w_ple, v_w_ple_gate)
    vecs = (norm_pre_g, q_norm_g, kv_norm_g, sb_out_norm_g, mla_out_norm_g, norm_post_g, ple_norm_g, b_ple_gate)
    m_vecs = (m_norm_pre_g, m_q_norm_g, m_kv_norm_g, m_sb_out_norm_g, m_mla_out_norm_g, m_norm_post_g,
              m_ple_norm_g, m_b_ple_gate)
    v_vecs = (v_norm_pre_g, v_q_norm_g, v_kv_norm_g, v_sb_out_norm_g, v_mla_out_norm_g, v_norm_post_g,
              v_ple_norm_g, v_b_ple_gate)

    shards = [a[0].astype(BF16) for a in mats]
    w_in_p = _kernel_w_in(_all_gather(shards[:1])[0])
    grad_x, reduced, vec_slab = _step(x[0], p[0, 0], positions[0], loss_target[0], *vecs, w_in_p, shards[1:])
    upd = [_adamw_matrix(own, l2, w, m, v, "adamw_%d" % o)
           for o, ((own, l2), w, m, v) in enumerate(zip(reduced, mats, m_mats, v_mats))]
    sm = _adamw_vectors(_slab_exchange(vec_slab), vecs, m_vecs, v_vecs)

    outs = []
    for kind in range(4):
        mat = [upd[o][kind] for o in range(len(mats))]
        vec = sm[1 + 8 * kind:9 + 8 * kind]
        outs += [vec[0], mat[0], vec[1], mat[1], vec[2], mat[2], vec[3], vec[4], mat[3], vec[5],
                 mat[4], vec[6], mat[5], vec[7]]
    return (sm[0][0, 0], grad_x[None], *outs)


def _step(xs, ps, pos, tgt, norm_pre_g, q_norm_g, kv_norm_g, sb_out_norm_g, mla_out_norm_g,
          norm_post_g, ple_norm_g, b_ple_gate, w_in_p, shards):
    s = xs.shape[0]
    place = jnp.stack([lax.axis_index("c"), 2 * lax.axis_index("x") + lax.axis_index("y")]).astype(jnp.int32)

    half = ROPE_DIM // 2
    freq = ROPE_THETA ** (-jnp.arange(half, dtype=F32) / half)
    ang = pos.astype(F32)[:, None] * freq
    cos, sin = jnp.cos(ang), jnp.sin(ang)
    cos_t = jnp.concatenate([jnp.ones((s, 64), F32), cos, cos, jnp.zeros((s, 32), F32)], axis=1)
    sin_t = jnp.concatenate([jnp.zeros((s, 64), F32), -sin, sin, jnp.zeros((s, 32), F32)], axis=1)
    seg = jnp.arange(D_GRP) // HEAD_DIM
    bd = (seg[:, None] == seg[None, :]).astype(BF16)

    qkv, rest, h_b, *gath = _in_proj(xs, norm_pre_g, w_in_p, shards)
    w_uq_p, w_uk_p, w_uv, f_out, f_ple, f_pg = _kernel_weights(gath)
    sb_o = _sb_fwd(qkv, 8)
    qp, kp, vv, cqn_b, ckvn_b = _mla_prep(rest, q_norm_g, kv_norm_g, w_uq_p, w_uk_p, w_uv, cos_t, sin_t)
    mla_o, lse = _mla_fwd(qp, kp, vv, 4)

    (dx1, d_sbo, d_mlo, d_sbg, d_mlg, x1_b, dgl_b, yc_b, dy_b, p_b, du_b, small_mid) = _mid(
        xs, ps, tgt, sb_o, mla_o, rest, sb_out_norm_g, mla_out_norm_g, f_out, norm_post_g,
        f_ple, ple_norm_g, f_pg, b_ple_gate, bd)
    pay_a = [_tn_matmul(yc_b, dy_b, "dw_out").reshape(N_DEV, 128, D_MODEL),
             _tn_matmul(p_b, du_b, "dw_ple", blocked=True),
             _tn_matmul(x1_b, dgl_b, "dw_pg").reshape(N_DEV, 128, D_MODEL)]
    dqp, dkp, dvv, *sib_a = _mla_bwd(qp, kp, vv, d_mlo, mla_o, lse, 4, pay_a)
    pair_a = _pair_sums(pay_a, sib_a, place, "grad_pair_sums_a")
    dq_sb, dk_sb, dv_sb, *landed_a = _sb_bwd(qkv, d_sbo, [sm for sm, _ in pair_a])
    dcq, dckv, dkr, dq_b, dk_b, dv_b, small_prep = _mla_prep_bwd(
        dqp, dkp, dvv, rest, q_norm_g, kv_norm_g, w_uq_p, w_uk_p, w_uv, cos_t, sin_t)
    pieces = [dq_sb, dk_sb, dv_sb, d_sbg, d_mlg, dcq, dckv, dkr]
    d_cols = [_tn_matmul_multi(h_b, pieces[0:2], "dw_in_0"), _tn_matmul_multi(h_b, pieces[2:4], "dw_in_1"),
              _tn_matmul_multi(h_b, pieces[4:8], "dw_in_2")]
    pay_b = [_payload_in(d_cols), _tn_matmul(cqn_b, dq_b, "dw_uq", blocked=True),
             _payload_ukv(_tn_matmul(ckvn_b, dk_b, "dw_uk", blocked=True), _tn_matmul(ckvn_b, dv_b, "dw_uv"))]
    pair_b = _pair_sums(pay_b, _pair_exchange(pay_b, "grad_pair_exchange"), place, "grad_pair_sums_b")
    grad_x, small_in, *landed_b = _in_bwd(xs, norm_pre_g, dx1, pieces, w_in_p, [sm for sm, _ in pair_b])
    reduced = [(own, l2) for (_, own), l2 in zip(pair_b + pair_a, landed_b + landed_a)]
    slab = jnp.concatenate([small_in[0:1], jnp.pad(small_prep[0:2], ((0, 0), (0, D_MODEL - Q_LORA))),
                            small_mid[3:8]], axis=0)
    return grad_x, reduced, slab
```

```python
import jax
import jax.numpy as jnp
from jax import lax
from jax.experimental import pallas as pl
from jax.experimental.pallas import tpu as pltpu

F32 = jnp.float32
BF16 = jnp.bfloat16
MESH = pl.DeviceIdType.MESH

N_DEV = 8
D_MODEL = 1024
N_HEADS = 8
HEAD_DIM = 64
D_GRP = N_HEADS * HEAD_DIM
Q_LORA = 256
KV_LORA = 128
ROPE_DIM = 32
PLE_DIM = 256
CHUNK_SHIFT = 6
ROPE_THETA = 10000.0
EPS = 1e-6
SB_SCALE = HEAD_DIM ** -0.5
MLA_SCALE = (HEAD_DIM + ROPE_DIM) ** -0.5
NEG = -1e30
LOG2_E = 1.4426950408889634
LN_2 = 0.6931471805599453
SB_CUTOFF = 110.0

ADAM_LR = 0.001
ADAM_B1 = 0.9
ADAM_B2 = 0.999
ADAM_EPS = 1e-08
ADAM_WD = 0.01
ADAM_STEP = 10

LANES = 128
TQ = 256
TK = 256
TM = 256
TM_IO = 512
TS_DW = 2048

D_IN_P = 3072

_NT = (((1,), (1,)), ((), ()))
_TN = (((0,), (0,)), ((), ()))


def _params(sem, vmem_mb):
    return pltpu.CompilerParams(dimension_semantics=sem, vmem_limit_bytes=vmem_mb << 20)


def _dot(a, b):
    return jnp.dot(a, b, preferred_element_type=F32)


def _dot_nt(a, b):
    return lax.dot_general(a, b, _NT, preferred_element_type=F32)


def _dot_tn(a, b):
    return lax.dot_general(a, b, _TN, preferred_element_type=F32)


def _hl_dot(a, b):
    hi = a.astype(BF16)
    lo = (a - hi.astype(F32)).astype(BF16)
    return _dot(hi, b) + _dot(lo, b)


def _sigmoid(x):
    return 1.0 / (1.0 + jnp.exp(-x))


def _rope_swap(x, lane):
    left = pltpu.roll(x, LANES - 16, axis=1)
    right = pltpu.roll(x, 16, axis=1)
    lo = (lane >= 64) & (lane < 80)
    hi = (lane >= 80) & (lane < 96)
    return jnp.where(lo, left, jnp.where(hi, right, 0.0))


def _two_level_gather(x_refs, out_refs, send_sems, recv_sems, local_sems):
    x, y, c = lax.axis_index("x"), lax.axis_index("y"), lax.axis_index("c")
    me, sibling = (x, y, c), (x, y, 1 - c)
    chips = [(1 - x, y), (x, 1 - y), (1 - x, 1 - y)]
    ops = range(len(x_refs))

    def slot(o, px, py, pc):
        return out_refs[o].at[4 * px + 2 * py + pc]

    def copy(o, k, block, to, src=None):
        return pltpu.make_async_remote_copy(
            src_ref=slot(o, *block) if src is None else src, dst_ref=slot(o, *block),
            send_sem=send_sems.at[o, k], recv_sem=recv_sems.at[o, k],
            device_id=to, device_id_type=MESH)

    def mine():
        return [pltpu.make_async_copy(x_refs[o], slot(o, *me), local_sems.at[o]) for o in ops]

    def first():
        return ([copy(o, 0, me, sibling, src=x_refs[o]) for o in ops]
                + [copy(o, 1 + j, me, (*chip, c), src=x_refs[o]) for j, chip in enumerate(chips) for o in ops])

    def start():
        for cp in mine() + first():
            cp.start()

    def finish():
        passed = []
        for j, chip in enumerate(chips):
            for o in ops:
                copy(o, 1 + j, (*chip, c), me).wait_recv()
                passed.append(copy(o, 4 + j, (*chip, c), sibling))
                passed[-1].start()
        for o in ops:
            copy(o, 0, sibling, me).wait_recv()
        for j, chip in enumerate(chips):
            for o in ops:
                copy(o, 4 + j, (*chip, 1 - c), me).wait_recv()
        for cp in first() + passed:
            cp.wait_send()
        for cp in mine():
            cp.wait()

    return start, finish


def _gather_sems(n_op):
    return [pltpu.SemaphoreType.DMA((n_op, 7)), pltpu.SemaphoreType.DMA((n_op, 7)),
            pltpu.SemaphoreType.DMA((n_op,))]


def _all_gather(shards):
    n_op = len(shards)

    def body(*refs):
        start, finish = _two_level_gather(refs[:n_op], refs[n_op:2 * n_op], *refs[2 * n_op:])
        start()
        finish()

    any_spec = pl.BlockSpec(memory_space=pl.ANY)
    return pl.pallas_call(
        body, name="weight_all_gather",
        out_shape=[jax.ShapeDtypeStruct((N_DEV,) + a.shape, a.dtype) for a in shards],
        in_specs=[any_spec] * n_op, out_specs=[any_spec] * n_op, scratch_shapes=_gather_sems(n_op),
        compiler_params=pltpu.CompilerParams(vmem_limit_bytes=4 << 20),
    )(*shards)


def _pair_copies(g_refs, l_refs, ssem, rsem):
    x, y, c = lax.axis_index("x"), lax.axis_index("y"), lax.axis_index("c")
    copies = []
    for o in range(len(g_refs)):
        for chip in range(4):
            copies.append(pltpu.make_async_remote_copy(
                src_ref=g_refs[o].at[2 * chip + (1 - c)], dst_ref=l_refs[o].at[chip],
                send_sem=ssem.at[o, chip], recv_sem=rsem.at[o, chip],
                device_id=(x, y, 1 - c), device_id_type=MESH))
    return copies


def _pair_specs(pays):
    n_op = len(pays)
    any_spec = pl.BlockSpec(memory_space=pl.ANY)
    return ([any_spec] * n_op, [any_spec] * n_op,
            [jax.ShapeDtypeStruct((4,) + a.shape[1:], F32) for a in pays],
            [pltpu.SemaphoreType.DMA((n_op, 4)), pltpu.SemaphoreType.DMA((n_op, 4))])


def _pair_exchange(pays, name):
    n_op = len(pays)
    in_specs, out_specs, out_shape, sems = _pair_specs(pays)

    def body(*refs):
        copies = _pair_copies(refs[:n_op], refs[n_op:2 * n_op], *refs[2 * n_op:])
        for cp in copies:
            cp.start()
        for cp in copies:
            cp.wait()

    return pl.pallas_call(body, name=name, out_shape=out_shape, in_specs=in_specs, out_specs=out_specs,
                          scratch_shapes=sems,
                          compiler_params=pltpu.CompilerParams(vmem_limit_bytes=4 << 20))(*pays)


def _slab_exchange(small):
    sr, n = small.shape

    def body(s_ref, sland_ref, ssem, rsem, lsem):
        x, y, c = lax.axis_index("x"), lax.axis_index("y"), lax.axis_index("c")
        me = 4 * x + 2 * y + c
        copies = []
        for k in range(1, N_DEV):
            peer = (1 - x if (k >> 2) & 1 else x, 1 - y if (k >> 1) & 1 else y, 1 - c if k & 1 else c)
            copies.append(pltpu.make_async_remote_copy(
                src_ref=s_ref, dst_ref=sland_ref.at[me], send_sem=ssem.at[k], recv_sem=rsem.at[k],
                device_id=peer, device_id_type=MESH))
        own = pltpu.make_async_copy(s_ref, sland_ref.at[me], lsem)
        own.start()
        for cp in copies:
            cp.start()
        for cp in copies:
            cp.wait()
        own.wait()

    any_spec = pl.BlockSpec(memory_space=pl.ANY)
    return pl.pallas_call(
        body, name="grad_slab_exchange", out_shape=jax.ShapeDtypeStruct((N_DEV, sr, n), F32),
        in_specs=[any_spec], out_specs=any_spec,
        scratch_shapes=[pltpu.SemaphoreType.DMA((N_DEV,)), pltpu.SemaphoreType.DMA((N_DEV,)),
                        pltpu.SemaphoreType.DMA],
        compiler_params=pltpu.CompilerParams(vmem_limit_bytes=4 << 20),
    )(small)


def _pair_sums(pays, landed, place, name):
    n = len(pays)
    dims = [p.shape[1:] for p in pays]

    def body(place_ref, *refs):
        g_refs, l_refs, s_refs, own_refs = refs[:n], refs[n:2 * n], refs[2 * n:3 * n], refs[3 * n:]
        i = pl.program_id(0)
        for o in range(n):
            tot = g_refs[o][...] + l_refs[o][...]
            s_refs[o][...] = tot.astype(BF16)

            @pl.when(i == place_ref[1])
            def _(o=o, tot=tot):
                own_refs[o][...] = tot

    grid_spec = pltpu.PrefetchScalarGridSpec(
        num_scalar_prefetch=1, grid=(4,),
        in_specs=[pl.BlockSpec((None, r, c), lambda i, pr: (2 * i + pr[0], 0, 0)) for r, c in dims]
        + [pl.BlockSpec((None, r, c), lambda i, pr: (i, 0, 0)) for r, c in dims],
        out_specs=[pl.BlockSpec((None, r, c), lambda i, pr: (i, 0, 0)) for r, c in dims]
        + [pl.BlockSpec((r, c), lambda i, pr: (0, 0)) for r, c in dims])
    out = pl.pallas_call(
        body, name=name, grid_spec=grid_spec,
        out_shape=[jax.ShapeDtypeStruct((4, r, c), BF16) for r, c in dims]
        + [jax.ShapeDtypeStruct((r, c), F32) for r, c in dims],
        compiler_params=_params(("arbitrary",), 16),
    )(place, *pays, *landed)
    return list(zip(out[:n], out[n:]))


def _chip_copies(s_refs, l_refs, ssem, rsem):
    x, y, c = lax.axis_index("x"), lax.axis_index("y"), lax.axis_index("c")
    copies = []
    for rel in range(1, 4):
        px = 1 - x if rel & 2 else x
        py = 1 - y if rel & 1 else y
        for o in range(len(s_refs)):
            copies.append(pltpu.make_async_remote_copy(
                src_ref=s_refs[o].at[2 * px + py], dst_ref=l_refs[o].at[rel - 1],
                send_sem=ssem.at[o, rel - 1], recv_sem=rsem.at[o, rel - 1],
                device_id=(px, py, c), device_id_type=MESH))
    return copies


def _chip_specs(sums):
    n_op = len(sums)
    any_spec = pl.BlockSpec(memory_space=pl.ANY)
    return ([any_spec] * n_op, [any_spec] * n_op,
            [jax.ShapeDtypeStruct((3,) + a.shape[1:], BF16) for a in sums],
            [pltpu.SemaphoreType.DMA((n_op, 3)), pltpu.SemaphoreType.DMA((n_op, 3))])


def _adamw_math(g, w, m, v):
    mn = ADAM_B1 * m + (1.0 - ADAM_B1) * g
    vn = ADAM_B2 * v + (1.0 - ADAM_B2) * (g * g)
    m_hat = mn / (1.0 - ADAM_B1 ** ADAM_STEP)
    v_hat = vn / (1.0 - ADAM_B2 ** ADAM_STEP)
    return -ADAM_LR * (m_hat / (jnp.sqrt(v_hat) + ADAM_EPS) + ADAM_WD * w), mn, vn


def _adamw_matrix(own, landed, w, m, v, name):
    _, r, c = w.shape
    cp = own.shape[1]
    br = min(r, 256)

    def body(own_ref, l_ref, w_ref, m_ref, v_ref, g_out, d_out, m_out, v_out):
        g = own_ref[...]
        for k in range(3):
            g = g + l_ref[k].astype(F32)
        g = g[:, :c]
        g_out[...] = g
        d_out[...], m_out[...], v_out[...] = _adamw_math(g, w_ref[...], m_ref[...], v_ref[...])

    row = pl.BlockSpec((None, br, c), lambda i: (0, i, 0))
    shp = jax.ShapeDtypeStruct((1, r, c), F32)
    return pl.pallas_call(
        body, name=name, grid=(r // br,),
        in_specs=[pl.BlockSpec((br, cp), lambda i: (i, 0)), pl.BlockSpec((3, br, cp), lambda i: (0, i, 0)),
                  row, row, row],
        out_specs=(row, row, row, row), out_shape=(shp, shp, shp, shp),
        compiler_params=_params(("parallel",), 12),
    )(own, landed, w, m, v)


_VEC_PLACE = ((0, 0), (1, 0), (2, 0), (3, 0), (3, D_GRP), (4, 0), (5, 0), (6, 0))


def _adamw_vectors(sland, ws, ms, vs):
    nv = len(ws)

    def body(l_ref, *refs):
        w_refs, m_refs, v_refs = refs[:nv], refs[nv:2 * nv], refs[2 * nv:3 * nv]
        loss_ref = refs[3 * nv]
        outs = refs[3 * nv + 1:]
        g_all = l_ref[0]
        for j in range(1, N_DEV):
            g_all = g_all + l_ref[j]
        loss_ref[...] = jnp.sum(g_all[7:8, :], axis=1, keepdims=True)
        for k, (row, lane0) in enumerate(_VEC_PLACE):
            n = w_refs[k].shape[1]
            g = g_all[row:row + 1, lane0:lane0 + n]
            d, mn, vn = _adamw_math(g, w_refs[k][...], m_refs[k][...], v_refs[k][...])
            outs[k][...] = g
            outs[nv + k][...] = d
            outs[2 * nv + k][...] = mn
            outs[3 * nv + k][...] = vn

    def whole(shape):
        return pl.BlockSpec(shape, lambda i: (0,) * len(shape))

    shapes = [jax.ShapeDtypeStruct(w.shape, F32) for w in ws]
    return pl.pallas_call(
        body, name="adamw_vectors", grid=(1,),
        in_specs=[whole(sland.shape)] + [whole(w.shape) for w in ws] * 3,
        out_specs=[whole((1, 1))] + [whole(w.shape) for w in ws] * 4,
        out_shape=[jax.ShapeDtypeStruct((1, 1), F32)] + shapes * 4,
        compiler_params=_params(("arbitrary",), 4),
    )(sland, *ws, *ms, *vs)


def _in_proj(x, g, w, shards):
    s = x.shape[0]
    n_op = len(shards)
    steps = s // TM_IO

    def body(x_ref, g_ref, w_ref, *refs):
        shard_refs = refs[:n_op]
        qkv_ref, rest_ref, h_ref = refs[n_op:n_op + 3]
        gath_refs = refs[n_op + 3:2 * n_op + 3]
        start, finish = _two_level_gather(shard_refs, gath_refs, *refs[2 * n_op + 3:])
        i = pl.program_id(0)

        @pl.when(i == 0)
        def _():
            start()

        xv = x_ref[...]
        r = lax.rsqrt(jnp.mean(xv * xv, axis=-1, keepdims=True) + EPS)
        h = ((xv * r) * g_ref[...]).astype(BF16)
        h_ref[...] = h
        qkv_ref[...] = _dot(h, w_ref[:, :1536]).astype(BF16)
        rest_ref[...] = _dot(h, w_ref[:, 1536:])

        @pl.when(i == steps - 1)
        def _():
            finish()

    any_spec = pl.BlockSpec(memory_space=pl.ANY)
    return pl.pallas_call(
        body, name="in_proj", grid=(steps,),
        in_specs=[pl.BlockSpec((TM_IO, D_MODEL), lambda i: (i, 0)),
                  pl.BlockSpec((1, D_MODEL), lambda i: (0, 0)),
                  pl.BlockSpec((D_MODEL, D_IN_P), lambda i: (0, 0))] + [any_spec] * n_op,
        out_specs=[pl.BlockSpec((TM_IO, 1536), lambda i: (i, 0)),
                   pl.BlockSpec((TM_IO, 1536), lambda i: (i, 0)),
                   pl.BlockSpec((TM_IO, D_MODEL), lambda i: (i, 0))] + [any_spec] * n_op,
        out_shape=[jax.ShapeDtypeStruct((s, 1536), BF16), jax.ShapeDtypeStruct((s, 1536), F32),
                   jax.ShapeDtypeStruct((s, D_MODEL), BF16)]
        + [jax.ShapeDtypeStruct((N_DEV,) + a.shape, a.dtype) for a in shards],
        scratch_shapes=_gather_sems(n_op),
        compiler_params=_params(("arbitrary",), 32),
    )(x, g, w, *shards)


def _mla_prep(rest, gq, gkv, wuq, wuk, wuv, cos_t, sin_t):
    s = rest.shape[0]

    def body(cq_ref, ckv_ref, kr_ref, gq_ref, gkv_ref, wuq_ref, wuk_ref, wuv_ref, c_ref, s_ref,
             qp_ref, kp_ref, vv_ref, cqn_ref, ckvn_ref):
        lane = lax.broadcasted_iota(jnp.int32, (1, LANES), 1)
        cos_v, sin_v = c_ref[...], s_ref[...]
        cq = cq_ref[...]
        rq = lax.rsqrt(jnp.mean(cq * cq, axis=-1, keepdims=True) + EPS)
        cqn = ((cq * rq) * gq_ref[...]).astype(BF16)
        cqn_ref[...] = cqn
        q = _dot(cqn, wuq_ref[...])
        ckv = ckv_ref[...]
        rkv = lax.rsqrt(jnp.mean(ckv * ckv, axis=-1, keepdims=True) + EPS)
        ckvn = ((ckv * rkv) * gkv_ref[...]).astype(BF16)
        ckvn_ref[...] = ckvn
        kn = _dot(ckvn, wuk_ref[...])
        vv_ref[...] = _dot(ckvn, wuv_ref[...]).astype(BF16)
        kr = kr_ref[...]
        kr_roped = kr * cos_v + _rope_swap(kr, lane) * sin_v
        for h in range(N_HEADS):
            sl = slice(h * LANES, (h + 1) * LANES)
            qh = q[:, sl]
            qp_ref[:, sl] = (qh * cos_v + _rope_swap(qh, lane) * sin_v).astype(BF16)
            kp_ref[:, sl] = (kn[:, sl] + kr_roped).astype(BF16)

    def row(width, idx):
        return pl.BlockSpec((TM, width), lambda i: (i, idx))

    def full(a):
        return pl.BlockSpec(a.shape, lambda i: (0, 0))

    return pl.pallas_call(
        body, name="mla_prep", grid=(s // TM,),
        in_specs=[row(Q_LORA, 4), row(KV_LORA, 10), row(LANES, 11), full(gq), full(gkv),
                  full(wuq), full(wuk), full(wuv), row(LANES, 0), row(LANES, 0)],
        out_specs=(row(1024, 0), row(1024, 0), row(D_GRP, 0), row(Q_LORA, 0), row(KV_LORA, 0)),
        out_shape=(jax.ShapeDtypeStruct((s, 1024), BF16), jax.ShapeDtypeStruct((s, 1024), BF16),
                   jax.ShapeDtypeStruct((s, D_GRP), BF16), jax.ShapeDtypeStruct((s, Q_LORA), BF16),
                   jax.ShapeDtypeStruct((s, KV_LORA), BF16)),
        compiler_params=_params(("parallel",), 12),
    )(rest, rest, rest, gq, gkv, wuq, wuk, wuv, cos_t, sin_t)


def _sb_live(n, qi, carries):
    top = carries[0]
    for c in carries[1:]:
        top = jnp.maximum(top, c)
    return jnp.logical_and(n < qi, jnp.max(top) > -SB_CUTOFF)


def _sb_fwd(qkv, hb):
    s = qkv.shape[0]

    def body(q_ref, k_ref, v_ref, o_ref, acc):
        qi = pl.program_id(1)
        lane = lax.broadcasted_iota(jnp.int32, (1, LANES), 1)
        is_a = lane < HEAD_DIM
        pair = lambda h: slice((h // 2) * LANES, (h // 2 + 1) * LANES)
        q_h = []
        for h in range(hb):
            qs = q_ref[:, pair(h)] * SB_SCALE
            mine = is_a if h % 2 == 0 else jnp.logical_not(is_a)
            q_h.append(jnp.where(mine, qs, jnp.zeros_like(qs)))
        r_i = lax.broadcasted_iota(jnp.int32, (TQ, TK), 0)
        c_i = lax.broadcasted_iota(jnp.int32, (TQ, TK), 1)
        past = c_i < r_i
        upper = (r_i > c_i).astype(BF16)
        acc[...] = jnp.zeros_like(acc)

        def tile(j, carries, diag):
            ks = pl.ds(pl.multiple_of(j * TK, TK), TK)
            zs = [_dot_nt(q_h[h], k_ref[ks, pair(h)]) for h in range(hb)]
            if diag:
                zs = [jnp.where(past, z, NEG) for z in zs]
            lfs = [-(jnp.maximum(z, 0.0) + jnp.log(1.0 + jnp.exp(-jnp.abs(z)))) for z in zs]
            sufs = [_hl_dot(lfs[h], upper) for h in range(hb)]
            out = []
            for h in range(hb):
                w = jnp.exp(zs[h] + lfs[h] + (sufs[h] + carries[h]))
                acc[h] += _dot(w.astype(BF16), v_ref[ks, pair(h)])
                out.append(carries[h] + jnp.sum(lfs[h], axis=1, keepdims=True))
            return tuple(out)

        zero = jnp.zeros((TQ, 1), F32)
        carries = tile(qi, (zero,) * hb, True)

        def step(st):
            return (st[0] + 1,) + tile(qi - 1 - st[0], st[1:], False)

        lax.while_loop(lambda st: _sb_live(st[0], qi, st[1:]), step, (0,) + carries)
        for pr in range(hb // 2):
            o_ref[:, pr * LANES:(pr + 1) * LANES] = jnp.where(is_a, acc[2 * pr], acc[2 * pr + 1])

    width = hb * HEAD_DIM
    nb = D_GRP // width
    slab = lambda part: pl.BlockSpec((s, width), lambda g, qi: (0, part * nb + g))
    blk = pl.BlockSpec((TQ, width), lambda g, qi: (qi, g))
    return pl.pallas_call(
        body, name="sb_fwd", grid=(nb, s // TQ),
        in_specs=[blk, slab(1), slab(2)], out_specs=blk,
        out_shape=jax.ShapeDtypeStruct((s, D_GRP), F32),
        scratch_shapes=[pltpu.VMEM((hb, TQ, LANES), F32)],
        compiler_params=_params(("arbitrary", "arbitrary"), 28),
    )(qkv, qkv, qkv)


def _sb_bwd(qkv, d_o, sums):
    s = qkv.shape[0]
    nq = s // TQ
    nk = s // TK
    n_op = len(sums)
    ride_in, ride_out, ride_shape, ride_sems = _chip_specs(sums)

    def body(q_ref, k_ref, v_ref, do_ref, *refs):
        s_refs = refs[:n_op]
        dq_ref, dk_ref, dv_ref = refs[n_op:n_op + 3]
        l_refs = refs[n_op + 3:2 * n_op + 3]
        x1s, bts, dqacc, dkacc, dvacc, ssem, rsem = refs[2 * n_op + 3:]
        qi = pl.program_id(1)
        first_step = jnp.logical_and(pl.program_id(0) == 0, qi == 0)
        last_step = jnp.logical_and(pl.program_id(0) == pl.num_programs(0) - 1, qi == nq - 1)

        @pl.when(first_step)
        def _():
            for cp in _chip_copies(s_refs, l_refs, ssem, rsem):
                cp.start()

        lane = lax.broadcasted_iota(jnp.int32, (1, LANES), 1)
        is_a = lane < HEAD_DIM

        @pl.when(qi == 0)
        def _():
            dkacc[...] = jnp.zeros_like(dkacc)
            dvacc[...] = jnp.zeros_like(dvacc)

        qs = q_ref[...] * SB_SCALE
        zq = jnp.zeros_like(qs)
        qs_x = (jnp.where(is_a, qs, zq), jnp.where(is_a, zq, qs))
        dob = do_ref[...].astype(BF16)
        do_x = (jnp.where(is_a, dob, zq), jnp.where(is_a, zq, dob))
        r_i = lax.broadcasted_iota(jnp.int32, (TQ, TK), 0)
        c_i = lax.broadcasted_iota(jnp.int32, (TQ, TK), 1)
        past = c_i < r_i
        upper = (r_i > c_i).astype(BF16)
        upper_incl = (r_i >= c_i).astype(BF16)
        dqacc[...] = jnp.zeros_like(dqacc)
        both = ((0, 0), (0, 1), (1, 0), (1, 1))

        def tiles(n):
            j_hi = qi - 2 * n
            lo_ok = j_hi >= 1
            j_lo = jnp.maximum(j_hi - 1, 0)
            ks = (pl.ds(pl.multiple_of(j_hi * TK, TK), TK), pl.ds(pl.multiple_of(j_lo * TK, TK), TK))
            return j_hi, lo_ok, j_lo, ks

        def sweep(n, carries):
            j_hi, lo_ok, j_lo, ks = tiles(n)
            slot = (j_hi, jnp.where(lo_ok, j_lo, nk))
            valid = (jnp.logical_or(past, j_hi < qi), lo_ok)
            z = {th: jnp.where(valid[th[0]], _dot_nt(qs_x[th[1]], k_ref[ks[th[0]], :]), NEG) for th in both}
            log_b, lf_sum, suf = {}, {}, {}
            for th in both:
                lf = -(jnp.maximum(z[th], 0.0) + jnp.log(1.0 + jnp.exp(-jnp.abs(z[th]))))
                log_b[th] = z[th] + lf
                lf_sum[th] = jnp.sum(lf, axis=1, keepdims=True)
                suf[th] = _hl_dot(lf, upper)
            c, g_in = {}, {}
            for h in range(2):
                c[0, h], g_in[0, h] = carries[2 * h], carries[2 * h + 1]
                c[1, h] = c[0, h] + lf_sum[0, h]
            d_a = {th: _dot_nt(do_x[th[1]], v_ref[ks[th[0]], :]) for th in both}
            a_b, g, g_sum, sg = {}, {}, {}, {}
            for th in both:
                a = jnp.exp(log_b[th] + (suf[th] + c[th]))
                a_b[th] = a.astype(BF16)
                g[th] = a * d_a[th]
                g_sum[th] = jnp.sum(g[th], axis=1, keepdims=True)
                sg[th] = _hl_dot(g[th], upper_incl)
            for h in range(2):
                g_in[1, h] = g_in[0, h] + g_sum[0, h]
            for th in both:
                t, h = th
                beta = jnp.exp(log_b[th])
                x1s[slot[t], h] = g[th] * (1.0 - beta) + beta * (sg[th] + g_in[th])
                bts[slot[t], h] = beta
                dvacc[ks[t], :] += _dot_tn(a_b[th], do_x[h])
            out = []
            for h in range(2):
                out.append(c[1, h] + lf_sum[1, h])
                out.append(g_in[1, h] + g_sum[1, h])
            return tuple(out)

        zero = jnp.zeros((TQ, 1), F32)
        first = sweep(0, (zero, zero, zero, zero))

        def more(st):
            return jnp.logical_and(2 * st[0] <= qi, jnp.max(jnp.maximum(st[1], st[3])) > -SB_CUTOFF)

        swept = lax.while_loop(more, lambda st: (st[0] + 1,) + sweep(st[0], st[1:]), (1,) + first)
        g_tot = (swept[2], swept[4])

        def apply(n, carry):
            j_hi, lo_ok, j_lo, ks = tiles(n)

            def one(j, kslice):
                for h in range(2):
                    dz = (x1s[j, h] - bts[j, h] * g_tot[h]).astype(BF16)
                    dqacc[h] += _dot(dz, k_ref[kslice, :])
                    dkacc[kslice, :] += _dot_tn(dz, qs_x[h])

            one(j_hi, ks[0])

            @pl.when(lo_ok)
            def _():
                one(j_lo, ks[1])

            return carry

        lax.fori_loop(0, swept[0], apply, 0)
        dq_ref[...] = (jnp.where(is_a, dqacc[0], dqacc[1]) * SB_SCALE).astype(BF16)

        @pl.when(qi == nq - 1)
        def _():
            dk_ref[...] = dkacc[...].astype(BF16)
            dv_ref[...] = dvacc[...].astype(BF16)

        @pl.when(last_step)
        def _():
            for cp in _chip_copies(s_refs, l_refs, ssem, rsem):
                cp.wait()

    slab = lambda off: pl.BlockSpec((s, LANES), lambda p, qi: (0, off + p))
    blk = pl.BlockSpec((TQ, LANES), lambda p, qi: (qi, p))
    out_slab = pl.BlockSpec((s, LANES), lambda p, qi: (0, p))
    shp = jax.ShapeDtypeStruct((s, D_GRP), BF16)
    return pl.pallas_call(
        body, name="sb_bwd", grid=(4, nq),
        in_specs=[blk, slab(4), slab(8), blk] + ride_in,
        out_specs=[blk, out_slab, out_slab] + ride_out, out_shape=[shp, shp, shp] + ride_shape,
        scratch_shapes=[pltpu.VMEM((nk + 1, 2, TQ, TK), F32)] * 2
        + [pltpu.VMEM((2, TQ, LANES), F32), pltpu.VMEM((s, LANES), F32), pltpu.VMEM((s, LANES), F32)]
        + ride_sems,
        compiler_params=_params(("arbitrary", "arbitrary"), 44),
    )(qkv, qkv, qkv, d_o, *sums)


def _mla_fwd(qp, kp, vv, hb):
    s = qp.shape[0]
    c2 = MLA_SCALE * LOG2_E

    def body(q_ref, k_ref, v_ref, o_ref, lse_ref, vaug, mrun, mb, acc, zbuf):
        qi = pl.program_id(1)
        lane = lax.broadcasted_iota(jnp.int32, (1, LANES), 1)
        is_a = lane < HEAD_DIM

        @pl.when(qi == 0)
        def _():
            for h in range(hb):
                vp = v_ref[:, (h // 2) * LANES:(h // 2 + 1) * LANES]
                mine = is_a if h % 2 == 0 else jnp.logical_not(is_a)
                vaug[h] = jnp.where(mine, vp, jnp.ones_like(vp))

        r_i = lax.broadcasted_iota(jnp.int32, (TQ, TK), 0)
        c_i = lax.broadcasted_iota(jnp.int32, (TQ, TK), 1)
        visible = (c_i >> CHUNK_SHIFT) <= (r_i >> CHUNK_SHIFT)

        def key_rows(j):
            return pl.ds(pl.multiple_of(j * TK, TK), TK)

        def sweep(tiles):
            def loop(n, carry):
                tiles(((2 * n, False), (2 * n + 1, False)))
                return carry

            lax.fori_loop(0, qi // 2, loop, 0)

            @pl.when(qi % 2 == 1)
            def _():
                tiles(((qi - 1, False), (qi, True)))

            @pl.when(qi % 2 == 0)
            def _():
                tiles(((qi, True),))

        mrun[...] = jnp.full_like(mrun, NEG)

        def tiles_max(js):
            zs = [[_dot_nt(q_ref[:, h * LANES:(h + 1) * LANES], k_ref[key_rows(j), h * LANES:(h + 1) * LANES])
                   for h in range(hb)] for j, _ in js]
            for t, (j, diag) in enumerate(js):
                for h in range(hb):
                    z = jnp.where(visible, zs[t][h], NEG) if diag else zs[t][h]
                    zbuf[j, h] = z
                    mrun[h] = jnp.maximum(mrun[h], z)

        sweep(tiles_max)
        for h in range(hb):
            m = jnp.max(mrun[h], axis=1, keepdims=True) * c2
            mb[h] = jnp.broadcast_to(m, (TQ, TK))
        acc[...] = jnp.zeros_like(acc)

        def tiles_pv(js):
            ps = [[jnp.exp2((zbuf[j, h] * c2 - mb[h]).astype(BF16)) for h in range(hb)] for j, _ in js]
            for t, (j, _) in enumerate(js):
                for h in range(hb):
                    acc[h] += _dot(ps[t][h], vaug[h, key_rows(j), :])

        sweep(tiles_pv)
        for pr in range(hb // 2):
            a, b = 2 * pr, 2 * pr + 1
            psl = slice(pr * LANES, (pr + 1) * LANES)
            acc_a, acc_b = acc[a], acc[b]
            l_a = pltpu.roll(acc_a, HEAD_DIM, axis=1)
            l_b = pltpu.roll(acc_b, HEAD_DIM, axis=1)
            o_ref[:, psl] = jnp.where(is_a, acc_a * (1.0 / l_a), acc_b * (1.0 / l_b))
            lse_ref[:, psl] = jnp.where(is_a, mb[a, :, :LANES] * LN_2 + jnp.log(l_a),
                                        mb[b, :, :LANES] * LN_2 + jnp.log(l_b))

    blk = pl.BlockSpec((TQ, hb * HEAD_DIM), lambda g, qi: (qi, g))
    shp = jax.ShapeDtypeStruct((s, D_GRP), F32)
    return pl.pallas_call(
        body, name="mla_fwd", grid=(N_HEADS // hb, s // TQ),
        in_specs=[pl.BlockSpec((TQ, hb * LANES), lambda g, qi: (qi, g)),
                  pl.BlockSpec((s, hb * LANES), lambda g, qi: (0, g)),
                  pl.BlockSpec((s, hb * HEAD_DIM), lambda g, qi: (0, g))],
        out_specs=(blk, blk), out_shape=(shp, shp),
        scratch_shapes=[pltpu.VMEM((hb, s, LANES), BF16), pltpu.VMEM((hb, TQ, TK), F32),
                        pltpu.VMEM((hb, TQ, TK), F32), pltpu.VMEM((hb, TQ, LANES), F32),
                        pltpu.VMEM((s // TK, hb, TQ, TK), F32)],
        compiler_params=_params(("arbitrary", "arbitrary"), 44),
    )(qp, kp, vv)


def _mla_bwd(qp, kp, vv, d_o, o, lse, hb, pays):
    s = qp.shape[0]
    nq = s // TQ
    c2 = MLA_SCALE * LOG2_E
    n_op = len(pays)
    ride_in, ride_out, ride_shape, ride_sems = _pair_specs(pays)

    def body(q_ref, k_ref, v_ref, do_ref, o_ref, lse_ref, *refs):
        g_refs = refs[:n_op]
        dq_ref, dk_ref, dv_ref = refs[n_op:n_op + 3]
        l_refs = refs[n_op + 3:2 * n_op + 3]
        dqacc, lse_b, delta_b, q_t, do_t, ssem, rsem = refs[2 * n_op + 3:]
        qi = pl.program_id(1)

        @pl.when(jnp.logical_and(pl.program_id(0) == 0, qi == 0))
        def _():
            for cp in _pair_copies(g_refs, l_refs, ssem, rsem):
                cp.start()

        lane = lax.broadcasted_iota(jnp.int32, (1, LANES), 1)
        is_a = lane < HEAD_DIM

        @pl.when(qi == 0)
        def _():
            dk_ref[...] = jnp.zeros_like(dk_ref)
            dv_ref[...] = jnp.zeros_like(dv_ref)

        r_i = lax.broadcasted_iota(jnp.int32, (TQ, TK), 0)
        c_i = lax.broadcasted_iota(jnp.int32, (TQ, TK), 1)
        visible = (c_i >> CHUNK_SHIFT) <= (r_i >> CHUNK_SHIFT)
        do_x = []
        for h in range(hb):
            psl = slice((h // 2) * LANES, (h // 2 + 1) * LANES)
            mine = is_a if h % 2 == 0 else jnp.logical_not(is_a)
            d_o = do_ref[:, psl]
            delta = jnp.sum(jnp.where(mine, d_o * o_ref[:, psl], 0.0), axis=1, keepdims=True)
            lse_h = jnp.sum(jnp.where(lane == (h % 2) * HEAD_DIM, lse_ref[:, psl], 0.0), axis=1, keepdims=True)
            lse_b[h] = jnp.broadcast_to(lse_h * LOG2_E, (TQ, TK))
            delta_b[h] = jnp.broadcast_to(delta, (TQ, TK))
            do_h = jnp.where(mine, d_o, 0.0)
            do_x.append(do_h.astype(BF16))
            do_t[h] = do_h.T.astype(BF16)
            q_t[h] = q_ref[:, h * LANES:(h + 1) * LANES].astype(F32).T.astype(BF16)
        dqacc[...] = jnp.zeros_like(dqacc)

        head = lambda h: slice(h * LANES, (h + 1) * LANES)
        pair = lambda h: slice((h // 2) * LANES, (h // 2 + 1) * LANES)

        def tiles(js):
            th = [(j, diag, pl.ds(pl.multiple_of(j * TK, TK), TK), h) for j, diag in js for h in range(hb)]
            zs = [_dot_nt(q_ref[:, head(h)], k_ref[ks, head(h)]) for _, _, ks, h in th]
            dps = [_dot_nt(do_x[h], v_ref[ks, pair(h)]) for _, _, ks, h in th]
            for i, (j, diag, ks, h) in enumerate(th):
                e = zs[i] * c2 - lse_b[h]
                if diag:
                    e = jnp.where(visible, e, NEG)
                p = jnp.exp2(e)
                ds = (p * (dps[i] - delta_b[h]) * MLA_SCALE).astype(BF16)
                dqacc[h] += _dot(ds, k_ref[ks, head(h)])
                dk_ref[head(h), ks] += _dot(q_t[h], ds)
                dv_ref[pair(h), ks] += _dot(do_t[h], p.astype(BF16))

        def loop(n, c):
            tiles(((2 * n, False), (2 * n + 1, False)))
            return c

        lax.fori_loop(0, qi // 2, loop, 0)

        @pl.when(qi % 2 == 1)
        def _():
            tiles(((qi - 1, False), (qi, True)))

        @pl.when(qi % 2 == 0)
        def _():
            tiles(((qi, True),))

        for h in range(hb):
            dq_ref[:, h * LANES:(h + 1) * LANES] = dqacc[h]

        @pl.when(jnp.logical_and(pl.program_id(0) == pl.num_programs(0) - 1, qi == nq - 1))
        def _():
            for cp in _pair_copies(g_refs, l_refs, ssem, rsem):
                cp.wait()

    blk = pl.BlockSpec((TQ, hb * HEAD_DIM), lambda g, qi: (qi, g))
    return pl.pallas_call(
        body, name="mla_bwd", grid=(N_HEADS // hb, nq),
        in_specs=[pl.BlockSpec((TQ, hb * LANES), lambda g, qi: (qi, g)),
                  pl.BlockSpec((s, hb * LANES), lambda g, qi: (0, g)),
                  pl.BlockSpec((s, hb * HEAD_DIM), lambda g, qi: (0, g)), blk, blk, blk] + ride_in,
        out_specs=[pl.BlockSpec((TQ, hb * LANES), lambda g, qi: (qi, g)),
                   pl.BlockSpec((hb * LANES, s), lambda g, qi: (g, 0)),
                   pl.BlockSpec((hb * HEAD_DIM, s), lambda g, qi: (g, 0))] + ride_out,
        out_shape=[jax.ShapeDtypeStruct((s, 1024), F32), jax.ShapeDtypeStruct((1024, s), F32),
                   jax.ShapeDtypeStruct((D_GRP, s), F32)] + ride_shape,
        scratch_shapes=[pltpu.VMEM((hb, TQ, LANES), F32), pltpu.VMEM((hb, TQ, TK), F32),
                        pltpu.VMEM((hb, TQ, TK), F32), pltpu.VMEM((hb, LANES, TQ), BF16),
                        pltpu.VMEM((hb, LANES, TQ), BF16)] + ride_sems,
        compiler_params=_params(("arbitrary", "arbitrary"), 52),
    )(qp, kp, vv, d_o, o, lse, *pays)


def _mid(x, p, target, sb_o, mla_o, rest, g_sb, g_mla, w_out, g_post, w_ple, g_ple, w_pg, b_pg, bd):
    s = x.shape[0]

    def body(x_ref, p_ref, t_ref, sbo_ref, mlo_ref, sbg_ref, mlg_ref, gsb_ref, gml_ref, wout_ref,
             gpost_ref, wple_ref, gple_ref, wpg_ref, bpg_ref, bd_ref,
             dx1_ref, dsbo_ref, dmlo_ref, dsbg_ref, dmlg_ref, x1b_ref, dglb_ref, ycb_ref, dyb_ref,
             pb_ref, dub_ref, small_ref):
        i = pl.program_id(0)
        bd_m = bd_ref[...]

        def seg_mean(v):
            return _dot(v.astype(BF16), bd_m) * (1.0 / HEAD_DIM)

        groups = []
        for o_ref, gate_ref, gain_ref in ((sbo_ref, sbg_ref, gsb_ref), (mlo_ref, mlg_ref, gml_ref)):
            o = o_ref[...]
            r = lax.rsqrt(seg_mean(o * o) + EPS)
            n = o * r
            hn = n * gain_ref[...]
            gate = gate_ref[...]
            sg = _sigmoid(gate)
            si = gate * sg
            groups.append((r, n, hn, gate, sg, si, gain_ref[...]))
        ya = (groups[0][2] * groups[0][5]).astype(BF16)
        yb = (groups[1][2] * groups[1][5]).astype(BF16)
        ycb_ref[:, :D_GRP] = ya
        ycb_ref[:, D_GRP:] = yb
        y = _dot(ya, wout_ref[:D_GRP, :]) + _dot(yb, wout_ref[D_GRP:, :])
        ry = lax.rsqrt(jnp.mean(y * y, axis=-1, keepdims=True) + EPS)
        ny = y * ry
        x1 = x_ref[...] + ny * gpost_ref[...]
        x1b = x1.astype(BF16)
        x1b_ref[...] = x1b
        pb = p_ref[...].astype(BF16)
        pb_ref[...] = pb
        u = _dot(pb, wple_ref[...])
        ru = lax.rsqrt(jnp.mean(u * u, axis=-1, keepdims=True) + EPS)
        nu = u * ru
        ple = nu * gple_ref[...]
        gate = _sigmoid(_dot(x1b, wpg_ref[...]) + bpg_ref[...])
        x2 = x1 + ple * gate
        diff = x2 - t_ref[...]
        dx2 = diff * (1.0 / D_MODEL)

        d_ple = dx2 * gate
        d_glin = (dx2 * ple) * (gate * (1.0 - gate))
        dglb = d_glin.astype(BF16)
        dglb_ref[...] = dglb
        dx1 = dx2 + _dot_nt(dglb, wpg_ref[...])
        dx1_ref[...] = dx1
        d_nu = d_ple * gple_ref[...]
        d_u = ru * (d_nu - nu * jnp.mean(d_nu * nu, axis=-1, keepdims=True))
        dub_ref[...] = d_u.astype(BF16)
        d_ny = dx1 * gpost_ref[...]
        d_y = ry * (d_ny - ny * jnp.mean(d_ny * ny, axis=-1, keepdims=True))
        dyb = d_y.astype(BF16)
        dyb_ref[...] = dyb
        d_yc = (_dot_nt(dyb, wout_ref[:D_GRP, :]), _dot_nt(dyb, wout_ref[D_GRP:, :]))

        d_gain = []
        for gx, (do_ref, dg_ref) in enumerate(((dsbo_ref, dsbg_ref), (dmlo_ref, dmlg_ref))):
            r, n, hn, gate_g, sg, si, gain = groups[gx]
            dyg = d_yc[gx]
            d_hn = dyg * si
            dg_ref[...] = (dyg * hn * (sg * (1.0 + gate_g * (1.0 - sg)))).astype(BF16)
            d_gain.append(jnp.sum(d_hn * n, axis=0, keepdims=True))
            d_n = d_hn * gain
            do_ref[...] = r * (d_n - n * seg_mean(d_n * n))

        @pl.when(i == 0)
        def _():
            small_ref[...] = jnp.zeros_like(small_ref)

        small_ref[3:4, :D_GRP] += d_gain[0]
        small_ref[3:4, D_GRP:] += d_gain[1]
        small_ref[4:5, :] += jnp.sum(dx1 * ny, axis=0, keepdims=True)
        small_ref[5:6, :] += jnp.sum(d_ple * nu, axis=0, keepdims=True)
        small_ref[6:7, :] += jnp.sum(d_glin, axis=0, keepdims=True)
        small_ref[7:8, :] += jnp.sum(diff * diff, axis=0, keepdims=True) * (0.5 / D_MODEL)

    def row(width, idx=0):
        return pl.BlockSpec((TM, width), lambda i: (i, idx))

    def full(a):
        return pl.BlockSpec(a.shape, lambda i: (0, 0))

    f32 = lambda w: jax.ShapeDtypeStruct((s, w), F32)
    b16 = lambda w: jax.ShapeDtypeStruct((s, w), BF16)
    return pl.pallas_call(
        body, name="mid", grid=(s // TM,),
        in_specs=[row(D_MODEL), row(PLE_DIM), row(D_MODEL), row(D_GRP), row(D_GRP),
                  row(D_GRP, 0), row(D_GRP, 1), full(g_sb), full(g_mla), full(w_out), full(g_post),
                  full(w_ple), full(g_ple), full(w_pg), full(b_pg), full(bd)],
        out_specs=(row(D_MODEL), row(D_GRP), row(D_GRP), row(D_GRP), row(D_GRP), row(D_MODEL),
                   row(D_MODEL), row(D_MODEL), row(D_MODEL), row(PLE_DIM), row(D_MODEL),
                   pl.BlockSpec((8, D_MODEL), lambda i: (0, 0))),
        out_shape=(f32(D_MODEL), f32(D_GRP), f32(D_GRP), b16(D_GRP), b16(D_GRP), b16(D_MODEL),
                   b16(D_MODEL), b16(D_MODEL), b16(D_MODEL), b16(PLE_DIM), b16(D_MODEL),
                   jax.ShapeDtypeStruct((8, D_MODEL), F32)),
        compiler_params=_params(("arbitrary",), 46),
    )(x, p, target, sb_o, mla_o, rest, rest, g_sb, g_mla, w_out, g_post, w_ple, g_ple, w_pg, b_pg, bd)


def _mla_prep_bwd(dqp, dkp, dvv, rest, gq, gkv, wuq, wuk, wuv, cos_t, sin_t):
    s = rest.shape[0]

    def body(dqp_ref, dkp_ref, dvv_ref, cq_ref, ckv_ref, gq_ref, gkv_ref, wuq_ref, wuk_ref, wuv_ref,
             c_ref, s_ref, dcq_ref, dckv_ref, dkr_ref, dqb_ref, dkb_ref, dvb_ref, small_ref):
        i = pl.program_id(0)
        lane = lax.broadcasted_iota(jnp.int32, (1, LANES), 1)
        in_rope = (lane >= HEAD_DIM) & (lane < HEAD_DIM + ROPE_DIM)
        cos_v, sin_v = c_ref[...], s_ref[...]
        dkr_roped = jnp.zeros((TM, LANES), F32)
        for h in range(N_HEADS):
            sl = slice(h * LANES, (h + 1) * LANES)
            dy = dqp_ref[:, sl]
            dqb_ref[:, sl] = (dy * cos_v + _rope_swap(dy * sin_v, lane)).astype(BF16)
            dkh = dkp_ref[sl, :].T
            dkb_ref[:, sl] = dkh.astype(BF16)
            dkr_roped = dkr_roped + jnp.where(in_rope, dkh, 0.0)
        dkr_ref[...] = (dkr_roped * cos_v + _rope_swap(dkr_roped * sin_v, lane)).astype(BF16)
        dvb = dvv_ref[...].T.astype(BF16)
        dvb_ref[...] = dvb

        cq = cq_ref[...]
        rq = lax.rsqrt(jnp.mean(cq * cq, axis=-1, keepdims=True) + EPS)
        nq_ = cq * rq
        d_cqn = _dot_nt(dqb_ref[...], wuq_ref[...])
        d_n = d_cqn * gq_ref[...]
        dcq_ref[...] = (rq * (d_n - nq_ * jnp.mean(d_n * nq_, axis=-1, keepdims=True))).astype(BF16)

        ckv = ckv_ref[...]
        rkv = lax.rsqrt(jnp.mean(ckv * ckv, axis=-1, keepdims=True) + EPS)
        nkv = ckv * rkv
        d_ckvn = _dot_nt(dkb_ref[...], wuk_ref[...]) + _dot_nt(dvb, wuv_ref[...])
        d_n2 = d_ckvn * gkv_ref[...]
        dckv_ref[...] = (rkv * (d_n2 - nkv * jnp.mean(d_n2 * nkv, axis=-1, keepdims=True))).astype(BF16)

        @pl.when(i == 0)
        def _():
            small_ref[...] = jnp.zeros_like(small_ref)

        small_ref[0:1, :] += jnp.sum(d_cqn * nq_, axis=0, keepdims=True)
        small_ref[1:2, :KV_LORA] += jnp.sum(d_ckvn * nkv, axis=0, keepdims=True)

    def row(width, idx=0):
        return pl.BlockSpec((TM, width), lambda i: (i, idx))

    def full(a):
        return pl.BlockSpec(a.shape, lambda i: (0, 0))

    b16 = lambda w: jax.ShapeDtypeStruct((s, w), BF16)
    return pl.pallas_call(
        body, name="mla_prep_bwd", grid=(s // TM,),
        in_specs=[row(1024), pl.BlockSpec((1024, TM), lambda i: (0, i)), pl.BlockSpec((D_GRP, TM), lambda i: (0, i)),
                  row(Q_LORA, 4), row(KV_LORA, 10), full(gq), full(gkv),
                  full(wuq), full(wuk), full(wuv), row(LANES), row(LANES)],
        out_specs=(row(Q_LORA), row(KV_LORA), row(LANES), row(1024), row(1024), row(D_GRP),
                   pl.BlockSpec((8, Q_LORA), lambda i: (0, 0))),
        out_shape=(b16(Q_LORA), b16(KV_LORA), b16(LANES), b16(1024), b16(1024), b16(D_GRP),
                   jax.ShapeDtypeStruct((8, Q_LORA), F32)),
        compiler_params=_params(("arbitrary",), 16),
    )(dqp, dkp, dvv, rest, rest, gq, gkv, wuq, wuk, wuv, cos_t, sin_t)


def _in_bwd(x, g, dx1, pieces, w, sums):
    s = x.shape[0]
    steps = s // TM_IO
    widths = [a.shape[1] for a in pieces]
    offs = [sum(widths[:k]) for k in range(len(widths))]
    n_pc, n_op = len(pieces), len(sums)
    ride_in, ride_out, ride_shape, ride_sems = _chip_specs(sums)

    def body(x_ref, g_ref, dx1_ref, *refs):
        piece_refs = refs[:n_pc]
        w_ref = refs[n_pc]
        s_refs = refs[n_pc + 1:n_pc + 1 + n_op]
        dx_ref, small_ref = refs[n_pc + 1 + n_op:n_pc + 3 + n_op]
        l_refs = refs[n_pc + 3 + n_op:n_pc + 3 + 2 * n_op]
        ssem, rsem = refs[n_pc + 3 + 2 * n_op:]
        i = pl.program_id(0)

        @pl.when(i == 0)
        def _():
            for cp in _chip_copies(s_refs, l_refs, ssem, rsem):
                cp.start()

        dh = jnp.zeros((TM_IO, D_MODEL), F32)
        for pr, off, wd in zip(piece_refs, offs, widths):
            dh = dh + _dot_nt(pr[...], w_ref[:, off:off + wd])
        xv = x_ref[...]
        r = lax.rsqrt(jnp.mean(xv * xv, axis=-1, keepdims=True) + EPS)
        n = xv * r
        d_n = dh * g_ref[...]
        dx_ref[...] = dx1_ref[...] + r * (d_n - n * jnp.mean(d_n * n, axis=-1, keepdims=True))

        @pl.when(i == 0)
        def _():
            small_ref[...] = jnp.zeros_like(small_ref)

        small_ref[0:1, :] += jnp.sum(dh * n, axis=0, keepdims=True)

        @pl.when(i == steps - 1)
        def _():
            for cp in _chip_copies(s_refs, l_refs, ssem, rsem):
                cp.wait()

    def row(width):
        return pl.BlockSpec((TM_IO, width), lambda i: (i, 0))

    return pl.pallas_call(
        body, name="in_bwd", grid=(steps,),
        in_specs=[row(D_MODEL), pl.BlockSpec((1, D_MODEL), lambda i: (0, 0)), row(D_MODEL)]
        + [row(wd) for wd in widths] + [pl.BlockSpec(w.shape, lambda i: (0, 0))] + ride_in,
        out_specs=[row(D_MODEL), pl.BlockSpec((8, D_MODEL), lambda i: (0, 0))] + ride_out,
        out_shape=[jax.ShapeDtypeStruct((s, D_MODEL), F32), jax.ShapeDtypeStruct((8, D_MODEL), F32)]
        + ride_shape,
        scratch_shapes=ride_sems,
        compiler_params=_params(("arbitrary",), 40),
    )(x, g, dx1, *pieces, w, *sums)


def _tn_matmul(a, b, name, blocked=False):
    s, k = a.shape
    n = b.shape[1]
    ts = min(s, TS_DW)
    tn = n if blocked else min(n, 512)
    steps = s // ts

    def body(a_ref, b_ref, o_ref):
        t = pl.program_id(1)

        @pl.when(t == 0)
        def _():
            o_ref[...] = jnp.zeros_like(o_ref)

        prod = _dot_tn(a_ref[...], b_ref[...])
        if blocked:
            for j in range(n // LANES):
                o_ref[j] += prod[:, j * LANES:(j + 1) * LANES]
        else:
            o_ref[...] += prod

    if blocked:
        out_spec = pl.BlockSpec((n // LANES, k, LANES), lambda j, t: (0, 0, 0))
        out_shape = jax.ShapeDtypeStruct((n // LANES, k, LANES), F32)
    else:
        out_spec = pl.BlockSpec((k, tn), lambda j, t: (0, j))
        out_shape = jax.ShapeDtypeStruct((k, n), F32)
    return pl.pallas_call(
        body, name=name, grid=(n // tn, steps),
        in_specs=[pl.BlockSpec((ts, k), lambda j, t: (t, 0)), pl.BlockSpec((ts, tn), lambda j, t: (t, j))],
        out_specs=out_spec, out_shape=out_shape,
        compiler_params=_params(("parallel", "arbitrary"), 20),
    )(a, b)


def _tn_matmul_multi(a, bs, name):
    s, k = a.shape
    widths = [b.shape[1] for b in bs]
    ts = min(s, TS_DW)

    def body(a_ref, *refs):
        b_refs, o_ref = refs[:-1], refs[-1]
        t = pl.program_id(0)

        @pl.when(t == 0)
        def _():
            o_ref[...] = jnp.zeros_like(o_ref)

        av = a_ref[...]
        off = 0
        for b_ref, wd in zip(b_refs, widths):
            o_ref[:, off:off + wd] += _dot_tn(av, b_ref[...])
            off += wd

    return pl.pallas_call(
        body, name=name, grid=(s // ts,),
        in_specs=[pl.BlockSpec((ts, k), lambda t: (t, 0))] + [pl.BlockSpec((ts, wd), lambda t: (t, 0)) for wd in widths],
        out_specs=pl.BlockSpec((k, sum(widths)), lambda t: (0, 0)),
        out_shape=jax.ShapeDtypeStruct((k, sum(widths)), F32),
        compiler_params=_params(("arbitrary",), 30),
    )(a, *bs)


IN_SHARD = 372
_IN_KERNEL_ORDER = ((0, 2048), (2464, 2976), (2048, 2432))
_IN_ROPE = (2432, 2464)
_IN_GRAD_SRC = ((0, 512, 0, 0), (512, 1024, 0, 512), (1024, 1536, 1, 0), (1536, 2048, 1, 512),
                (2048, 2304, 2, 512), (2304, 2432, 2, 768), (2432, 2464, 2, 960), (2464, 2976, 2, 0))


def _shard_cols(gath_in, lo, hi):
    out = []
    while lo < hi:
        j, a = divmod(lo, IN_SHARD)
        b = min(IN_SHARD, a + hi - lo)
        out.append(gath_in[j][:, a:b])
        lo += b - a
    return out


def _kernel_w_in(g_in):
    zc = lambda n: jnp.zeros((D_MODEL, n), BF16)
    parts = [pc for lo, hi in _IN_KERNEL_ORDER for pc in _shard_cols(g_in, lo, hi)]
    parts += [zc(64)] + _shard_cols(g_in, *_IN_ROPE) + [zc(32)]
    return jnp.concatenate(parts, axis=1)


def _kernel_weights(gath):
    g_uq, g_ukv, g_out, g_ple, g_pg = gath
    w_uq_p = jnp.pad(g_uq, ((0, 0), (0, 0), (0, 32))).transpose(1, 0, 2).reshape(Q_LORA, 1024)
    k_only = jnp.where(jnp.arange(LANES) < HEAD_DIM, g_ukv, jnp.zeros_like(g_ukv))
    w_uk_p = k_only.transpose(1, 0, 2).reshape(KV_LORA, 1024)
    w_uv = g_ukv[:, :, HEAD_DIM:].transpose(1, 0, 2).reshape(KV_LORA, D_GRP)
    w_ple = g_ple.transpose(1, 0, 2).reshape(PLE_DIM, D_MODEL)
    return (w_uq_p, w_uk_p, w_uv, g_out.reshape(D_MODEL, D_MODEL), w_ple, g_pg.reshape(D_MODEL, D_MODEL))


def _payload_in(d_cols):
    blocks = []
    for j in range(N_DEV):
        lo, hi = j * IN_SHARD, (j + 1) * IN_SHARD
        parts = []
        for o_lo, o_hi, idx, off in _IN_GRAD_SRC:
            a, b = max(lo, o_lo), min(hi, o_hi)
            if a < b:
                parts.append(d_cols[idx][:, off + a - o_lo:off + b - o_lo])
        blocks.append(jnp.concatenate(parts, axis=1))
    return jnp.stack(blocks)


def _payload_ukv(duk_blk, d_uv):
    dv_blk = d_uv.reshape(KV_LORA, N_HEADS, HEAD_DIM).transpose(1, 0, 2)
    return jnp.concatenate([duk_blk[:, :, :HEAD_DIM], dv_blk], axis=2)


def kernel(x, p, positions, norm_pre_g, w_in, q_norm_g, w_uq, kv_norm_g, w_ukv, sb_out_norm_g, mla_out_norm_g, w_out, norm_post_g, w_ple, ple_norm_g, w_ple_gate, b_ple_gate, loss_target, m_norm_pre_g, m_w_in, m_q_norm_g, m_w_uq, m_kv_norm_g, m_w_ukv, m_sb_out_norm_g, m_mla_out_norm_g, m_w_out, m_norm_post_g, m_w_ple, m_ple_norm_g, m_w_ple_gate, m_b_ple_gate, v_norm_pre_g, v_w_in, v_q_norm_g, v_w_uq, v_kv_norm_g, v_w_ukv, v_sb_out_norm_g, v_mla_out_norm_g, v_w_out, v_norm_post_g, v_w_ple, v_ple_norm_g, v_w_ple_gate, v_b_ple_gate):
    mats = (w_in, w_uq, w_ukv, w_out, w_ple, w_ple_gate)
    m_mats = (m_w_in, m_w_uq, m_w_ukv, m_w_out, m_w_ple, m_w_ple_gate)
    v_mats = (v_w_in, v_w_uq, v_w_ukv, v_w_out, v_w_ple, v_w_ple_gate)
    vecs = (norm_pre_g, q_norm_g, kv_norm_g, sb_out_norm_g, mla_out_norm_g, norm_post_g, ple_norm_g, b_ple_gate)
    m_vecs = (m_norm_pre_g, m_q_norm_g, m_kv_norm_g, m_sb_out_norm_g, m_mla_out_norm_g, m_norm_post_g,
              m_ple_norm_g, m_b_ple_gate)
    v_vecs = (v_norm_pre_g, v_q_norm_g, v_kv_norm_g, v_sb_out_norm_g, v_mla_out_norm_g, v_norm_post_g,
              v_ple_norm_g, v_b_ple_gate)

    shards = [a[0].astype(BF16) for a in mats]
    w_in_p = _kernel_w_in(_all_gather(shards[:1])[0])
    grad_x, reduced, vec_slab = _step(x[0], p[0, 0], positions[0], loss_target[0], *vecs, w_in_p, shards[1:])
    upd = [_adamw_matrix(own, l2, w, m, v, "adamw_%d" % o)
           for o, ((own, l2), w, m, v) in enumerate(zip(reduced, mats, m_mats, v_mats))]
    sm = _adamw_vectors(_slab_exchange(vec_slab), vecs, m_vecs, v_vecs)

    outs = []
    for kind in range(4):
        mat = [upd[o][kind] for o in range(len(mats))]
        vec = sm[1 + 8 * kind:9 + 8 * kind]
        outs += [vec[0], mat[0], vec[1], mat[1], vec[2], mat[2], vec[3], vec[4], mat[3], vec[5],
                 mat[4], vec[6], mat[5], vec[7]]
    return (sm[0][0, 0], grad_x[None], *outs)


def _step(xs, ps, pos, tgt, norm_pre_g, q_norm_g, kv_norm_g, sb_out_norm_g, mla_out_norm_g,
          norm_post_g, ple_norm_g, b_ple_gate, w_in_p, shards):
    s = xs.shape[0]
    place = jnp.stack([lax.axis_index("c"), 2 * lax.axis_index("x") + lax.axis_index("y")]).astype(jnp.int32)

    half = ROPE_DIM // 2
    freq = ROPE_THETA ** (-jnp.arange(half, dtype=F32) / half)
    ang = pos.astype(F32)[:, None] * freq
    cos, sin = jnp.cos(ang), jnp.sin(ang)
    cos_t = jnp.concatenate([jnp.ones((s, 64), F32), cos, cos, jnp.zeros((s, 32), F32)], axis=1)
    sin_t = jnp.concatenate([jnp.zeros((s, 64), F32), -sin, sin, jnp.zeros((s, 32), F32)], axis=1)
    seg = jnp.arange(D_GRP) // HEAD_DIM
    bd = (seg[:, None] == seg[None, :]).astype(BF16)

    qkv, rest, h_b, *gath = _in_proj(xs, norm_pre_g, w_in_p, shards)
    w_uq_p, w_uk_p, w_uv, f_out, f_ple, f_pg = _kernel_weights(gath)
    sb_o = _sb_fwd(qkv, 8)
    qp, kp, vv, cqn_b, ckvn_b = _mla_prep(rest, q_norm_g, kv_norm_g, w_uq_p, w_uk_p, w_uv, cos_t, sin_t)
    mla_o, lse = _mla_fwd(qp, kp, vv, 4)

    (dx1, d_sbo, d_mlo, d_sbg, d_mlg, x1_b, dgl_b, yc_b, dy_b, p_b, du_b, small_mid) = _mid(
        xs, ps, tgt, sb_o, mla_o, rest, sb_out_norm_g, mla_out_norm_g, f_out, norm_post_g,
        f_ple, ple_norm_g, f_pg, b_ple_gate, bd)
    pay_a = [_tn_matmul(yc_b, dy_b, "dw_out").reshape(N_DEV, 128, D_MODEL),
             _tn_matmul(p_b, du_b, "dw_ple", blocked=True),
             _tn_matmul(x1_b, dgl_b, "dw_pg").reshape(N_DEV, 128, D_MODEL)]
    dqp, dkp, dvv, *sib_a = _mla_bwd(qp, kp, vv, d_mlo, mla_o, lse, 4, pay_a)
    pair_a = _pair_sums(pay_a, sib_a, place, "grad_pair_sums_a")
    dq_sb, dk_sb, dv_sb, *landed_a = _sb_bwd(qkv, d_sbo, [sm for sm, _ in pair_a])
    dcq, dckv, dkr, dq_b, dk_b, dv_b, small_prep = _mla_prep_bwd(
        dqp, dkp, dvv, rest, q_norm_g, kv_norm_g, w_uq_p, w_uk_p, w_uv, cos_t, sin_t)
    pieces = [dq_sb, dk_sb, dv_sb, d_sbg, d_mlg, dcq, dckv, dkr]
    d_cols = [_tn_matmul_multi(h_b, pieces[0:2], "dw_in_0"), _tn_matmul_multi(h_b, pieces[2:4], "dw_in_1"),
              _tn_matmul_multi(h_b, pieces[4:8], "dw_in_2")]
    pay_b = [_payload_in(d_cols), _tn_matmul(cqn_b, dq_b, "dw_uq", blocked=True),
             _payload_ukv(_tn_matmul(ckvn_b, dk_b, "dw_uk", blocked=True), _tn_matmul(ckvn_b, dv_b, "dw_uv"))]
    pair_b = _pair_sums(pay_b, _pair_exchange(pay_b, "grad_pair_exchange"), place, "grad_pair_sums_b")
    grad_x, small_in, *landed_b = _in_bwd(xs, norm_pre_g, dx1, pieces, w_in_p, [sm for sm, _ in pair_b])
    reduced = [(own, l2) for (_, own), l2 in zip(pair_b + pair_a, landed_b + landed_a)]
    slab = jnp.concatenate([small_in[0:1], jnp.pad(small_prep[0:2], ((0, 0), (0, D_MODEL - Q_LORA))),
                            small_mid[3:8]], axis=0)
    return grad_x, reduced, slab
```

```python
import jax
import jax.numpy as jnp
from jax import lax
from jax.experimental import pallas as pl
from jax.experimental.pallas import tpu as pltpu

F32 = jnp.float32
BF16 = jnp.bfloat16
MESH = pl.DeviceIdType.MESH

N_DEV = 8
D_MODEL = 1024
N_HEADS = 8
HEAD_DIM = 64
D_GRP = N_HEADS * HEAD_DIM
Q_LORA = 256
KV_LORA = 128
ROPE_DIM = 32
PLE_DIM = 256
CHUNK_SHIFT = 6
ROPE_THETA = 10000.0
EPS = 1e-6
SB_SCALE = HEAD_DIM ** -0.5
MLA_SCALE = (HEAD_DIM + ROPE_DIM) ** -0.5
NEG = -1e30
LOG2_E = 1.4426950408889634
LN_2 = 0.6931471805599453
SB_CUTOFF = 110.0

ADAM_LR = 0.001
ADAM_B1 = 0.9
ADAM_B2 = 0.999
ADAM_EPS = 1e-08
ADAM_WD = 0.01
ADAM_STEP = 10

LANES = 128
TQ = 256
TK = 256
TM = 256
TM_IO = 512
TS_DW = 2048

D_IN_P = 3072

_NT = (((1,), (1,)), ((), ()))
_TN = (((0,), (0,)), ((), ()))


def _params(sem, vmem_mb):
    return pltpu.CompilerParams(dimension_semantics=sem, vmem_limit_bytes=vmem_mb << 20)


def _hbm(*arrays):
    return [pltpu.with_memory_space_constraint(a, pltpu.HBM) for a in arrays]


def _dot(a, b):
    return jnp.dot(a, b, preferred_element_type=F32)


def _dot_nt(a, b):
    return lax.dot_general(a, b, _NT, preferred_element_type=F32)


def _dot_tn(a, b):
    return lax.dot_general(a, b, _TN, preferred_element_type=F32)


def _hl_dot(a, b):
    hi = a.astype(BF16)
    lo = (a - hi.astype(F32)).astype(BF16)
    return _dot(hi, b) + _dot(lo, b)


def _sigmoid(x):
    return 1.0 / (1.0 + jnp.exp(-x))


def _rope_swap(x, lane):
    left = pltpu.roll(x, LANES - 16, axis=1)
    right = pltpu.roll(x, 16, axis=1)
    lo = (lane >= 64) & (lane < 80)
    hi = (lane >= 80) & (lane < 96)
    return jnp.where(lo, left, jnp.where(hi, right, 0.0))


def _two_level_gather(x_refs, out_refs, send_sems, recv_sems, local_sems):
    x, y, c = lax.axis_index("x"), lax.axis_index("y"), lax.axis_index("c")
    me, sibling = (x, y, c), (x, y, 1 - c)
    chips = [(1 - x, y), (x, 1 - y), (1 - x, 1 - y)]
    ops = range(len(x_refs))

    def slot(o, px, py, pc):
        return out_refs[o].at[4 * px + 2 * py + pc]

    def copy(o, k, block, to, src=None):
        return pltpu.make_async_remote_copy(
            src_ref=slot(o, *block) if src is None else src, dst_ref=slot(o, *block),
            send_sem=send_sems.at[o, k], recv_sem=recv_sems.at[o, k],
            device_id=to, device_id_type=MESH)

    def mine():
        return [pltpu.make_async_copy(x_refs[o], slot(o, *me), local_sems.at[o]) for o in ops]

    def first():
        return ([copy(o, 0, me, sibling, src=x_refs[o]) for o in ops]
                + [copy(o, 1 + j, me, (*chip, c), src=x_refs[o]) for j, chip in enumerate(chips) for o in ops])

    def start():
        for cp in mine() + first():
            cp.start()

    def finish():
        passed = []
        for j, chip in enumerate(chips):
            for o in ops:
                copy(o, 1 + j, (*chip, c), me).wait_recv()
                passed.append(copy(o, 4 + j, (*chip, c), sibling))
                passed[-1].start()
        for o in ops:
            copy(o, 0, sibling, me).wait_recv()
        for j, chip in enumerate(chips):
            for o in ops:
                copy(o, 4 + j, (*chip, 1 - c), me).wait_recv()
        for cp in first() + passed:
            cp.wait_send()
        for cp in mine():
            cp.wait()

    return start, finish


def _gather_sems(n_op):
    return [pltpu.SemaphoreType.DMA((n_op, 7)), pltpu.SemaphoreType.DMA((n_op, 7)),
            pltpu.SemaphoreType.DMA((n_op,))]


def _all_gather(shards):
    n_op = len(shards)

    def body(*refs):
        start, finish = _two_level_gather(refs[:n_op], refs[n_op:2 * n_op], *refs[2 * n_op:])
        start()
        finish()

    any_spec = pl.BlockSpec(memory_space=pl.ANY)
    return pl.pallas_call(
        body, name="weight_all_gather",
        out_shape=[jax.ShapeDtypeStruct((N_DEV,) + a.shape, a.dtype) for a in shards],
        in_specs=[any_spec] * n_op, out_specs=[any_spec] * n_op, scratch_shapes=_gather_sems(n_op),
        compiler_params=pltpu.CompilerParams(vmem_limit_bytes=4 << 20),
    )(*shards)


def _pair_copies(g_refs, l_refs, ssem, rsem):
    x, y, c = lax.axis_index("x"), lax.axis_index("y"), lax.axis_index("c")
    copies = []
    for o in range(len(g_refs)):
        for chip in range(4):
            copies.append(pltpu.make_async_remote_copy(
                src_ref=g_refs[o].at[2 * chip + (1 - c)], dst_ref=l_refs[o].at[chip],
                send_sem=ssem.at[o, chip], recv_sem=rsem.at[o, chip],
                device_id=(x, y, 1 - c), device_id_type=MESH))
    return copies


def _pair_specs(pays):
    n_op = len(pays)
    any_spec = pl.BlockSpec(memory_space=pl.ANY)
    return ([any_spec] * n_op, [any_spec] * n_op,
            [jax.ShapeDtypeStruct((4,) + a.shape[1:], F32) for a in pays],
            [pltpu.SemaphoreType.DMA((n_op, 4)), pltpu.SemaphoreType.DMA((n_op, 4))])


def _pair_exchange(pays, name):
    n_op = len(pays)
    in_specs, out_specs, out_shape, sems = _pair_specs(pays)

    def body(*refs):
        copies = _pair_copies(refs[:n_op], refs[n_op:2 * n_op], *refs[2 * n_op:])
        for cp in copies:
            cp.start()
        for cp in copies:
            cp.wait()

    return pl.pallas_call(body, name=name, out_shape=out_shape, in_specs=in_specs, out_specs=out_specs,
                          scratch_shapes=sems,
                          compiler_params=pltpu.CompilerParams(vmem_limit_bytes=4 << 20))(*pays)


def _slab_exchange(small):
    sr, n = small.shape

    def body(s_ref, sland_ref, ssem, rsem, lsem):
        x, y, c = lax.axis_index("x"), lax.axis_index("y"), lax.axis_index("c")
        me = 4 * x + 2 * y + c
        copies = []
        for k in range(1, N_DEV):
            peer = (1 - x if (k >> 2) & 1 else x, 1 - y if (k >> 1) & 1 else y, 1 - c if k & 1 else c)
            copies.append(pltpu.make_async_remote_copy(
                src_ref=s_ref, dst_ref=sland_ref.at[me], send_sem=ssem.at[k], recv_sem=rsem.at[k],
                device_id=peer, device_id_type=MESH))
        own = pltpu.make_async_copy(s_ref, sland_ref.at[me], lsem)
        own.start()
        for cp in copies:
            cp.start()
        for cp in copies:
            cp.wait()
        own.wait()

    any_spec = pl.BlockSpec(memory_space=pl.ANY)
    return pl.pallas_call(
        body, name="grad_slab_exchange", out_shape=jax.ShapeDtypeStruct((N_DEV, sr, n), F32),
        in_specs=[any_spec], out_specs=any_spec,
        scratch_shapes=[pltpu.SemaphoreType.DMA((N_DEV,)), pltpu.SemaphoreType.DMA((N_DEV,)),
                        pltpu.SemaphoreType.DMA],
        compiler_params=pltpu.CompilerParams(vmem_limit_bytes=4 << 20),
    )(small)


def _pair_sums(pays, landed, place, name):
    n = len(pays)
    dims = [p.shape[1:] for p in pays]

    def body(place_ref, *refs):
        g_refs, l_refs, s_refs, own_refs = refs[:n], refs[n:2 * n], refs[2 * n:3 * n], refs[3 * n:]
        i = pl.program_id(0)
        for o in range(n):
            tot = g_refs[o][...] + l_refs[o][...]
            s_refs[o][...] = tot.astype(BF16)

            @pl.when(i == place_ref[1])
            def _(o=o, tot=tot):
                own_refs[o][...] = tot

    grid_spec = pltpu.PrefetchScalarGridSpec(
        num_scalar_prefetch=1, grid=(4,),
        in_specs=[pl.BlockSpec((None, r, c), lambda i, pr: (2 * i + pr[0], 0, 0)) for r, c in dims]
        + [pl.BlockSpec((None, r, c), lambda i, pr: (i, 0, 0)) for r, c in dims],
        out_specs=[pl.BlockSpec((None, r, c), lambda i, pr: (i, 0, 0)) for r, c in dims]
        + [pl.BlockSpec((r, c), lambda i, pr: (0, 0)) for r, c in dims])
    out = pl.pallas_call(
        body, name=name, grid_spec=grid_spec,
        out_shape=[jax.ShapeDtypeStruct((4, r, c), BF16) for r, c in dims]
        + [jax.ShapeDtypeStruct((r, c), F32) for r, c in dims],
        compiler_params=_params(("arbitrary",), 16),
    )(place, *pays, *landed)
    return list(zip(out[:n], out[n:]))


def _chip_copies(s_refs, l_refs, ssem, rsem):
    x, y, c = lax.axis_index("x"), lax.axis_index("y"), lax.axis_index("c")
    copies = []
    for rel in range(1, 4):
        px = 1 - x if rel & 2 else x
        py = 1 - y if rel & 1 else y
        for o in range(len(s_refs)):
            copies.append(pltpu.make_async_remote_copy(
                src_ref=s_refs[o].at[2 * px + py], dst_ref=l_refs[o].at[rel - 1],
                send_sem=ssem.at[o, rel - 1], recv_sem=rsem.at[o, rel - 1],
                device_id=(px, py, c), device_id_type=MESH))
    return copies


def _chip_specs(sums):
    n_op = len(sums)
    any_spec = pl.BlockSpec(memory_space=pl.ANY)
    return ([any_spec] * n_op, [any_spec] * n_op,
            [jax.ShapeDtypeStruct((3,) + a.shape[1:], BF16) for a in sums],
            [pltpu.SemaphoreType.DMA((n_op, 3)), pltpu.SemaphoreType.DMA((n_op, 3))])


def _adamw_math(g, w, m, v):
    mn = ADAM_B1 * m + (1.0 - ADAM_B1) * g
    vn = ADAM_B2 * v + (1.0 - ADAM_B2) * (g * g)
    m_hat = mn / (1.0 - ADAM_B1 ** ADAM_STEP)
    v_hat = vn / (1.0 - ADAM_B2 ** ADAM_STEP)
    return -ADAM_LR * (m_hat / (jnp.sqrt(v_hat) + ADAM_EPS) + ADAM_WD * w), mn, vn


def _adamw_matrix(own, landed, w, m, v, name):
    _, r, c = w.shape
    cp = own.shape[1]
    br = min(r, 256)

    def body(own_ref, l_ref, w_ref, m_ref, v_ref, g_out, d_out, m_out, v_out):
        g = own_ref[...]
        for k in range(3):
            g = g + l_ref[k].astype(F32)
        g = g[:, :c]
        g_out[...] = g
        d_out[...], m_out[...], v_out[...] = _adamw_math(g, w_ref[...], m_ref[...], v_ref[...])

    row = pl.BlockSpec((None, br, c), lambda i: (0, i, 0))
    shp = jax.ShapeDtypeStruct((1, r, c), F32)
    return pl.pallas_call(
        body, name=name, grid=(r // br,),
        in_specs=[pl.BlockSpec((br, cp), lambda i: (i, 0)), pl.BlockSpec((3, br, cp), lambda i: (0, i, 0)),
                  row, row, row],
        out_specs=(row, row, row, row), out_shape=(shp, shp, shp, shp),
        compiler_params=_params(("parallel",), 12),
    )(own, landed, w, m, v)


_VEC_PLACE = ((0, 0), (1, 0), (2, 0), (3, 0), (3, D_GRP), (4, 0), (5, 0), (6, 0))


def _adamw_vectors(sland, ws, ms, vs):
    nv = len(ws)

    def body(l_ref, *refs):
        w_refs, m_refs, v_refs = refs[:nv], refs[nv:2 * nv], refs[2 * nv:3 * nv]
        loss_ref = refs[3 * nv]
        outs = refs[3 * nv + 1:]
        g_all = l_ref[0]
        for j in range(1, N_DEV):
            g_all = g_all + l_ref[j]
        loss_ref[...] = jnp.sum(g_all[7:8, :], axis=1, keepdims=True)
        for k, (row, lane0) in enumerate(_VEC_PLACE):
            n = w_refs[k].shape[1]
            g = g_all[row:row + 1, lane0:lane0 + n]
            d, mn, vn = _adamw_math(g, w_refs[k][...], m_refs[k][...], v_refs[k][...])
            outs[k][...] = g
            outs[nv + k][...] = d
            outs[2 * nv + k][...] = mn
            outs[3 * nv + k][...] = vn

    def whole(shape):
        return pl.BlockSpec(shape, lambda i: (0,) * len(shape))

    shapes = [jax.ShapeDtypeStruct(w.shape, F32) for w in ws]
    return pl.pallas_call(
        body, name="adamw_vectors", grid=(1,),
        in_specs=[whole(sland.shape)] + [whole(w.shape) for w in ws] * 3,
        out_specs=[whole((1, 1))] + [whole(w.shape) for w in ws] * 4,
        out_shape=[jax.ShapeDtypeStruct((1, 1), F32)] + shapes * 4,
        compiler_params=_params(("arbitrary",), 4),
    )(sland, *ws, *ms, *vs)


def _in_proj(x, g, w, shards):
    s = x.shape[0]
    n_op = len(shards)
    steps = s // TM_IO

    def body(x_ref, g_ref, w_ref, *refs):
        shard_refs = refs[:n_op]
        qkv_ref, rest_ref, h_ref = refs[n_op:n_op + 3]
        gath_refs = refs[n_op + 3:2 * n_op + 3]
        start, finish = _two_level_gather(shard_refs, gath_refs, *refs[2 * n_op + 3:])
        i = pl.program_id(0)

        @pl.when(i == 0)
        def _():
            start()

        xv = x_ref[...]
        r = lax.rsqrt(jnp.mean(xv * xv, axis=-1, keepdims=True) + EPS)
        h = ((xv * r) * g_ref[...]).astype(BF16)
        h_ref[...] = h
        qkv_ref[...] = _dot(h, w_ref[:, :1536]).astype(BF16)
        rest_ref[...] = _dot(h, w_ref[:, 1536:])

        @pl.when(i == steps - 1)
        def _():
            finish()

    any_spec = pl.BlockSpec(memory_space=pl.ANY)
    return pl.pallas_call(
        body, name="in_proj", grid=(steps,),
        in_specs=[pl.BlockSpec((TM_IO, D_MODEL), lambda i: (i, 0)),
                  pl.BlockSpec((1, D_MODEL), lambda i: (0, 0)),
                  pl.BlockSpec((D_MODEL, D_IN_P), lambda i: (0, 0))] + [any_spec] * n_op,
        out_specs=[pl.BlockSpec((TM_IO, 1536), lambda i: (i, 0)),
                   pl.BlockSpec((TM_IO, 1536), lambda i: (i, 0)),
                   pl.BlockSpec((TM_IO, D_MODEL), lambda i: (i, 0))] + [any_spec] * n_op,
        out_shape=[pltpu.HBM((s, 1536), BF16), pltpu.HBM((s, 1536), F32),
                   pltpu.HBM((s, D_MODEL), BF16)]
        + [jax.ShapeDtypeStruct((N_DEV,) + a.shape, a.dtype) for a in shards],
        scratch_shapes=_gather_sems(n_op),
        compiler_params=_params(("arbitrary",), 32),
    )(x, g, w, *shards)


def _mla_prep(rest, gq, gkv, wuq, wuk, wuv, cos_t, sin_t):
    s = rest.shape[0]

    def body(cq_ref, ckv_ref, kr_ref, gq_ref, gkv_ref, wuq_ref, wuk_ref, wuv_ref, c_ref, s_ref,
             qp_ref, kp_ref, vv_ref, cqn_ref, ckvn_ref):
        lane = lax.broadcasted_iota(jnp.int32, (1, LANES), 1)
        cos_v, sin_v = c_ref[...], s_ref[...]
        cq = cq_ref[...]
        rq = lax.rsqrt(jnp.mean(cq * cq, axis=-1, keepdims=True) + EPS)
        cqn = ((cq * rq) * gq_ref[...]).astype(BF16)
        cqn_ref[...] = cqn
        q = _dot(cqn, wuq_ref[...])
        ckv = ckv_ref[...]
        rkv = lax.rsqrt(jnp.mean(ckv * ckv, axis=-1, keepdims=True) + EPS)
        ckvn = ((ckv * rkv) * gkv_ref[...]).astype(BF16)
        ckvn_ref[...] = ckvn
        kn = _dot(ckvn, wuk_ref[...])
        vv_ref[...] = _dot(ckvn, wuv_ref[...]).astype(BF16)
        kr = kr_ref[...]
        kr_roped = kr * cos_v + _rope_swap(kr, lane) * sin_v
        for h in range(N_HEADS):
            sl = slice(h * LANES, (h + 1) * LANES)
            qh = q[:, sl]
            qp_ref[:, sl] = (qh * cos_v + _rope_swap(qh, lane) * sin_v).astype(BF16)
            kp_ref[:, sl] = (kn[:, sl] + kr_roped).astype(BF16)

    def row(width, idx):
        return pl.BlockSpec((TM, width), lambda i: (i, idx))

    def full(a):
        return pl.BlockSpec(a.shape, lambda i: (0, 0))

    return pl.pallas_call(
        body, name="mla_prep", grid=(s // TM,),
        in_specs=[row(Q_LORA, 4), row(KV_LORA, 10), row(LANES, 11), full(gq), full(gkv),
                  full(wuq), full(wuk), full(wuv), row(LANES, 0), row(LANES, 0)],
        out_specs=(row(1024, 0), row(1024, 0), row(D_GRP, 0), row(Q_LORA, 0), row(KV_LORA, 0)),
        out_shape=(pltpu.HBM((s, 1024), BF16), pltpu.HBM((s, 1024), BF16),
                   pltpu.HBM((s, D_GRP), BF16), pltpu.HBM((s, Q_LORA), BF16),
                   pltpu.HBM((s, KV_LORA), BF16)),
        compiler_params=_params(("parallel",), 12),
    )(*_hbm(rest, rest, rest), gq, gkv, wuq, wuk, wuv, cos_t, sin_t)


def _sb_live(n, qi, carries):
    top = carries[0]
    for c in carries[1:]:
        top = jnp.maximum(top, c)
    return jnp.logical_and(n < qi, jnp.max(top) > -SB_CUTOFF)


def _sb_fwd(qkv, hb):
    s = qkv.shape[0]

    def body(q_ref, k_ref, v_ref, o_ref, acc):
        qi = pl.program_id(1)
        lane = lax.broadcasted_iota(jnp.int32, (1, LANES), 1)
        is_a = lane < HEAD_DIM
        pair = lambda h: slice((h // 2) * LANES, (h // 2 + 1) * LANES)
        q_h = []
        for h in range(hb):
            qs = q_ref[:, pair(h)] * SB_SCALE
            mine = is_a if h % 2 == 0 else jnp.logical_not(is_a)
            q_h.append(jnp.where(mine, qs, jnp.zeros_like(qs)))
        r_i = lax.broadcasted_iota(jnp.int32, (TQ, TK), 0)
        c_i = lax.broadcasted_iota(jnp.int32, (TQ, TK), 1)
        past = c_i < r_i
        upper = (r_i > c_i).astype(BF16)
        acc[...] = jnp.zeros_like(acc)

        def tile(j, carries, diag):
            ks = pl.ds(pl.multiple_of(j * TK, TK), TK)
            zs = [_dot_nt(q_h[h], k_ref[ks, pair(h)]) for h in range(hb)]
            if diag:
                zs = [jnp.where(past, z, NEG) for z in zs]
            lfs = [-(jnp.maximum(z, 0.0) + jnp.log(1.0 + jnp.exp(-jnp.abs(z)))) for z in zs]
            sufs = [_hl_dot(lfs[h], upper) for h in range(hb)]
            out = []
            for h in range(hb):
                w = jnp.exp(zs[h] + lfs[h] + (sufs[h] + carries[h]))
                acc[h] += _dot(w.astype(BF16), v_ref[ks, pair(h)])
                out.append(carries[h] + jnp.sum(lfs[h], axis=1, keepdims=True))
            return tuple(out)

        zero = jnp.zeros((TQ, 1), F32)
        carries = tile(qi, (zero,) * hb, True)

        def step(st):
            return (st[0] + 1,) + tile(qi - 1 - st[0], st[1:], False)

        lax.while_loop(lambda st: _sb_live(st[0], qi, st[1:]), step, (0,) + carries)
        for pr in range(hb // 2):
            o_ref[:, pr * LANES:(pr + 1) * LANES] = jnp.where(is_a, acc[2 * pr], acc[2 * pr + 1])

    width = hb * HEAD_DIM
    nb = D_GRP // width
    slab = lambda part: pl.BlockSpec((s, width), lambda g, qi: (0, part * nb + g))
    blk = pl.BlockSpec((TQ, width), lambda g, qi: (qi, g))
    return pl.pallas_call(
        body, name="sb_fwd", grid=(nb, s // TQ),
        in_specs=[blk, slab(1), slab(2)], out_specs=blk,
        out_shape=pltpu.HBM((s, D_GRP), F32),
        scratch_shapes=[pltpu.VMEM((hb, TQ, LANES), F32)],
        compiler_params=_params(("arbitrary", "arbitrary"), 28),
    )(*_hbm(qkv, qkv, qkv))


def _sb_bwd(qkv, d_o, sums):
    s = qkv.shape[0]
    nq = s // TQ
    nk = s // TK
    n_op = len(sums)
    ride_in, ride_out, ride_shape, ride_sems = _chip_specs(sums)

    def body(q_ref, k_ref, v_ref, do_ref, *refs):
        s_refs = refs[:n_op]
        dq_ref, dk_ref, dv_ref = refs[n_op:n_op + 3]
        l_refs = refs[n_op + 3:2 * n_op + 3]
        x1s, bts, dqacc, dkacc, dvacc, ssem, rsem = refs[2 * n_op + 3:]
        qi = pl.program_id(1)
        first_step = jnp.logical_and(pl.program_id(0) == 0, qi == 0)
        last_step = jnp.logical_and(pl.program_id(0) == pl.num_programs(0) - 1, qi == nq - 1)

        @pl.when(first_step)
        def _():
            for cp in _chip_copies(s_refs, l_refs, ssem, rsem):
                cp.start()

        lane = lax.broadcasted_iota(jnp.int32, (1, LANES), 1)
        is_a = lane < HEAD_DIM

        @pl.when(qi == 0)
        def _():
            dkacc[...] = jnp.zeros_like(dkacc)
            dvacc[...] = jnp.zeros_like(dvacc)

        qs = q_ref[...] * SB_SCALE
        zq = jnp.zeros_like(qs)
        qs_x = (jnp.where(is_a, qs, zq), jnp.where(is_a, zq, qs))
        dob = do_ref[...].astype(BF16)
        do_x = (jnp.where(is_a, dob, zq), jnp.where(is_a, zq, dob))
        r_i = lax.broadcasted_iota(jnp.int32, (TQ, TK), 0)
        c_i = lax.broadcasted_iota(jnp.int32, (TQ, TK), 1)
        past = c_i < r_i
        upper = (r_i > c_i).astype(BF16)
        upper_incl = (r_i >= c_i).astype(BF16)
        dqacc[...] = jnp.zeros_like(dqacc)
        both = ((0, 0), (0, 1), (1, 0), (1, 1))

        def tiles(n):
            j_hi = qi - 2 * n
            lo_ok = j_hi >= 1
            j_lo = jnp.maximum(j_hi - 1, 0)
            ks = (pl.ds(pl.multiple_of(j_hi * TK, TK), TK), pl.ds(pl.multiple_of(j_lo * TK, TK), TK))
            return j_hi, lo_ok, j_lo, ks

        def sweep(n, carries):
            j_hi, lo_ok, j_lo, ks = tiles(n)
            slot = (j_hi, jnp.where(lo_ok, j_lo, nk))
            valid = (jnp.logical_or(past, j_hi < qi), lo_ok)
            z = {th: jnp.where(valid[th[0]], _dot_nt(qs_x[th[1]], k_ref[ks[th[0]], :]), NEG) for th in both}
            log_b, lf_sum, suf = {}, {}, {}
            for th in both:
                lf = -(jnp.maximum(z[th], 0.0) + jnp.log(1.0 + jnp.exp(-jnp.abs(z[th]))))
                log_b[th] = z[th] + lf
                lf_sum[th] = jnp.sum(lf, axis=1, keepdims=True)
                suf[th] = _hl_dot(lf, upper)
            c, g_in = {}, {}
            for h in range(2):
                c[0, h], g_in[0, h] = carries[2 * h], carries[2 * h + 1]
                c[1, h] = c[0, h] + lf_sum[0, h]
            d_a = {th: _dot_nt(do_x[th[1]], v_ref[ks[th[0]], :]) for th in both}
            a_b, g, g_sum, sg = {}, {}, {}, {}
            for th in both:
                a = jnp.exp(log_b[th] + (suf[th] + c[th]))
                a_b[th] = a.astype(BF16)
                g[th] = a * d_a[th]
                g_sum[th] = jnp.sum(g[th], axis=1, keepdims=True)
                sg[th] = _hl_dot(g[th], upper_incl)
            for h in range(2):
                g_in[1, h] = g_in[0, h] + g_sum[0, h]
            for th in both:
                t, h = th
                beta = jnp.exp(log_b[th])
                x1s[slot[t], h] = g[th] * (1.0 - beta) + beta * (sg[th] + g_in[th])
                bts[slot[t], h] = beta
                dvacc[ks[t], :] += _dot_tn(a_b[th], do_x[h])
            out = []
            for h in range(2):
                out.append(c[1, h] + lf_sum[1, h])
                out.append(g_in[1, h] + g_sum[1, h])
            return tuple(out)

        zero = jnp.zeros((TQ, 1), F32)
        first = sweep(0, (zero, zero, zero, zero))

        def more(st):
            return jnp.logical_and(2 * st[0] <= qi, jnp.max(jnp.maximum(st[1], st[3])) > -SB_CUTOFF)

        swept = lax.while_loop(more, lambda st: (st[0] + 1,) + sweep(st[0], st[1:]), (1,) + first)
        g_tot = (swept[2], swept[4])

        def apply(n, carry):
            j_hi, lo_ok, j_lo, ks = tiles(n)

            def one(j, kslice):
                for h in range(2):
                    dz = (x1s[j, h] - bts[j, h] * g_tot[h]).astype(BF16)
                    dqacc[h] += _dot(dz, k_ref[kslice, :])
                    dkacc[kslice, :] += _dot_tn(dz, qs_x[h])

            one(j_hi, ks[0])

            @pl.when(lo_ok)
            def _():
                one(j_lo, ks[1])

            return carry

        lax.fori_loop(0, swept[0], apply, 0)
        dq_ref[...] = (jnp.where(is_a, dqacc[0], dqacc[1]) * SB_SCALE).astype(BF16)

        @pl.when(qi == nq - 1)
        def _():
            dk_ref[...] = dkacc[...].astype(BF16)
            dv_ref[...] = dvacc[...].astype(BF16)

        @pl.when(last_step)
        def _():
            for cp in _chip_copies(s_refs, l_refs, ssem, rsem):
                cp.wait()

    slab = lambda off: pl.BlockSpec((s, LANES), lambda p, qi: (0, off + p))
    blk = pl.BlockSpec((TQ, LANES), lambda p, qi: (qi, p))
    out_slab = pl.BlockSpec((s, LANES), lambda p, qi: (0, p))
    shp = pltpu.HBM((s, D_GRP), BF16)
    return pl.pallas_call(
        body, name="sb_bwd", grid=(4, nq),
        in_specs=[blk, slab(4), slab(8), blk] + ride_in,
        out_specs=[blk, out_slab, out_slab] + ride_out, out_shape=[shp, shp, shp] + ride_shape,
        scratch_shapes=[pltpu.VMEM((nk + 1, 2, TQ, TK), F32)] * 2
        + [pltpu.VMEM((2, TQ, LANES), F32), pltpu.VMEM((s, LANES), F32), pltpu.VMEM((s, LANES), F32)]
        + ride_sems,
        compiler_params=_params(("arbitrary", "arbitrary"), 44),
    )(*_hbm(qkv, qkv, qkv, d_o), *sums)


def _mla_fwd(qp, kp, vv, hb):
    s = qp.shape[0]
    c2 = MLA_SCALE * LOG2_E

    def body(q_ref, k_ref, v_ref, o_ref, lse_ref, vaug, mrun, mb, acc, zbuf):
        qi = pl.program_id(1)
        lane = lax.broadcasted_iota(jnp.int32, (1, LANES), 1)
        is_a = lane < HEAD_DIM

        @pl.when(qi == 0)
        def _():
            for h in range(hb):
                vp = v_ref[:, (h // 2) * LANES:(h // 2 + 1) * LANES]
                mine = is_a if h % 2 == 0 else jnp.logical_not(is_a)
                vaug[h] = jnp.where(mine, vp, jnp.ones_like(vp))

        r_i = lax.broadcasted_iota(jnp.int32, (TQ, TK), 0)
        c_i = lax.broadcasted_iota(jnp.int32, (TQ, TK), 1)
        visible = (c_i >> CHUNK_SHIFT) <= (r_i >> CHUNK_SHIFT)

        def key_rows(j):
            return pl.ds(pl.multiple_of(j * TK, TK), TK)

        def sweep(tiles):
            def loop(n, carry):
                tiles(((2 * n, False), (2 * n + 1, False)))
                return carry

            lax.fori_loop(0, qi // 2, loop, 0)

            @pl.when(qi % 2 == 1)
            def _():
                tiles(((qi - 1, False), (qi, True)))

            @pl.when(qi % 2 == 0)
            def _():
                tiles(((qi, True),))

        mrun[...] = jnp.full_like(mrun, NEG)

        def tiles_max(js):
            zs = [[_dot_nt(q_ref[:, h * LANES:(h + 1) * LANES], k_ref[key_rows(j), h * LANES:(h + 1) * LANES])
                   for h in range(hb)] for j, _ in js]
            for t, (j, diag) in enumerate(js):
                for h in range(hb):
                    z = jnp.where(visible, zs[t][h], NEG) if diag else zs[t][h]
                    zbuf[j, h] = z
                    mrun[h] = jnp.maximum(mrun[h], z)

        sweep(tiles_max)
        for h in range(hb):
            m = jnp.max(mrun[h], axis=1, keepdims=True) * c2
            mb[h] = jnp.broadcast_to(m, (TQ, TK))
        acc[...] = jnp.zeros_like(acc)

        def tiles_pv(js):
            ps = [[jnp.exp2((zbuf[j, h] * c2 - mb[h]).astype(BF16)) for h in range(hb)] for j, _ in js]
            for t, (j, _) in enumerate(js):
                for h in range(hb):
                    acc[h] += _dot(ps[t][h], vaug[h, key_rows(j), :])

        sweep(tiles_pv)
        for pr in range(hb // 2):
            a, b = 2 * pr, 2 * pr + 1
            psl = slice(pr * LANES, (pr + 1) * LANES)
            acc_a, acc_b = acc[a], acc[b]
            l_a = pltpu.roll(acc_a, HEAD_DIM, axis=1)
            l_b = pltpu.roll(acc_b, HEAD_DIM, axis=1)
            o_ref[:, psl] = jnp.where(is_a, acc_a * (1.0 / l_a), acc_b * (1.0 / l_b))
            lse_ref[:, psl] = jnp.where(is_a, mb[a, :, :LANES] * LN_2 + jnp.log(l_a),
                                        mb[b, :, :LANES] * LN_2 + jnp.log(l_b))

    blk = pl.BlockSpec((TQ, hb * HEAD_DIM), lambda g, qi: (qi, g))
    shp = pltpu.HBM((s, D_GRP), F32)
    return pl.pallas_call(
        body, name="mla_fwd", grid=(N_HEADS // hb, s // TQ),
        in_specs=[pl.BlockSpec((TQ, hb * LANES), lambda g, qi: (qi, g)),
                  pl.BlockSpec((s, hb * LANES), lambda g, qi: (0, g)),
                  pl.BlockSpec((s, hb * HEAD_DIM), lambda g, qi: (0, g))],
        out_specs=(blk, blk), out_shape=(shp, shp),
        scratch_shapes=[pltpu.VMEM((hb, s, LANES), BF16), pltpu.VMEM((hb, TQ, TK), F32),
                        pltpu.VMEM((hb, TQ, TK), F32), pltpu.VMEM((hb, TQ, LANES), F32),
                        pltpu.VMEM((s // TK, hb, TQ, TK), F32)],
        compiler_params=_params(("arbitrary", "arbitrary"), 44),
    )(*_hbm(qp, kp, vv))


def _mla_bwd(qp, kp, vv, d_o, o, lse, hb, pays):
    s = qp.shape[0]
    nq = s // TQ
    c2 = MLA_SCALE * LOG2_E
    n_op = len(pays)
    ride_in, ride_out, ride_shape, ride_sems = _pair_specs(pays)

    def body(q_ref, k_ref, v_ref, do_ref, o_ref, lse_ref, *refs):
        g_refs = refs[:n_op]
        dq_ref, dk_ref, dv_ref = refs[n_op:n_op + 3]
        l_refs = refs[n_op + 3:2 * n_op + 3]
        dqacc, lse_b, delta_b, q_t, do_t, ssem, rsem = refs[2 * n_op + 3:]
        qi = pl.program_id(1)

        @pl.when(jnp.logical_and(pl.program_id(0) == 0, qi == 0))
        def _():
            for cp in _pair_copies(g_refs, l_refs, ssem, rsem):
                cp.start()

        lane = lax.broadcasted_iota(jnp.int32, (1, LANES), 1)
        is_a = lane < HEAD_DIM

        @pl.when(qi == 0)
        def _():
            dk_ref[...] = jnp.zeros_like(dk_ref)
            dv_ref[...] = jnp.zeros_like(dv_ref)

        r_i = lax.broadcasted_iota(jnp.int32, (TQ, TK), 0)
        c_i = lax.broadcasted_iota(jnp.int32, (TQ, TK), 1)
        visible = (c_i >> CHUNK_SHIFT) <= (r_i >> CHUNK_SHIFT)
        do_x = []
        for h in range(hb):
            psl = slice((h // 2) * LANES, (h // 2 + 1) * LANES)
            mine = is_a if h % 2 == 0 else jnp.logical_not(is_a)
            d_o = do_ref[:, psl]
            delta = jnp.sum(jnp.where(mine, d_o * o_ref[:, psl], 0.0), axis=1, keepdims=True)
            lse_h = jnp.sum(jnp.where(lane == (h % 2) * HEAD_DIM, lse_ref[:, psl], 0.0), axis=1, keepdims=True)
            lse_b[h] = jnp.broadcast_to(lse_h * LOG2_E, (TQ, TK))
            delta_b[h] = jnp.broadcast_to(delta, (TQ, TK))
            do_h = jnp.where(mine, d_o, 0.0)
            do_x.append(do_h.astype(BF16))
            do_t[h] = do_h.T.astype(BF16)
            q_t[h] = q_ref[:, h * LANES:(h + 1) * LANES].astype(F32).T.astype(BF16)
        dqacc[...] = jnp.zeros_like(dqacc)

        head = lambda h: slice(h * LANES, (h + 1) * LANES)
        pair = lambda h: slice((h // 2) * LANES, (h // 2 + 1) * LANES)

        def tiles(js):
            th = [(j, diag, pl.ds(pl.multiple_of(j * TK, TK), TK), h) for j, diag in js for h in range(hb)]
            zs = [_dot_nt(q_ref[:, head(h)], k_ref[ks, head(h)]) for _, _, ks, h in th]
            dps = [_dot_nt(do_x[h], v_ref[ks, pair(h)]) for _, _, ks, h in th]
            for i, (j, diag, ks, h) in enumerate(th):
                e = zs[i] * c2 - lse_b[h]
                if diag:
                    e = jnp.where(visible, e, NEG)
                p = jnp.exp2(e)
                ds = (p * (dps[i] - delta_b[h]) * MLA_SCALE).astype(BF16)
                dqacc[h] += _dot(ds, k_ref[ks, head(h)])
                dk_ref[head(h), ks] += _dot(q_t[h], ds)
                dv_ref[pair(h), ks] += _dot(do_t[h], p.astype(BF16))

        def loop(n, c):
            tiles(((2 * n, False), (2 * n + 1, False)))
            return c

        lax.fori_loop(0, qi // 2, loop, 0)

        @pl.when(qi % 2 == 1)
        def _():
            tiles(((qi - 1, False), (qi, True)))

        @pl.when(qi % 2 == 0)
        def _():
            tiles(((qi, True),))

        for h in range(hb):
            dq_ref[:, h * LANES:(h + 1) * LANES] = dqacc[h]

        @pl.when(jnp.logical_and(pl.program_id(0) == pl.num_programs(0) - 1, qi == nq - 1))
        def _():
            for cp in _pair_copies(g_refs, l_refs, ssem, rsem):
                cp.wait()

    blk = pl.BlockSpec((TQ, hb * HEAD_DIM), lambda g, qi: (qi, g))
    return pl.pallas_call(
        body, name="mla_bwd", grid=(N_HEADS // hb, nq),
        in_specs=[pl.BlockSpec((TQ, hb * LANES), lambda g, qi: (qi, g)),
                  pl.BlockSpec((s, hb * LANES), lambda g, qi: (0, g)),
                  pl.BlockSpec((s, hb * HEAD_DIM), lambda g, qi: (0, g)), blk, blk, blk] + ride_in,
        out_specs=[pl.BlockSpec((TQ, hb * LANES), lambda g, qi: (qi, g)),
                   pl.BlockSpec((hb * LANES, s), lambda g, qi: (g, 0)),
                   pl.BlockSpec((hb * HEAD_DIM, s), lambda g, qi: (g, 0))] + ride_out,
        out_shape=[pltpu.HBM((s, 1024), F32), pltpu.HBM((1024, s), F32),
                   pltpu.HBM((D_GRP, s), F32)] + ride_shape,
        scratch_shapes=[pltpu.VMEM((hb, TQ, LANES), F32), pltpu.VMEM((hb, TQ, TK), F32),
                        pltpu.VMEM((hb, TQ, TK), F32), pltpu.VMEM((hb, LANES, TQ), BF16),
                        pltpu.VMEM((hb, LANES, TQ), BF16)] + ride_sems,
        compiler_params=_params(("arbitrary", "arbitrary"), 52),
    )(*_hbm(qp, kp, vv, d_o, o, lse), *pays)


def _mid(x, p, target, sb_o, mla_o, rest, g_sb, g_mla, w_out, g_post, w_ple, g_ple, w_pg, b_pg, bd):
    s = x.shape[0]

    def body(x_ref, p_ref, t_ref, sbo_ref, mlo_ref, sbg_ref, mlg_ref, gsb_ref, gml_ref, wout_ref,
             gpost_ref, wple_ref, gple_ref, wpg_ref, bpg_ref, bd_ref,
             dx1_ref, dsbo_ref, dmlo_ref, dsbg_ref, dmlg_ref, x1b_ref, dglb_ref, ycb_ref, dyb_ref,
             pb_ref, dub_ref, small_ref):
        i = pl.program_id(0)
        bd_m = bd_ref[...]

        def seg_mean(v):
            return _dot(v.astype(BF16), bd_m) * (1.0 / HEAD_DIM)

        groups = []
        for o_ref, gate_ref, gain_ref in ((sbo_ref, sbg_ref, gsb_ref), (mlo_ref, mlg_ref, gml_ref)):
            o = o_ref[...]
            r = lax.rsqrt(seg_mean(o * o) + EPS)
            n = o * r
            hn = n * gain_ref[...]
            gate = gate_ref[...]
            sg = _sigmoid(gate)
            si = gate * sg
            groups.append((r, n, hn, gate, sg, si, gain_ref[...]))
        ya = (groups[0][2] * groups[0][5]).astype(BF16)
        yb = (groups[1][2] * groups[1][5]).astype(BF16)
        ycb_ref[:, :D_GRP] = ya
        ycb_ref[:, D_GRP:] = yb
        y = _dot(ya, wout_ref[:D_GRP, :]) + _dot(yb, wout_ref[D_GRP:, :])
        ry = lax.rsqrt(jnp.mean(y * y, axis=-1, keepdims=True) + EPS)
        ny = y * ry
        x1 = x_ref[...] + ny * gpost_ref[...]
        x1b = x1.astype(BF16)
        x1b_ref[...] = x1b
        pb = p_ref[...].astype(BF16)
        pb_ref[...] = pb
        u = _dot(pb, wple_ref[...])
        ru = lax.rsqrt(jnp.mean(u * u, axis=-1, keepdims=True) + EPS)
        nu = u * ru
        ple = nu * gple_ref[...]
        gate = _sigmoid(_dot(x1b, wpg_ref[...]) + bpg_ref[...])
        x2 = x1 + ple * gate
        diff = x2 - t_ref[...]
        dx2 = diff * (1.0 / D_MODEL)

        d_ple = dx2 * gate
        d_glin = (dx2 * ple) * (gate * (1.0 - gate))
        dglb = d_glin.astype(BF16)
        dglb_ref[...] = dglb
        dx1 = dx2 + _dot_nt(dglb, wpg_ref[...])
        dx1_ref[...] = dx1
        d_nu = d_ple * gple_ref[...]
        d_u = ru * (d_nu - nu * jnp.mean(d_nu * nu, axis=-1, keepdims=True))
        dub_ref[...] = d_u.astype(BF16)
        d_ny = dx1 * gpost_ref[...]
        d_y = ry * (d_ny - ny * jnp.mean(d_ny * ny, axis=-1, keepdims=True))
        dyb = d_y.astype(BF16)
        dyb_ref[...] = dyb
        d_yc = (_dot_nt(dyb, wout_ref[:D_GRP, :]), _dot_nt(dyb, wout_ref[D_GRP:, :]))

        d_gain = []
        for gx, (do_ref, dg_ref) in enumerate(((dsbo_ref, dsbg_ref), (dmlo_ref, dmlg_ref))):
            r, n, hn, gate_g, sg, si, gain = groups[gx]
            dyg = d_yc[gx]
            d_hn = dyg * si
            dg_ref[...] = (dyg * hn * (sg * (1.0 + gate_g * (1.0 - sg)))).astype(BF16)
            d_gain.append(jnp.sum(d_hn * n, axis=0, keepdims=True))
            d_n = d_hn * gain
            do_ref[...] = r * (d_n - n * seg_mean(d_n * n))

        @pl.when(i == 0)
        def _():
            small_ref[...] = jnp.zeros_like(small_ref)

        small_ref[3:4, :D_GRP] += d_gain[0]
        small_ref[3:4, D_GRP:] += d_gain[1]
        small_ref[4:5, :] += jnp.sum(dx1 * ny, axis=0, keepdims=True)
        small_ref[5:6, :] += jnp.sum(d_ple * nu, axis=0, keepdims=True)
        small_ref[6:7, :] += jnp.sum(d_glin, axis=0, keepdims=True)
        small_ref[7:8, :] += jnp.sum(diff * diff, axis=0, keepdims=True) * (0.5 / D_MODEL)

    def row(width, idx=0):
        return pl.BlockSpec((TM, width), lambda i: (i, idx))

    def full(a):
        return pl.BlockSpec(a.shape, lambda i: (0, 0))

    f32 = lambda w: pltpu.HBM((s, w), F32)
    b16 = lambda w: pltpu.HBM((s, w), BF16)
    return pl.pallas_call(
        body, name="mid", grid=(s // TM,),
        in_specs=[row(D_MODEL), row(PLE_DIM), row(D_MODEL), row(D_GRP), row(D_GRP),
                  row(D_GRP, 0), row(D_GRP, 1), full(g_sb), full(g_mla), full(w_out), full(g_post),
                  full(w_ple), full(g_ple), full(w_pg), full(b_pg), full(bd)],
        out_specs=(row(D_MODEL), row(D_GRP), row(D_GRP), row(D_GRP), row(D_GRP), row(D_MODEL),
                   row(D_MODEL), row(D_MODEL), row(D_MODEL), row(PLE_DIM), row(D_MODEL),
                   pl.BlockSpec((8, D_MODEL), lambda i: (0, 0))),
        out_shape=(f32(D_MODEL), f32(D_GRP), f32(D_GRP), b16(D_GRP), b16(D_GRP), b16(D_MODEL),
                   b16(D_MODEL), b16(D_MODEL), b16(D_MODEL), b16(PLE_DIM), b16(D_MODEL),
                   jax.ShapeDtypeStruct((8, D_MODEL), F32)),
        compiler_params=_params(("arbitrary",), 46),
    )(*_hbm(x, p, target, sb_o, mla_o, rest, rest), g_sb, g_mla, w_out, g_post, w_ple, g_ple, w_pg, b_pg, bd)


def _mla_prep_bwd(dqp, dkp, dvv, rest, gq, gkv, wuq, wuk, wuv, cos_t, sin_t):
    s = rest.shape[0]

    def body(dqp_ref, dkp_ref, dvv_ref, cq_ref, ckv_ref, gq_ref, gkv_ref, wuq_ref, wuk_ref, wuv_ref,
             c_ref, s_ref, dcq_ref, dckv_ref, dkr_ref, dqb_ref, dkb_ref, dvb_ref, small_ref):
        i = pl.program_id(0)
        lane = lax.broadcasted_iota(jnp.int32, (1, LANES), 1)
        in_rope = (lane >= HEAD_DIM) & (lane < HEAD_DIM + ROPE_DIM)
        cos_v, sin_v = c_ref[...], s_ref[...]
        dkr_roped = jnp.zeros((TM, LANES), F32)
        for h in range(N_HEADS):
            sl = slice(h * LANES, (h + 1) * LANES)
            dy = dqp_ref[:, sl]
            dqb_ref[:, sl] = (dy * cos_v + _rope_swap(dy * sin_v, lane)).astype(BF16)
            dkh = dkp_ref[sl, :].T
            dkb_ref[:, sl] = dkh.astype(BF16)
            dkr_roped = dkr_roped + jnp.where(in_rope, dkh, 0.0)
        dkr_ref[...] = (dkr_roped * cos_v + _rope_swap(dkr_roped * sin_v, lane)).astype(BF16)
        dvb = dvv_ref[...].T.astype(BF16)
        dvb_ref[...] = dvb

        cq = cq_ref[...]
        rq = lax.rsqrt(jnp.mean(cq * cq, axis=-1, keepdims=True) + EPS)
        nq_ = cq * rq
        d_cqn = _dot_nt(dqb_ref[...], wuq_ref[...])
        d_n = d_cqn * gq_ref[...]
        dcq_ref[...] = (rq * (d_n - nq_ * jnp.mean(d_n * nq_, axis=-1, keepdims=True))).astype(BF16)

        ckv = ckv_ref[...]
        rkv = lax.rsqrt(jnp.mean(ckv * ckv, axis=-1, keepdims=True) + EPS)
        nkv = ckv * rkv
        d_ckvn = _dot_nt(dkb_ref[...], wuk_ref[...]) + _dot_nt(dvb, wuv_ref[...])
        d_n2 = d_ckvn * gkv_ref[...]
        dckv_ref[...] = (rkv * (d_n2 - nkv * jnp.mean(d_n2 * nkv, axis=-1, keepdims=True))).astype(BF16)

        @pl.when(i == 0)
        def _():
            small_ref[...] = jnp.zeros_like(small_ref)

        small_ref[0:1, :] += jnp.sum(d_cqn * nq_, axis=0, keepdims=True)
        small_ref[1:2, :KV_LORA] += jnp.sum(d_ckvn * nkv, axis=0, keepdims=True)

    def row(width, idx=0):
        return pl.BlockSpec((TM, width), lambda i: (i, idx))

    def full(a):
        return pl.BlockSpec(a.shape, lambda i: (0, 0))

    b16 = lambda w: pltpu.HBM((s, w), BF16)
    return pl.pallas_call(
        body, name="mla_prep_bwd", grid=(s // TM,),
        in_specs=[row(1024), pl.BlockSpec((1024, TM), lambda i: (0, i)), pl.BlockSpec((D_GRP, TM), lambda i: (0, i)),
                  row(Q_LORA, 4), row(KV_LORA, 10), full(gq), full(gkv),
                  full(wuq), full(wuk), full(wuv), row(LANES), row(LANES)],
        out_specs=(row(Q_LORA), row(KV_LORA), row(LANES), row(1024), row(1024), row(D_GRP),
                   pl.BlockSpec((8, Q_LORA), lambda i: (0, 0))),
        out_shape=(b16(Q_LORA), b16(KV_LORA), b16(LANES), b16(1024), b16(1024), b16(D_GRP),
                   jax.ShapeDtypeStruct((8, Q_LORA), F32)),
        compiler_params=_params(("arbitrary",), 16),
    )(*_hbm(dqp, dkp, dvv, rest, rest), gq, gkv, wuq, wuk, wuv, cos_t, sin_t)


def _in_bwd(x, g, dx1, pieces, w, sums):
    s = x.shape[0]
    steps = s // TM_IO
    widths = [a.shape[1] for a in pieces]
    offs = [sum(widths[:k]) for k in range(len(widths))]
    n_pc, n_op = len(pieces), len(sums)
    ride_in, ride_out, ride_shape, ride_sems = _chip_specs(sums)

    def body(x_ref, g_ref, dx1_ref, *refs):
        piece_refs = refs[:n_pc]
        w_ref = refs[n_pc]
        s_refs = refs[n_pc + 1:n_pc + 1 + n_op]
        dx_ref, small_ref = refs[n_pc + 1 + n_op:n_pc + 3 + n_op]
        l_refs = refs[n_pc + 3 + n_op:n_pc + 3 + 2 * n_op]
        ssem, rsem = refs[n_pc + 3 + 2 * n_op:]
        i = pl.program_id(0)

        @pl.when(i == 0)
        def _():
            for cp in _chip_copies(s_refs, l_refs, ssem, rsem):
                cp.start()

        dh = jnp.zeros((TM_IO, D_MODEL), F32)
        for pr, off, wd in zip(piece_refs, offs, widths):
            dh = dh + _dot_nt(pr[...], w_ref[:, off:off + wd])
        xv = x_ref[...]
        r = lax.rsqrt(jnp.mean(xv * xv, axis=-1, keepdims=True) + EPS)
        n = xv * r
        d_n = dh * g_ref[...]
        dx_ref[...] = dx1_ref[...] + r * (d_n - n * jnp.mean(d_n * n, axis=-1, keepdims=True))

        @pl.when(i == 0)
        def _():
            small_ref[...] = jnp.zeros_like(small_ref)

        small_ref[0:1, :] += jnp.sum(dh * n, axis=0, keepdims=True)

        @pl.when(i == steps - 1)
        def _():
            for cp in _chip_copies(s_refs, l_refs, ssem, rsem):
                cp.wait()

    def row(width):
        return pl.BlockSpec((TM_IO, width), lambda i: (i, 0))

    return pl.pallas_call(
        body, name="in_bwd", grid=(steps,),
        in_specs=[row(D_MODEL), pl.BlockSpec((1, D_MODEL), lambda i: (0, 0)), row(D_MODEL)]
        + [row(wd) for wd in widths] + [pl.BlockSpec(w.shape, lambda i: (0, 0))] + ride_in,
        out_specs=[row(D_MODEL), pl.BlockSpec((8, D_MODEL), lambda i: (0, 0))] + ride_out,
        out_shape=[pltpu.HBM((s, D_MODEL), F32), jax.ShapeDtypeStruct((8, D_MODEL), F32)]
        + ride_shape,
        scratch_shapes=ride_sems,
        compiler_params=_params(("arbitrary",), 40),
    )(*_hbm(x), g, *_hbm(dx1, *pieces), w, *sums)


def _tn_matmul(a, b, name, blocked=False):
    s, k = a.shape
    n = b.shape[1]
    ts = min(s, TS_DW)
    tn = n if blocked else min(n, 512)
    steps = s // ts

    def body(a_ref, b_ref, o_ref):
        t = pl.program_id(1)

        @pl.when(t == 0)
        def _():
            o_ref[...] = jnp.zeros_like(o_ref)

        prod = _dot_tn(a_ref[...], b_ref[...])
        if blocked:
            for j in range(n // LANES):
                o_ref[j] += prod[:, j * LANES:(j + 1) * LANES]
        else:
            o_ref[...] += prod

    if blocked:
        out_spec = pl.BlockSpec((n // LANES, k, LANES), lambda j, t: (0, 0, 0))
        out_shape = jax.ShapeDtypeStruct((n // LANES, k, LANES), F32)
    else:
        out_spec = pl.BlockSpec((k, tn), lambda j, t: (0, j))
        out_shape = jax.ShapeDtypeStruct((k, n), F32)
    return pl.pallas_call(
        body, name=name, grid=(n // tn, steps),
        in_specs=[pl.BlockSpec((ts, k), lambda j, t: (t, 0)), pl.BlockSpec((ts, tn), lambda j, t: (t, j))],
        out_specs=out_spec, out_shape=out_shape,
        compiler_params=_params(("parallel", "arbitrary"), 20),
    )(*_hbm(a, b))


def _tn_matmul_multi(a, bs, name):
    s, k = a.shape
    widths = [b.shape[1] for b in bs]
    ts = min(s, TS_DW)

    def body(a_ref, *refs):
        b_refs, o_ref = refs[:-1], refs[-1]
        t = pl.program_id(0)

        @pl.when(t == 0)
        def _():
            o_ref[...] = jnp.zeros_like(o_ref)

        av = a_ref[...]
        off = 0
        for b_ref, wd in zip(b_refs, widths):
            o_ref[:, off:off + wd] += _dot_tn(av, b_ref[...])
            off += wd

    return pl.pallas_call(
        body, name=name, grid=(s // ts,),
        in_specs=[pl.BlockSpec((ts, k), lambda t: (t, 0))] + [pl.BlockSpec((ts, wd), lambda t: (t, 0)) for wd in widths],
        out_specs=pl.BlockSpec((k, sum(widths)), lambda t: (0, 0)),
        out_shape=jax.ShapeDtypeStruct((k, sum(widths)), F32),
        compiler_params=_params(("arbitrary",), 30),
    )(*_hbm(a, *bs))


IN_SHARD = 372
_IN_KERNEL_ORDER = ((0, 2048), (2464, 2976), (2048, 2432))
_IN_ROPE = (2432, 2464)
_IN_GRAD_SRC = ((0, 512, 0, 0), (512, 1024, 0, 512), (1024, 1536, 1, 0), (1536, 2048, 1, 512),
                (2048, 2304, 2, 512), (2304, 2432, 2, 768), (2432, 2464, 2, 960), (2464, 2976, 2, 0))


def _shard_cols(gath_in, lo, hi):
    out = []
    while lo < hi:
        j, a = divmod(lo, IN_SHARD)
        b = min(IN_SHARD, a + hi - lo)
        out.append(gath_in[j][:, a:b])
        lo += b - a
    return out


def _kernel_w_in(g_in):
    zc = lambda n: jnp.zeros((D_MODEL, n), BF16)
    parts = [pc for lo, hi in _IN_KERNEL_ORDER for pc in _shard_cols(g_in, lo, hi)]
    parts += [zc(64)] + _shard_cols(g_in, *_IN_ROPE) + [zc(32)]
    return jnp.concatenate(parts, axis=1)


def _kernel_weights(gath):
    g_uq, g_ukv, g_out, g_ple, g_pg = gath
    w_uq_p = jnp.pad(g_uq, ((0, 0), (0, 0), (0, 32))).transpose(1, 0, 2).reshape(Q_LORA, 1024)
    k_only = jnp.where(jnp.arange(LANES) < HEAD_DIM, g_ukv, jnp.zeros_like(g_ukv))
    w_uk_p = k_only.transpose(1, 0, 2).reshape(KV_LORA, 1024)
    w_uv = g_ukv[:, :, HEAD_DIM:].transpose(1, 0, 2).reshape(KV_LORA, D_GRP)
    w_ple = g_ple.transpose(1, 0, 2).reshape(PLE_DIM, D_MODEL)
    return (w_uq_p, w_uk_p, w_uv, g_out.reshape(D_MODEL, D_MODEL), w_ple, g_pg.reshape(D_MODEL, D_MODEL))


def _payload_in(d_cols):
    blocks = []
    for j in range(N_DEV):
        lo, hi = j * IN_SHARD, (j + 1) * IN_SHARD
        parts = []
        for o_lo, o_hi, idx, off in _IN_GRAD_SRC:
            a, b = max(lo, o_lo), min(hi, o_hi)
            if a < b:
                parts.append(d_cols[idx][:, off + a - o_lo:off + b - o_lo])
        blocks.append(jnp.concatenate(parts, axis=1))
    return jnp.stack(blocks)


def _payload_ukv(duk_blk, d_uv):
    dv_blk = d_uv.reshape(KV_LORA, N_HEADS, HEAD_DIM).transpose(1, 0, 2)
    return jnp.concatenate([duk_blk[:, :, :HEAD_DIM], dv_blk], axis=2)


def kernel(x, p, positions, norm_pre_g, w_in, q_norm_g, w_uq, kv_norm_g, w_ukv, sb_out_norm_g, mla_out_norm_g, w_out, norm_post_g, w_ple, ple_norm_g, w_ple_gate, b_ple_gate, loss_target, m_norm_pre_g, m_w_in, m_q_norm_g, m_w_uq, m_kv_norm_g, m_w_ukv, m_sb_out_norm_g, m_mla_out_norm_g, m_w_out, m_norm_post_g, m_w_ple, m_ple_norm_g, m_w_ple_gate, m_b_ple_gate, v_norm_pre_g, v_w_in, v_q_norm_g, v_w_uq, v_kv_norm_g, v_w_ukv, v_sb_out_norm_g, v_mla_out_norm_g, v_w_out, v_norm_post_g, v_w_ple, v_ple_norm_g, v_w_ple_gate, v_b_ple_gate):
    mats = (w_in, w_uq, w_ukv, w_out, w_ple, w_ple_gate)
    m_mats = (m_w_in, m_w_uq, m_w_ukv, m_w_out, m_w_ple, m_w_ple_gate)
    v_mats = (v_w_in, v_w_uq, v_w_ukv, v_w_out, v_w_ple, v_w_ple_gate)
    vecs = (norm_pre_g, q_norm_g, kv_norm_g, sb_out_norm_g, mla_out_norm_g, norm_post_g, ple_norm_g, b_ple_gate)
    m_vecs = (m_norm_pre_g, m_q_norm_g, m_kv_norm_g, m_sb_out_norm_g, m_mla_out_norm_g, m_norm_post_g,
              m_ple_norm_g, m_b_ple_gate)
    v_vecs = (v_norm_pre_g, v_q_norm_g, v_kv_norm_g, v_sb_out_norm_g, v_mla_out_norm_g, v_norm_post_g,
              v_ple_norm_g, v_b_ple_gate)

    shards = [a[0].astype(BF16) for a in mats]
    w_in_p = _kernel_w_in(_all_gather(shards[:1])[0])
    grad_x, reduced, vec_slab = _step(x[0], p[0, 0], positions[0], loss_target[0], *vecs, w_in_p, shards[1:])
    upd = [_adamw_matrix(own, l2, w, m, v, "adamw_%d" % o)
           for o, ((own, l2), w, m, v) in enumerate(zip(reduced, mats, m_mats, v_mats))]
    sm = _adamw_vectors(_slab_exchange(vec_slab), vecs, m_vecs, v_vecs)

    outs = []
    for kind in range(4):
        mat = [upd[o][kind] for o in range(len(mats))]
        vec = sm[1 + 8 * kind:9 + 8 * kind]
        outs += [vec[0], mat[0], vec[1], mat[1], vec[2], mat[2], vec[3], vec[4], mat[3], vec[5],
                 mat[4], vec[6], mat[5], vec[7]]
    return (sm[0][0, 0], grad_x[None], *outs)


def _step(xs, ps, pos, tgt, norm_pre_g, q_norm_g, kv_norm_g, sb_out_norm_g, mla_out_norm_g,
          norm_post_g, ple_norm_g, b_ple_gate, w_in_p, shards):
    s = xs.shape[0]
    place = jnp.stack([lax.axis_index("c"), 2 * lax.axis_index("x") + lax.axis_index("y")]).astype(jnp.int32)

    half = ROPE_DIM // 2
    freq = ROPE_THETA ** (-jnp.arange(half, dtype=F32) / half)
    ang = pos.astype(F32)[:, None] * freq
    cos, sin = jnp.cos(ang), jnp.sin(ang)
    cos_t = jnp.concatenate([jnp.ones((s, 64), F32), cos, cos, jnp.zeros((s, 32), F32)], axis=1)
    sin_t = jnp.concatenate([jnp.zeros((s, 64), F32), -sin, sin, jnp.zeros((s, 32), F32)], axis=1)
    seg = jnp.arange(D_GRP) // HEAD_DIM
    bd = (seg[:, None] == seg[None, :]).astype(BF16)

    qkv, rest, h_b, *gath = _in_proj(xs, norm_pre_g, w_in_p, shards)
    w_uq_p, w_uk_p, w_uv, f_out, f_ple, f_pg = _kernel_weights(gath)
    sb_o = _sb_fwd(qkv, 8)
    qp, kp, vv, cqn_b, ckvn_b = _mla_prep(rest, q_norm_g, kv_norm_g, w_uq_p, w_uk_p, w_uv, cos_t, sin_t)
    mla_o, lse = _mla_fwd(qp, kp, vv, 4)

    (dx1, d_sbo, d_mlo, d_sbg, d_mlg, x1_b, dgl_b, yc_b, dy_b, p_b, du_b, small_mid) = _mid(
        xs, ps, tgt, sb_o, mla_o, rest, sb_out_norm_g, mla_out_norm_g, f_out, norm_post_g,
        f_ple, ple_norm_g, f_pg, b_ple_gate, bd)
    pay_a = [_tn_matmul(yc_b, dy_b, "dw_out").reshape(N_DEV, 128, D_MODEL),
             _tn_matmul(p_b, du_b, "dw_ple", blocked=True),
             _tn_matmul(x1_b, dgl_b, "dw_pg").reshape(N_DEV, 128, D_MODEL)]
    dqp, dkp, dvv, *sib_a = _mla_bwd(qp, kp, vv, d_mlo, mla_o, lse, 4, pay_a)
    pair_a = _pair_sums(pay_a, sib_a, place, "grad_pair_sums_a")
    dq_sb, dk_sb, dv_sb, *landed_a = _sb_bwd(qkv, d_sbo, [sm for sm, _ in pair_a])
    dcq, dckv, dkr, dq_b, dk_b, dv_b, small_prep = _mla_prep_bwd(
        dqp, dkp, dvv, rest, q_norm_g, kv_norm_g, w_uq_p, w_uk_p, w_uv, cos_t, sin_t)
    pieces = [dq_sb, dk_sb, dv_sb, d_sbg, d_mlg, dcq, dckv, dkr]
    d_cols = [_tn_matmul_multi(h_b, pieces[0:2], "dw_in_0"), _tn_matmul_multi(h_b, pieces[2:4], "dw_in_1"),
              _tn_matmul_multi(h_b, pieces[4:8], "dw_in_2")]
    pay_b = [_payload_in(d_cols), _tn_matmul(cqn_b, dq_b, "dw_uq", blocked=True),
             _payload_ukv(_tn_matmul(ckvn_b, dk_b, "dw_uk", blocked=True), _tn_matmul(ckvn_b, dv_b, "dw_uv"))]
    pair_b = _pair_sums(pay_b, _pair_exchange(pay_b, "grad_pair_exchange"), place, "grad_pair_sums_b")
    grad_x, small_in, *landed_b = _in_bwd(xs, norm_pre_g, dx1, pieces, w_in_p, [sm for sm, _ in pair_b])
    reduced = [(own, l2) for (_, own), l2 in zip(pair_b + pair_a, landed_b + landed_a)]
    slab = jnp.concatenate([small_in[0:1], jnp.pad(small_prep[0:2], ((0, 0), (0, D_MODEL - Q_LORA))),
                            small_mid[3:8]], axis=0)
    return grad_x, reduced, slab
```

```python
import jax
import jax.numpy as jnp
from jax import lax
from jax.experimental import pallas as pl
from jax.experimental.pallas import tpu as pltpu

F32 = jnp.float32
BF16 = jnp.bfloat16
MESH = pl.DeviceIdType.MESH

N_DEV = 8
D_MODEL = 1024
N_HEADS = 8
HEAD_DIM = 64
D_GRP = N_HEADS * HEAD_DIM
Q_LORA = 256
KV_LORA = 128
ROPE_DIM = 32
PLE_DIM = 256
CHUNK_SHIFT = 6
ROPE_THETA = 10000.0
EPS = 1e-6
SB_SCALE = HEAD_DIM ** -0.5
MLA_SCALE = (HEAD_DIM + ROPE_DIM) ** -0.5
NEG = -1e30
LOG2_E = 1.4426950408889634
LN_2 = 0.6931471805599453
SB_CUTOFF = 110.0

ADAM_LR = 0.001
ADAM_B1 = 0.9
ADAM_B2 = 0.999
ADAM_EPS = 1e-08
ADAM_WD = 0.01
ADAM_STEP = 10

LANES = 128
TQ = 256
TK = 256
TM = 256
TM_IO = 512
TS_DW = 2048

D_IN_P = 3072

_NT = (((1,), (1,)), ((), ()))
_TN = (((0,), (0,)), ((), ()))


def _params(sem, vmem_mb):
    return pltpu.CompilerParams(dimension_semantics=sem, vmem_limit_bytes=vmem_mb << 20)


def _hbm(*arrays):
    return [pltpu.with_memory_space_constraint(a, pltpu.HBM) for a in arrays]


def _dot(a, b):
    return jnp.dot(a, b, preferred_element_type=F32)


def _dot_nt(a, b):
    return lax.dot_general(a, b, _NT, preferred_element_type=F32)


def _dot_tn(a, b):
    return lax.dot_general(a, b, _TN, preferred_element_type=F32)


def _hl_dot(a, b):
    hi = a.astype(BF16)
    lo = (a - hi.astype(F32)).astype(BF16)
    return _dot(hi, b) + _dot(lo, b)


def _sigmoid(x):
    return 1.0 / (1.0 + jnp.exp(-x))


def _rope_swap(x, lane):
    left = pltpu.roll(x, LANES - 16, axis=1)
    right = pltpu.roll(x, 16, axis=1)
    lo = (lane >= 64) & (lane < 80)
    hi = (lane >= 80) & (lane < 96)
    return jnp.where(lo, left, jnp.where(hi, right, 0.0))


def _two_level_gather(x_refs, out_refs, send_sems, recv_sems, local_sems):
    x, y, c = lax.axis_index("x"), lax.axis_index("y"), lax.axis_index("c")
    me, sibling = (x, y, c), (x, y, 1 - c)
    chips = [(1 - x, y), (x, 1 - y), (1 - x, 1 - y)]
    ops = range(len(x_refs))

    def slot(o, px, py, pc):
        return out_refs[o].at[4 * px + 2 * py + pc]

    def copy(o, k, block, to, src=None):
        return pltpu.make_async_remote_copy(
            src_ref=slot(o, *block) if src is None else src, dst_ref=slot(o, *block),
            send_sem=send_sems.at[o, k], recv_sem=recv_sems.at[o, k],
            device_id=to, device_id_type=MESH)

    def mine():
        return [pltpu.make_async_copy(x_refs[o], slot(o, *me), local_sems.at[o]) for o in ops]

    def first():
        return ([copy(o, 0, me, sibling, src=x_refs[o]) for o in ops]
                + [copy(o, 1 + j, me, (*chip, c), src=x_refs[o]) for j, chip in enumerate(chips) for o in ops])

    def start():
        for cp in mine() + first():
            cp.start()

    def finish():
        passed = []
        for j, chip in enumerate(chips):
            for o in ops:
                copy(o, 1 + j, (*chip, c), me).wait_recv()
                passed.append(copy(o, 4 + j, (*chip, c), sibling))
                passed[-1].start()
        for o in ops:
            copy(o, 0, sibling, me).wait_recv()
        for j, chip in enumerate(chips):
            for o in ops:
                copy(o, 4 + j, (*chip, 1 - c), me).wait_recv()
        for cp in first() + passed:
            cp.wait_send()
        for cp in mine():
            cp.wait()

    return start, finish


def _gather_sems(n_op):
    return [pltpu.SemaphoreType.DMA((n_op, 7)), pltpu.SemaphoreType.DMA((n_op, 7)),
            pltpu.SemaphoreType.DMA((n_op,))]


def _all_gather(shards):
    n_op = len(shards)

    def body(*refs):
        start, finish = _two_level_gather(refs[:n_op], refs[n_op:2 * n_op], *refs[2 * n_op:])
        start()
        finish()

    any_spec = pl.BlockSpec(memory_space=pl.ANY)
    return pl.pallas_call(
        body, name="weight_all_gather",
        out_shape=[jax.ShapeDtypeStruct((N_DEV,) + a.shape, a.dtype) for a in shards],
        in_specs=[any_spec] * n_op, out_specs=[any_spec] * n_op, scratch_shapes=_gather_sems(n_op),
        compiler_params=pltpu.CompilerParams(vmem_limit_bytes=4 << 20),
    )(*shards)


def _pair_copies(g_refs, l_refs, ssem, rsem):
    x, y, c = lax.axis_index("x"), lax.axis_index("y"), lax.axis_index("c")
    copies = []
    for o in range(len(g_refs)):
        for chip in range(4):
            copies.append(pltpu.make_async_remote_copy(
                src_ref=g_refs[o].at[2 * chip + (1 - c)], dst_ref=l_refs[o].at[chip],
                send_sem=ssem.at[o, chip], recv_sem=rsem.at[o, chip],
                device_id=(x, y, 1 - c), device_id_type=MESH))
    return copies


def _pair_specs(pays):
    n_op = len(pays)
    any_spec = pl.BlockSpec(memory_space=pl.ANY)
    return ([any_spec] * n_op, [any_spec] * n_op,
            [jax.ShapeDtypeStruct((4,) + a.shape[1:], F32) for a in pays],
            [pltpu.SemaphoreType.DMA((n_op, 4)), pltpu.SemaphoreType.DMA((n_op, 4))])


def _pair_exchange(pays, name):
    n_op = len(pays)
    in_specs, out_specs, out_shape, sems = _pair_specs(pays)

    def body(*refs):
        copies = _pair_copies(refs[:n_op], refs[n_op:2 * n_op], *refs[2 * n_op:])
        for cp in copies:
            cp.start()
        for cp in copies:
            cp.wait()

    return pl.pallas_call(body, name=name, out_shape=out_shape, in_specs=in_specs, out_specs=out_specs,
                          scratch_shapes=sems,
                          compiler_params=pltpu.CompilerParams(vmem_limit_bytes=4 << 20))(*pays)


def _slab_exchange(small):
    sr, n = small.shape

    def body(s_ref, sland_ref, ssem, rsem, lsem):
        x, y, c = lax.axis_index("x"), lax.axis_index("y"), lax.axis_index("c")
        me = 4 * x + 2 * y + c
        copies = []
        for k in range(1, N_DEV):
            peer = (1 - x if (k >> 2) & 1 else x, 1 - y if (k >> 1) & 1 else y, 1 - c if k & 1 else c)
            copies.append(pltpu.make_async_remote_copy(
                src_ref=s_ref, dst_ref=sland_ref.at[me], send_sem=ssem.at[k], recv_sem=rsem.at[k],
                device_id=peer, device_id_type=MESH))
        own = pltpu.make_async_copy(s_ref, sland_ref.at[me], lsem)
        own.start()
        for cp in copies:
            cp.start()
        for cp in copies:
            cp.wait()
        own.wait()

    any_spec = pl.BlockSpec(memory_space=pl.ANY)
    return pl.pallas_call(
        body, name="grad_slab_exchange", out_shape=jax.ShapeDtypeStruct((N_DEV, sr, n), F32),
        in_specs=[any_spec], out_specs=any_spec,
        scratch_shapes=[pltpu.SemaphoreType.DMA((N_DEV,)), pltpu.SemaphoreType.DMA((N_DEV,)),
                        pltpu.SemaphoreType.DMA],
        compiler_params=pltpu.CompilerParams(vmem_limit_bytes=4 << 20),
    )(small)


def _pair_sums(pays, landed, place, name):
    n = len(pays)
    dims = [p.shape[1:] for p in pays]

    def body(place_ref, *refs):
        g_refs, l_refs, s_refs, own_refs = refs[:n], refs[n:2 * n], refs[2 * n:3 * n], refs[3 * n:]
        i = pl.program_id(0)
        for o in range(n):
            tot = g_refs[o][...] + l_refs[o][...]
            s_refs[o][...] = tot.astype(BF16)

            @pl.when(i == place_ref[1])
            def _(o=o, tot=tot):
                own_refs[o][...] = tot

    grid_spec = pltpu.PrefetchScalarGridSpec(
        num_scalar_prefetch=1, grid=(4,),
        in_specs=[pl.BlockSpec((None, r, c), lambda i, pr: (2 * i + pr[0], 0, 0)) for r, c in dims]
        + [pl.BlockSpec((None, r, c), lambda i, pr: (i, 0, 0)) for r, c in dims],
        out_specs=[pl.BlockSpec((None, r, c), lambda i, pr: (i, 0, 0)) for r, c in dims]
        + [pl.BlockSpec((r, c), lambda i, pr: (0, 0)) for r, c in dims])
    out = pl.pallas_call(
        body, name=name, grid_spec=grid_spec,
        out_shape=[jax.ShapeDtypeStruct((4, r, c), BF16) for r, c in dims]
        + [jax.ShapeDtypeStruct((r, c), F32) for r, c in dims],
        compiler_params=_params(("arbitrary",), 16),
    )(place, *pays, *landed)
    return list(zip(out[:n], out[n:]))


def _chip_copies(s_refs, l_refs, ssem, rsem):
    x, y, c = lax.axis_index("x"), lax.axis_index("y"), lax.axis_index("c")
    copies = []
    for rel in range(1, 4):
        px = 1 - x if rel & 2 else x
        py = 1 - y if rel & 1 else y
        for o in range(len(s_refs)):
            copies.append(pltpu.make_async_remote_copy(
                src_ref=s_refs[o].at[2 * px + py], dst_ref=l_refs[o].at[rel - 1],
                send_sem=ssem.at[o, rel - 1], recv_sem=rsem.at[o, rel - 1],
                device_id=(px, py, c), device_id_type=MESH))
    return copies


def _chip_specs(sums):
    n_op = len(sums)
    any_spec = pl.BlockSpec(memory_space=pl.ANY)
    return ([any_spec] * n_op, [any_spec] * n_op,
            [jax.ShapeDtypeStruct((3,) + a.shape[1:], BF16) for a in sums],
            [pltpu.SemaphoreType.DMA((n_op, 3)), pltpu.SemaphoreType.DMA((n_op, 3))])


def _adamw_math(g, w, m, v):
    mn = ADAM_B1 * m + (1.0 - ADAM_B1) * g
    vn = ADAM_B2 * v + (1.0 - ADAM_B2) * (g * g)
    m_hat = mn / (1.0 - ADAM_B1 ** ADAM_STEP)
    v_hat = vn / (1.0 - ADAM_B2 ** ADAM_STEP)
    return -ADAM_LR * (m_hat / (jnp.sqrt(v_hat) + ADAM_EPS) + ADAM_WD * w), mn, vn


def _adamw_matrix(own, landed, w, m, v, name):
    _, r, c = w.shape
    cp = own.shape[1]
    br = min(r, 256)

    def body(own_ref, l_ref, w_ref, m_ref, v_ref, g_out, d_out, m_out, v_out):
        g = own_ref[...]
        for k in range(3):
            g = g + l_ref[k].astype(F32)
        g = g[:, :c]
        g_out[...] = g
        d_out[...], m_out[...], v_out[...] = _adamw_math(g, w_ref[...], m_ref[...], v_ref[...])

    row = pl.BlockSpec((None, br, c), lambda i: (0, i, 0))
    shp = jax.ShapeDtypeStruct((1, r, c), F32)
    return pl.pallas_call(
        body, name=name, grid=(r // br,),
        in_specs=[pl.BlockSpec((br, cp), lambda i: (i, 0)), pl.BlockSpec((3, br, cp), lambda i: (0, i, 0)),
                  row, row, row],
        out_specs=(row, row, row, row), out_shape=(shp, shp, shp, shp),
        compiler_params=_params(("parallel",), 12),
    )(own, landed, w, m, v)


_VEC_PLACE = ((0, 0), (1, 0), (2, 0), (3, 0), (3, D_GRP), (4, 0), (5, 0), (6, 0))


def _adamw_vectors(sland, ws, ms, vs):
    nv = len(ws)

    def body(l_ref, *refs):
        w_refs, m_refs, v_refs = refs[:nv], refs[nv:2 * nv], refs[2 * nv:3 * nv]
        loss_ref = refs[3 * nv]
        outs = refs[3 * nv + 1:]
        g_all = l_ref[0]
        for j in range(1, N_DEV):
            g_all = g_all + l_ref[j]
        loss_ref[...] = jnp.sum(g_all[7:8, :], axis=1, keepdims=True)
        for k, (row, lane0) in enumerate(_VEC_PLACE):
            n = w_refs[k].shape[1]
            g = g_all[row:row + 1, lane0:lane0 + n]
            d, mn, vn = _adamw_math(g, w_refs[k][...], m_refs[k][...], v_refs[k][...])
            outs[k][...] = g
            outs[nv + k][...] = d
            outs[2 * nv + k][...] = mn
            outs[3 * nv + k][...] = vn

    def whole(shape):
        return pl.BlockSpec(shape, lambda i: (0,) * len(shape))

    shapes = [jax.ShapeDtypeStruct(w.shape, F32) for w in ws]
    return pl.pallas_call(
        body, name="adamw_vectors", grid=(1,),
        in_specs=[whole(sland.shape)] + [whole(w.shape) for w in ws] * 3,
        out_specs=[whole((1, 1))] + [whole(w.shape) for w in ws] * 4,
        out_shape=[jax.ShapeDtypeStruct((1, 1), F32)] + shapes * 4,
        compiler_params=_params(("arbitrary",), 4),
    )(sland, *ws, *ms, *vs)


def _in_proj(x, g, w, shards):
    s = x.shape[0]
    n_op = len(shards)
    steps = s // TM_IO

    def body(x_ref, g_ref, w_ref, *refs):
        shard_refs = refs[:n_op]
        qkv_ref, rest_ref, h_ref = refs[n_op:n_op + 3]
        gath_refs = refs[n_op + 3:2 * n_op + 3]
        start, finish = _two_level_gather(shard_refs, gath_refs, *refs[2 * n_op + 3:])
        i = pl.program_id(0)

        @pl.when(i == 0)
        def _():
            start()

        xv = x_ref[...]
        r = lax.rsqrt(jnp.mean(xv * xv, axis=-1, keepdims=True) + EPS)
        h = ((xv * r) * g_ref[...]).astype(BF16)
        h_ref[...] = h
        qkv_ref[...] = _dot(h, w_ref[:, :1536]).astype(BF16)
        rest_ref[...] = _dot(h, w_ref[:, 1536:])

        @pl.when(i == steps - 1)
        def _():
            finish()

    any_spec = pl.BlockSpec(memory_space=pl.ANY)
    return pl.pallas_call(
        body, name="in_proj", grid=(steps,),
        in_specs=[pl.BlockSpec((TM_IO, D_MODEL), lambda i: (i, 0)),
                  pl.BlockSpec((1, D_MODEL), lambda i: (0, 0)),
                  pl.BlockSpec((D_MODEL, D_IN_P), lambda i: (0, 0))] + [any_spec] * n_op,
        out_specs=[pl.BlockSpec((TM_IO, 1536), lambda i: (i, 0)),
                   pl.BlockSpec((TM_IO, 1536), lambda i: (i, 0)),
                   pl.BlockSpec((TM_IO, D_MODEL), lambda i: (i, 0))] + [any_spec] * n_op,
        out_shape=[pltpu.HBM((s, 1536), BF16), pltpu.HBM((s, 1536), F32),
                   pltpu.HBM((s, D_MODEL), BF16)]
        + [jax.ShapeDtypeStruct((N_DEV,) + a.shape, a.dtype) for a in shards],
        scratch_shapes=_gather_sems(n_op),
        compiler_params=_params(("arbitrary",), 32),
    )(x, g, w, *shards)


def _mla_prep(rest, gq, gkv, wuq, wuk, wuv, cos_t, sin_t):
    s = rest.shape[0]

    def body(cq_ref, ckv_ref, kr_ref, gq_ref, gkv_ref, wuq_ref, wuk_ref, wuv_ref, c_ref, s_ref,
             qp_ref, kp_ref, vv_ref, cqn_ref, ckvn_ref):
        lane = lax.broadcasted_iota(jnp.int32, (1, LANES), 1)
        cos_v, sin_v = c_ref[...], s_ref[...]
        cq = cq_ref[...]
        rq = lax.rsqrt(jnp.mean(cq * cq, axis=-1, keepdims=True) + EPS)
        cqn = ((cq * rq) * gq_ref[...]).astype(BF16)
        cqn_ref[...] = cqn
        q = _dot(cqn, wuq_ref[...])
        ckv = ckv_ref[...]
        rkv = lax.rsqrt(jnp.mean(ckv * ckv, axis=-1, keepdims=True) + EPS)
        ckvn = ((ckv * rkv) * gkv_ref[...]).astype(BF16)
        ckvn_ref[...] = ckvn
        kn = _dot(ckvn, wuk_ref[...])
        vv_ref[...] = _dot(ckvn, wuv_ref[...]).astype(BF16)
        kr = kr_ref[...]
        kr_roped = kr * cos_v + _rope_swap(kr, lane) * sin_v
        for h in range(N_HEADS):
            sl = slice(h * LANES, (h + 1) * LANES)
            qh = q[:, sl]
            qp_ref[:, sl] = (qh * cos_v + _rope_swap(qh, lane) * sin_v).astype(BF16)
            kp_ref[:, sl] = (kn[:, sl] + kr_roped).astype(BF16)

    def row(width, idx):
        return pl.BlockSpec((TM, width), lambda i: (i, idx))

    def full(a):
        return pl.BlockSpec(a.shape, lambda i: (0, 0))

    return pl.pallas_call(
        body, name="mla_prep", grid=(s // TM,),
        in_specs=[row(Q_LORA, 4), row(KV_LORA, 10), row(LANES, 11), full(gq), full(gkv),
                  full(wuq), full(wuk), full(wuv), row(LANES, 0), row(LANES, 0)],
        out_specs=(row(1024, 0), row(1024, 0), row(D_GRP, 0), row(Q_LORA, 0), row(KV_LORA, 0)),
        out_shape=(pltpu.HBM((s, 1024), BF16), pltpu.HBM((s, 1024), BF16),
                   pltpu.HBM((s, D_GRP), BF16), pltpu.HBM((s, Q_LORA), BF16),
                   pltpu.HBM((s, KV_LORA), BF16)),
        compiler_params=_params(("parallel",), 12),
    )(*_hbm(rest, rest, rest), gq, gkv, wuq, wuk, wuv, cos_t, sin_t)


def _sb_live(n, qi, carries):
    top = carries[0]
    for c in carries[1:]:
        top = jnp.maximum(top, c)
    return jnp.logical_and(n < qi, jnp.max(top) > -SB_CUTOFF)


def _sb_fwd(qkv, hb):
    s = qkv.shape[0]

    def body(q_ref, k_ref, v_ref, o_ref, acc):
        qi = pl.program_id(1)
        lane = lax.broadcasted_iota(jnp.int32, (1, LANES), 1)
        is_a = lane < HEAD_DIM
        pair = lambda h: slice((h // 2) * LANES, (h // 2 + 1) * LANES)
        q_h = []
        for h in range(hb):
            qs = q_ref[:, pair(h)] * SB_SCALE
            mine = is_a if h % 2 == 0 else jnp.logical_not(is_a)
            q_h.append(jnp.where(mine, qs, jnp.zeros_like(qs)))
        r_i = lax.broadcasted_iota(jnp.int32, (TQ, TK), 0)
        c_i = lax.broadcasted_iota(jnp.int32, (TQ, TK), 1)
        past = c_i < r_i
        upper = (r_i > c_i).astype(BF16)
        acc[...] = jnp.zeros_like(acc)

        def tile(j, carries, diag):
            ks = pl.ds(pl.multiple_of(j * TK, TK), TK)
            zs = [_dot_nt(q_h[h], k_ref[ks, pair(h)]) for h in range(hb)]
            if diag:
                zs = [jnp.where(past, z, NEG) for z in zs]
            lfs = [-(jnp.maximum(z, 0.0) + jnp.log(1.0 + jnp.exp(-jnp.abs(z)))) for z in zs]
            sufs = [_hl_dot(lfs[h], upper) for h in range(hb)]
            out = []
            for h in range(hb):
                w = jnp.exp(zs[h] + lfs[h] + (sufs[h] + carries[h]))
                acc[h] += _dot(w.astype(BF16), v_ref[ks, pair(h)])
                out.append(carries[h] + jnp.sum(lfs[h], axis=1, keepdims=True))
            return tuple(out)

        zero = jnp.zeros((TQ, 1), F32)
        carries = tile(qi, (zero,) * hb, True)

        def step(st):
            return (st[0] + 1,) + tile(qi - 1 - st[0], st[1:], False)

        lax.while_loop(lambda st: _sb_live(st[0], qi, st[1:]), step, (0,) + carries)
        for pr in range(hb // 2):
            o_ref[:, pr * LANES:(pr + 1) * LANES] = jnp.where(is_a, acc[2 * pr], acc[2 * pr + 1])

    width = hb * HEAD_DIM
    nb = D_GRP // width
    slab = lambda part: pl.BlockSpec((s, width), lambda g, qi: (0, part * nb + g))
    blk = pl.BlockSpec((TQ, width), lambda g, qi: (qi, g))
    return pl.pallas_call(
        body, name="sb_fwd", grid=(nb, s // TQ),
        in_specs=[blk, slab(1), slab(2)], out_specs=blk,
        out_shape=pltpu.HBM((s, D_GRP), F32),
        scratch_shapes=[pltpu.VMEM((hb, TQ, LANES), F32)],
        compiler_params=_params(("arbitrary", "arbitrary"), 28),
    )(*_hbm(qkv, qkv, qkv))


def _sb_bwd(qkv, d_o, sums):
    s = qkv.shape[0]
    nq = s // TQ
    nk = s // TK
    n_op = len(sums)
    ride_in, ride_out, ride_shape, ride_sems = _chip_specs(sums)

    def body(q_ref, k_ref, v_ref, do_ref, *refs):
        s_refs = refs[:n_op]
        dq_ref, dk_ref, dv_ref = refs[n_op:n_op + 3]
        l_refs = refs[n_op + 3:2 * n_op + 3]
        x1s, bts, dqacc, dkacc, dvacc, ssem, rsem = refs[2 * n_op + 3:]
        qi = pl.program_id(1)
        first_step = jnp.logical_and(pl.program_id(0) == 0, qi == 0)
        last_step = jnp.logical_and(pl.program_id(0) == pl.num_programs(0) - 1, qi == nq - 1)

        @pl.when(first_step)
        def _():
            for cp in _chip_copies(s_refs, l_refs, ssem, rsem):
                cp.start()

        lane = lax.broadcasted_iota(jnp.int32, (1, LANES), 1)
        is_a = lane < HEAD_DIM

        @pl.when(qi == 0)
        def _():
            dkacc[...] = jnp.zeros_like(dkacc)
            dvacc[...] = jnp.zeros_like(dvacc)

        qs = q_ref[...] * SB_SCALE
        zq = jnp.zeros_like(qs)
        qs_x = (jnp.where(is_a, qs, zq), jnp.where(is_a, zq, qs))
        dob = do_ref[...].astype(BF16)
        do_x = (jnp.where(is_a, dob, zq), jnp.where(is_a, zq, dob))
        r_i = lax.broadcasted_iota(jnp.int32, (TQ, TK), 0)
        c_i = lax.broadcasted_iota(jnp.int32, (TQ, TK), 1)
        past = c_i < r_i
        upper = (r_i > c_i).astype(BF16)
        upper_incl = (r_i >= c_i).astype(BF16)
        dqacc[...] = jnp.zeros_like(dqacc)
        both = ((0, 0), (0, 1), (1, 0), (1, 1))

        def tiles(n):
            j_hi = qi - 2 * n
            lo_ok = j_hi >= 1
            j_lo = jnp.maximum(j_hi - 1, 0)
            ks = (pl.ds(pl.multiple_of(j_hi * TK, TK), TK), pl.ds(pl.multiple_of(j_lo * TK, TK), TK))
            return j_hi, lo_ok, j_lo, ks

        def sweep(n, carries):
            j_hi, lo_ok, j_lo, ks = tiles(n)
            slot = (j_hi, jnp.where(lo_ok, j_lo, nk))
            valid = (jnp.logical_or(past, j_hi < qi), lo_ok)
            z = {th: jnp.where(valid[th[0]], _dot_nt(qs_x[th[1]], k_ref[ks[th[0]], :]), NEG) for th in both}
            log_b, lf_sum, suf = {}, {}, {}
            for th in both:
                lf = -(jnp.maximum(z[th], 0.0) + jnp.log(1.0 + jnp.exp(-jnp.abs(z[th]))))
                log_b[th] = z[th] + lf
                lf_sum[th] = jnp.sum(lf, axis=1, keepdims=True)
                suf[th] = _hl_dot(lf, upper)
            c, g_in = {}, {}
            for h in range(2):
                c[0, h], g_in[0, h] = carries[2 * h], carries[2 * h + 1]
                c[1, h] = c[0, h] + lf_sum[0, h]
            d_a = {th: _dot_nt(do_x[th[1]], v_ref[ks[th[0]], :]) for th in both}
            a_b, g, g_sum, sg = {}, {}, {}, {}
            for th in both:
                a = jnp.exp(log_b[th] + (suf[th] + c[th]))
                a_b[th] = a.astype(BF16)
                g[th] = a * d_a[th]
                g_sum[th] = jnp.sum(g[th], axis=1, keepdims=True)
                sg[th] = _hl_dot(g[th], upper_incl)
            for h in range(2):
                g_in[1, h] = g_in[0, h] + g_sum[0, h]
            for th in both:
                t, h = th
                beta = jnp.exp(log_b[th])
                x1s[slot[t], h] = g[th] * (1.0 - beta) + beta * (sg[th] + g_in[th])
                bts[slot[t], h] = beta
                dvacc[ks[t], :] += _dot_tn(a_b[th], do_x[h])
            out = []
            for h in range(2):
                out.append(c[1, h] + lf_sum[1, h])
                out.append(g_in[1, h] + g_sum[1, h])
            return tuple(out)

        zero = jnp.zeros((TQ, 1), F32)
        first = sweep(0, (zero, zero, zero, zero))

        def more(st):
            return jnp.logical_and(2 * st[0] <= qi, jnp.max(jnp.maximum(st[1], st[3])) > -SB_CUTOFF)

        swept = lax.while_loop(more, lambda st: (st[0] + 1,) + sweep(st[0], st[1:]), (1,) + first)
        g_tot = (swept[2], swept[4])

        def apply(n, carry):
            j_hi, lo_ok, j_lo, ks = tiles(n)

            def one(j, kslice):
                for h in range(2):
                    dz = (x1s[j, h] - bts[j, h] * g_tot[h]).astype(BF16)
                    dqacc[h] += _dot(dz, k_ref[kslice, :])
                    dkacc[kslice, :] += _dot_tn(dz, qs_x[h])

            one(j_hi, ks[0])

            @pl.when(lo_ok)
            def _():
                one(j_lo, ks[1])

            return carry

        lax.fori_loop(0, swept[0], apply, 0)
        dq_ref[...] = (jnp.where(is_a, dqacc[0], dqacc[1]) * SB_SCALE).astype(BF16)

        @pl.when(qi == nq - 1)
        def _():
            dk_ref[...] = dkacc[...].astype(BF16)
            dv_ref[...] = dvacc[...].astype(BF16)

        @pl.when(last_step)
        def _():
            for cp in _chip_copies(s_refs, l_refs, ssem, rsem):
                cp.wait()

    slab = lambda off: pl.BlockSpec((s, LANES), lambda p, qi: (0, off + p))
    blk = pl.BlockSpec((TQ, LANES), lambda p, qi: (qi, p))
    out_slab = pl.BlockSpec((s, LANES), lambda p, qi: (0, p))
    shp = pltpu.HBM((s, D_GRP), BF16)
    return pl.pallas_call(
        body, name="sb_bwd", grid=(4, nq),
        in_specs=[blk, slab(4), slab(8), blk] + ride_in,
        out_specs=[blk, out_slab, out_slab] + ride_out, out_shape=[shp, shp, shp] + ride_shape,
        scratch_shapes=[pltpu.VMEM((nk + 1, 2, TQ, TK), F32)] * 2
        + [pltpu.VMEM((2, TQ, LANES), F32), pltpu.VMEM((s, LANES), F32), pltpu.VMEM((s, LANES), F32)]
        + ride_sems,
        compiler_params=_params(("arbitrary", "arbitrary"), 44),
    )(*_hbm(qkv, qkv, qkv, d_o), *sums)


def _mla_fwd(qp, kp, vv, hb):
    s = qp.shape[0]
    c2 = MLA_SCALE * LOG2_E

    def body(q_ref, k_ref, v_ref, o_ref, lse_ref, vaug, mrun, mb, acc, zbuf):
        qi = pl.program_id(1)
        lane = lax.broadcasted_iota(jnp.int32, (1, LANES), 1)
        is_a = lane < HEAD_DIM

        @pl.when(qi == 0)
        def _():
            for h in range(hb):
                vp = v_ref[:, (h // 2) * LANES:(h // 2 + 1) * LANES]
                mine = is_a if h % 2 == 0 else jnp.logical_not(is_a)
                vaug[h] = jnp.where(mine, vp, jnp.ones_like(vp))

        r_i = lax.broadcasted_iota(jnp.int32, (TQ, TK), 0)
        c_i = lax.broadcasted_iota(jnp.int32, (TQ, TK), 1)
        visible = (c_i >> CHUNK_SHIFT) <= (r_i >> CHUNK_SHIFT)

        def key_rows(j):
            return pl.ds(pl.multiple_of(j * TK, TK), TK)

        def sweep(tiles):
            def loop(n, carry):
                tiles(((2 * n, False), (2 * n + 1, False)))
                return carry

            lax.fori_loop(0, qi // 2, loop, 0)

            @pl.when(qi % 2 == 1)
            def _():
                tiles(((qi - 1, False), (qi, True)))

            @pl.when(qi % 2 == 0)
            def _():
                tiles(((qi, True),))

        mrun[...] = jnp.full_like(mrun, NEG)

        def tiles_max(js):
            zs = [[_dot_nt(q_ref[:, h * LANES:(h + 1) * LANES], k_ref[key_rows(j), h * LANES:(h + 1) * LANES])
                   for h in range(hb)] for j, _ in js]
            for t, (j, diag) in enumerate(js):
                for h in range(hb):
                    z = jnp.where(visible, zs[t][h], NEG) if diag else zs[t][h]
                    zbuf[j, h] = z
                    mrun[h] = jnp.maximum(mrun[h], z)

        sweep(tiles_max)
        for h in range(hb):
            m = jnp.max(mrun[h], axis=1, keepdims=True) * c2
            mb[h] = jnp.broadcast_to(m, (TQ, TK))
        acc[...] = jnp.zeros_like(acc)

        def tiles_pv(js):
            ps = [[jnp.exp2((zbuf[j, h] * c2 - mb[h]).astype(BF16)) for h in range(hb)] for j, _ in js]
            for t, (j, _) in enumerate(js):
                for h in range(hb):
                    acc[h] += _dot(ps[t][h], vaug[h, key_rows(j), :])

        sweep(tiles_pv)
        for pr in range(hb // 2):
            a, b = 2 * pr, 2 * pr + 1
            psl = slice(pr * LANES, (pr + 1) * LANES)
            acc_a, acc_b = acc[a], acc[b]
            l_a = pltpu.roll(acc_a, HEAD_DIM, axis=1)
            l_b = pltpu.roll(acc_b, HEAD_DIM, axis=1)
            o_ref[:, psl] = jnp.where(is_a, acc_a * (1.0 / l_a), acc_b * (1.0 / l_b))
            lse_ref[:, psl] = jnp.where(is_a, mb[a, :, :LANES] * LN_2 + jnp.log(l_a),
                                        mb[b, :, :LANES] * LN_2 + jnp.log(l_b))

    blk = pl.BlockSpec((TQ, hb * HEAD_DIM), lambda g, qi: (qi, g))
    shp = pltpu.HBM((s, D_GRP), F32)
    return pl.pallas_call(
        body, name="mla_fwd", grid=(N_HEADS // hb, s // TQ),
        in_specs=[pl.BlockSpec((TQ, hb * LANES), lambda g, qi: (qi, g)),
                  pl.BlockSpec((s, hb * LANES), lambda g, qi: (0, g)),
                  pl.BlockSpec((s, hb * HEAD_DIM), lambda g, qi: (0, g))],
        out_specs=(blk, blk), out_shape=(shp, shp),
        scratch_shapes=[pltpu.VMEM((hb, s, LANES), BF16), pltpu.VMEM((hb, TQ, TK), F32),
                        pltpu.VMEM((hb, TQ, TK), F32), pltpu.VMEM((hb, TQ, LANES), F32),
                        pltpu.VMEM((s // TK, hb, TQ, TK), F32)],
        compiler_params=_params(("arbitrary", "arbitrary"), 44),
    )(*_hbm(qp, kp, vv))


def _mla_bwd(qp, kp, vv, d_o, o, lse, hb, pays):
    s = qp.shape[0]
    nq = s // TQ
    c2 = MLA_SCALE * LOG2_E
    n_op = len(pays)
    ride_in, ride_out, ride_shape, ride_sems = _pair_specs(pays)

    def body(q_ref, k_ref, v_ref, do_ref, o_ref, lse_ref, *refs):
        g_refs = refs[:n_op]
        dq_ref, dk_ref, dv_ref = refs[n_op:n_op + 3]
        l_refs = refs[n_op + 3:2 * n_op + 3]
        dqacc, lse_b, delta_b, q_t, do_t, ssem, rsem = refs[2 * n_op + 3:]
        qi = pl.program_id(1)

        @pl.when(jnp.logical_and(pl.program_id(0) == 0, qi == 0))
        def _():
            for cp in _pair_copies(g_refs, l_refs, ssem, rsem):
                cp.start()

        lane = lax.broadcasted_iota(jnp.int32, (1, LANES), 1)
        is_a = lane < HEAD_DIM

        @pl.when(qi == 0)
        def _():
            dk_ref[...] = jnp.zeros_like(dk_ref)
            dv_ref[...] = jnp.zeros_like(dv_ref)

        r_i = lax.broadcasted_iota(jnp.int32, (TQ, TK), 0)
        c_i = lax.broadcasted_iota(jnp.int32, (TQ, TK), 1)
        visible = (c_i >> CHUNK_SHIFT) <= (r_i >> CHUNK_SHIFT)
        do_x = []
        for h in range(hb):
            psl = slice((h // 2) * LANES, (h // 2 + 1) * LANES)
            mine = is_a if h % 2 == 0 else jnp.logical_not(is_a)
            d_o = do_ref[:, psl]
            delta = jnp.sum(jnp.where(mine, d_o * o_ref[:, psl], 0.0), axis=1, keepdims=True)
            lse_h = jnp.sum(jnp.where(lane == (h % 2) * HEAD_DIM, lse_ref[:, psl], 0.0), axis=1, keepdims=True)
            lse_b[h] = jnp.broadcast_to(lse_h * LOG2_E, (TQ, TK))
            delta_b[h] = jnp.broadcast_to(delta, (TQ, TK))
            do_h = jnp.where(mine, d_o, 0.0)
            do_x.append(do_h.astype(BF16))
            do_t[h] = do_h.T.astype(BF16)
            q_t[h] = q_ref[:, h * LANES:(h + 1) * LANES].astype(F32).T.astype(BF16)
        dqacc[...] = jnp.zeros_like(dqacc)

        head = lambda h: slice(h * LANES, (h + 1) * LANES)
        pair = lambda h: slice((h // 2) * LANES, (h // 2 + 1) * LANES)

        def tiles(js):
            th = [(j, diag, pl.ds(pl.multiple_of(j * TK, TK), TK), h) for j, diag in js for h in range(hb)]
            zs = [_dot_nt(q_ref[:, head(h)], k_ref[ks, head(h)]) for _, _, ks, h in th]
            dps = [_dot_nt(do_x[h], v_ref[ks, pair(h)]) for _, _, ks, h in th]
            for i, (j, diag, ks, h) in enumerate(th):
                e = zs[i] * c2 - lse_b[h]
                if diag:
                    e = jnp.where(visible, e, NEG)
                p = jnp.exp2(e)
                ds = (p * (dps[i] - delta_b[h]) * MLA_SCALE).astype(BF16)
                dqacc[h] += _dot(ds, k_ref[ks, head(h)])
                dk_ref[head(h), ks] += _dot(q_t[h], ds)
                dv_ref[pair(h), ks] += _dot(do_t[h], p.astype(BF16))

        def loop(n, c):
            tiles(((2 * n, False), (2 * n + 1, False)))
            return c

        lax.fori_loop(0, qi // 2, loop, 0)

        @pl.when(qi % 2 == 1)
        def _():
            tiles(((qi - 1, False), (qi, True)))

        @pl.when(qi % 2 == 0)
        def _():
            tiles(((qi, True),))

        for h in range(hb):
            dq_ref[:, h * LANES:(h + 1) * LANES] = dqacc[h]

        @pl.when(jnp.logical_and(pl.program_id(0) == pl.num_programs(0) - 1, qi == nq - 1))
        def _():
            for cp in _pair_copies(g_refs, l_refs, ssem, rsem):
                cp.wait()

    blk = pl.BlockSpec((TQ, hb * HEAD_DIM), lambda g, qi: (qi, g))
    return pl.pallas_call(
        body, name="mla_bwd", grid=(N_HEADS // hb, nq),
        in_specs=[pl.BlockSpec((TQ, hb * LANES), lambda g, qi: (qi, g)),
                  pl.BlockSpec((s, hb * LANES), lambda g, qi: (0, g)),
                  pl.BlockSpec((s, hb * HEAD_DIM), lambda g, qi: (0, g)), blk, blk, blk] + ride_in,
        out_specs=[pl.BlockSpec((TQ, hb * LANES), lambda g, qi: (qi, g)),
                   pl.BlockSpec((hb * LANES, s), lambda g, qi: (g, 0)),
                   pl.BlockSpec((hb * HEAD_DIM, s), lambda g, qi: (g, 0))] + ride_out,
        out_shape=[pltpu.HBM((s, 1024), F32), pltpu.HBM((1024, s), F32),
                   pltpu.HBM((D_GRP, s), F32)] + ride_shape,
        scratch_shapes=[pltpu.VMEM((hb, TQ, LANES), F32), pltpu.VMEM((hb, TQ, TK), F32),
                        pltpu.VMEM((hb, TQ, TK), F32), pltpu.VMEM((hb, LANES, TQ), BF16),
                        pltpu.VMEM((hb, LANES, TQ), BF16)] + ride_sems,
        compiler_params=_params(("arbitrary", "arbitrary"), 52),
    )(*_hbm(qp, kp, vv, d_o, o, lse), *pays)


def _mid(x, p, target, sb_o, mla_o, rest, g_sb, g_mla, w_out, g_post, w_ple, g_ple, w_pg, b_pg, bd):
    s = x.shape[0]

    def body(x_ref, p_ref, t_ref, sbo_ref, mlo_ref, sbg_ref, mlg_ref, gsb_ref, gml_ref, wout_ref,
             gpost_ref, wple_ref, gple_ref, wpg_ref, bpg_ref, bd_ref,
             dx1_ref, dsbo_ref, dmlo_ref, dsbg_ref, dmlg_ref, x1b_ref, dglb_ref, ycb_ref, dyb_ref,
             pb_ref, dub_ref, small_ref):
        i = pl.program_id(0)
        bd_m = bd_ref[...]

        def seg_mean(v):
            return _dot(v.astype(BF16), bd_m) * (1.0 / HEAD_DIM)

        groups = []
        for o_ref, gate_ref, gain_ref in ((sbo_ref, sbg_ref, gsb_ref), (mlo_ref, mlg_ref, gml_ref)):
            o = o_ref[...]
            r = lax.rsqrt(seg_mean(o * o) + EPS)
            n = o * r
            hn = n * gain_ref[...]
            gate = gate_ref[...]
            sg = _sigmoid(gate)
            si = gate * sg
            groups.append((r, n, hn, gate, sg, si, gain_ref[...]))
        ya = (groups[0][2] * groups[0][5]).astype(BF16)
        yb = (groups[1][2] * groups[1][5]).astype(BF16)
        ycb_ref[:, :D_GRP] = ya
        ycb_ref[:, D_GRP:] = yb
        y = _dot(ya, wout_ref[:D_GRP, :]) + _dot(yb, wout_ref[D_GRP:, :])
        ry = lax.rsqrt(jnp.mean(y * y, axis=-1, keepdims=True) + EPS)
        ny = y * ry
        x1 = x_ref[...] + ny * gpost_ref[...]
        x1b = x1.astype(BF16)
        x1b_ref[...] = x1b
        pb = p_ref[...].astype(BF16)
        pb_ref[...] = pb
        u = _dot(pb, wple_ref[...])
        ru = lax.rsqrt(jnp.mean(u * u, axis=-1, keepdims=True) + EPS)
        nu = u * ru
        ple = nu * gple_ref[...]
        gate = _sigmoid(_dot(x1b, wpg_ref[...]) + bpg_ref[...])
        x2 = x1 + ple * gate
        diff = x2 - t_ref[...]
        dx2 = diff * (1.0 / D_MODEL)

        d_ple = dx2 * gate
        d_glin = (dx2 * ple) * (gate * (1.0 - gate))
        dglb = d_glin.astype(BF16)
        dglb_ref[...] = dglb
        dx1 = dx2 + _dot_nt(dglb, wpg_ref[...])
        dx1_ref[...] = dx1
        d_nu = d_ple * gple_ref[...]
        d_u = ru * (d_nu - nu * jnp.mean(d_nu * nu, axis=-1, keepdims=True))
        dub_ref[...] = d_u.astype(BF16)
        d_ny = dx1 * gpost_ref[...]
        d_y = ry * (d_ny - ny * jnp.mean(d_ny * ny, axis=-1, keepdims=True))
        dyb = d_y.astype(BF16)
        dyb_ref[...] = dyb
        d_yc = (_dot_nt(dyb, wout_ref[:D_GRP, :]), _dot_nt(dyb, wout_ref[D_GRP:, :]))

        d_gain = []
        for gx, (do_ref, dg_ref) in enumerate(((dsbo_ref, dsbg_ref), (dmlo_ref, dmlg_ref))):
            r, n, hn, gate_g, sg, si, gain = groups[gx]
            dyg = d_yc[gx]
            d_hn = dyg * si
            dg_ref[...] = (dyg * hn * (sg * (1.0 + gate_g * (1.0 - sg)))).astype(BF16)
            d_gain.append(jnp.sum(d_hn * n, axis=0, keepdims=True))
            d_n = d_hn * gain
            do_ref[...] = r * (d_n - n * seg_mean(d_n * n))

        @pl.when(i == 0)
        def _():
            small_ref[...] = jnp.zeros_like(small_ref)

        small_ref[3:4, :D_GRP] += d_gain[0]
        small_ref[3:4, D_GRP:] += d_gain[1]
        small_ref[4:5, :] += jnp.sum(dx1 * ny, axis=0, keepdims=True)
        small_ref[5:6, :] += jnp.sum(d_ple * nu, axis=0, keepdims=True)
        small_ref[6:7, :] += jnp.sum(d_glin, axis=0, keepdims=True)
        small_ref[7:8, :] += jnp.sum(diff * diff, axis=0, keepdims=True) * (0.5 / D_MODEL)

    def row(width, idx=0):
        return pl.BlockSpec((TM, width), lambda i: (i, idx))

    def full(a):
        return pl.BlockSpec(a.shape, lambda i: (0, 0))

    f32 = lambda w: pltpu.HBM((s, w), F32)
    b16 = lambda w: pltpu.HBM((s, w), BF16)
    return pl.pallas_call(
        body, name="mid", grid=(s // TM,),
        in_specs=[row(D_MODEL), row(PLE_DIM), row(D_MODEL), row(D_GRP), row(D_GRP),
                  row(D_GRP, 0), row(D_GRP, 1), full(g_sb), full(g_mla), full(w_out), full(g_post),
                  full(w_ple), full(g_ple), full(w_pg), full(b_pg), full(bd)],
        out_specs=(row(D_MODEL), row(D_GRP), row(D_GRP), row(D_GRP), row(D_GRP), row(D_MODEL),
                   row(D_MODEL), row(D_MODEL), row(D_MODEL), row(PLE_DIM), row(D_MODEL),
                   pl.BlockSpec((8, D_MODEL), lambda i: (0, 0))),
        out_shape=(f32(D_MODEL), f32(D_GRP), f32(D_GRP), b16(D_GRP), b16(D_GRP), b16(D_MODEL),
                   b16(D_MODEL), b16(D_MODEL), b16(D_MODEL), b16(PLE_DIM), b16(D_MODEL),
                   jax.ShapeDtypeStruct((8, D_MODEL), F32)),
        compiler_params=_params(("arbitrary",), 46),
    )(*_hbm(x, p, target, sb_o, mla_o, rest, rest), g_sb, g_mla, w_out, g_post, w_ple, g_ple, w_pg, b_pg, bd)


def _mla_prep_bwd(dqp, dkp, dvv, rest, gq, gkv, wuq, wuk, wuv, cos_t, sin_t):
    s = rest.shape[0]

    def body(dqp_ref, dkp_ref, dvv_ref, cq_ref, ckv_ref, gq_ref, gkv_ref, wuq_ref, wuk_ref, wuv_ref,
             c_ref, s_ref, dcq_ref, dckv_ref, dkr_ref, dqb_ref, dkb_ref, dvb_ref, small_ref):
        i = pl.program_id(0)
        lane = lax.broadcasted_iota(jnp.int32, (1, LANES), 1)
        in_rope = (lane >= HEAD_DIM) & (lane < HEAD_DIM + ROPE_DIM)
        cos_v, sin_v = c_ref[...], s_ref[...]
        dkr_roped = jnp.zeros((TM, LANES), F32)
        for h in range(N_HEADS):
            sl = slice(h * LANES, (h + 1) * LANES)
            dy = dqp_ref[:, sl]
            dqb_ref[:, sl] = (dy * cos_v + _rope_swap(dy * sin_v, lane)).astype(BF16)
            dkh = dkp_ref[sl, :].T
            dkb_ref[:, sl] = dkh.astype(BF16)
            dkr_roped = dkr_roped + jnp.where(in_rope, dkh, 0.0)
        dkr_ref[...] = (dkr_roped * cos_v + _rope_swap(dkr_roped * sin_v, lane)).astype(BF16)
        dvb = dvv_ref[...].T.astype(BF16)
        dvb_ref[...] = dvb

        cq = cq_ref[...]
        rq = lax.rsqrt(jnp.mean(cq * cq, axis=-1, keepdims=True) + EPS)
        nq_ = cq * rq
        d_cqn = _dot_nt(dqb_ref[...], wuq_ref[...])
        d_n = d_cqn * gq_ref[...]
        dcq_ref[...] = (rq * (d_n - nq_ * jnp.mean(d_n * nq_, axis=-1, keepdims=True))).astype(BF16)

        ckv = ckv_ref[...]
        rkv = lax.rsqrt(jnp.mean(ckv * ckv, axis=-1, keepdims=True) + EPS)
        nkv = ckv * rkv
        d_ckvn = _dot_nt(dkb_ref[...], wuk_ref[...]) + _dot_nt(dvb, wuv_ref[...])
        d_n2 = d_ckvn * gkv_ref[...]
        dckv_ref[...] = (rkv * (d_n2 - nkv * jnp.mean(d_n2 * nkv, axis=-1, keepdims=True))).astype(BF16)

        @pl.when(i == 0)
        def _():
            small_ref[...] = jnp.zeros_like(small_ref)

        small_ref[0:1, :] += jnp.sum(d_cqn * nq_, axis=0, keepdims=True)
        small_ref[1:2, :KV_LORA] += jnp.sum(d_ckvn * nkv, axis=0, keepdims=True)

    def row(width, idx=0):
        return pl.BlockSpec((TM, width), lambda i: (i, idx))

    def full(a):
        return pl.BlockSpec(a.shape, lambda i: (0, 0))

    b16 = lambda w: pltpu.HBM((s, w), BF16)
    return pl.pallas_call(
        body, name="mla_prep_bwd", grid=(s // TM,),
        in_specs=[row(1024), pl.BlockSpec((1024, TM), lambda i: (0, i)), pl.BlockSpec((D_GRP, TM), lambda i: (0, i)),
                  row(Q_LORA, 4), row(KV_LORA, 10), full(gq), full(gkv),
                  full(wuq), full(wuk), full(wuv), row(LANES), row(LANES)],
        out_specs=(row(Q_LORA), row(KV_LORA), row(LANES), row(1024), row(1024), row(D_GRP),
                   pl.BlockSpec((8, Q_LORA), lambda i: (0, 0))),
        out_shape=(b16(Q_LORA), b16(KV_LORA), b16(LANES), b16(1024), b16(1024), b16(D_GRP),
                   jax.ShapeDtypeStruct((8, Q_LORA), F32)),
        compiler_params=_params(("arbitrary",), 16),
    )(*_hbm(dqp, dkp, dvv, rest, rest), gq, gkv, wuq, wuk, wuv, cos_t, sin_t)


def _in_bwd(x, g, dx1, pieces, w, sums):
    s = x.shape[0]
    steps = s // TM_IO
    widths = [a.shape[1] for a in pieces]
    offs = [sum(widths[:k]) for k in range(len(widths))]
    n_pc, n_op = len(pieces), len(sums)
    ride_in, ride_out, ride_shape, ride_sems = _chip_specs(sums)

    def body(x_ref, g_ref, dx1_ref, *refs):
        piece_refs = refs[:n_pc]
        w_ref = refs[n_pc]
        s_refs = refs[n_pc + 1:n_pc + 1 + n_op]
        dx_ref, small_ref = refs[n_pc + 1 + n_op:n_pc + 3 + n_op]
        l_refs = refs[n_pc + 3 + n_op:n_pc + 3 + 2 * n_op]
        ssem, rsem = refs[n_pc + 3 + 2 * n_op:]
        i = pl.program_id(0)

        @pl.when(i == 0)
        def _():
            for cp in _chip_copies(s_refs, l_refs, ssem, rsem):
                cp.start()

        dh = jnp.zeros((TM_IO, D_MODEL), F32)
        for pr, off, wd in zip(piece_refs, offs, widths):
            dh = dh + _dot_nt(pr[...], w_ref[:, off:off + wd])
        xv = x_ref[...]
        r = lax.rsqrt(jnp.mean(xv * xv, axis=-1, keepdims=True) + EPS)
        n = xv * r
        d_n = dh * g_ref[...]
        dx_ref[...] = dx1_ref[...] + r * (d_n - n * jnp.mean(d_n * n, axis=-1, keepdims=True))

        @pl.when(i == 0)
        def _():
            small_ref[...] = jnp.zeros_like(small_ref)

        small_ref[0:1, :] += jnp.sum(dh * n, axis=0, keepdims=True)

        @pl.when(i == steps - 1)
        def _():
            for cp in _chip_copies(s_refs, l_refs, ssem, rsem):
                cp.wait()

    def row(width):
        return pl.BlockSpec((TM_IO, width), lambda i: (i, 0))

    return pl.pallas_call(
        body, name="in_bwd", grid=(steps,),
        in_specs=[row(D_MODEL), pl.BlockSpec((1, D_MODEL), lambda i: (0, 0)), row(D_MODEL)]
        + [row(wd) for wd in widths] + [pl.BlockSpec(w.shape, lambda i: (0, 0))] + ride_in,
        out_specs=[row(D_MODEL), pl.BlockSpec((8, D_MODEL), lambda i: (0, 0))] + ride_out,
        out_shape=[pltpu.HBM((s, D_MODEL), F32), jax.ShapeDtypeStruct((8, D_MODEL), F32)]
        + ride_shape,
        scratch_shapes=ride_sems,
        compiler_params=_params(("arbitrary",), 40),
    )(*_hbm(x), g, *_hbm(dx1, *pieces), w, *sums)


def _tn_matmul(a, b, name, blocked=False):
    s, k = a.shape
    n = b.shape[1]
    ts = min(s, TS_DW)
    tn = n if blocked else min(n, 512)
    steps = s // ts

    def body(a_ref, b_ref, o_ref):
        t = pl.program_id(1)

        @pl.when(t == 0)
        def _():
            o_ref[...] = jnp.zeros_like(o_ref)

        prod = _dot_tn(a_ref[...], b_ref[...])
        if blocked:
            for j in range(n // LANES):
                o_ref[j] += prod[:, j * LANES:(j + 1) * LANES]
        else:
            o_ref[...] += prod

    if blocked:
        out_spec = pl.BlockSpec((n // LANES, k, LANES), lambda j, t: (0, 0, 0))
        out_shape = jax.ShapeDtypeStruct((n // LANES, k, LANES), F32)
    else:
        out_spec = pl.BlockSpec((k, tn), lambda j, t: (0, j))
        out_shape = jax.ShapeDtypeStruct((k, n), F32)
    return pl.pallas_call(
        body, name=name, grid=(n // tn, steps),
        in_specs=[pl.BlockSpec((ts, k), lambda j, t: (t, 0)), pl.BlockSpec((ts, tn), lambda j, t: (t, j))],
        out_specs=out_spec, out_shape=out_shape,
        compiler_params=_params(("parallel", "arbitrary"), 20),
    )(a, b)


def _tn_matmul_multi(a, bs, name):
    s, k = a.shape
    widths = [b.shape[1] for b in bs]
    ts = min(s, TS_DW)

    def body(a_ref, *refs):
        b_refs, o_ref = refs[:-1], refs[-1]
        t = pl.program_id(0)

        @pl.when(t == 0)
        def _():
            o_ref[...] = jnp.zeros_like(o_ref)

        av = a_ref[...]
        off = 0
        for b_ref, wd in zip(b_refs, widths):
            o_ref[:, off:off + wd] += _dot_tn(av, b_ref[...])
            off += wd

    return pl.pallas_call(
        body, name=name, grid=(s // ts,),
        in_specs=[pl.BlockSpec((ts, k), lambda t: (t, 0))] + [pl.BlockSpec((ts, wd), lambda t: (t, 0)) for wd in widths],
        out_specs=pl.BlockSpec((k, sum(widths)), lambda t: (0, 0)),
        out_shape=jax.ShapeDtypeStruct((k, sum(widths)), F32),
        compiler_params=_params(("arbitrary",), 30),
    )(a, *bs)


IN_SHARD = 372
_IN_KERNEL_ORDER = ((0, 2048), (2464, 2976), (2048, 2432))
_IN_ROPE = (2432, 2464)
_IN_GRAD_SRC = ((0, 512, 0, 0), (512, 1024, 0, 512), (1024, 1536, 1, 0), (1536, 2048, 1, 512),
                (2048, 2304, 2, 512), (2304, 2432, 2, 768), (2432, 2464, 2, 960), (2464, 2976, 2, 0))


def _shard_cols(gath_in, lo, hi):
    out = []
    while lo < hi:
        j, a = divmod(lo, IN_SHARD)
        b = min(IN_SHARD, a + hi - lo)
        out.append(gath_in[j][:, a:b])
        lo += b - a
    return out


def _kernel_w_in(g_in):
    zc = lambda n: jnp.zeros((D_MODEL, n), BF16)
    parts = [pc for lo, hi in _IN_KERNEL_ORDER for pc in _shard_cols(g_in, lo, hi)]
    parts += [zc(64)] + _shard_cols(g_in, *_IN_ROPE) + [zc(32)]
    return jnp.concatenate(parts, axis=1)


def _kernel_weights(gath):
    g_uq, g_ukv, g_out, g_ple, g_pg = gath
    w_uq_p = jnp.pad(g_uq, ((0, 0), (0, 0), (0, 32))).transpose(1, 0, 2).reshape(Q_LORA, 1024)
    k_only = jnp.where(jnp.arange(LANES) < HEAD_DIM, g_ukv, jnp.zeros_like(g_ukv))
    w_uk_p = k_only.transpose(1, 0, 2).reshape(KV_LORA, 1024)
    w_uv = g_ukv[:, :, HEAD_DIM:].transpose(1, 0, 2).reshape(KV_LORA, D_GRP)
    w_ple = g_ple.transpose(1, 0, 2).reshape(PLE_DIM, D_MODEL)
    return (w_uq_p, w_uk_p, w_uv, g_out.reshape(D_MODEL, D_MODEL), w_ple, g_pg.reshape(D_MODEL, D_MODEL))


def _payload_in(d_cols):
    blocks = []
    for j in range(N_DEV):
        lo, hi = j * IN_SHARD, (j + 1) * IN_SHARD
        parts = []
        for o_lo, o_hi, idx, off in _IN_GRAD_SRC:
            a, b = max(lo, o_lo), min(hi, o_hi)
            if a < b:
                parts.append(d_cols[idx][:, off + a - o_lo:off + b - o_lo])
        blocks.append(jnp.concatenate(parts, axis=1))
    return jnp.stack(blocks)


def _payload_ukv(duk_blk, d_uv):
    dv_blk = d_uv.reshape(KV_LORA, N_HEADS, HEAD_DIM).transpose(1, 0, 2)
    return jnp.concatenate([duk_blk[:, :, :HEAD_DIM], dv_blk], axis=2)


def kernel(x, p, positions, norm_pre_g, w_in, q_norm_g, w_uq, kv_norm_g, w_ukv, sb_out_norm_g, mla_out_norm_g, w_out, norm_post_g, w_ple, ple_norm_g, w_ple_gate, b_ple_gate, loss_target, m_norm_pre_g, m_w_in, m_q_norm_g, m_w_uq, m_kv_norm_g, m_w_ukv, m_sb_out_norm_g, m_mla_out_norm_g, m_w_out, m_norm_post_g, m_w_ple, m_ple_norm_g, m_w_ple_gate, m_b_ple_gate, v_norm_pre_g, v_w_in, v_q_norm_g, v_w_uq, v_kv_norm_g, v_w_ukv, v_sb_out_norm_g, v_mla_out_norm_g, v_w_out, v_norm_post_g, v_w_ple, v_ple_norm_g, v_w_ple_gate, v_b_ple_gate):
    mats = (w_in, w_uq, w_ukv, w_out, w_ple, w_ple_gate)
    m_mats = (m_w_in, m_w_uq, m_w_ukv, m_w_out, m_w_ple, m_w_ple_gate)
    v_mats = (v_w_in, v_w_uq, v_w_ukv, v_w_out, v_w_ple, v_w_ple_gate)
    vecs = (norm_pre_g, q_norm_g, kv_norm_g, sb_out_norm_g, mla_out_norm_g, norm_post_g, ple_norm_g, b_ple_gate)
    m_vecs = (m_norm_pre_g, m_q_norm_g, m_kv_norm_g, m_sb_out_norm_g, m_mla_out_norm_g, m_norm_post_g,
              m_ple_norm_g, m_b_ple_gate)
    v_vecs = (v_norm_pre_g, v_q_norm_g, v_kv_norm_g, v_sb_out_norm_g, v_mla_out_norm_g, v_norm_post_g,
              v_ple_norm_g, v_b_ple_gate)

    shards = [a[0].astype(BF16) for a in mats]
    w_in_p = _kernel_w_in(_all_gather(shards[:1])[0])
    grad_x, reduced, vec_slab = _step(x[0], p[0, 0], positions[0], loss_target[0], *vecs, w_in_p, shards[1:])
    upd = [_adamw_matrix(own, l2, w, m, v, "adamw_%d" % o)
           for o, ((own, l2), w, m, v) in enumerate(zip(reduced, mats, m_mats, v_mats))]
    sm = _adamw_vectors(_slab_exchange(vec_slab), vecs, m_vecs, v_vecs)

    outs = []
    for kind in range(4):
        mat = [upd[o][kind] for o in range(len(mats))]
        vec = sm[1 + 8 * kind:9 + 8 * kind]
        outs += [vec[0], mat[0], vec[1], mat[1], vec[2], mat[2], vec[3], vec[4], mat[3], vec[5],
                 mat[4], vec[6], mat[5], vec[7]]
    return (sm[0][0, 0], grad_x[None], *outs)


def _step(xs, ps, pos, tgt, norm_pre_g, q_norm_g, kv_norm_g, sb_out_norm_g, mla_out_norm_g,
          norm_post_g, ple_norm_g, b_ple_gate, w_in_p, shards):
    s = xs.shape[0]
    place = jnp.stack([lax.axis_index("c"), 2 * lax.axis_index("x") + lax.axis_index("y")]).astype(jnp.int32)

    half = ROPE_DIM // 2
    freq = ROPE_THETA ** (-jnp.arange(half, dtype=F32) / half)
    ang = pos.astype(F32)[:, None] * freq
    cos, sin = jnp.cos(ang), jnp.sin(ang)
    cos_t = jnp.concatenate([jnp.ones((s, 64), F32), cos, cos, jnp.zeros((s, 32), F32)], axis=1)
    sin_t = jnp.concatenate([jnp.zeros((s, 64), F32), -sin, sin, jnp.zeros((s, 32), F32)], axis=1)
    seg = jnp.arange(D_GRP) // HEAD_DIM
    bd = (seg[:, None] == seg[None, :]).astype(BF16)

    qkv, rest, h_b, *gath = _in_proj(xs, norm_pre_g, w_in_p, shards)
    w_uq_p, w_uk_p, w_uv, f_out, f_ple, f_pg = _kernel_weights(gath)
    sb_o = _sb_fwd(qkv, 8)
    qp, kp, vv, cqn_b, ckvn_b = _mla_prep(rest, q_norm_g, kv_norm_g, w_uq_p, w_uk_p, w_uv, cos_t, sin_t)
    mla_o, lse = _mla_fwd(qp, kp, vv, 4)

    (dx1, d_sbo, d_mlo, d_sbg, d_mlg, x1_b, dgl_b, yc_b, dy_b, p_b, du_b, small_mid) = _mid(
        xs, ps, tgt, sb_o, mla_o, rest, sb_out_norm_g, mla_out_norm_g, f_out, norm_post_g,
        f_ple, ple_norm_g, f_pg, b_ple_gate, bd)
    pay_a = [_tn_matmul(yc_b, dy_b, "dw_out").reshape(N_DEV, 128, D_MODEL),
             _tn_matmul(p_b, du_b, "dw_ple", blocked=True),
             _tn_matmul(x1_b, dgl_b, "dw_pg").reshape(N_DEV, 128, D_MODEL)]
    dqp, dkp, dvv, *sib_a = _mla_bwd(qp, kp, vv, d_mlo, mla_o, lse, 4, pay_a)
    pair_a = _pair_sums(pay_a, sib_a, place, "grad_pair_sums_a")
    dq_sb, dk_sb, dv_sb, *landed_a = _sb_bwd(qkv, d_sbo, [sm for sm, _ in pair_a])
    dcq, dckv, dkr, dq_b, dk_b, dv_b, small_prep = _mla_prep_bwd(
        dqp, dkp, dvv, rest, q_norm_g, kv_norm_g, w_uq_p, w_uk_p, w_uv, cos_t, sin_t)
    pieces = [dq_sb, dk_sb, dv_sb, d_sbg, d_mlg, dcq, dckv, dkr]
    d_cols = [_tn_matmul_multi(h_b, pieces[0:2], "dw_in_0"), _tn_matmul_multi(h_b, pieces[2:4], "dw_in_1"),
              _tn_matmul_multi(h_b, pieces[4:8], "dw_in_2")]
    pay_b = [_payload_in(d_cols), _tn_matmul(cqn_b, dq_b, "dw_uq", blocked=True),
             _payload_ukv(_tn_matmul(ckvn_b, dk_b, "dw_uk", blocked=True), _tn_matmul(ckvn_b, dv_b, "dw_uv"))]
    pair_b = _pair_sums(pay_b, _pair_exchange(pay_b, "grad_pair_exchange"), place, "grad_pair_sums_b")
    grad_x, small_in, *landed_b = _in_bwd(xs, norm_pre_g, dx1, pieces, w_in_p, [sm for sm, _ in pair_b])
    reduced = [(own, l2) for (_, own), l2 in zip(pair_b + pair_a, landed_b + landed_a)]
    slab = jnp.concatenate([small_in[0:1], jnp.pad(small_prep[0:2], ((0, 0), (0, D_MODEL - Q_LORA))),
                            small_mid[3:8]], axis=0)
    return grad_x, reduced, slab
```

```python
import jax
import jax.numpy as jnp
from jax import lax
from jax.experimental import pallas as pl
from jax.experimental.pallas import tpu as pltpu

F32 = jnp.float32
BF16 = jnp.bfloat16
MESH = pl.DeviceIdType.MESH

N_DEV = 8
D_MODEL = 1024
N_HEADS = 8
HEAD_DIM = 64
D_GRP = N_HEADS * HEAD_DIM
Q_LORA = 256
KV_LORA = 128
ROPE_DIM = 32
PLE_DIM = 256
CHUNK_SHIFT = 6
ROPE_THETA = 10000.0
EPS = 1e-6
SB_SCALE = HEAD_DIM ** -0.5
MLA_SCALE = (HEAD_DIM + ROPE_DIM) ** -0.5
NEG = -1e30
LOG2_E = 1.4426950408889634
LN_2 = 0.6931471805599453
SB_CUTOFF = 110.0

ADAM_LR = 0.001
ADAM_B1 = 0.9
ADAM_B2 = 0.999
ADAM_EPS = 1e-08
ADAM_WD = 0.01
ADAM_STEP = 10

LANES = 128
TQ = 256
TK = 256
TM = 256
TM_IO = 512
TS_DW = 2048

D_IN_P = 3072

_NT = (((1,), (1,)), ((), ()))
_TN = (((0,), (0,)), ((), ()))


def _params(sem, vmem_mb):
    return pltpu.CompilerParams(dimension_semantics=sem, vmem_limit_bytes=vmem_mb << 20)


def _hbm(*arrays):
    return [pltpu.with_memory_space_constraint(a, pltpu.HBM) for a in arrays]


def _dot(a, b):
    return jnp.dot(a, b, preferred_element_type=F32)


def _dot_nt(a, b):
    return lax.dot_general(a, b, _NT, preferred_element_type=F32)


def _dot_tn(a, b):
    return lax.dot_general(a, b, _TN, preferred_element_type=F32)


def _hl_dot(a, b):
    hi = a.astype(BF16)
    lo = (a - hi.astype(F32)).astype(BF16)
    return _dot(hi, b) + _dot(lo, b)


def _sigmoid(x):
    return 1.0 / (1.0 + jnp.exp(-x))


def _rope_swap(x, lane):
    left = pltpu.roll(x, LANES - 16, axis=1)
    right = pltpu.roll(x, 16, axis=1)
    lo = (lane >= 64) & (lane < 80)
    hi = (lane >= 80) & (lane < 96)
    return jnp.where(lo, left, jnp.where(hi, right, 0.0))


def _two_level_gather(x_refs, out_refs, send_sems, recv_sems, local_sems):
    x, y, c = lax.axis_index("x"), lax.axis_index("y"), lax.axis_index("c")
    me, sibling = (x, y, c), (x, y, 1 - c)
    chips = [(1 - x, y), (x, 1 - y), (1 - x, 1 - y)]
    ops = range(len(x_refs))

    def slot(o, px, py, pc):
        return out_refs[o].at[4 * px + 2 * py + pc]

    def copy(o, k, block, to, src=None):
        return pltpu.make_async_remote_copy(
            src_ref=slot(o, *block) if src is None else src, dst_ref=slot(o, *block),
            send_sem=send_sems.at[o, k], recv_sem=recv_sems.at[o, k],
            device_id=to, device_id_type=MESH)

    def mine():
        return [pltpu.make_async_copy(x_refs[o], slot(o, *me), local_sems.at[o]) for o in ops]

    def first():
        return ([copy(o, 0, me, sibling, src=x_refs[o]) for o in ops]
                + [copy(o, 1 + j, me, (*chip, c), src=x_refs[o]) for j, chip in enumerate(chips) for o in ops])

    def start():
        for cp in mine() + first():
            cp.start()

    def finish():
        passed = []
        for j, chip in enumerate(chips):
            for o in ops:
                copy(o, 1 + j, (*chip, c), me).wait_recv()
                passed.append(copy(o, 4 + j, (*chip, c), sibling))
                passed[-1].start()
        for o in ops:
            copy(o, 0, sibling, me).wait_recv()
        for j, chip in enumerate(chips):
            for o in ops:
                copy(o, 4 + j, (*chip, 1 - c), me).wait_recv()
        for cp in first() + passed:
            cp.wait_send()
        for cp in mine():
            cp.wait()

    return start, finish


def _gather_sems(n_op):
    return [pltpu.SemaphoreType.DMA((n_op, 7)), pltpu.SemaphoreType.DMA((n_op, 7)),
            pltpu.SemaphoreType.DMA((n_op,))]


def _all_gather(shards):
    n_op = len(shards)

    def body(*refs):
        start, finish = _two_level_gather(refs[:n_op], refs[n_op:2 * n_op], *refs[2 * n_op:])
        start()
        finish()

    any_spec = pl.BlockSpec(memory_space=pl.ANY)
    return pl.pallas_call(
        body, name="weight_all_gather",
        out_shape=[jax.ShapeDtypeStruct((N_DEV,) + a.shape, a.dtype) for a in shards],
        in_specs=[any_spec] * n_op, out_specs=[any_spec] * n_op, scratch_shapes=_gather_sems(n_op),
        compiler_params=pltpu.CompilerParams(vmem_limit_bytes=4 << 20),
    )(*shards)


def _pair_copies(g_refs, l_refs, ssem, rsem):
    x, y, c = lax.axis_index("x"), lax.axis_index("y"), lax.axis_index("c")
    copies = []
    for o in range(len(g_refs)):
        for chip in range(4):
            copies.append(pltpu.make_async_remote_copy(
                src_ref=g_refs[o].at[2 * chip + (1 - c)], dst_ref=l_refs[o].at[chip],
                send_sem=ssem.at[o, chip], recv_sem=rsem.at[o, chip],
                device_id=(x, y, 1 - c), device_id_type=MESH))
    return copies


def _pair_specs(pays):
    n_op = len(pays)
    any_spec = pl.BlockSpec(memory_space=pl.ANY)
    return ([any_spec] * n_op, [any_spec] * n_op,
            [jax.ShapeDtypeStruct((4,) + a.shape[1:], F32) for a in pays],
            [pltpu.SemaphoreType.DMA((n_op, 4)), pltpu.SemaphoreType.DMA((n_op, 4))])


def _pair_exchange(pays, name):
    n_op = len(pays)
    in_specs, out_specs, out_shape, sems = _pair_specs(pays)

    def body(*refs):
        copies = _pair_copies(refs[:n_op], refs[n_op:2 * n_op], *refs[2 * n_op:])
        for cp in copies:
            cp.start()
        for cp in copies:
            cp.wait()

    return pl.pallas_call(body, name=name, out_shape=out_shape, in_specs=in_specs, out_specs=out_specs,
                          scratch_shapes=sems,
                          compiler_params=pltpu.CompilerParams(vmem_limit_bytes=4 << 20))(*pays)


def _slab_exchange(small):
    sr, n = small.shape

    def body(s_ref, sland_ref, ssem, rsem, lsem):
        x, y, c = lax.axis_index("x"), lax.axis_index("y"), lax.axis_index("c")
        me = 4 * x + 2 * y + c
        copies = []
        for k in range(1, N_DEV):
            peer = (1 - x if (k >> 2) & 1 else x, 1 - y if (k >> 1) & 1 else y, 1 - c if k & 1 else c)
            copies.append(pltpu.make_async_remote_copy(
                src_ref=s_ref, dst_ref=sland_ref.at[me], send_sem=ssem.at[k], recv_sem=rsem.at[k],
                device_id=peer, device_id_type=MESH))
        own = pltpu.make_async_copy(s_ref, sland_ref.at[me], lsem)
        own.start()
        for cp in copies:
            cp.start()
        for cp in copies:
            cp.wait()
        own.wait()

    any_spec = pl.BlockSpec(memory_space=pl.ANY)
    return pl.pallas_call(
        body, name="grad_slab_exchange", out_shape=jax.ShapeDtypeStruct((N_DEV, sr, n), F32),
        in_specs=[any_spec], out_specs=any_spec,
        scratch_shapes=[pltpu.SemaphoreType.DMA((N_DEV,)), pltpu.SemaphoreType.DMA((N_DEV,)),
                        pltpu.SemaphoreType.DMA],
        compiler_params=pltpu.CompilerParams(vmem_limit_bytes=4 << 20),
    )(small)


def _pair_sums(pays, landed, place, name):
    n = len(pays)
    dims = [p.shape[1:] for p in pays]

    def body(place_ref, *refs):
        g_refs, l_refs, s_refs, own_refs = refs[:n], refs[n:2 * n], refs[2 * n:3 * n], refs[3 * n:]
        i = pl.program_id(0)
        for o in range(n):
            tot = g_refs[o][...] + l_refs[o][...]
            s_refs[o][...] = tot.astype(BF16)

            @pl.when(i == place_ref[1])
            def _(o=o, tot=tot):
                own_refs[o][...] = tot

    grid_spec = pltpu.PrefetchScalarGridSpec(
        num_scalar_prefetch=1, grid=(4,),
        in_specs=[pl.BlockSpec((None, r, c), lambda i, pr: (2 * i + pr[0], 0, 0)) for r, c in dims]
        + [pl.BlockSpec((None, r, c), lambda i, pr: (i, 0, 0)) for r, c in dims],
        out_specs=[pl.BlockSpec((None, r, c), lambda i, pr: (i, 0, 0)) for r, c in dims]
        + [pl.BlockSpec((r, c), lambda i, pr: (0, 0)) for r, c in dims])
    out = pl.pallas_call(
        body, name=name, grid_spec=grid_spec,
        out_shape=[jax.ShapeDtypeStruct((4, r, c), BF16) for r, c in dims]
        + [jax.ShapeDtypeStruct((r, c), F32) for r, c in dims],
        compiler_params=_params(("arbitrary",), 16),
    )(place, *pays, *landed)
    return list(zip(out[:n], out[n:]))


def _chip_copies(s_refs, l_refs, ssem, rsem):
    x, y, c = lax.axis_index("x"), lax.axis_index("y"), lax.axis_index("c")
    copies = []
    for rel in range(1, 4):
        px = 1 - x if rel & 2 else x
        py = 1 - y if rel & 1 else y
        for o in range(len(s_refs)):
            copies.append(pltpu.make_async_remote_copy(
                src_ref=s_refs[o].at[2 * px + py], dst_ref=l_refs[o].at[rel - 1],
                send_sem=ssem.at[o, rel - 1], recv_sem=rsem.at[o, rel - 1],
                device_id=(px, py, c), device_id_type=MESH))
    return copies


def _chip_specs(sums):
    n_op = len(sums)
    any_spec = pl.BlockSpec(memory_space=pl.ANY)
    return ([any_spec] * n_op, [any_spec] * n_op,
            [jax.ShapeDtypeStruct((3,) + a.shape[1:], BF16) for a in sums],
            [pltpu.SemaphoreType.DMA((n_op, 3)), pltpu.SemaphoreType.DMA((n_op, 3))])


def _adamw_math(g, w, m, v):
    mn = ADAM_B1 * m + (1.0 - ADAM_B1) * g
    vn = ADAM_B2 * v + (1.0 - ADAM_B2) * (g * g)
    m_hat = mn / (1.0 - ADAM_B1 ** ADAM_STEP)
    v_hat = vn / (1.0 - ADAM_B2 ** ADAM_STEP)
    return -ADAM_LR * (m_hat / (jnp.sqrt(v_hat) + ADAM_EPS) + ADAM_WD * w), mn, vn


def _adamw_matrix(own, landed, w, m, v, name):
    _, r, c = w.shape
    cp = own.shape[1]
    br = min(r, 256)

    def body(own_ref, l_ref, w_ref, m_ref, v_ref, g_out, d_out, m_out, v_out):
        g = own_ref[...]
        for k in range(3):
            g = g + l_ref[k].astype(F32)
        g = g[:, :c]
        g_out[...] = g
        d_out[...], m_out[...], v_out[...] = _adamw_math(g, w_ref[...], m_ref[...], v_ref[...])

    row = pl.BlockSpec((None, br, c), lambda i: (0, i, 0))
    shp = jax.ShapeDtypeStruct((1, r, c), F32)
    return pl.pallas_call(
        body, name=name, grid=(r // br,),
        in_specs=[pl.BlockSpec((br, cp), lambda i: (i, 0)), pl.BlockSpec((3, br, cp), lambda i: (0, i, 0)),
                  row, row, row],
        out_specs=(row, row, row, row), out_shape=(shp, shp, shp, shp),
        compiler_params=_params(("parallel",), 12),
    )(own, landed, w, m, v)


_VEC_PLACE = ((0, 0), (1, 0), (2, 0), (3, 0), (3, D_GRP), (4, 0), (5, 0), (6, 0))


def _adamw_vectors(sland, ws, ms, vs):
    nv = len(ws)

    def body(l_ref, *refs):
        w_refs, m_refs, v_refs = refs[:nv], refs[nv:2 * nv], refs[2 * nv:3 * nv]
        loss_ref = refs[3 * nv]
        outs = refs[3 * nv + 1:]
        g_all = l_ref[0]
        for j in range(1, N_DEV):
            g_all = g_all + l_ref[j]
        loss_ref[...] = jnp.sum(g_all[7:8, :], axis=1, keepdims=True)
        for k, (row, lane0) in enumerate(_VEC_PLACE):
            n = w_refs[k].shape[1]
            g = g_all[row:row + 1, lane0:lane0 + n]
            d, mn, vn = _adamw_math(g, w_refs[k][...], m_refs[k][...], v_refs[k][...])
            outs[k][...] = g
            outs[nv + k][...] = d
            outs[2 * nv + k][...] = mn
            outs[3 * nv + k][...] = vn

    def whole(shape):
        return pl.BlockSpec(shape, lambda i: (0,) * len(shape))

    shapes = [jax.ShapeDtypeStruct(w.shape, F32) for w in ws]
    return pl.pallas_call(
        body, name="adamw_vectors", grid=(1,),
        in_specs=[whole(sland.shape)] + [whole(w.shape) for w in ws] * 3,
        out_specs=[whole((1, 1))] + [whole(w.shape) for w in ws] * 4,
        out_shape=[jax.ShapeDtypeStruct((1, 1), F32)] + shapes * 4,
        compiler_params=_params(("arbitrary",), 4),
    )(sland, *ws, *ms, *vs)


def _in_proj(x, g, w, shards):
    s = x.shape[0]
    n_op = len(shards)
    steps = s // TM_IO

    def body(x_ref, g_ref, w_ref, *refs):
        shard_refs = refs[:n_op]
        qkv_ref, rest_ref, h_ref = refs[n_op:n_op + 3]
        gath_refs = refs[n_op + 3:2 * n_op + 3]
        start, finish = _two_level_gather(shard_refs, gath_refs, *refs[2 * n_op + 3:])
        i = pl.program_id(0)

        @pl.when(i == 0)
        def _():
            start()

        xv = x_ref[...]
        r = lax.rsqrt(jnp.mean(xv * xv, axis=-1, keepdims=True) + EPS)
        h = ((xv * r) * g_ref[...]).astype(BF16)
        h_ref[...] = h
        qkv_ref[...] = _dot(h, w_ref[:, :1536]).astype(BF16)
        rest_ref[...] = _dot(h, w_ref[:, 1536:])

        @pl.when(i == steps - 1)
        def _():
            finish()

    any_spec = pl.BlockSpec(memory_space=pl.ANY)
    return pl.pallas_call(
        body, name="in_proj", grid=(steps,),
        in_specs=[pl.BlockSpec((TM_IO, D_MODEL), lambda i: (i, 0)),
                  pl.BlockSpec((1, D_MODEL), lambda i: (0, 0)),
                  pl.BlockSpec((D_MODEL, D_IN_P), lambda i: (0, 0))] + [any_spec] * n_op,
        out_specs=[pl.BlockSpec((TM_IO, 1536), lambda i: (i, 0)),
                   pl.BlockSpec((TM_IO, 1536), lambda i: (i, 0)),
                   pl.BlockSpec((TM_IO, D_MODEL), lambda i: (i, 0))] + [any_spec] * n_op,
        out_shape=[pltpu.HBM((s, 1536), BF16), pltpu.HBM((s, 1536), F32),
                   pltpu.HBM((s, D_MODEL), BF16)]
        + [jax.ShapeDtypeStruct((N_DEV,) + a.shape, a.dtype) for a in shards],
        scratch_shapes=_gather_sems(n_op),
        compiler_params=_params(("arbitrary",), 32),
    )(x, g, w, *shards)


def _mla_prep(rest, gq, gkv, wuq, wuk, wuv, cos_t, sin_t):
    s = rest.shape[0]

    def body(cq_ref, ckv_ref, kr_ref, gq_ref, gkv_ref, wuq_ref, wuk_ref, wuv_ref, c_ref, s_ref,
             qp_ref, kp_ref, vv_ref, cqn_ref, ckvn_ref):
        lane = lax.broadcasted_iota(jnp.int32, (1, LANES), 1)
        cos_v, sin_v = c_ref[...], s_ref[...]
        cq = cq_ref[...]
        rq = lax.rsqrt(jnp.mean(cq * cq, axis=-1, keepdims=True) + EPS)
        cqn = ((cq * rq) * gq_ref[...]).astype(BF16)
        cqn_ref[...] = cqn
        q = _dot(cqn, wuq_ref[...])
        ckv = ckv_ref[...]
        rkv = lax.rsqrt(jnp.mean(ckv * ckv, axis=-1, keepdims=True) + EPS)
        ckvn = ((ckv * rkv) * gkv_ref[...]).astype(BF16)
        ckvn_ref[...] = ckvn
        kn = _dot(ckvn, wuk_ref[...])
        vv_ref[...] = _dot(ckvn, wuv_ref[...]).astype(BF16)
        kr = kr_ref[...]
        kr_roped = kr * cos_v + _rope_swap(kr, lane) * sin_v
        for h in range(N_HEADS):
            sl = slice(h * LANES, (h + 1) * LANES)
            qh = q[:, sl]
            qp_ref[:, sl] = (qh * cos_v + _rope_swap(qh, lane) * sin_v).astype(BF16)
            kp_ref[:, sl] = (kn[:, sl] + kr_roped).astype(BF16)

    def row(width, idx):
        return pl.BlockSpec((TM_IO, width), lambda i: (i, idx))

    def full(a):
        return pl.BlockSpec(a.shape, lambda i: (0, 0))

    return pl.pallas_call(
        body, name="mla_prep", grid=(s // TM_IO,),
        in_specs=[row(Q_LORA, 4), row(KV_LORA, 10), row(LANES, 11), full(gq), full(gkv),
                  full(wuq), full(wuk), full(wuv), row(LANES, 0), row(LANES, 0)],
        out_specs=(row(1024, 0), row(1024, 0), row(D_GRP, 0), row(Q_LORA, 0), row(KV_LORA, 0)),
        out_shape=(pltpu.HBM((s, 1024), BF16), pltpu.HBM((s, 1024), BF16),
                   pltpu.HBM((s, D_GRP), BF16), pltpu.HBM((s, Q_LORA), BF16),
                   pltpu.HBM((s, KV_LORA), BF16)),
        compiler_params=_params(("parallel",), 13),
    )(*_hbm(rest, rest, rest), gq, gkv, wuq, wuk, wuv, cos_t, sin_t)


def _sb_live(n, qi, carries):
    top = carries[0]
    for c in carries[1:]:
        top = jnp.maximum(top, c)
    return jnp.logical_and(n < qi, jnp.max(top) > -SB_CUTOFF)


def _sb_fwd(qkv, hb):
    s = qkv.shape[0]

    def body(q_ref, k_ref, v_ref, o_ref, acc):
        qi = pl.program_id(1)
        lane = lax.broadcasted_iota(jnp.int32, (1, LANES), 1)
        is_a = lane < HEAD_DIM
        pair = lambda h: slice((h // 2) * LANES, (h // 2 + 1) * LANES)
        q_h = []
        for h in range(hb):
            qs = q_ref[:, pair(h)] * SB_SCALE
            mine = is_a if h % 2 == 0 else jnp.logical_not(is_a)
            q_h.append(jnp.where(mine, qs, jnp.zeros_like(qs)))
        r_i = lax.broadcasted_iota(jnp.int32, (TQ, TK), 0)
        c_i = lax.broadcasted_iota(jnp.int32, (TQ, TK), 1)
        past = c_i < r_i
        upper = (r_i > c_i).astype(BF16)
        acc[...] = jnp.zeros_like(acc)

        def tile(j, carries, diag):
            ks = pl.ds(pl.multiple_of(j * TK, TK), TK)
            zs = [_dot_nt(q_h[h], k_ref[ks, pair(h)]) for h in range(hb)]
            if diag:
                zs = [jnp.where(past, z, NEG) for z in zs]
            lfs = [-(jnp.maximum(z, 0.0) + jnp.log(1.0 + jnp.exp(-jnp.abs(z)))) for z in zs]
            sufs = [_hl_dot(lfs[h], upper) for h in range(hb)]
            out = []
            for h in range(hb):
                w = jnp.exp(zs[h] + lfs[h] + (sufs[h] + carries[h]))
                acc[h] += _dot(w.astype(BF16), v_ref[ks, pair(h)])
                out.append(carries[h] + jnp.sum(lfs[h], axis=1, keepdims=True))
            return tuple(out)

        zero = jnp.zeros((TQ, 1), F32)
        carries = tile(qi, (zero,) * hb, True)

        def step(st):
            return (st[0] + 1,) + tile(qi - 1 - st[0], st[1:], False)

        lax.while_loop(lambda st: _sb_live(st[0], qi, st[1:]), step, (0,) + carries)
        for pr in range(hb // 2):
            o_ref[:, pr * LANES:(pr + 1) * LANES] = jnp.where(is_a, acc[2 * pr], acc[2 * pr + 1])

    width = hb * HEAD_DIM
    nb = D_GRP // width
    slab = lambda part: pl.BlockSpec((s, width), lambda g, qi: (0, part * nb + g))
    blk = pl.BlockSpec((TQ, width), lambda g, qi: (qi, g))
    return pl.pallas_call(
        body, name="sb_fwd", grid=(nb, s // TQ),
        in_specs=[blk, slab(1), slab(2)], out_specs=blk,
        out_shape=pltpu.HBM((s, D_GRP), F32),
        scratch_shapes=[pltpu.VMEM((hb, TQ, LANES), F32)],
        compiler_params=_params(("arbitrary", "arbitrary"), 28),
    )(*_hbm(qkv, qkv, qkv))


def _sb_bwd(qkv, d_o, sums):
    s = qkv.shape[0]
    nq = s // TQ
    nk = s // TK
    n_op = len(sums)
    ride_in, ride_out, ride_shape, ride_sems = _chip_specs(sums)

    def body(q_ref, k_ref, v_ref, do_ref, *refs):
        s_refs = refs[:n_op]
        dq_ref, dk_ref, dv_ref = refs[n_op:n_op + 3]
        l_refs = refs[n_op + 3:2 * n_op + 3]
        x1s, bts, dqacc, dkacc, dvacc, ssem, rsem = refs[2 * n_op + 3:]
        qi = pl.program_id(1)
        first_step = jnp.logical_and(pl.program_id(0) == 0, qi == 0)
        last_step = jnp.logical_and(pl.program_id(0) == pl.num_programs(0) - 1, qi == nq - 1)

        @pl.when(first_step)
        def _():
            for cp in _chip_copies(s_refs, l_refs, ssem, rsem):
                cp.start()

        lane = lax.broadcasted_iota(jnp.int32, (1, LANES), 1)
        is_a = lane < HEAD_DIM

        @pl.when(qi == 0)
        def _():
            dkacc[...] = jnp.zeros_like(dkacc)
            dvacc[...] = jnp.zeros_like(dvacc)

        qs = q_ref[...] * SB_SCALE
        zq = jnp.zeros_like(qs)
        qs_x = (jnp.where(is_a, qs, zq), jnp.where(is_a, zq, qs))
        dob = do_ref[...].astype(BF16)
        do_x = (jnp.where(is_a, dob, zq), jnp.where(is_a, zq, dob))
        r_i = lax.broadcasted_iota(jnp.int32, (TQ, TK), 0)
        c_i = lax.broadcasted_iota(jnp.int32, (TQ, TK), 1)
        past = c_i < r_i
        upper = (r_i > c_i).astype(BF16)
        upper_incl = (r_i >= c_i).astype(BF16)
        dqacc[...] = jnp.zeros_like(dqacc)
        both = ((0, 0), (0, 1), (1, 0), (1, 1))

        def tiles(n):
            j_hi = qi - 2 * n
            lo_ok = j_hi >= 1
            j_lo = jnp.maximum(j_hi - 1, 0)
            ks = (pl.ds(pl.multiple_of(j_hi * TK, TK), TK), pl.ds(pl.multiple_of(j_lo * TK, TK), TK))
            return j_hi, lo_ok, j_lo, ks

        def sweep(n, carries):
            j_hi, lo_ok, j_lo, ks = tiles(n)
            slot = (j_hi, jnp.where(lo_ok, j_lo, nk))
            valid = (jnp.logical_or(past, j_hi < qi), lo_ok)
            z = {th: jnp.where(valid[th[0]], _dot_nt(qs_x[th[1]], k_ref[ks[th[0]], :]), NEG) for th in both}
            log_b, lf_sum, suf = {}, {}, {}
            for th in both:
                lf = -(jnp.maximum(z[th], 0.0) + jnp.log(1.0 + jnp.exp(-jnp.abs(z[th]))))
                log_b[th] = z[th] + lf
                lf_sum[th] = jnp.sum(lf, axis=1, keepdims=True)
                suf[th] = _hl_dot(lf, upper)
            c, g_in = {}, {}
            for h in range(2):
                c[0, h], g_in[0, h] = carries[2 * h], carries[2 * h + 1]
                c[1, h] = c[0, h] + lf_sum[0, h]
            d_a = {th: _dot_nt(do_x[th[1]], v_ref[ks[th[0]], :]) for th in both}
            a_b, g, g_sum, sg = {}, {}, {}, {}
            for th in both:
                a = jnp.exp(log_b[th] + (suf[th] + c[th]))
                a_b[th] = a.astype(BF16)
                g[th] = a * d_a[th]
                g_sum[th] = jnp.sum(g[th], axis=1, keepdims=True)
                sg[th] = _hl_dot(g[th], upper_incl)
            for h in range(2):
                g_in[1, h] = g_in[0, h] + g_sum[0, h]
            for th in both:
                t, h = th
                beta = jnp.exp(log_b[th])
                x1s[slot[t], h] = g[th] * (1.0 - beta) + beta * (sg[th] + g_in[th])
                bts[slot[t], h] = beta
                dvacc[ks[t], :] += _dot_tn(a_b[th], do_x[h])
            out = []
            for h in range(2):
                out.append(c[1, h] + lf_sum[1, h])
                out.append(g_in[1, h] + g_sum[1, h])
            return tuple(out)

        zero = jnp.zeros((TQ, 1), F32)
        first = sweep(0, (zero, zero, zero, zero))

        def more(st):
            return jnp.logical_and(2 * st[0] <= qi, jnp.max(jnp.maximum(st[1], st[3])) > -SB_CUTOFF)

        swept = lax.while_loop(more, lambda st: (st[0] + 1,) + sweep(st[0], st[1:]), (1,) + first)
        g_tot = (swept[2], swept[4])

        def apply(n, carry):
            j_hi, lo_ok, j_lo, ks = tiles(n)

            def one(j, kslice):
                for h in range(2):
                    dz = (x1s[j, h] - bts[j, h] * g_tot[h]).astype(BF16)
                    dqacc[h] += _dot(dz, k_ref[kslice, :])
                    dkacc[kslice, :] += _dot_tn(dz, qs_x[h])

            one(j_hi, ks[0])

            @pl.when(lo_ok)
            def _():
                one(j_lo, ks[1])

            return carry

        lax.fori_loop(0, swept[0], apply, 0)
        dq_ref[...] = (jnp.where(is_a, dqacc[0], dqacc[1]) * SB_SCALE).astype(BF16)

        @pl.when(qi == nq - 1)
        def _():
            dk_ref[...] = dkacc[...].astype(BF16)
            dv_ref[...] = dvacc[...].astype(BF16)

        @pl.when(last_step)
        def _():
            for cp in _chip_copies(s_refs, l_refs, ssem, rsem):
                cp.wait()

    slab = lambda off: pl.BlockSpec((s, LANES), lambda p, qi: (0, off + p))
    blk = pl.BlockSpec((TQ, LANES), lambda p, qi: (qi, p))
    out_slab = pl.BlockSpec((s, LANES), lambda p, qi: (0, p))
    shp = pltpu.HBM((s, D_GRP), BF16)
    return pl.pallas_call(
        body, name="sb_bwd", grid=(4, nq),
        in_specs=[blk, slab(4), slab(8), blk] + ride_in,
        out_specs=[blk, out_slab, out_slab] + ride_out, out_shape=[shp, shp, shp] + ride_shape,
        scratch_shapes=[pltpu.VMEM((nk + 1, 2, TQ, TK), F32)] * 2
        + [pltpu.VMEM((2, TQ, LANES), F32), pltpu.VMEM((s, LANES), F32), pltpu.VMEM((s, LANES), F32)]
        + ride_sems,
        compiler_params=_params(("arbitrary", "arbitrary"), 44),
    )(*_hbm(qkv, qkv, qkv, d_o), *sums)


def _mla_fwd(qp, kp, vv, hb):
    s = qp.shape[0]
    c2 = MLA_SCALE * LOG2_E

    def body(q_ref, k_ref, v_ref, o_ref, lse_ref, vaug, mrun, mb, acc, zbuf):
        qi = pl.program_id(1)
        lane = lax.broadcasted_iota(jnp.int32, (1, LANES), 1)
        is_a = lane < HEAD_DIM

        @pl.when(qi == 0)
        def _():
            for h in range(hb):
                vp = v_ref[:, (h // 2) * LANES:(h // 2 + 1) * LANES]
                mine = is_a if h % 2 == 0 else jnp.logical_not(is_a)
                vaug[h] = jnp.where(mine, vp, jnp.ones_like(vp))

        r_i = lax.broadcasted_iota(jnp.int32, (TQ, TK), 0)
        c_i = lax.broadcasted_iota(jnp.int32, (TQ, TK), 1)
        visible = (c_i >> CHUNK_SHIFT) <= (r_i >> CHUNK_SHIFT)

        def key_rows(j):
            return pl.ds(pl.multiple_of(j * TK, TK), TK)

        def sweep(tiles):
            def loop(n, carry):
                tiles(((2 * n, False), (2 * n + 1, False)))
                return carry

            lax.fori_loop(0, qi // 2, loop, 0)

            @pl.when(qi % 2 == 1)
            def _():
                tiles(((qi - 1, False), (qi, True)))

            @pl.when(qi % 2 == 0)
            def _():
                tiles(((qi, True),))

        mrun[...] = jnp.full_like(mrun, NEG)

        def tiles_max(js):
            zs = [[_dot_nt(q_ref[:, h * LANES:(h + 1) * LANES], k_ref[key_rows(j), h * LANES:(h + 1) * LANES])
                   for h in range(hb)] for j, _ in js]
            for t, (j, diag) in enumerate(js):
                for h in range(hb):
                    z = jnp.where(visible, zs[t][h], NEG) if diag else zs[t][h]
                    zbuf[j, h] = z
                    mrun[h] = jnp.maximum(mrun[h], z)

        sweep(tiles_max)
        for h in range(hb):
            m = jnp.max(mrun[h], axis=1, keepdims=True) * c2
            mb[h] = jnp.broadcast_to(m, (TQ, TK))
        acc[...] = jnp.zeros_like(acc)

        def tiles_pv(js):
            ps = [[jnp.exp2((zbuf[j, h] * c2 - mb[h]).astype(BF16)) for h in range(hb)] for j, _ in js]
            for t, (j, _) in enumerate(js):
                for h in range(hb):
                    acc[h] += _dot(ps[t][h], vaug[h, key_rows(j), :])

        sweep(tiles_pv)
        for pr in range(hb // 2):
            a, b = 2 * pr, 2 * pr + 1
            psl = slice(pr * LANES, (pr + 1) * LANES)
            acc_a, acc_b = acc[a], acc[b]
            l_a = pltpu.roll(acc_a, HEAD_DIM, axis=1)
            l_b = pltpu.roll(acc_b, HEAD_DIM, axis=1)
            o_ref[:, psl] = jnp.where(is_a, acc_a * (1.0 / l_a), acc_b * (1.0 / l_b))
            lse_ref[:, psl] = jnp.where(is_a, mb[a, :, :LANES] * LN_2 + jnp.log(l_a),
                                        mb[b, :, :LANES] * LN_2 + jnp.log(l_b))

    blk = pl.BlockSpec((TQ, hb * HEAD_DIM), lambda g, qi: (qi, g))
    shp = pltpu.HBM((s, D_GRP), F32)
    return pl.pallas_call(
        body, name="mla_fwd", grid=(N_HEADS // hb, s // TQ),
        in_specs=[pl.BlockSpec((TQ, hb * LANES), lambda g, qi: (qi, g)),
                  pl.BlockSpec((s, hb * LANES), lambda g, qi: (0, g)),
                  pl.BlockSpec((s, hb * HEAD_DIM), lambda g, qi: (0, g))],
        out_specs=(blk, blk), out_shape=(shp, shp),
        scratch_shapes=[pltpu.VMEM((hb, s, LANES), BF16), pltpu.VMEM((hb, TQ, TK), F32),
                        pltpu.VMEM((hb, TQ, TK), F32), pltpu.VMEM((hb, TQ, LANES), F32),
                        pltpu.VMEM((s // TK, hb, TQ, TK), F32)],
        compiler_params=_params(("arbitrary", "arbitrary"), 44),
    )(*_hbm(qp, kp, vv))


def _mla_bwd(qp, kp, vv, d_o, o, lse, hb, pays):
    s = qp.shape[0]
    nq = s // TQ
    c2 = MLA_SCALE * LOG2_E
    n_op = len(pays)
    ride_in, ride_out, ride_shape, ride_sems = _pair_specs(pays)

    def body(q_ref, k_ref, v_ref, do_ref, o_ref, lse_ref, *refs):
        g_refs = refs[:n_op]
        dq_ref, dk_ref, dv_ref = refs[n_op:n_op + 3]
        l_refs = refs[n_op + 3:2 * n_op + 3]
        dqacc, lse_b, delta_b, q_t, do_t, ssem, rsem = refs[2 * n_op + 3:]
        qi = pl.program_id(1)

        @pl.when(jnp.logical_and(pl.program_id(0) == 0, qi == 0))
        def _():
            for cp in _pair_copies(g_refs, l_refs, ssem, rsem):
                cp.start()

        lane = lax.broadcasted_iota(jnp.int32, (1, LANES), 1)
        is_a = lane < HEAD_DIM

        @pl.when(qi == 0)
        def _():
            dk_ref[...] = jnp.zeros_like(dk_ref)
            dv_ref[...] = jnp.zeros_like(dv_ref)

        r_i = lax.broadcasted_iota(jnp.int32, (TQ, TK), 0)
        c_i = lax.broadcasted_iota(jnp.int32, (TQ, TK), 1)
        visible = (c_i >> CHUNK_SHIFT) <= (r_i >> CHUNK_SHIFT)
        do_x = []
        for h in range(hb):
            psl = slice((h // 2) * LANES, (h // 2 + 1) * LANES)
            mine = is_a if h % 2 == 0 else jnp.logical_not(is_a)
            d_o = do_ref[:, psl]
            delta = jnp.sum(jnp.where(mine, d_o * o_ref[:, psl], 0.0), axis=1, keepdims=True)
            lse_h = jnp.sum(jnp.where(lane == (h % 2) * HEAD_DIM, lse_ref[:, psl], 0.0), axis=1, keepdims=True)
            lse_b[h] = jnp.broadcast_to(lse_h * LOG2_E, (TQ, TK))
            delta_b[h] = jnp.broadcast_to(delta, (TQ, TK))
            do_h = jnp.where(mine, d_o, 0.0)
            do_x.append(do_h.astype(BF16))
            do_t[h] = do_h.T.astype(BF16)
            q_t[h] = q_ref[:, h * LANES:(h + 1) * LANES].astype(F32).T.astype(BF16)
        dqacc[...] = jnp.zeros_like(dqacc)

        head = lambda h: slice(h * LANES, (h + 1) * LANES)
        pair = lambda h: slice((h // 2) * LANES, (h // 2 + 1) * LANES)

        def tiles(js):
            th = [(j, diag, pl.ds(pl.multiple_of(j * TK, TK), TK), h) for j, diag in js for h in range(hb)]
            zs = [_dot_nt(q_ref[:, head(h)], k_ref[ks, head(h)]) for _, _, ks, h in th]
            dps = [_dot_nt(do_x[h], v_ref[ks, pair(h)]) for _, _, ks, h in th]
            for i, (j, diag, ks, h) in enumerate(th):
                e = zs[i] * c2 - lse_b[h]
                if diag:
                    e = jnp.where(visible, e, NEG)
                p = jnp.exp2(e)
                ds = (p * (dps[i] - delta_b[h]) * MLA_SCALE).astype(BF16)
                dqacc[h] += _dot(ds, k_ref[ks, head(h)])
                dk_ref[head(h), ks] += _dot(q_t[h], ds)
                dv_ref[pair(h), ks] += _dot(do_t[h], p.astype(BF16))

        def loop(n, c):
            tiles(((2 * n, False), (2 * n + 1, False)))
            return c

        lax.fori_loop(0, qi // 2, loop, 0)

        @pl.when(qi % 2 == 1)
        def _():
            tiles(((qi - 1, False), (qi, True)))

        @pl.when(qi % 2 == 0)
        def _():
            tiles(((qi, True),))

        for h in range(hb):
            dq_ref[:, h * LANES:(h + 1) * LANES] = dqacc[h]

        @pl.when(jnp.logical_and(pl.program_id(0) == pl.num_programs(0) - 1, qi == nq - 1))
        def _():
            for cp in _pair_copies(g_refs, l_refs, ssem, rsem):
                cp.wait()

    blk = pl.BlockSpec((TQ, hb * HEAD_DIM), lambda g, qi: (qi, g))
    return pl.pallas_call(
        body, name="mla_bwd", grid=(N_HEADS // hb, nq),
        in_specs=[pl.BlockSpec((TQ, hb * LANES), lambda g, qi: (qi, g)),
                  pl.BlockSpec((s, hb * LANES), lambda g, qi: (0, g)),
                  pl.BlockSpec((s, hb * HEAD_DIM), lambda g, qi: (0, g)), blk, blk, blk] + ride_in,
        out_specs=[pl.BlockSpec((TQ, hb * LANES), lambda g, qi: (qi, g)),
                   pl.BlockSpec((hb * LANES, s), lambda g, qi: (g, 0)),
                   pl.BlockSpec((hb * HEAD_DIM, s), lambda g, qi: (g, 0))] + ride_out,
        out_shape=[pltpu.HBM((s, 1024), F32), pltpu.HBM((1024, s), F32),
                   pltpu.HBM((D_GRP, s), F32)] + ride_shape,
        scratch_shapes=[pltpu.VMEM((hb, TQ, LANES), F32), pltpu.VMEM((hb, TQ, TK), F32),
                        pltpu.VMEM((hb, TQ, TK), F32), pltpu.VMEM((hb, LANES, TQ), BF16),
                        pltpu.VMEM((hb, LANES, TQ), BF16)] + ride_sems,
        compiler_params=_params(("arbitrary", "arbitrary"), 52),
    )(*_hbm(qp, kp, vv, d_o, o, lse), *pays)


def _mid(x, p, target, sb_o, mla_o, rest, g_sb, g_mla, w_out, g_post, w_ple, g_ple, w_pg, b_pg, bd):
    s = x.shape[0]

    def body(x_ref, p_ref, t_ref, sbo_ref, mlo_ref, sbg_ref, mlg_ref, gsb_ref, gml_ref, wout_ref,
             gpost_ref, wple_ref, gple_ref, wpg_ref, bpg_ref, bd_ref,
             dx1_ref, dsbo_ref, dmlo_ref, dsbg_ref, dmlg_ref, x1b_ref, dglb_ref, ycb_ref, dyb_ref,
             pb_ref, dub_ref, small_ref):
        i = pl.program_id(0)
        bd_m = bd_ref[...]

        def seg_mean(v):
            return _dot(v.astype(BF16), bd_m) * (1.0 / HEAD_DIM)

        groups = []
        for o_ref, gate_ref, gain_ref in ((sbo_ref, sbg_ref, gsb_ref), (mlo_ref, mlg_ref, gml_ref)):
            o = o_ref[...]
            r = lax.rsqrt(seg_mean(o * o) + EPS)
            n = o * r
            hn = n * gain_ref[...]
            gate = gate_ref[...]
            sg = _sigmoid(gate)
            si = gate * sg
            groups.append((r, n, hn, gate, sg, si, gain_ref[...]))
        ya = (groups[0][2] * groups[0][5]).astype(BF16)
        yb = (groups[1][2] * groups[1][5]).astype(BF16)
        ycb_ref[:, :D_GRP] = ya
        ycb_ref[:, D_GRP:] = yb
        y = _dot(ya, wout_ref[:D_GRP, :]) + _dot(yb, wout_ref[D_GRP:, :])
        ry = lax.rsqrt(jnp.mean(y * y, axis=-1, keepdims=True) + EPS)
        ny = y * ry
        x1 = x_ref[...] + ny * gpost_ref[...]
        x1b = x1.astype(BF16)
        x1b_ref[...] = x1b
        pb = p_ref[...].astype(BF16)
        pb_ref[...] = pb
        u = _dot(pb, wple_ref[...])
        ru = lax.rsqrt(jnp.mean(u * u, axis=-1, keepdims=True) + EPS)
        nu = u * ru
        ple = nu * gple_ref[...]
        gate = _sigmoid(_dot(x1b, wpg_ref[...]) + bpg_ref[...])
        x2 = x1 + ple * gate
        diff = x2 - t_ref[...]
        dx2 = diff * (1.0 / D_MODEL)

        d_ple = dx2 * gate
        d_glin = (dx2 * ple) * (gate * (1.0 - gate))
        dglb = d_glin.astype(BF16)
        dglb_ref[...] = dglb
        dx1 = dx2 + _dot_nt(dglb, wpg_ref[...])
        dx1_ref[...] = dx1
        d_nu = d_ple * gple_ref[...]
        d_u = ru * (d_nu - nu * jnp.mean(d_nu * nu, axis=-1, keepdims=True))
        dub_ref[...] = d_u.astype(BF16)
        d_ny = dx1 * gpost_ref[...]
        d_y = ry * (d_ny - ny * jnp.mean(d_ny * ny, axis=-1, keepdims=True))
        dyb = d_y.astype(BF16)
        dyb_ref[...] = dyb
        d_yc = (_dot_nt(dyb, wout_ref[:D_GRP, :]), _dot_nt(dyb, wout_ref[D_GRP:, :]))

        d_gain = []
        for gx, (do_ref, dg_ref) in enumerate(((dsbo_ref, dsbg_ref), (dmlo_ref, dmlg_ref))):
            r, n, hn, gate_g, sg, si, gain = groups[gx]
            dyg = d_yc[gx]
            d_hn = dyg * si
            dg_ref[...] = (dyg * hn * (sg * (1.0 + gate_g * (1.0 - sg)))).astype(BF16)
            d_gain.append(jnp.sum(d_hn * n, axis=0, keepdims=True))
            d_n = d_hn * gain
            do_ref[...] = r * (d_n - n * seg_mean(d_n * n))

        @pl.when(i == 0)
        def _():
            small_ref[...] = jnp.zeros_like(small_ref)

        small_ref[3:4, :D_GRP] += d_gain[0]
        small_ref[3:4, D_GRP:] += d_gain[1]
        small_ref[4:5, :] += jnp.sum(dx1 * ny, axis=0, keepdims=True)
        small_ref[5:6, :] += jnp.sum(d_ple * nu, axis=0, keepdims=True)
        small_ref[6:7, :] += jnp.sum(d_glin, axis=0, keepdims=True)
        small_ref[7:8, :] += jnp.sum(diff * diff, axis=0, keepdims=True) * (0.5 / D_MODEL)

    def row(width, idx=0):
        return pl.BlockSpec((TM, width), lambda i: (i, idx))

    def full(a):
        return pl.BlockSpec(a.shape, lambda i: (0, 0))

    f32 = lambda w: pltpu.HBM((s, w), F32)
    b16 = lambda w: pltpu.HBM((s, w), BF16)
    return pl.pallas_call(
        body, name="mid", grid=(s // TM,),
        in_specs=[row(D_MODEL), row(PLE_DIM), row(D_MODEL), row(D_GRP), row(D_GRP),
                  row(D_GRP, 0), row(D_GRP, 1), full(g_sb), full(g_mla), full(w_out), full(g_post),
                  full(w_ple), full(g_ple), full(w_pg), full(b_pg), full(bd)],
        out_specs=(row(D_MODEL), row(D_GRP), row(D_GRP), row(D_GRP), row(D_GRP), row(D_MODEL),
                   row(D_MODEL), row(D_MODEL), row(D_MODEL), row(PLE_DIM), row(D_MODEL),
                   pl.BlockSpec((8, D_MODEL), lambda i: (0, 0))),
        out_shape=(f32(D_MODEL), f32(D_GRP), f32(D_GRP), b16(D_GRP), b16(D_GRP), b16(D_MODEL),
                   b16(D_MODEL), b16(D_MODEL), b16(D_MODEL), b16(PLE_DIM), b16(D_MODEL),
                   jax.ShapeDtypeStruct((8, D_MODEL), F32)),
        compiler_params=_params(("arbitrary",), 46),
    )(*_hbm(x, p, target, sb_o, mla_o, rest, rest), g_sb, g_mla, w_out, g_post, w_ple, g_ple, w_pg, b_pg, bd)


def _mla_prep_bwd(dqp, dkp, dvv, rest, gq, gkv, wuq, wuk, wuv, cos_t, sin_t):
    s = rest.shape[0]

    def body(dqp_ref, dkp_ref, dvv_ref, cq_ref, ckv_ref, gq_ref, gkv_ref, wuq_ref, wuk_ref, wuv_ref,
             c_ref, s_ref, dcq_ref, dckv_ref, dkr_ref, dqb_ref, dkb_ref, dvb_ref, small_ref):
        i = pl.program_id(0)
        lane = lax.broadcasted_iota(jnp.int32, (1, LANES), 1)
        in_rope = (lane >= HEAD_DIM) & (lane < HEAD_DIM + ROPE_DIM)
        cos_v, sin_v = c_ref[...], s_ref[...]
        dkr_roped = jnp.zeros((TM_IO, LANES), F32)
        for h in range(N_HEADS):
            sl = slice(h * LANES, (h + 1) * LANES)
            dy = dqp_ref[:, sl]
            dqb_ref[:, sl] = (dy * cos_v + _rope_swap(dy * sin_v, lane)).astype(BF16)
            dkh = dkp_ref[sl, :].T
            dkb_ref[:, sl] = dkh.astype(BF16)
            dkr_roped = dkr_roped + jnp.where(in_rope, dkh, 0.0)
        dkr_ref[...] = (dkr_roped * cos_v + _rope_swap(dkr_roped * sin_v, lane)).astype(BF16)
        dvb = dvv_ref[...].T.astype(BF16)
        dvb_ref[...] = dvb

        cq = cq_ref[...]
        rq = lax.rsqrt(jnp.mean(cq * cq, axis=-1, keepdims=True) + EPS)
        nq_ = cq * rq
        d_cqn = _dot_nt(dqb_ref[...], wuq_ref[...])
        d_n = d_cqn * gq_ref[...]
        dcq_ref[...] = (rq * (d_n - nq_ * jnp.mean(d_n * nq_, axis=-1, keepdims=True))).astype(BF16)

        ckv = ckv_ref[...]
        rkv = lax.rsqrt(jnp.mean(ckv * ckv, axis=-1, keepdims=True) + EPS)
        nkv = ckv * rkv
        d_ckvn = _dot_nt(dkb_ref[...], wuk_ref[...]) + _dot_nt(dvb, wuv_ref[...])
        d_n2 = d_ckvn * gkv_ref[...]
        dckv_ref[...] = (rkv * (d_n2 - nkv * jnp.mean(d_n2 * nkv, axis=-1, keepdims=True))).astype(BF16)

        @pl.when(i == 0)
        def _():
            small_ref[...] = jnp.zeros_like(small_ref)

        small_ref[0:1, :] += jnp.sum(d_cqn * nq_, axis=0, keepdims=True)
        small_ref[1:2, :KV_LORA] += jnp.sum(d_ckvn * nkv, axis=0, keepdims=True)

    def row(width, idx=0):
        return pl.BlockSpec((TM_IO, width), lambda i: (i, idx))

    def full(a):
        return pl.BlockSpec(a.shape, lambda i: (0, 0))

    b16 = lambda w: pltpu.HBM((s, w), BF16)
    return pl.pallas_call(
        body, name="mla_prep_bwd", grid=(s // TM_IO,),
        in_specs=[row(1024), pl.BlockSpec((1024, TM_IO), lambda i: (0, i)), pl.BlockSpec((D_GRP, TM_IO), lambda i: (0, i)),
                  row(Q_LORA, 4), row(KV_LORA, 10), full(gq), full(gkv),
                  full(wuq), full(wuk), full(wuv), row(LANES), row(LANES)],
        out_specs=(row(Q_LORA), row(KV_LORA), row(LANES), row(1024), row(1024), row(D_GRP),
                   pl.BlockSpec((8, Q_LORA), lambda i: (0, 0))),
        out_shape=(b16(Q_LORA), b16(KV_LORA), b16(LANES), b16(1024), b16(1024), b16(D_GRP),
                   jax.ShapeDtypeStruct((8, Q_LORA), F32)),
        compiler_params=_params(("arbitrary",), 24),
    )(*_hbm(dqp, dkp, dvv, rest, rest), gq, gkv, wuq, wuk, wuv, cos_t, sin_t)


def _in_bwd(x, g, dx1, pieces, w, sums):
    s = x.shape[0]
    steps = s // TM_IO
    widths = [a.shape[1] for a in pieces]
    offs = [sum(widths[:k]) for k in range(len(widths))]
    n_pc, n_op = len(pieces), len(sums)
    ride_in, ride_out, ride_shape, ride_sems = _chip_specs(sums)

    def body(x_ref, g_ref, dx1_ref, *refs):
        piece_refs = refs[:n_pc]
        w_ref = refs[n_pc]
        s_refs = refs[n_pc + 1:n_pc + 1 + n_op]
        dx_ref, small_ref = refs[n_pc + 1 + n_op:n_pc + 3 + n_op]
        l_refs = refs[n_pc + 3 + n_op:n_pc + 3 + 2 * n_op]
        ssem, rsem = refs[n_pc + 3 + 2 * n_op:]
        i = pl.program_id(0)

        @pl.when(i == 0)
        def _():
            for cp in _chip_copies(s_refs, l_refs, ssem, rsem):
                cp.start()

        dh = jnp.zeros((TM_IO, D_MODEL), F32)
        for pr, off, wd in zip(piece_refs, offs, widths):
            dh = dh + _dot_nt(pr[...], w_ref[:, off:off + wd])
        xv = x_ref[...]
        r = lax.rsqrt(jnp.mean(xv * xv, axis=-1, keepdims=True) + EPS)
        n = xv * r
        d_n = dh * g_ref[...]
        dx_ref[...] = dx1_ref[...] + r * (d_n - n * jnp.mean(d_n * n, axis=-1, keepdims=True))

        @pl.when(i == 0)
        def _():
            small_ref[...] = jnp.zeros_like(small_ref)

        small_ref[0:1, :] += jnp.sum(dh * n, axis=0, keepdims=True)

        @pl.when(i == steps - 1)
        def _():
            for cp in _chip_copies(s_refs, l_refs, ssem, rsem):
                cp.wait()

    def row(width):
        return pl.BlockSpec((TM_IO, width), lambda i: (i, 0))

    return pl.pallas_call(
        body, name="in_bwd", grid=(steps,),
        in_specs=[row(D_MODEL), pl.BlockSpec((1, D_MODEL), lambda i: (0, 0)), row(D_MODEL)]
        + [row(wd) for wd in widths] + [pl.BlockSpec(w.shape, lambda i: (0, 0))] + ride_in,
        out_specs=[row(D_MODEL), pl.BlockSpec((8, D_MODEL), lambda i: (0, 0))] + ride_out,
        out_shape=[pltpu.HBM((s, D_MODEL), F32), jax.ShapeDtypeStruct((8, D_MODEL), F32)]
        + ride_shape,
        scratch_shapes=ride_sems,
        compiler_params=_params(("arbitrary",), 40),
    )(*_hbm(x), g, *_hbm(dx1, *pieces), w, *sums)


def _tn_matmul(a, b, name, blocked=False):
    s, k = a.shape
    n = b.shape[1]
    ts = min(s, TS_DW)
    tn = n if blocked else min(n, 512)
    steps = s // ts

    def body(a_ref, b_ref, o_ref):
        t = pl.program_id(1)

        @pl.when(t == 0)
        def _():
            o_ref[...] = jnp.zeros_like(o_ref)

        prod = _dot_tn(a_ref[...], b_ref[...])
        if blocked:
            for j in range(n // LANES):
                o_ref[j] += prod[:, j * LANES:(j + 1) * LANES]
        else:
            o_ref[...] += prod

    if blocked:
        out_spec = pl.BlockSpec((n // LANES, k, LANES), lambda j, t: (0, 0, 0))
        out_shape = jax.ShapeDtypeStruct((n // LANES, k, LANES), F32)
    else:
        out_spec = pl.BlockSpec((k, tn), lambda j, t: (0, j))
        out_shape = jax.ShapeDtypeStruct((k, n), F32)
    return pl.pallas_call(
        body, name=name, grid=(n // tn, steps),
        in_specs=[pl.BlockSpec((ts, k), lambda j, t: (t, 0)), pl.BlockSpec((ts, tn), lambda j, t: (t, j))],
        out_specs=out_spec, out_shape=out_shape,
        compiler_params=_params(("parallel", "arbitrary"), 20),
    )(*_hbm(a, b))


def _tn_matmul_multi(a, bs, name):
    s, k = a.shape
    widths = [b.shape[1] for b in bs]
    ts = min(s, TS_DW)

    def body(a_ref, *refs):
        b_refs, o_ref = refs[:-1], refs[-1]
        t = pl.program_id(0)

        @pl.when(t == 0)
        def _():
            o_ref[...] = jnp.zeros_like(o_ref)

        av = a_ref[...]
        off = 0
        for b_ref, wd in zip(b_refs, widths):
            o_ref[:, off:off + wd] += _dot_tn(av, b_ref[...])
            off += wd

    return pl.pallas_call(
        body, name=name, grid=(s // ts,),
        in_specs=[pl.BlockSpec((ts, k), lambda t: (t, 0))] + [pl.BlockSpec((ts, wd), lambda t: (t, 0)) for wd in widths],
        out_specs=pl.BlockSpec((k, sum(widths)), lambda t: (0, 0)),
        out_shape=jax.ShapeDtypeStruct((k, sum(widths)), F32),
        compiler_params=_params(("arbitrary",), 30),
    )(*_hbm(a, *bs))


IN_SHARD = 372
_IN_KERNEL_ORDER = ((0, 2048), (2464, 2976), (2048, 2432))
_IN_ROPE = (2432, 2464)
_IN_GRAD_SRC = ((0, 512, 0, 0), (512, 1024, 0, 512), (1024, 1536, 1, 0), (1536, 2048, 1, 512),
                (2048, 2304, 2, 512), (2304, 2432, 2, 768), (2432, 2464, 2, 960), (2464, 2976, 2, 0))


def _shard_cols(gath_in, lo, hi):
    out = []
    while lo < hi:
        j, a = divmod(lo, IN_SHARD)
        b = min(IN_SHARD, a + hi - lo)
        out.append(gath_in[j][:, a:b])
        lo += b - a
    return out


def _kernel_w_in(g_in):
    zc = lambda n: jnp.zeros((D_MODEL, n), BF16)
    parts = [pc for lo, hi in _IN_KERNEL_ORDER for pc in _shard_cols(g_in, lo, hi)]
    parts += [zc(64)] + _shard_cols(g_in, *_IN_ROPE) + [zc(32)]
    return jnp.concatenate(parts, axis=1)


def _kernel_weights(gath):
    g_uq, g_ukv, g_out, g_ple, g_pg = gath
    w_uq_p = jnp.pad(g_uq, ((0, 0), (0, 0), (0, 32))).transpose(1, 0, 2).reshape(Q_LORA, 1024)
    k_only = jnp.where(jnp.arange(LANES) < HEAD_DIM, g_ukv, jnp.zeros_like(g_ukv))
    w_uk_p = k_only.transpose(1, 0, 2).reshape(KV_LORA, 1024)
    w_uv = g_ukv[:, :, HEAD_DIM:].transpose(1, 0, 2).reshape(KV_LORA, D_GRP)
    w_ple = g_ple.transpose(1, 0, 2).reshape(PLE_DIM, D_MODEL)
    return (w_uq_p, w_uk_p, w_uv, g_out.reshape(D_MODEL, D_MODEL), w_ple, g_pg.reshape(D_MODEL, D_MODEL))


def _payload_in(d_cols):
    blocks = []
    for j in range(N_DEV):
        lo, hi = j * IN_SHARD, (j + 1) * IN_SHARD
        parts = []
        for o_lo, o_hi, idx, off in _IN_GRAD_SRC:
            a, b = max(lo, o_lo), min(hi, o_hi)
            if a < b:
                parts.append(d_cols[idx][:, off + a - o_lo:off + b - o_lo])
        blocks.append(jnp.concatenate(parts, axis=1))
    return jnp.stack(blocks)


def _payload_ukv(duk_blk, d_uv):
    dv_blk = d_uv.reshape(KV_LORA, N_HEADS, HEAD_DIM).transpose(1, 0, 2)
    return jnp.concatenate([duk_blk[:, :, :HEAD_DIM], dv_blk], axis=2)


def kernel(x, p, positions, norm_pre_g, w_in, q_norm_g, w_uq, kv_norm_g, w_ukv, sb_out_norm_g, mla_out_norm_g, w_out, norm_post_g, w_ple, ple_norm_g, w_ple_gate, b_ple_gate, loss_target, m_norm_pre_g, m_w_in, m_q_norm_g, m_w_uq, m_kv_norm_g, m_w_ukv, m_sb_out_norm_g, m_mla_out_norm_g, m_w_out, m_norm_post_g, m_w_ple, m_ple_norm_g, m_w_ple_gate, m_b_ple_gate, v_norm_pre_g, v_w_in, v_q_norm_g, v_w_uq, v_kv_norm_g, v_w_ukv, v_sb_out_norm_g, v_mla_out_norm_g, v_w_out, v_norm_post_g, v_w_ple, v_ple_norm_g, v_w_ple_gate, v_b_ple_gate):
    mats = (w_in, w_uq, w_ukv, w_out, w_ple, w_ple_gate)
    m_mats = (m_w_in, m_w_uq, m_w_ukv, m_w_out, m_w_ple, m_w_ple_gate)
    v_mats = (v_w_in, v_w_uq, v_w_ukv, v_w_out, v_w_ple, v_w_ple_gate)
    vecs = (norm_pre_g, q_norm_g, kv_norm_g, sb_out_norm_g, mla_out_norm_g, norm_post_g, ple_norm_g, b_ple_gate)
    m_vecs = (m_norm_pre_g, m_q_norm_g, m_kv_norm_g, m_sb_out_norm_g, m_mla_out_norm_g, m_norm_post_g,
              m_ple_norm_g, m_b_ple_gate)
    v_vecs = (v_norm_pre_g, v_q_norm_g, v_kv_norm_g, v_sb_out_norm_g, v_mla_out_norm_g, v_norm_post_g,
              v_ple_norm_g, v_b_ple_gate)

    shards = [a[0].astype(BF16) for a in mats]
    w_in_p = _kernel_w_in(_all_gather(shards[:1])[0])
    grad_x, reduced, vec_slab = _step(x[0], p[0, 0], positions[0], loss_target[0], *vecs, w_in_p, shards[1:])
    upd = [_adamw_matrix(own, l2, w, m, v, "adamw_%d" % o)
           for o, ((own, l2), w, m, v) in enumerate(zip(reduced, mats, m_mats, v_mats))]
    sm = _adamw_vectors(_slab_exchange(vec_slab), vecs, m_vecs, v_vecs)

    outs = []
    for kind in range(4):
        mat = [upd[o][kind] for o in range(len(mats))]
        vec = sm[1 + 8 * kind:9 + 8 * kind]
        outs += [vec[0], mat[0], vec[1], mat[1], vec[2], mat[2], vec[3], vec[4], mat[3], vec[5],
                 mat[4], vec[6], mat[5], vec[7]]
    return (sm[0][0, 0], grad_x[None], *outs)


def _step(xs, ps, pos, tgt, norm_pre_g, q_norm_g, kv_norm_g, sb_out_norm_g, mla_out_norm_g,
          norm_post_g, ple_norm_g, b_ple_gate, w_in_p, shards):
    s = xs.shape[0]
    place = jnp.stack([lax.axis_index("c"), 2 * lax.axis_index("x") + lax.axis_index("y")]).astype(jnp.int32)

    half = ROPE_DIM // 2
    freq = ROPE_THETA ** (-jnp.arange(half, dtype=F32) / half)
    ang = pos.astype(F32)[:, None] * freq
    cos, sin = jnp.cos(ang), jnp.sin(ang)
    cos_t = jnp.concatenate([jnp.ones((s, 64), F32), cos, cos, jnp.zeros((s, 32), F32)], axis=1)
    sin_t = jnp.concatenate([jnp.zeros((s, 64), F32), -sin, sin, jnp.zeros((s, 32), F32)], axis=1)
    seg = jnp.arange(D_GRP) // HEAD_DIM
    bd = (seg[:, None] == seg[None, :]).astype(BF16)

    qkv, rest, h_b, *gath = _in_proj(xs, norm_pre_g, w_in_p, shards)
    w_uq_p, w_uk_p, w_uv, f_out, f_ple, f_pg = _kernel_weights(gath)
    sb_o = _sb_fwd(qkv, 8)
    qp, kp, vv, cqn_b, ckvn_b = _mla_prep(rest, q_norm_g, kv_norm_g, w_uq_p, w_uk_p, w_uv, cos_t, sin_t)
    mla_o, lse = _mla_fwd(qp, kp, vv, 4)

    (dx1, d_sbo, d_mlo, d_sbg, d_mlg, x1_b, dgl_b, yc_b, dy_b, p_b, du_b, small_mid) = _mid(
        xs, ps, tgt, sb_o, mla_o, rest, sb_out_norm_g, mla_out_norm_g, f_out, norm_post_g,
        f_ple, ple_norm_g, f_pg, b_ple_gate, bd)
    pay_a = [_tn_matmul(yc_b, dy_b, "dw_out").reshape(N_DEV, 128, D_MODEL),
             _tn_matmul(p_b, du_b, "dw_ple", blocked=True),
             _tn_matmul(x1_b, dgl_b, "dw_pg").reshape(N_DEV, 128, D_MODEL)]
    dqp, dkp, dvv, *sib_a = _mla_bwd(qp, kp, vv, d_mlo, mla_o, lse, 4, pay_a)
    pair_a = _pair_sums(pay_a, sib_a, place, "grad_pair_sums_a")
    dq_sb, dk_sb, dv_sb, *landed_a = _sb_bwd(qkv, d_sbo, [sm for sm, _ in pair_a])
    dcq, dckv, dkr, dq_b, dk_b, dv_b, small_prep = _mla_prep_bwd(
        dqp, dkp, dvv, rest, q_norm_g, kv_norm_g, w_uq_p, w_uk_p, w_uv, cos_t, sin_t)
    pieces = [dq_sb, dk_sb, dv_sb, d_sbg, d_mlg, dcq, dckv, dkr]
    d_cols = [_tn_matmul_multi(h_b, pieces[0:2], "dw_in_0"), _tn_matmul_multi(h_b, pieces[2:4], "dw_in_1"),
              _tn_matmul_multi(h_b, pieces[4:8], "dw_in_2")]
    pay_b = [_payload_in(d_cols), _tn_matmul(cqn_b, dq_b, "dw_uq", blocked=True),
             _payload_ukv(_tn_matmul(ckvn_b, dk_b, "dw_uk", blocked=True), _tn_matmul(ckvn_b, dv_b, "dw_uv"))]
    pair_b = _pair_sums(pay_b, _pair_exchange(pay_b, "grad_pair_exchange"), place, "grad_pair_sums_b")
    grad_x, small_in, *landed_b = _in_bwd(xs, norm_pre_g, dx1, pieces, w_in_p, [sm for sm, _ in pair_b])
    reduced = [(own, l2) for (_, own), l2 in zip(pair_b + pair_a, landed_b + landed_a)]
    slab = jnp.concatenate([small_in[0:1], jnp.pad(small_prep[0:2], ((0, 0), (0, D_MODEL - Q_LORA))),
                            small_mid[3:8]], axis=0)
    return grad_x, reduced, slab
```

```python
import jax
import jax.numpy as jnp
from jax import lax
from jax.experimental import pallas as pl
from jax.experimental.pallas import tpu as pltpu

F32 = jnp.float32
BF16 = jnp.bfloat16
MESH = pl.DeviceIdType.MESH

N_DEV = 8
D_MODEL = 1024
N_HEADS = 8
HEAD_DIM = 64
D_GRP = N_HEADS * HEAD_DIM
Q_LORA = 256
KV_LORA = 128
ROPE_DIM = 32
PLE_DIM = 256
CHUNK_SHIFT = 6
ROPE_THETA = 10000.0
EPS = 1e-6
SB_SCALE = HEAD_DIM ** -0.5
MLA_SCALE = (HEAD_DIM + ROPE_DIM) ** -0.5
NEG = -1e30
LOG2_E = 1.4426950408889634
LN_2 = 0.6931471805599453
SB_CUTOFF = 110.0

ADAM_LR = 0.001
ADAM_B1 = 0.9
ADAM_B2 = 0.999
ADAM_EPS = 1e-08
ADAM_WD = 0.01
ADAM_STEP = 10

LANES = 128
TQ = 256
TK = 256
TM = 256
TM_IO = 512
TS_DW = 2048

D_IN_P = 3072

_NT = (((1,), (1,)), ((), ()))
_TN = (((0,), (0,)), ((), ()))


def _params(sem, vmem_mb):
    return pltpu.CompilerParams(dimension_semantics=sem, vmem_limit_bytes=vmem_mb << 20)


def _hbm(*arrays):
    return [pltpu.with_memory_space_constraint(a, pltpu.HBM) for a in arrays]


def _dot(a, b):
    return jnp.dot(a, b, preferred_element_type=F32)


def _dot_nt(a, b):
    return lax.dot_general(a, b, _NT, preferred_element_type=F32)


def _dot_tn(a, b):
    return lax.dot_general(a, b, _TN, preferred_element_type=F32)


def _hl_dot(a, b):
    hi = a.astype(BF16)
    lo = (a - hi.astype(F32)).astype(BF16)
    return _dot(hi, b) + _dot(lo, b)


def _sigmoid(x):
    return 1.0 / (1.0 + jnp.exp(-x))


def _rope_swap(x, lane):
    left = pltpu.roll(x, LANES - 16, axis=1)
    right = pltpu.roll(x, 16, axis=1)
    lo = (lane >= 64) & (lane < 80)
    hi = (lane >= 80) & (lane < 96)
    return jnp.where(lo, left, jnp.where(hi, right, 0.0))


def _two_level_gather(x_refs, out_refs, send_sems, recv_sems, local_sems):
    x, y, c = lax.axis_index("x"), lax.axis_index("y"), lax.axis_index("c")
    me, sibling = (x, y, c), (x, y, 1 - c)
    chips = [(1 - x, y), (x, 1 - y), (1 - x, 1 - y)]
    ops = range(len(x_refs))

    def slot(o, px, py, pc):
        return out_refs[o].at[4 * px + 2 * py + pc]

    def copy(o, k, block, to, src=None):
        return pltpu.make_async_remote_copy(
            src_ref=slot(o, *block) if src is None else src, dst_ref=slot(o, *block),
            send_sem=send_sems.at[o, k], recv_sem=recv_sems.at[o, k],
            device_id=to, device_id_type=MESH)

    def mine():
        return [pltpu.make_async_copy(x_refs[o], slot(o, *me), local_sems.at[o]) for o in ops]

    def first():
        return ([copy(o, 0, me, sibling, src=x_refs[o]) for o in ops]
                + [copy(o, 1 + j, me, (*chip, c), src=x_refs[o]) for j, chip in enumerate(chips) for o in ops])

    def start():
        for cp in mine() + first():
            cp.start()

    def finish():
        passed = []
        for j, chip in enumerate(chips):
            for o in ops:
                copy(o, 1 + j, (*chip, c), me).wait_recv()
                passed.append(copy(o, 4 + j, (*chip, c), sibling))
                passed[-1].start()
        for o in ops:
            copy(o, 0, sibling, me).wait_recv()
        for j, chip in enumerate(chips):
            for o in ops:
                copy(o, 4 + j, (*chip, 1 - c), me).wait_recv()
        for cp in first() + passed:
            cp.wait_send()
        for cp in mine():
            cp.wait()

    return start, finish


def _gather_sems(n_op):
    return [pltpu.SemaphoreType.DMA((n_op, 7)), pltpu.SemaphoreType.DMA((n_op, 7)),
            pltpu.SemaphoreType.DMA((n_op,))]


def _all_gather(shards):
    n_op = len(shards)

    def body(*refs):
        start, finish = _two_level_gather(refs[:n_op], refs[n_op:2 * n_op], *refs[2 * n_op:])
        start()
        finish()

    any_spec = pl.BlockSpec(memory_space=pl.ANY)
    return pl.pallas_call(
        body, name="weight_all_gather",
        out_shape=[jax.ShapeDtypeStruct((N_DEV,) + a.shape, a.dtype) for a in shards],
        in_specs=[any_spec] * n_op, out_specs=[any_spec] * n_op, scratch_shapes=_gather_sems(n_op),
        compiler_params=pltpu.CompilerParams(vmem_limit_bytes=4 << 20),
    )(*shards)


def _pair_copies(g_refs, l_refs, ssem, rsem):
    x, y, c = lax.axis_index("x"), lax.axis_index("y"), lax.axis_index("c")
    copies = []
    for o in range(len(g_refs)):
        for chip in range(4):
            copies.append(pltpu.make_async_remote_copy(
                src_ref=g_refs[o].at[2 * chip + (1 - c)], dst_ref=l_refs[o].at[chip],
                send_sem=ssem.at[o, chip], recv_sem=rsem.at[o, chip],
                device_id=(x, y, 1 - c), device_id_type=MESH))
    return copies


def _pair_specs(pays):
    n_op = len(pays)
    any_spec = pl.BlockSpec(memory_space=pl.ANY)
    return ([any_spec] * n_op, [any_spec] * n_op,
            [jax.ShapeDtypeStruct((4,) + a.shape[1:], F32) for a in pays],
            [pltpu.SemaphoreType.DMA((n_op, 4)), pltpu.SemaphoreType.DMA((n_op, 4))])


def _pair_exchange(pays, name):
    n_op = len(pays)
    in_specs, out_specs, out_shape, sems = _pair_specs(pays)

    def body(*refs):
        copies = _pair_copies(refs[:n_op], refs[n_op:2 * n_op], *refs[2 * n_op:])
        for cp in copies:
            cp.start()
        for cp in copies:
            cp.wait()

    return pl.pallas_call(body, name=name, out_shape=out_shape, in_specs=in_specs, out_specs=out_specs,
                          scratch_shapes=sems,
                          compiler_params=pltpu.CompilerParams(vmem_limit_bytes=4 << 20))(*pays)


def _slab_exchange(small):
    sr, n = small.shape

    def body(s_ref, sland_ref, ssem, rsem, lsem):
        x, y, c = lax.axis_index("x"), lax.axis_index("y"), lax.axis_index("c")
        me = 4 * x + 2 * y + c
        copies = []
        for k in range(1, N_DEV):
            peer = (1 - x if (k >> 2) & 1 else x, 1 - y if (k >> 1) & 1 else y, 1 - c if k & 1 else c)
            copies.append(pltpu.make_async_remote_copy(
                src_ref=s_ref, dst_ref=sland_ref.at[me], send_sem=ssem.at[k], recv_sem=rsem.at[k],
                device_id=peer, device_id_type=MESH))
        own = pltpu.make_async_copy(s_ref, sland_ref.at[me], lsem)
        own.start()
        for cp in copies:
            cp.start()
        for cp in copies:
            cp.wait()
        own.wait()

    any_spec = pl.BlockSpec(memory_space=pl.ANY)
    return pl.pallas_call(
        body, name="grad_slab_exchange", out_shape=jax.ShapeDtypeStruct((N_DEV, sr, n), F32),
        in_specs=[any_spec], out_specs=any_spec,
        scratch_shapes=[pltpu.SemaphoreType.DMA((N_DEV,)), pltpu.SemaphoreType.DMA((N_DEV,)),
                        pltpu.SemaphoreType.DMA],
        compiler_params=pltpu.CompilerParams(vmem_limit_bytes=4 << 20),
    )(small)


def _pair_sums(pays, landed, place, name):
    n = len(pays)
    dims = [p.shape[1:] for p in pays]

    def body(place_ref, *refs):
        g_refs, l_refs, s_refs, own_refs = refs[:n], refs[n:2 * n], refs[2 * n:3 * n], refs[3 * n:]
        i = pl.program_id(0)
        for o in range(n):
            tot = g_refs[o][...] + l_refs[o][...]
            s_refs[o][...] = tot.astype(BF16)

            @pl.when(i == place_ref[1])
            def _(o=o, tot=tot):
                own_refs[o][...] = tot

    grid_spec = pltpu.PrefetchScalarGridSpec(
        num_scalar_prefetch=1, grid=(4,),
        in_specs=[pl.BlockSpec((None, r, c), lambda i, pr: (2 * i + pr[0], 0, 0)) for r, c in dims]
        + [pl.BlockSpec((None, r, c), lambda i, pr: (i, 0, 0)) for r, c in dims],
        out_specs=[pl.BlockSpec((None, r, c), lambda i, pr: (i, 0, 0)) for r, c in dims]
        + [pl.BlockSpec((r, c), lambda i, pr: (0, 0)) for r, c in dims])
    out = pl.pallas_call(
        body, name=name, grid_spec=grid_spec,
        out_shape=[jax.ShapeDtypeStruct((4, r, c), BF16) for r, c in dims]
        + [jax.ShapeDtypeStruct((r, c), F32) for r, c in dims],
        compiler_params=_params(("arbitrary",), 16),
    )(place, *pays, *landed)
    return list(zip(out[:n], out[n:]))


def _chip_copies(s_refs, l_refs, ssem, rsem):
    x, y, c = lax.axis_index("x"), lax.axis_index("y"), lax.axis_index("c")
    copies = []
    for rel in range(1, 4):
        px = 1 - x if rel & 2 else x
        py = 1 - y if rel & 1 else y
        for o in range(len(s_refs)):
            copies.append(pltpu.make_async_remote_copy(
                src_ref=s_refs[o].at[2 * px + py], dst_ref=l_refs[o].at[rel - 1],
                send_sem=ssem.at[o, rel - 1], recv_sem=rsem.at[o, rel - 1],
                device_id=(px, py, c), device_id_type=MESH))
    return copies


def _chip_specs(sums):
    n_op = len(sums)
    any_spec = pl.BlockSpec(memory_space=pl.ANY)
    return ([any_spec] * n_op, [any_spec] * n_op,
            [jax.ShapeDtypeStruct((3,) + a.shape[1:], BF16) for a in sums],
            [pltpu.SemaphoreType.DMA((n_op, 3)), pltpu.SemaphoreType.DMA((n_op, 3))])


def _adamw_math(g, w, m, v):
    mn = ADAM_B1 * m + (1.0 - ADAM_B1) * g
    vn = ADAM_B2 * v + (1.0 - ADAM_B2) * (g * g)
    m_hat = mn / (1.0 - ADAM_B1 ** ADAM_STEP)
    v_hat = vn / (1.0 - ADAM_B2 ** ADAM_STEP)
    return -ADAM_LR * (m_hat / (jnp.sqrt(v_hat) + ADAM_EPS) + ADAM_WD * w), mn, vn


def _adamw_matrix(own, landed, w, m, v, name):
    _, r, c = w.shape
    cp = own.shape[1]
    br = min(r, 256)

    def body(own_ref, l_ref, w_ref, m_ref, v_ref, g_out, d_out, m_out, v_out):
        g = own_ref[...]
        for k in range(3):
            g = g + l_ref[k].astype(F32)
        g = g[:, :c]
        g_out[...] = g
        d_out[...], m_out[...], v_out[...] = _adamw_math(g, w_ref[...], m_ref[...], v_ref[...])

    row = pl.BlockSpec((None, br, c), lambda i: (0, i, 0))
    shp = jax.ShapeDtypeStruct((1, r, c), F32)
    return pl.pallas_call(
        body, name=name, grid=(r // br,),
        in_specs=[pl.BlockSpec((br, cp), lambda i: (i, 0)), pl.BlockSpec((3, br, cp), lambda i: (0, i, 0)),
                  row, row, row],
        out_specs=(row, row, row, row), out_shape=(shp, shp, shp, shp),
        compiler_params=_params(("parallel",), 12),
    )(own, landed, w, m, v)


_VEC_PLACE = ((0, 0), (1, 0), (2, 0), (3, 0), (3, D_GRP), (4, 0), (5, 0), (6, 0))


def _adamw_vectors(sland, ws, ms, vs):
    nv = len(ws)

    def body(l_ref, *refs):
        w_refs, m_refs, v_refs = refs[:nv], refs[nv:2 * nv], refs[2 * nv:3 * nv]
        loss_ref = refs[3 * nv]
        outs = refs[3 * nv + 1:]
        g_all = l_ref[0]
        for j in range(1, N_DEV):
            g_all = g_all + l_ref[j]
        loss_ref[...] = jnp.sum(g_all[7:8, :], axis=1, keepdims=True)
        for k, (row, lane0) in enumerate(_VEC_PLACE):
            n = w_refs[k].shape[1]
            g = g_all[row:row + 1, lane0:lane0 + n]
            d, mn, vn = _adamw_math(g, w_refs[k][...], m_refs[k][...], v_refs[k][...])
            outs[k][...] = g
            outs[nv + k][...] = d
            outs[2 * nv + k][...] = mn
            outs[3 * nv + k][...] = vn

    def whole(shape):
        return pl.BlockSpec(shape, lambda i: (0,) * len(shape))

    shapes = [jax.ShapeDtypeStruct(w.shape, F32) for w in ws]
    return pl.pallas_call(
        body, name="adamw_vectors", grid=(1,),
        in_specs=[whole(sland.shape)] + [whole(w.shape) for w in ws] * 3,
        out_specs=[whole((1, 1))] + [whole(w.shape) for w in ws] * 4,
        out_shape=[jax.ShapeDtypeStruct((1, 1), F32)] + shapes * 4,
        compiler_params=_params(("arbitrary",), 4),
    )(sland, *ws, *ms, *vs)


def _in_proj(x, g, w, shards):
    s = x.shape[0]
    n_op = len(shards)
    steps = s // TM_IO

    def body(x_ref, g_ref, w_ref, *refs):
        shard_refs = refs[:n_op]
        qkv_ref, rest_ref, h_ref = refs[n_op:n_op + 3]
        gath_refs = refs[n_op + 3:2 * n_op + 3]
        start, finish = _two_level_gather(shard_refs, gath_refs, *refs[2 * n_op + 3:])
        i = pl.program_id(0)

        @pl.when(i == 0)
        def _():
            start()

        xv = x_ref[...]
        r = lax.rsqrt(jnp.mean(xv * xv, axis=-1, keepdims=True) + EPS)
        h = ((xv * r) * g_ref[...]).astype(BF16)
        h_ref[...] = h
        qkv_ref[...] = _dot(h, w_ref[:, :1536]).astype(BF16)
        rest_ref[...] = _dot(h, w_ref[:, 1536:])

        @pl.when(i == steps - 1)
        def _():
            finish()

    any_spec = pl.BlockSpec(memory_space=pl.ANY)
    return pl.pallas_call(
        body, name="in_proj", grid=(steps,),
        in_specs=[pl.BlockSpec((TM_IO, D_MODEL), lambda i: (i, 0)),
                  pl.BlockSpec((1, D_MODEL), lambda i: (0, 0)),
                  pl.BlockSpec((D_MODEL, D_IN_P), lambda i: (0, 0))] + [any_spec] * n_op,
        out_specs=[pl.BlockSpec((TM_IO, 1536), lambda i: (i, 0)),
                   pl.BlockSpec((TM_IO, 1536), lambda i: (i, 0)),
                   pl.BlockSpec((TM_IO, D_MODEL), lambda i: (i, 0))] + [any_spec] * n_op,
        out_shape=[pltpu.HBM((s, 1536), BF16), pltpu.HBM((s, 1536), F32),
                   pltpu.HBM((s, D_MODEL), BF16)]
        + [jax.ShapeDtypeStruct((N_DEV,) + a.shape, a.dtype) for a in shards],
        scratch_shapes=_gather_sems(n_op),
        compiler_params=_params(("arbitrary",), 32),
    )(x, g, w, *shards)


def _mla_prep(rest, gq, gkv, wuq, wuk, wuv, cos_t, sin_t):
    s = rest.shape[0]

    def body(cq_ref, ckv_ref, kr_ref, gq_ref, gkv_ref, wuq_ref, wuk_ref, wuv_ref, c_ref, s_ref,
             qp_ref, kp_ref, vv_ref, cqn_ref, ckvn_ref):
        lane = lax.broadcasted_iota(jnp.int32, (1, LANES), 1)
        cos_v, sin_v = c_ref[...], s_ref[...]
        cq = cq_ref[...]
        rq = lax.rsqrt(jnp.mean(cq * cq, axis=-1, keepdims=True) + EPS)
        cqn = ((cq * rq) * gq_ref[...]).astype(BF16)
        cqn_ref[...] = cqn
        q = _dot(cqn, wuq_ref[...])
        ckv = ckv_ref[...]
        rkv = lax.rsqrt(jnp.mean(ckv * ckv, axis=-1, keepdims=True) + EPS)
        ckvn = ((ckv * rkv) * gkv_ref[...]).astype(BF16)
        ckvn_ref[...] = ckvn
        kn = _dot(ckvn, wuk_ref[...])
        vv_ref[...] = _dot(ckvn, wuv_ref[...]).astype(BF16)
        kr = kr_ref[...]
        kr_roped = kr * cos_v + _rope_swap(kr, lane) * sin_v
        for h in range(N_HEADS):
            sl = slice(h * LANES, (h + 1) * LANES)
            qh = q[:, sl]
            qp_ref[:, sl] = (qh * cos_v + _rope_swap(qh, lane) * sin_v).astype(BF16)
            kp_ref[:, sl] = (kn[:, sl] + kr_roped).astype(BF16)

    def row(width, idx):
        return pl.BlockSpec((TM_IO, width), lambda i: (i, idx))

    def full(a):
        return pl.BlockSpec(a.shape, lambda i: (0, 0))

    return pl.pallas_call(
        body, name="mla_prep", grid=(s // TM_IO,),
        in_specs=[row(Q_LORA, 4), row(KV_LORA, 10), row(LANES, 11), full(gq), full(gkv),
                  full(wuq), full(wuk), full(wuv), row(LANES, 0), row(LANES, 0)],
        out_specs=(row(1024, 0), row(1024, 0), row(D_GRP, 0), row(Q_LORA, 0), row(KV_LORA, 0)),
        out_shape=(pltpu.HBM((s, 1024), BF16), pltpu.HBM((s, 1024), BF16),
                   pltpu.HBM((s, D_GRP), BF16), pltpu.HBM((s, Q_LORA), BF16),
                   pltpu.HBM((s, KV_LORA), BF16)),
        compiler_params=_params(("parallel",), 13),
    )(*_hbm(rest, rest, rest), gq, gkv, wuq, wuk, wuv, cos_t, sin_t)


def _sb_live(n, qi, carries):
    top = carries[0]
    for c in carries[1:]:
        top = jnp.maximum(top, c)
    return jnp.logical_and(n < qi, jnp.max(top) > -SB_CUTOFF)


def _sb_fwd(qkv, hb):
    s = qkv.shape[0]

    def body(q_ref, k_ref, v_ref, o_ref, acc):
        qi = pl.program_id(1)
        lane = lax.broadcasted_iota(jnp.int32, (1, LANES), 1)
        is_a = lane < HEAD_DIM
        pair = lambda h: slice((h // 2) * LANES, (h // 2 + 1) * LANES)
        q_h = []
        for h in range(hb):
            qs = q_ref[:, pair(h)] * SB_SCALE
            mine = is_a if h % 2 == 0 else jnp.logical_not(is_a)
            q_h.append(jnp.where(mine, qs, jnp.zeros_like(qs)))
        r_i = lax.broadcasted_iota(jnp.int32, (TQ, TK), 0)
        c_i = lax.broadcasted_iota(jnp.int32, (TQ, TK), 1)
        past = c_i < r_i
        upper = (r_i > c_i).astype(BF16)
        acc[...] = jnp.zeros_like(acc)

        def tile(j, carries, diag):
            ks = pl.ds(pl.multiple_of(j * TK, TK), TK)
            zs = [_dot_nt(q_h[h], k_ref[ks, pair(h)]) for h in range(hb)]
            if diag:
                zs = [jnp.where(past, z, NEG) for z in zs]
            lfs = [-(jnp.maximum(z, 0.0) + jnp.log(1.0 + jnp.exp(-jnp.abs(z)))) for z in zs]
            sufs = [_hl_dot(lfs[h], upper) for h in range(hb)]
            out = []
            for h in range(hb):
                w = jnp.exp(zs[h] + lfs[h] + (sufs[h] + carries[h]))
                acc[h] += _dot(w.astype(BF16), v_ref[ks, pair(h)])
                out.append(carries[h] + jnp.sum(lfs[h], axis=1, keepdims=True))
            return tuple(out)

        zero = jnp.zeros((TQ, 1), F32)
        carries = tile(qi, (zero,) * hb, True)

        def step(st):
            return (st[0] + 1,) + tile(qi - 1 - st[0], st[1:], False)

        lax.while_loop(lambda st: _sb_live(st[0], qi, st[1:]), step, (0,) + carries)
        for pr in range(hb // 2):
            o_ref[:, pr * LANES:(pr + 1) * LANES] = jnp.where(is_a, acc[2 * pr], acc[2 * pr + 1])

    width = hb * HEAD_DIM
    nb = D_GRP // width
    slab = lambda part: pl.BlockSpec((s, width), lambda g, qi: (0, part * nb + g))
    blk = pl.BlockSpec((TQ, width), lambda g, qi: (qi, g))
    return pl.pallas_call(
        body, name="sb_fwd", grid=(nb, s // TQ),
        in_specs=[blk, slab(1), slab(2)], out_specs=blk,
        out_shape=pltpu.HBM((s, D_GRP), F32),
        scratch_shapes=[pltpu.VMEM((hb, TQ, LANES), F32)],
        compiler_params=_params(("arbitrary", "arbitrary"), 28),
    )(*_hbm(qkv, qkv, qkv))


def _sb_bwd(qkv, d_o, sums):
    s = qkv.shape[0]
    nq = s // TQ
    nk = s // TK
    n_op = len(sums)
    ride_in, ride_out, ride_shape, ride_sems = _chip_specs(sums)

    def body(q_ref, k_ref, v_ref, do_ref, *refs):
        s_refs = refs[:n_op]
        dq_ref, dk_ref, dv_ref = refs[n_op:n_op + 3]
        l_refs = refs[n_op + 3:2 * n_op + 3]
        x1s, bts, dqacc, dkacc, dvacc, ssem, rsem = refs[2 * n_op + 3:]
        qi = pl.program_id(1)
        first_step = jnp.logical_and(pl.program_id(0) == 0, qi == 0)
        last_step = jnp.logical_and(pl.program_id(0) == pl.num_programs(0) - 1, qi == nq - 1)

        @pl.when(first_step)
        def _():
            for cp in _chip_copies(s_refs, l_refs, ssem, rsem):
                cp.start()

        lane = lax.broadcasted_iota(jnp.int32, (1, LANES), 1)
        is_a = lane < HEAD_DIM

        @pl.when(qi == 0)
        def _():
            dkacc[...] = jnp.zeros_like(dkacc)
            dvacc[...] = jnp.zeros_like(dvacc)

        qs = q_ref[...] * SB_SCALE
        zq = jnp.zeros_like(qs)
        qs_x = (jnp.where(is_a, qs, zq), jnp.where(is_a, zq, qs))
        dob = do_ref[...].astype(BF16)
        do_x = (jnp.where(is_a, dob, zq), jnp.where(is_a, zq, dob))
        r_i = lax.broadcasted_iota(jnp.int32, (TQ, TK), 0)
        c_i = lax.broadcasted_iota(jnp.int32, (TQ, TK), 1)
        past = c_i < r_i
        upper = (r_i > c_i).astype(BF16)
        upper_incl = (r_i >= c_i).astype(BF16)
        dqacc[...] = jnp.zeros_like(dqacc)
        both = ((0, 0), (0, 1), (1, 0), (1, 1))

        def tiles(n):
            j_hi = qi - 2 * n
            lo_ok = j_hi >= 1
            j_lo = jnp.maximum(j_hi - 1, 0)
            ks = (pl.ds(pl.multiple_of(j_hi * TK, TK), TK), pl.ds(pl.multiple_of(j_lo * TK, TK), TK))
            return j_hi, lo_ok, j_lo, ks

        def sweep(n, carries):
            j_hi, lo_ok, j_lo, ks = tiles(n)
            slot = (j_hi, jnp.where(lo_ok, j_lo, nk))
            valid = (jnp.logical_or(past, j_hi < qi), lo_ok)
            z = {th: jnp.where(valid[th[0]], _dot_nt(qs_x[th[1]], k_ref[ks[th[0]], :]), NEG) for th in both}
            log_b, lf_sum, suf = {}, {}, {}
            for th in both:
                lf = -(jnp.maximum(z[th], 0.0) + jnp.log(1.0 + jnp.exp(-jnp.abs(z[th]))))
                log_b[th] = z[th] + lf
                lf_sum[th] = jnp.sum(lf, axis=1, keepdims=True)
                suf[th] = _hl_dot(lf, upper)
            c, g_in = {}, {}
            for h in range(2):
                c[0, h], g_in[0, h] = carries[2 * h], carries[2 * h + 1]
                c[1, h] = c[0, h] + lf_sum[0, h]
            d_a = {th: _dot_nt(do_x[th[1]], v_ref[ks[th[0]], :]) for th in both}
            a_b, g, g_sum, sg = {}, {}, {}, {}
            for th in both:
                a = jnp.exp(log_b[th] + (suf[th] + c[th]))
                a_b[th] = a.astype(BF16)
                g[th] = a * d_a[th]
                g_sum[th] = jnp.sum(g[th], axis=1, keepdims=True)
                sg[th] = _hl_dot(g[th], upper_incl)
            for h in range(2):
                g_in[1, h] = g_in[0, h] + g_sum[0, h]
            for th in both:
                t, h = th
                beta = jnp.exp(log_b[th])
                x1s[slot[t], h] = g[th] * (1.0 - beta) + beta * (sg[th] + g_in[th])
                bts[slot[t], h] = beta
                dvacc[ks[t], :] += _dot_tn(a_b[th], do_x[h])
            out = []
            for h in range(2):
                out.append(c[1, h] + lf_sum[1, h])
                out.append(g_in[1, h] + g_sum[1, h])
            return tuple(out)

        zero = jnp.zeros((TQ, 1), F32)
        first = sweep(0, (zero, zero, zero, zero))

        def more(st):
            return jnp.logical_and(2 * st[0] <= qi, jnp.max(jnp.maximum(st[1], st[3])) > -SB_CUTOFF)

        swept = lax.while_loop(more, lambda st: (st[0] + 1,) + sweep(st[0], st[1:]), (1,) + first)
        g_tot = (swept[2], swept[4])

        def apply(n, carry):
            j_hi, lo_ok, j_lo, ks = tiles(n)

            def one(j, kslice):
                for h in range(2):
                    dz = (x1s[j, h] - bts[j, h] * g_tot[h]).astype(BF16)
                    dqacc[h] += _dot(dz, k_ref[kslice, :])
                    dkacc[kslice, :] += _dot_tn(dz, qs_x[h])

            one(j_hi, ks[0])

            @pl.when(lo_ok)
            def _():
                one(j_lo, ks[1])

            return carry

        lax.fori_loop(0, swept[0], apply, 0)
        dq_ref[...] = (jnp.where(is_a, dqacc[0], dqacc[1]) * SB_SCALE).astype(BF16)

        @pl.when(qi == nq - 1)
        def _():
            dk_ref[...] = dkacc[...].astype(BF16)
            dv_ref[...] = dvacc[...].astype(BF16)

        @pl.when(last_step)
        def _():
            for cp in _chip_copies(s_refs, l_refs, ssem, rsem):
                cp.wait()

    slab = lambda off: pl.BlockSpec((s, LANES), lambda p, qi: (0, off + p))
    blk = pl.BlockSpec((TQ, LANES), lambda p, qi: (qi, p))
    out_slab = pl.BlockSpec((s, LANES), lambda p, qi: (0, p))
    shp = pltpu.HBM((s, D_GRP), BF16)
    return pl.pallas_call(
        body, name="sb_bwd", grid=(4, nq),
        in_specs=[blk, slab(4), slab(8), blk] + ride_in,
        out_specs=[blk, out_slab, out_slab] + ride_out, out_shape=[shp, shp, shp] + ride_shape,
        scratch_shapes=[pltpu.VMEM((nk + 1, 2, TQ, TK), F32)] * 2
        + [pltpu.VMEM((2, TQ, LANES), F32), pltpu.VMEM((s, LANES), F32), pltpu.VMEM((s, LANES), F32)]
        + ride_sems,
        compiler_params=_params(("arbitrary", "arbitrary"), 44),
    )(*_hbm(qkv, qkv, qkv, d_o), *sums)


def _mla_fwd(qp, kp, vv, hb):
    s = qp.shape[0]
    c2 = MLA_SCALE * LOG2_E

    def body(q_ref, k_ref, v_ref, o_ref, lse_ref, vaug, mrun, mb, acc, zbuf):
        qi = pl.program_id(1)
        lane = lax.broadcasted_iota(jnp.int32, (1, LANES), 1)
        is_a = lane < HEAD_DIM

        @pl.when(qi == 0)
        def _():
            for h in range(hb):
                vp = v_ref[:, (h // 2) * LANES:(h // 2 + 1) * LANES]
                mine = is_a if h % 2 == 0 else jnp.logical_not(is_a)
                vaug[h] = jnp.where(mine, vp, jnp.ones_like(vp))

        r_i = lax.broadcasted_iota(jnp.int32, (TQ, TK), 0)
        c_i = lax.broadcasted_iota(jnp.int32, (TQ, TK), 1)
        visible = (c_i >> CHUNK_SHIFT) <= (r_i >> CHUNK_SHIFT)

        def key_rows(j):
            return pl.ds(pl.multiple_of(j * TK, TK), TK)

        def sweep(tiles):
            def loop(n, carry):
                tiles(((2 * n, False), (2 * n + 1, False)))
                return carry

            lax.fori_loop(0, qi // 2, loop, 0)

            @pl.when(qi % 2 == 1)
            def _():
                tiles(((qi - 1, False), (qi, True)))

            @pl.when(qi % 2 == 0)
            def _():
                tiles(((qi, True),))

        mrun[...] = jnp.full_like(mrun, NEG)

        def tiles_max(js):
            zs = [[_dot_nt(q_ref[:, h * LANES:(h + 1) * LANES], k_ref[key_rows(j), h * LANES:(h + 1) * LANES])
                   for h in range(hb)] for j, _ in js]
            for t, (j, diag) in enumerate(js):
                for h in range(hb):
                    z = jnp.where(visible, zs[t][h], NEG) if diag else zs[t][h]
                    zbuf[j, h] = z
                    mrun[h] = jnp.maximum(mrun[h], z)

        sweep(tiles_max)
        for h in range(hb):
            m = jnp.max(mrun[h], axis=1, keepdims=True) * c2
            mb[h] = jnp.broadcast_to(m, (TQ, TK))
        acc[...] = jnp.zeros_like(acc)

        def tiles_pv(js):
            ps = [[jnp.exp2((zbuf[j, h] * c2 - mb[h]).astype(BF16)) for h in range(hb)] for j, _ in js]
            for t, (j, _) in enumerate(js):
                for h in range(hb):
                    acc[h] += _dot(ps[t][h], vaug[h, key_rows(j), :])

        sweep(tiles_pv)
        for pr in range(hb // 2):
            a, b = 2 * pr, 2 * pr + 1
            psl = slice(pr * LANES, (pr + 1) * LANES)
            acc_a, acc_b = acc[a], acc[b]
            l_a = pltpu.roll(acc_a, HEAD_DIM, axis=1)
            l_b = pltpu.roll(acc_b, HEAD_DIM, axis=1)
            o_ref[:, psl] = jnp.where(is_a, acc_a * (1.0 / l_a), acc_b * (1.0 / l_b))
            lse_ref[:, psl] = jnp.where(is_a, mb[a, :, :LANES] * LN_2 + jnp.log(l_a),
                                        mb[b, :, :LANES] * LN_2 + jnp.log(l_b))

    blk = pl.BlockSpec((TQ, hb * HEAD_DIM), lambda g, qi: (qi, g))
    shp = pltpu.HBM((s, D_GRP), F32)
    return pl.pallas_call(
        body, name="mla_fwd", grid=(N_HEADS // hb, s // TQ),
        in_specs=[pl.BlockSpec((TQ, hb * LANES), lambda g, qi: (qi, g)),
                  pl.BlockSpec((s, hb * LANES), lambda g, qi: (0, g)),
                  pl.BlockSpec((s, hb * HEAD_DIM), lambda g, qi: (0, g))],
        out_specs=(blk, blk), out_shape=(shp, shp),
        scratch_shapes=[pltpu.VMEM((hb, s, LANES), BF16), pltpu.VMEM((hb, TQ, TK), F32),
                        pltpu.VMEM((hb, TQ, TK), F32), pltpu.VMEM((hb, TQ, LANES), F32),
                        pltpu.VMEM((s // TK, hb, TQ, TK), F32)],
        compiler_params=_params(("arbitrary", "arbitrary"), 44),
    )(*_hbm(qp, kp, vv))


def _mla_bwd(qp, kp, vv, d_o, o, lse, hb, pays):
    s = qp.shape[0]
    nq = s // TQ
    c2 = MLA_SCALE * LOG2_E
    n_op = len(pays)
    ride_in, ride_out, ride_shape, ride_sems = _pair_specs(pays)

    def body(q_ref, k_ref, v_ref, do_ref, o_ref, lse_ref, *refs):
        g_refs = refs[:n_op]
        dq_ref, dk_ref, dv_ref = refs[n_op:n_op + 3]
        l_refs = refs[n_op + 3:2 * n_op + 3]
        dqacc, lse_b, delta_b, q_t, do_t, ssem, rsem = refs[2 * n_op + 3:]
        qi = pl.program_id(1)

        @pl.when(jnp.logical_and(pl.program_id(0) == 0, qi == 0))
        def _():
            for cp in _pair_copies(g_refs, l_refs, ssem, rsem):
                cp.start()

        lane = lax.broadcasted_iota(jnp.int32, (1, LANES), 1)
        is_a = lane < HEAD_DIM

        @pl.when(qi == 0)
        def _():
            dk_ref[...] = jnp.zeros_like(dk_ref)
            dv_ref[...] = jnp.zeros_like(dv_ref)

        r_i = lax.broadcasted_iota(jnp.int32, (TQ, TK), 0)
        c_i = lax.broadcasted_iota(jnp.int32, (TQ, TK), 1)
        visible = (c_i >> CHUNK_SHIFT) <= (r_i >> CHUNK_SHIFT)
        do_x = []
        for h in range(hb):
            psl = slice((h // 2) * LANES, (h // 2 + 1) * LANES)
            mine = is_a if h % 2 == 0 else jnp.logical_not(is_a)
            d_o = do_ref[:, psl]
            delta = jnp.sum(jnp.where(mine, d_o * o_ref[:, psl], 0.0), axis=1, keepdims=True)
            lse_h = jnp.sum(jnp.where(lane == (h % 2) * HEAD_DIM, lse_ref[:, psl], 0.0), axis=1, keepdims=True)
            lse_b[h] = jnp.broadcast_to(lse_h * LOG2_E, (TQ, TK))
            delta_b[h] = jnp.broadcast_to(delta, (TQ, TK))
            do_h = jnp.where(mine, d_o, 0.0)
            do_x.append(do_h.astype(BF16))
            do_t[h] = do_h.T.astype(BF16)
            q_t[h] = q_ref[:, h * LANES:(h + 1) * LANES].astype(F32).T.astype(BF16)
        dqacc[...] = jnp.zeros_like(dqacc)

        head = lambda h: slice(h * LANES, (h + 1) * LANES)
        pair = lambda h: slice((h // 2) * LANES, (h // 2 + 1) * LANES)

        def tiles(js):
            th = [(j, diag, pl.ds(pl.multiple_of(j * TK, TK), TK), h) for j, diag in js for h in range(hb)]
            zs = [_dot_nt(q_ref[:, head(h)], k_ref[ks, head(h)]) for _, _, ks, h in th]
            dps = [_dot_nt(do_x[h], v_ref[ks, pair(h)]) for _, _, ks, h in th]
            for i, (j, diag, ks, h) in enumerate(th):
                e = zs[i] * c2 - lse_b[h]
                if diag:
                    e = jnp.where(visible, e, NEG)
                p = jnp.exp2(e)
                ds = (p * (dps[i] - delta_b[h]) * MLA_SCALE).astype(BF16)
                dqacc[h] += _dot(ds, k_ref[ks, head(h)])
                dk_ref[head(h), ks] += _dot(q_t[h], ds)
                dv_ref[pair(h), ks] += _dot(do_t[h], p.astype(BF16))

        def loop(n, c):
            tiles(((2 * n, False), (2 * n + 1, False)))
            return c

        lax.fori_loop(0, qi // 2, loop, 0)

        @pl.when(qi % 2 == 1)
        def _():
            tiles(((qi - 1, False), (qi, True)))

        @pl.when(qi % 2 == 0)
        def _():
            tiles(((qi, True),))

        for h in range(hb):
            dq_ref[:, h * LANES:(h + 1) * LANES] = dqacc[h]

        @pl.when(jnp.logical_and(pl.program_id(0) == pl.num_programs(0) - 1, qi == nq - 1))
        def _():
            for cp in _pair_copies(g_refs, l_refs, ssem, rsem):
                cp.wait()

    blk = pl.BlockSpec((TQ, hb * HEAD_DIM), lambda g, qi: (qi, g))
    return pl.pallas_call(
        body, name="mla_bwd", grid=(N_HEADS // hb, nq),
        in_specs=[pl.BlockSpec((TQ, hb * LANES), lambda g, qi: (qi, g)),
                  pl.BlockSpec((s, hb * LANES), lambda g, qi: (0, g)),
                  pl.BlockSpec((s, hb * HEAD_DIM), lambda g, qi: (0, g)), blk, blk, blk] + ride_in,
        out_specs=[pl.BlockSpec((TQ, hb * LANES), lambda g, qi: (qi, g)),
                   pl.BlockSpec((hb * LANES, s), lambda g, qi: (g, 0)),
                   pl.BlockSpec((hb * HEAD_DIM, s), lambda g, qi: (g, 0))] + ride_out,
        out_shape=[pltpu.HBM((s, 1024), F32), pltpu.HBM((1024, s), F32),
                   pltpu.HBM((D_GRP, s), F32)] + ride_shape,
        scratch_shapes=[pltpu.VMEM((hb, TQ, LANES), F32), pltpu.VMEM((hb, TQ, TK), F32),
                        pltpu.VMEM((hb, TQ, TK), F32), pltpu.VMEM((hb, LANES, TQ), BF16),
                        pltpu.VMEM((hb, LANES, TQ), BF16)] + ride_sems,
        compiler_params=_params(("arbitrary", "arbitrary"), 52),
    )(*_hbm(qp, kp, vv, d_o, o, lse), *pays)


def _mid(x, p, target, sb_o, mla_o, rest, g_sb, g_mla, w_out, g_post, w_ple, g_ple, w_pg, b_pg, bd):
    s = x.shape[0]

    def body(x_ref, p_ref, t_ref, sbo_ref, mlo_ref, sbg_ref, mlg_ref, gsb_ref, gml_ref, wout_ref,
             gpost_ref, wple_ref, gple_ref, wpg_ref, bpg_ref, bd_ref,
             dx1_ref, dsbo_ref, dmlo_ref, dsbg_ref, dmlg_ref, x1b_ref, dglb_ref, ycb_ref, dyb_ref,
             pb_ref, dub_ref, small_ref):
        i = pl.program_id(0)
        bd_m = bd_ref[...]

        def seg_mean(v):
            return _dot(v.astype(BF16), bd_m) * (1.0 / HEAD_DIM)

        groups = []
        for o_ref, gate_ref, gain_ref in ((sbo_ref, sbg_ref, gsb_ref), (mlo_ref, mlg_ref, gml_ref)):
            o = o_ref[...]
            r = lax.rsqrt(seg_mean(o * o) + EPS)
            n = o * r
            hn = n * gain_ref[...]
            gate = gate_ref[...]
            sg = _sigmoid(gate)
            si = gate * sg
            groups.append((r, n, hn, gate, sg, si, gain_ref[...]))
        ya = (groups[0][2] * groups[0][5]).astype(BF16)
        yb = (groups[1][2] * groups[1][5]).astype(BF16)
        ycb_ref[:, :D_GRP] = ya
        ycb_ref[:, D_GRP:] = yb
        y = _dot(ya, wout_ref[:D_GRP, :]) + _dot(yb, wout_ref[D_GRP:, :])
        ry = lax.rsqrt(jnp.mean(y * y, axis=-1, keepdims=True) + EPS)
        ny = y * ry
        x1 = x_ref[...] + ny * gpost_ref[...]
        x1b = x1.astype(BF16)
        x1b_ref[...] = x1b
        pb = p_ref[...].astype(BF16)
        pb_ref[...] = pb
        u = _dot(pb, wple_ref[...])
        ru = lax.rsqrt(jnp.mean(u * u, axis=-1, keepdims=True) + EPS)
        nu = u * ru
        ple = nu * gple_ref[...]
        gate = _sigmoid(_dot(x1b, wpg_ref[...]) + bpg_ref[...])
        x2 = x1 + ple * gate
        diff = x2 - t_ref[...]
        dx2 = diff * (1.0 / D_MODEL)

        d_ple = dx2 * gate
        d_glin = (dx2 * ple) * (gate * (1.0 - gate))
        dglb = d_glin.astype(BF16)
        dglb_ref[...] = dglb
        dx1 = dx2 + _dot_nt(dglb, wpg_ref[...])
        dx1_ref[...] = dx1
        d_nu = d_ple * gple_ref[...]
        d_u = ru * (d_nu - nu * jnp.mean(d_nu * nu, axis=-1, keepdims=True))
        dub_ref[...] = d_u.astype(BF16)
        d_ny = dx1 * gpost_ref[...]
        d_y = ry * (d_ny - ny * jnp.mean(d_ny * ny, axis=-1, keepdims=True))
        dyb = d_y.astype(BF16)
        dyb_ref[...] = dyb
        d_yc = (_dot_nt(dyb, wout_ref[:D_GRP, :]), _dot_nt(dyb, wout_ref[D_GRP:, :]))

        d_gain = []
        for gx, (do_ref, dg_ref) in enumerate(((dsbo_ref, dsbg_ref), (dmlo_ref, dmlg_ref))):
            r, n, hn, gate_g, sg, si, gain = groups[gx]
            dyg = d_yc[gx]
            d_hn = dyg * si
            dg_ref[...] = (dyg * hn * (sg * (1.0 + gate_g * (1.0 - sg)))).astype(BF16)
            d_gain.append(jnp.sum(d_hn * n, axis=0, keepdims=True))
            d_n = d_hn * gain
            do_ref[...] = r * (d_n - n * seg_mean(d_n * n))

        @pl.when(i == 0)
        def _():
            small_ref[...] = jnp.zeros_like(small_ref)

        small_ref[3:4, :D_GRP] += d_gain[0]
        small_ref[3:4, D_GRP:] += d_gain[1]
        small_ref[4:5, :] += jnp.sum(dx1 * ny, axis=0, keepdims=True)
        small_ref[5:6, :] += jnp.sum(d_ple * nu, axis=0, keepdims=True)
        small_ref[6:7, :] += jnp.sum(d_glin, axis=0, keepdims=True)
        small_ref[7:8, :] += jnp.sum(diff * diff, axis=0, keepdims=True) * (0.5 / D_MODEL)

    def row(width, idx=0):
        return pl.BlockSpec((TM, width), lambda i: (i, idx))

    def full(a):
        return pl.BlockSpec(a.shape, lambda i: (0, 0))

    f32 = lambda w: pltpu.HBM((s, w), F32)
    b16 = lambda w: pltpu.HBM((s, w), BF16)
    return pl.pallas_call(
        body, name="mid", grid=(s // TM,),
        in_specs=[row(D_MODEL), row(PLE_DIM), row(D_MODEL), row(D_GRP), row(D_GRP),
                  row(D_GRP, 0), row(D_GRP, 1), full(g_sb), full(g_mla), full(w_out), full(g_post),
                  full(w_ple), full(g_ple), full(w_pg), full(b_pg), full(bd)],
        out_specs=(row(D_MODEL), row(D_GRP), row(D_GRP), row(D_GRP), row(D_GRP), row(D_MODEL),
                   row(D_MODEL), row(D_MODEL), row(D_MODEL), row(PLE_DIM), row(D_MODEL),
                   pl.BlockSpec((8, D_MODEL), lambda i: (0, 0))),
        out_shape=(f32(D_MODEL), f32(D_GRP), f32(D_GRP), b16(D_GRP), b16(D_GRP), b16(D_MODEL),
                   b16(D_MODEL), b16(D_MODEL), b16(D_MODEL), b16(PLE_DIM), b16(D_MODEL),
                   jax.ShapeDtypeStruct((8, D_MODEL), F32)),
        compiler_params=_params(("arbitrary",), 46),
    )(*_hbm(x, p, target, sb_o, mla_o, rest, rest), g_sb, g_mla, w_out, g_post, w_ple, g_ple, w_pg, b_pg, bd)


def _mla_prep_bwd(dqp, dkp, dvv, rest, gq, gkv, wuq, wuk, wuv, cos_t, sin_t):
    s = rest.shape[0]

    def body(dqp_ref, dkp_ref, dvv_ref, cq_ref, ckv_ref, gq_ref, gkv_ref, wuq_ref, wuk_ref, wuv_ref,
             c_ref, s_ref, dcq_ref, dckv_ref, dkr_ref, dqb_ref, dkb_ref, dvb_ref, small_ref):
        i = pl.program_id(0)
        lane = lax.broadcasted_iota(jnp.int32, (1, LANES), 1)
        in_rope = (lane >= HEAD_DIM) & (lane < HEAD_DIM + ROPE_DIM)
        cos_v, sin_v = c_ref[...], s_ref[...]
        dkr_roped = jnp.zeros((TM_IO, LANES), F32)
        for h in range(N_HEADS):
            sl = slice(h * LANES, (h + 1) * LANES)
            dy = dqp_ref[:, sl]
            dqb_ref[:, sl] = (dy * cos_v + _rope_swap(dy * sin_v, lane)).astype(BF16)
            dkh = dkp_ref[sl, :].T
            dkb_ref[:, sl] = dkh.astype(BF16)
            dkr_roped = dkr_roped + jnp.where(in_rope, dkh, 0.0)
        dkr_ref[...] = (dkr_roped * cos_v + _rope_swap(dkr_roped * sin_v, lane)).astype(BF16)
        dvb = dvv_ref[...].T.astype(BF16)
        dvb_ref[...] = dvb

        cq = cq_ref[...]
        rq = lax.rsqrt(jnp.mean(cq * cq, axis=-1, keepdims=True) + EPS)
        nq_ = cq * rq
        d_cqn = _dot_nt(dqb_ref[...], wuq_ref[...])
        d_n = d_cqn * gq_ref[...]
        dcq_ref[...] = (rq * (d_n - nq_ * jnp.mean(d_n * nq_, axis=-1, keepdims=True))).astype(BF16)

        ckv = ckv_ref[...]
        rkv = lax.rsqrt(jnp.mean(ckv * ckv, axis=-1, keepdims=True) + EPS)
        nkv = ckv * rkv
        d_ckvn = _dot_nt(dkb_ref[...], wuk_ref[...]) + _dot_nt(dvb, wuv_ref[...])
        d_n2 = d_ckvn * gkv_ref[...]
        dckv_ref[...] = (rkv * (d_n2 - nkv * jnp.mean(d_n2 * nkv, axis=-1, keepdims=True))).astype(BF16)

        @pl.when(i == 0)
        def _():
            small_ref[...] = jnp.zeros_like(small_ref)

        small_ref[0:1, :] += jnp.sum(d_cqn * nq_, axis=0, keepdims=True)
        small_ref[1:2, :KV_LORA] += jnp.sum(d_ckvn * nkv, axis=0, keepdims=True)

    def row(width, idx=0):
        return pl.BlockSpec((TM_IO, width), lambda i: (i, idx))

    def full(a):
        return pl.BlockSpec(a.shape, lambda i: (0, 0))

    b16 = lambda w: pltpu.HBM((s, w), BF16)
    return pl.pallas_call(
        body, name="mla_prep_bwd", grid=(s // TM_IO,),
        in_specs=[row(1024), pl.BlockSpec((1024, TM_IO), lambda i: (0, i)), pl.BlockSpec((D_GRP, TM_IO), lambda i: (0, i)),
                  row(Q_LORA, 4), row(KV_LORA, 10), full(gq), full(gkv),
                  full(wuq), full(wuk), full(wuv), row(LANES), row(LANES)],
        out_specs=(row(Q_LORA), row(KV_LORA), row(LANES), row(1024), row(1024), row(D_GRP),
                   pl.BlockSpec((8, Q_LORA), lambda i: (0, 0))),
        out_shape=(b16(Q_LORA), b16(KV_LORA), b16(LANES), b16(1024), b16(1024), b16(D_GRP),
                   jax.ShapeDtypeStruct((8, Q_LORA), F32)),
        compiler_params=_params(("arbitrary",), 24),
    )(*_hbm(dqp, dkp, dvv, rest, rest), gq, gkv, wuq, wuk, wuv, cos_t, sin_t)


def _in_bwd(x, g, dx1, pieces, w, sums):
    s = x.shape[0]
    steps = s // TM_IO
    widths = [a.shape[1] for a in pieces]
    offs = [sum(widths[:k]) for k in range(len(widths))]
    n_pc, n_op = len(pieces), len(sums)
    ride_in, ride_out, ride_shape, ride_sems = _chip_specs(sums)

    def body(x_ref, g_ref, dx1_ref, *refs):
        piece_refs = refs[:n_pc]
        w_ref = refs[n_pc]
        s_refs = refs[n_pc + 1:n_pc + 1 + n_op]
        dx_ref, small_ref = refs[n_pc + 1 + n_op:n_pc + 3 + n_op]
        l_refs = refs[n_pc + 3 + n_op:n_pc + 3 + 2 * n_op]
        ssem, rsem = refs[n_pc + 3 + 2 * n_op:]
        i = pl.program_id(0)

        @pl.when(i == 0)
        def _():
            for cp in _chip_copies(s_refs, l_refs, ssem, rsem):
                cp.start()

        dh = jnp.zeros((TM_IO, D_MODEL), F32)
        for pr, off, wd in zip(piece_refs, offs, widths):
            dh = dh + _dot_nt(pr[...], w_ref[:, off:off + wd])
        xv = x_ref[...]
        r = lax.rsqrt(jnp.mean(xv * xv, axis=-1, keepdims=True) + EPS)
        n = xv * r
        d_n = dh * g_ref[...]
        dx_ref[...] = dx1_ref[...] + r * (d_n - n * jnp.mean(d_n * n, axis=-1, keepdims=True))

        @pl.when(i == 0)
        def _():
            small_ref[...] = jnp.zeros_like(small_ref)

        small_ref[0:1, :] += jnp.sum(dh * n, axis=0, keepdims=True)

        @pl.when(i == steps - 1)
        def _():
            for cp in _chip_copies(s_refs, l_refs, ssem, rsem):
                cp.wait()

    def row(width):
        return pl.BlockSpec((TM_IO, width), lambda i: (i, 0))

    return pl.pallas_call(
        body, name="in_bwd", grid=(steps,),
        in_specs=[row(D_MODEL), pl.BlockSpec((1, D_MODEL), lambda i: (0, 0)), row(D_MODEL)]
        + [row(wd) for wd in widths] + [pl.BlockSpec(w.shape, lambda i: (0, 0))] + ride_in,
        out_specs=[row(D_MODEL), pl.BlockSpec((8, D_MODEL), lambda i: (0, 0))] + ride_out,
        out_shape=[pltpu.HBM((s, D_MODEL), F32), jax.ShapeDtypeStruct((8, D_MODEL), F32)]
        + ride_shape,
        scratch_shapes=ride_sems,
        compiler_params=_params(("arbitrary",), 40),
    )(*_hbm(x), g, *_hbm(dx1, *pieces), w, *sums)


def _tn_matmul(a, b, name, blocked=False):
    s, k = a.shape
    n = b.shape[1]
    ts = min(s, TS_DW)
    tn = n if blocked else min(n, 512)
    steps = s // ts

    def body(a_ref, b_ref, o_ref):
        t = pl.program_id(1)

        @pl.when(t == 0)
        def _():
            o_ref[...] = jnp.zeros_like(o_ref)

        prod = _dot_tn(a_ref[...], b_ref[...])
        if blocked:
            for j in range(n // LANES):
                o_ref[j] += prod[:, j * LANES:(j + 1) * LANES]
        else:
            o_ref[...] += prod

    if blocked:
        out_spec = pl.BlockSpec((n // LANES, k, LANES), lambda j, t: (0, 0, 0))
        out_shape = pltpu.HBM((n // LANES, k, LANES), F32)
    else:
        out_spec = pl.BlockSpec((k, tn), lambda j, t: (0, j))
        out_shape = pltpu.HBM((k, n), F32)
    return pl.pallas_call(
        body, name=name, grid=(n // tn, steps),
        in_specs=[pl.BlockSpec((ts, k), lambda j, t: (t, 0)), pl.BlockSpec((ts, tn), lambda j, t: (t, j))],
        out_specs=out_spec, out_shape=out_shape,
        compiler_params=_params(("parallel", "arbitrary"), 20),
    )(*_hbm(a, b))


def _tn_matmul_multi(a, bs, name):
    s, k = a.shape
    widths = [b.shape[1] for b in bs]
    ts = min(s, TS_DW)

    def body(a_ref, *refs):
        b_refs, o_ref = refs[:-1], refs[-1]
        t = pl.program_id(0)

        @pl.when(t == 0)
        def _():
            o_ref[...] = jnp.zeros_like(o_ref)

        av = a_ref[...]
        off = 0
        for b_ref, wd in zip(b_refs, widths):
            o_ref[:, off:off + wd] += _dot_tn(av, b_ref[...])
            off += wd

    return pl.pallas_call(
        body, name=name, grid=(s // ts,),
        in_specs=[pl.BlockSpec((ts, k), lambda t: (t, 0))] + [pl.BlockSpec((ts, wd), lambda t: (t, 0)) for wd in widths],
        out_specs=pl.BlockSpec((k, sum(widths)), lambda t: (0, 0)),
        out_shape=pltpu.HBM((k, sum(widths)), F32),
        compiler_params=_params(("arbitrary",), 30),
    )(*_hbm(a, *bs))


IN_SHARD = 372
_IN_KERNEL_ORDER = ((0, 2048), (2464, 2976), (2048, 2432))
_IN_ROPE = (2432, 2464)
_IN_GRAD_SRC = ((0, 512, 0, 0), (512, 1024, 0, 512), (1024, 1536, 1, 0), (1536, 2048, 1, 512),
                (2048, 2304, 2, 512), (2304, 2432, 2, 768), (2432, 2464, 2, 960), (2464, 2976, 2, 0))


def _shard_cols(gath_in, lo, hi):
    out = []
    while lo < hi:
        j, a = divmod(lo, IN_SHARD)
        b = min(IN_SHARD, a + hi - lo)
        out.append(gath_in[j][:, a:b])
        lo += b - a
    return out


def _kernel_w_in(g_in):
    zc = lambda n: jnp.zeros((D_MODEL, n), BF16)
    parts = [pc for lo, hi in _IN_KERNEL_ORDER for pc in _shard_cols(g_in, lo, hi)]
    parts += [zc(64)] + _shard_cols(g_in, *_IN_ROPE) + [zc(32)]
    return jnp.concatenate(parts, axis=1)


def _kernel_weights(gath):
    g_uq, g_ukv, g_out, g_ple, g_pg = gath
    w_uq_p = jnp.pad(g_uq, ((0, 0), (0, 0), (0, 32))).transpose(1, 0, 2).reshape(Q_LORA, 1024)
    k_only = jnp.where(jnp.arange(LANES) < HEAD_DIM, g_ukv, jnp.zeros_like(g_ukv))
    w_uk_p = k_only.transpose(1, 0, 2).reshape(KV_LORA, 1024)
    w_uv = g_ukv[:, :, HEAD_DIM:].transpose(1, 0, 2).reshape(KV_LORA, D_GRP)
    w_ple = g_ple.transpose(1, 0, 2).reshape(PLE_DIM, D_MODEL)
    return (w_uq_p, w_uk_p, w_uv, g_out.reshape(D_MODEL, D_MODEL), w_ple, g_pg.reshape(D_MODEL, D_MODEL))


def _payload_in(d_cols):
    blocks = []
    for j in range(N_DEV):
        lo, hi = j * IN_SHARD, (j + 1) * IN_SHARD
        parts = []
        for o_lo, o_hi, idx, off in _IN_GRAD_SRC:
            a, b = max(lo, o_lo), min(hi, o_hi)
            if a < b:
                parts.append(d_cols[idx][:, off + a - o_lo:off + b - o_lo])
        blocks.append(jnp.concatenate(parts, axis=1))
    return jnp.stack(blocks)


def _payload_ukv(duk_blk, d_uv):
    dv_blk = d_uv.reshape(KV_LORA, N_HEADS, HEAD_DIM).transpose(1, 0, 2)
    return jnp.concatenate([duk_blk[:, :, :HEAD_DIM], dv_blk], axis=2)


def kernel(x, p, positions, norm_pre_g, w_in, q_norm_g, w_uq, kv_norm_g, w_ukv, sb_out_norm_g, mla_out_norm_g, w_out, norm_post_g, w_ple, ple_norm_g, w_ple_gate, b_ple_gate, loss_target, m_norm_pre_g, m_w_in, m_q_norm_g, m_w_uq, m_kv_norm_g, m_w_ukv, m_sb_out_norm_g, m_mla_out_norm_g, m_w_out, m_norm_post_g, m_w_ple, m_ple_norm_g, m_w_ple_gate, m_b_ple_gate, v_norm_pre_g, v_w_in, v_q_norm_g, v_w_uq, v_kv_norm_g, v_w_ukv, v_sb_out_norm_g, v_mla_out_norm_g, v_w_out, v_norm_post_g, v_w_ple, v_ple_norm_g, v_w_ple_gate, v_b_ple_gate):
    mats = (w_in, w_uq, w_ukv, w_out, w_ple, w_ple_gate)
    m_mats = (m_w_in, m_w_uq, m_w_ukv, m_w_out, m_w_ple, m_w_ple_gate)
    v_mats = (v_w_in, v_w_uq, v_w_ukv, v_w_out, v_w_ple, v_w_ple_gate)
    vecs = (norm_pre_g, q_norm_g, kv_norm_g, sb_out_norm_g, mla_out_norm_g, norm_post_g, ple_norm_g, b_ple_gate)
    m_vecs = (m_norm_pre_g, m_q_norm_g, m_kv_norm_g, m_sb_out_norm_g, m_mla_out_norm_g, m_norm_post_g,
              m_ple_norm_g, m_b_ple_gate)
    v_vecs = (v_norm_pre_g, v_q_norm_g, v_kv_norm_g, v_sb_out_norm_g, v_mla_out_norm_g, v_norm_post_g,
              v_ple_norm_g, v_b_ple_gate)

    shards = [a[0].astype(BF16) for a in mats]
    w_in_p = _kernel_w_in(_all_gather(shards[:1])[0])
    grad_x, reduced, vec_slab = _step(x[0], p[0, 0], positions[0], loss_target[0], *vecs, w_in_p, shards[1:])
    upd = [_adamw_matrix(own, l2, w, m, v, "adamw_%d" % o)
           for o, ((own, l2), w, m, v) in enumerate(zip(reduced, mats, m_mats, v_mats))]
    sm = _adamw_vectors(_slab_exchange(vec_slab), vecs, m_vecs, v_vecs)

    outs = []
    for kind in range(4):
        mat = [upd[o][kind] for o in range(len(mats))]
        vec = sm[1 + 8 * kind:9 + 8 * kind]
        outs += [vec[0], mat[0], vec[1], mat[1], vec[2], mat[2], vec[3], vec[4], mat[3], vec[5],
                 mat[4], vec[6], mat[5], vec[7]]
    return (sm[0][0, 0], grad_x[None], *outs)


def _step(xs, ps, pos, tgt, norm_pre_g, q_norm_g, kv_norm_g, sb_out_norm_g, mla_out_norm_g,
          norm_post_g, ple_norm_g, b_ple_gate, w_in_p, shards):
    s = xs.shape[0]
    place = jnp.stack([lax.axis_index("c"), 2 * lax.axis_index("x") + lax.axis_index("y")]).astype(jnp.int32)

    half = ROPE_DIM // 2
    freq = ROPE_THETA ** (-jnp.arange(half, dtype=F32) / half)
    ang = pos.astype(F32)[:, None] * freq
    cos, sin = jnp.cos(ang), jnp.sin(ang)
    cos_t = jnp.concatenate([jnp.ones((s, 64), F32), cos, cos, jnp.zeros((s, 32), F32)], axis=1)
    sin_t = jnp.concatenate([jnp.zeros((s, 64), F32), -sin, sin, jnp.zeros((s, 32), F32)], axis=1)
    seg = jnp.arange(D_GRP) // HEAD_DIM
    bd = (seg[:, None] == seg[None, :]).astype(BF16)

    qkv, rest, h_b, *gath = _in_proj(xs, norm_pre_g, w_in_p, shards)
    w_uq_p, w_uk_p, w_uv, f_out, f_ple, f_pg = _kernel_weights(gath)
    sb_o = _sb_fwd(qkv, 8)
    qp, kp, vv, cqn_b, ckvn_b = _mla_prep(rest, q_norm_g, kv_norm_g, w_uq_p, w_uk_p, w_uv, cos_t, sin_t)
    mla_o, lse = _mla_fwd(qp, kp, vv, 4)

    (dx1, d_sbo, d_mlo, d_sbg, d_mlg, x1_b, dgl_b, yc_b, dy_b, p_b, du_b, small_mid) = _mid(
        xs, ps, tgt, sb_o, mla_o, rest, sb_out_norm_g, mla_out_norm_g, f_out, norm_post_g,
        f_ple, ple_norm_g, f_pg, b_ple_gate, bd)
    pay_a = [_tn_matmul(yc_b, dy_b, "dw_out").reshape(N_DEV, 128, D_MODEL),
             _tn_matmul(p_b, du_b, "dw_ple", blocked=True),
             _tn_matmul(x1_b, dgl_b, "dw_pg").reshape(N_DEV, 128, D_MODEL)]
    dqp, dkp, dvv, *sib_a = _mla_bwd(qp, kp, vv, d_mlo, mla_o, lse, 4, pay_a)
    pair_a = _pair_sums(pay_a, sib_a, place, "grad_pair_sums_a")
    dq_sb, dk_sb, dv_sb, *landed_a = _sb_bwd(qkv, d_sbo, [sm for sm, _ in pair_a])
    dcq, dckv, dkr, dq_b, dk_b, dv_b, small_prep = _mla_prep_bwd(
        dqp, dkp, dvv, rest, q_norm_g, kv_norm_g, w_uq_p, w_uk_p, w_uv, cos_t, sin_t)
    pieces = [dq_sb, dk_sb, dv_sb, d_sbg, d_mlg, dcq, dckv, dkr]
    d_cols = [_tn_matmul_multi(h_b, pieces[0:2], "dw_in_0"), _tn_matmul_multi(h_b, pieces[2:4], "dw_in_1"),
              _tn_matmul_multi(h_b, pieces[4:8], "dw_in_2")]
    pay_b = [_payload_in(d_cols), _tn_matmul(cqn_b, dq_b, "dw_uq", blocked=True),
             _payload_ukv(_tn_matmul(ckvn_b, dk_b, "dw_uk", blocked=True), _tn_matmul(ckvn_b, dv_b, "dw_uv"))]
    pair_b = _pair_sums(pay_b, _pair_exchange(pay_b, "grad_pair_exchange"), place, "grad_pair_sums_b")
    grad_x, small_in, *landed_b = _in_bwd(xs, norm_pre_g, dx1, pieces, w_in_p, [sm for sm, _ in pair_b])
    reduced = [(own, l2) for (_, own), l2 in zip(pair_b + pair_a, landed_b + landed_a)]
    slab = jnp.concatenate([small_in[0:1], jnp.pad(small_prep[0:2], ((0, 0), (0, D_MODEL - Q_LORA))),
                            small_mid[3:8]], axis=0)
    return grad_x, reduced, slab
```

```python
import jax
import jax.numpy as jnp
from jax import lax
from jax.experimental import pallas as pl
from jax.experimental.pallas import tpu as pltpu

F32 = jnp.float32
BF16 = jnp.bfloat16
MESH = pl.DeviceIdType.MESH

N_DEV = 8
D_MODEL = 1024
N_HEADS = 8
HEAD_DIM = 64
D_GRP = N_HEADS * HEAD_DIM
Q_LORA = 256
KV_LORA = 128
ROPE_DIM = 32
PLE_DIM = 256
CHUNK_SHIFT = 6
ROPE_THETA = 10000.0
EPS = 1e-6
SB_SCALE = HEAD_DIM ** -0.5
MLA_SCALE = (HEAD_DIM + ROPE_DIM) ** -0.5
NEG = -1e30
LOG2_E = 1.4426950408889634
LN_2 = 0.6931471805599453
SB_CUTOFF = 110.0

ADAM_LR = 0.001
ADAM_B1 = 0.9
ADAM_B2 = 0.999
ADAM_EPS = 1e-08
ADAM_WD = 0.01
ADAM_STEP = 10

LANES = 128
TQ = 256
TK = 256
TM = 256
TM_IO = 512
TS_DW = 2048

D_IN_P = 3072

_NT = (((1,), (1,)), ((), ()))
_TN = (((0,), (0,)), ((), ()))


def _params(sem, vmem_mb):
    return pltpu.CompilerParams(dimension_semantics=sem, vmem_limit_bytes=vmem_mb << 20)


def _hbm(*arrays):
    return [pltpu.with_memory_space_constraint(a, pltpu.HBM) for a in arrays]


def _dot(a, b):
    return jnp.dot(a, b, preferred_element_type=F32)


def _dot_nt(a, b):
    return lax.dot_general(a, b, _NT, preferred_element_type=F32)


def _dot_tn(a, b):
    return lax.dot_general(a, b, _TN, preferred_element_type=F32)


def _hl_dot(a, b):
    hi = a.astype(BF16)
    lo = (a - hi.astype(F32)).astype(BF16)
    return _dot(hi, b) + _dot(lo, b)


def _sigmoid(x):
    return 1.0 / (1.0 + jnp.exp(-x))


def _rope_swap(x, lane):
    left = pltpu.roll(x, LANES - 16, axis=1)
    right = pltpu.roll(x, 16, axis=1)
    lo = (lane >= 64) & (lane < 80)
    hi = (lane >= 80) & (lane < 96)
    return jnp.where(lo, left, jnp.where(hi, right, 0.0))


def _two_level_gather(x_refs, out_refs, send_sems, recv_sems, local_sems):
    x, y, c = lax.axis_index("x"), lax.axis_index("y"), lax.axis_index("c")
    me, sibling = (x, y, c), (x, y, 1 - c)
    chips = [(1 - x, y), (x, 1 - y), (1 - x, 1 - y)]
    ops = range(len(x_refs))

    def slot(o, px, py, pc):
        return out_refs[o].at[4 * px + 2 * py + pc]

    def copy(o, k, block, to, src=None):
        return pltpu.make_async_remote_copy(
            src_ref=slot(o, *block) if src is None else src, dst_ref=slot(o, *block),
            send_sem=send_sems.at[o, k], recv_sem=recv_sems.at[o, k],
            device_id=to, device_id_type=MESH)

    def mine():
        return [pltpu.make_async_copy(x_refs[o], slot(o, *me), local_sems.at[o]) for o in ops]

    def first():
        return ([copy(o, 0, me, sibling, src=x_refs[o]) for o in ops]
                + [copy(o, 1 + j, me, (*chip, c), src=x_refs[o]) for j, chip in enumerate(chips) for o in ops])

    def start():
        for cp in mine() + first():
            cp.start()

    def finish():
        passed = []
        for j, chip in enumerate(chips):
            for o in ops:
                copy(o, 1 + j, (*chip, c), me).wait_recv()
                passed.append(copy(o, 4 + j, (*chip, c), sibling))
                passed[-1].start()
        for o in ops:
            copy(o, 0, sibling, me).wait_recv()
        for j, chip in enumerate(chips):
            for o in ops:
                copy(o, 4 + j, (*chip, 1 - c), me).wait_recv()
        for cp in first() + passed:
            cp.wait_send()
        for cp in mine():
            cp.wait()

    return start, finish


def _gather_sems(n_op):
    return [pltpu.SemaphoreType.DMA((n_op, 7)), pltpu.SemaphoreType.DMA((n_op, 7)),
            pltpu.SemaphoreType.DMA((n_op,))]


def _all_gather(shards):
    n_op = len(shards)

    def body(*refs):
        start, finish = _two_level_gather(refs[:n_op], refs[n_op:2 * n_op], *refs[2 * n_op:])
        start()
        finish()

    any_spec = pl.BlockSpec(memory_space=pl.ANY)
    return pl.pallas_call(
        body, name="weight_all_gather",
        out_shape=[jax.ShapeDtypeStruct((N_DEV,) + a.shape, a.dtype) for a in shards],
        in_specs=[any_spec] * n_op, out_specs=[any_spec] * n_op, scratch_shapes=_gather_sems(n_op),
        compiler_params=pltpu.CompilerParams(vmem_limit_bytes=4 << 20),
    )(*shards)


def _pair_copies(g_refs, l_refs, ssem, rsem):
    x, y, c = lax.axis_index("x"), lax.axis_index("y"), lax.axis_index("c")
    copies = []
    for o in range(len(g_refs)):
        for chip in range(4):
            copies.append(pltpu.make_async_remote_copy(
                src_ref=g_refs[o].at[2 * chip + (1 - c)], dst_ref=l_refs[o].at[chip],
                send_sem=ssem.at[o, chip], recv_sem=rsem.at[o, chip],
                device_id=(x, y, 1 - c), device_id_type=MESH))
    return copies


def _pair_specs(pays):
    n_op = len(pays)
    any_spec = pl.BlockSpec(memory_space=pl.ANY)
    return ([any_spec] * n_op, [any_spec] * n_op,
            [jax.ShapeDtypeStruct((4,) + a.shape[1:], F32) for a in pays],
            [pltpu.SemaphoreType.DMA((n_op, 4)), pltpu.SemaphoreType.DMA((n_op, 4))])


def _pair_exchange(pays, name):
    n_op = len(pays)
    in_specs, out_specs, out_shape, sems = _pair_specs(pays)

    def body(*refs):
        copies = _pair_copies(refs[:n_op], refs[n_op:2 * n_op], *refs[2 * n_op:])
        for cp in copies:
            cp.start()
        for cp in copies:
            cp.wait()

    return pl.pallas_call(body, name=name, out_shape=out_shape, in_specs=in_specs, out_specs=out_specs,
                          scratch_shapes=sems,
                          compiler_params=pltpu.CompilerParams(vmem_limit_bytes=4 << 20))(*pays)


def _slab_exchange(small):
    sr, n = small.shape

    def body(s_ref, sland_ref, ssem, rsem, lsem):
        x, y, c = lax.axis_index("x"), lax.axis_index("y"), lax.axis_index("c")
        me = 4 * x + 2 * y + c
        copies = []
        for k in range(1, N_DEV):
            peer = (1 - x if (k >> 2) & 1 else x, 1 - y if (k >> 1) & 1 else y, 1 - c if k & 1 else c)
            copies.append(pltpu.make_async_remote_copy(
                src_ref=s_ref, dst_ref=sland_ref.at[me], send_sem=ssem.at[k], recv_sem=rsem.at[k],
                device_id=peer, device_id_type=MESH))
        own = pltpu.make_async_copy(s_ref, sland_ref.at[me], lsem)
        own.start()
        for cp in copies:
            cp.start()
        for cp in copies:
            cp.wait()
        own.wait()

    any_spec = pl.BlockSpec(memory_space=pl.ANY)
    return pl.pallas_call(
        body, name="grad_slab_exchange", out_shape=jax.ShapeDtypeStruct((N_DEV, sr, n), F32),
        in_specs=[any_spec], out_specs=any_spec,
        scratch_shapes=[pltpu.SemaphoreType.DMA((N_DEV,)), pltpu.SemaphoreType.DMA((N_DEV,)),
                        pltpu.SemaphoreType.DMA],
        compiler_params=pltpu.CompilerParams(vmem_limit_bytes=4 << 20),
    )(small)


def _pair_sums(pays, landed, place, name):
    n = len(pays)
    dims = [p.shape[1:] for p in pays]

    def body(place_ref, *refs):
        g_refs, l_refs, s_refs, own_refs = refs[:n], refs[n:2 * n], refs[2 * n:3 * n], refs[3 * n:]
        i = pl.program_id(0)
        for o in range(n):
            tot = g_refs[o][...] + l_refs[o][...]
            s_refs[o][...] = tot.astype(BF16)

            @pl.when(i == place_ref[1])
            def _(o=o, tot=tot):
                own_refs[o][...] = tot

    grid_spec = pltpu.PrefetchScalarGridSpec(
        num_scalar_prefetch=1, grid=(4,),
        in_specs=[pl.BlockSpec((None, r, c), lambda i, pr: (2 * i + pr[0], 0, 0)) for r, c in dims]
        + [pl.BlockSpec((None, r, c), lambda i, pr: (i, 0, 0)) for r, c in dims],
        out_specs=[pl.BlockSpec((None, r, c), lambda i, pr: (i, 0, 0)) for r, c in dims]
        + [pl.BlockSpec((r, c), lambda i, pr: (0, 0)) for r, c in dims])
    out = pl.pallas_call(
        body, name=name, grid_spec=grid_spec,
        out_shape=[jax.ShapeDtypeStruct((4, r, c), BF16) for r, c in dims]
        + [jax.ShapeDtypeStruct((r, c), F32) for r, c in dims],
        compiler_params=_params(("arbitrary",), 16),
    )(place, *pays, *landed)
    return list(zip(out[:n], out[n:]))


def _chip_copies(s_refs, l_refs, ssem, rsem):
    x, y, c = lax.axis_index("x"), lax.axis_index("y"), lax.axis_index("c")
    copies = []
    for rel in range(1, 4):
        px = 1 - x if rel & 2 else x
        py = 1 - y if rel & 1 else y
        for o in range(len(s_refs)):
            copies.append(pltpu.make_async_remote_copy(
                src_ref=s_refs[o].at[2 * px + py], dst_ref=l_refs[o].at[rel - 1],
                send_sem=ssem.at[o, rel - 1], recv_sem=rsem.at[o, rel - 1],
                device_id=(px, py, c), device_id_type=MESH))
    return copies


def _chip_specs(sums):
    n_op = len(sums)
    any_spec = pl.BlockSpec(memory_space=pl.ANY)
    return ([any_spec] * n_op, [any_spec] * n_op,
            [jax.ShapeDtypeStruct((3,) + a.shape[1:], BF16) for a in sums],
            [pltpu.SemaphoreType.DMA((n_op, 3)), pltpu.SemaphoreType.DMA((n_op, 3))])


def _adamw_math(g, w, m, v):
    mn = ADAM_B1 * m + (1.0 - ADAM_B1) * g
    vn = ADAM_B2 * v + (1.0 - ADAM_B2) * (g * g)
    m_hat = mn / (1.0 - ADAM_B1 ** ADAM_STEP)
    v_hat = vn / (1.0 - ADAM_B2 ** ADAM_STEP)
    return -ADAM_LR * (m_hat / (jnp.sqrt(v_hat) + ADAM_EPS) + ADAM_WD * w), mn, vn


def _adamw_matrix(own, landed, w, m, v, name):
    _, r, c = w.shape
    cp = own.shape[1]
    br = min(r, 256)

    def body(own_ref, l_ref, w_ref, m_ref, v_ref, g_out, d_out, m_out, v_out):
        g = own_ref[...]
        for k in range(3):
            g = g + l_ref[k].astype(F32)
        g = g[:, :c]
        g_out[...] = g
        d_out[...], m_out[...], v_out[...] = _adamw_math(g, w_ref[...], m_ref[...], v_ref[...])

    row = pl.BlockSpec((None, br, c), lambda i: (0, i, 0))
    shp = jax.ShapeDtypeStruct((1, r, c), F32)
    return pl.pallas_call(
        body, name=name, grid=(r // br,),
        in_specs=[pl.BlockSpec((br, cp), lambda i: (i, 0)), pl.BlockSpec((3, br, cp), lambda i: (0, i, 0)),
                  row, row, row],
        out_specs=(row, row, row, row), out_shape=(shp, shp, shp, shp),
        compiler_params=_params(("parallel",), 12),
    )(own, landed, w, m, v)


_VEC_PLACE = ((0, 0), (1, 0), (2, 0), (3, 0), (3, D_GRP), (4, 0), (5, 0), (6, 0))


def _adamw_vectors(sland, ws, ms, vs):
    nv = len(ws)

    def body(l_ref, *refs):
        w_refs, m_refs, v_refs = refs[:nv], refs[nv:2 * nv], refs[2 * nv:3 * nv]
        loss_ref = refs[3 * nv]
        outs = refs[3 * nv + 1:]
        g_all = l_ref[0]
        for j in range(1, N_DEV):
            g_all = g_all + l_ref[j]
        loss_ref[...] = jnp.sum(g_all[7:8, :], axis=1, keepdims=True)
        for k, (row, lane0) in enumerate(_VEC_PLACE):
            n = w_refs[k].shape[1]
            g = g_all[row:row + 1, lane0:lane0 + n]
            d, mn, vn = _adamw_math(g, w_refs[k][...], m_refs[k][...], v_refs[k][...])
            outs[k][...] = g
            outs[nv + k][...] = d
            outs[2 * nv + k][...] = mn
            outs[3 * nv + k][...] = vn

    def whole(shape):
        return pl.BlockSpec(shape, lambda i: (0,) * len(shape))

    shapes = [jax.ShapeDtypeStruct(w.shape, F32) for w in ws]
    return pl.pallas_call(
        body, name="adamw_vectors", grid=(1,),
        in_specs=[whole(sland.shape)] + [whole(w.shape) for w in ws] * 3,
        out_specs=[whole((1, 1))] + [whole(w.shape) for w in ws] * 4,
        out_shape=[jax.ShapeDtypeStruct((1, 1), F32)] + shapes * 4,
        compiler_params=_params(("arbitrary",), 4),
    )(sland, *ws, *ms, *vs)


def _in_proj(x, g, w, shards):
    s = x.shape[0]
    n_op = len(shards)
    steps = s // TM_IO

    def body(x_ref, g_ref, w_ref, *refs):
        shard_refs = refs[:n_op]
        qkv_ref, rest_ref, h_ref = refs[n_op:n_op + 3]
        gath_refs = refs[n_op + 3:2 * n_op + 3]
        start, finish = _two_level_gather(shard_refs, gath_refs, *refs[2 * n_op + 3:])
        i = pl.program_id(0)

        @pl.when(i == 0)
        def _():
            start()

        xv = x_ref[...]
        r = lax.rsqrt(jnp.mean(xv * xv, axis=-1, keepdims=True) + EPS)
        h = ((xv * r) * g_ref[...]).astype(BF16)
        h_ref[...] = h
        qkv_ref[...] = _dot(h, w_ref[:, :1536]).astype(BF16)
        rest_ref[...] = _dot(h, w_ref[:, 1536:])

        @pl.when(i == steps - 1)
        def _():
            finish()

    any_spec = pl.BlockSpec(memory_space=pl.ANY)
    return pl.pallas_call(
        body, name="in_proj", grid=(steps,),
        in_specs=[pl.BlockSpec((TM_IO, D_MODEL), lambda i: (i, 0)),
                  pl.BlockSpec((1, D_MODEL), lambda i: (0, 0)),
                  pl.BlockSpec((D_MODEL, D_IN_P), lambda i: (0, 0))] + [any_spec] * n_op,
        out_specs=[pl.BlockSpec((TM_IO, 1536), lambda i: (i, 0)),
                   pl.BlockSpec((TM_IO, 1536), lambda i: (i, 0)),
                   pl.BlockSpec((TM_IO, D_MODEL), lambda i: (i, 0))] + [any_spec] * n_op,
        out_shape=[pltpu.HBM((s, 1536), BF16), pltpu.HBM((s, 1536), F32),
                   pltpu.HBM((s, D_MODEL), BF16)]
        + [jax.ShapeDtypeStruct((N_DEV,) + a.shape, a.dtype) for a in shards],
        scratch_shapes=_gather_sems(n_op),
        compiler_params=_params(("arbitrary",), 32),
    )(x, g, w, *shards)


def _mla_prep(rest, gq, gkv, wuq, wuk, wuv, cos_t, sin_t):
    s = rest.shape[0]

    def body(cq_ref, ckv_ref, kr_ref, gq_ref, gkv_ref, wuq_ref, wuk_ref, wuv_ref, c_ref, s_ref,
             qp_ref, kp_ref, vv_ref, cqn_ref, ckvn_ref):
        lane = lax.broadcasted_iota(jnp.int32, (1, LANES), 1)
        cos_v, sin_v = c_ref[...], s_ref[...]
        cq = cq_ref[...]
        rq = lax.rsqrt(jnp.mean(cq * cq, axis=-1, keepdims=True) + EPS)
        cqn = ((cq * rq) * gq_ref[...]).astype(BF16)
        cqn_ref[...] = cqn
        q = _dot(cqn, wuq_ref[...])
        ckv = ckv_ref[...]
        rkv = lax.rsqrt(jnp.mean(ckv * ckv, axis=-1, keepdims=True) + EPS)
        ckvn = ((ckv * rkv) * gkv_ref[...]).astype(BF16)
        ckvn_ref[...] = ckvn
        kn = _dot(ckvn, wuk_ref[...])
        vv_ref[...] = _dot(ckvn, wuv_ref[...]).astype(BF16)
        kr = kr_ref[...]
        kr_roped = kr * cos_v + _rope_swap(kr, lane) * sin_v
        for h in range(N_HEADS):
            sl = slice(h * LANES, (h + 1) * LANES)
            qh = q[:, sl]
            qp_ref[:, sl] = (qh * cos_v + _rope_swap(qh, lane) * sin_v).astype(BF16)
            kp_ref[:, sl] = (kn[:, sl] + kr_roped).astype(BF16)

    def row(width, idx):
        return pl.BlockSpec((TM_IO, width), lambda i: (i, idx))

    def full(a):
        return pl.BlockSpec(a.shape, lambda i: (0, 0))

    return pl.pallas_call(
        body, name="mla_prep", grid=(s // TM_IO,),
        in_specs=[row(Q_LORA, 4), row(KV_LORA, 10), row(LANES, 11), full(gq), full(gkv),
                  full(wuq), full(wuk), full(wuv), row(LANES, 0), row(LANES, 0)],
        out_specs=(row(1024, 0), row(1024, 0), row(D_GRP, 0), row(Q_LORA, 0), row(KV_LORA, 0)),
        out_shape=(pltpu.HBM((s, 1024), BF16), pltpu.HBM((s, 1024), BF16),
                   pltpu.HBM((s, D_GRP), BF16), pltpu.HBM((s, Q_LORA), BF16),
                   pltpu.HBM((s, KV_LORA), BF16)),
        compiler_params=_params(("parallel",), 13),
    )(*_hbm(rest, rest, rest), gq, gkv, wuq, wuk, wuv, cos_t, sin_t)


def _sb_live(n, qi, carries):
    top = carries[0]
    for c in carries[1:]:
        top = jnp.maximum(top, c)
    return jnp.logical_and(n < qi, jnp.max(top) > -SB_CUTOFF)


def _sb_fwd(qkv, hb):
    s = qkv.shape[0]

    def body(q_ref, k_ref, v_ref, o_ref, acc):
        qi = pl.program_id(1)
        lane = lax.broadcasted_iota(jnp.int32, (1, LANES), 1)
        is_a = lane < HEAD_DIM
        pair = lambda h: slice((h // 2) * LANES, (h // 2 + 1) * LANES)
        q_h = []
        for h in range(hb):
            qs = q_ref[:, pair(h)] * SB_SCALE
            mine = is_a if h % 2 == 0 else jnp.logical_not(is_a)
            q_h.append(jnp.where(mine, qs, jnp.zeros_like(qs)))
        r_i = lax.broadcasted_iota(jnp.int32, (TQ, TK), 0)
        c_i = lax.broadcasted_iota(jnp.int32, (TQ, TK), 1)
        past = c_i < r_i
        upper = (r_i > c_i).astype(BF16)
        acc[...] = jnp.zeros_like(acc)

        def tile(j, carries, diag):
            ks = pl.ds(pl.multiple_of(j * TK, TK), TK)
            zs = [_dot_nt(q_h[h], k_ref[ks, pair(h)]) for h in range(hb)]
            if diag:
                zs = [jnp.where(past, z, NEG) for z in zs]
            lfs = [-(jnp.maximum(z, 0.0) + jnp.log(1.0 + jnp.exp(-jnp.abs(z)))) for z in zs]
            sufs = [_hl_dot(lfs[h], upper) for h in range(hb)]
            out = []
            for h in range(hb):
                w = jnp.exp(zs[h] + lfs[h] + (sufs[h] + carries[h]))
                acc[h] += _dot(w.astype(BF16), v_ref[ks, pair(h)])
                out.append(carries[h] + jnp.sum(lfs[h], axis=1, keepdims=True))
            return tuple(out)

        zero = jnp.zeros((TQ, 1), F32)
        carries = tile(qi, (zero,) * hb, True)

        def step(st):
            return (st[0] + 1,) + tile(qi - 1 - st[0], st[1:], False)

        lax.while_loop(lambda st: _sb_live(st[0], qi, st[1:]), step, (0,) + carries)
        for pr in range(hb // 2):
            o_ref[:, pr * LANES:(pr + 1) * LANES] = jnp.where(is_a, acc[2 * pr], acc[2 * pr + 1])

    width = hb * HEAD_DIM
    nb = D_GRP // width
    slab = lambda part: pl.BlockSpec((s, width), lambda g, qi: (0, part * nb + g))
    blk = pl.BlockSpec((TQ, width), lambda g, qi: (qi, g))
    return pl.pallas_call(
        body, name="sb_fwd", grid=(nb, s // TQ),
        in_specs=[blk, slab(1), slab(2)], out_specs=blk,
        out_shape=pltpu.HBM((s, D_GRP), F32),
        scratch_shapes=[pltpu.VMEM((hb, TQ, LANES), F32)],
        compiler_params=_params(("arbitrary", "arbitrary"), 28),
    )(*_hbm(qkv, qkv, qkv))


def _sb_bwd(qkv, d_o, sums):
    s = qkv.shape[0]
    nq = s // TQ
    nk = s // TK
    n_op = len(sums)
    ride_in, ride_out, ride_shape, ride_sems = _chip_specs(sums)

    def body(q_ref, k_ref, v_ref, do_ref, *refs):
        s_refs = refs[:n_op]
        dq_ref, dk_ref, dv_ref = refs[n_op:n_op + 3]
        l_refs = refs[n_op + 3:2 * n_op + 3]
        x1s, bts, dqacc, dkacc, dvacc, ssem, rsem = refs[2 * n_op + 3:]
        qi = pl.program_id(1)
        first_step = jnp.logical_and(pl.program_id(0) == 0, qi == 0)
        last_step = jnp.logical_and(pl.program_id(0) == pl.num_programs(0) - 1, qi == nq - 1)

        @pl.when(first_step)
        def _():
            for cp in _chip_copies(s_refs, l_refs, ssem, rsem):
                cp.start()

        lane = lax.broadcasted_iota(jnp.int32, (1, LANES), 1)
        is_a = lane < HEAD_DIM

        @pl.when(qi == 0)
        def _():
            dkacc[...] = jnp.zeros_like(dkacc)
            dvacc[...] = jnp.zeros_like(dvacc)

        qs = q_ref[...] * SB_SCALE
        zq = jnp.zeros_like(qs)
        qs_x = (jnp.where(is_a, qs, zq), jnp.where(is_a, zq, qs))
        dob = do_ref[...].astype(BF16)
        do_x = (jnp.where(is_a, dob, zq), jnp.where(is_a, zq, dob))
        r_i = lax.broadcasted_iota(jnp.int32, (TQ, TK), 0)
        c_i = lax.broadcasted_iota(jnp.int32, (TQ, TK), 1)
        past = c_i < r_i
        upper = (r_i > c_i).astype(BF16)
        upper_incl = (r_i >= c_i).astype(BF16)
        dqacc[...] = jnp.zeros_like(dqacc)
        both = ((0, 0), (0, 1), (1, 0), (1, 1))

        def tiles(n):
            j_hi = qi - 2 * n
            lo_ok = j_hi >= 1
            j_lo = jnp.maximum(j_hi - 1, 0)
            ks = (pl.ds(pl.multiple_of(j_hi * TK, TK), TK), pl.ds(pl.multiple_of(j_lo * TK, TK), TK))
            return j_hi, lo_ok, j_lo, ks

        def sweep(n, carries):
            j_hi, lo_ok, j_lo, ks = tiles(n)
            slot = (j_hi, jnp.where(lo_ok, j_lo, nk))
            valid = (jnp.logical_or(past, j_hi < qi), lo_ok)
            z = {th: jnp.where(valid[th[0]], _dot_nt(qs_x[th[1]], k_ref[ks[th[0]], :]), NEG) for th in both}
            log_b, lf_sum, suf = {}, {}, {}
            for th in both:
                lf = -(jnp.maximum(z[th], 0.0) + jnp.log(1.0 + jnp.exp(-jnp.abs(z[th]))))
                log_b[th] = z[th] + lf
                lf_sum[th] = jnp.sum(lf, axis=1, keepdims=True)
                suf[th] = _hl_dot(lf, upper)
            c, g_in = {}, {}
            for h in range(2):
                c[0, h], g_in[0, h] = carries[2 * h], carries[2 * h + 1]
                c[1, h] = c[0, h] + lf_sum[0, h]
            d_a = {th: _dot_nt(do_x[th[1]], v_ref[ks[th[0]], :]) for th in both}
            a_b, g, g_sum, sg = {}, {}, {}, {}
            for th in both:
                a = jnp.exp(log_b[th] + (suf[th] + c[th]))
                a_b[th] = a.astype(BF16)
                g[th] = a * d_a[th]
                g_sum[th] = jnp.sum(g[th], axis=1, keepdims=True)
                sg[th] = _hl_dot(g[th], upper_incl)
            for h in range(2):
                g_in[1, h] = g_in[0, h] + g_sum[0, h]
            for th in both:
                t, h = th
                beta = jnp.exp(log_b[th])
                x1s[slot[t], h] = g[th] * (1.0 - beta) + beta * (sg[th] + g_in[th])
                bts[slot[t], h] = beta
                dvacc[ks[t], :] += _dot_tn(a_b[th], do_x[h])
            out = []
            for h in range(2):
                out.append(c[1, h] + lf_sum[1, h])
                out.append(g_in[1, h] + g_sum[1, h])
            return tuple(out)

        zero = jnp.zeros((TQ, 1), F32)
        first = sweep(0, (zero, zero, zero, zero))

        def more(st):
            return jnp.logical_and(2 * st[0] <= qi, jnp.max(jnp.maximum(st[1], st[3])) > -SB_CUTOFF)

        swept = lax.while_loop(more, lambda st: (st[0] + 1,) + sweep(st[0], st[1:]), (1,) + first)
        g_tot = (swept[2], swept[4])

        def apply(n, carry):
            j_hi, lo_ok, j_lo, ks = tiles(n)

            def one(j, kslice):
                for h in range(2):
                    dz = (x1s[j, h] - bts[j, h] * g_tot[h]).astype(BF16)
                    dqacc[h] += _dot(dz, k_ref[kslice, :])
                    dkacc[kslice, :] += _dot_tn(dz, qs_x[h])

            one(j_hi, ks[0])

            @pl.when(lo_ok)
            def _():
                one(j_lo, ks[1])

            return carry

        lax.fori_loop(0, swept[0], apply, 0)
        dq_ref[...] = (jnp.where(is_a, dqacc[0], dqacc[1]) * SB_SCALE).astype(BF16)

        @pl.when(qi == nq - 1)
        def _():
            dk_ref[...] = dkacc[...].astype(BF16)
            dv_ref[...] = dvacc[...].astype(BF16)

        @pl.when(last_step)
        def _():
            for cp in _chip_copies(s_refs, l_refs, ssem, rsem):
                cp.wait()

    slab = lambda off: pl.BlockSpec((s, LANES), lambda p, qi: (0, off + p))
    blk = pl.BlockSpec((TQ, LANES), lambda p, qi: (qi, p))
    out_slab = pl.BlockSpec((s, LANES), lambda p, qi: (0, p))
    shp = pltpu.HBM((s, D_GRP), BF16)
    return pl.pallas_call(
        body, name="sb_bwd", grid=(4, nq),
        in_specs=[blk, slab(4), slab(8), blk] + ride_in,
        out_specs=[blk, out_slab, out_slab] + ride_out, out_shape=[shp, shp, shp] + ride_shape,
        scratch_shapes=[pltpu.VMEM((nk + 1, 2, TQ, TK), F32)] * 2
        + [pltpu.VMEM((2, TQ, LANES), F32), pltpu.VMEM((s, LANES), F32), pltpu.VMEM((s, LANES), F32)]
        + ride_sems,
        compiler_params=_params(("arbitrary", "arbitrary"), 44),
    )(*_hbm(qkv, qkv, qkv, d_o), *sums)


def _mla_fwd(qp, kp, vv, hb):
    s = qp.shape[0]
    c2 = MLA_SCALE * LOG2_E

    def body(q_ref, k_ref, v_ref, o_ref, lse_ref, vaug, mrun, mb, acc, zbuf):
        qi = pl.program_id(1)
        lane = lax.broadcasted_iota(jnp.int32, (1, LANES), 1)
        is_a = lane < HEAD_DIM

        @pl.when(qi == 0)
        def _():
            for h in range(hb):
                vp = v_ref[:, (h // 2) * LANES:(h // 2 + 1) * LANES]
                mine = is_a if h % 2 == 0 else jnp.logical_not(is_a)
                vaug[h] = jnp.where(mine, vp, jnp.ones_like(vp))

        r_i = lax.broadcasted_iota(jnp.int32, (TQ, TK), 0)
        c_i = lax.broadcasted_iota(jnp.int32, (TQ, TK), 1)
        visible = (c_i >> CHUNK_SHIFT) <= (r_i >> CHUNK_SHIFT)

        def key_rows(j):
            return pl.ds(pl.multiple_of(j * TK, TK), TK)

        def sweep(tiles):
            def loop(n, carry):
                tiles(((2 * n, False), (2 * n + 1, False)))
                return carry

            lax.fori_loop(0, qi // 2, loop, 0)

            @pl.when(qi % 2 == 1)
            def _():
                tiles(((qi - 1, False), (qi, True)))

            @pl.when(qi % 2 == 0)
            def _():
                tiles(((qi, True),))

        mrun[...] = jnp.full_like(mrun, NEG)

        def tiles_max(js):
            zs = [[_dot_nt(q_ref[:, h * LANES:(h + 1) * LANES], k_ref[key_rows(j), h * LANES:(h + 1) * LANES])
                   for h in range(hb)] for j, _ in js]
            for t, (j, diag) in enumerate(js):
                for h in range(hb):
                    z = jnp.where(visible, zs[t][h], NEG) if diag else zs[t][h]
                    zbuf[j, h] = z
                    mrun[h] = jnp.maximum(mrun[h], z)

        sweep(tiles_max)
        for h in range(hb):
            m = jnp.max(mrun[h], axis=1, keepdims=True) * c2
            mb[h] = jnp.broadcast_to(m, (TQ, TK))
        acc[...] = jnp.zeros_like(acc)

        def tiles_pv(js):
            ps = [[jnp.exp2((zbuf[j, h] * c2 - mb[h]).astype(BF16)) for h in range(hb)] for j, _ in js]
            for t, (j, _) in enumerate(js):
                for h in range(hb):
                    acc[h] += _dot(ps[t][h], vaug[h, key_rows(j), :])

        sweep(tiles_pv)
        for pr in range(hb // 2):
            a, b = 2 * pr, 2 * pr + 1
            psl = slice(pr * LANES, (pr + 1) * LANES)
            acc_a, acc_b = acc[a], acc[b]
            l_a = pltpu.roll(acc_a, HEAD_DIM, axis=1)
            l_b = pltpu.roll(acc_b, HEAD_DIM, axis=1)
            o_ref[:, psl] = jnp.where(is_a, acc_a * (1.0 / l_a), acc_b * (1.0 / l_b))
            lse_ref[:, psl] = jnp.where(is_a, mb[a, :, :LANES] * LN_2 + jnp.log(l_a),
                                        mb[b, :, :LANES] * LN_2 + jnp.log(l_b))

    blk = pl.BlockSpec((TQ, hb * HEAD_DIM), lambda g, qi: (qi, g))
    shp = pltpu.HBM((s, D_GRP), F32)
    return pl.pallas_call(
        body, name="mla_fwd", grid=(N_HEADS // hb, s // TQ),
        in_specs=[pl.BlockSpec((TQ, hb * LANES), lambda g, qi: (qi, g)),
                  pl.BlockSpec((s, hb * LANES), lambda g, qi: (0, g)),
                  pl.BlockSpec((s, hb * HEAD_DIM), lambda g, qi: (0, g))],
        out_specs=(blk, blk), out_shape=(shp, shp),
        scratch_shapes=[pltpu.VMEM((hb, s, LANES), BF16), pltpu.VMEM((hb, TQ, TK), F32),
                        pltpu.VMEM((hb, TQ, TK), F32), pltpu.VMEM((hb, TQ, LANES), F32),
                        pltpu.VMEM((s // TK, hb, TQ, TK), F32)],
        compiler_params=_params(("arbitrary", "arbitrary"), 44),
    )(*_hbm(qp, kp, vv))


def _mla_bwd(qp, kp, vv, d_o, o, lse, hb, pays):
    s = qp.shape[0]
    nq = s // TQ
    c2 = MLA_SCALE * LOG2_E
    n_op = len(pays)
    ride_in, ride_out, ride_shape, ride_sems = _pair_specs(pays)

    def body(q_ref, k_ref, v_ref, do_ref, o_ref, lse_ref, *refs):
        g_refs = refs[:n_op]
        dq_ref, dk_ref, dv_ref = refs[n_op:n_op + 3]
        l_refs = refs[n_op + 3:2 * n_op + 3]
        dqacc, lse_b, delta_b, q_t, do_t, ssem, rsem = refs[2 * n_op + 3:]
        qi = pl.program_id(1)

        @pl.when(jnp.logical_and(pl.program_id(0) == 0, qi == 0))
        def _():
            for cp in _pair_copies(g_refs, l_refs, ssem, rsem):
                cp.start()

        lane = lax.broadcasted_iota(jnp.int32, (1, LANES), 1)
        is_a = lane < HEAD_DIM

        @pl.when(qi == 0)
        def _():
            dk_ref[...] = jnp.zeros_like(dk_ref)
            dv_ref[...] = jnp.zeros_like(dv_ref)

        r_i = lax.broadcasted_iota(jnp.int32, (TQ, TK), 0)
        c_i = lax.broadcasted_iota(jnp.int32, (TQ, TK), 1)
        visible = (c_i >> CHUNK_SHIFT) <= (r_i >> CHUNK_SHIFT)
        do_x = []
        for h in range(hb):
            psl = slice((h // 2) * LANES, (h // 2 + 1) * LANES)
            mine = is_a if h % 2 == 0 else jnp.logical_not(is_a)
            d_o = do_ref[:, psl]
            delta = jnp.sum(jnp.where(mine, d_o * o_ref[:, psl], 0.0), axis=1, keepdims=True)
            lse_h = jnp.sum(jnp.where(lane == (h % 2) * HEAD_DIM, lse_ref[:, psl], 0.0), axis=1, keepdims=True)
            lse_b[h] = jnp.broadcast_to(lse_h * LOG2_E, (TQ, TK))
            delta_b[h] = jnp.broadcast_to(delta, (TQ, TK))
            do_h = jnp.where(mine, d_o, 0.0)
            do_x.append(do_h.astype(BF16))
            do_t[h] = do_h.T.astype(BF16)
            q_t[h] = q_ref[:, h * LANES:(h + 1) * LANES].astype(F32).T.astype(BF16)
        dqacc[...] = jnp.zeros_like(dqacc)

        head = lambda h: slice(h * LANES, (h + 1) * LANES)
        pair = lambda h: slice((h // 2) * LANES, (h // 2 + 1) * LANES)

        def tiles(js):
            th = [(j, diag, pl.ds(pl.multiple_of(j * TK, TK), TK), h) for j, diag in js for h in range(hb)]
            zs = [_dot_nt(q_ref[:, head(h)], k_ref[ks, head(h)]) for _, _, ks, h in th]
            dps = [_dot_nt(do_x[h], v_ref[ks, pair(h)]) for _, _, ks, h in th]
            for i, (j, diag, ks, h) in enumerate(th):
                e = zs[i] * c2 - lse_b[h]
                if diag:
                    e = jnp.where(visible, e, NEG)
                p = jnp.exp2(e)
                ds = (p * (dps[i] - delta_b[h]) * MLA_SCALE).astype(BF16)
                dqacc[h] += _dot(ds, k_ref[ks, head(h)])
                dk_ref[head(h), ks] += _dot(q_t[h], ds)
                dv_ref[pair(h), ks] += _dot(do_t[h], p.astype(BF16))

        def loop(n, c):
            tiles(((2 * n, False), (2 * n + 1, False)))
            return c

        lax.fori_loop(0, qi // 2, loop, 0)

        @pl.when(qi % 2 == 1)
        def _():
            tiles(((qi - 1, False), (qi, True)))

        @pl.when(qi % 2 == 0)
        def _():
            tiles(((qi, True),))

        for h in range(hb):
            dq_ref[:, h * LANES:(h + 1) * LANES] = dqacc[h]

        @pl.when(jnp.logical_and(pl.program_id(0) == pl.num_programs(0) - 1, qi == nq - 1))
        def _():
            for cp in _pair_copies(g_refs, l_refs, ssem, rsem):
                cp.wait()

    blk = pl.BlockSpec((TQ, hb * HEAD_DIM), lambda g, qi: (qi, g))
    return pl.pallas_call(
        body, name="mla_bwd", grid=(N_HEADS // hb, nq),
        in_specs=[pl.BlockSpec((TQ, hb * LANES), lambda g, qi: (qi, g)),
                  pl.BlockSpec((s, hb * LANES), lambda g, qi: (0, g)),
                  pl.BlockSpec((s, hb * HEAD_DIM), lambda g, qi: (0, g)), blk, blk, blk] + ride_in,
        out_specs=[pl.BlockSpec((TQ, hb * LANES), lambda g, qi: (qi, g)),
                   pl.BlockSpec((hb * LANES, s), lambda g, qi: (g, 0)),
                   pl.BlockSpec((hb * HEAD_DIM, s), lambda g, qi: (g, 0))] + ride_out,
        out_shape=[pltpu.HBM((s, 1024), F32), pltpu.HBM((1024, s), F32),
                   pltpu.HBM((D_GRP, s), F32)] + ride_shape,
        scratch_shapes=[pltpu.VMEM((hb, TQ, LANES), F32), pltpu.VMEM((hb, TQ, TK), F32),
                        pltpu.VMEM((hb, TQ, TK), F32), pltpu.VMEM((hb, LANES, TQ), BF16),
                        pltpu.VMEM((hb, LANES, TQ), BF16)] + ride_sems,
        compiler_params=_params(("arbitrary", "arbitrary"), 52),
    )(*_hbm(qp, kp, vv, d_o, o, lse), *pays)


def _mid(x, p, target, sb_o, mla_o, rest, g_sb, g_mla, w_out, g_post, w_ple, g_ple, w_pg, b_pg, bd):
    s = x.shape[0]

    def body(x_ref, p_ref, t_ref, sbo_ref, mlo_ref, sbg_ref, mlg_ref, gsb_ref, gml_ref, wout_ref,
             gpost_ref, wple_ref, gple_ref, wpg_ref, bpg_ref, bd_ref,
             dx1_ref, dsbo_ref, dmlo_ref, dsbg_ref, dmlg_ref, x1b_ref, dglb_ref, ycb_ref, dyb_ref,
             pb_ref, dub_ref, small_ref):
        i = pl.program_id(0)
        bd_m = bd_ref[...]

        def seg_mean(v):
            return _dot(v.astype(BF16), bd_m) * (1.0 / HEAD_DIM)

        groups = []
        for o_ref, gate_ref, gain_ref in ((sbo_ref, sbg_ref, gsb_ref), (mlo_ref, mlg_ref, gml_ref)):
            o = o_ref[...]
            r = lax.rsqrt(seg_mean(o * o) + EPS)
            n = o * r
            hn = n * gain_ref[...]
            gate = gate_ref[...]
            sg = _sigmoid(gate)
            si = gate * sg
            groups.append((r, n, hn, gate, sg, si, gain_ref[...]))
        ya = (groups[0][2] * groups[0][5]).astype(BF16)
        yb = (groups[1][2] * groups[1][5]).astype(BF16)
        ycb_ref[:, :D_GRP] = ya
        ycb_ref[:, D_GRP:] = yb
        y = _dot(ya, wout_ref[:D_GRP, :]) + _dot(yb, wout_ref[D_GRP:, :])
        ry = lax.rsqrt(jnp.mean(y * y, axis=-1, keepdims=True) + EPS)
        ny = y * ry
        x1 = x_ref[...] + ny * gpost_ref[...]
        x1b = x1.astype(BF16)
        x1b_ref[...] = x1b
        pb = p_ref[...].astype(BF16)
        pb_ref[...] = pb
        u = _dot(pb, wple_ref[...])
        ru = lax.rsqrt(jnp.mean(u * u, axis=-1, keepdims=True) + EPS)
        nu = u * ru
        ple = nu * gple_ref[...]
        gate = _sigmoid(_dot(x1b, wpg_ref[...]) + bpg_ref[...])
        x2 = x1 + ple * gate
        diff = x2 - t_ref[...]
        dx2 = diff * (1.0 / D_MODEL)

        d_ple = dx2 * gate
        d_glin = (dx2 * ple) * (gate * (1.0 - gate))
        dglb = d_glin.astype(BF16)
        dglb_ref[...] = dglb
        dx1 = dx2 + _dot_nt(dglb, wpg_ref[...])
        dx1_ref[...] = dx1
        d_nu = d_ple * gple_ref[...]
        d_u = ru * (d_nu - nu * jnp.mean(d_nu * nu, axis=-1, keepdims=True))
        dub_ref[...] = d_u.astype(BF16)
        d_ny = dx1 * gpost_ref[...]
        d_y = ry * (d_ny - ny * jnp.mean(d_ny * ny, axis=-1, keepdims=True))
        dyb = d_y.astype(BF16)
        dyb_ref[...] = dyb
        d_yc = (_dot_nt(dyb, wout_ref[:D_GRP, :]), _dot_nt(dyb, wout_ref[D_GRP:, :]))

        d_gain = []
        for gx, (do_ref, dg_ref) in enumerate(((dsbo_ref, dsbg_ref), (dmlo_ref, dmlg_ref))):
            r, n, hn, gate_g, sg, si, gain = groups[gx]
            dyg = d_yc[gx]
            d_hn = dyg * si
            dg_ref[...] = (dyg * hn * (sg * (1.0 + gate_g * (1.0 - sg)))).astype(BF16)
            d_gain.append(jnp.sum(d_hn * n, axis=0, keepdims=True))
            d_n = d_hn * gain
            do_ref[...] = r * (d_n - n * seg_mean(d_n * n))

        @pl.when(i == 0)
        def _():
            small_ref[...] = jnp.zeros_like(small_ref)

        small_ref[3:4, :D_GRP] += d_gain[0]
        small_ref[3:4, D_GRP:] += d_gain[1]
        small_ref[4:5, :] += jnp.sum(dx1 * ny, axis=0, keepdims=True)
        small_ref[5:6, :] += jnp.sum(d_ple * nu, axis=0, keepdims=True)
        small_ref[6:7, :] += jnp.sum(d_glin, axis=0, keepdims=True)
        small_ref[7:8, :] += jnp.sum(diff * diff, axis=0, keepdims=True) * (0.5 / D_MODEL)

    def row(width, idx=0):
        return pl.BlockSpec((TM, width), lambda i: (i, idx))

    def full(a):
        return pl.BlockSpec(a.shape, lambda i: (0, 0))

    f32 = lambda w: pltpu.HBM((s, w), F32)
    b16 = lambda w: pltpu.HBM((s, w), BF16)
    return pl.pallas_call(
        body, name="mid", grid=(s // TM,),
        in_specs=[row(D_MODEL), row(PLE_DIM), row(D_MODEL), row(D_GRP), row(D_GRP),
                  row(D_GRP, 0), row(D_GRP, 1), full(g_sb), full(g_mla), full(w_out), full(g_post),
                  full(w_ple), full(g_ple), full(w_pg), full(b_pg), full(bd)],
        out_specs=(row(D_MODEL), row(D_GRP), row(D_GRP), row(D_GRP), row(D_GRP), row(D_MODEL),
                   row(D_MODEL), row(D_MODEL), row(D_MODEL), row(PLE_DIM), row(D_MODEL),
                   pl.BlockSpec((8, D_MODEL), lambda i: (0, 0))),
        out_shape=(f32(D_MODEL), f32(D_GRP), f32(D_GRP), b16(D_GRP), b16(D_GRP), b16(D_MODEL),
                   b16(D_MODEL), b16(D_MODEL), b16(D_MODEL), b16(PLE_DIM), b16(D_MODEL),
                   jax.ShapeDtypeStruct((8, D_MODEL), F32)),
        compiler_params=_params(("arbitrary",), 46),
    )(*_hbm(x, p, target, sb_o, mla_o, rest, rest), g_sb, g_mla, w_out, g_post, w_ple, g_ple, w_pg, b_pg, bd)


def _mla_prep_bwd(dqp, dkp, dvv, rest, gq, gkv, wuq, wuk, wuv, cos_t, sin_t):
    s = rest.shape[0]

    def body(dqp_ref, dkp_ref, dvv_ref, cq_ref, ckv_ref, gq_ref, gkv_ref, wuq_ref, wuk_ref, wuv_ref,
             c_ref, s_ref, dcq_ref, dckv_ref, dkr_ref, dqb_ref, dkb_ref, dvb_ref, small_ref):
        i = pl.program_id(0)
        lane = lax.broadcasted_iota(jnp.int32, (1, LANES), 1)
        in_rope = (lane >= HEAD_DIM) & (lane < HEAD_DIM + ROPE_DIM)
        cos_v, sin_v = c_ref[...], s_ref[...]
        dkr_roped = jnp.zeros((TM_IO, LANES), F32)
        for h in range(N_HEADS):
            sl = slice(h * LANES, (h + 1) * LANES)
            dy = dqp_ref[:, sl]
            dqb_ref[:, sl] = (dy * cos_v + _rope_swap(dy * sin_v, lane)).astype(BF16)
            dkh = dkp_ref[sl, :].T
            dkb_ref[:, sl] = dkh.astype(BF16)
            dkr_roped = dkr_roped + jnp.where(in_rope, dkh, 0.0)
        dkr_ref[...] = (dkr_roped * cos_v + _rope_swap(dkr_roped * sin_v, lane)).astype(BF16)
        dvb = dvv_ref[...].T.astype(BF16)
        dvb_ref[...] = dvb

        cq = cq_ref[...]
        rq = lax.rsqrt(jnp.mean(cq * cq, axis=-1, keepdims=True) + EPS)
        nq_ = cq * rq
        d_cqn = _dot_nt(dqb_ref[...], wuq_ref[...])
        d_n = d_cqn * gq_ref[...]
        dcq_ref[...] = (rq * (d_n - nq_ * jnp.mean(d_n * nq_, axis=-1, keepdims=True))).astype(BF16)

        ckv = ckv_ref[...]
        rkv = lax.rsqrt(jnp.mean(ckv * ckv, axis=-1, keepdims=True) + EPS)
        nkv = ckv * rkv
        d_ckvn = _dot_nt(dkb_ref[...], wuk_ref[...]) + _dot_nt(dvb, wuv_ref[...])
        d_n2 = d_ckvn * gkv_ref[...]
        dckv_ref[...] = (rkv * (d_n2 - nkv * jnp.mean(d_n2 * nkv, axis=-1, keepdims=True))).astype(BF16)

        @pl.when(i == 0)
        def _():
            small_ref[...] = jnp.zeros_like(small_ref)

        small_ref[0:1, :] += jnp.sum(d_cqn * nq_, axis=0, keepdims=True)
        small_ref[1:2, :KV_LORA] += jnp.sum(d_ckvn * nkv, axis=0, keepdims=True)

    def row(width, idx=0):
        return pl.BlockSpec((TM_IO, width), lambda i: (i, idx))

    def full(a):
        return pl.BlockSpec(a.shape, lambda i: (0, 0))

    b16 = lambda w: pltpu.HBM((s, w), BF16)
    return pl.pallas_call(
        body, name="mla_prep_bwd", grid=(s // TM_IO,),
        in_specs=[row(1024), pl.BlockSpec((1024, TM_IO), lambda i: (0, i)), pl.BlockSpec((D_GRP, TM_IO), lambda i: (0, i)),
                  row(Q_LORA, 4), row(KV_LORA, 10), full(gq), full(gkv),
                  full(wuq), full(wuk), full(wuv), row(LANES), row(LANES)],
        out_specs=(row(Q_LORA), row(KV_LORA), row(LANES), row(1024), row(1024), row(D_GRP),
                   pl.BlockSpec((8, Q_LORA), lambda i: (0, 0))),
        out_shape=(b16(Q_LORA), b16(KV_LORA), b16(LANES), b16(1024), b16(1024), b16(D_GRP),
                   jax.ShapeDtypeStruct((8, Q_LORA), F32)),
        compiler_params=_params(("arbitrary",), 24),
    )(*_hbm(dqp, dkp, dvv, rest, rest), gq, gkv, wuq, wuk, wuv, cos_t, sin_t)


def _in_bwd(x, g, dx1, pieces, w, sums):
    s = x.shape[0]
    steps = s // TM_IO
    widths = [a.shape[1] for a in pieces]
    offs = [sum(widths[:k]) for k in range(len(widths))]
    n_pc, n_op = len(pieces), len(sums)
    ride_in, ride_out, ride_shape, ride_sems = _chip_specs(sums)

    def body(x_ref, g_ref, dx1_ref, *refs):
        piece_refs = refs[:n_pc]
        w_ref = refs[n_pc]
        s_refs = refs[n_pc + 1:n_pc + 1 + n_op]
        dx_ref, small_ref = refs[n_pc + 1 + n_op:n_pc + 3 + n_op]
        l_refs = refs[n_pc + 3 + n_op:n_pc + 3 + 2 * n_op]
        ssem, rsem = refs[n_pc + 3 + 2 * n_op:]
        i = pl.program_id(0)

        @pl.when(i == 0)
        def _():
            for cp in _chip_copies(s_refs, l_refs, ssem, rsem):
                cp.start()

        dh = jnp.zeros((TM_IO, D_MODEL), F32)
        for pr, off, wd in zip(piece_refs, offs, widths):
            dh = dh + _dot_nt(pr[...], w_ref[:, off:off + wd])
        xv = x_ref[...]
        r = lax.rsqrt(jnp.mean(xv * xv, axis=-1, keepdims=True) + EPS)
        n = xv * r
        d_n = dh * g_ref[...]
        dx_ref[...] = dx1_ref[...] + r * (d_n - n * jnp.mean(d_n * n, axis=-1, keepdims=True))

        @pl.when(i == 0)
        def _():
            small_ref[...] = jnp.zeros_like(small_ref)

        small_ref[0:1, :] += jnp.sum(dh * n, axis=0, keepdims=True)

        @pl.when(i == steps - 1)
        def _():
            for cp in _chip_copies(s_refs, l_refs, ssem, rsem):
                cp.wait()

    def row(width):
        return pl.BlockSpec((TM_IO, width), lambda i: (i, 0))

    return pl.pallas_call(
        body, name="in_bwd", grid=(steps,),
        in_specs=[row(D_MODEL), pl.BlockSpec((1, D_MODEL), lambda i: (0, 0)), row(D_MODEL)]
        + [row(wd) for wd in widths] + [pl.BlockSpec(w.shape, lambda i: (0, 0))] + ride_in,
        out_specs=[row(D_MODEL), pl.BlockSpec((8, D_MODEL), lambda i: (0, 0))] + ride_out,
        out_shape=[pltpu.HBM((s, D_MODEL), F32), jax.ShapeDtypeStruct((8, D_MODEL), F32)]
        + ride_shape,
        scratch_shapes=ride_sems,
        compiler_params=_params(("arbitrary",), 40),
    )(*_hbm(x), g, *_hbm(dx1, *pieces), w, *sums)


def _tn_matmul(a, b, name, blocked=False):
    s, k = a.shape
    n = b.shape[1]
    ts = min(s, TS_DW)
    tn = n if blocked else min(n, 512)
    steps = s // ts

    def body(a_ref, b_ref, o_ref):
        t = pl.program_id(1)

        @pl.when(t == 0)
        def _():
            o_ref[...] = jnp.zeros_like(o_ref)

        prod = _dot_tn(a_ref[...], b_ref[...])
        if blocked:
            for j in range(n // LANES):
                o_ref[j] += prod[:, j * LANES:(j + 1) * LANES]
        else:
            o_ref[...] += prod

    if blocked:
        out_spec = pl.BlockSpec((n // LANES, k, LANES), lambda j, t: (0, 0, 0))
        out_shape = jax.ShapeDtypeStruct((n // LANES, k, LANES), F32)
    else:
        out_spec = pl.BlockSpec((k, tn), lambda j, t: (0, j))
        out_shape = jax.ShapeDtypeStruct((k, n), F32)
    return pl.pallas_call(
        body, name=name, grid=(n // tn, steps),
        in_specs=[pl.BlockSpec((ts, k), lambda j, t: (t, 0)), pl.BlockSpec((ts, tn), lambda j, t: (t, j))],
        out_specs=out_spec, out_shape=out_shape,
        compiler_params=_params(("parallel", "arbitrary"), 20),
    )(*_hbm(a, b))


def _tn_matmul_multi(a, bs, name):
    s, k = a.shape
    widths = [b.shape[1] for b in bs]
    ts = min(s, TS_DW)

    def body(a_ref, *refs):
        b_refs, o_ref = refs[:-1], refs[-1]
        t = pl.program_id(0)

        @pl.when(t == 0)
        def _():
            o_ref[...] = jnp.zeros_like(o_ref)

        av = a_ref[...]
        off = 0
        for b_ref, wd in zip(b_refs, widths):
            o_ref[:, off:off + wd] += _dot_tn(av, b_ref[...])
            off += wd

    return pl.pallas_call(
        body, name=name, grid=(s // ts,),
        in_specs=[pl.BlockSpec((ts, k), lambda t: (t, 0))] + [pl.BlockSpec((ts, wd), lambda t: (t, 0)) for wd in widths],
        out_specs=pl.BlockSpec((k, sum(widths)), lambda t: (0, 0)),
        out_shape=jax.ShapeDtypeStruct((k, sum(widths)), F32),
        compiler_params=_params(("arbitrary",), 30),
    )(a, *_hbm(*bs))


IN_SHARD = 372
_IN_KERNEL_ORDER = ((0, 2048), (2464, 2976), (2048, 2432))
_IN_ROPE = (2432, 2464)
_IN_GRAD_SRC = ((0, 512, 0, 0), (512, 1024, 0, 512), (1024, 1536, 1, 0), (1536, 2048, 1, 512),
                (2048, 2304, 2, 512), (2304, 2432, 2, 768), (2432, 2464, 2, 960), (2464, 2976, 2, 0))


def _shard_cols(gath_in, lo, hi):
    out = []
    while lo < hi:
        j, a = divmod(lo, IN_SHARD)
        b = min(IN_SHARD, a + hi - lo)
        out.append(gath_in[j][:, a:b])
        lo += b - a
    return out


def _kernel_w_in(g_in):
    zc = lambda n: jnp.zeros((D_MODEL, n), BF16)
    parts = [pc for lo, hi in _IN_KERNEL_ORDER for pc in _shard_cols(g_in, lo, hi)]
    parts += [zc(64)] + _shard_cols(g_in, *_IN_ROPE) + [zc(32)]
    return jnp.concatenate(parts, axis=1)


def _kernel_weights(gath):
    g_uq, g_ukv, g_out, g_ple, g_pg = gath
    w_uq_p = jnp.pad(g_uq, ((0, 0), (0, 0), (0, 32))).transpose(1, 0, 2).reshape(Q_LORA, 1024)
    k_only = jnp.where(jnp.arange(LANES) < HEAD_DIM, g_ukv, jnp.zeros_like(g_ukv))
    w_uk_p = k_only.transpose(1, 0, 2).reshape(KV_LORA, 1024)
    w_uv = g_ukv[:, :, HEAD_DIM:].transpose(1, 0, 2).reshape(KV_LORA, D_GRP)
    w_ple = g_ple.transpose(1, 0, 2).reshape(PLE_DIM, D_MODEL)
    return (w_uq_p, w_uk_p, w_uv, g_out.reshape(D_MODEL, D_MODEL), w_ple, g_pg.reshape(D_MODEL, D_MODEL))


def _payload_in(d_cols):
    blocks = []
    for j in range(N_DEV):
        lo, hi = j * IN_SHARD, (j + 1) * IN_SHARD
        parts = []
        for o_lo, o_hi, idx, off in _IN_GRAD_SRC:
            a, b = max(lo, o_lo), min(hi, o_hi)
            if a < b:
                parts.append(d_cols[idx][:, off + a - o_lo:off + b - o_lo])
        blocks.append(jnp.concatenate(parts, axis=1))
    return jnp.stack(blocks)


def _payload_ukv(duk_blk, d_uv):
    dv_blk = d_uv.reshape(KV_LORA, N_HEADS, HEAD_DIM).transpose(1, 0, 2)
    return jnp.concatenate([duk_blk[:, :, :HEAD_DIM], dv_blk], axis=2)


def kernel(x, p, positions, norm_pre_g, w_in, q_norm_g, w_uq, kv_norm_g, w_ukv, sb_out_norm_g, mla_out_norm_g, w_out, norm_post_g, w_ple, ple_norm_g, w_ple_gate, b_ple_gate, loss_target, m_norm_pre_g, m_w_in, m_q_norm_g, m_w_uq, m_kv_norm_g, m_w_ukv, m_sb_out_norm_g, m_mla_out_norm_g, m_w_out, m_norm_post_g, m_w_ple, m_ple_norm_g, m_w_ple_gate, m_b_ple_gate, v_norm_pre_g, v_w_in, v_q_norm_g, v_w_uq, v_kv_norm_g, v_w_ukv, v_sb_out_norm_g, v_mla_out_norm_g, v_w_out, v_norm_post_g, v_w_ple, v_ple_norm_g, v_w_ple_gate, v_b_ple_gate):
    mats = (w_in, w_uq, w_ukv, w_out, w_ple, w_ple_gate)
    m_mats = (m_w_in, m_w_uq, m_w_ukv, m_w_out, m_w_ple, m_w_ple_gate)
    v_mats = (v_w_in, v_w_uq, v_w_ukv, v_w_out, v_w_ple, v_w_ple_gate)
    vecs = (norm_pre_g, q_norm_g, kv_norm_g, sb_out_norm_g, mla_out_norm_g, norm_post_g, ple_norm_g, b_ple_gate)
    m_vecs = (m_norm_pre_g, m_q_norm_g, m_kv_norm_g, m_sb_out_norm_g, m_mla_out_norm_g, m_norm_post_g,
              m_ple_norm_g, m_b_ple_gate)
    v_vecs = (v_norm_pre_g, v_q_norm_g, v_kv_norm_g, v_sb_out_norm_g, v_mla_out_norm_g, v_norm_post_g,
              v_ple_norm_g, v_b_ple_gate)

    shards = [a[0].astype(BF16) for a in mats]
    w_in_p = _kernel_w_in(_all_gather(shards[:1])[0])
    grad_x, reduced, vec_slab = _step(x[0], p[0, 0], positions[0], loss_target[0], *vecs, w_in_p, shards[1:])
    upd = [_adamw_matrix(own, l2, w, m, v, "adamw_%d" % o)
           for o, ((own, l2), w, m, v) in enumerate(zip(reduced, mats, m_mats, v_mats))]
    sm = _adamw_vectors(_slab_exchange(vec_slab), vecs, m_vecs, v_vecs)

    outs = []
    for kind in range(4):
        mat = [upd[o][kind] for o in range(len(mats))]
        vec = sm[1 + 8 * kind:9 + 8 * kind]
        outs += [vec[0], mat[0], vec[1], mat[1], vec[2], mat[2], vec[3], vec[4], mat[3], vec[5],
                 mat[4], vec[6], mat[5], vec[7]]
    return (sm[0][0, 0], grad_x[None], *outs)


def _step(xs, ps, pos, tgt, norm_pre_g, q_norm_g, kv_norm_g, sb_out_norm_g, mla_out_norm_g,
          norm_post_g, ple_norm_g, b_ple_gate, w_in_p, shards):
    s = xs.shape[0]
    place = jnp.stack([lax.axis_index("c"), 2 * lax.axis_index("x") + lax.axis_index("y")]).astype(jnp.int32)

    half = ROPE_DIM // 2
    freq = ROPE_THETA ** (-jnp.arange(half, dtype=F32) / half)
    ang = pos.astype(F32)[:, None] * freq
    cos, sin = jnp.cos(ang), jnp.sin(ang)
    cos_t = jnp.concatenate([jnp.ones((s, 64), F32), cos, cos, jnp.zeros((s, 32), F32)], axis=1)
    sin_t = jnp.concatenate([jnp.zeros((s, 64), F32), -sin, sin, jnp.zeros((s, 32), F32)], axis=1)
    seg = jnp.arange(D_GRP) // HEAD_DIM
    bd = (seg[:, None] == seg[None, :]).astype(BF16)

    qkv, rest, h_b, *gath = _in_proj(xs, norm_pre_g, w_in_p, shards)
    w_uq_p, w_uk_p, w_uv, f_out, f_ple, f_pg = _kernel_weights(gath)
    sb_o = _sb_fwd(qkv, 8)
    qp, kp, vv, cqn_b, ckvn_b = _mla_prep(rest, q_norm_g, kv_norm_g, w_uq_p, w_uk_p, w_uv, cos_t, sin_t)
    mla_o, lse = _mla_fwd(qp, kp, vv, 4)

    (dx1, d_sbo, d_mlo, d_sbg, d_mlg, x1_b, dgl_b, yc_b, dy_b, p_b, du_b, small_mid) = _mid(
        xs, ps, tgt, sb_o, mla_o, rest, sb_out_norm_g, mla_out_norm_g, f_out, norm_post_g,
        f_ple, ple_norm_g, f_pg, b_ple_gate, bd)
    pay_a = [_tn_matmul(yc_b, dy_b, "dw_out").reshape(N_DEV, 128, D_MODEL),
             _tn_matmul(p_b, du_b, "dw_ple", blocked=True),
             _tn_matmul(x1_b, dgl_b, "dw_pg").reshape(N_DEV, 128, D_MODEL)]
    dqp, dkp, dvv, *sib_a = _mla_bwd(qp, kp, vv, d_mlo, mla_o, lse, 4, pay_a)
    pair_a = _pair_sums(pay_a, sib_a, place, "grad_pair_sums_a")
    dq_sb, dk_sb, dv_sb, *landed_a = _sb_bwd(qkv, d_sbo, [sm for sm, _ in pair_a])
    dcq, dckv, dkr, dq_b, dk_b, dv_b, small_prep = _mla_prep_bwd(
        dqp, dkp, dvv, rest, q_norm_g, kv_norm_g, w_uq_p, w_uk_p, w_uv, cos_t, sin_t)
    pieces = [dq_sb, dk_sb, dv_sb, d_sbg, d_mlg, dcq, dckv, dkr]
    d_cols = [_tn_matmul_multi(h_b, pieces[0:2], "dw_in_0"), _tn_matmul_multi(h_b, pieces[2:4], "dw_in_1"),
              _tn_matmul_multi(h_b, pieces[4:8], "dw_in_2")]
    pay_b = [_payload_in(d_cols), _tn_matmul(cqn_b, dq_b, "dw_uq", blocked=True),
             _payload_ukv(_tn_matmul(ckvn_b, dk_b, "dw_uk", blocked=True), _tn_matmul(ckvn_b, dv_b, "dw_uv"))]
    pair_b = _pair_sums(pay_b, _pair_exchange(pay_b, "grad_pair_exchange"), place, "grad_pair_sums_b")
    grad_x, small_in, *landed_b = _in_bwd(xs, norm_pre_g, dx1, pieces, w_in_p, [sm for sm, _ in pair_b])
    reduced = [(own, l2) for (_, own), l2 in zip(pair_b + pair_a, landed_b + landed_a)]
    slab = jnp.concatenate([small_in[0:1], jnp.pad(small_prep[0:2], ((0, 0), (0, D_MODEL - Q_LORA))),
                            small_mid[3:8]], axis=0)
    return grad_x, reduced, slab
```

```python
import jax
import jax.numpy as jnp
from jax import lax
from jax.experimental import pallas as pl
from jax.experimental.pallas import tpu as pltpu

F32 = jnp.float32
BF16 = jnp.bfloat16
MESH = pl.DeviceIdType.MESH

N_DEV = 8
D_MODEL = 1024
N_HEADS = 8
HEAD_DIM = 64
D_GRP = N_HEADS * HEAD_DIM
Q_LORA = 256
KV_LORA = 128
ROPE_DIM = 32
PLE_DIM = 256
CHUNK_SHIFT = 6
ROPE_THETA = 10000.0
EPS = 1e-6
SB_SCALE = HEAD_DIM ** -0.5
MLA_SCALE = (HEAD_DIM + ROPE_DIM) ** -0.5
NEG = -1e30
LOG2_E = 1.4426950408889634
LN_2 = 0.6931471805599453
SB_CUTOFF = 110.0

ADAM_LR = 0.001
ADAM_B1 = 0.9
ADAM_B2 = 0.999
ADAM_EPS = 1e-08
ADAM_WD = 0.01
ADAM_STEP = 10

LANES = 128
TQ = 256
TK = 256
TM = 256
TM_IO = 512
TS_DW = 2048

D_IN_P = 3072

_NT = (((1,), (1,)), ((), ()))
_TN = (((0,), (0,)), ((), ()))


def _params(sem, vmem_mb):
    return pltpu.CompilerParams(dimension_semantics=sem, vmem_limit_bytes=vmem_mb << 20)


def _hbm(*arrays):
    return [pltpu.with_memory_space_constraint(a, pltpu.HBM) for a in arrays]


def _dot(a, b):
    return jnp.dot(a, b, preferred_element_type=F32)


def _dot_nt(a, b):
    return lax.dot_general(a, b, _NT, preferred_element_type=F32)


def _dot_tn(a, b):
    return lax.dot_general(a, b, _TN, preferred_element_type=F32)


def _hl_dot(a, b):
    hi = a.astype(BF16)
    lo = (a - hi.astype(F32)).astype(BF16)
    return _dot(hi, b) + _dot(lo, b)


def _sigmoid(x):
    return 1.0 / (1.0 + jnp.exp(-x))


def _rope_swap(x, lane):
    left = pltpu.roll(x, LANES - 16, axis=1)
    right = pltpu.roll(x, 16, axis=1)
    lo = (lane >= 64) & (lane < 80)
    hi = (lane >= 80) & (lane < 96)
    return jnp.where(lo, left, jnp.where(hi, right, 0.0))


def _two_level_gather(x_refs, out_refs, send_sems, recv_sems, local_sems):
    x, y, c = lax.axis_index("x"), lax.axis_index("y"), lax.axis_index("c")
    me, sibling = (x, y, c), (x, y, 1 - c)
    chips = [(1 - x, y), (x, 1 - y), (1 - x, 1 - y)]
    ops = range(len(x_refs))

    def slot(o, px, py, pc):
        return out_refs[o].at[4 * px + 2 * py + pc]

    def copy(o, k, block, to, src=None):
        return pltpu.make_async_remote_copy(
            src_ref=slot(o, *block) if src is None else src, dst_ref=slot(o, *block),
            send_sem=send_sems.at[o, k], recv_sem=recv_sems.at[o, k],
            device_id=to, device_id_type=MESH)

    def mine():
        return [pltpu.make_async_copy(x_refs[o], slot(o, *me), local_sems.at[o]) for o in ops]

    def first():
        return ([copy(o, 0, me, sibling, src=x_refs[o]) for o in ops]
                + [copy(o, 1 + j, me, (*chip, c), src=x_refs[o]) for j, chip in enumerate(chips) for o in ops])

    def start():
        for cp in mine() + first():
            cp.start()

    def finish():
        passed = []
        for j, chip in enumerate(chips):
            for o in ops:
                copy(o, 1 + j, (*chip, c), me).wait_recv()
                passed.append(copy(o, 4 + j, (*chip, c), sibling))
                passed[-1].start()
        for o in ops:
            copy(o, 0, sibling, me).wait_recv()
        for j, chip in enumerate(chips):
            for o in ops:
                copy(o, 4 + j, (*chip, 1 - c), me).wait_recv()
        for cp in first() + passed:
            cp.wait_send()
        for cp in mine():
            cp.wait()

    return start, finish


def _gather_sems(n_op):
    return [pltpu.SemaphoreType.DMA((n_op, 7)), pltpu.SemaphoreType.DMA((n_op, 7)),
            pltpu.SemaphoreType.DMA((n_op,))]


def _all_gather(shards):
    n_op = len(shards)

    def body(*refs):
        start, finish = _two_level_gather(refs[:n_op], refs[n_op:2 * n_op], *refs[2 * n_op:])
        start()
        finish()

    any_spec = pl.BlockSpec(memory_space=pl.ANY)
    return pl.pallas_call(
        body, name="weight_all_gather",
        out_shape=[jax.ShapeDtypeStruct((N_DEV,) + a.shape, a.dtype) for a in shards],
        in_specs=[any_spec] * n_op, out_specs=[any_spec] * n_op, scratch_shapes=_gather_sems(n_op),
        compiler_params=pltpu.CompilerParams(vmem_limit_bytes=4 << 20),
    )(*shards)


def _pair_copies(g_refs, l_refs, ssem, rsem):
    x, y, c = lax.axis_index("x"), lax.axis_index("y"), lax.axis_index("c")
    copies = []
    for o in range(len(g_refs)):
        for chip in range(4):
            copies.append(pltpu.make_async_remote_copy(
                src_ref=g_refs[o].at[2 * chip + (1 - c)], dst_ref=l_refs[o].at[chip],
                send_sem=ssem.at[o, chip], recv_sem=rsem.at[o, chip],
                device_id=(x, y, 1 - c), device_id_type=MESH))
    return copies


def _pair_specs(pays):
    n_op = len(pays)
    any_spec = pl.BlockSpec(memory_space=pl.ANY)
    return ([any_spec] * n_op, [any_spec] * n_op,
            [jax.ShapeDtypeStruct((4,) + a.shape[1:], F32) for a in pays],
            [pltpu.SemaphoreType.DMA((n_op, 4)), pltpu.SemaphoreType.DMA((n_op, 4))])


def _pair_exchange(pays, name):
    n_op = len(pays)
    in_specs, out_specs, out_shape, sems = _pair_specs(pays)

    def body(*refs):
        copies = _pair_copies(refs[:n_op], refs[n_op:2 * n_op], *refs[2 * n_op:])
        for cp in copies:
            cp.start()
        for cp in copies:
            cp.wait()

    return pl.pallas_call(body, name=name, out_shape=out_shape, in_specs=in_specs, out_specs=out_specs,
                          scratch_shapes=sems,
                          compiler_params=pltpu.CompilerParams(vmem_limit_bytes=4 << 20))(*pays)


def _slab_exchange(small):
    sr, n = small.shape

    def body(s_ref, sland_ref, ssem, rsem, lsem):
        x, y, c = lax.axis_index("x"), lax.axis_index("y"), lax.axis_index("c")
        me = 4 * x + 2 * y + c
        copies = []
        for k in range(1, N_DEV):
            peer = (1 - x if (k >> 2) & 1 else x, 1 - y if (k >> 1) & 1 else y, 1 - c if k & 1 else c)
            copies.append(pltpu.make_async_remote_copy(
                src_ref=s_ref, dst_ref=sland_ref.at[me], send_sem=ssem.at[k], recv_sem=rsem.at[k],
                device_id=peer, device_id_type=MESH))
        own = pltpu.make_async_copy(s_ref, sland_ref.at[me], lsem)
        own.start()
        for cp in copies:
            cp.start()
        for cp in copies:
            cp.wait()
        own.wait()

    any_spec = pl.BlockSpec(memory_space=pl.ANY)
    return pl.pallas_call(
        body, name="grad_slab_exchange", out_shape=jax.ShapeDtypeStruct((N_DEV, sr, n), F32),
        in_specs=[any_spec], out_specs=any_spec,
        scratch_shapes=[pltpu.SemaphoreType.DMA((N_DEV,)), pltpu.SemaphoreType.DMA((N_DEV,)),
                        pltpu.SemaphoreType.DMA],
        compiler_params=pltpu.CompilerParams(vmem_limit_bytes=4 << 20),
    )(small)


def _pair_sums(pays, landed, place, name):
    n = len(pays)
    dims = [p.shape[1:] for p in pays]

    def body(place_ref, *refs):
        g_refs, l_refs, s_refs, own_refs = refs[:n], refs[n:2 * n], refs[2 * n:3 * n], refs[3 * n:]
        i = pl.program_id(0)
        for o in range(n):
            tot = g_refs[o][...] + l_refs[o][...]
            s_refs[o][...] = tot.astype(BF16)

            @pl.when(i == place_ref[1])
            def _(o=o, tot=tot):
                own_refs[o][...] = tot

    grid_spec = pltpu.PrefetchScalarGridSpec(
        num_scalar_prefetch=1, grid=(4,),
        in_specs=[pl.BlockSpec((None, r, c), lambda i, pr: (2 * i + pr[0], 0, 0)) for r, c in dims]
        + [pl.BlockSpec((None, r, c), lambda i, pr: (i, 0, 0)) for r, c in dims],
        out_specs=[pl.BlockSpec((None, r, c), lambda i, pr: (i, 0, 0)) for r, c in dims]
        + [pl.BlockSpec((r, c), lambda i, pr: (0, 0)) for r, c in dims])
    out = pl.pallas_call(
        body, name=name, grid_spec=grid_spec,
        out_shape=[jax.ShapeDtypeStruct((4, r, c), BF16) for r, c in dims]
        + [jax.ShapeDtypeStruct((r, c), F32) for r, c in dims],
        compiler_params=_params(("arbitrary",), 16),
    )(place, *pays, *landed)
    return list(zip(out[:n], out[n:]))


def _chip_copies(s_refs, l_refs, ssem, rsem):
    x, y, c = lax.axis_index("x"), lax.axis_index("y"), lax.axis_index("c")
    copies = []
    for rel in range(1, 4):
        px = 1 - x if rel & 2 else x
        py = 1 - y if rel & 1 else y
        for o in range(len(s_refs)):
            copies.append(pltpu.make_async_remote_copy(
                src_ref=s_refs[o].at[2 * px + py], dst_ref=l_refs[o].at[rel - 1],
                send_sem=ssem.at[o, rel - 1], recv_sem=rsem.at[o, rel - 1],
                device_id=(px, py, c), device_id_type=MESH))
    return copies


def _chip_specs(sums):
    n_op = len(sums)
    any_spec = pl.BlockSpec(memory_space=pl.ANY)
    return ([any_spec] * n_op, [any_spec] * n_op,
            [jax.ShapeDtypeStruct((3,) + a.shape[1:], BF16) for a in sums],
            [pltpu.SemaphoreType.DMA((n_op, 3)), pltpu.SemaphoreType.DMA((n_op, 3))])


def _adamw_math(g, w, m, v):
    mn = ADAM_B1 * m + (1.0 - ADAM_B1) * g
    vn = ADAM_B2 * v + (1.0 - ADAM_B2) * (g * g)
    m_hat = mn / (1.0 - ADAM_B1 ** ADAM_STEP)
    v_hat = vn / (1.0 - ADAM_B2 ** ADAM_STEP)
    return -ADAM_LR * (m_hat / (jnp.sqrt(v_hat) + ADAM_EPS) + ADAM_WD * w), mn, vn


def _adamw_matrix(own, landed, w, m, v, name):
    _, r, c = w.shape
    cp = own.shape[1]
    br = min(r, 256)

    def body(own_ref, l_ref, w_ref, m_ref, v_ref, g_out, d_out, m_out, v_out):
        g = own_ref[...]
        for k in range(3):
            g = g + l_ref[k].astype(F32)
        g = g[:, :c]
        g_out[...] = g
        d_out[...], m_out[...], v_out[...] = _adamw_math(g, w_ref[...], m_ref[...], v_ref[...])

    row = pl.BlockSpec((None, br, c), lambda i: (0, i, 0))
    shp = jax.ShapeDtypeStruct((1, r, c), F32)
    return pl.pallas_call(
        body, name=name, grid=(r // br,),
        in_specs=[pl.BlockSpec((br, cp), lambda i: (i, 0)), pl.BlockSpec((3, br, cp), lambda i: (0, i, 0)),
                  row, row, row],
        out_specs=(row, row, row, row), out_shape=(shp, shp, shp, shp),
        compiler_params=_params(("parallel",), 12),
    )(own, landed, w, m, v)


_VEC_PLACE = ((0, 0), (1, 0), (2, 0), (3, 0), (3, D_GRP), (4, 0), (5, 0), (6, 0))


def _adamw_vectors(sland, ws, ms, vs):
    nv = len(ws)

    def body(l_ref, *refs):
        w_refs, m_refs, v_refs = refs[:nv], refs[nv:2 * nv], refs[2 * nv:3 * nv]
        loss_ref = refs[3 * nv]
        outs = refs[3 * nv + 1:]
        g_all = l_ref[0]
        for j in range(1, N_DEV):
            g_all = g_all + l_ref[j]
        loss_ref[...] = jnp.sum(g_all[7:8, :], axis=1, keepdims=True)
        for k, (row, lane0) in enumerate(_VEC_PLACE):
            n = w_refs[k].shape[1]
            g = g_all[row:row + 1, lane0:lane0 + n]
            d, mn, vn = _adamw_math(g, w_refs[k][...], m_refs[k][...], v_refs[k][...])
            outs[k][...] = g
            outs[nv + k][...] = d
            outs[2 * nv + k][...] = mn
            outs[3 * nv + k][...] = vn

    def whole(shape):
        return pl.BlockSpec(shape, lambda i: (0,) * len(shape))

    shapes = [jax.ShapeDtypeStruct(w.shape, F32) for w in ws]
    return pl.pallas_call(
        body, name="adamw_vectors", grid=(1,),
        in_specs=[whole(sland.shape)] + [whole(w.shape) for w in ws] * 3,
        out_specs=[whole((1, 1))] + [whole(w.shape) for w in ws] * 4,
        out_shape=[jax.ShapeDtypeStruct((1, 1), F32)] + shapes * 4,
        compiler_params=_params(("arbitrary",), 4),
    )(sland, *ws, *ms, *vs)


def _in_proj(x, g, w, shards):
    s = x.shape[0]
    n_op = len(shards)
    steps = s // TM_IO

    def body(x_ref, g_ref, w_ref, *refs):
        shard_refs = refs[:n_op]
        qkv_ref, rest_ref, h_ref = refs[n_op:n_op + 3]
        gath_refs = refs[n_op + 3:2 * n_op + 3]
        start, finish = _two_level_gather(shard_refs, gath_refs, *refs[2 * n_op + 3:])
        i = pl.program_id(0)

        @pl.when(i == 0)
        def _():
            start()

        xv = x_ref[...]
        r = lax.rsqrt(jnp.mean(xv * xv, axis=-1, keepdims=True) + EPS)
        h = ((xv * r) * g_ref[...]).astype(BF16)
        h_ref[...] = h
        qkv_ref[...] = _dot(h, w_ref[:, :1536]).astype(BF16)
        rest_ref[...] = _dot(h, w_ref[:, 1536:])

        @pl.when(i == steps - 1)
        def _():
            finish()

    any_spec = pl.BlockSpec(memory_space=pl.ANY)
    return pl.pallas_call(
        body, name="in_proj", grid=(steps,),
        in_specs=[pl.BlockSpec((TM_IO, D_MODEL), lambda i: (i, 0)),
                  pl.BlockSpec((1, D_MODEL), lambda i: (0, 0)),
                  pl.BlockSpec((D_MODEL, D_IN_P), lambda i: (0, 0))] + [any_spec] * n_op,
        out_specs=[pl.BlockSpec((TM_IO, 1536), lambda i: (i, 0)),
                   pl.BlockSpec((TM_IO, 1536), lambda i: (i, 0)),
                   pl.BlockSpec((TM_IO, D_MODEL), lambda i: (i, 0))] + [any_spec] * n_op,
        out_shape=[pltpu.HBM((s, 1536), BF16), pltpu.HBM((s, 1536), F32),
                   pltpu.HBM((s, D_MODEL), BF16)]
        + [jax.ShapeDtypeStruct((N_DEV,) + a.shape, a.dtype) for a in shards],
        scratch_shapes=_gather_sems(n_op),
        compiler_params=_params(("arbitrary",), 32),
    )(x, g, w, *shards)


def _mla_prep(rest, gq, gkv, wuq, wuk, wuv, cos_t, sin_t):
    s = rest.shape[0]

    def body(cq_ref, ckv_ref, kr_ref, gq_ref, gkv_ref, wuq_ref, wuk_ref, wuv_ref, c_ref, s_ref,
             qp_ref, kp_ref, vv_ref, cqn_ref, ckvn_ref):
        lane = lax.broadcasted_iota(jnp.int32, (1, LANES), 1)
        cos_v, sin_v = c_ref[...], s_ref[...]
        cq = cq_ref[...]
        rq = lax.rsqrt(jnp.mean(cq * cq, axis=-1, keepdims=True) + EPS)
        cqn = ((cq * rq) * gq_ref[...]).astype(BF16)
        cqn_ref[...] = cqn
        q = _dot(cqn, wuq_ref[...])
        ckv = ckv_ref[...]
        rkv = lax.rsqrt(jnp.mean(ckv * ckv, axis=-1, keepdims=True) + EPS)
        ckvn = ((ckv * rkv) * gkv_ref[...]).astype(BF16)
        ckvn_ref[...] = ckvn
        kn = _dot(ckvn, wuk_ref[...])
        vv_ref[...] = _dot(ckvn, wuv_ref[...]).astype(BF16)
        kr = kr_ref[...]
        kr_roped = kr * cos_v + _rope_swap(kr, lane) * sin_v
        for h in range(N_HEADS):
            sl = slice(h * LANES, (h + 1) * LANES)
            qh = q[:, sl]
            qp_ref[:, sl] = (qh * cos_v + _rope_swap(qh, lane) * sin_v).astype(BF16)
            kp_ref[:, sl] = (kn[:, sl] + kr_roped).astype(BF16)

    def row(width, idx):
        return pl.BlockSpec((TM_IO, width), lambda i: (i, idx))

    def full(a):
        return pl.BlockSpec(a.shape, lambda i: (0, 0))

    return pl.pallas_call(
        body, name="mla_prep", grid=(s // TM_IO,),
        in_specs=[row(Q_LORA, 4), row(KV_LORA, 10), row(LANES, 11), full(gq), full(gkv),
                  full(wuq), full(wuk), full(wuv), row(LANES, 0), row(LANES, 0)],
        out_specs=(row(1024, 0), row(1024, 0), row(D_GRP, 0), row(Q_LORA, 0), row(KV_LORA, 0)),
        out_shape=(pltpu.HBM((s, 1024), BF16), pltpu.HBM((s, 1024), BF16),
                   pltpu.HBM((s, D_GRP), BF16), pltpu.HBM((s, Q_LORA), BF16),
                   pltpu.HBM((s, KV_LORA), BF16)),
        compiler_params=_params(("parallel",), 13),
    )(*_hbm(rest, rest, rest), gq, gkv, wuq, wuk, wuv, cos_t, sin_t)


def _sb_live(n, qi, carries):
    top = carries[0]
    for c in carries[1:]:
        top = jnp.maximum(top, c)
    return jnp.logical_and(n < qi, jnp.max(top) > -SB_CUTOFF)


def _sb_fwd(qkv, hb):
    s = qkv.shape[0]

    def body(q_ref, k_ref, v_ref, o_ref, acc):
        qi = pl.program_id(1)
        lane = lax.broadcasted_iota(jnp.int32, (1, LANES), 1)
        is_a = lane < HEAD_DIM
        pair = lambda h: slice((h // 2) * LANES, (h // 2 + 1) * LANES)
        q_h = []
        for h in range(hb):
            qs = q_ref[:, pair(h)] * SB_SCALE
            mine = is_a if h % 2 == 0 else jnp.logical_not(is_a)
            q_h.append(jnp.where(mine, qs, jnp.zeros_like(qs)))
        r_i = lax.broadcasted_iota(jnp.int32, (TQ, TK), 0)
        c_i = lax.broadcasted_iota(jnp.int32, (TQ, TK), 1)
        past = c_i < r_i
        upper = (r_i > c_i).astype(BF16)
        acc[...] = jnp.zeros_like(acc)

        def tile(j, carries, diag):
            ks = pl.ds(pl.multiple_of(j * TK, TK), TK)
            zs = [_dot_nt(q_h[h], k_ref[ks, pair(h)]) for h in range(hb)]
            if diag:
                zs = [jnp.where(past, z, NEG) for z in zs]
            lfs = [-(jnp.maximum(z, 0.0) + jnp.log(1.0 + jnp.exp(-jnp.abs(z)))) for z in zs]
            sufs = [_hl_dot(lfs[h], upper) for h in range(hb)]
            out = []
            for h in range(hb):
                w = jnp.exp(zs[h] + lfs[h] + (sufs[h] + carries[h]))
                acc[h] += _dot(w.astype(BF16), v_ref[ks, pair(h)])
                out.append(carries[h] + jnp.sum(lfs[h], axis=1, keepdims=True))
            return tuple(out)

        zero = jnp.zeros((TQ, 1), F32)
        carries = tile(qi, (zero,) * hb, True)

        def step(st):
            return (st[0] + 1,) + tile(qi - 1 - st[0], st[1:], False)

        lax.while_loop(lambda st: _sb_live(st[0], qi, st[1:]), step, (0,) + carries)
        for pr in range(hb // 2):
            o_ref[:, pr * LANES:(pr + 1) * LANES] = jnp.where(is_a, acc[2 * pr], acc[2 * pr + 1])

    width = hb * HEAD_DIM
    nb = D_GRP // width
    slab = lambda part: pl.BlockSpec((s, width), lambda g, qi: (0, part * nb + g))
    blk = pl.BlockSpec((TQ, width), lambda g, qi: (qi, g))
    return pl.pallas_call(
        body, name="sb_fwd", grid=(nb, s // TQ),
        in_specs=[blk, slab(1), slab(2)], out_specs=blk,
        out_shape=pltpu.HBM((s, D_GRP), F32),
        scratch_shapes=[pltpu.VMEM((hb, TQ, LANES), F32)],
        compiler_params=_params(("arbitrary", "arbitrary"), 28),
    )(*_hbm(qkv, qkv, qkv))


def _sb_bwd(qkv, d_o, sums):
    s = qkv.shape[0]
    nq = s // TQ
    nk = s // TK
    n_op = len(sums)
    ride_in, ride_out, ride_shape, ride_sems = _chip_specs(sums)

    def body(q_ref, k_ref, v_ref, do_ref, *refs):
        s_refs = refs[:n_op]
        dq_ref, dk_ref, dv_ref = refs[n_op:n_op + 3]
        l_refs = refs[n_op + 3:2 * n_op + 3]
        x1s, bts, dqacc, dkacc, dvacc, ssem, rsem = refs[2 * n_op + 3:]
        qi = pl.program_id(1)
        first_step = jnp.logical_and(pl.program_id(0) == 0, qi == 0)
        last_step = jnp.logical_and(pl.program_id(0) == pl.num_programs(0) - 1, qi == nq - 1)

        @pl.when(first_step)
        def _():
            for cp in _chip_copies(s_refs, l_refs, ssem, rsem):
                cp.start()

        lane = lax.broadcasted_iota(jnp.int32, (1, LANES), 1)
        is_a = lane < HEAD_DIM

        @pl.when(qi == 0)
        def _():
            dkacc[...] = jnp.zeros_like(dkacc)
            dvacc[...] = jnp.zeros_like(dvacc)

        qs = q_ref[...] * SB_SCALE
        zq = jnp.zeros_like(qs)
        qs_x = (jnp.where(is_a, qs, zq), jnp.where(is_a, zq, qs))
        dob = do_ref[...].astype(BF16)
        do_x = (jnp.where(is_a, dob, zq), jnp.where(is_a, zq, dob))
        r_i = lax.broadcasted_iota(jnp.int32, (TQ, TK), 0)
        c_i = lax.broadcasted_iota(jnp.int32, (TQ, TK), 1)
        past = c_i < r_i
        upper = (r_i > c_i).astype(BF16)
        upper_incl = (r_i >= c_i).astype(BF16)
        dqacc[...] = jnp.zeros_like(dqacc)
        both = ((0, 0), (0, 1), (1, 0), (1, 1))

        def tiles(n):
            j_hi = qi - 2 * n
            lo_ok = j_hi >= 1
            j_lo = jnp.maximum(j_hi - 1, 0)
            ks = (pl.ds(pl.multiple_of(j_hi * TK, TK), TK), pl.ds(pl.multiple_of(j_lo * TK, TK), TK))
            return j_hi, lo_ok, j_lo, ks

        def sweep(n, carries):
            j_hi, lo_ok, j_lo, ks = tiles(n)
            slot = (j_hi, jnp.where(lo_ok, j_lo, nk))
            valid = (jnp.logical_or(past, j_hi < qi), lo_ok)
            z = {th: jnp.where(valid[th[0]], _dot_nt(qs_x[th[1]], k_ref[ks[th[0]], :]), NEG) for th in both}
            log_b, lf_sum, suf = {}, {}, {}
            for th in both:
                lf = -(jnp.maximum(z[th], 0.0) + jnp.log(1.0 + jnp.exp(-jnp.abs(z[th]))))
                log_b[th] = z[th] + lf
                lf_sum[th] = jnp.sum(lf, axis=1, keepdims=True)
                suf[th] = _hl_dot(lf, upper)
            c, g_in = {}, {}
            for h in range(2):
                c[0, h], g_in[0, h] = carries[2 * h], carries[2 * h + 1]
                c[1, h] = c[0, h] + lf_sum[0, h]
            d_a = {th: _dot_nt(do_x[th[1]], v_ref[ks[th[0]], :]) for th in both}
            a_b, g, g_sum, sg = {}, {}, {}, {}
            for th in both:
                a = jnp.exp(log_b[th] + (suf[th] + c[th]))
                a_b[th] = a.astype(BF16)
                g[th] = a * d_a[th]
                g_sum[th] = jnp.sum(g[th], axis=1, keepdims=True)
                sg[th] = _hl_dot(g[th], upper_incl)
            for h in range(2):
                g_in[1, h] = g_in[0, h] + g_sum[0, h]
            for th in both:
                t, h = th
                beta = jnp.exp(log_b[th])
                x1s[slot[t], h] = g[th] * (1.0 - beta) + beta * (sg[th] + g_in[th])
                bts[slot[t], h] = beta
                dvacc[ks[t], :] += _dot_tn(a_b[th], do_x[h])
            out = []
            for h in range(2):
                out.append(c[1, h] + lf_sum[1, h])
                out.append(g_in[1, h] + g_sum[1, h])
            return tuple(out)

        zero = jnp.zeros((TQ, 1), F32)
        first = sweep(0, (zero, zero, zero, zero))

        def more(st):
            return jnp.logical_and(2 * st[0] <= qi, jnp.max(jnp.maximum(st[1], st[3])) > -SB_CUTOFF)

        swept = lax.while_loop(more, lambda st: (st[0] + 1,) + sweep(st[0], st[1:]), (1,) + first)
        g_tot = (swept[2], swept[4])

        def apply(n, carry):
            j_hi, lo_ok, j_lo, ks = tiles(n)

            def one(j, kslice):
                for h in range(2):
                    dz = (x1s[j, h] - bts[j, h] * g_tot[h]).astype(BF16)
                    dqacc[h] += _dot(dz, k_ref[kslice, :])
                    dkacc[kslice, :] += _dot_tn(dz, qs_x[h])

            one(j_hi, ks[0])

            @pl.when(lo_ok)
            def _():
                one(j_lo, ks[1])

            return carry

        lax.fori_loop(0, swept[0], apply, 0)
        dq_ref[...] = (jnp.where(is_a, dqacc[0], dqacc[1]) * SB_SCALE).astype(BF16)

        @pl.when(qi == nq - 1)
        def _():
            dk_ref[...] = dkacc[...].astype(BF16)
            dv_ref[...] = dvacc[...].astype(BF16)

        @pl.when(last_step)
        def _():
            for cp in _chip_copies(s_refs, l_refs, ssem, rsem):
                cp.wait()

    slab = lambda off: pl.BlockSpec((s, LANES), lambda p, qi: (0, off + p))
    blk = pl.BlockSpec((TQ, LANES), lambda p, qi: (qi, p))
    out_slab = pl.BlockSpec((s, LANES), lambda p, qi: (0, p))
    shp = pltpu.HBM((s, D_GRP), BF16)
    return pl.pallas_call(
        body, name="sb_bwd", grid=(4, nq),
        in_specs=[blk, slab(4), slab(8), blk] + ride_in,
        out_specs=[blk, out_slab, out_slab] + ride_out, out_shape=[shp, shp, shp] + ride_shape,
        scratch_shapes=[pltpu.VMEM((nk + 1, 2, TQ, TK), F32)] * 2
        + [pltpu.VMEM((2, TQ, LANES), F32), pltpu.VMEM((s, LANES), F32), pltpu.VMEM((s, LANES), F32)]
        + ride_sems,
        compiler_params=_params(("arbitrary", "arbitrary"), 44),
    )(*_hbm(qkv, qkv, qkv, d_o), *sums)


def _mla_fwd(qp, kp, vv, hb):
    s = qp.shape[0]
    c2 = MLA_SCALE * LOG2_E

    def body(q_ref, k_ref, v_ref, o_ref, lse_ref, vaug, mrun, mb, acc, zbuf):
        qi = pl.program_id(1)
        lane = lax.broadcasted_iota(jnp.int32, (1, LANES), 1)
        is_a = lane < HEAD_DIM

        @pl.when(qi == 0)
        def _():
            for h in range(hb):
                vp = v_ref[:, (h // 2) * LANES:(h // 2 + 1) * LANES]
                mine = is_a if h % 2 == 0 else jnp.logical_not(is_a)
                vaug[h] = jnp.where(mine, vp, jnp.ones_like(vp))

        r_i = lax.broadcasted_iota(jnp.int32, (TQ, TK), 0)
        c_i = lax.broadcasted_iota(jnp.int32, (TQ, TK), 1)
        visible = (c_i >> CHUNK_SHIFT) <= (r_i >> CHUNK_SHIFT)

        def key_rows(j):
            return pl.ds(pl.multiple_of(j * TK, TK), TK)

        def sweep(tiles):
            def loop(n, carry):
                tiles(((2 * n, False), (2 * n + 1, False)))
                return carry

            lax.fori_loop(0, qi // 2, loop, 0)

            @pl.when(qi % 2 == 1)
            def _():
                tiles(((qi - 1, False), (qi, True)))

            @pl.when(qi % 2 == 0)
            def _():
                tiles(((qi, True),))

        mrun[...] = jnp.full_like(mrun, NEG)

        def tiles_max(js):
            zs = [[_dot_nt(q_ref[:, h * LANES:(h + 1) * LANES], k_ref[key_rows(j), h * LANES:(h + 1) * LANES])
                   for h in range(hb)] for j, _ in js]
            for t, (j, diag) in enumerate(js):
                for h in range(hb):
                    z = jnp.where(visible, zs[t][h], NEG) if diag else zs[t][h]
                    zbuf[j, h] = z
                    mrun[h] = jnp.maximum(mrun[h], z)

        sweep(tiles_max)
        for h in range(hb):
            m = jnp.max(mrun[h], axis=1, keepdims=True) * c2
            mb[h] = jnp.broadcast_to(m, (TQ, TK))
        acc[...] = jnp.zeros_like(acc)

        def tiles_pv(js):
            ps = [[jnp.exp2((zbuf[j, h] * c2 - mb[h]).astype(BF16)) for h in range(hb)] for j, _ in js]
            for t, (j, _) in enumerate(js):
                for h in range(hb):
                    acc[h] += _dot(ps[t][h], vaug[h, key_rows(j), :])

        sweep(tiles_pv)
        for pr in range(hb // 2):
            a, b = 2 * pr, 2 * pr + 1
            psl = slice(pr * LANES, (pr + 1) * LANES)
            acc_a, acc_b = acc[a], acc[b]
            l_a = pltpu.roll(acc_a, HEAD_DIM, axis=1)
            l_b = pltpu.roll(acc_b, HEAD_DIM, axis=1)
            o_ref[:, psl] = jnp.where(is_a, acc_a * (1.0 / l_a), acc_b * (1.0 / l_b))
            lse_ref[:, psl] = jnp.where(is_a, mb[a, :, :LANES] * LN_2 + jnp.log(l_a),
                                        mb[b, :, :LANES] * LN_2 + jnp.log(l_b))

    blk = pl.BlockSpec((TQ, hb * HEAD_DIM), lambda g, qi: (qi, g))
    shp = pltpu.HBM((s, D_GRP), F32)
    return pl.pallas_call(
        body, name="mla_fwd", grid=(N_HEADS // hb, s // TQ),
        in_specs=[pl.BlockSpec((TQ, hb * LANES), lambda g, qi: (qi, g)),
                  pl.BlockSpec((s, hb * LANES), lambda g, qi: (0, g)),
                  pl.BlockSpec((s, hb * HEAD_DIM), lambda g, qi: (0, g))],
        out_specs=(blk, blk), out_shape=(shp, shp),
        scratch_shapes=[pltpu.VMEM((hb, s, LANES), BF16), pltpu.VMEM((hb, TQ, TK), F32),
                        pltpu.VMEM((hb, TQ, TK), F32), pltpu.VMEM((hb, TQ, LANES), F32),
                        pltpu.VMEM((s // TK, hb, TQ, TK), F32)],
        compiler_params=_params(("arbitrary", "arbitrary"), 44),
    )(*_hbm(qp, kp, vv))


def _mla_bwd(qp, kp, vv, d_o, o, lse, hb, pays):
    s = qp.shape[0]
    nq = s // TQ
    c2 = MLA_SCALE * LOG2_E
    n_op = len(pays)
    ride_in, ride_out, ride_shape, ride_sems = _pair_specs(pays)

    def body(q_ref, k_ref, v_ref, do_ref, o_ref, lse_ref, *refs):
        g_refs = refs[:n_op]
        dq_ref, dk_ref, dv_ref = refs[n_op:n_op + 3]
        l_refs = refs[n_op + 3:2 * n_op + 3]
        dqacc, lse_b, delta_b, q_t, do_t, ssem, rsem = refs[2 * n_op + 3:]
        qi = pl.program_id(1)

        @pl.when(jnp.logical_and(pl.program_id(0) == 0, qi == 0))
        def _():
            for cp in _pair_copies(g_refs, l_refs, ssem, rsem):
                cp.start()

        lane = lax.broadcasted_iota(jnp.int32, (1, LANES), 1)
        is_a = lane < HEAD_DIM

        @pl.when(qi == 0)
        def _():
            dk_ref[...] = jnp.zeros_like(dk_ref)
            dv_ref[...] = jnp.zeros_like(dv_ref)

        r_i = lax.broadcasted_iota(jnp.int32, (TQ, TK), 0)
        c_i = lax.broadcasted_iota(jnp.int32, (TQ, TK), 1)
        visible = (c_i >> CHUNK_SHIFT) <= (r_i >> CHUNK_SHIFT)
        do_x = []
        for h in range(hb):
            psl = slice((h // 2) * LANES, (h // 2 + 1) * LANES)
            mine = is_a if h % 2 == 0 else jnp.logical_not(is_a)
            d_o = do_ref[:, psl]
            delta = jnp.sum(jnp.where(mine, d_o * o_ref[:, psl], 0.0), axis=1, keepdims=True)
            lse_h = jnp.sum(jnp.where(lane == (h % 2) * HEAD_DIM, lse_ref[:, psl], 0.0), axis=1, keepdims=True)
            lse_b[h] = jnp.broadcast_to(lse_h * LOG2_E, (TQ, TK))
            delta_b[h] = jnp.broadcast_to(delta, (TQ, TK))
            do_h = jnp.where(mine, d_o, 0.0)
            do_x.append(do_h.astype(BF16))
            do_t[h] = do_h.T.astype(BF16)
            q_t[h] = q_ref[:, h * LANES:(h + 1) * LANES].astype(F32).T.astype(BF16)
        dqacc[...] = jnp.zeros_like(dqacc)

        head = lambda h: slice(h * LANES, (h + 1) * LANES)
        pair = lambda h: slice((h // 2) * LANES, (h // 2 + 1) * LANES)

        def tiles(js):
            th = [(j, diag, pl.ds(pl.multiple_of(j * TK, TK), TK), h) for j, diag in js for h in range(hb)]
            zs = [_dot_nt(q_ref[:, head(h)], k_ref[ks, head(h)]) for _, _, ks, h in th]
            dps = [_dot_nt(do_x[h], v_ref[ks, pair(h)]) for _, _, ks, h in th]
            for i, (j, diag, ks, h) in enumerate(th):
                e = zs[i] * c2 - lse_b[h]
                if diag:
                    e = jnp.where(visible, e, NEG)
                p = jnp.exp2(e)
                ds = (p * (dps[i] - delta_b[h]) * MLA_SCALE).astype(BF16)
                dqacc[h] += _dot(ds, k_ref[ks, head(h)])
                dk_ref[head(h), ks] += _dot(q_t[h], ds)
                dv_ref[pair(h), ks] += _dot(do_t[h], p.astype(BF16))

        def loop(n, c):
            tiles(((2 * n, False), (2 * n + 1, False)))
            return c

        lax.fori_loop(0, qi // 2, loop, 0)

        @pl.when(qi % 2 == 1)
        def _():
            tiles(((qi - 1, False), (qi, True)))

        @pl.when(qi % 2 == 0)
        def _():
            tiles(((qi, True),))

        for h in range(hb):
            dq_ref[:, h * LANES:(h + 1) * LANES] = dqacc[h]

        @pl.when(jnp.logical_and(pl.program_id(0) == pl.num_programs(0) - 1, qi == nq - 1))
        def _():
            for cp in _pair_copies(g_refs, l_refs, ssem, rsem):
                cp.wait()

    blk = pl.BlockSpec((TQ, hb * HEAD_DIM), lambda g, qi: (qi, g))
    return pl.pallas_call(
        body, name="mla_bwd", grid=(N_HEADS // hb, nq),
        in_specs=[pl.BlockSpec((TQ, hb * LANES), lambda g, qi: (qi, g)),
                  pl.BlockSpec((s, hb * LANES), lambda g, qi: (0, g)),
                  pl.BlockSpec((s, hb * HEAD_DIM), lambda g, qi: (0, g)), blk, blk, blk] + ride_in,
        out_specs=[pl.BlockSpec((TQ, hb * LANES), lambda g, qi: (qi, g)),
                   pl.BlockSpec((hb * LANES, s), lambda g, qi: (g, 0)),
                   pl.BlockSpec((hb * HEAD_DIM, s), lambda g, qi: (g, 0))] + ride_out,
        out_shape=[pltpu.HBM((s, 1024), F32), pltpu.HBM((1024, s), F32),
                   pltpu.HBM((D_GRP, s), F32)] + ride_shape,
        scratch_shapes=[pltpu.VMEM((hb, TQ, LANES), F32), pltpu.VMEM((hb, TQ, TK), F32),
                        pltpu.VMEM((hb, TQ, TK), F32), pltpu.VMEM((hb, LANES, TQ), BF16),
                        pltpu.VMEM((hb, LANES, TQ), BF16)] + ride_sems,
        compiler_params=_params(("arbitrary", "arbitrary"), 52),
    )(*_hbm(qp, kp, vv, d_o, o, lse), *pays)


def _mid(x, p, target, sb_o, mla_o, rest, g_sb, g_mla, w_out, g_post, w_ple, g_ple, w_pg, b_pg, bd):
    s = x.shape[0]

    def body(x_ref, p_ref, t_ref, sbo_ref, mlo_ref, sbg_ref, mlg_ref, gsb_ref, gml_ref, wout_ref,
             gpost_ref, wple_ref, gple_ref, wpg_ref, bpg_ref, bd_ref,
             dx1_ref, dsbo_ref, dmlo_ref, dsbg_ref, dmlg_ref, x1b_ref, dglb_ref, ycb_ref, dyb_ref,
             pb_ref, dub_ref, small_ref):
        i = pl.program_id(0)
        bd_m = bd_ref[...]

        def seg_mean(v):
            return _dot(v.astype(BF16), bd_m) * (1.0 / HEAD_DIM)

        groups = []
        for o_ref, gate_ref, gain_ref in ((sbo_ref, sbg_ref, gsb_ref), (mlo_ref, mlg_ref, gml_ref)):
            o = o_ref[...]
            r = lax.rsqrt(seg_mean(o * o) + EPS)
            n = o * r
            hn = n * gain_ref[...]
            gate = gate_ref[...]
            sg = _sigmoid(gate)
            si = gate * sg
            groups.append((r, n, hn, gate, sg, si, gain_ref[...]))
        ya = (groups[0][2] * groups[0][5]).astype(BF16)
        yb = (groups[1][2] * groups[1][5]).astype(BF16)
        ycb_ref[:, :D_GRP] = ya
        ycb_ref[:, D_GRP:] = yb
        y = _dot(ya, wout_ref[:D_GRP, :]) + _dot(yb, wout_ref[D_GRP:, :])
        ry = lax.rsqrt(jnp.mean(y * y, axis=-1, keepdims=True) + EPS)
        ny = y * ry
        x1 = x_ref[...] + ny * gpost_ref[...]
        x1b = x1.astype(BF16)
        x1b_ref[...] = x1b
        pb = p_ref[...].astype(BF16)
        pb_ref[...] = pb
        u = _dot(pb, wple_ref[...])
        ru = lax.rsqrt(jnp.mean(u * u, axis=-1, keepdims=True) + EPS)
        nu = u * ru
        ple = nu * gple_ref[...]
        gate = _sigmoid(_dot(x1b, wpg_ref[...]) + bpg_ref[...])
        x2 = x1 + ple * gate
        diff = x2 - t_ref[...]
        dx2 = diff * (1.0 / D_MODEL)

        d_ple = dx2 * gate
        d_glin = (dx2 * ple) * (gate * (1.0 - gate))
        dglb = d_glin.astype(BF16)
        dglb_ref[...] = dglb
        dx1 = dx2 + _dot_nt(dglb, wpg_ref[...])
        dx1_ref[...] = dx1
        d_nu = d_ple * gple_ref[...]
        d_u = ru * (d_nu - nu * jnp.mean(d_nu * nu, axis=-1, keepdims=True))
        dub_ref[...] = d_u.astype(BF16)
        d_ny = dx1 * gpost_ref[...]
        d_y = ry * (d_ny - ny * jnp.mean(d_ny * ny, axis=-1, keepdims=True))
        dyb = d_y.astype(BF16)
        dyb_ref[...] = dyb
        d_yc = (_dot_nt(dyb, wout_ref[:D_GRP, :]), _dot_nt(dyb, wout_ref[D_GRP:, :]))

        d_gain = []
        for gx, (do_ref, dg_ref) in enumerate(((dsbo_ref, dsbg_ref), (dmlo_ref, dmlg_ref))):
            r, n, hn, gate_g, sg, si, gain = groups[gx]
            dyg = d_yc[gx]
            d_hn = dyg * si
            dg_ref[...] = (dyg * hn * (sg * (1.0 + gate_g * (1.0 - sg)))).astype(BF16)
            d_gain.append(jnp.sum(d_hn * n, axis=0, keepdims=True))
            d_n = d_hn * gain
            do_ref[...] = r * (d_n - n * seg_mean(d_n * n))

        @pl.when(i == 0)
        def _():
            small_ref[...] = jnp.zeros_like(small_ref)

        small_ref[3:4, :D_GRP] += d_gain[0]
        small_ref[3:4, D_GRP:] += d_gain[1]
        small_ref[4:5, :] += jnp.sum(dx1 * ny, axis=0, keepdims=True)
        small_ref[5:6, :] += jnp.sum(d_ple * nu, axis=0, keepdims=True)
        small_ref[6:7, :] += jnp.sum(d_glin, axis=0, keepdims=True)
        small_ref[7:8, :] += jnp.sum(diff * diff, axis=0, keepdims=True) * (0.5 / D_MODEL)

    def row(width, idx=0):
        return pl.BlockSpec((TM, width), lambda i: (i, idx))

    def full(a):
        return pl.BlockSpec(a.shape, lambda i: (0, 0))

    f32 = lambda w: pltpu.HBM((s, w), F32)
    b16 = lambda w: pltpu.HBM((s, w), BF16)
    return pl.pallas_call(
        body, name="mid", grid=(s // TM,),
        in_specs=[row(D_MODEL), row(PLE_DIM), row(D_MODEL), row(D_GRP), row(D_GRP),
                  row(D_GRP, 0), row(D_GRP, 1), full(g_sb), full(g_mla), full(w_out), full(g_post),
                  full(w_ple), full(g_ple), full(w_pg), full(b_pg), full(bd)],
        out_specs=(row(D_MODEL), row(D_GRP), row(D_GRP), row(D_GRP), row(D_GRP), row(D_MODEL),
                   row(D_MODEL), row(D_MODEL), row(D_MODEL), row(PLE_DIM), row(D_MODEL),
                   pl.BlockSpec((8, D_MODEL), lambda i: (0, 0))),
        out_shape=(f32(D_MODEL), f32(D_GRP), f32(D_GRP), b16(D_GRP), b16(D_GRP), b16(D_MODEL),
                   b16(D_MODEL), b16(D_MODEL), b16(D_MODEL), b16(PLE_DIM), b16(D_MODEL),
                   jax.ShapeDtypeStruct((8, D_MODEL), F32)),
        compiler_params=_params(("arbitrary",), 46),
    )(*_hbm(x, p, target, sb_o, mla_o, rest, rest), g_sb, g_mla, w_out, g_post, w_ple, g_ple, w_pg, b_pg, bd)


def _mla_prep_bwd(dqp, dkp, dvv, rest, gq, gkv, wuq, wuk, wuv, cos_t, sin_t):
    s = rest.shape[0]

    def body(dqp_ref, dkp_ref, dvv_ref, cq_ref, ckv_ref, gq_ref, gkv_ref, wuq_ref, wuk_ref, wuv_ref,
             c_ref, s_ref, dcq_ref, dckv_ref, dkr_ref, dqb_ref, dkb_ref, dvb_ref, small_ref):
        i = pl.program_id(0)
        lane = lax.broadcasted_iota(jnp.int32, (1, LANES), 1)
        in_rope = (lane >= HEAD_DIM) & (lane < HEAD_DIM + ROPE_DIM)
        cos_v, sin_v = c_ref[...], s_ref[...]
        dkr_roped = jnp.zeros((TM_IO, LANES), F32)
        for h in range(N_HEADS):
            sl = slice(h * LANES, (h + 1) * LANES)
            dy = dqp_ref[:, sl]
            dqb_ref[:, sl] = (dy * cos_v + _rope_swap(dy * sin_v, lane)).astype(BF16)
            dkh = dkp_ref[sl, :].T
            dkb_ref[:, sl] = dkh.astype(BF16)
            dkr_roped = dkr_roped + jnp.where(in_rope, dkh, 0.0)
        dkr_ref[...] = (dkr_roped * cos_v + _rope_swap(dkr_roped * sin_v, lane)).astype(BF16)
        dvb = dvv_ref[...].T.astype(BF16)
        dvb_ref[...] = dvb

        cq = cq_ref[...]
        rq = lax.rsqrt(jnp.mean(cq * cq, axis=-1, keepdims=True) + EPS)
        nq_ = cq * rq
        d_cqn = _dot_nt(dqb_ref[...], wuq_ref[...])
        d_n = d_cqn * gq_ref[...]
        dcq_ref[...] = (rq * (d_n - nq_ * jnp.mean(d_n * nq_, axis=-1, keepdims=True))).astype(BF16)

        ckv = ckv_ref[...]
        rkv = lax.rsqrt(jnp.mean(ckv * ckv, axis=-1, keepdims=True) + EPS)
        nkv = ckv * rkv
        d_ckvn = _dot_nt(dkb_ref[...], wuk_ref[...]) + _dot_nt(dvb, wuv_ref[...])
        d_n2 = d_ckvn * gkv_ref[...]
        dckv_ref[...] = (rkv * (d_n2 - nkv * jnp.mean(d_n2 * nkv, axis=-1, keepdims=True))).astype(BF16)

        @pl.when(i == 0)
        def _():
            small_ref[...] = jnp.zeros_like(small_ref)

        small_ref[0:1, :] += jnp.sum(d_cqn * nq_, axis=0, keepdims=True)
        small_ref[1:2, :KV_LORA] += jnp.sum(d_ckvn * nkv, axis=0, keepdims=True)

    def row(width, idx=0):
        return pl.BlockSpec((TM_IO, width), lambda i: (i, idx))

    def full(a):
        return pl.BlockSpec(a.shape, lambda i: (0, 0))

    b16 = lambda w: pltpu.HBM((s, w), BF16)
    return pl.pallas_call(
        body, name="mla_prep_bwd", grid=(s // TM_IO,),
        in_specs=[row(1024), pl.BlockSpec((1024, TM_IO), lambda i: (0, i)), pl.BlockSpec((D_GRP, TM_IO), lambda i: (0, i)),
                  row(Q_LORA, 4), row(KV_LORA, 10), full(gq), full(gkv),
                  full(wuq), full(wuk), full(wuv), row(LANES), row(LANES)],
        out_specs=(row(Q_LORA), row(KV_LORA), row(LANES), row(1024), row(1024), row(D_GRP),
                   pl.BlockSpec((8, Q_LORA), lambda i: (0, 0))),
        out_shape=(b16(Q_LORA), b16(KV_LORA), b16(LANES), b16(1024), b16(1024), b16(D_GRP),
                   jax.ShapeDtypeStruct((8, Q_LORA), F32)),
        compiler_params=_params(("arbitrary",), 24),
    )(*_hbm(dqp, dkp, dvv, rest, rest), gq, gkv, wuq, wuk, wuv, cos_t, sin_t)


def _in_bwd(x, g, dx1, pieces, w, sums):
    s = x.shape[0]
    steps = s // TM_IO
    widths = [a.shape[1] for a in pieces]
    offs = [sum(widths[:k]) for k in range(len(widths))]
    n_pc, n_op = len(pieces), len(sums)
    ride_in, ride_out, ride_shape, ride_sems = _chip_specs(sums)

    def body(x_ref, g_ref, dx1_ref, *refs):
        piece_refs = refs[:n_pc]
        w_ref = refs[n_pc]
        s_refs = refs[n_pc + 1:n_pc + 1 + n_op]
        dx_ref, small_ref = refs[n_pc + 1 + n_op:n_pc + 3 + n_op]
        l_refs = refs[n_pc + 3 + n_op:n_pc + 3 + 2 * n_op]
        ssem, rsem = refs[n_pc + 3 + 2 * n_op:]
        i = pl.program_id(0)

        @pl.when(i == 0)
        def _():
            for cp in _chip_copies(s_refs, l_refs, ssem, rsem):
                cp.start()

        dh = jnp.zeros((TM_IO, D_MODEL), F32)
        for pr, off, wd in zip(piece_refs, offs, widths):
            dh = dh + _dot_nt(pr[...], w_ref[:, off:off + wd])
        xv = x_ref[...]
        r = lax.rsqrt(jnp.mean(xv * xv, axis=-1, keepdims=True) + EPS)
        n = xv * r
        d_n = dh * g_ref[...]
        dx_ref[...] = dx1_ref[...] + r * (d_n - n * jnp.mean(d_n * n, axis=-1, keepdims=True))

        @pl.when(i == 0)
        def _():
            small_ref[...] = jnp.zeros_like(small_ref)

        small_ref[0:1, :] += jnp.sum(dh * n, axis=0, keepdims=True)

        @pl.when(i == steps - 1)
        def _():
            for cp in _chip_copies(s_refs, l_refs, ssem, rsem):
                cp.wait()

    def row(width):
        return pl.BlockSpec((TM_IO, width), lambda i: (i, 0))

    return pl.pallas_call(
        body, name="in_bwd", grid=(steps,),
        in_specs=[row(D_MODEL), pl.BlockSpec((1, D_MODEL), lambda i: (0, 0)), row(D_MODEL)]
        + [row(wd) for wd in widths] + [pl.BlockSpec(w.shape, lambda i: (0, 0))] + ride_in,
        out_specs=[row(D_MODEL), pl.BlockSpec((8, D_MODEL), lambda i: (0, 0))] + ride_out,
        out_shape=[pltpu.HBM((s, D_MODEL), F32), jax.ShapeDtypeStruct((8, D_MODEL), F32)]
        + ride_shape,
        scratch_shapes=ride_sems,
        compiler_params=_params(("arbitrary",), 40),
    )(*_hbm(x), g, *_hbm(dx1, *pieces), w, *sums)


def _tn_matmul(a, b, name, blocked=False):
    s, k = a.shape
    n = b.shape[1]
    ts = min(s, TS_DW)
    tn = n if blocked else min(n, 512)
    steps = s // ts

    def body(a_ref, b_ref, o_ref):
        t = pl.program_id(1)

        @pl.when(t == 0)
        def _():
            o_ref[...] = jnp.zeros_like(o_ref)

        prod = _dot_tn(a_ref[...], b_ref[...])
        if blocked:
            for j in range(n // LANES):
                o_ref[j] += prod[:, j * LANES:(j + 1) * LANES]
        else:
            o_ref[...] += prod

    if blocked:
        out_spec = pl.BlockSpec((n // LANES, k, LANES), lambda j, t: (0, 0, 0))
        out_shape = jax.ShapeDtypeStruct((n // LANES, k, LANES), F32)
    else:
        out_spec = pl.BlockSpec((k, tn), lambda j, t: (0, j))
        out_shape = jax.ShapeDtypeStruct((k, n), F32)
    return pl.pallas_call(
        body, name=name, grid=(n // tn, steps),
        in_specs=[pl.BlockSpec((ts, k), lambda j, t: (t, 0)), pl.BlockSpec((ts, tn), lambda j, t: (t, j))],
        out_specs=out_spec, out_shape=out_shape,
        compiler_params=_params(("parallel", "arbitrary"), 20),
    )(a, *_hbm(b))


def _tn_matmul_multi(a, bs, name):
    s, k = a.shape
    widths = [b.shape[1] for b in bs]
    ts = min(s, TS_DW)

    def body(a_ref, *refs):
        b_refs, o_ref = refs[:-1], refs[-1]
        t = pl.program_id(0)

        @pl.when(t == 0)
        def _():
            o_ref[...] = jnp.zeros_like(o_ref)

        av = a_ref[...]
        off = 0
        for b_ref, wd in zip(b_refs, widths):
            o_ref[:, off:off + wd] += _dot_tn(av, b_ref[...])
            off += wd

    return pl.pallas_call(
        body, name=name, grid=(s // ts,),
        in_specs=[pl.BlockSpec((ts, k), lambda t: (t, 0))] + [pl.BlockSpec((ts, wd), lambda t: (t, 0)) for wd in widths],
        out_specs=pl.BlockSpec((k, sum(widths)), lambda t: (0, 0)),
        out_shape=jax.ShapeDtypeStruct((k, sum(widths)), F32),
        compiler_params=_params(("arbitrary",), 30),
    )(a, *_hbm(*bs))


IN_SHARD = 372
_IN_KERNEL_ORDER = ((0, 2048), (2464, 2976), (2048, 2432))
_IN_ROPE = (2432, 2464)
_IN_GRAD_SRC = ((0, 512, 0, 0), (512, 1024, 0, 512), (1024, 1536, 1, 0), (1536, 2048, 1, 512),
                (2048, 2304, 2, 512), (2304, 2432, 2, 768), (2432, 2464, 2, 960), (2464, 2976, 2, 0))


def _shard_cols(gath_in, lo, hi):
    out = []
    while lo < hi:
        j, a = divmod(lo, IN_SHARD)
        b = min(IN_SHARD, a + hi - lo)
        out.append(gath_in[j][:, a:b])
        lo += b - a
    return out


def _kernel_w_in(g_in):
    zc = lambda n: jnp.zeros((D_MODEL, n), BF16)
    parts = [pc for lo, hi in _IN_KERNEL_ORDER for pc in _shard_cols(g_in, lo, hi)]
    parts += [zc(64)] + _shard_cols(g_in, *_IN_ROPE) + [zc(32)]
    return jnp.concatenate(parts, axis=1)


def _kernel_weights(gath):
    g_uq, g_ukv, g_out, g_ple, g_pg = gath
    w_uq_p = jnp.pad(g_uq, ((0, 0), (0, 0), (0, 32))).transpose(1, 0, 2).reshape(Q_LORA, 1024)
    k_only = jnp.where(jnp.arange(LANES) < HEAD_DIM, g_ukv, jnp.zeros_like(g_ukv))
    w_uk_p = k_only.transpose(1, 0, 2).reshape(KV_LORA, 1024)
    w_uv = g_ukv[:, :, HEAD_DIM:].transpose(1, 0, 2).reshape(KV_LORA, D_GRP)
    w_ple = g_ple.transpose(1, 0, 2).reshape(PLE_DIM, D_MODEL)
    return (w_uq_p, w_uk_p, w_uv, g_out.reshape(D_MODEL, D_MODEL), w_ple, g_pg.reshape(D_MODEL, D_MODEL))


def _payload_in(d_cols):
    blocks = []
    for j in range(N_DEV):
        lo, hi = j * IN_SHARD, (j + 1) * IN_SHARD
        parts = []
        for o_lo, o_hi, idx, off in _IN_GRAD_SRC:
            a, b = max(lo, o_lo), min(hi, o_hi)
            if a < b:
                parts.append(d_cols[idx][:, off + a - o_lo:off + b - o_lo])
        blocks.append(jnp.concatenate(parts, axis=1))
    return jnp.stack(blocks)


def _payload_ukv(duk_blk, d_uv):
    dv_blk = d_uv.reshape(KV_LORA, N_HEADS, HEAD_DIM).transpose(1, 0, 2)
    return jnp.concatenate([duk_blk[:, :, :HEAD_DIM], dv_blk], axis=2)


def kernel(x, p, positions, norm_pre_g, w_in, q_norm_g, w_uq, kv_norm_g, w_ukv, sb_out_norm_g, mla_out_norm_g, w_out, norm_post_g, w_ple, ple_norm_g, w_ple_gate, b_ple_gate, loss_target, m_norm_pre_g, m_w_in, m_q_norm_g, m_w_uq, m_kv_norm_g, m_w_ukv, m_sb_out_norm_g, m_mla_out_norm_g, m_w_out, m_norm_post_g, m_w_ple, m_ple_norm_g, m_w_ple_gate, m_b_ple_gate, v_norm_pre_g, v_w_in, v_q_norm_g, v_w_uq, v_kv_norm_g, v_w_ukv, v_sb_out_norm_g, v_mla_out_norm_g, v_w_out, v_norm_post_g, v_w_ple, v_ple_norm_g, v_w_ple_gate, v_b_ple_gate):
    mats = (w_in, w_uq, w_ukv, w_out, w_ple, w_ple_gate)
    m_mats = (m_w_in, m_w_uq, m_w_ukv, m_w_out, m_w_ple, m_w_ple_gate)
    v_mats = (v_w_in, v_w_uq, v_w_ukv, v_w_out, v_w_ple, v_w_ple_gate)
    vecs = (norm_pre_g, q_norm_g, kv_norm_g, sb_out_norm_g, mla_out_norm_g, norm_post_g, ple_norm_g, b_ple_gate)
    m_vecs = (m_norm_pre_g, m_q_norm_g, m_kv_norm_g, m_sb_out_norm_g, m_mla_out_norm_g, m_norm_post_g,
              m_ple_norm_g, m_b_ple_gate)
    v_vecs = (v_norm_pre_g, v_q_norm_g, v_kv_norm_g, v_sb_out_norm_g, v_mla_out_norm_g, v_norm_post_g,
              v_ple_norm_g, v_b_ple_gate)

    shards = [a[0].astype(BF16) for a in mats]
    w_in_p = _kernel_w_in(_all_gather(shards[:1])[0])
    grad_x, reduced, vec_slab = _step(x[0], p[0, 0], positions[0], loss_target[0], *vecs, w_in_p, shards[1:])
    upd = [_adamw_matrix(own, l2, w, m, v, "adamw_%d" % o)
           for o, ((own, l2), w, m, v) in enumerate(zip(reduced, mats, m_mats, v_mats))]
    sm = _adamw_vectors(_slab_exchange(vec_slab), vecs, m_vecs, v_vecs)

    outs = []
    for kind in range(4):
        mat = [upd[o][kind] for o in range(len(mats))]
        vec = sm[1 + 8 * kind:9 + 8 * kind]
        outs += [vec[0], mat[0], vec[1], mat[1], vec[2], mat[2], vec[3], vec[4], mat[3], vec[5],
                 mat[4], vec[6], mat[5], vec[7]]
    return (sm[0][0, 0], grad_x[None], *outs)


def _step(xs, ps, pos, tgt, norm_pre_g, q_norm_g, kv_norm_g, sb_out_norm_g, mla_out_norm_g,
          norm_post_g, ple_norm_g, b_ple_gate, w_in_p, shards):
    s = xs.shape[0]
    place = jnp.stack([lax.axis_index("c"), 2 * lax.axis_index("x") + lax.axis_index("y")]).astype(jnp.int32)

    half = ROPE_DIM // 2
    freq = ROPE_THETA ** (-jnp.arange(half, dtype=F32) / half)
    ang = pos.astype(F32)[:, None] * freq
    cos, sin = jnp.cos(ang), jnp.sin(ang)
    cos_t = jnp.concatenate([jnp.ones((s, 64), F32), cos, cos, jnp.zeros((s, 32), F32)], axis=1)
    sin_t = jnp.concatenate([jnp.zeros((s, 64), F32), -sin, sin, jnp.zeros((s, 32), F32)], axis=1)
    seg = jnp.arange(D_GRP) // HEAD_DIM
    bd = (seg[:, None] == seg[None, :]).astype(BF16)

    qkv, rest, h_b, *gath = _in_proj(xs, norm_pre_g, w_in_p, shards)
    w_uq_p, w_uk_p, w_uv, f_out, f_ple, f_pg = _kernel_weights(gath)
    sb_o = _sb_fwd(qkv, 8)
    qp, kp, vv, cqn_b, ckvn_b = _mla_prep(rest, q_norm_g, kv_norm_g, w_uq_p, w_uk_p, w_uv, cos_t, sin_t)
    mla_o, lse = _mla_fwd(qp, kp, vv, 4)

    (dx1, d_sbo, d_mlo, d_sbg, d_mlg, x1_b, dgl_b, yc_b, dy_b, p_b, du_b, small_mid) = _mid(
        xs, ps, tgt, sb_o, mla_o, rest, sb_out_norm_g, mla_out_norm_g, f_out, norm_post_g,
        f_ple, ple_norm_g, f_pg, b_ple_gate, bd)
    pay_a = [_tn_matmul(yc_b, dy_b, "dw_out").reshape(N_DEV, 128, D_MODEL),
             _tn_matmul(p_b, du_b, "dw_ple", blocked=True),
             _tn_matmul(x1_b, dgl_b, "dw_pg").reshape(N_DEV, 128, D_MODEL)]
    dqp, dkp, dvv, *sib_a = _mla_bwd(qp, kp, vv, d_mlo, mla_o, lse, 4, pay_a)
    pair_a = _pair_sums(pay_a, sib_a, place, "grad_pair_sums_a")
    dq_sb, dk_sb, dv_sb, *landed_a = _sb_bwd(qkv, d_sbo, [sm for sm, _ in pair_a])
    dcq, dckv, dkr, dq_b, dk_b, dv_b, small_prep = _mla_prep_bwd(
        dqp, dkp, dvv, rest, q_norm_g, kv_norm_g, w_uq_p, w_uk_p, w_uv, cos_t, sin_t)
    pieces = [dq_sb, dk_sb, dv_sb, d_sbg, d_mlg, dcq, dckv, dkr]
    d_cols = [_tn_matmul_multi(h_b, pieces[0:2], "dw_in_0"), _tn_matmul_multi(h_b, pieces[2:4], "dw_in_1"),
              _tn_matmul_multi(h_b, pieces[4:8], "dw_in_2")]
    pay_b = [_payload_in(d_cols), _tn_matmul(cqn_b, dq_b, "dw_uq", blocked=True),
             _payload_ukv(_tn_matmul(ckvn_b, dk_b, "dw_uk", blocked=True), _tn_matmul(ckvn_b, dv_b, "dw_uv"))]
    pair_b = _pair_sums(pay_b, _pair_exchange(pay_b, "grad_pair_exchange"), place, "grad_pair_sums_b")
    grad_x, small_in, *landed_b = _in_bwd(xs, norm_pre_g, dx1, pieces, w_in_p, [sm for sm, _ in pair_b])
    reduced = [(own, l2) for (_, own), l2 in zip(pair_b + pair_a, landed_b + landed_a)]
    slab = jnp.concatenate([small_in[0:1], jnp.pad(small_prep[0:2], ((0, 0), (0, D_MODEL - Q_LORA))),
                            small_mid[3:8]], axis=0)
    return grad_x, reduced, slab
```

```python
import jax
import jax.numpy as jnp
from jax import lax
from jax.experimental import pallas as pl
from jax.experimental.pallas import tpu as pltpu

F32 = jnp.float32
BF16 = jnp.bfloat16
MESH = pl.DeviceIdType.MESH

N_DEV = 8
D_MODEL = 1024
N_HEADS = 8
HEAD_DIM = 64
D_GRP = N_HEADS * HEAD_DIM
Q_LORA = 256
KV_LORA = 128
ROPE_DIM = 32
PLE_DIM = 256
CHUNK_SHIFT = 6
ROPE_THETA = 10000.0
EPS = 1e-6
SB_SCALE = HEAD_DIM ** -0.5
MLA_SCALE = (HEAD_DIM + ROPE_DIM) ** -0.5
NEG = -1e30
LOG2_E = 1.4426950408889634
LN_2 = 0.6931471805599453
SB_CUTOFF = 110.0

ADAM_LR = 0.001
ADAM_B1 = 0.9
ADAM_B2 = 0.999
ADAM_EPS = 1e-08
ADAM_WD = 0.01
ADAM_STEP = 10

LANES = 128
TQ = 256
TK = 256
TM = 256
TM_IO = 512
TS_DW = 2048
TS_RING = 1024
RING_SLOTS = 3

D_IN_P = 3072

_NT = (((1,), (1,)), ((), ()))
_TN = (((0,), (0,)), ((), ()))


def _params(sem, vmem_mb):
    return pltpu.CompilerParams(dimension_semantics=sem, vmem_limit_bytes=vmem_mb << 20)


def _hbm(*arrays):
    return [pltpu.with_memory_space_constraint(a, pltpu.HBM) for a in arrays]


def _dot(a, b):
    return jnp.dot(a, b, preferred_element_type=F32)


def _dot_nt(a, b):
    return lax.dot_general(a, b, _NT, preferred_element_type=F32)


def _dot_tn(a, b):
    return lax.dot_general(a, b, _TN, preferred_element_type=F32)


def _hl_dot(a, b):
    hi = a.astype(BF16)
    lo = (a - hi.astype(F32)).astype(BF16)
    return _dot(hi, b) + _dot(lo, b)


def _sigmoid(x):
    return 1.0 / (1.0 + jnp.exp(-x))


def _rope_swap(x, lane):
    left = pltpu.roll(x, LANES - 16, axis=1)
    right = pltpu.roll(x, 16, axis=1)
    lo = (lane >= 64) & (lane < 80)
    hi = (lane >= 80) & (lane < 96)
    return jnp.where(lo, left, jnp.where(hi, right, 0.0))


def _two_level_gather(x_refs, out_refs, send_sems, recv_sems, local_sems):
    x, y, c = lax.axis_index("x"), lax.axis_index("y"), lax.axis_index("c")
    me, sibling = (x, y, c), (x, y, 1 - c)
    chips = [(1 - x, y), (x, 1 - y), (1 - x, 1 - y)]
    ops = range(len(x_refs))

    def slot(o, px, py, pc):
        return out_refs[o].at[4 * px + 2 * py + pc]

    def copy(o, k, block, to, src=None):
        return pltpu.make_async_remote_copy(
            src_ref=slot(o, *block) if src is None else src, dst_ref=slot(o, *block),
            send_sem=send_sems.at[o, k], recv_sem=recv_sems.at[o, k],
            device_id=to, device_id_type=MESH)

    def mine():
        return [pltpu.make_async_copy(x_refs[o], slot(o, *me), local_sems.at[o]) for o in ops]

    def first():
        return ([copy(o, 0, me, sibling, src=x_refs[o]) for o in ops]
                + [copy(o, 1 + j, me, (*chip, c), src=x_refs[o]) for j, chip in enumerate(chips) for o in ops])

    def start():
        for cp in mine() + first():
            cp.start()

    def finish():
        passed = []
        for j, chip in enumerate(chips):
            for o in ops:
                copy(o, 1 + j, (*chip, c), me).wait_recv()
                passed.append(copy(o, 4 + j, (*chip, c), sibling))
                passed[-1].start()
        for o in ops:
            copy(o, 0, sibling, me).wait_recv()
        for j, chip in enumerate(chips):
            for o in ops:
                copy(o, 4 + j, (*chip, 1 - c), me).wait_recv()
        for cp in first() + passed:
            cp.wait_send()
        for cp in mine():
            cp.wait()

    return start, finish


def _gather_sems(n_op):
    return [pltpu.SemaphoreType.DMA((n_op, 7)), pltpu.SemaphoreType.DMA((n_op, 7)),
            pltpu.SemaphoreType.DMA((n_op,))]


def _all_gather(shards):
    n_op = len(shards)

    def body(*refs):
        start, finish = _two_level_gather(refs[:n_op], refs[n_op:2 * n_op], *refs[2 * n_op:])
        start()
        finish()

    any_spec = pl.BlockSpec(memory_space=pl.ANY)
    return pl.pallas_call(
        body, name="weight_all_gather",
        out_shape=[jax.ShapeDtypeStruct((N_DEV,) + a.shape, a.dtype) for a in shards],
        in_specs=[any_spec] * n_op, out_specs=[any_spec] * n_op, scratch_shapes=_gather_sems(n_op),
        compiler_params=pltpu.CompilerParams(vmem_limit_bytes=4 << 20),
    )(*shards)


def _pair_copies(g_refs, l_refs, ssem, rsem):
    x, y, c = lax.axis_index("x"), lax.axis_index("y"), lax.axis_index("c")
    copies = []
    for o in range(len(g_refs)):
        for chip in range(4):
            copies.append(pltpu.make_async_remote_copy(
                src_ref=g_refs[o].at[2 * chip + (1 - c)], dst_ref=l_refs[o].at[chip],
                send_sem=ssem.at[o, chip], recv_sem=rsem.at[o, chip],
                device_id=(x, y, 1 - c), device_id_type=MESH))
    return copies


def _pair_specs(pays):
    n_op = len(pays)
    any_spec = pl.BlockSpec(memory_space=pl.ANY)
    return ([any_spec] * n_op, [any_spec] * n_op,
            [jax.ShapeDtypeStruct((4,) + a.shape[1:], F32) for a in pays],
            [pltpu.SemaphoreType.DMA((n_op, 4)), pltpu.SemaphoreType.DMA((n_op, 4))])


def _pair_exchange(pays, name):
    n_op = len(pays)
    in_specs, out_specs, out_shape, sems = _pair_specs(pays)

    def body(*refs):
        copies = _pair_copies(refs[:n_op], refs[n_op:2 * n_op], *refs[2 * n_op:])
        for cp in copies:
            cp.start()
        for cp in copies:
            cp.wait()

    return pl.pallas_call(body, name=name, out_shape=out_shape, in_specs=in_specs, out_specs=out_specs,
                          scratch_shapes=sems,
                          compiler_params=pltpu.CompilerParams(vmem_limit_bytes=4 << 20))(*pays)


def _slab_exchange(small):
    sr, n = small.shape

    def body(s_ref, sland_ref, ssem, rsem, lsem):
        x, y, c = lax.axis_index("x"), lax.axis_index("y"), lax.axis_index("c")
        me = 4 * x + 2 * y + c
        copies = []
        for k in range(1, N_DEV):
            peer = (1 - x if (k >> 2) & 1 else x, 1 - y if (k >> 1) & 1 else y, 1 - c if k & 1 else c)
            copies.append(pltpu.make_async_remote_copy(
                src_ref=s_ref, dst_ref=sland_ref.at[me], send_sem=ssem.at[k], recv_sem=rsem.at[k],
                device_id=peer, device_id_type=MESH))
        own = pltpu.make_async_copy(s_ref, sland_ref.at[me], lsem)
        own.start()
        for cp in copies:
            cp.start()
        for cp in copies:
            cp.wait()
        own.wait()

    any_spec = pl.BlockSpec(memory_space=pl.ANY)
    return pl.pallas_call(
        body, name="grad_slab_exchange", out_shape=jax.ShapeDtypeStruct((N_DEV, sr, n), F32),
        in_specs=[any_spec], out_specs=any_spec,
        scratch_shapes=[pltpu.SemaphoreType.DMA((N_DEV,)), pltpu.SemaphoreType.DMA((N_DEV,)),
                        pltpu.SemaphoreType.DMA],
        compiler_params=pltpu.CompilerParams(vmem_limit_bytes=4 << 20),
    )(small)


def _pair_sums(pays, landed, place, name):
    n = len(pays)
    dims = [p.shape[1:] for p in pays]

    def body(place_ref, *refs):
        g_refs, l_refs, s_refs, own_refs = refs[:n], refs[n:2 * n], refs[2 * n:3 * n], refs[3 * n:]
        i = pl.program_id(0)
        for o in range(n):
            tot = g_refs[o][...] + l_refs[o][...]
            s_refs[o][...] = tot.astype(BF16)

            @pl.when(i == place_ref[1])
            def _(o=o, tot=tot):
                own_refs[o][...] = tot

    grid_spec = pltpu.PrefetchScalarGridSpec(
        num_scalar_prefetch=1, grid=(4,),
        in_specs=[pl.BlockSpec((None, r, c), lambda i, pr: (2 * i + pr[0], 0, 0)) for r, c in dims]
        + [pl.BlockSpec((None, r, c), lambda i, pr: (i, 0, 0)) for r, c in dims],
        out_specs=[pl.BlockSpec((None, r, c), lambda i, pr: (i, 0, 0)) for r, c in dims]
        + [pl.BlockSpec((r, c), lambda i, pr: (0, 0)) for r, c in dims])
    out = pl.pallas_call(
        body, name=name, grid_spec=grid_spec,
        out_shape=[jax.ShapeDtypeStruct((4, r, c), BF16) for r, c in dims]
        + [jax.ShapeDtypeStruct((r, c), F32) for r, c in dims],
        compiler_params=_params(("arbitrary",), 16),
    )(place, *pays, *landed)
    return list(zip(out[:n], out[n:]))


def _chip_copies(s_refs, l_refs, ssem, rsem):
    x, y, c = lax.axis_index("x"), lax.axis_index("y"), lax.axis_index("c")
    copies = []
    for rel in range(1, 4):
        px = 1 - x if rel & 2 else x
        py = 1 - y if rel & 1 else y
        for o in range(len(s_refs)):
            copies.append(pltpu.make_async_remote_copy(
                src_ref=s_refs[o].at[2 * px + py], dst_ref=l_refs[o].at[rel - 1],
                send_sem=ssem.at[o, rel - 1], recv_sem=rsem.at[o, rel - 1],
                device_id=(px, py, c), device_id_type=MESH))
    return copies


def _chip_specs(sums):
    n_op = len(sums)
    any_spec = pl.BlockSpec(memory_space=pl.ANY)
    return ([any_spec] * n_op, [any_spec] * n_op,
            [jax.ShapeDtypeStruct((3,) + a.shape[1:], BF16) for a in sums],
            [pltpu.SemaphoreType.DMA((n_op, 3)), pltpu.SemaphoreType.DMA((n_op, 3))])


def _adamw_math(g, w, m, v):
    mn = ADAM_B1 * m + (1.0 - ADAM_B1) * g
    vn = ADAM_B2 * v + (1.0 - ADAM_B2) * (g * g)
    m_hat = mn / (1.0 - ADAM_B1 ** ADAM_STEP)
    v_hat = vn / (1.0 - ADAM_B2 ** ADAM_STEP)
    return -ADAM_LR * (m_hat / (jnp.sqrt(v_hat) + ADAM_EPS) + ADAM_WD * w), mn, vn


def _adamw_matrix(own, landed, w, m, v, name):
    _, r, c = w.shape
    cp = own.shape[1]
    br = min(r, 256)

    def body(own_ref, l_ref, w_ref, m_ref, v_ref, g_out, d_out, m_out, v_out):
        g = own_ref[...]
        for k in range(3):
            g = g + l_ref[k].astype(F32)
        g = g[:, :c]
        g_out[...] = g
        d_out[...], m_out[...], v_out[...] = _adamw_math(g, w_ref[...], m_ref[...], v_ref[...])

    row = pl.BlockSpec((None, br, c), lambda i: (0, i, 0))
    shp = jax.ShapeDtypeStruct((1, r, c), F32)
    return pl.pallas_call(
        body, name=name, grid=(r // br,),
        in_specs=[pl.BlockSpec((br, cp), lambda i: (i, 0)), pl.BlockSpec((3, br, cp), lambda i: (0, i, 0)),
                  row, row, row],
        out_specs=(row, row, row, row), out_shape=(shp, shp, shp, shp),
        compiler_params=_params(("parallel",), 12),
    )(own, landed, w, m, v)


_VEC_PLACE = ((0, 0), (1, 0), (2, 0), (3, 0), (3, D_GRP), (4, 0), (5, 0), (6, 0))


def _adamw_vectors(sland, ws, ms, vs):
    nv = len(ws)

    def body(l_ref, *refs):
        w_refs, m_refs, v_refs = refs[:nv], refs[nv:2 * nv], refs[2 * nv:3 * nv]
        loss_ref = refs[3 * nv]
        outs = refs[3 * nv + 1:]
        g_all = l_ref[0]
        for j in range(1, N_DEV):
            g_all = g_all + l_ref[j]
        loss_ref[...] = jnp.sum(g_all[7:8, :], axis=1, keepdims=True)
        for k, (row, lane0) in enumerate(_VEC_PLACE):
            n = w_refs[k].shape[1]
            g = g_all[row:row + 1, lane0:lane0 + n]
            d, mn, vn = _adamw_math(g, w_refs[k][...], m_refs[k][...], v_refs[k][...])
            outs[k][...] = g
            outs[nv + k][...] = d
            outs[2 * nv + k][...] = mn
            outs[3 * nv + k][...] = vn

    def whole(shape):
        return pl.BlockSpec(shape, lambda i: (0,) * len(shape))

    shapes = [jax.ShapeDtypeStruct(w.shape, F32) for w in ws]
    return pl.pallas_call(
        body, name="adamw_vectors", grid=(1,),
        in_specs=[whole(sland.shape)] + [whole(w.shape) for w in ws] * 3,
        out_specs=[whole((1, 1))] + [whole(w.shape) for w in ws] * 4,
        out_shape=[jax.ShapeDtypeStruct((1, 1), F32)] + shapes * 4,
        compiler_params=_params(("arbitrary",), 4),
    )(sland, *ws, *ms, *vs)


def _in_proj(x, g, w, shards):
    s = x.shape[0]
    n_op = len(shards)
    steps = s // TM_IO

    def body(x_ref, g_ref, w_ref, *refs):
        shard_refs = refs[:n_op]
        qkv_ref, rest_ref, h_ref = refs[n_op:n_op + 3]
        gath_refs = refs[n_op + 3:2 * n_op + 3]
        start, finish = _two_level_gather(shard_refs, gath_refs, *refs[2 * n_op + 3:])
        i = pl.program_id(0)

        @pl.when(i == 0)
        def _():
            start()

        xv = x_ref[...]
        r = lax.rsqrt(jnp.mean(xv * xv, axis=-1, keepdims=True) + EPS)
        h = ((xv * r) * g_ref[...]).astype(BF16)
        h_ref[...] = h
        qkv_ref[...] = _dot(h, w_ref[:, :1536]).astype(BF16)
        rest_ref[...] = _dot(h, w_ref[:, 1536:])

        @pl.when(i == steps - 1)
        def _():
            finish()

    any_spec = pl.BlockSpec(memory_space=pl.ANY)
    return pl.pallas_call(
        body, name="in_proj", grid=(steps,),
        in_specs=[pl.BlockSpec((TM_IO, D_MODEL), lambda i: (i, 0)),
                  pl.BlockSpec((1, D_MODEL), lambda i: (0, 0)),
                  pl.BlockSpec((D_MODEL, D_IN_P), lambda i: (0, 0))] + [any_spec] * n_op,
        out_specs=[pl.BlockSpec((TM_IO, 1536), lambda i: (i, 0)),
                   pl.BlockSpec((TM_IO, 1536), lambda i: (i, 0)),
                   pl.BlockSpec((TM_IO, D_MODEL), lambda i: (i, 0))] + [any_spec] * n_op,
        out_shape=[pltpu.HBM((s, 1536), BF16), pltpu.HBM((s, 1536), F32),
                   pltpu.HBM((s, D_MODEL), BF16)]
        + [jax.ShapeDtypeStruct((N_DEV,) + a.shape, a.dtype) for a in shards],
        scratch_shapes=_gather_sems(n_op),
        compiler_params=_params(("arbitrary",), 32),
    )(x, g, w, *shards)


def _mla_prep(rest, gq, gkv, wuq, wuk, wuv, cos_t, sin_t):
    s = rest.shape[0]

    def body(cq_ref, ckv_ref, kr_ref, gq_ref, gkv_ref, wuq_ref, wuk_ref, wuv_ref, c_ref, s_ref,
             qp_ref, kp_ref, vv_ref, cqn_ref, ckvn_ref):
        lane = lax.broadcasted_iota(jnp.int32, (1, LANES), 1)
        cos_v, sin_v = c_ref[...], s_ref[...]
        cq = cq_ref[...]
        rq = lax.rsqrt(jnp.mean(cq * cq, axis=-1, keepdims=True) + EPS)
        cqn = ((cq * rq) * gq_ref[...]).astype(BF16)
        cqn_ref[...] = cqn
        q = _dot(cqn, wuq_ref[...])
        ckv = ckv_ref[...]
        rkv = lax.rsqrt(jnp.mean(ckv * ckv, axis=-1, keepdims=True) + EPS)
        ckvn = ((ckv * rkv) * gkv_ref[...]).astype(BF16)
        ckvn_ref[...] = ckvn
        kn = _dot(ckvn, wuk_ref[...])
        vv_ref[...] = _dot(ckvn, wuv_ref[...]).astype(BF16)
        kr = kr_ref[...]
        kr_roped = kr * cos_v + _rope_swap(kr, lane) * sin_v
        for h in range(N_HEADS):
            sl = slice(h * LANES, (h + 1) * LANES)
            qh = q[:, sl]
            qp_ref[:, sl] = (qh * cos_v + _rope_swap(qh, lane) * sin_v).astype(BF16)
            kp_ref[:, sl] = (kn[:, sl] + kr_roped).astype(BF16)

    def row(width, idx):
        return pl.BlockSpec((TM_IO, width), lambda i: (i, idx))

    def full(a):
        return pl.BlockSpec(a.shape, lambda i: (0, 0))

    return pl.pallas_call(
        body, name="mla_prep", grid=(s // TM_IO,),
        in_specs=[row(Q_LORA, 4), row(KV_LORA, 10), row(LANES, 11), full(gq), full(gkv),
                  full(wuq), full(wuk), full(wuv), row(LANES, 0), row(LANES, 0)],
        out_specs=(row(1024, 0), row(1024, 0), row(D_GRP, 0), row(Q_LORA, 0), row(KV_LORA, 0)),
        out_shape=(pltpu.HBM((s, 1024), BF16), pltpu.HBM((s, 1024), BF16),
                   pltpu.HBM((s, D_GRP), BF16), pltpu.HBM((s, Q_LORA), BF16),
                   pltpu.HBM((s, KV_LORA), BF16)),
        compiler_params=_params(("parallel",), 13),
    )(*_hbm(rest, rest, rest), gq, gkv, wuq, wuk, wuv, cos_t, sin_t)


def _sb_live(n, qi, carries):
    top = carries[0]
    for c in carries[1:]:
        top = jnp.maximum(top, c)
    return jnp.logical_and(n < qi, jnp.max(top) > -SB_CUTOFF)


def _sb_fwd(qkv, hb):
    s = qkv.shape[0]

    def body(q_ref, k_ref, v_ref, o_ref, acc):
        qi = pl.program_id(1)
        lane = lax.broadcasted_iota(jnp.int32, (1, LANES), 1)
        is_a = lane < HEAD_DIM
        pair = lambda h: slice((h // 2) * LANES, (h // 2 + 1) * LANES)
        q_h = []
        for h in range(hb):
            qs = q_ref[:, pair(h)] * SB_SCALE
            mine = is_a if h % 2 == 0 else jnp.logical_not(is_a)
            q_h.append(jnp.where(mine, qs, jnp.zeros_like(qs)))
        r_i = lax.broadcasted_iota(jnp.int32, (TQ, TK), 0)
        c_i = lax.broadcasted_iota(jnp.int32, (TQ, TK), 1)
        past = c_i < r_i
        upper = (r_i > c_i).astype(BF16)
        acc[...] = jnp.zeros_like(acc)

        def tile(j, carries, diag):
            ks = pl.ds(pl.multiple_of(j * TK, TK), TK)
            zs = [_dot_nt(q_h[h], k_ref[ks, pair(h)]) for h in range(hb)]
            if diag:
                zs = [jnp.where(past, z, NEG) for z in zs]
            lfs = [-(jnp.maximum(z, 0.0) + jnp.log(1.0 + jnp.exp(-jnp.abs(z)))) for z in zs]
            sufs = [_hl_dot(lfs[h], upper) for h in range(hb)]
            out = []
            for h in range(hb):
                w = jnp.exp(zs[h] + lfs[h] + (sufs[h] + carries[h]))
                acc[h] += _dot(w.astype(BF16), v_ref[ks, pair(h)])
                out.append(carries[h] + jnp.sum(lfs[h], axis=1, keepdims=True))
            return tuple(out)

        zero = jnp.zeros((TQ, 1), F32)
        carries = tile(qi, (zero,) * hb, True)

        def step(st):
            return (st[0] + 1,) + tile(qi - 1 - st[0], st[1:], False)

        lax.while_loop(lambda st: _sb_live(st[0], qi, st[1:]), step, (0,) + carries)
        for pr in range(hb // 2):
            o_ref[:, pr * LANES:(pr + 1) * LANES] = jnp.where(is_a, acc[2 * pr], acc[2 * pr + 1])

    width = hb * HEAD_DIM
    nb = D_GRP // width
    slab = lambda part: pl.BlockSpec((s, width), lambda g, qi: (0, part * nb + g))
    blk = pl.BlockSpec((TQ, width), lambda g, qi: (qi, g))
    return pl.pallas_call(
        body, name="sb_fwd", grid=(nb, s // TQ),
        in_specs=[blk, slab(1), slab(2)], out_specs=blk,
        out_shape=pltpu.HBM((s, D_GRP), F32),
        scratch_shapes=[pltpu.VMEM((hb, TQ, LANES), F32)],
        compiler_params=_params(("arbitrary", "arbitrary"), 28),
    )(*_hbm(qkv, qkv, qkv))


def _sb_bwd(qkv, d_o, sums):
    s = qkv.shape[0]
    nq = s // TQ
    nk = s // TK
    n_op = len(sums)
    ride_in, ride_out, ride_shape, ride_sems = _chip_specs(sums)

    def body(q_ref, k_ref, v_ref, do_ref, *refs):
        s_refs = refs[:n_op]
        dq_ref, dk_ref, dv_ref = refs[n_op:n_op + 3]
        l_refs = refs[n_op + 3:2 * n_op + 3]
        x1s, bts, dqacc, dkacc, dvacc, ssem, rsem = refs[2 * n_op + 3:]
        qi = pl.program_id(1)
        first_step = jnp.logical_and(pl.program_id(0) == 0, qi == 0)
        last_step = jnp.logical_and(pl.program_id(0) == pl.num_programs(0) - 1, qi == nq - 1)

        @pl.when(first_step)
        def _():
            for cp in _chip_copies(s_refs, l_refs, ssem, rsem):
                cp.start()

        lane = lax.broadcasted_iota(jnp.int32, (1, LANES), 1)
        is_a = lane < HEAD_DIM

        @pl.when(qi == 0)
        def _():
            dkacc[...] = jnp.zeros_like(dkacc)
            dvacc[...] = jnp.zeros_like(dvacc)

        qs = q_ref[...] * SB_SCALE
        zq = jnp.zeros_like(qs)
        qs_x = (jnp.where(is_a, qs, zq), jnp.where(is_a, zq, qs))
        dob = do_ref[...].astype(BF16)
        do_x = (jnp.where(is_a, dob, zq), jnp.where(is_a, zq, dob))
        r_i = lax.broadcasted_iota(jnp.int32, (TQ, TK), 0)
        c_i = lax.broadcasted_iota(jnp.int32, (TQ, TK), 1)
        past = c_i < r_i
        upper = (r_i > c_i).astype(BF16)
        upper_incl = (r_i >= c_i).astype(BF16)
        dqacc[...] = jnp.zeros_like(dqacc)
        both = ((0, 0), (0, 1), (1, 0), (1, 1))

        def tiles(n):
            j_hi = qi - 2 * n
            lo_ok = j_hi >= 1
            j_lo = jnp.maximum(j_hi - 1, 0)
            ks = (pl.ds(pl.multiple_of(j_hi * TK, TK), TK), pl.ds(pl.multiple_of(j_lo * TK, TK), TK))
            return j_hi, lo_ok, j_lo, ks

        def sweep(n, carries):
            j_hi, lo_ok, j_lo, ks = tiles(n)
            slot = (j_hi, jnp.where(lo_ok, j_lo, nk))
            valid = (jnp.logical_or(past, j_hi < qi), lo_ok)
            z = {th: jnp.where(valid[th[0]], _dot_nt(qs_x[th[1]], k_ref[ks[th[0]], :]), NEG) for th in both}
            log_b, lf_sum, suf = {}, {}, {}
            for th in both:
                lf = -(jnp.maximum(z[th], 0.0) + jnp.log(1.0 + jnp.exp(-jnp.abs(z[th]))))
                log_b[th] = z[th] + lf
                lf_sum[th] = jnp.sum(lf, axis=1, keepdims=True)
                suf[th] = _hl_dot(lf, upper)
            c, g_in = {}, {}
            for h in range(2):
                c[0, h], g_in[0, h] = carries[2 * h], carries[2 * h + 1]
                c[1, h] = c[0, h] + lf_sum[0, h]
            d_a = {th: _dot_nt(do_x[th[1]], v_ref[ks[th[0]], :]) for th in both}
            a_b, g, g_sum, sg = {}, {}, {}, {}
            for th in both:
                a = jnp.exp(log_b[th] + (suf[th] + c[th]))
                a_b[th] = a.astype(BF16)
                g[th] = a * d_a[th]
                g_sum[th] = jnp.sum(g[th], axis=1, keepdims=True)
                sg[th] = _hl_dot(g[th], upper_incl)
            for h in range(2):
                g_in[1, h] = g_in[0, h] + g_sum[0, h]
            for th in both:
                t, h = th
                beta = jnp.exp(log_b[th])
                x1s[slot[t], h] = g[th] * (1.0 - beta) + beta * (sg[th] + g_in[th])
                bts[slot[t], h] = beta
                dvacc[ks[t], :] += _dot_tn(a_b[th], do_x[h])
            out = []
            for h in range(2):
                out.append(c[1, h] + lf_sum[1, h])
                out.append(g_in[1, h] + g_sum[1, h])
            return tuple(out)

        zero = jnp.zeros((TQ, 1), F32)
        first = sweep(0, (zero, zero, zero, zero))

        def more(st):
            return jnp.logical_and(2 * st[0] <= qi, jnp.max(jnp.maximum(st[1], st[3])) > -SB_CUTOFF)

        swept = lax.while_loop(more, lambda st: (st[0] + 1,) + sweep(st[0], st[1:]), (1,) + first)
        g_tot = (swept[2], swept[4])

        def apply(n, carry):
            j_hi, lo_ok, j_lo, ks = tiles(n)

            def one(j, kslice):
                for h in range(2):
                    dz = (x1s[j, h] - bts[j, h] * g_tot[h]).astype(BF16)
                    dqacc[h] += _dot(dz, k_ref[kslice, :])
                    dkacc[kslice, :] += _dot_tn(dz, qs_x[h])

            one(j_hi, ks[0])

            @pl.when(lo_ok)
            def _():
                one(j_lo, ks[1])

            return carry

        lax.fori_loop(0, swept[0], apply, 0)
        dq_ref[...] = (jnp.where(is_a, dqacc[0], dqacc[1]) * SB_SCALE).astype(BF16)

        @pl.when(qi == nq - 1)
        def _():
            dk_ref[...] = dkacc[...].astype(BF16)
            dv_ref[...] = dvacc[...].astype(BF16)

        @pl.when(last_step)
        def _():
            for cp in _chip_copies(s_refs, l_refs, ssem, rsem):
                cp.wait()

    slab = lambda off: pl.BlockSpec((s, LANES), lambda p, qi: (0, off + p))
    blk = pl.BlockSpec((TQ, LANES), lambda p, qi: (qi, p))
    out_slab = pl.BlockSpec((s, LANES), lambda p, qi: (0, p))
    shp = pltpu.HBM((s, D_GRP), BF16)
    return pl.pallas_call(
        body, name="sb_bwd", grid=(4, nq),
        in_specs=[blk, slab(4), slab(8), blk] + ride_in,
        out_specs=[blk, out_slab, out_slab] + ride_out, out_shape=[shp, shp, shp] + ride_shape,
        scratch_shapes=[pltpu.VMEM((nk + 1, 2, TQ, TK), F32)] * 2
        + [pltpu.VMEM((2, TQ, LANES), F32), pltpu.VMEM((s, LANES), F32), pltpu.VMEM((s, LANES), F32)]
        + ride_sems,
        compiler_params=_params(("arbitrary", "arbitrary"), 44),
    )(*_hbm(qkv, qkv, qkv, d_o), *sums)


def _mla_fwd(qp, kp, vv, hb):
    s = qp.shape[0]
    c2 = MLA_SCALE * LOG2_E

    def body(q_ref, k_ref, v_ref, o_ref, lse_ref, vaug, mrun, mb, acc, zbuf):
        qi = pl.program_id(1)
        lane = lax.broadcasted_iota(jnp.int32, (1, LANES), 1)
        is_a = lane < HEAD_DIM

        @pl.when(qi == 0)
        def _():
            for h in range(hb):
                vp = v_ref[:, (h // 2) * LANES:(h // 2 + 1) * LANES]
                mine = is_a if h % 2 == 0 else jnp.logical_not(is_a)
                vaug[h] = jnp.where(mine, vp, jnp.ones_like(vp))

        r_i = lax.broadcasted_iota(jnp.int32, (TQ, TK), 0)
        c_i = lax.broadcasted_iota(jnp.int32, (TQ, TK), 1)
        visible = (c_i >> CHUNK_SHIFT) <= (r_i >> CHUNK_SHIFT)

        def key_rows(j):
            return pl.ds(pl.multiple_of(j * TK, TK), TK)

        def sweep(tiles):
            def loop(n, carry):
                tiles(((2 * n, False), (2 * n + 1, False)))
                return carry

            lax.fori_loop(0, qi // 2, loop, 0)

            @pl.when(qi % 2 == 1)
            def _():
                tiles(((qi - 1, False), (qi, True)))

            @pl.when(qi % 2 == 0)
            def _():
                tiles(((qi, True),))

        mrun[...] = jnp.full_like(mrun, NEG)

        def tiles_max(js):
            zs = [[_dot_nt(q_ref[:, h * LANES:(h + 1) * LANES], k_ref[key_rows(j), h * LANES:(h + 1) * LANES])
                   for h in range(hb)] for j, _ in js]
            for t, (j, diag) in enumerate(js):
                for h in range(hb):
                    z = jnp.where(visible, zs[t][h], NEG) if diag else zs[t][h]
                    zbuf[j, h] = z
                    mrun[h] = jnp.maximum(mrun[h], z)

        sweep(tiles_max)
        for h in range(hb):
            m = jnp.max(mrun[h], axis=1, keepdims=True) * c2
            mb[h] = jnp.broadcast_to(m, (TQ, TK))
        acc[...] = jnp.zeros_like(acc)

        def tiles_pv(js):
            ps = [[jnp.exp2((zbuf[j, h] * c2 - mb[h]).astype(BF16)) for h in range(hb)] for j, _ in js]
            for t, (j, _) in enumerate(js):
                for h in range(hb):
                    acc[h] += _dot(ps[t][h], vaug[h, key_rows(j), :])

        sweep(tiles_pv)
        for pr in range(hb // 2):
            a, b = 2 * pr, 2 * pr + 1
            psl = slice(pr * LANES, (pr + 1) * LANES)
            acc_a, acc_b = acc[a], acc[b]
            l_a = pltpu.roll(acc_a, HEAD_DIM, axis=1)
            l_b = pltpu.roll(acc_b, HEAD_DIM, axis=1)
            o_ref[:, psl] = jnp.where(is_a, acc_a * (1.0 / l_a), acc_b * (1.0 / l_b))
            lse_ref[:, psl] = jnp.where(is_a, mb[a, :, :LANES] * LN_2 + jnp.log(l_a),
                                        mb[b, :, :LANES] * LN_2 + jnp.log(l_b))

    blk = pl.BlockSpec((TQ, hb * HEAD_DIM), lambda g, qi: (qi, g))
    shp = pltpu.HBM((s, D_GRP), F32)
    return pl.pallas_call(
        body, name="mla_fwd", grid=(N_HEADS // hb, s // TQ),
        in_specs=[pl.BlockSpec((TQ, hb * LANES), lambda g, qi: (qi, g)),
                  pl.BlockSpec((s, hb * LANES), lambda g, qi: (0, g)),
                  pl.BlockSpec((s, hb * HEAD_DIM), lambda g, qi: (0, g))],
        out_specs=(blk, blk), out_shape=(shp, shp),
        scratch_shapes=[pltpu.VMEM((hb, s, LANES), BF16), pltpu.VMEM((hb, TQ, TK), F32),
                        pltpu.VMEM((hb, TQ, TK), F32), pltpu.VMEM((hb, TQ, LANES), F32),
                        pltpu.VMEM((s // TK, hb, TQ, TK), F32)],
        compiler_params=_params(("arbitrary", "arbitrary"), 44),
    )(*_hbm(qp, kp, vv))


def _mla_bwd(qp, kp, vv, d_o, o, lse, hb, pays):
    s = qp.shape[0]
    nq = s // TQ
    c2 = MLA_SCALE * LOG2_E
    n_op = len(pays)
    ride_in, ride_out, ride_shape, ride_sems = _pair_specs(pays)

    def body(q_ref, k_ref, v_ref, do_ref, o_ref, lse_ref, *refs):
        g_refs = refs[:n_op]
        dq_ref, dk_ref, dv_ref = refs[n_op:n_op + 3]
        l_refs = refs[n_op + 3:2 * n_op + 3]
        dqacc, lse_b, delta_b, q_t, do_t, ssem, rsem = refs[2 * n_op + 3:]
        qi = pl.program_id(1)

        @pl.when(jnp.logical_and(pl.program_id(0) == 0, qi == 0))
        def _():
            for cp in _pair_copies(g_refs, l_refs, ssem, rsem):
                cp.start()

        lane = lax.broadcasted_iota(jnp.int32, (1, LANES), 1)
        is_a = lane < HEAD_DIM

        @pl.when(qi == 0)
        def _():
            dk_ref[...] = jnp.zeros_like(dk_ref)
            dv_ref[...] = jnp.zeros_like(dv_ref)

        r_i = lax.broadcasted_iota(jnp.int32, (TQ, TK), 0)
        c_i = lax.broadcasted_iota(jnp.int32, (TQ, TK), 1)
        visible = (c_i >> CHUNK_SHIFT) <= (r_i >> CHUNK_SHIFT)
        do_x = []
        for h in range(hb):
            psl = slice((h // 2) * LANES, (h // 2 + 1) * LANES)
            mine = is_a if h % 2 == 0 else jnp.logical_not(is_a)
            d_o = do_ref[:, psl]
            delta = jnp.sum(jnp.where(mine, d_o * o_ref[:, psl], 0.0), axis=1, keepdims=True)
            lse_h = jnp.sum(jnp.where(lane == (h % 2) * HEAD_DIM, lse_ref[:, psl], 0.0), axis=1, keepdims=True)
            lse_b[h] = jnp.broadcast_to(lse_h * LOG2_E, (TQ, TK))
            delta_b[h] = jnp.broadcast_to(delta, (TQ, TK))
            do_h = jnp.where(mine, d_o, 0.0)
            do_x.append(do_h.astype(BF16))
            do_t[h] = do_h.T.astype(BF16)
            q_t[h] = q_ref[:, h * LANES:(h + 1) * LANES].astype(F32).T.astype(BF16)
        dqacc[...] = jnp.zeros_like(dqacc)

        head = lambda h: slice(h * LANES, (h + 1) * LANES)
        pair = lambda h: slice((h // 2) * LANES, (h // 2 + 1) * LANES)

        def tiles(js):
            th = [(j, diag, pl.ds(pl.multiple_of(j * TK, TK), TK), h) for j, diag in js for h in range(hb)]
            zs = [_dot_nt(q_ref[:, head(h)], k_ref[ks, head(h)]) for _, _, ks, h in th]
            dps = [_dot_nt(do_x[h], v_ref[ks, pair(h)]) for _, _, ks, h in th]
            for i, (j, diag, ks, h) in enumerate(th):
                e = zs[i] * c2 - lse_b[h]
                if diag:
                    e = jnp.where(visible, e, NEG)
                p = jnp.exp2(e)
                ds = (p * (dps[i] - delta_b[h]) * MLA_SCALE).astype(BF16)
                dqacc[h] += _dot(ds, k_ref[ks, head(h)])
                dk_ref[head(h), ks] += _dot(q_t[h], ds)
                dv_ref[pair(h), ks] += _dot(do_t[h], p.astype(BF16))

        def loop(n, c):
            tiles(((2 * n, False), (2 * n + 1, False)))
            return c

        lax.fori_loop(0, qi // 2, loop, 0)

        @pl.when(qi % 2 == 1)
        def _():
            tiles(((qi - 1, False), (qi, True)))

        @pl.when(qi % 2 == 0)
        def _():
            tiles(((qi, True),))

        for h in range(hb):
            dq_ref[:, h * LANES:(h + 1) * LANES] = dqacc[h]

        @pl.when(jnp.logical_and(pl.program_id(0) == pl.num_programs(0) - 1, qi == nq - 1))
        def _():
            for cp in _pair_copies(g_refs, l_refs, ssem, rsem):
                cp.wait()

    blk = pl.BlockSpec((TQ, hb * HEAD_DIM), lambda g, qi: (qi, g))
    return pl.pallas_call(
        body, name="mla_bwd", grid=(N_HEADS // hb, nq),
        in_specs=[pl.BlockSpec((TQ, hb * LANES), lambda g, qi: (qi, g)),
                  pl.BlockSpec((s, hb * LANES), lambda g, qi: (0, g)),
                  pl.BlockSpec((s, hb * HEAD_DIM), lambda g, qi: (0, g)), blk, blk, blk] + ride_in,
        out_specs=[pl.BlockSpec((TQ, hb * LANES), lambda g, qi: (qi, g)),
                   pl.BlockSpec((hb * LANES, s), lambda g, qi: (g, 0)),
                   pl.BlockSpec((hb * HEAD_DIM, s), lambda g, qi: (g, 0))] + ride_out,
        out_shape=[pltpu.HBM((s, 1024), F32), pltpu.HBM((1024, s), F32),
                   pltpu.HBM((D_GRP, s), F32)] + ride_shape,
        scratch_shapes=[pltpu.VMEM((hb, TQ, LANES), F32), pltpu.VMEM((hb, TQ, TK), F32),
                        pltpu.VMEM((hb, TQ, TK), F32), pltpu.VMEM((hb, LANES, TQ), BF16),
                        pltpu.VMEM((hb, LANES, TQ), BF16)] + ride_sems,
        compiler_params=_params(("arbitrary", "arbitrary"), 52),
    )(*_hbm(qp, kp, vv, d_o, o, lse), *pays)


def _mid(x, p, target, sb_o, mla_o, rest, g_sb, g_mla, w_out, g_post, w_ple, g_ple, w_pg, b_pg, bd):
    s = x.shape[0]

    def body(x_ref, p_ref, t_ref, sbo_ref, mlo_ref, sbg_ref, mlg_ref, gsb_ref, gml_ref, wout_ref,
             gpost_ref, wple_ref, gple_ref, wpg_ref, bpg_ref, bd_ref,
             dx1_ref, dsbo_ref, dmlo_ref, dsbg_ref, dmlg_ref, x1b_ref, dglb_ref, ycb_ref, dyb_ref,
             pb_ref, dub_ref, small_ref):
        i = pl.program_id(0)
        bd_m = bd_ref[...]

        def seg_mean(v):
            return _dot(v.astype(BF16), bd_m) * (1.0 / HEAD_DIM)

        groups = []
        for o_ref, gate_ref, gain_ref in ((sbo_ref, sbg_ref, gsb_ref), (mlo_ref, mlg_ref, gml_ref)):
            o = o_ref[...]
            r = lax.rsqrt(seg_mean(o * o) + EPS)
            n = o * r
            hn = n * gain_ref[...]
            gate = gate_ref[...]
            sg = _sigmoid(gate)
            si = gate * sg
            groups.append((r, n, hn, gate, sg, si, gain_ref[...]))
        ya = (groups[0][2] * groups[0][5]).astype(BF16)
        yb = (groups[1][2] * groups[1][5]).astype(BF16)
        ycb_ref[:, :D_GRP] = ya
        ycb_ref[:, D_GRP:] = yb
        y = _dot(ya, wout_ref[:D_GRP, :]) + _dot(yb, wout_ref[D_GRP:, :])
        ry = lax.rsqrt(jnp.mean(y * y, axis=-1, keepdims=True) + EPS)
        ny = y * ry
        x1 = x_ref[...] + ny * gpost_ref[...]
        x1b = x1.astype(BF16)
        x1b_ref[...] = x1b
        pb = p_ref[...].astype(BF16)
        pb_ref[...] = pb
        u = _dot(pb, wple_ref[...])
        ru = lax.rsqrt(jnp.mean(u * u, axis=-1, keepdims=True) + EPS)
        nu = u * ru
        ple = nu * gple_ref[...]
        gate = _sigmoid(_dot(x1b, wpg_ref[...]) + bpg_ref[...])
        x2 = x1 + ple * gate
        diff = x2 - t_ref[...]
        dx2 = diff * (1.0 / D_MODEL)

        d_ple = dx2 * gate
        d_glin = (dx2 * ple) * (gate * (1.0 - gate))
        dglb = d_glin.astype(BF16)
        dglb_ref[...] = dglb
        dx1 = dx2 + _dot_nt(dglb, wpg_ref[...])
        dx1_ref[...] = dx1
        d_nu = d_ple * gple_ref[...]
        d_u = ru * (d_nu - nu * jnp.mean(d_nu * nu, axis=-1, keepdims=True))
        dub_ref[...] = d_u.astype(BF16)
        d_ny = dx1 * gpost_ref[...]
        d_y = ry * (d_ny - ny * jnp.mean(d_ny * ny, axis=-1, keepdims=True))
        dyb = d_y.astype(BF16)
        dyb_ref[...] = dyb
        d_yc = (_dot_nt(dyb, wout_ref[:D_GRP, :]), _dot_nt(dyb, wout_ref[D_GRP:, :]))

        d_gain = []
        for gx, (do_ref, dg_ref) in enumerate(((dsbo_ref, dsbg_ref), (dmlo_ref, dmlg_ref))):
            r, n, hn, gate_g, sg, si, gain = groups[gx]
            dyg = d_yc[gx]
            d_hn = dyg * si
            dg_ref[...] = (dyg * hn * (sg * (1.0 + gate_g * (1.0 - sg)))).astype(BF16)
            d_gain.append(jnp.sum(d_hn * n, axis=0, keepdims=True))
            d_n = d_hn * gain
            do_ref[...] = r * (d_n - n * seg_mean(d_n * n))

        @pl.when(i == 0)
        def _():
            small_ref[...] = jnp.zeros_like(small_ref)

        small_ref[3:4, :D_GRP] += d_gain[0]
        small_ref[3:4, D_GRP:] += d_gain[1]
        small_ref[4:5, :] += jnp.sum(dx1 * ny, axis=0, keepdims=True)
        small_ref[5:6, :] += jnp.sum(d_ple * nu, axis=0, keepdims=True)
        small_ref[6:7, :] += jnp.sum(d_glin, axis=0, keepdims=True)
        small_ref[7:8, :] += jnp.sum(diff * diff, axis=0, keepdims=True) * (0.5 / D_MODEL)

    def row(width, idx=0):
        return pl.BlockSpec((TM, width), lambda i: (i, idx))

    def full(a):
        return pl.BlockSpec(a.shape, lambda i: (0, 0))

    f32 = lambda w: pltpu.HBM((s, w), F32)
    b16 = lambda w: pltpu.HBM((s, w), BF16)
    return pl.pallas_call(
        body, name="mid", grid=(s // TM,),
        in_specs=[row(D_MODEL), row(PLE_DIM), row(D_MODEL), row(D_GRP), row(D_GRP),
                  row(D_GRP, 0), row(D_GRP, 1), full(g_sb), full(g_mla), full(w_out), full(g_post),
                  full(w_ple), full(g_ple), full(w_pg), full(b_pg), full(bd)],
        out_specs=(row(D_MODEL), row(D_GRP), row(D_GRP), row(D_GRP), row(D_GRP), row(D_MODEL),
                   row(D_MODEL), row(D_MODEL), row(D_MODEL), row(PLE_DIM), row(D_MODEL),
                   pl.BlockSpec((8, D_MODEL), lambda i: (0, 0))),
        out_shape=(f32(D_MODEL), f32(D_GRP), f32(D_GRP), b16(D_GRP), b16(D_GRP), b16(D_MODEL),
                   b16(D_MODEL), b16(D_MODEL), b16(D_MODEL), b16(PLE_DIM), b16(D_MODEL),
                   jax.ShapeDtypeStruct((8, D_MODEL), F32)),
        compiler_params=_params(("arbitrary",), 46),
    )(*_hbm(x, p, target, sb_o, mla_o, rest, rest), g_sb, g_mla, w_out, g_post, w_ple, g_ple, w_pg, b_pg, bd)


def _mla_prep_bwd(dqp, dkp, dvv, rest, gq, gkv, wuq, wuk, wuv, cos_t, sin_t):
    s = rest.shape[0]

    def body(dqp_ref, dkp_ref, dvv_ref, cq_ref, ckv_ref, gq_ref, gkv_ref, wuq_ref, wuk_ref, wuv_ref,
             c_ref, s_ref, dcq_ref, dckv_ref, dkr_ref, dqb_ref, dkb_ref, dvb_ref, small_ref):
        i = pl.program_id(0)
        lane = lax.broadcasted_iota(jnp.int32, (1, LANES), 1)
        in_rope = (lane >= HEAD_DIM) & (lane < HEAD_DIM + ROPE_DIM)
        cos_v, sin_v = c_ref[...], s_ref[...]
        dkr_roped = jnp.zeros((TM_IO, LANES), F32)
        for h in range(N_HEADS):
            sl = slice(h * LANES, (h + 1) * LANES)
            dy = dqp_ref[:, sl]
            dqb_ref[:, sl] = (dy * cos_v + _rope_swap(dy * sin_v, lane)).astype(BF16)
            dkh = dkp_ref[sl, :].T
            dkb_ref[:, sl] = dkh.astype(BF16)
            dkr_roped = dkr_roped + jnp.where(in_rope, dkh, 0.0)
        dkr_ref[...] = (dkr_roped * cos_v + _rope_swap(dkr_roped * sin_v, lane)).astype(BF16)
        dvb = dvv_ref[...].T.astype(BF16)
        dvb_ref[...] = dvb

        cq = cq_ref[...]
        rq = lax.rsqrt(jnp.mean(cq * cq, axis=-1, keepdims=True) + EPS)
        nq_ = cq * rq
        d_cqn = _dot_nt(dqb_ref[...], wuq_ref[...])
        d_n = d_cqn * gq_ref[...]
        dcq_ref[...] = (rq * (d_n - nq_ * jnp.mean(d_n * nq_, axis=-1, keepdims=True))).astype(BF16)

        ckv = ckv_ref[...]
        rkv = lax.rsqrt(jnp.mean(ckv * ckv, axis=-1, keepdims=True) + EPS)
        nkv = ckv * rkv
        d_ckvn = _dot_nt(dkb_ref[...], wuk_ref[...]) + _dot_nt(dvb, wuv_ref[...])
        d_n2 = d_ckvn * gkv_ref[...]
        dckv_ref[...] = (rkv * (d_n2 - nkv * jnp.mean(d_n2 * nkv, axis=-1, keepdims=True))).astype(BF16)

        @pl.when(i == 0)
        def _():
            small_ref[...] = jnp.zeros_like(small_ref)

        small_ref[0:1, :] += jnp.sum(d_cqn * nq_, axis=0, keepdims=True)
        small_ref[1:2, :KV_LORA] += jnp.sum(d_ckvn * nkv, axis=0, keepdims=True)

    def row(width, idx=0):
        return pl.BlockSpec((TM_IO, width), lambda i: (i, idx))

    def full(a):
        return pl.BlockSpec(a.shape, lambda i: (0, 0))

    b16 = lambda w: pltpu.HBM((s, w), BF16)
    return pl.pallas_call(
        body, name="mla_prep_bwd", grid=(s // TM_IO,),
        in_specs=[row(1024), pl.BlockSpec((1024, TM_IO), lambda i: (0, i)), pl.BlockSpec((D_GRP, TM_IO), lambda i: (0, i)),
                  row(Q_LORA, 4), row(KV_LORA, 10), full(gq), full(gkv),
                  full(wuq), full(wuk), full(wuv), row(LANES), row(LANES)],
        out_specs=(row(Q_LORA), row(KV_LORA), row(LANES), row(1024), row(1024), row(D_GRP),
                   pl.BlockSpec((8, Q_LORA), lambda i: (0, 0))),
        out_shape=(b16(Q_LORA), b16(KV_LORA), b16(LANES), b16(1024), b16(1024), b16(D_GRP),
                   jax.ShapeDtypeStruct((8, Q_LORA), F32)),
        compiler_params=_params(("arbitrary",), 24),
    )(*_hbm(dqp, dkp, dvv, rest, rest), gq, gkv, wuq, wuk, wuv, cos_t, sin_t)


def _in_bwd(x, g, dx1, pieces, w, sums):
    s = x.shape[0]
    steps = s // TM_IO
    widths = [a.shape[1] for a in pieces]
    offs = [sum(widths[:k]) for k in range(len(widths))]
    n_pc, n_op = len(pieces), len(sums)
    ride_in, ride_out, ride_shape, ride_sems = _chip_specs(sums)

    def body(x_ref, g_ref, dx1_ref, *refs):
        piece_refs = refs[:n_pc]
        w_ref = refs[n_pc]
        s_refs = refs[n_pc + 1:n_pc + 1 + n_op]
        dx_ref, small_ref = refs[n_pc + 1 + n_op:n_pc + 3 + n_op]
        l_refs = refs[n_pc + 3 + n_op:n_pc + 3 + 2 * n_op]
        ssem, rsem = refs[n_pc + 3 + 2 * n_op:]
        i = pl.program_id(0)

        @pl.when(i == 0)
        def _():
            for cp in _chip_copies(s_refs, l_refs, ssem, rsem):
                cp.start()

        dh = jnp.zeros((TM_IO, D_MODEL), F32)
        for pr, off, wd in zip(piece_refs, offs, widths):
            dh = dh + _dot_nt(pr[...], w_ref[:, off:off + wd])
        xv = x_ref[...]
        r = lax.rsqrt(jnp.mean(xv * xv, axis=-1, keepdims=True) + EPS)
        n = xv * r
        d_n = dh * g_ref[...]
        dx_ref[...] = dx1_ref[...] + r * (d_n - n * jnp.mean(d_n * n, axis=-1, keepdims=True))

        @pl.when(i == 0)
        def _():
            small_ref[...] = jnp.zeros_like(small_ref)

        small_ref[0:1, :] += jnp.sum(dh * n, axis=0, keepdims=True)

        @pl.when(i == steps - 1)
        def _():
            for cp in _chip_copies(s_refs, l_refs, ssem, rsem):
                cp.wait()

    def row(width):
        return pl.BlockSpec((TM_IO, width), lambda i: (i, 0))

    return pl.pallas_call(
        body, name="in_bwd", grid=(steps,),
        in_specs=[row(D_MODEL), pl.BlockSpec((1, D_MODEL), lambda i: (0, 0)), row(D_MODEL)]
        + [row(wd) for wd in widths] + [pl.BlockSpec(w.shape, lambda i: (0, 0))] + ride_in,
        out_specs=[row(D_MODEL), pl.BlockSpec((8, D_MODEL), lambda i: (0, 0))] + ride_out,
        out_shape=[pltpu.HBM((s, D_MODEL), F32), jax.ShapeDtypeStruct((8, D_MODEL), F32)]
        + ride_shape,
        scratch_shapes=ride_sems,
        compiler_params=_params(("arbitrary",), 40),
    )(*_hbm(x), g, *_hbm(dx1, *pieces), w, *sums)


def _tn_matmul(a, b, name, blocked=False):
    s, k = a.shape
    n = b.shape[1]
    ts = min(s, TS_DW)
    tn = n if blocked else min(n, 512)
    steps = s // ts

    def body(a_ref, b_ref, o_ref):
        t = pl.program_id(1)

        @pl.when(t == 0)
        def _():
            o_ref[...] = jnp.zeros_like(o_ref)

        prod = _dot_tn(a_ref[...], b_ref[...])
        if blocked:
            for j in range(n // LANES):
                o_ref[j] += prod[:, j * LANES:(j + 1) * LANES]
        else:
            o_ref[...] += prod

    if blocked:
        out_spec = pl.BlockSpec((n // LANES, k, LANES), lambda j, t: (0, 0, 0))
        out_shape = jax.ShapeDtypeStruct((n // LANES, k, LANES), F32)
    else:
        out_spec = pl.BlockSpec((k, tn), lambda j, t: (0, j))
        out_shape = jax.ShapeDtypeStruct((k, n), F32)
    return pl.pallas_call(
        body, name=name, grid=(n // tn, steps),
        in_specs=[pl.BlockSpec((ts, k), lambda j, t: (t, 0)), pl.BlockSpec((ts, tn), lambda j, t: (t, j))],
        out_specs=out_spec, out_shape=out_shape,
        compiler_params=_params(("parallel", "arbitrary"), 20),
    )(*_hbm(a, b))


def _tn_matmul_ring(pairs, name):
    s, k = pairs[0][0].shape
    n = pairs[0][1].shape[1]
    n_pr = len(pairs)
    sched = [(p, c) for p in range(n_pr) for c in range(s // TS_RING)]

    def body(*refs):
        a_refs, b_refs = refs[0:2 * n_pr:2], refs[1:2 * n_pr:2]
        o_refs = refs[2 * n_pr:3 * n_pr]
        abuf, bbuf, sem = refs[3 * n_pr:]

        def copies(i):
            p, c = sched[i]
            rows = pl.ds(c * TS_RING, TS_RING)
            slot = i % RING_SLOTS
            return (pltpu.make_async_copy(a_refs[p].at[rows, :], abuf.at[slot], sem.at[0, slot]),
                    pltpu.make_async_copy(b_refs[p].at[rows, :], bbuf.at[slot], sem.at[1, slot]))

        for i in range(min(RING_SLOTS - 1, len(sched))):
            for cp in copies(i):
                cp.start()
        for i, (p, c) in enumerate(sched):
            if i + RING_SLOTS - 1 < len(sched):
                for cp in copies(i + RING_SLOTS - 1):
                    cp.start()
            for cp in copies(i):
                cp.wait()
            prod = _dot_tn(abuf[i % RING_SLOTS], bbuf[i % RING_SLOTS])
            if c == 0:
                o_refs[p][...] = prod
            else:
                o_refs[p][...] += prod

    return pl.pallas_call(
        body, name=name,
        in_specs=[pl.BlockSpec(memory_space=pl.ANY)] * (2 * n_pr),
        out_specs=[pl.BlockSpec(memory_space=pltpu.VMEM)] * n_pr,
        out_shape=[jax.ShapeDtypeStruct((k, n), F32)] * n_pr,
        scratch_shapes=[pltpu.VMEM((RING_SLOTS, TS_RING, k), BF16), pltpu.VMEM((RING_SLOTS, TS_RING, n), BF16),
                        pltpu.SemaphoreType.DMA((2, RING_SLOTS))],
        compiler_params=pltpu.CompilerParams(vmem_limit_bytes=36 << 20),
    )(*_hbm(*[x for pr in pairs for x in pr]))


def _tn_matmul_multi(a, bs, name):
    s, k = a.shape
    widths = [b.shape[1] for b in bs]
    ts = min(s, TS_DW)

    def body(a_ref, *refs):
        b_refs, o_ref = refs[:-1], refs[-1]
        t = pl.program_id(0)

        @pl.when(t == 0)
        def _():
            o_ref[...] = jnp.zeros_like(o_ref)

        av = a_ref[...]
        off = 0
        for b_ref, wd in zip(b_refs, widths):
            o_ref[:, off:off + wd] += _dot_tn(av, b_ref[...])
            off += wd

    return pl.pallas_call(
        body, name=name, grid=(s // ts,),
        in_specs=[pl.BlockSpec((ts, k), lambda t: (t, 0))] + [pl.BlockSpec((ts, wd), lambda t: (t, 0)) for wd in widths],
        out_specs=pl.BlockSpec((k, sum(widths)), lambda t: (0, 0)),
        out_shape=jax.ShapeDtypeStruct((k, sum(widths)), F32),
        compiler_params=_params(("arbitrary",), 30),
    )(a, *_hbm(*bs))


IN_SHARD = 372
_IN_KERNEL_ORDER = ((0, 2048), (2464, 2976), (2048, 2432))
_IN_ROPE = (2432, 2464)
_IN_GRAD_SRC = ((0, 512, 0, 0), (512, 1024, 0, 512), (1024, 1536, 1, 0), (1536, 2048, 1, 512),
                (2048, 2304, 2, 512), (2304, 2432, 2, 768), (2432, 2464, 2, 960), (2464, 2976, 2, 0))


def _shard_cols(gath_in, lo, hi):
    out = []
    while lo < hi:
        j, a = divmod(lo, IN_SHARD)
        b = min(IN_SHARD, a + hi - lo)
        out.append(gath_in[j][:, a:b])
        lo += b - a
    return out


def _kernel_w_in(g_in):
    zc = lambda n: jnp.zeros((D_MODEL, n), BF16)
    parts = [pc for lo, hi in _IN_KERNEL_ORDER for pc in _shard_cols(g_in, lo, hi)]
    parts += [zc(64)] + _shard_cols(g_in, *_IN_ROPE) + [zc(32)]
    return jnp.concatenate(parts, axis=1)


def _kernel_weights(gath):
    g_uq, g_ukv, g_out, g_ple, g_pg = gath
    w_uq_p = jnp.pad(g_uq, ((0, 0), (0, 0), (0, 32))).transpose(1, 0, 2).reshape(Q_LORA, 1024)
    k_only = jnp.where(jnp.arange(LANES) < HEAD_DIM, g_ukv, jnp.zeros_like(g_ukv))
    w_uk_p = k_only.transpose(1, 0, 2).reshape(KV_LORA, 1024)
    w_uv = g_ukv[:, :, HEAD_DIM:].transpose(1, 0, 2).reshape(KV_LORA, D_GRP)
    w_ple = g_ple.transpose(1, 0, 2).reshape(PLE_DIM, D_MODEL)
    return (w_uq_p, w_uk_p, w_uv, g_out.reshape(D_MODEL, D_MODEL), w_ple, g_pg.reshape(D_MODEL, D_MODEL))


def _payload_in(d_cols):
    blocks = []
    for j in range(N_DEV):
        lo, hi = j * IN_SHARD, (j + 1) * IN_SHARD
        parts = []
        for o_lo, o_hi, idx, off in _IN_GRAD_SRC:
            a, b = max(lo, o_lo), min(hi, o_hi)
            if a < b:
                parts.append(d_cols[idx][:, off + a - o_lo:off + b - o_lo])
        blocks.append(jnp.concatenate(parts, axis=1))
    return jnp.stack(blocks)


def _payload_ukv(duk_blk, d_uv):
    dv_blk = d_uv.reshape(KV_LORA, N_HEADS, HEAD_DIM).transpose(1, 0, 2)
    return jnp.concatenate([duk_blk[:, :, :HEAD_DIM], dv_blk], axis=2)


def kernel(x, p, positions, norm_pre_g, w_in, q_norm_g, w_uq, kv_norm_g, w_ukv, sb_out_norm_g, mla_out_norm_g, w_out, norm_post_g, w_ple, ple_norm_g, w_ple_gate, b_ple_gate, loss_target, m_norm_pre_g, m_w_in, m_q_norm_g, m_w_uq, m_kv_norm_g, m_w_ukv, m_sb_out_norm_g, m_mla_out_norm_g, m_w_out, m_norm_post_g, m_w_ple, m_ple_norm_g, m_w_ple_gate, m_b_ple_gate, v_norm_pre_g, v_w_in, v_q_norm_g, v_w_uq, v_kv_norm_g, v_w_ukv, v_sb_out_norm_g, v_mla_out_norm_g, v_w_out, v_norm_post_g, v_w_ple, v_ple_norm_g, v_w_ple_gate, v_b_ple_gate):
    mats = (w_in, w_uq, w_ukv, w_out, w_ple, w_ple_gate)
    m_mats = (m_w_in, m_w_uq, m_w_ukv, m_w_out, m_w_ple, m_w_ple_gate)
    v_mats = (v_w_in, v_w_uq, v_w_ukv, v_w_out, v_w_ple, v_w_ple_gate)
    vecs = (norm_pre_g, q_norm_g, kv_norm_g, sb_out_norm_g, mla_out_norm_g, norm_post_g, ple_norm_g, b_ple_gate)
    m_vecs = (m_norm_pre_g, m_q_norm_g, m_kv_norm_g, m_sb_out_norm_g, m_mla_out_norm_g, m_norm_post_g,
              m_ple_norm_g, m_b_ple_gate)
    v_vecs = (v_norm_pre_g, v_q_norm_g, v_kv_norm_g, v_sb_out_norm_g, v_mla_out_norm_g, v_norm_post_g,
              v_ple_norm_g, v_b_ple_gate)

    shards = [a[0].astype(BF16) for a in mats]
    w_in_p = _kernel_w_in(_all_gather(shards[:1])[0])
    grad_x, reduced, vec_slab = _step(x[0], p[0, 0], positions[0], loss_target[0], *vecs, w_in_p, shards[1:])
    upd = [_adamw_matrix(own, l2, w, m, v, "adamw_%d" % o)
           for o, ((own, l2), w, m, v) in enumerate(zip(reduced, mats, m_mats, v_mats))]
    sm = _adamw_vectors(_slab_exchange(vec_slab), vecs, m_vecs, v_vecs)

    outs = []
    for kind in range(4):
        mat = [upd[o][kind] for o in range(len(mats))]
        vec = sm[1 + 8 * kind:9 + 8 * kind]
        outs += [vec[0], mat[0], vec[1], mat[1], vec[2], mat[2], vec[3], vec[4], mat[3], vec[5],
                 mat[4], vec[6], mat[5], vec[7]]
    return (sm[0][0, 0], grad_x[None], *outs)


def _step(xs, ps, pos, tgt, norm_pre_g, q_norm_g, kv_norm_g, sb_out_norm_g, mla_out_norm_g,
          norm_post_g, ple_norm_g, b_ple_gate, w_in_p, shards):
    s = xs.shape[0]
    place = jnp.stack([lax.axis_index("c"), 2 * lax.axis_index("x") + lax.axis_index("y")]).astype(jnp.int32)

    half = ROPE_DIM // 2
    freq = ROPE_THETA ** (-jnp.arange(half, dtype=F32) / half)
    ang = pos.astype(F32)[:, None] * freq
    cos, sin = jnp.cos(ang), jnp.sin(ang)
    cos_t = jnp.concatenate([jnp.ones((s, 64), F32), cos, cos, jnp.zeros((s, 32), F32)], axis=1)
    sin_t = jnp.concatenate([jnp.zeros((s, 64), F32), -sin, sin, jnp.zeros((s, 32), F32)], axis=1)
    seg = jnp.arange(D_GRP) // HEAD_DIM
    bd = (seg[:, None] == seg[None, :]).astype(BF16)

    qkv, rest, h_b, *gath = _in_proj(xs, norm_pre_g, w_in_p, shards)
    w_uq_p, w_uk_p, w_uv, f_out, f_ple, f_pg = _kernel_weights(gath)
    sb_o = _sb_fwd(qkv, 8)
    qp, kp, vv, cqn_b, ckvn_b = _mla_prep(rest, q_norm_g, kv_norm_g, w_uq_p, w_uk_p, w_uv, cos_t, sin_t)
    mla_o, lse = _mla_fwd(qp, kp, vv, 4)

    (dx1, d_sbo, d_mlo, d_sbg, d_mlg, x1_b, dgl_b, yc_b, dy_b, p_b, du_b, small_mid) = _mid(
        xs, ps, tgt, sb_o, mla_o, rest, sb_out_norm_g, mla_out_norm_g, f_out, norm_post_g,
        f_ple, ple_norm_g, f_pg, b_ple_gate, bd)
    d_out, d_pg = _tn_matmul_ring([(yc_b, dy_b), (x1_b, dgl_b)], "dw_out_pg")
    pay_a = [d_out.reshape(N_DEV, 128, D_MODEL), _tn_matmul(p_b, du_b, "dw_ple", blocked=True),
             d_pg.reshape(N_DEV, 128, D_MODEL)]
    dqp, dkp, dvv, *sib_a = _mla_bwd(qp, kp, vv, d_mlo, mla_o, lse, 4, pay_a)
    pair_a = _pair_sums(pay_a, sib_a, place, "grad_pair_sums_a")
    dq_sb, dk_sb, dv_sb, *landed_a = _sb_bwd(qkv, d_sbo, [sm for sm, _ in pair_a])
    dcq, dckv, dkr, dq_b, dk_b, dv_b, small_prep = _mla_prep_bwd(
        dqp, dkp, dvv, rest, q_norm_g, kv_norm_g, w_uq_p, w_uk_p, w_uv, cos_t, sin_t)
    pieces = [dq_sb, dk_sb, dv_sb, d_sbg, d_mlg, dcq, dckv, dkr]
    d_cols = [_tn_matmul_multi(h_b, pieces[0:2], "dw_in_0"), _tn_matmul_multi(h_b, pieces[2:4], "dw_in_1"),
              _tn_matmul_multi(h_b, pieces[4:8], "dw_in_2")]
    pay_b = [_payload_in(d_cols), _tn_matmul(cqn_b, dq_b, "dw_uq", blocked=True),
             _payload_ukv(_tn_matmul(ckvn_b, dk_b, "dw_uk", blocked=True), _tn_matmul(ckvn_b, dv_b, "dw_uv"))]
    pair_b = _pair_sums(pay_b, _pair_exchange(pay_b, "grad_pair_exchange"), place, "grad_pair_sums_b")
    grad_x, small_in, *landed_b = _in_bwd(xs, norm_pre_g, dx1, pieces, w_in_p, [sm for sm, _ in pair_b])
    reduced = [(own, l2) for (_, own), l2 in zip(pair_b + pair_a, landed_b + landed_a)]
    slab = jnp.concatenate([small_in[0:1], jnp.pad(small_prep[0:2], ((0, 0), (0, D_MODEL - Q_LORA))),
                            small_mid[3:8]], axis=0)
    return grad_x, reduced, slab
```

```python
import jax
import jax.numpy as jnp
from jax import lax
from jax.experimental import pallas as pl
from jax.experimental.pallas import tpu as pltpu

F32 = jnp.float32
BF16 = jnp.bfloat16
MESH = pl.DeviceIdType.MESH

N_DEV = 8
D_MODEL = 1024
N_HEADS = 8
HEAD_DIM = 64
D_GRP = N_HEADS * HEAD_DIM
Q_LORA = 256
KV_LORA = 128
ROPE_DIM = 32
PLE_DIM = 256
CHUNK_SHIFT = 6
ROPE_THETA = 10000.0
EPS = 1e-6
SB_SCALE = HEAD_DIM ** -0.5
MLA_SCALE = (HEAD_DIM + ROPE_DIM) ** -0.5
NEG = -1e30
LOG2_E = 1.4426950408889634
LN_2 = 0.6931471805599453
SB_CUTOFF = 110.0

ADAM_LR = 0.001
ADAM_B1 = 0.9
ADAM_B2 = 0.999
ADAM_EPS = 1e-08
ADAM_WD = 0.01
ADAM_STEP = 10

LANES = 128
TQ = 256
TK = 256
TM = 256
TM_IO = 512
TS_DW = 2048
TS_RING = 1024
RING_SLOTS = 3

D_IN_P = 3072

_NT = (((1,), (1,)), ((), ()))
_TN = (((0,), (0,)), ((), ()))


def _params(sem, vmem_mb):
    return pltpu.CompilerParams(dimension_semantics=sem, vmem_limit_bytes=vmem_mb << 20)


def _hbm(*arrays):
    return [pltpu.with_memory_space_constraint(a, pltpu.HBM) for a in arrays]


def _dot(a, b):
    return jnp.dot(a, b, preferred_element_type=F32)


def _dot_nt(a, b):
    return lax.dot_general(a, b, _NT, preferred_element_type=F32)


def _dot_tn(a, b):
    return lax.dot_general(a, b, _TN, preferred_element_type=F32)


def _hl_dot(a, b):
    hi = a.astype(BF16)
    lo = (a - hi.astype(F32)).astype(BF16)
    return _dot(hi, b) + _dot(lo, b)


def _sigmoid(x):
    return 1.0 / (1.0 + jnp.exp(-x))


def _rope_swap(x, lane):
    left = pltpu.roll(x, LANES - 16, axis=1)
    right = pltpu.roll(x, 16, axis=1)
    lo = (lane >= 64) & (lane < 80)
    hi = (lane >= 80) & (lane < 96)
    return jnp.where(lo, left, jnp.where(hi, right, 0.0))


def _two_level_gather(x_refs, out_refs, send_sems, recv_sems, local_sems):
    x, y, c = lax.axis_index("x"), lax.axis_index("y"), lax.axis_index("c")
    me, sibling = (x, y, c), (x, y, 1 - c)
    chips = [(1 - x, y), (x, 1 - y), (1 - x, 1 - y)]
    ops = range(len(x_refs))

    def slot(o, px, py, pc):
        return out_refs[o].at[4 * px + 2 * py + pc]

    def copy(o, k, block, to, src=None):
        return pltpu.make_async_remote_copy(
            src_ref=slot(o, *block) if src is None else src, dst_ref=slot(o, *block),
            send_sem=send_sems.at[o, k], recv_sem=recv_sems.at[o, k],
            device_id=to, device_id_type=MESH)

    def mine():
        return [pltpu.make_async_copy(x_refs[o], slot(o, *me), local_sems.at[o]) for o in ops]

    def first():
        return ([copy(o, 0, me, sibling, src=x_refs[o]) for o in ops]
                + [copy(o, 1 + j, me, (*chip, c), src=x_refs[o]) for j, chip in enumerate(chips) for o in ops])

    def start():
        for cp in mine() + first():
            cp.start()

    def finish():
        passed = []
        for j, chip in enumerate(chips):
            for o in ops:
                copy(o, 1 + j, (*chip, c), me).wait_recv()
                passed.append(copy(o, 4 + j, (*chip, c), sibling))
                passed[-1].start()
        for o in ops:
            copy(o, 0, sibling, me).wait_recv()
        for j, chip in enumerate(chips):
            for o in ops:
                copy(o, 4 + j, (*chip, 1 - c), me).wait_recv()
        for cp in first() + passed:
            cp.wait_send()
        for cp in mine():
            cp.wait()

    return start, finish


def _gather_sems(n_op):
    return [pltpu.SemaphoreType.DMA((n_op, 7)), pltpu.SemaphoreType.DMA((n_op, 7)),
            pltpu.SemaphoreType.DMA((n_op,))]


def _all_gather(shards):
    n_op = len(shards)

    def body(*refs):
        start, finish = _two_level_gather(refs[:n_op], refs[n_op:2 * n_op], *refs[2 * n_op:])
        start()
        finish()

    any_spec = pl.BlockSpec(memory_space=pl.ANY)
    return pl.pallas_call(
        body, name="weight_all_gather",
        out_shape=[jax.ShapeDtypeStruct((N_DEV,) + a.shape, a.dtype) for a in shards],
        in_specs=[any_spec] * n_op, out_specs=[any_spec] * n_op, scratch_shapes=_gather_sems(n_op),
        compiler_params=pltpu.CompilerParams(vmem_limit_bytes=4 << 20),
    )(*shards)


def _pair_copies(g_refs, l_refs, ssem, rsem):
    x, y, c = lax.axis_index("x"), lax.axis_index("y"), lax.axis_index("c")
    copies = []
    for o in range(len(g_refs)):
        for chip in range(4):
            copies.append(pltpu.make_async_remote_copy(
                src_ref=g_refs[o].at[2 * chip + (1 - c)], dst_ref=l_refs[o].at[chip],
                send_sem=ssem.at[o, chip], recv_sem=rsem.at[o, chip],
                device_id=(x, y, 1 - c), device_id_type=MESH))
    return copies


def _pair_specs(pays):
    n_op = len(pays)
    any_spec = pl.BlockSpec(memory_space=pl.ANY)
    return ([any_spec] * n_op, [any_spec] * n_op,
            [jax.ShapeDtypeStruct((4,) + a.shape[1:], F32) for a in pays],
            [pltpu.SemaphoreType.DMA((n_op, 4)), pltpu.SemaphoreType.DMA((n_op, 4))])


def _pair_exchange(pays, name):
    n_op = len(pays)
    in_specs, out_specs, out_shape, sems = _pair_specs(pays)

    def body(*refs):
        copies = _pair_copies(refs[:n_op], refs[n_op:2 * n_op], *refs[2 * n_op:])
        for cp in copies:
            cp.start()
        for cp in copies:
            cp.wait()

    return pl.pallas_call(body, name=name, out_shape=out_shape, in_specs=in_specs, out_specs=out_specs,
                          scratch_shapes=sems,
                          compiler_params=pltpu.CompilerParams(vmem_limit_bytes=4 << 20))(*pays)


def _slab_exchange(small):
    sr, n = small.shape

    def body(s_ref, sland_ref, ssem, rsem, lsem):
        x, y, c = lax.axis_index("x"), lax.axis_index("y"), lax.axis_index("c")
        me = 4 * x + 2 * y + c
        copies = []
        for k in range(1, N_DEV):
            peer = (1 - x if (k >> 2) & 1 else x, 1 - y if (k >> 1) & 1 else y, 1 - c if k & 1 else c)
            copies.append(pltpu.make_async_remote_copy(
                src_ref=s_ref, dst_ref=sland_ref.at[me], send_sem=ssem.at[k], recv_sem=rsem.at[k],
                device_id=peer, device_id_type=MESH))
        own = pltpu.make_async_copy(s_ref, sland_ref.at[me], lsem)
        own.start()
        for cp in copies:
            cp.start()
        for cp in copies:
            cp.wait()
        own.wait()

    any_spec = pl.BlockSpec(memory_space=pl.ANY)
    return pl.pallas_call(
        body, name="grad_slab_exchange", out_shape=jax.ShapeDtypeStruct((N_DEV, sr, n), F32),
        in_specs=[any_spec], out_specs=any_spec,
        scratch_shapes=[pltpu.SemaphoreType.DMA((N_DEV,)), pltpu.SemaphoreType.DMA((N_DEV,)),
                        pltpu.SemaphoreType.DMA],
        compiler_params=pltpu.CompilerParams(vmem_limit_bytes=4 << 20),
    )(small)


def _pair_sums(pays, landed, place, name):
    n = len(pays)
    dims = [p.shape[1:] for p in pays]

    def body(place_ref, *refs):
        g_refs, l_refs, s_refs, own_refs = refs[:n], refs[n:2 * n], refs[2 * n:3 * n], refs[3 * n:]
        i = pl.program_id(0)
        for o in range(n):
            tot = g_refs[o][...] + l_refs[o][...]
            s_refs[o][...] = tot.astype(BF16)

            @pl.when(i == place_ref[1])
            def _(o=o, tot=tot):
                own_refs[o][...] = tot

    grid_spec = pltpu.PrefetchScalarGridSpec(
        num_scalar_prefetch=1, grid=(4,),
        in_specs=[pl.BlockSpec((None, r, c), lambda i, pr: (2 * i + pr[0], 0, 0)) for r, c in dims]
        + [pl.BlockSpec((None, r, c), lambda i, pr: (i, 0, 0)) for r, c in dims],
        out_specs=[pl.BlockSpec((None, r, c), lambda i, pr: (i, 0, 0)) for r, c in dims]
        + [pl.BlockSpec((r, c), lambda i, pr: (0, 0)) for r, c in dims])
    out = pl.pallas_call(
        body, name=name, grid_spec=grid_spec,
        out_shape=[jax.ShapeDtypeStruct((4, r, c), BF16) for r, c in dims]
        + [jax.ShapeDtypeStruct((r, c), F32) for r, c in dims],
        compiler_params=_params(("arbitrary",), 16),
    )(place, *pays, *landed)
    return list(zip(out[:n], out[n:]))


def _chip_copies(s_refs, l_refs, ssem, rsem):
    x, y, c = lax.axis_index("x"), lax.axis_index("y"), lax.axis_index("c")
    copies = []
    for rel in range(1, 4):
        px = 1 - x if rel & 2 else x
        py = 1 - y if rel & 1 else y
        for o in range(len(s_refs)):
            copies.append(pltpu.make_async_remote_copy(
                src_ref=s_refs[o].at[2 * px + py], dst_ref=l_refs[o].at[rel - 1],
                send_sem=ssem.at[o, rel - 1], recv_sem=rsem.at[o, rel - 1],
                device_id=(px, py, c), device_id_type=MESH))
    return copies


def _chip_specs(sums):
    n_op = len(sums)
    any_spec = pl.BlockSpec(memory_space=pl.ANY)
    return ([any_spec] * n_op, [any_spec] * n_op,
            [jax.ShapeDtypeStruct((3,) + a.shape[1:], BF16) for a in sums],
            [pltpu.SemaphoreType.DMA((n_op, 3)), pltpu.SemaphoreType.DMA((n_op, 3))])


def _adamw_math(g, w, m, v):
    mn = ADAM_B1 * m + (1.0 - ADAM_B1) * g
    vn = ADAM_B2 * v + (1.0 - ADAM_B2) * (g * g)
    m_hat = mn / (1.0 - ADAM_B1 ** ADAM_STEP)
    v_hat = vn / (1.0 - ADAM_B2 ** ADAM_STEP)
    return -ADAM_LR * (m_hat / (jnp.sqrt(v_hat) + ADAM_EPS) + ADAM_WD * w), mn, vn


def _adamw_matrix(own, landed, w, m, v, name):
    _, r, c = w.shape
    cp = own.shape[1]
    br = min(r, 256)

    def body(own_ref, l_ref, w_ref, m_ref, v_ref, g_out, d_out, m_out, v_out):
        g = own_ref[...]
        for k in range(3):
            g = g + l_ref[k].astype(F32)
        g = g[:, :c]
        g_out[...] = g
        d_out[...], m_out[...], v_out[...] = _adamw_math(g, w_ref[...], m_ref[...], v_ref[...])

    row = pl.BlockSpec((None, br, c), lambda i: (0, i, 0))
    shp = jax.ShapeDtypeStruct((1, r, c), F32)
    return pl.pallas_call(
        body, name=name, grid=(r // br,),
        in_specs=[pl.BlockSpec((br, cp), lambda i: (i, 0)), pl.BlockSpec((3, br, cp), lambda i: (0, i, 0)),
                  row, row, row],
        out_specs=(row, row, row, row), out_shape=(shp, shp, shp, shp),
        compiler_params=_params(("parallel",), 12),
    )(own, landed, w, m, v)


_VEC_PLACE = ((0, 0), (1, 0), (2, 0), (3, 0), (3, D_GRP), (4, 0), (5, 0), (6, 0))


def _adamw_vectors(sland, ws, ms, vs):
    nv = len(ws)

    def body(l_ref, *refs):
        w_refs, m_refs, v_refs = refs[:nv], refs[nv:2 * nv], refs[2 * nv:3 * nv]
        loss_ref = refs[3 * nv]
        outs = refs[3 * nv + 1:]
        g_all = l_ref[0]
        for j in range(1, N_DEV):
            g_all = g_all + l_ref[j]
        loss_ref[...] = jnp.sum(g_all[7:8, :], axis=1, keepdims=True)
        for k, (row, lane0) in enumerate(_VEC_PLACE):
            n = w_refs[k].shape[1]
            g = g_all[row:row + 1, lane0:lane0 + n]
            d, mn, vn = _adamw_math(g, w_refs[k][...], m_refs[k][...], v_refs[k][...])
            outs[k][...] = g
            outs[nv + k][...] = d
            outs[2 * nv + k][...] = mn
            outs[3 * nv + k][...] = vn

    def whole(shape):
        return pl.BlockSpec(shape, lambda i: (0,) * len(shape))

    shapes = [jax.ShapeDtypeStruct(w.shape, F32) for w in ws]
    return pl.pallas_call(
        body, name="adamw_vectors", grid=(1,),
        in_specs=[whole(sland.shape)] + [whole(w.shape) for w in ws] * 3,
        out_specs=[whole((1, 1))] + [whole(w.shape) for w in ws] * 4,
        out_shape=[jax.ShapeDtypeStruct((1, 1), F32)] + shapes * 4,
        compiler_params=_params(("arbitrary",), 4),
    )(sland, *ws, *ms, *vs)


def _in_proj(x, g, w, shards):
    s = x.shape[0]
    n_op = len(shards)
    steps = s // TM_IO

    def body(x_ref, g_ref, w_ref, *refs):
        shard_refs = refs[:n_op]
        qkv_ref, rest_ref, h_ref = refs[n_op:n_op + 3]
        gath_refs = refs[n_op + 3:2 * n_op + 3]
        start, finish = _two_level_gather(shard_refs, gath_refs, *refs[2 * n_op + 3:])
        i = pl.program_id(0)

        @pl.when(i == 0)
        def _():
            start()

        xv = x_ref[...]
        r = lax.rsqrt(jnp.mean(xv * xv, axis=-1, keepdims=True) + EPS)
        h = ((xv * r) * g_ref[...]).astype(BF16)
        h_ref[...] = h
        qkv_ref[...] = _dot(h, w_ref[:, :1536]).astype(BF16)
        rest_ref[...] = _dot(h, w_ref[:, 1536:])

        @pl.when(i == steps - 1)
        def _():
            finish()

    any_spec = pl.BlockSpec(memory_space=pl.ANY)
    return pl.pallas_call(
        body, name="in_proj", grid=(steps,),
        in_specs=[pl.BlockSpec((TM_IO, D_MODEL), lambda i: (i, 0)),
                  pl.BlockSpec((1, D_MODEL), lambda i: (0, 0)),
                  pl.BlockSpec((D_MODEL, D_IN_P), lambda i: (0, 0))] + [any_spec] * n_op,
        out_specs=[pl.BlockSpec((TM_IO, 1536), lambda i: (i, 0)),
                   pl.BlockSpec((TM_IO, 1536), lambda i: (i, 0)),
                   pl.BlockSpec((TM_IO, D_MODEL), lambda i: (i, 0))] + [any_spec] * n_op,
        out_shape=[pltpu.HBM((s, 1536), BF16), pltpu.HBM((s, 1536), F32),
                   pltpu.HBM((s, D_MODEL), BF16)]
        + [jax.ShapeDtypeStruct((N_DEV,) + a.shape, a.dtype) for a in shards],
        scratch_shapes=_gather_sems(n_op),
        compiler_params=_params(("arbitrary",), 32),
    )(x, g, w, *shards)


def _mla_prep(rest, gq, gkv, wuq, wuk, wuv, cos_t, sin_t):
    s = rest.shape[0]

    def body(cq_ref, ckv_ref, kr_ref, gq_ref, gkv_ref, wuq_ref, wuk_ref, wuv_ref, c_ref, s_ref,
             qp_ref, kp_ref, vv_ref, cqn_ref, ckvn_ref):
        lane = lax.broadcasted_iota(jnp.int32, (1, LANES), 1)
        cos_v, sin_v = c_ref[...], s_ref[...]
        cq = cq_ref[...]
        rq = lax.rsqrt(jnp.mean(cq * cq, axis=-1, keepdims=True) + EPS)
        cqn = ((cq * rq) * gq_ref[...]).astype(BF16)
        cqn_ref[...] = cqn
        q = _dot(cqn, wuq_ref[...])
        ckv = ckv_ref[...]
        rkv = lax.rsqrt(jnp.mean(ckv * ckv, axis=-1, keepdims=True) + EPS)
        ckvn = ((ckv * rkv) * gkv_ref[...]).astype(BF16)
        ckvn_ref[...] = ckvn
        kn = _dot(ckvn, wuk_ref[...])
        vv_ref[...] = _dot(ckvn, wuv_ref[...]).astype(BF16)
        kr = kr_ref[...]
        kr_roped = kr * cos_v + _rope_swap(kr, lane) * sin_v
        for h in range(N_HEADS):
            sl = slice(h * LANES, (h + 1) * LANES)
            qh = q[:, sl]
            qp_ref[:, sl] = (qh * cos_v + _rope_swap(qh, lane) * sin_v).astype(BF16)
            kp_ref[:, sl] = (kn[:, sl] + kr_roped).astype(BF16)

    def row(width, idx):
        return pl.BlockSpec((TM_IO, width), lambda i: (i, idx))

    def full(a):
        return pl.BlockSpec(a.shape, lambda i: (0, 0))

    return pl.pallas_call(
        body, name="mla_prep", grid=(s // TM_IO,),
        in_specs=[row(Q_LORA, 4), row(KV_LORA, 10), row(LANES, 11), full(gq), full(gkv),
                  full(wuq), full(wuk), full(wuv), row(LANES, 0), row(LANES, 0)],
        out_specs=(row(1024, 0), row(1024, 0), row(D_GRP, 0), row(Q_LORA, 0), row(KV_LORA, 0)),
        out_shape=(pltpu.HBM((s, 1024), BF16), pltpu.HBM((s, 1024), BF16),
                   pltpu.HBM((s, D_GRP), BF16), pltpu.HBM((s, Q_LORA), BF16),
                   pltpu.HBM((s, KV_LORA), BF16)),
        compiler_params=_params(("parallel",), 13),
    )(*_hbm(rest, rest, rest), gq, gkv, wuq, wuk, wuv, cos_t, sin_t)


def _sb_live(n, qi, carries):
    top = carries[0]
    for c in carries[1:]:
        top = jnp.maximum(top, c)
    return jnp.logical_and(n < qi, jnp.max(top) > -SB_CUTOFF)


def _sb_fwd(qkv, hb):
    s = qkv.shape[0]

    def body(q_ref, k_ref, v_ref, o_ref, acc):
        qi = pl.program_id(1)
        lane = lax.broadcasted_iota(jnp.int32, (1, LANES), 1)
        is_a = lane < HEAD_DIM
        pair = lambda h: slice((h // 2) * LANES, (h // 2 + 1) * LANES)
        q_h = []
        for h in range(hb):
            qs = q_ref[:, pair(h)] * SB_SCALE
            mine = is_a if h % 2 == 0 else jnp.logical_not(is_a)
            q_h.append(jnp.where(mine, qs, jnp.zeros_like(qs)))
        r_i = lax.broadcasted_iota(jnp.int32, (TQ, TK), 0)
        c_i = lax.broadcasted_iota(jnp.int32, (TQ, TK), 1)
        past = c_i < r_i
        upper = (r_i > c_i).astype(BF16)
        acc[...] = jnp.zeros_like(acc)

        def tile(j, carries, diag):
            ks = pl.ds(pl.multiple_of(j * TK, TK), TK)
            zs = [_dot_nt(q_h[h], k_ref[ks, pair(h)]) for h in range(hb)]
            if diag:
                zs = [jnp.where(past, z, NEG) for z in zs]
            lfs = [-(jnp.maximum(z, 0.0) + jnp.log(1.0 + jnp.exp(-jnp.abs(z)))) for z in zs]
            sufs = [_hl_dot(lfs[h], upper) for h in range(hb)]
            out = []
            for h in range(hb):
                w = jnp.exp(zs[h] + lfs[h] + (sufs[h] + carries[h]))
                acc[h] += _dot(w.astype(BF16), v_ref[ks, pair(h)])
                out.append(carries[h] + jnp.sum(lfs[h], axis=1, keepdims=True))
            return tuple(out)

        zero = jnp.zeros((TQ, 1), F32)
        carries = tile(qi, (zero,) * hb, True)

        def step(st):
            return (st[0] + 1,) + tile(qi - 1 - st[0], st[1:], False)

        lax.while_loop(lambda st: _sb_live(st[0], qi, st[1:]), step, (0,) + carries)
        for pr in range(hb // 2):
            o_ref[:, pr * LANES:(pr + 1) * LANES] = jnp.where(is_a, acc[2 * pr], acc[2 * pr + 1])

    width = hb * HEAD_DIM
    nb = D_GRP // width
    slab = lambda part: pl.BlockSpec((s, width), lambda g, qi: (0, part * nb + g))
    blk = pl.BlockSpec((TQ, width), lambda g, qi: (qi, g))
    return pl.pallas_call(
        body, name="sb_fwd", grid=(nb, s // TQ),
        in_specs=[blk, slab(1), slab(2)], out_specs=blk,
        out_shape=pltpu.HBM((s, D_GRP), F32),
        scratch_shapes=[pltpu.VMEM((hb, TQ, LANES), F32)],
        compiler_params=_params(("arbitrary", "arbitrary"), 28),
    )(*_hbm(qkv, qkv, qkv))


def _sb_bwd(qkv, d_o, sums):
    s = qkv.shape[0]
    nq = s // TQ
    nk = s // TK
    n_op = len(sums)
    ride_in, ride_out, ride_shape, ride_sems = _chip_specs(sums)

    def body(q_ref, k_ref, v_ref, do_ref, *refs):
        s_refs = refs[:n_op]
        dq_ref, dk_ref, dv_ref = refs[n_op:n_op + 3]
        l_refs = refs[n_op + 3:2 * n_op + 3]
        x1s, bts, dqacc, dkacc, dvacc, ssem, rsem = refs[2 * n_op + 3:]
        qi = pl.program_id(1)
        first_step = jnp.logical_and(pl.program_id(0) == 0, qi == 0)
        last_step = jnp.logical_and(pl.program_id(0) == pl.num_programs(0) - 1, qi == nq - 1)

        @pl.when(first_step)
        def _():
            for cp in _chip_copies(s_refs, l_refs, ssem, rsem):
                cp.start()

        lane = lax.broadcasted_iota(jnp.int32, (1, LANES), 1)
        is_a = lane < HEAD_DIM

        @pl.when(qi == 0)
        def _():
            dkacc[...] = jnp.zeros_like(dkacc)
            dvacc[...] = jnp.zeros_like(dvacc)

        qs = q_ref[...] * SB_SCALE
        zq = jnp.zeros_like(qs)
        qs_x = (jnp.where(is_a, qs, zq), jnp.where(is_a, zq, qs))
        dob = do_ref[...].astype(BF16)
        do_x = (jnp.where(is_a, dob, zq), jnp.where(is_a, zq, dob))
        r_i = lax.broadcasted_iota(jnp.int32, (TQ, TK), 0)
        c_i = lax.broadcasted_iota(jnp.int32, (TQ, TK), 1)
        past = c_i < r_i
        upper = (r_i > c_i).astype(BF16)
        upper_incl = (r_i >= c_i).astype(BF16)
        dqacc[...] = jnp.zeros_like(dqacc)
        both = ((0, 0), (0, 1), (1, 0), (1, 1))

        def tiles(n):
            j_hi = qi - 2 * n
            lo_ok = j_hi >= 1
            j_lo = jnp.maximum(j_hi - 1, 0)
            ks = (pl.ds(pl.multiple_of(j_hi * TK, TK), TK), pl.ds(pl.multiple_of(j_lo * TK, TK), TK))
            return j_hi, lo_ok, j_lo, ks

        def sweep(n, carries):
            j_hi, lo_ok, j_lo, ks = tiles(n)
            slot = (j_hi, jnp.where(lo_ok, j_lo, nk))
            valid = (jnp.logical_or(past, j_hi < qi), lo_ok)
            z = {th: jnp.where(valid[th[0]], _dot_nt(qs_x[th[1]], k_ref[ks[th[0]], :]), NEG) for th in both}
            log_b, lf_sum, suf = {}, {}, {}
            for th in both:
                lf = -(jnp.maximum(z[th], 0.0) + jnp.log(1.0 + jnp.exp(-jnp.abs(z[th]))))
                log_b[th] = z[th] + lf
                lf_sum[th] = jnp.sum(lf, axis=1, keepdims=True)
                suf[th] = _hl_dot(lf, upper)
            c, g_in = {}, {}
            for h in range(2):
                c[0, h], g_in[0, h] = carries[2 * h], carries[2 * h + 1]
                c[1, h] = c[0, h] + lf_sum[0, h]
            d_a = {th: _dot_nt(do_x[th[1]], v_ref[ks[th[0]], :]) for th in both}
            a_b, g, g_sum, sg = {}, {}, {}, {}
            for th in both:
                a = jnp.exp(log_b[th] + (suf[th] + c[th]))
                a_b[th] = a.astype(BF16)
                g[th] = a * d_a[th]
                g_sum[th] = jnp.sum(g[th], axis=1, keepdims=True)
                sg[th] = _hl_dot(g[th], upper_incl)
            for h in range(2):
                g_in[1, h] = g_in[0, h] + g_sum[0, h]
            for th in both:
                t, h = th
                beta = jnp.exp(log_b[th])
                x1s[slot[t], h] = g[th] * (1.0 - beta) + beta * (sg[th] + g_in[th])
                bts[slot[t], h] = beta
                dvacc[ks[t], :] += _dot_tn(a_b[th], do_x[h])
            out = []
            for h in range(2):
                out.append(c[1, h] + lf_sum[1, h])
                out.append(g_in[1, h] + g_sum[1, h])
            return tuple(out)

        zero = jnp.zeros((TQ, 1), F32)
        first = sweep(0, (zero, zero, zero, zero))

        def more(st):
            return jnp.logical_and(2 * st[0] <= qi, jnp.max(jnp.maximum(st[1], st[3])) > -SB_CUTOFF)

        swept = lax.while_loop(more, lambda st: (st[0] + 1,) + sweep(st[0], st[1:]), (1,) + first)
        g_tot = (swept[2], swept[4])

        def apply(n, carry):
            j_hi, lo_ok, j_lo, ks = tiles(n)

            def one(j, kslice):
                for h in range(2):
                    dz = (x1s[j, h] - bts[j, h] * g_tot[h]).astype(BF16)
                    dqacc[h] += _dot(dz, k_ref[kslice, :])
                    dkacc[kslice, :] += _dot_tn(dz, qs_x[h])

            one(j_hi, ks[0])

            @pl.when(lo_ok)
            def _():
                one(j_lo, ks[1])

            return carry

        lax.fori_loop(0, swept[0], apply, 0)
        dq_ref[...] = (jnp.where(is_a, dqacc[0], dqacc[1]) * SB_SCALE).astype(BF16)

        @pl.when(qi == nq - 1)
        def _():
            dk_ref[...] = dkacc[...].astype(BF16)
            dv_ref[...] = dvacc[...].astype(BF16)

        @pl.when(last_step)
        def _():
            for cp in _chip_copies(s_refs, l_refs, ssem, rsem):
                cp.wait()

    slab = lambda off: pl.BlockSpec((s, LANES), lambda p, qi: (0, off + p))
    blk = pl.BlockSpec((TQ, LANES), lambda p, qi: (qi, p))
    out_slab = pl.BlockSpec((s, LANES), lambda p, qi: (0, p))
    shp = pltpu.HBM((s, D_GRP), BF16)
    return pl.pallas_call(
        body, name="sb_bwd", grid=(4, nq),
        in_specs=[blk, slab(4), slab(8), blk] + ride_in,
        out_specs=[blk, out_slab, out_slab] + ride_out, out_shape=[shp, shp, shp] + ride_shape,
        scratch_shapes=[pltpu.VMEM((nk + 1, 2, TQ, TK), F32)] * 2
        + [pltpu.VMEM((2, TQ, LANES), F32), pltpu.VMEM((s, LANES), F32), pltpu.VMEM((s, LANES), F32)]
        + ride_sems,
        compiler_params=_params(("arbitrary", "arbitrary"), 44),
    )(*_hbm(qkv, qkv, qkv, d_o), *sums)


def _mla_fwd(qp, kp, vv, hb):
    s = qp.shape[0]
    c2 = MLA_SCALE * LOG2_E

    def body(q_ref, k_ref, v_ref, o_ref, lse_ref, vaug, mrun, mb, acc, zbuf):
        qi = pl.program_id(1)
        lane = lax.broadcasted_iota(jnp.int32, (1, LANES), 1)
        is_a = lane < HEAD_DIM

        @pl.when(qi == 0)
        def _():
            for h in range(hb):
                vp = v_ref[:, (h // 2) * LANES:(h // 2 + 1) * LANES]
                mine = is_a if h % 2 == 0 else jnp.logical_not(is_a)
                vaug[h] = jnp.where(mine, vp, jnp.ones_like(vp))

        r_i = lax.broadcasted_iota(jnp.int32, (TQ, TK), 0)
        c_i = lax.broadcasted_iota(jnp.int32, (TQ, TK), 1)
        visible = (c_i >> CHUNK_SHIFT) <= (r_i >> CHUNK_SHIFT)

        def key_rows(j):
            return pl.ds(pl.multiple_of(j * TK, TK), TK)

        def sweep(tiles):
            def loop(n, carry):
                tiles(((2 * n, False), (2 * n + 1, False)))
                return carry

            lax.fori_loop(0, qi // 2, loop, 0)

            @pl.when(qi % 2 == 1)
            def _():
                tiles(((qi - 1, False), (qi, True)))

            @pl.when(qi % 2 == 0)
            def _():
                tiles(((qi, True),))

        mrun[...] = jnp.full_like(mrun, NEG)

        def tiles_max(js):
            zs = [[_dot_nt(q_ref[:, h * LANES:(h + 1) * LANES], k_ref[key_rows(j), h * LANES:(h + 1) * LANES])
                   for h in range(hb)] for j, _ in js]
            for t, (j, diag) in enumerate(js):
                for h in range(hb):
                    z = jnp.where(visible, zs[t][h], NEG) if diag else zs[t][h]
                    zbuf[j, h] = z
                    mrun[h] = jnp.maximum(mrun[h], z)

        sweep(tiles_max)
        for h in range(hb):
            m = jnp.max(mrun[h], axis=1, keepdims=True) * c2
            mb[h] = jnp.broadcast_to(m, (TQ, TK))
        acc[...] = jnp.zeros_like(acc)

        def tiles_pv(js):
            ps = [[jnp.exp2((zbuf[j, h] * c2 - mb[h]).astype(BF16)) for h in range(hb)] for j, _ in js]
            for t, (j, _) in enumerate(js):
                for h in range(hb):
                    acc[h] += _dot(ps[t][h], vaug[h, key_rows(j), :])

        sweep(tiles_pv)
        for pr in range(hb // 2):
            a, b = 2 * pr, 2 * pr + 1
            psl = slice(pr * LANES, (pr + 1) * LANES)
            acc_a, acc_b = acc[a], acc[b]
            l_a = pltpu.roll(acc_a, HEAD_DIM, axis=1)
            l_b = pltpu.roll(acc_b, HEAD_DIM, axis=1)
            o_ref[:, psl] = jnp.where(is_a, acc_a * (1.0 / l_a), acc_b * (1.0 / l_b))
            lse_ref[:, psl] = jnp.where(is_a, mb[a, :, :LANES] * LN_2 + jnp.log(l_a),
                                        mb[b, :, :LANES] * LN_2 + jnp.log(l_b))

    blk = pl.BlockSpec((TQ, hb * HEAD_DIM), lambda g, qi: (qi, g))
    shp = pltpu.HBM((s, D_GRP), F32)
    return pl.pallas_call(
        body, name="mla_fwd", grid=(N_HEADS // hb, s // TQ),
        in_specs=[pl.BlockSpec((TQ, hb * LANES), lambda g, qi: (qi, g)),
                  pl.BlockSpec((s, hb * LANES), lambda g, qi: (0, g)),
                  pl.BlockSpec((s, hb * HEAD_DIM), lambda g, qi: (0, g))],
        out_specs=(blk, blk), out_shape=(shp, shp),
        scratch_shapes=[pltpu.VMEM((hb, s, LANES), BF16), pltpu.VMEM((hb, TQ, TK), F32),
                        pltpu.VMEM((hb, TQ, TK), F32), pltpu.VMEM((hb, TQ, LANES), F32),
                        pltpu.VMEM((s // TK, hb, TQ, TK), F32)],
        compiler_params=_params(("arbitrary", "arbitrary"), 44),
    )(*_hbm(qp, kp, vv))


def _mla_bwd(qp, kp, vv, d_o, o, lse, hb, pays):
    s = qp.shape[0]
    nq = s // TQ
    c2 = MLA_SCALE * LOG2_E
    n_op = len(pays)
    ride_in, ride_out, ride_shape, ride_sems = _pair_specs(pays)

    def body(q_ref, k_ref, v_ref, do_ref, o_ref, lse_ref, *refs):
        g_refs = refs[:n_op]
        dq_ref, dk_ref, dv_ref = refs[n_op:n_op + 3]
        l_refs = refs[n_op + 3:2 * n_op + 3]
        dqacc, lse_b, delta_b, q_t, do_t, ssem, rsem = refs[2 * n_op + 3:]
        qi = pl.program_id(1)

        @pl.when(jnp.logical_and(pl.program_id(0) == 0, qi == 0))
        def _():
            for cp in _pair_copies(g_refs, l_refs, ssem, rsem):
                cp.start()

        lane = lax.broadcasted_iota(jnp.int32, (1, LANES), 1)
        is_a = lane < HEAD_DIM

        @pl.when(qi == 0)
        def _():
            dk_ref[...] = jnp.zeros_like(dk_ref)
            dv_ref[...] = jnp.zeros_like(dv_ref)

        r_i = lax.broadcasted_iota(jnp.int32, (TQ, TK), 0)
        c_i = lax.broadcasted_iota(jnp.int32, (TQ, TK), 1)
        visible = (c_i >> CHUNK_SHIFT) <= (r_i >> CHUNK_SHIFT)
        do_x = []
        for h in range(hb):
            psl = slice((h // 2) * LANES, (h // 2 + 1) * LANES)
            mine = is_a if h % 2 == 0 else jnp.logical_not(is_a)
            d_o = do_ref[:, psl]
            delta = jnp.sum(jnp.where(mine, d_o * o_ref[:, psl], 0.0), axis=1, keepdims=True)
            lse_h = jnp.sum(jnp.where(lane == (h % 2) * HEAD_DIM, lse_ref[:, psl], 0.0), axis=1, keepdims=True)
            lse_b[h] = jnp.broadcast_to(lse_h * LOG2_E, (TQ, TK))
            delta_b[h] = jnp.broadcast_to(delta, (TQ, TK))
            do_h = jnp.where(mine, d_o, 0.0)
            do_x.append(do_h.astype(BF16))
            do_t[h] = do_h.T.astype(BF16)
            q_t[h] = q_ref[:, h * LANES:(h + 1) * LANES].astype(F32).T.astype(BF16)
        dqacc[...] = jnp.zeros_like(dqacc)

        head = lambda h: slice(h * LANES, (h + 1) * LANES)
        pair = lambda h: slice((h // 2) * LANES, (h // 2 + 1) * LANES)

        def tiles(js):
            th = [(j, diag, pl.ds(pl.multiple_of(j * TK, TK), TK), h) for j, diag in js for h in range(hb)]
            zs = [_dot_nt(q_ref[:, head(h)], k_ref[ks, head(h)]) for _, _, ks, h in th]
            dps = [_dot_nt(do_x[h], v_ref[ks, pair(h)]) for _, _, ks, h in th]
            for i, (j, diag, ks, h) in enumerate(th):
                e = zs[i] * c2 - lse_b[h]
                if diag:
                    e = jnp.where(visible, e, NEG)
                p = jnp.exp2(e)
                ds = (p * (dps[i] - delta_b[h]) * MLA_SCALE).astype(BF16)
                dqacc[h] += _dot(ds, k_ref[ks, head(h)])
                dk_ref[head(h), ks] += _dot(q_t[h], ds)
                dv_ref[pair(h), ks] += _dot(do_t[h], p.astype(BF16))

        def loop(n, c):
            tiles(((2 * n, False), (2 * n + 1, False)))
            return c

        lax.fori_loop(0, qi // 2, loop, 0)

        @pl.when(qi % 2 == 1)
        def _():
            tiles(((qi - 1, False), (qi, True)))

        @pl.when(qi % 2 == 0)
        def _():
            tiles(((qi, True),))

        for h in range(hb):
            dq_ref[:, h * LANES:(h + 1) * LANES] = dqacc[h]

        @pl.when(jnp.logical_and(pl.program_id(0) == pl.num_programs(0) - 1, qi == nq - 1))
        def _():
            for cp in _pair_copies(g_refs, l_refs, ssem, rsem):
                cp.wait()

    blk = pl.BlockSpec((TQ, hb * HEAD_DIM), lambda g, qi: (qi, g))
    return pl.pallas_call(
        body, name="mla_bwd", grid=(N_HEADS // hb, nq),
        in_specs=[pl.BlockSpec((TQ, hb * LANES), lambda g, qi: (qi, g)),
                  pl.BlockSpec((s, hb * LANES), lambda g, qi: (0, g)),
                  pl.BlockSpec((s, hb * HEAD_DIM), lambda g, qi: (0, g)), blk, blk, blk] + ride_in,
        out_specs=[pl.BlockSpec((TQ, hb * LANES), lambda g, qi: (qi, g)),
                   pl.BlockSpec((hb * LANES, s), lambda g, qi: (g, 0)),
                   pl.BlockSpec((hb * HEAD_DIM, s), lambda g, qi: (g, 0))] + ride_out,
        out_shape=[pltpu.HBM((s, 1024), F32), pltpu.HBM((1024, s), F32),
                   pltpu.HBM((D_GRP, s), F32)] + ride_shape,
        scratch_shapes=[pltpu.VMEM((hb, TQ, LANES), F32), pltpu.VMEM((hb, TQ, TK), F32),
                        pltpu.VMEM((hb, TQ, TK), F32), pltpu.VMEM((hb, LANES, TQ), BF16),
                        pltpu.VMEM((hb, LANES, TQ), BF16)] + ride_sems,
        compiler_params=_params(("arbitrary", "arbitrary"), 52),
    )(*_hbm(qp, kp, vv, d_o, o, lse), *pays)


def _mid(x, p, target, sb_o, mla_o, rest, g_sb, g_mla, w_out, g_post, w_ple, g_ple, w_pg, b_pg, bd):
    s = x.shape[0]

    def body(x_ref, p_ref, t_ref, sbo_ref, mlo_ref, sbg_ref, mlg_ref, gsb_ref, gml_ref, wout_ref,
             gpost_ref, wple_ref, gple_ref, wpg_ref, bpg_ref, bd_ref,
             dx1_ref, dsbo_ref, dmlo_ref, dsbg_ref, dmlg_ref, x1b_ref, dglb_ref, ycb_ref, dyb_ref,
             pb_ref, dub_ref, small_ref):
        i = pl.program_id(0)
        bd_m = bd_ref[...]

        def seg_mean(v):
            return _dot(v.astype(BF16), bd_m) * (1.0 / HEAD_DIM)

        groups = []
        for o_ref, gate_ref, gain_ref in ((sbo_ref, sbg_ref, gsb_ref), (mlo_ref, mlg_ref, gml_ref)):
            o = o_ref[...]
            r = lax.rsqrt(seg_mean(o * o) + EPS)
            n = o * r
            hn = n * gain_ref[...]
            gate = gate_ref[...]
            sg = _sigmoid(gate)
            si = gate * sg
            groups.append((r, n, hn, gate, sg, si, gain_ref[...]))
        ya = (groups[0][2] * groups[0][5]).astype(BF16)
        yb = (groups[1][2] * groups[1][5]).astype(BF16)
        ycb_ref[:, :D_GRP] = ya
        ycb_ref[:, D_GRP:] = yb
        y = _dot(ya, wout_ref[:D_GRP, :]) + _dot(yb, wout_ref[D_GRP:, :])
        ry = lax.rsqrt(jnp.mean(y * y, axis=-1, keepdims=True) + EPS)
        ny = y * ry
        x1 = x_ref[...] + ny * gpost_ref[...]
        x1b = x1.astype(BF16)
        x1b_ref[...] = x1b
        pb = p_ref[...].astype(BF16)
        pb_ref[...] = pb
        u = _dot(pb, wple_ref[...])
        ru = lax.rsqrt(jnp.mean(u * u, axis=-1, keepdims=True) + EPS)
        nu = u * ru
        ple = nu * gple_ref[...]
        gate = _sigmoid(_dot(x1b, wpg_ref[...]) + bpg_ref[...])
        x2 = x1 + ple * gate
        diff = x2 - t_ref[...]
        dx2 = diff * (1.0 / D_MODEL)

        d_ple = dx2 * gate
        d_glin = (dx2 * ple) * (gate * (1.0 - gate))
        dglb = d_glin.astype(BF16)
        dglb_ref[...] = dglb
        dx1 = dx2 + _dot_nt(dglb, wpg_ref[...])
        dx1_ref[...] = dx1
        d_nu = d_ple * gple_ref[...]
        d_u = ru * (d_nu - nu * jnp.mean(d_nu * nu, axis=-1, keepdims=True))
        dub_ref[...] = d_u.astype(BF16)
        d_ny = dx1 * gpost_ref[...]
        d_y = ry * (d_ny - ny * jnp.mean(d_ny * ny, axis=-1, keepdims=True))
        dyb = d_y.astype(BF16)
        dyb_ref[...] = dyb
        d_yc = (_dot_nt(dyb, wout_ref[:D_GRP, :]), _dot_nt(dyb, wout_ref[D_GRP:, :]))

        d_gain = []
        for gx, (do_ref, dg_ref) in enumerate(((dsbo_ref, dsbg_ref), (dmlo_ref, dmlg_ref))):
            r, n, hn, gate_g, sg, si, gain = groups[gx]
            dyg = d_yc[gx]
            d_hn = dyg * si
            dg_ref[...] = (dyg * hn * (sg * (1.0 + gate_g * (1.0 - sg)))).astype(BF16)
            d_gain.append(jnp.sum(d_hn * n, axis=0, keepdims=True))
            d_n = d_hn * gain
            do_ref[...] = r * (d_n - n * seg_mean(d_n * n))

        @pl.when(i == 0)
        def _():
            small_ref[...] = jnp.zeros_like(small_ref)

        small_ref[3:4, :D_GRP] += d_gain[0]
        small_ref[3:4, D_GRP:] += d_gain[1]
        small_ref[4:5, :] += jnp.sum(dx1 * ny, axis=0, keepdims=True)
        small_ref[5:6, :] += jnp.sum(d_ple * nu, axis=0, keepdims=True)
        small_ref[6:7, :] += jnp.sum(d_glin, axis=0, keepdims=True)
        small_ref[7:8, :] += jnp.sum(diff * diff, axis=0, keepdims=True) * (0.5 / D_MODEL)

    def row(width, idx=0):
        return pl.BlockSpec((TM, width), lambda i: (i, idx))

    def full(a):
        return pl.BlockSpec(a.shape, lambda i: (0, 0))

    f32 = lambda w: pltpu.HBM((s, w), F32)
    b16 = lambda w: pltpu.HBM((s, w), BF16)
    return pl.pallas_call(
        body, name="mid", grid=(s // TM,),
        in_specs=[row(D_MODEL), row(PLE_DIM), row(D_MODEL), row(D_GRP), row(D_GRP),
                  row(D_GRP, 0), row(D_GRP, 1), full(g_sb), full(g_mla), full(w_out), full(g_post),
                  full(w_ple), full(g_ple), full(w_pg), full(b_pg), full(bd)],
        out_specs=(row(D_MODEL), row(D_GRP), row(D_GRP), row(D_GRP), row(D_GRP), row(D_MODEL),
                   row(D_MODEL), row(D_MODEL), row(D_MODEL), row(PLE_DIM), row(D_MODEL),
                   pl.BlockSpec((8, D_MODEL), lambda i: (0, 0))),
        out_shape=(f32(D_MODEL), f32(D_GRP), f32(D_GRP), b16(D_GRP), b16(D_GRP), b16(D_MODEL),
                   b16(D_MODEL), b16(D_MODEL), b16(D_MODEL), b16(PLE_DIM), b16(D_MODEL),
                   jax.ShapeDtypeStruct((8, D_MODEL), F32)),
        compiler_params=_params(("arbitrary",), 46),
    )(*_hbm(x, p, target, sb_o, mla_o, rest, rest), g_sb, g_mla, w_out, g_post, w_ple, g_ple, w_pg, b_pg, bd)


def _mla_prep_bwd(dqp, dkp, dvv, rest, gq, gkv, wuq, wuk, wuv, cos_t, sin_t):
    s = rest.shape[0]

    def body(dqp_ref, dkp_ref, dvv_ref, cq_ref, ckv_ref, gq_ref, gkv_ref, wuq_ref, wuk_ref, wuv_ref,
             c_ref, s_ref, dcq_ref, dckv_ref, dkr_ref, dqb_ref, dkb_ref, dvb_ref, small_ref):
        i = pl.program_id(0)
        lane = lax.broadcasted_iota(jnp.int32, (1, LANES), 1)
        in_rope = (lane >= HEAD_DIM) & (lane < HEAD_DIM + ROPE_DIM)
        cos_v, sin_v = c_ref[...], s_ref[...]
        dkr_roped = jnp.zeros((TM_IO, LANES), F32)
        for h in range(N_HEADS):
            sl = slice(h * LANES, (h + 1) * LANES)
            dy = dqp_ref[:, sl]
            dqb_ref[:, sl] = (dy * cos_v + _rope_swap(dy * sin_v, lane)).astype(BF16)
            dkh = dkp_ref[sl, :].T
            dkb_ref[:, sl] = dkh.astype(BF16)
            dkr_roped = dkr_roped + jnp.where(in_rope, dkh, 0.0)
        dkr_ref[...] = (dkr_roped * cos_v + _rope_swap(dkr_roped * sin_v, lane)).astype(BF16)
        dvb = dvv_ref[...].T.astype(BF16)
        dvb_ref[...] = dvb

        cq = cq_ref[...]
        rq = lax.rsqrt(jnp.mean(cq * cq, axis=-1, keepdims=True) + EPS)
        nq_ = cq * rq
        d_cqn = _dot_nt(dqb_ref[...], wuq_ref[...])
        d_n = d_cqn * gq_ref[...]
        dcq_ref[...] = (rq * (d_n - nq_ * jnp.mean(d_n * nq_, axis=-1, keepdims=True))).astype(BF16)

        ckv = ckv_ref[...]
        rkv = lax.rsqrt(jnp.mean(ckv * ckv, axis=-1, keepdims=True) + EPS)
        nkv = ckv * rkv
        d_ckvn = _dot_nt(dkb_ref[...], wuk_ref[...]) + _dot_nt(dvb, wuv_ref[...])
        d_n2 = d_ckvn * gkv_ref[...]
        dckv_ref[...] = (rkv * (d_n2 - nkv * jnp.mean(d_n2 * nkv, axis=-1, keepdims=True))).astype(BF16)

        @pl.when(i == 0)
        def _():
            small_ref[...] = jnp.zeros_like(small_ref)

        small_ref[0:1, :] += jnp.sum(d_cqn * nq_, axis=0, keepdims=True)
        small_ref[1:2, :KV_LORA] += jnp.sum(d_ckvn * nkv, axis=0, keepdims=True)

    def row(width, idx=0):
        return pl.BlockSpec((TM_IO, width), lambda i: (i, idx))

    def full(a):
        return pl.BlockSpec(a.shape, lambda i: (0, 0))

    b16 = lambda w: pltpu.HBM((s, w), BF16)
    return pl.pallas_call(
        body, name="mla_prep_bwd", grid=(s // TM_IO,),
        in_specs=[row(1024), pl.BlockSpec((1024, TM_IO), lambda i: (0, i)), pl.BlockSpec((D_GRP, TM_IO), lambda i: (0, i)),
                  row(Q_LORA, 4), row(KV_LORA, 10), full(gq), full(gkv),
                  full(wuq), full(wuk), full(wuv), row(LANES), row(LANES)],
        out_specs=(row(Q_LORA), row(KV_LORA), row(LANES), row(1024), row(1024), row(D_GRP),
                   pl.BlockSpec((8, Q_LORA), lambda i: (0, 0))),
        out_shape=(b16(Q_LORA), b16(KV_LORA), b16(LANES), b16(1024), b16(1024), b16(D_GRP),
                   jax.ShapeDtypeStruct((8, Q_LORA), F32)),
        compiler_params=_params(("arbitrary",), 24),
    )(*_hbm(dqp, dkp, dvv, rest, rest), gq, gkv, wuq, wuk, wuv, cos_t, sin_t)


def _in_bwd(x, g, dx1, pieces, w, sums):
    s = x.shape[0]
    steps = s // TM_IO
    widths = [a.shape[1] for a in pieces]
    offs = [sum(widths[:k]) for k in range(len(widths))]
    n_pc, n_op = len(pieces), len(sums)
    ride_in, ride_out, ride_shape, ride_sems = _chip_specs(sums)

    def body(x_ref, g_ref, dx1_ref, *refs):
        piece_refs = refs[:n_pc]
        w_ref = refs[n_pc]
        s_refs = refs[n_pc + 1:n_pc + 1 + n_op]
        dx_ref, small_ref = refs[n_pc + 1 + n_op:n_pc + 3 + n_op]
        l_refs = refs[n_pc + 3 + n_op:n_pc + 3 + 2 * n_op]
        ssem, rsem = refs[n_pc + 3 + 2 * n_op:]
        i = pl.program_id(0)

        @pl.when(i == 0)
        def _():
            for cp in _chip_copies(s_refs, l_refs, ssem, rsem):
                cp.start()

        dh = jnp.zeros((TM_IO, D_MODEL), F32)
        for pr, off, wd in zip(piece_refs, offs, widths):
            dh = dh + _dot_nt(pr[...], w_ref[:, off:off + wd])
        xv = x_ref[...]
        r = lax.rsqrt(jnp.mean(xv * xv, axis=-1, keepdims=True) + EPS)
        n = xv * r
        d_n = dh * g_ref[...]
        dx_ref[...] = dx1_ref[...] + r * (d_n - n * jnp.mean(d_n * n, axis=-1, keepdims=True))

        @pl.when(i == 0)
        def _():
            small_ref[...] = jnp.zeros_like(small_ref)

        small_ref[0:1, :] += jnp.sum(dh * n, axis=0, keepdims=True)

        @pl.when(i == steps - 1)
        def _():
            for cp in _chip_copies(s_refs, l_refs, ssem, rsem):
                cp.wait()

    def row(width):
        return pl.BlockSpec((TM_IO, width), lambda i: (i, 0))

    return pl.pallas_call(
        body, name="in_bwd", grid=(steps,),
        in_specs=[row(D_MODEL), pl.BlockSpec((1, D_MODEL), lambda i: (0, 0)), row(D_MODEL)]
        + [row(wd) for wd in widths] + [pl.BlockSpec(w.shape, lambda i: (0, 0))] + ride_in,
        out_specs=[row(D_MODEL), pl.BlockSpec((8, D_MODEL), lambda i: (0, 0))] + ride_out,
        out_shape=[pltpu.HBM((s, D_MODEL), F32), jax.ShapeDtypeStruct((8, D_MODEL), F32)]
        + ride_shape,
        scratch_shapes=ride_sems,
        compiler_params=_params(("arbitrary",), 40),
    )(*_hbm(x), g, *_hbm(dx1, *pieces), w, *sums)


def _tn_matmul(a, b, name, blocked=False):
    s, k = a.shape
    n = b.shape[1]
    ts = min(s, TS_DW)
    tn = n if blocked else min(n, 512)
    steps = s // ts

    def body(a_ref, b_ref, o_ref):
        t = pl.program_id(1)

        @pl.when(t == 0)
        def _():
            o_ref[...] = jnp.zeros_like(o_ref)

        prod = _dot_tn(a_ref[...], b_ref[...])
        if blocked:
            for j in range(n // LANES):
                o_ref[j] += prod[:, j * LANES:(j + 1) * LANES]
        else:
            o_ref[...] += prod

    if blocked:
        out_spec = pl.BlockSpec((n // LANES, k, LANES), lambda j, t: (0, 0, 0))
        out_shape = jax.ShapeDtypeStruct((n // LANES, k, LANES), F32)
    else:
        out_spec = pl.BlockSpec((k, tn), lambda j, t: (0, j))
        out_shape = jax.ShapeDtypeStruct((k, n), F32)
    return pl.pallas_call(
        body, name=name, grid=(n // tn, steps),
        in_specs=[pl.BlockSpec((ts, k), lambda j, t: (t, 0)), pl.BlockSpec((ts, tn), lambda j, t: (t, j))],
        out_specs=out_spec, out_shape=out_shape,
        compiler_params=_params(("parallel", "arbitrary"), 20),
    )(*_hbm(a, b))


def _tn_matmul_ring(tasks, name):
    s, k = tasks[0][0].shape
    n = sum(b.shape[1] for b in tasks[0][1])
    n_pr = len(tasks)
    n_in = [1 + len(bs) for _, bs in tasks]
    first = [sum(n_in[:p]) for p in range(n_pr)]
    sched = [(p, c) for p in range(n_pr) for c in range(s // TS_RING)]

    def body(*refs):
        o_refs = refs[sum(n_in):sum(n_in) + n_pr]
        abuf, bbuf, sem = refs[sum(n_in) + n_pr:]

        def copies(i):
            p, c = sched[i]
            rows = pl.ds(c * TS_RING, TS_RING)
            slot = i % RING_SLOTS
            out = [pltpu.make_async_copy(refs[first[p]].at[rows, :], abuf.at[slot], sem.at[0, slot])]
            off = 0
            for j, b in enumerate(tasks[p][1]):
                wd = b.shape[1]
                out.append(pltpu.make_async_copy(refs[first[p] + 1 + j].at[rows, :],
                                                 bbuf.at[slot, :, off:off + wd], sem.at[1 + j, slot]))
                off += wd
            return out

        for i in range(min(RING_SLOTS - 1, len(sched))):
            for cp in copies(i):
                cp.start()
        for i, (p, c) in enumerate(sched):
            if i + RING_SLOTS - 1 < len(sched):
                for cp in copies(i + RING_SLOTS - 1):
                    cp.start()
            for cp in copies(i):
                cp.wait()
            prod = _dot_tn(abuf[i % RING_SLOTS], bbuf[i % RING_SLOTS])
            if c == 0:
                o_refs[p][...] = prod
            else:
                o_refs[p][...] += prod

    return pl.pallas_call(
        body, name=name,
        in_specs=[pl.BlockSpec(memory_space=pl.ANY)] * sum(n_in),
        out_specs=[pl.BlockSpec(memory_space=pltpu.VMEM)] * n_pr,
        out_shape=[jax.ShapeDtypeStruct((k, n), F32)] * n_pr,
        scratch_shapes=[pltpu.VMEM((RING_SLOTS, TS_RING, k), BF16), pltpu.VMEM((RING_SLOTS, TS_RING, n), BF16),
                        pltpu.SemaphoreType.DMA((max(n_in), RING_SLOTS))],
        compiler_params=pltpu.CompilerParams(vmem_limit_bytes=(24 + 4 * n_pr) << 20),
    )(*_hbm(*[x for a_op, bs in tasks for x in (a_op, *bs)]))


IN_SHARD = 372
_IN_KERNEL_ORDER = ((0, 2048), (2464, 2976), (2048, 2432))
_IN_ROPE = (2432, 2464)
_IN_GRAD_SRC = ((0, 512, 0, 0), (512, 1024, 0, 512), (1024, 1536, 1, 0), (1536, 2048, 1, 512),
                (2048, 2304, 2, 512), (2304, 2432, 2, 768), (2432, 2464, 2, 960), (2464, 2976, 2, 0))


def _shard_cols(gath_in, lo, hi):
    out = []
    while lo < hi:
        j, a = divmod(lo, IN_SHARD)
        b = min(IN_SHARD, a + hi - lo)
        out.append(gath_in[j][:, a:b])
        lo += b - a
    return out


def _kernel_w_in(g_in):
    zc = lambda n: jnp.zeros((D_MODEL, n), BF16)
    parts = [pc for lo, hi in _IN_KERNEL_ORDER for pc in _shard_cols(g_in, lo, hi)]
    parts += [zc(64)] + _shard_cols(g_in, *_IN_ROPE) + [zc(32)]
    return jnp.concatenate(parts, axis=1)


def _kernel_weights(gath):
    g_uq, g_ukv, g_out, g_ple, g_pg = gath
    w_uq_p = jnp.pad(g_uq, ((0, 0), (0, 0), (0, 32))).transpose(1, 0, 2).reshape(Q_LORA, 1024)
    k_only = jnp.where(jnp.arange(LANES) < HEAD_DIM, g_ukv, jnp.zeros_like(g_ukv))
    w_uk_p = k_only.transpose(1, 0, 2).reshape(KV_LORA, 1024)
    w_uv = g_ukv[:, :, HEAD_DIM:].transpose(1, 0, 2).reshape(KV_LORA, D_GRP)
    w_ple = g_ple.transpose(1, 0, 2).reshape(PLE_DIM, D_MODEL)
    return (w_uq_p, w_uk_p, w_uv, g_out.reshape(D_MODEL, D_MODEL), w_ple, g_pg.reshape(D_MODEL, D_MODEL))


def _payload_in(d_cols):
    blocks = []
    for j in range(N_DEV):
        lo, hi = j * IN_SHARD, (j + 1) * IN_SHARD
        parts = []
        for o_lo, o_hi, idx, off in _IN_GRAD_SRC:
            a, b = max(lo, o_lo), min(hi, o_hi)
            if a < b:
                parts.append(d_cols[idx][:, off + a - o_lo:off + b - o_lo])
        blocks.append(jnp.concatenate(parts, axis=1))
    return jnp.stack(blocks)


def _payload_ukv(duk_blk, d_uv):
    dv_blk = d_uv.reshape(KV_LORA, N_HEADS, HEAD_DIM).transpose(1, 0, 2)
    return jnp.concatenate([duk_blk[:, :, :HEAD_DIM], dv_blk], axis=2)


def kernel(x, p, positions, norm_pre_g, w_in, q_norm_g, w_uq, kv_norm_g, w_ukv, sb_out_norm_g, mla_out_norm_g, w_out, norm_post_g, w_ple, ple_norm_g, w_ple_gate, b_ple_gate, loss_target, m_norm_pre_g, m_w_in, m_q_norm_g, m_w_uq, m_kv_norm_g, m_w_ukv, m_sb_out_norm_g, m_mla_out_norm_g, m_w_out, m_norm_post_g, m_w_ple, m_ple_norm_g, m_w_ple_gate, m_b_ple_gate, v_norm_pre_g, v_w_in, v_q_norm_g, v_w_uq, v_kv_norm_g, v_w_ukv, v_sb_out_norm_g, v_mla_out_norm_g, v_w_out, v_norm_post_g, v_w_ple, v_ple_norm_g, v_w_ple_gate, v_b_ple_gate):
    mats = (w_in, w_uq, w_ukv, w_out, w_ple, w_ple_gate)
    m_mats = (m_w_in, m_w_uq, m_w_ukv, m_w_out, m_w_ple, m_w_ple_gate)
    v_mats = (v_w_in, v_w_uq, v_w_ukv, v_w_out, v_w_ple, v_w_ple_gate)
    vecs = (norm_pre_g, q_norm_g, kv_norm_g, sb_out_norm_g, mla_out_norm_g, norm_post_g, ple_norm_g, b_ple_gate)
    m_vecs = (m_norm_pre_g, m_q_norm_g, m_kv_norm_g, m_sb_out_norm_g, m_mla_out_norm_g, m_norm_post_g,
              m_ple_norm_g, m_b_ple_gate)
    v_vecs = (v_norm_pre_g, v_q_norm_g, v_kv_norm_g, v_sb_out_norm_g, v_mla_out_norm_g, v_norm_post_g,
              v_ple_norm_g, v_b_ple_gate)

    shards = [a[0].astype(BF16) for a in mats]
    w_in_p = _kernel_w_in(_all_gather(shards[:1])[0])
    grad_x, reduced, vec_slab = _step(x[0], p[0, 0], positions[0], loss_target[0], *vecs, w_in_p, shards[1:])
    upd = [_adamw_matrix(own, l2, w, m, v, "adamw_%d" % o)
           for o, ((own, l2), w, m, v) in enumerate(zip(reduced, mats, m_mats, v_mats))]
    sm = _adamw_vectors(_slab_exchange(vec_slab), vecs, m_vecs, v_vecs)

    outs = []
    for kind in range(4):
        mat = [upd[o][kind] for o in range(len(mats))]
        vec = sm[1 + 8 * kind:9 + 8 * kind]
        outs += [vec[0], mat[0], vec[1], mat[1], vec[2], mat[2], vec[3], vec[4], mat[3], vec[5],
                 mat[4], vec[6], mat[5], vec[7]]
    return (sm[0][0, 0], grad_x[None], *outs)


def _step(xs, ps, pos, tgt, norm_pre_g, q_norm_g, kv_norm_g, sb_out_norm_g, mla_out_norm_g,
          norm_post_g, ple_norm_g, b_ple_gate, w_in_p, shards):
    s = xs.shape[0]
    place = jnp.stack([lax.axis_index("c"), 2 * lax.axis_index("x") + lax.axis_index("y")]).astype(jnp.int32)

    half = ROPE_DIM // 2
    freq = ROPE_THETA ** (-jnp.arange(half, dtype=F32) / half)
    ang = pos.astype(F32)[:, None] * freq
    cos, sin = jnp.cos(ang), jnp.sin(ang)
    cos_t = jnp.concatenate([jnp.ones((s, 64), F32), cos, cos, jnp.zeros((s, 32), F32)], axis=1)
    sin_t = jnp.concatenate([jnp.zeros((s, 64), F32), -sin, sin, jnp.zeros((s, 32), F32)], axis=1)
    seg = jnp.arange(D_GRP) // HEAD_DIM
    bd = (seg[:, None] == seg[None, :]).astype(BF16)

    qkv, rest, h_b, *gath = _in_proj(xs, norm_pre_g, w_in_p, shards)
    w_uq_p, w_uk_p, w_uv, f_out, f_ple, f_pg = _kernel_weights(gath)
    sb_o = _sb_fwd(qkv, 8)
    qp, kp, vv, cqn_b, ckvn_b = _mla_prep(rest, q_norm_g, kv_norm_g, w_uq_p, w_uk_p, w_uv, cos_t, sin_t)
    mla_o, lse = _mla_fwd(qp, kp, vv, 4)

    (dx1, d_sbo, d_mlo, d_sbg, d_mlg, x1_b, dgl_b, yc_b, dy_b, p_b, du_b, small_mid) = _mid(
        xs, ps, tgt, sb_o, mla_o, rest, sb_out_norm_g, mla_out_norm_g, f_out, norm_post_g,
        f_ple, ple_norm_g, f_pg, b_ple_gate, bd)
    d_out, d_pg = _tn_matmul_ring([(yc_b, [dy_b]), (x1_b, [dgl_b])], "dw_out_pg")
    pay_a = [d_out.reshape(N_DEV, 128, D_MODEL), _tn_matmul(p_b, du_b, "dw_ple", blocked=True),
             d_pg.reshape(N_DEV, 128, D_MODEL)]
    dqp, dkp, dvv, *sib_a = _mla_bwd(qp, kp, vv, d_mlo, mla_o, lse, 4, pay_a)
    pair_a = _pair_sums(pay_a, sib_a, place, "grad_pair_sums_a")
    dq_sb, dk_sb, dv_sb, *landed_a = _sb_bwd(qkv, d_sbo, [sm for sm, _ in pair_a])
    dcq, dckv, dkr, dq_b, dk_b, dv_b, small_prep = _mla_prep_bwd(
        dqp, dkp, dvv, rest, q_norm_g, kv_norm_g, w_uq_p, w_uk_p, w_uv, cos_t, sin_t)
    pieces = [dq_sb, dk_sb, dv_sb, d_sbg, d_mlg, dcq, dckv, dkr]
    d_cols = _tn_matmul_ring([(h_b, pieces[0:2]), (h_b, pieces[2:4]), (h_b, pieces[4:8])], "dw_in")
    pay_b = [_payload_in(d_cols), _tn_matmul(cqn_b, dq_b, "dw_uq", blocked=True),
             _payload_ukv(_tn_matmul(ckvn_b, dk_b, "dw_uk", blocked=True), _tn_matmul(ckvn_b, dv_b, "dw_uv"))]
    pair_b = _pair_sums(pay_b, _pair_exchange(pay_b, "grad_pair_exchange"), place, "grad_pair_sums_b")
    grad_x, small_in, *landed_b = _in_bwd(xs, norm_pre_g, dx1, pieces, w_in_p, [sm for sm, _ in pair_b])
    reduced = [(own, l2) for (_, own), l2 in zip(pair_b + pair_a, landed_b + landed_a)]
    slab = jnp.concatenate([small_in[0:1], jnp.pad(small_prep[0:2], ((0, 0), (0, D_MODEL - Q_LORA))),
                            small_mid[3:8]], axis=0)
    return grad_x, reduced, slab
```

```python
import jax
import jax.numpy as jnp
from jax import lax
from jax.experimental import pallas as pl
from jax.experimental.pallas import tpu as pltpu

F32 = jnp.float32
BF16 = jnp.bfloat16
MESH = pl.DeviceIdType.MESH

N_DEV = 8
D_MODEL = 1024
N_HEADS = 8
HEAD_DIM = 64
D_GRP = N_HEADS * HEAD_DIM
Q_LORA = 256
KV_LORA = 128
ROPE_DIM = 32
PLE_DIM = 256
CHUNK_SHIFT = 6
ROPE_THETA = 10000.0
EPS = 1e-6
SB_SCALE = HEAD_DIM ** -0.5
MLA_SCALE = (HEAD_DIM + ROPE_DIM) ** -0.5
NEG = -1e30
LOG2_E = 1.4426950408889634
LN_2 = 0.6931471805599453
SB_CUTOFF = 110.0

ADAM_LR = 0.001
ADAM_B1 = 0.9
ADAM_B2 = 0.999
ADAM_EPS = 1e-08
ADAM_WD = 0.01
ADAM_STEP = 10

LANES = 128
TQ = 256
TK = 256
TM = 256
TM_IO = 512
TS_DW = 2048
TS_RING = 2048
RING_SLOTS = 3

D_IN_P = 3072

_NT = (((1,), (1,)), ((), ()))
_TN = (((0,), (0,)), ((), ()))


def _params(sem, vmem_mb):
    return pltpu.CompilerParams(dimension_semantics=sem, vmem_limit_bytes=vmem_mb << 20)


def _hbm(*arrays):
    return [pltpu.with_memory_space_constraint(a, pltpu.HBM) for a in arrays]


def _dot(a, b):
    return jnp.dot(a, b, preferred_element_type=F32)


def _dot_nt(a, b):
    return lax.dot_general(a, b, _NT, preferred_element_type=F32)


def _dot_tn(a, b):
    return lax.dot_general(a, b, _TN, preferred_element_type=F32)


def _hl_dot(a, b):
    hi = a.astype(BF16)
    lo = (a - hi.astype(F32)).astype(BF16)
    return _dot(hi, b) + _dot(lo, b)


def _sigmoid(x):
    return 1.0 / (1.0 + jnp.exp(-x))


def _rope_swap(x, lane):
    left = pltpu.roll(x, LANES - 16, axis=1)
    right = pltpu.roll(x, 16, axis=1)
    lo = (lane >= 64) & (lane < 80)
    hi = (lane >= 80) & (lane < 96)
    return jnp.where(lo, left, jnp.where(hi, right, 0.0))


def _two_level_gather(x_refs, out_refs, send_sems, recv_sems, local_sems):
    x, y, c = lax.axis_index("x"), lax.axis_index("y"), lax.axis_index("c")
    me, sibling = (x, y, c), (x, y, 1 - c)
    chips = [(1 - x, y), (x, 1 - y), (1 - x, 1 - y)]
    ops = range(len(x_refs))

    def slot(o, px, py, pc):
        return out_refs[o].at[4 * px + 2 * py + pc]

    def copy(o, k, block, to, src=None):
        return pltpu.make_async_remote_copy(
            src_ref=slot(o, *block) if src is None else src, dst_ref=slot(o, *block),
            send_sem=send_sems.at[o, k], recv_sem=recv_sems.at[o, k],
            device_id=to, device_id_type=MESH)

    def mine():
        return [pltpu.make_async_copy(x_refs[o], slot(o, *me), local_sems.at[o]) for o in ops]

    def first():
        return ([copy(o, 0, me, sibling, src=x_refs[o]) for o in ops]
                + [copy(o, 1 + j, me, (*chip, c), src=x_refs[o]) for j, chip in enumerate(chips) for o in ops])

    def start():
        for cp in mine() + first():
            cp.start()

    def finish():
        passed = []
        for j, chip in enumerate(chips):
            for o in ops:
                copy(o, 1 + j, (*chip, c), me).wait_recv()
                passed.append(copy(o, 4 + j, (*chip, c), sibling))
                passed[-1].start()
        for o in ops:
            copy(o, 0, sibling, me).wait_recv()
        for j, chip in enumerate(chips):
            for o in ops:
                copy(o, 4 + j, (*chip, 1 - c), me).wait_recv()
        for cp in first() + passed:
            cp.wait_send()
        for cp in mine():
            cp.wait()

    return start, finish


def _gather_sems(n_op):
    return [pltpu.SemaphoreType.DMA((n_op, 7)), pltpu.SemaphoreType.DMA((n_op, 7)),
            pltpu.SemaphoreType.DMA((n_op,))]


def _all_gather(shards):
    n_op = len(shards)

    def body(*refs):
        start, finish = _two_level_gather(refs[:n_op], refs[n_op:2 * n_op], *refs[2 * n_op:])
        start()
        finish()

    any_spec = pl.BlockSpec(memory_space=pl.ANY)
    return pl.pallas_call(
        body, name="weight_all_gather",
        out_shape=[jax.ShapeDtypeStruct((N_DEV,) + a.shape, a.dtype) for a in shards],
        in_specs=[any_spec] * n_op, out_specs=[any_spec] * n_op, scratch_shapes=_gather_sems(n_op),
        compiler_params=pltpu.CompilerParams(vmem_limit_bytes=4 << 20),
    )(*shards)


def _pair_copies(g_refs, l_refs, ssem, rsem):
    x, y, c = lax.axis_index("x"), lax.axis_index("y"), lax.axis_index("c")
    copies = []
    for o in range(len(g_refs)):
        for chip in range(4):
            copies.append(pltpu.make_async_remote_copy(
                src_ref=g_refs[o].at[2 * chip + (1 - c)], dst_ref=l_refs[o].at[chip],
                send_sem=ssem.at[o, chip], recv_sem=rsem.at[o, chip],
                device_id=(x, y, 1 - c), device_id_type=MESH))
    return copies


def _pair_specs(pays):
    n_op = len(pays)
    any_spec = pl.BlockSpec(memory_space=pl.ANY)
    return ([any_spec] * n_op, [any_spec] * n_op,
            [jax.ShapeDtypeStruct((4,) + a.shape[1:], F32) for a in pays],
            [pltpu.SemaphoreType.DMA((n_op, 4)), pltpu.SemaphoreType.DMA((n_op, 4))])


def _pair_exchange(pays, name):
    n_op = len(pays)
    in_specs, out_specs, out_shape, sems = _pair_specs(pays)

    def body(*refs):
        copies = _pair_copies(refs[:n_op], refs[n_op:2 * n_op], *refs[2 * n_op:])
        for cp in copies:
            cp.start()
        for cp in copies:
            cp.wait()

    return pl.pallas_call(body, name=name, out_shape=out_shape, in_specs=in_specs, out_specs=out_specs,
                          scratch_shapes=sems,
                          compiler_params=pltpu.CompilerParams(vmem_limit_bytes=4 << 20))(*pays)


def _slab_exchange(small):
    sr, n = small.shape

    def body(s_ref, sland_ref, ssem, rsem, lsem):
        x, y, c = lax.axis_index("x"), lax.axis_index("y"), lax.axis_index("c")
        me = 4 * x + 2 * y + c
        copies = []
        for k in range(1, N_DEV):
            peer = (1 - x if (k >> 2) & 1 else x, 1 - y if (k >> 1) & 1 else y, 1 - c if k & 1 else c)
            copies.append(pltpu.make_async_remote_copy(
                src_ref=s_ref, dst_ref=sland_ref.at[me], send_sem=ssem.at[k], recv_sem=rsem.at[k],
                device_id=peer, device_id_type=MESH))
        own = pltpu.make_async_copy(s_ref, sland_ref.at[me], lsem)
        own.start()
        for cp in copies:
            cp.start()
        for cp in copies:
            cp.wait()
        own.wait()

    any_spec = pl.BlockSpec(memory_space=pl.ANY)
    return pl.pallas_call(
        body, name="grad_slab_exchange", out_shape=jax.ShapeDtypeStruct((N_DEV, sr, n), F32),
        in_specs=[any_spec], out_specs=any_spec,
        scratch_shapes=[pltpu.SemaphoreType.DMA((N_DEV,)), pltpu.SemaphoreType.DMA((N_DEV,)),
                        pltpu.SemaphoreType.DMA],
        compiler_params=pltpu.CompilerParams(vmem_limit_bytes=4 << 20),
    )(small)


def _pair_sums(pays, landed, place, name):
    n = len(pays)
    dims = [p.shape[1:] for p in pays]

    def body(place_ref, *refs):
        g_refs, l_refs, s_refs, own_refs = refs[:n], refs[n:2 * n], refs[2 * n:3 * n], refs[3 * n:]
        i = pl.program_id(0)
        for o in range(n):
            tot = g_refs[o][...] + l_refs[o][...]
            s_refs[o][...] = tot.astype(BF16)

            @pl.when(i == place_ref[1])
            def _(o=o, tot=tot):
                own_refs[o][...] = tot

    grid_spec = pltpu.PrefetchScalarGridSpec(
        num_scalar_prefetch=1, grid=(4,),
        in_specs=[pl.BlockSpec((None, r, c), lambda i, pr: (2 * i + pr[0], 0, 0)) for r, c in dims]
        + [pl.BlockSpec((None, r, c), lambda i, pr: (i, 0, 0)) for r, c in dims],
        out_specs=[pl.BlockSpec((None, r, c), lambda i, pr: (i, 0, 0)) for r, c in dims]
        + [pl.BlockSpec((r, c), lambda i, pr: (0, 0)) for r, c in dims])
    out = pl.pallas_call(
        body, name=name, grid_spec=grid_spec,
        out_shape=[jax.ShapeDtypeStruct((4, r, c), BF16) for r, c in dims]
        + [jax.ShapeDtypeStruct((r, c), F32) for r, c in dims],
        compiler_params=_params(("arbitrary",), 16),
    )(place, *pays, *landed)
    return list(zip(out[:n], out[n:]))


def _chip_copies(s_refs, l_refs, ssem, rsem):
    x, y, c = lax.axis_index("x"), lax.axis_index("y"), lax.axis_index("c")
    copies = []
    for rel in range(1, 4):
        px = 1 - x if rel & 2 else x
        py = 1 - y if rel & 1 else y
        for o in range(len(s_refs)):
            copies.append(pltpu.make_async_remote_copy(
                src_ref=s_refs[o].at[2 * px + py], dst_ref=l_refs[o].at[rel - 1],
                send_sem=ssem.at[o, rel - 1], recv_sem=rsem.at[o, rel - 1],
                device_id=(px, py, c), device_id_type=MESH))
    return copies


def _chip_specs(sums):
    n_op = len(sums)
    any_spec = pl.BlockSpec(memory_space=pl.ANY)
    return ([any_spec] * n_op, [any_spec] * n_op,
            [jax.ShapeDtypeStruct((3,) + a.shape[1:], BF16) for a in sums],
            [pltpu.SemaphoreType.DMA((n_op, 3)), pltpu.SemaphoreType.DMA((n_op, 3))])


def _adamw_math(g, w, m, v):
    mn = ADAM_B1 * m + (1.0 - ADAM_B1) * g
    vn = ADAM_B2 * v + (1.0 - ADAM_B2) * (g * g)
    m_hat = mn / (1.0 - ADAM_B1 ** ADAM_STEP)
    v_hat = vn / (1.0 - ADAM_B2 ** ADAM_STEP)
    return -ADAM_LR * (m_hat / (jnp.sqrt(v_hat) + ADAM_EPS) + ADAM_WD * w), mn, vn


def _adamw_matrix(own, landed, w, m, v, name):
    _, r, c = w.shape
    cp = own.shape[1]
    br = min(r, 256)

    def body(own_ref, l_ref, w_ref, m_ref, v_ref, g_out, d_out, m_out, v_out):
        g = own_ref[...]
        for k in range(3):
            g = g + l_ref[k].astype(F32)
        g = g[:, :c]
        g_out[...] = g
        d_out[...], m_out[...], v_out[...] = _adamw_math(g, w_ref[...], m_ref[...], v_ref[...])

    row = pl.BlockSpec((None, br, c), lambda i: (0, i, 0))
    shp = jax.ShapeDtypeStruct((1, r, c), F32)
    return pl.pallas_call(
        body, name=name, grid=(r // br,),
        in_specs=[pl.BlockSpec((br, cp), lambda i: (i, 0)), pl.BlockSpec((3, br, cp), lambda i: (0, i, 0)),
                  row, row, row],
        out_specs=(row, row, row, row), out_shape=(shp, shp, shp, shp),
        compiler_params=_params(("parallel",), 12),
    )(own, landed, w, m, v)


_VEC_PLACE = ((0, 0), (1, 0), (2, 0), (3, 0), (3, D_GRP), (4, 0), (5, 0), (6, 0))


def _adamw_vectors(sland, ws, ms, vs):
    nv = len(ws)

    def body(l_ref, *refs):
        w_refs, m_refs, v_refs = refs[:nv], refs[nv:2 * nv], refs[2 * nv:3 * nv]
        loss_ref = refs[3 * nv]
        outs = refs[3 * nv + 1:]
        g_all = l_ref[0]
        for j in range(1, N_DEV):
            g_all = g_all + l_ref[j]
        loss_ref[...] = jnp.sum(g_all[7:8, :], axis=1, keepdims=True)
        for k, (row, lane0) in enumerate(_VEC_PLACE):
            n = w_refs[k].shape[1]
            g = g_all[row:row + 1, lane0:lane0 + n]
            d, mn, vn = _adamw_math(g, w_refs[k][...], m_refs[k][...], v_refs[k][...])
            outs[k][...] = g
            outs[nv + k][...] = d
            outs[2 * nv + k][...] = mn
            outs[3 * nv + k][...] = vn

    def whole(shape):
        return pl.BlockSpec(shape, lambda i: (0,) * len(shape))

    shapes = [jax.ShapeDtypeStruct(w.shape, F32) for w in ws]
    return pl.pallas_call(
        body, name="adamw_vectors", grid=(1,),
        in_specs=[whole(sland.shape)] + [whole(w.shape) for w in ws] * 3,
        out_specs=[whole((1, 1))] + [whole(w.shape) for w in ws] * 4,
        out_shape=[jax.ShapeDtypeStruct((1, 1), F32)] + shapes * 4,
        compiler_params=_params(("arbitrary",), 4),
    )(sland, *ws, *ms, *vs)


def _in_proj(x, g, w, shards):
    s = x.shape[0]
    n_op = len(shards)
    steps = s // TM_IO

    def body(x_ref, g_ref, w_ref, *refs):
        shard_refs = refs[:n_op]
        qkv_ref, rest_ref, h_ref = refs[n_op:n_op + 3]
        gath_refs = refs[n_op + 3:2 * n_op + 3]
        start, finish = _two_level_gather(shard_refs, gath_refs, *refs[2 * n_op + 3:])
        i = pl.program_id(0)

        @pl.when(i == 0)
        def _():
            start()

        xv = x_ref[...]
        r = lax.rsqrt(jnp.mean(xv * xv, axis=-1, keepdims=True) + EPS)
        h = ((xv * r) * g_ref[...]).astype(BF16)
        h_ref[...] = h
        qkv_ref[...] = _dot(h, w_ref[:, :1536]).astype(BF16)
        rest_ref[...] = _dot(h, w_ref[:, 1536:])

        @pl.when(i == steps - 1)
        def _():
            finish()

    any_spec = pl.BlockSpec(memory_space=pl.ANY)
    return pl.pallas_call(
        body, name="in_proj", grid=(steps,),
        in_specs=[pl.BlockSpec((TM_IO, D_MODEL), lambda i: (i, 0)),
                  pl.BlockSpec((1, D_MODEL), lambda i: (0, 0)),
                  pl.BlockSpec((D_MODEL, D_IN_P), lambda i: (0, 0))] + [any_spec] * n_op,
        out_specs=[pl.BlockSpec((TM_IO, 1536), lambda i: (i, 0)),
                   pl.BlockSpec((TM_IO, 1536), lambda i: (i, 0)),
                   pl.BlockSpec((TM_IO, D_MODEL), lambda i: (i, 0))] + [any_spec] * n_op,
        out_shape=[pltpu.HBM((s, 1536), BF16), pltpu.HBM((s, 1536), F32),
                   pltpu.HBM((s, D_MODEL), BF16)]
        + [jax.ShapeDtypeStruct((N_DEV,) + a.shape, a.dtype) for a in shards],
        scratch_shapes=_gather_sems(n_op),
        compiler_params=_params(("arbitrary",), 32),
    )(x, g, w, *shards)


def _mla_prep(rest, gq, gkv, wuq, wuk, wuv, cos_t, sin_t):
    s = rest.shape[0]

    def body(cq_ref, ckv_ref, kr_ref, gq_ref, gkv_ref, wuq_ref, wuk_ref, wuv_ref, c_ref, s_ref,
             qp_ref, kp_ref, vv_ref, cqn_ref, ckvn_ref):
        lane = lax.broadcasted_iota(jnp.int32, (1, LANES), 1)
        cos_v, sin_v = c_ref[...], s_ref[...]
        cq = cq_ref[...]
        rq = lax.rsqrt(jnp.mean(cq * cq, axis=-1, keepdims=True) + EPS)
        cqn = ((cq * rq) * gq_ref[...]).astype(BF16)
        cqn_ref[...] = cqn
        q = _dot(cqn, wuq_ref[...])
        ckv = ckv_ref[...]
        rkv = lax.rsqrt(jnp.mean(ckv * ckv, axis=-1, keepdims=True) + EPS)
        ckvn = ((ckv * rkv) * gkv_ref[...]).astype(BF16)
        ckvn_ref[...] = ckvn
        kn = _dot(ckvn, wuk_ref[...])
        vv_ref[...] = _dot(ckvn, wuv_ref[...]).astype(BF16)
        kr = kr_ref[...]
        kr_roped = kr * cos_v + _rope_swap(kr, lane) * sin_v
        for h in range(N_HEADS):
            sl = slice(h * LANES, (h + 1) * LANES)
            qh = q[:, sl]
            qp_ref[:, sl] = (qh * cos_v + _rope_swap(qh, lane) * sin_v).astype(BF16)
            kp_ref[:, sl] = (kn[:, sl] + kr_roped).astype(BF16)

    def row(width, idx):
        return pl.BlockSpec((TM_IO, width), lambda i: (i, idx))

    def full(a):
        return pl.BlockSpec(a.shape, lambda i: (0, 0))

    return pl.pallas_call(
        body, name="mla_prep", grid=(s // TM_IO,),
        in_specs=[row(Q_LORA, 4), row(KV_LORA, 10), row(LANES, 11), full(gq), full(gkv),
                  full(wuq), full(wuk), full(wuv), row(LANES, 0), row(LANES, 0)],
        out_specs=(row(1024, 0), row(1024, 0), row(D_GRP, 0), row(Q_LORA, 0), row(KV_LORA, 0)),
        out_shape=(pltpu.HBM((s, 1024), BF16), pltpu.HBM((s, 1024), BF16),
                   pltpu.HBM((s, D_GRP), BF16), pltpu.HBM((s, Q_LORA), BF16),
                   pltpu.HBM((s, KV_LORA), BF16)),
        compiler_params=_params(("parallel",), 13),
    )(*_hbm(rest, rest, rest), gq, gkv, wuq, wuk, wuv, cos_t, sin_t)


def _sb_live(n, qi, carries):
    top = carries[0]
    for c in carries[1:]:
        top = jnp.maximum(top, c)
    return jnp.logical_and(n < qi, jnp.max(top) > -SB_CUTOFF)


def _sb_fwd(qkv, hb):
    s = qkv.shape[0]

    def body(q_ref, k_ref, v_ref, o_ref, acc):
        qi = pl.program_id(1)
        lane = lax.broadcasted_iota(jnp.int32, (1, LANES), 1)
        is_a = lane < HEAD_DIM
        pair = lambda h: slice((h // 2) * LANES, (h // 2 + 1) * LANES)
        q_h = []
        for h in range(hb):
            qs = q_ref[:, pair(h)] * SB_SCALE
            mine = is_a if h % 2 == 0 else jnp.logical_not(is_a)
            q_h.append(jnp.where(mine, qs, jnp.zeros_like(qs)))
        r_i = lax.broadcasted_iota(jnp.int32, (TQ, TK), 0)
        c_i = lax.broadcasted_iota(jnp.int32, (TQ, TK), 1)
        past = c_i < r_i
        upper = (r_i > c_i).astype(BF16)
        acc[...] = jnp.zeros_like(acc)

        def tile(j, carries, diag):
            ks = pl.ds(pl.multiple_of(j * TK, TK), TK)
            zs = [_dot_nt(q_h[h], k_ref[ks, pair(h)]) for h in range(hb)]
            if diag:
                zs = [jnp.where(past, z, NEG) for z in zs]
            lfs = [-(jnp.maximum(z, 0.0) + jnp.log(1.0 + jnp.exp(-jnp.abs(z)))) for z in zs]
            sufs = [_hl_dot(lfs[h], upper) for h in range(hb)]
            out = []
            for h in range(hb):
                w = jnp.exp(zs[h] + lfs[h] + (sufs[h] + carries[h]))
                acc[h] += _dot(w.astype(BF16), v_ref[ks, pair(h)])
                out.append(carries[h] + jnp.sum(lfs[h], axis=1, keepdims=True))
            return tuple(out)

        zero = jnp.zeros((TQ, 1), F32)
        carries = tile(qi, (zero,) * hb, True)

        def step(st):
            return (st[0] + 1,) + tile(qi - 1 - st[0], st[1:], False)

        lax.while_loop(lambda st: _sb_live(st[0], qi, st[1:]), step, (0,) + carries)
        for pr in range(hb // 2):
            o_ref[:, pr * LANES:(pr + 1) * LANES] = jnp.where(is_a, acc[2 * pr], acc[2 * pr + 1])

    width = hb * HEAD_DIM
    nb = D_GRP // width
    slab = lambda part: pl.BlockSpec((s, width), lambda g, qi: (0, part * nb + g))
    blk = pl.BlockSpec((TQ, width), lambda g, qi: (qi, g))
    return pl.pallas_call(
        body, name="sb_fwd", grid=(nb, s // TQ),
        in_specs=[blk, slab(1), slab(2)], out_specs=blk,
        out_shape=pltpu.HBM((s, D_GRP), F32),
        scratch_shapes=[pltpu.VMEM((hb, TQ, LANES), F32)],
        compiler_params=_params(("arbitrary", "arbitrary"), 28),
    )(*_hbm(qkv, qkv, qkv))


def _sb_bwd(qkv, d_o, sums):
    s = qkv.shape[0]
    nq = s // TQ
    nk = s // TK
    n_op = len(sums)
    ride_in, ride_out, ride_shape, ride_sems = _chip_specs(sums)

    def body(q_ref, k_ref, v_ref, do_ref, *refs):
        s_refs = refs[:n_op]
        dq_ref, dk_ref, dv_ref = refs[n_op:n_op + 3]
        l_refs = refs[n_op + 3:2 * n_op + 3]
        x1s, bts, dqacc, dkacc, dvacc, ssem, rsem = refs[2 * n_op + 3:]
        qi = pl.program_id(1)
        first_step = jnp.logical_and(pl.program_id(0) == 0, qi == 0)
        last_step = jnp.logical_and(pl.program_id(0) == pl.num_programs(0) - 1, qi == nq - 1)

        @pl.when(first_step)
        def _():
            for cp in _chip_copies(s_refs, l_refs, ssem, rsem):
                cp.start()

        lane = lax.broadcasted_iota(jnp.int32, (1, LANES), 1)
        is_a = lane < HEAD_DIM

        @pl.when(qi == 0)
        def _():
            dkacc[...] = jnp.zeros_like(dkacc)
            dvacc[...] = jnp.zeros_like(dvacc)

        qs = q_ref[...] * SB_SCALE
        zq = jnp.zeros_like(qs)
        qs_x = (jnp.where(is_a, qs, zq), jnp.where(is_a, zq, qs))
        dob = do_ref[...].astype(BF16)
        do_x = (jnp.where(is_a, dob, zq), jnp.where(is_a, zq, dob))
        r_i = lax.broadcasted_iota(jnp.int32, (TQ, TK), 0)
        c_i = lax.broadcasted_iota(jnp.int32, (TQ, TK), 1)
        past = c_i < r_i
        upper = (r_i > c_i).astype(BF16)
        upper_incl = (r_i >= c_i).astype(BF16)
        dqacc[...] = jnp.zeros_like(dqacc)
        both = ((0, 0), (0, 1), (1, 0), (1, 1))

        def tiles(n):
            j_hi = qi - 2 * n
            lo_ok = j_hi >= 1
            j_lo = jnp.maximum(j_hi - 1, 0)
            ks = (pl.ds(pl.multiple_of(j_hi * TK, TK), TK), pl.ds(pl.multiple_of(j_lo * TK, TK), TK))
            return j_hi, lo_ok, j_lo, ks

        def sweep(n, carries):
            j_hi, lo_ok, j_lo, ks = tiles(n)
            slot = (j_hi, jnp.where(lo_ok, j_lo, nk))
            valid = (jnp.logical_or(past, j_hi < qi), lo_ok)
            z = {th: jnp.where(valid[th[0]], _dot_nt(qs_x[th[1]], k_ref[ks[th[0]], :]), NEG) for th in both}
            log_b, lf_sum, suf = {}, {}, {}
            for th in both:
                lf = -(jnp.maximum(z[th], 0.0) + jnp.log(1.0 + jnp.exp(-jnp.abs(z[th]))))
                log_b[th] = z[th] + lf
                lf_sum[th] = jnp.sum(lf, axis=1, keepdims=True)
                suf[th] = _hl_dot(lf, upper)
            c, g_in = {}, {}
            for h in range(2):
                c[0, h], g_in[0, h] = carries[2 * h], carries[2 * h + 1]
                c[1, h] = c[0, h] + lf_sum[0, h]
            d_a = {th: _dot_nt(do_x[th[1]], v_ref[ks[th[0]], :]) for th in both}
            a_b, g, g_sum, sg = {}, {}, {}, {}
            for th in both:
                a = jnp.exp(log_b[th] + (suf[th] + c[th]))
                a_b[th] = a.astype(BF16)
                g[th] = a * d_a[th]
                g_sum[th] = jnp.sum(g[th], axis=1, keepdims=True)
                sg[th] = _hl_dot(g[th], upper_incl)
            for h in range(2):
                g_in[1, h] = g_in[0, h] + g_sum[0, h]
            for th in both:
                t, h = th
                beta = jnp.exp(log_b[th])
                x1s[slot[t], h] = g[th] * (1.0 - beta) + beta * (sg[th] + g_in[th])
                bts[slot[t], h] = beta
                dvacc[ks[t], :] += _dot_tn(a_b[th], do_x[h])
            out = []
            for h in range(2):
                out.append(c[1, h] + lf_sum[1, h])
                out.append(g_in[1, h] + g_sum[1, h])
            return tuple(out)

        zero = jnp.zeros((TQ, 1), F32)
        first = sweep(0, (zero, zero, zero, zero))

        def more(st):
            return jnp.logical_and(2 * st[0] <= qi, jnp.max(jnp.maximum(st[1], st[3])) > -SB_CUTOFF)

        swept = lax.while_loop(more, lambda st: (st[0] + 1,) + sweep(st[0], st[1:]), (1,) + first)
        g_tot = (swept[2], swept[4])

        def apply(n, carry):
            j_hi, lo_ok, j_lo, ks = tiles(n)

            def one(j, kslice):
                for h in range(2):
                    dz = (x1s[j, h] - bts[j, h] * g_tot[h]).astype(BF16)
                    dqacc[h] += _dot(dz, k_ref[kslice, :])
                    dkacc[kslice, :] += _dot_tn(dz, qs_x[h])

            one(j_hi, ks[0])

            @pl.when(lo_ok)
            def _():
                one(j_lo, ks[1])

            return carry

        lax.fori_loop(0, swept[0], apply, 0)
        dq_ref[...] = (jnp.where(is_a, dqacc[0], dqacc[1]) * SB_SCALE).astype(BF16)

        @pl.when(qi == nq - 1)
        def _():
            dk_ref[...] = dkacc[...].astype(BF16)
            dv_ref[...] = dvacc[...].astype(BF16)

        @pl.when(last_step)
        def _():
            for cp in _chip_copies(s_refs, l_refs, ssem, rsem):
                cp.wait()

    slab = lambda off: pl.BlockSpec((s, LANES), lambda p, qi: (0, off + p))
    blk = pl.BlockSpec((TQ, LANES), lambda p, qi: (qi, p))
    out_slab = pl.BlockSpec((s, LANES), lambda p, qi: (0, p))
    shp = pltpu.HBM((s, D_GRP), BF16)
    return pl.pallas_call(
        body, name="sb_bwd", grid=(4, nq),
        in_specs=[blk, slab(4), slab(8), blk] + ride_in,
        out_specs=[blk, out_slab, out_slab] + ride_out, out_shape=[shp, shp, shp] + ride_shape,
        scratch_shapes=[pltpu.VMEM((nk + 1, 2, TQ, TK), F32)] * 2
        + [pltpu.VMEM((2, TQ, LANES), F32), pltpu.VMEM((s, LANES), F32), pltpu.VMEM((s, LANES), F32)]
        + ride_sems,
        compiler_params=_params(("arbitrary", "arbitrary"), 44),
    )(*_hbm(qkv, qkv, qkv, d_o), *sums)


def _mla_fwd(qp, kp, vv, hb):
    s = qp.shape[0]
    c2 = MLA_SCALE * LOG2_E

    def body(q_ref, k_ref, v_ref, o_ref, lse_ref, vaug, mrun, mb, acc, zbuf):
        qi = pl.program_id(1)
        lane = lax.broadcasted_iota(jnp.int32, (1, LANES), 1)
        is_a = lane < HEAD_DIM

        @pl.when(qi == 0)
        def _():
            for h in range(hb):
                vp = v_ref[:, (h // 2) * LANES:(h // 2 + 1) * LANES]
                mine = is_a if h % 2 == 0 else jnp.logical_not(is_a)
                vaug[h] = jnp.where(mine, vp, jnp.ones_like(vp))

        r_i = lax.broadcasted_iota(jnp.int32, (TQ, TK), 0)
        c_i = lax.broadcasted_iota(jnp.int32, (TQ, TK), 1)
        visible = (c_i >> CHUNK_SHIFT) <= (r_i >> CHUNK_SHIFT)

        def key_rows(j):
            return pl.ds(pl.multiple_of(j * TK, TK), TK)

        def sweep(tiles):
            def loop(n, carry):
                tiles(((2 * n, False), (2 * n + 1, False)))
                return carry

            lax.fori_loop(0, qi // 2, loop, 0)

            @pl.when(qi % 2 == 1)
            def _():
                tiles(((qi - 1, False), (qi, True)))

            @pl.when(qi % 2 == 0)
            def _():
                tiles(((qi, True),))

        mrun[...] = jnp.full_like(mrun, NEG)

        def tiles_max(js):
            zs = [[_dot_nt(q_ref[:, h * LANES:(h + 1) * LANES], k_ref[key_rows(j), h * LANES:(h + 1) * LANES])
                   for h in range(hb)] for j, _ in js]
            for t, (j, diag) in enumerate(js):
                for h in range(hb):
                    z = jnp.where(visible, zs[t][h], NEG) if diag else zs[t][h]
                    zbuf[j, h] = z
                    mrun[h] = jnp.maximum(mrun[h], z)

        sweep(tiles_max)
        for h in range(hb):
            m = jnp.max(mrun[h], axis=1, keepdims=True) * c2
            mb[h] = jnp.broadcast_to(m, (TQ, TK))
        acc[...] = jnp.zeros_like(acc)

        def tiles_pv(js):
            ps = [[jnp.exp2((zbuf[j, h] * c2 - mb[h]).astype(BF16)) for h in range(hb)] for j, _ in js]
            for t, (j, _) in enumerate(js):
                for h in range(hb):
                    acc[h] += _dot(ps[t][h], vaug[h, key_rows(j), :])

        sweep(tiles_pv)
        for pr in range(hb // 2):
            a, b = 2 * pr, 2 * pr + 1
            psl = slice(pr * LANES, (pr + 1) * LANES)
            acc_a, acc_b = acc[a], acc[b]
            l_a = pltpu.roll(acc_a, HEAD_DIM, axis=1)
            l_b = pltpu.roll(acc_b, HEAD_DIM, axis=1)
            o_ref[:, psl] = jnp.where(is_a, acc_a * (1.0 / l_a), acc_b * (1.0 / l_b))
            lse_ref[:, psl] = jnp.where(is_a, mb[a, :, :LANES] * LN_2 + jnp.log(l_a),
                                        mb[b, :, :LANES] * LN_2 + jnp.log(l_b))

    blk = pl.BlockSpec((TQ, hb * HEAD_DIM), lambda g, qi: (qi, g))
    shp = pltpu.HBM((s, D_GRP), F32)
    return pl.pallas_call(
        body, name="mla_fwd", grid=(N_HEADS // hb, s // TQ),
        in_specs=[pl.BlockSpec((TQ, hb * LANES), lambda g, qi: (qi, g)),
                  pl.BlockSpec((s, hb * LANES), lambda g, qi: (0, g)),
                  pl.BlockSpec((s, hb * HEAD_DIM), lambda g, qi: (0, g))],
        out_specs=(blk, blk), out_shape=(shp, shp),
        scratch_shapes=[pltpu.VMEM((hb, s, LANES), BF16), pltpu.VMEM((hb, TQ, TK), F32),
                        pltpu.VMEM((hb, TQ, TK), F32), pltpu.VMEM((hb, TQ, LANES), F32),
                        pltpu.VMEM((s // TK, hb, TQ, TK), F32)],
        compiler_params=_params(("arbitrary", "arbitrary"), 44),
    )(*_hbm(qp, kp, vv))


def _mla_bwd(qp, kp, vv, d_o, o, lse, hb, pays):
    s = qp.shape[0]
    nq = s // TQ
    c2 = MLA_SCALE * LOG2_E
    n_op = len(pays)
    ride_in, ride_out, ride_shape, ride_sems = _pair_specs(pays)

    def body(q_ref, k_ref, v_ref, do_ref, o_ref, lse_ref, *refs):
        g_refs = refs[:n_op]
        dq_ref, dk_ref, dv_ref = refs[n_op:n_op + 3]
        l_refs = refs[n_op + 3:2 * n_op + 3]
        dqacc, lse_b, delta_b, q_t, do_t, ssem, rsem = refs[2 * n_op + 3:]
        qi = pl.program_id(1)

        @pl.when(jnp.logical_and(pl.program_id(0) == 0, qi == 0))
        def _():
            for cp in _pair_copies(g_refs, l_refs, ssem, rsem):
                cp.start()

        lane = lax.broadcasted_iota(jnp.int32, (1, LANES), 1)
        is_a = lane < HEAD_DIM

        @pl.when(qi == 0)
        def _():
            dk_ref[...] = jnp.zeros_like(dk_ref)
            dv_ref[...] = jnp.zeros_like(dv_ref)

        r_i = lax.broadcasted_iota(jnp.int32, (TQ, TK), 0)
        c_i = lax.broadcasted_iota(jnp.int32, (TQ, TK), 1)
        visible = (c_i >> CHUNK_SHIFT) <= (r_i >> CHUNK_SHIFT)
        do_x = []
        for h in range(hb):
            psl = slice((h // 2) * LANES, (h // 2 + 1) * LANES)
            mine = is_a if h % 2 == 0 else jnp.logical_not(is_a)
            d_o = do_ref[:, psl]
            delta = jnp.sum(jnp.where(mine, d_o * o_ref[:, psl], 0.0), axis=1, keepdims=True)
            lse_h = jnp.sum(jnp.where(lane == (h % 2) * HEAD_DIM, lse_ref[:, psl], 0.0), axis=1, keepdims=True)
            lse_b[h] = jnp.broadcast_to(lse_h * LOG2_E, (TQ, TK))
            delta_b[h] = jnp.broadcast_to(delta, (TQ, TK))
            do_h = jnp.where(mine, d_o, 0.0)
            do_x.append(do_h.astype(BF16))
            do_t[h] = do_h.T.astype(BF16)
            q_t[h] = q_ref[:, h * LANES:(h + 1) * LANES].astype(F32).T.astype(BF16)
        dqacc[...] = jnp.zeros_like(dqacc)

        head = lambda h: slice(h * LANES, (h + 1) * LANES)
        pair = lambda h: slice((h // 2) * LANES, (h // 2 + 1) * LANES)

        def tiles(js):
            th = [(j, diag, pl.ds(pl.multiple_of(j * TK, TK), TK), h) for j, diag in js for h in range(hb)]
            zs = [_dot_nt(q_ref[:, head(h)], k_ref[ks, head(h)]) for _, _, ks, h in th]
            dps = [_dot_nt(do_x[h], v_ref[ks, pair(h)]) for _, _, ks, h in th]
            for i, (j, diag, ks, h) in enumerate(th):
                e = zs[i] * c2 - lse_b[h]
                if diag:
                    e = jnp.where(visible, e, NEG)
                p = jnp.exp2(e)
                ds = (p * (dps[i] - delta_b[h]) * MLA_SCALE).astype(BF16)
                dqacc[h] += _dot(ds, k_ref[ks, head(h)])
                dk_ref[head(h), ks] += _dot(q_t[h], ds)
                dv_ref[pair(h), ks] += _dot(do_t[h], p.astype(BF16))

        def loop(n, c):
            tiles(((2 * n, False), (2 * n + 1, False)))
            return c

        lax.fori_loop(0, qi // 2, loop, 0)

        @pl.when(qi % 2 == 1)
        def _():
            tiles(((qi - 1, False), (qi, True)))

        @pl.when(qi % 2 == 0)
        def _():
            tiles(((qi, True),))

        for h in range(hb):
            dq_ref[:, h * LANES:(h + 1) * LANES] = dqacc[h]

        @pl.when(jnp.logical_and(pl.program_id(0) == pl.num_programs(0) - 1, qi == nq - 1))
        def _():
            for cp in _pair_copies(g_refs, l_refs, ssem, rsem):
                cp.wait()

    blk = pl.BlockSpec((TQ, hb * HEAD_DIM), lambda g, qi: (qi, g))
    return pl.pallas_call(
        body, name="mla_bwd", grid=(N_HEADS // hb, nq),
        in_specs=[pl.BlockSpec((TQ, hb * LANES), lambda g, qi: (qi, g)),
                  pl.BlockSpec((s, hb * LANES), lambda g, qi: (0, g)),
                  pl.BlockSpec((s, hb * HEAD_DIM), lambda g, qi: (0, g)), blk, blk, blk] + ride_in,
        out_specs=[pl.BlockSpec((TQ, hb * LANES), lambda g, qi: (qi, g)),
                   pl.BlockSpec((hb * LANES, s), lambda g, qi: (g, 0)),
                   pl.BlockSpec((hb * HEAD_DIM, s), lambda g, qi: (g, 0))] + ride_out,
        out_shape=[pltpu.HBM((s, 1024), F32), pltpu.HBM((1024, s), F32),
                   pltpu.HBM((D_GRP, s), F32)] + ride_shape,
        scratch_shapes=[pltpu.VMEM((hb, TQ, LANES), F32), pltpu.VMEM((hb, TQ, TK), F32),
                        pltpu.VMEM((hb, TQ, TK), F32), pltpu.VMEM((hb, LANES, TQ), BF16),
                        pltpu.VMEM((hb, LANES, TQ), BF16)] + ride_sems,
        compiler_params=_params(("arbitrary", "arbitrary"), 52),
    )(*_hbm(qp, kp, vv, d_o, o, lse), *pays)


def _mid(x, p, target, sb_o, mla_o, rest, g_sb, g_mla, w_out, g_post, w_ple, g_ple, w_pg, b_pg, bd):
    s = x.shape[0]

    def body(x_ref, p_ref, t_ref, sbo_ref, mlo_ref, sbg_ref, mlg_ref, gsb_ref, gml_ref, wout_ref,
             gpost_ref, wple_ref, gple_ref, wpg_ref, bpg_ref, bd_ref,
             dx1_ref, dsbo_ref, dmlo_ref, dsbg_ref, dmlg_ref, x1b_ref, dglb_ref, ycb_ref, dyb_ref,
             pb_ref, dub_ref, small_ref):
        i = pl.program_id(0)
        bd_m = bd_ref[...]

        def seg_mean(v):
            return _dot(v.astype(BF16), bd_m) * (1.0 / HEAD_DIM)

        groups = []
        for o_ref, gate_ref, gain_ref in ((sbo_ref, sbg_ref, gsb_ref), (mlo_ref, mlg_ref, gml_ref)):
            o = o_ref[...]
            r = lax.rsqrt(seg_mean(o * o) + EPS)
            n = o * r
            hn = n * gain_ref[...]
            gate = gate_ref[...]
            sg = _sigmoid(gate)
            si = gate * sg
            groups.append((r, n, hn, gate, sg, si, gain_ref[...]))
        ya = (groups[0][2] * groups[0][5]).astype(BF16)
        yb = (groups[1][2] * groups[1][5]).astype(BF16)
        ycb_ref[:, :D_GRP] = ya
        ycb_ref[:, D_GRP:] = yb
        y = _dot(ya, wout_ref[:D_GRP, :]) + _dot(yb, wout_ref[D_GRP:, :])
        ry = lax.rsqrt(jnp.mean(y * y, axis=-1, keepdims=True) + EPS)
        ny = y * ry
        x1 = x_ref[...] + ny * gpost_ref[...]
        x1b = x1.astype(BF16)
        x1b_ref[...] = x1b
        pb = p_ref[...].astype(BF16)
        pb_ref[...] = pb
        u = _dot(pb, wple_ref[...])
        ru = lax.rsqrt(jnp.mean(u * u, axis=-1, keepdims=True) + EPS)
        nu = u * ru
        ple = nu * gple_ref[...]
        gate = _sigmoid(_dot(x1b, wpg_ref[...]) + bpg_ref[...])
        x2 = x1 + ple * gate
        diff = x2 - t_ref[...]
        dx2 = diff * (1.0 / D_MODEL)

        d_ple = dx2 * gate
        d_glin = (dx2 * ple) * (gate * (1.0 - gate))
        dglb = d_glin.astype(BF16)
        dglb_ref[...] = dglb
        dx1 = dx2 + _dot_nt(dglb, wpg_ref[...])
        dx1_ref[...] = dx1
        d_nu = d_ple * gple_ref[...]
        d_u = ru * (d_nu - nu * jnp.mean(d_nu * nu, axis=-1, keepdims=True))
        dub_ref[...] = d_u.astype(BF16)
        d_ny = dx1 * gpost_ref[...]
        d_y = ry * (d_ny - ny * jnp.mean(d_ny * ny, axis=-1, keepdims=True))
        dyb = d_y.astype(BF16)
        dyb_ref[...] = dyb
        d_yc = (_dot_nt(dyb, wout_ref[:D_GRP, :]), _dot_nt(dyb, wout_ref[D_GRP:, :]))

        d_gain = []
        for gx, (do_ref, dg_ref) in enumerate(((dsbo_ref, dsbg_ref), (dmlo_ref, dmlg_ref))):
            r, n, hn, gate_g, sg, si, gain = groups[gx]
            dyg = d_yc[gx]
            d_hn = dyg * si
            dg_ref[...] = (dyg * hn * (sg * (1.0 + gate_g * (1.0 - sg)))).astype(BF16)
            d_gain.append(jnp.sum(d_hn * n, axis=0, keepdims=True))
            d_n = d_hn * gain
            do_ref[...] = r * (d_n - n * seg_mean(d_n * n))

        @pl.when(i == 0)
        def _():
            small_ref[...] = jnp.zeros_like(small_ref)

        small_ref[3:4, :D_GRP] += d_gain[0]
        small_ref[3:4, D_GRP:] += d_gain[1]
        small_ref[4:5, :] += jnp.sum(dx1 * ny, axis=0, keepdims=True)
        small_ref[5:6, :] += jnp.sum(d_ple * nu, axis=0, keepdims=True)
        small_ref[6:7, :] += jnp.sum(d_glin, axis=0, keepdims=True)
        small_ref[7:8, :] += jnp.sum(diff * diff, axis=0, keepdims=True) * (0.5 / D_MODEL)

    def row(width, idx=0):
        return pl.BlockSpec((TM, width), lambda i: (i, idx))

    def full(a):
        return pl.BlockSpec(a.shape, lambda i: (0, 0))

    f32 = lambda w: pltpu.HBM((s, w), F32)
    b16 = lambda w: pltpu.HBM((s, w), BF16)
    return pl.pallas_call(
        body, name="mid", grid=(s // TM,),
        in_specs=[row(D_MODEL), row(PLE_DIM), row(D_MODEL), row(D_GRP), row(D_GRP),
                  row(D_GRP, 0), row(D_GRP, 1), full(g_sb), full(g_mla), full(w_out), full(g_post),
                  full(w_ple), full(g_ple), full(w_pg), full(b_pg), full(bd)],
        out_specs=(row(D_MODEL), row(D_GRP), row(D_GRP), row(D_GRP), row(D_GRP), row(D_MODEL),
                   row(D_MODEL), row(D_MODEL), row(D_MODEL), row(PLE_DIM), row(D_MODEL),
                   pl.BlockSpec((8, D_MODEL), lambda i: (0, 0))),
        out_shape=(f32(D_MODEL), f32(D_GRP), f32(D_GRP), b16(D_GRP), b16(D_GRP), b16(D_MODEL),
                   b16(D_MODEL), b16(D_MODEL), b16(D_MODEL), b16(PLE_DIM), b16(D_MODEL),
                   jax.ShapeDtypeStruct((8, D_MODEL), F32)),
        compiler_params=_params(("arbitrary",), 46),
    )(*_hbm(x, p, target, sb_o, mla_o, rest, rest), g_sb, g_mla, w_out, g_post, w_ple, g_ple, w_pg, b_pg, bd)


def _mla_prep_bwd(dqp, dkp, dvv, rest, gq, gkv, wuq, wuk, wuv, cos_t, sin_t):
    s = rest.shape[0]

    def body(dqp_ref, dkp_ref, dvv_ref, cq_ref, ckv_ref, gq_ref, gkv_ref, wuq_ref, wuk_ref, wuv_ref,
             c_ref, s_ref, dcq_ref, dckv_ref, dkr_ref, dqb_ref, dkb_ref, dvb_ref, small_ref):
        i = pl.program_id(0)
        lane = lax.broadcasted_iota(jnp.int32, (1, LANES), 1)
        in_rope = (lane >= HEAD_DIM) & (lane < HEAD_DIM + ROPE_DIM)
        cos_v, sin_v = c_ref[...], s_ref[...]
        dkr_roped = jnp.zeros((TM_IO, LANES), F32)
        for h in range(N_HEADS):
            sl = slice(h * LANES, (h + 1) * LANES)
            dy = dqp_ref[:, sl]
            dqb_ref[:, sl] = (dy * cos_v + _rope_swap(dy * sin_v, lane)).astype(BF16)
            dkh = dkp_ref[sl, :].T
            dkb_ref[:, sl] = dkh.astype(BF16)
            dkr_roped = dkr_roped + jnp.where(in_rope, dkh, 0.0)
        dkr_ref[...] = (dkr_roped * cos_v + _rope_swap(dkr_roped * sin_v, lane)).astype(BF16)
        dvb = dvv_ref[...].T.astype(BF16)
        dvb_ref[...] = dvb

        cq = cq_ref[...]
        rq = lax.rsqrt(jnp.mean(cq * cq, axis=-1, keepdims=True) + EPS)
        nq_ = cq * rq
        d_cqn = _dot_nt(dqb_ref[...], wuq_ref[...])
        d_n = d_cqn * gq_ref[...]
        dcq_ref[...] = (rq * (d_n - nq_ * jnp.mean(d_n * nq_, axis=-1, keepdims=True))).astype(BF16)

        ckv = ckv_ref[...]
        rkv = lax.rsqrt(jnp.mean(ckv * ckv, axis=-1, keepdims=True) + EPS)
        nkv = ckv * rkv
        d_ckvn = _dot_nt(dkb_ref[...], wuk_ref[...]) + _dot_nt(dvb, wuv_ref[...])
        d_n2 = d_ckvn * gkv_ref[...]
        dckv_ref[...] = (rkv * (d_n2 - nkv * jnp.mean(d_n2 * nkv, axis=-1, keepdims=True))).astype(BF16)

        @pl.when(i == 0)
        def _():
            small_ref[...] = jnp.zeros_like(small_ref)

        small_ref[0:1, :] += jnp.sum(d_cqn * nq_, axis=0, keepdims=True)
        small_ref[1:2, :KV_LORA] += jnp.sum(d_ckvn * nkv, axis=0, keepdims=True)

    def row(width, idx=0):
        return pl.BlockSpec((TM_IO, width), lambda i: (i, idx))

    def full(a):
        return pl.BlockSpec(a.shape, lambda i: (0, 0))

    b16 = lambda w: pltpu.HBM((s, w), BF16)
    return pl.pallas_call(
        body, name="mla_prep_bwd", grid=(s // TM_IO,),
        in_specs=[row(1024), pl.BlockSpec((1024, TM_IO), lambda i: (0, i)), pl.BlockSpec((D_GRP, TM_IO), lambda i: (0, i)),
                  row(Q_LORA, 4), row(KV_LORA, 10), full(gq), full(gkv),
                  full(wuq), full(wuk), full(wuv), row(LANES), row(LANES)],
        out_specs=(row(Q_LORA), row(KV_LORA), row(LANES), row(1024), row(1024), row(D_GRP),
                   pl.BlockSpec((8, Q_LORA), lambda i: (0, 0))),
        out_shape=(b16(Q_LORA), b16(KV_LORA), b16(LANES), b16(1024), b16(1024), b16(D_GRP),
                   jax.ShapeDtypeStruct((8, Q_LORA), F32)),
        compiler_params=_params(("arbitrary",), 24),
    )(*_hbm(dqp, dkp, dvv, rest, rest), gq, gkv, wuq, wuk, wuv, cos_t, sin_t)


def _in_bwd(x, g, dx1, pieces, w, sums):
    s = x.shape[0]
    steps = s // TM_IO
    widths = [a.shape[1] for a in pieces]
    offs = [sum(widths[:k]) for k in range(len(widths))]
    n_pc, n_op = len(pieces), len(sums)
    ride_in, ride_out, ride_shape, ride_sems = _chip_specs(sums)

    def body(x_ref, g_ref, dx1_ref, *refs):
        piece_refs = refs[:n_pc]
        w_ref = refs[n_pc]
        s_refs = refs[n_pc + 1:n_pc + 1 + n_op]
        dx_ref, small_ref = refs[n_pc + 1 + n_op:n_pc + 3 + n_op]
        l_refs = refs[n_pc + 3 + n_op:n_pc + 3 + 2 * n_op]
        ssem, rsem = refs[n_pc + 3 + 2 * n_op:]
        i = pl.program_id(0)

        @pl.when(i == 0)
        def _():
            for cp in _chip_copies(s_refs, l_refs, ssem, rsem):
                cp.start()

        dh = jnp.zeros((TM_IO, D_MODEL), F32)
        for pr, off, wd in zip(piece_refs, offs, widths):
            dh = dh + _dot_nt(pr[...], w_ref[:, off:off + wd])
        xv = x_ref[...]
        r = lax.rsqrt(jnp.mean(xv * xv, axis=-1, keepdims=True) + EPS)
        n = xv * r
        d_n = dh * g_ref[...]
        dx_ref[...] = dx1_ref[...] + r * (d_n - n * jnp.mean(d_n * n, axis=-1, keepdims=True))

        @pl.when(i == 0)
        def _():
            small_ref[...] = jnp.zeros_like(small_ref)

        small_ref[0:1, :] += jnp.sum(dh * n, axis=0, keepdims=True)

        @pl.when(i == steps - 1)
        def _():
            for cp in _chip_copies(s_refs, l_refs, ssem, rsem):
                cp.wait()

    def row(width):
        return pl.BlockSpec((TM_IO, width), lambda i: (i, 0))

    return pl.pallas_call(
        body, name="in_bwd", grid=(steps,),
        in_specs=[row(D_MODEL), pl.BlockSpec((1, D_MODEL), lambda i: (0, 0)), row(D_MODEL)]
        + [row(wd) for wd in widths] + [pl.BlockSpec(w.shape, lambda i: (0, 0))] + ride_in,
        out_specs=[row(D_MODEL), pl.BlockSpec((8, D_MODEL), lambda i: (0, 0))] + ride_out,
        out_shape=[pltpu.HBM((s, D_MODEL), F32), jax.ShapeDtypeStruct((8, D_MODEL), F32)]
        + ride_shape,
        scratch_shapes=ride_sems,
        compiler_params=_params(("arbitrary",), 40),
    )(*_hbm(x), g, *_hbm(dx1, *pieces), w, *sums)


def _tn_matmul(a, b, name, blocked=False):
    s, k = a.shape
    n = b.shape[1]
    ts = min(s, TS_DW)
    tn = n if blocked else min(n, 512)
    steps = s // ts

    def body(a_ref, b_ref, o_ref):
        t = pl.program_id(1)

        @pl.when(t == 0)
        def _():
            o_ref[...] = jnp.zeros_like(o_ref)

        prod = _dot_tn(a_ref[...], b_ref[...])
        if blocked:
            for j in range(n // LANES):
                o_ref[j] += prod[:, j * LANES:(j + 1) * LANES]
        else:
            o_ref[...] += prod

    if blocked:
        out_spec = pl.BlockSpec((n // LANES, k, LANES), lambda j, t: (0, 0, 0))
        out_shape = jax.ShapeDtypeStruct((n // LANES, k, LANES), F32)
    else:
        out_spec = pl.BlockSpec((k, tn), lambda j, t: (0, j))
        out_shape = jax.ShapeDtypeStruct((k, n), F32)
    return pl.pallas_call(
        body, name=name, grid=(n // tn, steps),
        in_specs=[pl.BlockSpec((ts, k), lambda j, t: (t, 0)), pl.BlockSpec((ts, tn), lambda j, t: (t, j))],
        out_specs=out_spec, out_shape=out_shape,
        compiler_params=_params(("parallel", "arbitrary"), 20),
    )(*_hbm(a, b))


def _tn_matmul_ring(tasks, name):
    s, k = tasks[0][0].shape
    n = sum(b.shape[1] for b in tasks[0][1])
    n_pr = len(tasks)
    n_in = [1 + len(bs) for _, bs in tasks]
    first = [sum(n_in[:p]) for p in range(n_pr)]
    sched = [(p, c) for p in range(n_pr) for c in range(s // TS_RING)]

    def body(*refs):
        o_refs = refs[sum(n_in):sum(n_in) + n_pr]
        abuf, bbuf, sem = refs[sum(n_in) + n_pr:]

        def copies(i):
            p, c = sched[i]
            rows = pl.ds(c * TS_RING, TS_RING)
            slot = i % RING_SLOTS
            out = [pltpu.make_async_copy(refs[first[p]].at[rows, :], abuf.at[slot], sem.at[0, slot])]
            off = 0
            for j, b in enumerate(tasks[p][1]):
                wd = b.shape[1]
                out.append(pltpu.make_async_copy(refs[first[p] + 1 + j].at[rows, :],
                                                 bbuf.at[slot, :, off:off + wd], sem.at[1 + j, slot]))
                off += wd
            return out

        for i in range(min(RING_SLOTS - 1, len(sched))):
            for cp in copies(i):
                cp.start()
        for i, (p, c) in enumerate(sched):
            if i + RING_SLOTS - 1 < len(sched):
                for cp in copies(i + RING_SLOTS - 1):
                    cp.start()
            for cp in copies(i):
                cp.wait()
            prod = _dot_tn(abuf[i % RING_SLOTS], bbuf[i % RING_SLOTS])
            if c == 0:
                o_refs[p][...] = prod
            else:
                o_refs[p][...] += prod

    return pl.pallas_call(
        body, name=name,
        in_specs=[pl.BlockSpec(memory_space=pl.ANY)] * sum(n_in),
        out_specs=[pl.BlockSpec(memory_space=pltpu.VMEM)] * n_pr,
        out_shape=[jax.ShapeDtypeStruct((k, n), F32)] * n_pr,
        scratch_shapes=[pltpu.VMEM((RING_SLOTS, TS_RING, k), BF16), pltpu.VMEM((RING_SLOTS, TS_RING, n), BF16),
                        pltpu.SemaphoreType.DMA((max(n_in), RING_SLOTS))],
        compiler_params=pltpu.CompilerParams(vmem_limit_bytes=(36 + 4 * n_pr) << 20),
    )(*_hbm(*[x for a_op, bs in tasks for x in (a_op, *bs)]))


IN_SHARD = 372
_IN_KERNEL_ORDER = ((0, 2048), (2464, 2976), (2048, 2432))
_IN_ROPE = (2432, 2464)
_IN_GRAD_SRC = ((0, 512, 0, 0), (512, 1024, 0, 512), (1024, 1536, 1, 0), (1536, 2048, 1, 512),
                (2048, 2304, 2, 512), (2304, 2432, 2, 768), (2432, 2464, 2, 960), (2464, 2976, 2, 0))


def _shard_cols(gath_in, lo, hi):
    out = []
    while lo < hi:
        j, a = divmod(lo, IN_SHARD)
        b = min(IN_SHARD, a + hi - lo)
        out.append(gath_in[j][:, a:b])
        lo += b - a
    return out


def _kernel_w_in(g_in):
    zc = lambda n: jnp.zeros((D_MODEL, n), BF16)
    parts = [pc for lo, hi in _IN_KERNEL_ORDER for pc in _shard_cols(g_in, lo, hi)]
    parts += [zc(64)] + _shard_cols(g_in, *_IN_ROPE) + [zc(32)]
    return jnp.concatenate(parts, axis=1)


def _kernel_weights(gath):
    g_uq, g_ukv, g_out, g_ple, g_pg = gath
    w_uq_p = jnp.pad(g_uq, ((0, 0), (0, 0), (0, 32))).transpose(1, 0, 2).reshape(Q_LORA, 1024)
    k_only = jnp.where(jnp.arange(LANES) < HEAD_DIM, g_ukv, jnp.zeros_like(g_ukv))
    w_uk_p = k_only.transpose(1, 0, 2).reshape(KV_LORA, 1024)
    w_uv = g_ukv[:, :, HEAD_DIM:].transpose(1, 0, 2).reshape(KV_LORA, D_GRP)
    w_ple = g_ple.transpose(1, 0, 2).reshape(PLE_DIM, D_MODEL)
    return (w_uq_p, w_uk_p, w_uv, g_out.reshape(D_MODEL, D_MODEL), w_ple, g_pg.reshape(D_MODEL, D_MODEL))


def _payload_in(d_cols):
    blocks = []
    for j in range(N_DEV):
        lo, hi = j * IN_SHARD, (j + 1) * IN_SHARD
        parts = []
        for o_lo, o_hi, idx, off in _IN_GRAD_SRC:
            a, b = max(lo, o_lo), min(hi, o_hi)
            if a < b:
                parts.append(d_cols[idx][:, off + a - o_lo:off + b - o_lo])
        blocks.append(jnp.concatenate(parts, axis=1))
    return jnp.stack(blocks)


def _payload_ukv(duk_blk, d_uv):
    dv_blk = d_uv.reshape(KV_LORA, N_HEADS, HEAD_DIM).transpose(1, 0, 2)
    return jnp.concatenate([duk_blk[:, :, :HEAD_DIM], dv_blk], axis=2)


def kernel(x, p, positions, norm_pre_g, w_in, q_norm_g, w_uq, kv_norm_g, w_ukv, sb_out_norm_g, mla_out_norm_g, w_out, norm_post_g, w_ple, ple_norm_g, w_ple_gate, b_ple_gate, loss_target, m_norm_pre_g, m_w_in, m_q_norm_g, m_w_uq, m_kv_norm_g, m_w_ukv, m_sb_out_norm_g, m_mla_out_norm_g, m_w_out, m_norm_post_g, m_w_ple, m_ple_norm_g, m_w_ple_gate, m_b_ple_gate, v_norm_pre_g, v_w_in, v_q_norm_g, v_w_uq, v_kv_norm_g, v_w_ukv, v_sb_out_norm_g, v_mla_out_norm_g, v_w_out, v_norm_post_g, v_w_ple, v_ple_norm_g, v_w_ple_gate, v_b_ple_gate):
    mats = (w_in, w_uq, w_ukv, w_out, w_ple, w_ple_gate)
    m_mats = (m_w_in, m_w_uq, m_w_ukv, m_w_out, m_w_ple, m_w_ple_gate)
    v_mats = (v_w_in, v_w_uq, v_w_ukv, v_w_out, v_w_ple, v_w_ple_gate)
    vecs = (norm_pre_g, q_norm_g, kv_norm_g, sb_out_norm_g, mla_out_norm_g, norm_post_g, ple_norm_g, b_ple_gate)
    m_vecs = (m_norm_pre_g, m_q_norm_g, m_kv_norm_g, m_sb_out_norm_g, m_mla_out_norm_g, m_norm_post_g,
              m_ple_norm_g, m_b_ple_gate)
    v_vecs = (v_norm_pre_g, v_q_norm_g, v_kv_norm_g, v_sb_out_norm_g, v_mla_out_norm_g, v_norm_post_g,
              v_ple_norm_g, v_b_ple_gate)

    shards = [a[0].astype(BF16) for a in mats]
    w_in_p = _kernel_w_in(_all_gather(shards[:1])[0])
    grad_x, reduced, vec_slab = _step(x[0], p[0, 0], positions[0], loss_target[0], *vecs, w_in_p, shards[1:])
    upd = [_adamw_matrix(own, l2, w, m, v, "adamw_%d" % o)
           for o, ((own, l2), w, m, v) in enumerate(zip(reduced, mats, m_mats, v_mats))]
    sm = _adamw_vectors(_slab_exchange(vec_slab), vecs, m_vecs, v_vecs)

    outs = []
    for kind in range(4):
        mat = [upd[o][kind] for o in range(len(mats))]
        vec = sm[1 + 8 * kind:9 + 8 * kind]
        outs += [vec[0], mat[0], vec[1], mat[1], vec[2], mat[2], vec[3], vec[4], mat[3], vec[5],
                 mat[4], vec[6], mat[5], vec[7]]
    return (sm[0][0, 0], grad_x[None], *outs)


def _step(xs, ps, pos, tgt, norm_pre_g, q_norm_g, kv_norm_g, sb_out_norm_g, mla_out_norm_g,
          norm_post_g, ple_norm_g, b_ple_gate, w_in_p, shards):
    s = xs.shape[0]
    place = jnp.stack([lax.axis_index("c"), 2 * lax.axis_index("x") + lax.axis_index("y")]).astype(jnp.int32)

    half = ROPE_DIM // 2
    freq = ROPE_THETA ** (-jnp.arange(half, dtype=F32) / half)
    ang = pos.astype(F32)[:, None] * freq
    cos, sin = jnp.cos(ang), jnp.sin(ang)
    cos_t = jnp.concatenate([jnp.ones((s, 64), F32), cos, cos, jnp.zeros((s, 32), F32)], axis=1)
    sin_t = jnp.concatenate([jnp.zeros((s, 64), F32), -sin, sin, jnp.zeros((s, 32), F32)], axis=1)
    seg = jnp.arange(D_GRP) // HEAD_DIM
    bd = (seg[:, None] == seg[None, :]).astype(BF16)

    qkv, rest, h_b, *gath = _in_proj(xs, norm_pre_g, w_in_p, shards)
    w_uq_p, w_uk_p, w_uv, f_out, f_ple, f_pg = _kernel_weights(gath)
    sb_o = _sb_fwd(qkv, 8)
    qp, kp, vv, cqn_b, ckvn_b = _mla_prep(rest, q_norm_g, kv_norm_g, w_uq_p, w_uk_p, w_uv, cos_t, sin_t)
    mla_o, lse = _mla_fwd(qp, kp, vv, 4)

    (dx1, d_sbo, d_mlo, d_sbg, d_mlg, x1_b, dgl_b, yc_b, dy_b, p_b, du_b, small_mid) = _mid(
        xs, ps, tgt, sb_o, mla_o, rest, sb_out_norm_g, mla_out_norm_g, f_out, norm_post_g,
        f_ple, ple_norm_g, f_pg, b_ple_gate, bd)
    d_out, d_pg = _tn_matmul_ring([(yc_b, [dy_b]), (x1_b, [dgl_b])], "dw_out_pg")
    pay_a = [d_out.reshape(N_DEV, 128, D_MODEL), _tn_matmul(p_b, du_b, "dw_ple", blocked=True),
             d_pg.reshape(N_DEV, 128, D_MODEL)]
    dqp, dkp, dvv, *sib_a = _mla_bwd(qp, kp, vv, d_mlo, mla_o, lse, 4, pay_a)
    pair_a = _pair_sums(pay_a, sib_a, place, "grad_pair_sums_a")
    dq_sb, dk_sb, dv_sb, *landed_a = _sb_bwd(qkv, d_sbo, [sm for sm, _ in pair_a])
    dcq, dckv, dkr, dq_b, dk_b, dv_b, small_prep = _mla_prep_bwd(
        dqp, dkp, dvv, rest, q_norm_g, kv_norm_g, w_uq_p, w_uk_p, w_uv, cos_t, sin_t)
    pieces = [dq_sb, dk_sb, dv_sb, d_sbg, d_mlg, dcq, dckv, dkr]
    d_cols = _tn_matmul_ring([(h_b, pieces[0:2]), (h_b, pieces[2:4]), (h_b, pieces[4:8])], "dw_in")
    pay_b = [_payload_in(d_cols), _tn_matmul(cqn_b, dq_b, "dw_uq", blocked=True),
             _payload_ukv(_tn_matmul(ckvn_b, dk_b, "dw_uk", blocked=True), _tn_matmul(ckvn_b, dv_b, "dw_uv"))]
    pair_b = _pair_sums(pay_b, _pair_exchange(pay_b, "grad_pair_exchange"), place, "grad_pair_sums_b")
    grad_x, small_in, *landed_b = _in_bwd(xs, norm_pre_g, dx1, pieces, w_in_p, [sm for sm, _ in pair_b])
    reduced = [(own, l2) for (_, own), l2 in zip(pair_b + pair_a, landed_b + landed_a)]
    slab = jnp.concatenate([small_in[0:1], jnp.pad(small_prep[0:2], ((0, 0), (0, D_MODEL - Q_LORA))),
                            small_mid[3:8]], axis=0)
    return grad_x, reduced, slab
```
